```python
import jax, jax.numpy as jnp
from jax import lax
import numpy as np

D_MODEL = 1024
BATCH = 32
SEQ = 2048
DEPTH = 1

N_META = 16
MIX_WIDTH = D_MODEL
CONV_DIM = MIX_WIDTH // 2
ATTN_DIM = MIX_WIDTH - CONV_DIM
HEAD_DIM = 64
N_HEADS = ATTN_DIM // HEAD_DIM
N_CONV_GROUPS = CONV_DIM // HEAD_DIM
CONV_K = 3
D_FF = ((8 * D_MODEL // 3 + 255) // 256) * 256
Q_BLOCK = 128
IN_DIM = 3 * CONV_DIM + 3 * ATTN_DIM + N_HEADS
EPS = 1e-6

kernel_name = "hymba_conv_fox_macaron_layer"


def rms_norm(x, g):
    xf = x.astype(jnp.float32)
    y = xf * lax.rsqrt(jnp.mean(xf * xf, axis=-1, keepdims=True) + EPS)
    return (y * g.astype(jnp.float32)).astype(x.dtype)


def group_rms_norm(x, g, n_groups):
    b, l, c = x.shape
    xg = x.astype(jnp.float32).reshape(b, l, n_groups, c // n_groups)
    xg = xg * lax.rsqrt(jnp.mean(xg * xg, axis=-1, keepdims=True) + EPS)
    return (xg.reshape(b, l, c) * g.astype(jnp.float32)).astype(x.dtype)


def swiglu_ffn(h, w_gu, w_down):
    gate, up = jnp.split(h @ w_gu, 2, axis=-1)
    return (jax.nn.silu(gate) * up) @ w_down


def short_conv_mixer(b_gate, c_gate, hc, conv_w):
    u = c_gate * hc
    y = lax.conv_general_dilated(
        u, conv_w.astype(u.dtype)[:, None, :], window_strides=(1,),
        padding=[(CONV_K - 1, 0)], dimension_numbers=('NWC', 'WIO', 'NWC'),
        feature_group_count=CONV_DIM)
    return b_gate * y


def fox_block(q_blk, fq_blk, q_pos, k, v, fk, k_pos):
    s = jnp.einsum('bhqd,bhkd->bhqk', q_blk, k, preferred_element_type=jnp.float32) * (HEAD_DIM ** -0.5)
    s = s + fq_blk[..., :, None] - fk[..., None, :]
    s = jnp.where(k_pos[None, :] <= q_pos[:, None], s, -jnp.inf)
    p = jax.nn.softmax(s, axis=-1)
    return jnp.einsum('bhqk,bhkd->bhqd', p.astype(v.dtype), v)


def forgetting_attention(q, k, v, fg_logit, b_f):
    bsz, L = q.shape[0], q.shape[1]
    log_f = jax.nn.log_sigmoid(fg_logit.astype(jnp.float32) + b_f.astype(jnp.float32))
    F = jnp.cumsum(log_f, axis=1).transpose(0, 2, 1)
    q, k, v = (t.transpose(0, 2, 1, 3) for t in (q, k, v))
    pos = jnp.arange(L)
    o_meta = fox_block(q[:, :, :N_META], F[:, :, :N_META], pos[:N_META],
                       k[:, :, :N_META], v[:, :, :N_META], F[:, :, :N_META], pos[:N_META])
    n_blk = (L - N_META) // Q_BLOCK
    qr = q[:, :, N_META:].reshape(bsz, N_HEADS, n_blk, Q_BLOCK, HEAD_DIM).transpose(2, 0, 1, 3, 4)
    fr = F[:, :, N_META:].reshape(bsz, N_HEADS, n_blk, Q_BLOCK).transpose(2, 0, 1, 3)
    pr = pos[N_META:].reshape(n_blk, Q_BLOCK)
    o_real = lax.map(lambda a: fox_block(a[0], a[1], a[2], k, v, F, pos), (qr, fr, pr))
    o_real = o_real.transpose(1, 2, 0, 3, 4).reshape(bsz, N_HEADS, L - N_META, HEAD_DIM)
    o = jnp.concatenate([o_meta, o_real], axis=2)
    return o.transpose(0, 2, 1, 3).reshape(bsz, L, ATTN_DIM)


def hybrid_mixer(h, w_in, conv_w, b_f, g_conv, g_attn, w_out):
    bsz, L, _ = h.shape
    proj = h @ w_in
    c0 = 3 * CONV_DIM
    b_gate, c_gate, hc, q, k, v, fg = jnp.split(
        proj, [CONV_DIM, 2 * CONV_DIM, c0, c0 + ATTN_DIM, c0 + 2 * ATTN_DIM, c0 + 3 * ATTN_DIM], axis=-1)
    y_conv = short_conv_mixer(b_gate, c_gate, hc, conv_w)
    hs = (bsz, L, N_HEADS, HEAD_DIM)
    y_attn = forgetting_attention(q.reshape(hs), k.reshape(hs), v.reshape(hs), fg, b_f)
    y = jnp.concatenate([group_rms_norm(y_conv, g_conv, N_CONV_GROUPS),
                         group_rms_norm(y_attn, g_attn, N_HEADS)], axis=-1)
    return y @ w_out


def _fwd_setup_inputs(seed: int = 0) -> dict:
    key = jax.random.key(seed)
    ks = jax.random.split(key, 20)
    nrm = lambda k, shape, scale: jax.random.normal(k, shape, jnp.float32) * scale
    gain = lambda k, shape: 1.0 + 0.02 * jax.random.normal(k, shape, jnp.float32)
    return {
        'x': nrm(ks[0], (BATCH, SEQ, D_MODEL), 1.0),
        'meta_tokens': nrm(ks[1], (N_META, D_MODEL), 1.0),
        'ffn1_norm': gain(ks[2], (DEPTH, D_MODEL)),
        'ffn1_w_gu': nrm(ks[3], (DEPTH, D_MODEL, 2 * D_FF), D_MODEL ** -0.5),
        'ffn1_w_down': nrm(ks[4], (DEPTH, D_FF, D_MODEL), D_FF ** -0.5),
        'mix_norm': gain(ks[5], (DEPTH, D_MODEL)),
        'w_in': nrm(ks[6], (DEPTH, D_MODEL, IN_DIM), D_MODEL ** -0.5),
        'conv_w': nrm(ks[7], (DEPTH, CONV_K, CONV_DIM), CONV_K ** -0.5),
        'b_f': jnp.linspace(1.0, 6.0, N_HEADS, dtype=jnp.float32)[None, :] + nrm(ks[8], (DEPTH, N_HEADS), 0.1),
        'out_norm_conv': gain(ks[9], (DEPTH, CONV_DIM)),
        'out_norm_attn': gain(ks[10], (DEPTH, ATTN_DIM)),
        'w_out': nrm(ks[11], (DEPTH, MIX_WIDTH, D_MODEL), MIX_WIDTH ** -0.5),
        'ffn2_norm': gain(ks[12], (DEPTH, D_MODEL)),
        'ffn2_w_gu': nrm(ks[13], (DEPTH, D_MODEL, 2 * D_FF), D_MODEL ** -0.5),
        'ffn2_w_down': nrm(ks[14], (DEPTH, D_FF, D_MODEL), D_FF ** -0.5),
        'final_norm': gain(ks[15], (D_MODEL,)),
    }


def _fwd_reference(x, meta_tokens, ffn1_norm, ffn1_w_gu, ffn1_w_down, mix_norm, w_in, conv_w, b_f,
              out_norm_conv, out_norm_attn, w_out, ffn2_norm, ffn2_w_gu, ffn2_w_down, final_norm):
    bsz = x.shape[0]
    meta = jnp.broadcast_to(meta_tokens.astype(x.dtype)[None], (bsz, N_META, D_MODEL))
    h = jnp.concatenate([meta, x], axis=1)
    for l in range(DEPTH):
        h = h + 0.5 * swiglu_ffn(rms_norm(h, ffn1_norm[l]), ffn1_w_gu[l], ffn1_w_down[l])
        h = h + hybrid_mixer(rms_norm(h, mix_norm[l]), w_in[l], conv_w[l], b_f[l],
                             out_norm_conv[l], out_norm_attn[l], w_out[l])
        h = h + 0.5 * swiglu_ffn(rms_norm(h, ffn2_norm[l]), ffn2_w_gu[l], ffn2_w_down[l])
    h = h[:, N_META:]
    return rms_norm(h, final_norm)


import jax as _jax
import jax.numpy as _jnp

TWIN_FORMAT = 'train_step'
FWD_PARAMS = ['x', 'meta_tokens', 'ffn1_norm', 'ffn1_w_gu', 'ffn1_w_down', 'mix_norm', 'w_in', 'conv_w', 'b_f', 'out_norm_conv', 'out_norm_attn', 'w_out', 'ffn2_norm', 'ffn2_w_gu', 'ffn2_w_down', 'final_norm']
TWIN_WEIGHTS = ['meta_tokens', 'ffn1_norm', 'ffn1_w_gu', 'ffn1_w_down', 'mix_norm', 'w_in', 'conv_w', 'b_f', 'out_norm_conv', 'out_norm_attn', 'w_out', 'ffn2_norm', 'ffn2_w_gu', 'ffn2_w_down', 'final_norm']
TWIN_DIFF_INPUT = 'x'
TWIN_INPUTS = ['x', 'meta_tokens', 'ffn1_norm', 'ffn1_w_gu', 'ffn1_w_down', 'mix_norm', 'w_in', 'conv_w', 'b_f', 'out_norm_conv', 'out_norm_attn', 'w_out', 'ffn2_norm', 'ffn2_w_gu', 'ffn2_w_down', 'final_norm', 'loss_target', 'm_meta_tokens', 'm_ffn1_norm', 'm_ffn1_w_gu', 'm_ffn1_w_down', 'm_mix_norm', 'm_w_in', 'm_conv_w', 'm_b_f', 'm_out_norm_conv', 'm_out_norm_attn', 'm_w_out', 'm_ffn2_norm', 'm_ffn2_w_gu', 'm_ffn2_w_down', 'm_final_norm', 'v_meta_tokens', 'v_ffn1_norm', 'v_ffn1_w_gu', 'v_ffn1_w_down', 'v_mix_norm', 'v_w_in', 'v_conv_w', 'v_b_f', 'v_out_norm_conv', 'v_out_norm_attn', 'v_w_out', 'v_ffn2_norm', 'v_ffn2_w_gu', 'v_ffn2_w_down', 'v_final_norm']
TWIN_OUTPUTS = ['loss', 'grad_x', 'grad_meta_tokens', 'grad_ffn1_norm', 'grad_ffn1_w_gu', 'grad_ffn1_w_down', 'grad_mix_norm', 'grad_w_in', 'grad_conv_w', 'grad_b_f', 'grad_out_norm_conv', 'grad_out_norm_attn', 'grad_w_out', 'grad_ffn2_norm', 'grad_ffn2_w_gu', 'grad_ffn2_w_down', 'grad_final_norm', 'delta_meta_tokens', 'delta_ffn1_norm', 'delta_ffn1_w_gu', 'delta_ffn1_w_down', 'delta_mix_norm', 'delta_w_in', 'delta_conv_w', 'delta_b_f', 'delta_out_norm_conv', 'delta_out_norm_attn', 'delta_w_out', 'delta_ffn2_norm', 'delta_ffn2_w_gu', 'delta_ffn2_w_down', 'delta_final_norm', 'new_m_meta_tokens', 'new_m_ffn1_norm', 'new_m_ffn1_w_gu', 'new_m_ffn1_w_down', 'new_m_mix_norm', 'new_m_w_in', 'new_m_conv_w', 'new_m_b_f', 'new_m_out_norm_conv', 'new_m_out_norm_attn', 'new_m_w_out', 'new_m_ffn2_norm', 'new_m_ffn2_w_gu', 'new_m_ffn2_w_down', 'new_m_final_norm', 'new_v_meta_tokens', 'new_v_ffn1_norm', 'new_v_ffn1_w_gu', 'new_v_ffn1_w_down', 'new_v_mix_norm', 'new_v_w_in', 'new_v_conv_w', 'new_v_b_f', 'new_v_out_norm_conv', 'new_v_out_norm_attn', 'new_v_w_out', 'new_v_ffn2_norm', 'new_v_ffn2_w_gu', 'new_v_ffn2_w_down', 'new_v_final_norm']
TWIN_LEAF_KINDS = {'loss': 'loss', 'grad_x': 'grad_x', 'grad_meta_tokens': 'grad_w', 'grad_ffn1_norm': 'grad_w', 'grad_ffn1_w_gu': 'grad_w', 'grad_ffn1_w_down': 'grad_w', 'grad_mix_norm': 'grad_w', 'grad_w_in': 'grad_w', 'grad_conv_w': 'grad_w', 'grad_b_f': 'grad_w', 'grad_out_norm_conv': 'grad_w', 'grad_out_norm_attn': 'grad_w', 'grad_w_out': 'grad_w', 'grad_ffn2_norm': 'grad_w', 'grad_ffn2_w_gu': 'grad_w', 'grad_ffn2_w_down': 'grad_w', 'grad_final_norm': 'grad_w', 'delta_meta_tokens': 'delta_w', 'delta_ffn1_norm': 'delta_w', 'delta_ffn1_w_gu': 'delta_w', 'delta_ffn1_w_down': 'delta_w', 'delta_mix_norm': 'delta_w', 'delta_w_in': 'delta_w', 'delta_conv_w': 'delta_w', 'delta_b_f': 'delta_w', 'delta_out_norm_conv': 'delta_w', 'delta_out_norm_attn': 'delta_w', 'delta_w_out': 'delta_w', 'delta_ffn2_norm': 'delta_w', 'delta_ffn2_w_gu': 'delta_w', 'delta_ffn2_w_down': 'delta_w', 'delta_final_norm': 'delta_w', 'new_m_meta_tokens': 'new_m', 'new_m_ffn1_norm': 'new_m', 'new_m_ffn1_w_gu': 'new_m', 'new_m_ffn1_w_down': 'new_m', 'new_m_mix_norm': 'new_m', 'new_m_w_in': 'new_m', 'new_m_conv_w': 'new_m', 'new_m_b_f': 'new_m', 'new_m_out_norm_conv': 'new_m', 'new_m_out_norm_attn': 'new_m', 'new_m_w_out': 'new_m', 'new_m_ffn2_norm': 'new_m', 'new_m_ffn2_w_gu': 'new_m', 'new_m_ffn2_w_down': 'new_m', 'new_m_final_norm': 'new_m', 'new_v_meta_tokens': 'new_v', 'new_v_ffn1_norm': 'new_v', 'new_v_ffn1_w_gu': 'new_v', 'new_v_ffn1_w_down': 'new_v', 'new_v_mix_norm': 'new_v', 'new_v_w_in': 'new_v', 'new_v_conv_w': 'new_v', 'new_v_b_f': 'new_v', 'new_v_out_norm_conv': 'new_v', 'new_v_out_norm_attn': 'new_v', 'new_v_w_out': 'new_v', 'new_v_ffn2_norm': 'new_v', 'new_v_ffn2_w_gu': 'new_v', 'new_v_ffn2_w_down': 'new_v', 'new_v_final_norm': 'new_v'}


def _forward(args):
    return _fwd_reference(*[args[k] for k in FWD_PARAMS])


def _output_shape():
    out = _jax.eval_shape(lambda: _forward(_fwd_setup_inputs(0)))
    return out.shape, out.dtype

N_MICROBATCH = 1
ADAM_LR = 0.001
ADAM_B1 = 0.9
ADAM_B2 = 0.999
ADAM_EPS = 1e-08
ADAM_WD = 0.01
ADAM_STEP = 10
PER_EXAMPLE_BATCH_AXIS = {'x': 0, 'loss_target': 0}
SHARED_INPUTS = []
_WEIGHT_DTYPES = {'meta_tokens': _jnp.float32, 'ffn1_norm': _jnp.float32, 'ffn1_w_gu': _jnp.float32, 'ffn1_w_down': _jnp.float32, 'mix_norm': _jnp.float32, 'w_in': _jnp.float32, 'conv_w': _jnp.float32, 'b_f': _jnp.float32, 'out_norm_conv': _jnp.float32, 'out_norm_attn': _jnp.float32, 'w_out': _jnp.float32, 'ffn2_norm': _jnp.float32, 'ffn2_w_gu': _jnp.float32, 'ffn2_w_down': _jnp.float32, 'final_norm': _jnp.float32}
MOMENT_SCALE = {'meta_tokens': 1.074747e-02, 'ffn1_norm': 1.482925e-01, 'ffn1_w_gu': 6.193404e-02, 'ffn1_w_down': 1.010250e-01, 'mix_norm': 3.034636e-01, 'w_in': 1.747245e-01, 'conv_w': 1.882023e-01, 'b_f': 3.058025e-01, 'out_norm_conv': 1.737838e-01, 'out_norm_attn': 1.913811e-01, 'w_out': 1.775399e-01, 'ffn2_norm': 7.613584e-02, 'ffn2_w_gu': 3.114746e-02, 'ffn2_w_down': 5.079382e-02, 'final_norm': 6.389982e+01}


def _to_microbatches(a, axis):
    t = _jnp.moveaxis(a, axis, 0)
    t = t.reshape((N_MICROBATCH, t.shape[0] // N_MICROBATCH) + t.shape[1:])
    return _jnp.moveaxis(t, 1, axis + 1)


def setup_inputs(seed: int = 0) -> dict:
    inp = _fwd_setup_inputs(seed)
    key = _jax.random.fold_in(_jax.random.key(seed), 7919)
    shape, _ = _output_shape()
    out = dict(inp)
    out["loss_target"] = _jax.random.normal(_jax.random.fold_in(key, 0), shape, _jnp.float32)
    for i, name in enumerate(TWIN_WEIGHTS):
        w = inp[name].astype(_jnp.float32)
        if MOMENT_SCALE is None:
            s = _jnp.sqrt(_jnp.mean(_jnp.square(w)) + 1e-30)
        else:
            s = MOMENT_SCALE[name]
        km, kv = _jax.random.split(_jax.random.fold_in(key, i + 1))
        out[name] = w
        out["m_" + name] = s * _jax.random.normal(km, w.shape, _jnp.float32)
        out["v_" + name] = (s * s) * _jax.random.uniform(kv, w.shape, _jnp.float32, 0.5, 1.5)
    if N_MICROBATCH > 1:
        for name, axis in PER_EXAMPLE_BATCH_AXIS.items():
            out[name] = _to_microbatches(out[name], axis)
    return {'x': out['x'], 'meta_tokens': out['meta_tokens'], 'ffn1_norm': out['ffn1_norm'], 'ffn1_w_gu': out['ffn1_w_gu'], 'ffn1_w_down': out['ffn1_w_down'], 'mix_norm': out['mix_norm'], 'w_in': out['w_in'], 'conv_w': out['conv_w'], 'b_f': out['b_f'], 'out_norm_conv': out['out_norm_conv'], 'out_norm_attn': out['out_norm_attn'], 'w_out': out['w_out'], 'ffn2_norm': out['ffn2_norm'], 'ffn2_w_gu': out['ffn2_w_gu'], 'ffn2_w_down': out['ffn2_w_down'], 'final_norm': out['final_norm'], 'loss_target': out['loss_target'], 'm_meta_tokens': out['m_meta_tokens'], 'm_ffn1_norm': out['m_ffn1_norm'], 'm_ffn1_w_gu': out['m_ffn1_w_gu'], 'm_ffn1_w_down': out['m_ffn1_w_down'], 'm_mix_norm': out['m_mix_norm'], 'm_w_in': out['m_w_in'], 'm_conv_w': out['m_conv_w'], 'm_b_f': out['m_b_f'], 'm_out_norm_conv': out['m_out_norm_conv'], 'm_out_norm_attn': out['m_out_norm_attn'], 'm_w_out': out['m_w_out'], 'm_ffn2_norm': out['m_ffn2_norm'], 'm_ffn2_w_gu': out['m_ffn2_w_gu'], 'm_ffn2_w_down': out['m_ffn2_w_down'], 'm_final_norm': out['m_final_norm'], 'v_meta_tokens': out['v_meta_tokens'], 'v_ffn1_norm': out['v_ffn1_norm'], 'v_ffn1_w_gu': out['v_ffn1_w_gu'], 'v_ffn1_w_down': out['v_ffn1_w_down'], 'v_mix_norm': out['v_mix_norm'], 'v_w_in': out['v_w_in'], 'v_conv_w': out['v_conv_w'], 'v_b_f': out['v_b_f'], 'v_out_norm_conv': out['v_out_norm_conv'], 'v_out_norm_attn': out['v_out_norm_attn'], 'v_w_out': out['v_w_out'], 'v_ffn2_norm': out['v_ffn2_norm'], 'v_ffn2_w_gu': out['v_ffn2_w_gu'], 'v_ffn2_w_down': out['v_ffn2_w_down'], 'v_final_norm': out['v_final_norm']}


def _loss(weights, diff, rest, loss_target):
    with _jax.named_scope("forward"):
        args = {**rest, TWIN_DIFF_INPUT: diff, **{k: w.astype(_WEIGHT_DTYPES[k]) for k, w in weights.items()}}
        y = _forward(args)
    with _jax.named_scope("loss_head"):
        err = _jnp.square(y.astype(_jnp.float32) - loss_target)
        return 0.5 * _jnp.sum(_jnp.mean(err, axis=-1)) if err.ndim else 0.5 * err


def _adamw(w, g, m, v):
    m = ADAM_B1 * m + (1.0 - ADAM_B1) * g
    v = ADAM_B2 * v + (1.0 - ADAM_B2) * _jnp.square(g)
    m_hat = m / (1.0 - ADAM_B1 ** ADAM_STEP)
    v_hat = v / (1.0 - ADAM_B2 ** ADAM_STEP)
    delta = -ADAM_LR * (m_hat / (_jnp.sqrt(v_hat) + ADAM_EPS) + ADAM_WD * w)
    return delta, m, v


def reference(x, meta_tokens, ffn1_norm, ffn1_w_gu, ffn1_w_down, mix_norm, w_in, conv_w, b_f, out_norm_conv, out_norm_attn, w_out, ffn2_norm, ffn2_w_gu, ffn2_w_down, final_norm, loss_target, m_meta_tokens, m_ffn1_norm, m_ffn1_w_gu, m_ffn1_w_down, m_mix_norm, m_w_in, m_conv_w, m_b_f, m_out_norm_conv, m_out_norm_attn, m_w_out, m_ffn2_norm, m_ffn2_w_gu, m_ffn2_w_down, m_final_norm, v_meta_tokens, v_ffn1_norm, v_ffn1_w_gu, v_ffn1_w_down, v_mix_norm, v_w_in, v_conv_w, v_b_f, v_out_norm_conv, v_out_norm_attn, v_w_out, v_ffn2_norm, v_ffn2_w_gu, v_ffn2_w_down, v_final_norm):
    given = dict(x=x, meta_tokens=meta_tokens, ffn1_norm=ffn1_norm, ffn1_w_gu=ffn1_w_gu, ffn1_w_down=ffn1_w_down, mix_norm=mix_norm, w_in=w_in, conv_w=conv_w, b_f=b_f, out_norm_conv=out_norm_conv, out_norm_attn=out_norm_attn, w_out=w_out, ffn2_norm=ffn2_norm, ffn2_w_gu=ffn2_w_gu, ffn2_w_down=ffn2_w_down, final_norm=final_norm, loss_target=loss_target, m_meta_tokens=m_meta_tokens, m_ffn1_norm=m_ffn1_norm, m_ffn1_w_gu=m_ffn1_w_gu, m_ffn1_w_down=m_ffn1_w_down, m_mix_norm=m_mix_norm, m_w_in=m_w_in, m_conv_w=m_conv_w, m_b_f=m_b_f, m_out_norm_conv=m_out_norm_conv, m_out_norm_attn=m_out_norm_attn, m_w_out=m_w_out, m_ffn2_norm=m_ffn2_norm, m_ffn2_w_gu=m_ffn2_w_gu, m_ffn2_w_down=m_ffn2_w_down, m_final_norm=m_final_norm, v_meta_tokens=v_meta_tokens, v_ffn1_norm=v_ffn1_norm, v_ffn1_w_gu=v_ffn1_w_gu, v_ffn1_w_down=v_ffn1_w_down, v_mix_norm=v_mix_norm, v_w_in=v_w_in, v_conv_w=v_conv_w, v_b_f=v_b_f, v_out_norm_conv=v_out_norm_conv, v_out_norm_attn=v_out_norm_attn, v_w_out=v_w_out, v_ffn2_norm=v_ffn2_norm, v_ffn2_w_gu=v_ffn2_w_gu, v_ffn2_w_down=v_ffn2_w_down, v_final_norm=v_final_norm)
    weights = {n: given[n] for n in TWIN_WEIGHTS}
    shared = {n: given[n] for n in SHARED_INPUTS}
    per_example = {n: given[n] for n in ['x']}
    grad_fn = _jax.value_and_grad(_loss, argnums=(0, 1))

    def one_microbatch(ex, loss_target):
        ex = dict(ex)
        diff = ex.pop(TWIN_DIFF_INPUT)
        return grad_fn(weights, diff, {**shared, **ex}, loss_target)

    if N_MICROBATCH == 1:
        loss, (grad_w, grad_x) = one_microbatch(per_example, given["loss_target"])
    else:
        def body(carry, xs):
            loss_sum, grad_sum = carry
            l_k, (gw_k, gx_k) = one_microbatch(xs[0], xs[1])
            with _jax.named_scope("update"):
                return (loss_sum + l_k, _jax.tree.map(_jnp.add, grad_sum, gw_k)), gx_k

        init = (_jnp.zeros((), _jnp.float32), _jax.tree.map(_jnp.zeros_like, weights))
        (loss, grad_w), grad_x = _jax.lax.scan(body, init, (per_example, given["loss_target"]))
    with _jax.named_scope("update"):
        delta_w, new_m, new_v = {}, {}, {}
        for n in TWIN_WEIGHTS:
            delta_w[n], new_m[n], new_v[n] = _adamw(weights[n], grad_w[n], given["m_" + n], given["v_" + n])
    return (loss, grad_x, *[grad_w[n] for n in TWIN_WEIGHTS], *[delta_w[n] for n in TWIN_WEIGHTS],
            *[new_m[n] for n in TWIN_WEIGHTS], *[new_v[n] for n in TWIN_WEIGHTS])
```

```python
import functools

import jax
import jax.numpy as jnp
from jax import lax
from jax.experimental import pallas as pl
from jax.experimental.pallas import tpu as pltpu

F32 = jnp.float32
BF16 = jnp.bfloat16

EPS = 1e-6
N_META = 16
HEAD_DIM = 64
N_SHARD = 4
N_DEV = 8
HALO = 16
LANES = 128
SMALL_ROWS = 32
VMEM_LIMIT_V7X = 56 * 1024 * 1024
NEG = -1e30

ADAM_LR = 0.001
ADAM_B1 = 0.9
ADAM_B2 = 0.999
ADAM_EPS = 1e-08
ADAM_WD = 0.01
ADAM_STEP = 10

MESH = pl.DeviceIdType.MESH
ANY = pl.BlockSpec(memory_space=pl.ANY)
NT_DIMS = (((1,), (1,)), ((), ()))
TN_DIMS = (((0,), (0,)), ((), ()))


def _params(*sem):
    return pltpu.CompilerParams(dimension_semantics=sem, vmem_limit_bytes=VMEM_LIMIT_V7X)


def _chunks(width, step=512):
    out, c0 = [], 0
    while c0 < width:
        cw = min(step, width - c0)
        out.append((c0, cw))
        c0 += cw
    return out


def _split2(v):
    hi = v.astype(BF16)
    lo = (v - hi.astype(F32)).astype(BF16)
    return hi, lo


def _split3(v):
    hi = v.astype(BF16)
    r = v - hi.astype(F32)
    mid = r.astype(BF16)
    lo = (r - mid.astype(F32)).astype(BF16)
    return hi, mid, lo


def _dot(a, b):
    return jnp.dot(a, b, preferred_element_type=F32)


def _dot_nt(a, b):
    return lax.dot_general(a, b, NT_DIMS, preferred_element_type=F32)


def _dot_tn(a, b):
    return lax.dot_general(a, b, TN_DIMS, preferred_element_type=F32)


def _silu_mul(g, u):
    return g * jax.nn.sigmoid(g) * u


def _rms_bwd(dn, h, gain, dres):
    r = lax.rsqrt(jnp.mean(h * h, axis=-1, keepdims=True) + EPS)
    y = h * r
    dgain = jnp.sum(dn * y, axis=0, keepdims=True)
    dy = dn * gain
    dh = dres + r * (dy - y * jnp.mean(dy * y, axis=-1, keepdims=True))
    return dh, dgain


def _group_mean(v, p):
    hi, lo = _split2(v)
    return _dot(hi, p) + _dot(lo, p)


def _row_of(a, k):
    rows = lax.broadcasted_iota(jnp.int32, a.shape, 0)
    return jnp.sum(jnp.where(rows == k, a, 0.0), axis=0, keepdims=True)


def _causal_conv(u, prev, w):
    rows = lax.broadcasted_iota(jnp.int32, u.shape, 0)
    p1 = _row_of(prev, HALO - 1)
    p2 = _row_of(prev, HALO - 2)
    u1 = jnp.where(rows == 0, p1, pltpu.roll(u, 1, 0))
    u2 = jnp.where(rows == 0, p2, jnp.where(rows == 1, p1, pltpu.roll(u, 2, 0)))
    return w[2:3, :] * u + w[1:2, :] * u1 + w[0:1, :] * u2, u1, u2


def _rmsnorm(h, g, *, tm, name):
    T, D = h.shape

    def body(h_ref, g_ref, n_ref):
        x = h_ref[...]
        r = lax.rsqrt(jnp.mean(x * x, axis=-1, keepdims=True) + EPS)
        n_ref[...] = (x * r * g_ref[...]).astype(BF16)

    return pl.pallas_call(
        body, name=name, grid=(T // tm,),
        in_specs=[pl.BlockSpec((tm, D), lambda i: (i, 0)), pl.BlockSpec((1, D), lambda i: (0, 0))],
        out_specs=pl.BlockSpec((tm, D), lambda i: (i, 0)),
        out_shape=jax.ShapeDtypeStruct((T, D), BF16),
        compiler_params=_params("arbitrary"),
    )(h, g)


def _matmul_nn(x, w, *, tm, nb, w_spec, out_shape, out_spec, name):
    T, K = x.shape

    def body(x_ref, w_ref, o_ref):
        o_ref[...] = _dot(x_ref[...], w_ref[...]).astype(o_ref.dtype)

    return pl.pallas_call(
        body, name=name, grid=(nb, T // tm),
        in_specs=[pl.BlockSpec((tm, K), lambda s, i: (i, 0)), w_spec],
        out_specs=out_spec, out_shape=out_shape,
        compiler_params=_params("arbitrary", "arbitrary"),
    )(x, w)


def _ffn_down(gu, wd, h, *, tm, name):
    _, T, ff = gu.shape
    D = h.shape[1]
    chunks = _chunks(ff)

    def body(g_ref, u_ref, wd_hbm, h_ref, o_ref, wd_v, sem):
        @pl.when(pl.program_id(0) == 0)
        def _():
            cp = pltpu.make_async_copy(wd_hbm, wd_v, sem)
            cp.start()
            cp.wait()

        acc = jnp.zeros((tm, D), F32)
        for c0, cw in chunks:
            a = _silu_mul(g_ref[:, c0:c0 + cw].astype(F32), u_ref[:, c0:c0 + cw].astype(F32))
            acc = acc + _dot(a.astype(BF16), wd_v[c0:c0 + cw, :])
        o_ref[...] = h_ref[...] + 0.5 * acc

    return pl.pallas_call(
        body, name=name, grid=(T // tm,),
        in_specs=[pl.BlockSpec((None, tm, ff), lambda i: (0, i, 0)),
                  pl.BlockSpec((None, tm, ff), lambda i: (1, i, 0)),
                  ANY,
                  pl.BlockSpec((tm, D), lambda i: (i, 0))],
        out_specs=pl.BlockSpec((tm, D), lambda i: (i, 0)),
        out_shape=jax.ShapeDtypeStruct((T, D), F32),
        scratch_shapes=[pltpu.VMEM((ff, D), BF16), pltpu.SemaphoreType.DMA],
        compiler_params=_params("arbitrary"),
    )(gu, gu, wd, h)


def _ffn_bwd_act(df, gu, wd, *, tm, guc, name):
    _, T, ff = gu.shape
    D = df.shape[1]
    nj = ff // guc
    chunks = _chunks(guc)

    def body(df_ref, g_ref, u_ref, wd_ref, o_ref):
        dfv = df_ref[...]
        for c0, cw in chunks:
            da = _dot_nt(dfv, wd_ref[c0:c0 + cw, :])
            g = g_ref[:, c0:c0 + cw].astype(F32)
            u = u_ref[:, c0:c0 + cw].astype(F32)
            sg = jax.nn.sigmoid(g)
            silu = g * sg
            o_ref[0, :, c0:c0 + cw] = (da * u * (sg * (1.0 + g * (1.0 - sg)))).astype(BF16)
            o_ref[1, :, c0:c0 + cw] = (da * silu).astype(BF16)

    return pl.pallas_call(
        body, name=name, grid=(nj, T // tm),
        in_specs=[pl.BlockSpec((tm, D), lambda j, i: (i, 0)),
                  pl.BlockSpec((None, tm, guc), lambda j, i: (0, i, j)),
                  pl.BlockSpec((None, tm, guc), lambda j, i: (1, i, j)),
                  pl.BlockSpec((guc, D), lambda j, i: (j, 0))],
        out_specs=pl.BlockSpec((2, tm, guc), lambda j, i: (0, i, j)),
        out_shape=jax.ShapeDtypeStruct((2, T, ff), BF16),
        compiler_params=_params("arbitrary", "arbitrary"),
    )(df, gu, gu, wd)


def _ffn_bwd_in(dgu, wgu, h, g, dres, *, tm, scale, name):
    _, T, ff = dgu.shape
    ns, D, guc = wgu.shape
    nj = ff // guc
    chunks = _chunks(guc)

    def body(dgu_ref, w_hbm, h_ref, g_ref, dres_ref, dh_ref, dhb_ref, dg_ref, w_v, acc, sem):
        i, j = pl.program_id(0), pl.program_id(1)

        @pl.when((i == 0) & (j == 0))
        def _():
            cp = pltpu.make_async_copy(w_hbm, w_v, sem)
            cp.start()
            cp.wait()
            dg_ref[...] = jnp.zeros_like(dg_ref)

        part = jnp.zeros((tm, D), F32)
        for c0, cw in chunks:
            part = part + _dot_nt(dgu_ref[0, :, c0:c0 + cw], w_v[j, :, c0:c0 + cw])
            part = part + _dot_nt(dgu_ref[1, :, c0:c0 + cw], w_v[nj + j, :, c0:c0 + cw])

        @pl.when(j == 0)
        def _():
            acc[...] = part

        @pl.when(j > 0)
        def _():
            acc[...] += part

        @pl.when(j == nj - 1)
        def _():
            dh, dgain = _rms_bwd(acc[...], h_ref[...], g_ref[...], dres_ref[...])
            dh_ref[...] = dh
            dhb_ref[...] = (scale * dh).astype(BF16)
            dg_ref[...] += dgain

    return pl.pallas_call(
        body, name=name, grid=(T // tm, nj),
        in_specs=[pl.BlockSpec((2, tm, guc), lambda i, j: (0, i, j)),
                  ANY,
                  pl.BlockSpec((tm, D), lambda i, j: (i, 0)),
                  pl.BlockSpec((1, D), lambda i, j: (0, 0)),
                  pl.BlockSpec((tm, D), lambda i, j: (i, 0))],
        out_specs=[pl.BlockSpec((tm, D), lambda i, j: (i, 0)),
                   pl.BlockSpec((tm, D), lambda i, j: (i, 0)),
                   pl.BlockSpec((1, D), lambda i, j: (0, 0))],
        out_shape=[jax.ShapeDtypeStruct((T, D), F32), jax.ShapeDtypeStruct((T, D), BF16),
                   jax.ShapeDtypeStruct((1, D), F32)],
        scratch_shapes=[pltpu.VMEM((ns, D, guc), BF16), pltpu.VMEM((tm, D), F32), pltpu.SemaphoreType.DMA],
        compiler_params=_params("arbitrary", "arbitrary"),
    )(dgu, wgu, h, g, dres)


def _mix_bwd_in(parts, w_main, w_fg, h, g, dres, *, tm, scale, name):
    T, D = h.shape
    widths = [p.shape[1] for p in parts[:-1]]
    offs = [sum(widths[:k]) for k in range(len(widths))]
    npart = len(parts)

    def body(*refs):
        p_refs = refs[:npart]
        wm_ref, wf_ref, h_ref, g_ref, dres_ref, dh_ref, dhb_ref, dg_ref = refs[npart:]

        @pl.when(pl.program_id(0) == 0)
        def _():
            dg_ref[...] = jnp.zeros_like(dg_ref)

        dn = _dot_nt(p_refs[-1][...].astype(BF16), wf_ref[...])
        for p_ref, off, wd_ in zip(p_refs[:-1], offs, widths):
            for c0, cw in _chunks(wd_):
                dn = dn + _dot_nt(p_ref[:, c0:c0 + cw].astype(BF16), wm_ref[:, off + c0:off + c0 + cw])
        dh, dgain = _rms_bwd(dn, h_ref[...], g_ref[...], dres_ref[...])
        dh_ref[...] = dh
        dhb_ref[...] = (scale * dh).astype(BF16)
        dg_ref[...] += dgain

    row = lambda i: (i, 0)
    const = lambda i: (0, 0)
    return pl.pallas_call(
        body, name=name, grid=(T // tm,),
        in_specs=[pl.BlockSpec((tm, p.shape[1]), row) for p in parts]
                 + [pl.BlockSpec(w_main.shape, const), pl.BlockSpec(w_fg.shape, const),
                    pl.BlockSpec((tm, D), row), pl.BlockSpec((1, D), const), pl.BlockSpec((tm, D), row)],
        out_specs=[pl.BlockSpec((tm, D), row), pl.BlockSpec((tm, D), row), pl.BlockSpec((1, D), const)],
        out_shape=[jax.ShapeDtypeStruct((T, D), F32), jax.ShapeDtypeStruct((T, D), BF16),
                   jax.ShapeDtypeStruct((1, D), F32)],
        compiler_params=_params("arbitrary"),
    )(*parts, w_main, w_fg, h, g, dres)


def _matmul_tn(xs, y, *, tm, nb, x_specs, y_spec, out_shape, out_spec, kb, silu, name):
    T = y.shape[-2]
    nx = len(xs)
    chunks = _chunks(kb)

    def body(*refs):
        x_refs, y_ref, o_ref = refs[:nx], refs[nx], refs[nx + 1]
        i = pl.program_id(1)

        @pl.when(i == 0)
        def _():
            o_ref[...] = jnp.zeros_like(o_ref)

        yv = y_ref[...].astype(BF16)
        for c0, cw in chunks:
            if silu:
                xv = _silu_mul(x_refs[0][:, c0:c0 + cw].astype(F32), x_refs[1][:, c0:c0 + cw].astype(F32)).astype(BF16)
            else:
                xv = x_refs[0][:, c0:c0 + cw]
            o_ref[c0:c0 + cw, :] += _dot_tn(xv, yv)

    return pl.pallas_call(
        body, name=name, grid=(nb, T // tm),
        in_specs=list(x_specs) + [y_spec], out_specs=out_spec, out_shape=out_shape,
        compiler_params=_params("arbitrary", "arbitrary"),
    )(*xs, y)


def _tri(n, lower):
    r = lax.broadcasted_iota(jnp.int32, (n, n), 0)
    c = lax.broadcasted_iota(jnp.int32, (n, n), 1)
    return jnp.where((r >= c) if lower else (r <= c), 1.0, 0.0).astype(BF16)


def _tri_dot(tri, v):
    hi, mid, lo = _split3(v)
    return _dot(tri, hi) + _dot(tri, mid) + _dot(tri, lo)


def _fcum(fg, bf, *, ch, name):
    B, L, W = fg.shape
    nch = L // ch

    def body(fg_ref, bf_ref, f_ref):
        tri = _tri(ch, True)
        carry = jnp.zeros((1, W), F32)
        for c in range(nch):
            x = fg_ref[c * ch:(c + 1) * ch, :] + bf_ref[...]
            lf = jnp.minimum(x, 0.0) - jnp.log(1.0 + jnp.exp(-jnp.abs(x)))
            f_ref[c * ch:(c + 1) * ch, :] = _tri_dot(tri, lf) + carry
            carry = carry + jnp.sum(lf, axis=0, keepdims=True)

    return pl.pallas_call(
        body, name=name, grid=(B,),
        in_specs=[pl.BlockSpec((None, L, W), lambda b: (b, 0, 0)), pl.BlockSpec((1, W), lambda b: (0, 0))],
        out_specs=pl.BlockSpec((None, L, W), lambda b: (b, 0, 0)),
        out_shape=jax.ShapeDtypeStruct((B, L, W), F32),
        compiler_params=_params("arbitrary"),
    )(fg, bf)


def _fcum_bwd(dF, fg, bf, *, ch, name):
    B, L, W = fg.shape
    nch = L // ch

    def body(df_ref, fg_ref, bf_ref, dfg_ref, db_ref):
        @pl.when(pl.program_id(0) == 0)
        def _():
            db_ref[...] = jnp.zeros_like(db_ref)

        tri = _tri(ch, False)
        carry = jnp.zeros((1, W), F32)
        dbs = jnp.zeros((1, W), F32)
        for c in reversed(range(nch)):
            d = df_ref[c * ch:(c + 1) * ch, :]
            dlf = _tri_dot(tri, d) + carry
            carry = carry + jnp.sum(d, axis=0, keepdims=True)
            x = fg_ref[c * ch:(c + 1) * ch, :] + bf_ref[...]
            dfg = dlf * jax.nn.sigmoid(-x)
            dfg_ref[c * ch:(c + 1) * ch, :] = dfg.astype(BF16)
            dbs = dbs + jnp.sum(dfg, axis=0, keepdims=True)
        db_ref[...] += dbs

    blk = pl.BlockSpec((None, L, W), lambda b: (b, 0, 0))
    return pl.pallas_call(
        body, name=name, grid=(B,),
        in_specs=[blk, blk, pl.BlockSpec((1, W), lambda b: (0, 0))],
        out_specs=[blk, pl.BlockSpec((1, W), lambda b: (0, 0))],
        out_shape=[jax.ShapeDtypeStruct((B, L, W), BF16), jax.ShapeDtypeStruct((1, W), F32)],
        compiler_params=_params("arbitrary"),
    )(dF, fg, bf)


def _attn_fwd(proj, fc, fr, *, tq, n_heads, name):
    B, L, _ = proj.shape
    AD = n_heads * HEAD_DIM
    nq = L // tq
    W = fc.shape[-1]
    scale = HEAD_DIM ** -0.5

    def body(q_ref, k_ref, v_ref, fc_ref, fr_ref, o_ref, lse_ref, m_s, l_s, acc_s):
        qi, ki = pl.program_id(1), pl.program_id(2)

        @pl.when(ki == 0)
        def _():
            m_s[...] = jnp.full_like(m_s, NEG)
            l_s[...] = jnp.zeros_like(l_s)
            acc_s[...] = jnp.zeros_like(acc_s)

        @pl.when(ki <= qi)
        def _():
            row = qi * tq + lax.broadcasted_iota(jnp.int32, (tq, tq), 0)
            col = ki * tq + lax.broadcasted_iota(jnp.int32, (tq, tq), 1)
            mask = col <= row
            for h in range(n_heads):
                sl = slice(h * HEAD_DIM, (h + 1) * HEAD_DIM)
                s = _dot_nt(q_ref[:, sl], k_ref[:, sl]) * scale
                s = s + fc_ref[:, h:h + 1] - fr_ref[h:h + 1, :]
                s = jnp.where(mask, s, NEG)
                m_old = m_s[:, h:h + 1]
                m_new = jnp.maximum(m_old, jnp.max(s, axis=1, keepdims=True))
                alpha = jnp.exp(m_old - m_new)
                p = jnp.exp(s - m_new)
                l_s[:, h:h + 1] = alpha * l_s[:, h:h + 1] + jnp.sum(p, axis=1, keepdims=True)
                acc_s[:, sl] = alpha * acc_s[:, sl] + _dot(p.astype(BF16), v_ref[:, sl])
                m_s[:, h:h + 1] = m_new

        @pl.when(ki == qi)
        def _():
            for h in range(n_heads):
                sl = slice(h * HEAD_DIM, (h + 1) * HEAD_DIM)
                o_ref[:, sl] = acc_s[:, sl] / l_s[:, h:h + 1]
            l = l_s[...]
            lse_ref[...] = jnp.where(l > 0.0, m_s[...] + jnp.log(jnp.where(l > 0.0, l, 1.0)), 0.0)

    kv = lambda b, qi, ki: jnp.minimum(ki, qi)
    return pl.pallas_call(
        body, name=name, grid=(B, nq, nq),
        in_specs=[pl.BlockSpec((None, tq, AD), lambda b, qi, ki: (b, qi, 3)),
                  pl.BlockSpec((None, tq, AD), lambda b, qi, ki: (b, kv(b, qi, ki), 4)),
                  pl.BlockSpec((None, tq, AD), lambda b, qi, ki: (b, kv(b, qi, ki), 5)),
                  pl.BlockSpec((None, tq, W), lambda b, qi, ki: (b, qi, 0)),
                  pl.BlockSpec((None, None, n_heads, tq), lambda b, qi, ki: (b, kv(b, qi, ki), 0, 0))],
        out_specs=[pl.BlockSpec((None, tq, AD), lambda b, qi, ki: (b, qi, 0)),
                   pl.BlockSpec((None, tq, W), lambda b, qi, ki: (b, qi, 0))],
        out_shape=[jax.ShapeDtypeStruct((B, L, AD), F32), jax.ShapeDtypeStruct((B, L, W), F32)],
        scratch_shapes=[pltpu.VMEM((tq, W), F32), pltpu.VMEM((tq, W), F32), pltpu.VMEM((tq, AD), F32)],
        compiler_params=_params("arbitrary", "arbitrary", "arbitrary"),
    )(proj, proj, proj, fc, fr)


def _attn_bwd(proj, o, do, lse, fc, fr, *, tq, n_heads, name):
    B, L, _ = proj.shape
    AD = n_heads * HEAD_DIM
    nq = L // tq
    W = fc.shape[-1]
    scale = HEAD_DIM ** -0.5

    def body(q_ref, k_ref, v_ref, o_ref, do_ref, lse_ref, fc_ref, fr_ref,
             dq_ref, dk_ref, dv_ref, dfr_ref, dfq_ref, dk_s, dv_s):
        kj, qi = pl.program_id(1), pl.program_id(2)

        @pl.when((kj == 0) & (qi == 0))
        def _():
            dq_ref[...] = jnp.zeros_like(dq_ref)
            dfq_ref[...] = jnp.zeros_like(dfq_ref)

        @pl.when(qi == kj)
        def _():
            dk_s[...] = jnp.zeros_like(dk_s)
            dv_s[...] = jnp.zeros_like(dv_s)
            dfr_ref[...] = jnp.zeros_like(dfr_ref)

        @pl.when(qi >= kj)
        def _():
            row = qi * tq + lax.broadcasted_iota(jnp.int32, (tq, tq), 0)
            col = kj * tq + lax.broadcasted_iota(jnp.int32, (tq, tq), 1)
            mask = col <= row
            rows = pl.ds(pl.multiple_of(qi * tq, 8), tq)
            for h in range(n_heads):
                sl = slice(h * HEAD_DIM, (h + 1) * HEAD_DIM)
                q, k, v, dov = q_ref[:, sl], k_ref[:, sl], v_ref[:, sl], do_ref[:, sl]
                s = _dot_nt(q, k) * scale
                s = s + fc_ref[:, h:h + 1] - fr_ref[h:h + 1, :]
                s = jnp.where(mask, s, NEG)
                p = jnp.exp(s - lse_ref[:, h:h + 1])
                dp = _dot_nt(dov, v)
                dsum = jnp.sum(dov.astype(F32) * o_ref[:, sl].astype(F32), axis=1, keepdims=True)
                ds = p * (dp - dsum)
                dsb = ds.astype(BF16)
                dv_s[:, sl] += _dot_tn(p.astype(BF16), dov)
                dk_s[:, sl] += _dot_tn(dsb, q) * scale
                dq_ref[rows, sl] += _dot(dsb, k) * scale
                dfr_ref[h:h + 1, :] -= jnp.sum(ds, axis=0, keepdims=True)
                dfq_ref[rows, h:h + 1] += jnp.sum(ds, axis=1, keepdims=True)

        @pl.when(qi == nq - 1)
        def _():
            dk_ref[...] = dk_s[...].astype(BF16)
            dv_ref[...] = dv_s[...].astype(BF16)

    qq = lambda b, kj, qi: jnp.maximum(qi, kj)
    qblk = lambda w, cb: pl.BlockSpec((None, tq, w), lambda b, kj, qi: (b, qq(b, kj, qi), cb))
    kblk = lambda cb: pl.BlockSpec((None, tq, AD), lambda b, kj, qi: (b, kj, cb))
    return pl.pallas_call(
        body, name=name, grid=(B, nq, nq),
        in_specs=[qblk(AD, 3), kblk(4), kblk(5), qblk(AD, 0), qblk(AD, 0), qblk(W, 0), qblk(W, 0),
                  pl.BlockSpec((None, None, n_heads, tq), lambda b, kj, qi: (b, kj, 0, 0))],
        out_specs=[pl.BlockSpec((None, L, AD), lambda b, kj, qi: (b, 0, 0)),
                   kblk(0), kblk(0),
                   pl.BlockSpec((None, None, n_heads, tq), lambda b, kj, qi: (b, kj, 0, 0)),
                   pl.BlockSpec((None, L, W), lambda b, kj, qi: (b, 0, 0))],
        out_shape=[jax.ShapeDtypeStruct((B, L, AD), F32), jax.ShapeDtypeStruct((B, L, AD), BF16),
                   jax.ShapeDtypeStruct((B, L, AD), BF16), jax.ShapeDtypeStruct((B, nq, n_heads, tq), F32),
                   jax.ShapeDtypeStruct((B, L, W), F32)],
        scratch_shapes=[pltpu.VMEM((tq, AD), F32), pltpu.VMEM((tq, AD), F32)],
        compiler_params=_params("arbitrary", "arbitrary", "arbitrary"),
    )(proj, proj, proj, o, do, lse, fc, fr)


def _mix_gather(refs, first):
    b_ref, c_ref, hc_ref, cp_ref, hcp_ref, o_ref, cw_ref, p_ref = refs
    bg = b_ref[...].astype(F32)
    u = c_ref[...].astype(F32) * hc_ref[...].astype(F32)
    prev = cp_ref[...].astype(F32) * hcp_ref[...].astype(F32)
    prev = jnp.where(first, 0.0, prev)
    cv, u1, u2 = _causal_conv(u, prev, cw_ref[...])
    yc = bg * cv
    p = p_ref[...]
    rc = lax.rsqrt(_group_mean(yc * yc, p) + EPS)
    ya = o_ref[...].astype(F32)
    ra = lax.rsqrt(_group_mean(ya * ya, p) + EPS)
    return bg, (u, u1, u2), cv, yc * rc, rc, ya * ra, ra


def _mix_specs(tm, CD, D, grid_rank_fn):
    per = tm // HALO
    cur = lambda cb: pl.BlockSpec((None, tm, CD), lambda b, i: (b, i, cb))
    prev = lambda cb: pl.BlockSpec((None, HALO, CD), lambda b, i: (b, jnp.maximum(i * per - 1, 0), cb))
    return [cur(0), cur(1), cur(2), prev(1), prev(2), cur(0)]


def _mix_out(proj, o, cw, gc, ga, wout, h, pmat, *, tm, name):
    B, L, D = h.shape
    CD = o.shape[-1]
    const = lambda b, i: (0, 0)

    def body(b_ref, c_ref, hc_ref, cp_ref, hcp_ref, o_ref, cw_ref, p_ref, gc_ref, ga_ref, w_ref, h_ref,
             out_ref, y_ref):
        first = pl.program_id(1) == 0
        _, _, _, zc, _, za, _ = _mix_gather((b_ref, c_ref, hc_ref, cp_ref, hcp_ref, o_ref, cw_ref, p_ref), first)
        yc = (zc * gc_ref[...]).astype(BF16)
        ya = (za * ga_ref[...]).astype(BF16)
        y_ref[:, :CD] = yc
        y_ref[:, CD:] = ya
        out_ref[...] = h_ref[...] + _dot(yc, w_ref[:CD, :]) + _dot(ya, w_ref[CD:, :])

    return pl.pallas_call(
        body, name=name, grid=(B, L // tm),
        in_specs=_mix_specs(tm, CD, D, None)
                 + [pl.BlockSpec(cw.shape, const), pl.BlockSpec(pmat.shape, const),
                    pl.BlockSpec((1, CD), const), pl.BlockSpec((1, CD), const), pl.BlockSpec((D, D), const),
                    pl.BlockSpec((None, tm, D), lambda b, i: (b, i, 0))],
        out_specs=[pl.BlockSpec((None, tm, D), lambda b, i: (b, i, 0)),
                   pl.BlockSpec((None, tm, D), lambda b, i: (b, i, 0))],
        out_shape=[jax.ShapeDtypeStruct((B, L, D), F32), jax.ShapeDtypeStruct((B, L, D), BF16)],
        compiler_params=_params("arbitrary", "arbitrary"),
    )(proj, proj, proj, proj, proj, o, cw, pmat, gc, ga, wout, h)


def _mix_out_bwd(dhb, proj, o, cw, gc, ga, wout, pmat, *, tm, name):
    B, L, D = dhb.shape
    CD = o.shape[-1]
    const = lambda b, i: (0, 0)

    def body(dh_ref, b_ref, c_ref, hc_ref, cp_ref, hcp_ref, o_ref, cw_ref, p_ref, gc_ref, ga_ref, w_ref,
             db_ref, dcv_ref, do_ref, dgc_ref, dga_ref, dcw_ref):
        first = pl.program_id(1) == 0

        @pl.when((pl.program_id(0) == 0) & first)
        def _():
            dgc_ref[...] = jnp.zeros_like(dgc_ref)
            dga_ref[...] = jnp.zeros_like(dga_ref)
            dcw_ref[...] = jnp.zeros_like(dcw_ref)

        bg, us, cv, zc, rc, za, ra = _mix_gather(
            (b_ref, c_ref, hc_ref, cp_ref, hcp_ref, o_ref, cw_ref, p_ref), first)
        p = p_ref[...]
        dh = dh_ref[...]
        dyc = _dot_nt(dh, w_ref[:CD, :])
        dya = _dot_nt(dh, w_ref[CD:, :])

        dgc_ref[...] += jnp.sum(dyc * zc, axis=0, keepdims=True)
        dz = dyc * gc_ref[...]
        dx = rc * (dz - zc * _group_mean(dz * zc, p))
        db_ref[...] = (dx * cv).astype(BF16)
        dcv = dx * bg
        dcv_ref[...] = dcv.astype(BF16)
        for k in range(3):
            dcw_ref[k:k + 1, :] += jnp.sum(dcv * us[2 - k], axis=0, keepdims=True)

        dga_ref[...] += jnp.sum(dya * za, axis=0, keepdims=True)
        dz = dya * ga_ref[...]
        do_ref[...] = (ra * (dz - za * _group_mean(dz * za, p))).astype(BF16)

    tile = lambda w: pl.BlockSpec((None, tm, w), lambda b, i: (b, i, 0))
    return pl.pallas_call(
        body, name=name, grid=(B, L // tm),
        in_specs=[tile(D)] + _mix_specs(tm, CD, D, None)
                 + [pl.BlockSpec(cw.shape, const), pl.BlockSpec(pmat.shape, const),
                    pl.BlockSpec((1, CD), const), pl.BlockSpec((1, CD), const), pl.BlockSpec((D, D), const)],
        out_specs=[tile(CD), tile(CD), tile(CD),
                   pl.BlockSpec((1, CD), const), pl.BlockSpec((1, CD), const), pl.BlockSpec((8, CD), const)],
        out_shape=[jax.ShapeDtypeStruct((B, L, CD), BF16)] * 3
                  + [jax.ShapeDtypeStruct((1, CD), F32)] * 2 + [jax.ShapeDtypeStruct((8, CD), F32)],
        compiler_params=_params("arbitrary", "arbitrary"),
    )(dhb, proj, proj, proj, proj, proj, o, cw, pmat, gc, ga, wout)


def _conv_bwd(dcv, proj, cw, *, tm, name):
    B, L, CD = dcv.shape
    per = tm // HALO
    nhalo = L // HALO
    nt = L // tm

    def body(d_ref, dn_ref, c_ref, hc_ref, cw_ref, out_ref):
        last = pl.program_id(1) == nt - 1
        d = d_ref[...].astype(F32)
        nxt = jnp.where(last, 0.0, dn_ref[...].astype(F32))
        n0, n1 = _row_of(nxt, 0), _row_of(nxt, 1)
        rows = lax.broadcasted_iota(jnp.int32, d.shape, 0)
        d1 = jnp.where(rows == tm - 1, n0, pltpu.roll(d, tm - 1, 0))
        d2 = jnp.where(rows == tm - 2, n0, jnp.where(rows == tm - 1, n1, pltpu.roll(d, tm - 2, 0)))
        w = cw_ref[...]
        du = w[2:3, :] * d + w[1:2, :] * d1 + w[0:1, :] * d2
        out_ref[:, :CD] = (du * hc_ref[...].astype(F32)).astype(BF16)
        out_ref[:, CD:] = (du * c_ref[...].astype(F32)).astype(BF16)

    return pl.pallas_call(
        body, name=name, grid=(B, nt),
        in_specs=[pl.BlockSpec((None, tm, CD), lambda b, i: (b, i, 0)),
                  pl.BlockSpec((None, HALO, CD), lambda b, i: (b, jnp.minimum((i + 1) * per, nhalo - 1), 0)),
                  pl.BlockSpec((None, tm, CD), lambda b, i: (b, i, 1)),
                  pl.BlockSpec((None, tm, CD), lambda b, i: (b, i, 2)),
                  pl.BlockSpec(cw.shape, lambda b, i: (0, 0))],
        out_specs=pl.BlockSpec((None, tm, 2 * CD), lambda b, i: (b, i, 0)),
        out_shape=jax.ShapeDtypeStruct((B, L, 2 * CD), BF16),
        compiler_params=_params("arbitrary", "arbitrary"),
    )(dcv, dcv, proj, proj, cw)


def _final(h, gf, tgt, *, tm, name):
    B, L, D = h.shape

    def body(h_ref, g_ref, t_ref, dh_ref, dhb_ref, dg_ref, loss_ref):
        b, i = pl.program_id(0), pl.program_id(1)

        @pl.when((b == 0) & (i == 0))
        def _():
            dg_ref[...] = jnp.zeros_like(dg_ref)
            loss_ref[...] = jnp.zeros_like(loss_ref)

        x = h_ref[...]
        g = g_ref[...]
        r = lax.rsqrt(jnp.mean(x * x, axis=-1, keepdims=True) + EPS)
        y = x * r
        pos = i * tm + lax.broadcasted_iota(jnp.int32, (tm, 1), 0)
        err = jnp.where(pos >= N_META, y * g - t_ref[...], 0.0)
        loss_ref[...] += 0.5 * jnp.sum(jnp.mean(err * err, axis=-1, keepdims=True))
        dout = err / D
        dg_ref[...] += jnp.sum(dout * y, axis=0, keepdims=True)
        dy = dout * g
        dh = r * (dy - y * jnp.mean(dy * y, axis=-1, keepdims=True))
        dh_ref[...] = dh
        dhb_ref[...] = (0.5 * dh).astype(BF16)

    tile = pl.BlockSpec((None, tm, D), lambda b, i: (b, i, 0))
    const = lambda b, i: (0, 0)
    return pl.pallas_call(
        body, name=name, grid=(B, L // tm),
        in_specs=[tile, pl.BlockSpec((1, D), const), tile],
        out_specs=[tile, tile, pl.BlockSpec((1, D), const), pl.BlockSpec((1, LANES), const)],
        out_shape=[jax.ShapeDtypeStruct((B, L, D), F32), jax.ShapeDtypeStruct((B, L, D), BF16),
                   jax.ShapeDtypeStruct((1, D), F32), jax.ShapeDtypeStruct((1, LANES), F32)],
        compiler_params=_params("arbitrary", "arbitrary"),
    )(h, gf, tgt)


def _place():
    x, y, c = lax.axis_index("x"), lax.axis_index("y"), lax.axis_index("c")
    others = [(1 - x, y), (x, 1 - y), (1 - x, 1 - y)]
    return x, y, c, others


def _all_gather_shards(shards, *, name):
    n = len(shards)

    def body(*refs):
        ins, outs = refs[:n], refs[n:2 * n]
        send, recv, fsend, frecv, lsem = refs[2 * n:]
        x, y, c, others = _place()
        me = 2 * x + y
        local = [pltpu.make_async_copy(ins[t], outs[t].at[me], lsem.at[t]) for t in range(n)]
        for cp in local:
            cp.start()

        def half(t, k):
            hr = shards[t].shape[0] // 2
            return pl.ds(pl.multiple_of(k * hr, HALO), hr)

        def ici(t, j, src_chip, to):
            src = ins[t].at[half(t, c)] if to is not None else outs[t].at[src_chip, half(t, c)]
            return pltpu.make_async_remote_copy(
                src_ref=src, dst_ref=outs[t].at[src_chip, half(t, c)],
                send_sem=send.at[3 * t + j], recv_sem=recv.at[3 * t + j],
                device_id=(x, y, c) if to is None else to, device_id_type=MESH)

        def d2d(t, j, src_chip, k):
            return pltpu.make_async_remote_copy(
                src_ref=outs[t].at[src_chip, half(t, k)], dst_ref=outs[t].at[src_chip, half(t, k)],
                send_sem=fsend.at[3 * t + j], recv_sem=frecv.at[3 * t + j],
                device_id=(x, y, 1 - c), device_id_type=MESH)

        firsts = [ici(t, j, me, (ox, oy, c)) for t in range(n) for j, (ox, oy) in enumerate(others)]
        for cp in firsts:
            cp.start()
        passed = []
        for t in range(n):
            for j, (ox, oy) in enumerate(others):
                ici(t, j, 2 * ox + oy, None).wait_recv()
                cp = d2d(t, j, 2 * ox + oy, c)
                cp.start()
                passed.append(cp)
        for t in range(n):
            for j, (ox, oy) in enumerate(others):
                d2d(t, j, 2 * ox + oy, 1 - c).wait_recv()
        for cp in firsts + passed:
            cp.wait_send()
        for cp in local:
            cp.wait()

    return pl.pallas_call(
        body, name=name,
        in_specs=[ANY] * n, out_specs=[ANY] * n,
        out_shape=[jax.ShapeDtypeStruct((N_SHARD,) + s.shape, s.dtype) for s in shards],
        scratch_shapes=[pltpu.SemaphoreType.DMA((3 * n,))] * 4 + [pltpu.SemaphoreType.DMA((n,))],
    )(*shards)


def _all_reduce_small(slab, *, name):
    def body(in_ref, out_ref, gath, send, recv):
        x, y, c, _ = _place()
        me = 4 * x + 2 * y + c
        gath[me] = in_ref[...]
        copies, peers = [], []
        for m in range(1, N_DEV):
            px = jnp.where((m >> 2) & 1, 1 - x, x)
            py = jnp.where((m >> 1) & 1, 1 - y, y)
            pc = jnp.where(m & 1, 1 - c, c)
            cp = pltpu.make_async_remote_copy(
                src_ref=in_ref, dst_ref=gath.at[me], send_sem=send.at[m - 1], recv_sem=recv.at[m - 1],
                device_id=(px, py, pc), device_id_type=MESH)
            cp.start()
            copies.append(cp)
            peers.append(4 * px + 2 * py + pc)
        for m in range(1, N_DEV):
            pltpu.make_async_remote_copy(
                src_ref=in_ref, dst_ref=gath.at[peers[m - 1]], send_sem=send.at[m - 1], recv_sem=recv.at[m - 1],
                device_id=(x, y, c), device_id_type=MESH).wait_recv()
        for cp in copies:
            cp.wait_send()
        acc = gath[0]
        for k in range(1, N_DEV):
            acc = acc + gath[k]
        out_ref[...] = acc

    vm = pl.BlockSpec(memory_space=pltpu.VMEM)
    return pl.pallas_call(
        body, name=name, in_specs=[vm], out_specs=vm,
        out_shape=jax.ShapeDtypeStruct(slab.shape, slab.dtype),
        scratch_shapes=[pltpu.VMEM((N_DEV,) + slab.shape, slab.dtype),
                        pltpu.SemaphoreType.DMA((N_DEV - 1,)), pltpu.SemaphoreType.DMA((N_DEV - 1,))],
    )(slab)


def _swap_halves(grads, *, name):
    n = len(grads)

    def body(*refs):
        ins, outs = refs[:n], refs[n:2 * n]
        send, recv = refs[2 * n:]
        x, y, c, _ = _place()
        copies = []
        for t in range(n):
            hr = grads[t].shape[1] // 2
            rows = pl.ds(pl.multiple_of((1 - c) * hr, 8), hr)
            cp = pltpu.make_async_remote_copy(
                src_ref=ins[t].at[:, rows, :], dst_ref=outs[t], send_sem=send.at[t], recv_sem=recv.at[t],
                device_id=(x, y, 1 - c), device_id_type=MESH)
            cp.start()
            copies.append(cp)
        for cp in copies:
            cp.wait()

    return pl.pallas_call(
        body, name=name, in_specs=[ANY] * n, out_specs=[ANY] * n,
        out_shape=[jax.ShapeDtypeStruct((N_SHARD, g.shape[1] // 2, g.shape[2]), g.dtype) for g in grads],
        scratch_shapes=[pltpu.SemaphoreType.DMA((n,))] * 2,
    )(*grads)


def _pair_sum(g, got, c, *, name):
    ns, R, C = g.shape
    hr = R // 2

    def body(c_ref, g_ref, r_ref, o_ref):
        o_ref[...] = (g_ref[...] + r_ref[...]).astype(BF16)

    return pl.pallas_call(
        body, name=name,
        grid_spec=pltpu.PrefetchScalarGridSpec(
            num_scalar_prefetch=1, grid=(ns,),
            in_specs=[pl.BlockSpec((None, hr, C), lambda s, cr: (s, cr[0], 0)),
                      pl.BlockSpec((None, hr, C), lambda s, cr: (s, 0, 0))],
            out_specs=pl.BlockSpec((None, hr, C), lambda s, cr: (s, 0, 0))),
        out_shape=jax.ShapeDtypeStruct((ns, hr, C), BF16),
        compiler_params=_params("arbitrary"),
    )(c, g, got)


def _scatter_chips(sums, *, name):
    n = len(sums)

    def body(*refs):
        ins, outs = refs[:n], refs[n:2 * n]
        send, recv = refs[2 * n:]
        x, y, c, others = _place()
        me = 2 * x + y
        copies = []
        for t in range(n):
            for j, (ox, oy) in enumerate(others):
                cp = pltpu.make_async_remote_copy(
                    src_ref=ins[t].at[2 * ox + oy], dst_ref=outs[t].at[me],
                    send_sem=send.at[3 * t + j], recv_sem=recv.at[3 * t + j],
                    device_id=(ox, oy, c), device_id_type=MESH)
                cp.start()
                copies.append(cp)
        for t in range(n):
            for j, (ox, oy) in enumerate(others):
                pltpu.make_async_remote_copy(
                    src_ref=ins[t].at[me], dst_ref=outs[t].at[2 * ox + oy],
                    send_sem=send.at[3 * t + j], recv_sem=recv.at[3 * t + j],
                    device_id=(x, y, c), device_id_type=MESH).wait_recv()
        for cp in copies:
            cp.wait_send()

    return pl.pallas_call(
        body, name=name, in_specs=[ANY] * n, out_specs=[ANY] * n,
        out_shape=[jax.ShapeDtypeStruct(s.shape, s.dtype) for s in sums],
        scratch_shapes=[pltpu.SemaphoreType.DMA((3 * n,))] * 2,
    )(*sums)


def _chip_sum(g, got, landed, idx, *, name):
    ns, R, C = g.shape
    hr = R // 2

    def body(i_ref, g_ref, r_ref, a_ref, b_ref, c_ref, o_ref):
        acc = g_ref[...] + r_ref[...]
        for ref in (a_ref, b_ref, c_ref):
            acc = acc + ref[...].astype(F32)
        o_ref[...] = acc

    other = lambda k: pl.BlockSpec((None, hr, C), lambda s, ir: (ir[2 + k], 0, 0))
    return pl.pallas_call(
        body, name=name,
        grid_spec=pltpu.PrefetchScalarGridSpec(
            num_scalar_prefetch=1, grid=(1,),
            in_specs=[pl.BlockSpec((None, hr, C), lambda s, ir: (ir[0], ir[1], 0)),
                      pl.BlockSpec((None, hr, C), lambda s, ir: (ir[0], 0, 0)),
                      other(0), other(1), other(2)],
            out_specs=pl.BlockSpec((hr, C), lambda s, ir: (ir[1], 0))),
        out_shape=jax.ShapeDtypeStruct((R, C), F32),
        compiler_params=_params("arbitrary"),
    )(idx, g, got, landed, landed, landed)


def _share_halves(halves, *, name):
    n = len(halves)

    def body(*refs):
        outs = refs[n:2 * n]
        send, recv = refs[2 * n:]
        x, y, c, _ = _place()
        copies = []
        for t in range(n):
            hr = halves[t].shape[0] // 2
            rows = pl.ds(pl.multiple_of(c * hr, 8), hr)
            cp = pltpu.make_async_remote_copy(
                src_ref=outs[t].at[rows, :], dst_ref=outs[t].at[rows, :], send_sem=send.at[t], recv_sem=recv.at[t],
                device_id=(x, y, 1 - c), device_id_type=MESH)
            cp.start()
            copies.append(cp)
        for t in range(n):
            hr = halves[t].shape[0] // 2
            rows = pl.ds(pl.multiple_of((1 - c) * hr, 8), hr)
            pltpu.make_async_remote_copy(
                src_ref=outs[t].at[rows, :], dst_ref=outs[t].at[rows, :], send_sem=send.at[t], recv_sem=recv.at[t],
                device_id=(x, y, c), device_id_type=MESH).wait_recv()
        for cp in copies:
            cp.wait_send()

    return pl.pallas_call(
        body, name=name, in_specs=[ANY] * n, out_specs=[ANY] * n,
        out_shape=[jax.ShapeDtypeStruct(h.shape, h.dtype) for h in halves],
        input_output_aliases={t: t for t in range(n)},
        scratch_shapes=[pltpu.SemaphoreType.DMA((n,))] * 2,
    )(*halves)


def _adamw(w, g, m, v, *, name):
    R, C = w.shape
    tr = R
    for cand in (256, 128, 64, 32, 16, 8):
        if R % cand == 0:
            tr = cand
            break

    def body(w_ref, g_ref, m_ref, v_ref, d_ref, mo_ref, vo_ref):
        gv = g_ref[...]
        mn = ADAM_B1 * m_ref[...] + (1.0 - ADAM_B1) * gv
        vn = ADAM_B2 * v_ref[...] + (1.0 - ADAM_B2) * (gv * gv)
        m_hat = mn / (1.0 - ADAM_B1 ** ADAM_STEP)
        v_hat = vn / (1.0 - ADAM_B2 ** ADAM_STEP)
        d_ref[...] = -ADAM_LR * (m_hat / (jnp.sqrt(v_hat) + ADAM_EPS) + ADAM_WD * w_ref[...])
        mo_ref[...] = mn
        vo_ref[...] = vn

    blk = pl.BlockSpec((tr, C), lambda i: (i, 0))
    return pl.pallas_call(
        body, name=name, grid=(R // tr,), in_specs=[blk] * 4, out_specs=[blk] * 3,
        out_shape=[jax.ShapeDtypeStruct((R, C), F32)] * 3,
        compiler_params=_params("arbitrary"),
    )(w, g, m, v)


def _pack_small(D, meta, n1, nm, n3, nf, gc, ga, bf, cw):
    def row(a):
        a = a.reshape(-1, a.shape[-1])
        return jnp.pad(a, ((0, 0), (0, D - a.shape[-1])))
    rows = [row(meta), row(n1), row(nm), row(n3), row(nf), row(jnp.concatenate([gc, ga], axis=-1)), row(bf), row(cw)]
    slab = jnp.concatenate(rows, axis=0)
    return jnp.pad(slab, ((0, SMALL_ROWS - slab.shape[0]), (0, 0)))


def _unpack_small(slab, like):
    meta, n1, nm, n3, nf, gc, ga, bf, cw = like
    nmeta, mc = meta.shape
    out = [slab[:nmeta, :mc].reshape(meta.shape)]
    r = nmeta
    for a in (n1, nm, n3, nf):
        out.append(slab[r, :a.shape[-1]].reshape(a.shape))
        r += 1
    cd = gc.shape[-1]
    out.append(slab[r, :cd].reshape(gc.shape))
    out.append(slab[r, cd:cd + ga.shape[-1]].reshape(ga.shape))
    r += 1
    out.append(slab[r, :bf.shape[-1]].reshape(bf.shape))
    r += 1
    out.append(slab[r:r + 3, :cw.shape[-1]].reshape(cw.shape))
    return out


def kernel(x, meta_tokens, ffn1_norm, ffn1_w_gu, ffn1_w_down, mix_norm, w_in, conv_w, b_f, out_norm_conv, out_norm_attn, w_out, ffn2_norm, ffn2_w_gu, ffn2_w_down, final_norm, loss_target, m_meta_tokens, m_ffn1_norm, m_ffn1_w_gu, m_ffn1_w_down, m_mix_norm, m_w_in, m_conv_w, m_b_f, m_out_norm_conv, m_out_norm_attn, m_w_out, m_ffn2_norm, m_ffn2_w_gu, m_ffn2_w_down, m_final_norm, v_meta_tokens, v_ffn1_norm, v_ffn1_w_gu, v_ffn1_w_down, v_mix_norm, v_w_in, v_conv_w, v_b_f, v_out_norm_conv, v_out_norm_attn, v_w_out, v_ffn2_norm, v_ffn2_w_gu, v_ffn2_w_down, v_final_norm):
    B, S, D = x.shape
    L = S + N_META
    T = B * L
    tm = L // 3
    assert tm * 3 == L and tm % HALO == 0
    guc = ffn1_w_gu.shape[-1]
    ff = N_SHARD * guc // 2
    H = b_f.shape[-1]
    AD = H * HEAD_DIM
    CD = conv_w.shape[-1] * N_SHARD
    assert CD == AD and CD + AD == D and CD % LANES == 0
    n_main = 3 * CD + 3 * AD
    ins = w_in.shape[-1]

    xi, yi, ci = lax.axis_index("x"), lax.axis_index("y"), lax.axis_index("c")
    chip = 2 * xi + yi

    small_shard = jnp.zeros((2 * HALO, meta_tokens.shape[-1]), F32)
    small_shard = small_shard.at[:N_META].set(meta_tokens)
    small_shard = small_shard.at[N_META:N_META + 3, :conv_w.shape[-1]].set(conv_w[0])
    big = [ffn1_w_gu[0], ffn1_w_down[0], w_in[0], w_out[0], ffn2_w_gu[0], ffn2_w_down[0]]
    gathered = _all_gather_shards([w.astype(BF16) for w in big] + [small_shard], name="gather_weights")
    wgu1, wd1, win_g, wout_g, wgu2, wd2, small_g = gathered
    wd1 = wd1.reshape(ff, D)
    wd2 = wd2.reshape(ff, D)
    wout_f = wout_g.reshape(D, D)
    win_f = jnp.moveaxis(win_g, 0, 1).reshape(D, N_SHARD * ins)
    win_main = win_f[:, :n_main]
    win_fg = jnp.pad(win_f[:, n_main:], ((0, 0), (0, LANES - H)))
    meta_f = jnp.moveaxis(small_g[:, :N_META], 0, 1).reshape(N_META, D)
    cw_f = jnp.moveaxis(small_g[:, N_META:N_META + 3, :conv_w.shape[-1]], 0, 1).reshape(3, CD)
    cw8 = jnp.pad(cw_f, ((0, 5), (0, 0)))
    bf_p = jnp.pad(b_f, ((0, 0), (0, LANES - H)))
    gid = jnp.arange(CD) // HEAD_DIM
    pmat = jnp.where(gid[:, None] == gid[None, :], 1.0 / HEAD_DIM, 0.0).astype(BF16)

    gu_shape = jax.ShapeDtypeStruct((2, T, ff), BF16)
    gu_w_spec = pl.BlockSpec((None, D, guc), lambda s, i: (s, 0, 0))
    gu_o_spec = pl.BlockSpec((None, tm, guc), lambda s, i: (s // 2, i, s % 2))

    h0 = jnp.concatenate([jnp.broadcast_to(meta_f[None], (B, N_META, D)), x], axis=1).reshape(T, D)
    n1 = _rmsnorm(h0, ffn1_norm, tm=tm, name="ffn1_norm")
    gu1 = _matmul_nn(n1, wgu1, tm=tm, nb=N_SHARD, w_spec=gu_w_spec, out_shape=gu_shape, out_spec=gu_o_spec,
                     name="ffn1_up")
    h1 = _ffn_down(gu1, wd1, h0, tm=tm, name="ffn1_down")

    n2 = _rmsnorm(h1, mix_norm, tm=tm, name="mix_norm")
    proj = _matmul_nn(n2, win_main, tm=tm, nb=n_main // CD,
                      w_spec=pl.BlockSpec((D, CD), lambda s, i: (0, s)),
                      out_shape=jax.ShapeDtypeStruct((T, n_main), BF16),
                      out_spec=pl.BlockSpec((tm, CD), lambda s, i: (i, s)), name="mix_in")
    fg = _matmul_nn(n2, win_fg, tm=tm, nb=1, w_spec=pl.BlockSpec((D, LANES), lambda s, i: (0, 0)),
                    out_shape=jax.ShapeDtypeStruct((T, LANES), F32),
                    out_spec=pl.BlockSpec((tm, LANES), lambda s, i: (i, 0)), name="mix_in_fg")
    proj3 = proj.reshape(B, L, n_main)
    fg3 = fg.reshape(B, L, LANES)
    fc = _fcum(fg3, bf_p, ch=tm, name="forget_cumsum")
    fr = fc[:, :, :H].reshape(B, L // tm, tm, H).transpose(0, 1, 3, 2)
    o, lse = _attn_fwd(proj3, fc, fr, tq=tm, n_heads=H, name="attn_fwd")
    h2, ymix = _mix_out(proj3, o, cw8, out_norm_conv, out_norm_attn, wout_f, h1.reshape(B, L, D), pmat,
                        tm=tm, name="mix_out")
    h2 = h2.reshape(T, D)

    n3 = _rmsnorm(h2, ffn2_norm, tm=tm, name="ffn2_norm")
    gu2 = _matmul_nn(n3, wgu2, tm=tm, nb=N_SHARD, w_spec=gu_w_spec, out_shape=gu_shape, out_spec=gu_o_spec,
                     name="ffn2_up")
    h3 = _ffn_down(gu2, wd2, h2, tm=tm, name="ffn2_down")

    tgt = jnp.pad(loss_target, ((0, 0), (N_META, 0), (0, 0)))
    dh3, dh3b, d_gf, loss_part = _final(h3.reshape(B, L, D), final_norm.reshape(1, D), tgt, tm=tm, name="final")

    def ffn_backward(dh, dhb, gu, wgu, wd, n, h_in, gain, scale, tag):
        dgu = _ffn_bwd_act(dhb, gu, wd, tm=tm, guc=guc, name=tag + "_bwd_act")
        dh_in, dhb_in, d_gain = _ffn_bwd_in(dgu, wgu, h_in, gain, dh, tm=tm, scale=scale, name=tag + "_bwd_in")
        d_wd = _matmul_tn(
            [gu, gu], dhb, tm=tm, nb=ff // guc, kb=guc, silu=True,
            x_specs=[pl.BlockSpec((None, tm, guc), lambda j, i: (0, i, j)),
                     pl.BlockSpec((None, tm, guc), lambda j, i: (1, i, j))],
            y_spec=pl.BlockSpec((tm, D), lambda j, i: (i, 0)),
            out_shape=jax.ShapeDtypeStruct((ff, D), F32), out_spec=pl.BlockSpec((guc, D), lambda j, i: (j, 0)),
            name=tag + "_dw_down")
        d_wgu = _matmul_tn(
            [n], dgu, tm=tm, nb=N_SHARD, kb=D, silu=False,
            x_specs=[pl.BlockSpec((tm, D), lambda s, i: (i, 0))],
            y_spec=pl.BlockSpec((None, tm, guc), lambda s, i: (s // 2, i, s % 2)),
            out_shape=jax.ShapeDtypeStruct((N_SHARD, D, guc), F32),
            out_spec=pl.BlockSpec((None, D, guc), lambda s, i: (s, 0, 0)), name=tag + "_dw_up")
        return dh_in, dhb_in, d_gain, d_wgu, d_wd

    dh2, dh2b, d_g3, d_wgu2, d_wd2 = ffn_backward(
        dh3.reshape(T, D), dh3b.reshape(T, D), gu2, wgu2, wd2, n3, h2, ffn2_norm, 1.0, "ffn2")

    dh2b3 = dh2b.reshape(B, L, D)
    d_bg, d_cv, d_o, d_gc, d_ga, d_cw = _mix_out_bwd(
        dh2b3, proj3, o, cw8, out_norm_conv, out_norm_attn, wout_f, pmat, tm=tm, name="mix_out_bwd")
    d_wout = _matmul_tn(
        [ymix.reshape(T, D)], dh2b, tm=tm, nb=1, kb=D, silu=False,
        x_specs=[pl.BlockSpec((tm, D), lambda s, i: (i, 0))], y_spec=pl.BlockSpec((tm, D), lambda s, i: (i, 0)),
        out_shape=jax.ShapeDtypeStruct((D, D), F32), out_spec=pl.BlockSpec((D, D), lambda s, i: (0, 0)),
        name="dw_out")
    d_cc = _conv_bwd(d_cv, proj3, cw8, tm=tm, name="conv_bwd")
    d_q, d_k, d_v, d_fr, d_fq = _attn_bwd(proj3, o, d_o, lse, fc, fr, tq=tm, n_heads=H, name="attn_bwd")
    d_fc = d_fq + jnp.pad(d_fr.transpose(0, 1, 3, 2).reshape(B, L, H), ((0, 0), (0, 0), (0, LANES - H)))
    d_fg, d_bf = _fcum_bwd(d_fc, fg3, bf_p, ch=tm, name="forget_cumsum_bwd")

    parts = [d_bg.reshape(T, CD), d_cc.reshape(T, 2 * CD), d_q.reshape(T, AD), d_k.reshape(T, AD),
             d_v.reshape(T, AD), d_fg.reshape(T, LANES)]
    dh1, dh1b, d_gm = _mix_bwd_in(parts, win_main, win_fg, h1, mix_norm, dh2, tm=tm, scale=0.5, name="mix_bwd_in")
    d_win_parts = []
    for k, p in enumerate(parts):
        wdt = p.shape[1]
        nb = max(wdt // CD, 1)
        bw = wdt // nb
        d_win_parts.append(_matmul_tn(
            [n2], p, tm=tm, nb=nb, kb=D, silu=False,
            x_specs=[pl.BlockSpec((tm, D), lambda s, i: (i, 0))], y_spec=pl.BlockSpec((tm, bw), lambda s, i: (i, s)),
            out_shape=jax.ShapeDtypeStruct((D, wdt), F32), out_spec=pl.BlockSpec((D, bw), lambda s, i: (0, s)),
            name="dw_in_%d" % k))
    d_win_parts[-1] = d_win_parts[-1][:, :H]
    d_win = jnp.moveaxis(jnp.concatenate(d_win_parts, axis=1).reshape(D, N_SHARD, ins), 1, 0)

    dh0, _, d_g1, d_wgu1, d_wd1 = ffn_backward(dh1, dh1b, gu1, wgu1, wd1, n1, h0, ffn1_norm, 1.0, "ffn1")
    dh0 = dh0.reshape(B, L, D)
    grad_x = dh0[:, N_META:]
    d_meta = jnp.sum(dh0[:, :N_META], axis=0)

    loss_row = jnp.zeros((1, D), F32).at[0, 0].set(loss_part[0, 0])
    slab = _pack_small(D, d_meta, d_g1, d_gm, d_g3, d_gf, d_gc, d_ga, d_bf[:, :H], d_cw[:3])
    slab = slab.at[SMALL_ROWS - 1].set(loss_row[0])
    total = _all_reduce_small(slab, name="reduce_small")
    loss = total[SMALL_ROWS - 1, 0]
    mcols = meta_tokens.shape[-1]
    ccols = conv_w.shape[-1]
    full_like = (jnp.zeros((N_META, D)), ffn1_norm, mix_norm, ffn2_norm, final_norm.reshape(1, D), out_norm_conv,
                 out_norm_attn, b_f, jnp.zeros((1, 3, CD)))
    g_small = _unpack_small(total, full_like)
    g_small[0] = lax.dynamic_slice_in_dim(g_small[0], chip * mcols, mcols, axis=1)
    g_small[8] = lax.dynamic_slice_in_dim(g_small[8], chip * ccols, ccols, axis=2)

    def small_slab(meta, a1, am, a3, af, gc, ga, bf, cw):
        return _pack_small(D, meta, a1, am, a3, af.reshape(1, D), gc, ga, bf, cw[0])

    w_small = small_slab(meta_tokens, ffn1_norm, mix_norm, ffn2_norm, final_norm, out_norm_conv, out_norm_attn, b_f, conv_w)
    m_small = small_slab(m_meta_tokens, m_ffn1_norm, m_mix_norm, m_ffn2_norm, m_final_norm, m_out_norm_conv,
                         m_out_norm_attn, m_b_f, m_conv_w)
    v_small = small_slab(v_meta_tokens, v_ffn1_norm, v_mix_norm, v_ffn2_norm, v_final_norm, v_out_norm_conv,
                         v_out_norm_attn, v_b_f, v_conv_w)
    gs = list(g_small)
    gs[4] = gs[4].reshape(final_norm.shape)
    g_slab = small_slab(gs[0], gs[1], gs[2], gs[3], gs[4], gs[5], gs[6], gs[7], gs[8])
    local_like = (meta_tokens, ffn1_norm, mix_norm, ffn2_norm, final_norm.reshape(1, D), out_norm_conv, out_norm_attn,
                  b_f, conv_w)
    small_out = [_unpack_small(s, local_like) for s in _adamw(w_small, g_slab, m_small, v_small, name="adamw_small")]
    for lst in small_out:
        lst[4] = lst[4].reshape(final_norm.shape)

    grads = [d_wgu1, d_wd1.reshape(N_SHARD, ff // N_SHARD, D), d_win, d_wout.reshape(N_SHARD, D // N_SHARD, D),
             d_wgu2, d_wd2.reshape(N_SHARD, ff // N_SHARD, D)]
    names = ["wgu1", "wd1", "win", "wout", "wgu2", "wd2"]
    got = _swap_halves(grads, name="swap_halves")
    c_arr = jnp.reshape(ci, (1,)).astype(jnp.int32)
    sums = [_pair_sum(g, r, c_arr, name="pair_sum_" + nm) for g, r, nm in zip(grads, got, names)]
    landed = _scatter_chips(sums, name="scatter_chips")
    ks = jnp.arange(N_SHARD - 1, dtype=jnp.int32)
    idx = jnp.concatenate([jnp.stack([chip, ci]).astype(jnp.int32), ks + (ks >= chip).astype(jnp.int32)])
    halves = [_chip_sum(g, r, l, idx, name="chip_sum_" + nm) for g, r, l, nm in zip(grads, got, landed, names)]
    g_big = _share_halves(halves, name="share_halves")

    w_big = big
    m_big = [m_ffn1_w_gu[0], m_ffn1_w_down[0], m_w_in[0], m_w_out[0], m_ffn2_w_gu[0], m_ffn2_w_down[0]]
    v_big = [v_ffn1_w_gu[0], v_ffn1_w_down[0], v_w_in[0], v_w_out[0], v_ffn2_w_gu[0], v_ffn2_w_down[0]]
    big_out = [_adamw(w, g, m, v, name="adamw_" + nm) for w, g, m, v, nm in zip(w_big, g_big, m_big, v_big, names)]

    def assemble(small, bigs):
        meta, a1, am, a3, af, gc, ga, bf, cw = small
        gu1_, d1_, win_, wout_, gu2_, d2_ = [b[None] for b in bigs]
        return [meta, a1, gu1_, d1_, am, win_, cw, bf, gc, ga, wout_, a3, gu2_, d2_, af]

    gs_out = list(g_small)
    gs_out[4] = gs_out[4].reshape(final_norm.shape)
    grads_out = assemble(gs_out, g_big)
    delta_out = assemble(small_out[0], [b[0] for b in big_out])
    m_out = assemble(small_out[1], [b[1] for b in big_out])
    v_out = assemble(small_out[2], [b[2] for b in big_out])
    return (loss, grad_x, *grads_out, *delta_out, *m_out, *v_out)
```

```python
import functools

import jax
import jax.numpy as jnp
from jax import lax
from jax.experimental import pallas as pl
from jax.experimental.pallas import tpu as pltpu

F32 = jnp.float32
BF16 = jnp.bfloat16

EPS = 1e-6
N_META = 16
HEAD_DIM = 64
N_SHARD = 4
N_DEV = 8
HALO = 16
LANES = 128
SMALL_ROWS = 32
VMEM_LIMIT_V7X = 56 * 1024 * 1024
NEG = -1e30

ADAM_LR = 0.001
ADAM_B1 = 0.9
ADAM_B2 = 0.999
ADAM_EPS = 1e-08
ADAM_WD = 0.01
ADAM_STEP = 10

MESH = pl.DeviceIdType.MESH
ANY = pl.BlockSpec(memory_space=pl.ANY)
NT_DIMS = (((1,), (1,)), ((), ()))
TN_DIMS = (((0,), (0,)), ((), ()))


def _params(*sem):
    return pltpu.CompilerParams(dimension_semantics=sem, vmem_limit_bytes=VMEM_LIMIT_V7X)


class _Comm:
    def __init__(self, ins, out_shapes, sems, start, finish, aliases=None):
        self.ins, self.out_shapes, self.sems = list(ins), list(out_shapes), list(sems)
        self.start, self.finish, self.aliases = start, finish, dict(aliases or {})


def _launch(body, *, name, grid, in_specs, out_specs, out_shape, args, scratch_shapes=(), comm=None):
    single = not isinstance(out_shape, (list, tuple))
    out_specs = [out_specs] if single else list(out_specs)
    out_shape = [out_shape] if single else list(out_shape)
    in_specs, scratch_shapes = list(in_specs), list(scratch_shapes)
    params = _params(*(("arbitrary",) * len(grid)))
    if comm is None:
        res = pl.pallas_call(body, name=name, grid=grid, in_specs=in_specs, out_specs=out_specs,
                             out_shape=out_shape, scratch_shapes=scratch_shapes, compiler_params=params)(*args)
        return (res[0] if single else list(res)), None
    n_in, n_out, n_scr = len(in_specs), len(out_specs), len(scratch_shapes)
    c_in, c_out = len(comm.ins), len(comm.out_shapes)

    def carrier(*refs):
        p = 0
        a = refs[p:p + n_in]; p += n_in
        ci = refs[p:p + c_in]; p += c_in
        o = refs[p:p + n_out]; p += n_out
        co = refs[p:p + c_out]; p += c_out
        s = refs[p:p + n_scr]; p += n_scr
        cs = refs[p:]
        first = functools.reduce(lambda u, v: u & v, [pl.program_id(k) == 0 for k in range(len(grid))])
        last = functools.reduce(lambda u, v: u & v, [pl.program_id(k) == grid[k] - 1 for k in range(len(grid))])

        @pl.when(first)
        def _():
            comm.start(ci, co, cs)

        body(*a, *o, *s)

        @pl.when(last)
        def _():
            comm.finish(ci, co, cs)

    res = pl.pallas_call(
        carrier, name=name, grid=grid, in_specs=in_specs + [ANY] * c_in, out_specs=out_specs + [ANY] * c_out,
        out_shape=out_shape + comm.out_shapes,
        scratch_shapes=scratch_shapes + [pltpu.SemaphoreType.DMA((k,)) for k in comm.sems],
        input_output_aliases={n_in + i: n_out + j for i, j in comm.aliases.items()},
        compiler_params=params)(*args, *comm.ins)
    main = list(res[:n_out])
    return (main[0] if single else main), list(res[n_out:])


def _run_comm(comm, *, name):
    c_in, c_out = len(comm.ins), len(comm.out_shapes)

    def body(*refs):
        ci, co, cs = refs[:c_in], refs[c_in:c_in + c_out], refs[c_in + c_out:]
        comm.start(ci, co, cs)
        comm.finish(ci, co, cs)

    return list(pl.pallas_call(
        body, name=name, in_specs=[ANY] * c_in, out_specs=[ANY] * c_out, out_shape=comm.out_shapes,
        scratch_shapes=[pltpu.SemaphoreType.DMA((k,)) for k in comm.sems],
        input_output_aliases=comm.aliases)(*comm.ins))


def _chunks(width, step=512):
    out, c0 = [], 0
    while c0 < width:
        cw = min(step, width - c0)
        out.append((c0, cw))
        c0 += cw
    return out


def _split2(v):
    hi = v.astype(BF16)
    lo = (v - hi.astype(F32)).astype(BF16)
    return hi, lo


def _split3(v):
    hi = v.astype(BF16)
    r = v - hi.astype(F32)
    mid = r.astype(BF16)
    lo = (r - mid.astype(F32)).astype(BF16)
    return hi, mid, lo


def _dot(a, b):
    return jnp.dot(a, b, preferred_element_type=F32)


def _dot_nt(a, b):
    return lax.dot_general(a, b, NT_DIMS, preferred_element_type=F32)


def _dot_tn(a, b):
    return lax.dot_general(a, b, TN_DIMS, preferred_element_type=F32)


def _silu_mul(g, u):
    return g * jax.nn.sigmoid(g) * u


def _rms_bwd(dn, h, gain, dres):
    r = lax.rsqrt(jnp.mean(h * h, axis=-1, keepdims=True) + EPS)
    y = h * r
    dgain = jnp.sum(dn * y, axis=0, keepdims=True)
    dy = dn * gain
    dh = dres + r * (dy - y * jnp.mean(dy * y, axis=-1, keepdims=True))
    return dh, dgain


def _group_mean(v, p):
    hi, lo = _split2(v)
    return _dot(hi, p) + _dot(lo, p)


def _row_of(a, k):
    rows = lax.broadcasted_iota(jnp.int32, a.shape, 0)
    return jnp.sum(jnp.where(rows == k, a, 0.0), axis=0, keepdims=True)


def _causal_conv(u, prev, w):
    rows = lax.broadcasted_iota(jnp.int32, u.shape, 0)
    p1 = _row_of(prev, HALO - 1)
    p2 = _row_of(prev, HALO - 2)
    u1 = jnp.where(rows == 0, p1, pltpu.roll(u, 1, 0))
    u2 = jnp.where(rows == 0, p2, jnp.where(rows == 1, p1, pltpu.roll(u, 2, 0)))
    return w[2:3, :] * u + w[1:2, :] * u1 + w[0:1, :] * u2, u1, u2


def _rmsnorm(h, g, *, tm, name):
    T, D = h.shape

    def body(h_ref, g_ref, n_ref):
        x = h_ref[...]
        r = lax.rsqrt(jnp.mean(x * x, axis=-1, keepdims=True) + EPS)
        n_ref[...] = (x * r * g_ref[...]).astype(BF16)

    return pl.pallas_call(
        body, name=name, grid=(T // tm,),
        in_specs=[pl.BlockSpec((tm, D), lambda i: (i, 0)), pl.BlockSpec((1, D), lambda i: (0, 0))],
        out_specs=pl.BlockSpec((tm, D), lambda i: (i, 0)),
        out_shape=jax.ShapeDtypeStruct((T, D), BF16),
        compiler_params=_params("arbitrary"),
    )(h, g)


def _matmul_nn(x, w, *, tm, nb, w_spec, out_shape, out_spec, name, comm=None):
    T, K = x.shape

    def body(x_ref, w_ref, o_ref):
        o_ref[...] = _dot(x_ref[...], w_ref[...]).astype(o_ref.dtype)

    return _launch(
        body, name=name, grid=(nb, T // tm),
        in_specs=[pl.BlockSpec((tm, K), lambda s, i: (i, 0)), w_spec],
        out_specs=out_spec, out_shape=out_shape, args=(x, w), comm=comm)


def _ffn_down(gu, wd, h, *, tm, name, comm=None):
    _, T, ff = gu.shape
    D = h.shape[1]
    chunks = _chunks(ff)

    def body(g_ref, u_ref, wd_hbm, h_ref, o_ref, wd_v, sem):
        @pl.when(pl.program_id(0) == 0)
        def _():
            cp = pltpu.make_async_copy(wd_hbm, wd_v, sem)
            cp.start()
            cp.wait()

        acc = jnp.zeros((tm, D), F32)
        for c0, cw in chunks:
            a = _silu_mul(g_ref[:, c0:c0 + cw].astype(F32), u_ref[:, c0:c0 + cw].astype(F32))
            acc = acc + _dot(a.astype(BF16), wd_v[c0:c0 + cw, :])
        o_ref[...] = h_ref[...] + 0.5 * acc

    return _launch(
        body, name=name, grid=(T // tm,),
        in_specs=[pl.BlockSpec((None, tm, ff), lambda i: (0, i, 0)),
                  pl.BlockSpec((None, tm, ff), lambda i: (1, i, 0)),
                  ANY,
                  pl.BlockSpec((tm, D), lambda i: (i, 0))],
        out_specs=pl.BlockSpec((tm, D), lambda i: (i, 0)),
        out_shape=jax.ShapeDtypeStruct((T, D), F32),
        scratch_shapes=[pltpu.VMEM((ff, D), BF16), pltpu.SemaphoreType.DMA],
        args=(gu, gu, wd, h), comm=comm)


def _ffn_bwd_act(df, gu, wd, *, tm, guc, name, comm=None):
    _, T, ff = gu.shape
    D = df.shape[1]
    nj = ff // guc
    chunks = _chunks(guc)

    def body(df_ref, g_ref, u_ref, wd_ref, o_ref):
        dfv = df_ref[...]
        for c0, cw in chunks:
            da = _dot_nt(dfv, wd_ref[c0:c0 + cw, :])
            g = g_ref[:, c0:c0 + cw].astype(F32)
            u = u_ref[:, c0:c0 + cw].astype(F32)
            sg = jax.nn.sigmoid(g)
            silu = g * sg
            o_ref[0, :, c0:c0 + cw] = (da * u * (sg * (1.0 + g * (1.0 - sg)))).astype(BF16)
            o_ref[1, :, c0:c0 + cw] = (da * silu).astype(BF16)

    return _launch(
        body, name=name, grid=(nj, T // tm),
        in_specs=[pl.BlockSpec((tm, D), lambda j, i: (i, 0)),
                  pl.BlockSpec((None, tm, guc), lambda j, i: (0, i, j)),
                  pl.BlockSpec((None, tm, guc), lambda j, i: (1, i, j)),
                  pl.BlockSpec((guc, D), lambda j, i: (j, 0))],
        out_specs=pl.BlockSpec((2, tm, guc), lambda j, i: (0, i, j)),
        out_shape=jax.ShapeDtypeStruct((2, T, ff), BF16),
        args=(df, gu, gu, wd), comm=comm)


def _ffn_bwd_in(dgu, wgu, h, g, dres, *, tm, scale, name, comm=None):
    _, T, ff = dgu.shape
    ns, D, guc = wgu.shape
    nj = ff // guc
    chunks = _chunks(guc)

    def body(dgu_ref, w_hbm, h_ref, g_ref, dres_ref, dh_ref, dhb_ref, dg_ref, w_v, acc, sem):
        i, j = pl.program_id(0), pl.program_id(1)

        @pl.when((i == 0) & (j == 0))
        def _():
            cp = pltpu.make_async_copy(w_hbm, w_v, sem)
            cp.start()
            cp.wait()
            dg_ref[...] = jnp.zeros_like(dg_ref)

        part = jnp.zeros((tm, D), F32)
        for c0, cw in chunks:
            part = part + _dot_nt(dgu_ref[0, :, c0:c0 + cw], w_v[j, :, c0:c0 + cw])
            part = part + _dot_nt(dgu_ref[1, :, c0:c0 + cw], w_v[nj + j, :, c0:c0 + cw])

        @pl.when(j == 0)
        def _():
            acc[...] = part

        @pl.when(j > 0)
        def _():
            acc[...] += part

        @pl.when(j == nj - 1)
        def _():
            dh, dgain = _rms_bwd(acc[...], h_ref[...], g_ref[...], dres_ref[...])
            dh_ref[...] = dh
            dhb_ref[...] = (scale * dh).astype(BF16)
            dg_ref[...] += dgain

    return _launch(
        body, name=name, grid=(T // tm, nj),
        in_specs=[pl.BlockSpec((2, tm, guc), lambda i, j: (0, i, j)),
                  ANY,
                  pl.BlockSpec((tm, D), lambda i, j: (i, 0)),
                  pl.BlockSpec((1, D), lambda i, j: (0, 0)),
                  pl.BlockSpec((tm, D), lambda i, j: (i, 0))],
        out_specs=[pl.BlockSpec((tm, D), lambda i, j: (i, 0)),
                   pl.BlockSpec((tm, D), lambda i, j: (i, 0)),
                   pl.BlockSpec((1, D), lambda i, j: (0, 0))],
        out_shape=[jax.ShapeDtypeStruct((T, D), F32), jax.ShapeDtypeStruct((T, D), BF16),
                   jax.ShapeDtypeStruct((1, D), F32)],
        scratch_shapes=[pltpu.VMEM((ns, D, guc), BF16), pltpu.VMEM((tm, D), F32), pltpu.SemaphoreType.DMA],
        args=(dgu, wgu, h, g, dres), comm=comm)


def _mix_bwd_in(parts, w_main, w_fg, h, g, dres, *, tm, scale, name, comm=None):
    T, D = h.shape
    widths = [p.shape[1] for p in parts[:-1]]
    offs = [sum(widths[:k]) for k in range(len(widths))]
    npart = len(parts)

    def body(*refs):
        p_refs = refs[:npart]
        wm_ref, wf_ref, h_ref, g_ref, dres_ref, dh_ref, dhb_ref, dg_ref = refs[npart:]

        @pl.when(pl.program_id(0) == 0)
        def _():
            dg_ref[...] = jnp.zeros_like(dg_ref)

        dn = _dot_nt(p_refs[-1][...].astype(BF16), wf_ref[...])
        for p_ref, off, wd_ in zip(p_refs[:-1], offs, widths):
            for c0, cw in _chunks(wd_):
                dn = dn + _dot_nt(p_ref[:, c0:c0 + cw].astype(BF16), wm_ref[:, off + c0:off + c0 + cw])
        dh, dgain = _rms_bwd(dn, h_ref[...], g_ref[...], dres_ref[...])
        dh_ref[...] = dh
        dhb_ref[...] = (scale * dh).astype(BF16)
        dg_ref[...] += dgain

    row = lambda i: (i, 0)
    const = lambda i: (0, 0)
    return _launch(
        body, name=name, grid=(T // tm,),
        in_specs=[pl.BlockSpec((tm, p.shape[1]), row) for p in parts]
                 + [pl.BlockSpec(w_main.shape, const), pl.BlockSpec(w_fg.shape, const),
                    pl.BlockSpec((tm, D), row), pl.BlockSpec((1, D), const), pl.BlockSpec((tm, D), row)],
        out_specs=[pl.BlockSpec((tm, D), row), pl.BlockSpec((tm, D), row), pl.BlockSpec((1, D), const)],
        out_shape=[jax.ShapeDtypeStruct((T, D), F32), jax.ShapeDtypeStruct((T, D), BF16),
                   jax.ShapeDtypeStruct((1, D), F32)],
        args=(*parts, w_main, w_fg, h, g, dres), comm=comm)


def _matmul_tn(xs, y, *, tm, nb, x_specs, y_spec, out_shape, out_spec, kb, silu, name, comm=None):
    T = y.shape[-2]
    nx = len(xs)
    chunks = _chunks(kb)

    def body(*refs):
        x_refs, y_ref, o_ref = refs[:nx], refs[nx], refs[nx + 1]
        i = pl.program_id(1)

        @pl.when(i == 0)
        def _():
            o_ref[...] = jnp.zeros_like(o_ref)

        yv = y_ref[...].astype(BF16)
        for c0, cw in chunks:
            if silu:
                xv = _silu_mul(x_refs[0][:, c0:c0 + cw].astype(F32), x_refs[1][:, c0:c0 + cw].astype(F32)).astype(BF16)
            else:
                xv = x_refs[0][:, c0:c0 + cw]
            o_ref[c0:c0 + cw, :] += _dot_tn(xv, yv)

    return _launch(
        body, name=name, grid=(nb, T // tm),
        in_specs=list(x_specs) + [y_spec], out_specs=out_spec, out_shape=out_shape,
        args=(*xs, y), comm=comm)


def _tri(n, lower):
    r = lax.broadcasted_iota(jnp.int32, (n, n), 0)
    c = lax.broadcasted_iota(jnp.int32, (n, n), 1)
    return jnp.where((r >= c) if lower else (r <= c), 1.0, 0.0).astype(BF16)


def _tri_dot(tri, v):
    hi, mid, lo = _split3(v)
    return _dot(tri, hi) + _dot(tri, mid) + _dot(tri, lo)


def _fcum(fg, bf, *, ch, name):
    B, L, W = fg.shape
    nch = L // ch

    def body(fg_ref, bf_ref, f_ref):
        tri = _tri(ch, True)
        carry = jnp.zeros((1, W), F32)
        for c in range(nch):
            x = fg_ref[c * ch:(c + 1) * ch, :] + bf_ref[...]
            lf = jnp.minimum(x, 0.0) - jnp.log(1.0 + jnp.exp(-jnp.abs(x)))
            f_ref[c * ch:(c + 1) * ch, :] = _tri_dot(tri, lf) + carry
            carry = carry + jnp.sum(lf, axis=0, keepdims=True)

    return pl.pallas_call(
        body, name=name, grid=(B,),
        in_specs=[pl.BlockSpec((None, L, W), lambda b: (b, 0, 0)), pl.BlockSpec((1, W), lambda b: (0, 0))],
        out_specs=pl.BlockSpec((None, L, W), lambda b: (b, 0, 0)),
        out_shape=jax.ShapeDtypeStruct((B, L, W), F32),
        compiler_params=_params("arbitrary"),
    )(fg, bf)


def _fcum_bwd(dF, fg, bf, *, ch, name):
    B, L, W = fg.shape
    nch = L // ch

    def body(df_ref, fg_ref, bf_ref, dfg_ref, db_ref):
        @pl.when(pl.program_id(0) == 0)
        def _():
            db_ref[...] = jnp.zeros_like(db_ref)

        tri = _tri(ch, False)
        carry = jnp.zeros((1, W), F32)
        dbs = jnp.zeros((1, W), F32)
        for c in reversed(range(nch)):
            d = df_ref[c * ch:(c + 1) * ch, :]
            dlf = _tri_dot(tri, d) + carry
            carry = carry + jnp.sum(d, axis=0, keepdims=True)
            x = fg_ref[c * ch:(c + 1) * ch, :] + bf_ref[...]
            dfg = dlf * jax.nn.sigmoid(-x)
            dfg_ref[c * ch:(c + 1) * ch, :] = dfg.astype(BF16)
            dbs = dbs + jnp.sum(dfg, axis=0, keepdims=True)
        db_ref[...] += dbs

    blk = pl.BlockSpec((None, L, W), lambda b: (b, 0, 0))
    return pl.pallas_call(
        body, name=name, grid=(B,),
        in_specs=[blk, blk, pl.BlockSpec((1, W), lambda b: (0, 0))],
        out_specs=[blk, pl.BlockSpec((1, W), lambda b: (0, 0))],
        out_shape=[jax.ShapeDtypeStruct((B, L, W), BF16), jax.ShapeDtypeStruct((1, W), F32)],
        compiler_params=_params("arbitrary"),
    )(dF, fg, bf)


def _attn_fwd(proj, fc, fr, *, tq, n_heads, name, comm=None):
    B, L, _ = proj.shape
    AD = n_heads * HEAD_DIM
    nq = L // tq
    W = fc.shape[-1]
    scale = HEAD_DIM ** -0.5

    def body(q_ref, k_ref, v_ref, fc_ref, fr_ref, o_ref, lse_ref, m_s, l_s, acc_s):
        qi, ki = pl.program_id(1), pl.program_id(2)

        @pl.when(ki == 0)
        def _():
            m_s[...] = jnp.full_like(m_s, NEG)
            l_s[...] = jnp.zeros_like(l_s)
            acc_s[...] = jnp.zeros_like(acc_s)

        @pl.when(ki <= qi)
        def _():
            row = qi * tq + lax.broadcasted_iota(jnp.int32, (tq, tq), 0)
            col = ki * tq + lax.broadcasted_iota(jnp.int32, (tq, tq), 1)
            mask = col <= row
            for h in range(n_heads):
                sl = slice(h * HEAD_DIM, (h + 1) * HEAD_DIM)
                s = _dot_nt(q_ref[:, sl], k_ref[:, sl]) * scale
                s = s + fc_ref[:, h:h + 1] - fr_ref[h:h + 1, :]
                s = jnp.where(mask, s, NEG)
                m_old = m_s[:, h:h + 1]
                m_new = jnp.maximum(m_old, jnp.max(s, axis=1, keepdims=True))
                alpha = jnp.exp(m_old - m_new)
                p = jnp.exp(s - m_new)
                l_s[:, h:h + 1] = alpha * l_s[:, h:h + 1] + jnp.sum(p, axis=1, keepdims=True)
                acc_s[:, sl] = alpha * acc_s[:, sl] + _dot(p.astype(BF16), v_ref[:, sl])
                m_s[:, h:h + 1] = m_new

        @pl.when(ki == qi)
        def _():
            for h in range(n_heads):
                sl = slice(h * HEAD_DIM, (h + 1) * HEAD_DIM)
                o_ref[:, sl] = acc_s[:, sl] / l_s[:, h:h + 1]
            l = l_s[...]
            lse_ref[...] = jnp.where(l > 0.0, m_s[...] + jnp.log(jnp.where(l > 0.0, l, 1.0)), 0.0)

    kv = lambda b, qi, ki: jnp.minimum(ki, qi)
    return _launch(
        body, name=name, grid=(B, nq, nq), args=(proj, proj, proj, fc, fr), comm=comm,
        in_specs=[pl.BlockSpec((None, tq, AD), lambda b, qi, ki: (b, qi, 3)),
                  pl.BlockSpec((None, tq, AD), lambda b, qi, ki: (b, kv(b, qi, ki), 4)),
                  pl.BlockSpec((None, tq, AD), lambda b, qi, ki: (b, kv(b, qi, ki), 5)),
                  pl.BlockSpec((None, tq, W), lambda b, qi, ki: (b, qi, 0)),
                  pl.BlockSpec((None, None, n_heads, tq), lambda b, qi, ki: (b, kv(b, qi, ki), 0, 0))],
        out_specs=[pl.BlockSpec((None, tq, AD), lambda b, qi, ki: (b, qi, 0)),
                   pl.BlockSpec((None, tq, W), lambda b, qi, ki: (b, qi, 0))],
        out_shape=[jax.ShapeDtypeStruct((B, L, AD), F32), jax.ShapeDtypeStruct((B, L, W), F32)],
        scratch_shapes=[pltpu.VMEM((tq, W), F32), pltpu.VMEM((tq, W), F32), pltpu.VMEM((tq, AD), F32)])


def _attn_bwd(proj, o, do, lse, fc, fr, *, tq, n_heads, name, comm=None):
    B, L, _ = proj.shape
    AD = n_heads * HEAD_DIM
    nq = L // tq
    W = fc.shape[-1]
    scale = HEAD_DIM ** -0.5

    def body(q_ref, k_ref, v_ref, o_ref, do_ref, lse_ref, fc_ref, fr_ref,
             dq_ref, dk_ref, dv_ref, dfr_ref, dfq_ref, dk_s, dv_s):
        kj, qi = pl.program_id(1), pl.program_id(2)

        @pl.when((kj == 0) & (qi == 0))
        def _():
            dq_ref[...] = jnp.zeros_like(dq_ref)
            dfq_ref[...] = jnp.zeros_like(dfq_ref)

        @pl.when(qi == kj)
        def _():
            dk_s[...] = jnp.zeros_like(dk_s)
            dv_s[...] = jnp.zeros_like(dv_s)
            dfr_ref[...] = jnp.zeros_like(dfr_ref)

        @pl.when(qi >= kj)
        def _():
            row = qi * tq + lax.broadcasted_iota(jnp.int32, (tq, tq), 0)
            col = kj * tq + lax.broadcasted_iota(jnp.int32, (tq, tq), 1)
            mask = col <= row
            rows = pl.ds(pl.multiple_of(qi * tq, 8), tq)
            for h in range(n_heads):
                sl = slice(h * HEAD_DIM, (h + 1) * HEAD_DIM)
                q, k, v, dov = q_ref[:, sl], k_ref[:, sl], v_ref[:, sl], do_ref[:, sl]
                s = _dot_nt(q, k) * scale
                s = s + fc_ref[:, h:h + 1] - fr_ref[h:h + 1, :]
                s = jnp.where(mask, s, NEG)
                p = jnp.exp(s - lse_ref[:, h:h + 1])
                dp = _dot_nt(dov, v)
                dsum = jnp.sum(dov.astype(F32) * o_ref[:, sl].astype(F32), axis=1, keepdims=True)
                ds = p * (dp - dsum)
                dsb = ds.astype(BF16)
                dv_s[:, sl] += _dot_tn(p.astype(BF16), dov)
                dk_s[:, sl] += _dot_tn(dsb, q) * scale
                dq_ref[rows, sl] += _dot(dsb, k) * scale
                dfr_ref[h:h + 1, :] -= jnp.sum(ds, axis=0, keepdims=True)
                dfq_ref[rows, h:h + 1] += jnp.sum(ds, axis=1, keepdims=True)

        @pl.when(qi == nq - 1)
        def _():
            dk_ref[...] = dk_s[...].astype(BF16)
            dv_ref[...] = dv_s[...].astype(BF16)

    qq = lambda b, kj, qi: jnp.maximum(qi, kj)
    qblk = lambda w, cb: pl.BlockSpec((None, tq, w), lambda b, kj, qi: (b, qq(b, kj, qi), cb))
    kblk = lambda cb: pl.BlockSpec((None, tq, AD), lambda b, kj, qi: (b, kj, cb))
    return _launch(
        body, name=name, grid=(B, nq, nq), args=(proj, proj, proj, o, do, lse, fc, fr), comm=comm,
        in_specs=[qblk(AD, 3), kblk(4), kblk(5), qblk(AD, 0), qblk(AD, 0), qblk(W, 0), qblk(W, 0),
                  pl.BlockSpec((None, None, n_heads, tq), lambda b, kj, qi: (b, kj, 0, 0))],
        out_specs=[pl.BlockSpec((None, L, AD), lambda b, kj, qi: (b, 0, 0)),
                   kblk(0), kblk(0),
                   pl.BlockSpec((None, None, n_heads, tq), lambda b, kj, qi: (b, kj, 0, 0)),
                   pl.BlockSpec((None, L, W), lambda b, kj, qi: (b, 0, 0))],
        out_shape=[jax.ShapeDtypeStruct((B, L, AD), F32), jax.ShapeDtypeStruct((B, L, AD), BF16),
                   jax.ShapeDtypeStruct((B, L, AD), BF16), jax.ShapeDtypeStruct((B, nq, n_heads, tq), F32),
                   jax.ShapeDtypeStruct((B, L, W), F32)],
        scratch_shapes=[pltpu.VMEM((tq, AD), F32), pltpu.VMEM((tq, AD), F32)])


def _mix_gather(refs, first):
    b_ref, c_ref, hc_ref, cp_ref, hcp_ref, o_ref, cw_ref, p_ref = refs
    bg = b_ref[...].astype(F32)
    u = c_ref[...].astype(F32) * hc_ref[...].astype(F32)
    prev = cp_ref[...].astype(F32) * hcp_ref[...].astype(F32)
    prev = jnp.where(first, 0.0, prev)
    cv, u1, u2 = _causal_conv(u, prev, cw_ref[...])
    yc = bg * cv
    p = p_ref[...]
    rc = lax.rsqrt(_group_mean(yc * yc, p) + EPS)
    ya = o_ref[...].astype(F32)
    ra = lax.rsqrt(_group_mean(ya * ya, p) + EPS)
    return bg, (u, u1, u2), cv, yc * rc, rc, ya * ra, ra


def _mix_specs(tm, CD, D, grid_rank_fn):
    per = tm // HALO
    cur = lambda cb: pl.BlockSpec((None, tm, CD), lambda b, i: (b, i, cb))
    prev = lambda cb: pl.BlockSpec((None, HALO, CD), lambda b, i: (b, jnp.maximum(i * per - 1, 0), cb))
    return [cur(0), cur(1), cur(2), prev(1), prev(2), cur(0)]


def _mix_out(proj, o, cw, gc, ga, wout, h, pmat, *, tm, name, comm=None):
    B, L, D = h.shape
    CD = o.shape[-1]
    const = lambda b, i: (0, 0)

    def body(b_ref, c_ref, hc_ref, cp_ref, hcp_ref, o_ref, cw_ref, p_ref, gc_ref, ga_ref, w_ref, h_ref,
             out_ref, y_ref):
        first = pl.program_id(1) == 0
        _, _, _, zc, _, za, _ = _mix_gather((b_ref, c_ref, hc_ref, cp_ref, hcp_ref, o_ref, cw_ref, p_ref), first)
        yc = (zc * gc_ref[...]).astype(BF16)
        ya = (za * ga_ref[...]).astype(BF16)
        y_ref[:, :CD] = yc
        y_ref[:, CD:] = ya
        out_ref[...] = h_ref[...] + _dot(yc, w_ref[:CD, :]) + _dot(ya, w_ref[CD:, :])

    return _launch(
        body, name=name, grid=(B, L // tm),
        in_specs=_mix_specs(tm, CD, D, None)
                 + [pl.BlockSpec(cw.shape, const), pl.BlockSpec(pmat.shape, const),
                    pl.BlockSpec((1, CD), const), pl.BlockSpec((1, CD), const), pl.BlockSpec((D, D), const),
                    pl.BlockSpec((None, tm, D), lambda b, i: (b, i, 0))],
        out_specs=[pl.BlockSpec((None, tm, D), lambda b, i: (b, i, 0)),
                   pl.BlockSpec((None, tm, D), lambda b, i: (b, i, 0))],
        out_shape=[jax.ShapeDtypeStruct((B, L, D), F32), jax.ShapeDtypeStruct((B, L, D), BF16)],
        args=(proj, proj, proj, proj, proj, o, cw, pmat, gc, ga, wout, h), comm=comm)


def _mix_out_bwd(dhb, proj, o, cw, gc, ga, wout, pmat, *, tm, name, comm=None):
    B, L, D = dhb.shape
    CD = o.shape[-1]
    const = lambda b, i: (0, 0)

    def body(dh_ref, b_ref, c_ref, hc_ref, cp_ref, hcp_ref, o_ref, cw_ref, p_ref, gc_ref, ga_ref, w_ref,
             db_ref, dcv_ref, do_ref, dgc_ref, dga_ref, dcw_ref):
        first = pl.program_id(1) == 0

        @pl.when((pl.program_id(0) == 0) & first)
        def _():
            dgc_ref[...] = jnp.zeros_like(dgc_ref)
            dga_ref[...] = jnp.zeros_like(dga_ref)
            dcw_ref[...] = jnp.zeros_like(dcw_ref)

        bg, us, cv, zc, rc, za, ra = _mix_gather(
            (b_ref, c_ref, hc_ref, cp_ref, hcp_ref, o_ref, cw_ref, p_ref), first)
        p = p_ref[...]
        dh = dh_ref[...]
        dyc = _dot_nt(dh, w_ref[:CD, :])
        dya = _dot_nt(dh, w_ref[CD:, :])

        dgc_ref[...] += jnp.sum(dyc * zc, axis=0, keepdims=True)
        dz = dyc * gc_ref[...]
        dx = rc * (dz - zc * _group_mean(dz * zc, p))
        db_ref[...] = (dx * cv).astype(BF16)
        dcv = dx * bg
        dcv_ref[...] = dcv.astype(BF16)
        for k in range(3):
            dcw_ref[k:k + 1, :] += jnp.sum(dcv * us[2 - k], axis=0, keepdims=True)

        dga_ref[...] += jnp.sum(dya * za, axis=0, keepdims=True)
        dz = dya * ga_ref[...]
        do_ref[...] = (ra * (dz - za * _group_mean(dz * za, p))).astype(BF16)

    tile = lambda w: pl.BlockSpec((None, tm, w), lambda b, i: (b, i, 0))
    return _launch(
        body, name=name, grid=(B, L // tm), comm=comm,
        args=(dhb, proj, proj, proj, proj, proj, o, cw, pmat, gc, ga, wout),
        in_specs=[tile(D)] + _mix_specs(tm, CD, D, None)
                 + [pl.BlockSpec(cw.shape, const), pl.BlockSpec(pmat.shape, const),
                    pl.BlockSpec((1, CD), const), pl.BlockSpec((1, CD), const), pl.BlockSpec((D, D), const)],
        out_specs=[tile(CD), tile(CD), tile(CD),
                   pl.BlockSpec((1, CD), const), pl.BlockSpec((1, CD), const), pl.BlockSpec((8, CD), const)],
        out_shape=[jax.ShapeDtypeStruct((B, L, CD), BF16)] * 3
                  + [jax.ShapeDtypeStruct((1, CD), F32)] * 2 + [jax.ShapeDtypeStruct((8, CD), F32)])


def _conv_bwd(dcv, proj, cw, *, tm, name):
    B, L, CD = dcv.shape
    per = tm // HALO
    nhalo = L // HALO
    nt = L // tm

    def body(d_ref, dn_ref, c_ref, hc_ref, cw_ref, out_ref):
        last = pl.program_id(1) == nt - 1
        d = d_ref[...].astype(F32)
        nxt = jnp.where(last, 0.0, dn_ref[...].astype(F32))
        n0, n1 = _row_of(nxt, 0), _row_of(nxt, 1)
        rows = lax.broadcasted_iota(jnp.int32, d.shape, 0)
        d1 = jnp.where(rows == tm - 1, n0, pltpu.roll(d, tm - 1, 0))
        d2 = jnp.where(rows == tm - 2, n0, jnp.where(rows == tm - 1, n1, pltpu.roll(d, tm - 2, 0)))
        w = cw_ref[...]
        du = w[2:3, :] * d + w[1:2, :] * d1 + w[0:1, :] * d2
        out_ref[:, :CD] = (du * hc_ref[...].astype(F32)).astype(BF16)
        out_ref[:, CD:] = (du * c_ref[...].astype(F32)).astype(BF16)

    return pl.pallas_call(
        body, name=name, grid=(B, nt),
        in_specs=[pl.BlockSpec((None, tm, CD), lambda b, i: (b, i, 0)),
                  pl.BlockSpec((None, HALO, CD), lambda b, i: (b, jnp.minimum((i + 1) * per, nhalo - 1), 0)),
                  pl.BlockSpec((None, tm, CD), lambda b, i: (b, i, 1)),
                  pl.BlockSpec((None, tm, CD), lambda b, i: (b, i, 2)),
                  pl.BlockSpec(cw.shape, lambda b, i: (0, 0))],
        out_specs=pl.BlockSpec((None, tm, 2 * CD), lambda b, i: (b, i, 0)),
        out_shape=jax.ShapeDtypeStruct((B, L, 2 * CD), BF16),
        compiler_params=_params("arbitrary", "arbitrary"),
    )(dcv, dcv, proj, proj, cw)


def _final(h, gf, tgt, *, tm, name):
    B, L, D = h.shape

    def body(h_ref, g_ref, t_ref, dh_ref, dhb_ref, dg_ref, loss_ref):
        b, i = pl.program_id(0), pl.program_id(1)

        @pl.when((b == 0) & (i == 0))
        def _():
            dg_ref[...] = jnp.zeros_like(dg_ref)
            loss_ref[...] = jnp.zeros_like(loss_ref)

        x = h_ref[...]
        g = g_ref[...]
        r = lax.rsqrt(jnp.mean(x * x, axis=-1, keepdims=True) + EPS)
        y = x * r
        pos = i * tm + lax.broadcasted_iota(jnp.int32, (tm, 1), 0)
        err = jnp.where(pos >= N_META, y * g - t_ref[...], 0.0)
        loss_ref[...] += 0.5 * jnp.sum(jnp.mean(err * err, axis=-1, keepdims=True))
        dout = err / D
        dg_ref[...] += jnp.sum(dout * y, axis=0, keepdims=True)
        dy = dout * g
        dh = r * (dy - y * jnp.mean(dy * y, axis=-1, keepdims=True))
        dh_ref[...] = dh
        dhb_ref[...] = (0.5 * dh).astype(BF16)

    tile = pl.BlockSpec((None, tm, D), lambda b, i: (b, i, 0))
    const = lambda b, i: (0, 0)
    return pl.pallas_call(
        body, name=name, grid=(B, L // tm),
        in_specs=[tile, pl.BlockSpec((1, D), const), tile],
        out_specs=[tile, tile, pl.BlockSpec((1, D), const), pl.BlockSpec((1, LANES), const)],
        out_shape=[jax.ShapeDtypeStruct((B, L, D), F32), jax.ShapeDtypeStruct((B, L, D), BF16),
                   jax.ShapeDtypeStruct((1, D), F32), jax.ShapeDtypeStruct((1, LANES), F32)],
        compiler_params=_params("arbitrary", "arbitrary"),
    )(h, gf, tgt)


def _place():
    x, y, c = lax.axis_index("x"), lax.axis_index("y"), lax.axis_index("c")
    others = [(1 - x, y), (x, 1 - y), (1 - x, 1 - y)]
    return x, y, c, others


def _all_gather_shards(shards, *, name):
    n = len(shards)

    def body(*refs):
        ins, outs = refs[:n], refs[n:2 * n]
        send, recv, fsend, frecv, lsem = refs[2 * n:]
        x, y, c, others = _place()
        me = 2 * x + y
        local = [pltpu.make_async_copy(ins[t], outs[t].at[me], lsem.at[t]) for t in range(n)]
        for cp in local:
            cp.start()

        def half(t, k):
            hr = shards[t].shape[0] // 2
            return pl.ds(pl.multiple_of(k * hr, HALO), hr)

        def ici(t, j, src_chip, to):
            src = ins[t].at[half(t, c)] if to is not None else outs[t].at[src_chip, half(t, c)]
            return pltpu.make_async_remote_copy(
                src_ref=src, dst_ref=outs[t].at[src_chip, half(t, c)],
                send_sem=send.at[3 * t + j], recv_sem=recv.at[3 * t + j],
                device_id=(x, y, c) if to is None else to, device_id_type=MESH)

        def d2d(t, j, src_chip, k):
            return pltpu.make_async_remote_copy(
                src_ref=outs[t].at[src_chip, half(t, k)], dst_ref=outs[t].at[src_chip, half(t, k)],
                send_sem=fsend.at[3 * t + j], recv_sem=frecv.at[3 * t + j],
                device_id=(x, y, 1 - c), device_id_type=MESH)

        firsts = [ici(t, j, me, (ox, oy, c)) for t in range(n) for j, (ox, oy) in enumerate(others)]
        for cp in firsts:
            cp.start()
        passed = []
        for t in range(n):
            for j, (ox, oy) in enumerate(others):
                ici(t, j, 2 * ox + oy, None).wait_recv()
                cp = d2d(t, j, 2 * ox + oy, c)
                cp.start()
                passed.append(cp)
        for t in range(n):
            for j, (ox, oy) in enumerate(others):
                d2d(t, j, 2 * ox + oy, 1 - c).wait_recv()
        for cp in firsts + passed:
            cp.wait_send()
        for cp in local:
            cp.wait()

    return pl.pallas_call(
        body, name=name,
        in_specs=[ANY] * n, out_specs=[ANY] * n,
        out_shape=[jax.ShapeDtypeStruct((N_SHARD,) + s.shape, s.dtype) for s in shards],
        scratch_shapes=[pltpu.SemaphoreType.DMA((3 * n,))] * 4 + [pltpu.SemaphoreType.DMA((n,))],
    )(*shards)


def _all_reduce_small(slab, *, name):
    def body(in_ref, out_ref, gath, send, recv):
        x, y, c, _ = _place()
        me = 4 * x + 2 * y + c
        gath[me] = in_ref[...]
        copies, peers = [], []
        for m in range(1, N_DEV):
            px = jnp.where((m >> 2) & 1, 1 - x, x)
            py = jnp.where((m >> 1) & 1, 1 - y, y)
            pc = jnp.where(m & 1, 1 - c, c)
            cp = pltpu.make_async_remote_copy(
                src_ref=in_ref, dst_ref=gath.at[me], send_sem=send.at[m - 1], recv_sem=recv.at[m - 1],
                device_id=(px, py, pc), device_id_type=MESH)
            cp.start()
            copies.append(cp)
            peers.append(4 * px + 2 * py + pc)
        for m in range(1, N_DEV):
            pltpu.make_async_remote_copy(
                src_ref=in_ref, dst_ref=gath.at[peers[m - 1]], send_sem=send.at[m - 1], recv_sem=recv.at[m - 1],
                device_id=(x, y, c), device_id_type=MESH).wait_recv()
        for cp in copies:
            cp.wait_send()
        acc = gath[0]
        for k in range(1, N_DEV):
            acc = acc + gath[k]
        out_ref[...] = acc

    vm = pl.BlockSpec(memory_space=pltpu.VMEM)
    return pl.pallas_call(
        body, name=name, in_specs=[vm], out_specs=vm,
        out_shape=jax.ShapeDtypeStruct(slab.shape, slab.dtype),
        scratch_shapes=[pltpu.VMEM((N_DEV,) + slab.shape, slab.dtype),
                        pltpu.SemaphoreType.DMA((N_DEV - 1,)), pltpu.SemaphoreType.DMA((N_DEV - 1,))],
    )(slab)


def _gather_ici(shards):
    n = len(shards)

    def copies(ins, outs, sems, sending):
        send, recv, _ = sems
        x, y, c, others = _place()
        me = 2 * x + y
        out = []
        for t in range(n):
            hr = shards[t].shape[0] // 2
            rows = pl.ds(pl.multiple_of(c * hr, HALO), hr)
            for j, (ox, oy) in enumerate(others):
                src_chip = me if sending else 2 * ox + oy
                out.append(pltpu.make_async_remote_copy(
                    src_ref=ins[t].at[rows], dst_ref=outs[t].at[src_chip, rows],
                    send_sem=send.at[3 * t + j], recv_sem=recv.at[3 * t + j],
                    device_id=(ox, oy, c) if sending else (x, y, c), device_id_type=MESH))
        return out

    def local(ins, outs, sems):
        x, y, _, _ = _place()
        return [pltpu.make_async_copy(ins[t], outs[t].at[2 * x + y], sems[2].at[t]) for t in range(n)]

    def start(ins, outs, sems):
        for cp in local(ins, outs, sems) + copies(ins, outs, sems, True):
            cp.start()

    def finish(ins, outs, sems):
        for cp in copies(ins, outs, sems, False):
            cp.wait_recv()
        for cp in copies(ins, outs, sems, True):
            cp.wait_send()
        for cp in local(ins, outs, sems):
            cp.wait()

    return _Comm(shards, [jax.ShapeDtypeStruct((N_SHARD,) + s.shape, s.dtype) for s in shards],
                 [3 * n, 3 * n, n], start, finish)


def _gather_d2d(parts):
    n = len(parts)

    def copies(outs, sems, sending):
        send, recv = sems
        x, y, c, others = _place()
        out = []
        for t in range(n):
            hr = parts[t].shape[1] // 2
            rows = pl.ds(pl.multiple_of((c if sending else 1 - c) * hr, HALO), hr)
            for j, (ox, oy) in enumerate(others):
                blk = outs[t].at[2 * ox + oy, rows]
                out.append(pltpu.make_async_remote_copy(
                    src_ref=blk, dst_ref=blk, send_sem=send.at[3 * t + j], recv_sem=recv.at[3 * t + j],
                    device_id=(x, y, 1 - c) if sending else (x, y, c), device_id_type=MESH))
        return out

    def start(ins, outs, sems):
        for cp in copies(outs, sems, True):
            cp.start()

    def finish(ins, outs, sems):
        for cp in copies(outs, sems, False):
            cp.wait_recv()
        for cp in copies(outs, sems, True):
            cp.wait_send()

    return _Comm(parts, [jax.ShapeDtypeStruct(p.shape, p.dtype) for p in parts], [3 * n, 3 * n], start, finish,
                 aliases={t: t for t in range(n)})


def _swap_halves(grads):
    n = len(grads)

    def copies(ins, outs, sems):
        x, y, c, _ = _place()
        out = []
        for t in range(n):
            hr = grads[t].shape[1] // 2
            rows = pl.ds(pl.multiple_of((1 - c) * hr, 8), hr)
            out.append(pltpu.make_async_remote_copy(
                src_ref=ins[t].at[:, rows, :], dst_ref=outs[t], send_sem=sems[0].at[t], recv_sem=sems[1].at[t],
                device_id=(x, y, 1 - c), device_id_type=MESH))
        return out

    def start(ins, outs, sems):
        for cp in copies(ins, outs, sems):
            cp.start()

    def finish(ins, outs, sems):
        for cp in copies(ins, outs, sems):
            cp.wait()

    return _Comm(grads, [jax.ShapeDtypeStruct((N_SHARD, g.shape[1] // 2, g.shape[2]), g.dtype) for g in grads],
                 [n, n], start, finish)


def _pair_sum(g, got, c, *, name):
    ns, R, C = g.shape
    hr = R // 2

    def body(c_ref, g_ref, r_ref, o_ref):
        o_ref[...] = (g_ref[...] + r_ref[...]).astype(BF16)

    return pl.pallas_call(
        body, name=name,
        grid_spec=pltpu.PrefetchScalarGridSpec(
            num_scalar_prefetch=1, grid=(ns,),
            in_specs=[pl.BlockSpec((None, hr, C), lambda s, cr: (s, cr[0], 0)),
                      pl.BlockSpec((None, hr, C), lambda s, cr: (s, 0, 0))],
            out_specs=pl.BlockSpec((None, hr, C), lambda s, cr: (s, 0, 0))),
        out_shape=jax.ShapeDtypeStruct((ns, hr, C), BF16),
        compiler_params=_params("arbitrary"),
    )(c, g, got)


def _scatter_chips(sums):
    n = len(sums)

    def copies(ins, outs, sems, sending):
        x, y, c, others = _place()
        me = 2 * x + y
        out = []
        for t in range(n):
            for j, (ox, oy) in enumerate(others):
                there = 2 * ox + oy
                out.append(pltpu.make_async_remote_copy(
                    src_ref=ins[t].at[there if sending else me], dst_ref=outs[t].at[me if sending else there],
                    send_sem=sems[0].at[3 * t + j], recv_sem=sems[1].at[3 * t + j],
                    device_id=(ox, oy, c) if sending else (x, y, c), device_id_type=MESH))
        return out

    def start(ins, outs, sems):
        for cp in copies(ins, outs, sems, True):
            cp.start()

    def finish(ins, outs, sems):
        for cp in copies(ins, outs, sems, False):
            cp.wait_recv()
        for cp in copies(ins, outs, sems, True):
            cp.wait_send()

    return _Comm(sums, [jax.ShapeDtypeStruct(s.shape, s.dtype) for s in sums], [3 * n, 3 * n], start, finish)


def _chip_sum(g, got, landed, idx, *, name):
    ns, R, C = g.shape
    hr = R // 2

    def body(i_ref, g_ref, r_ref, a_ref, b_ref, c_ref, o_ref):
        acc = g_ref[...] + r_ref[...]
        for ref in (a_ref, b_ref, c_ref):
            acc = acc + ref[...].astype(F32)
        o_ref[...] = acc

    other = lambda k: pl.BlockSpec((None, hr, C), lambda s, ir: (ir[2 + k], 0, 0))
    return pl.pallas_call(
        body, name=name,
        grid_spec=pltpu.PrefetchScalarGridSpec(
            num_scalar_prefetch=1, grid=(1,),
            in_specs=[pl.BlockSpec((None, hr, C), lambda s, ir: (ir[0], ir[1], 0)),
                      pl.BlockSpec((None, hr, C), lambda s, ir: (ir[0], 0, 0)),
                      other(0), other(1), other(2)],
            out_specs=pl.BlockSpec((hr, C), lambda s, ir: (ir[1], 0))),
        out_shape=jax.ShapeDtypeStruct((R, C), F32),
        compiler_params=_params("arbitrary"),
    )(idx, g, got, landed, landed, landed)


def _share_halves(halves):
    n = len(halves)

    def copies(outs, sems, sending):
        x, y, c, _ = _place()
        out = []
        for t in range(n):
            hr = halves[t].shape[0] // 2
            rows = pl.ds(pl.multiple_of((c if sending else 1 - c) * hr, 8), hr)
            out.append(pltpu.make_async_remote_copy(
                src_ref=outs[t].at[rows, :], dst_ref=outs[t].at[rows, :], send_sem=sems[0].at[t],
                recv_sem=sems[1].at[t], device_id=(x, y, 1 - c) if sending else (x, y, c), device_id_type=MESH))
        return out

    def start(ins, outs, sems):
        for cp in copies(outs, sems, True):
            cp.start()

    def finish(ins, outs, sems):
        for cp in copies(outs, sems, False):
            cp.wait_recv()
        for cp in copies(outs, sems, True):
            cp.wait_send()

    return _Comm(halves, [jax.ShapeDtypeStruct(h.shape, h.dtype) for h in halves], [n, n], start, finish,
                 aliases={t: t for t in range(n)})


def _adamw(w, g, m, v, *, name):
    R, C = w.shape
    tr = R
    for cand in (256, 128, 64, 32, 16, 8):
        if R % cand == 0:
            tr = cand
            break

    def body(w_ref, g_ref, m_ref, v_ref, d_ref, mo_ref, vo_ref):
        gv = g_ref[...]
        mn = ADAM_B1 * m_ref[...] + (1.0 - ADAM_B1) * gv
        vn = ADAM_B2 * v_ref[...] + (1.0 - ADAM_B2) * (gv * gv)
        m_hat = mn / (1.0 - ADAM_B1 ** ADAM_STEP)
        v_hat = vn / (1.0 - ADAM_B2 ** ADAM_STEP)
        d_ref[...] = -ADAM_LR * (m_hat / (jnp.sqrt(v_hat) + ADAM_EPS) + ADAM_WD * w_ref[...])
        mo_ref[...] = mn
        vo_ref[...] = vn

    blk = pl.BlockSpec((tr, C), lambda i: (i, 0))
    return pl.pallas_call(
        body, name=name, grid=(R // tr,), in_specs=[blk] * 4, out_specs=[blk] * 3,
        out_shape=[jax.ShapeDtypeStruct((R, C), F32)] * 3,
        compiler_params=_params("arbitrary"),
    )(w, g, m, v)


def _pack_small(D, meta, n1, nm, n3, nf, gc, ga, bf, cw):
    def row(a):
        a = a.reshape(-1, a.shape[-1])
        return jnp.pad(a, ((0, 0), (0, D - a.shape[-1])))
    rows = [row(meta), row(n1), row(nm), row(n3), row(nf), row(jnp.concatenate([gc, ga], axis=-1)), row(bf), row(cw)]
    slab = jnp.concatenate(rows, axis=0)
    return jnp.pad(slab, ((0, SMALL_ROWS - slab.shape[0]), (0, 0)))


def _unpack_small(slab, like):
    meta, n1, nm, n3, nf, gc, ga, bf, cw = like
    nmeta, mc = meta.shape
    out = [slab[:nmeta, :mc].reshape(meta.shape)]
    r = nmeta
    for a in (n1, nm, n3, nf):
        out.append(slab[r, :a.shape[-1]].reshape(a.shape))
        r += 1
    cd = gc.shape[-1]
    out.append(slab[r, :cd].reshape(gc.shape))
    out.append(slab[r, cd:cd + ga.shape[-1]].reshape(ga.shape))
    r += 1
    out.append(slab[r, :bf.shape[-1]].reshape(bf.shape))
    r += 1
    out.append(slab[r:r + 3, :cw.shape[-1]].reshape(cw.shape))
    return out


def kernel(x, meta_tokens, ffn1_norm, ffn1_w_gu, ffn1_w_down, mix_norm, w_in, conv_w, b_f, out_norm_conv, out_norm_attn, w_out, ffn2_norm, ffn2_w_gu, ffn2_w_down, final_norm, loss_target, m_meta_tokens, m_ffn1_norm, m_ffn1_w_gu, m_ffn1_w_down, m_mix_norm, m_w_in, m_conv_w, m_b_f, m_out_norm_conv, m_out_norm_attn, m_w_out, m_ffn2_norm, m_ffn2_w_gu, m_ffn2_w_down, m_final_norm, v_meta_tokens, v_ffn1_norm, v_ffn1_w_gu, v_ffn1_w_down, v_mix_norm, v_w_in, v_conv_w, v_b_f, v_out_norm_conv, v_out_norm_attn, v_w_out, v_ffn2_norm, v_ffn2_w_gu, v_ffn2_w_down, v_final_norm):
    B, S, D = x.shape
    L = S + N_META
    T = B * L
    tm = L // 3
    assert tm * 3 == L and tm % HALO == 0
    guc = ffn1_w_gu.shape[-1]
    ff = N_SHARD * guc // 2
    H = b_f.shape[-1]
    AD = H * HEAD_DIM
    CD = conv_w.shape[-1] * N_SHARD
    assert CD == AD and CD + AD == D and CD % LANES == 0
    n_main = 3 * CD + 3 * AD
    ins = w_in.shape[-1]

    xi, yi, ci = lax.axis_index("x"), lax.axis_index("y"), lax.axis_index("c")
    chip = 2 * xi + yi

    small_shard = jnp.zeros((2 * HALO, meta_tokens.shape[-1]), F32)
    small_shard = small_shard.at[:N_META].set(meta_tokens)
    small_shard = small_shard.at[N_META:N_META + 3, :conv_w.shape[-1]].set(conv_w[0])
    big = [ffn1_w_gu[0], ffn1_w_down[0], w_in[0], w_out[0], ffn2_w_gu[0], ffn2_w_down[0]]
    wgu1_s, wd1_s, win_s, wout_s, wgu2_s, wd2_s = [w.astype(BF16) for w in big]
    wgu1, wd1, small_g = _all_gather_shards([wgu1_s, wd1_s, small_shard], name="gather_ffn1")
    wd1 = wd1.reshape(ff, D)
    meta_f = jnp.moveaxis(small_g[:, :N_META], 0, 1).reshape(N_META, D)
    cw_f = jnp.moveaxis(small_g[:, N_META:N_META + 3, :conv_w.shape[-1]], 0, 1).reshape(3, CD)
    cw8 = jnp.pad(cw_f, ((0, 5), (0, 0)))
    bf_p = jnp.pad(b_f, ((0, 0), (0, LANES - H)))
    gid = jnp.arange(CD) // HEAD_DIM
    pmat = jnp.where(gid[:, None] == gid[None, :], 1.0 / HEAD_DIM, 0.0).astype(BF16)

    gu_shape = jax.ShapeDtypeStruct((2, T, ff), BF16)
    gu_w_spec = pl.BlockSpec((None, D, guc), lambda s, i: (s, 0, 0))
    gu_o_spec = pl.BlockSpec((None, tm, guc), lambda s, i: (s // 2, i, s % 2))

    h0 = jnp.concatenate([jnp.broadcast_to(meta_f[None], (B, N_META, D)), x], axis=1).reshape(T, D)
    n1 = _rmsnorm(h0, ffn1_norm, tm=tm, name="ffn1_norm")
    gu1, mix_w = _matmul_nn(n1, wgu1, tm=tm, nb=N_SHARD, w_spec=gu_w_spec, out_shape=gu_shape, out_spec=gu_o_spec,
                            name="ffn1_up", comm=_gather_ici([win_s, wout_s]))
    h1, (win_g, wout_g) = _ffn_down(gu1, wd1, h0, tm=tm, name="ffn1_down", comm=_gather_d2d(mix_w))
    wout_f = wout_g.reshape(D, D)
    win_f = jnp.moveaxis(win_g, 0, 1).reshape(D, N_SHARD * ins)
    win_main = win_f[:, :n_main]
    win_fg = jnp.pad(win_f[:, n_main:], ((0, 0), (0, LANES - H)))

    n2 = _rmsnorm(h1, mix_norm, tm=tm, name="mix_norm")
    proj, _ = _matmul_nn(n2, win_main, tm=tm, nb=n_main // CD,
                         w_spec=pl.BlockSpec((D, CD), lambda s, i: (0, s)),
                         out_shape=jax.ShapeDtypeStruct((T, n_main), BF16),
                         out_spec=pl.BlockSpec((tm, CD), lambda s, i: (i, s)), name="mix_in")
    fg, _ = _matmul_nn(n2, win_fg, tm=tm, nb=1, w_spec=pl.BlockSpec((D, LANES), lambda s, i: (0, 0)),
                       out_shape=jax.ShapeDtypeStruct((T, LANES), F32),
                       out_spec=pl.BlockSpec((tm, LANES), lambda s, i: (i, 0)), name="mix_in_fg")
    proj3 = proj.reshape(B, L, n_main)
    fg3 = fg.reshape(B, L, LANES)
    fc = _fcum(fg3, bf_p, ch=tm, name="forget_cumsum")
    fr = fc[:, :, :H].reshape(B, L // tm, tm, H).transpose(0, 1, 3, 2)
    (o, lse), ffn2_w = _attn_fwd(proj3, fc, fr, tq=tm, n_heads=H, name="attn_fwd",
                                 comm=_gather_ici([wgu2_s, wd2_s]))
    (h2, ymix), (wgu2, wd2) = _mix_out(proj3, o, cw8, out_norm_conv, out_norm_attn, wout_f, h1.reshape(B, L, D), pmat,
                                       tm=tm, name="mix_out", comm=_gather_d2d(ffn2_w))
    wd2 = wd2.reshape(ff, D)
    h2 = h2.reshape(T, D)

    n3 = _rmsnorm(h2, ffn2_norm, tm=tm, name="ffn2_norm")
    gu2, _ = _matmul_nn(n3, wgu2, tm=tm, nb=N_SHARD, w_spec=gu_w_spec, out_shape=gu_shape, out_spec=gu_o_spec,
                        name="ffn2_up")
    h3, _ = _ffn_down(gu2, wd2, h2, tm=tm, name="ffn2_down")

    tgt = jnp.pad(loss_target, ((0, 0), (N_META, 0), (0, 0)))
    dh3, dh3b, d_gf, loss_part = _final(h3.reshape(B, L, D), final_norm.reshape(1, D), tgt, tm=tm, name="final")

    c_arr = jnp.reshape(ci, (1,)).astype(jnp.int32)
    ks = jnp.arange(N_SHARD - 1, dtype=jnp.int32)
    idx = jnp.concatenate([jnp.stack([chip, ci]).astype(jnp.int32), ks + (ks >= chip).astype(jnp.int32)])

    def pair_sums(grads, got, names):
        return [_pair_sum(g, r, c_arr, name="pair_sum_" + nm) for g, r, nm in zip(grads, got, names)]

    def chip_sums(grads, got, landed, names):
        return [_chip_sum(g, r, l, idx, name="chip_sum_" + nm) for g, r, l, nm in zip(grads, got, landed, names)]

    def dw_down(gu, dhb, name, comm=None):
        return _matmul_tn(
            [gu, gu], dhb, tm=tm, nb=ff // guc, kb=guc, silu=True,
            x_specs=[pl.BlockSpec((None, tm, guc), lambda j, i: (0, i, j)),
                     pl.BlockSpec((None, tm, guc), lambda j, i: (1, i, j))],
            y_spec=pl.BlockSpec((tm, D), lambda j, i: (i, 0)),
            out_shape=jax.ShapeDtypeStruct((ff, D), F32), out_spec=pl.BlockSpec((guc, D), lambda j, i: (j, 0)),
            name=name, comm=comm)

    def dw_up(n, dgu, name, comm=None):
        return _matmul_tn(
            [n], dgu, tm=tm, nb=N_SHARD, kb=D, silu=False,
            x_specs=[pl.BlockSpec((tm, D), lambda s, i: (i, 0))],
            y_spec=pl.BlockSpec((None, tm, guc), lambda s, i: (s // 2, i, s % 2)),
            out_shape=jax.ShapeDtypeStruct((N_SHARD, D, guc), F32),
            out_spec=pl.BlockSpec((None, D, guc), lambda s, i: (s, 0, 0)), name=name, comm=comm)

    dh3f, dh3b = dh3.reshape(T, D), dh3b.reshape(T, D)
    dgu2, _ = _ffn_bwd_act(dh3b, gu2, wd2, tm=tm, guc=guc, name="ffn2_bwd_act")
    (dh2, dh2b, d_g3), _ = _ffn_bwd_in(dgu2, wgu2, h2, ffn2_norm, dh3f, tm=tm, scale=1.0, name="ffn2_bwd_in")
    d_wd2, _ = dw_down(gu2, dh3b, "ffn2_dw_down")
    d_wgu2, _ = dw_up(n3, dgu2, "ffn2_dw_up")
    grads_f2 = [d_wgu2, d_wd2.reshape(N_SHARD, ff // N_SHARD, D)]
    names_f2 = ["wgu2", "wd2"]

    dh2b3 = dh2b.reshape(B, L, D)
    (d_bg, d_cv, d_o, d_gc, d_ga, d_cw), got_f2 = _mix_out_bwd(
        dh2b3, proj3, o, cw8, out_norm_conv, out_norm_attn, wout_f, pmat, tm=tm, name="mix_out_bwd",
        comm=_swap_halves(grads_f2))
    sums_f2 = pair_sums(grads_f2, got_f2, names_f2)
    d_wout, _ = _matmul_tn(
        [ymix.reshape(T, D)], dh2b, tm=tm, nb=1, kb=D, silu=False,
        x_specs=[pl.BlockSpec((tm, D), lambda s, i: (i, 0))], y_spec=pl.BlockSpec((tm, D), lambda s, i: (i, 0)),
        out_shape=jax.ShapeDtypeStruct((D, D), F32), out_spec=pl.BlockSpec((D, D), lambda s, i: (0, 0)),
        name="dw_out")
    d_cc = _conv_bwd(d_cv, proj3, cw8, tm=tm, name="conv_bwd")
    (d_q, d_k, d_v, d_fr, d_fq), landed_f2 = _attn_bwd(proj3, o, d_o, lse, fc, fr, tq=tm, n_heads=H, name="attn_bwd",
                                                       comm=_scatter_chips(sums_f2))
    halves_f2 = chip_sums(grads_f2, got_f2, landed_f2, names_f2)
    d_fc = d_fq + jnp.pad(d_fr.transpose(0, 1, 3, 2).reshape(B, L, H), ((0, 0), (0, 0), (0, LANES - H)))
    d_fg, d_bf = _fcum_bwd(d_fc, fg3, bf_p, ch=tm, name="forget_cumsum_bwd")

    parts = [d_bg.reshape(T, CD), d_cc.reshape(T, 2 * CD), d_q.reshape(T, AD), d_k.reshape(T, AD),
             d_v.reshape(T, AD), d_fg.reshape(T, LANES)]
    (dh1, dh1b, d_gm), g_f2 = _mix_bwd_in(parts, win_main, win_fg, h1, mix_norm, dh2, tm=tm, scale=0.5,
                                          name="mix_bwd_in", comm=_share_halves(halves_f2))
    d_win_parts = []
    for k, p in enumerate(parts):
        wdt = p.shape[1]
        nb = max(wdt // CD, 1)
        bw = wdt // nb
        d_win_parts.append(_matmul_tn(
            [n2], p, tm=tm, nb=nb, kb=D, silu=False,
            x_specs=[pl.BlockSpec((tm, D), lambda s, i: (i, 0))], y_spec=pl.BlockSpec((tm, bw), lambda s, i: (i, s)),
            out_shape=jax.ShapeDtypeStruct((D, wdt), F32), out_spec=pl.BlockSpec((D, bw), lambda s, i: (0, s)),
            name="dw_in_%d" % k)[0])
    d_win_parts[-1] = d_win_parts[-1][:, :H]
    d_win = jnp.moveaxis(jnp.concatenate(d_win_parts, axis=1).reshape(D, N_SHARD, ins), 1, 0)
    grads_mx = [d_win, d_wout.reshape(N_SHARD, D // N_SHARD, D)]
    names_mx = ["win", "wout"]

    dgu1, got_mx = _ffn_bwd_act(dh1b, gu1, wd1, tm=tm, guc=guc, name="ffn1_bwd_act", comm=_swap_halves(grads_mx))
    sums_mx = pair_sums(grads_mx, got_mx, names_mx)
    d_wd1, landed_mx = dw_down(gu1, dh1b, "ffn1_dw_down", comm=_scatter_chips(sums_mx))
    halves_mx = chip_sums(grads_mx, got_mx, landed_mx, names_mx)
    d_wgu1, g_mx = dw_up(n1, dgu1, "ffn1_dw_up", comm=_share_halves(halves_mx))
    grads_f1 = [d_wgu1, d_wd1.reshape(N_SHARD, ff // N_SHARD, D)]
    names_f1 = ["wgu1", "wd1"]
    (dh0, _, d_g1), got_f1 = _ffn_bwd_in(dgu1, wgu1, h0, ffn1_norm, dh1, tm=tm, scale=1.0, name="ffn1_bwd_in",
                                         comm=_swap_halves(grads_f1))
    sums_f1 = pair_sums(grads_f1, got_f1, names_f1)
    landed_f1 = _run_comm(_scatter_chips(sums_f1), name="scatter_ffn1")
    halves_f1 = chip_sums(grads_f1, got_f1, landed_f1, names_f1)
    g_f1 = _run_comm(_share_halves(halves_f1), name="share_ffn1")
    g_big = [g_f1[0], g_f1[1], g_mx[0], g_mx[1], g_f2[0], g_f2[1]]
    dh0 = dh0.reshape(B, L, D)
    grad_x = dh0[:, N_META:]
    d_meta = jnp.sum(dh0[:, :N_META], axis=0)

    loss_row = jnp.zeros((1, D), F32).at[0, 0].set(loss_part[0, 0])
    slab = _pack_small(D, d_meta, d_g1, d_gm, d_g3, d_gf, d_gc, d_ga, d_bf[:, :H], d_cw[:3])
    slab = slab.at[SMALL_ROWS - 1].set(loss_row[0])
    total = _all_reduce_small(slab, name="reduce_small")
    loss = total[SMALL_ROWS - 1, 0]
    mcols = meta_tokens.shape[-1]
    ccols = conv_w.shape[-1]
    full_like = (jnp.zeros((N_META, D)), ffn1_norm, mix_norm, ffn2_norm, final_norm.reshape(1, D), out_norm_conv,
                 out_norm_attn, b_f, jnp.zeros((1, 3, CD)))
    g_small = _unpack_small(total, full_like)
    g_small[0] = lax.dynamic_slice_in_dim(g_small[0], chip * mcols, mcols, axis=1)
    g_small[8] = lax.dynamic_slice_in_dim(g_small[8], chip * ccols, ccols, axis=2)

    def small_slab(meta, a1, am, a3, af, gc, ga, bf, cw):
        return _pack_small(D, meta, a1, am, a3, af.reshape(1, D), gc, ga, bf, cw[0])

    w_small = small_slab(meta_tokens, ffn1_norm, mix_norm, ffn2_norm, final_norm, out_norm_conv, out_norm_attn, b_f, conv_w)
    m_small = small_slab(m_meta_tokens, m_ffn1_norm, m_mix_norm, m_ffn2_norm, m_final_norm, m_out_norm_conv,
                         m_out_norm_attn, m_b_f, m_conv_w)
    v_small = small_slab(v_meta_tokens, v_ffn1_norm, v_mix_norm, v_ffn2_norm, v_final_norm, v_out_norm_conv,
                         v_out_norm_attn, v_b_f, v_conv_w)
    gs = list(g_small)
    gs[4] = gs[4].reshape(final_norm.shape)
    g_slab = small_slab(gs[0], gs[1], gs[2], gs[3], gs[4], gs[5], gs[6], gs[7], gs[8])
    local_like = (meta_tokens, ffn1_norm, mix_norm, ffn2_norm, final_norm.reshape(1, D), out_norm_conv, out_norm_attn,
                  b_f, conv_w)
    small_out = [_unpack_small(s, local_like) for s in _adamw(w_small, g_slab, m_small, v_small, name="adamw_small")]
    for lst in small_out:
        lst[4] = lst[4].reshape(final_norm.shape)

    names = ["wgu1", "wd1", "win", "wout", "wgu2", "wd2"]
    w_big = big
    m_big = [m_ffn1_w_gu[0], m_ffn1_w_down[0], m_w_in[0], m_w_out[0], m_ffn2_w_gu[0], m_ffn2_w_down[0]]
    v_big = [v_ffn1_w_gu[0], v_ffn1_w_down[0], v_w_in[0], v_w_out[0], v_ffn2_w_gu[0], v_ffn2_w_down[0]]
    big_out = [_adamw(w, g, m, v, name="adamw_" + nm) for w, g, m, v, nm in zip(w_big, g_big, m_big, v_big, names)]

    def assemble(small, bigs):
        meta, a1, am, a3, af, gc, ga, bf, cw = small
        gu1_, d1_, win_, wout_, gu2_, d2_ = [b[None] for b in bigs]
        return [meta, a1, gu1_, d1_, am, win_, cw, bf, gc, ga, wout_, a3, gu2_, d2_, af]

    gs_out = list(g_small)
    gs_out[4] = gs_out[4].reshape(final_norm.shape)
    grads_out = assemble(gs_out, g_big)
    delta_out = assemble(small_out[0], [b[0] for b in big_out])
    m_out = assemble(small_out[1], [b[1] for b in big_out])
    v_out = assemble(small_out[2], [b[2] for b in big_out])
    return (loss, grad_x, *grads_out, *delta_out, *m_out, *v_out)
```

```python
import functools

import jax
import jax.numpy as jnp
from jax import lax
from jax.experimental import pallas as pl
from jax.experimental.pallas import tpu as pltpu

F32 = jnp.float32
BF16 = jnp.bfloat16

EPS = 1e-6
N_META = 16
HEAD_DIM = 64
N_SHARD = 4
N_DEV = 8
HALO = 16
LANES = 128
SMALL_ROWS = 32
VMEM_LIMIT_V7X = 56 * 1024 * 1024
NEG = -1e30

ADAM_LR = 0.001
ADAM_B1 = 0.9
ADAM_B2 = 0.999
ADAM_EPS = 1e-08
ADAM_WD = 0.01
ADAM_STEP = 10

MESH = pl.DeviceIdType.MESH
ANY = pl.BlockSpec(memory_space=pl.ANY)
NT_DIMS = (((1,), (1,)), ((), ()))
TN_DIMS = (((0,), (0,)), ((), ()))


def _params(*sem):
    return pltpu.CompilerParams(dimension_semantics=sem, vmem_limit_bytes=VMEM_LIMIT_V7X)


class _Comm:
    def __init__(self, ins, out_shapes, sems, start, finish, aliases=None):
        self.ins, self.out_shapes, self.sems = list(ins), list(out_shapes), list(sems)
        self.start, self.finish, self.aliases = start, finish, dict(aliases or {})


def _join(a, b):
    ni, no, ns = len(a.ins), len(a.out_shapes), len(a.sems)

    def start(ins, outs, sems):
        a.start(ins[:ni], outs[:no], sems[:ns])
        b.start(ins[ni:], outs[no:], sems[ns:])

    def finish(ins, outs, sems):
        a.finish(ins[:ni], outs[:no], sems[:ns])
        b.finish(ins[ni:], outs[no:], sems[ns:])

    aliases = dict(a.aliases)
    aliases.update({ni + i: no + j for i, j in b.aliases.items()})
    return _Comm(a.ins + b.ins, a.out_shapes + b.out_shapes, a.sems + b.sems, start, finish, aliases)


def _launch(body, *, name, grid, in_specs, out_specs, out_shape, args, scratch_shapes=(), comm=None):
    single = not isinstance(out_shape, (list, tuple))
    out_specs = [out_specs] if single else list(out_specs)
    out_shape = [out_shape] if single else list(out_shape)
    in_specs, scratch_shapes = list(in_specs), list(scratch_shapes)
    params = _params(*(("arbitrary",) * len(grid)))
    if comm is None:
        res = pl.pallas_call(body, name=name, grid=grid, in_specs=in_specs, out_specs=out_specs,
                             out_shape=out_shape, scratch_shapes=scratch_shapes, compiler_params=params)(*args)
        return (res[0] if single else list(res)), None
    n_in, n_out, n_scr = len(in_specs), len(out_specs), len(scratch_shapes)
    c_in, c_out = len(comm.ins), len(comm.out_shapes)

    def carrier(*refs):
        p = 0
        a = refs[p:p + n_in]; p += n_in
        ci = refs[p:p + c_in]; p += c_in
        o = refs[p:p + n_out]; p += n_out
        co = refs[p:p + c_out]; p += c_out
        s = refs[p:p + n_scr]; p += n_scr
        cs = refs[p:]
        first = functools.reduce(lambda u, v: u & v, [pl.program_id(k) == 0 for k in range(len(grid))])
        last = functools.reduce(lambda u, v: u & v, [pl.program_id(k) == grid[k] - 1 for k in range(len(grid))])

        @pl.when(first)
        def _():
            comm.start(ci, co, cs)

        body(*a, *o, *s)

        @pl.when(last)
        def _():
            comm.finish(ci, co, cs)

    res = pl.pallas_call(
        carrier, name=name, grid=grid, in_specs=in_specs + [ANY] * c_in, out_specs=out_specs + [ANY] * c_out,
        out_shape=out_shape + comm.out_shapes,
        scratch_shapes=scratch_shapes + [pltpu.SemaphoreType.DMA((k,)) for k in comm.sems],
        input_output_aliases={n_in + i: n_out + j for i, j in comm.aliases.items()},
        compiler_params=params)(*args, *comm.ins)
    main = list(res[:n_out])
    return (main[0] if single else main), list(res[n_out:])


def _run_comm(comm, *, name):
    c_in, c_out = len(comm.ins), len(comm.out_shapes)

    def body(*refs):
        ci, co, cs = refs[:c_in], refs[c_in:c_in + c_out], refs[c_in + c_out:]
        comm.start(ci, co, cs)
        comm.finish(ci, co, cs)

    return list(pl.pallas_call(
        body, name=name, in_specs=[ANY] * c_in, out_specs=[ANY] * c_out, out_shape=comm.out_shapes,
        scratch_shapes=[pltpu.SemaphoreType.DMA((k,)) for k in comm.sems],
        input_output_aliases=comm.aliases)(*comm.ins))


def _chunks(width, step=512):
    out, c0 = [], 0
    while c0 < width:
        cw = min(step, width - c0)
        out.append((c0, cw))
        c0 += cw
    return out


def _split2(v):
    hi = v.astype(BF16)
    lo = (v - hi.astype(F32)).astype(BF16)
    return hi, lo


def _split3(v):
    hi = v.astype(BF16)
    r = v - hi.astype(F32)
    mid = r.astype(BF16)
    lo = (r - mid.astype(F32)).astype(BF16)
    return hi, mid, lo


def _dot(a, b):
    return jnp.dot(a, b, preferred_element_type=F32)


def _dot_nt(a, b):
    return lax.dot_general(a, b, NT_DIMS, preferred_element_type=F32)


def _dot_tn(a, b):
    return lax.dot_general(a, b, TN_DIMS, preferred_element_type=F32)


def _silu_mul(g, u):
    return g * jax.nn.sigmoid(g) * u


def _rms_bwd(dn, h, gain, dres):
    r = lax.rsqrt(jnp.mean(h * h, axis=-1, keepdims=True) + EPS)
    y = h * r
    dgain = jnp.sum(dn * y, axis=0, keepdims=True)
    dy = dn * gain
    dh = dres + r * (dy - y * jnp.mean(dy * y, axis=-1, keepdims=True))
    return dh, dgain


def _group_mean(v, p):
    hi, lo = _split2(v)
    return _dot(hi, p) + _dot(lo, p)


def _row_of(a, k):
    rows = lax.broadcasted_iota(jnp.int32, a.shape, 0)
    return jnp.sum(jnp.where(rows == k, a, 0.0), axis=0, keepdims=True)


def _causal_conv(u, prev, w):
    rows = lax.broadcasted_iota(jnp.int32, u.shape, 0)
    p1 = _row_of(prev, HALO - 1)
    p2 = _row_of(prev, HALO - 2)
    u1 = jnp.where(rows == 0, p1, pltpu.roll(u, 1, 0))
    u2 = jnp.where(rows == 0, p2, jnp.where(rows == 1, p1, pltpu.roll(u, 2, 0)))
    return w[2:3, :] * u + w[1:2, :] * u1 + w[0:1, :] * u2, u1, u2


def _rmsnorm(h, g, *, tm, name):
    T, D = h.shape

    def body(h_ref, g_ref, n_ref):
        x = h_ref[...]
        r = lax.rsqrt(jnp.mean(x * x, axis=-1, keepdims=True) + EPS)
        n_ref[...] = (x * r * g_ref[...]).astype(BF16)

    return pl.pallas_call(
        body, name=name, grid=(T // tm,),
        in_specs=[pl.BlockSpec((tm, D), lambda i: (i, 0)), pl.BlockSpec((1, D), lambda i: (0, 0))],
        out_specs=pl.BlockSpec((tm, D), lambda i: (i, 0)),
        out_shape=jax.ShapeDtypeStruct((T, D), BF16),
        compiler_params=_params("arbitrary"),
    )(h, g)


def _matmul_nn(x, w, *, tm, nb, w_spec, out_shape, out_spec, name, comm=None):
    T, K = x.shape

    def body(x_ref, w_ref, o_ref):
        o_ref[...] = _dot(x_ref[...], w_ref[...]).astype(o_ref.dtype)

    return _launch(
        body, name=name, grid=(nb, T // tm),
        in_specs=[pl.BlockSpec((tm, K), lambda s, i: (i, 0)), w_spec],
        out_specs=out_spec, out_shape=out_shape, args=(x, w), comm=comm)


def _ffn_down(gu, wd, h, *, tm, name, comm=None):
    _, T, ff = gu.shape
    D = h.shape[1]
    chunks = _chunks(ff)

    def body(g_ref, u_ref, wd_hbm, h_ref, o_ref, wd_v, sem):
        @pl.when(pl.program_id(0) == 0)
        def _():
            cp = pltpu.make_async_copy(wd_hbm, wd_v, sem)
            cp.start()
            cp.wait()

        acc = jnp.zeros((tm, D), F32)
        for c0, cw in chunks:
            a = _silu_mul(g_ref[:, c0:c0 + cw].astype(F32), u_ref[:, c0:c0 + cw].astype(F32))
            acc = acc + _dot(a.astype(BF16), wd_v[c0:c0 + cw, :])
        o_ref[...] = h_ref[...] + 0.5 * acc

    return _launch(
        body, name=name, grid=(T // tm,),
        in_specs=[pl.BlockSpec((None, tm, ff), lambda i: (0, i, 0)),
                  pl.BlockSpec((None, tm, ff), lambda i: (1, i, 0)),
                  ANY,
                  pl.BlockSpec((tm, D), lambda i: (i, 0))],
        out_specs=pl.BlockSpec((tm, D), lambda i: (i, 0)),
        out_shape=jax.ShapeDtypeStruct((T, D), F32),
        scratch_shapes=[pltpu.VMEM((ff, D), BF16), pltpu.SemaphoreType.DMA],
        args=(gu, gu, wd, h), comm=comm)


def _ffn_bwd_act(df, gu, wd, *, tm, guc, name, comm=None):
    _, T, ff = gu.shape
    D = df.shape[1]
    nj = ff // guc
    chunks = _chunks(guc)

    def body(df_ref, g_ref, u_ref, wd_ref, o_ref):
        dfv = df_ref[...]
        for c0, cw in chunks:
            da = _dot_nt(dfv, wd_ref[c0:c0 + cw, :])
            g = g_ref[:, c0:c0 + cw].astype(F32)
            u = u_ref[:, c0:c0 + cw].astype(F32)
            sg = jax.nn.sigmoid(g)
            silu = g * sg
            o_ref[0, :, c0:c0 + cw] = (da * u * (sg * (1.0 + g * (1.0 - sg)))).astype(BF16)
            o_ref[1, :, c0:c0 + cw] = (da * silu).astype(BF16)

    return _launch(
        body, name=name, grid=(nj, T // tm),
        in_specs=[pl.BlockSpec((tm, D), lambda j, i: (i, 0)),
                  pl.BlockSpec((None, tm, guc), lambda j, i: (0, i, j)),
                  pl.BlockSpec((None, tm, guc), lambda j, i: (1, i, j)),
                  pl.BlockSpec((guc, D), lambda j, i: (j, 0))],
        out_specs=pl.BlockSpec((2, tm, guc), lambda j, i: (0, i, j)),
        out_shape=jax.ShapeDtypeStruct((2, T, ff), BF16),
        args=(df, gu, gu, wd), comm=comm)


def _ffn_bwd_in(dgu, wgu, h, g, dres, *, tm, scale, name, comm=None):
    _, T, ff = dgu.shape
    ns, D, guc = wgu.shape
    nj = ff // guc
    chunks = _chunks(guc)

    def body(dgu_ref, w_hbm, h_ref, g_ref, dres_ref, dh_ref, dhb_ref, dg_ref, w_v, acc, sem):
        i, j = pl.program_id(0), pl.program_id(1)

        @pl.when((i == 0) & (j == 0))
        def _():
            cp = pltpu.make_async_copy(w_hbm, w_v, sem)
            cp.start()
            cp.wait()
            dg_ref[...] = jnp.zeros_like(dg_ref)

        part = jnp.zeros((tm, D), F32)
        for c0, cw in chunks:
            part = part + _dot_nt(dgu_ref[0, :, c0:c0 + cw], w_v[j, :, c0:c0 + cw])
            part = part + _dot_nt(dgu_ref[1, :, c0:c0 + cw], w_v[nj + j, :, c0:c0 + cw])

        @pl.when(j == 0)
        def _():
            acc[...] = part

        @pl.when(j > 0)
        def _():
            acc[...] += part

        @pl.when(j == nj - 1)
        def _():
            dh, dgain = _rms_bwd(acc[...], h_ref[...], g_ref[...], dres_ref[...])
            dh_ref[...] = dh
            dhb_ref[...] = (scale * dh).astype(BF16)
            dg_ref[...] += dgain

    return _launch(
        body, name=name, grid=(T // tm, nj),
        in_specs=[pl.BlockSpec((2, tm, guc), lambda i, j: (0, i, j)),
                  ANY,
                  pl.BlockSpec((tm, D), lambda i, j: (i, 0)),
                  pl.BlockSpec((1, D), lambda i, j: (0, 0)),
                  pl.BlockSpec((tm, D), lambda i, j: (i, 0))],
        out_specs=[pl.BlockSpec((tm, D), lambda i, j: (i, 0)),
                   pl.BlockSpec((tm, D), lambda i, j: (i, 0)),
                   pl.BlockSpec((1, D), lambda i, j: (0, 0))],
        out_shape=[jax.ShapeDtypeStruct((T, D), F32), jax.ShapeDtypeStruct((T, D), BF16),
                   jax.ShapeDtypeStruct((1, D), F32)],
        scratch_shapes=[pltpu.VMEM((ns, D, guc), BF16), pltpu.VMEM((tm, D), F32), pltpu.SemaphoreType.DMA],
        args=(dgu, wgu, h, g, dres), comm=comm)


def _mix_bwd_in(parts, w_main, w_fg, h, g, dres, *, tm, scale, name, comm=None):
    T, D = h.shape
    widths = [p.shape[1] for p in parts[:-1]]
    offs = [sum(widths[:k]) for k in range(len(widths))]
    npart = len(parts)

    def body(*refs):
        p_refs = refs[:npart]
        wm_ref, wf_ref, h_ref, g_ref, dres_ref, dh_ref, dhb_ref, dg_ref = refs[npart:]

        @pl.when(pl.program_id(0) == 0)
        def _():
            dg_ref[...] = jnp.zeros_like(dg_ref)

        dn = _dot_nt(p_refs[-1][...].astype(BF16), wf_ref[...])
        for p_ref, off, wd_ in zip(p_refs[:-1], offs, widths):
            for c0, cw in _chunks(wd_):
                dn = dn + _dot_nt(p_ref[:, c0:c0 + cw].astype(BF16), wm_ref[:, off + c0:off + c0 + cw])
        dh, dgain = _rms_bwd(dn, h_ref[...], g_ref[...], dres_ref[...])
        dh_ref[...] = dh
        dhb_ref[...] = (scale * dh).astype(BF16)
        dg_ref[...] += dgain

    row = lambda i: (i, 0)
    const = lambda i: (0, 0)
    return _launch(
        body, name=name, grid=(T // tm,),
        in_specs=[pl.BlockSpec((tm, p.shape[1]), row) for p in parts]
                 + [pl.BlockSpec(w_main.shape, const), pl.BlockSpec(w_fg.shape, const),
                    pl.BlockSpec((tm, D), row), pl.BlockSpec((1, D), const), pl.BlockSpec((tm, D), row)],
        out_specs=[pl.BlockSpec((tm, D), row), pl.BlockSpec((tm, D), row), pl.BlockSpec((1, D), const)],
        out_shape=[jax.ShapeDtypeStruct((T, D), F32), jax.ShapeDtypeStruct((T, D), BF16),
                   jax.ShapeDtypeStruct((1, D), F32)],
        args=(*parts, w_main, w_fg, h, g, dres), comm=comm)


def _matmul_tn(xs, y, *, tm, nb, x_specs, y_spec, out_shape, out_spec, kb, silu, name, comm=None):
    T = y.shape[-2]
    nx = len(xs)
    chunks = _chunks(kb)

    def body(*refs):
        x_refs, y_ref, o_ref = refs[:nx], refs[nx], refs[nx + 1]
        i = pl.program_id(1)

        @pl.when(i == 0)
        def _():
            o_ref[...] = jnp.zeros_like(o_ref)

        yv = y_ref[...].astype(BF16)
        for c0, cw in chunks:
            if silu:
                xv = _silu_mul(x_refs[0][:, c0:c0 + cw].astype(F32), x_refs[1][:, c0:c0 + cw].astype(F32)).astype(BF16)
            else:
                xv = x_refs[0][:, c0:c0 + cw]
            o_ref[c0:c0 + cw, :] += _dot_tn(xv, yv)

    return _launch(
        body, name=name, grid=(nb, T // tm),
        in_specs=list(x_specs) + [y_spec], out_specs=out_spec, out_shape=out_shape,
        args=(*xs, y), comm=comm)


def _tri(n, lower):
    r = lax.broadcasted_iota(jnp.int32, (n, n), 0)
    c = lax.broadcasted_iota(jnp.int32, (n, n), 1)
    return jnp.where((r >= c) if lower else (r <= c), 1.0, 0.0).astype(BF16)


def _tri_dot(tri, v):
    hi, mid, lo = _split3(v)
    return _dot(tri, hi) + _dot(tri, mid) + _dot(tri, lo)


def _fcum(fg, bf, *, ch, name):
    B, L, W = fg.shape
    nch = L // ch

    def body(fg_ref, bf_ref, f_ref):
        tri = _tri(ch, True)
        carry = jnp.zeros((1, W), F32)
        for c in range(nch):
            x = fg_ref[c * ch:(c + 1) * ch, :] + bf_ref[...]
            lf = jnp.minimum(x, 0.0) - jnp.log(1.0 + jnp.exp(-jnp.abs(x)))
            f_ref[c * ch:(c + 1) * ch, :] = _tri_dot(tri, lf) + carry
            carry = carry + jnp.sum(lf, axis=0, keepdims=True)

    return pl.pallas_call(
        body, name=name, grid=(B,),
        in_specs=[pl.BlockSpec((None, L, W), lambda b: (b, 0, 0)), pl.BlockSpec((1, W), lambda b: (0, 0))],
        out_specs=pl.BlockSpec((None, L, W), lambda b: (b, 0, 0)),
        out_shape=jax.ShapeDtypeStruct((B, L, W), F32),
        compiler_params=_params("arbitrary"),
    )(fg, bf)


def _fcum_bwd(dF, fg, bf, *, ch, name):
    B, L, W = fg.shape
    nch = L // ch

    def body(df_ref, fg_ref, bf_ref, dfg_ref, db_ref):
        @pl.when(pl.program_id(0) == 0)
        def _():
            db_ref[...] = jnp.zeros_like(db_ref)

        tri = _tri(ch, False)
        carry = jnp.zeros((1, W), F32)
        dbs = jnp.zeros((1, W), F32)
        for c in reversed(range(nch)):
            d = df_ref[c * ch:(c + 1) * ch, :]
            dlf = _tri_dot(tri, d) + carry
            carry = carry + jnp.sum(d, axis=0, keepdims=True)
            x = fg_ref[c * ch:(c + 1) * ch, :] + bf_ref[...]
            dfg = dlf * jax.nn.sigmoid(-x)
            dfg_ref[c * ch:(c + 1) * ch, :] = dfg.astype(BF16)
            dbs = dbs + jnp.sum(dfg, axis=0, keepdims=True)
        db_ref[...] += dbs

    blk = pl.BlockSpec((None, L, W), lambda b: (b, 0, 0))
    return pl.pallas_call(
        body, name=name, grid=(B,),
        in_specs=[blk, blk, pl.BlockSpec((1, W), lambda b: (0, 0))],
        out_specs=[blk, pl.BlockSpec((1, W), lambda b: (0, 0))],
        out_shape=[jax.ShapeDtypeStruct((B, L, W), BF16), jax.ShapeDtypeStruct((1, W), F32)],
        compiler_params=_params("arbitrary"),
    )(dF, fg, bf)


def _attn_fwd(proj, fc, fr, *, tq, n_heads, name, comm=None):
    B, L, _ = proj.shape
    AD = n_heads * HEAD_DIM
    nq = L // tq
    W = fc.shape[-1]
    scale = HEAD_DIM ** -0.5

    def body(q_ref, k_ref, v_ref, fr_ref, o_ref, lse_ref, m_s, l_s, acc_s):
        qi, ki = pl.program_id(1), pl.program_id(2)

        @pl.when(ki == 0)
        def _():
            m_s[...] = jnp.full_like(m_s, NEG)
            l_s[...] = jnp.zeros_like(l_s)
            acc_s[...] = jnp.zeros_like(acc_s)

        def tile(diagonal):
            if diagonal:
                mask = (lax.broadcasted_iota(jnp.int32, (tq, tq), 1)
                        <= lax.broadcasted_iota(jnp.int32, (tq, tq), 0))
            lane = lax.broadcasted_iota(jnp.int32, (tq, W), 1)
            m_all, l_all = m_s[...], l_s[...]
            m_out, l_out = m_all, l_all

            def scores(h):
                sl = slice(h * HEAD_DIM, (h + 1) * HEAD_DIM)
                return _dot_nt(q_ref[:, sl] * scale, k_ref[:, sl])

            nxt = scores(0)
            for h in range(n_heads):
                sl = slice(h * HEAD_DIM, (h + 1) * HEAD_DIM)
                s = nxt - fr_ref[h:h + 1, :]
                if h + 1 < n_heads:
                    nxt = scores(h + 1)
                if diagonal:
                    s = jnp.where(mask, s, NEG)
                m_old = m_all[:, h:h + 1]
                m_new = jnp.maximum(m_old, jnp.max(s, axis=1, keepdims=True))
                alpha = jnp.exp(m_old - m_new)
                p = jnp.exp(s - m_new)
                l_new = alpha * l_all[:, h:h + 1] + jnp.sum(p, axis=1, keepdims=True)
                acc_s[:, sl] = alpha * acc_s[:, sl] + _dot(p.astype(BF16), v_ref[:, sl])
                m_out = jnp.where(lane == h, m_new, m_out)
                l_out = jnp.where(lane == h, l_new, l_out)
            m_s[...] = m_out
            l_s[...] = l_out

        @pl.when(ki < qi)
        def _():
            tile(False)

        @pl.when(ki == qi)
        def _():
            tile(True)
            l = l_s[...]
            for h in range(n_heads):
                sl = slice(h * HEAD_DIM, (h + 1) * HEAD_DIM)
                o_ref[:, sl] = acc_s[:, sl] / l[:, h:h + 1]
            lse_ref[...] = jnp.where(l > 0.0, m_s[...] + jnp.log(jnp.where(l > 0.0, l, 1.0)), 0.0)

    kv = lambda b, qi, ki: jnp.minimum(ki, qi)
    return _launch(
        body, name=name, grid=(B, nq, nq), args=(proj, proj, proj, fr), comm=comm,
        in_specs=[pl.BlockSpec((None, tq, AD), lambda b, qi, ki: (b, qi, 3)),
                  pl.BlockSpec((None, tq, AD), lambda b, qi, ki: (b, kv(b, qi, ki), 4)),
                  pl.BlockSpec((None, tq, AD), lambda b, qi, ki: (b, kv(b, qi, ki), 5)),
                  pl.BlockSpec((None, None, n_heads, tq), lambda b, qi, ki: (b, kv(b, qi, ki), 0, 0))],
        out_specs=[pl.BlockSpec((None, tq, AD), lambda b, qi, ki: (b, qi, 0)),
                   pl.BlockSpec((None, tq, W), lambda b, qi, ki: (b, qi, 0))],
        out_shape=[jax.ShapeDtypeStruct((B, L, AD), F32), jax.ShapeDtypeStruct((B, L, W), F32)],
        scratch_shapes=[pltpu.VMEM((tq, W), F32), pltpu.VMEM((tq, W), F32), pltpu.VMEM((tq, AD), F32)])


def _attn_bwd(proj, o, do, lse, fc, fr, *, tq, n_heads, name, comm=None):
    B, L, _ = proj.shape
    AD = n_heads * HEAD_DIM
    nq = L // tq
    W = fc.shape[-1]
    scale = HEAD_DIM ** -0.5

    def body(q_ref, k_ref, v_ref, o_ref, do_ref, lse_ref, fr_ref,
             dq_ref, dk_ref, dv_ref, dfr_ref, dfq_ref, dk_s, dv_s):
        kj, qi = pl.program_id(1), pl.program_id(2)

        @pl.when((kj == 0) & (qi == 0))
        def _():
            dq_ref[...] = jnp.zeros_like(dq_ref)
            dfq_ref[...] = jnp.zeros_like(dfq_ref)

        @pl.when(qi == kj)
        def _():
            dk_s[...] = jnp.zeros_like(dk_s)
            dv_s[...] = jnp.zeros_like(dv_s)
            dfr_ref[...] = jnp.zeros_like(dfr_ref)

        def tile(diagonal):
            if diagonal:
                mask = (lax.broadcasted_iota(jnp.int32, (tq, tq), 1)
                        <= lax.broadcasted_iota(jnp.int32, (tq, tq), 0))
            rows = pl.ds(pl.multiple_of(qi * tq, 8), tq)
            lane = lax.broadcasted_iota(jnp.int32, (tq, W), 1)
            head = lax.broadcasted_iota(jnp.int32, (n_heads, tq), 0)
            lse = lse_ref[...]
            dfq = jnp.zeros((tq, W), F32)
            dfr = jnp.zeros((n_heads, tq), F32)
            for h in range(n_heads):
                sl = slice(h * HEAD_DIM, (h + 1) * HEAD_DIM)
                k, v, dov = k_ref[:, sl], v_ref[:, sl], do_ref[:, sl]
                q = q_ref[:, sl] * scale
                s = _dot_nt(q, k) - fr_ref[h:h + 1, :]
                if diagonal:
                    s = jnp.where(mask, s, NEG)
                p = jnp.exp(s - lse[:, h:h + 1])
                dp = _dot_nt(dov, v)
                dsum = jnp.sum(dov.astype(F32) * o_ref[:, sl], axis=1, keepdims=True)
                ds = p * (dp - dsum)
                dsb = ds.astype(BF16)
                dv_s[:, sl] += _dot_tn(p.astype(BF16), dov)
                dk_s[:, sl] += _dot_tn(dsb, q)
                dq_ref[rows, sl] += _dot(dsb, k) * scale
                dfr = jnp.where(head == h, jnp.sum(ds, axis=0, keepdims=True), dfr)
                dfq = jnp.where(lane == h, jnp.sum(ds, axis=1, keepdims=True), dfq)
            dfr_ref[...] -= dfr
            dfq_ref[rows, :] += dfq

        @pl.when(qi > kj)
        def _():
            tile(False)

        @pl.when(qi == kj)
        def _():
            tile(True)

        @pl.when(qi == nq - 1)
        def _():
            dk_ref[...] = dk_s[...].astype(BF16)
            dv_ref[...] = dv_s[...].astype(BF16)

    qq = lambda b, kj, qi: jnp.maximum(qi, kj)
    qblk = lambda w, cb: pl.BlockSpec((None, tq, w), lambda b, kj, qi: (b, qq(b, kj, qi), cb))
    kblk = lambda cb: pl.BlockSpec((None, tq, AD), lambda b, kj, qi: (b, kj, cb))
    return _launch(
        body, name=name, grid=(B, nq, nq), args=(proj, proj, proj, o, do, lse, fr), comm=comm,
        in_specs=[qblk(AD, 3), kblk(4), kblk(5), qblk(AD, 0), qblk(AD, 0), qblk(W, 0),
                  pl.BlockSpec((None, None, n_heads, tq), lambda b, kj, qi: (b, kj, 0, 0))],
        out_specs=[pl.BlockSpec((None, L, AD), lambda b, kj, qi: (b, 0, 0)),
                   kblk(0), kblk(0),
                   pl.BlockSpec((None, None, n_heads, tq), lambda b, kj, qi: (b, kj, 0, 0)),
                   pl.BlockSpec((None, L, W), lambda b, kj, qi: (b, 0, 0))],
        out_shape=[jax.ShapeDtypeStruct((B, L, AD), F32), jax.ShapeDtypeStruct((B, L, AD), BF16),
                   jax.ShapeDtypeStruct((B, L, AD), BF16), jax.ShapeDtypeStruct((B, nq, n_heads, tq), F32),
                   jax.ShapeDtypeStruct((B, L, W), F32)],
        scratch_shapes=[pltpu.VMEM((tq, AD), F32), pltpu.VMEM((tq, AD), F32)])


def _mix_gather(refs, first):
    b_ref, c_ref, hc_ref, cp_ref, hcp_ref, o_ref, cw_ref, p_ref = refs
    bg = b_ref[...].astype(F32)
    u = c_ref[...].astype(F32) * hc_ref[...].astype(F32)
    prev = cp_ref[...].astype(F32) * hcp_ref[...].astype(F32)
    prev = jnp.where(first, 0.0, prev)
    cv, u1, u2 = _causal_conv(u, prev, cw_ref[...])
    yc = bg * cv
    p = p_ref[...]
    rc = lax.rsqrt(_group_mean(yc * yc, p) + EPS)
    ya = o_ref[...].astype(F32)
    ra = lax.rsqrt(_group_mean(ya * ya, p) + EPS)
    return bg, (u, u1, u2), cv, yc * rc, rc, ya * ra, ra


def _mix_specs(tm, CD, D, grid_rank_fn):
    per = tm // HALO
    cur = lambda cb: pl.BlockSpec((None, tm, CD), lambda b, i: (b, i, cb))
    prev = lambda cb: pl.BlockSpec((None, HALO, CD), lambda b, i: (b, jnp.maximum(i * per - 1, 0), cb))
    return [cur(0), cur(1), cur(2), prev(1), prev(2), cur(0)]


def _mix_out(proj, o, cw, gc, ga, wout, h, pmat, *, tm, name, comm=None):
    B, L, D = h.shape
    CD = o.shape[-1]
    const = lambda b, i: (0, 0)

    def body(b_ref, c_ref, hc_ref, cp_ref, hcp_ref, o_ref, cw_ref, p_ref, gc_ref, ga_ref, w_ref, h_ref,
             out_ref, y_ref):
        first = pl.program_id(1) == 0
        _, _, _, zc, _, za, _ = _mix_gather((b_ref, c_ref, hc_ref, cp_ref, hcp_ref, o_ref, cw_ref, p_ref), first)
        yc = (zc * gc_ref[...]).astype(BF16)
        ya = (za * ga_ref[...]).astype(BF16)
        y_ref[:, :CD] = yc
        y_ref[:, CD:] = ya
        out_ref[...] = h_ref[...] + _dot(yc, w_ref[:CD, :]) + _dot(ya, w_ref[CD:, :])

    return _launch(
        body, name=name, grid=(B, L // tm),
        in_specs=_mix_specs(tm, CD, D, None)
                 + [pl.BlockSpec(cw.shape, const), pl.BlockSpec(pmat.shape, const),
                    pl.BlockSpec((1, CD), const), pl.BlockSpec((1, CD), const), pl.BlockSpec((D, D), const),
                    pl.BlockSpec((None, tm, D), lambda b, i: (b, i, 0))],
        out_specs=[pl.BlockSpec((None, tm, D), lambda b, i: (b, i, 0)),
                   pl.BlockSpec((None, tm, D), lambda b, i: (b, i, 0))],
        out_shape=[jax.ShapeDtypeStruct((B, L, D), F32), jax.ShapeDtypeStruct((B, L, D), BF16)],
        args=(proj, proj, proj, proj, proj, o, cw, pmat, gc, ga, wout, h), comm=comm)


def _mix_out_bwd(dhb, proj, o, cw, gc, ga, wout, pmat, *, tm, name, comm=None):
    B, L, D = dhb.shape
    CD = o.shape[-1]
    const = lambda b, i: (0, 0)

    def body(dh_ref, b_ref, c_ref, hc_ref, cp_ref, hcp_ref, o_ref, cw_ref, p_ref, gc_ref, ga_ref, w_ref,
             db_ref, dcv_ref, do_ref, dgc_ref, dga_ref, dcw_ref):
        first = pl.program_id(1) == 0

        @pl.when((pl.program_id(0) == 0) & first)
        def _():
            dgc_ref[...] = jnp.zeros_like(dgc_ref)
            dga_ref[...] = jnp.zeros_like(dga_ref)
            dcw_ref[...] = jnp.zeros_like(dcw_ref)

        bg, us, cv, zc, rc, za, ra = _mix_gather(
            (b_ref, c_ref, hc_ref, cp_ref, hcp_ref, o_ref, cw_ref, p_ref), first)
        p = p_ref[...]
        dh = dh_ref[...]
        dyc = _dot_nt(dh, w_ref[:CD, :])
        dya = _dot_nt(dh, w_ref[CD:, :])

        dgc_ref[...] += jnp.sum(dyc * zc, axis=0, keepdims=True)
        dz = dyc * gc_ref[...]
        dx = rc * (dz - zc * _group_mean(dz * zc, p))
        db_ref[...] = (dx * cv).astype(BF16)
        dcv = dx * bg
        dcv_ref[...] = dcv.astype(BF16)
        for k in range(3):
            dcw_ref[k:k + 1, :] += jnp.sum(dcv * us[2 - k], axis=0, keepdims=True)

        dga_ref[...] += jnp.sum(dya * za, axis=0, keepdims=True)
        dz = dya * ga_ref[...]
        do_ref[...] = (ra * (dz - za * _group_mean(dz * za, p))).astype(BF16)

    tile = lambda w: pl.BlockSpec((None, tm, w), lambda b, i: (b, i, 0))
    return _launch(
        body, name=name, grid=(B, L // tm), comm=comm,
        args=(dhb, proj, proj, proj, proj, proj, o, cw, pmat, gc, ga, wout),
        in_specs=[tile(D)] + _mix_specs(tm, CD, D, None)
                 + [pl.BlockSpec(cw.shape, const), pl.BlockSpec(pmat.shape, const),
                    pl.BlockSpec((1, CD), const), pl.BlockSpec((1, CD), const), pl.BlockSpec((D, D), const)],
        out_specs=[tile(CD), tile(CD), tile(CD),
                   pl.BlockSpec((1, CD), const), pl.BlockSpec((1, CD), const), pl.BlockSpec((8, CD), const)],
        out_shape=[jax.ShapeDtypeStruct((B, L, CD), BF16)] * 3
                  + [jax.ShapeDtypeStruct((1, CD), F32)] * 2 + [jax.ShapeDtypeStruct((8, CD), F32)])


def _conv_bwd(dcv, proj, cw, *, tm, name):
    B, L, CD = dcv.shape
    per = tm // HALO
    nhalo = L // HALO
    nt = L // tm

    def body(d_ref, dn_ref, c_ref, hc_ref, cw_ref, out_ref):
        last = pl.program_id(1) == nt - 1
        d = d_ref[...].astype(F32)
        nxt = jnp.where(last, 0.0, dn_ref[...].astype(F32))
        n0, n1 = _row_of(nxt, 0), _row_of(nxt, 1)
        rows = lax.broadcasted_iota(jnp.int32, d.shape, 0)
        d1 = jnp.where(rows == tm - 1, n0, pltpu.roll(d, tm - 1, 0))
        d2 = jnp.where(rows == tm - 2, n0, jnp.where(rows == tm - 1, n1, pltpu.roll(d, tm - 2, 0)))
        w = cw_ref[...]
        du = w[2:3, :] * d + w[1:2, :] * d1 + w[0:1, :] * d2
        out_ref[:, :CD] = (du * hc_ref[...].astype(F32)).astype(BF16)
        out_ref[:, CD:] = (du * c_ref[...].astype(F32)).astype(BF16)

    return pl.pallas_call(
        body, name=name, grid=(B, nt),
        in_specs=[pl.BlockSpec((None, tm, CD), lambda b, i: (b, i, 0)),
                  pl.BlockSpec((None, HALO, CD), lambda b, i: (b, jnp.minimum((i + 1) * per, nhalo - 1), 0)),
                  pl.BlockSpec((None, tm, CD), lambda b, i: (b, i, 1)),
                  pl.BlockSpec((None, tm, CD), lambda b, i: (b, i, 2)),
                  pl.BlockSpec(cw.shape, lambda b, i: (0, 0))],
        out_specs=pl.BlockSpec((None, tm, 2 * CD), lambda b, i: (b, i, 0)),
        out_shape=jax.ShapeDtypeStruct((B, L, 2 * CD), BF16),
        compiler_params=_params("arbitrary", "arbitrary"),
    )(dcv, dcv, proj, proj, cw)


def _final(h, gf, tgt, *, tm, name):
    B, L, D = h.shape

    def body(h_ref, g_ref, t_ref, dh_ref, dhb_ref, dg_ref, loss_ref):
        b, i = pl.program_id(0), pl.program_id(1)

        @pl.when((b == 0) & (i == 0))
        def _():
            dg_ref[...] = jnp.zeros_like(dg_ref)
            loss_ref[...] = jnp.zeros_like(loss_ref)

        x = h_ref[...]
        g = g_ref[...]
        r = lax.rsqrt(jnp.mean(x * x, axis=-1, keepdims=True) + EPS)
        y = x * r
        pos = i * tm + lax.broadcasted_iota(jnp.int32, (tm, 1), 0)
        err = jnp.where(pos >= N_META, y * g - t_ref[...], 0.0)
        loss_ref[...] += 0.5 * jnp.sum(jnp.mean(err * err, axis=-1, keepdims=True))
        dout = err / D
        dg_ref[...] += jnp.sum(dout * y, axis=0, keepdims=True)
        dy = dout * g
        dh = r * (dy - y * jnp.mean(dy * y, axis=-1, keepdims=True))
        dh_ref[...] = dh
        dhb_ref[...] = (0.5 * dh).astype(BF16)

    tile = pl.BlockSpec((None, tm, D), lambda b, i: (b, i, 0))
    const = lambda b, i: (0, 0)
    return pl.pallas_call(
        body, name=name, grid=(B, L // tm),
        in_specs=[tile, pl.BlockSpec((1, D), const), tile],
        out_specs=[tile, tile, pl.BlockSpec((1, D), const), pl.BlockSpec((1, LANES), const)],
        out_shape=[jax.ShapeDtypeStruct((B, L, D), F32), jax.ShapeDtypeStruct((B, L, D), BF16),
                   jax.ShapeDtypeStruct((1, D), F32), jax.ShapeDtypeStruct((1, LANES), F32)],
        compiler_params=_params("arbitrary", "arbitrary"),
    )(h, gf, tgt)


def _place():
    x, y, c = lax.axis_index("x"), lax.axis_index("y"), lax.axis_index("c")
    others = [(1 - x, y), (x, 1 - y), (1 - x, 1 - y)]
    return x, y, c, others


def _all_gather_shards(shards, *, name):
    n = len(shards)

    def body(*refs):
        ins, outs = refs[:n], refs[n:2 * n]
        send, recv, fsend, frecv, lsem = refs[2 * n:]
        x, y, c, others = _place()
        me = 2 * x + y
        local = [pltpu.make_async_copy(ins[t], outs[t].at[me], lsem.at[t]) for t in range(n)]
        for cp in local:
            cp.start()

        def half(t, k):
            hr = shards[t].shape[0] // 2
            return pl.ds(pl.multiple_of(k * hr, HALO), hr)

        def ici(t, j, src_chip, to):
            src = ins[t].at[half(t, c)] if to is not None else outs[t].at[src_chip, half(t, c)]
            return pltpu.make_async_remote_copy(
                src_ref=src, dst_ref=outs[t].at[src_chip, half(t, c)],
                send_sem=send.at[3 * t + j], recv_sem=recv.at[3 * t + j],
                device_id=(x, y, c) if to is None else to, device_id_type=MESH)

        def d2d(t, j, src_chip, k):
            return pltpu.make_async_remote_copy(
                src_ref=outs[t].at[src_chip, half(t, k)], dst_ref=outs[t].at[src_chip, half(t, k)],
                send_sem=fsend.at[3 * t + j], recv_sem=frecv.at[3 * t + j],
                device_id=(x, y, 1 - c), device_id_type=MESH)

        firsts = [ici(t, j, me, (ox, oy, c)) for t in range(n) for j, (ox, oy) in enumerate(others)]
        for cp in firsts:
            cp.start()
        passed = []
        for t in range(n):
            for j, (ox, oy) in enumerate(others):
                ici(t, j, 2 * ox + oy, None).wait_recv()
                cp = d2d(t, j, 2 * ox + oy, c)
                cp.start()
                passed.append(cp)
        for t in range(n):
            for j, (ox, oy) in enumerate(others):
                d2d(t, j, 2 * ox + oy, 1 - c).wait_recv()
        for cp in firsts + passed:
            cp.wait_send()
        for cp in local:
            cp.wait()

    return pl.pallas_call(
        body, name=name,
        in_specs=[ANY] * n, out_specs=[ANY] * n,
        out_shape=[jax.ShapeDtypeStruct((N_SHARD,) + s.shape, s.dtype) for s in shards],
        scratch_shapes=[pltpu.SemaphoreType.DMA((3 * n,))] * 4 + [pltpu.SemaphoreType.DMA((n,))],
    )(*shards)


def _all_reduce_small(slab, *, name):
    def body(in_ref, out_ref, gath, send, recv):
        x, y, c, _ = _place()
        me = 4 * x + 2 * y + c
        gath[me] = in_ref[...]
        copies, peers = [], []
        for m in range(1, N_DEV):
            px = jnp.where((m >> 2) & 1, 1 - x, x)
            py = jnp.where((m >> 1) & 1, 1 - y, y)
            pc = jnp.where(m & 1, 1 - c, c)
            cp = pltpu.make_async_remote_copy(
                src_ref=in_ref, dst_ref=gath.at[me], send_sem=send.at[m - 1], recv_sem=recv.at[m - 1],
                device_id=(px, py, pc), device_id_type=MESH)
            cp.start()
            copies.append(cp)
            peers.append(4 * px + 2 * py + pc)
        for m in range(1, N_DEV):
            pltpu.make_async_remote_copy(
                src_ref=in_ref, dst_ref=gath.at[peers[m - 1]], send_sem=send.at[m - 1], recv_sem=recv.at[m - 1],
                device_id=(x, y, c), device_id_type=MESH).wait_recv()
        for cp in copies:
            cp.wait_send()
        acc = gath[0]
        for k in range(1, N_DEV):
            acc = acc + gath[k]
        out_ref[...] = acc

    vm = pl.BlockSpec(memory_space=pltpu.VMEM)
    return pl.pallas_call(
        body, name=name, in_specs=[vm], out_specs=vm,
        out_shape=jax.ShapeDtypeStruct(slab.shape, slab.dtype),
        scratch_shapes=[pltpu.VMEM((N_DEV,) + slab.shape, slab.dtype),
                        pltpu.SemaphoreType.DMA((N_DEV - 1,)), pltpu.SemaphoreType.DMA((N_DEV - 1,))],
    )(slab)


def _gather_ici(shards):
    n = len(shards)

    def copies(ins, outs, sems, sending):
        send, recv, _ = sems
        x, y, c, others = _place()
        me = 2 * x + y
        out = []
        for t in range(n):
            hr = shards[t].shape[0] // 2
            rows = pl.ds(pl.multiple_of(c * hr, HALO), hr)
            for j, (ox, oy) in enumerate(others):
                src_chip = me if sending else 2 * ox + oy
                out.append(pltpu.make_async_remote_copy(
                    src_ref=ins[t].at[rows], dst_ref=outs[t].at[src_chip, rows],
                    send_sem=send.at[3 * t + j], recv_sem=recv.at[3 * t + j],
                    device_id=(ox, oy, c) if sending else (x, y, c), device_id_type=MESH))
        return out

    def local(ins, outs, sems):
        x, y, _, _ = _place()
        return [pltpu.make_async_copy(ins[t], outs[t].at[2 * x + y], sems[2].at[t]) for t in range(n)]

    def start(ins, outs, sems):
        for cp in local(ins, outs, sems) + copies(ins, outs, sems, True):
            cp.start()

    def finish(ins, outs, sems):
        for cp in copies(ins, outs, sems, False):
            cp.wait_recv()
        for cp in copies(ins, outs, sems, True):
            cp.wait_send()
        for cp in local(ins, outs, sems):
            cp.wait()

    return _Comm(shards, [jax.ShapeDtypeStruct((N_SHARD,) + s.shape, s.dtype) for s in shards],
                 [3 * n, 3 * n, n], start, finish)


def _gather_d2d(parts):
    n = len(parts)

    def copies(outs, sems, sending):
        send, recv = sems
        x, y, c, others = _place()
        out = []
        for t in range(n):
            hr = parts[t].shape[1] // 2
            rows = pl.ds(pl.multiple_of((c if sending else 1 - c) * hr, HALO), hr)
            for j, (ox, oy) in enumerate(others):
                blk = outs[t].at[2 * ox + oy, rows]
                out.append(pltpu.make_async_remote_copy(
                    src_ref=blk, dst_ref=blk, send_sem=send.at[3 * t + j], recv_sem=recv.at[3 * t + j],
                    device_id=(x, y, 1 - c) if sending else (x, y, c), device_id_type=MESH))
        return out

    def start(ins, outs, sems):
        for cp in copies(outs, sems, True):
            cp.start()

    def finish(ins, outs, sems):
        for cp in copies(outs, sems, False):
            cp.wait_recv()
        for cp in copies(outs, sems, True):
            cp.wait_send()

    return _Comm(parts, [jax.ShapeDtypeStruct(p.shape, p.dtype) for p in parts], [3 * n, 3 * n], start, finish,
                 aliases={t: t for t in range(n)})


def _swap_halves(grads):
    n = len(grads)

    def copies(ins, outs, sems):
        x, y, c, _ = _place()
        out = []
        for t in range(n):
            hr = grads[t].shape[1] // 2
            rows = pl.ds(pl.multiple_of((1 - c) * hr, 8), hr)
            out.append(pltpu.make_async_remote_copy(
                src_ref=ins[t].at[:, rows, :], dst_ref=outs[t], send_sem=sems[0].at[t], recv_sem=sems[1].at[t],
                device_id=(x, y, 1 - c), device_id_type=MESH))
        return out

    def start(ins, outs, sems):
        for cp in copies(ins, outs, sems):
            cp.start()

    def finish(ins, outs, sems):
        for cp in copies(ins, outs, sems):
            cp.wait()

    return _Comm(grads, [jax.ShapeDtypeStruct((N_SHARD, g.shape[1] // 2, g.shape[2]), g.dtype) for g in grads],
                 [n, n], start, finish)


def _pair_sum(g, got, c, *, name):
    ns, R, C = g.shape
    hr = R // 2

    def body(c_ref, g_ref, r_ref, o_ref):
        o_ref[...] = (g_ref[...] + r_ref[...]).astype(BF16)

    return pl.pallas_call(
        body, name=name,
        grid_spec=pltpu.PrefetchScalarGridSpec(
            num_scalar_prefetch=1, grid=(ns,),
            in_specs=[pl.BlockSpec((None, hr, C), lambda s, cr: (s, cr[0], 0)),
                      pl.BlockSpec((None, hr, C), lambda s, cr: (s, 0, 0))],
            out_specs=pl.BlockSpec((None, hr, C), lambda s, cr: (s, 0, 0))),
        out_shape=jax.ShapeDtypeStruct((ns, hr, C), BF16),
        compiler_params=_params("arbitrary"),
    )(c, g, got)


def _scatter_chips(sums):
    n = len(sums)

    def copies(ins, outs, sems, sending):
        x, y, c, others = _place()
        me = 2 * x + y
        out = []
        for t in range(n):
            for j, (ox, oy) in enumerate(others):
                there = 2 * ox + oy
                out.append(pltpu.make_async_remote_copy(
                    src_ref=ins[t].at[there if sending else me], dst_ref=outs[t].at[me if sending else there],
                    send_sem=sems[0].at[3 * t + j], recv_sem=sems[1].at[3 * t + j],
                    device_id=(ox, oy, c) if sending else (x, y, c), device_id_type=MESH))
        return out

    def start(ins, outs, sems):
        for cp in copies(ins, outs, sems, True):
            cp.start()

    def finish(ins, outs, sems):
        for cp in copies(ins, outs, sems, False):
            cp.wait_recv()
        for cp in copies(ins, outs, sems, True):
            cp.wait_send()

    return _Comm(sums, [jax.ShapeDtypeStruct(s.shape, s.dtype) for s in sums], [3 * n, 3 * n], start, finish)


def _chip_sum(g, got, landed, idx, *, name):
    ns, R, C = g.shape
    hr = R // 2

    def body(i_ref, g_ref, r_ref, a_ref, b_ref, c_ref, o_ref):
        acc = g_ref[...] + r_ref[...]
        for ref in (a_ref, b_ref, c_ref):
            acc = acc + ref[...].astype(F32)
        o_ref[...] = acc

    other = lambda k: pl.BlockSpec((None, hr, C), lambda s, ir: (ir[2 + k], 0, 0))
    return pl.pallas_call(
        body, name=name,
        grid_spec=pltpu.PrefetchScalarGridSpec(
            num_scalar_prefetch=1, grid=(1,),
            in_specs=[pl.BlockSpec((None, hr, C), lambda s, ir: (ir[0], ir[1], 0)),
                      pl.BlockSpec((None, hr, C), lambda s, ir: (ir[0], 0, 0)),
                      other(0), other(1), other(2)],
            out_specs=pl.BlockSpec((hr, C), lambda s, ir: (ir[1], 0))),
        out_shape=jax.ShapeDtypeStruct((R, C), F32),
        compiler_params=_params("arbitrary"),
    )(idx, g, got, landed, landed, landed)


def _share_halves(halves):
    n = len(halves)

    def copies(outs, sems, sending):
        x, y, c, _ = _place()
        out = []
        for t in range(n):
            hr = halves[t].shape[0] // 2
            rows = pl.ds(pl.multiple_of((c if sending else 1 - c) * hr, 8), hr)
            out.append(pltpu.make_async_remote_copy(
                src_ref=outs[t].at[rows, :], dst_ref=outs[t].at[rows, :], send_sem=sems[0].at[t],
                recv_sem=sems[1].at[t], device_id=(x, y, 1 - c) if sending else (x, y, c), device_id_type=MESH))
        return out

    def start(ins, outs, sems):
        for cp in copies(outs, sems, True):
            cp.start()

    def finish(ins, outs, sems):
        for cp in copies(outs, sems, False):
            cp.wait_recv()
        for cp in copies(outs, sems, True):
            cp.wait_send()

    return _Comm(halves, [jax.ShapeDtypeStruct(h.shape, h.dtype) for h in halves], [n, n], start, finish,
                 aliases={t: t for t in range(n)})


def _adamw(w, g, m, v, *, name):
    R, C = w.shape
    tr = R
    for cand in (256, 128, 64, 32, 16, 8):
        if R % cand == 0:
            tr = cand
            break

    def body(w_ref, g_ref, m_ref, v_ref, d_ref, mo_ref, vo_ref):
        gv = g_ref[...]
        mn = ADAM_B1 * m_ref[...] + (1.0 - ADAM_B1) * gv
        vn = ADAM_B2 * v_ref[...] + (1.0 - ADAM_B2) * (gv * gv)
        m_hat = mn / (1.0 - ADAM_B1 ** ADAM_STEP)
        v_hat = vn / (1.0 - ADAM_B2 ** ADAM_STEP)
        d_ref[...] = -ADAM_LR * (m_hat / (jnp.sqrt(v_hat) + ADAM_EPS) + ADAM_WD * w_ref[...])
        mo_ref[...] = mn
        vo_ref[...] = vn

    blk = pl.BlockSpec((tr, C), lambda i: (i, 0))
    return pl.pallas_call(
        body, name=name, grid=(R // tr,), in_specs=[blk] * 4, out_specs=[blk] * 3,
        out_shape=[jax.ShapeDtypeStruct((R, C), F32)] * 3,
        compiler_params=_params("arbitrary"),
    )(w, g, m, v)


def _pack_small(D, meta, n1, nm, n3, nf, gc, ga, bf, cw):
    def row(a):
        a = a.reshape(-1, a.shape[-1])
        return jnp.pad(a, ((0, 0), (0, D - a.shape[-1])))
    rows = [row(meta), row(n1), row(nm), row(n3), row(nf), row(jnp.concatenate([gc, ga], axis=-1)), row(bf), row(cw)]
    slab = jnp.concatenate(rows, axis=0)
    return jnp.pad(slab, ((0, SMALL_ROWS - slab.shape[0]), (0, 0)))


def _unpack_small(slab, like):
    meta, n1, nm, n3, nf, gc, ga, bf, cw = like
    nmeta, mc = meta.shape
    out = [slab[:nmeta, :mc].reshape(meta.shape)]
    r = nmeta
    for a in (n1, nm, n3, nf):
        out.append(slab[r, :a.shape[-1]].reshape(a.shape))
        r += 1
    cd = gc.shape[-1]
    out.append(slab[r, :cd].reshape(gc.shape))
    out.append(slab[r, cd:cd + ga.shape[-1]].reshape(ga.shape))
    r += 1
    out.append(slab[r, :bf.shape[-1]].reshape(bf.shape))
    r += 1
    out.append(slab[r:r + 3, :cw.shape[-1]].reshape(cw.shape))
    return out


def kernel(x, meta_tokens, ffn1_norm, ffn1_w_gu, ffn1_w_down, mix_norm, w_in, conv_w, b_f, out_norm_conv, out_norm_attn, w_out, ffn2_norm, ffn2_w_gu, ffn2_w_down, final_norm, loss_target, m_meta_tokens, m_ffn1_norm, m_ffn1_w_gu, m_ffn1_w_down, m_mix_norm, m_w_in, m_conv_w, m_b_f, m_out_norm_conv, m_out_norm_attn, m_w_out, m_ffn2_norm, m_ffn2_w_gu, m_ffn2_w_down, m_final_norm, v_meta_tokens, v_ffn1_norm, v_ffn1_w_gu, v_ffn1_w_down, v_mix_norm, v_w_in, v_conv_w, v_b_f, v_out_norm_conv, v_out_norm_attn, v_w_out, v_ffn2_norm, v_ffn2_w_gu, v_ffn2_w_down, v_final_norm):
    B, S, D = x.shape
    L = S + N_META
    T = B * L
    tm = L // 3
    assert tm * 3 == L and tm % HALO == 0
    guc = ffn1_w_gu.shape[-1]
    ff = N_SHARD * guc // 2
    H = b_f.shape[-1]
    AD = H * HEAD_DIM
    CD = conv_w.shape[-1] * N_SHARD
    assert CD == AD and CD + AD == D and CD % LANES == 0
    n_main = 3 * CD + 3 * AD
    ins = w_in.shape[-1]

    xi, yi, ci = lax.axis_index("x"), lax.axis_index("y"), lax.axis_index("c")
    chip = 2 * xi + yi

    small_shard = jnp.zeros((2 * HALO, meta_tokens.shape[-1]), F32)
    small_shard = small_shard.at[:N_META].set(meta_tokens)
    small_shard = small_shard.at[N_META:N_META + 3, :conv_w.shape[-1]].set(conv_w[0])
    big = [ffn1_w_gu[0], ffn1_w_down[0], w_in[0], w_out[0], ffn2_w_gu[0], ffn2_w_down[0]]
    wgu1_s, wd1_s, win_s, wout_s, wgu2_s, wd2_s = [w.astype(BF16) for w in big]
    wgu1, wd1, small_g = _all_gather_shards([wgu1_s, wd1_s, small_shard], name="gather_ffn1")
    wd1 = wd1.reshape(ff, D)
    meta_f = jnp.moveaxis(small_g[:, :N_META], 0, 1).reshape(N_META, D)
    cw_f = jnp.moveaxis(small_g[:, N_META:N_META + 3, :conv_w.shape[-1]], 0, 1).reshape(3, CD)
    cw8 = jnp.pad(cw_f, ((0, 5), (0, 0)))
    bf_p = jnp.pad(b_f, ((0, 0), (0, LANES - H)))
    gid = jnp.arange(CD) // HEAD_DIM
    pmat = jnp.where(gid[:, None] == gid[None, :], 1.0 / HEAD_DIM, 0.0).astype(BF16)

    gu_shape = jax.ShapeDtypeStruct((2, T, ff), BF16)
    gu_w_spec = pl.BlockSpec((None, D, guc), lambda s, i: (s, 0, 0))
    gu_o_spec = pl.BlockSpec((None, tm, guc), lambda s, i: (s // 2, i, s % 2))

    h0 = jnp.concatenate([jnp.broadcast_to(meta_f[None], (B, N_META, D)), x], axis=1).reshape(T, D)
    n1 = _rmsnorm(h0, ffn1_norm, tm=tm, name="ffn1_norm")
    gu1, mix_w = _matmul_nn(n1, wgu1, tm=tm, nb=N_SHARD, w_spec=gu_w_spec, out_shape=gu_shape, out_spec=gu_o_spec,
                            name="ffn1_up", comm=_gather_ici([win_s, wout_s]))
    h1, (win_g, wout_g) = _ffn_down(gu1, wd1, h0, tm=tm, name="ffn1_down", comm=_gather_d2d(mix_w))
    wout_f = wout_g.reshape(D, D)
    win_f = jnp.moveaxis(win_g, 0, 1).reshape(D, N_SHARD * ins)
    win_main = win_f[:, :n_main]
    win_fg = jnp.pad(win_f[:, n_main:], ((0, 0), (0, LANES - H)))

    n2 = _rmsnorm(h1, mix_norm, tm=tm, name="mix_norm")
    proj, _ = _matmul_nn(n2, win_main, tm=tm, nb=n_main // CD,
                         w_spec=pl.BlockSpec((D, CD), lambda s, i: (0, s)),
                         out_shape=jax.ShapeDtypeStruct((T, n_main), BF16),
                         out_spec=pl.BlockSpec((tm, CD), lambda s, i: (i, s)), name="mix_in")
    fg, _ = _matmul_nn(n2, win_fg, tm=tm, nb=1, w_spec=pl.BlockSpec((D, LANES), lambda s, i: (0, 0)),
                       out_shape=jax.ShapeDtypeStruct((T, LANES), F32),
                       out_spec=pl.BlockSpec((tm, LANES), lambda s, i: (i, 0)), name="mix_in_fg")
    proj3 = proj.reshape(B, L, n_main)
    fg3 = fg.reshape(B, L, LANES)
    fc = _fcum(fg3, bf_p, ch=tm, name="forget_cumsum")
    fr = fc[:, :, :H].reshape(B, L // tm, tm, H).transpose(0, 1, 3, 2)
    (o, lse), ffn2_w = _attn_fwd(proj3, fc, fr, tq=tm, n_heads=H, name="attn_fwd",
                                 comm=_gather_ici([wgu2_s, wd2_s]))
    (h2, ymix), (wgu2, wd2) = _mix_out(proj3, o, cw8, out_norm_conv, out_norm_attn, wout_f, h1.reshape(B, L, D), pmat,
                                       tm=tm, name="mix_out", comm=_gather_d2d(ffn2_w))
    wd2 = wd2.reshape(ff, D)
    h2 = h2.reshape(T, D)

    n3 = _rmsnorm(h2, ffn2_norm, tm=tm, name="ffn2_norm")
    gu2, _ = _matmul_nn(n3, wgu2, tm=tm, nb=N_SHARD, w_spec=gu_w_spec, out_shape=gu_shape, out_spec=gu_o_spec,
                        name="ffn2_up")
    h3, _ = _ffn_down(gu2, wd2, h2, tm=tm, name="ffn2_down")

    tgt = jnp.pad(loss_target, ((0, 0), (N_META, 0), (0, 0)))
    dh3, dh3b, d_gf, loss_part = _final(h3.reshape(B, L, D), final_norm.reshape(1, D), tgt, tm=tm, name="final")

    c_arr = jnp.reshape(ci, (1,)).astype(jnp.int32)
    ks = jnp.arange(N_SHARD - 1, dtype=jnp.int32)
    idx = jnp.concatenate([jnp.stack([chip, ci]).astype(jnp.int32), ks + (ks >= chip).astype(jnp.int32)])

    def pair_sums(grads, got, names):
        return [_pair_sum(g, r, c_arr, name="pair_sum_" + nm) for g, r, nm in zip(grads, got, names)]

    def chip_sums(grads, got, landed, names):
        return [_chip_sum(g, r, l, idx, name="chip_sum_" + nm) for g, r, l, nm in zip(grads, got, landed, names)]

    def dw_down(gu, dhb, name, comm=None):
        return _matmul_tn(
            [gu, gu], dhb, tm=tm, nb=ff // guc, kb=guc, silu=True,
            x_specs=[pl.BlockSpec((None, tm, guc), lambda j, i: (0, i, j)),
                     pl.BlockSpec((None, tm, guc), lambda j, i: (1, i, j))],
            y_spec=pl.BlockSpec((tm, D), lambda j, i: (i, 0)),
            out_shape=jax.ShapeDtypeStruct((ff, D), F32), out_spec=pl.BlockSpec((guc, D), lambda j, i: (j, 0)),
            name=name, comm=comm)

    def dw_up(n, dgu, name, comm=None):
        return _matmul_tn(
            [n], dgu, tm=tm, nb=N_SHARD, kb=D, silu=False,
            x_specs=[pl.BlockSpec((tm, D), lambda s, i: (i, 0))],
            y_spec=pl.BlockSpec((None, tm, guc), lambda s, i: (s // 2, i, s % 2)),
            out_shape=jax.ShapeDtypeStruct((N_SHARD, D, guc), F32),
            out_spec=pl.BlockSpec((None, D, guc), lambda s, i: (s, 0, 0)), name=name, comm=comm)

    dh3f, dh3b = dh3.reshape(T, D), dh3b.reshape(T, D)
    dgu2, _ = _ffn_bwd_act(dh3b, gu2, wd2, tm=tm, guc=guc, name="ffn2_bwd_act")
    (dh2, dh2b, d_g3), _ = _ffn_bwd_in(dgu2, wgu2, h2, ffn2_norm, dh3f, tm=tm, scale=1.0, name="ffn2_bwd_in")
    d_wd2, _ = dw_down(gu2, dh3b, "ffn2_dw_down")
    d_wgu2, _ = dw_up(n3, dgu2, "ffn2_dw_up")
    grads_f2 = [d_wgu2, d_wd2.reshape(N_SHARD, ff // N_SHARD, D)]
    names_f2 = ["wgu2", "wd2"]

    dh2b3 = dh2b.reshape(B, L, D)
    (d_bg, d_cv, d_o, d_gc, d_ga, d_cw), got_f2 = _mix_out_bwd(
        dh2b3, proj3, o, cw8, out_norm_conv, out_norm_attn, wout_f, pmat, tm=tm, name="mix_out_bwd",
        comm=_swap_halves(grads_f2))
    sums_f2 = pair_sums(grads_f2, got_f2, names_f2)
    d_wout, _ = _matmul_tn(
        [ymix.reshape(T, D)], dh2b, tm=tm, nb=1, kb=D, silu=False,
        x_specs=[pl.BlockSpec((tm, D), lambda s, i: (i, 0))], y_spec=pl.BlockSpec((tm, D), lambda s, i: (i, 0)),
        out_shape=jax.ShapeDtypeStruct((D, D), F32), out_spec=pl.BlockSpec((D, D), lambda s, i: (0, 0)),
        name="dw_out")
    d_cc = _conv_bwd(d_cv, proj3, cw8, tm=tm, name="conv_bwd")
    (d_q, d_k, d_v, d_fr, d_fq), landed_f2 = _attn_bwd(proj3, o, d_o, lse, fc, fr, tq=tm, n_heads=H, name="attn_bwd",
                                                       comm=_scatter_chips(sums_f2))
    halves_f2 = chip_sums(grads_f2, got_f2, landed_f2, names_f2)
    d_fc = d_fq + jnp.pad(d_fr.transpose(0, 1, 3, 2).reshape(B, L, H), ((0, 0), (0, 0), (0, LANES - H)))
    d_fg, d_bf = _fcum_bwd(d_fc, fg3, bf_p, ch=tm, name="forget_cumsum_bwd")

    parts = [d_bg.reshape(T, CD), d_cc.reshape(T, 2 * CD), d_q.reshape(T, AD), d_k.reshape(T, AD),
             d_v.reshape(T, AD), d_fg.reshape(T, LANES)]
    (dh1, dh1b, d_gm), g_f2 = _mix_bwd_in(parts, win_main, win_fg, h1, mix_norm, dh2, tm=tm, scale=0.5,
                                          name="mix_bwd_in", comm=_share_halves(halves_f2))
    d_win_parts = []
    for k, p in enumerate(parts):
        wdt = p.shape[1]
        nb = max(wdt // CD, 1)
        bw = wdt // nb
        d_win_parts.append(_matmul_tn(
            [n2], p, tm=tm, nb=nb, kb=D, silu=False,
            x_specs=[pl.BlockSpec((tm, D), lambda s, i: (i, 0))], y_spec=pl.BlockSpec((tm, bw), lambda s, i: (i, s)),
            out_shape=jax.ShapeDtypeStruct((D, wdt), F32), out_spec=pl.BlockSpec((D, bw), lambda s, i: (0, s)),
            name="dw_in_%d" % k)[0])
    d_win_parts[-1] = d_win_parts[-1][:, :H]
    d_win = jnp.moveaxis(jnp.concatenate(d_win_parts, axis=1).reshape(D, N_SHARD, ins), 1, 0)
    grads_mx = [d_win, d_wout.reshape(N_SHARD, D // N_SHARD, D)]
    names_mx = ["win", "wout"]

    dgu1, got_mx = _ffn_bwd_act(dh1b, gu1, wd1, tm=tm, guc=guc, name="ffn1_bwd_act", comm=_swap_halves(grads_mx))
    sums_mx = pair_sums(grads_mx, got_mx, names_mx)
    d_wd1, landed_mx = dw_down(gu1, dh1b, "ffn1_dw_down", comm=_scatter_chips(sums_mx))
    halves_mx = chip_sums(grads_mx, got_mx, landed_mx, names_mx)
    grads_d1 = [d_wd1.reshape(N_SHARD, ff // N_SHARD, D)]
    d_wgu1, out = dw_up(n1, dgu1, "ffn1_dw_up", comm=_join(_share_halves(halves_mx), _swap_halves(grads_d1)))
    g_mx, got_d1 = out[:2], out[2:]
    sums_d1 = pair_sums(grads_d1, got_d1, ["wd1"])
    grads_u1 = [d_wgu1]
    (dh0, _, d_g1), out = _ffn_bwd_in(dgu1, wgu1, h0, ffn1_norm, dh1, tm=tm, scale=1.0, name="ffn1_bwd_in",
                                      comm=_join(_scatter_chips(sums_d1), _swap_halves(grads_u1)))
    landed_d1, got_u1 = out[:1], out[1:]
    halves_d1 = chip_sums(grads_d1, got_d1, landed_d1, ["wd1"])
    sums_u1 = pair_sums(grads_u1, got_u1, ["wgu1"])
    out = _run_comm(_join(_share_halves(halves_d1), _scatter_chips(sums_u1)), name="scatter_ffn1")
    g_d1, landed_u1 = out[:1], out[1:]
    halves_u1 = chip_sums(grads_u1, got_u1, landed_u1, ["wgu1"])
    g_u1 = _run_comm(_share_halves(halves_u1), name="share_ffn1")
    g_big = [g_u1[0], g_d1[0], g_mx[0], g_mx[1], g_f2[0], g_f2[1]]
    dh0 = dh0.reshape(B, L, D)
    grad_x = dh0[:, N_META:]
    d_meta = jnp.sum(dh0[:, :N_META], axis=0)

    loss_row = jnp.zeros((1, D), F32).at[0, 0].set(loss_part[0, 0])
    slab = _pack_small(D, d_meta, d_g1, d_gm, d_g3, d_gf, d_gc, d_ga, d_bf[:, :H], d_cw[:3])
    slab = slab.at[SMALL_ROWS - 1].set(loss_row[0])
    total = _all_reduce_small(slab, name="reduce_small")
    loss = total[SMALL_ROWS - 1, 0]
    mcols = meta_tokens.shape[-1]
    ccols = conv_w.shape[-1]
    full_like = (jnp.zeros((N_META, D)), ffn1_norm, mix_norm, ffn2_norm, final_norm.reshape(1, D), out_norm_conv,
                 out_norm_attn, b_f, jnp.zeros((1, 3, CD)))
    g_small = _unpack_small(total, full_like)
    g_small[0] = lax.dynamic_slice_in_dim(g_small[0], chip * mcols, mcols, axis=1)
    g_small[8] = lax.dynamic_slice_in_dim(g_small[8], chip * ccols, ccols, axis=2)

    def small_slab(meta, a1, am, a3, af, gc, ga, bf, cw):
        return _pack_small(D, meta, a1, am, a3, af.reshape(1, D), gc, ga, bf, cw[0])

    w_small = small_slab(meta_tokens, ffn1_norm, mix_norm, ffn2_norm, final_norm, out_norm_conv, out_norm_attn, b_f, conv_w)
    m_small = small_slab(m_meta_tokens, m_ffn1_norm, m_mix_norm, m_ffn2_norm, m_final_norm, m_out_norm_conv,
                         m_out_norm_attn, m_b_f, m_conv_w)
    v_small = small_slab(v_meta_tokens, v_ffn1_norm, v_mix_norm, v_ffn2_norm, v_final_norm, v_out_norm_conv,
                         v_out_norm_attn, v_b_f, v_conv_w)
    gs = list(g_small)
    gs[4] = gs[4].reshape(final_norm.shape)
    g_slab = small_slab(gs[0], gs[1], gs[2], gs[3], gs[4], gs[5], gs[6], gs[7], gs[8])
    local_like = (meta_tokens, ffn1_norm, mix_norm, ffn2_norm, final_norm.reshape(1, D), out_norm_conv, out_norm_attn,
                  b_f, conv_w)
    small_out = [_unpack_small(s, local_like) for s in _adamw(w_small, g_slab, m_small, v_small, name="adamw_small")]
    for lst in small_out:
        lst[4] = lst[4].reshape(final_norm.shape)

    names = ["wgu1", "wd1", "win", "wout", "wgu2", "wd2"]
    w_big = big
    m_big = [m_ffn1_w_gu[0], m_ffn1_w_down[0], m_w_in[0], m_w_out[0], m_ffn2_w_gu[0], m_ffn2_w_down[0]]
    v_big = [v_ffn1_w_gu[0], v_ffn1_w_down[0], v_w_in[0], v_w_out[0], v_ffn2_w_gu[0], v_ffn2_w_down[0]]
    big_out = [_adamw(w, g, m, v, name="adamw_" + nm) for w, g, m, v, nm in zip(w_big, g_big, m_big, v_big, names)]

    def assemble(small, bigs):
        meta, a1, am, a3, af, gc, ga, bf, cw = small
        gu1_, d1_, win_, wout_, gu2_, d2_ = [b[None] for b in bigs]
        return [meta, a1, gu1_, d1_, am, win_, cw, bf, gc, ga, wout_, a3, gu2_, d2_, af]

    gs_out = list(g_small)
    gs_out[4] = gs_out[4].reshape(final_norm.shape)
    grads_out = assemble(gs_out, g_big)
    delta_out = assemble(small_out[0], [b[0] for b in big_out])
    m_out = assemble(small_out[1], [b[1] for b in big_out])
    v_out = assemble(small_out[2], [b[2] for b in big_out])
    return (loss, grad_x, *grads_out, *delta_out, *m_out, *v_out)
```

```python
import functools

import jax
import jax.numpy as jnp
from jax import lax
from jax.experimental import pallas as pl
from jax.experimental.pallas import tpu as pltpu

F32 = jnp.float32
BF16 = jnp.bfloat16

EPS = 1e-6
N_META = 16
HEAD_DIM = 64
N_SHARD = 4
N_DEV = 8
HALO = 16
LANES = 128
SMALL_ROWS = 32
VMEM_LIMIT_V7X = 56 * 1024 * 1024
NEG = -1e30

ADAM_LR = 0.001
ADAM_B1 = 0.9
ADAM_B2 = 0.999
ADAM_EPS = 1e-08
ADAM_WD = 0.01
ADAM_STEP = 10

MESH = pl.DeviceIdType.MESH
ANY = pl.BlockSpec(memory_space=pl.ANY)
NT_DIMS = (((1,), (1,)), ((), ()))
TN_DIMS = (((0,), (0,)), ((), ()))


def _params(*sem):
    return pltpu.CompilerParams(dimension_semantics=sem, vmem_limit_bytes=VMEM_LIMIT_V7X)


class _Comm:
    def __init__(self, ins, out_shapes, sems, start, finish, aliases=None):
        self.ins, self.out_shapes, self.sems = list(ins), list(out_shapes), list(sems)
        self.start, self.finish, self.aliases = start, finish, dict(aliases or {})


def _join(a, b):
    ni, no, ns = len(a.ins), len(a.out_shapes), len(a.sems)

    def start(ins, outs, sems):
        a.start(ins[:ni], outs[:no], sems[:ns])
        b.start(ins[ni:], outs[no:], sems[ns:])

    def finish(ins, outs, sems):
        a.finish(ins[:ni], outs[:no], sems[:ns])
        b.finish(ins[ni:], outs[no:], sems[ns:])

    aliases = dict(a.aliases)
    aliases.update({ni + i: no + j for i, j in b.aliases.items()})
    return _Comm(a.ins + b.ins, a.out_shapes + b.out_shapes, a.sems + b.sems, start, finish, aliases)


def _launch(body, *, name, grid, in_specs, out_specs, out_shape, args, scratch_shapes=(), comm=None, prefetch=(),
            aliases=None):
    single = not isinstance(out_shape, (list, tuple))
    out_specs = [out_specs] if single else list(out_specs)
    out_shape = [out_shape] if single else list(out_shape)
    in_specs, scratch_shapes, prefetch = list(in_specs), list(scratch_shapes), list(prefetch)
    params = _params(*(("arbitrary",) * len(grid)))
    n_pf, n_in, n_out, n_scr = len(prefetch), len(in_specs), len(out_specs), len(scratch_shapes)
    c_ins = comm.ins if comm else []
    c_shapes = comm.out_shapes if comm else []
    c_sems = comm.sems if comm else []
    c_in, c_out = len(c_ins), len(c_shapes)

    def carrier(*refs):
        p = 0
        pf = refs[p:p + n_pf]; p += n_pf
        a = refs[p:p + n_in]; p += n_in
        ci = refs[p:p + c_in]; p += c_in
        o = refs[p:p + n_out]; p += n_out
        co = refs[p:p + c_out]; p += c_out
        s = refs[p:p + n_scr]; p += n_scr
        cs = refs[p:]
        if comm:
            first = functools.reduce(lambda u, v: u & v, [pl.program_id(k) == 0 for k in range(len(grid))])

            @pl.when(first)
            def _():
                comm.start(ci, co, cs)

        body(*pf, *a, *o, *s)

        if comm:
            last = functools.reduce(lambda u, v: u & v, [pl.program_id(k) == grid[k] - 1 for k in range(len(grid))])

            @pl.when(last)
            def _():
                comm.finish(ci, co, cs)

    io_aliases = {n_pf + i: j for i, j in (aliases or {}).items()}
    if comm:
        io_aliases.update({n_pf + n_in + i: n_out + j for i, j in comm.aliases.items()})
    all_in, all_out = in_specs + [ANY] * c_in, out_specs + [ANY] * c_out
    all_scratch = scratch_shapes + [pltpu.SemaphoreType.DMA((k,)) for k in c_sems]
    if n_pf:
        spec = dict(grid_spec=pltpu.PrefetchScalarGridSpec(
            num_scalar_prefetch=n_pf, grid=grid, in_specs=all_in, out_specs=all_out, scratch_shapes=all_scratch))
    else:
        spec = dict(grid=grid, in_specs=all_in, out_specs=all_out, scratch_shapes=all_scratch)
    res = pl.pallas_call(carrier, name=name, out_shape=out_shape + c_shapes, input_output_aliases=io_aliases,
                         compiler_params=params, **spec)(*prefetch, *args, *c_ins)
    main = list(res[:n_out])
    return (main[0] if single else main), (list(res[n_out:]) if comm else None)


def _run_comm(comm, *, name):
    c_in, c_out = len(comm.ins), len(comm.out_shapes)

    def body(*refs):
        ci, co, cs = refs[:c_in], refs[c_in:c_in + c_out], refs[c_in + c_out:]
        comm.start(ci, co, cs)
        comm.finish(ci, co, cs)

    return list(pl.pallas_call(
        body, name=name, in_specs=[ANY] * c_in, out_specs=[ANY] * c_out, out_shape=comm.out_shapes,
        scratch_shapes=[pltpu.SemaphoreType.DMA((k,)) for k in comm.sems],
        input_output_aliases=comm.aliases)(*comm.ins))


def _chunks(width, step=512):
    out, c0 = [], 0
    while c0 < width:
        cw = min(step, width - c0)
        out.append((c0, cw))
        c0 += cw
    return out


def _split2(v):
    hi = v.astype(BF16)
    lo = (v - hi.astype(F32)).astype(BF16)
    return hi, lo


def _split3(v):
    hi = v.astype(BF16)
    r = v - hi.astype(F32)
    mid = r.astype(BF16)
    lo = (r - mid.astype(F32)).astype(BF16)
    return hi, mid, lo


def _dot(a, b):
    return jnp.dot(a, b, preferred_element_type=F32)


def _dot_nt(a, b):
    return lax.dot_general(a, b, NT_DIMS, preferred_element_type=F32)


def _dot_tn(a, b):
    return lax.dot_general(a, b, TN_DIMS, preferred_element_type=F32)


def _silu_mul(g, u):
    return g * jax.nn.sigmoid(g) * u


def _rms_bwd(dn, h, gain, dres):
    r = lax.rsqrt(jnp.mean(h * h, axis=-1, keepdims=True) + EPS)
    y = h * r
    dgain = jnp.sum(dn * y, axis=0, keepdims=True)
    dy = dn * gain
    dh = dres + r * (dy - y * jnp.mean(dy * y, axis=-1, keepdims=True))
    return dh, dgain


def _group_mean(v, p):
    hi, lo = _split2(v)
    return _dot(hi, p) + _dot(lo, p)


def _row_of(a, k):
    rows = lax.broadcasted_iota(jnp.int32, a.shape, 0)
    return jnp.sum(jnp.where(rows == k, a, 0.0), axis=0, keepdims=True)


def _causal_conv(u, prev, w):
    rows = lax.broadcasted_iota(jnp.int32, u.shape, 0)
    p1 = _row_of(prev, HALO - 1)
    p2 = _row_of(prev, HALO - 2)
    u1 = jnp.where(rows == 0, p1, pltpu.roll(u, 1, 0))
    u2 = jnp.where(rows == 0, p2, jnp.where(rows == 1, p1, pltpu.roll(u, 2, 0)))
    return w[2:3, :] * u + w[1:2, :] * u1 + w[0:1, :] * u2, u1, u2


def _rms(x, gain):
    return (x * lax.rsqrt(jnp.mean(x * x, axis=-1, keepdims=True) + EPS) * gain).astype(BF16)


def _rmsnorm(h, g, *, tm, name, comm=None):
    T, D = h.shape

    def body(h_ref, g_ref, n_ref):
        n_ref[...] = _rms(h_ref[...], g_ref[...])

    return _launch(
        body, name=name, grid=(T // tm,),
        in_specs=[pl.BlockSpec((tm, D), lambda i: (i, 0)), pl.BlockSpec((1, D), lambda i: (0, 0))],
        out_specs=pl.BlockSpec((tm, D), lambda i: (i, 0)),
        out_shape=jax.ShapeDtypeStruct((T, D), BF16), args=(h, g), comm=comm)


def _ffn_up(n, wgu, sid, gu_prev, *, tm, first, count, name, comm=None):
    T, D = n.shape
    ns, _, guc = wgu.shape
    ff = N_SHARD * guc // 2

    def body(sid_ref, x_ref, w_ref, *rest):
        rest[-1][...] = _dot(x_ref[...], w_ref[...]).astype(BF16)

    where = lambda s, sid: sid[first + s]
    w_at = (lambda s, sid: 0) if ns == 1 else where
    return _launch(
        body, name=name, grid=(count, T // tm), prefetch=(sid,),
        in_specs=[pl.BlockSpec((tm, D), lambda s, i, sid: (i, 0)),
                  pl.BlockSpec((None, D, guc), lambda s, i, sid: (w_at(s, sid), 0, 0))]
                 + ([] if gu_prev is None else [ANY]),
        out_specs=pl.BlockSpec((None, tm, guc), lambda s, i, sid: (where(s, sid) // 2, i, where(s, sid) % 2)),
        out_shape=jax.ShapeDtypeStruct((2, T, ff), BF16),
        args=(n, wgu) + (() if gu_prev is None else (gu_prev,)),
        aliases=None if gu_prev is None else {2: 0}, comm=comm)


def _matmul_nn(x, w, *, tm, nb, w_spec, out_shape, out_spec, name, comm=None):
    T, K = x.shape

    def body(x_ref, w_ref, o_ref):
        o_ref[...] = _dot(x_ref[...], w_ref[...]).astype(o_ref.dtype)

    return _launch(
        body, name=name, grid=(nb, T // tm),
        in_specs=[pl.BlockSpec((tm, K), lambda s, i: (i, 0)), w_spec],
        out_specs=out_spec, out_shape=out_shape, args=(x, w), comm=comm)


def _ffn_down(gu, wd, h, next_gain, *, tm, name, comm=None):
    _, T, ff = gu.shape
    D = h.shape[1]
    chunks = _chunks(ff)

    def body(g_ref, u_ref, wd_hbm, h_ref, ng_ref, o_ref, n_ref, wd_v, sem):
        @pl.when(pl.program_id(0) == 0)
        def _():
            cp = pltpu.make_async_copy(wd_hbm, wd_v, sem)
            cp.start()
            cp.wait()

        acc = jnp.zeros((tm, D), F32)
        for c0, cw in chunks:
            a = _silu_mul(g_ref[:, c0:c0 + cw].astype(F32), u_ref[:, c0:c0 + cw].astype(F32))
            acc = acc + _dot(a.astype(BF16), wd_v[c0:c0 + cw, :])
        out = h_ref[...] + 0.5 * acc
        o_ref[...] = out
        n_ref[...] = _rms(out, ng_ref[...])

    return _launch(
        body, name=name, grid=(T // tm,),
        in_specs=[pl.BlockSpec((None, tm, ff), lambda i: (0, i, 0)),
                  pl.BlockSpec((None, tm, ff), lambda i: (1, i, 0)),
                  ANY,
                  pl.BlockSpec((tm, D), lambda i: (i, 0)),
                  pl.BlockSpec((1, D), lambda i: (0, 0))],
        out_specs=[pl.BlockSpec((tm, D), lambda i: (i, 0)), pl.BlockSpec((tm, D), lambda i: (i, 0))],
        out_shape=[jax.ShapeDtypeStruct((T, D), F32), jax.ShapeDtypeStruct((T, D), BF16)],
        scratch_shapes=[pltpu.VMEM((ff, D), BF16), pltpu.SemaphoreType.DMA],
        args=(gu, gu, wd, h, next_gain), comm=comm)


def _ffn_bwd_act(df, gu, wd, *, tm, guc, name, comm=None):
    _, T, ff = gu.shape
    D = df.shape[1]
    nj = ff // guc
    chunks = _chunks(guc)

    def body(df_ref, g_ref, u_ref, wd_ref, o_ref):
        dfv = df_ref[...]
        for c0, cw in chunks:
            da = _dot_nt(dfv, wd_ref[c0:c0 + cw, :])
            g = g_ref[:, c0:c0 + cw].astype(F32)
            u = u_ref[:, c0:c0 + cw].astype(F32)
            sg = jax.nn.sigmoid(g)
            silu = g * sg
            o_ref[0, :, c0:c0 + cw] = (da * u * (sg * (1.0 + g * (1.0 - sg)))).astype(BF16)
            o_ref[1, :, c0:c0 + cw] = (da * silu).astype(BF16)

    return _launch(
        body, name=name, grid=(nj, T // tm),
        in_specs=[pl.BlockSpec((tm, D), lambda j, i: (i, 0)),
                  pl.BlockSpec((None, tm, guc), lambda j, i: (0, i, j)),
                  pl.BlockSpec((None, tm, guc), lambda j, i: (1, i, j)),
                  pl.BlockSpec((guc, D), lambda j, i: (j, 0))],
        out_specs=pl.BlockSpec((2, tm, guc), lambda j, i: (0, i, j)),
        out_shape=jax.ShapeDtypeStruct((2, T, ff), BF16),
        args=(df, gu, gu, wd), comm=comm)


def _ffn_bwd_in(dgu, wgu, h, g, dres, *, tm, scale, name, comm=None):
    _, T, ff = dgu.shape
    ns, D, guc = wgu.shape
    nj = ff // guc
    chunks = _chunks(guc)

    def body(dgu_ref, w_hbm, h_ref, g_ref, dres_ref, dh_ref, dhb_ref, dg_ref, w_v, acc, sem):
        i, j = pl.program_id(0), pl.program_id(1)

        @pl.when((i == 0) & (j == 0))
        def _():
            cp = pltpu.make_async_copy(w_hbm, w_v, sem)
            cp.start()
            cp.wait()
            dg_ref[...] = jnp.zeros_like(dg_ref)

        part = jnp.zeros((tm, D), F32)
        for c0, cw in chunks:
            part = part + _dot_nt(dgu_ref[0, :, c0:c0 + cw], w_v[j, :, c0:c0 + cw])
            part = part + _dot_nt(dgu_ref[1, :, c0:c0 + cw], w_v[nj + j, :, c0:c0 + cw])

        @pl.when(j == 0)
        def _():
            acc[...] = part

        @pl.when(j > 0)
        def _():
            acc[...] += part

        @pl.when(j == nj - 1)
        def _():
            dh, dgain = _rms_bwd(acc[...], h_ref[...], g_ref[...], dres_ref[...])
            dh_ref[...] = dh
            dhb_ref[...] = (scale * dh).astype(BF16)
            dg_ref[...] += dgain

    return _launch(
        body, name=name, grid=(T // tm, nj),
        in_specs=[pl.BlockSpec((2, tm, guc), lambda i, j: (0, i, j)),
                  ANY,
                  pl.BlockSpec((tm, D), lambda i, j: (i, 0)),
                  pl.BlockSpec((1, D), lambda i, j: (0, 0)),
                  pl.BlockSpec((tm, D), lambda i, j: (i, 0))],
        out_specs=[pl.BlockSpec((tm, D), lambda i, j: (i, 0)),
                   pl.BlockSpec((tm, D), lambda i, j: (i, 0)),
                   pl.BlockSpec((1, D), lambda i, j: (0, 0))],
        out_shape=[jax.ShapeDtypeStruct((T, D), F32), jax.ShapeDtypeStruct((T, D), BF16),
                   jax.ShapeDtypeStruct((1, D), F32)],
        scratch_shapes=[pltpu.VMEM((ns, D, guc), BF16), pltpu.VMEM((tm, D), F32), pltpu.SemaphoreType.DMA],
        args=(dgu, wgu, h, g, dres), comm=comm)


def _mix_bwd_in(parts, w_main, w_fg, h, g, dres, *, tm, scale, name, comm=None):
    T, D = h.shape
    widths = [p.shape[1] for p in parts[:-1]]
    offs = [sum(widths[:k]) for k in range(len(widths))]
    npart = len(parts)

    def body(*refs):
        p_refs = refs[:npart]
        wm_ref, wf_ref, h_ref, g_ref, dres_ref, dh_ref, dhb_ref, dg_ref = refs[npart:]

        @pl.when(pl.program_id(0) == 0)
        def _():
            dg_ref[...] = jnp.zeros_like(dg_ref)

        dn = _dot_nt(p_refs[-1][...].astype(BF16), wf_ref[...])
        for p_ref, off, wd_ in zip(p_refs[:-1], offs, widths):
            for c0, cw in _chunks(wd_):
                dn = dn + _dot_nt(p_ref[:, c0:c0 + cw].astype(BF16), wm_ref[:, off + c0:off + c0 + cw])
        dh, dgain = _rms_bwd(dn, h_ref[...], g_ref[...], dres_ref[...])
        dh_ref[...] = dh
        dhb_ref[...] = (scale * dh).astype(BF16)
        dg_ref[...] += dgain

    row = lambda i: (i, 0)
    const = lambda i: (0, 0)
    return _launch(
        body, name=name, grid=(T // tm,),
        in_specs=[pl.BlockSpec((tm, p.shape[1]), row) for p in parts]
                 + [pl.BlockSpec(w_main.shape, const), pl.BlockSpec(w_fg.shape, const),
                    pl.BlockSpec((tm, D), row), pl.BlockSpec((1, D), const), pl.BlockSpec((tm, D), row)],
        out_specs=[pl.BlockSpec((tm, D), row), pl.BlockSpec((tm, D), row), pl.BlockSpec((1, D), const)],
        out_shape=[jax.ShapeDtypeStruct((T, D), F32), jax.ShapeDtypeStruct((T, D), BF16),
                   jax.ShapeDtypeStruct((1, D), F32)],
        args=(*parts, w_main, w_fg, h, g, dres), comm=comm)


def _matmul_tn(xs, y, *, tm, nb, x_specs, y_spec, out_shape, out_spec, kb, silu, name, comm=None):
    T = y.shape[-2]
    nx = len(xs)
    chunks = _chunks(kb)

    def body(*refs):
        x_refs, y_ref, o_ref = refs[:nx], refs[nx], refs[nx + 1]
        i = pl.program_id(1)

        @pl.when(i == 0)
        def _():
            o_ref[...] = jnp.zeros_like(o_ref)

        yv = y_ref[...].astype(BF16)
        for c0, cw in chunks:
            if silu:
                xv = _silu_mul(x_refs[0][:, c0:c0 + cw].astype(F32), x_refs[1][:, c0:c0 + cw].astype(F32)).astype(BF16)
            else:
                xv = x_refs[0][:, c0:c0 + cw]
            o_ref[c0:c0 + cw, :] += _dot_tn(xv, yv)

    return _launch(
        body, name=name, grid=(nb, T // tm),
        in_specs=list(x_specs) + [y_spec], out_specs=out_spec, out_shape=out_shape,
        args=(*xs, y), comm=comm)


def _tri(n, lower):
    r = lax.broadcasted_iota(jnp.int32, (n, n), 0)
    c = lax.broadcasted_iota(jnp.int32, (n, n), 1)
    return jnp.where((r >= c) if lower else (r <= c), 1.0, 0.0).astype(BF16)


def _tri_dot(tri, v):
    hi, mid, lo = _split3(v)
    return _dot(tri, hi) + _dot(tri, mid) + _dot(tri, lo)


def _fcum(fg, bf, *, ch, name):
    B, L, W = fg.shape
    nch = L // ch

    def body(fg_ref, bf_ref, f_ref):
        tri = _tri(ch, True)
        carry = jnp.zeros((1, W), F32)
        for c in range(nch):
            x = fg_ref[c * ch:(c + 1) * ch, :] + bf_ref[...]
            lf = jnp.minimum(x, 0.0) - jnp.log(1.0 + jnp.exp(-jnp.abs(x)))
            f_ref[c * ch:(c + 1) * ch, :] = _tri_dot(tri, lf) + carry
            carry = carry + jnp.sum(lf, axis=0, keepdims=True)

    return pl.pallas_call(
        body, name=name, grid=(B,),
        in_specs=[pl.BlockSpec((None, L, W), lambda b: (b, 0, 0)), pl.BlockSpec((1, W), lambda b: (0, 0))],
        out_specs=pl.BlockSpec((None, L, W), lambda b: (b, 0, 0)),
        out_shape=jax.ShapeDtypeStruct((B, L, W), F32),
        compiler_params=_params("arbitrary"),
    )(fg, bf)


def _fcum_bwd(dF, fg, bf, *, ch, name):
    B, L, W = fg.shape
    nch = L // ch

    def body(df_ref, fg_ref, bf_ref, dfg_ref, db_ref):
        @pl.when(pl.program_id(0) == 0)
        def _():
            db_ref[...] = jnp.zeros_like(db_ref)

        tri = _tri(ch, False)
        carry = jnp.zeros((1, W), F32)
        dbs = jnp.zeros((1, W), F32)
        for c in reversed(range(nch)):
            d = df_ref[c * ch:(c + 1) * ch, :]
            dlf = _tri_dot(tri, d) + carry
            carry = carry + jnp.sum(d, axis=0, keepdims=True)
            x = fg_ref[c * ch:(c + 1) * ch, :] + bf_ref[...]
            dfg = dlf * jax.nn.sigmoid(-x)
            dfg_ref[c * ch:(c + 1) * ch, :] = dfg.astype(BF16)
            dbs = dbs + jnp.sum(dfg, axis=0, keepdims=True)
        db_ref[...] += dbs

    blk = pl.BlockSpec((None, L, W), lambda b: (b, 0, 0))
    return pl.pallas_call(
        body, name=name, grid=(B,),
        in_specs=[blk, blk, pl.BlockSpec((1, W), lambda b: (0, 0))],
        out_specs=[blk, pl.BlockSpec((1, W), lambda b: (0, 0))],
        out_shape=[jax.ShapeDtypeStruct((B, L, W), BF16), jax.ShapeDtypeStruct((1, W), F32)],
        compiler_params=_params("arbitrary"),
    )(dF, fg, bf)


def _attn_fwd(proj, fc, fr, *, tq, n_heads, name, comm=None):
    B, L, _ = proj.shape
    AD = n_heads * HEAD_DIM
    nq = L // tq
    W = fc.shape[-1]
    scale = HEAD_DIM ** -0.5

    def body(q_ref, k_ref, v_ref, fr_ref, o_ref, lse_ref, m_s, l_s, acc_s):
        qi, ki = pl.program_id(1), pl.program_id(2)

        @pl.when(ki == 0)
        def _():
            m_s[...] = jnp.full_like(m_s, NEG)
            l_s[...] = jnp.zeros_like(l_s)
            acc_s[...] = jnp.zeros_like(acc_s)

        def tile(diagonal):
            if diagonal:
                mask = (lax.broadcasted_iota(jnp.int32, (tq, tq), 1)
                        <= lax.broadcasted_iota(jnp.int32, (tq, tq), 0))
            lane = lax.broadcasted_iota(jnp.int32, (tq, W), 1)
            m_all, l_all = m_s[...], l_s[...]
            m_out, l_out = m_all, l_all

            def scores(h):
                sl = slice(h * HEAD_DIM, (h + 1) * HEAD_DIM)
                return _dot_nt(q_ref[:, sl] * scale, k_ref[:, sl])

            nxt = scores(0)
            for h in range(n_heads):
                sl = slice(h * HEAD_DIM, (h + 1) * HEAD_DIM)
                s = nxt - fr_ref[h:h + 1, :]
                if h + 1 < n_heads:
                    nxt = scores(h + 1)
                if diagonal:
                    s = jnp.where(mask, s, NEG)
                m_old = m_all[:, h:h + 1]
                m_new = jnp.maximum(m_old, jnp.max(s, axis=1, keepdims=True))
                alpha = jnp.exp(m_old - m_new)
                p = jnp.exp(s - m_new)
                l_new = alpha * l_all[:, h:h + 1] + jnp.sum(p, axis=1, keepdims=True)
                acc_s[:, sl] = alpha * acc_s[:, sl] + _dot(p.astype(BF16), v_ref[:, sl])
                m_out = jnp.where(lane == h, m_new, m_out)
                l_out = jnp.where(lane == h, l_new, l_out)
            m_s[...] = m_out
            l_s[...] = l_out

        @pl.when(ki < qi)
        def _():
            tile(False)

        @pl.when(ki == qi)
        def _():
            tile(True)
            l = l_s[...]
            for h in range(n_heads):
                sl = slice(h * HEAD_DIM, (h + 1) * HEAD_DIM)
                o_ref[:, sl] = acc_s[:, sl] / l[:, h:h + 1]
            lse_ref[...] = jnp.where(l > 0.0, m_s[...] + jnp.log(jnp.where(l > 0.0, l, 1.0)), 0.0)

    kv = lambda b, qi, ki: jnp.minimum(ki, qi)
    return _launch(
        body, name=name, grid=(B, nq, nq), args=(proj, proj, proj, fr), comm=comm,
        in_specs=[pl.BlockSpec((None, tq, AD), lambda b, qi, ki: (b, qi, 3)),
                  pl.BlockSpec((None, tq, AD), lambda b, qi, ki: (b, kv(b, qi, ki), 4)),
                  pl.BlockSpec((None, tq, AD), lambda b, qi, ki: (b, kv(b, qi, ki), 5)),
                  pl.BlockSpec((None, None, n_heads, tq), lambda b, qi, ki: (b, kv(b, qi, ki), 0, 0))],
        out_specs=[pl.BlockSpec((None, tq, AD), lambda b, qi, ki: (b, qi, 0)),
                   pl.BlockSpec((None, tq, W), lambda b, qi, ki: (b, qi, 0))],
        out_shape=[jax.ShapeDtypeStruct((B, L, AD), F32), jax.ShapeDtypeStruct((B, L, W), F32)],
        scratch_shapes=[pltpu.VMEM((tq, W), F32), pltpu.VMEM((tq, W), F32), pltpu.VMEM((tq, AD), F32)])


def _attn_bwd(proj, o, do, lse, fc, fr, *, tq, n_heads, name, comm=None):
    B, L, _ = proj.shape
    AD = n_heads * HEAD_DIM
    nq = L // tq
    W = fc.shape[-1]
    scale = HEAD_DIM ** -0.5

    def body(q_ref, k_ref, v_ref, o_ref, do_ref, lse_ref, fr_ref,
             dq_ref, dk_ref, dv_ref, dfr_ref, dfq_ref, dk_s, dv_s):
        kj, qi = pl.program_id(1), pl.program_id(2)

        @pl.when((kj == 0) & (qi == 0))
        def _():
            dq_ref[...] = jnp.zeros_like(dq_ref)
            dfq_ref[...] = jnp.zeros_like(dfq_ref)

        @pl.when(qi == kj)
        def _():
            dk_s[...] = jnp.zeros_like(dk_s)
            dv_s[...] = jnp.zeros_like(dv_s)
            dfr_ref[...] = jnp.zeros_like(dfr_ref)

        def tile(diagonal):
            if diagonal:
                mask = (lax.broadcasted_iota(jnp.int32, (tq, tq), 1)
                        <= lax.broadcasted_iota(jnp.int32, (tq, tq), 0))
            rows = pl.ds(pl.multiple_of(qi * tq, 8), tq)
            lane = lax.broadcasted_iota(jnp.int32, (tq, W), 1)
            head = lax.broadcasted_iota(jnp.int32, (n_heads, tq), 0)
            lse = lse_ref[...]
            dfq = jnp.zeros((tq, W), F32)
            dfr = jnp.zeros((n_heads, tq), F32)
            for h in range(n_heads):
                sl = slice(h * HEAD_DIM, (h + 1) * HEAD_DIM)
                k, v, dov = k_ref[:, sl], v_ref[:, sl], do_ref[:, sl]
                q = q_ref[:, sl] * scale
                s = _dot_nt(q, k) - fr_ref[h:h + 1, :]
                if diagonal:
                    s = jnp.where(mask, s, NEG)
                p = jnp.exp(s - lse[:, h:h + 1])
                dp = _dot_nt(dov, v)
                dsum = jnp.sum(dov.astype(F32) * o_ref[:, sl], axis=1, keepdims=True)
                ds = p * (dp - dsum)
                dsb = ds.astype(BF16)
                dv_s[:, sl] += _dot_tn(p.astype(BF16), dov)
                dk_s[:, sl] += _dot_tn(dsb, q)
                dq_ref[rows, sl] += _dot(dsb, k) * scale
                dfr = jnp.where(head == h, jnp.sum(ds, axis=0, keepdims=True), dfr)
                dfq = jnp.where(lane == h, jnp.sum(ds, axis=1, keepdims=True), dfq)
            dfr_ref[...] -= dfr
            dfq_ref[rows, :] += dfq

        @pl.when(qi > kj)
        def _():
            tile(False)

        @pl.when(qi == kj)
        def _():
            tile(True)

        @pl.when(qi == nq - 1)
        def _():
            dk_ref[...] = dk_s[...].astype(BF16)
            dv_ref[...] = dv_s[...].astype(BF16)

    qq = lambda b, kj, qi: jnp.maximum(qi, kj)
    qblk = lambda w, cb: pl.BlockSpec((None, tq, w), lambda b, kj, qi: (b, qq(b, kj, qi), cb))
    kblk = lambda cb: pl.BlockSpec((None, tq, AD), lambda b, kj, qi: (b, kj, cb))
    return _launch(
        body, name=name, grid=(B, nq, nq), args=(proj, proj, proj, o, do, lse, fr), comm=comm,
        in_specs=[qblk(AD, 3), kblk(4), kblk(5), qblk(AD, 0), qblk(AD, 0), qblk(W, 0),
                  pl.BlockSpec((None, None, n_heads, tq), lambda b, kj, qi: (b, kj, 0, 0))],
        out_specs=[pl.BlockSpec((None, L, AD), lambda b, kj, qi: (b, 0, 0)),
                   kblk(0), kblk(0),
                   pl.BlockSpec((None, None, n_heads, tq), lambda b, kj, qi: (b, kj, 0, 0)),
                   pl.BlockSpec((None, L, W), lambda b, kj, qi: (b, 0, 0))],
        out_shape=[jax.ShapeDtypeStruct((B, L, AD), F32), jax.ShapeDtypeStruct((B, L, AD), BF16),
                   jax.ShapeDtypeStruct((B, L, AD), BF16), jax.ShapeDtypeStruct((B, nq, n_heads, tq), F32),
                   jax.ShapeDtypeStruct((B, L, W), F32)],
        scratch_shapes=[pltpu.VMEM((tq, AD), F32), pltpu.VMEM((tq, AD), F32)])


def _mix_gather(refs, first):
    b_ref, c_ref, hc_ref, cp_ref, hcp_ref, o_ref, cw_ref, p_ref = refs
    bg = b_ref[...].astype(F32)
    u = c_ref[...].astype(F32) * hc_ref[...].astype(F32)
    prev = cp_ref[...].astype(F32) * hcp_ref[...].astype(F32)
    prev = jnp.where(first, 0.0, prev)
    cv, u1, u2 = _causal_conv(u, prev, cw_ref[...])
    yc = bg * cv
    p = p_ref[...]
    rc = lax.rsqrt(_group_mean(yc * yc, p) + EPS)
    ya = o_ref[...].astype(F32)
    ra = lax.rsqrt(_group_mean(ya * ya, p) + EPS)
    return bg, (u, u1, u2), cv, yc * rc, rc, ya * ra, ra


def _mix_specs(tm, CD, D, grid_rank_fn):
    per = tm // HALO
    cur = lambda cb: pl.BlockSpec((None, tm, CD), lambda b, i: (b, i, cb))
    prev = lambda cb: pl.BlockSpec((None, HALO, CD), lambda b, i: (b, jnp.maximum(i * per - 1, 0), cb))
    return [cur(0), cur(1), cur(2), prev(1), prev(2), cur(0)]


def _mix_out(proj, o, cw, gc, ga, wout, h, pmat, next_gain, *, tm, name, comm=None):
    B, L, D = h.shape
    CD = o.shape[-1]
    const = lambda b, i: (0, 0)

    def body(b_ref, c_ref, hc_ref, cp_ref, hcp_ref, o_ref, cw_ref, p_ref, gc_ref, ga_ref, w_ref, h_ref, ng_ref,
             out_ref, y_ref, n_ref):
        first = pl.program_id(1) == 0
        _, _, _, zc, _, za, _ = _mix_gather((b_ref, c_ref, hc_ref, cp_ref, hcp_ref, o_ref, cw_ref, p_ref), first)
        yc = (zc * gc_ref[...]).astype(BF16)
        ya = (za * ga_ref[...]).astype(BF16)
        y_ref[:, :CD] = yc
        y_ref[:, CD:] = ya
        out = h_ref[...] + _dot(yc, w_ref[:CD, :]) + _dot(ya, w_ref[CD:, :])
        out_ref[...] = out
        n_ref[...] = _rms(out, ng_ref[...])

    tile = pl.BlockSpec((None, tm, D), lambda b, i: (b, i, 0))
    return _launch(
        body, name=name, grid=(B, L // tm),
        in_specs=_mix_specs(tm, CD, D, None)
                 + [pl.BlockSpec(cw.shape, const), pl.BlockSpec(pmat.shape, const),
                    pl.BlockSpec((1, CD), const), pl.BlockSpec((1, CD), const), pl.BlockSpec((D, D), const),
                    tile, pl.BlockSpec((1, D), const)],
        out_specs=[tile, tile, tile],
        out_shape=[jax.ShapeDtypeStruct((B, L, D), F32), jax.ShapeDtypeStruct((B, L, D), BF16),
                   jax.ShapeDtypeStruct((B, L, D), BF16)],
        args=(proj, proj, proj, proj, proj, o, cw, pmat, gc, ga, wout, h, next_gain), comm=comm)


def _mix_out_bwd(dhb, proj, o, cw, gc, ga, wout, pmat, *, tm, name, comm=None):
    B, L, D = dhb.shape
    CD = o.shape[-1]
    const = lambda b, i: (0, 0)

    def body(dh_ref, b_ref, c_ref, hc_ref, cp_ref, hcp_ref, o_ref, cw_ref, p_ref, gc_ref, ga_ref, w_ref,
             db_ref, dcv_ref, do_ref, dgc_ref, dga_ref, dcw_ref):
        first = pl.program_id(1) == 0

        @pl.when((pl.program_id(0) == 0) & first)
        def _():
            dgc_ref[...] = jnp.zeros_like(dgc_ref)
            dga_ref[...] = jnp.zeros_like(dga_ref)
            dcw_ref[...] = jnp.zeros_like(dcw_ref)

        bg, us, cv, zc, rc, za, ra = _mix_gather(
            (b_ref, c_ref, hc_ref, cp_ref, hcp_ref, o_ref, cw_ref, p_ref), first)
        p = p_ref[...]
        dh = dh_ref[...]
        dyc = _dot_nt(dh, w_ref[:CD, :])
        dya = _dot_nt(dh, w_ref[CD:, :])

        dgc_ref[...] += jnp.sum(dyc * zc, axis=0, keepdims=True)
        dz = dyc * gc_ref[...]
        dx = rc * (dz - zc * _group_mean(dz * zc, p))
        db_ref[...] = (dx * cv).astype(BF16)
        dcv = dx * bg
        dcv_ref[...] = dcv.astype(BF16)
        for k in range(3):
            dcw_ref[k:k + 1, :] += jnp.sum(dcv * us[2 - k], axis=0, keepdims=True)

        dga_ref[...] += jnp.sum(dya * za, axis=0, keepdims=True)
        dz = dya * ga_ref[...]
        do_ref[...] = (ra * (dz - za * _group_mean(dz * za, p))).astype(BF16)

    tile = lambda w: pl.BlockSpec((None, tm, w), lambda b, i: (b, i, 0))
    return _launch(
        body, name=name, grid=(B, L // tm), comm=comm,
        args=(dhb, proj, proj, proj, proj, proj, o, cw, pmat, gc, ga, wout),
        in_specs=[tile(D)] + _mix_specs(tm, CD, D, None)
                 + [pl.BlockSpec(cw.shape, const), pl.BlockSpec(pmat.shape, const),
                    pl.BlockSpec((1, CD), const), pl.BlockSpec((1, CD), const), pl.BlockSpec((D, D), const)],
        out_specs=[tile(CD), tile(CD), tile(CD),
                   pl.BlockSpec((1, CD), const), pl.BlockSpec((1, CD), const), pl.BlockSpec((8, CD), const)],
        out_shape=[jax.ShapeDtypeStruct((B, L, CD), BF16)] * 3
                  + [jax.ShapeDtypeStruct((1, CD), F32)] * 2 + [jax.ShapeDtypeStruct((8, CD), F32)])


def _conv_bwd(dcv, proj, cw, *, tm, name):
    B, L, CD = dcv.shape
    per = tm // HALO
    nhalo = L // HALO
    nt = L // tm

    def body(d_ref, dn_ref, c_ref, hc_ref, cw_ref, out_ref):
        last = pl.program_id(1) == nt - 1
        d = d_ref[...].astype(F32)
        nxt = jnp.where(last, 0.0, dn_ref[...].astype(F32))
        n0, n1 = _row_of(nxt, 0), _row_of(nxt, 1)
        rows = lax.broadcasted_iota(jnp.int32, d.shape, 0)
        d1 = jnp.where(rows == tm - 1, n0, pltpu.roll(d, tm - 1, 0))
        d2 = jnp.where(rows == tm - 2, n0, jnp.where(rows == tm - 1, n1, pltpu.roll(d, tm - 2, 0)))
        w = cw_ref[...]
        du = w[2:3, :] * d + w[1:2, :] * d1 + w[0:1, :] * d2
        out_ref[:, :CD] = (du * hc_ref[...].astype(F32)).astype(BF16)
        out_ref[:, CD:] = (du * c_ref[...].astype(F32)).astype(BF16)

    return pl.pallas_call(
        body, name=name, grid=(B, nt),
        in_specs=[pl.BlockSpec((None, tm, CD), lambda b, i: (b, i, 0)),
                  pl.BlockSpec((None, HALO, CD), lambda b, i: (b, jnp.minimum((i + 1) * per, nhalo - 1), 0)),
                  pl.BlockSpec((None, tm, CD), lambda b, i: (b, i, 1)),
                  pl.BlockSpec((None, tm, CD), lambda b, i: (b, i, 2)),
                  pl.BlockSpec(cw.shape, lambda b, i: (0, 0))],
        out_specs=pl.BlockSpec((None, tm, 2 * CD), lambda b, i: (b, i, 0)),
        out_shape=jax.ShapeDtypeStruct((B, L, 2 * CD), BF16),
        compiler_params=_params("arbitrary", "arbitrary"),
    )(dcv, dcv, proj, proj, cw)


def _final(h, gf, tgt, *, tm, name):
    B, L, D = h.shape

    def body(h_ref, g_ref, t_ref, dh_ref, dhb_ref, dg_ref, loss_ref):
        b, i = pl.program_id(0), pl.program_id(1)

        @pl.when((b == 0) & (i == 0))
        def _():
            dg_ref[...] = jnp.zeros_like(dg_ref)
            loss_ref[...] = jnp.zeros_like(loss_ref)

        x = h_ref[...]
        g = g_ref[...]
        r = lax.rsqrt(jnp.mean(x * x, axis=-1, keepdims=True) + EPS)
        y = x * r
        pos = i * tm + lax.broadcasted_iota(jnp.int32, (tm, 1), 0)
        err = jnp.where(pos >= N_META, y * g - t_ref[...], 0.0)
        loss_ref[...] += 0.5 * jnp.sum(jnp.mean(err * err, axis=-1, keepdims=True))
        dout = err / D
        dg_ref[...] += jnp.sum(dout * y, axis=0, keepdims=True)
        dy = dout * g
        dh = r * (dy - y * jnp.mean(dy * y, axis=-1, keepdims=True))
        dh_ref[...] = dh
        dhb_ref[...] = (0.5 * dh).astype(BF16)

    tile = pl.BlockSpec((None, tm, D), lambda b, i: (b, i, 0))
    const = lambda b, i: (0, 0)
    return pl.pallas_call(
        body, name=name, grid=(B, L // tm),
        in_specs=[tile, pl.BlockSpec((1, D), const), tile],
        out_specs=[tile, tile, pl.BlockSpec((1, D), const), pl.BlockSpec((1, LANES), const)],
        out_shape=[jax.ShapeDtypeStruct((B, L, D), F32), jax.ShapeDtypeStruct((B, L, D), BF16),
                   jax.ShapeDtypeStruct((1, D), F32), jax.ShapeDtypeStruct((1, LANES), F32)],
        compiler_params=_params("arbitrary", "arbitrary"),
    )(h, gf, tgt)


def _place():
    x, y, c = lax.axis_index("x"), lax.axis_index("y"), lax.axis_index("c")
    others = [(1 - x, y), (x, 1 - y), (1 - x, 1 - y)]
    return x, y, c, others


def _all_gather_shards(shards, *, name):
    n = len(shards)

    def body(*refs):
        ins, outs = refs[:n], refs[n:2 * n]
        send, recv, fsend, frecv, lsem = refs[2 * n:]
        x, y, c, others = _place()
        me = 2 * x + y
        local = [pltpu.make_async_copy(ins[t], outs[t].at[me], lsem.at[t]) for t in range(n)]
        for cp in local:
            cp.start()

        def half(t, k):
            hr = shards[t].shape[0] // 2
            return pl.ds(pl.multiple_of(k * hr, HALO), hr)

        def ici(t, j, src_chip, to):
            src = ins[t].at[half(t, c)] if to is not None else outs[t].at[src_chip, half(t, c)]
            return pltpu.make_async_remote_copy(
                src_ref=src, dst_ref=outs[t].at[src_chip, half(t, c)],
                send_sem=send.at[3 * t + j], recv_sem=recv.at[3 * t + j],
                device_id=(x, y, c) if to is None else to, device_id_type=MESH)

        def d2d(t, j, src_chip, k):
            return pltpu.make_async_remote_copy(
                src_ref=outs[t].at[src_chip, half(t, k)], dst_ref=outs[t].at[src_chip, half(t, k)],
                send_sem=fsend.at[3 * t + j], recv_sem=frecv.at[3 * t + j],
                device_id=(x, y, 1 - c), device_id_type=MESH)

        firsts = [ici(t, j, me, (ox, oy, c)) for t in range(n) for j, (ox, oy) in enumerate(others)]
        for cp in firsts:
            cp.start()
        passed = []
        for t in range(n):
            for j, (ox, oy) in enumerate(others):
                ici(t, j, 2 * ox + oy, None).wait_recv()
                cp = d2d(t, j, 2 * ox + oy, c)
                cp.start()
                passed.append(cp)
        for t in range(n):
            for j, (ox, oy) in enumerate(others):
                d2d(t, j, 2 * ox + oy, 1 - c).wait_recv()
        for cp in firsts + passed:
            cp.wait_send()
        for cp in local:
            cp.wait()

    return pl.pallas_call(
        body, name=name,
        in_specs=[ANY] * n, out_specs=[ANY] * n,
        out_shape=[jax.ShapeDtypeStruct((N_SHARD,) + s.shape, s.dtype) for s in shards],
        scratch_shapes=[pltpu.SemaphoreType.DMA((3 * n,))] * 4 + [pltpu.SemaphoreType.DMA((n,))],
    )(*shards)


def _all_reduce_small(slab, *, name):
    def body(in_ref, out_ref, gath, send, recv):
        x, y, c, _ = _place()
        me = 4 * x + 2 * y + c
        gath[me] = in_ref[...]
        copies, peers = [], []
        for m in range(1, N_DEV):
            px = jnp.where((m >> 2) & 1, 1 - x, x)
            py = jnp.where((m >> 1) & 1, 1 - y, y)
            pc = jnp.where(m & 1, 1 - c, c)
            cp = pltpu.make_async_remote_copy(
                src_ref=in_ref, dst_ref=gath.at[me], send_sem=send.at[m - 1], recv_sem=recv.at[m - 1],
                device_id=(px, py, pc), device_id_type=MESH)
            cp.start()
            copies.append(cp)
            peers.append(4 * px + 2 * py + pc)
        for m in range(1, N_DEV):
            pltpu.make_async_remote_copy(
                src_ref=in_ref, dst_ref=gath.at[peers[m - 1]], send_sem=send.at[m - 1], recv_sem=recv.at[m - 1],
                device_id=(x, y, c), device_id_type=MESH).wait_recv()
        for cp in copies:
            cp.wait_send()
        acc = gath[0]
        for k in range(1, N_DEV):
            acc = acc + gath[k]
        out_ref[...] = acc

    vm = pl.BlockSpec(memory_space=pltpu.VMEM)
    return pl.pallas_call(
        body, name=name, in_specs=[vm], out_specs=vm,
        out_shape=jax.ShapeDtypeStruct(slab.shape, slab.dtype),
        scratch_shapes=[pltpu.VMEM((N_DEV,) + slab.shape, slab.dtype),
                        pltpu.SemaphoreType.DMA((N_DEV - 1,)), pltpu.SemaphoreType.DMA((N_DEV - 1,))],
    )(slab)


def _gather_ici(shards):
    n = len(shards)

    def copies(ins, outs, sems, sending):
        send, recv, _ = sems
        x, y, c, others = _place()
        me = 2 * x + y
        out = []
        for t in range(n):
            hr = shards[t].shape[0] // 2
            rows = pl.ds(pl.multiple_of(c * hr, HALO), hr)
            for j, (ox, oy) in enumerate(others):
                src_chip = me if sending else 2 * ox + oy
                out.append(pltpu.make_async_remote_copy(
                    src_ref=ins[t].at[rows], dst_ref=outs[t].at[src_chip, rows],
                    send_sem=send.at[3 * t + j], recv_sem=recv.at[3 * t + j],
                    device_id=(ox, oy, c) if sending else (x, y, c), device_id_type=MESH))
        return out

    def local(ins, outs, sems):
        x, y, _, _ = _place()
        return [pltpu.make_async_copy(ins[t], outs[t].at[2 * x + y], sems[2].at[t]) for t in range(n)]

    def start(ins, outs, sems):
        for cp in local(ins, outs, sems) + copies(ins, outs, sems, True):
            cp.start()

    def finish(ins, outs, sems):
        for cp in copies(ins, outs, sems, False):
            cp.wait_recv()
        for cp in copies(ins, outs, sems, True):
            cp.wait_send()
        for cp in local(ins, outs, sems):
            cp.wait()

    return _Comm(shards, [jax.ShapeDtypeStruct((N_SHARD,) + s.shape, s.dtype) for s in shards],
                 [3 * n, 3 * n, n], start, finish)


def _gather_d2d(parts):
    n = len(parts)

    def copies(outs, sems, sending):
        send, recv = sems
        x, y, c, others = _place()
        out = []
        for t in range(n):
            hr = parts[t].shape[1] // 2
            rows = pl.ds(pl.multiple_of((c if sending else 1 - c) * hr, HALO), hr)
            for j, (ox, oy) in enumerate(others):
                blk = outs[t].at[2 * ox + oy, rows]
                out.append(pltpu.make_async_remote_copy(
                    src_ref=blk, dst_ref=blk, send_sem=send.at[3 * t + j], recv_sem=recv.at[3 * t + j],
                    device_id=(x, y, 1 - c) if sending else (x, y, c), device_id_type=MESH))
        return out

    def start(ins, outs, sems):
        for cp in copies(outs, sems, True):
            cp.start()

    def finish(ins, outs, sems):
        for cp in copies(outs, sems, False):
            cp.wait_recv()
        for cp in copies(outs, sems, True):
            cp.wait_send()

    return _Comm(parts, [jax.ShapeDtypeStruct(p.shape, p.dtype) for p in parts], [3 * n, 3 * n], start, finish,
                 aliases={t: t for t in range(n)})


def _swap_halves(grads):
    n = len(grads)

    def copies(ins, outs, sems):
        x, y, c, _ = _place()
        out = []
        for t in range(n):
            hr = grads[t].shape[1] // 2
            rows = pl.ds(pl.multiple_of((1 - c) * hr, 8), hr)
            out.append(pltpu.make_async_remote_copy(
                src_ref=ins[t].at[:, rows, :], dst_ref=outs[t], send_sem=sems[0].at[t], recv_sem=sems[1].at[t],
                device_id=(x, y, 1 - c), device_id_type=MESH))
        return out

    def start(ins, outs, sems):
        for cp in copies(ins, outs, sems):
            cp.start()

    def finish(ins, outs, sems):
        for cp in copies(ins, outs, sems):
            cp.wait()

    return _Comm(grads, [jax.ShapeDtypeStruct((N_SHARD, g.shape[1] // 2, g.shape[2]), g.dtype) for g in grads],
                 [n, n], start, finish)


def _pair_sum(g, got, c, *, name):
    ns, R, C = g.shape
    hr = R // 2

    def body(c_ref, g_ref, r_ref, o_ref):
        o_ref[...] = (g_ref[...] + r_ref[...]).astype(BF16)

    return pl.pallas_call(
        body, name=name,
        grid_spec=pltpu.PrefetchScalarGridSpec(
            num_scalar_prefetch=1, grid=(ns,),
            in_specs=[pl.BlockSpec((None, hr, C), lambda s, cr: (s, cr[0], 0)),
                      pl.BlockSpec((None, hr, C), lambda s, cr: (s, 0, 0))],
            out_specs=pl.BlockSpec((None, hr, C), lambda s, cr: (s, 0, 0))),
        out_shape=jax.ShapeDtypeStruct((ns, hr, C), BF16),
        compiler_params=_params("arbitrary"),
    )(c, g, got)


def _scatter_chips(sums):
    n = len(sums)

    def copies(ins, outs, sems, sending):
        x, y, c, others = _place()
        me = 2 * x + y
        out = []
        for t in range(n):
            for j, (ox, oy) in enumerate(others):
                there = 2 * ox + oy
                out.append(pltpu.make_async_remote_copy(
                    src_ref=ins[t].at[there if sending else me], dst_ref=outs[t].at[me if sending else there],
                    send_sem=sems[0].at[3 * t + j], recv_sem=sems[1].at[3 * t + j],
                    device_id=(ox, oy, c) if sending else (x, y, c), device_id_type=MESH))
        return out

    def start(ins, outs, sems):
        for cp in copies(ins, outs, sems, True):
            cp.start()

    def finish(ins, outs, sems):
        for cp in copies(ins, outs, sems, False):
            cp.wait_recv()
        for cp in copies(ins, outs, sems, True):
            cp.wait_send()

    return _Comm(sums, [jax.ShapeDtypeStruct(s.shape, s.dtype) for s in sums], [3 * n, 3 * n], start, finish)


def _chip_sum(g, got, landed, idx, *, name):
    ns, R, C = g.shape
    hr = R // 2

    def body(i_ref, g_ref, r_ref, a_ref, b_ref, c_ref, o_ref):
        acc = g_ref[...] + r_ref[...]
        for ref in (a_ref, b_ref, c_ref):
            acc = acc + ref[...].astype(F32)
        o_ref[...] = acc

    other = lambda k: pl.BlockSpec((None, hr, C), lambda s, ir: (ir[2 + k], 0, 0))
    return pl.pallas_call(
        body, name=name,
        grid_spec=pltpu.PrefetchScalarGridSpec(
            num_scalar_prefetch=1, grid=(1,),
            in_specs=[pl.BlockSpec((None, hr, C), lambda s, ir: (ir[0], ir[1], 0)),
                      pl.BlockSpec((None, hr, C), lambda s, ir: (ir[0], 0, 0)),
                      other(0), other(1), other(2)],
            out_specs=pl.BlockSpec((hr, C), lambda s, ir: (ir[1], 0))),
        out_shape=jax.ShapeDtypeStruct((R, C), F32),
        compiler_params=_params("arbitrary"),
    )(idx, g, got, landed, landed, landed)


def _share_halves(halves):
    n = len(halves)

    def copies(outs, sems, sending):
        x, y, c, _ = _place()
        out = []
        for t in range(n):
            hr = halves[t].shape[0] // 2
            rows = pl.ds(pl.multiple_of((c if sending else 1 - c) * hr, 8), hr)
            out.append(pltpu.make_async_remote_copy(
                src_ref=outs[t].at[rows, :], dst_ref=outs[t].at[rows, :], send_sem=sems[0].at[t],
                recv_sem=sems[1].at[t], device_id=(x, y, 1 - c) if sending else (x, y, c), device_id_type=MESH))
        return out

    def start(ins, outs, sems):
        for cp in copies(outs, sems, True):
            cp.start()

    def finish(ins, outs, sems):
        for cp in copies(outs, sems, False):
            cp.wait_recv()
        for cp in copies(outs, sems, True):
            cp.wait_send()

    return _Comm(halves, [jax.ShapeDtypeStruct(h.shape, h.dtype) for h in halves], [n, n], start, finish,
                 aliases={t: t for t in range(n)})


def _adamw(w, g, m, v, *, name):
    R, C = w.shape
    tr = R
    for cand in (256, 128, 64, 32, 16, 8):
        if R % cand == 0:
            tr = cand
            break

    def body(w_ref, g_ref, m_ref, v_ref, d_ref, mo_ref, vo_ref):
        gv = g_ref[...]
        mn = ADAM_B1 * m_ref[...] + (1.0 - ADAM_B1) * gv
        vn = ADAM_B2 * v_ref[...] + (1.0 - ADAM_B2) * (gv * gv)
        m_hat = mn / (1.0 - ADAM_B1 ** ADAM_STEP)
        v_hat = vn / (1.0 - ADAM_B2 ** ADAM_STEP)
        d_ref[...] = -ADAM_LR * (m_hat / (jnp.sqrt(v_hat) + ADAM_EPS) + ADAM_WD * w_ref[...])
        mo_ref[...] = mn
        vo_ref[...] = vn

    blk = pl.BlockSpec((tr, C), lambda i: (i, 0))
    return pl.pallas_call(
        body, name=name, grid=(R // tr,), in_specs=[blk] * 4, out_specs=[blk] * 3,
        out_shape=[jax.ShapeDtypeStruct((R, C), F32)] * 3,
        compiler_params=_params("arbitrary"),
    )(w, g, m, v)


def _pack_small(D, meta, n1, nm, n3, nf, gc, ga, bf, cw):
    def row(a):
        a = a.reshape(-1, a.shape[-1])
        return jnp.pad(a, ((0, 0), (0, D - a.shape[-1])))
    rows = [row(meta), row(n1), row(nm), row(n3), row(nf), row(jnp.concatenate([gc, ga], axis=-1)), row(bf), row(cw)]
    slab = jnp.concatenate(rows, axis=0)
    return jnp.pad(slab, ((0, SMALL_ROWS - slab.shape[0]), (0, 0)))


def _unpack_small(slab, like):
    meta, n1, nm, n3, nf, gc, ga, bf, cw = like
    nmeta, mc = meta.shape
    out = [slab[:nmeta, :mc].reshape(meta.shape)]
    r = nmeta
    for a in (n1, nm, n3, nf):
        out.append(slab[r, :a.shape[-1]].reshape(a.shape))
        r += 1
    cd = gc.shape[-1]
    out.append(slab[r, :cd].reshape(gc.shape))
    out.append(slab[r, cd:cd + ga.shape[-1]].reshape(ga.shape))
    r += 1
    out.append(slab[r, :bf.shape[-1]].reshape(bf.shape))
    r += 1
    out.append(slab[r:r + 3, :cw.shape[-1]].reshape(cw.shape))
    return out


def kernel(x, meta_tokens, ffn1_norm, ffn1_w_gu, ffn1_w_down, mix_norm, w_in, conv_w, b_f, out_norm_conv, out_norm_attn, w_out, ffn2_norm, ffn2_w_gu, ffn2_w_down, final_norm, loss_target, m_meta_tokens, m_ffn1_norm, m_ffn1_w_gu, m_ffn1_w_down, m_mix_norm, m_w_in, m_conv_w, m_b_f, m_out_norm_conv, m_out_norm_attn, m_w_out, m_ffn2_norm, m_ffn2_w_gu, m_ffn2_w_down, m_final_norm, v_meta_tokens, v_ffn1_norm, v_ffn1_w_gu, v_ffn1_w_down, v_mix_norm, v_w_in, v_conv_w, v_b_f, v_out_norm_conv, v_out_norm_attn, v_w_out, v_ffn2_norm, v_ffn2_w_gu, v_ffn2_w_down, v_final_norm):
    B, S, D = x.shape
    L = S + N_META
    T = B * L
    tm = L // 3
    assert tm * 3 == L and tm % HALO == 0
    guc = ffn1_w_gu.shape[-1]
    ff = N_SHARD * guc // 2
    H = b_f.shape[-1]
    AD = H * HEAD_DIM
    CD = conv_w.shape[-1] * N_SHARD
    assert CD == AD and CD + AD == D and CD % LANES == 0
    n_main = 3 * CD + 3 * AD
    ins = w_in.shape[-1]

    xi, yi, ci = lax.axis_index("x"), lax.axis_index("y"), lax.axis_index("c")
    chip = 2 * xi + yi

    small_shard = jnp.zeros((2 * HALO, meta_tokens.shape[-1]), F32)
    small_shard = small_shard.at[:N_META].set(meta_tokens)
    small_shard = small_shard.at[N_META:N_META + 3, :conv_w.shape[-1]].set(conv_w[0])
    big = [ffn1_w_gu[0], ffn1_w_down[0], w_in[0], w_out[0], ffn2_w_gu[0], ffn2_w_down[0]]
    wgu1_s, wd1_s, win_s, wout_s, wgu2_s, wd2_s = [w.astype(BF16) for w in big]
    small_g, = _all_gather_shards([small_shard], name="gather_small")
    meta_f = jnp.moveaxis(small_g[:, :N_META], 0, 1).reshape(N_META, D)
    cw_f = jnp.moveaxis(small_g[:, N_META:N_META + 3, :conv_w.shape[-1]], 0, 1).reshape(3, CD)
    cw8 = jnp.pad(cw_f, ((0, 5), (0, 0)))
    bf_p = jnp.pad(b_f, ((0, 0), (0, LANES - H)))
    gid = jnp.arange(CD) // HEAD_DIM
    pmat = jnp.where(gid[:, None] == gid[None, :], 1.0 / HEAD_DIM, 0.0).astype(BF16)

    gu_shape = jax.ShapeDtypeStruct((2, T, ff), BF16)
    gu_w_spec = pl.BlockSpec((None, D, guc), lambda s, i: (s, 0, 0))
    gu_o_spec = pl.BlockSpec((None, tm, guc), lambda s, i: (s // 2, i, s % 2))

    h0 = jnp.concatenate([jnp.broadcast_to(meta_f[None], (B, N_META, D)), x], axis=1).reshape(T, D)
    sid = ((chip + jnp.arange(N_SHARD, dtype=jnp.int32)) % N_SHARD).astype(jnp.int32)
    n1, wgu1_h = _rmsnorm(h0, ffn1_norm, tm=tm, name="ffn1_norm", comm=_gather_ici([wgu1_s]))
    gu1, out = _ffn_up(n1, wgu1_s[None], sid, None, tm=tm, first=0, count=1, name="ffn1_up_own",
                       comm=_join(_gather_d2d(wgu1_h), _gather_ici([wd1_s])))
    wgu1, wd1_h = out[0], out[1:]
    gu1, out = _ffn_up(n1, wgu1, sid, gu1, tm=tm, first=1, count=N_SHARD - 1, name="ffn1_up_rest",
                       comm=_join(_gather_d2d(wd1_h), _gather_ici([win_s, wout_s])))
    wd1, mix_w = out[0].reshape(ff, D), out[1:]
    (h1, n2), (win_g, wout_g) = _ffn_down(gu1, wd1, h0, mix_norm, tm=tm, name="ffn1_down", comm=_gather_d2d(mix_w))
    wout_f = wout_g.reshape(D, D)
    win_f = jnp.moveaxis(win_g, 0, 1).reshape(D, N_SHARD * ins)
    win_main = win_f[:, :n_main]
    win_fg = jnp.pad(win_f[:, n_main:], ((0, 0), (0, LANES - H)))

    proj, _ = _matmul_nn(n2, win_main, tm=tm, nb=n_main // (3 * CD),
                         w_spec=pl.BlockSpec((D, 3 * CD), lambda s, i: (0, s)),
                         out_shape=jax.ShapeDtypeStruct((T, n_main), BF16),
                         out_spec=pl.BlockSpec((tm, 3 * CD), lambda s, i: (i, s)), name="mix_in")
    fg, _ = _matmul_nn(n2, win_fg, tm=tm, nb=1, w_spec=pl.BlockSpec((D, LANES), lambda s, i: (0, 0)),
                       out_shape=jax.ShapeDtypeStruct((T, LANES), F32),
                       out_spec=pl.BlockSpec((tm, LANES), lambda s, i: (i, 0)), name="mix_in_fg")
    proj3 = proj.reshape(B, L, n_main)
    fg3 = fg.reshape(B, L, LANES)
    fc = _fcum(fg3, bf_p, ch=tm, name="forget_cumsum")
    fr = fc[:, :, :H].reshape(B, L // tm, tm, H).transpose(0, 1, 3, 2)
    (o, lse), ffn2_w = _attn_fwd(proj3, fc, fr, tq=tm, n_heads=H, name="attn_fwd",
                                 comm=_gather_ici([wgu2_s, wd2_s]))
    (h2, ymix, n3), (wgu2, wd2) = _mix_out(
        proj3, o, cw8, out_norm_conv, out_norm_attn, wout_f, h1.reshape(B, L, D), pmat, ffn2_norm,
        tm=tm, name="mix_out", comm=_gather_d2d(ffn2_w))
    wd2 = wd2.reshape(ff, D)
    h2 = h2.reshape(T, D)
    n3 = n3.reshape(T, D)

    gu2, _ = _matmul_nn(n3, wgu2, tm=tm, nb=N_SHARD, w_spec=gu_w_spec, out_shape=gu_shape, out_spec=gu_o_spec,
                        name="ffn2_up")
    (h3, _), _ = _ffn_down(gu2, wd2, h2, final_norm.reshape(1, D), tm=tm, name="ffn2_down")

    tgt = jnp.pad(loss_target, ((0, 0), (N_META, 0), (0, 0)))
    dh3, dh3b, d_gf, loss_part = _final(h3.reshape(B, L, D), final_norm.reshape(1, D), tgt, tm=tm, name="final")

    c_arr = jnp.reshape(ci, (1,)).astype(jnp.int32)
    ks = jnp.arange(N_SHARD - 1, dtype=jnp.int32)
    idx = jnp.concatenate([jnp.stack([chip, ci]).astype(jnp.int32), ks + (ks >= chip).astype(jnp.int32)])

    def pair_sums(grads, got, names):
        return [_pair_sum(g, r, c_arr, name="pair_sum_" + nm) for g, r, nm in zip(grads, got, names)]

    def chip_sums(grads, got, landed, names):
        return [_chip_sum(g, r, l, idx, name="chip_sum_" + nm) for g, r, l, nm in zip(grads, got, landed, names)]

    def dw_down(gu, dhb, name, comm=None):
        return _matmul_tn(
            [gu, gu], dhb, tm=tm, nb=ff // guc, kb=guc, silu=True,
            x_specs=[pl.BlockSpec((None, tm, guc), lambda j, i: (0, i, j)),
                     pl.BlockSpec((None, tm, guc), lambda j, i: (1, i, j))],
            y_spec=pl.BlockSpec((tm, D), lambda j, i: (i, 0)),
            out_shape=jax.ShapeDtypeStruct((ff, D), F32), out_spec=pl.BlockSpec((guc, D), lambda j, i: (j, 0)),
            name=name, comm=comm)

    def dw_up(n, dgu, name, comm=None):
        return _matmul_tn(
            [n], dgu, tm=tm, nb=N_SHARD, kb=D, silu=False,
            x_specs=[pl.BlockSpec((tm, D), lambda s, i: (i, 0))],
            y_spec=pl.BlockSpec((None, tm, guc), lambda s, i: (s // 2, i, s % 2)),
            out_shape=jax.ShapeDtypeStruct((N_SHARD, D, guc), F32),
            out_spec=pl.BlockSpec((None, D, guc), lambda s, i: (s, 0, 0)), name=name, comm=comm)

    dh3f, dh3b = dh3.reshape(T, D), dh3b.reshape(T, D)
    dgu2, _ = _ffn_bwd_act(dh3b, gu2, wd2, tm=tm, guc=guc, name="ffn2_bwd_act")
    (dh2, dh2b, d_g3), _ = _ffn_bwd_in(dgu2, wgu2, h2, ffn2_norm, dh3f, tm=tm, scale=1.0, name="ffn2_bwd_in")
    d_wd2, _ = dw_down(gu2, dh3b, "ffn2_dw_down")
    d_wgu2, _ = dw_up(n3, dgu2, "ffn2_dw_up")
    grads_f2 = [d_wgu2, d_wd2.reshape(N_SHARD, ff // N_SHARD, D)]
    names_f2 = ["wgu2", "wd2"]

    dh2b3 = dh2b.reshape(B, L, D)
    (d_bg, d_cv, d_o, d_gc, d_ga, d_cw), got_f2 = _mix_out_bwd(
        dh2b3, proj3, o, cw8, out_norm_conv, out_norm_attn, wout_f, pmat, tm=tm, name="mix_out_bwd",
        comm=_swap_halves(grads_f2))
    sums_f2 = pair_sums(grads_f2, got_f2, names_f2)
    d_wout, _ = _matmul_tn(
        [ymix.reshape(T, D)], dh2b, tm=tm, nb=1, kb=D, silu=False,
        x_specs=[pl.BlockSpec((tm, D), lambda s, i: (i, 0))], y_spec=pl.BlockSpec((tm, D), lambda s, i: (i, 0)),
        out_shape=jax.ShapeDtypeStruct((D, D), F32), out_spec=pl.BlockSpec((D, D), lambda s, i: (0, 0)),
        name="dw_out")
    d_cc = _conv_bwd(d_cv, proj3, cw8, tm=tm, name="conv_bwd")
    (d_q, d_k, d_v, d_fr, d_fq), landed_f2 = _attn_bwd(proj3, o, d_o, lse, fc, fr, tq=tm, n_heads=H, name="attn_bwd",
                                                       comm=_scatter_chips(sums_f2))
    halves_f2 = chip_sums(grads_f2, got_f2, landed_f2, names_f2)
    d_fc = d_fq + jnp.pad(d_fr.transpose(0, 1, 3, 2).reshape(B, L, H), ((0, 0), (0, 0), (0, LANES - H)))
    d_fg, d_bf = _fcum_bwd(d_fc, fg3, bf_p, ch=tm, name="forget_cumsum_bwd")

    parts = [d_bg.reshape(T, CD), d_cc.reshape(T, 2 * CD), d_q.reshape(T, AD), d_k.reshape(T, AD),
             d_v.reshape(T, AD), d_fg.reshape(T, LANES)]
    (dh1, dh1b, d_gm), g_f2 = _mix_bwd_in(parts, win_main, win_fg, h1, mix_norm, dh2, tm=tm, scale=0.5,
                                          name="mix_bwd_in", comm=_share_halves(halves_f2))
    d_win_parts = []
    for k, p in enumerate(parts):
        wdt = p.shape[1]
        nb = max(wdt // CD, 1)
        bw = wdt // nb
        d_win_parts.append(_matmul_tn(
            [n2], p, tm=tm, nb=nb, kb=D, silu=False,
            x_specs=[pl.BlockSpec((tm, D), lambda s, i: (i, 0))], y_spec=pl.BlockSpec((tm, bw), lambda s, i: (i, s)),
            out_shape=jax.ShapeDtypeStruct((D, wdt), F32), out_spec=pl.BlockSpec((D, bw), lambda s, i: (0, s)),
            name="dw_in_%d" % k)[0])
    d_win_parts[-1] = d_win_parts[-1][:, :H]
    d_win = jnp.moveaxis(jnp.concatenate(d_win_parts, axis=1).reshape(D, N_SHARD, ins), 1, 0)
    grads_mx = [d_win, d_wout.reshape(N_SHARD, D // N_SHARD, D)]
    names_mx = ["win", "wout"]

    dgu1, got_mx = _ffn_bwd_act(dh1b, gu1, wd1, tm=tm, guc=guc, name="ffn1_bwd_act", comm=_swap_halves(grads_mx))
    sums_mx = pair_sums(grads_mx, got_mx, names_mx)
    d_wd1, landed_mx = dw_down(gu1, dh1b, "ffn1_dw_down", comm=_scatter_chips(sums_mx))
    halves_mx = chip_sums(grads_mx, got_mx, landed_mx, names_mx)
    grads_d1 = [d_wd1.reshape(N_SHARD, ff // N_SHARD, D)]
    d_wgu1, out = dw_up(n1, dgu1, "ffn1_dw_up", comm=_join(_share_halves(halves_mx), _swap_halves(grads_d1)))
    g_mx, got_d1 = out[:2], out[2:]
    sums_d1 = pair_sums(grads_d1, got_d1, ["wd1"])
    grads_u1 = [d_wgu1]
    (dh0, _, d_g1), out = _ffn_bwd_in(dgu1, wgu1, h0, ffn1_norm, dh1, tm=tm, scale=1.0, name="ffn1_bwd_in",
                                      comm=_join(_scatter_chips(sums_d1), _swap_halves(grads_u1)))
    landed_d1, got_u1 = out[:1], out[1:]
    halves_d1 = chip_sums(grads_d1, got_d1, landed_d1, ["wd1"])
    sums_u1 = pair_sums(grads_u1, got_u1, ["wgu1"])
    out = _run_comm(_join(_share_halves(halves_d1), _scatter_chips(sums_u1)), name="scatter_ffn1")
    g_d1, landed_u1 = out[:1], out[1:]
    halves_u1 = chip_sums(grads_u1, got_u1, landed_u1, ["wgu1"])
    g_u1 = _run_comm(_share_halves(halves_u1), name="share_ffn1")
    g_big = [g_u1[0], g_d1[0], g_mx[0], g_mx[1], g_f2[0], g_f2[1]]
    dh0 = dh0.reshape(B, L, D)
    grad_x = dh0[:, N_META:]
    d_meta = jnp.sum(dh0[:, :N_META], axis=0)

    loss_row = jnp.zeros((1, D), F32).at[0, 0].set(loss_part[0, 0])
    slab = _pack_small(D, d_meta, d_g1, d_gm, d_g3, d_gf, d_gc, d_ga, d_bf[:, :H], d_cw[:3])
    slab = slab.at[SMALL_ROWS - 1].set(loss_row[0])
    total = _all_reduce_small(slab, name="reduce_small")
    loss = total[SMALL_ROWS - 1, 0]
    mcols = meta_tokens.shape[-1]
    ccols = conv_w.shape[-1]
    full_like = (jnp.zeros((N_META, D)), ffn1_norm, mix_norm, ffn2_norm, final_norm.reshape(1, D), out_norm_conv,
                 out_norm_attn, b_f, jnp.zeros((1, 3, CD)))
    g_small = _unpack_small(total, full_like)
    g_small[0] = lax.dynamic_slice_in_dim(g_small[0], chip * mcols, mcols, axis=1)
    g_small[8] = lax.dynamic_slice_in_dim(g_small[8], chip * ccols, ccols, axis=2)

    def small_slab(meta, a1, am, a3, af, gc, ga, bf, cw):
        return _pack_small(D, meta, a1, am, a3, af.reshape(1, D), gc, ga, bf, cw[0])

    w_small = small_slab(meta_tokens, ffn1_norm, mix_norm, ffn2_norm, final_norm, out_norm_conv, out_norm_attn, b_f, conv_w)
    m_small = small_slab(m_meta_tokens, m_ffn1_norm, m_mix_norm, m_ffn2_norm, m_final_norm, m_out_norm_conv,
                         m_out_norm_attn, m_b_f, m_conv_w)
    v_small = small_slab(v_meta_tokens, v_ffn1_norm, v_mix_norm, v_ffn2_norm, v_final_norm, v_out_norm_conv,
                         v_out_norm_attn, v_b_f, v_conv_w)
    gs = list(g_small)
    gs[4] = gs[4].reshape(final_norm.shape)
    g_slab = small_slab(gs[0], gs[1], gs[2], gs[3], gs[4], gs[5], gs[6], gs[7], gs[8])
    local_like = (meta_tokens, ffn1_norm, mix_norm, ffn2_norm, final_norm.reshape(1, D), out_norm_conv, out_norm_attn,
                  b_f, conv_w)
    small_out = [_unpack_small(s, local_like) for s in _adamw(w_small, g_slab, m_small, v_small, name="adamw_small")]
    for lst in small_out:
        lst[4] = lst[4].reshape(final_norm.shape)

    names = ["wgu1", "wd1", "win", "wout", "wgu2", "wd2"]
    w_big = big
    m_big = [m_ffn1_w_gu[0], m_ffn1_w_down[0], m_w_in[0], m_w_out[0], m_ffn2_w_gu[0], m_ffn2_w_down[0]]
    v_big = [v_ffn1_w_gu[0], v_ffn1_w_down[0], v_w_in[0], v_w_out[0], v_ffn2_w_gu[0], v_ffn2_w_down[0]]
    big_out = [_adamw(w, g, m, v, name="adamw_" + nm) for w, g, m, v, nm in zip(w_big, g_big, m_big, v_big, names)]

    def assemble(small, bigs):
        meta, a1, am, a3, af, gc, ga, bf, cw = small
        gu1_, d1_, win_, wout_, gu2_, d2_ = [b[None] for b in bigs]
        return [meta, a1, gu1_, d1_, am, win_, cw, bf, gc, ga, wout_, a3, gu2_, d2_, af]

    gs_out = list(g_small)
    gs_out[4] = gs_out[4].reshape(final_norm.shape)
    grads_out = assemble(gs_out, g_big)
    delta_out = assemble(small_out[0], [b[0] for b in big_out])
    m_out = assemble(small_out[1], [b[1] for b in big_out])
    v_out = assemble(small_out[2], [b[2] for b in big_out])
    return (loss, grad_x, *grads_out, *delta_out, *m_out, *v_out)
```

```python
import functools

import jax
import jax.numpy as jnp
from jax import lax
from jax.experimental import pallas as pl
from jax.experimental.pallas import tpu as pltpu

F32 = jnp.float32
BF16 = jnp.bfloat16

EPS = 1e-6
N_META = 16
HEAD_DIM = 64
N_SHARD = 4
N_DEV = 8
HALO = 16
LANES = 128
SMALL_ROWS = 32
VMEM_LIMIT_V7X = 56 * 1024 * 1024
NEG = -1e30
ATTN_BANDS = 2

ADAM_LR = 0.001
ADAM_B1 = 0.9
ADAM_B2 = 0.999
ADAM_EPS = 1e-08
ADAM_WD = 0.01
ADAM_STEP = 10

MESH = pl.DeviceIdType.MESH
ANY = pl.BlockSpec(memory_space=pl.ANY)
NT_DIMS = (((1,), (1,)), ((), ()))
TN_DIMS = (((0,), (0,)), ((), ()))


def _params(*sem):
    return pltpu.CompilerParams(dimension_semantics=sem, vmem_limit_bytes=VMEM_LIMIT_V7X)


class _Comm:
    def __init__(self, ins, out_shapes, sems, start, finish, aliases=None):
        self.ins, self.out_shapes, self.sems = list(ins), list(out_shapes), list(sems)
        self.start, self.finish, self.aliases = start, finish, dict(aliases or {})


def _join(a, b):
    ni, no, ns = len(a.ins), len(a.out_shapes), len(a.sems)

    def start(ins, outs, sems):
        a.start(ins[:ni], outs[:no], sems[:ns])
        b.start(ins[ni:], outs[no:], sems[ns:])

    def finish(ins, outs, sems):
        a.finish(ins[:ni], outs[:no], sems[:ns])
        b.finish(ins[ni:], outs[no:], sems[ns:])

    aliases = dict(a.aliases)
    aliases.update({ni + i: no + j for i, j in b.aliases.items()})
    return _Comm(a.ins + b.ins, a.out_shapes + b.out_shapes, a.sems + b.sems, start, finish, aliases)


def _launch(body, *, name, grid, in_specs, out_specs, out_shape, args, scratch_shapes=(), comm=None, prefetch=(),
            aliases=None):
    single = not isinstance(out_shape, (list, tuple))
    out_specs = [out_specs] if single else list(out_specs)
    out_shape = [out_shape] if single else list(out_shape)
    in_specs, scratch_shapes, prefetch = list(in_specs), list(scratch_shapes), list(prefetch)
    params = _params(*(("arbitrary",) * len(grid)))
    n_pf, n_in, n_out, n_scr = len(prefetch), len(in_specs), len(out_specs), len(scratch_shapes)
    c_ins = comm.ins if comm else []
    c_shapes = comm.out_shapes if comm else []
    c_sems = comm.sems if comm else []
    c_in, c_out = len(c_ins), len(c_shapes)

    def carrier(*refs):
        p = 0
        pf = refs[p:p + n_pf]; p += n_pf
        a = refs[p:p + n_in]; p += n_in
        ci = refs[p:p + c_in]; p += c_in
        o = refs[p:p + n_out]; p += n_out
        co = refs[p:p + c_out]; p += c_out
        s = refs[p:p + n_scr]; p += n_scr
        cs = refs[p:]
        if comm:
            first = functools.reduce(lambda u, v: u & v, [pl.program_id(k) == 0 for k in range(len(grid))])

            @pl.when(first)
            def _():
                comm.start(ci, co, cs)

        body(*pf, *a, *o, *s)

        if comm:
            last = functools.reduce(lambda u, v: u & v, [pl.program_id(k) == grid[k] - 1 for k in range(len(grid))])

            @pl.when(last)
            def _():
                comm.finish(ci, co, cs)

    io_aliases = {n_pf + i: j for i, j in (aliases or {}).items()}
    if comm:
        io_aliases.update({n_pf + n_in + i: n_out + j for i, j in comm.aliases.items()})
    all_in, all_out = in_specs + [ANY] * c_in, out_specs + [ANY] * c_out
    all_scratch = scratch_shapes + [pltpu.SemaphoreType.DMA((k,)) for k in c_sems]
    if n_pf:
        spec = dict(grid_spec=pltpu.PrefetchScalarGridSpec(
            num_scalar_prefetch=n_pf, grid=grid, in_specs=all_in, out_specs=all_out, scratch_shapes=all_scratch))
    else:
        spec = dict(grid=grid, in_specs=all_in, out_specs=all_out, scratch_shapes=all_scratch)
    res = pl.pallas_call(carrier, name=name, out_shape=out_shape + c_shapes, input_output_aliases=io_aliases,
                         compiler_params=params, **spec)(*prefetch, *args, *c_ins)
    main = list(res[:n_out])
    return (main[0] if single else main), (list(res[n_out:]) if comm else None)


def _run_comm(comm, *, name):
    c_in, c_out = len(comm.ins), len(comm.out_shapes)

    def body(*refs):
        ci, co, cs = refs[:c_in], refs[c_in:c_in + c_out], refs[c_in + c_out:]
        comm.start(ci, co, cs)
        comm.finish(ci, co, cs)

    return list(pl.pallas_call(
        body, name=name, in_specs=[ANY] * c_in, out_specs=[ANY] * c_out, out_shape=comm.out_shapes,
        scratch_shapes=[pltpu.SemaphoreType.DMA((k,)) for k in comm.sems],
        input_output_aliases=comm.aliases)(*comm.ins))


def _chunks(width, step=512):
    out, c0 = [], 0
    while c0 < width:
        cw = min(step, width - c0)
        out.append((c0, cw))
        c0 += cw
    return out


def _split2(v):
    hi = v.astype(BF16)
    lo = (v - hi.astype(F32)).astype(BF16)
    return hi, lo


def _split3(v):
    hi = v.astype(BF16)
    r = v - hi.astype(F32)
    mid = r.astype(BF16)
    lo = (r - mid.astype(F32)).astype(BF16)
    return hi, mid, lo


def _dot(a, b):
    return jnp.dot(a, b, preferred_element_type=F32)


def _dot_nt(a, b):
    return lax.dot_general(a, b, NT_DIMS, preferred_element_type=F32)


def _dot_tn(a, b):
    return lax.dot_general(a, b, TN_DIMS, preferred_element_type=F32)


def _silu_mul(g, u):
    return g * jax.nn.sigmoid(g) * u


def _rms_bwd(dn, h, gain, dres):
    r = lax.rsqrt(jnp.mean(h * h, axis=-1, keepdims=True) + EPS)
    y = h * r
    dgain = jnp.sum(dn * y, axis=0, keepdims=True)
    dy = dn * gain
    dh = dres + r * (dy - y * jnp.mean(dy * y, axis=-1, keepdims=True))
    return dh, dgain


def _group_mean(v, p):
    hi, lo = _split2(v)
    return _dot(hi, p) + _dot(lo, p)


def _row_of(a, k):
    rows = lax.broadcasted_iota(jnp.int32, a.shape, 0)
    return jnp.sum(jnp.where(rows == k, a, 0.0), axis=0, keepdims=True)


def _causal_conv(u, prev, w):
    rows = lax.broadcasted_iota(jnp.int32, u.shape, 0)
    p1 = _row_of(prev, HALO - 1)
    p2 = _row_of(prev, HALO - 2)
    u1 = jnp.where(rows == 0, p1, pltpu.roll(u, 1, 0))
    u2 = jnp.where(rows == 0, p2, jnp.where(rows == 1, p1, pltpu.roll(u, 2, 0)))
    return w[2:3, :] * u + w[1:2, :] * u1 + w[0:1, :] * u2, u1, u2


def _rms(x, gain):
    return (x * lax.rsqrt(jnp.mean(x * x, axis=-1, keepdims=True) + EPS) * gain).astype(BF16)


def _rmsnorm(h, g, *, tm, name, comm=None):
    T, D = h.shape

    def body(h_ref, g_ref, n_ref):
        n_ref[...] = _rms(h_ref[...], g_ref[...])

    return _launch(
        body, name=name, grid=(T // tm,),
        in_specs=[pl.BlockSpec((tm, D), lambda i: (i, 0)), pl.BlockSpec((1, D), lambda i: (0, 0))],
        out_specs=pl.BlockSpec((tm, D), lambda i: (i, 0)),
        out_shape=jax.ShapeDtypeStruct((T, D), BF16), args=(h, g), comm=comm)


def _ffn_up(n, wgu, sid, gu_prev, *, tm, first, count, name, comm=None):
    T, D = n.shape
    ns, _, guc = wgu.shape
    ff = N_SHARD * guc // 2

    def body(sid_ref, x_ref, w_ref, *rest):
        rest[-1][...] = _dot(x_ref[...], w_ref[...]).astype(BF16)

    where = lambda s, sid: sid[first + s]
    w_at = (lambda s, sid: 0) if ns == 1 else where
    return _launch(
        body, name=name, grid=(count, T // tm), prefetch=(sid,),
        in_specs=[pl.BlockSpec((tm, D), lambda s, i, sid: (i, 0)),
                  pl.BlockSpec((None, D, guc), lambda s, i, sid: (w_at(s, sid), 0, 0))]
                 + ([] if gu_prev is None else [ANY]),
        out_specs=pl.BlockSpec((None, tm, guc), lambda s, i, sid: (where(s, sid) // 2, i, where(s, sid) % 2)),
        out_shape=jax.ShapeDtypeStruct((2, T, ff), BF16),
        args=(n, wgu) + (() if gu_prev is None else (gu_prev,)),
        aliases=None if gu_prev is None else {2: 0}, comm=comm)


def _matmul_nn(x, w, *, tm, nb, w_spec, out_shape, out_spec, name, comm=None):
    T, K = x.shape

    def body(x_ref, w_ref, o_ref):
        o_ref[...] = _dot(x_ref[...], w_ref[...]).astype(o_ref.dtype)

    return _launch(
        body, name=name, grid=(nb, T // tm),
        in_specs=[pl.BlockSpec((tm, K), lambda s, i: (i, 0)), w_spec],
        out_specs=out_spec, out_shape=out_shape, args=(x, w), comm=comm)


def _ffn_down(gu, wd, h, next_gain, *, tm, name, comm=None):
    _, T, ff = gu.shape
    D = h.shape[1]
    chunks = _chunks(ff)

    def body(g_ref, u_ref, wd_hbm, h_ref, ng_ref, o_ref, n_ref, wd_v, sem):
        @pl.when(pl.program_id(0) == 0)
        def _():
            cp = pltpu.make_async_copy(wd_hbm, wd_v, sem)
            cp.start()
            cp.wait()

        acc = jnp.zeros((tm, D), F32)
        for c0, cw in chunks:
            a = _silu_mul(g_ref[:, c0:c0 + cw].astype(F32), u_ref[:, c0:c0 + cw].astype(F32))
            acc = acc + _dot(a.astype(BF16), wd_v[c0:c0 + cw, :])
        out = h_ref[...] + 0.5 * acc
        o_ref[...] = out
        n_ref[...] = _rms(out, ng_ref[...])

    return _launch(
        body, name=name, grid=(T // tm,),
        in_specs=[pl.BlockSpec((None, tm, ff), lambda i: (0, i, 0)),
                  pl.BlockSpec((None, tm, ff), lambda i: (1, i, 0)),
                  ANY,
                  pl.BlockSpec((tm, D), lambda i: (i, 0)),
                  pl.BlockSpec((1, D), lambda i: (0, 0))],
        out_specs=[pl.BlockSpec((tm, D), lambda i: (i, 0)), pl.BlockSpec((tm, D), lambda i: (i, 0))],
        out_shape=[jax.ShapeDtypeStruct((T, D), F32), jax.ShapeDtypeStruct((T, D), BF16)],
        scratch_shapes=[pltpu.VMEM((ff, D), BF16), pltpu.SemaphoreType.DMA],
        args=(gu, gu, wd, h, next_gain), comm=comm)


def _ffn_bwd_act(df, gu, wd, *, tm, guc, name, comm=None):
    _, T, ff = gu.shape
    D = df.shape[1]
    nj = ff // guc
    chunks = _chunks(guc)

    def body(df_ref, g_ref, u_ref, wd_ref, o_ref):
        dfv = df_ref[...]
        for c0, cw in chunks:
            da = _dot_nt(dfv, wd_ref[c0:c0 + cw, :])
            g = g_ref[:, c0:c0 + cw].astype(F32)
            u = u_ref[:, c0:c0 + cw].astype(F32)
            sg = jax.nn.sigmoid(g)
            silu = g * sg
            o_ref[0, :, c0:c0 + cw] = (da * u * (sg * (1.0 + g * (1.0 - sg)))).astype(BF16)
            o_ref[1, :, c0:c0 + cw] = (da * silu).astype(BF16)

    return _launch(
        body, name=name, grid=(nj, T // tm),
        in_specs=[pl.BlockSpec((tm, D), lambda j, i: (i, 0)),
                  pl.BlockSpec((None, tm, guc), lambda j, i: (0, i, j)),
                  pl.BlockSpec((None, tm, guc), lambda j, i: (1, i, j)),
                  pl.BlockSpec((guc, D), lambda j, i: (j, 0))],
        out_specs=pl.BlockSpec((2, tm, guc), lambda j, i: (0, i, j)),
        out_shape=jax.ShapeDtypeStruct((2, T, ff), BF16),
        args=(df, gu, gu, wd), comm=comm)


def _ffn_bwd_in(dgu, wgu, h, g, dres, *, tm, scale, name, comm=None):
    _, T, ff = dgu.shape
    ns, D, guc = wgu.shape
    nj = ff // guc
    chunks = _chunks(guc)

    def body(dgu_ref, w_hbm, h_ref, g_ref, dres_ref, dh_ref, dhb_ref, dg_ref, w_v, acc, sem):
        i, j = pl.program_id(0), pl.program_id(1)

        @pl.when((i == 0) & (j == 0))
        def _():
            cp = pltpu.make_async_copy(w_hbm, w_v, sem)
            cp.start()
            cp.wait()
            dg_ref[...] = jnp.zeros_like(dg_ref)

        part = jnp.zeros((tm, D), F32)
        for c0, cw in chunks:
            part = part + _dot_nt(dgu_ref[0, :, c0:c0 + cw], w_v[j, :, c0:c0 + cw])
            part = part + _dot_nt(dgu_ref[1, :, c0:c0 + cw], w_v[nj + j, :, c0:c0 + cw])

        @pl.when(j == 0)
        def _():
            acc[...] = part

        @pl.when(j > 0)
        def _():
            acc[...] += part

        @pl.when(j == nj - 1)
        def _():
            dh, dgain = _rms_bwd(acc[...], h_ref[...], g_ref[...], dres_ref[...])
            dh_ref[...] = dh
            dhb_ref[...] = (scale * dh).astype(BF16)
            dg_ref[...] += dgain

    return _launch(
        body, name=name, grid=(T // tm, nj),
        in_specs=[pl.BlockSpec((2, tm, guc), lambda i, j: (0, i, j)),
                  ANY,
                  pl.BlockSpec((tm, D), lambda i, j: (i, 0)),
                  pl.BlockSpec((1, D), lambda i, j: (0, 0)),
                  pl.BlockSpec((tm, D), lambda i, j: (i, 0))],
        out_specs=[pl.BlockSpec((tm, D), lambda i, j: (i, 0)),
                   pl.BlockSpec((tm, D), lambda i, j: (i, 0)),
                   pl.BlockSpec((1, D), lambda i, j: (0, 0))],
        out_shape=[jax.ShapeDtypeStruct((T, D), F32), jax.ShapeDtypeStruct((T, D), BF16),
                   jax.ShapeDtypeStruct((1, D), F32)],
        scratch_shapes=[pltpu.VMEM((ns, D, guc), BF16), pltpu.VMEM((tm, D), F32), pltpu.SemaphoreType.DMA],
        args=(dgu, wgu, h, g, dres), comm=comm)


def _mix_bwd_in(parts, w_main, w_fg, h, g, dres, *, tm, scale, name, comm=None):
    T, D = h.shape
    widths = [p.shape[1] for p in parts[:-1]]
    offs = [sum(widths[:k]) for k in range(len(widths))]
    npart = len(parts)

    def body(*refs):
        p_refs = refs[:npart]
        wm_ref, wf_ref, h_ref, g_ref, dres_ref, dh_ref, dhb_ref, dg_ref = refs[npart:]

        @pl.when(pl.program_id(0) == 0)
        def _():
            dg_ref[...] = jnp.zeros_like(dg_ref)

        dn = _dot_nt(p_refs[-1][...].astype(BF16), wf_ref[...])
        for p_ref, off, wd_ in zip(p_refs[:-1], offs, widths):
            for c0, cw in _chunks(wd_):
                dn = dn + _dot_nt(p_ref[:, c0:c0 + cw].astype(BF16), wm_ref[:, off + c0:off + c0 + cw])
        dh, dgain = _rms_bwd(dn, h_ref[...], g_ref[...], dres_ref[...])
        dh_ref[...] = dh
        dhb_ref[...] = (scale * dh).astype(BF16)
        dg_ref[...] += dgain

    row = lambda i: (i, 0)
    const = lambda i: (0, 0)
    return _launch(
        body, name=name, grid=(T // tm,),
        in_specs=[pl.BlockSpec((tm, p.shape[1]), row) for p in parts]
                 + [pl.BlockSpec(w_main.shape, const), pl.BlockSpec(w_fg.shape, const),
                    pl.BlockSpec((tm, D), row), pl.BlockSpec((1, D), const), pl.BlockSpec((tm, D), row)],
        out_specs=[pl.BlockSpec((tm, D), row), pl.BlockSpec((tm, D), row), pl.BlockSpec((1, D), const)],
        out_shape=[jax.ShapeDtypeStruct((T, D), F32), jax.ShapeDtypeStruct((T, D), BF16),
                   jax.ShapeDtypeStruct((1, D), F32)],
        args=(*parts, w_main, w_fg, h, g, dres), comm=comm)


def _matmul_tn(xs, y, *, tm, nb, x_specs, y_spec, out_shape, out_spec, kb, silu, name, comm=None):
    T = y.shape[-2]
    nx = len(xs)
    chunks = _chunks(kb)

    def body(*refs):
        x_refs, y_ref, o_ref = refs[:nx], refs[nx], refs[nx + 1]
        i = pl.program_id(1)

        @pl.when(i == 0)
        def _():
            o_ref[...] = jnp.zeros_like(o_ref)

        yv = y_ref[...].astype(BF16)
        for c0, cw in chunks:
            if silu:
                xv = _silu_mul(x_refs[0][:, c0:c0 + cw].astype(F32), x_refs[1][:, c0:c0 + cw].astype(F32)).astype(BF16)
            else:
                xv = x_refs[0][:, c0:c0 + cw]
            o_ref[c0:c0 + cw, :] += _dot_tn(xv, yv)

    return _launch(
        body, name=name, grid=(nb, T // tm),
        in_specs=list(x_specs) + [y_spec], out_specs=out_spec, out_shape=out_shape,
        args=(*xs, y), comm=comm)


def _tri(n, lower):
    r = lax.broadcasted_iota(jnp.int32, (n, n), 0)
    c = lax.broadcasted_iota(jnp.int32, (n, n), 1)
    return jnp.where((r >= c) if lower else (r <= c), 1.0, 0.0).astype(BF16)


def _tri_dot(tri, v):
    hi, mid, lo = _split3(v)
    return _dot(tri, hi) + _dot(tri, mid) + _dot(tri, lo)


def _fcum(fg, bf, *, ch, name):
    B, L, W = fg.shape
    nch = L // ch

    def body(fg_ref, bf_ref, f_ref):
        tri = _tri(ch, True)
        carry = jnp.zeros((1, W), F32)
        for c in range(nch):
            x = fg_ref[c * ch:(c + 1) * ch, :] + bf_ref[...]
            lf = jnp.minimum(x, 0.0) - jnp.log(1.0 + jnp.exp(-jnp.abs(x)))
            f_ref[c * ch:(c + 1) * ch, :] = _tri_dot(tri, lf) + carry
            carry = carry + jnp.sum(lf, axis=0, keepdims=True)

    return pl.pallas_call(
        body, name=name, grid=(B,),
        in_specs=[pl.BlockSpec((None, L, W), lambda b: (b, 0, 0)), pl.BlockSpec((1, W), lambda b: (0, 0))],
        out_specs=pl.BlockSpec((None, L, W), lambda b: (b, 0, 0)),
        out_shape=jax.ShapeDtypeStruct((B, L, W), F32),
        compiler_params=_params("arbitrary"),
    )(fg, bf)


def _fcum_bwd(dF, fg, bf, *, ch, name):
    B, L, W = fg.shape
    nch = L // ch

    def body(df_ref, fg_ref, bf_ref, dfg_ref, db_ref):
        @pl.when(pl.program_id(0) == 0)
        def _():
            db_ref[...] = jnp.zeros_like(db_ref)

        tri = _tri(ch, False)
        carry = jnp.zeros((1, W), F32)
        dbs = jnp.zeros((1, W), F32)
        for c in reversed(range(nch)):
            d = df_ref[c * ch:(c + 1) * ch, :]
            dlf = _tri_dot(tri, d) + carry
            carry = carry + jnp.sum(d, axis=0, keepdims=True)
            x = fg_ref[c * ch:(c + 1) * ch, :] + bf_ref[...]
            dfg = dlf * jax.nn.sigmoid(-x)
            dfg_ref[c * ch:(c + 1) * ch, :] = dfg.astype(BF16)
            dbs = dbs + jnp.sum(dfg, axis=0, keepdims=True)
        db_ref[...] += dbs

    blk = pl.BlockSpec((None, L, W), lambda b: (b, 0, 0))
    return pl.pallas_call(
        body, name=name, grid=(B,),
        in_specs=[blk, blk, pl.BlockSpec((1, W), lambda b: (0, 0))],
        out_specs=[blk, pl.BlockSpec((1, W), lambda b: (0, 0))],
        out_shape=[jax.ShapeDtypeStruct((B, L, W), BF16), jax.ShapeDtypeStruct((1, W), F32)],
        compiler_params=_params("arbitrary"),
    )(dF, fg, bf)


def _band_edges(tq):
    return sorted({min(tq, (k * tq // ATTN_BANDS + HALO - 1) // HALO * HALO) for k in range(ATTN_BANDS + 1)})


def _pair(h):
    return slice((h // 2) * 2 * HEAD_DIM, (h // 2 + 1) * 2 * HEAD_DIM)


def _own_lanes(a, h):
    low = lax.broadcasted_iota(jnp.int32, a.shape, 1) < HEAD_DIM
    return jnp.where(low if h % 2 == 0 else jnp.logical_not(low), a, jnp.zeros_like(a))


def _attn_fwd(proj, fc, fr, *, tq, n_heads, name, comm=None):
    B, L, _ = proj.shape
    AD = n_heads * HEAD_DIM
    nq = L // tq
    W = fc.shape[-1]
    scale = HEAD_DIM ** -0.5
    edges = _band_edges(tq)

    def body(q_ref, k_ref, v_ref, fr_ref, o_ref, lse_ref, m_s, l_s, acc_s):
        qi, ki = pl.program_id(1), pl.program_id(2)

        @pl.when(ki == 0)
        def _():
            m_s[...] = jnp.full_like(m_s, NEG)
            l_s[...] = jnp.zeros_like(l_s)
            acc_s[...] = jnp.zeros_like(acc_s)

        def tile(diagonal):
            lane = lax.broadcasted_iota(jnp.int32, (tq, W), 1)
            m_all, l_all = m_s[...], l_s[...]
            m_out, l_out = m_all, l_all
            bands = [(r0, r1, r1 if diagonal else tq) for r0, r1 in zip(edges[:-1], edges[1:])]
            if diagonal:
                masks = {r0: (lax.broadcasted_iota(jnp.int32, (r1 - r0, c1), 1)
                              <= r0 + lax.broadcasted_iota(jnp.int32, (r1 - r0, c1), 0)) for r0, r1, c1 in bands}

            def scores(h, band):
                r0, r1, c1 = band
                sl = slice(h * HEAD_DIM, (h + 1) * HEAD_DIM)
                return _dot_nt(q_ref[r0:r1, sl] * scale, k_ref[0:c1, sl])

            work = [(h, band) for h in range(n_heads) for band in bands]
            nxt = scores(*work[0])
            for w, (h, band) in enumerate(work):
                r0, r1, c1 = band
                sl = slice(h * HEAD_DIM, (h + 1) * HEAD_DIM)
                s = nxt - fr_ref[h:h + 1, 0:c1]
                if w + 1 < len(work):
                    nxt = scores(*work[w + 1])
                if diagonal:
                    s = jnp.where(masks[r0], s, NEG)
                m_old = m_all[r0:r1, h:h + 1]
                m_new = jnp.maximum(m_old, jnp.max(s, axis=1, keepdims=True))
                alpha = jnp.exp(m_old - m_new)
                p = jnp.exp(s - m_new)
                l_new = alpha * l_all[r0:r1, h:h + 1] + jnp.sum(p, axis=1, keepdims=True)
                acc_s[r0:r1, sl] = alpha * acc_s[r0:r1, sl] + _dot(p.astype(BF16), v_ref[0:c1, sl])
                if r0 == 0:
                    m_parts, l_parts = [], []
                m_parts.append(m_new)
                l_parts.append(l_new)
                if r1 == tq:
                    m_out = jnp.where(lane == h, jnp.concatenate(m_parts, axis=0), m_out)
                    l_out = jnp.where(lane == h, jnp.concatenate(l_parts, axis=0), l_out)
            m_s[...] = m_out
            l_s[...] = l_out

        @pl.when(ki < qi)
        def _():
            tile(False)

        @pl.when(ki == qi)
        def _():
            tile(True)
            l = l_s[...]
            for h in range(n_heads):
                sl = slice(h * HEAD_DIM, (h + 1) * HEAD_DIM)
                o_ref[:, sl] = acc_s[:, sl] / l[:, h:h + 1]
            lse_ref[...] = jnp.where(l > 0.0, m_s[...] + jnp.log(jnp.where(l > 0.0, l, 1.0)), 0.0)

    kv = lambda b, qi, ki: jnp.minimum(ki, qi)
    return _launch(
        body, name=name, grid=(B, nq, nq), args=(proj, proj, proj, fr), comm=comm,
        in_specs=[pl.BlockSpec((None, tq, AD), lambda b, qi, ki: (b, qi, 3)),
                  pl.BlockSpec((None, tq, AD), lambda b, qi, ki: (b, kv(b, qi, ki), 4)),
                  pl.BlockSpec((None, tq, AD), lambda b, qi, ki: (b, kv(b, qi, ki), 5)),
                  pl.BlockSpec((None, None, n_heads, tq), lambda b, qi, ki: (b, kv(b, qi, ki), 0, 0))],
        out_specs=[pl.BlockSpec((None, tq, AD), lambda b, qi, ki: (b, qi, 0)),
                   pl.BlockSpec((None, tq, W), lambda b, qi, ki: (b, qi, 0))],
        out_shape=[jax.ShapeDtypeStruct((B, L, AD), F32), jax.ShapeDtypeStruct((B, L, W), F32)],
        scratch_shapes=[pltpu.VMEM((tq, W), F32), pltpu.VMEM((tq, W), F32), pltpu.VMEM((tq, AD), F32)])


def _attn_bwd(proj, o, do, lse, fc, fr, *, tq, n_heads, name, comm=None):
    B, L, _ = proj.shape
    AD = n_heads * HEAD_DIM
    nq = L // tq
    W = fc.shape[-1]
    scale = HEAD_DIM ** -0.5
    edges = _band_edges(tq)

    def body(q_ref, k_ref, v_ref, o_ref, do_ref, lse_ref, fr_ref,
             dq_ref, dk_ref, dv_ref, dfr_ref, dfq_ref, dk_s, dv_s):
        kj, qi = pl.program_id(1), pl.program_id(2)

        @pl.when((kj == 0) & (qi == 0))
        def _():
            dq_ref[...] = jnp.zeros_like(dq_ref)
            dfq_ref[...] = jnp.zeros_like(dfq_ref)

        @pl.when(qi == kj)
        def _():
            dk_s[...] = jnp.zeros_like(dk_s)
            dv_s[...] = jnp.zeros_like(dv_s)
            dfr_ref[...] = jnp.zeros_like(dfr_ref)

        def tile(diagonal):
            bands = [(r0, r1, r1) for r0, r1 in zip(edges[:-1], edges[1:])] if diagonal else [(0, tq, tq)]
            lse = lse_ref[...]
            for r0, r1, c1 in bands:
                nr = r1 - r0
                rows = pl.ds(pl.multiple_of(qi * tq + r0, 8), nr)
                if diagonal:
                    mask = (lax.broadcasted_iota(jnp.int32, (nr, c1), 1)
                            <= r0 + lax.broadcasted_iota(jnp.int32, (nr, c1), 0))
                lane = lax.broadcasted_iota(jnp.int32, (nr, W), 1)
                head = lax.broadcasted_iota(jnp.int32, (n_heads, c1), 0)
                dfq = jnp.zeros((nr, W), F32)
                dfr = jnp.zeros((n_heads, c1), F32)
                for h in range(n_heads):
                    ps = _pair(h)
                    k, v = k_ref[0:c1, ps], v_ref[0:c1, ps]
                    q = _own_lanes(q_ref[r0:r1, ps] * scale, h)
                    dov = _own_lanes(do_ref[r0:r1, ps], h)
                    s = _dot_nt(q, k) - fr_ref[h:h + 1, 0:c1]
                    if diagonal:
                        s = jnp.where(mask, s, NEG)
                    p = jnp.exp(s - lse[r0:r1, h:h + 1])
                    dp = _dot_nt(dov, v)
                    dsum = jnp.sum(dov.astype(F32) * o_ref[r0:r1, ps], axis=1, keepdims=True)
                    ds = p * (dp - dsum)
                    dsb = ds.astype(BF16)
                    dv = _dot_tn(p.astype(BF16), dov)
                    dk = _dot_tn(dsb, q)
                    dq = _dot(dsb, _own_lanes(k, h))
                    if h % 2 == 0:
                        dv_even, dk_even, dq_even = dv, dk, dq
                    else:
                        dv_s[0:c1, ps] += dv_even + dv
                        dk_s[0:c1, ps] += dk_even + dk
                        dq_ref[rows, ps] += (dq_even + dq) * scale
                    dfr = jnp.where(head == h, jnp.sum(ds, axis=0, keepdims=True), dfr)
                    dfq = jnp.where(lane == h, jnp.sum(ds, axis=1, keepdims=True), dfq)
                dfr_ref[:, 0:c1] -= dfr
                dfq_ref[rows, :] += dfq

        @pl.when(qi > kj)
        def _():
            tile(False)

        @pl.when(qi == kj)
        def _():
            tile(True)

        @pl.when(qi == nq - 1)
        def _():
            dk_ref[...] = dk_s[...].astype(BF16)
            dv_ref[...] = dv_s[...].astype(BF16)

    qq = lambda b, kj, qi: jnp.maximum(qi, kj)
    qblk = lambda w, cb: pl.BlockSpec((None, tq, w), lambda b, kj, qi: (b, qq(b, kj, qi), cb))
    kblk = lambda cb: pl.BlockSpec((None, tq, AD), lambda b, kj, qi: (b, kj, cb))
    return _launch(
        body, name=name, grid=(B, nq, nq), args=(proj, proj, proj, o, do, lse, fr), comm=comm,
        in_specs=[qblk(AD, 3), kblk(4), kblk(5), qblk(AD, 0), qblk(AD, 0), qblk(W, 0),
                  pl.BlockSpec((None, None, n_heads, tq), lambda b, kj, qi: (b, kj, 0, 0))],
        out_specs=[pl.BlockSpec((None, L, AD), lambda b, kj, qi: (b, 0, 0)),
                   kblk(0), kblk(0),
                   pl.BlockSpec((None, None, n_heads, tq), lambda b, kj, qi: (b, kj, 0, 0)),
                   pl.BlockSpec((None, L, W), lambda b, kj, qi: (b, 0, 0))],
        out_shape=[jax.ShapeDtypeStruct((B, L, AD), F32), jax.ShapeDtypeStruct((B, L, AD), BF16),
                   jax.ShapeDtypeStruct((B, L, AD), BF16), jax.ShapeDtypeStruct((B, nq, n_heads, tq), F32),
                   jax.ShapeDtypeStruct((B, L, W), F32)],
        scratch_shapes=[pltpu.VMEM((tq, AD), F32), pltpu.VMEM((tq, AD), F32)])


def _mix_gather(refs, first):
    b_ref, c_ref, hc_ref, cp_ref, hcp_ref, o_ref, cw_ref, p_ref = refs
    bg = b_ref[...].astype(F32)
    u = c_ref[...].astype(F32) * hc_ref[...].astype(F32)
    prev = cp_ref[...].astype(F32) * hcp_ref[...].astype(F32)
    prev = jnp.where(first, 0.0, prev)
    cv, u1, u2 = _causal_conv(u, prev, cw_ref[...])
    yc = bg * cv
    p = p_ref[...]
    rc = lax.rsqrt(_group_mean(yc * yc, p) + EPS)
    ya = o_ref[...].astype(F32)
    ra = lax.rsqrt(_group_mean(ya * ya, p) + EPS)
    return bg, (u, u1, u2), cv, yc * rc, rc, ya * ra, ra


def _mix_specs(tm, CD, D, grid_rank_fn):
    per = tm // HALO
    cur = lambda cb: pl.BlockSpec((None, tm, CD), lambda b, i: (b, i, cb))
    prev = lambda cb: pl.BlockSpec((None, HALO, CD), lambda b, i: (b, jnp.maximum(i * per - 1, 0), cb))
    return [cur(0), cur(1), cur(2), prev(1), prev(2), cur(0)]


def _mix_out(proj, o, cw, gc, ga, wout, h, pmat, next_gain, *, tm, name, comm=None):
    B, L, D = h.shape
    CD = o.shape[-1]
    const = lambda b, i: (0, 0)

    def body(b_ref, c_ref, hc_ref, cp_ref, hcp_ref, o_ref, cw_ref, p_ref, gc_ref, ga_ref, w_ref, h_ref, ng_ref,
             out_ref, y_ref, n_ref):
        first = pl.program_id(1) == 0
        _, _, _, zc, _, za, _ = _mix_gather((b_ref, c_ref, hc_ref, cp_ref, hcp_ref, o_ref, cw_ref, p_ref), first)
        yc = (zc * gc_ref[...]).astype(BF16)
        ya = (za * ga_ref[...]).astype(BF16)
        y_ref[:, :CD] = yc
        y_ref[:, CD:] = ya
        out = h_ref[...] + _dot(yc, w_ref[:CD, :]) + _dot(ya, w_ref[CD:, :])
        out_ref[...] = out
        n_ref[...] = _rms(out, ng_ref[...])

    tile = pl.BlockSpec((None, tm, D), lambda b, i: (b, i, 0))
    return _launch(
        body, name=name, grid=(B, L // tm),
        in_specs=_mix_specs(tm, CD, D, None)
                 + [pl.BlockSpec(cw.shape, const), pl.BlockSpec(pmat.shape, const),
                    pl.BlockSpec((1, CD), const), pl.BlockSpec((1, CD), const), pl.BlockSpec((D, D), const),
                    tile, pl.BlockSpec((1, D), const)],
        out_specs=[tile, tile, tile],
        out_shape=[jax.ShapeDtypeStruct((B, L, D), F32), jax.ShapeDtypeStruct((B, L, D), BF16),
                   jax.ShapeDtypeStruct((B, L, D), BF16)],
        args=(proj, proj, proj, proj, proj, o, cw, pmat, gc, ga, wout, h, next_gain), comm=comm)


def _mix_out_bwd(dhb, proj, o, cw, gc, ga, wout, pmat, *, tm, name, comm=None):
    B, L, D = dhb.shape
    CD = o.shape[-1]
    const = lambda b, i: (0, 0)

    def body(dh_ref, b_ref, c_ref, hc_ref, cp_ref, hcp_ref, o_ref, cw_ref, p_ref, gc_ref, ga_ref, w_ref,
             db_ref, dcv_ref, do_ref, dgc_ref, dga_ref, dcw_ref):
        first = pl.program_id(1) == 0

        @pl.when((pl.program_id(0) == 0) & first)
        def _():
            dgc_ref[...] = jnp.zeros_like(dgc_ref)
            dga_ref[...] = jnp.zeros_like(dga_ref)
            dcw_ref[...] = jnp.zeros_like(dcw_ref)

        bg, us, cv, zc, rc, za, ra = _mix_gather(
            (b_ref, c_ref, hc_ref, cp_ref, hcp_ref, o_ref, cw_ref, p_ref), first)
        p = p_ref[...]
        dh = dh_ref[...]
        dyc = _dot_nt(dh, w_ref[:CD, :])
        dya = _dot_nt(dh, w_ref[CD:, :])

        dgc_ref[...] += jnp.sum(dyc * zc, axis=0, keepdims=True)
        dz = dyc * gc_ref[...]
        dx = rc * (dz - zc * _group_mean(dz * zc, p))
        db_ref[...] = (dx * cv).astype(BF16)
        dcv = dx * bg
        dcv_ref[...] = dcv.astype(BF16)
        for k in range(3):
            dcw_ref[k:k + 1, :] += jnp.sum(dcv * us[2 - k], axis=0, keepdims=True)

        dga_ref[...] += jnp.sum(dya * za, axis=0, keepdims=True)
        dz = dya * ga_ref[...]
        do_ref[...] = (ra * (dz - za * _group_mean(dz * za, p))).astype(BF16)

    tile = lambda w: pl.BlockSpec((None, tm, w), lambda b, i: (b, i, 0))
    return _launch(
        body, name=name, grid=(B, L // tm), comm=comm,
        args=(dhb, proj, proj, proj, proj, proj, o, cw, pmat, gc, ga, wout),
        in_specs=[tile(D)] + _mix_specs(tm, CD, D, None)
                 + [pl.BlockSpec(cw.shape, const), pl.BlockSpec(pmat.shape, const),
                    pl.BlockSpec((1, CD), const), pl.BlockSpec((1, CD), const), pl.BlockSpec((D, D), const)],
        out_specs=[tile(CD), tile(CD), tile(CD),
                   pl.BlockSpec((1, CD), const), pl.BlockSpec((1, CD), const), pl.BlockSpec((8, CD), const)],
        out_shape=[jax.ShapeDtypeStruct((B, L, CD), BF16)] * 3
                  + [jax.ShapeDtypeStruct((1, CD), F32)] * 2 + [jax.ShapeDtypeStruct((8, CD), F32)])


def _conv_bwd(dcv, proj, cw, *, tm, name):
    B, L, CD = dcv.shape
    per = tm // HALO
    nhalo = L // HALO
    nt = L // tm

    def body(d_ref, dn_ref, c_ref, hc_ref, cw_ref, out_ref):
        last = pl.program_id(1) == nt - 1
        d = d_ref[...].astype(F32)
        nxt = jnp.where(last, 0.0, dn_ref[...].astype(F32))
        n0, n1 = _row_of(nxt, 0), _row_of(nxt, 1)
        rows = lax.broadcasted_iota(jnp.int32, d.shape, 0)
        d1 = jnp.where(rows == tm - 1, n0, pltpu.roll(d, tm - 1, 0))
        d2 = jnp.where(rows == tm - 2, n0, jnp.where(rows == tm - 1, n1, pltpu.roll(d, tm - 2, 0)))
        w = cw_ref[...]
        du = w[2:3, :] * d + w[1:2, :] * d1 + w[0:1, :] * d2
        out_ref[:, :CD] = (du * hc_ref[...].astype(F32)).astype(BF16)
        out_ref[:, CD:] = (du * c_ref[...].astype(F32)).astype(BF16)

    return pl.pallas_call(
        body, name=name, grid=(B, nt),
        in_specs=[pl.BlockSpec((None, tm, CD), lambda b, i: (b, i, 0)),
                  pl.BlockSpec((None, HALO, CD), lambda b, i: (b, jnp.minimum((i + 1) * per, nhalo - 1), 0)),
                  pl.BlockSpec((None, tm, CD), lambda b, i: (b, i, 1)),
                  pl.BlockSpec((None, tm, CD), lambda b, i: (b, i, 2)),
                  pl.BlockSpec(cw.shape, lambda b, i: (0, 0))],
        out_specs=pl.BlockSpec((None, tm, 2 * CD), lambda b, i: (b, i, 0)),
        out_shape=jax.ShapeDtypeStruct((B, L, 2 * CD), BF16),
        compiler_params=_params("arbitrary", "arbitrary"),
    )(dcv, dcv, proj, proj, cw)


def _final(h, gf, tgt, *, tm, name):
    B, L, D = h.shape

    def body(h_ref, g_ref, t_ref, dh_ref, dhb_ref, dg_ref, loss_ref):
        b, i = pl.program_id(0), pl.program_id(1)

        @pl.when((b == 0) & (i == 0))
        def _():
            dg_ref[...] = jnp.zeros_like(dg_ref)
            loss_ref[...] = jnp.zeros_like(loss_ref)

        x = h_ref[...]
        g = g_ref[...]
        r = lax.rsqrt(jnp.mean(x * x, axis=-1, keepdims=True) + EPS)
        y = x * r
        pos = i * tm + lax.broadcasted_iota(jnp.int32, (tm, 1), 0)
        err = jnp.where(pos >= N_META, y * g - t_ref[...], 0.0)
        loss_ref[...] += 0.5 * jnp.sum(jnp.mean(err * err, axis=-1, keepdims=True))
        dout = err / D
        dg_ref[...] += jnp.sum(dout * y, axis=0, keepdims=True)
        dy = dout * g
        dh = r * (dy - y * jnp.mean(dy * y, axis=-1, keepdims=True))
        dh_ref[...] = dh
        dhb_ref[...] = (0.5 * dh).astype(BF16)

    tile = pl.BlockSpec((None, tm, D), lambda b, i: (b, i, 0))
    const = lambda b, i: (0, 0)
    return pl.pallas_call(
        body, name=name, grid=(B, L // tm),
        in_specs=[tile, pl.BlockSpec((1, D), const), tile],
        out_specs=[tile, tile, pl.BlockSpec((1, D), const), pl.BlockSpec((1, LANES), const)],
        out_shape=[jax.ShapeDtypeStruct((B, L, D), F32), jax.ShapeDtypeStruct((B, L, D), BF16),
                   jax.ShapeDtypeStruct((1, D), F32), jax.ShapeDtypeStruct((1, LANES), F32)],
        compiler_params=_params("arbitrary", "arbitrary"),
    )(h, gf, tgt)


def _place():
    x, y, c = lax.axis_index("x"), lax.axis_index("y"), lax.axis_index("c")
    others = [(1 - x, y), (x, 1 - y), (1 - x, 1 - y)]
    return x, y, c, others


def _all_gather_shards(shards, *, name):
    n = len(shards)

    def body(*refs):
        ins, outs = refs[:n], refs[n:2 * n]
        send, recv, fsend, frecv, lsem = refs[2 * n:]
        x, y, c, others = _place()
        me = 2 * x + y
        local = [pltpu.make_async_copy(ins[t], outs[t].at[me], lsem.at[t]) for t in range(n)]
        for cp in local:
            cp.start()

        def half(t, k):
            hr = shards[t].shape[0] // 2
            return pl.ds(pl.multiple_of(k * hr, HALO), hr)

        def ici(t, j, src_chip, to):
            src = ins[t].at[half(t, c)] if to is not None else outs[t].at[src_chip, half(t, c)]
            return pltpu.make_async_remote_copy(
                src_ref=src, dst_ref=outs[t].at[src_chip, half(t, c)],
                send_sem=send.at[3 * t + j], recv_sem=recv.at[3 * t + j],
                device_id=(x, y, c) if to is None else to, device_id_type=MESH)

        def d2d(t, j, src_chip, k):
            return pltpu.make_async_remote_copy(
                src_ref=outs[t].at[src_chip, half(t, k)], dst_ref=outs[t].at[src_chip, half(t, k)],
                send_sem=fsend.at[3 * t + j], recv_sem=frecv.at[3 * t + j],
                device_id=(x, y, 1 - c), device_id_type=MESH)

        firsts = [ici(t, j, me, (ox, oy, c)) for t in range(n) for j, (ox, oy) in enumerate(others)]
        for cp in firsts:
            cp.start()
        passed = []
        for t in range(n):
            for j, (ox, oy) in enumerate(others):
                ici(t, j, 2 * ox + oy, None).wait_recv()
                cp = d2d(t, j, 2 * ox + oy, c)
                cp.start()
                passed.append(cp)
        for t in range(n):
            for j, (ox, oy) in enumerate(others):
                d2d(t, j, 2 * ox + oy, 1 - c).wait_recv()
        for cp in firsts + passed:
            cp.wait_send()
        for cp in local:
            cp.wait()

    return pl.pallas_call(
        body, name=name,
        in_specs=[ANY] * n, out_specs=[ANY] * n,
        out_shape=[jax.ShapeDtypeStruct((N_SHARD,) + s.shape, s.dtype) for s in shards],
        scratch_shapes=[pltpu.SemaphoreType.DMA((3 * n,))] * 4 + [pltpu.SemaphoreType.DMA((n,))],
    )(*shards)


def _all_reduce_small(slab, *, name):
    def body(in_ref, out_ref, gath, send, recv):
        x, y, c, _ = _place()
        me = 4 * x + 2 * y + c
        gath[me] = in_ref[...]
        copies, peers = [], []
        for m in range(1, N_DEV):
            px = jnp.where((m >> 2) & 1, 1 - x, x)
            py = jnp.where((m >> 1) & 1, 1 - y, y)
            pc = jnp.where(m & 1, 1 - c, c)
            cp = pltpu.make_async_remote_copy(
                src_ref=in_ref, dst_ref=gath.at[me], send_sem=send.at[m - 1], recv_sem=recv.at[m - 1],
                device_id=(px, py, pc), device_id_type=MESH)
            cp.start()
            copies.append(cp)
            peers.append(4 * px + 2 * py + pc)
        for m in range(1, N_DEV):
            pltpu.make_async_remote_copy(
                src_ref=in_ref, dst_ref=gath.at[peers[m - 1]], send_sem=send.at[m - 1], recv_sem=recv.at[m - 1],
                device_id=(x, y, c), device_id_type=MESH).wait_recv()
        for cp in copies:
            cp.wait_send()
        acc = gath[0]
        for k in range(1, N_DEV):
            acc = acc + gath[k]
        out_ref[...] = acc

    vm = pl.BlockSpec(memory_space=pltpu.VMEM)
    return pl.pallas_call(
        body, name=name, in_specs=[vm], out_specs=vm,
        out_shape=jax.ShapeDtypeStruct(slab.shape, slab.dtype),
        scratch_shapes=[pltpu.VMEM((N_DEV,) + slab.shape, slab.dtype),
                        pltpu.SemaphoreType.DMA((N_DEV - 1,)), pltpu.SemaphoreType.DMA((N_DEV - 1,))],
    )(slab)


def _gather_ici(shards):
    n = len(shards)

    def copies(ins, outs, sems, sending):
        send, recv, _ = sems
        x, y, c, others = _place()
        me = 2 * x + y
        out = []
        for t in range(n):
            hr = shards[t].shape[0] // 2
            rows = pl.ds(pl.multiple_of(c * hr, HALO), hr)
            for j, (ox, oy) in enumerate(others):
                src_chip = me if sending else 2 * ox + oy
                out.append(pltpu.make_async_remote_copy(
                    src_ref=ins[t].at[rows], dst_ref=outs[t].at[src_chip, rows],
                    send_sem=send.at[3 * t + j], recv_sem=recv.at[3 * t + j],
                    device_id=(ox, oy, c) if sending else (x, y, c), device_id_type=MESH))
        return out

    def local(ins, outs, sems):
        x, y, _, _ = _place()
        return [pltpu.make_async_copy(ins[t], outs[t].at[2 * x + y], sems[2].at[t]) for t in range(n)]

    def start(ins, outs, sems):
        for cp in local(ins, outs, sems) + copies(ins, outs, sems, True):
            cp.start()

    def finish(ins, outs, sems):
        for cp in copies(ins, outs, sems, False):
            cp.wait_recv()
        for cp in copies(ins, outs, sems, True):
            cp.wait_send()
        for cp in local(ins, outs, sems):
            cp.wait()

    return _Comm(shards, [jax.ShapeDtypeStruct((N_SHARD,) + s.shape, s.dtype) for s in shards],
                 [3 * n, 3 * n, n], start, finish)


def _gather_d2d(parts):
    n = len(parts)

    def copies(outs, sems, sending):
        send, recv = sems
        x, y, c, others = _place()
        out = []
        for t in range(n):
            hr = parts[t].shape[1] // 2
            rows = pl.ds(pl.multiple_of((c if sending else 1 - c) * hr, HALO), hr)
            for j, (ox, oy) in enumerate(others):
                blk = outs[t].at[2 * ox + oy, rows]
                out.append(pltpu.make_async_remote_copy(
                    src_ref=blk, dst_ref=blk, send_sem=send.at[3 * t + j], recv_sem=recv.at[3 * t + j],
                    device_id=(x, y, 1 - c) if sending else (x, y, c), device_id_type=MESH))
        return out

    def start(ins, outs, sems):
        for cp in copies(outs, sems, True):
            cp.start()

    def finish(ins, outs, sems):
        for cp in copies(outs, sems, False):
            cp.wait_recv()
        for cp in copies(outs, sems, True):
            cp.wait_send()

    return _Comm(parts, [jax.ShapeDtypeStruct(p.shape, p.dtype) for p in parts], [3 * n, 3 * n], start, finish,
                 aliases={t: t for t in range(n)})


def _swap_halves(grads):
    n = len(grads)

    def copies(ins, outs, sems):
        x, y, c, _ = _place()
        out = []
        for t in range(n):
            hr = grads[t].shape[1] // 2
            rows = pl.ds(pl.multiple_of((1 - c) * hr, 8), hr)
            out.append(pltpu.make_async_remote_copy(
                src_ref=ins[t].at[:, rows, :], dst_ref=outs[t], send_sem=sems[0].at[t], recv_sem=sems[1].at[t],
                device_id=(x, y, 1 - c), device_id_type=MESH))
        return out

    def start(ins, outs, sems):
        for cp in copies(ins, outs, sems):
            cp.start()

    def finish(ins, outs, sems):
        for cp in copies(ins, outs, sems):
            cp.wait()

    return _Comm(grads, [jax.ShapeDtypeStruct((N_SHARD, g.shape[1] // 2, g.shape[2]), g.dtype) for g in grads],
                 [n, n], start, finish)


def _pair_sum(g, got, c, *, name):
    ns, R, C = g.shape
    hr = R // 2

    def body(c_ref, g_ref, r_ref, o_ref):
        o_ref[...] = (g_ref[...] + r_ref[...]).astype(BF16)

    return pl.pallas_call(
        body, name=name,
        grid_spec=pltpu.PrefetchScalarGridSpec(
            num_scalar_prefetch=1, grid=(ns,),
            in_specs=[pl.BlockSpec((None, hr, C), lambda s, cr: (s, cr[0], 0)),
                      pl.BlockSpec((None, hr, C), lambda s, cr: (s, 0, 0))],
            out_specs=pl.BlockSpec((None, hr, C), lambda s, cr: (s, 0, 0))),
        out_shape=jax.ShapeDtypeStruct((ns, hr, C), BF16),
        compiler_params=_params("arbitrary"),
    )(c, g, got)


def _scatter_chips(sums):
    n = len(sums)

    def copies(ins, outs, sems, sending):
        x, y, c, others = _place()
        me = 2 * x + y
        out = []
        for t in range(n):
            for j, (ox, oy) in enumerate(others):
                there = 2 * ox + oy
                out.append(pltpu.make_async_remote_copy(
                    src_ref=ins[t].at[there if sending else me], dst_ref=outs[t].at[me if sending else there],
                    send_sem=sems[0].at[3 * t + j], recv_sem=sems[1].at[3 * t + j],
                    device_id=(ox, oy, c) if sending else (x, y, c), device_id_type=MESH))
        return out

    def start(ins, outs, sems):
        for cp in copies(ins, outs, sems, True):
            cp.start()

    def finish(ins, outs, sems):
        for cp in copies(ins, outs, sems, False):
            cp.wait_recv()
        for cp in copies(ins, outs, sems, True):
            cp.wait_send()

    return _Comm(sums, [jax.ShapeDtypeStruct(s.shape, s.dtype) for s in sums], [3 * n, 3 * n], start, finish)


def _chip_sum(g, got, landed, idx, *, name):
    ns, R, C = g.shape
    hr = R // 2

    def body(i_ref, g_ref, r_ref, a_ref, b_ref, c_ref, o_ref):
        acc = g_ref[...] + r_ref[...]
        for ref in (a_ref, b_ref, c_ref):
            acc = acc + ref[...].astype(F32)
        o_ref[...] = acc

    other = lambda k: pl.BlockSpec((None, hr, C), lambda s, ir: (ir[2 + k], 0, 0))
    return pl.pallas_call(
        body, name=name,
        grid_spec=pltpu.PrefetchScalarGridSpec(
            num_scalar_prefetch=1, grid=(1,),
            in_specs=[pl.BlockSpec((None, hr, C), lambda s, ir: (ir[0], ir[1], 0)),
                      pl.BlockSpec((None, hr, C), lambda s, ir: (ir[0], 0, 0)),
                      other(0), other(1), other(2)],
            out_specs=pl.BlockSpec((hr, C), lambda s, ir: (ir[1], 0))),
        out_shape=jax.ShapeDtypeStruct((R, C), F32),
        compiler_params=_params("arbitrary"),
    )(idx, g, got, landed, landed, landed)


def _share_halves(halves):
    n = len(halves)

    def copies(outs, sems, sending):
        x, y, c, _ = _place()
        out = []
        for t in range(n):
            hr = halves[t].shape[0] // 2
            rows = pl.ds(pl.multiple_of((c if sending else 1 - c) * hr, 8), hr)
            out.append(pltpu.make_async_remote_copy(
                src_ref=outs[t].at[rows, :], dst_ref=outs[t].at[rows, :], send_sem=sems[0].at[t],
                recv_sem=sems[1].at[t], device_id=(x, y, 1 - c) if sending else (x, y, c), device_id_type=MESH))
        return out

    def start(ins, outs, sems):
        for cp in copies(outs, sems, True):
            cp.start()

    def finish(ins, outs, sems):
        for cp in copies(outs, sems, False):
            cp.wait_recv()
        for cp in copies(outs, sems, True):
            cp.wait_send()

    return _Comm(halves, [jax.ShapeDtypeStruct(h.shape, h.dtype) for h in halves], [n, n], start, finish,
                 aliases={t: t for t in range(n)})


def _adamw(w, g, m, v, *, name):
    R, C = w.shape
    tr = R
    for cand in (256, 128, 64, 32, 16, 8):
        if R % cand == 0:
            tr = cand
            break

    def body(w_ref, g_ref, m_ref, v_ref, d_ref, mo_ref, vo_ref):
        gv = g_ref[...]
        mn = ADAM_B1 * m_ref[...] + (1.0 - ADAM_B1) * gv
        vn = ADAM_B2 * v_ref[...] + (1.0 - ADAM_B2) * (gv * gv)
        m_hat = mn / (1.0 - ADAM_B1 ** ADAM_STEP)
        v_hat = vn / (1.0 - ADAM_B2 ** ADAM_STEP)
        d_ref[...] = -ADAM_LR * (m_hat / (jnp.sqrt(v_hat) + ADAM_EPS) + ADAM_WD * w_ref[...])
        mo_ref[...] = mn
        vo_ref[...] = vn

    blk = pl.BlockSpec((tr, C), lambda i: (i, 0))
    return pl.pallas_call(
        body, name=name, grid=(R // tr,), in_specs=[blk] * 4, out_specs=[blk] * 3,
        out_shape=[jax.ShapeDtypeStruct((R, C), F32)] * 3,
        compiler_params=_params("arbitrary"),
    )(w, g, m, v)


def _pack_small(D, meta, n1, nm, n3, nf, gc, ga, bf, cw):
    def row(a):
        a = a.reshape(-1, a.shape[-1])
        return jnp.pad(a, ((0, 0), (0, D - a.shape[-1])))
    rows = [row(meta), row(n1), row(nm), row(n3), row(nf), row(jnp.concatenate([gc, ga], axis=-1)), row(bf), row(cw)]
    slab = jnp.concatenate(rows, axis=0)
    return jnp.pad(slab, ((0, SMALL_ROWS - slab.shape[0]), (0, 0)))


def _unpack_small(slab, like):
    meta, n1, nm, n3, nf, gc, ga, bf, cw = like
    nmeta, mc = meta.shape
    out = [slab[:nmeta, :mc].reshape(meta.shape)]
    r = nmeta
    for a in (n1, nm, n3, nf):
        out.append(slab[r, :a.shape[-1]].reshape(a.shape))
        r += 1
    cd = gc.shape[-1]
    out.append(slab[r, :cd].reshape(gc.shape))
    out.append(slab[r, cd:cd + ga.shape[-1]].reshape(ga.shape))
    r += 1
    out.append(slab[r, :bf.shape[-1]].reshape(bf.shape))
    r += 1
    out.append(slab[r:r + 3, :cw.shape[-1]].reshape(cw.shape))
    return out


def kernel(x, meta_tokens, ffn1_norm, ffn1_w_gu, ffn1_w_down, mix_norm, w_in, conv_w, b_f, out_norm_conv, out_norm_attn, w_out, ffn2_norm, ffn2_w_gu, ffn2_w_down, final_norm, loss_target, m_meta_tokens, m_ffn1_norm, m_ffn1_w_gu, m_ffn1_w_down, m_mix_norm, m_w_in, m_conv_w, m_b_f, m_out_norm_conv, m_out_norm_attn, m_w_out, m_ffn2_norm, m_ffn2_w_gu, m_ffn2_w_down, m_final_norm, v_meta_tokens, v_ffn1_norm, v_ffn1_w_gu, v_ffn1_w_down, v_mix_norm, v_w_in, v_conv_w, v_b_f, v_out_norm_conv, v_out_norm_attn, v_w_out, v_ffn2_norm, v_ffn2_w_gu, v_ffn2_w_down, v_final_norm):
    B, S, D = x.shape
    L = S + N_META
    T = B * L
    tm = L // 3
    assert tm * 3 == L and tm % HALO == 0
    guc = ffn1_w_gu.shape[-1]
    ff = N_SHARD * guc // 2
    H = b_f.shape[-1]
    AD = H * HEAD_DIM
    CD = conv_w.shape[-1] * N_SHARD
    assert CD == AD and CD + AD == D and CD % LANES == 0
    n_main = 3 * CD + 3 * AD
    ins = w_in.shape[-1]

    xi, yi, ci = lax.axis_index("x"), lax.axis_index("y"), lax.axis_index("c")
    chip = 2 * xi + yi

    small_shard = jnp.zeros((2 * HALO, meta_tokens.shape[-1]), F32)
    small_shard = small_shard.at[:N_META].set(meta_tokens)
    small_shard = small_shard.at[N_META:N_META + 3, :conv_w.shape[-1]].set(conv_w[0])
    big = [ffn1_w_gu[0], ffn1_w_down[0], w_in[0], w_out[0], ffn2_w_gu[0], ffn2_w_down[0]]
    wgu1_s, wd1_s, win_s, wout_s, wgu2_s, wd2_s = [w.astype(BF16) for w in big]
    small_g, = _all_gather_shards([small_shard], name="gather_small")
    meta_f = jnp.moveaxis(small_g[:, :N_META], 0, 1).reshape(N_META, D)
    cw_f = jnp.moveaxis(small_g[:, N_META:N_META + 3, :conv_w.shape[-1]], 0, 1).reshape(3, CD)
    cw8 = jnp.pad(cw_f, ((0, 5), (0, 0)))
    bf_p = jnp.pad(b_f, ((0, 0), (0, LANES - H)))
    gid = jnp.arange(CD) // HEAD_DIM
    pmat = jnp.where(gid[:, None] == gid[None, :], 1.0 / HEAD_DIM, 0.0).astype(BF16)

    gu_shape = jax.ShapeDtypeStruct((2, T, ff), BF16)
    gu_w_spec = pl.BlockSpec((None, D, guc), lambda s, i: (s, 0, 0))
    gu_o_spec = pl.BlockSpec((None, tm, guc), lambda s, i: (s // 2, i, s % 2))

    h0 = jnp.concatenate([jnp.broadcast_to(meta_f[None], (B, N_META, D)), x], axis=1).reshape(T, D)
    sid = ((chip + jnp.arange(N_SHARD, dtype=jnp.int32)) % N_SHARD).astype(jnp.int32)
    n1, wgu1_h = _rmsnorm(h0, ffn1_norm, tm=tm, name="ffn1_norm", comm=_gather_ici([wgu1_s]))
    gu1, out = _ffn_up(n1, wgu1_s[None], sid, None, tm=tm, first=0, count=1, name="ffn1_up_own",
                       comm=_join(_gather_d2d(wgu1_h), _gather_ici([wd1_s])))
    wgu1, wd1_h = out[0], out[1:]
    gu1, out = _ffn_up(n1, wgu1, sid, gu1, tm=tm, first=1, count=N_SHARD - 1, name="ffn1_up_rest",
                       comm=_join(_gather_d2d(wd1_h), _gather_ici([win_s, wout_s])))
    wd1, mix_w = out[0].reshape(ff, D), out[1:]
    (h1, n2), (win_g, wout_g) = _ffn_down(gu1, wd1, h0, mix_norm, tm=tm, name="ffn1_down", comm=_gather_d2d(mix_w))
    wout_f = wout_g.reshape(D, D)
    win_f = jnp.moveaxis(win_g, 0, 1).reshape(D, N_SHARD * ins)
    win_main = win_f[:, :n_main]
    win_fg = jnp.pad(win_f[:, n_main:], ((0, 0), (0, LANES - H)))

    proj, _ = _matmul_nn(n2, win_main, tm=tm, nb=n_main // (3 * CD),
                         w_spec=pl.BlockSpec((D, 3 * CD), lambda s, i: (0, s)),
                         out_shape=jax.ShapeDtypeStruct((T, n_main), BF16),
                         out_spec=pl.BlockSpec((tm, 3 * CD), lambda s, i: (i, s)), name="mix_in")
    fg, _ = _matmul_nn(n2, win_fg, tm=tm, nb=1, w_spec=pl.BlockSpec((D, LANES), lambda s, i: (0, 0)),
                       out_shape=jax.ShapeDtypeStruct((T, LANES), F32),
                       out_spec=pl.BlockSpec((tm, LANES), lambda s, i: (i, 0)), name="mix_in_fg")
    proj3 = proj.reshape(B, L, n_main)
    fg3 = fg.reshape(B, L, LANES)
    fc = _fcum(fg3, bf_p, ch=tm, name="forget_cumsum")
    fr = fc[:, :, :H].reshape(B, L // tm, tm, H).transpose(0, 1, 3, 2)
    (o, lse), ffn2_w = _attn_fwd(proj3, fc, fr, tq=tm, n_heads=H, name="attn_fwd",
                                 comm=_gather_ici([wgu2_s, wd2_s]))
    (h2, ymix, n3), (wgu2, wd2) = _mix_out(
        proj3, o, cw8, out_norm_conv, out_norm_attn, wout_f, h1.reshape(B, L, D), pmat, ffn2_norm,
        tm=tm, name="mix_out", comm=_gather_d2d(ffn2_w))
    wd2 = wd2.reshape(ff, D)
    h2 = h2.reshape(T, D)
    n3 = n3.reshape(T, D)

    gu2, _ = _matmul_nn(n3, wgu2, tm=tm, nb=N_SHARD, w_spec=gu_w_spec, out_shape=gu_shape, out_spec=gu_o_spec,
                        name="ffn2_up")
    (h3, _), _ = _ffn_down(gu2, wd2, h2, final_norm.reshape(1, D), tm=tm, name="ffn2_down")

    tgt = jnp.pad(loss_target, ((0, 0), (N_META, 0), (0, 0)))
    dh3, dh3b, d_gf, loss_part = _final(h3.reshape(B, L, D), final_norm.reshape(1, D), tgt, tm=tm, name="final")

    c_arr = jnp.reshape(ci, (1,)).astype(jnp.int32)
    ks = jnp.arange(N_SHARD - 1, dtype=jnp.int32)
    idx = jnp.concatenate([jnp.stack([chip, ci]).astype(jnp.int32), ks + (ks >= chip).astype(jnp.int32)])

    def pair_sums(grads, got, names):
        return [_pair_sum(g, r, c_arr, name="pair_sum_" + nm) for g, r, nm in zip(grads, got, names)]

    def chip_sums(grads, got, landed, names):
        return [_chip_sum(g, r, l, idx, name="chip_sum_" + nm) for g, r, l, nm in zip(grads, got, landed, names)]

    def dw_down(gu, dhb, name, comm=None):
        return _matmul_tn(
            [gu, gu], dhb, tm=tm, nb=ff // guc, kb=guc, silu=True,
            x_specs=[pl.BlockSpec((None, tm, guc), lambda j, i: (0, i, j)),
                     pl.BlockSpec((None, tm, guc), lambda j, i: (1, i, j))],
            y_spec=pl.BlockSpec((tm, D), lambda j, i: (i, 0)),
            out_shape=jax.ShapeDtypeStruct((ff, D), F32), out_spec=pl.BlockSpec((guc, D), lambda j, i: (j, 0)),
            name=name, comm=comm)

    def dw_up(n, dgu, name, comm=None):
        return _matmul_tn(
            [n], dgu, tm=tm, nb=N_SHARD, kb=D, silu=False,
            x_specs=[pl.BlockSpec((tm, D), lambda s, i: (i, 0))],
            y_spec=pl.BlockSpec((None, tm, guc), lambda s, i: (s // 2, i, s % 2)),
            out_shape=jax.ShapeDtypeStruct((N_SHARD, D, guc), F32),
            out_spec=pl.BlockSpec((None, D, guc), lambda s, i: (s, 0, 0)), name=name, comm=comm)

    dh3f, dh3b = dh3.reshape(T, D), dh3b.reshape(T, D)
    dgu2, _ = _ffn_bwd_act(dh3b, gu2, wd2, tm=tm, guc=guc, name="ffn2_bwd_act")
    (dh2, dh2b, d_g3), _ = _ffn_bwd_in(dgu2, wgu2, h2, ffn2_norm, dh3f, tm=tm, scale=1.0, name="ffn2_bwd_in")
    d_wd2, _ = dw_down(gu2, dh3b, "ffn2_dw_down")
    d_wgu2, _ = dw_up(n3, dgu2, "ffn2_dw_up")
    grads_f2 = [d_wgu2, d_wd2.reshape(N_SHARD, ff // N_SHARD, D)]
    names_f2 = ["wgu2", "wd2"]

    dh2b3 = dh2b.reshape(B, L, D)
    (d_bg, d_cv, d_o, d_gc, d_ga, d_cw), got_f2 = _mix_out_bwd(
        dh2b3, proj3, o, cw8, out_norm_conv, out_norm_attn, wout_f, pmat, tm=tm, name="mix_out_bwd",
        comm=_swap_halves(grads_f2))
    sums_f2 = pair_sums(grads_f2, got_f2, names_f2)
    d_wout, _ = _matmul_tn(
        [ymix.reshape(T, D)], dh2b, tm=tm, nb=1, kb=D, silu=False,
        x_specs=[pl.BlockSpec((tm, D), lambda s, i: (i, 0))], y_spec=pl.BlockSpec((tm, D), lambda s, i: (i, 0)),
        out_shape=jax.ShapeDtypeStruct((D, D), F32), out_spec=pl.BlockSpec((D, D), lambda s, i: (0, 0)),
        name="dw_out")
    d_cc = _conv_bwd(d_cv, proj3, cw8, tm=tm, name="conv_bwd")
    (d_q, d_k, d_v, d_fr, d_fq), landed_f2 = _attn_bwd(proj3, o, d_o, lse, fc, fr, tq=tm, n_heads=H, name="attn_bwd",
                                                       comm=_scatter_chips(sums_f2))
    halves_f2 = chip_sums(grads_f2, got_f2, landed_f2, names_f2)
    d_fc = d_fq + jnp.pad(d_fr.transpose(0, 1, 3, 2).reshape(B, L, H), ((0, 0), (0, 0), (0, LANES - H)))
    d_fg, d_bf = _fcum_bwd(d_fc, fg3, bf_p, ch=tm, name="forget_cumsum_bwd")

    parts = [d_bg.reshape(T, CD), d_cc.reshape(T, 2 * CD), d_q.reshape(T, AD), d_k.reshape(T, AD),
             d_v.reshape(T, AD), d_fg.reshape(T, LANES)]
    (dh1, dh1b, d_gm), g_f2 = _mix_bwd_in(parts, win_main, win_fg, h1, mix_norm, dh2, tm=tm, scale=0.5,
                                          name="mix_bwd_in", comm=_share_halves(halves_f2))
    d_win_parts = []
    for k, p in enumerate(parts):
        wdt = p.shape[1]
        nb = max(wdt // CD, 1)
        bw = wdt // nb
        d_win_parts.append(_matmul_tn(
            [n2], p, tm=tm, nb=nb, kb=D, silu=False,
            x_specs=[pl.BlockSpec((tm, D), lambda s, i: (i, 0))], y_spec=pl.BlockSpec((tm, bw), lambda s, i: (i, s)),
            out_shape=jax.ShapeDtypeStruct((D, wdt), F32), out_spec=pl.BlockSpec((D, bw), lambda s, i: (0, s)),
            name="dw_in_%d" % k)[0])
    d_win_parts[-1] = d_win_parts[-1][:, :H]
    d_win = jnp.moveaxis(jnp.concatenate(d_win_parts, axis=1).reshape(D, N_SHARD, ins), 1, 0)
    grads_mx = [d_win, d_wout.reshape(N_SHARD, D // N_SHARD, D)]
    names_mx = ["win", "wout"]

    dgu1, got_mx = _ffn_bwd_act(dh1b, gu1, wd1, tm=tm, guc=guc, name="ffn1_bwd_act", comm=_swap_halves(grads_mx))
    sums_mx = pair_sums(grads_mx, got_mx, names_mx)
    d_wd1, landed_mx = dw_down(gu1, dh1b, "ffn1_dw_down", comm=_scatter_chips(sums_mx))
    halves_mx = chip_sums(grads_mx, got_mx, landed_mx, names_mx)
    grads_d1 = [d_wd1.reshape(N_SHARD, ff // N_SHARD, D)]
    d_wgu1, out = dw_up(n1, dgu1, "ffn1_dw_up", comm=_join(_share_halves(halves_mx), _swap_halves(grads_d1)))
    g_mx, got_d1 = out[:2], out[2:]
    sums_d1 = pair_sums(grads_d1, got_d1, ["wd1"])
    grads_u1 = [d_wgu1]
    (dh0, _, d_g1), out = _ffn_bwd_in(dgu1, wgu1, h0, ffn1_norm, dh1, tm=tm, scale=1.0, name="ffn1_bwd_in",
                                      comm=_join(_scatter_chips(sums_d1), _swap_halves(grads_u1)))
    landed_d1, got_u1 = out[:1], out[1:]
    halves_d1 = chip_sums(grads_d1, got_d1, landed_d1, ["wd1"])
    sums_u1 = pair_sums(grads_u1, got_u1, ["wgu1"])
    out = _run_comm(_join(_share_halves(halves_d1), _scatter_chips(sums_u1)), name="scatter_ffn1")
    g_d1, landed_u1 = out[:1], out[1:]
    halves_u1 = chip_sums(grads_u1, got_u1, landed_u1, ["wgu1"])
    g_u1 = _run_comm(_share_halves(halves_u1), name="share_ffn1")
    g_big = [g_u1[0], g_d1[0], g_mx[0], g_mx[1], g_f2[0], g_f2[1]]
    dh0 = dh0.reshape(B, L, D)
    grad_x = dh0[:, N_META:]
    d_meta = jnp.sum(dh0[:, :N_META], axis=0)

    loss_row = jnp.zeros((1, D), F32).at[0, 0].set(loss_part[0, 0])
    slab = _pack_small(D, d_meta, d_g1, d_gm, d_g3, d_gf, d_gc, d_ga, d_bf[:, :H], d_cw[:3])
    slab = slab.at[SMALL_ROWS - 1].set(loss_row[0])
    total = _all_reduce_small(slab, name="reduce_small")
    loss = total[SMALL_ROWS - 1, 0]
    mcols = meta_tokens.shape[-1]
    ccols = conv_w.shape[-1]
    full_like = (jnp.zeros((N_META, D)), ffn1_norm, mix_norm, ffn2_norm, final_norm.reshape(1, D), out_norm_conv,
                 out_norm_attn, b_f, jnp.zeros((1, 3, CD)))
    g_small = _unpack_small(total, full_like)
    g_small[0] = lax.dynamic_slice_in_dim(g_small[0], chip * mcols, mcols, axis=1)
    g_small[8] = lax.dynamic_slice_in_dim(g_small[8], chip * ccols, ccols, axis=2)

    def small_slab(meta, a1, am, a3, af, gc, ga, bf, cw):
        return _pack_small(D, meta, a1, am, a3, af.reshape(1, D), gc, ga, bf, cw[0])

    w_small = small_slab(meta_tokens, ffn1_norm, mix_norm, ffn2_norm, final_norm, out_norm_conv, out_norm_attn, b_f, conv_w)
    m_small = small_slab(m_meta_tokens, m_ffn1_norm, m_mix_norm, m_ffn2_norm, m_final_norm, m_out_norm_conv,
                         m_out_norm_attn, m_b_f, m_conv_w)
    v_small = small_slab(v_meta_tokens, v_ffn1_norm, v_mix_norm, v_ffn2_norm, v_final_norm, v_out_norm_conv,
                         v_out_norm_attn, v_b_f, v_conv_w)
    gs = list(g_small)
    gs[4] = gs[4].reshape(final_norm.shape)
    g_slab = small_slab(gs[0], gs[1], gs[2], gs[3], gs[4], gs[5], gs[6], gs[7], gs[8])
    local_like = (meta_tokens, ffn1_norm, mix_norm, ffn2_norm, final_norm.reshape(1, D), out_norm_conv, out_norm_attn,
                  b_f, conv_w)
    small_out = [_unpack_small(s, local_like) for s in _adamw(w_small, g_slab, m_small, v_small, name="adamw_small")]
    for lst in small_out:
        lst[4] = lst[4].reshape(final_norm.shape)

    names = ["wgu1", "wd1", "win", "wout", "wgu2", "wd2"]
    w_big = big
    m_big = [m_ffn1_w_gu[0], m_ffn1_w_down[0], m_w_in[0], m_w_out[0], m_ffn2_w_gu[0], m_ffn2_w_down[0]]
    v_big = [v_ffn1_w_gu[0], v_ffn1_w_down[0], v_w_in[0], v_w_out[0], v_ffn2_w_gu[0], v_ffn2_w_down[0]]
    big_out = [_adamw(w, g, m, v, name="adamw_" + nm) for w, g, m, v, nm in zip(w_big, g_big, m_big, v_big, names)]

    def assemble(small, bigs):
        meta, a1, am, a3, af, gc, ga, bf, cw = small
        gu1_, d1_, win_, wout_, gu2_, d2_ = [b[None] for b in bigs]
        return [meta, a1, gu1_, d1_, am, win_, cw, bf, gc, ga, wout_, a3, gu2_, d2_, af]

    gs_out = list(g_small)
    gs_out[4] = gs_out[4].reshape(final_norm.shape)
    grads_out = assemble(gs_out, g_big)
    delta_out = assemble(small_out[0], [b[0] for b in big_out])
    m_out = assemble(small_out[1], [b[1] for b in big_out])
    v_out = assemble(small_out[2], [b[2] for b in big_out])
    return (loss, grad_x, *grads_out, *delta_out, *m_out, *v_out)
```

```python
import functools

import jax
import jax.numpy as jnp
from jax import lax
from jax.experimental import pallas as pl
from jax.experimental.pallas import tpu as pltpu

F32 = jnp.float32
BF16 = jnp.bfloat16

EPS = 1e-6
N_META = 16
HEAD_DIM = 64
N_SHARD = 4
N_DEV = 8
HALO = 16
LANES = 128
SMALL_ROWS = 32
VMEM_LIMIT_V7X = 56 * 1024 * 1024
NEG = -1e30
ATTN_BANDS = 2

ADAM_LR = 0.001
ADAM_B1 = 0.9
ADAM_B2 = 0.999
ADAM_EPS = 1e-08
ADAM_WD = 0.01
ADAM_STEP = 10

MESH = pl.DeviceIdType.MESH
ANY = pl.BlockSpec(memory_space=pl.ANY)
NT_DIMS = (((1,), (1,)), ((), ()))
TN_DIMS = (((0,), (0,)), ((), ()))


def _params(*sem):
    return pltpu.CompilerParams(dimension_semantics=sem, vmem_limit_bytes=VMEM_LIMIT_V7X)


class _Comm:
    def __init__(self, ins, out_shapes, sems, start, finish, aliases=None):
        self.ins, self.out_shapes, self.sems = list(ins), list(out_shapes), list(sems)
        self.start, self.finish, self.aliases = start, finish, dict(aliases or {})


def _join(a, b):
    ni, no, ns = len(a.ins), len(a.out_shapes), len(a.sems)

    def start(ins, outs, sems):
        a.start(ins[:ni], outs[:no], sems[:ns])
        b.start(ins[ni:], outs[no:], sems[ns:])

    def finish(ins, outs, sems):
        a.finish(ins[:ni], outs[:no], sems[:ns])
        b.finish(ins[ni:], outs[no:], sems[ns:])

    aliases = dict(a.aliases)
    aliases.update({ni + i: no + j for i, j in b.aliases.items()})
    return _Comm(a.ins + b.ins, a.out_shapes + b.out_shapes, a.sems + b.sems, start, finish, aliases)


def _launch(body, *, name, grid, in_specs, out_specs, out_shape, args, scratch_shapes=(), comm=None, prefetch=(),
            aliases=None):
    single = not isinstance(out_shape, (list, tuple))
    out_specs = [out_specs] if single else list(out_specs)
    out_shape = [out_shape] if single else list(out_shape)
    in_specs, scratch_shapes, prefetch = list(in_specs), list(scratch_shapes), list(prefetch)
    params = _params(*(("arbitrary",) * len(grid)))
    n_pf, n_in, n_out, n_scr = len(prefetch), len(in_specs), len(out_specs), len(scratch_shapes)
    c_ins = comm.ins if comm else []
    c_shapes = comm.out_shapes if comm else []
    c_sems = comm.sems if comm else []
    c_in, c_out = len(c_ins), len(c_shapes)

    def carrier(*refs):
        p = 0
        pf = refs[p:p + n_pf]; p += n_pf
        a = refs[p:p + n_in]; p += n_in
        ci = refs[p:p + c_in]; p += c_in
        o = refs[p:p + n_out]; p += n_out
        co = refs[p:p + c_out]; p += c_out
        s = refs[p:p + n_scr]; p += n_scr
        cs = refs[p:]
        if comm:
            first = functools.reduce(lambda u, v: u & v, [pl.program_id(k) == 0 for k in range(len(grid))])

            @pl.when(first)
            def _():
                comm.start(ci, co, cs)

        body(*pf, *a, *o, *s)

        if comm:
            last = functools.reduce(lambda u, v: u & v, [pl.program_id(k) == grid[k] - 1 for k in range(len(grid))])

            @pl.when(last)
            def _():
                comm.finish(ci, co, cs)

    io_aliases = {n_pf + i: j for i, j in (aliases or {}).items()}
    if comm:
        io_aliases.update({n_pf + n_in + i: n_out + j for i, j in comm.aliases.items()})
    all_in, all_out = in_specs + [ANY] * c_in, out_specs + [ANY] * c_out
    all_scratch = scratch_shapes + [pltpu.SemaphoreType.DMA((k,)) for k in c_sems]
    if n_pf:
        spec = dict(grid_spec=pltpu.PrefetchScalarGridSpec(
            num_scalar_prefetch=n_pf, grid=grid, in_specs=all_in, out_specs=all_out, scratch_shapes=all_scratch))
    else:
        spec = dict(grid=grid, in_specs=all_in, out_specs=all_out, scratch_shapes=all_scratch)
    res = pl.pallas_call(carrier, name=name, out_shape=out_shape + c_shapes, input_output_aliases=io_aliases,
                         compiler_params=params, **spec)(*prefetch, *args, *c_ins)
    main = list(res[:n_out])
    return (main[0] if single else main), (list(res[n_out:]) if comm else None)


def _run_comm(comm, *, name):
    c_in, c_out = len(comm.ins), len(comm.out_shapes)

    def body(*refs):
        ci, co, cs = refs[:c_in], refs[c_in:c_in + c_out], refs[c_in + c_out:]
        comm.start(ci, co, cs)
        comm.finish(ci, co, cs)

    return list(pl.pallas_call(
        body, name=name, in_specs=[ANY] * c_in, out_specs=[ANY] * c_out, out_shape=comm.out_shapes,
        scratch_shapes=[pltpu.SemaphoreType.DMA((k,)) for k in comm.sems],
        input_output_aliases=comm.aliases)(*comm.ins))


def _chunks(width, step=512):
    out, c0 = [], 0
    while c0 < width:
        cw = min(step, width - c0)
        out.append((c0, cw))
        c0 += cw
    return out


def _split2(v):
    hi = v.astype(BF16)
    lo = (v - hi.astype(F32)).astype(BF16)
    return hi, lo


def _split3(v):
    hi = v.astype(BF16)
    r = v - hi.astype(F32)
    mid = r.astype(BF16)
    lo = (r - mid.astype(F32)).astype(BF16)
    return hi, mid, lo


def _dot(a, b):
    return jnp.dot(a, b, preferred_element_type=F32)


def _dot_nt(a, b):
    return lax.dot_general(a, b, NT_DIMS, preferred_element_type=F32)


def _dot_tn(a, b):
    return lax.dot_general(a, b, TN_DIMS, preferred_element_type=F32)


def _silu_mul(g, u):
    return g * jax.nn.sigmoid(g) * u


def _rms_bwd(dn, h, gain, dres):
    r = lax.rsqrt(jnp.mean(h * h, axis=-1, keepdims=True) + EPS)
    y = h * r
    dgain = jnp.sum(dn * y, axis=0, keepdims=True)
    dy = dn * gain
    dh = dres + r * (dy - y * jnp.mean(dy * y, axis=-1, keepdims=True))
    return dh, dgain


def _group_mean(v, p):
    hi, lo = _split2(v)
    return _dot(hi, p) + _dot(lo, p)


def _row_of(a, k):
    rows = lax.broadcasted_iota(jnp.int32, a.shape, 0)
    return jnp.sum(jnp.where(rows == k, a, 0.0), axis=0, keepdims=True)


def _causal_conv(u, prev, w):
    rows = lax.broadcasted_iota(jnp.int32, u.shape, 0)
    p1 = _row_of(prev, HALO - 1)
    p2 = _row_of(prev, HALO - 2)
    u1 = jnp.where(rows == 0, p1, pltpu.roll(u, 1, 0))
    u2 = jnp.where(rows == 0, p2, jnp.where(rows == 1, p1, pltpu.roll(u, 2, 0)))
    return w[2:3, :] * u + w[1:2, :] * u1 + w[0:1, :] * u2, u1, u2


def _rms(x, gain):
    return (x * lax.rsqrt(jnp.mean(x * x, axis=-1, keepdims=True) + EPS) * gain).astype(BF16)


def _rmsnorm(h, g, *, tm, name, comm=None):
    T, D = h.shape

    def body(h_ref, g_ref, n_ref):
        n_ref[...] = _rms(h_ref[...], g_ref[...])

    return _launch(
        body, name=name, grid=(T // tm,),
        in_specs=[pl.BlockSpec((tm, D), lambda i: (i, 0)), pl.BlockSpec((1, D), lambda i: (0, 0))],
        out_specs=pl.BlockSpec((tm, D), lambda i: (i, 0)),
        out_shape=jax.ShapeDtypeStruct((T, D), BF16), args=(h, g), comm=comm)


def _ffn_up(n, wgu, sid, gu_prev, *, tm, first, count, name, comm=None):
    T, D = n.shape
    ns, _, guc = wgu.shape
    ff = N_SHARD * guc // 2

    def body(sid_ref, x_ref, w_ref, *rest):
        rest[-1][...] = _dot(x_ref[...], w_ref[...]).astype(BF16)

    where = lambda s, sid: sid[first + s]
    w_at = (lambda s, sid: 0) if ns == 1 else where
    return _launch(
        body, name=name, grid=(count, T // tm), prefetch=(sid,),
        in_specs=[pl.BlockSpec((tm, D), lambda s, i, sid: (i, 0)),
                  pl.BlockSpec((None, D, guc), lambda s, i, sid: (w_at(s, sid), 0, 0))]
                 + ([] if gu_prev is None else [ANY]),
        out_specs=pl.BlockSpec((None, tm, guc), lambda s, i, sid: (where(s, sid) // 2, i, where(s, sid) % 2)),
        out_shape=jax.ShapeDtypeStruct((2, T, ff), BF16),
        args=(n, wgu) + (() if gu_prev is None else (gu_prev,)),
        aliases=None if gu_prev is None else {2: 0}, comm=comm)


def _matmul_nn(x, w, *, tm, nb, w_spec, out_shape, out_spec, name, comm=None):
    T, K = x.shape

    def body(x_ref, w_ref, o_ref):
        o_ref[...] = _dot(x_ref[...], w_ref[...]).astype(o_ref.dtype)

    return _launch(
        body, name=name, grid=(nb, T // tm),
        in_specs=[pl.BlockSpec((tm, K), lambda s, i: (i, 0)), w_spec],
        out_specs=out_spec, out_shape=out_shape, args=(x, w), comm=comm)


def _ffn_down(gu, wd, h, next_gain, *, tm, name, comm=None):
    _, T, ff = gu.shape
    D = h.shape[1]
    chunks = _chunks(ff)

    def body(g_ref, u_ref, wd_hbm, h_ref, ng_ref, o_ref, n_ref, wd_v, sem):
        @pl.when(pl.program_id(0) == 0)
        def _():
            cp = pltpu.make_async_copy(wd_hbm, wd_v, sem)
            cp.start()
            cp.wait()

        acc = jnp.zeros((tm, D), F32)
        for c0, cw in chunks:
            a = _silu_mul(g_ref[:, c0:c0 + cw].astype(F32), u_ref[:, c0:c0 + cw].astype(F32))
            acc = acc + _dot(a.astype(BF16), wd_v[c0:c0 + cw, :])
        out = h_ref[...] + 0.5 * acc
        o_ref[...] = out
        n_ref[...] = _rms(out, ng_ref[...])

    return _launch(
        body, name=name, grid=(T // tm,),
        in_specs=[pl.BlockSpec((None, tm, ff), lambda i: (0, i, 0)),
                  pl.BlockSpec((None, tm, ff), lambda i: (1, i, 0)),
                  ANY,
                  pl.BlockSpec((tm, D), lambda i: (i, 0)),
                  pl.BlockSpec((1, D), lambda i: (0, 0))],
        out_specs=[pl.BlockSpec((tm, D), lambda i: (i, 0)), pl.BlockSpec((tm, D), lambda i: (i, 0))],
        out_shape=[jax.ShapeDtypeStruct((T, D), F32), jax.ShapeDtypeStruct((T, D), BF16)],
        scratch_shapes=[pltpu.VMEM((ff, D), BF16), pltpu.SemaphoreType.DMA],
        args=(gu, gu, wd, h, next_gain), comm=comm)


def _ffn_down_loss(gu, wd, h, gf, tgt, *, tm, name, comm=None):
    _, T, ff = gu.shape
    D = h.shape[1]
    B, S, _ = tgt.shape
    per_seq = (S + N_META) // tm
    body_rows = tm - N_META
    chunks = _chunks(ff)

    def body(g_ref, u_ref, wd_hbm, h_ref, gf_ref, tgt_hbm, dh_ref, dhb_ref, dg_ref, loss_ref, wd_v, tg_v, sem, tsem):
        i = pl.program_id(0)
        b, t = i // per_seq, i % per_seq

        @pl.when(i == 0)
        def _():
            cp = pltpu.make_async_copy(wd_hbm, wd_v, sem)
            cp.start()
            cp.wait()
            dg_ref[...] = jnp.zeros_like(dg_ref)
            loss_ref[...] = jnp.zeros_like(loss_ref)
            tg_v[0:N_META, :] = jnp.zeros((N_META, D), F32)

        def fetch(fn):
            @pl.when(t == 0)
            def _():
                fn(pltpu.make_async_copy(tgt_hbm.at[b, pl.ds(0, body_rows)], tg_v.at[pl.ds(N_META, body_rows)], tsem))

            @pl.when(t != 0)
            def _():
                fn(pltpu.make_async_copy(tgt_hbm.at[b, pl.ds(pl.multiple_of(t * tm - N_META, 8), tm)], tg_v, tsem))

        fetch(lambda cp: cp.start())
        acc = jnp.zeros((tm, D), F32)
        for c0, cw in chunks:
            a = _silu_mul(g_ref[:, c0:c0 + cw].astype(F32), u_ref[:, c0:c0 + cw].astype(F32))
            acc = acc + _dot(a.astype(BF16), wd_v[c0:c0 + cw, :])
        x = h_ref[...] + 0.5 * acc
        fetch(lambda cp: cp.wait())

        gain = gf_ref[...]
        r = lax.rsqrt(jnp.mean(x * x, axis=-1, keepdims=True) + EPS)
        y = x * r
        pos = t * tm + lax.broadcasted_iota(jnp.int32, (tm, 1), 0)
        err = jnp.where(pos >= N_META, y * gain - tg_v[...], 0.0)
        loss_ref[...] += 0.5 * jnp.sum(jnp.mean(err * err, axis=-1, keepdims=True))
        dout = err / D
        dg_ref[...] += jnp.sum(dout * y, axis=0, keepdims=True)
        dy = dout * gain
        dh = r * (dy - y * jnp.mean(dy * y, axis=-1, keepdims=True))
        dh_ref[...] = dh
        dhb_ref[...] = (0.5 * dh).astype(BF16)

    row = pl.BlockSpec((tm, D), lambda i: (i, 0))
    const = lambda i: (0, 0)
    return _launch(
        body, name=name, grid=(T // tm,),
        in_specs=[pl.BlockSpec((None, tm, ff), lambda i: (0, i, 0)),
                  pl.BlockSpec((None, tm, ff), lambda i: (1, i, 0)),
                  ANY, row, pl.BlockSpec((1, D), const), ANY],
        out_specs=[row, row, pl.BlockSpec((1, D), const), pl.BlockSpec((1, LANES), const)],
        out_shape=[jax.ShapeDtypeStruct((T, D), F32), jax.ShapeDtypeStruct((T, D), BF16),
                   jax.ShapeDtypeStruct((1, D), F32), jax.ShapeDtypeStruct((1, LANES), F32)],
        scratch_shapes=[pltpu.VMEM((ff, D), BF16), pltpu.VMEM((tm, D), F32), pltpu.SemaphoreType.DMA,
                        pltpu.SemaphoreType.DMA],
        args=(gu, gu, wd, h, gf, tgt), comm=comm)


def _ffn_bwd_act(df, gu, wd, *, tm, guc, name, comm=None):
    _, T, ff = gu.shape
    D = df.shape[1]
    nj = ff // guc
    chunks = _chunks(guc)

    def body(df_ref, g_ref, u_ref, wd_ref, o_ref, dwd_ref):
        @pl.when(pl.program_id(1) == 0)
        def _():
            dwd_ref[...] = jnp.zeros_like(dwd_ref)

        dfv = df_ref[...]
        for c0, cw in chunks:
            da = _dot_nt(dfv, wd_ref[c0:c0 + cw, :])
            g = g_ref[:, c0:c0 + cw].astype(F32)
            u = u_ref[:, c0:c0 + cw].astype(F32)
            sg = jax.nn.sigmoid(g)
            silu = g * sg
            o_ref[0, :, c0:c0 + cw] = (da * u * (sg * (1.0 + g * (1.0 - sg)))).astype(BF16)
            o_ref[1, :, c0:c0 + cw] = (da * silu).astype(BF16)
            dwd_ref[c0:c0 + cw, :] += _dot_tn((silu * u).astype(BF16), dfv)

    return _launch(
        body, name=name, grid=(nj, T // tm),
        in_specs=[pl.BlockSpec((tm, D), lambda j, i: (i, 0)),
                  pl.BlockSpec((None, tm, guc), lambda j, i: (0, i, j)),
                  pl.BlockSpec((None, tm, guc), lambda j, i: (1, i, j)),
                  pl.BlockSpec((guc, D), lambda j, i: (j, 0))],
        out_specs=[pl.BlockSpec((2, tm, guc), lambda j, i: (0, i, j)), pl.BlockSpec((guc, D), lambda j, i: (j, 0))],
        out_shape=[jax.ShapeDtypeStruct((2, T, ff), BF16), jax.ShapeDtypeStruct((ff, D), F32)],
        args=(df, gu, gu, wd), comm=comm)


def _ffn_bwd_in(dgu, wgu, h, g, dres, *, tm, scale, name, comm=None):
    _, T, ff = dgu.shape
    ns, D, guc = wgu.shape
    nj = ff // guc
    chunks = _chunks(guc)

    def body(dgu_ref, w_hbm, h_ref, g_ref, dres_ref, dh_ref, dhb_ref, dg_ref, w_v, acc, sem):
        i, j = pl.program_id(0), pl.program_id(1)

        @pl.when((i == 0) & (j == 0))
        def _():
            cp = pltpu.make_async_copy(w_hbm, w_v, sem)
            cp.start()
            cp.wait()
            dg_ref[...] = jnp.zeros_like(dg_ref)

        part = jnp.zeros((tm, D), F32)
        for c0, cw in chunks:
            part = part + _dot_nt(dgu_ref[0, :, c0:c0 + cw], w_v[j, :, c0:c0 + cw])
            part = part + _dot_nt(dgu_ref[1, :, c0:c0 + cw], w_v[nj + j, :, c0:c0 + cw])

        @pl.when(j == 0)
        def _():
            acc[...] = part

        @pl.when(j > 0)
        def _():
            acc[...] += part

        @pl.when(j == nj - 1)
        def _():
            dh, dgain = _rms_bwd(acc[...], h_ref[...], g_ref[...], dres_ref[...])
            dh_ref[...] = dh
            dhb_ref[...] = (scale * dh).astype(BF16)
            dg_ref[...] += dgain

    return _launch(
        body, name=name, grid=(T // tm, nj),
        in_specs=[pl.BlockSpec((2, tm, guc), lambda i, j: (0, i, j)),
                  ANY,
                  pl.BlockSpec((tm, D), lambda i, j: (i, 0)),
                  pl.BlockSpec((1, D), lambda i, j: (0, 0)),
                  pl.BlockSpec((tm, D), lambda i, j: (i, 0))],
        out_specs=[pl.BlockSpec((tm, D), lambda i, j: (i, 0)),
                   pl.BlockSpec((tm, D), lambda i, j: (i, 0)),
                   pl.BlockSpec((1, D), lambda i, j: (0, 0))],
        out_shape=[jax.ShapeDtypeStruct((T, D), F32), jax.ShapeDtypeStruct((T, D), BF16),
                   jax.ShapeDtypeStruct((1, D), F32)],
        scratch_shapes=[pltpu.VMEM((ns, D, guc), BF16), pltpu.VMEM((tm, D), F32), pltpu.SemaphoreType.DMA],
        args=(dgu, wgu, h, g, dres), comm=comm)


def _ffn_bwd_in_first(dgu, wgu, h, g, dres, *, tm, batch, name, comm=None):
    _, T, ff = dgu.shape
    ns, D, guc = wgu.shape
    nj = ff // guc
    nt = T // tm
    L = T // batch
    per_seq = L // tm
    body_rows = tm - N_META
    chunks = _chunks(guc)

    def body(dgu_ref, w_hbm, h_ref, g_ref, dres_ref, dx_hbm, dmeta_ref, dg_ref, w_v, acc, dh_v, sem, osem):
        i, j = pl.program_id(0), pl.program_id(1)

        @pl.when((i == 0) & (j == 0))
        def _():
            cp = pltpu.make_async_copy(w_hbm, w_v, sem)
            cp.start()
            cp.wait()
            dg_ref[...] = jnp.zeros_like(dg_ref)
            dmeta_ref[...] = jnp.zeros_like(dmeta_ref)

        part = jnp.zeros((tm, D), F32)
        for c0, cw in chunks:
            part = part + _dot_nt(dgu_ref[0, :, c0:c0 + cw], w_v[j, :, c0:c0 + cw])
            part = part + _dot_nt(dgu_ref[1, :, c0:c0 + cw], w_v[nj + j, :, c0:c0 + cw])

        @pl.when(j == 0)
        def _():
            acc[...] = part

        @pl.when(j > 0)
        def _():
            acc[...] += part

        def head_copy(b):
            return pltpu.make_async_copy(dh_v.at[pl.ds(N_META, body_rows)], dx_hbm.at[b, pl.ds(0, body_rows)], osem)

        def tail_copy(b, t):
            return pltpu.make_async_copy(dh_v, dx_hbm.at[b, pl.ds(pl.multiple_of(t * tm - N_META, 8), tm)], osem)

        def on_tile(k, head_fn, tail_fn):
            @pl.when(k % per_seq == 0)
            def _():
                head_fn(head_copy(k // per_seq))

            @pl.when(k % per_seq != 0)
            def _():
                tail_fn(tail_copy(k // per_seq, k % per_seq))

        @pl.when(j == nj - 1)
        def _():
            dh, dgain = _rms_bwd(acc[...], h_ref[...], g_ref[...], dres_ref[...])
            dg_ref[...] += dgain

            @pl.when(i > 0)
            def _():
                on_tile(i - 1, lambda cp: cp.wait(), lambda cp: cp.wait())

            dh_v[...] = dh

            @pl.when(i % per_seq == 0)
            def _():
                dmeta_ref[...] += dh[0:N_META, :]

            on_tile(i, lambda cp: cp.start(), lambda cp: cp.start())

            @pl.when(i == nt - 1)
            def _():
                on_tile(i, lambda cp: cp.wait(), lambda cp: cp.wait())

    return _launch(
        body, name=name, grid=(nt, nj),
        in_specs=[pl.BlockSpec((2, tm, guc), lambda i, j: (0, i, j)),
                  ANY,
                  pl.BlockSpec((tm, D), lambda i, j: (i, 0)),
                  pl.BlockSpec((1, D), lambda i, j: (0, 0)),
                  pl.BlockSpec((tm, D), lambda i, j: (i, 0))],
        out_specs=[ANY, pl.BlockSpec((N_META, D), lambda i, j: (0, 0)), pl.BlockSpec((1, D), lambda i, j: (0, 0))],
        out_shape=[jax.ShapeDtypeStruct((batch, L - N_META, D), F32), jax.ShapeDtypeStruct((N_META, D), F32),
                   jax.ShapeDtypeStruct((1, D), F32)],
        scratch_shapes=[pltpu.VMEM((ns, D, guc), BF16), pltpu.VMEM((tm, D), F32), pltpu.VMEM((tm, D), F32),
                        pltpu.SemaphoreType.DMA, pltpu.SemaphoreType.DMA],
        args=(dgu, wgu, h, g, dres), comm=comm)


def _mix_bwd_in(parts, w_main, w_fg, h, g, dres, *, tm, scale, name, comm=None):
    T, D = h.shape
    widths = [p.shape[1] for p in parts[:-1]]
    offs = [sum(widths[:k]) for k in range(len(widths))]
    npart = len(parts)

    def body(*refs):
        p_refs = refs[:npart]
        wm_ref, wf_ref, h_ref, g_ref, dres_ref, dh_ref, dhb_ref, dg_ref = refs[npart:]

        @pl.when(pl.program_id(0) == 0)
        def _():
            dg_ref[...] = jnp.zeros_like(dg_ref)

        dn = _dot_nt(p_refs[-1][...].astype(BF16), wf_ref[...])
        for p_ref, off, wd_ in zip(p_refs[:-1], offs, widths):
            for c0, cw in _chunks(wd_):
                dn = dn + _dot_nt(p_ref[:, c0:c0 + cw].astype(BF16), wm_ref[:, off + c0:off + c0 + cw])
        dh, dgain = _rms_bwd(dn, h_ref[...], g_ref[...], dres_ref[...])
        dh_ref[...] = dh
        dhb_ref[...] = (scale * dh).astype(BF16)
        dg_ref[...] += dgain

    row = lambda i: (i, 0)
    const = lambda i: (0, 0)
    return _launch(
        body, name=name, grid=(T // tm,),
        in_specs=[pl.BlockSpec((tm, p.shape[1]), row) for p in parts]
                 + [pl.BlockSpec(w_main.shape, const), pl.BlockSpec(w_fg.shape, const),
                    pl.BlockSpec((tm, D), row), pl.BlockSpec((1, D), const), pl.BlockSpec((tm, D), row)],
        out_specs=[pl.BlockSpec((tm, D), row), pl.BlockSpec((tm, D), row), pl.BlockSpec((1, D), const)],
        out_shape=[jax.ShapeDtypeStruct((T, D), F32), jax.ShapeDtypeStruct((T, D), BF16),
                   jax.ShapeDtypeStruct((1, D), F32)],
        args=(*parts, w_main, w_fg, h, g, dres), comm=comm)


def _matmul_tn(xs, y, *, tm, nb, x_specs, y_spec, out_shape, out_spec, kb, silu, name, comm=None):
    T = y.shape[-2]
    nx = len(xs)
    chunks = _chunks(kb)

    def body(*refs):
        x_refs, y_ref, o_ref = refs[:nx], refs[nx], refs[nx + 1]
        i = pl.program_id(1)

        @pl.when(i == 0)
        def _():
            o_ref[...] = jnp.zeros_like(o_ref)

        yv = y_ref[...].astype(BF16)
        for c0, cw in chunks:
            if silu:
                xv = _silu_mul(x_refs[0][:, c0:c0 + cw].astype(F32), x_refs[1][:, c0:c0 + cw].astype(F32)).astype(BF16)
            else:
                xv = x_refs[0][:, c0:c0 + cw]
            o_ref[c0:c0 + cw, :] += _dot_tn(xv, yv)

    return _launch(
        body, name=name, grid=(nb, T // tm),
        in_specs=list(x_specs) + [y_spec], out_specs=out_spec, out_shape=out_shape,
        args=(*xs, y), comm=comm)


def _tri(n, lower):
    r = lax.broadcasted_iota(jnp.int32, (n, n), 0)
    c = lax.broadcasted_iota(jnp.int32, (n, n), 1)
    return jnp.where((r >= c) if lower else (r <= c), 1.0, 0.0).astype(BF16)


def _tri_dot(tri, v):
    hi, mid, lo = _split3(v)
    return _dot(tri, hi) + _dot(tri, mid) + _dot(tri, lo)


def _fcum(fg, bf, *, ch, name):
    B, L, W = fg.shape
    nch = L // ch

    def body(fg_ref, bf_ref, f_ref):
        tri = _tri(ch, True)
        carry = jnp.zeros((1, W), F32)
        for c in range(nch):
            x = fg_ref[c * ch:(c + 1) * ch, :] + bf_ref[...]
            lf = jnp.minimum(x, 0.0) - jnp.log(1.0 + jnp.exp(-jnp.abs(x)))
            f_ref[c * ch:(c + 1) * ch, :] = _tri_dot(tri, lf) + carry
            carry = carry + jnp.sum(lf, axis=0, keepdims=True)

    return pl.pallas_call(
        body, name=name, grid=(B,),
        in_specs=[pl.BlockSpec((None, L, W), lambda b: (b, 0, 0)), pl.BlockSpec((1, W), lambda b: (0, 0))],
        out_specs=pl.BlockSpec((None, L, W), lambda b: (b, 0, 0)),
        out_shape=jax.ShapeDtypeStruct((B, L, W), F32),
        compiler_params=_params("arbitrary"),
    )(fg, bf)


def _fcum_bwd(dF, fg, bf, *, ch, name):
    B, L, W = fg.shape
    nch = L // ch

    def body(df_ref, fg_ref, bf_ref, dfg_ref, db_ref):
        @pl.when(pl.program_id(0) == 0)
        def _():
            db_ref[...] = jnp.zeros_like(db_ref)

        tri = _tri(ch, False)
        carry = jnp.zeros((1, W), F32)
        dbs = jnp.zeros((1, W), F32)
        for c in reversed(range(nch)):
            d = df_ref[c * ch:(c + 1) * ch, :]
            dlf = _tri_dot(tri, d) + carry
            carry = carry + jnp.sum(d, axis=0, keepdims=True)
            x = fg_ref[c * ch:(c + 1) * ch, :] + bf_ref[...]
            dfg = dlf * jax.nn.sigmoid(-x)
            dfg_ref[c * ch:(c + 1) * ch, :] = dfg.astype(BF16)
            dbs = dbs + jnp.sum(dfg, axis=0, keepdims=True)
        db_ref[...] += dbs

    blk = pl.BlockSpec((None, L, W), lambda b: (b, 0, 0))
    return pl.pallas_call(
        body, name=name, grid=(B,),
        in_specs=[blk, blk, pl.BlockSpec((1, W), lambda b: (0, 0))],
        out_specs=[blk, pl.BlockSpec((1, W), lambda b: (0, 0))],
        out_shape=[jax.ShapeDtypeStruct((B, L, W), BF16), jax.ShapeDtypeStruct((1, W), F32)],
        compiler_params=_params("arbitrary"),
    )(dF, fg, bf)


def _band_edges(tq):
    return sorted({min(tq, (k * tq // ATTN_BANDS + HALO - 1) // HALO * HALO) for k in range(ATTN_BANDS + 1)})


def _pair(h):
    return slice((h // 2) * 2 * HEAD_DIM, (h // 2 + 1) * 2 * HEAD_DIM)


def _own_lanes(a, h):
    low = lax.broadcasted_iota(jnp.int32, a.shape, 1) < HEAD_DIM
    return jnp.where(low if h % 2 == 0 else jnp.logical_not(low), a, jnp.zeros_like(a))


def _attn_fwd(proj, fc, fr, *, tq, n_heads, name, comm=None):
    B, L, _ = proj.shape
    AD = n_heads * HEAD_DIM
    nq = L // tq
    W = fc.shape[-1]
    scale = HEAD_DIM ** -0.5
    edges = _band_edges(tq)

    def body(q_ref, k_ref, v_ref, fr_ref, o_ref, lse_ref, m_s, l_s, acc_s):
        qi, ki = pl.program_id(1), pl.program_id(2)

        @pl.when(ki == 0)
        def _():
            m_s[...] = jnp.full_like(m_s, NEG)
            l_s[...] = jnp.zeros_like(l_s)
            acc_s[...] = jnp.zeros_like(acc_s)

        def tile(diagonal):
            lane = lax.broadcasted_iota(jnp.int32, (tq, W), 1)
            m_all, l_all = m_s[...], l_s[...]
            m_out, l_out = m_all, l_all
            bands = [(r0, r1, r1 if diagonal else tq) for r0, r1 in zip(edges[:-1], edges[1:])]
            if diagonal:
                masks = {r0: (lax.broadcasted_iota(jnp.int32, (r1 - r0, c1), 1)
                              <= r0 + lax.broadcasted_iota(jnp.int32, (r1 - r0, c1), 0)) for r0, r1, c1 in bands}

            def scores(h, band):
                r0, r1, c1 = band
                sl = slice(h * HEAD_DIM, (h + 1) * HEAD_DIM)
                return _dot_nt(q_ref[r0:r1, sl] * scale, k_ref[0:c1, sl])

            work = [(h, band) for h in range(n_heads) for band in bands]
            nxt = scores(*work[0])
            for w, (h, band) in enumerate(work):
                r0, r1, c1 = band
                sl = slice(h * HEAD_DIM, (h + 1) * HEAD_DIM)
                s = nxt - fr_ref[h:h + 1, 0:c1]
                if w + 1 < len(work):
                    nxt = scores(*work[w + 1])
                if diagonal:
                    s = jnp.where(masks[r0], s, NEG)
                m_old = m_all[r0:r1, h:h + 1]
                m_new = jnp.maximum(m_old, jnp.max(s, axis=1, keepdims=True))
                alpha = jnp.exp(m_old - m_new)
                p = jnp.exp(s - m_new)
                l_new = alpha * l_all[r0:r1, h:h + 1] + jnp.sum(p, axis=1, keepdims=True)
                acc_s[r0:r1, sl] = alpha * acc_s[r0:r1, sl] + _dot(p.astype(BF16), v_ref[0:c1, sl])
                if r0 == 0:
                    m_parts, l_parts = [], []
                m_parts.append(m_new)
                l_parts.append(l_new)
                if r1 == tq:
                    m_out = jnp.where(lane == h, jnp.concatenate(m_parts, axis=0), m_out)
                    l_out = jnp.where(lane == h, jnp.concatenate(l_parts, axis=0), l_out)
            m_s[...] = m_out
            l_s[...] = l_out

        @pl.when(ki < qi)
        def _():
            tile(False)

        @pl.when(ki == qi)
        def _():
            tile(True)
            l = l_s[...]
            for h in range(n_heads):
                sl = slice(h * HEAD_DIM, (h + 1) * HEAD_DIM)
                o_ref[:, sl] = acc_s[:, sl] / l[:, h:h + 1]
            lse_ref[...] = jnp.where(l > 0.0, m_s[...] + jnp.log(jnp.where(l > 0.0, l, 1.0)), 0.0)

    kv = lambda b, qi, ki: jnp.minimum(ki, qi)
    return _launch(
        body, name=name, grid=(B, nq, nq), args=(proj, proj, proj, fr), comm=comm,
        in_specs=[pl.BlockSpec((None, tq, AD), lambda b, qi, ki: (b, qi, 3)),
                  pl.BlockSpec((None, tq, AD), lambda b, qi, ki: (b, kv(b, qi, ki), 4)),
                  pl.BlockSpec((None, tq, AD), lambda b, qi, ki: (b, kv(b, qi, ki), 5)),
                  pl.BlockSpec((None, None, n_heads, tq), lambda b, qi, ki: (b, kv(b, qi, ki), 0, 0))],
        out_specs=[pl.BlockSpec((None, tq, AD), lambda b, qi, ki: (b, qi, 0)),
                   pl.BlockSpec((None, tq, W), lambda b, qi, ki: (b, qi, 0))],
        out_shape=[jax.ShapeDtypeStruct((B, L, AD), F32), jax.ShapeDtypeStruct((B, L, W), F32)],
        scratch_shapes=[pltpu.VMEM((tq, W), F32), pltpu.VMEM((tq, W), F32), pltpu.VMEM((tq, AD), F32)])


def _attn_bwd(proj, o, do, lse, fc, fr, *, tq, n_heads, name, comm=None):
    B, L, _ = proj.shape
    AD = n_heads * HEAD_DIM
    nq = L // tq
    W = fc.shape[-1]
    scale = HEAD_DIM ** -0.5
    edges = _band_edges(tq)

    def body(q_ref, k_ref, v_ref, o_ref, do_ref, lse_ref, fr_ref,
             dq_ref, dk_ref, dv_ref, dfr_ref, dfq_ref, dk_s, dv_s):
        kj, qi = pl.program_id(1), pl.program_id(2)

        @pl.when((kj == 0) & (qi == 0))
        def _():
            dq_ref[...] = jnp.zeros_like(dq_ref)
            dfq_ref[...] = jnp.zeros_like(dfq_ref)

        @pl.when(qi == kj)
        def _():
            dk_s[...] = jnp.zeros_like(dk_s)
            dv_s[...] = jnp.zeros_like(dv_s)
            dfr_ref[...] = jnp.zeros_like(dfr_ref)

        def tile(diagonal):
            bands = [(r0, r1, r1) for r0, r1 in zip(edges[:-1], edges[1:])] if diagonal else [(0, tq, tq)]
            lse = lse_ref[...]
            for r0, r1, c1 in bands:
                nr = r1 - r0
                rows = pl.ds(pl.multiple_of(qi * tq + r0, 8), nr)
                if diagonal:
                    mask = (lax.broadcasted_iota(jnp.int32, (nr, c1), 1)
                            <= r0 + lax.broadcasted_iota(jnp.int32, (nr, c1), 0))
                lane = lax.broadcasted_iota(jnp.int32, (nr, W), 1)
                head = lax.broadcasted_iota(jnp.int32, (n_heads, c1), 0)
                dfq = jnp.zeros((nr, W), F32)
                dfr = jnp.zeros((n_heads, c1), F32)
                for h in range(n_heads):
                    ps = _pair(h)
                    k, v = k_ref[0:c1, ps], v_ref[0:c1, ps]
                    q = _own_lanes(q_ref[r0:r1, ps] * scale, h)
                    dov = _own_lanes(do_ref[r0:r1, ps], h)
                    s = _dot_nt(q, k) - fr_ref[h:h + 1, 0:c1]
                    if diagonal:
                        s = jnp.where(mask, s, NEG)
                    p = jnp.exp(s - lse[r0:r1, h:h + 1])
                    dp = _dot_nt(dov, v)
                    dsum = jnp.sum(dov.astype(F32) * o_ref[r0:r1, ps], axis=1, keepdims=True)
                    ds = p * (dp - dsum)
                    dsb = ds.astype(BF16)
                    dv = _dot_tn(p.astype(BF16), dov)
                    dk = _dot_tn(dsb, q)
                    dq = _dot(dsb, _own_lanes(k, h))
                    if h % 2 == 0:
                        dv_even, dk_even, dq_even = dv, dk, dq
                    else:
                        dv_s[0:c1, ps] += dv_even + dv
                        dk_s[0:c1, ps] += dk_even + dk
                        dq_ref[rows, ps] += (dq_even + dq) * scale
                    dfr = jnp.where(head == h, jnp.sum(ds, axis=0, keepdims=True), dfr)
                    dfq = jnp.where(lane == h, jnp.sum(ds, axis=1, keepdims=True), dfq)
                dfr_ref[:, 0:c1] -= dfr
                dfq_ref[rows, :] += dfq

        @pl.when(qi > kj)
        def _():
            tile(False)

        @pl.when(qi == kj)
        def _():
            tile(True)

        @pl.when(qi == nq - 1)
        def _():
            dk_ref[...] = dk_s[...].astype(BF16)
            dv_ref[...] = dv_s[...].astype(BF16)

    qq = lambda b, kj, qi: jnp.maximum(qi, kj)
    qblk = lambda w, cb: pl.BlockSpec((None, tq, w), lambda b, kj, qi: (b, qq(b, kj, qi), cb))
    kblk = lambda cb: pl.BlockSpec((None, tq, AD), lambda b, kj, qi: (b, kj, cb))
    return _launch(
        body, name=name, grid=(B, nq, nq), args=(proj, proj, proj, o, do, lse, fr), comm=comm,
        in_specs=[qblk(AD, 3), kblk(4), kblk(5), qblk(AD, 0), qblk(AD, 0), qblk(W, 0),
                  pl.BlockSpec((None, None, n_heads, tq), lambda b, kj, qi: (b, kj, 0, 0))],
        out_specs=[pl.BlockSpec((None, L, AD), lambda b, kj, qi: (b, 0, 0)),
                   kblk(0), kblk(0),
                   pl.BlockSpec((None, None, n_heads, tq), lambda b, kj, qi: (b, kj, 0, 0)),
                   pl.BlockSpec((None, L, W), lambda b, kj, qi: (b, 0, 0))],
        out_shape=[jax.ShapeDtypeStruct((B, L, AD), F32), jax.ShapeDtypeStruct((B, L, AD), BF16),
                   jax.ShapeDtypeStruct((B, L, AD), BF16), jax.ShapeDtypeStruct((B, nq, n_heads, tq), F32),
                   jax.ShapeDtypeStruct((B, L, W), F32)],
        scratch_shapes=[pltpu.VMEM((tq, AD), F32), pltpu.VMEM((tq, AD), F32)])


def _mix_gather(refs, first):
    b_ref, c_ref, hc_ref, cp_ref, hcp_ref, o_ref, cw_ref, p_ref = refs
    bg = b_ref[...].astype(F32)
    u = c_ref[...].astype(F32) * hc_ref[...].astype(F32)
    prev = cp_ref[...].astype(F32) * hcp_ref[...].astype(F32)
    prev = jnp.where(first, 0.0, prev)
    cv, u1, u2 = _causal_conv(u, prev, cw_ref[...])
    yc = bg * cv
    p = p_ref[...]
    rc = lax.rsqrt(_group_mean(yc * yc, p) + EPS)
    ya = o_ref[...].astype(F32)
    ra = lax.rsqrt(_group_mean(ya * ya, p) + EPS)
    return bg, (u, u1, u2), cv, yc * rc, rc, ya * ra, ra


def _mix_specs(tm, CD, D, grid_rank_fn):
    per = tm // HALO
    cur = lambda cb: pl.BlockSpec((None, tm, CD), lambda b, i: (b, i, cb))
    prev = lambda cb: pl.BlockSpec((None, HALO, CD), lambda b, i: (b, jnp.maximum(i * per - 1, 0), cb))
    return [cur(0), cur(1), cur(2), prev(1), prev(2), cur(0)]


def _mix_out(proj, o, cw, gc, ga, wout, h, pmat, next_gain, *, tm, name, comm=None):
    B, L, D = h.shape
    CD = o.shape[-1]
    const = lambda b, i: (0, 0)

    def body(b_ref, c_ref, hc_ref, cp_ref, hcp_ref, o_ref, cw_ref, p_ref, gc_ref, ga_ref, w_ref, h_ref, ng_ref,
             out_ref, y_ref, n_ref):
        first = pl.program_id(1) == 0
        _, _, _, zc, _, za, _ = _mix_gather((b_ref, c_ref, hc_ref, cp_ref, hcp_ref, o_ref, cw_ref, p_ref), first)
        yc = (zc * gc_ref[...]).astype(BF16)
        ya = (za * ga_ref[...]).astype(BF16)
        y_ref[:, :CD] = yc
        y_ref[:, CD:] = ya
        out = h_ref[...] + _dot(yc, w_ref[:CD, :]) + _dot(ya, w_ref[CD:, :])
        out_ref[...] = out
        n_ref[...] = _rms(out, ng_ref[...])

    tile = pl.BlockSpec((None, tm, D), lambda b, i: (b, i, 0))
    return _launch(
        body, name=name, grid=(B, L // tm),
        in_specs=_mix_specs(tm, CD, D, None)
                 + [pl.BlockSpec(cw.shape, const), pl.BlockSpec(pmat.shape, const),
                    pl.BlockSpec((1, CD), const), pl.BlockSpec((1, CD), const), pl.BlockSpec((D, D), const),
                    tile, pl.BlockSpec((1, D), const)],
        out_specs=[tile, tile, tile],
        out_shape=[jax.ShapeDtypeStruct((B, L, D), F32), jax.ShapeDtypeStruct((B, L, D), BF16),
                   jax.ShapeDtypeStruct((B, L, D), BF16)],
        args=(proj, proj, proj, proj, proj, o, cw, pmat, gc, ga, wout, h, next_gain), comm=comm)


def _mix_out_bwd(dhb, proj, o, cw, gc, ga, wout, pmat, *, tm, name, comm=None):
    B, L, D = dhb.shape
    CD = o.shape[-1]
    const = lambda b, i: (0, 0)

    def body(dh_ref, b_ref, c_ref, hc_ref, cp_ref, hcp_ref, o_ref, cw_ref, p_ref, gc_ref, ga_ref, w_ref,
             db_ref, dcv_ref, do_ref, dgc_ref, dga_ref, dcw_ref):
        first = pl.program_id(1) == 0

        @pl.when((pl.program_id(0) == 0) & first)
        def _():
            dgc_ref[...] = jnp.zeros_like(dgc_ref)
            dga_ref[...] = jnp.zeros_like(dga_ref)
            dcw_ref[...] = jnp.zeros_like(dcw_ref)

        bg, us, cv, zc, rc, za, ra = _mix_gather(
            (b_ref, c_ref, hc_ref, cp_ref, hcp_ref, o_ref, cw_ref, p_ref), first)
        p = p_ref[...]
        dh = dh_ref[...]
        dyc = _dot_nt(dh, w_ref[:CD, :])
        dya = _dot_nt(dh, w_ref[CD:, :])

        dgc_ref[...] += jnp.sum(dyc * zc, axis=0, keepdims=True)
        dz = dyc * gc_ref[...]
        dx = rc * (dz - zc * _group_mean(dz * zc, p))
        db_ref[...] = (dx * cv).astype(BF16)
        dcv = dx * bg
        dcv_ref[...] = dcv.astype(BF16)
        for k in range(3):
            dcw_ref[k:k + 1, :] += jnp.sum(dcv * us[2 - k], axis=0, keepdims=True)

        dga_ref[...] += jnp.sum(dya * za, axis=0, keepdims=True)
        dz = dya * ga_ref[...]
        do_ref[...] = (ra * (dz - za * _group_mean(dz * za, p))).astype(BF16)

    tile = lambda w: pl.BlockSpec((None, tm, w), lambda b, i: (b, i, 0))
    return _launch(
        body, name=name, grid=(B, L // tm), comm=comm,
        args=(dhb, proj, proj, proj, proj, proj, o, cw, pmat, gc, ga, wout),
        in_specs=[tile(D)] + _mix_specs(tm, CD, D, None)
                 + [pl.BlockSpec(cw.shape, const), pl.BlockSpec(pmat.shape, const),
                    pl.BlockSpec((1, CD), const), pl.BlockSpec((1, CD), const), pl.BlockSpec((D, D), const)],
        out_specs=[tile(CD), tile(CD), tile(CD),
                   pl.BlockSpec((1, CD), const), pl.BlockSpec((1, CD), const), pl.BlockSpec((8, CD), const)],
        out_shape=[jax.ShapeDtypeStruct((B, L, CD), BF16)] * 3
                  + [jax.ShapeDtypeStruct((1, CD), F32)] * 2 + [jax.ShapeDtypeStruct((8, CD), F32)])


def _conv_bwd(dcv, proj, cw, *, tm, name):
    B, L, CD = dcv.shape
    per = tm // HALO
    nhalo = L // HALO
    nt = L // tm

    def body(d_ref, dn_ref, c_ref, hc_ref, cw_ref, out_ref):
        last = pl.program_id(1) == nt - 1
        d = d_ref[...].astype(F32)
        nxt = jnp.where(last, 0.0, dn_ref[...].astype(F32))
        n0, n1 = _row_of(nxt, 0), _row_of(nxt, 1)
        rows = lax.broadcasted_iota(jnp.int32, d.shape, 0)
        d1 = jnp.where(rows == tm - 1, n0, pltpu.roll(d, tm - 1, 0))
        d2 = jnp.where(rows == tm - 2, n0, jnp.where(rows == tm - 1, n1, pltpu.roll(d, tm - 2, 0)))
        w = cw_ref[...]
        du = w[2:3, :] * d + w[1:2, :] * d1 + w[0:1, :] * d2
        out_ref[:, :CD] = (du * hc_ref[...].astype(F32)).astype(BF16)
        out_ref[:, CD:] = (du * c_ref[...].astype(F32)).astype(BF16)

    return pl.pallas_call(
        body, name=name, grid=(B, nt),
        in_specs=[pl.BlockSpec((None, tm, CD), lambda b, i: (b, i, 0)),
                  pl.BlockSpec((None, HALO, CD), lambda b, i: (b, jnp.minimum((i + 1) * per, nhalo - 1), 0)),
                  pl.BlockSpec((None, tm, CD), lambda b, i: (b, i, 1)),
                  pl.BlockSpec((None, tm, CD), lambda b, i: (b, i, 2)),
                  pl.BlockSpec(cw.shape, lambda b, i: (0, 0))],
        out_specs=pl.BlockSpec((None, tm, 2 * CD), lambda b, i: (b, i, 0)),
        out_shape=jax.ShapeDtypeStruct((B, L, 2 * CD), BF16),
        compiler_params=_params("arbitrary", "arbitrary"),
    )(dcv, dcv, proj, proj, cw)


def _final(h, gf, tgt, *, tm, name):
    B, L, D = h.shape

    def body(h_ref, g_ref, t_ref, dh_ref, dhb_ref, dg_ref, loss_ref):
        b, i = pl.program_id(0), pl.program_id(1)

        @pl.when((b == 0) & (i == 0))
        def _():
            dg_ref[...] = jnp.zeros_like(dg_ref)
            loss_ref[...] = jnp.zeros_like(loss_ref)

        x = h_ref[...]
        g = g_ref[...]
        r = lax.rsqrt(jnp.mean(x * x, axis=-1, keepdims=True) + EPS)
        y = x * r
        pos = i * tm + lax.broadcasted_iota(jnp.int32, (tm, 1), 0)
        err = jnp.where(pos >= N_META, y * g - t_ref[...], 0.0)
        loss_ref[...] += 0.5 * jnp.sum(jnp.mean(err * err, axis=-1, keepdims=True))
        dout = err / D
        dg_ref[...] += jnp.sum(dout * y, axis=0, keepdims=True)
        dy = dout * g
        dh = r * (dy - y * jnp.mean(dy * y, axis=-1, keepdims=True))
        dh_ref[...] = dh
        dhb_ref[...] = (0.5 * dh).astype(BF16)

    tile = pl.BlockSpec((None, tm, D), lambda b, i: (b, i, 0))
    const = lambda b, i: (0, 0)
    return pl.pallas_call(
        body, name=name, grid=(B, L // tm),
        in_specs=[tile, pl.BlockSpec((1, D), const), tile],
        out_specs=[tile, tile, pl.BlockSpec((1, D), const), pl.BlockSpec((1, LANES), const)],
        out_shape=[jax.ShapeDtypeStruct((B, L, D), F32), jax.ShapeDtypeStruct((B, L, D), BF16),
                   jax.ShapeDtypeStruct((1, D), F32), jax.ShapeDtypeStruct((1, LANES), F32)],
        compiler_params=_params("arbitrary", "arbitrary"),
    )(h, gf, tgt)


def _place():
    x, y, c = lax.axis_index("x"), lax.axis_index("y"), lax.axis_index("c")
    others = [(1 - x, y), (x, 1 - y), (1 - x, 1 - y)]
    return x, y, c, others


def _all_gather_shards(shards, *, name):
    n = len(shards)

    def body(*refs):
        ins, outs = refs[:n], refs[n:2 * n]
        send, recv, fsend, frecv, lsem = refs[2 * n:]
        x, y, c, others = _place()
        me = 2 * x + y
        local = [pltpu.make_async_copy(ins[t], outs[t].at[me], lsem.at[t]) for t in range(n)]
        for cp in local:
            cp.start()

        def half(t, k):
            hr = shards[t].shape[0] // 2
            return pl.ds(pl.multiple_of(k * hr, HALO), hr)

        def ici(t, j, src_chip, to):
            src = ins[t].at[half(t, c)] if to is not None else outs[t].at[src_chip, half(t, c)]
            return pltpu.make_async_remote_copy(
                src_ref=src, dst_ref=outs[t].at[src_chip, half(t, c)],
                send_sem=send.at[3 * t + j], recv_sem=recv.at[3 * t + j],
                device_id=(x, y, c) if to is None else to, device_id_type=MESH)

        def d2d(t, j, src_chip, k):
            return pltpu.make_async_remote_copy(
                src_ref=outs[t].at[src_chip, half(t, k)], dst_ref=outs[t].at[src_chip, half(t, k)],
                send_sem=fsend.at[3 * t + j], recv_sem=frecv.at[3 * t + j],
                device_id=(x, y, 1 - c), device_id_type=MESH)

        firsts = [ici(t, j, me, (ox, oy, c)) for t in range(n) for j, (ox, oy) in enumerate(others)]
        for cp in firsts:
            cp.start()
        passed = []
        for t in range(n):
            for j, (ox, oy) in enumerate(others):
                ici(t, j, 2 * ox + oy, None).wait_recv()
                cp = d2d(t, j, 2 * ox + oy, c)
                cp.start()
                passed.append(cp)
        for t in range(n):
            for j, (ox, oy) in enumerate(others):
                d2d(t, j, 2 * ox + oy, 1 - c).wait_recv()
        for cp in firsts + passed:
            cp.wait_send()
        for cp in local:
            cp.wait()

    return pl.pallas_call(
        body, name=name,
        in_specs=[ANY] * n, out_specs=[ANY] * n,
        out_shape=[jax.ShapeDtypeStruct((N_SHARD,) + s.shape, s.dtype) for s in shards],
        scratch_shapes=[pltpu.SemaphoreType.DMA((3 * n,))] * 4 + [pltpu.SemaphoreType.DMA((n,))],
    )(*shards)


def _all_reduce_small(slab, *, name):
    def body(in_ref, out_ref, gath, send, recv):
        x, y, c, _ = _place()
        me = 4 * x + 2 * y + c
        gath[me] = in_ref[...]
        copies, peers = [], []
        for m in range(1, N_DEV):
            px = jnp.where((m >> 2) & 1, 1 - x, x)
            py = jnp.where((m >> 1) & 1, 1 - y, y)
            pc = jnp.where(m & 1, 1 - c, c)
            cp = pltpu.make_async_remote_copy(
                src_ref=in_ref, dst_ref=gath.at[me], send_sem=send.at[m - 1], recv_sem=recv.at[m - 1],
                device_id=(px, py, pc), device_id_type=MESH)
            cp.start()
            copies.append(cp)
            peers.append(4 * px + 2 * py + pc)
        for m in range(1, N_DEV):
            pltpu.make_async_remote_copy(
                src_ref=in_ref, dst_ref=gath.at[peers[m - 1]], send_sem=send.at[m - 1], recv_sem=recv.at[m - 1],
                device_id=(x, y, c), device_id_type=MESH).wait_recv()
        for cp in copies:
            cp.wait_send()
        acc = gath[0]
        for k in range(1, N_DEV):
            acc = acc + gath[k]
        out_ref[...] = acc

    vm = pl.BlockSpec(memory_space=pltpu.VMEM)
    return pl.pallas_call(
        body, name=name, in_specs=[vm], out_specs=vm,
        out_shape=jax.ShapeDtypeStruct(slab.shape, slab.dtype),
        scratch_shapes=[pltpu.VMEM((N_DEV,) + slab.shape, slab.dtype),
                        pltpu.SemaphoreType.DMA((N_DEV - 1,)), pltpu.SemaphoreType.DMA((N_DEV - 1,))],
    )(slab)


def _gather_ici(shards):
    n = len(shards)

    def copies(ins, outs, sems, sending):
        send, recv, _ = sems
        x, y, c, others = _place()
        me = 2 * x + y
        out = []
        for t in range(n):
            hr = shards[t].shape[0] // 2
            rows = pl.ds(pl.multiple_of(c * hr, HALO), hr)
            for j, (ox, oy) in enumerate(others):
                src_chip = me if sending else 2 * ox + oy
                out.append(pltpu.make_async_remote_copy(
                    src_ref=ins[t].at[rows], dst_ref=outs[t].at[src_chip, rows],
                    send_sem=send.at[3 * t + j], recv_sem=recv.at[3 * t + j],
                    device_id=(ox, oy, c) if sending else (x, y, c), device_id_type=MESH))
        return out

    def local(ins, outs, sems):
        x, y, _, _ = _place()
        return [pltpu.make_async_copy(ins[t], outs[t].at[2 * x + y], sems[2].at[t]) for t in range(n)]

    def start(ins, outs, sems):
        for cp in local(ins, outs, sems) + copies(ins, outs, sems, True):
            cp.start()

    def finish(ins, outs, sems):
        for cp in copies(ins, outs, sems, False):
            cp.wait_recv()
        for cp in copies(ins, outs, sems, True):
            cp.wait_send()
        for cp in local(ins, outs, sems):
            cp.wait()

    return _Comm(shards, [jax.ShapeDtypeStruct((N_SHARD,) + s.shape, s.dtype) for s in shards],
                 [3 * n, 3 * n, n], start, finish)


def _gather_d2d(parts):
    n = len(parts)

    def copies(outs, sems, sending):
        send, recv = sems
        x, y, c, others = _place()
        out = []
        for t in range(n):
            hr = parts[t].shape[1] // 2
            rows = pl.ds(pl.multiple_of((c if sending else 1 - c) * hr, HALO), hr)
            for j, (ox, oy) in enumerate(others):
                blk = outs[t].at[2 * ox + oy, rows]
                out.append(pltpu.make_async_remote_copy(
                    src_ref=blk, dst_ref=blk, send_sem=send.at[3 * t + j], recv_sem=recv.at[3 * t + j],
                    device_id=(x, y, 1 - c) if sending else (x, y, c), device_id_type=MESH))
        return out

    def start(ins, outs, sems):
        for cp in copies(outs, sems, True):
            cp.start()

    def finish(ins, outs, sems):
        for cp in copies(outs, sems, False):
            cp.wait_recv()
        for cp in copies(outs, sems, True):
            cp.wait_send()

    return _Comm(parts, [jax.ShapeDtypeStruct(p.shape, p.dtype) for p in parts], [3 * n, 3 * n], start, finish,
                 aliases={t: t for t in range(n)})


def _swap_halves(grads):
    n = len(grads)

    def copies(ins, outs, sems):
        x, y, c, _ = _place()
        out = []
        for t in range(n):
            hr = grads[t].shape[1] // 2
            rows = pl.ds(pl.multiple_of((1 - c) * hr, 8), hr)
            out.append(pltpu.make_async_remote_copy(
                src_ref=ins[t].at[:, rows, :], dst_ref=outs[t], send_sem=sems[0].at[t], recv_sem=sems[1].at[t],
                device_id=(x, y, 1 - c), device_id_type=MESH))
        return out

    def start(ins, outs, sems):
        for cp in copies(ins, outs, sems):
            cp.start()

    def finish(ins, outs, sems):
        for cp in copies(ins, outs, sems):
            cp.wait()

    return _Comm(grads, [jax.ShapeDtypeStruct((N_SHARD, g.shape[1] // 2, g.shape[2]), g.dtype) for g in grads],
                 [n, n], start, finish)


def _pair_sum(g, got, c, *, name):
    ns, R, C = g.shape
    hr = R // 2

    def body(c_ref, g_ref, r_ref, o_ref):
        o_ref[...] = (g_ref[...] + r_ref[...]).astype(BF16)

    return pl.pallas_call(
        body, name=name,
        grid_spec=pltpu.PrefetchScalarGridSpec(
            num_scalar_prefetch=1, grid=(ns,),
            in_specs=[pl.BlockSpec((None, hr, C), lambda s, cr: (s, cr[0], 0)),
                      pl.BlockSpec((None, hr, C), lambda s, cr: (s, 0, 0))],
            out_specs=pl.BlockSpec((None, hr, C), lambda s, cr: (s, 0, 0))),
        out_shape=jax.ShapeDtypeStruct((ns, hr, C), BF16),
        compiler_params=_params("arbitrary"),
    )(c, g, got)


def _scatter_chips(sums):
    n = len(sums)

    def copies(ins, outs, sems, sending):
        x, y, c, others = _place()
        me = 2 * x + y
        out = []
        for t in range(n):
            for j, (ox, oy) in enumerate(others):
                there = 2 * ox + oy
                out.append(pltpu.make_async_remote_copy(
                    src_ref=ins[t].at[there if sending else me], dst_ref=outs[t].at[me if sending else there],
                    send_sem=sems[0].at[3 * t + j], recv_sem=sems[1].at[3 * t + j],
                    device_id=(ox, oy, c) if sending else (x, y, c), device_id_type=MESH))
        return out

    def start(ins, outs, sems):
        for cp in copies(ins, outs, sems, True):
            cp.start()

    def finish(ins, outs, sems):
        for cp in copies(ins, outs, sems, False):
            cp.wait_recv()
        for cp in copies(ins, outs, sems, True):
            cp.wait_send()

    return _Comm(sums, [jax.ShapeDtypeStruct(s.shape, s.dtype) for s in sums], [3 * n, 3 * n], start, finish)


def _chip_sum(g, got, landed, idx, *, name):
    ns, R, C = g.shape
    hr = R // 2

    def body(i_ref, g_ref, r_ref, a_ref, b_ref, c_ref, o_ref):
        acc = g_ref[...] + r_ref[...]
        for ref in (a_ref, b_ref, c_ref):
            acc = acc + ref[...].astype(F32)
        o_ref[...] = acc

    other = lambda k: pl.BlockSpec((None, hr, C), lambda s, ir: (ir[2 + k], 0, 0))
    return pl.pallas_call(
        body, name=name,
        grid_spec=pltpu.PrefetchScalarGridSpec(
            num_scalar_prefetch=1, grid=(1,),
            in_specs=[pl.BlockSpec((None, hr, C), lambda s, ir: (ir[0], ir[1], 0)),
                      pl.BlockSpec((None, hr, C), lambda s, ir: (ir[0], 0, 0)),
                      other(0), other(1), other(2)],
            out_specs=pl.BlockSpec((hr, C), lambda s, ir: (ir[1], 0))),
        out_shape=jax.ShapeDtypeStruct((R, C), F32),
        compiler_params=_params("arbitrary"),
    )(idx, g, got, landed, landed, landed)


def _share_halves(halves):
    n = len(halves)

    def copies(outs, sems, sending):
        x, y, c, _ = _place()
        out = []
        for t in range(n):
            hr = halves[t].shape[0] // 2
            rows = pl.ds(pl.multiple_of((c if sending else 1 - c) * hr, 8), hr)
            out.append(pltpu.make_async_remote_copy(
                src_ref=outs[t].at[rows, :], dst_ref=outs[t].at[rows, :], send_sem=sems[0].at[t],
                recv_sem=sems[1].at[t], device_id=(x, y, 1 - c) if sending else (x, y, c), device_id_type=MESH))
        return out

    def start(ins, outs, sems):
        for cp in copies(outs, sems, True):
            cp.start()

    def finish(ins, outs, sems):
        for cp in copies(outs, sems, False):
            cp.wait_recv()
        for cp in copies(outs, sems, True):
            cp.wait_send()

    return _Comm(halves, [jax.ShapeDtypeStruct(h.shape, h.dtype) for h in halves], [n, n], start, finish,
                 aliases={t: t for t in range(n)})


def _adamw(w, g, m, v, *, name):
    R, C = w.shape
    tr = R
    for cand in (256, 128, 64, 32, 16, 8):
        if R % cand == 0:
            tr = cand
            break

    def body(w_ref, g_ref, m_ref, v_ref, d_ref, mo_ref, vo_ref):
        gv = g_ref[...]
        mn = ADAM_B1 * m_ref[...] + (1.0 - ADAM_B1) * gv
        vn = ADAM_B2 * v_ref[...] + (1.0 - ADAM_B2) * (gv * gv)
        m_hat = mn / (1.0 - ADAM_B1 ** ADAM_STEP)
        v_hat = vn / (1.0 - ADAM_B2 ** ADAM_STEP)
        d_ref[...] = -ADAM_LR * (m_hat / (jnp.sqrt(v_hat) + ADAM_EPS) + ADAM_WD * w_ref[...])
        mo_ref[...] = mn
        vo_ref[...] = vn

    blk = pl.BlockSpec((tr, C), lambda i: (i, 0))
    return pl.pallas_call(
        body, name=name, grid=(R // tr,), in_specs=[blk] * 4, out_specs=[blk] * 3,
        out_shape=[jax.ShapeDtypeStruct((R, C), F32)] * 3,
        compiler_params=_params("arbitrary"),
    )(w, g, m, v)


def _pack_small(D, meta, n1, nm, n3, nf, gc, ga, bf, cw):
    def row(a):
        a = a.reshape(-1, a.shape[-1])
        return jnp.pad(a, ((0, 0), (0, D - a.shape[-1])))
    rows = [row(meta), row(n1), row(nm), row(n3), row(nf), row(jnp.concatenate([gc, ga], axis=-1)), row(bf), row(cw)]
    slab = jnp.concatenate(rows, axis=0)
    return jnp.pad(slab, ((0, SMALL_ROWS - slab.shape[0]), (0, 0)))


def _unpack_small(slab, like):
    meta, n1, nm, n3, nf, gc, ga, bf, cw = like
    nmeta, mc = meta.shape
    out = [slab[:nmeta, :mc].reshape(meta.shape)]
    r = nmeta
    for a in (n1, nm, n3, nf):
        out.append(slab[r, :a.shape[-1]].reshape(a.shape))
        r += 1
    cd = gc.shape[-1]
    out.append(slab[r, :cd].reshape(gc.shape))
    out.append(slab[r, cd:cd + ga.shape[-1]].reshape(ga.shape))
    r += 1
    out.append(slab[r, :bf.shape[-1]].reshape(bf.shape))
    r += 1
    out.append(slab[r:r + 3, :cw.shape[-1]].reshape(cw.shape))
    return out


def kernel(x, meta_tokens, ffn1_norm, ffn1_w_gu, ffn1_w_down, mix_norm, w_in, conv_w, b_f, out_norm_conv, out_norm_attn, w_out, ffn2_norm, ffn2_w_gu, ffn2_w_down, final_norm, loss_target, m_meta_tokens, m_ffn1_norm, m_ffn1_w_gu, m_ffn1_w_down, m_mix_norm, m_w_in, m_conv_w, m_b_f, m_out_norm_conv, m_out_norm_attn, m_w_out, m_ffn2_norm, m_ffn2_w_gu, m_ffn2_w_down, m_final_norm, v_meta_tokens, v_ffn1_norm, v_ffn1_w_gu, v_ffn1_w_down, v_mix_norm, v_w_in, v_conv_w, v_b_f, v_out_norm_conv, v_out_norm_attn, v_w_out, v_ffn2_norm, v_ffn2_w_gu, v_ffn2_w_down, v_final_norm):
    B, S, D = x.shape
    L = S + N_META
    T = B * L
    tm = L // 3
    assert tm * 3 == L and tm % HALO == 0
    guc = ffn1_w_gu.shape[-1]
    ff = N_SHARD * guc // 2
    H = b_f.shape[-1]
    AD = H * HEAD_DIM
    CD = conv_w.shape[-1] * N_SHARD
    assert CD == AD and CD + AD == D and CD % LANES == 0
    n_main = 3 * CD + 3 * AD
    ins = w_in.shape[-1]

    xi, yi, ci = lax.axis_index("x"), lax.axis_index("y"), lax.axis_index("c")
    chip = 2 * xi + yi

    small_shard = jnp.zeros((2 * HALO, meta_tokens.shape[-1]), F32)
    small_shard = small_shard.at[:N_META].set(meta_tokens)
    small_shard = small_shard.at[N_META:N_META + 3, :conv_w.shape[-1]].set(conv_w[0])
    big = [ffn1_w_gu[0], ffn1_w_down[0], w_in[0], w_out[0], ffn2_w_gu[0], ffn2_w_down[0]]
    wgu1_s, wd1_s, win_s, wout_s, wgu2_s, wd2_s = [w.astype(BF16) for w in big]
    small_g, = _all_gather_shards([small_shard], name="gather_small")
    meta_f = jnp.moveaxis(small_g[:, :N_META], 0, 1).reshape(N_META, D)
    cw_f = jnp.moveaxis(small_g[:, N_META:N_META + 3, :conv_w.shape[-1]], 0, 1).reshape(3, CD)
    cw8 = jnp.pad(cw_f, ((0, 5), (0, 0)))
    bf_p = jnp.pad(b_f, ((0, 0), (0, LANES - H)))
    gid = jnp.arange(CD) // HEAD_DIM
    pmat = jnp.where(gid[:, None] == gid[None, :], 1.0 / HEAD_DIM, 0.0).astype(BF16)

    gu_shape = jax.ShapeDtypeStruct((2, T, ff), BF16)
    gu_w_spec = pl.BlockSpec((None, D, guc), lambda s, i: (s, 0, 0))
    gu_o_spec = pl.BlockSpec((None, tm, guc), lambda s, i: (s // 2, i, s % 2))

    h0 = jnp.concatenate([jnp.broadcast_to(meta_f[None], (B, N_META, D)), x], axis=1).reshape(T, D)
    sid = ((chip + jnp.arange(N_SHARD, dtype=jnp.int32)) % N_SHARD).astype(jnp.int32)
    n1, wgu1_h = _rmsnorm(h0, ffn1_norm, tm=tm, name="ffn1_norm", comm=_gather_ici([wgu1_s]))
    gu1, out = _ffn_up(n1, wgu1_s[None], sid, None, tm=tm, first=0, count=1, name="ffn1_up_own",
                       comm=_join(_gather_d2d(wgu1_h), _gather_ici([wd1_s])))
    wgu1, wd1_h = out[0], out[1:]
    gu1, out = _ffn_up(n1, wgu1, sid, gu1, tm=tm, first=1, count=N_SHARD - 1, name="ffn1_up_rest",
                       comm=_join(_gather_d2d(wd1_h), _gather_ici([win_s, wout_s])))
    wd1, mix_w = out[0].reshape(ff, D), out[1:]
    (h1, n2), (win_g, wout_g) = _ffn_down(gu1, wd1, h0, mix_norm, tm=tm, name="ffn1_down", comm=_gather_d2d(mix_w))
    wout_f = wout_g.reshape(D, D)
    win_f = jnp.moveaxis(win_g, 0, 1).reshape(D, N_SHARD * ins)
    win_main = win_f[:, :n_main]
    win_fg = jnp.pad(win_f[:, n_main:], ((0, 0), (0, LANES - H)))

    proj, _ = _matmul_nn(n2, win_main, tm=tm, nb=n_main // (3 * CD),
                         w_spec=pl.BlockSpec((D, 3 * CD), lambda s, i: (0, s)),
                         out_shape=jax.ShapeDtypeStruct((T, n_main), BF16),
                         out_spec=pl.BlockSpec((tm, 3 * CD), lambda s, i: (i, s)), name="mix_in")
    fg, _ = _matmul_nn(n2, win_fg, tm=tm, nb=1, w_spec=pl.BlockSpec((D, LANES), lambda s, i: (0, 0)),
                       out_shape=jax.ShapeDtypeStruct((T, LANES), F32),
                       out_spec=pl.BlockSpec((tm, LANES), lambda s, i: (i, 0)), name="mix_in_fg")
    proj3 = proj.reshape(B, L, n_main)
    fg3 = fg.reshape(B, L, LANES)
    fc = _fcum(fg3, bf_p, ch=tm, name="forget_cumsum")
    fr = fc[:, :, :H].reshape(B, L // tm, tm, H).transpose(0, 1, 3, 2)
    (o, lse), ffn2_w = _attn_fwd(proj3, fc, fr, tq=tm, n_heads=H, name="attn_fwd",
                                 comm=_gather_ici([wgu2_s, wd2_s]))
    (h2, ymix, n3), (wgu2, wd2) = _mix_out(
        proj3, o, cw8, out_norm_conv, out_norm_attn, wout_f, h1.reshape(B, L, D), pmat, ffn2_norm,
        tm=tm, name="mix_out", comm=_gather_d2d(ffn2_w))
    wd2 = wd2.reshape(ff, D)
    h2 = h2.reshape(T, D)
    n3 = n3.reshape(T, D)

    gu2, _ = _matmul_nn(n3, wgu2, tm=tm, nb=N_SHARD, w_spec=gu_w_spec, out_shape=gu_shape, out_spec=gu_o_spec,
                        name="ffn2_up")
    (dh3f, dh3b, d_gf, loss_part), _ = _ffn_down_loss(gu2, wd2, h2, final_norm.reshape(1, D), loss_target,
                                                      tm=tm, name="ffn2_down_loss")

    c_arr = jnp.reshape(ci, (1,)).astype(jnp.int32)
    ks = jnp.arange(N_SHARD - 1, dtype=jnp.int32)
    idx = jnp.concatenate([jnp.stack([chip, ci]).astype(jnp.int32), ks + (ks >= chip).astype(jnp.int32)])

    def pair_sums(grads, got, names):
        return [_pair_sum(g, r, c_arr, name="pair_sum_" + nm) for g, r, nm in zip(grads, got, names)]

    def chip_sums(grads, got, landed, names):
        return [_chip_sum(g, r, l, idx, name="chip_sum_" + nm) for g, r, l, nm in zip(grads, got, landed, names)]

    def dw_up(n, dgu, name, comm=None):
        return _matmul_tn(
            [n], dgu, tm=tm, nb=N_SHARD, kb=D, silu=False,
            x_specs=[pl.BlockSpec((tm, D), lambda s, i: (i, 0))],
            y_spec=pl.BlockSpec((None, tm, guc), lambda s, i: (s // 2, i, s % 2)),
            out_shape=jax.ShapeDtypeStruct((N_SHARD, D, guc), F32),
            out_spec=pl.BlockSpec((None, D, guc), lambda s, i: (s, 0, 0)), name=name, comm=comm)

    (dgu2, d_wd2), _ = _ffn_bwd_act(dh3b, gu2, wd2, tm=tm, guc=guc, name="ffn2_bwd_act")
    (dh2, dh2b, d_g3), _ = _ffn_bwd_in(dgu2, wgu2, h2, ffn2_norm, dh3f, tm=tm, scale=1.0, name="ffn2_bwd_in")
    d_wgu2, _ = dw_up(n3, dgu2, "ffn2_dw_up")
    grads_f2 = [d_wgu2, d_wd2.reshape(N_SHARD, ff // N_SHARD, D)]
    names_f2 = ["wgu2", "wd2"]

    dh2b3 = dh2b.reshape(B, L, D)
    (d_bg, d_cv, d_o, d_gc, d_ga, d_cw), got_f2 = _mix_out_bwd(
        dh2b3, proj3, o, cw8, out_norm_conv, out_norm_attn, wout_f, pmat, tm=tm, name="mix_out_bwd",
        comm=_swap_halves(grads_f2))
    sums_f2 = pair_sums(grads_f2, got_f2, names_f2)
    d_wout, _ = _matmul_tn(
        [ymix.reshape(T, D)], dh2b, tm=tm, nb=1, kb=D, silu=False,
        x_specs=[pl.BlockSpec((tm, D), lambda s, i: (i, 0))], y_spec=pl.BlockSpec((tm, D), lambda s, i: (i, 0)),
        out_shape=jax.ShapeDtypeStruct((D, D), F32), out_spec=pl.BlockSpec((D, D), lambda s, i: (0, 0)),
        name="dw_out")
    d_cc = _conv_bwd(d_cv, proj3, cw8, tm=tm, name="conv_bwd")
    (d_q, d_k, d_v, d_fr, d_fq), landed_f2 = _attn_bwd(proj3, o, d_o, lse, fc, fr, tq=tm, n_heads=H, name="attn_bwd",
                                                       comm=_scatter_chips(sums_f2))
    halves_f2 = chip_sums(grads_f2, got_f2, landed_f2, names_f2)
    d_fc = d_fq + jnp.pad(d_fr.transpose(0, 1, 3, 2).reshape(B, L, H), ((0, 0), (0, 0), (0, LANES - H)))
    d_fg, d_bf = _fcum_bwd(d_fc, fg3, bf_p, ch=tm, name="forget_cumsum_bwd")

    parts = [d_bg.reshape(T, CD), d_cc.reshape(T, 2 * CD), d_q.reshape(T, AD), d_k.reshape(T, AD),
             d_v.reshape(T, AD), d_fg.reshape(T, LANES)]
    (dh1, dh1b, d_gm), g_f2 = _mix_bwd_in(parts, win_main, win_fg, h1, mix_norm, dh2, tm=tm, scale=0.5,
                                          name="mix_bwd_in", comm=_share_halves(halves_f2))
    d_win_parts = []
    for k, p in enumerate(parts):
        wdt = p.shape[1]
        nb = max(wdt // CD, 1)
        bw = wdt // nb
        d_win_parts.append(_matmul_tn(
            [n2], p, tm=tm, nb=nb, kb=D, silu=False,
            x_specs=[pl.BlockSpec((tm, D), lambda s, i: (i, 0))], y_spec=pl.BlockSpec((tm, bw), lambda s, i: (i, s)),
            out_shape=jax.ShapeDtypeStruct((D, wdt), F32), out_spec=pl.BlockSpec((D, bw), lambda s, i: (0, s)),
            name="dw_in_%d" % k)[0])
    d_win_parts[-1] = d_win_parts[-1][:, :H]
    d_win = jnp.moveaxis(jnp.concatenate(d_win_parts, axis=1).reshape(D, N_SHARD, ins), 1, 0)
    grads_mx = [d_win, d_wout.reshape(N_SHARD, D // N_SHARD, D)]
    names_mx = ["win", "wout"]

    (dgu1, d_wd1), got_mx = _ffn_bwd_act(dh1b, gu1, wd1, tm=tm, guc=guc, name="ffn1_bwd_act",
                                         comm=_swap_halves(grads_mx))
    sums_mx = pair_sums(grads_mx, got_mx, names_mx)
    grads_d1 = [d_wd1.reshape(N_SHARD, ff // N_SHARD, D)]
    d_wgu1, out = dw_up(n1, dgu1, "ffn1_dw_up", comm=_join(_scatter_chips(sums_mx), _swap_halves(grads_d1)))
    landed_mx, got_d1 = out[:2], out[2:]
    halves_mx = chip_sums(grads_mx, got_mx, landed_mx, names_mx)
    sums_d1 = pair_sums(grads_d1, got_d1, ["wd1"])
    grads_u1 = [d_wgu1]
    (grad_x, d_meta, d_g1), out = _ffn_bwd_in_first(
        dgu1, wgu1, h0, ffn1_norm, dh1, tm=tm, batch=B, name="ffn1_bwd_in",
        comm=_join(_join(_share_halves(halves_mx), _scatter_chips(sums_d1)), _swap_halves(grads_u1)))
    g_mx, landed_d1, got_u1 = out[:2], out[2:3], out[3:]
    halves_d1 = chip_sums(grads_d1, got_d1, landed_d1, ["wd1"])
    sums_u1 = pair_sums(grads_u1, got_u1, ["wgu1"])
    out = _run_comm(_join(_share_halves(halves_d1), _scatter_chips(sums_u1)), name="scatter_ffn1")
    g_d1, landed_u1 = out[:1], out[1:]
    halves_u1 = chip_sums(grads_u1, got_u1, landed_u1, ["wgu1"])
    g_u1 = _run_comm(_share_halves(halves_u1), name="share_ffn1")
    g_big = [g_u1[0], g_d1[0], g_mx[0], g_mx[1], g_f2[0], g_f2[1]]

    loss_row = jnp.zeros((1, D), F32).at[0, 0].set(loss_part[0, 0])
    slab = _pack_small(D, d_meta, d_g1, d_gm, d_g3, d_gf, d_gc, d_ga, d_bf[:, :H], d_cw[:3])
    slab = slab.at[SMALL_ROWS - 1].set(loss_row[0])
    total = _all_reduce_small(slab, name="reduce_small")
    loss = total[SMALL_ROWS - 1, 0]
    mcols = meta_tokens.shape[-1]
    ccols = conv_w.shape[-1]
    full_like = (jnp.zeros((N_META, D)), ffn1_norm, mix_norm, ffn2_norm, final_norm.reshape(1, D), out_norm_conv,
                 out_norm_attn, b_f, jnp.zeros((1, 3, CD)))
    g_small = _unpack_small(total, full_like)
    g_small[0] = lax.dynamic_slice_in_dim(g_small[0], chip * mcols, mcols, axis=1)
    g_small[8] = lax.dynamic_slice_in_dim(g_small[8], chip * ccols, ccols, axis=2)

    def small_slab(meta, a1, am, a3, af, gc, ga, bf, cw):
        return _pack_small(D, meta, a1, am, a3, af.reshape(1, D), gc, ga, bf, cw[0])

    w_small = small_slab(meta_tokens, ffn1_norm, mix_norm, ffn2_norm, final_norm, out_norm_conv, out_norm_attn, b_f, conv_w)
    m_small = small_slab(m_meta_tokens, m_ffn1_norm, m_mix_norm, m_ffn2_norm, m_final_norm, m_out_norm_conv,
                         m_out_norm_attn, m_b_f, m_conv_w)
    v_small = small_slab(v_meta_tokens, v_ffn1_norm, v_mix_norm, v_ffn2_norm, v_final_norm, v_out_norm_conv,
                         v_out_norm_attn, v_b_f, v_conv_w)
    gs = list(g_small)
    gs[4] = gs[4].reshape(final_norm.shape)
    g_slab = small_slab(gs[0], gs[1], gs[2], gs[3], gs[4], gs[5], gs[6], gs[7], gs[8])
    local_like = (meta_tokens, ffn1_norm, mix_norm, ffn2_norm, final_norm.reshape(1, D), out_norm_conv, out_norm_attn,
                  b_f, conv_w)
    small_out = [_unpack_small(s, local_like) for s in _adamw(w_small, g_slab, m_small, v_small, name="adamw_small")]
    for lst in small_out:
        lst[4] = lst[4].reshape(final_norm.shape)

    names = ["wgu1", "wd1", "win", "wout", "wgu2", "wd2"]
    w_big = big
    m_big = [m_ffn1_w_gu[0], m_ffn1_w_down[0], m_w_in[0], m_w_out[0], m_ffn2_w_gu[0], m_ffn2_w_down[0]]
    v_big = [v_ffn1_w_gu[0], v_ffn1_w_down[0], v_w_in[0], v_w_out[0], v_ffn2_w_gu[0], v_ffn2_w_down[0]]
    big_out = [_adamw(w, g, m, v, name="adamw_" + nm) for w, g, m, v, nm in zip(w_big, g_big, m_big, v_big, names)]

    def assemble(small, bigs):
        meta, a1, am, a3, af, gc, ga, bf, cw = small
        gu1_, d1_, win_, wout_, gu2_, d2_ = [b[None] for b in bigs]
        return [meta, a1, gu1_, d1_, am, win_, cw, bf, gc, ga, wout_, a3, gu2_, d2_, af]

    gs_out = list(g_small)
    gs_out[4] = gs_out[4].reshape(final_norm.shape)
    grads_out = assemble(gs_out, g_big)
    delta_out = assemble(small_out[0], [b[0] for b in big_out])
    m_out = assemble(small_out[1], [b[1] for b in big_out])
    v_out = assemble(small_out[2], [b[2] for b in big_out])
    return (loss, grad_x, *grads_out, *delta_out, *m_out, *v_out)
```

```python
import functools

import jax
import jax.numpy as jnp
from jax import lax
from jax.experimental import pallas as pl
from jax.experimental.pallas import tpu as pltpu

F32 = jnp.float32
BF16 = jnp.bfloat16

EPS = 1e-6
N_META = 16
HEAD_DIM = 64
N_SHARD = 4
N_DEV = 8
HALO = 16
LANES = 128
SMALL_ROWS = 32
VMEM_LIMIT_V7X = 56 * 1024 * 1024
NEG = -1e30
ATTN_BANDS = 2

ADAM_LR = 0.001
ADAM_B1 = 0.9
ADAM_B2 = 0.999
ADAM_EPS = 1e-08
ADAM_WD = 0.01
ADAM_STEP = 10

MESH = pl.DeviceIdType.MESH
ANY = pl.BlockSpec(memory_space=pl.ANY)
NT_DIMS = (((1,), (1,)), ((), ()))
TN_DIMS = (((0,), (0,)), ((), ()))


def _params(*sem):
    return pltpu.CompilerParams(dimension_semantics=sem, vmem_limit_bytes=VMEM_LIMIT_V7X)


class _Comm:
    def __init__(self, ins, out_shapes, sems, start, finish, aliases=None):
        self.ins, self.out_shapes, self.sems = list(ins), list(out_shapes), list(sems)
        self.start, self.finish, self.aliases = start, finish, dict(aliases or {})


def _join(a, b):
    ni, no, ns = len(a.ins), len(a.out_shapes), len(a.sems)

    def start(ins, outs, sems):
        a.start(ins[:ni], outs[:no], sems[:ns])
        b.start(ins[ni:], outs[no:], sems[ns:])

    def finish(ins, outs, sems):
        a.finish(ins[:ni], outs[:no], sems[:ns])
        b.finish(ins[ni:], outs[no:], sems[ns:])

    aliases = dict(a.aliases)
    aliases.update({ni + i: no + j for i, j in b.aliases.items()})
    return _Comm(a.ins + b.ins, a.out_shapes + b.out_shapes, a.sems + b.sems, start, finish, aliases)


def _launch(body, *, name, grid, in_specs, out_specs, out_shape, args, scratch_shapes=(), comm=None, prefetch=(),
            aliases=None):
    single = not isinstance(out_shape, (list, tuple))
    out_specs = [out_specs] if single else list(out_specs)
    out_shape = [out_shape] if single else list(out_shape)
    in_specs, scratch_shapes, prefetch = list(in_specs), list(scratch_shapes), list(prefetch)
    params = _params(*(("arbitrary",) * len(grid)))
    n_pf, n_in, n_out, n_scr = len(prefetch), len(in_specs), len(out_specs), len(scratch_shapes)
    c_ins = comm.ins if comm else []
    c_shapes = comm.out_shapes if comm else []
    c_sems = comm.sems if comm else []
    c_in, c_out = len(c_ins), len(c_shapes)

    def carrier(*refs):
        p = 0
        pf = refs[p:p + n_pf]; p += n_pf
        a = refs[p:p + n_in]; p += n_in
        ci = refs[p:p + c_in]; p += c_in
        o = refs[p:p + n_out]; p += n_out
        co = refs[p:p + c_out]; p += c_out
        s = refs[p:p + n_scr]; p += n_scr
        cs = refs[p:]
        if comm:
            first = functools.reduce(lambda u, v: u & v, [pl.program_id(k) == 0 for k in range(len(grid))])

            @pl.when(first)
            def _():
                comm.start(ci, co, cs)

        body(*pf, *a, *o, *s)

        if comm:
            last = functools.reduce(lambda u, v: u & v, [pl.program_id(k) == grid[k] - 1 for k in range(len(grid))])

            @pl.when(last)
            def _():
                comm.finish(ci, co, cs)

    io_aliases = {n_pf + i: j for i, j in (aliases or {}).items()}
    if comm:
        io_aliases.update({n_pf + n_in + i: n_out + j for i, j in comm.aliases.items()})
    all_in, all_out = in_specs + [ANY] * c_in, out_specs + [ANY] * c_out
    all_scratch = scratch_shapes + [pltpu.SemaphoreType.DMA((k,)) for k in c_sems]
    if n_pf:
        spec = dict(grid_spec=pltpu.PrefetchScalarGridSpec(
            num_scalar_prefetch=n_pf, grid=grid, in_specs=all_in, out_specs=all_out, scratch_shapes=all_scratch))
    else:
        spec = dict(grid=grid, in_specs=all_in, out_specs=all_out, scratch_shapes=all_scratch)
    res = pl.pallas_call(carrier, name=name, out_shape=out_shape + c_shapes, input_output_aliases=io_aliases,
                         compiler_params=params, **spec)(*prefetch, *args, *c_ins)
    main = list(res[:n_out])
    return (main[0] if single else main), (list(res[n_out:]) if comm else None)


def _run_comm(comm, *, name):
    c_in, c_out = len(comm.ins), len(comm.out_shapes)

    def body(*refs):
        ci, co, cs = refs[:c_in], refs[c_in:c_in + c_out], refs[c_in + c_out:]
        comm.start(ci, co, cs)
        comm.finish(ci, co, cs)

    return list(pl.pallas_call(
        body, name=name, in_specs=[ANY] * c_in, out_specs=[ANY] * c_out, out_shape=comm.out_shapes,
        scratch_shapes=[pltpu.SemaphoreType.DMA((k,)) for k in comm.sems],
        input_output_aliases=comm.aliases)(*comm.ins))


def _chunks(width, step=512):
    out, c0 = [], 0
    while c0 < width:
        cw = min(step, width - c0)
        out.append((c0, cw))
        c0 += cw
    return out


def _split2(v):
    hi = v.astype(BF16)
    lo = (v - hi.astype(F32)).astype(BF16)
    return hi, lo


def _split3(v):
    hi = v.astype(BF16)
    r = v - hi.astype(F32)
    mid = r.astype(BF16)
    lo = (r - mid.astype(F32)).astype(BF16)
    return hi, mid, lo


def _dot(a, b):
    return jnp.dot(a, b, preferred_element_type=F32)


def _dot_nt(a, b):
    return lax.dot_general(a, b, NT_DIMS, preferred_element_type=F32)


def _dot_tn(a, b):
    return lax.dot_general(a, b, TN_DIMS, preferred_element_type=F32)


def _silu_mul(g, u):
    return g * jax.nn.sigmoid(g) * u


def _rms_bwd(dn, h, gain, dres):
    r = lax.rsqrt(jnp.mean(h * h, axis=-1, keepdims=True) + EPS)
    y = h * r
    dgain = jnp.sum(dn * y, axis=0, keepdims=True)
    dy = dn * gain
    dh = dres + r * (dy - y * jnp.mean(dy * y, axis=-1, keepdims=True))
    return dh, dgain


def _group_mean(v, p):
    hi, lo = _split2(v)
    return _dot(hi, p) + _dot(lo, p)


def _row_of(a, k):
    rows = lax.broadcasted_iota(jnp.int32, a.shape, 0)
    return jnp.sum(jnp.where(rows == k, a, 0.0), axis=0, keepdims=True)


def _causal_conv(u, prev, w):
    rows = lax.broadcasted_iota(jnp.int32, u.shape, 0)
    p1 = _row_of(prev, HALO - 1)
    p2 = _row_of(prev, HALO - 2)
    u1 = jnp.where(rows == 0, p1, pltpu.roll(u, 1, 0))
    u2 = jnp.where(rows == 0, p2, jnp.where(rows == 1, p1, pltpu.roll(u, 2, 0)))
    return w[2:3, :] * u + w[1:2, :] * u1 + w[0:1, :] * u2, u1, u2


def _rms(x, gain):
    return (x * lax.rsqrt(jnp.mean(x * x, axis=-1, keepdims=True) + EPS) * gain).astype(BF16)


def _embed_norm(x, meta, g, *, tm, name, comm=None):
    B, S, D = x.shape
    L = S + N_META
    per_seq = L // tm
    nt = B * per_seq
    body_rows = tm - N_META

    def body(meta_ref, g_ref, x_hbm, h_ref, n_ref, buf, sems):
        i = pl.program_id(0)

        def fetch(k, fn):
            slot, b, t = k % 2, k // per_seq, k % per_seq

            @pl.when(t == 0)
            def _():
                fn(pltpu.make_async_copy(x_hbm.at[b, pl.ds(0, body_rows)],
                                         buf.at[slot, pl.ds(N_META, body_rows)], sems.at[slot]))

            @pl.when(t != 0)
            def _():
                fn(pltpu.make_async_copy(x_hbm.at[b, pl.ds(pl.multiple_of(t * tm - N_META, 8), tm)],
                                         buf.at[slot], sems.at[slot]))

        @pl.when(i == 0)
        def _():
            fetch(i, lambda cp: cp.start())

        @pl.when(i + 1 < nt)
        def _():
            fetch(i + 1, lambda cp: cp.start())

        fetch(i, lambda cp: cp.wait())
        slot = i % 2

        @pl.when(i % per_seq == 0)
        def _():
            buf[slot, 0:N_META, :] = meta_ref[...]

        hv = buf[slot]
        h_ref[...] = hv
        n_ref[...] = _rms(hv, g_ref[...])

    row = pl.BlockSpec((tm, D), lambda i: (i, 0))
    return _launch(
        body, name=name, grid=(nt,),
        in_specs=[pl.BlockSpec((N_META, D), lambda i: (0, 0)), pl.BlockSpec((1, D), lambda i: (0, 0)), ANY],
        out_specs=[row, row],
        out_shape=[jax.ShapeDtypeStruct((B * L, D), F32), jax.ShapeDtypeStruct((B * L, D), BF16)],
        scratch_shapes=[pltpu.VMEM((2, tm, D), F32), pltpu.SemaphoreType.DMA((2,))],
        args=(meta, g, x), comm=comm)


def _ffn_up(n, wgu, sid, gu_prev, *, tm, first, count, name, comm=None):
    T, D = n.shape
    ns, _, guc = wgu.shape
    ff = N_SHARD * guc // 2

    def body(sid_ref, x_ref, w_ref, *rest):
        rest[-1][...] = _dot(x_ref[...], w_ref[...]).astype(BF16)

    where = lambda s, sid: sid[first + s]
    w_at = (lambda s, sid: 0) if ns == 1 else where
    return _launch(
        body, name=name, grid=(count, T // tm), prefetch=(sid,),
        in_specs=[pl.BlockSpec((tm, D), lambda s, i, sid: (i, 0)),
                  pl.BlockSpec((None, D, guc), lambda s, i, sid: (w_at(s, sid), 0, 0))]
                 + ([] if gu_prev is None else [ANY]),
        out_specs=pl.BlockSpec((None, tm, guc), lambda s, i, sid: (where(s, sid) // 2, i, where(s, sid) % 2)),
        out_shape=jax.ShapeDtypeStruct((2, T, ff), BF16),
        args=(n, wgu) + (() if gu_prev is None else (gu_prev,)),
        aliases=None if gu_prev is None else {2: 0}, comm=comm)


def _matmul_nn(x, w, *, tm, nb, w_spec, out_shape, out_spec, name, comm=None):
    T, K = x.shape

    def body(x_ref, w_ref, o_ref):
        o_ref[...] = _dot(x_ref[...], w_ref[...]).astype(o_ref.dtype)

    return _launch(
        body, name=name, grid=(nb, T // tm),
        in_specs=[pl.BlockSpec((tm, K), lambda s, i: (i, 0)), w_spec],
        out_specs=out_spec, out_shape=out_shape, args=(x, w), comm=comm)


def _ffn_down(gu, wd, h, next_gain, *, tm, name, comm=None):
    _, T, ff = gu.shape
    D = h.shape[1]
    chunks = _chunks(ff)

    def body(g_ref, u_ref, wd_hbm, h_ref, ng_ref, o_ref, n_ref, wd_v, sem):
        @pl.when(pl.program_id(0) == 0)
        def _():
            cp = pltpu.make_async_copy(wd_hbm, wd_v, sem)
            cp.start()
            cp.wait()

        acc = jnp.zeros((tm, D), F32)
        for c0, cw in chunks:
            a = _silu_mul(g_ref[:, c0:c0 + cw].astype(F32), u_ref[:, c0:c0 + cw].astype(F32))
            acc = acc + _dot(a.astype(BF16), wd_v[c0:c0 + cw, :])
        out = h_ref[...] + 0.5 * acc
        o_ref[...] = out
        n_ref[...] = _rms(out, ng_ref[...])

    return _launch(
        body, name=name, grid=(T // tm,),
        in_specs=[pl.BlockSpec((None, tm, ff), lambda i: (0, i, 0)),
                  pl.BlockSpec((None, tm, ff), lambda i: (1, i, 0)),
                  ANY,
                  pl.BlockSpec((tm, D), lambda i: (i, 0)),
                  pl.BlockSpec((1, D), lambda i: (0, 0))],
        out_specs=[pl.BlockSpec((tm, D), lambda i: (i, 0)), pl.BlockSpec((tm, D), lambda i: (i, 0))],
        out_shape=[jax.ShapeDtypeStruct((T, D), F32), jax.ShapeDtypeStruct((T, D), BF16)],
        scratch_shapes=[pltpu.VMEM((ff, D), BF16), pltpu.SemaphoreType.DMA],
        args=(gu, gu, wd, h, next_gain), comm=comm)


def _ffn_down_loss(gu, wd, h, gf, tgt, *, tm, name, comm=None):
    _, T, ff = gu.shape
    D = h.shape[1]
    B, S, _ = tgt.shape
    per_seq = (S + N_META) // tm
    body_rows = tm - N_META
    chunks = _chunks(ff)

    def body(g_ref, u_ref, wd_hbm, h_ref, gf_ref, tgt_hbm, dh_ref, dhb_ref, dg_ref, loss_ref, wd_v, tg_v, sem, tsem):
        i = pl.program_id(0)
        b, t = i // per_seq, i % per_seq

        @pl.when(i == 0)
        def _():
            cp = pltpu.make_async_copy(wd_hbm, wd_v, sem)
            cp.start()
            cp.wait()
            dg_ref[...] = jnp.zeros_like(dg_ref)
            loss_ref[...] = jnp.zeros_like(loss_ref)
            tg_v[0:N_META, :] = jnp.zeros((N_META, D), F32)

        def fetch(fn):
            @pl.when(t == 0)
            def _():
                fn(pltpu.make_async_copy(tgt_hbm.at[b, pl.ds(0, body_rows)], tg_v.at[pl.ds(N_META, body_rows)], tsem))

            @pl.when(t != 0)
            def _():
                fn(pltpu.make_async_copy(tgt_hbm.at[b, pl.ds(pl.multiple_of(t * tm - N_META, 8), tm)], tg_v, tsem))

        fetch(lambda cp: cp.start())
        acc = jnp.zeros((tm, D), F32)
        for c0, cw in chunks:
            a = _silu_mul(g_ref[:, c0:c0 + cw].astype(F32), u_ref[:, c0:c0 + cw].astype(F32))
            acc = acc + _dot(a.astype(BF16), wd_v[c0:c0 + cw, :])
        x = h_ref[...] + 0.5 * acc
        fetch(lambda cp: cp.wait())

        gain = gf_ref[...]
        r = lax.rsqrt(jnp.mean(x * x, axis=-1, keepdims=True) + EPS)
        y = x * r
        pos = t * tm + lax.broadcasted_iota(jnp.int32, (tm, 1), 0)
        err = jnp.where(pos >= N_META, y * gain - tg_v[...], 0.0)
        loss_ref[...] += 0.5 * jnp.sum(jnp.mean(err * err, axis=-1, keepdims=True))
        dout = err / D
        dg_ref[...] += jnp.sum(dout * y, axis=0, keepdims=True)
        dy = dout * gain
        dh = r * (dy - y * jnp.mean(dy * y, axis=-1, keepdims=True))
        dh_ref[...] = dh
        dhb_ref[...] = (0.5 * dh).astype(BF16)

    row = pl.BlockSpec((tm, D), lambda i: (i, 0))
    const = lambda i: (0, 0)
    return _launch(
        body, name=name, grid=(T // tm,),
        in_specs=[pl.BlockSpec((None, tm, ff), lambda i: (0, i, 0)),
                  pl.BlockSpec((None, tm, ff), lambda i: (1, i, 0)),
                  ANY, row, pl.BlockSpec((1, D), const), ANY],
        out_specs=[row, row, pl.BlockSpec((1, D), const), pl.BlockSpec((1, LANES), const)],
        out_shape=[jax.ShapeDtypeStruct((T, D), F32), jax.ShapeDtypeStruct((T, D), BF16),
                   jax.ShapeDtypeStruct((1, D), F32), jax.ShapeDtypeStruct((1, LANES), F32)],
        scratch_shapes=[pltpu.VMEM((ff, D), BF16), pltpu.VMEM((tm, D), F32), pltpu.SemaphoreType.DMA,
                        pltpu.SemaphoreType.DMA],
        args=(gu, gu, wd, h, gf, tgt), comm=comm)


def _ffn_bwd_act(df, gu, wd, *, tm, guc, name, comm=None):
    _, T, ff = gu.shape
    D = df.shape[1]
    nj = ff // guc
    chunks = _chunks(guc)

    def body(df_ref, g_ref, u_ref, wd_ref, o_ref, dwd_ref):
        @pl.when(pl.program_id(1) == 0)
        def _():
            dwd_ref[...] = jnp.zeros_like(dwd_ref)

        dfv = df_ref[...]
        for c0, cw in chunks:
            da = _dot_nt(dfv, wd_ref[c0:c0 + cw, :])
            g = g_ref[:, c0:c0 + cw].astype(F32)
            u = u_ref[:, c0:c0 + cw].astype(F32)
            sg = jax.nn.sigmoid(g)
            silu = g * sg
            o_ref[0, :, c0:c0 + cw] = (da * u * (sg * (1.0 + g * (1.0 - sg)))).astype(BF16)
            o_ref[1, :, c0:c0 + cw] = (da * silu).astype(BF16)
            dwd_ref[c0:c0 + cw, :] += _dot_tn((silu * u).astype(BF16), dfv)

    return _launch(
        body, name=name, grid=(nj, T // tm),
        in_specs=[pl.BlockSpec((tm, D), lambda j, i: (i, 0)),
                  pl.BlockSpec((None, tm, guc), lambda j, i: (0, i, j)),
                  pl.BlockSpec((None, tm, guc), lambda j, i: (1, i, j)),
                  pl.BlockSpec((guc, D), lambda j, i: (j, 0))],
        out_specs=[pl.BlockSpec((2, tm, guc), lambda j, i: (0, i, j)), pl.BlockSpec((guc, D), lambda j, i: (j, 0))],
        out_shape=[jax.ShapeDtypeStruct((2, T, ff), BF16), jax.ShapeDtypeStruct((ff, D), F32)],
        args=(df, gu, gu, wd), comm=comm)


def _ffn_bwd_in(dgu, wgu, h, g, dres, *, tm, scale, name, comm=None):
    _, T, ff = dgu.shape
    ns, D, guc = wgu.shape
    nj = ff // guc
    chunks = _chunks(guc)

    def body(dgu_ref, w_hbm, h_ref, g_ref, dres_ref, dh_ref, dhb_ref, dg_ref, w_v, acc, sem):
        i, j = pl.program_id(0), pl.program_id(1)

        @pl.when((i == 0) & (j == 0))
        def _():
            cp = pltpu.make_async_copy(w_hbm, w_v, sem)
            cp.start()
            cp.wait()
            dg_ref[...] = jnp.zeros_like(dg_ref)

        part = jnp.zeros((tm, D), F32)
        for c0, cw in chunks:
            part = part + _dot_nt(dgu_ref[0, :, c0:c0 + cw], w_v[j, :, c0:c0 + cw])
            part = part + _dot_nt(dgu_ref[1, :, c0:c0 + cw], w_v[nj + j, :, c0:c0 + cw])

        @pl.when(j == 0)
        def _():
            acc[...] = part

        @pl.when(j > 0)
        def _():
            acc[...] += part

        @pl.when(j == nj - 1)
        def _():
            dh, dgain = _rms_bwd(acc[...], h_ref[...], g_ref[...], dres_ref[...])
            dh_ref[...] = dh
            dhb_ref[...] = (scale * dh).astype(BF16)
            dg_ref[...] += dgain

    return _launch(
        body, name=name, grid=(T // tm, nj),
        in_specs=[pl.BlockSpec((2, tm, guc), lambda i, j: (0, i, j)),
                  ANY,
                  pl.BlockSpec((tm, D), lambda i, j: (i, 0)),
                  pl.BlockSpec((1, D), lambda i, j: (0, 0)),
                  pl.BlockSpec((tm, D), lambda i, j: (i, 0))],
        out_specs=[pl.BlockSpec((tm, D), lambda i, j: (i, 0)),
                   pl.BlockSpec((tm, D), lambda i, j: (i, 0)),
                   pl.BlockSpec((1, D), lambda i, j: (0, 0))],
        out_shape=[jax.ShapeDtypeStruct((T, D), F32), jax.ShapeDtypeStruct((T, D), BF16),
                   jax.ShapeDtypeStruct((1, D), F32)],
        scratch_shapes=[pltpu.VMEM((ns, D, guc), BF16), pltpu.VMEM((tm, D), F32), pltpu.SemaphoreType.DMA],
        args=(dgu, wgu, h, g, dres), comm=comm)


def _ffn_bwd_in_first(dgu, wgu, h, g, dres, *, tm, batch, name, comm=None):
    _, T, ff = dgu.shape
    ns, D, guc = wgu.shape
    nj = ff // guc
    nt = T // tm
    L = T // batch
    per_seq = L // tm
    body_rows = tm - N_META
    chunks = _chunks(guc)

    def body(dgu_ref, w_hbm, h_ref, g_ref, dres_ref, dx_hbm, dmeta_ref, dg_ref, w_v, acc, dh_v, sem, osem):
        i, j = pl.program_id(0), pl.program_id(1)

        @pl.when((i == 0) & (j == 0))
        def _():
            cp = pltpu.make_async_copy(w_hbm, w_v, sem)
            cp.start()
            cp.wait()
            dg_ref[...] = jnp.zeros_like(dg_ref)
            dmeta_ref[...] = jnp.zeros_like(dmeta_ref)

        part = jnp.zeros((tm, D), F32)
        for c0, cw in chunks:
            part = part + _dot_nt(dgu_ref[0, :, c0:c0 + cw], w_v[j, :, c0:c0 + cw])
            part = part + _dot_nt(dgu_ref[1, :, c0:c0 + cw], w_v[nj + j, :, c0:c0 + cw])

        @pl.when(j == 0)
        def _():
            acc[...] = part

        @pl.when(j > 0)
        def _():
            acc[...] += part

        def head_copy(b):
            return pltpu.make_async_copy(dh_v.at[pl.ds(N_META, body_rows)], dx_hbm.at[b, pl.ds(0, body_rows)], osem)

        def tail_copy(b, t):
            return pltpu.make_async_copy(dh_v, dx_hbm.at[b, pl.ds(pl.multiple_of(t * tm - N_META, 8), tm)], osem)

        def on_tile(k, head_fn, tail_fn):
            @pl.when(k % per_seq == 0)
            def _():
                head_fn(head_copy(k // per_seq))

            @pl.when(k % per_seq != 0)
            def _():
                tail_fn(tail_copy(k // per_seq, k % per_seq))

        @pl.when(j == nj - 1)
        def _():
            dh, dgain = _rms_bwd(acc[...], h_ref[...], g_ref[...], dres_ref[...])
            dg_ref[...] += dgain

            @pl.when(i > 0)
            def _():
                on_tile(i - 1, lambda cp: cp.wait(), lambda cp: cp.wait())

            dh_v[...] = dh

            @pl.when(i % per_seq == 0)
            def _():
                dmeta_ref[...] += dh[0:N_META, :]

            on_tile(i, lambda cp: cp.start(), lambda cp: cp.start())

            @pl.when(i == nt - 1)
            def _():
                on_tile(i, lambda cp: cp.wait(), lambda cp: cp.wait())

    return _launch(
        body, name=name, grid=(nt, nj),
        in_specs=[pl.BlockSpec((2, tm, guc), lambda i, j: (0, i, j)),
                  ANY,
                  pl.BlockSpec((tm, D), lambda i, j: (i, 0)),
                  pl.BlockSpec((1, D), lambda i, j: (0, 0)),
                  pl.BlockSpec((tm, D), lambda i, j: (i, 0))],
        out_specs=[ANY, pl.BlockSpec((N_META, D), lambda i, j: (0, 0)), pl.BlockSpec((1, D), lambda i, j: (0, 0))],
        out_shape=[jax.ShapeDtypeStruct((batch, L - N_META, D), F32), jax.ShapeDtypeStruct((N_META, D), F32),
                   jax.ShapeDtypeStruct((1, D), F32)],
        scratch_shapes=[pltpu.VMEM((ns, D, guc), BF16), pltpu.VMEM((tm, D), F32), pltpu.VMEM((tm, D), F32),
                        pltpu.SemaphoreType.DMA, pltpu.SemaphoreType.DMA],
        args=(dgu, wgu, h, g, dres), comm=comm)


def _mix_bwd_in(parts, w_main, w_fg, h, g, dres, *, tm, scale, name, comm=None):
    T, D = h.shape
    widths = [p.shape[1] for p in parts]
    offs = [sum(widths[:k]) for k in range(len(widths))]
    npart = len(parts)
    wide = sum(widths)

    def body(*refs):
        p_refs = refs[:npart]
        wm_ref, wf_ref, h_ref, g_ref, dres_ref, dh_ref, dhb_ref, dg_ref, all_ref = refs[npart:]

        @pl.when(pl.program_id(0) == 0)
        def _():
            dg_ref[...] = jnp.zeros_like(dg_ref)

        dn = jnp.zeros((tm, D), F32)
        for k, (p_ref, off, wd_) in enumerate(zip(p_refs, offs, widths)):
            for c0, cw in _chunks(wd_):
                piece = p_ref[:, c0:c0 + cw].astype(BF16)
                all_ref[:, off + c0:off + c0 + cw] = piece
                w = wf_ref[...] if k == npart - 1 else wm_ref[:, off + c0:off + c0 + cw]
                dn = dn + _dot_nt(piece, w)
        dh, dgain = _rms_bwd(dn, h_ref[...], g_ref[...], dres_ref[...])
        dh_ref[...] = dh
        dhb_ref[...] = (scale * dh).astype(BF16)
        dg_ref[...] += dgain

    row = lambda i: (i, 0)
    const = lambda i: (0, 0)
    return _launch(
        body, name=name, grid=(T // tm,),
        in_specs=[pl.BlockSpec((tm, p.shape[1]), row) for p in parts]
                 + [pl.BlockSpec(w_main.shape, const), pl.BlockSpec(w_fg.shape, const),
                    pl.BlockSpec((tm, D), row), pl.BlockSpec((1, D), const), pl.BlockSpec((tm, D), row)],
        out_specs=[pl.BlockSpec((tm, D), row), pl.BlockSpec((tm, D), row), pl.BlockSpec((1, D), const),
                   pl.BlockSpec((tm, wide), row)],
        out_shape=[jax.ShapeDtypeStruct((T, D), F32), jax.ShapeDtypeStruct((T, D), BF16),
                   jax.ShapeDtypeStruct((1, D), F32), jax.ShapeDtypeStruct((T, wide), BF16)],
        args=(*parts, w_main, w_fg, h, g, dres), comm=comm)


def _matmul_tn(x, y, *, tm, nb, x_spec, y_spec, out_shape, out_spec, kb, name, comm=None):
    T = y.shape[-2]
    chunks = _chunks(kb)

    def body(x_ref, y_ref, o_ref):
        @pl.when(pl.program_id(1) == 0)
        def _():
            o_ref[...] = jnp.zeros_like(o_ref)

        yv = y_ref[...].astype(BF16)
        for c0, cw in chunks:
            o_ref[c0:c0 + cw, :] += _dot_tn(x_ref[:, c0:c0 + cw], yv)

    return _launch(
        body, name=name, grid=(nb, T // tm),
        in_specs=[x_spec, y_spec], out_specs=out_spec, out_shape=out_shape, args=(x, y), comm=comm)


def _tri(n, lower):
    r = lax.broadcasted_iota(jnp.int32, (n, n), 0)
    c = lax.broadcasted_iota(jnp.int32, (n, n), 1)
    return jnp.where((r >= c) if lower else (r <= c), 1.0, 0.0).astype(BF16)


def _tri_dot(tri, v):
    hi, mid, lo = _split3(v)
    return _dot(tri, hi) + _dot(tri, mid) + _dot(tri, lo)


def _fcum(fg, bf, *, ch, name):
    B, L, W = fg.shape
    nch = L // ch

    def body(fg_ref, bf_ref, f_ref):
        tri = _tri(ch, True)
        carry = jnp.zeros((1, W), F32)
        for c in range(nch):
            x = fg_ref[c * ch:(c + 1) * ch, :] + bf_ref[...]
            lf = jnp.minimum(x, 0.0) - jnp.log(1.0 + jnp.exp(-jnp.abs(x)))
            f_ref[c * ch:(c + 1) * ch, :] = _tri_dot(tri, lf) + carry
            carry = carry + jnp.sum(lf, axis=0, keepdims=True)

    return pl.pallas_call(
        body, name=name, grid=(B,),
        in_specs=[pl.BlockSpec((None, L, W), lambda b: (b, 0, 0)), pl.BlockSpec((1, W), lambda b: (0, 0))],
        out_specs=pl.BlockSpec((None, L, W), lambda b: (b, 0, 0)),
        out_shape=jax.ShapeDtypeStruct((B, L, W), F32),
        compiler_params=_params("arbitrary"),
    )(fg, bf)


def _fcum_bwd(dF, fg, bf, *, ch, name):
    B, L, W = fg.shape
    nch = L // ch

    def body(df_ref, fg_ref, bf_ref, dfg_ref, db_ref):
        @pl.when(pl.program_id(0) == 0)
        def _():
            db_ref[...] = jnp.zeros_like(db_ref)

        tri = _tri(ch, False)
        carry = jnp.zeros((1, W), F32)
        dbs = jnp.zeros((1, W), F32)
        for c in reversed(range(nch)):
            d = df_ref[c * ch:(c + 1) * ch, :]
            dlf = _tri_dot(tri, d) + carry
            carry = carry + jnp.sum(d, axis=0, keepdims=True)
            x = fg_ref[c * ch:(c + 1) * ch, :] + bf_ref[...]
            dfg = dlf * jax.nn.sigmoid(-x)
            dfg_ref[c * ch:(c + 1) * ch, :] = dfg.astype(BF16)
            dbs = dbs + jnp.sum(dfg, axis=0, keepdims=True)
        db_ref[...] += dbs

    blk = pl.BlockSpec((None, L, W), lambda b: (b, 0, 0))
    return pl.pallas_call(
        body, name=name, grid=(B,),
        in_specs=[blk, blk, pl.BlockSpec((1, W), lambda b: (0, 0))],
        out_specs=[blk, pl.BlockSpec((1, W), lambda b: (0, 0))],
        out_shape=[jax.ShapeDtypeStruct((B, L, W), BF16), jax.ShapeDtypeStruct((1, W), F32)],
        compiler_params=_params("arbitrary"),
    )(dF, fg, bf)


def _band_edges(tq):
    return sorted({min(tq, (k * tq // ATTN_BANDS + HALO - 1) // HALO * HALO) for k in range(ATTN_BANDS + 1)})


def _pair(h):
    return slice((h // 2) * 2 * HEAD_DIM, (h // 2 + 1) * 2 * HEAD_DIM)


def _own_lanes(a, h):
    low = lax.broadcasted_iota(jnp.int32, a.shape, 1) < HEAD_DIM
    return jnp.where(low if h % 2 == 0 else jnp.logical_not(low), a, jnp.zeros_like(a))


def _attn_fwd(proj, fc, fr, *, tq, n_heads, name, comm=None):
    B, L, _ = proj.shape
    AD = n_heads * HEAD_DIM
    nq = L // tq
    W = fc.shape[-1]
    scale = HEAD_DIM ** -0.5
    edges = _band_edges(tq)

    def body(q_ref, k_ref, v_ref, fr_ref, o_ref, lse_ref, m_s, l_s, acc_s):
        qi, ki = pl.program_id(1), pl.program_id(2)

        @pl.when(ki == 0)
        def _():
            m_s[...] = jnp.full_like(m_s, NEG)
            l_s[...] = jnp.zeros_like(l_s)
            acc_s[...] = jnp.zeros_like(acc_s)

        def tile(diagonal):
            lane = lax.broadcasted_iota(jnp.int32, (tq, W), 1)
            m_all, l_all = m_s[...], l_s[...]
            m_out, l_out = m_all, l_all
            bands = [(r0, r1, r1 if diagonal else tq) for r0, r1 in zip(edges[:-1], edges[1:])]
            if diagonal:
                masks = {r0: (lax.broadcasted_iota(jnp.int32, (r1 - r0, c1), 1)
                              <= r0 + lax.broadcasted_iota(jnp.int32, (r1 - r0, c1), 0)) for r0, r1, c1 in bands}

            def scores(h, band):
                r0, r1, c1 = band
                sl = slice(h * HEAD_DIM, (h + 1) * HEAD_DIM)
                return _dot_nt(q_ref[r0:r1, sl] * scale, k_ref[0:c1, sl])

            work = [(h, band) for h in range(n_heads) for band in bands]
            nxt = scores(*work[0])
            for w, (h, band) in enumerate(work):
                r0, r1, c1 = band
                sl = slice(h * HEAD_DIM, (h + 1) * HEAD_DIM)
                s = nxt - fr_ref[h:h + 1, 0:c1]
                if w + 1 < len(work):
                    nxt = scores(*work[w + 1])
                if diagonal:
                    s = jnp.where(masks[r0], s, NEG)
                m_old = m_all[r0:r1, h:h + 1]
                m_new = jnp.maximum(m_old, jnp.max(s, axis=1, keepdims=True))
                alpha = jnp.exp(m_old - m_new)
                p = jnp.exp(s - m_new)
                l_new = alpha * l_all[r0:r1, h:h + 1] + jnp.sum(p, axis=1, keepdims=True)
                acc_s[r0:r1, sl] = alpha * acc_s[r0:r1, sl] + _dot(p.astype(BF16), v_ref[0:c1, sl])
                if r0 == 0:
                    m_parts, l_parts = [], []
                m_parts.append(m_new)
                l_parts.append(l_new)
                if r1 == tq:
                    m_out = jnp.where(lane == h, jnp.concatenate(m_parts, axis=0), m_out)
                    l_out = jnp.where(lane == h, jnp.concatenate(l_parts, axis=0), l_out)
            m_s[...] = m_out
            l_s[...] = l_out

        @pl.when(ki < qi)
        def _():
            tile(False)

        @pl.when(ki == qi)
        def _():
            tile(True)
            l = l_s[...]
            for h in range(n_heads):
                sl = slice(h * HEAD_DIM, (h + 1) * HEAD_DIM)
                o_ref[:, sl] = acc_s[:, sl] / l[:, h:h + 1]
            lse_ref[...] = jnp.where(l > 0.0, m_s[...] + jnp.log(jnp.where(l > 0.0, l, 1.0)), 0.0)

    kv = lambda b, qi, ki: jnp.minimum(ki, qi)
    return _launch(
        body, name=name, grid=(B, nq, nq), args=(proj, proj, proj, fr), comm=comm,
        in_specs=[pl.BlockSpec((None, tq, AD), lambda b, qi, ki: (b, qi, 3)),
                  pl.BlockSpec((None, tq, AD), lambda b, qi, ki: (b, kv(b, qi, ki), 4)),
                  pl.BlockSpec((None, tq, AD), lambda b, qi, ki: (b, kv(b, qi, ki), 5)),
                  pl.BlockSpec((None, None, n_heads, tq), lambda b, qi, ki: (b, kv(b, qi, ki), 0, 0))],
        out_specs=[pl.BlockSpec((None, tq, AD), lambda b, qi, ki: (b, qi, 0)),
                   pl.BlockSpec((None, tq, W), lambda b, qi, ki: (b, qi, 0))],
        out_shape=[jax.ShapeDtypeStruct((B, L, AD), F32), jax.ShapeDtypeStruct((B, L, W), F32)],
        scratch_shapes=[pltpu.VMEM((tq, W), F32), pltpu.VMEM((tq, W), F32), pltpu.VMEM((tq, AD), F32)])


def _attn_bwd(proj, o, do, lse, fc, fr, *, tq, n_heads, name, comm=None):
    B, L, _ = proj.shape
    AD = n_heads * HEAD_DIM
    nq = L // tq
    W = fc.shape[-1]
    scale = HEAD_DIM ** -0.5
    edges = _band_edges(tq)

    def body(q_ref, k_ref, v_ref, o_ref, do_ref, lse_ref, fr_ref,
             dq_ref, dk_ref, dv_ref, dfr_ref, dfq_ref, dk_s, dv_s):
        kj, qi = pl.program_id(1), pl.program_id(2)

        @pl.when((kj == 0) & (qi == 0))
        def _():
            dq_ref[...] = jnp.zeros_like(dq_ref)
            dfq_ref[...] = jnp.zeros_like(dfq_ref)

        @pl.when(qi == kj)
        def _():
            dk_s[...] = jnp.zeros_like(dk_s)
            dv_s[...] = jnp.zeros_like(dv_s)
            dfr_ref[...] = jnp.zeros_like(dfr_ref)

        def tile(diagonal):
            bands = [(r0, r1, r1) for r0, r1 in zip(edges[:-1], edges[1:])] if diagonal else [(0, tq, tq)]
            lse = lse_ref[...]
            for r0, r1, c1 in bands:
                nr = r1 - r0
                rows = pl.ds(pl.multiple_of(qi * tq + r0, 8), nr)
                if diagonal:
                    mask = (lax.broadcasted_iota(jnp.int32, (nr, c1), 1)
                            <= r0 + lax.broadcasted_iota(jnp.int32, (nr, c1), 0))
                lane = lax.broadcasted_iota(jnp.int32, (nr, W), 1)
                head = lax.broadcasted_iota(jnp.int32, (n_heads, c1), 0)
                dfq = jnp.zeros((nr, W), F32)
                dfr = jnp.zeros((n_heads, c1), F32)
                for h in range(n_heads):
                    ps = _pair(h)
                    k, v = k_ref[0:c1, ps], v_ref[0:c1, ps]
                    q = _own_lanes(q_ref[r0:r1, ps] * scale, h)
                    dov = _own_lanes(do_ref[r0:r1, ps], h)
                    s = _dot_nt(q, k) - fr_ref[h:h + 1, 0:c1]
                    if diagonal:
                        s = jnp.where(mask, s, NEG)
                    p = jnp.exp(s - lse[r0:r1, h:h + 1])
                    dp = _dot_nt(dov, v)
                    dsum = jnp.sum(dov.astype(F32) * o_ref[r0:r1, ps], axis=1, keepdims=True)
                    ds = p * (dp - dsum)
                    dsb = ds.astype(BF16)
                    dv = _dot_tn(p.astype(BF16), dov)
                    dk = _dot_tn(dsb, q)
                    dq = _dot(dsb, _own_lanes(k, h))
                    if h % 2 == 0:
                        dv_even, dk_even, dq_even = dv, dk, dq
                    else:
                        dv_s[0:c1, ps] += dv_even + dv
                        dk_s[0:c1, ps] += dk_even + dk
                        dq_ref[rows, ps] += (dq_even + dq) * scale
                    dfr = jnp.where(head == h, jnp.sum(ds, axis=0, keepdims=True), dfr)
                    dfq = jnp.where(lane == h, jnp.sum(ds, axis=1, keepdims=True), dfq)
                dfr_ref[:, 0:c1] -= dfr
                dfq_ref[rows, :] += dfq

        @pl.when(qi > kj)
        def _():
            tile(False)

        @pl.when(qi == kj)
        def _():
            tile(True)

        @pl.when(qi == nq - 1)
        def _():
            dk_ref[...] = dk_s[...].astype(BF16)
            dv_ref[...] = dv_s[...].astype(BF16)

    qq = lambda b, kj, qi: jnp.maximum(qi, kj)
    qblk = lambda w, cb: pl.BlockSpec((None, tq, w), lambda b, kj, qi: (b, qq(b, kj, qi), cb))
    kblk = lambda cb: pl.BlockSpec((None, tq, AD), lambda b, kj, qi: (b, kj, cb))
    return _launch(
        body, name=name, grid=(B, nq, nq), args=(proj, proj, proj, o, do, lse, fr), comm=comm,
        in_specs=[qblk(AD, 3), kblk(4), kblk(5), qblk(AD, 0), qblk(AD, 0), qblk(W, 0),
                  pl.BlockSpec((None, None, n_heads, tq), lambda b, kj, qi: (b, kj, 0, 0))],
        out_specs=[pl.BlockSpec((None, L, AD), lambda b, kj, qi: (b, 0, 0)),
                   kblk(0), kblk(0),
                   pl.BlockSpec((None, None, n_heads, tq), lambda b, kj, qi: (b, kj, 0, 0)),
                   pl.BlockSpec((None, L, W), lambda b, kj, qi: (b, 0, 0))],
        out_shape=[jax.ShapeDtypeStruct((B, L, AD), F32), jax.ShapeDtypeStruct((B, L, AD), BF16),
                   jax.ShapeDtypeStruct((B, L, AD), BF16), jax.ShapeDtypeStruct((B, nq, n_heads, tq), F32),
                   jax.ShapeDtypeStruct((B, L, W), F32)],
        scratch_shapes=[pltpu.VMEM((tq, AD), F32), pltpu.VMEM((tq, AD), F32)])


def _mix_gather(refs, first):
    b_ref, c_ref, hc_ref, cp_ref, hcp_ref, o_ref, cw_ref, p_ref = refs
    bg = b_ref[...].astype(F32)
    u = c_ref[...].astype(F32) * hc_ref[...].astype(F32)
    prev = cp_ref[...].astype(F32) * hcp_ref[...].astype(F32)
    prev = jnp.where(first, 0.0, prev)
    cv, u1, u2 = _causal_conv(u, prev, cw_ref[...])
    yc = bg * cv
    p = p_ref[...]
    rc = lax.rsqrt(_group_mean(yc * yc, p) + EPS)
    ya = o_ref[...].astype(F32)
    ra = lax.rsqrt(_group_mean(ya * ya, p) + EPS)
    return bg, (u, u1, u2), cv, yc * rc, rc, ya * ra, ra


def _mix_specs(tm, CD, D, grid_rank_fn):
    per = tm // HALO
    cur = lambda cb: pl.BlockSpec((None, tm, CD), lambda b, i: (b, i, cb))
    prev = lambda cb: pl.BlockSpec((None, HALO, CD), lambda b, i: (b, jnp.maximum(i * per - 1, 0), cb))
    return [cur(0), cur(1), cur(2), prev(1), prev(2), cur(0)]


def _mix_out(proj, o, cw, gc, ga, wout, h, pmat, next_gain, *, tm, name, comm=None):
    B, L, D = h.shape
    CD = o.shape[-1]
    const = lambda b, i: (0, 0)

    def body(b_ref, c_ref, hc_ref, cp_ref, hcp_ref, o_ref, cw_ref, p_ref, gc_ref, ga_ref, w_ref, h_ref, ng_ref,
             out_ref, y_ref, n_ref):
        first = pl.program_id(1) == 0
        _, _, _, zc, _, za, _ = _mix_gather((b_ref, c_ref, hc_ref, cp_ref, hcp_ref, o_ref, cw_ref, p_ref), first)
        yc = (zc * gc_ref[...]).astype(BF16)
        ya = (za * ga_ref[...]).astype(BF16)
        y_ref[:, :CD] = yc
        y_ref[:, CD:] = ya
        out = h_ref[...] + _dot(yc, w_ref[:CD, :]) + _dot(ya, w_ref[CD:, :])
        out_ref[...] = out
        n_ref[...] = _rms(out, ng_ref[...])

    tile = pl.BlockSpec((None, tm, D), lambda b, i: (b, i, 0))
    return _launch(
        body, name=name, grid=(B, L // tm),
        in_specs=_mix_specs(tm, CD, D, None)
                 + [pl.BlockSpec(cw.shape, const), pl.BlockSpec(pmat.shape, const),
                    pl.BlockSpec((1, CD), const), pl.BlockSpec((1, CD), const), pl.BlockSpec((D, D), const),
                    tile, pl.BlockSpec((1, D), const)],
        out_specs=[tile, tile, tile],
        out_shape=[jax.ShapeDtypeStruct((B, L, D), F32), jax.ShapeDtypeStruct((B, L, D), BF16),
                   jax.ShapeDtypeStruct((B, L, D), BF16)],
        args=(proj, proj, proj, proj, proj, o, cw, pmat, gc, ga, wout, h, next_gain), comm=comm)


def _mix_out_bwd(dhb, proj, o, cw, gc, ga, wout, pmat, *, tm, name, comm=None):
    B, L, D = dhb.shape
    CD = o.shape[-1]
    const = lambda b, i: (0, 0)

    def body(dh_ref, b_ref, c_ref, hc_ref, cp_ref, hcp_ref, o_ref, cw_ref, p_ref, gc_ref, ga_ref, w_ref,
             db_ref, dcv_ref, do_ref, dgc_ref, dga_ref, dcw_ref):
        first = pl.program_id(1) == 0

        @pl.when((pl.program_id(0) == 0) & first)
        def _():
            dgc_ref[...] = jnp.zeros_like(dgc_ref)
            dga_ref[...] = jnp.zeros_like(dga_ref)
            dcw_ref[...] = jnp.zeros_like(dcw_ref)

        bg, us, cv, zc, rc, za, ra = _mix_gather(
            (b_ref, c_ref, hc_ref, cp_ref, hcp_ref, o_ref, cw_ref, p_ref), first)
        p = p_ref[...]
        dh = dh_ref[...]
        dyc = _dot_nt(dh, w_ref[:CD, :])
        dya = _dot_nt(dh, w_ref[CD:, :])

        dgc_ref[...] += jnp.sum(dyc * zc, axis=0, keepdims=True)
        dz = dyc * gc_ref[...]
        dx = rc * (dz - zc * _group_mean(dz * zc, p))
        db_ref[...] = (dx * cv).astype(BF16)
        dcv = dx * bg
        dcv_ref[...] = dcv.astype(BF16)
        for k in range(3):
            dcw_ref[k:k + 1, :] += jnp.sum(dcv * us[2 - k], axis=0, keepdims=True)

        dga_ref[...] += jnp.sum(dya * za, axis=0, keepdims=True)
        dz = dya * ga_ref[...]
        do_ref[...] = (ra * (dz - za * _group_mean(dz * za, p))).astype(BF16)

    tile = lambda w: pl.BlockSpec((None, tm, w), lambda b, i: (b, i, 0))
    return _launch(
        body, name=name, grid=(B, L // tm), comm=comm,
        args=(dhb, proj, proj, proj, proj, proj, o, cw, pmat, gc, ga, wout),
        in_specs=[tile(D)] + _mix_specs(tm, CD, D, None)
                 + [pl.BlockSpec(cw.shape, const), pl.BlockSpec(pmat.shape, const),
                    pl.BlockSpec((1, CD), const), pl.BlockSpec((1, CD), const), pl.BlockSpec((D, D), const)],
        out_specs=[tile(CD), tile(CD), tile(CD),
                   pl.BlockSpec((1, CD), const), pl.BlockSpec((1, CD), const), pl.BlockSpec((8, CD), const)],
        out_shape=[jax.ShapeDtypeStruct((B, L, CD), BF16)] * 3
                  + [jax.ShapeDtypeStruct((1, CD), F32)] * 2 + [jax.ShapeDtypeStruct((8, CD), F32)])


def _conv_bwd(dcv, proj, cw, *, tm, name):
    B, L, CD = dcv.shape
    per = tm // HALO
    nhalo = L // HALO
    nt = L // tm

    def body(d_ref, dn_ref, c_ref, hc_ref, cw_ref, out_ref):
        last = pl.program_id(1) == nt - 1
        d = d_ref[...].astype(F32)
        nxt = jnp.where(last, 0.0, dn_ref[...].astype(F32))
        n0, n1 = _row_of(nxt, 0), _row_of(nxt, 1)
        rows = lax.broadcasted_iota(jnp.int32, d.shape, 0)
        d1 = jnp.where(rows == tm - 1, n0, pltpu.roll(d, tm - 1, 0))
        d2 = jnp.where(rows == tm - 2, n0, jnp.where(rows == tm - 1, n1, pltpu.roll(d, tm - 2, 0)))
        w = cw_ref[...]
        du = w[2:3, :] * d + w[1:2, :] * d1 + w[0:1, :] * d2
        out_ref[:, :CD] = (du * hc_ref[...].astype(F32)).astype(BF16)
        out_ref[:, CD:] = (du * c_ref[...].astype(F32)).astype(BF16)

    return pl.pallas_call(
        body, name=name, grid=(B, nt),
        in_specs=[pl.BlockSpec((None, tm, CD), lambda b, i: (b, i, 0)),
                  pl.BlockSpec((None, HALO, CD), lambda b, i: (b, jnp.minimum((i + 1) * per, nhalo - 1), 0)),
                  pl.BlockSpec((None, tm, CD), lambda b, i: (b, i, 1)),
                  pl.BlockSpec((None, tm, CD), lambda b, i: (b, i, 2)),
                  pl.BlockSpec(cw.shape, lambda b, i: (0, 0))],
        out_specs=pl.BlockSpec((None, tm, 2 * CD), lambda b, i: (b, i, 0)),
        out_shape=jax.ShapeDtypeStruct((B, L, 2 * CD), BF16),
        compiler_params=_params("arbitrary", "arbitrary"),
    )(dcv, dcv, proj, proj, cw)


def _place():
    x, y, c = lax.axis_index("x"), lax.axis_index("y"), lax.axis_index("c")
    others = [(1 - x, y), (x, 1 - y), (1 - x, 1 - y)]
    return x, y, c, others


def _all_gather_shards(shards, *, name):
    n = len(shards)

    def body(*refs):
        ins, outs = refs[:n], refs[n:2 * n]
        send, recv, fsend, frecv, lsem = refs[2 * n:]
        x, y, c, others = _place()
        me = 2 * x + y
        local = [pltpu.make_async_copy(ins[t], outs[t].at[me], lsem.at[t]) for t in range(n)]
        for cp in local:
            cp.start()

        def half(t, k):
            hr = shards[t].shape[0] // 2
            return pl.ds(pl.multiple_of(k * hr, HALO), hr)

        def ici(t, j, src_chip, to):
            src = ins[t].at[half(t, c)] if to is not None else outs[t].at[src_chip, half(t, c)]
            return pltpu.make_async_remote_copy(
                src_ref=src, dst_ref=outs[t].at[src_chip, half(t, c)],
                send_sem=send.at[3 * t + j], recv_sem=recv.at[3 * t + j],
                device_id=(x, y, c) if to is None else to, device_id_type=MESH)

        def d2d(t, j, src_chip, k):
            return pltpu.make_async_remote_copy(
                src_ref=outs[t].at[src_chip, half(t, k)], dst_ref=outs[t].at[src_chip, half(t, k)],
                send_sem=fsend.at[3 * t + j], recv_sem=frecv.at[3 * t + j],
                device_id=(x, y, 1 - c), device_id_type=MESH)

        firsts = [ici(t, j, me, (ox, oy, c)) for t in range(n) for j, (ox, oy) in enumerate(others)]
        for cp in firsts:
            cp.start()
        passed = []
        for t in range(n):
            for j, (ox, oy) in enumerate(others):
                ici(t, j, 2 * ox + oy, None).wait_recv()
                cp = d2d(t, j, 2 * ox + oy, c)
                cp.start()
                passed.append(cp)
        for t in range(n):
            for j, (ox, oy) in enumerate(others):
                d2d(t, j, 2 * ox + oy, 1 - c).wait_recv()
        for cp in firsts + passed:
            cp.wait_send()
        for cp in local:
            cp.wait()

    return pl.pallas_call(
        body, name=name,
        in_specs=[ANY] * n, out_specs=[ANY] * n,
        out_shape=[jax.ShapeDtypeStruct((N_SHARD,) + s.shape, s.dtype) for s in shards],
        scratch_shapes=[pltpu.SemaphoreType.DMA((3 * n,))] * 4 + [pltpu.SemaphoreType.DMA((n,))],
    )(*shards)


def _all_reduce_small(slab, *, name):
    def body(in_ref, out_ref, gath, send, recv):
        x, y, c, _ = _place()
        me = 4 * x + 2 * y + c
        gath[me] = in_ref[...]
        copies, peers = [], []
        for m in range(1, N_DEV):
            px = jnp.where((m >> 2) & 1, 1 - x, x)
            py = jnp.where((m >> 1) & 1, 1 - y, y)
            pc = jnp.where(m & 1, 1 - c, c)
            cp = pltpu.make_async_remote_copy(
                src_ref=in_ref, dst_ref=gath.at[me], send_sem=send.at[m - 1], recv_sem=recv.at[m - 1],
                device_id=(px, py, pc), device_id_type=MESH)
            cp.start()
            copies.append(cp)
            peers.append(4 * px + 2 * py + pc)
        for m in range(1, N_DEV):
            pltpu.make_async_remote_copy(
                src_ref=in_ref, dst_ref=gath.at[peers[m - 1]], send_sem=send.at[m - 1], recv_sem=recv.at[m - 1],
                device_id=(x, y, c), device_id_type=MESH).wait_recv()
        for cp in copies:
            cp.wait_send()
        acc = gath[0]
        for k in range(1, N_DEV):
            acc = acc + gath[k]
        out_ref[...] = acc

    vm = pl.BlockSpec(memory_space=pltpu.VMEM)
    return pl.pallas_call(
        body, name=name, in_specs=[vm], out_specs=vm,
        out_shape=jax.ShapeDtypeStruct(slab.shape, slab.dtype),
        scratch_shapes=[pltpu.VMEM((N_DEV,) + slab.shape, slab.dtype),
                        pltpu.SemaphoreType.DMA((N_DEV - 1,)), pltpu.SemaphoreType.DMA((N_DEV - 1,))],
    )(slab)


def _gather_ici(shards):
    n = len(shards)

    def copies(ins, outs, sems, sending):
        send, recv, _ = sems
        x, y, c, others = _place()
        me = 2 * x + y
        out = []
        for t in range(n):
            hr = shards[t].shape[0] // 2
            rows = pl.ds(pl.multiple_of(c * hr, HALO), hr)
            for j, (ox, oy) in enumerate(others):
                src_chip = me if sending else 2 * ox + oy
                out.append(pltpu.make_async_remote_copy(
                    src_ref=ins[t].at[rows], dst_ref=outs[t].at[src_chip, rows],
                    send_sem=send.at[3 * t + j], recv_sem=recv.at[3 * t + j],
                    device_id=(ox, oy, c) if sending else (x, y, c), device_id_type=MESH))
        return out

    def local(ins, outs, sems):
        x, y, _, _ = _place()
        return [pltpu.make_async_copy(ins[t], outs[t].at[2 * x + y], sems[2].at[t]) for t in range(n)]

    def start(ins, outs, sems):
        for cp in local(ins, outs, sems) + copies(ins, outs, sems, True):
            cp.start()

    def finish(ins, outs, sems):
        for cp in copies(ins, outs, sems, False):
            cp.wait_recv()
        for cp in copies(ins, outs, sems, True):
            cp.wait_send()
        for cp in local(ins, outs, sems):
            cp.wait()

    return _Comm(shards, [jax.ShapeDtypeStruct((N_SHARD,) + s.shape, s.dtype) for s in shards],
                 [3 * n, 3 * n, n], start, finish)


def _gather_d2d(parts):
    n = len(parts)

    def copies(outs, sems, sending):
        send, recv = sems
        x, y, c, others = _place()
        out = []
        for t in range(n):
            hr = parts[t].shape[1] // 2
            rows = pl.ds(pl.multiple_of((c if sending else 1 - c) * hr, HALO), hr)
            for j, (ox, oy) in enumerate(others):
                blk = outs[t].at[2 * ox + oy, rows]
                out.append(pltpu.make_async_remote_copy(
                    src_ref=blk, dst_ref=blk, send_sem=send.at[3 * t + j], recv_sem=recv.at[3 * t + j],
                    device_id=(x, y, 1 - c) if sending else (x, y, c), device_id_type=MESH))
        return out

    def start(ins, outs, sems):
        for cp in copies(outs, sems, True):
            cp.start()

    def finish(ins, outs, sems):
        for cp in copies(outs, sems, False):
            cp.wait_recv()
        for cp in copies(outs, sems, True):
            cp.wait_send()

    return _Comm(parts, [jax.ShapeDtypeStruct(p.shape, p.dtype) for p in parts], [3 * n, 3 * n], start, finish,
                 aliases={t: t for t in range(n)})


def _swap_halves(grads):
    n = len(grads)

    def copies(ins, outs, sems):
        x, y, c, _ = _place()
        out = []
        for t in range(n):
            hr = grads[t].shape[1] // 2
            rows = pl.ds(pl.multiple_of((1 - c) * hr, 8), hr)
            out.append(pltpu.make_async_remote_copy(
                src_ref=ins[t].at[:, rows, :], dst_ref=outs[t], send_sem=sems[0].at[t], recv_sem=sems[1].at[t],
                device_id=(x, y, 1 - c), device_id_type=MESH))
        return out

    def start(ins, outs, sems):
        for cp in copies(ins, outs, sems):
            cp.start()

    def finish(ins, outs, sems):
        for cp in copies(ins, outs, sems):
            cp.wait()

    return _Comm(grads, [jax.ShapeDtypeStruct((N_SHARD, g.shape[1] // 2, g.shape[2]), g.dtype) for g in grads],
                 [n, n], start, finish)


def _pair_sum(g, got, c, *, name):
    ns, R, C = g.shape
    hr = R // 2

    def body(c_ref, g_ref, r_ref, o_ref):
        o_ref[...] = (g_ref[...] + r_ref[...]).astype(BF16)

    return pl.pallas_call(
        body, name=name,
        grid_spec=pltpu.PrefetchScalarGridSpec(
            num_scalar_prefetch=1, grid=(ns,),
            in_specs=[pl.BlockSpec((None, hr, C), lambda s, cr: (s, cr[0], 0)),
                      pl.BlockSpec((None, hr, C), lambda s, cr: (s, 0, 0))],
            out_specs=pl.BlockSpec((None, hr, C), lambda s, cr: (s, 0, 0))),
        out_shape=jax.ShapeDtypeStruct((ns, hr, C), BF16),
        compiler_params=_params("arbitrary"),
    )(c, g, got)


def _scatter_chips(sums):
    n = len(sums)

    def copies(ins, outs, sems, sending):
        x, y, c, others = _place()
        me = 2 * x + y
        out = []
        for t in range(n):
            for j, (ox, oy) in enumerate(others):
                there = 2 * ox + oy
                out.append(pltpu.make_async_remote_copy(
                    src_ref=ins[t].at[there if sending else me], dst_ref=outs[t].at[me if sending else there],
                    send_sem=sems[0].at[3 * t + j], recv_sem=sems[1].at[3 * t + j],
                    device_id=(ox, oy, c) if sending else (x, y, c), device_id_type=MESH))
        return out

    def start(ins, outs, sems):
        for cp in copies(ins, outs, sems, True):
            cp.start()

    def finish(ins, outs, sems):
        for cp in copies(ins, outs, sems, False):
            cp.wait_recv()
        for cp in copies(ins, outs, sems, True):
            cp.wait_send()

    return _Comm(sums, [jax.ShapeDtypeStruct(s.shape, s.dtype) for s in sums], [3 * n, 3 * n], start, finish)


def _chip_sum(g, got, landed, idx, *, name):
    ns, R, C = g.shape
    hr = R // 2

    def body(i_ref, g_ref, r_ref, a_ref, b_ref, c_ref, o_ref):
        acc = g_ref[...] + r_ref[...]
        for ref in (a_ref, b_ref, c_ref):
            acc = acc + ref[...].astype(F32)
        o_ref[...] = acc

    other = lambda k: pl.BlockSpec((None, hr, C), lambda s, ir: (ir[2 + k], 0, 0))
    return pl.pallas_call(
        body, name=name,
        grid_spec=pltpu.PrefetchScalarGridSpec(
            num_scalar_prefetch=1, grid=(1,),
            in_specs=[pl.BlockSpec((None, hr, C), lambda s, ir: (ir[0], ir[1], 0)),
                      pl.BlockSpec((None, hr, C), lambda s, ir: (ir[0], 0, 0)),
                      other(0), other(1), other(2)],
            out_specs=pl.BlockSpec((hr, C), lambda s, ir: (ir[1], 0))),
        out_shape=jax.ShapeDtypeStruct((R, C), F32),
        compiler_params=_params("arbitrary"),
    )(idx, g, got, landed, landed, landed)


def _share_halves(halves):
    n = len(halves)

    def copies(outs, sems, sending):
        x, y, c, _ = _place()
        out = []
        for t in range(n):
            hr = halves[t].shape[0] // 2
            rows = pl.ds(pl.multiple_of((c if sending else 1 - c) * hr, 8), hr)
            out.append(pltpu.make_async_remote_copy(
                src_ref=outs[t].at[rows, :], dst_ref=outs[t].at[rows, :], send_sem=sems[0].at[t],
                recv_sem=sems[1].at[t], device_id=(x, y, 1 - c) if sending else (x, y, c), device_id_type=MESH))
        return out

    def start(ins, outs, sems):
        for cp in copies(outs, sems, True):
            cp.start()

    def finish(ins, outs, sems):
        for cp in copies(outs, sems, False):
            cp.wait_recv()
        for cp in copies(outs, sems, True):
            cp.wait_send()

    return _Comm(halves, [jax.ShapeDtypeStruct(h.shape, h.dtype) for h in halves], [n, n], start, finish,
                 aliases={t: t for t in range(n)})


def _adamw(w, g, m, v, *, name):
    R, C = w.shape
    tr = R
    for cand in (256, 128, 64, 32, 16, 8):
        if R % cand == 0:
            tr = cand
            break

    def body(w_ref, g_ref, m_ref, v_ref, go_ref, d_ref, mo_ref, vo_ref):
        gv = g_ref[...]
        go_ref[...] = gv
        mn = ADAM_B1 * m_ref[...] + (1.0 - ADAM_B1) * gv
        vn = ADAM_B2 * v_ref[...] + (1.0 - ADAM_B2) * (gv * gv)
        m_hat = mn / (1.0 - ADAM_B1 ** ADAM_STEP)
        v_hat = vn / (1.0 - ADAM_B2 ** ADAM_STEP)
        d_ref[...] = -ADAM_LR * (m_hat / (jnp.sqrt(v_hat) + ADAM_EPS) + ADAM_WD * w_ref[...])
        mo_ref[...] = mn
        vo_ref[...] = vn

    blk = pl.BlockSpec((tr, C), lambda i: (i, 0))
    return pl.pallas_call(
        body, name=name, grid=(R // tr,), in_specs=[blk] * 4, out_specs=[blk] * 4,
        out_shape=[jax.ShapeDtypeStruct((R, C), F32)] * 4,
        compiler_params=_params("arbitrary"),
    )(w, g, m, v)


def _pack_small(D, meta, n1, nm, n3, nf, gc, ga, bf, cw):
    def row(a):
        a = a.reshape(-1, a.shape[-1])
        return jnp.pad(a, ((0, 0), (0, D - a.shape[-1])))
    rows = [row(meta), row(n1), row(nm), row(n3), row(nf), row(jnp.concatenate([gc, ga], axis=-1)), row(bf), row(cw)]
    slab = jnp.concatenate(rows, axis=0)
    return jnp.pad(slab, ((0, SMALL_ROWS - slab.shape[0]), (0, 0)))


def _unpack_small(slab, like):
    meta, n1, nm, n3, nf, gc, ga, bf, cw = like
    nmeta, mc = meta.shape
    out = [slab[:nmeta, :mc].reshape(meta.shape)]
    r = nmeta
    for a in (n1, nm, n3, nf):
        out.append(slab[r, :a.shape[-1]].reshape(a.shape))
        r += 1
    cd = gc.shape[-1]
    out.append(slab[r, :cd].reshape(gc.shape))
    out.append(slab[r, cd:cd + ga.shape[-1]].reshape(ga.shape))
    r += 1
    out.append(slab[r, :bf.shape[-1]].reshape(bf.shape))
    r += 1
    out.append(slab[r:r + 3, :cw.shape[-1]].reshape(cw.shape))
    return out


def kernel(x, meta_tokens, ffn1_norm, ffn1_w_gu, ffn1_w_down, mix_norm, w_in, conv_w, b_f, out_norm_conv, out_norm_attn, w_out, ffn2_norm, ffn2_w_gu, ffn2_w_down, final_norm, loss_target, m_meta_tokens, m_ffn1_norm, m_ffn1_w_gu, m_ffn1_w_down, m_mix_norm, m_w_in, m_conv_w, m_b_f, m_out_norm_conv, m_out_norm_attn, m_w_out, m_ffn2_norm, m_ffn2_w_gu, m_ffn2_w_down, m_final_norm, v_meta_tokens, v_ffn1_norm, v_ffn1_w_gu, v_ffn1_w_down, v_mix_norm, v_w_in, v_conv_w, v_b_f, v_out_norm_conv, v_out_norm_attn, v_w_out, v_ffn2_norm, v_ffn2_w_gu, v_ffn2_w_down, v_final_norm):
    B, S, D = x.shape
    L = S + N_META
    T = B * L
    tm = L // 3
    assert tm * 3 == L and tm % HALO == 0
    guc = ffn1_w_gu.shape[-1]
    ff = N_SHARD * guc // 2
    H = b_f.shape[-1]
    AD = H * HEAD_DIM
    CD = conv_w.shape[-1] * N_SHARD
    assert CD == AD and CD + AD == D and CD % LANES == 0
    n_main = 3 * CD + 3 * AD
    ins = w_in.shape[-1]

    xi, yi, ci = lax.axis_index("x"), lax.axis_index("y"), lax.axis_index("c")
    chip = 2 * xi + yi

    small_shard = jnp.zeros((2 * HALO, meta_tokens.shape[-1]), F32)
    small_shard = small_shard.at[:N_META].set(meta_tokens)
    small_shard = small_shard.at[N_META:N_META + 3, :conv_w.shape[-1]].set(conv_w[0])
    big = [ffn1_w_gu[0], ffn1_w_down[0], w_in[0], w_out[0], ffn2_w_gu[0], ffn2_w_down[0]]
    wgu1_s, wd1_s, win_s, wout_s, wgu2_s, wd2_s = [w.astype(BF16) for w in big]
    small_g, = _all_gather_shards([small_shard], name="gather_small")
    meta_f = jnp.moveaxis(small_g[:, :N_META], 0, 1).reshape(N_META, D)
    cw_f = jnp.moveaxis(small_g[:, N_META:N_META + 3, :conv_w.shape[-1]], 0, 1).reshape(3, CD)
    cw8 = jnp.pad(cw_f, ((0, 5), (0, 0)))
    bf_p = jnp.pad(b_f, ((0, 0), (0, LANES - H)))
    gid = jnp.arange(CD) // HEAD_DIM
    pmat = jnp.where(gid[:, None] == gid[None, :], 1.0 / HEAD_DIM, 0.0).astype(BF16)

    gu_shape = jax.ShapeDtypeStruct((2, T, ff), BF16)
    gu_w_spec = pl.BlockSpec((None, D, guc), lambda s, i: (s, 0, 0))
    gu_o_spec = pl.BlockSpec((None, tm, guc), lambda s, i: (s // 2, i, s % 2))

    sid = ((chip + jnp.arange(N_SHARD, dtype=jnp.int32)) % N_SHARD).astype(jnp.int32)
    (h0, n1), wgu1_h = _embed_norm(x, meta_f, ffn1_norm, tm=tm, name="embed_norm", comm=_gather_ici([wgu1_s]))
    gu1, out = _ffn_up(n1, wgu1_s[None], sid, None, tm=tm, first=0, count=1, name="ffn1_up_own",
                       comm=_join(_gather_d2d(wgu1_h), _gather_ici([wd1_s])))
    wgu1, wd1_h = out[0], out[1:]
    gu1, out = _ffn_up(n1, wgu1, sid, gu1, tm=tm, first=1, count=N_SHARD - 1, name="ffn1_up_rest",
                       comm=_join(_gather_d2d(wd1_h), _gather_ici([win_s, wout_s])))
    wd1, mix_w = out[0].reshape(ff, D), out[1:]
    (h1, n2), (win_g, wout_g) = _ffn_down(gu1, wd1, h0, mix_norm, tm=tm, name="ffn1_down", comm=_gather_d2d(mix_w))
    wout_f = wout_g.reshape(D, D)
    win_f = jnp.moveaxis(win_g, 0, 1).reshape(D, N_SHARD * ins)
    win_main = win_f[:, :n_main]
    win_fg = jnp.pad(win_f[:, n_main:], ((0, 0), (0, LANES - H)))

    proj, _ = _matmul_nn(n2, win_main, tm=tm, nb=n_main // (3 * CD),
                         w_spec=pl.BlockSpec((D, 3 * CD), lambda s, i: (0, s)),
                         out_shape=jax.ShapeDtypeStruct((T, n_main), BF16),
                         out_spec=pl.BlockSpec((tm, 3 * CD), lambda s, i: (i, s)), name="mix_in")
    fg, _ = _matmul_nn(n2, win_fg, tm=tm, nb=1, w_spec=pl.BlockSpec((D, LANES), lambda s, i: (0, 0)),
                       out_shape=jax.ShapeDtypeStruct((T, LANES), F32),
                       out_spec=pl.BlockSpec((tm, LANES), lambda s, i: (i, 0)), name="mix_in_fg")
    proj3 = proj.reshape(B, L, n_main)
    fg3 = fg.reshape(B, L, LANES)
    fc = _fcum(fg3, bf_p, ch=tm, name="forget_cumsum")
    fr = fc[:, :, :H].reshape(B, L // tm, tm, H).transpose(0, 1, 3, 2)
    (o, lse), ffn2_w = _attn_fwd(proj3, fc, fr, tq=tm, n_heads=H, name="attn_fwd",
                                 comm=_gather_ici([wgu2_s, wd2_s]))
    (h2, ymix, n3), (wgu2, wd2) = _mix_out(
        proj3, o, cw8, out_norm_conv, out_norm_attn, wout_f, h1.reshape(B, L, D), pmat, ffn2_norm,
        tm=tm, name="mix_out", comm=_gather_d2d(ffn2_w))
    wd2 = wd2.reshape(ff, D)
    h2 = h2.reshape(T, D)
    n3 = n3.reshape(T, D)

    gu2, _ = _matmul_nn(n3, wgu2, tm=tm, nb=N_SHARD, w_spec=gu_w_spec, out_shape=gu_shape, out_spec=gu_o_spec,
                        name="ffn2_up")
    (dh3f, dh3b, d_gf, loss_part), _ = _ffn_down_loss(gu2, wd2, h2, final_norm.reshape(1, D), loss_target,
                                                      tm=tm, name="ffn2_down_loss")

    c_arr = jnp.reshape(ci, (1,)).astype(jnp.int32)
    ks = jnp.arange(N_SHARD - 1, dtype=jnp.int32)
    idx = jnp.concatenate([jnp.stack([chip, ci]).astype(jnp.int32), ks + (ks >= chip).astype(jnp.int32)])

    def pair_sums(grads, got, names):
        return [_pair_sum(g, r, c_arr, name="pair_sum_" + nm) for g, r, nm in zip(grads, got, names)]

    def chip_sums(grads, got, landed, names):
        return [_chip_sum(g, r, l, idx, name="chip_sum_" + nm) for g, r, l, nm in zip(grads, got, landed, names)]

    def dw_up(n, dgu, name, comm=None):
        return _matmul_tn(
            n, dgu, tm=tm, nb=N_SHARD, kb=D, x_spec=pl.BlockSpec((tm, D), lambda s, i: (i, 0)),
            y_spec=pl.BlockSpec((None, tm, guc), lambda s, i: (s // 2, i, s % 2)),
            out_shape=jax.ShapeDtypeStruct((N_SHARD, D, guc), F32),
            out_spec=pl.BlockSpec((None, D, guc), lambda s, i: (s, 0, 0)), name=name, comm=comm)

    (dgu2, d_wd2), _ = _ffn_bwd_act(dh3b, gu2, wd2, tm=tm, guc=guc, name="ffn2_bwd_act")
    (dh2, dh2b, d_g3), _ = _ffn_bwd_in(dgu2, wgu2, h2, ffn2_norm, dh3f, tm=tm, scale=1.0, name="ffn2_bwd_in")
    d_wgu2, _ = dw_up(n3, dgu2, "ffn2_dw_up")
    grads_f2 = [d_wgu2, d_wd2.reshape(N_SHARD, ff // N_SHARD, D)]
    names_f2 = ["wgu2", "wd2"]

    dh2b3 = dh2b.reshape(B, L, D)
    (d_bg, d_cv, d_o, d_gc, d_ga, d_cw), got_f2 = _mix_out_bwd(
        dh2b3, proj3, o, cw8, out_norm_conv, out_norm_attn, wout_f, pmat, tm=tm, name="mix_out_bwd",
        comm=_swap_halves(grads_f2))
    sums_f2 = pair_sums(grads_f2, got_f2, names_f2)
    d_wout, _ = _matmul_tn(
        ymix.reshape(T, D), dh2b, tm=tm, nb=1, kb=D,
        x_spec=pl.BlockSpec((tm, D), lambda s, i: (i, 0)), y_spec=pl.BlockSpec((tm, D), lambda s, i: (i, 0)),
        out_shape=jax.ShapeDtypeStruct((D, D), F32), out_spec=pl.BlockSpec((D, D), lambda s, i: (0, 0)),
        name="dw_out")
    d_cc = _conv_bwd(d_cv, proj3, cw8, tm=tm, name="conv_bwd")
    (d_q, d_k, d_v, d_fr, d_fq), landed_f2 = _attn_bwd(proj3, o, d_o, lse, fc, fr, tq=tm, n_heads=H, name="attn_bwd",
                                                       comm=_scatter_chips(sums_f2))
    halves_f2 = chip_sums(grads_f2, got_f2, landed_f2, names_f2)
    d_fc = d_fq + jnp.pad(d_fr.transpose(0, 1, 3, 2).reshape(B, L, H), ((0, 0), (0, 0), (0, LANES - H)))
    d_fg, d_bf = _fcum_bwd(d_fc, fg3, bf_p, ch=tm, name="forget_cumsum_bwd")

    parts = [d_bg.reshape(T, CD), d_cc.reshape(T, 2 * CD), d_q.reshape(T, AD), d_k.reshape(T, AD),
             d_v.reshape(T, AD), d_fg.reshape(T, LANES)]
    (dh1, dh1b, d_gm, d_proj), g_f2 = _mix_bwd_in(parts, win_main, win_fg, h1, mix_norm, dh2, tm=tm, scale=0.5,
                                                  name="mix_bwd_in", comm=_share_halves(halves_f2))
    wide = d_proj.shape[1]
    bw = next(c for c in (768, 640, 512, 384, 256, LANES) if wide % c == 0)
    d_win_nat, _ = _matmul_tn(
        n2, d_proj, tm=tm, nb=wide // bw, kb=D,
        x_spec=pl.BlockSpec((tm, D), lambda s, i: (i, 0)), y_spec=pl.BlockSpec((tm, bw), lambda s, i: (i, s)),
        out_shape=jax.ShapeDtypeStruct((D, wide), F32), out_spec=pl.BlockSpec((D, bw), lambda s, i: (0, s)),
        name="dw_in")
    d_win = jnp.moveaxis(d_win_nat[:, :N_SHARD * ins].reshape(D, N_SHARD, ins), 1, 0)
    grads_mx = [d_win, d_wout.reshape(N_SHARD, D // N_SHARD, D)]
    names_mx = ["win", "wout"]

    (dgu1, d_wd1), got_mx = _ffn_bwd_act(dh1b, gu1, wd1, tm=tm, guc=guc, name="ffn1_bwd_act",
                                         comm=_swap_halves(grads_mx))
    sums_mx = pair_sums(grads_mx, got_mx, names_mx)
    grads_d1 = [d_wd1.reshape(N_SHARD, ff // N_SHARD, D)]
    d_wgu1, out = dw_up(n1, dgu1, "ffn1_dw_up", comm=_join(_scatter_chips(sums_mx), _swap_halves(grads_d1)))
    landed_mx, got_d1 = out[:2], out[2:]
    halves_mx = chip_sums(grads_mx, got_mx, landed_mx, names_mx)
    sums_d1 = pair_sums(grads_d1, got_d1, ["wd1"])
    grads_u1 = [d_wgu1]
    (grad_x, d_meta, d_g1), out = _ffn_bwd_in_first(
        dgu1, wgu1, h0, ffn1_norm, dh1, tm=tm, batch=B, name="ffn1_bwd_in",
        comm=_join(_join(_share_halves(halves_mx), _scatter_chips(sums_d1)), _swap_halves(grads_u1)))
    g_mx, landed_d1, got_u1 = out[:2], out[2:3], out[3:]
    halves_d1 = chip_sums(grads_d1, got_d1, landed_d1, ["wd1"])
    sums_u1 = pair_sums(grads_u1, got_u1, ["wgu1"])
    out = _run_comm(_join(_share_halves(halves_d1), _scatter_chips(sums_u1)), name="scatter_ffn1")
    g_d1, landed_u1 = out[:1], out[1:]
    halves_u1 = chip_sums(grads_u1, got_u1, landed_u1, ["wgu1"])
    g_u1 = _run_comm(_share_halves(halves_u1), name="share_ffn1")
    g_big = [g_u1[0], g_d1[0], g_mx[0], g_mx[1], g_f2[0], g_f2[1]]

    loss_row = jnp.zeros((1, D), F32).at[0, 0].set(loss_part[0, 0])
    slab = _pack_small(D, d_meta, d_g1, d_gm, d_g3, d_gf, d_gc, d_ga, d_bf[:, :H], d_cw[:3])
    slab = slab.at[SMALL_ROWS - 1].set(loss_row[0])
    total = _all_reduce_small(slab, name="reduce_small")
    loss = total[SMALL_ROWS - 1, 0]
    mcols = meta_tokens.shape[-1]
    ccols = conv_w.shape[-1]
    full_like = (jnp.zeros((N_META, D)), ffn1_norm, mix_norm, ffn2_norm, final_norm.reshape(1, D), out_norm_conv,
                 out_norm_attn, b_f, jnp.zeros((1, 3, CD)))
    g_small = _unpack_small(total, full_like)
    g_small[0] = lax.dynamic_slice_in_dim(g_small[0], chip * mcols, mcols, axis=1)
    g_small[8] = lax.dynamic_slice_in_dim(g_small[8], chip * ccols, ccols, axis=2)

    def small_slab(meta, a1, am, a3, af, gc, ga, bf, cw):
        return _pack_small(D, meta, a1, am, a3, af.reshape(1, D), gc, ga, bf, cw[0])

    w_small = small_slab(meta_tokens, ffn1_norm, mix_norm, ffn2_norm, final_norm, out_norm_conv, out_norm_attn, b_f, conv_w)
    m_small = small_slab(m_meta_tokens, m_ffn1_norm, m_mix_norm, m_ffn2_norm, m_final_norm, m_out_norm_conv,
                         m_out_norm_attn, m_b_f, m_conv_w)
    v_small = small_slab(v_meta_tokens, v_ffn1_norm, v_mix_norm, v_ffn2_norm, v_final_norm, v_out_norm_conv,
                         v_out_norm_attn, v_b_f, v_conv_w)
    gs = list(g_small)
    gs[4] = gs[4].reshape(final_norm.shape)
    g_slab = small_slab(gs[0], gs[1], gs[2], gs[3], gs[4], gs[5], gs[6], gs[7], gs[8])
    local_like = (meta_tokens, ffn1_norm, mix_norm, ffn2_norm, final_norm.reshape(1, D), out_norm_conv, out_norm_attn,
                  b_f, conv_w)
    small_out = [_unpack_small(s, local_like)
                 for s in _adamw(w_small, g_slab, m_small, v_small, name="adamw_small")[1:]]
    for lst in small_out:
        lst[4] = lst[4].reshape(final_norm.shape)

    names = ["wgu1", "wd1", "win", "wout", "wgu2", "wd2"]
    w_big = big
    m_big = [m_ffn1_w_gu[0], m_ffn1_w_down[0], m_w_in[0], m_w_out[0], m_ffn2_w_gu[0], m_ffn2_w_down[0]]
    v_big = [v_ffn1_w_gu[0], v_ffn1_w_down[0], v_w_in[0], v_w_out[0], v_ffn2_w_gu[0], v_ffn2_w_down[0]]
    big_out = [_adamw(w, g, m, v, name="adamw_" + nm) for w, g, m, v, nm in zip(w_big, g_big, m_big, v_big, names)]

    def assemble(small, bigs):
        meta, a1, am, a3, af, gc, ga, bf, cw = small
        gu1_, d1_, win_, wout_, gu2_, d2_ = [b[None] for b in bigs]
        return [meta, a1, gu1_, d1_, am, win_, cw, bf, gc, ga, wout_, a3, gu2_, d2_, af]

    gs_out = list(g_small)
    gs_out[4] = gs_out[4].reshape(final_norm.shape)
    grads_out = assemble(gs_out, [b[0] for b in big_out])
    delta_out = assemble(small_out[0], [b[1] for b in big_out])
    m_out = assemble(small_out[1], [b[2] for b in big_out])
    v_out = assemble(small_out[2], [b[3] for b in big_out])
    return (loss, grad_x, *grads_out, *delta_out, *m_out, *v_out)
```

```python
import functools

import jax
import jax.numpy as jnp
from jax import lax
from jax.experimental import pallas as pl
from jax.experimental.pallas import tpu as pltpu

F32 = jnp.float32
BF16 = jnp.bfloat16

EPS = 1e-6
N_META = 16
HEAD_DIM = 64
N_SHARD = 4
N_DEV = 8
HALO = 16
LANES = 128
SMALL_ROWS = 32
VMEM_LIMIT_V7X = 56 * 1024 * 1024
NEG = -1e30
ATTN_BANDS = 2

ADAM_LR = 0.001
ADAM_B1 = 0.9
ADAM_B2 = 0.999
ADAM_EPS = 1e-08
ADAM_WD = 0.01
ADAM_STEP = 10

MESH = pl.DeviceIdType.MESH
ANY = pl.BlockSpec(memory_space=pl.ANY)
NT_DIMS = (((1,), (1,)), ((), ()))
TN_DIMS = (((0,), (0,)), ((), ()))


def _params(*sem):
    return pltpu.CompilerParams(dimension_semantics=sem, vmem_limit_bytes=VMEM_LIMIT_V7X)


class _Comm:
    def __init__(self, ins, out_shapes, sems, start, finish, aliases=None):
        self.ins, self.out_shapes, self.sems = list(ins), list(out_shapes), list(sems)
        self.start, self.finish, self.aliases = start, finish, dict(aliases or {})


def _join(a, b):
    ni, no, ns = len(a.ins), len(a.out_shapes), len(a.sems)

    def start(ins, outs, sems):
        a.start(ins[:ni], outs[:no], sems[:ns])
        b.start(ins[ni:], outs[no:], sems[ns:])

    def finish(ins, outs, sems):
        a.finish(ins[:ni], outs[:no], sems[:ns])
        b.finish(ins[ni:], outs[no:], sems[ns:])

    aliases = dict(a.aliases)
    aliases.update({ni + i: no + j for i, j in b.aliases.items()})
    return _Comm(a.ins + b.ins, a.out_shapes + b.out_shapes, a.sems + b.sems, start, finish, aliases)


def _launch(body, *, name, grid, in_specs, out_specs, out_shape, args, scratch_shapes=(), comm=None, prefetch=(),
            aliases=None):
    single = not isinstance(out_shape, (list, tuple))
    out_specs = [out_specs] if single else list(out_specs)
    out_shape = [out_shape] if single else list(out_shape)
    in_specs, scratch_shapes, prefetch = list(in_specs), list(scratch_shapes), list(prefetch)
    params = _params(*(("arbitrary",) * len(grid)))
    n_pf, n_in, n_out, n_scr = len(prefetch), len(in_specs), len(out_specs), len(scratch_shapes)
    c_ins = comm.ins if comm else []
    c_shapes = comm.out_shapes if comm else []
    c_sems = comm.sems if comm else []
    c_in, c_out = len(c_ins), len(c_shapes)

    def carrier(*refs):
        p = 0
        pf = refs[p:p + n_pf]; p += n_pf
        a = refs[p:p + n_in]; p += n_in
        ci = refs[p:p + c_in]; p += c_in
        o = refs[p:p + n_out]; p += n_out
        co = refs[p:p + c_out]; p += c_out
        s = refs[p:p + n_scr]; p += n_scr
        cs = refs[p:]
        if comm:
            first = functools.reduce(lambda u, v: u & v, [pl.program_id(k) == 0 for k in range(len(grid))])

            @pl.when(first)
            def _():
                comm.start(ci, co, cs)

        body(*pf, *a, *o, *s)

        if comm:
            last = functools.reduce(lambda u, v: u & v, [pl.program_id(k) == grid[k] - 1 for k in range(len(grid))])

            @pl.when(last)
            def _():
                comm.finish(ci, co, cs)

    io_aliases = {n_pf + i: j for i, j in (aliases or {}).items()}
    if comm:
        io_aliases.update({n_pf + n_in + i: n_out + j for i, j in comm.aliases.items()})
    all_in, all_out = in_specs + [ANY] * c_in, out_specs + [ANY] * c_out
    all_scratch = scratch_shapes + [pltpu.SemaphoreType.DMA((k,)) for k in c_sems]
    if n_pf:
        spec = dict(grid_spec=pltpu.PrefetchScalarGridSpec(
            num_scalar_prefetch=n_pf, grid=grid, in_specs=all_in, out_specs=all_out, scratch_shapes=all_scratch))
    else:
        spec = dict(grid=grid, in_specs=all_in, out_specs=all_out, scratch_shapes=all_scratch)
    res = pl.pallas_call(carrier, name=name, out_shape=out_shape + c_shapes, input_output_aliases=io_aliases,
                         compiler_params=params, **spec)(*prefetch, *args, *c_ins)
    main = list(res[:n_out])
    return (main[0] if single else main), (list(res[n_out:]) if comm else None)


def _run_comm(comm, *, name):
    c_in, c_out = len(comm.ins), len(comm.out_shapes)

    def body(*refs):
        ci, co, cs = refs[:c_in], refs[c_in:c_in + c_out], refs[c_in + c_out:]
        comm.start(ci, co, cs)
        comm.finish(ci, co, cs)

    return list(pl.pallas_call(
        body, name=name, in_specs=[ANY] * c_in, out_specs=[ANY] * c_out, out_shape=comm.out_shapes,
        scratch_shapes=[pltpu.SemaphoreType.DMA((k,)) for k in comm.sems],
        input_output_aliases=comm.aliases)(*comm.ins))


def _chunks(width, step=512):
    out, c0 = [], 0
    while c0 < width:
        cw = min(step, width - c0)
        out.append((c0, cw))
        c0 += cw
    return out


def _split2(v):
    hi = v.astype(BF16)
    lo = (v - hi.astype(F32)).astype(BF16)
    return hi, lo


def _split3(v):
    hi = v.astype(BF16)
    r = v - hi.astype(F32)
    mid = r.astype(BF16)
    lo = (r - mid.astype(F32)).astype(BF16)
    return hi, mid, lo


def _dot(a, b):
    return jnp.dot(a, b, preferred_element_type=F32)


def _dot_nt(a, b):
    return lax.dot_general(a, b, NT_DIMS, preferred_element_type=F32)


def _dot_tn(a, b):
    return lax.dot_general(a, b, TN_DIMS, preferred_element_type=F32)


def _silu_mul(g, u):
    return g * jax.nn.sigmoid(g) * u


def _rms_bwd(dn, h, gain, dres):
    r = lax.rsqrt(jnp.mean(h * h, axis=-1, keepdims=True) + EPS)
    y = h * r
    dgain = jnp.sum(dn * y, axis=0, keepdims=True)
    dy = dn * gain
    dh = dres + r * (dy - y * jnp.mean(dy * y, axis=-1, keepdims=True))
    return dh, dgain


def _group_mean(v, p):
    hi, lo = _split2(v)
    return _dot(hi, p) + _dot(lo, p)


def _row_of(a, k):
    rows = lax.broadcasted_iota(jnp.int32, a.shape, 0)
    return jnp.sum(jnp.where(rows == k, a, 0.0), axis=0, keepdims=True)


def _causal_conv(u, prev, w):
    rows = lax.broadcasted_iota(jnp.int32, u.shape, 0)
    p1 = _row_of(prev, HALO - 1)
    p2 = _row_of(prev, HALO - 2)
    u1 = jnp.where(rows == 0, p1, pltpu.roll(u, 1, 0))
    u2 = jnp.where(rows == 0, p2, jnp.where(rows == 1, p1, pltpu.roll(u, 2, 0)))
    return w[2:3, :] * u + w[1:2, :] * u1 + w[0:1, :] * u2, u1, u2


def _rms(x, gain):
    return (x * lax.rsqrt(jnp.mean(x * x, axis=-1, keepdims=True) + EPS) * gain).astype(BF16)


def _embed_norm(x, meta, g, *, tm, name, comm=None):
    B, S, D = x.shape
    L = S + N_META
    per_seq = L // tm
    nt = B * per_seq
    body_rows = tm - N_META

    def body(meta_ref, g_ref, x_hbm, h_ref, n_ref, buf, sems):
        i = pl.program_id(0)

        def fetch(k, fn):
            slot, b, t = k % 2, k // per_seq, k % per_seq

            @pl.when(t == 0)
            def _():
                fn(pltpu.make_async_copy(x_hbm.at[b, pl.ds(0, body_rows)],
                                         buf.at[slot, pl.ds(N_META, body_rows)], sems.at[slot]))

            @pl.when(t != 0)
            def _():
                fn(pltpu.make_async_copy(x_hbm.at[b, pl.ds(pl.multiple_of(t * tm - N_META, 8), tm)],
                                         buf.at[slot], sems.at[slot]))

        @pl.when(i == 0)
        def _():
            fetch(i, lambda cp: cp.start())

        @pl.when(i + 1 < nt)
        def _():
            fetch(i + 1, lambda cp: cp.start())

        fetch(i, lambda cp: cp.wait())
        slot = i % 2

        @pl.when(i % per_seq == 0)
        def _():
            buf[slot, 0:N_META, :] = meta_ref[...]

        hv = buf[slot]
        h_ref[...] = hv
        n_ref[...] = _rms(hv, g_ref[...])

    row = pl.BlockSpec((tm, D), lambda i: (i, 0))
    return _launch(
        body, name=name, grid=(nt,),
        in_specs=[pl.BlockSpec((N_META, D), lambda i: (0, 0)), pl.BlockSpec((1, D), lambda i: (0, 0)), ANY],
        out_specs=[row, row],
        out_shape=[jax.ShapeDtypeStruct((B * L, D), F32), jax.ShapeDtypeStruct((B * L, D), BF16)],
        scratch_shapes=[pltpu.VMEM((2, tm, D), F32), pltpu.SemaphoreType.DMA((2,))],
        args=(meta, g, x), comm=comm)


def _ffn_up(n, wgu, sid, gu_prev, *, tm, first, count, name, comm=None):
    T, D = n.shape
    ns, _, guc = wgu.shape
    ff = N_SHARD * guc // 2

    def body(sid_ref, x_ref, w_ref, *rest):
        rest[-1][...] = _dot(x_ref[...], w_ref[...]).astype(BF16)

    where = lambda s, sid: sid[first + s]
    w_at = (lambda s, sid: 0) if ns == 1 else where
    return _launch(
        body, name=name, grid=(count, T // tm), prefetch=(sid,),
        in_specs=[pl.BlockSpec((tm, D), lambda s, i, sid: (i, 0)),
                  pl.BlockSpec((None, D, guc), lambda s, i, sid: (w_at(s, sid), 0, 0))]
                 + ([] if gu_prev is None else [ANY]),
        out_specs=pl.BlockSpec((None, tm, guc), lambda s, i, sid: (where(s, sid) // 2, i, where(s, sid) % 2)),
        out_shape=jax.ShapeDtypeStruct((2, T, ff), BF16),
        args=(n, wgu) + (() if gu_prev is None else (gu_prev,)),
        aliases=None if gu_prev is None else {2: 0}, comm=comm)


def _matmul_nn(x, w, *, tm, nb, w_spec, out_shape, out_spec, name, comm=None):
    T, K = x.shape

    def body(x_ref, w_ref, o_ref):
        o_ref[...] = _dot(x_ref[...], w_ref[...]).astype(o_ref.dtype)

    return _launch(
        body, name=name, grid=(nb, T // tm),
        in_specs=[pl.BlockSpec((tm, K), lambda s, i: (i, 0)), w_spec],
        out_specs=out_spec, out_shape=out_shape, args=(x, w), comm=comm)


def _ffn_down(gu, wd, h, next_gain, *, tm, name, comm=None):
    _, T, ff = gu.shape
    D = h.shape[1]
    chunks = _chunks(ff)

    def body(g_ref, u_ref, wd_hbm, h_ref, ng_ref, o_ref, n_ref, wd_v, sem):
        @pl.when(pl.program_id(0) == 0)
        def _():
            cp = pltpu.make_async_copy(wd_hbm, wd_v, sem)
            cp.start()
            cp.wait()

        acc = jnp.zeros((tm, D), F32)
        for c0, cw in chunks:
            a = _silu_mul(g_ref[:, c0:c0 + cw].astype(F32), u_ref[:, c0:c0 + cw].astype(F32))
            acc = acc + _dot(a.astype(BF16), wd_v[c0:c0 + cw, :])
        out = h_ref[...] + 0.5 * acc
        o_ref[...] = out
        n_ref[...] = _rms(out, ng_ref[...])

    return _launch(
        body, name=name, grid=(T // tm,),
        in_specs=[pl.BlockSpec((None, tm, ff), lambda i: (0, i, 0)),
                  pl.BlockSpec((None, tm, ff), lambda i: (1, i, 0)),
                  ANY,
                  pl.BlockSpec((tm, D), lambda i: (i, 0)),
                  pl.BlockSpec((1, D), lambda i: (0, 0))],
        out_specs=[pl.BlockSpec((tm, D), lambda i: (i, 0)), pl.BlockSpec((tm, D), lambda i: (i, 0))],
        out_shape=[jax.ShapeDtypeStruct((T, D), F32), jax.ShapeDtypeStruct((T, D), BF16)],
        scratch_shapes=[pltpu.VMEM((ff, D), BF16), pltpu.SemaphoreType.DMA],
        args=(gu, gu, wd, h, next_gain), comm=comm)


def _ffn_down_loss(gu, wd, h, gf, tgt, *, tm, name, comm=None):
    _, T, ff = gu.shape
    D = h.shape[1]
    B, S, _ = tgt.shape
    per_seq = (S + N_META) // tm
    body_rows = tm - N_META
    chunks = _chunks(ff)

    def body(g_ref, u_ref, wd_hbm, h_ref, gf_ref, tgt_hbm, dh_ref, dhb_ref, dg_ref, loss_ref, wd_v, tg_v, sem, tsem):
        i = pl.program_id(0)
        b, t = i // per_seq, i % per_seq

        @pl.when(i == 0)
        def _():
            cp = pltpu.make_async_copy(wd_hbm, wd_v, sem)
            cp.start()
            cp.wait()
            dg_ref[...] = jnp.zeros_like(dg_ref)
            loss_ref[...] = jnp.zeros_like(loss_ref)
            tg_v[0:N_META, :] = jnp.zeros((N_META, D), F32)

        def fetch(fn):
            @pl.when(t == 0)
            def _():
                fn(pltpu.make_async_copy(tgt_hbm.at[b, pl.ds(0, body_rows)], tg_v.at[pl.ds(N_META, body_rows)], tsem))

            @pl.when(t != 0)
            def _():
                fn(pltpu.make_async_copy(tgt_hbm.at[b, pl.ds(pl.multiple_of(t * tm - N_META, 8), tm)], tg_v, tsem))

        fetch(lambda cp: cp.start())
        acc = jnp.zeros((tm, D), F32)
        for c0, cw in chunks:
            a = _silu_mul(g_ref[:, c0:c0 + cw].astype(F32), u_ref[:, c0:c0 + cw].astype(F32))
            acc = acc + _dot(a.astype(BF16), wd_v[c0:c0 + cw, :])
        x = h_ref[...] + 0.5 * acc
        fetch(lambda cp: cp.wait())

        gain = gf_ref[...]
        r = lax.rsqrt(jnp.mean(x * x, axis=-1, keepdims=True) + EPS)
        y = x * r
        pos = t * tm + lax.broadcasted_iota(jnp.int32, (tm, 1), 0)
        err = jnp.where(pos >= N_META, y * gain - tg_v[...], 0.0)
        loss_ref[...] += 0.5 * jnp.sum(jnp.mean(err * err, axis=-1, keepdims=True))
        dout = err / D
        dg_ref[...] += jnp.sum(dout * y, axis=0, keepdims=True)
        dy = dout * gain
        dh = r * (dy - y * jnp.mean(dy * y, axis=-1, keepdims=True))
        dh_ref[...] = dh
        dhb_ref[...] = (0.5 * dh).astype(BF16)

    row = pl.BlockSpec((tm, D), lambda i: (i, 0))
    const = lambda i: (0, 0)
    return _launch(
        body, name=name, grid=(T // tm,),
        in_specs=[pl.BlockSpec((None, tm, ff), lambda i: (0, i, 0)),
                  pl.BlockSpec((None, tm, ff), lambda i: (1, i, 0)),
                  ANY, row, pl.BlockSpec((1, D), const), ANY],
        out_specs=[row, row, pl.BlockSpec((1, D), const), pl.BlockSpec((1, LANES), const)],
        out_shape=[jax.ShapeDtypeStruct((T, D), F32), jax.ShapeDtypeStruct((T, D), BF16),
                   jax.ShapeDtypeStruct((1, D), F32), jax.ShapeDtypeStruct((1, LANES), F32)],
        scratch_shapes=[pltpu.VMEM((ff, D), BF16), pltpu.VMEM((tm, D), F32), pltpu.SemaphoreType.DMA,
                        pltpu.SemaphoreType.DMA],
        args=(gu, gu, wd, h, gf, tgt), comm=comm)


def _ffn_bwd_act(df, gu, wd, *, tm, guc, name, comm=None):
    _, T, ff = gu.shape
    D = df.shape[1]
    nj = ff // guc
    chunks = _chunks(guc)

    def body(df_ref, g_ref, u_ref, wd_ref, o_ref, dwd_ref):
        @pl.when(pl.program_id(1) == 0)
        def _():
            dwd_ref[...] = jnp.zeros_like(dwd_ref)

        dfv = df_ref[...]
        for c0, cw in chunks:
            da = _dot_nt(dfv, wd_ref[c0:c0 + cw, :])
            g = g_ref[:, c0:c0 + cw].astype(F32)
            u = u_ref[:, c0:c0 + cw].astype(F32)
            sg = jax.nn.sigmoid(g)
            silu = g * sg
            o_ref[0, :, c0:c0 + cw] = (da * u * (sg * (1.0 + g * (1.0 - sg)))).astype(BF16)
            o_ref[1, :, c0:c0 + cw] = (da * silu).astype(BF16)
            dwd_ref[c0:c0 + cw, :] += _dot_tn((silu * u).astype(BF16), dfv)

    return _launch(
        body, name=name, grid=(nj, T // tm),
        in_specs=[pl.BlockSpec((tm, D), lambda j, i: (i, 0)),
                  pl.BlockSpec((None, tm, guc), lambda j, i: (0, i, j)),
                  pl.BlockSpec((None, tm, guc), lambda j, i: (1, i, j)),
                  pl.BlockSpec((guc, D), lambda j, i: (j, 0))],
        out_specs=[pl.BlockSpec((2, tm, guc), lambda j, i: (0, i, j)), pl.BlockSpec((guc, D), lambda j, i: (j, 0))],
        out_shape=[jax.ShapeDtypeStruct((2, T, ff), BF16), jax.ShapeDtypeStruct((ff, D), F32)],
        args=(df, gu, gu, wd), comm=comm)


def _ffn_bwd_in(dgu, wgu, h, g, dres, *, tm, scale, name, comm=None):
    _, T, ff = dgu.shape
    ns, D, guc = wgu.shape
    nj = ff // guc
    chunks = _chunks(guc)

    def body(dgu_ref, w_hbm, h_ref, g_ref, dres_ref, dh_ref, dhb_ref, dg_ref, w_v, acc, sem):
        i, j = pl.program_id(0), pl.program_id(1)

        @pl.when((i == 0) & (j == 0))
        def _():
            cp = pltpu.make_async_copy(w_hbm, w_v, sem)
            cp.start()
            cp.wait()
            dg_ref[...] = jnp.zeros_like(dg_ref)

        part = jnp.zeros((tm, D), F32)
        for c0, cw in chunks:
            part = part + _dot_nt(dgu_ref[0, :, c0:c0 + cw], w_v[j, :, c0:c0 + cw])
            part = part + _dot_nt(dgu_ref[1, :, c0:c0 + cw], w_v[nj + j, :, c0:c0 + cw])

        @pl.when(j == 0)
        def _():
            acc[...] = part

        @pl.when(j > 0)
        def _():
            acc[...] += part

        @pl.when(j == nj - 1)
        def _():
            dh, dgain = _rms_bwd(acc[...], h_ref[...], g_ref[...], dres_ref[...])
            dh_ref[...] = dh
            dhb_ref[...] = (scale * dh).astype(BF16)
            dg_ref[...] += dgain

    return _launch(
        body, name=name, grid=(T // tm, nj),
        in_specs=[pl.BlockSpec((2, tm, guc), lambda i, j: (0, i, j)),
                  ANY,
                  pl.BlockSpec((tm, D), lambda i, j: (i, 0)),
                  pl.BlockSpec((1, D), lambda i, j: (0, 0)),
                  pl.BlockSpec((tm, D), lambda i, j: (i, 0))],
        out_specs=[pl.BlockSpec((tm, D), lambda i, j: (i, 0)),
                   pl.BlockSpec((tm, D), lambda i, j: (i, 0)),
                   pl.BlockSpec((1, D), lambda i, j: (0, 0))],
        out_shape=[jax.ShapeDtypeStruct((T, D), F32), jax.ShapeDtypeStruct((T, D), BF16),
                   jax.ShapeDtypeStruct((1, D), F32)],
        scratch_shapes=[pltpu.VMEM((ns, D, guc), BF16), pltpu.VMEM((tm, D), F32), pltpu.SemaphoreType.DMA],
        args=(dgu, wgu, h, g, dres), comm=comm)


def _ffn_bwd_in_first(dgu, wgu, h, g, dres, *, tm, batch, name, comm=None):
    _, T, ff = dgu.shape
    ns, D, guc = wgu.shape
    nj = ff // guc
    nt = T // tm
    L = T // batch
    per_seq = L // tm
    body_rows = tm - N_META
    chunks = _chunks(guc)

    def body(dgu_ref, w_hbm, h_ref, g_ref, dres_ref, dx_hbm, dmeta_ref, dg_ref, w_v, acc, dh_v, sem, osem):
        i, j = pl.program_id(0), pl.program_id(1)

        @pl.when((i == 0) & (j == 0))
        def _():
            cp = pltpu.make_async_copy(w_hbm, w_v, sem)
            cp.start()
            cp.wait()
            dg_ref[...] = jnp.zeros_like(dg_ref)
            dmeta_ref[...] = jnp.zeros_like(dmeta_ref)

        part = jnp.zeros((tm, D), F32)
        for c0, cw in chunks:
            part = part + _dot_nt(dgu_ref[0, :, c0:c0 + cw], w_v[j, :, c0:c0 + cw])
            part = part + _dot_nt(dgu_ref[1, :, c0:c0 + cw], w_v[nj + j, :, c0:c0 + cw])

        @pl.when(j == 0)
        def _():
            acc[...] = part

        @pl.when(j > 0)
        def _():
            acc[...] += part

        def head_copy(b):
            return pltpu.make_async_copy(dh_v.at[pl.ds(N_META, body_rows)], dx_hbm.at[b, pl.ds(0, body_rows)], osem)

        def tail_copy(b, t):
            return pltpu.make_async_copy(dh_v, dx_hbm.at[b, pl.ds(pl.multiple_of(t * tm - N_META, 8), tm)], osem)

        def on_tile(k, head_fn, tail_fn):
            @pl.when(k % per_seq == 0)
            def _():
                head_fn(head_copy(k // per_seq))

            @pl.when(k % per_seq != 0)
            def _():
                tail_fn(tail_copy(k // per_seq, k % per_seq))

        @pl.when(j == nj - 1)
        def _():
            dh, dgain = _rms_bwd(acc[...], h_ref[...], g_ref[...], dres_ref[...])
            dg_ref[...] += dgain

            @pl.when(i > 0)
            def _():
                on_tile(i - 1, lambda cp: cp.wait(), lambda cp: cp.wait())

            dh_v[...] = dh

            @pl.when(i % per_seq == 0)
            def _():
                dmeta_ref[...] += dh[0:N_META, :]

            on_tile(i, lambda cp: cp.start(), lambda cp: cp.start())

            @pl.when(i == nt - 1)
            def _():
                on_tile(i, lambda cp: cp.wait(), lambda cp: cp.wait())

    return _launch(
        body, name=name, grid=(nt, nj),
        in_specs=[pl.BlockSpec((2, tm, guc), lambda i, j: (0, i, j)),
                  ANY,
                  pl.BlockSpec((tm, D), lambda i, j: (i, 0)),
                  pl.BlockSpec((1, D), lambda i, j: (0, 0)),
                  pl.BlockSpec((tm, D), lambda i, j: (i, 0))],
        out_specs=[ANY, pl.BlockSpec((N_META, D), lambda i, j: (0, 0)), pl.BlockSpec((1, D), lambda i, j: (0, 0))],
        out_shape=[jax.ShapeDtypeStruct((batch, L - N_META, D), F32), jax.ShapeDtypeStruct((N_META, D), F32),
                   jax.ShapeDtypeStruct((1, D), F32)],
        scratch_shapes=[pltpu.VMEM((ns, D, guc), BF16), pltpu.VMEM((tm, D), F32), pltpu.VMEM((tm, D), F32),
                        pltpu.SemaphoreType.DMA, pltpu.SemaphoreType.DMA],
        args=(dgu, wgu, h, g, dres), comm=comm)


def _mix_bwd_in(parts, w_main, w_fg, h, g, dres, *, tm, scale, name, comm=None):
    T, D = h.shape
    widths = [p.shape[1] for p in parts]
    offs = [sum(widths[:k]) for k in range(len(widths))]
    npart = len(parts)
    wide = sum(widths)

    def body(*refs):
        p_refs = refs[:npart]
        wm_ref, wf_ref, h_ref, g_ref, dres_ref, dh_ref, dhb_ref, dg_ref, all_ref = refs[npart:]

        @pl.when(pl.program_id(0) == 0)
        def _():
            dg_ref[...] = jnp.zeros_like(dg_ref)

        dn = jnp.zeros((tm, D), F32)
        for k, (p_ref, off, wd_) in enumerate(zip(p_refs, offs, widths)):
            for c0, cw in _chunks(wd_):
                piece = p_ref[:, c0:c0 + cw].astype(BF16)
                all_ref[:, off + c0:off + c0 + cw] = piece
                w = wf_ref[...] if k == npart - 1 else wm_ref[:, off + c0:off + c0 + cw]
                dn = dn + _dot_nt(piece, w)
        dh, dgain = _rms_bwd(dn, h_ref[...], g_ref[...], dres_ref[...])
        dh_ref[...] = dh
        dhb_ref[...] = (scale * dh).astype(BF16)
        dg_ref[...] += dgain

    row = lambda i: (i, 0)
    const = lambda i: (0, 0)
    return _launch(
        body, name=name, grid=(T // tm,),
        in_specs=[pl.BlockSpec((tm, p.shape[1]), row) for p in parts]
                 + [pl.BlockSpec(w_main.shape, const), pl.BlockSpec(w_fg.shape, const),
                    pl.BlockSpec((tm, D), row), pl.BlockSpec((1, D), const), pl.BlockSpec((tm, D), row)],
        out_specs=[pl.BlockSpec((tm, D), row), pl.BlockSpec((tm, D), row), pl.BlockSpec((1, D), const),
                   pl.BlockSpec((tm, wide), row)],
        out_shape=[jax.ShapeDtypeStruct((T, D), F32), jax.ShapeDtypeStruct((T, D), BF16),
                   jax.ShapeDtypeStruct((1, D), F32), jax.ShapeDtypeStruct((T, wide), BF16)],
        args=(*parts, w_main, w_fg, h, g, dres), comm=comm)


def _matmul_tn(x, y, *, tm, nb, x_spec, y_spec, out_shape, out_spec, kb, name, comm=None):
    T = y.shape[-2]
    chunks = _chunks(kb)

    def body(x_ref, y_ref, o_ref):
        @pl.when(pl.program_id(1) == 0)
        def _():
            o_ref[...] = jnp.zeros_like(o_ref)

        yv = y_ref[...].astype(BF16)
        for c0, cw in chunks:
            o_ref[c0:c0 + cw, :] += _dot_tn(x_ref[:, c0:c0 + cw], yv)

    return _launch(
        body, name=name, grid=(nb, T // tm),
        in_specs=[x_spec, y_spec], out_specs=out_spec, out_shape=out_shape, args=(x, y), comm=comm)


def _tri(n, lower):
    r = lax.broadcasted_iota(jnp.int32, (n, n), 0)
    c = lax.broadcasted_iota(jnp.int32, (n, n), 1)
    return jnp.where((r >= c) if lower else (r <= c), 1.0, 0.0).astype(BF16)


def _tri_dot(tri, v):
    hi, mid, lo = _split3(v)
    return _dot(tri, hi) + _dot(tri, mid) + _dot(tri, lo)


def _fcum(fg, bf, *, ch, name):
    B, L, W = fg.shape
    nch = L // ch

    def body(fg_ref, bf_ref, f_ref):
        tri = _tri(ch, True)
        carry = jnp.zeros((1, W), F32)
        for c in range(nch):
            x = fg_ref[c * ch:(c + 1) * ch, :] + bf_ref[...]
            lf = jnp.minimum(x, 0.0) - jnp.log(1.0 + jnp.exp(-jnp.abs(x)))
            f_ref[c * ch:(c + 1) * ch, :] = _tri_dot(tri, lf) + carry
            carry = carry + jnp.sum(lf, axis=0, keepdims=True)

    return pl.pallas_call(
        body, name=name, grid=(B,),
        in_specs=[pl.BlockSpec((None, L, W), lambda b: (b, 0, 0)), pl.BlockSpec((1, W), lambda b: (0, 0))],
        out_specs=pl.BlockSpec((None, L, W), lambda b: (b, 0, 0)),
        out_shape=jax.ShapeDtypeStruct((B, L, W), F32),
        compiler_params=_params("arbitrary"),
    )(fg, bf)


def _fcum_bwd(dF, fg, bf, *, ch, name):
    B, L, W = fg.shape
    nch = L // ch

    def body(df_ref, fg_ref, bf_ref, dfg_ref, db_ref):
        @pl.when(pl.program_id(0) == 0)
        def _():
            db_ref[...] = jnp.zeros_like(db_ref)

        tri = _tri(ch, False)
        carry = jnp.zeros((1, W), F32)
        dbs = jnp.zeros((1, W), F32)
        for c in reversed(range(nch)):
            d = df_ref[c * ch:(c + 1) * ch, :]
            dlf = _tri_dot(tri, d) + carry
            carry = carry + jnp.sum(d, axis=0, keepdims=True)
            x = fg_ref[c * ch:(c + 1) * ch, :] + bf_ref[...]
            dfg = dlf * jax.nn.sigmoid(-x)
            dfg_ref[c * ch:(c + 1) * ch, :] = dfg.astype(BF16)
            dbs = dbs + jnp.sum(dfg, axis=0, keepdims=True)
        db_ref[...] += dbs

    blk = pl.BlockSpec((None, L, W), lambda b: (b, 0, 0))
    return pl.pallas_call(
        body, name=name, grid=(B,),
        in_specs=[blk, blk, pl.BlockSpec((1, W), lambda b: (0, 0))],
        out_specs=[blk, pl.BlockSpec((1, W), lambda b: (0, 0))],
        out_shape=[jax.ShapeDtypeStruct((B, L, W), BF16), jax.ShapeDtypeStruct((1, W), F32)],
        compiler_params=_params("arbitrary"),
    )(dF, fg, bf)


def _band_edges(tq):
    return sorted({min(tq, (k * tq // ATTN_BANDS + HALO - 1) // HALO * HALO) for k in range(ATTN_BANDS + 1)})


def _pair(h):
    return slice((h // 2) * 2 * HEAD_DIM, (h // 2 + 1) * 2 * HEAD_DIM)


def _own_lanes(a, h):
    low = lax.broadcasted_iota(jnp.int32, a.shape, 1) < HEAD_DIM
    return jnp.where(low if h % 2 == 0 else jnp.logical_not(low), a, jnp.zeros_like(a))


def _attn_fwd(proj, fc, fr, *, tq, n_heads, name, comm=None):
    B, L, _ = proj.shape
    AD = n_heads * HEAD_DIM
    nq = L // tq
    W = fc.shape[-1]
    scale = HEAD_DIM ** -0.5
    edges = _band_edges(tq)

    def v_ones(vp, h):
        lane = lax.broadcasted_iota(jnp.int32, vp.shape, 1)
        one = jnp.ones_like(vp)
        zero = jnp.zeros_like(vp)
        if h % 2 == 0:
            return jnp.where(lane < HEAD_DIM, vp, jnp.where(lane == HEAD_DIM, one, zero))
        return jnp.where(lane >= HEAD_DIM, vp, jnp.where(lane == 0, one, zero))

    def sum_lane(h):
        return HEAD_DIM if h % 2 == 0 else 0

    def body(q_ref, k_ref, v_ref, fr_ref, o_ref, lse_ref, m_s, acc_s):
        qi, ki = pl.program_id(1), pl.program_id(2)

        @pl.when(ki == 0)
        def _():
            m_s[...] = jnp.full_like(m_s, NEG)
            acc_s[...] = jnp.zeros_like(acc_s)

        def tile(diagonal):
            lane = lax.broadcasted_iota(jnp.int32, (tq, W), 1)
            m_all = m_s[...]
            m_out = m_all
            bands = [(r0, r1, r1 if diagonal else tq) for r0, r1 in zip(edges[:-1], edges[1:])]
            if diagonal:
                masks = {r0: (lax.broadcasted_iota(jnp.int32, (r1 - r0, c1), 1)
                              <= r0 + lax.broadcasted_iota(jnp.int32, (r1 - r0, c1), 0)) for r0, r1, c1 in bands}

            def scores(h, band):
                r0, r1, c1 = band
                sl = slice(h * HEAD_DIM, (h + 1) * HEAD_DIM)
                return _dot_nt(q_ref[r0:r1, sl] * scale, k_ref[0:c1, sl])

            work = [(h, band) for h in range(n_heads) for band in bands]
            nxt = scores(*work[0])
            for w, (h, band) in enumerate(work):
                r0, r1, c1 = band
                sl = slice(h * HEAD_DIM, (h + 1) * HEAD_DIM)
                s = nxt - fr_ref[h:h + 1, 0:c1]
                if w + 1 < len(work):
                    nxt = scores(*work[w + 1])
                if diagonal:
                    s = jnp.where(masks[r0], s, NEG)
                m_old = m_all[r0:r1, h:h + 1]
                m_new = jnp.maximum(m_old, jnp.max(s, axis=1, keepdims=True))
                alpha = jnp.exp(m_old - m_new)
                p = jnp.exp(s - m_new)
                own = slice(h * 2 * HEAD_DIM, (h + 1) * 2 * HEAD_DIM)
                acc_s[r0:r1, own] = alpha * acc_s[r0:r1, own] + _dot(p.astype(BF16), v_ones(v_ref[0:c1, _pair(h)], h))
                if r0 == 0:
                    m_parts = []
                m_parts.append(m_new)
                if r1 == tq:
                    m_out = jnp.where(lane == h, jnp.concatenate(m_parts, axis=0), m_out)
            m_s[...] = m_out

        @pl.when(ki < qi)
        def _():
            tile(False)

        @pl.when(ki == qi)
        def _():
            tile(True)
            lane = lax.broadcasted_iota(jnp.int32, (tq, W), 1)
            low = lax.broadcasted_iota(jnp.int32, (tq, 2 * HEAD_DIM), 1) < HEAD_DIM
            l_all = jnp.ones((tq, W), F32)
            for h in range(0, n_heads, 2):
                even = acc_s[:, h * 2 * HEAD_DIM:(h + 1) * 2 * HEAD_DIM]
                odd = acc_s[:, (h + 1) * 2 * HEAD_DIM:(h + 2) * 2 * HEAD_DIM]
                l_even = even[:, sum_lane(h):sum_lane(h) + 1]
                l_odd = odd[:, sum_lane(h + 1):sum_lane(h + 1) + 1]
                o_ref[:, _pair(h)] = jnp.where(low, even / l_even, odd / l_odd)
                l_all = jnp.where(lane == h, l_even, jnp.where(lane == h + 1, l_odd, l_all))
            lse_ref[...] = jnp.where(lane < n_heads, m_s[...] + jnp.log(l_all), 0.0)

    kv = lambda b, qi, ki: jnp.minimum(ki, qi)
    return _launch(
        body, name=name, grid=(B, nq, nq), args=(proj, proj, proj, fr), comm=comm,
        in_specs=[pl.BlockSpec((None, tq, AD), lambda b, qi, ki: (b, qi, 3)),
                  pl.BlockSpec((None, tq, AD), lambda b, qi, ki: (b, kv(b, qi, ki), 4)),
                  pl.BlockSpec((None, tq, AD), lambda b, qi, ki: (b, kv(b, qi, ki), 5)),
                  pl.BlockSpec((None, None, n_heads, tq), lambda b, qi, ki: (b, kv(b, qi, ki), 0, 0))],
        out_specs=[pl.BlockSpec((None, tq, AD), lambda b, qi, ki: (b, qi, 0)),
                   pl.BlockSpec((None, tq, W), lambda b, qi, ki: (b, qi, 0))],
        out_shape=[jax.ShapeDtypeStruct((B, L, AD), F32), jax.ShapeDtypeStruct((B, L, W), F32)],
        scratch_shapes=[pltpu.VMEM((tq, W), F32), pltpu.VMEM((tq, n_heads * 2 * HEAD_DIM), F32)])


def _attn_bwd(proj, o, do, lse, fc, fr, *, tq, n_heads, name, comm=None):
    B, L, _ = proj.shape
    AD = n_heads * HEAD_DIM
    nq = L // tq
    W = fc.shape[-1]
    scale = HEAD_DIM ** -0.5
    edges = _band_edges(tq)

    def body(q_ref, k_ref, v_ref, o_ref, do_ref, lse_ref, fr_ref,
             dq_ref, dk_ref, dv_ref, dfr_ref, dfq_ref, dk_s, dv_s):
        kj, qi = pl.program_id(1), pl.program_id(2)

        @pl.when((kj == 0) & (qi == 0))
        def _():
            dq_ref[...] = jnp.zeros_like(dq_ref)
            dfq_ref[...] = jnp.zeros_like(dfq_ref)

        @pl.when(qi == kj)
        def _():
            dk_s[...] = jnp.zeros_like(dk_s)
            dv_s[...] = jnp.zeros_like(dv_s)
            dfr_ref[...] = jnp.zeros_like(dfr_ref)

        def tile(diagonal):
            bands = [(r0, r1, r1) for r0, r1 in zip(edges[:-1], edges[1:])] if diagonal else [(0, tq, tq)]
            lse = lse_ref[...]
            for r0, r1, c1 in bands:
                nr = r1 - r0
                rows = pl.ds(pl.multiple_of(qi * tq + r0, 8), nr)
                if diagonal:
                    mask = (lax.broadcasted_iota(jnp.int32, (nr, c1), 1)
                            <= r0 + lax.broadcasted_iota(jnp.int32, (nr, c1), 0))
                lane = lax.broadcasted_iota(jnp.int32, (nr, W), 1)
                head = lax.broadcasted_iota(jnp.int32, (n_heads, c1), 0)
                dfq = jnp.zeros((nr, W), F32)
                dfr = jnp.zeros((n_heads, c1), F32)
                for h in range(n_heads):
                    ps = _pair(h)
                    k, v = k_ref[0:c1, ps], v_ref[0:c1, ps]
                    q = _own_lanes(q_ref[r0:r1, ps] * scale, h)
                    dov = _own_lanes(do_ref[r0:r1, ps], h)
                    s = _dot_nt(q, k) - fr_ref[h:h + 1, 0:c1]
                    if diagonal:
                        s = jnp.where(mask, s, NEG)
                    p = jnp.exp(s - lse[r0:r1, h:h + 1])
                    dp = _dot_nt(dov, v)
                    dsum = jnp.sum(dov.astype(F32) * o_ref[r0:r1, ps], axis=1, keepdims=True)
                    ds = p * (dp - dsum)
                    dsb = ds.astype(BF16)
                    dv = _dot_tn(p.astype(BF16), dov)
                    dk = _dot_tn(dsb, q)
                    dq = _dot(dsb, _own_lanes(k, h))
                    if h % 2 == 0:
                        dv_even, dk_even, dq_even = dv, dk, dq
                    else:
                        dv_s[0:c1, ps] += dv_even + dv
                        dk_s[0:c1, ps] += dk_even + dk
                        dq_ref[rows, ps] += (dq_even + dq) * scale
                    dfr = jnp.where(head == h, jnp.sum(ds, axis=0, keepdims=True), dfr)
                    dfq = jnp.where(lane == h, jnp.sum(ds, axis=1, keepdims=True), dfq)
                dfr_ref[:, 0:c1] -= dfr
                dfq_ref[rows, :] += dfq

        @pl.when(qi > kj)
        def _():
            tile(False)

        @pl.when(qi == kj)
        def _():
            tile(True)

        @pl.when(qi == nq - 1)
        def _():
            dk_ref[...] = dk_s[...].astype(BF16)
            dv_ref[...] = dv_s[...].astype(BF16)

    qq = lambda b, kj, qi: jnp.maximum(qi, kj)
    qblk = lambda w, cb: pl.BlockSpec((None, tq, w), lambda b, kj, qi: (b, qq(b, kj, qi), cb))
    kblk = lambda cb: pl.BlockSpec((None, tq, AD), lambda b, kj, qi: (b, kj, cb))
    return _launch(
        body, name=name, grid=(B, nq, nq), args=(proj, proj, proj, o, do, lse, fr), comm=comm,
        in_specs=[qblk(AD, 3), kblk(4), kblk(5), qblk(AD, 0), qblk(AD, 0), qblk(W, 0),
                  pl.BlockSpec((None, None, n_heads, tq), lambda b, kj, qi: (b, kj, 0, 0))],
        out_specs=[pl.BlockSpec((None, L, AD), lambda b, kj, qi: (b, 0, 0)),
                   kblk(0), kblk(0),
                   pl.BlockSpec((None, None, n_heads, tq), lambda b, kj, qi: (b, kj, 0, 0)),
                   pl.BlockSpec((None, L, W), lambda b, kj, qi: (b, 0, 0))],
        out_shape=[jax.ShapeDtypeStruct((B, L, AD), F32), jax.ShapeDtypeStruct((B, L, AD), BF16),
                   jax.ShapeDtypeStruct((B, L, AD), BF16), jax.ShapeDtypeStruct((B, nq, n_heads, tq), F32),
                   jax.ShapeDtypeStruct((B, L, W), F32)],
        scratch_shapes=[pltpu.VMEM((tq, AD), F32), pltpu.VMEM((tq, AD), F32)])


def _mix_gather(refs, first):
    b_ref, c_ref, hc_ref, cp_ref, hcp_ref, o_ref, cw_ref, p_ref = refs
    bg = b_ref[...].astype(F32)
    u = c_ref[...].astype(F32) * hc_ref[...].astype(F32)
    prev = cp_ref[...].astype(F32) * hcp_ref[...].astype(F32)
    prev = jnp.where(first, 0.0, prev)
    cv, u1, u2 = _causal_conv(u, prev, cw_ref[...])
    yc = bg * cv
    p = p_ref[...]
    rc = lax.rsqrt(_group_mean(yc * yc, p) + EPS)
    ya = o_ref[...].astype(F32)
    ra = lax.rsqrt(_group_mean(ya * ya, p) + EPS)
    return bg, (u, u1, u2), cv, yc * rc, rc, ya * ra, ra


def _mix_specs(tm, CD, D, grid_rank_fn):
    per = tm // HALO
    cur = lambda cb: pl.BlockSpec((None, tm, CD), lambda b, i: (b, i, cb))
    prev = lambda cb: pl.BlockSpec((None, HALO, CD), lambda b, i: (b, jnp.maximum(i * per - 1, 0), cb))
    return [cur(0), cur(1), cur(2), prev(1), prev(2), cur(0)]


def _mix_out(proj, o, cw, gc, ga, wout, h, pmat, next_gain, *, tm, name, comm=None):
    B, L, D = h.shape
    CD = o.shape[-1]
    const = lambda b, i: (0, 0)

    def body(b_ref, c_ref, hc_ref, cp_ref, hcp_ref, o_ref, cw_ref, p_ref, gc_ref, ga_ref, w_ref, h_ref, ng_ref,
             out_ref, y_ref, n_ref):
        first = pl.program_id(1) == 0
        _, _, _, zc, _, za, _ = _mix_gather((b_ref, c_ref, hc_ref, cp_ref, hcp_ref, o_ref, cw_ref, p_ref), first)
        yc = (zc * gc_ref[...]).astype(BF16)
        ya = (za * ga_ref[...]).astype(BF16)
        y_ref[:, :CD] = yc
        y_ref[:, CD:] = ya
        out = h_ref[...] + _dot(yc, w_ref[:CD, :]) + _dot(ya, w_ref[CD:, :])
        out_ref[...] = out
        n_ref[...] = _rms(out, ng_ref[...])

    tile = pl.BlockSpec((None, tm, D), lambda b, i: (b, i, 0))
    return _launch(
        body, name=name, grid=(B, L // tm),
        in_specs=_mix_specs(tm, CD, D, None)
                 + [pl.BlockSpec(cw.shape, const), pl.BlockSpec(pmat.shape, const),
                    pl.BlockSpec((1, CD), const), pl.BlockSpec((1, CD), const), pl.BlockSpec((D, D), const),
                    tile, pl.BlockSpec((1, D), const)],
        out_specs=[tile, tile, tile],
        out_shape=[jax.ShapeDtypeStruct((B, L, D), F32), jax.ShapeDtypeStruct((B, L, D), BF16),
                   jax.ShapeDtypeStruct((B, L, D), BF16)],
        args=(proj, proj, proj, proj, proj, o, cw, pmat, gc, ga, wout, h, next_gain), comm=comm)


def _mix_out_bwd(dhb, proj, o, cw, gc, ga, wout, pmat, *, tm, name, comm=None):
    B, L, D = dhb.shape
    CD = o.shape[-1]
    const = lambda b, i: (0, 0)

    def body(dh_ref, b_ref, c_ref, hc_ref, cp_ref, hcp_ref, o_ref, cw_ref, p_ref, gc_ref, ga_ref, w_ref,
             db_ref, dcv_ref, do_ref, dgc_ref, dga_ref, dcw_ref):
        first = pl.program_id(1) == 0

        @pl.when((pl.program_id(0) == 0) & first)
        def _():
            dgc_ref[...] = jnp.zeros_like(dgc_ref)
            dga_ref[...] = jnp.zeros_like(dga_ref)
            dcw_ref[...] = jnp.zeros_like(dcw_ref)

        bg, us, cv, zc, rc, za, ra = _mix_gather(
            (b_ref, c_ref, hc_ref, cp_ref, hcp_ref, o_ref, cw_ref, p_ref), first)
        p = p_ref[...]
        dh = dh_ref[...]
        dyc = _dot_nt(dh, w_ref[:CD, :])
        dya = _dot_nt(dh, w_ref[CD:, :])

        dgc_ref[...] += jnp.sum(dyc * zc, axis=0, keepdims=True)
        dz = dyc * gc_ref[...]
        dx = rc * (dz - zc * _group_mean(dz * zc, p))
        db_ref[...] = (dx * cv).astype(BF16)
        dcv = dx * bg
        dcv_ref[...] = dcv.astype(BF16)
        for k in range(3):
            dcw_ref[k:k + 1, :] += jnp.sum(dcv * us[2 - k], axis=0, keepdims=True)

        dga_ref[...] += jnp.sum(dya * za, axis=0, keepdims=True)
        dz = dya * ga_ref[...]
        do_ref[...] = (ra * (dz - za * _group_mean(dz * za, p))).astype(BF16)

    tile = lambda w: pl.BlockSpec((None, tm, w), lambda b, i: (b, i, 0))
    return _launch(
        body, name=name, grid=(B, L // tm), comm=comm,
        args=(dhb, proj, proj, proj, proj, proj, o, cw, pmat, gc, ga, wout),
        in_specs=[tile(D)] + _mix_specs(tm, CD, D, None)
                 + [pl.BlockSpec(cw.shape, const), pl.BlockSpec(pmat.shape, const),
                    pl.BlockSpec((1, CD), const), pl.BlockSpec((1, CD), const), pl.BlockSpec((D, D), const)],
        out_specs=[tile(CD), tile(CD), tile(CD),
                   pl.BlockSpec((1, CD), const), pl.BlockSpec((1, CD), const), pl.BlockSpec((8, CD), const)],
        out_shape=[jax.ShapeDtypeStruct((B, L, CD), BF16)] * 3
                  + [jax.ShapeDtypeStruct((1, CD), F32)] * 2 + [jax.ShapeDtypeStruct((8, CD), F32)])


def _conv_bwd(dcv, proj, cw, *, tm, name):
    B, L, CD = dcv.shape
    per = tm // HALO
    nhalo = L // HALO
    nt = L // tm

    def body(d_ref, dn_ref, c_ref, hc_ref, cw_ref, out_ref):
        last = pl.program_id(1) == nt - 1
        d = d_ref[...].astype(F32)
        nxt = jnp.where(last, 0.0, dn_ref[...].astype(F32))
        n0, n1 = _row_of(nxt, 0), _row_of(nxt, 1)
        rows = lax.broadcasted_iota(jnp.int32, d.shape, 0)
        d1 = jnp.where(rows == tm - 1, n0, pltpu.roll(d, tm - 1, 0))
        d2 = jnp.where(rows == tm - 2, n0, jnp.where(rows == tm - 1, n1, pltpu.roll(d, tm - 2, 0)))
        w = cw_ref[...]
        du = w[2:3, :] * d + w[1:2, :] * d1 + w[0:1, :] * d2
        out_ref[:, :CD] = (du * hc_ref[...].astype(F32)).astype(BF16)
        out_ref[:, CD:] = (du * c_ref[...].astype(F32)).astype(BF16)

    return pl.pallas_call(
        body, name=name, grid=(B, nt),
        in_specs=[pl.BlockSpec((None, tm, CD), lambda b, i: (b, i, 0)),
                  pl.BlockSpec((None, HALO, CD), lambda b, i: (b, jnp.minimum((i + 1) * per, nhalo - 1), 0)),
                  pl.BlockSpec((None, tm, CD), lambda b, i: (b, i, 1)),
                  pl.BlockSpec((None, tm, CD), lambda b, i: (b, i, 2)),
                  pl.BlockSpec(cw.shape, lambda b, i: (0, 0))],
        out_specs=pl.BlockSpec((None, tm, 2 * CD), lambda b, i: (b, i, 0)),
        out_shape=jax.ShapeDtypeStruct((B, L, 2 * CD), BF16),
        compiler_params=_params("arbitrary", "arbitrary"),
    )(dcv, dcv, proj, proj, cw)


def _place():
    x, y, c = lax.axis_index("x"), lax.axis_index("y"), lax.axis_index("c")
    others = [(1 - x, y), (x, 1 - y), (1 - x, 1 - y)]
    return x, y, c, others


def _all_gather_shards(shards, *, name):
    n = len(shards)

    def body(*refs):
        ins, outs = refs[:n], refs[n:2 * n]
        send, recv, fsend, frecv, lsem = refs[2 * n:]
        x, y, c, others = _place()
        me = 2 * x + y
        local = [pltpu.make_async_copy(ins[t], outs[t].at[me], lsem.at[t]) for t in range(n)]
        for cp in local:
            cp.start()

        def half(t, k):
            hr = shards[t].shape[0] // 2
            return pl.ds(pl.multiple_of(k * hr, HALO), hr)

        def ici(t, j, src_chip, to):
            src = ins[t].at[half(t, c)] if to is not None else outs[t].at[src_chip, half(t, c)]
            return pltpu.make_async_remote_copy(
                src_ref=src, dst_ref=outs[t].at[src_chip, half(t, c)],
                send_sem=send.at[3 * t + j], recv_sem=recv.at[3 * t + j],
                device_id=(x, y, c) if to is None else to, device_id_type=MESH)

        def d2d(t, j, src_chip, k):
            return pltpu.make_async_remote_copy(
                src_ref=outs[t].at[src_chip, half(t, k)], dst_ref=outs[t].at[src_chip, half(t, k)],
                send_sem=fsend.at[3 * t + j], recv_sem=frecv.at[3 * t + j],
                device_id=(x, y, 1 - c), device_id_type=MESH)

        firsts = [ici(t, j, me, (ox, oy, c)) for t in range(n) for j, (ox, oy) in enumerate(others)]
        for cp in firsts:
            cp.start()
        passed = []
        for t in range(n):
            for j, (ox, oy) in enumerate(others):
                ici(t, j, 2 * ox + oy, None).wait_recv()
                cp = d2d(t, j, 2 * ox + oy, c)
                cp.start()
                passed.append(cp)
        for t in range(n):
            for j, (ox, oy) in enumerate(others):
                d2d(t, j, 2 * ox + oy, 1 - c).wait_recv()
        for cp in firsts + passed:
            cp.wait_send()
        for cp in local:
            cp.wait()

    return pl.pallas_call(
        body, name=name,
        in_specs=[ANY] * n, out_specs=[ANY] * n,
        out_shape=[jax.ShapeDtypeStruct((N_SHARD,) + s.shape, s.dtype) for s in shards],
        scratch_shapes=[pltpu.SemaphoreType.DMA((3 * n,))] * 4 + [pltpu.SemaphoreType.DMA((n,))],
    )(*shards)


def _all_reduce_small(slab, *, name):
    def body(in_ref, out_ref, gath, send, recv):
        x, y, c, _ = _place()
        me = 4 * x + 2 * y + c
        gath[me] = in_ref[...]
        copies, peers = [], []
        for m in range(1, N_DEV):
            px = jnp.where((m >> 2) & 1, 1 - x, x)
            py = jnp.where((m >> 1) & 1, 1 - y, y)
            pc = jnp.where(m & 1, 1 - c, c)
            cp = pltpu.make_async_remote_copy(
                src_ref=in_ref, dst_ref=gath.at[me], send_sem=send.at[m - 1], recv_sem=recv.at[m - 1],
                device_id=(px, py, pc), device_id_type=MESH)
            cp.start()
            copies.append(cp)
            peers.append(4 * px + 2 * py + pc)
        for m in range(1, N_DEV):
            pltpu.make_async_remote_copy(
                src_ref=in_ref, dst_ref=gath.at[peers[m - 1]], send_sem=send.at[m - 1], recv_sem=recv.at[m - 1],
                device_id=(x, y, c), device_id_type=MESH).wait_recv()
        for cp in copies:
            cp.wait_send()
        acc = gath[0]
        for k in range(1, N_DEV):
            acc = acc + gath[k]
        out_ref[...] = acc

    vm = pl.BlockSpec(memory_space=pltpu.VMEM)
    return pl.pallas_call(
        body, name=name, in_specs=[vm], out_specs=vm,
        out_shape=jax.ShapeDtypeStruct(slab.shape, slab.dtype),
        scratch_shapes=[pltpu.VMEM((N_DEV,) + slab.shape, slab.dtype),
                        pltpu.SemaphoreType.DMA((N_DEV - 1,)), pltpu.SemaphoreType.DMA((N_DEV - 1,))],
    )(slab)


def _gather_ici(shards):
    n = len(shards)

    def copies(ins, outs, sems, sending):
        send, recv, _ = sems
        x, y, c, others = _place()
        me = 2 * x + y
        out = []
        for t in range(n):
            hr = shards[t].shape[0] // 2
            rows = pl.ds(pl.multiple_of(c * hr, HALO), hr)
            for j, (ox, oy) in enumerate(others):
                src_chip = me if sending else 2 * ox + oy
                out.append(pltpu.make_async_remote_copy(
                    src_ref=ins[t].at[rows], dst_ref=outs[t].at[src_chip, rows],
                    send_sem=send.at[3 * t + j], recv_sem=recv.at[3 * t + j],
                    device_id=(ox, oy, c) if sending else (x, y, c), device_id_type=MESH))
        return out

    def local(ins, outs, sems):
        x, y, _, _ = _place()
        return [pltpu.make_async_copy(ins[t], outs[t].at[2 * x + y], sems[2].at[t]) for t in range(n)]

    def start(ins, outs, sems):
        for cp in local(ins, outs, sems) + copies(ins, outs, sems, True):
            cp.start()

    def finish(ins, outs, sems):
        for cp in copies(ins, outs, sems, False):
            cp.wait_recv()
        for cp in copies(ins, outs, sems, True):
            cp.wait_send()
        for cp in local(ins, outs, sems):
            cp.wait()

    return _Comm(shards, [jax.ShapeDtypeStruct((N_SHARD,) + s.shape, s.dtype) for s in shards],
                 [3 * n, 3 * n, n], start, finish)


def _gather_d2d(parts):
    n = len(parts)

    def copies(outs, sems, sending):
        send, recv = sems
        x, y, c, others = _place()
        out = []
        for t in range(n):
            hr = parts[t].shape[1] // 2
            rows = pl.ds(pl.multiple_of((c if sending else 1 - c) * hr, HALO), hr)
            for j, (ox, oy) in enumerate(others):
                blk = outs[t].at[2 * ox + oy, rows]
                out.append(pltpu.make_async_remote_copy(
                    src_ref=blk, dst_ref=blk, send_sem=send.at[3 * t + j], recv_sem=recv.at[3 * t + j],
                    device_id=(x, y, 1 - c) if sending else (x, y, c), device_id_type=MESH))
        return out

    def start(ins, outs, sems):
        for cp in copies(outs, sems, True):
            cp.start()

    def finish(ins, outs, sems):
        for cp in copies(outs, sems, False):
            cp.wait_recv()
        for cp in copies(outs, sems, True):
            cp.wait_send()

    return _Comm(parts, [jax.ShapeDtypeStruct(p.shape, p.dtype) for p in parts], [3 * n, 3 * n], start, finish,
                 aliases={t: t for t in range(n)})


def _swap_halves(grads):
    n = len(grads)

    def copies(ins, outs, sems):
        x, y, c, _ = _place()
        out = []
        for t in range(n):
            hr = grads[t].shape[1] // 2
            rows = pl.ds(pl.multiple_of((1 - c) * hr, 8), hr)
            out.append(pltpu.make_async_remote_copy(
                src_ref=ins[t].at[:, rows, :], dst_ref=outs[t], send_sem=sems[0].at[t], recv_sem=sems[1].at[t],
                device_id=(x, y, 1 - c), device_id_type=MESH))
        return out

    def start(ins, outs, sems):
        for cp in copies(ins, outs, sems):
            cp.start()

    def finish(ins, outs, sems):
        for cp in copies(ins, outs, sems):
            cp.wait()

    return _Comm(grads, [jax.ShapeDtypeStruct((N_SHARD, g.shape[1] // 2, g.shape[2]), g.dtype) for g in grads],
                 [n, n], start, finish)


def _pair_sum(g, got, c, *, name):
    ns, R, C = g.shape
    hr = R // 2

    def body(c_ref, g_ref, r_ref, o_ref):
        o_ref[...] = (g_ref[...] + r_ref[...]).astype(BF16)

    return pl.pallas_call(
        body, name=name,
        grid_spec=pltpu.PrefetchScalarGridSpec(
            num_scalar_prefetch=1, grid=(ns,),
            in_specs=[pl.BlockSpec((None, hr, C), lambda s, cr: (s, cr[0], 0)),
                      pl.BlockSpec((None, hr, C), lambda s, cr: (s, 0, 0))],
            out_specs=pl.BlockSpec((None, hr, C), lambda s, cr: (s, 0, 0))),
        out_shape=jax.ShapeDtypeStruct((ns, hr, C), BF16),
        compiler_params=_params("arbitrary"),
    )(c, g, got)


def _scatter_chips(sums):
    n = len(sums)

    def copies(ins, outs, sems, sending):
        x, y, c, others = _place()
        me = 2 * x + y
        out = []
        for t in range(n):
            for j, (ox, oy) in enumerate(others):
                there = 2 * ox + oy
                out.append(pltpu.make_async_remote_copy(
                    src_ref=ins[t].at[there if sending else me], dst_ref=outs[t].at[me if sending else there],
                    send_sem=sems[0].at[3 * t + j], recv_sem=sems[1].at[3 * t + j],
                    device_id=(ox, oy, c) if sending else (x, y, c), device_id_type=MESH))
        return out

    def start(ins, outs, sems):
        for cp in copies(ins, outs, sems, True):
            cp.start()

    def finish(ins, outs, sems):
        for cp in copies(ins, outs, sems, False):
            cp.wait_recv()
        for cp in copies(ins, outs, sems, True):
            cp.wait_send()

    return _Comm(sums, [jax.ShapeDtypeStruct(s.shape, s.dtype) for s in sums], [3 * n, 3 * n], start, finish)


def _chip_sum(g, got, landed, idx, *, name):
    ns, R, C = g.shape
    hr = R // 2

    def body(i_ref, g_ref, r_ref, a_ref, b_ref, c_ref, o_ref):
        acc = g_ref[...] + r_ref[...]
        for ref in (a_ref, b_ref, c_ref):
            acc = acc + ref[...].astype(F32)
        o_ref[...] = acc

    other = lambda k: pl.BlockSpec((None, hr, C), lambda s, ir: (ir[2 + k], 0, 0))
    return pl.pallas_call(
        body, name=name,
        grid_spec=pltpu.PrefetchScalarGridSpec(
            num_scalar_prefetch=1, grid=(1,),
            in_specs=[pl.BlockSpec((None, hr, C), lambda s, ir: (ir[0], ir[1], 0)),
                      pl.BlockSpec((None, hr, C), lambda s, ir: (ir[0], 0, 0)),
                      other(0), other(1), other(2)],
            out_specs=pl.BlockSpec((hr, C), lambda s, ir: (ir[1], 0))),
        out_shape=jax.ShapeDtypeStruct((R, C), F32),
        compiler_params=_params("arbitrary"),
    )(idx, g, got, landed, landed, landed)


def _share_halves(halves):
    n = len(halves)

    def copies(outs, sems, sending):
        x, y, c, _ = _place()
        out = []
        for t in range(n):
            hr = halves[t].shape[0] // 2
            rows = pl.ds(pl.multiple_of((c if sending else 1 - c) * hr, 8), hr)
            out.append(pltpu.make_async_remote_copy(
                src_ref=outs[t].at[rows, :], dst_ref=outs[t].at[rows, :], send_sem=sems[0].at[t],
                recv_sem=sems[1].at[t], device_id=(x, y, 1 - c) if sending else (x, y, c), device_id_type=MESH))
        return out

    def start(ins, outs, sems):
        for cp in copies(outs, sems, True):
            cp.start()

    def finish(ins, outs, sems):
        for cp in copies(outs, sems, False):
            cp.wait_recv()
        for cp in copies(outs, sems, True):
            cp.wait_send()

    return _Comm(halves, [jax.ShapeDtypeStruct(h.shape, h.dtype) for h in halves], [n, n], start, finish,
                 aliases={t: t for t in range(n)})


def _adamw(w, g, m, v, *, name):
    R, C = w.shape
    tr = R
    for cand in (256, 128, 64, 32, 16, 8):
        if R % cand == 0:
            tr = cand
            break

    def body(w_ref, g_ref, m_ref, v_ref, go_ref, d_ref, mo_ref, vo_ref):
        gv = g_ref[...]
        go_ref[...] = gv
        mn = ADAM_B1 * m_ref[...] + (1.0 - ADAM_B1) * gv
        vn = ADAM_B2 * v_ref[...] + (1.0 - ADAM_B2) * (gv * gv)
        m_hat = mn / (1.0 - ADAM_B1 ** ADAM_STEP)
        v_hat = vn / (1.0 - ADAM_B2 ** ADAM_STEP)
        d_ref[...] = -ADAM_LR * (m_hat / (jnp.sqrt(v_hat) + ADAM_EPS) + ADAM_WD * w_ref[...])
        mo_ref[...] = mn
        vo_ref[...] = vn

    blk = pl.BlockSpec((tr, C), lambda i: (i, 0))
    return pl.pallas_call(
        body, name=name, grid=(R // tr,), in_specs=[blk] * 4, out_specs=[blk] * 4,
        out_shape=[jax.ShapeDtypeStruct((R, C), F32)] * 4,
        compiler_params=_params("arbitrary"),
    )(w, g, m, v)


def _pack_small(D, meta, n1, nm, n3, nf, gc, ga, bf, cw):
    def row(a):
        a = a.reshape(-1, a.shape[-1])
        return jnp.pad(a, ((0, 0), (0, D - a.shape[-1])))
    rows = [row(meta), row(n1), row(nm), row(n3), row(nf), row(jnp.concatenate([gc, ga], axis=-1)), row(bf), row(cw)]
    slab = jnp.concatenate(rows, axis=0)
    return jnp.pad(slab, ((0, SMALL_ROWS - slab.shape[0]), (0, 0)))


def _unpack_small(slab, like):
    meta, n1, nm, n3, nf, gc, ga, bf, cw = like
    nmeta, mc = meta.shape
    out = [slab[:nmeta, :mc].reshape(meta.shape)]
    r = nmeta
    for a in (n1, nm, n3, nf):
        out.append(slab[r, :a.shape[-1]].reshape(a.shape))
        r += 1
    cd = gc.shape[-1]
    out.append(slab[r, :cd].reshape(gc.shape))
    out.append(slab[r, cd:cd + ga.shape[-1]].reshape(ga.shape))
    r += 1
    out.append(slab[r, :bf.shape[-1]].reshape(bf.shape))
    r += 1
    out.append(slab[r:r + 3, :cw.shape[-1]].reshape(cw.shape))
    return out


def kernel(x, meta_tokens, ffn1_norm, ffn1_w_gu, ffn1_w_down, mix_norm, w_in, conv_w, b_f, out_norm_conv, out_norm_attn, w_out, ffn2_norm, ffn2_w_gu, ffn2_w_down, final_norm, loss_target, m_meta_tokens, m_ffn1_norm, m_ffn1_w_gu, m_ffn1_w_down, m_mix_norm, m_w_in, m_conv_w, m_b_f, m_out_norm_conv, m_out_norm_attn, m_w_out, m_ffn2_norm, m_ffn2_w_gu, m_ffn2_w_down, m_final_norm, v_meta_tokens, v_ffn1_norm, v_ffn1_w_gu, v_ffn1_w_down, v_mix_norm, v_w_in, v_conv_w, v_b_f, v_out_norm_conv, v_out_norm_attn, v_w_out, v_ffn2_norm, v_ffn2_w_gu, v_ffn2_w_down, v_final_norm):
    B, S, D = x.shape
    L = S + N_META
    T = B * L
    tm = L // 3
    assert tm * 3 == L and tm % HALO == 0
    guc = ffn1_w_gu.shape[-1]
    ff = N_SHARD * guc // 2
    H = b_f.shape[-1]
    AD = H * HEAD_DIM
    CD = conv_w.shape[-1] * N_SHARD
    assert CD == AD and CD + AD == D and CD % LANES == 0
    n_main = 3 * CD + 3 * AD
    ins = w_in.shape[-1]

    xi, yi, ci = lax.axis_index("x"), lax.axis_index("y"), lax.axis_index("c")
    chip = 2 * xi + yi

    small_shard = jnp.zeros((2 * HALO, meta_tokens.shape[-1]), F32)
    small_shard = small_shard.at[:N_META].set(meta_tokens)
    small_shard = small_shard.at[N_META:N_META + 3, :conv_w.shape[-1]].set(conv_w[0])
    big = [ffn1_w_gu[0], ffn1_w_down[0], w_in[0], w_out[0], ffn2_w_gu[0], ffn2_w_down[0]]
    wgu1_s, wd1_s, win_s, wout_s, wgu2_s, wd2_s = [w.astype(BF16) for w in big]
    small_g, = _all_gather_shards([small_shard], name="gather_small")
    meta_f = jnp.moveaxis(small_g[:, :N_META], 0, 1).reshape(N_META, D)
    cw_f = jnp.moveaxis(small_g[:, N_META:N_META + 3, :conv_w.shape[-1]], 0, 1).reshape(3, CD)
    cw8 = jnp.pad(cw_f, ((0, 5), (0, 0)))
    bf_p = jnp.pad(b_f, ((0, 0), (0, LANES - H)))
    gid = jnp.arange(CD) // HEAD_DIM
    pmat = jnp.where(gid[:, None] == gid[None, :], 1.0 / HEAD_DIM, 0.0).astype(BF16)

    gu_shape = jax.ShapeDtypeStruct((2, T, ff), BF16)
    gu_w_spec = pl.BlockSpec((None, D, guc), lambda s, i: (s, 0, 0))
    gu_o_spec = pl.BlockSpec((None, tm, guc), lambda s, i: (s // 2, i, s % 2))

    sid = ((chip + jnp.arange(N_SHARD, dtype=jnp.int32)) % N_SHARD).astype(jnp.int32)
    (h0, n1), wgu1_h = _embed_norm(x, meta_f, ffn1_norm, tm=tm, name="embed_norm", comm=_gather_ici([wgu1_s]))
    gu1, out = _ffn_up(n1, wgu1_s[None], sid, None, tm=tm, first=0, count=1, name="ffn1_up_own",
                       comm=_join(_gather_d2d(wgu1_h), _gather_ici([wd1_s])))
    wgu1, wd1_h = out[0], out[1:]
    gu1, out = _ffn_up(n1, wgu1, sid, gu1, tm=tm, first=1, count=N_SHARD - 1, name="ffn1_up_rest",
                       comm=_join(_gather_d2d(wd1_h), _gather_ici([win_s, wout_s])))
    wd1, mix_w = out[0].reshape(ff, D), out[1:]
    (h1, n2), (win_g, wout_g) = _ffn_down(gu1, wd1, h0, mix_norm, tm=tm, name="ffn1_down", comm=_gather_d2d(mix_w))
    wout_f = wout_g.reshape(D, D)
    win_f = jnp.moveaxis(win_g, 0, 1).reshape(D, N_SHARD * ins)
    win_main = win_f[:, :n_main]
    win_fg = jnp.pad(win_f[:, n_main:], ((0, 0), (0, LANES - H)))

    proj, _ = _matmul_nn(n2, win_main, tm=tm, nb=n_main // (3 * CD),
                         w_spec=pl.BlockSpec((D, 3 * CD), lambda s, i: (0, s)),
                         out_shape=jax.ShapeDtypeStruct((T, n_main), BF16),
                         out_spec=pl.BlockSpec((tm, 3 * CD), lambda s, i: (i, s)), name="mix_in")
    fg, _ = _matmul_nn(n2, win_fg, tm=tm, nb=1, w_spec=pl.BlockSpec((D, LANES), lambda s, i: (0, 0)),
                       out_shape=jax.ShapeDtypeStruct((T, LANES), F32),
                       out_spec=pl.BlockSpec((tm, LANES), lambda s, i: (i, 0)), name="mix_in_fg")
    proj3 = proj.reshape(B, L, n_main)
    fg3 = fg.reshape(B, L, LANES)
    fc = _fcum(fg3, bf_p, ch=tm, name="forget_cumsum")
    fr = fc[:, :, :H].reshape(B, L // tm, tm, H).transpose(0, 1, 3, 2)
    (o, lse), ffn2_w = _attn_fwd(proj3, fc, fr, tq=tm, n_heads=H, name="attn_fwd",
                                 comm=_gather_ici([wgu2_s, wd2_s]))
    (h2, ymix, n3), (wgu2, wd2) = _mix_out(
        proj3, o, cw8, out_norm_conv, out_norm_attn, wout_f, h1.reshape(B, L, D), pmat, ffn2_norm,
        tm=tm, name="mix_out", comm=_gather_d2d(ffn2_w))
    wd2 = wd2.reshape(ff, D)
    h2 = h2.reshape(T, D)
    n3 = n3.reshape(T, D)

    gu2, _ = _matmul_nn(n3, wgu2, tm=tm, nb=N_SHARD, w_spec=gu_w_spec, out_shape=gu_shape, out_spec=gu_o_spec,
                        name="ffn2_up")
    (dh3f, dh3b, d_gf, loss_part), _ = _ffn_down_loss(gu2, wd2, h2, final_norm.reshape(1, D), loss_target,
                                                      tm=tm, name="ffn2_down_loss")

    c_arr = jnp.reshape(ci, (1,)).astype(jnp.int32)
    ks = jnp.arange(N_SHARD - 1, dtype=jnp.int32)
    idx = jnp.concatenate([jnp.stack([chip, ci]).astype(jnp.int32), ks + (ks >= chip).astype(jnp.int32)])

    def pair_sums(grads, got, names):
        return [_pair_sum(g, r, c_arr, name="pair_sum_" + nm) for g, r, nm in zip(grads, got, names)]

    def chip_sums(grads, got, landed, names):
        return [_chip_sum(g, r, l, idx, name="chip_sum_" + nm) for g, r, l, nm in zip(grads, got, landed, names)]

    def dw_up(n, dgu, name, comm=None):
        return _matmul_tn(
            n, dgu, tm=tm, nb=N_SHARD, kb=D, x_spec=pl.BlockSpec((tm, D), lambda s, i: (i, 0)),
            y_spec=pl.BlockSpec((None, tm, guc), lambda s, i: (s // 2, i, s % 2)),
            out_shape=jax.ShapeDtypeStruct((N_SHARD, D, guc), F32),
            out_spec=pl.BlockSpec((None, D, guc), lambda s, i: (s, 0, 0)), name=name, comm=comm)

    (dgu2, d_wd2), _ = _ffn_bwd_act(dh3b, gu2, wd2, tm=tm, guc=guc, name="ffn2_bwd_act")
    (dh2, dh2b, d_g3), _ = _ffn_bwd_in(dgu2, wgu2, h2, ffn2_norm, dh3f, tm=tm, scale=1.0, name="ffn2_bwd_in")
    d_wgu2, _ = dw_up(n3, dgu2, "ffn2_dw_up")
    grads_f2 = [d_wgu2, d_wd2.reshape(N_SHARD, ff // N_SHARD, D)]
    names_f2 = ["wgu2", "wd2"]

    dh2b3 = dh2b.reshape(B, L, D)
    (d_bg, d_cv, d_o, d_gc, d_ga, d_cw), got_f2 = _mix_out_bwd(
        dh2b3, proj3, o, cw8, out_norm_conv, out_norm_attn, wout_f, pmat, tm=tm, name="mix_out_bwd",
        comm=_swap_halves(grads_f2))
    sums_f2 = pair_sums(grads_f2, got_f2, names_f2)
    d_wout, _ = _matmul_tn(
        ymix.reshape(T, D), dh2b, tm=tm, nb=1, kb=D,
        x_spec=pl.BlockSpec((tm, D), lambda s, i: (i, 0)), y_spec=pl.BlockSpec((tm, D), lambda s, i: (i, 0)),
        out_shape=jax.ShapeDtypeStruct((D, D), F32), out_spec=pl.BlockSpec((D, D), lambda s, i: (0, 0)),
        name="dw_out")
    d_cc = _conv_bwd(d_cv, proj3, cw8, tm=tm, name="conv_bwd")
    (d_q, d_k, d_v, d_fr, d_fq), landed_f2 = _attn_bwd(proj3, o, d_o, lse, fc, fr, tq=tm, n_heads=H, name="attn_bwd",
                                                       comm=_scatter_chips(sums_f2))
    halves_f2 = chip_sums(grads_f2, got_f2, landed_f2, names_f2)
    d_fc = d_fq + jnp.pad(d_fr.transpose(0, 1, 3, 2).reshape(B, L, H), ((0, 0), (0, 0), (0, LANES - H)))
    d_fg, d_bf = _fcum_bwd(d_fc, fg3, bf_p, ch=tm, name="forget_cumsum_bwd")

    parts = [d_bg.reshape(T, CD), d_cc.reshape(T, 2 * CD), d_q.reshape(T, AD), d_k.reshape(T, AD),
             d_v.reshape(T, AD), d_fg.reshape(T, LANES)]
    (dh1, dh1b, d_gm, d_proj), g_f2 = _mix_bwd_in(parts, win_main, win_fg, h1, mix_norm, dh2, tm=tm, scale=0.5,
                                                  name="mix_bwd_in", comm=_share_halves(halves_f2))
    wide = d_proj.shape[1]
    bw = next(c for c in (768, 640, 512, 384, 256, LANES) if wide % c == 0)
    d_win_nat, _ = _matmul_tn(
        n2, d_proj, tm=tm, nb=wide // bw, kb=D,
        x_spec=pl.BlockSpec((tm, D), lambda s, i: (i, 0)), y_spec=pl.BlockSpec((tm, bw), lambda s, i: (i, s)),
        out_shape=jax.ShapeDtypeStruct((D, wide), F32), out_spec=pl.BlockSpec((D, bw), lambda s, i: (0, s)),
        name="dw_in")
    d_win = jnp.moveaxis(d_win_nat[:, :N_SHARD * ins].reshape(D, N_SHARD, ins), 1, 0)
    grads_mx = [d_win, d_wout.reshape(N_SHARD, D // N_SHARD, D)]
    names_mx = ["win", "wout"]

    (dgu1, d_wd1), got_mx = _ffn_bwd_act(dh1b, gu1, wd1, tm=tm, guc=guc, name="ffn1_bwd_act",
                                         comm=_swap_halves(grads_mx))
    sums_mx = pair_sums(grads_mx, got_mx, names_mx)
    grads_d1 = [d_wd1.reshape(N_SHARD, ff // N_SHARD, D)]
    d_wgu1, out = dw_up(n1, dgu1, "ffn1_dw_up", comm=_join(_scatter_chips(sums_mx), _swap_halves(grads_d1)))
    landed_mx, got_d1 = out[:2], out[2:]
    halves_mx = chip_sums(grads_mx, got_mx, landed_mx, names_mx)
    sums_d1 = pair_sums(grads_d1, got_d1, ["wd1"])
    grads_u1 = [d_wgu1]
    (grad_x, d_meta, d_g1), out = _ffn_bwd_in_first(
        dgu1, wgu1, h0, ffn1_norm, dh1, tm=tm, batch=B, name="ffn1_bwd_in",
        comm=_join(_join(_share_halves(halves_mx), _scatter_chips(sums_d1)), _swap_halves(grads_u1)))
    g_mx, landed_d1, got_u1 = out[:2], out[2:3], out[3:]
    halves_d1 = chip_sums(grads_d1, got_d1, landed_d1, ["wd1"])
    sums_u1 = pair_sums(grads_u1, got_u1, ["wgu1"])
    out = _run_comm(_join(_share_halves(halves_d1), _scatter_chips(sums_u1)), name="scatter_ffn1")
    g_d1, landed_u1 = out[:1], out[1:]
    halves_u1 = chip_sums(grads_u1, got_u1, landed_u1, ["wgu1"])
    g_u1 = _run_comm(_share_halves(halves_u1), name="share_ffn1")
    g_big = [g_u1[0], g_d1[0], g_mx[0], g_mx[1], g_f2[0], g_f2[1]]

    loss_row = jnp.zeros((1, D), F32).at[0, 0].set(loss_part[0, 0])
    slab = _pack_small(D, d_meta, d_g1, d_gm, d_g3, d_gf, d_gc, d_ga, d_bf[:, :H], d_cw[:3])
    slab = slab.at[SMALL_ROWS - 1].set(loss_row[0])
    total = _all_reduce_small(slab, name="reduce_small")
    loss = total[SMALL_ROWS - 1, 0]
    mcols = meta_tokens.shape[-1]
    ccols = conv_w.shape[-1]
    full_like = (jnp.zeros((N_META, D)), ffn1_norm, mix_norm, ffn2_norm, final_norm.reshape(1, D), out_norm_conv,
                 out_norm_attn, b_f, jnp.zeros((1, 3, CD)))
    g_small = _unpack_small(total, full_like)
    g_small[0] = lax.dynamic_slice_in_dim(g_small[0], chip * mcols, mcols, axis=1)
    g_small[8] = lax.dynamic_slice_in_dim(g_small[8], chip * ccols, ccols, axis=2)

    def small_slab(meta, a1, am, a3, af, gc, ga, bf, cw):
        return _pack_small(D, meta, a1, am, a3, af.reshape(1, D), gc, ga, bf, cw[0])

    w_small = small_slab(meta_tokens, ffn1_norm, mix_norm, ffn2_norm, final_norm, out_norm_conv, out_norm_attn, b_f, conv_w)
    m_small = small_slab(m_meta_tokens, m_ffn1_norm, m_mix_norm, m_ffn2_norm, m_final_norm, m_out_norm_conv,
                         m_out_norm_attn, m_b_f, m_conv_w)
    v_small = small_slab(v_meta_tokens, v_ffn1_norm, v_mix_norm, v_ffn2_norm, v_final_norm, v_out_norm_conv,
                         v_out_norm_attn, v_b_f, v_conv_w)
    gs = list(g_small)
    gs[4] = gs[4].reshape(final_norm.shape)
    g_slab = small_slab(gs[0], gs[1], gs[2], gs[3], gs[4], gs[5], gs[6], gs[7], gs[8])
    local_like = (meta_tokens, ffn1_norm, mix_norm, ffn2_norm, final_norm.reshape(1, D), out_norm_conv, out_norm_attn,
                  b_f, conv_w)
    small_out = [_unpack_small(s, local_like)
                 for s in _adamw(w_small, g_slab, m_small, v_small, name="adamw_small")[1:]]
    for lst in small_out:
        lst[4] = lst[4].reshape(final_norm.shape)

    names = ["wgu1", "wd1", "win", "wout", "wgu2", "wd2"]
    w_big = big
    m_big = [m_ffn1_w_gu[0], m_ffn1_w_down[0], m_w_in[0], m_w_out[0], m_ffn2_w_gu[0], m_ffn2_w_down[0]]
    v_big = [v_ffn1_w_gu[0], v_ffn1_w_down[0], v_w_in[0], v_w_out[0], v_ffn2_w_gu[0], v_ffn2_w_down[0]]
    big_out = [_adamw(w, g, m, v, name="adamw_" + nm) for w, g, m, v, nm in zip(w_big, g_big, m_big, v_big, names)]

    def assemble(small, bigs):
        meta, a1, am, a3, af, gc, ga, bf, cw = small
        gu1_, d1_, win_, wout_, gu2_, d2_ = [b[None] for b in bigs]
        return [meta, a1, gu1_, d1_, am, win_, cw, bf, gc, ga, wout_, a3, gu2_, d2_, af]

    gs_out = list(g_small)
    gs_out[4] = gs_out[4].reshape(final_norm.shape)
    grads_out = assemble(gs_out, [b[0] for b in big_out])
    delta_out = assemble(small_out[0], [b[1] for b in big_out])
    m_out = assemble(small_out[1], [b[2] for b in big_out])
    v_out = assemble(small_out[2], [b[3] for b in big_out])
    return (loss, grad_x, *grads_out, *delta_out, *m_out, *v_out)
```

```python
import functools

import jax
import jax.numpy as jnp
from jax import lax
from jax.experimental import pallas as pl
from jax.experimental.pallas import tpu as pltpu

F32 = jnp.float32
BF16 = jnp.bfloat16

EPS = 1e-6
N_META = 16
HEAD_DIM = 64
N_SHARD = 4
N_DEV = 8
HALO = 16
LANES = 128
SMALL_ROWS = 32
VMEM_LIMIT_V7X = 56 * 1024 * 1024
NEG = -1e30
ATTN_BANDS = 2

ADAM_LR = 0.001
ADAM_B1 = 0.9
ADAM_B2 = 0.999
ADAM_EPS = 1e-08
ADAM_WD = 0.01
ADAM_STEP = 10

MESH = pl.DeviceIdType.MESH
ANY = pl.BlockSpec(memory_space=pl.ANY)
NT_DIMS = (((1,), (1,)), ((), ()))
TN_DIMS = (((0,), (0,)), ((), ()))


def _params(*sem):
    return pltpu.CompilerParams(dimension_semantics=sem, vmem_limit_bytes=VMEM_LIMIT_V7X)


class _Comm:
    def __init__(self, ins, out_shapes, sems, start, finish, aliases=None):
        self.ins, self.out_shapes, self.sems = list(ins), list(out_shapes), list(sems)
        self.start, self.finish, self.aliases = start, finish, dict(aliases or {})


def _join(a, b):
    ni, no, ns = len(a.ins), len(a.out_shapes), len(a.sems)

    def start(ins, outs, sems):
        a.start(ins[:ni], outs[:no], sems[:ns])
        b.start(ins[ni:], outs[no:], sems[ns:])

    def finish(ins, outs, sems):
        a.finish(ins[:ni], outs[:no], sems[:ns])
        b.finish(ins[ni:], outs[no:], sems[ns:])

    aliases = dict(a.aliases)
    aliases.update({ni + i: no + j for i, j in b.aliases.items()})
    return _Comm(a.ins + b.ins, a.out_shapes + b.out_shapes, a.sems + b.sems, start, finish, aliases)


def _launch(body, *, name, grid, in_specs, out_specs, out_shape, args, scratch_shapes=(), comm=None, prefetch=(),
            aliases=None):
    single = not isinstance(out_shape, (list, tuple))
    out_specs = [out_specs] if single else list(out_specs)
    out_shape = [out_shape] if single else list(out_shape)
    in_specs, scratch_shapes, prefetch = list(in_specs), list(scratch_shapes), list(prefetch)
    params = _params(*(("arbitrary",) * len(grid)))
    n_pf, n_in, n_out, n_scr = len(prefetch), len(in_specs), len(out_specs), len(scratch_shapes)
    c_ins = comm.ins if comm else []
    c_shapes = comm.out_shapes if comm else []
    c_sems = comm.sems if comm else []
    c_in, c_out = len(c_ins), len(c_shapes)

    def carrier(*refs):
        p = 0
        pf = refs[p:p + n_pf]; p += n_pf
        a = refs[p:p + n_in]; p += n_in
        ci = refs[p:p + c_in]; p += c_in
        o = refs[p:p + n_out]; p += n_out
        co = refs[p:p + c_out]; p += c_out
        s = refs[p:p + n_scr]; p += n_scr
        cs = refs[p:]
        if comm:
            first = functools.reduce(lambda u, v: u & v, [pl.program_id(k) == 0 for k in range(len(grid))])

            @pl.when(first)
            def _():
                comm.start(ci, co, cs)

        body(*pf, *a, *o, *s)

        if comm:
            last = functools.reduce(lambda u, v: u & v, [pl.program_id(k) == grid[k] - 1 for k in range(len(grid))])

            @pl.when(last)
            def _():
                comm.finish(ci, co, cs)

    io_aliases = {n_pf + i: j for i, j in (aliases or {}).items()}
    if comm:
        io_aliases.update({n_pf + n_in + i: n_out + j for i, j in comm.aliases.items()})
    all_in, all_out = in_specs + [ANY] * c_in, out_specs + [ANY] * c_out
    all_scratch = scratch_shapes + [pltpu.SemaphoreType.DMA((k,)) for k in c_sems]
    if n_pf:
        spec = dict(grid_spec=pltpu.PrefetchScalarGridSpec(
            num_scalar_prefetch=n_pf, grid=grid, in_specs=all_in, out_specs=all_out, scratch_shapes=all_scratch))
    else:
        spec = dict(grid=grid, in_specs=all_in, out_specs=all_out, scratch_shapes=all_scratch)
    res = pl.pallas_call(carrier, name=name, out_shape=out_shape + c_shapes, input_output_aliases=io_aliases,
                         compiler_params=params, **spec)(*prefetch, *args, *c_ins)
    main = list(res[:n_out])
    return (main[0] if single else main), (list(res[n_out:]) if comm else None)


def _run_comm(comm, *, name):
    c_in, c_out = len(comm.ins), len(comm.out_shapes)

    def body(*refs):
        ci, co, cs = refs[:c_in], refs[c_in:c_in + c_out], refs[c_in + c_out:]
        comm.start(ci, co, cs)
        comm.finish(ci, co, cs)

    return list(pl.pallas_call(
        body, name=name, in_specs=[ANY] * c_in, out_specs=[ANY] * c_out, out_shape=comm.out_shapes,
        scratch_shapes=[pltpu.SemaphoreType.DMA((k,)) for k in comm.sems],
        input_output_aliases=comm.aliases)(*comm.ins))


def _chunks(width, step=512):
    out, c0 = [], 0
    while c0 < width:
        cw = min(step, width - c0)
        out.append((c0, cw))
        c0 += cw
    return out


def _split2(v):
    hi = v.astype(BF16)
    lo = (v - hi.astype(F32)).astype(BF16)
    return hi, lo


def _split3(v):
    hi = v.astype(BF16)
    r = v - hi.astype(F32)
    mid = r.astype(BF16)
    lo = (r - mid.astype(F32)).astype(BF16)
    return hi, mid, lo


def _dot(a, b):
    return jnp.dot(a, b, preferred_element_type=F32)


def _dot_nt(a, b):
    return lax.dot_general(a, b, NT_DIMS, preferred_element_type=F32)


def _dot_tn(a, b):
    return lax.dot_general(a, b, TN_DIMS, preferred_element_type=F32)


def _silu_mul(g, u):
    return g * jax.nn.sigmoid(g) * u


def _rms_bwd(dn, h, gain, dres):
    r = lax.rsqrt(jnp.mean(h * h, axis=-1, keepdims=True) + EPS)
    y = h * r
    dgain = jnp.sum(dn * y, axis=0, keepdims=True)
    dy = dn * gain
    dh = dres + r * (dy - y * jnp.mean(dy * y, axis=-1, keepdims=True))
    return dh, dgain


def _group_mean(v, p):
    hi, lo = _split2(v)
    return _dot(hi, p) + _dot(lo, p)


def _row_of(a, k):
    rows = lax.broadcasted_iota(jnp.int32, a.shape, 0)
    return jnp.sum(jnp.where(rows == k, a, 0.0), axis=0, keepdims=True)


def _causal_conv(u, prev, w):
    rows = lax.broadcasted_iota(jnp.int32, u.shape, 0)
    p1 = _row_of(prev, HALO - 1)
    p2 = _row_of(prev, HALO - 2)
    u1 = jnp.where(rows == 0, p1, pltpu.roll(u, 1, 0))
    u2 = jnp.where(rows == 0, p2, jnp.where(rows == 1, p1, pltpu.roll(u, 2, 0)))
    return w[2:3, :] * u + w[1:2, :] * u1 + w[0:1, :] * u2, u1, u2


def _rms(x, gain):
    return (x * lax.rsqrt(jnp.mean(x * x, axis=-1, keepdims=True) + EPS) * gain).astype(BF16)


def _embed_norm(x, meta, g, *, tm, name, comm=None):
    B, S, D = x.shape
    L = S + N_META
    per_seq = L // tm
    nt = B * per_seq
    body_rows = tm - N_META

    def body(meta_ref, g_ref, x_hbm, h_ref, n_ref, buf, sems):
        i = pl.program_id(0)

        def fetch(k, fn):
            slot, b, t = k % 2, k // per_seq, k % per_seq

            @pl.when(t == 0)
            def _():
                fn(pltpu.make_async_copy(x_hbm.at[b, pl.ds(0, body_rows)],
                                         buf.at[slot, pl.ds(N_META, body_rows)], sems.at[slot]))

            @pl.when(t != 0)
            def _():
                fn(pltpu.make_async_copy(x_hbm.at[b, pl.ds(pl.multiple_of(t * tm - N_META, 8), tm)],
                                         buf.at[slot], sems.at[slot]))

        @pl.when(i == 0)
        def _():
            fetch(i, lambda cp: cp.start())

        @pl.when(i + 1 < nt)
        def _():
            fetch(i + 1, lambda cp: cp.start())

        fetch(i, lambda cp: cp.wait())
        slot = i % 2

        @pl.when(i % per_seq == 0)
        def _():
            buf[slot, 0:N_META, :] = meta_ref[...]

        hv = buf[slot]
        h_ref[...] = hv
        n_ref[...] = _rms(hv, g_ref[...])

    row = pl.BlockSpec((tm, D), lambda i: (i, 0))
    return _launch(
        body, name=name, grid=(nt,),
        in_specs=[pl.BlockSpec((N_META, D), lambda i: (0, 0)), pl.BlockSpec((1, D), lambda i: (0, 0)), ANY],
        out_specs=[row, row],
        out_shape=[jax.ShapeDtypeStruct((B * L, D), F32), jax.ShapeDtypeStruct((B * L, D), BF16)],
        scratch_shapes=[pltpu.VMEM((2, tm, D), F32), pltpu.SemaphoreType.DMA((2,))],
        args=(meta, g, x), comm=comm)


def _ffn_up(n, wgu, sid, gu_prev, *, tm, first, count, name, comm=None):
    T, D = n.shape
    ns, _, guc = wgu.shape
    ff = N_SHARD * guc // 2

    def body(sid_ref, x_ref, w_ref, *rest):
        rest[-1][...] = _dot(x_ref[...], w_ref[...]).astype(BF16)

    where = lambda s, sid: sid[first + s]
    w_at = (lambda s, sid: 0) if ns == 1 else where
    return _launch(
        body, name=name, grid=(count, T // tm), prefetch=(sid,),
        in_specs=[pl.BlockSpec((tm, D), lambda s, i, sid: (i, 0)),
                  pl.BlockSpec((None, D, guc), lambda s, i, sid: (w_at(s, sid), 0, 0))]
                 + ([] if gu_prev is None else [ANY]),
        out_specs=pl.BlockSpec((None, tm, guc), lambda s, i, sid: (where(s, sid) // 2, i, where(s, sid) % 2)),
        out_shape=jax.ShapeDtypeStruct((2, T, ff), BF16),
        args=(n, wgu) + (() if gu_prev is None else (gu_prev,)),
        aliases=None if gu_prev is None else {2: 0}, comm=comm)


def _matmul_nn(x, w, *, tm, nb, w_spec, out_shape, out_spec, name, comm=None):
    T, K = x.shape

    def body(x_ref, w_ref, o_ref):
        o_ref[...] = _dot(x_ref[...], w_ref[...]).astype(o_ref.dtype)

    return _launch(
        body, name=name, grid=(nb, T // tm),
        in_specs=[pl.BlockSpec((tm, K), lambda s, i: (i, 0)), w_spec],
        out_specs=out_spec, out_shape=out_shape, args=(x, w), comm=comm)


def _ffn_down(gu, wd, h, next_gain, *, tm, name, comm=None):
    _, T, ff = gu.shape
    D = h.shape[1]
    chunks = _chunks(ff)

    def body(g_ref, u_ref, wd_hbm, h_ref, ng_ref, o_ref, n_ref, wd_v, sem):
        @pl.when(pl.program_id(0) == 0)
        def _():
            cp = pltpu.make_async_copy(wd_hbm, wd_v, sem)
            cp.start()
            cp.wait()

        acc = jnp.zeros((tm, D), F32)
        for c0, cw in chunks:
            a = _silu_mul(g_ref[:, c0:c0 + cw].astype(F32), u_ref[:, c0:c0 + cw].astype(F32))
            acc = acc + _dot(a.astype(BF16), wd_v[c0:c0 + cw, :])
        out = h_ref[...] + 0.5 * acc
        o_ref[...] = out
        n_ref[...] = _rms(out, ng_ref[...])

    return _launch(
        body, name=name, grid=(T // tm,),
        in_specs=[pl.BlockSpec((None, tm, ff), lambda i: (0, i, 0)),
                  pl.BlockSpec((None, tm, ff), lambda i: (1, i, 0)),
                  ANY,
                  pl.BlockSpec((tm, D), lambda i: (i, 0)),
                  pl.BlockSpec((1, D), lambda i: (0, 0))],
        out_specs=[pl.BlockSpec((tm, D), lambda i: (i, 0)), pl.BlockSpec((tm, D), lambda i: (i, 0))],
        out_shape=[jax.ShapeDtypeStruct((T, D), F32), jax.ShapeDtypeStruct((T, D), BF16)],
        scratch_shapes=[pltpu.VMEM((ff, D), BF16), pltpu.SemaphoreType.DMA],
        args=(gu, gu, wd, h, next_gain), comm=comm)


def _ffn_down_loss(gu, wd, h, gf, tgt, *, tm, name, comm=None):
    _, T, ff = gu.shape
    D = h.shape[1]
    B, S, _ = tgt.shape
    per_seq = (S + N_META) // tm
    body_rows = tm - N_META
    chunks = _chunks(ff)

    def body(g_ref, u_ref, wd_hbm, h_ref, gf_ref, tgt_hbm, dh_ref, dhb_ref, dg_ref, loss_ref, wd_v, tg_v, sem, tsem):
        i = pl.program_id(0)
        b, t = i // per_seq, i % per_seq

        @pl.when(i == 0)
        def _():
            cp = pltpu.make_async_copy(wd_hbm, wd_v, sem)
            cp.start()
            cp.wait()
            dg_ref[...] = jnp.zeros_like(dg_ref)
            loss_ref[...] = jnp.zeros_like(loss_ref)
            tg_v[0:N_META, :] = jnp.zeros((N_META, D), F32)

        def fetch(fn):
            @pl.when(t == 0)
            def _():
                fn(pltpu.make_async_copy(tgt_hbm.at[b, pl.ds(0, body_rows)], tg_v.at[pl.ds(N_META, body_rows)], tsem))

            @pl.when(t != 0)
            def _():
                fn(pltpu.make_async_copy(tgt_hbm.at[b, pl.ds(pl.multiple_of(t * tm - N_META, 8), tm)], tg_v, tsem))

        fetch(lambda cp: cp.start())
        acc = jnp.zeros((tm, D), F32)
        for c0, cw in chunks:
            a = _silu_mul(g_ref[:, c0:c0 + cw].astype(F32), u_ref[:, c0:c0 + cw].astype(F32))
            acc = acc + _dot(a.astype(BF16), wd_v[c0:c0 + cw, :])
        x = h_ref[...] + 0.5 * acc
        fetch(lambda cp: cp.wait())

        gain = gf_ref[...]
        r = lax.rsqrt(jnp.mean(x * x, axis=-1, keepdims=True) + EPS)
        y = x * r
        pos = t * tm + lax.broadcasted_iota(jnp.int32, (tm, 1), 0)
        err = jnp.where(pos >= N_META, y * gain - tg_v[...], 0.0)
        loss_ref[...] += 0.5 * jnp.sum(jnp.mean(err * err, axis=-1, keepdims=True))
        dout = err / D
        dg_ref[...] += jnp.sum(dout * y, axis=0, keepdims=True)
        dy = dout * gain
        dh = r * (dy - y * jnp.mean(dy * y, axis=-1, keepdims=True))
        dh_ref[...] = dh
        dhb_ref[...] = (0.5 * dh).astype(BF16)

    row = pl.BlockSpec((tm, D), lambda i: (i, 0))
    const = lambda i: (0, 0)
    return _launch(
        body, name=name, grid=(T // tm,),
        in_specs=[pl.BlockSpec((None, tm, ff), lambda i: (0, i, 0)),
                  pl.BlockSpec((None, tm, ff), lambda i: (1, i, 0)),
                  ANY, row, pl.BlockSpec((1, D), const), ANY],
        out_specs=[row, row, pl.BlockSpec((1, D), const), pl.BlockSpec((1, LANES), const)],
        out_shape=[jax.ShapeDtypeStruct((T, D), F32), jax.ShapeDtypeStruct((T, D), BF16),
                   jax.ShapeDtypeStruct((1, D), F32), jax.ShapeDtypeStruct((1, LANES), F32)],
        scratch_shapes=[pltpu.VMEM((ff, D), BF16), pltpu.VMEM((tm, D), F32), pltpu.SemaphoreType.DMA,
                        pltpu.SemaphoreType.DMA],
        args=(gu, gu, wd, h, gf, tgt), comm=comm)


def _ffn_bwd_act(df, gu, wd, *, tm, guc, name, comm=None):
    _, T, ff = gu.shape
    D = df.shape[1]
    nj = ff // guc
    chunks = _chunks(guc)

    def body(df_ref, g_ref, u_ref, wd_ref, o_ref, dwd_ref):
        @pl.when(pl.program_id(1) == 0)
        def _():
            dwd_ref[...] = jnp.zeros_like(dwd_ref)

        dfv = df_ref[...]
        for c0, cw in chunks:
            da = _dot_nt(dfv, wd_ref[c0:c0 + cw, :])
            g = g_ref[:, c0:c0 + cw].astype(F32)
            u = u_ref[:, c0:c0 + cw].astype(F32)
            sg = jax.nn.sigmoid(g)
            silu = g * sg
            o_ref[0, :, c0:c0 + cw] = (da * u * (sg * (1.0 + g * (1.0 - sg)))).astype(BF16)
            o_ref[1, :, c0:c0 + cw] = (da * silu).astype(BF16)
            dwd_ref[c0:c0 + cw, :] += _dot_tn((silu * u).astype(BF16), dfv)

    return _launch(
        body, name=name, grid=(nj, T // tm),
        in_specs=[pl.BlockSpec((tm, D), lambda j, i: (i, 0)),
                  pl.BlockSpec((None, tm, guc), lambda j, i: (0, i, j)),
                  pl.BlockSpec((None, tm, guc), lambda j, i: (1, i, j)),
                  pl.BlockSpec((guc, D), lambda j, i: (j, 0))],
        out_specs=[pl.BlockSpec((2, tm, guc), lambda j, i: (0, i, j)), pl.BlockSpec((guc, D), lambda j, i: (j, 0))],
        out_shape=[jax.ShapeDtypeStruct((2, T, ff), BF16), jax.ShapeDtypeStruct((ff, D), F32)],
        args=(df, gu, gu, wd), comm=comm)


def _ffn_bwd_in(dgu, wgu, h, g, dres, *, tm, scale, name, comm=None):
    _, T, ff = dgu.shape
    ns, D, guc = wgu.shape
    nj = ff // guc
    chunks = _chunks(guc)

    def body(dgu_ref, w_hbm, h_ref, g_ref, dres_ref, dh_ref, dhb_ref, dg_ref, w_v, acc, sem):
        i, j = pl.program_id(0), pl.program_id(1)

        @pl.when((i == 0) & (j == 0))
        def _():
            cp = pltpu.make_async_copy(w_hbm, w_v, sem)
            cp.start()
            cp.wait()
            dg_ref[...] = jnp.zeros_like(dg_ref)

        part = jnp.zeros((tm, D), F32)
        for c0, cw in chunks:
            part = part + _dot_nt(dgu_ref[0, :, c0:c0 + cw], w_v[j, :, c0:c0 + cw])
            part = part + _dot_nt(dgu_ref[1, :, c0:c0 + cw], w_v[nj + j, :, c0:c0 + cw])

        @pl.when(j == 0)
        def _():
            acc[...] = part

        @pl.when(j > 0)
        def _():
            acc[...] += part

        @pl.when(j == nj - 1)
        def _():
            dh, dgain = _rms_bwd(acc[...], h_ref[...], g_ref[...], dres_ref[...])
            dh_ref[...] = dh
            dhb_ref[...] = (scale * dh).astype(BF16)
            dg_ref[...] += dgain

    return _launch(
        body, name=name, grid=(T // tm, nj),
        in_specs=[pl.BlockSpec((2, tm, guc), lambda i, j: (0, i, j)),
                  ANY,
                  pl.BlockSpec((tm, D), lambda i, j: (i, 0)),
                  pl.BlockSpec((1, D), lambda i, j: (0, 0)),
                  pl.BlockSpec((tm, D), lambda i, j: (i, 0))],
        out_specs=[pl.BlockSpec((tm, D), lambda i, j: (i, 0)),
                   pl.BlockSpec((tm, D), lambda i, j: (i, 0)),
                   pl.BlockSpec((1, D), lambda i, j: (0, 0))],
        out_shape=[jax.ShapeDtypeStruct((T, D), F32), jax.ShapeDtypeStruct((T, D), BF16),
                   jax.ShapeDtypeStruct((1, D), F32)],
        scratch_shapes=[pltpu.VMEM((ns, D, guc), BF16), pltpu.VMEM((tm, D), F32), pltpu.SemaphoreType.DMA],
        args=(dgu, wgu, h, g, dres), comm=comm)


def _ffn_bwd_in_first(dgu, wgu, h, g, dres, *, tm, batch, name, comm=None):
    _, T, ff = dgu.shape
    ns, D, guc = wgu.shape
    nj = ff // guc
    nt = T // tm
    L = T // batch
    per_seq = L // tm
    body_rows = tm - N_META
    chunks = _chunks(guc)

    def body(dgu_ref, w_hbm, h_ref, g_ref, dres_ref, dx_hbm, dmeta_ref, dg_ref, w_v, acc, dh_v, sem, osem):
        i, j = pl.program_id(0), pl.program_id(1)

        @pl.when((i == 0) & (j == 0))
        def _():
            cp = pltpu.make_async_copy(w_hbm, w_v, sem)
            cp.start()
            cp.wait()
            dg_ref[...] = jnp.zeros_like(dg_ref)
            dmeta_ref[...] = jnp.zeros_like(dmeta_ref)

        part = jnp.zeros((tm, D), F32)
        for c0, cw in chunks:
            part = part + _dot_nt(dgu_ref[0, :, c0:c0 + cw], w_v[j, :, c0:c0 + cw])
            part = part + _dot_nt(dgu_ref[1, :, c0:c0 + cw], w_v[nj + j, :, c0:c0 + cw])

        @pl.when(j == 0)
        def _():
            acc[...] = part

        @pl.when(j > 0)
        def _():
            acc[...] += part

        def head_copy(b):
            return pltpu.make_async_copy(dh_v.at[pl.ds(N_META, body_rows)], dx_hbm.at[b, pl.ds(0, body_rows)], osem)

        def tail_copy(b, t):
            return pltpu.make_async_copy(dh_v, dx_hbm.at[b, pl.ds(pl.multiple_of(t * tm - N_META, 8), tm)], osem)

        def on_tile(k, head_fn, tail_fn):
            @pl.when(k % per_seq == 0)
            def _():
                head_fn(head_copy(k // per_seq))

            @pl.when(k % per_seq != 0)
            def _():
                tail_fn(tail_copy(k // per_seq, k % per_seq))

        @pl.when(j == nj - 1)
        def _():
            dh, dgain = _rms_bwd(acc[...], h_ref[...], g_ref[...], dres_ref[...])
            dg_ref[...] += dgain

            @pl.when(i > 0)
            def _():
                on_tile(i - 1, lambda cp: cp.wait(), lambda cp: cp.wait())

            dh_v[...] = dh

            @pl.when(i % per_seq == 0)
            def _():
                dmeta_ref[...] += dh[0:N_META, :]

            on_tile(i, lambda cp: cp.start(), lambda cp: cp.start())

            @pl.when(i == nt - 1)
            def _():
                on_tile(i, lambda cp: cp.wait(), lambda cp: cp.wait())

    return _launch(
        body, name=name, grid=(nt, nj),
        in_specs=[pl.BlockSpec((2, tm, guc), lambda i, j: (0, i, j)),
                  ANY,
                  pl.BlockSpec((tm, D), lambda i, j: (i, 0)),
                  pl.BlockSpec((1, D), lambda i, j: (0, 0)),
                  pl.BlockSpec((tm, D), lambda i, j: (i, 0))],
        out_specs=[ANY, pl.BlockSpec((N_META, D), lambda i, j: (0, 0)), pl.BlockSpec((1, D), lambda i, j: (0, 0))],
        out_shape=[jax.ShapeDtypeStruct((batch, L - N_META, D), F32), jax.ShapeDtypeStruct((N_META, D), F32),
                   jax.ShapeDtypeStruct((1, D), F32)],
        scratch_shapes=[pltpu.VMEM((ns, D, guc), BF16), pltpu.VMEM((tm, D), F32), pltpu.VMEM((tm, D), F32),
                        pltpu.SemaphoreType.DMA, pltpu.SemaphoreType.DMA],
        args=(dgu, wgu, h, g, dres), comm=comm)


def _mix_bwd_in(parts, w_main, w_fg, h, g, dres, *, tm, scale, name, comm=None):
    T, D = h.shape
    widths = [p.shape[1] for p in parts]
    offs = [sum(widths[:k]) for k in range(len(widths))]
    npart = len(parts)
    wide = sum(widths)

    def body(*refs):
        p_refs = refs[:npart]
        wm_ref, wf_ref, h_ref, g_ref, dres_ref, dh_ref, dhb_ref, dg_ref, all_ref = refs[npart:]

        @pl.when(pl.program_id(0) == 0)
        def _():
            dg_ref[...] = jnp.zeros_like(dg_ref)

        dn = jnp.zeros((tm, D), F32)
        for k, (p_ref, off, wd_) in enumerate(zip(p_refs, offs, widths)):
            for c0, cw in _chunks(wd_):
                piece = p_ref[:, c0:c0 + cw].astype(BF16)
                all_ref[:, off + c0:off + c0 + cw] = piece
                w = wf_ref[...] if k == npart - 1 else wm_ref[:, off + c0:off + c0 + cw]
                dn = dn + _dot_nt(piece, w)
        dh, dgain = _rms_bwd(dn, h_ref[...], g_ref[...], dres_ref[...])
        dh_ref[...] = dh
        dhb_ref[...] = (scale * dh).astype(BF16)
        dg_ref[...] += dgain

    row = lambda i: (i, 0)
    const = lambda i: (0, 0)
    return _launch(
        body, name=name, grid=(T // tm,),
        in_specs=[pl.BlockSpec((tm, p.shape[1]), row) for p in parts]
                 + [pl.BlockSpec(w_main.shape, const), pl.BlockSpec(w_fg.shape, const),
                    pl.BlockSpec((tm, D), row), pl.BlockSpec((1, D), const), pl.BlockSpec((tm, D), row)],
        out_specs=[pl.BlockSpec((tm, D), row), pl.BlockSpec((tm, D), row), pl.BlockSpec((1, D), const),
                   pl.BlockSpec((tm, wide), row)],
        out_shape=[jax.ShapeDtypeStruct((T, D), F32), jax.ShapeDtypeStruct((T, D), BF16),
                   jax.ShapeDtypeStruct((1, D), F32), jax.ShapeDtypeStruct((T, wide), BF16)],
        args=(*parts, w_main, w_fg, h, g, dres), comm=comm)


def _matmul_tn(x, y, *, tm, nb, x_spec, y_spec, out_shape, out_spec, kb, name, comm=None):
    T = y.shape[-2]
    chunks = _chunks(kb)

    def body(x_ref, y_ref, o_ref):
        @pl.when(pl.program_id(1) == 0)
        def _():
            o_ref[...] = jnp.zeros_like(o_ref)

        yv = y_ref[...].astype(BF16)
        for c0, cw in chunks:
            o_ref[c0:c0 + cw, :] += _dot_tn(x_ref[:, c0:c0 + cw], yv)

    return _launch(
        body, name=name, grid=(nb, T // tm),
        in_specs=[x_spec, y_spec], out_specs=out_spec, out_shape=out_shape, args=(x, y), comm=comm)


def _tri(n, lower):
    r = lax.broadcasted_iota(jnp.int32, (n, n), 0)
    c = lax.broadcasted_iota(jnp.int32, (n, n), 1)
    return jnp.where((r >= c) if lower else (r <= c), 1.0, 0.0).astype(BF16)


def _tri_dot(tri, v):
    hi, mid, lo = _split3(v)
    return _dot(tri, hi) + _dot(tri, mid) + _dot(tri, lo)


def _fcum(fg, bf, *, ch, name):
    B, L, W = fg.shape
    nch = L // ch

    def body(fg_ref, bf_ref, f_ref):
        tri = _tri(ch, True)
        carry = jnp.zeros((1, W), F32)
        for c in range(nch):
            x = fg_ref[c * ch:(c + 1) * ch, :] + bf_ref[...]
            lf = jnp.minimum(x, 0.0) - jnp.log(1.0 + jnp.exp(-jnp.abs(x)))
            f_ref[c * ch:(c + 1) * ch, :] = _tri_dot(tri, lf) + carry
            carry = carry + jnp.sum(lf, axis=0, keepdims=True)

    return pl.pallas_call(
        body, name=name, grid=(B,),
        in_specs=[pl.BlockSpec((None, L, W), lambda b: (b, 0, 0)), pl.BlockSpec((1, W), lambda b: (0, 0))],
        out_specs=pl.BlockSpec((None, L, W), lambda b: (b, 0, 0)),
        out_shape=jax.ShapeDtypeStruct((B, L, W), F32),
        compiler_params=_params("arbitrary"),
    )(fg, bf)


def _fcum_bwd(dF, fg, bf, *, ch, name):
    B, L, W = fg.shape
    nch = L // ch

    def body(df_ref, fg_ref, bf_ref, dfg_ref, db_ref):
        @pl.when(pl.program_id(0) == 0)
        def _():
            db_ref[...] = jnp.zeros_like(db_ref)

        tri = _tri(ch, False)
        carry = jnp.zeros((1, W), F32)
        dbs = jnp.zeros((1, W), F32)
        for c in reversed(range(nch)):
            d = df_ref[c * ch:(c + 1) * ch, :]
            dlf = _tri_dot(tri, d) + carry
            carry = carry + jnp.sum(d, axis=0, keepdims=True)
            x = fg_ref[c * ch:(c + 1) * ch, :] + bf_ref[...]
            dfg = dlf * jax.nn.sigmoid(-x)
            dfg_ref[c * ch:(c + 1) * ch, :] = dfg.astype(BF16)
            dbs = dbs + jnp.sum(dfg, axis=0, keepdims=True)
        db_ref[...] += dbs

    blk = pl.BlockSpec((None, L, W), lambda b: (b, 0, 0))
    return pl.pallas_call(
        body, name=name, grid=(B,),
        in_specs=[blk, blk, pl.BlockSpec((1, W), lambda b: (0, 0))],
        out_specs=[blk, pl.BlockSpec((1, W), lambda b: (0, 0))],
        out_shape=[jax.ShapeDtypeStruct((B, L, W), BF16), jax.ShapeDtypeStruct((1, W), F32)],
        compiler_params=_params("arbitrary"),
    )(dF, fg, bf)


def _band_edges(tq):
    return sorted({min(tq, (k * tq // ATTN_BANDS + HALO - 1) // HALO * HALO) for k in range(ATTN_BANDS + 1)})


def _pair(h):
    return slice((h // 2) * 2 * HEAD_DIM, (h // 2 + 1) * 2 * HEAD_DIM)


def _own_lanes(a, h):
    low = lax.broadcasted_iota(jnp.int32, a.shape, 1) < HEAD_DIM
    return jnp.where(low if h % 2 == 0 else jnp.logical_not(low), a, jnp.zeros_like(a))


def _sum_lane(h):
    return HEAD_DIM if h % 2 == 0 else 0


def _own_lanes_and_ones(a, h):
    lane = lax.broadcasted_iota(jnp.int32, a.shape, 1)
    low = lane < HEAD_DIM
    return jnp.where(low if h % 2 == 0 else jnp.logical_not(low), a,
                     jnp.where(lane == _sum_lane(h), jnp.ones_like(a), jnp.zeros_like(a)))


def _attn_fwd(proj, fc, fr, *, tq, n_heads, name, comm=None):
    B, L, _ = proj.shape
    AD = n_heads * HEAD_DIM
    nq = L // tq
    W = fc.shape[-1]
    scale = HEAD_DIM ** -0.5
    edges = _band_edges(tq)

    v_ones, sum_lane = _own_lanes_and_ones, _sum_lane

    def body(q_ref, k_ref, v_ref, fr_ref, o_ref, lse_ref, m_s, acc_s):
        qi, ki = pl.program_id(1), pl.program_id(2)

        @pl.when(ki == 0)
        def _():
            m_s[...] = jnp.full_like(m_s, NEG)
            acc_s[...] = jnp.zeros_like(acc_s)

        def tile(diagonal):
            lane = lax.broadcasted_iota(jnp.int32, (tq, W), 1)
            m_all = m_s[...]
            m_out = m_all
            bands = [(r0, r1, r1 if diagonal else tq) for r0, r1 in zip(edges[:-1], edges[1:])]
            if diagonal:
                masks = {r0: (lax.broadcasted_iota(jnp.int32, (r1 - r0, c1), 1)
                              <= r0 + lax.broadcasted_iota(jnp.int32, (r1 - r0, c1), 0)) for r0, r1, c1 in bands}

            def scores(h, band):
                r0, r1, c1 = band
                sl = slice(h * HEAD_DIM, (h + 1) * HEAD_DIM)
                return _dot_nt(q_ref[r0:r1, sl] * scale, k_ref[0:c1, sl])

            work = [(h, band) for h in range(n_heads) for band in bands]
            nxt = scores(*work[0])
            for w, (h, band) in enumerate(work):
                r0, r1, c1 = band
                sl = slice(h * HEAD_DIM, (h + 1) * HEAD_DIM)
                s = nxt - fr_ref[h:h + 1, 0:c1]
                if w + 1 < len(work):
                    nxt = scores(*work[w + 1])
                if diagonal:
                    s = jnp.where(masks[r0], s, NEG)
                m_old = m_all[r0:r1, h:h + 1]
                m_new = jnp.maximum(m_old, jnp.max(s, axis=1, keepdims=True))
                alpha = jnp.exp(m_old - m_new)
                p = jnp.exp(s - m_new)
                own = slice(h * 2 * HEAD_DIM, (h + 1) * 2 * HEAD_DIM)
                acc_s[r0:r1, own] = alpha * acc_s[r0:r1, own] + _dot(p.astype(BF16), v_ones(v_ref[0:c1, _pair(h)], h))
                if r0 == 0:
                    m_parts = []
                m_parts.append(m_new)
                if r1 == tq:
                    m_out = jnp.where(lane == h, jnp.concatenate(m_parts, axis=0), m_out)
            m_s[...] = m_out

        @pl.when(ki < qi)
        def _():
            tile(False)

        @pl.when(ki == qi)
        def _():
            tile(True)
            lane = lax.broadcasted_iota(jnp.int32, (tq, W), 1)
            low = lax.broadcasted_iota(jnp.int32, (tq, 2 * HEAD_DIM), 1) < HEAD_DIM
            l_all = jnp.ones((tq, W), F32)
            for h in range(0, n_heads, 2):
                even = acc_s[:, h * 2 * HEAD_DIM:(h + 1) * 2 * HEAD_DIM]
                odd = acc_s[:, (h + 1) * 2 * HEAD_DIM:(h + 2) * 2 * HEAD_DIM]
                l_even = even[:, sum_lane(h):sum_lane(h) + 1]
                l_odd = odd[:, sum_lane(h + 1):sum_lane(h + 1) + 1]
                o_ref[:, _pair(h)] = jnp.where(low, even / l_even, odd / l_odd)
                l_all = jnp.where(lane == h, l_even, jnp.where(lane == h + 1, l_odd, l_all))
            lse_ref[...] = jnp.where(lane < n_heads, m_s[...] + jnp.log(l_all), 0.0)

    kv = lambda b, qi, ki: jnp.minimum(ki, qi)
    return _launch(
        body, name=name, grid=(B, nq, nq), args=(proj, proj, proj, fr), comm=comm,
        in_specs=[pl.BlockSpec((None, tq, AD), lambda b, qi, ki: (b, qi, 3)),
                  pl.BlockSpec((None, tq, AD), lambda b, qi, ki: (b, kv(b, qi, ki), 4)),
                  pl.BlockSpec((None, tq, AD), lambda b, qi, ki: (b, kv(b, qi, ki), 5)),
                  pl.BlockSpec((None, None, n_heads, tq), lambda b, qi, ki: (b, kv(b, qi, ki), 0, 0))],
        out_specs=[pl.BlockSpec((None, tq, AD), lambda b, qi, ki: (b, qi, 0)),
                   pl.BlockSpec((None, tq, W), lambda b, qi, ki: (b, qi, 0))],
        out_shape=[jax.ShapeDtypeStruct((B, L, AD), F32), jax.ShapeDtypeStruct((B, L, W), F32)],
        scratch_shapes=[pltpu.VMEM((tq, W), F32), pltpu.VMEM((tq, n_heads * 2 * HEAD_DIM), F32)])


def _attn_bwd(proj, o, do, lse, fc, fr, *, tq, n_heads, name, comm=None):
    B, L, _ = proj.shape
    AD = n_heads * HEAD_DIM
    nq = L // tq
    W = fc.shape[-1]
    HW = 2 * HEAD_DIM
    scale = HEAD_DIM ** -0.5
    edges = _band_edges(tq)

    def body(q_ref, k_ref, v_ref, o_ref, do_ref, lse_ref, fr_ref,
             dq_ref, dk_ref, dv_ref, dfk_ref, dfq_ref, dq_s, dk_s, dv_s):
        kj, qi = pl.program_id(1), pl.program_id(2)

        @pl.when((kj == 0) & (qi == 0))
        def _():
            dq_s[...] = jnp.zeros_like(dq_s)

        @pl.when(qi == kj)
        def _():
            dk_s[...] = jnp.zeros_like(dk_s)
            dv_s[...] = jnp.zeros_like(dv_s)

        def tile(diagonal):
            bands = [(r0, r1, r1) for r0, r1 in zip(edges[:-1], edges[1:])] if diagonal else [(0, tq, tq)]
            lse = lse_ref[...]
            for r0, r1, c1 in bands:
                nr = r1 - r0
                rows = pl.ds(pl.multiple_of(qi * tq + r0, 8), nr)
                if diagonal:
                    mask = (lax.broadcasted_iota(jnp.int32, (nr, c1), 1)
                            <= r0 + lax.broadcasted_iota(jnp.int32, (nr, c1), 0))
                for h in range(n_heads):
                    ps = _pair(h)
                    own = slice(h * HW, (h + 1) * HW)
                    k, v = k_ref[0:c1, ps], v_ref[0:c1, ps]
                    qs = q_ref[r0:r1, ps] * scale
                    dov = _own_lanes(do_ref[r0:r1, ps], h)
                    s = _dot_nt(_own_lanes(qs, h), k) - fr_ref[h:h + 1, 0:c1]
                    if diagonal:
                        s = jnp.where(mask, s, NEG)
                    p = jnp.exp(s - lse[r0:r1, h:h + 1])
                    dp = _dot_nt(dov, v)
                    dsum = jnp.sum(dov.astype(F32) * o_ref[r0:r1, ps], axis=1, keepdims=True)
                    dsb = (p * (dp - dsum)).astype(BF16)
                    dv = _dot_tn(p.astype(BF16), dov)
                    dk_s[0:c1, own] += _dot_tn(dsb, _own_lanes_and_ones(qs, h))
                    dq_s[rows, own] += _dot(dsb, _own_lanes_and_ones(k, h))
                    if h % 2 == 0:
                        dv_even = dv
                    else:
                        dv_s[0:c1, ps] += dv_even + dv

        def compact(acc, data_scale):
            rows = acc.shape[0]
            low = lax.broadcasted_iota(jnp.int32, (rows, HW), 1) < HEAD_DIM
            lane = lax.broadcasted_iota(jnp.int32, (rows, W), 1)
            vals, sums = [], jnp.zeros((rows, W), F32)
            for h in range(0, n_heads, 2):
                even, odd = acc[:, h * HW:(h + 1) * HW], acc[:, (h + 1) * HW:(h + 2) * HW]
                vals.append(jnp.where(low, even, odd) * data_scale)
                sums = jnp.where(lane == h, even[:, _sum_lane(h):_sum_lane(h) + 1],
                                 jnp.where(lane == h + 1, odd[:, _sum_lane(h + 1):_sum_lane(h + 1) + 1], sums))
            return vals, sums

        @pl.when(qi > kj)
        def _():
            tile(False)

        @pl.when(qi == kj)
        def _():
            tile(True)
            rows = pl.ds(pl.multiple_of(qi * tq, 8), tq)
            vals, sums = compact(dq_s[rows, :], scale)
            for h in range(0, n_heads, 2):
                dq_ref[rows, _pair(h)] = vals[h // 2]
            dfq_ref[rows, :] = sums

        @pl.when(qi == nq - 1)
        def _():
            vals, sums = compact(dk_s[...], 1.0)
            for h in range(0, n_heads, 2):
                dk_ref[:, _pair(h)] = vals[h // 2].astype(BF16)
            dfk_ref[...] = sums
            dv_ref[...] = dv_s[...].astype(BF16)

    qq = lambda b, kj, qi: jnp.maximum(qi, kj)
    qblk = lambda w, cb: pl.BlockSpec((None, tq, w), lambda b, kj, qi: (b, qq(b, kj, qi), cb))
    kblk = lambda w, cb: pl.BlockSpec((None, tq, w), lambda b, kj, qi: (b, kj, cb))
    return _launch(
        body, name=name, grid=(B, nq, nq), args=(proj, proj, proj, o, do, lse, fr), comm=comm,
        in_specs=[qblk(AD, 3), kblk(AD, 4), kblk(AD, 5), qblk(AD, 0), qblk(AD, 0), qblk(W, 0),
                  pl.BlockSpec((None, None, n_heads, tq), lambda b, kj, qi: (b, kj, 0, 0))],
        out_specs=[pl.BlockSpec((None, L, AD), lambda b, kj, qi: (b, 0, 0)),
                   kblk(AD, 0), kblk(AD, 0), kblk(W, 0),
                   pl.BlockSpec((None, L, W), lambda b, kj, qi: (b, 0, 0))],
        out_shape=[jax.ShapeDtypeStruct((B, L, AD), F32), jax.ShapeDtypeStruct((B, L, AD), BF16),
                   jax.ShapeDtypeStruct((B, L, AD), BF16), jax.ShapeDtypeStruct((B, L, W), F32),
                   jax.ShapeDtypeStruct((B, L, W), F32)],
        scratch_shapes=[pltpu.VMEM((L, n_heads * HW), F32), pltpu.VMEM((tq, n_heads * HW), F32),
                        pltpu.VMEM((tq, AD), F32)])


def _mix_gather(refs, first):
    b_ref, c_ref, hc_ref, cp_ref, hcp_ref, o_ref, cw_ref, p_ref = refs
    bg = b_ref[...].astype(F32)
    u = c_ref[...].astype(F32) * hc_ref[...].astype(F32)
    prev = cp_ref[...].astype(F32) * hcp_ref[...].astype(F32)
    prev = jnp.where(first, 0.0, prev)
    cv, u1, u2 = _causal_conv(u, prev, cw_ref[...])
    yc = bg * cv
    p = p_ref[...]
    rc = lax.rsqrt(_group_mean(yc * yc, p) + EPS)
    ya = o_ref[...].astype(F32)
    ra = lax.rsqrt(_group_mean(ya * ya, p) + EPS)
    return bg, (u, u1, u2), cv, yc * rc, rc, ya * ra, ra


def _mix_specs(tm, CD, D, grid_rank_fn):
    per = tm // HALO
    cur = lambda cb: pl.BlockSpec((None, tm, CD), lambda b, i: (b, i, cb))
    prev = lambda cb: pl.BlockSpec((None, HALO, CD), lambda b, i: (b, jnp.maximum(i * per - 1, 0), cb))
    return [cur(0), cur(1), cur(2), prev(1), prev(2), cur(0)]


def _mix_out(proj, o, cw, gc, ga, wout, h, pmat, next_gain, *, tm, name, comm=None):
    B, L, D = h.shape
    CD = o.shape[-1]
    const = lambda b, i: (0, 0)

    def body(b_ref, c_ref, hc_ref, cp_ref, hcp_ref, o_ref, cw_ref, p_ref, gc_ref, ga_ref, w_ref, h_ref, ng_ref,
             out_ref, y_ref, n_ref):
        first = pl.program_id(1) == 0
        _, _, _, zc, _, za, _ = _mix_gather((b_ref, c_ref, hc_ref, cp_ref, hcp_ref, o_ref, cw_ref, p_ref), first)
        yc = (zc * gc_ref[...]).astype(BF16)
        ya = (za * ga_ref[...]).astype(BF16)
        y_ref[:, :CD] = yc
        y_ref[:, CD:] = ya
        out = h_ref[...] + _dot(yc, w_ref[:CD, :]) + _dot(ya, w_ref[CD:, :])
        out_ref[...] = out
        n_ref[...] = _rms(out, ng_ref[...])

    tile = pl.BlockSpec((None, tm, D), lambda b, i: (b, i, 0))
    return _launch(
        body, name=name, grid=(B, L // tm),
        in_specs=_mix_specs(tm, CD, D, None)
                 + [pl.BlockSpec(cw.shape, const), pl.BlockSpec(pmat.shape, const),
                    pl.BlockSpec((1, CD), const), pl.BlockSpec((1, CD), const), pl.BlockSpec((D, D), const),
                    tile, pl.BlockSpec((1, D), const)],
        out_specs=[tile, tile, tile],
        out_shape=[jax.ShapeDtypeStruct((B, L, D), F32), jax.ShapeDtypeStruct((B, L, D), BF16),
                   jax.ShapeDtypeStruct((B, L, D), BF16)],
        args=(proj, proj, proj, proj, proj, o, cw, pmat, gc, ga, wout, h, next_gain), comm=comm)


def _mix_out_bwd(dhb, proj, o, cw, gc, ga, wout, pmat, *, tm, name, comm=None):
    B, L, D = dhb.shape
    CD = o.shape[-1]
    const = lambda b, i: (0, 0)

    def body(dh_ref, b_ref, c_ref, hc_ref, cp_ref, hcp_ref, o_ref, cw_ref, p_ref, gc_ref, ga_ref, w_ref,
             db_ref, dcv_ref, do_ref, dgc_ref, dga_ref, dcw_ref):
        first = pl.program_id(1) == 0

        @pl.when((pl.program_id(0) == 0) & first)
        def _():
            dgc_ref[...] = jnp.zeros_like(dgc_ref)
            dga_ref[...] = jnp.zeros_like(dga_ref)
            dcw_ref[...] = jnp.zeros_like(dcw_ref)

        bg, us, cv, zc, rc, za, ra = _mix_gather(
            (b_ref, c_ref, hc_ref, cp_ref, hcp_ref, o_ref, cw_ref, p_ref), first)
        p = p_ref[...]
        dh = dh_ref[...]
        dyc = _dot_nt(dh, w_ref[:CD, :])
        dya = _dot_nt(dh, w_ref[CD:, :])

        dgc_ref[...] += jnp.sum(dyc * zc, axis=0, keepdims=True)
        dz = dyc * gc_ref[...]
        dx = rc * (dz - zc * _group_mean(dz * zc, p))
        db_ref[...] = (dx * cv).astype(BF16)
        dcv = dx * bg
        dcv_ref[...] = dcv.astype(BF16)
        for k in range(3):
            dcw_ref[k:k + 1, :] += jnp.sum(dcv * us[2 - k], axis=0, keepdims=True)

        dga_ref[...] += jnp.sum(dya * za, axis=0, keepdims=True)
        dz = dya * ga_ref[...]
        do_ref[...] = (ra * (dz - za * _group_mean(dz * za, p))).astype(BF16)

    tile = lambda w: pl.BlockSpec((None, tm, w), lambda b, i: (b, i, 0))
    return _launch(
        body, name=name, grid=(B, L // tm), comm=comm,
        args=(dhb, proj, proj, proj, proj, proj, o, cw, pmat, gc, ga, wout),
        in_specs=[tile(D)] + _mix_specs(tm, CD, D, None)
                 + [pl.BlockSpec(cw.shape, const), pl.BlockSpec(pmat.shape, const),
                    pl.BlockSpec((1, CD), const), pl.BlockSpec((1, CD), const), pl.BlockSpec((D, D), const)],
        out_specs=[tile(CD), tile(CD), tile(CD),
                   pl.BlockSpec((1, CD), const), pl.BlockSpec((1, CD), const), pl.BlockSpec((8, CD), const)],
        out_shape=[jax.ShapeDtypeStruct((B, L, CD), BF16)] * 3
                  + [jax.ShapeDtypeStruct((1, CD), F32)] * 2 + [jax.ShapeDtypeStruct((8, CD), F32)])


def _conv_bwd(dcv, proj, cw, *, tm, name):
    B, L, CD = dcv.shape
    per = tm // HALO
    nhalo = L // HALO
    nt = L // tm

    def body(d_ref, dn_ref, c_ref, hc_ref, cw_ref, out_ref):
        last = pl.program_id(1) == nt - 1
        d = d_ref[...].astype(F32)
        nxt = jnp.where(last, 0.0, dn_ref[...].astype(F32))
        n0, n1 = _row_of(nxt, 0), _row_of(nxt, 1)
        rows = lax.broadcasted_iota(jnp.int32, d.shape, 0)
        d1 = jnp.where(rows == tm - 1, n0, pltpu.roll(d, tm - 1, 0))
        d2 = jnp.where(rows == tm - 2, n0, jnp.where(rows == tm - 1, n1, pltpu.roll(d, tm - 2, 0)))
        w = cw_ref[...]
        du = w[2:3, :] * d + w[1:2, :] * d1 + w[0:1, :] * d2
        out_ref[:, :CD] = (du * hc_ref[...].astype(F32)).astype(BF16)
        out_ref[:, CD:] = (du * c_ref[...].astype(F32)).astype(BF16)

    return pl.pallas_call(
        body, name=name, grid=(B, nt),
        in_specs=[pl.BlockSpec((None, tm, CD), lambda b, i: (b, i, 0)),
                  pl.BlockSpec((None, HALO, CD), lambda b, i: (b, jnp.minimum((i + 1) * per, nhalo - 1), 0)),
                  pl.BlockSpec((None, tm, CD), lambda b, i: (b, i, 1)),
                  pl.BlockSpec((None, tm, CD), lambda b, i: (b, i, 2)),
                  pl.BlockSpec(cw.shape, lambda b, i: (0, 0))],
        out_specs=pl.BlockSpec((None, tm, 2 * CD), lambda b, i: (b, i, 0)),
        out_shape=jax.ShapeDtypeStruct((B, L, 2 * CD), BF16),
        compiler_params=_params("arbitrary", "arbitrary"),
    )(dcv, dcv, proj, proj, cw)


def _place():
    x, y, c = lax.axis_index("x"), lax.axis_index("y"), lax.axis_index("c")
    others = [(1 - x, y), (x, 1 - y), (1 - x, 1 - y)]
    return x, y, c, others


def _all_gather_shards(shards, *, name):
    n = len(shards)

    def body(*refs):
        ins, outs = refs[:n], refs[n:2 * n]
        send, recv, fsend, frecv, lsem = refs[2 * n:]
        x, y, c, others = _place()
        me = 2 * x + y
        local = [pltpu.make_async_copy(ins[t], outs[t].at[me], lsem.at[t]) for t in range(n)]
        for cp in local:
            cp.start()

        def half(t, k):
            hr = shards[t].shape[0] // 2
            return pl.ds(pl.multiple_of(k * hr, HALO), hr)

        def ici(t, j, src_chip, to):
            src = ins[t].at[half(t, c)] if to is not None else outs[t].at[src_chip, half(t, c)]
            return pltpu.make_async_remote_copy(
                src_ref=src, dst_ref=outs[t].at[src_chip, half(t, c)],
                send_sem=send.at[3 * t + j], recv_sem=recv.at[3 * t + j],
                device_id=(x, y, c) if to is None else to, device_id_type=MESH)

        def d2d(t, j, src_chip, k):
            return pltpu.make_async_remote_copy(
                src_ref=outs[t].at[src_chip, half(t, k)], dst_ref=outs[t].at[src_chip, half(t, k)],
                send_sem=fsend.at[3 * t + j], recv_sem=frecv.at[3 * t + j],
                device_id=(x, y, 1 - c), device_id_type=MESH)

        firsts = [ici(t, j, me, (ox, oy, c)) for t in range(n) for j, (ox, oy) in enumerate(others)]
        for cp in firsts:
            cp.start()
        passed = []
        for t in range(n):
            for j, (ox, oy) in enumerate(others):
                ici(t, j, 2 * ox + oy, None).wait_recv()
                cp = d2d(t, j, 2 * ox + oy, c)
                cp.start()
                passed.append(cp)
        for t in range(n):
            for j, (ox, oy) in enumerate(others):
                d2d(t, j, 2 * ox + oy, 1 - c).wait_recv()
        for cp in firsts + passed:
            cp.wait_send()
        for cp in local:
            cp.wait()

    return pl.pallas_call(
        body, name=name,
        in_specs=[ANY] * n, out_specs=[ANY] * n,
        out_shape=[jax.ShapeDtypeStruct((N_SHARD,) + s.shape, s.dtype) for s in shards],
        scratch_shapes=[pltpu.SemaphoreType.DMA((3 * n,))] * 4 + [pltpu.SemaphoreType.DMA((n,))],
    )(*shards)


def _all_reduce_small(slab, *, name):
    def body(in_ref, out_ref, gath, send, recv):
        x, y, c, _ = _place()
        me = 4 * x + 2 * y + c
        gath[me] = in_ref[...]
        copies, peers = [], []
        for m in range(1, N_DEV):
            px = jnp.where((m >> 2) & 1, 1 - x, x)
            py = jnp.where((m >> 1) & 1, 1 - y, y)
            pc = jnp.where(m & 1, 1 - c, c)
            cp = pltpu.make_async_remote_copy(
                src_ref=in_ref, dst_ref=gath.at[me], send_sem=send.at[m - 1], recv_sem=recv.at[m - 1],
                device_id=(px, py, pc), device_id_type=MESH)
            cp.start()
            copies.append(cp)
            peers.append(4 * px + 2 * py + pc)
        for m in range(1, N_DEV):
            pltpu.make_async_remote_copy(
                src_ref=in_ref, dst_ref=gath.at[peers[m - 1]], send_sem=send.at[m - 1], recv_sem=recv.at[m - 1],
                device_id=(x, y, c), device_id_type=MESH).wait_recv()
        for cp in copies:
            cp.wait_send()
        acc = gath[0]
        for k in range(1, N_DEV):
            acc = acc + gath[k]
        out_ref[...] = acc

    vm = pl.BlockSpec(memory_space=pltpu.VMEM)
    return pl.pallas_call(
        body, name=name, in_specs=[vm], out_specs=vm,
        out_shape=jax.ShapeDtypeStruct(slab.shape, slab.dtype),
        scratch_shapes=[pltpu.VMEM((N_DEV,) + slab.shape, slab.dtype),
                        pltpu.SemaphoreType.DMA((N_DEV - 1,)), pltpu.SemaphoreType.DMA((N_DEV - 1,))],
    )(slab)


def _gather_stage(shards, into, *, ici=(), d2d=()):
    n = len(shards) if into is None else len(into)
    ns = len(shards) if ici else 0
    ni, nd = max(len(ici), 1), max(len(d2d), 1)
    shapes = [s.shape for s in shards] if into is None else [p.shape[1:] for p in into]
    dtypes = [s.dtype for s in shards] if into is None else [p.dtype for p in into]

    def copies(ins, outs, sems, sending):
        x, y, c, others = _place()
        me = 2 * x + y
        out = []
        for t in range(n):
            hr = shapes[t][0] // 2
            mine = pl.ds(pl.multiple_of(c * hr, HALO), hr)
            theirs = pl.ds(pl.multiple_of((1 - c) * hr, HALO), hr)
            for a, j in enumerate(ici):
                ox, oy = others[j]
                src_chip = me if sending else 2 * ox + oy
                out.append(pltpu.make_async_remote_copy(
                    src_ref=ins[t].at[mine], dst_ref=outs[t].at[src_chip, mine],
                    send_sem=sems[0].at[ni * t + a], recv_sem=sems[1].at[ni * t + a],
                    device_id=(ox, oy, c) if sending else (x, y, c), device_id_type=MESH))
            for a, j in enumerate(d2d):
                ox, oy = others[j]
                blk = outs[t].at[2 * ox + oy, mine if sending else theirs]
                out.append(pltpu.make_async_remote_copy(
                    src_ref=blk, dst_ref=blk, send_sem=sems[2].at[nd * t + a], recv_sem=sems[3].at[nd * t + a],
                    device_id=(x, y, 1 - c) if sending else (x, y, c), device_id_type=MESH))
        return out

    def local(ins, outs, sems):
        if into is not None:
            return []
        x, y, _, _ = _place()
        return [pltpu.make_async_copy(ins[t], outs[t].at[2 * x + y], sems[4].at[t]) for t in range(n)]

    def start(ins, outs, sems):
        for cp in local(ins, outs, sems) + copies(ins, outs, sems, True):
            cp.start()

    def finish(ins, outs, sems):
        for cp in copies(ins, outs, sems, False):
            cp.wait_recv()
        for cp in copies(ins, outs, sems, True):
            cp.wait_send()
        for cp in local(ins, outs, sems):
            cp.wait()

    return _Comm((list(shards) if ici or into is None else []) + (list(into) if into is not None else []),
                 [jax.ShapeDtypeStruct((N_SHARD,) + tuple(sh), dt) for sh, dt in zip(shapes, dtypes)],
                 [ni * n, ni * n, nd * n, nd * n, n], start, finish,
                 aliases=None if into is None else {ns + t: t for t in range(n)})


def _gather_ici(shards):
    return _gather_stage(shards, None, ici=(0, 1, 2))


def _gather_d2d(parts):
    return _gather_stage((), parts, d2d=(0, 1, 2))


def _swap_halves(grads):
    n = len(grads)

    def copies(ins, outs, sems):
        x, y, c, _ = _place()
        out = []
        for t in range(n):
            hr = grads[t].shape[1] // 2
            rows = pl.ds(pl.multiple_of((1 - c) * hr, 8), hr)
            out.append(pltpu.make_async_remote_copy(
                src_ref=ins[t].at[:, rows, :], dst_ref=outs[t], send_sem=sems[0].at[t], recv_sem=sems[1].at[t],
                device_id=(x, y, 1 - c), device_id_type=MESH))
        return out

    def start(ins, outs, sems):
        for cp in copies(ins, outs, sems):
            cp.start()

    def finish(ins, outs, sems):
        for cp in copies(ins, outs, sems):
            cp.wait()

    return _Comm(grads, [jax.ShapeDtypeStruct((N_SHARD, g.shape[1] // 2, g.shape[2]), g.dtype) for g in grads],
                 [n, n], start, finish)


def _pair_sum(g, got, c, *, name):
    ns, R, C = g.shape
    hr = R // 2

    def body(c_ref, g_ref, r_ref, o_ref):
        o_ref[...] = (g_ref[...] + r_ref[...]).astype(BF16)

    return pl.pallas_call(
        body, name=name,
        grid_spec=pltpu.PrefetchScalarGridSpec(
            num_scalar_prefetch=1, grid=(ns,),
            in_specs=[pl.BlockSpec((None, hr, C), lambda s, cr: (s, cr[0], 0)),
                      pl.BlockSpec((None, hr, C), lambda s, cr: (s, 0, 0))],
            out_specs=pl.BlockSpec((None, hr, C), lambda s, cr: (s, 0, 0))),
        out_shape=jax.ShapeDtypeStruct((ns, hr, C), BF16),
        compiler_params=_params("arbitrary"),
    )(c, g, got)


def _scatter_chips(sums):
    n = len(sums)

    def copies(ins, outs, sems, sending):
        x, y, c, others = _place()
        me = 2 * x + y
        out = []
        for t in range(n):
            for j, (ox, oy) in enumerate(others):
                there = 2 * ox + oy
                out.append(pltpu.make_async_remote_copy(
                    src_ref=ins[t].at[there if sending else me], dst_ref=outs[t].at[me if sending else there],
                    send_sem=sems[0].at[3 * t + j], recv_sem=sems[1].at[3 * t + j],
                    device_id=(ox, oy, c) if sending else (x, y, c), device_id_type=MESH))
        return out

    def start(ins, outs, sems):
        for cp in copies(ins, outs, sems, True):
            cp.start()

    def finish(ins, outs, sems):
        for cp in copies(ins, outs, sems, False):
            cp.wait_recv()
        for cp in copies(ins, outs, sems, True):
            cp.wait_send()

    return _Comm(sums, [jax.ShapeDtypeStruct(s.shape, s.dtype) for s in sums], [3 * n, 3 * n], start, finish)


def _chip_sum(g, got, landed, idx, *, name):
    ns, R, C = g.shape
    hr = R // 2

    def body(i_ref, g_ref, r_ref, a_ref, b_ref, c_ref, o_ref):
        acc = g_ref[...] + r_ref[...]
        for ref in (a_ref, b_ref, c_ref):
            acc = acc + ref[...].astype(F32)
        o_ref[...] = acc

    other = lambda k: pl.BlockSpec((None, hr, C), lambda s, ir: (ir[2 + k], 0, 0))
    return pl.pallas_call(
        body, name=name,
        grid_spec=pltpu.PrefetchScalarGridSpec(
            num_scalar_prefetch=1, grid=(1,),
            in_specs=[pl.BlockSpec((None, hr, C), lambda s, ir: (ir[0], ir[1], 0)),
                      pl.BlockSpec((None, hr, C), lambda s, ir: (ir[0], 0, 0)),
                      other(0), other(1), other(2)],
            out_specs=pl.BlockSpec((hr, C), lambda s, ir: (ir[1], 0))),
        out_shape=jax.ShapeDtypeStruct((R, C), F32),
        compiler_params=_params("arbitrary"),
    )(idx, g, got, landed, landed, landed)


def _share_halves(halves):
    n = len(halves)

    def copies(outs, sems, sending):
        x, y, c, _ = _place()
        out = []
        for t in range(n):
            hr = halves[t].shape[0] // 2
            rows = pl.ds(pl.multiple_of((c if sending else 1 - c) * hr, 8), hr)
            out.append(pltpu.make_async_remote_copy(
                src_ref=outs[t].at[rows, :], dst_ref=outs[t].at[rows, :], send_sem=sems[0].at[t],
                recv_sem=sems[1].at[t], device_id=(x, y, 1 - c) if sending else (x, y, c), device_id_type=MESH))
        return out

    def start(ins, outs, sems):
        for cp in copies(outs, sems, True):
            cp.start()

    def finish(ins, outs, sems):
        for cp in copies(outs, sems, False):
            cp.wait_recv()
        for cp in copies(outs, sems, True):
            cp.wait_send()

    return _Comm(halves, [jax.ShapeDtypeStruct(h.shape, h.dtype) for h in halves], [n, n], start, finish,
                 aliases={t: t for t in range(n)})


def _adamw(w, g, m, v, *, name):
    R, C = w.shape
    tr = R
    for cand in (256, 128, 64, 32, 16, 8):
        if R % cand == 0:
            tr = cand
            break

    def body(w_ref, g_ref, m_ref, v_ref, go_ref, d_ref, mo_ref, vo_ref):
        gv = g_ref[...]
        go_ref[...] = gv
        mn = ADAM_B1 * m_ref[...] + (1.0 - ADAM_B1) * gv
        vn = ADAM_B2 * v_ref[...] + (1.0 - ADAM_B2) * (gv * gv)
        m_hat = mn / (1.0 - ADAM_B1 ** ADAM_STEP)
        v_hat = vn / (1.0 - ADAM_B2 ** ADAM_STEP)
        d_ref[...] = -ADAM_LR * (m_hat / (jnp.sqrt(v_hat) + ADAM_EPS) + ADAM_WD * w_ref[...])
        mo_ref[...] = mn
        vo_ref[...] = vn

    blk = pl.BlockSpec((tr, C), lambda i: (i, 0))
    return pl.pallas_call(
        body, name=name, grid=(R // tr,), in_specs=[blk] * 4, out_specs=[blk] * 4,
        out_shape=[jax.ShapeDtypeStruct((R, C), F32)] * 4,
        compiler_params=_params("arbitrary"),
    )(w, g, m, v)


def _pack_small(D, meta, n1, nm, n3, nf, gc, ga, bf, cw):
    def row(a):
        a = a.reshape(-1, a.shape[-1])
        return jnp.pad(a, ((0, 0), (0, D - a.shape[-1])))
    rows = [row(meta), row(n1), row(nm), row(n3), row(nf), row(jnp.concatenate([gc, ga], axis=-1)), row(bf), row(cw)]
    slab = jnp.concatenate(rows, axis=0)
    return jnp.pad(slab, ((0, SMALL_ROWS - slab.shape[0]), (0, 0)))


def _unpack_small(slab, like):
    meta, n1, nm, n3, nf, gc, ga, bf, cw = like
    nmeta, mc = meta.shape
    out = [slab[:nmeta, :mc].reshape(meta.shape)]
    r = nmeta
    for a in (n1, nm, n3, nf):
        out.append(slab[r, :a.shape[-1]].reshape(a.shape))
        r += 1
    cd = gc.shape[-1]
    out.append(slab[r, :cd].reshape(gc.shape))
    out.append(slab[r, cd:cd + ga.shape[-1]].reshape(ga.shape))
    r += 1
    out.append(slab[r, :bf.shape[-1]].reshape(bf.shape))
    r += 1
    out.append(slab[r:r + 3, :cw.shape[-1]].reshape(cw.shape))
    return out


def kernel(x, meta_tokens, ffn1_norm, ffn1_w_gu, ffn1_w_down, mix_norm, w_in, conv_w, b_f, out_norm_conv, out_norm_attn, w_out, ffn2_norm, ffn2_w_gu, ffn2_w_down, final_norm, loss_target, m_meta_tokens, m_ffn1_norm, m_ffn1_w_gu, m_ffn1_w_down, m_mix_norm, m_w_in, m_conv_w, m_b_f, m_out_norm_conv, m_out_norm_attn, m_w_out, m_ffn2_norm, m_ffn2_w_gu, m_ffn2_w_down, m_final_norm, v_meta_tokens, v_ffn1_norm, v_ffn1_w_gu, v_ffn1_w_down, v_mix_norm, v_w_in, v_conv_w, v_b_f, v_out_norm_conv, v_out_norm_attn, v_w_out, v_ffn2_norm, v_ffn2_w_gu, v_ffn2_w_down, v_final_norm):
    B, S, D = x.shape
    L = S + N_META
    T = B * L
    tm = L // 3
    assert tm * 3 == L and tm % HALO == 0
    guc = ffn1_w_gu.shape[-1]
    ff = N_SHARD * guc // 2
    H = b_f.shape[-1]
    AD = H * HEAD_DIM
    CD = conv_w.shape[-1] * N_SHARD
    assert CD == AD and CD + AD == D and CD % LANES == 0
    n_main = 3 * CD + 3 * AD
    ins = w_in.shape[-1]

    xi, yi, ci = lax.axis_index("x"), lax.axis_index("y"), lax.axis_index("c")
    chip = 2 * xi + yi

    small_shard = jnp.zeros((2 * HALO, meta_tokens.shape[-1]), F32)
    small_shard = small_shard.at[:N_META].set(meta_tokens)
    small_shard = small_shard.at[N_META:N_META + 3, :conv_w.shape[-1]].set(conv_w[0])
    big = [ffn1_w_gu[0], ffn1_w_down[0], w_in[0], w_out[0], ffn2_w_gu[0], ffn2_w_down[0]]
    wgu1_s, wd1_s, win_s, wout_s, wgu2_s, wd2_s = [w.astype(BF16) for w in big]
    small_g, = _all_gather_shards([small_shard], name="gather_small")
    meta_f = jnp.moveaxis(small_g[:, :N_META], 0, 1).reshape(N_META, D)
    cw_f = jnp.moveaxis(small_g[:, N_META:N_META + 3, :conv_w.shape[-1]], 0, 1).reshape(3, CD)
    cw8 = jnp.pad(cw_f, ((0, 5), (0, 0)))
    bf_p = jnp.pad(b_f, ((0, 0), (0, LANES - H)))
    gid = jnp.arange(CD) // HEAD_DIM
    pmat = jnp.where(gid[:, None] == gid[None, :], 1.0 / HEAD_DIM, 0.0).astype(BF16)

    gu_shape = jax.ShapeDtypeStruct((2, T, ff), BF16)
    gu_w_spec = pl.BlockSpec((None, D, guc), lambda s, i: (s, 0, 0))
    gu_o_spec = pl.BlockSpec((None, tm, guc), lambda s, i: (s // 2, i, s % 2))

    sid = jnp.bitwise_xor(chip, jnp.array([0, 2, 1, 3], jnp.int32)).astype(jnp.int32)
    (h0, n1), wgu1_h = _embed_norm(x, meta_f, ffn1_norm, tm=tm, name="embed_norm",
                                   comm=_gather_stage([wgu1_s], None, ici=(0, 1)))
    gu1, wgu1_h = _ffn_up(n1, wgu1_s[None], sid, None, tm=tm, first=0, count=1, name="ffn1_up_own",
                          comm=_gather_stage([wgu1_s], wgu1_h, ici=(2,), d2d=(0, 1)))
    gu1, out = _ffn_up(n1, wgu1_h[0], sid, gu1, tm=tm, first=1, count=2, name="ffn1_up_near",
                       comm=_join(_gather_stage((), wgu1_h, d2d=(2,)), _gather_ici([wd1_s, wout_s])))
    wgu1, down_w = out[0], out[1:]
    gu1, (wd1, wout_g) = _ffn_up(n1, wgu1, sid, gu1, tm=tm, first=3, count=1, name="ffn1_up_far",
                                 comm=_gather_d2d(down_w))
    wd1 = wd1.reshape(ff, D)
    (h1, n2), win_h = _ffn_down(gu1, wd1, h0, mix_norm, tm=tm, name="ffn1_down", comm=_gather_ici([win_s]))
    win_g, = _run_comm(_gather_d2d(win_h), name="gather_w_in")
    wout_f = wout_g.reshape(D, D)
    win_f = jnp.moveaxis(win_g, 0, 1).reshape(D, N_SHARD * ins)
    win_main = win_f[:, :n_main]
    win_fg = jnp.pad(win_f[:, n_main:], ((0, 0), (0, LANES - H)))

    proj, _ = _matmul_nn(n2, win_main, tm=tm, nb=n_main // (3 * CD),
                         w_spec=pl.BlockSpec((D, 3 * CD), lambda s, i: (0, s)),
                         out_shape=jax.ShapeDtypeStruct((T, n_main), BF16),
                         out_spec=pl.BlockSpec((tm, 3 * CD), lambda s, i: (i, s)), name="mix_in")
    fg, _ = _matmul_nn(n2, win_fg, tm=tm, nb=1, w_spec=pl.BlockSpec((D, LANES), lambda s, i: (0, 0)),
                       out_shape=jax.ShapeDtypeStruct((T, LANES), F32),
                       out_spec=pl.BlockSpec((tm, LANES), lambda s, i: (i, 0)), name="mix_in_fg")
    proj3 = proj.reshape(B, L, n_main)
    fg3 = fg.reshape(B, L, LANES)
    fc = _fcum(fg3, bf_p, ch=tm, name="forget_cumsum")
    fr = fc[:, :, :H].reshape(B, L // tm, tm, H).transpose(0, 1, 3, 2)
    (o, lse), ffn2_w = _attn_fwd(proj3, fc, fr, tq=tm, n_heads=H, name="attn_fwd",
                                 comm=_gather_ici([wgu2_s, wd2_s]))
    (h2, ymix, n3), (wgu2, wd2) = _mix_out(
        proj3, o, cw8, out_norm_conv, out_norm_attn, wout_f, h1.reshape(B, L, D), pmat, ffn2_norm,
        tm=tm, name="mix_out", comm=_gather_d2d(ffn2_w))
    wd2 = wd2.reshape(ff, D)
    h2 = h2.reshape(T, D)
    n3 = n3.reshape(T, D)

    gu2, _ = _matmul_nn(n3, wgu2, tm=tm, nb=N_SHARD, w_spec=gu_w_spec, out_shape=gu_shape, out_spec=gu_o_spec,
                        name="ffn2_up")
    (dh3f, dh3b, d_gf, loss_part), _ = _ffn_down_loss(gu2, wd2, h2, final_norm.reshape(1, D), loss_target,
                                                      tm=tm, name="ffn2_down_loss")

    c_arr = jnp.reshape(ci, (1,)).astype(jnp.int32)
    ks = jnp.arange(N_SHARD - 1, dtype=jnp.int32)
    idx = jnp.concatenate([jnp.stack([chip, ci]).astype(jnp.int32), ks + (ks >= chip).astype(jnp.int32)])

    def pair_sums(grads, got, names):
        return [_pair_sum(g, r, c_arr, name="pair_sum_" + nm) for g, r, nm in zip(grads, got, names)]

    def chip_sums(grads, got, landed, names):
        return [_chip_sum(g, r, l, idx, name="chip_sum_" + nm) for g, r, l, nm in zip(grads, got, landed, names)]

    def dw_up(n, dgu, name, comm=None):
        return _matmul_tn(
            n, dgu, tm=tm, nb=N_SHARD, kb=D, x_spec=pl.BlockSpec((tm, D), lambda s, i: (i, 0)),
            y_spec=pl.BlockSpec((None, tm, guc), lambda s, i: (s // 2, i, s % 2)),
            out_shape=jax.ShapeDtypeStruct((N_SHARD, D, guc), F32),
            out_spec=pl.BlockSpec((None, D, guc), lambda s, i: (s, 0, 0)), name=name, comm=comm)

    (dgu2, d_wd2), _ = _ffn_bwd_act(dh3b, gu2, wd2, tm=tm, guc=guc, name="ffn2_bwd_act")
    (dh2, dh2b, d_g3), _ = _ffn_bwd_in(dgu2, wgu2, h2, ffn2_norm, dh3f, tm=tm, scale=1.0, name="ffn2_bwd_in")
    d_wgu2, _ = dw_up(n3, dgu2, "ffn2_dw_up")
    grads_f2 = [d_wgu2, d_wd2.reshape(N_SHARD, ff // N_SHARD, D)]
    names_f2 = ["wgu2", "wd2"]

    dh2b3 = dh2b.reshape(B, L, D)
    (d_bg, d_cv, d_o, d_gc, d_ga, d_cw), got_f2 = _mix_out_bwd(
        dh2b3, proj3, o, cw8, out_norm_conv, out_norm_attn, wout_f, pmat, tm=tm, name="mix_out_bwd",
        comm=_swap_halves(grads_f2))
    sums_f2 = pair_sums(grads_f2, got_f2, names_f2)
    d_wout, _ = _matmul_tn(
        ymix.reshape(T, D), dh2b, tm=tm, nb=1, kb=D,
        x_spec=pl.BlockSpec((tm, D), lambda s, i: (i, 0)), y_spec=pl.BlockSpec((tm, D), lambda s, i: (i, 0)),
        out_shape=jax.ShapeDtypeStruct((D, D), F32), out_spec=pl.BlockSpec((D, D), lambda s, i: (0, 0)),
        name="dw_out")
    d_cc = _conv_bwd(d_cv, proj3, cw8, tm=tm, name="conv_bwd")
    (d_q, d_k, d_v, d_fk, d_fq), landed_f2 = _attn_bwd(proj3, o, d_o, lse, fc, fr, tq=tm, n_heads=H, name="attn_bwd",
                                                       comm=_scatter_chips(sums_f2))
    halves_f2 = chip_sums(grads_f2, got_f2, landed_f2, names_f2)
    d_fc = d_fq - d_fk
    d_fg, d_bf = _fcum_bwd(d_fc, fg3, bf_p, ch=tm, name="forget_cumsum_bwd")

    parts = [d_bg.reshape(T, CD), d_cc.reshape(T, 2 * CD), d_q.reshape(T, AD), d_k.reshape(T, AD),
             d_v.reshape(T, AD), d_fg.reshape(T, LANES)]
    (dh1, dh1b, d_gm, d_proj), g_f2 = _mix_bwd_in(parts, win_main, win_fg, h1, mix_norm, dh2, tm=tm, scale=0.5,
                                                  name="mix_bwd_in", comm=_share_halves(halves_f2))
    wide = d_proj.shape[1]
    bw = next(c for c in (768, 640, 512, 384, 256, LANES) if wide % c == 0)
    d_win_nat, _ = _matmul_tn(
        n2, d_proj, tm=tm, nb=wide // bw, kb=D,
        x_spec=pl.BlockSpec((tm, D), lambda s, i: (i, 0)), y_spec=pl.BlockSpec((tm, bw), lambda s, i: (i, s)),
        out_shape=jax.ShapeDtypeStruct((D, wide), F32), out_spec=pl.BlockSpec((D, bw), lambda s, i: (0, s)),
        name="dw_in")
    d_win = jnp.moveaxis(d_win_nat[:, :N_SHARD * ins].reshape(D, N_SHARD, ins), 1, 0)
    grads_mx = [d_win, d_wout.reshape(N_SHARD, D // N_SHARD, D)]
    names_mx = ["win", "wout"]

    (dgu1, d_wd1), got_mx = _ffn_bwd_act(dh1b, gu1, wd1, tm=tm, guc=guc, name="ffn1_bwd_act",
                                         comm=_swap_halves(grads_mx))
    sums_mx = pair_sums(grads_mx, got_mx, names_mx)
    grads_d1 = [d_wd1.reshape(N_SHARD, ff // N_SHARD, D)]
    d_wgu1, out = dw_up(n1, dgu1, "ffn1_dw_up", comm=_join(_scatter_chips(sums_mx), _swap_halves(grads_d1)))
    landed_mx, got_d1 = out[:2], out[2:]
    halves_mx = chip_sums(grads_mx, got_mx, landed_mx, names_mx)
    sums_d1 = pair_sums(grads_d1, got_d1, ["wd1"])
    grads_u1 = [d_wgu1]
    (grad_x, d_meta, d_g1), out = _ffn_bwd_in_first(
        dgu1, wgu1, h0, ffn1_norm, dh1, tm=tm, batch=B, name="ffn1_bwd_in",
        comm=_join(_join(_share_halves(halves_mx), _scatter_chips(sums_d1)), _swap_halves(grads_u1)))
    g_mx, landed_d1, got_u1 = out[:2], out[2:3], out[3:]
    halves_d1 = chip_sums(grads_d1, got_d1, landed_d1, ["wd1"])
    sums_u1 = pair_sums(grads_u1, got_u1, ["wgu1"])
    out = _run_comm(_join(_share_halves(halves_d1), _scatter_chips(sums_u1)), name="scatter_ffn1")
    g_d1, landed_u1 = out[:1], out[1:]
    halves_u1 = chip_sums(grads_u1, got_u1, landed_u1, ["wgu1"])
    g_u1 = _run_comm(_share_halves(halves_u1), name="share_ffn1")
    g_big = [g_u1[0], g_d1[0], g_mx[0], g_mx[1], g_f2[0], g_f2[1]]

    loss_row = jnp.zeros((1, D), F32).at[0, 0].set(loss_part[0, 0])
    slab = _pack_small(D, d_meta, d_g1, d_gm, d_g3, d_gf, d_gc, d_ga, d_bf[:, :H], d_cw[:3])
    slab = slab.at[SMALL_ROWS - 1].set(loss_row[0])
    total = _all_reduce_small(slab, name="reduce_small")
    loss = total[SMALL_ROWS - 1, 0]
    mcols = meta_tokens.shape[-1]
    ccols = conv_w.shape[-1]
    full_like = (jnp.zeros((N_META, D)), ffn1_norm, mix_norm, ffn2_norm, final_norm.reshape(1, D), out_norm_conv,
                 out_norm_attn, b_f, jnp.zeros((1, 3, CD)))
    g_small = _unpack_small(total, full_like)
    g_small[0] = lax.dynamic_slice_in_dim(g_small[0], chip * mcols, mcols, axis=1)
    g_small[8] = lax.dynamic_slice_in_dim(g_small[8], chip * ccols, ccols, axis=2)

    def small_slab(meta, a1, am, a3, af, gc, ga, bf, cw):
        return _pack_small(D, meta, a1, am, a3, af.reshape(1, D), gc, ga, bf, cw[0])

    w_small = small_slab(meta_tokens, ffn1_norm, mix_norm, ffn2_norm, final_norm, out_norm_conv, out_norm_attn, b_f, conv_w)
    m_small = small_slab(m_meta_tokens, m_ffn1_norm, m_mix_norm, m_ffn2_norm, m_final_norm, m_out_norm_conv,
                         m_out_norm_attn, m_b_f, m_conv_w)
    v_small = small_slab(v_meta_tokens, v_ffn1_norm, v_mix_norm, v_ffn2_norm, v_final_norm, v_out_norm_conv,
                         v_out_norm_attn, v_b_f, v_conv_w)
    gs = list(g_small)
    gs[4] = gs[4].reshape(final_norm.shape)
    g_slab = small_slab(gs[0], gs[1], gs[2], gs[3], gs[4], gs[5], gs[6], gs[7], gs[8])
    local_like = (meta_tokens, ffn1_norm, mix_norm, ffn2_norm, final_norm.reshape(1, D), out_norm_conv, out_norm_attn,
                  b_f, conv_w)
    small_out = [_unpack_small(s, local_like)
                 for s in _adamw(w_small, g_slab, m_small, v_small, name="adamw_small")[1:]]
    for lst in small_out:
        lst[4] = lst[4].reshape(final_norm.shape)

    names = ["wgu1", "wd1", "win", "wout", "wgu2", "wd2"]
    w_big = big
    m_big = [m_ffn1_w_gu[0], m_ffn1_w_down[0], m_w_in[0], m_w_out[0], m_ffn2_w_gu[0], m_ffn2_w_down[0]]
    v_big = [v_ffn1_w_gu[0], v_ffn1_w_down[0], v_w_in[0], v_w_out[0], v_ffn2_w_gu[0], v_ffn2_w_down[0]]
    big_out = [_adamw(w, g, m, v, name="adamw_" + nm) for w, g, m, v, nm in zip(w_big, g_big, m_big, v_big, names)]

    def assemble(small, bigs):
        meta, a1, am, a3, af, gc, ga, bf, cw = small
        gu1_, d1_, win_, wout_, gu2_, d2_ = [b[None] for b in bigs]
        return [meta, a1, gu1_, d1_, am, win_, cw, bf, gc, ga, wout_, a3, gu2_, d2_, af]

    gs_out = list(g_small)
    gs_out[4] = gs_out[4].reshape(final_norm.shape)
    grads_out = assemble(gs_out, [b[0] for b in big_out])
    delta_out = assemble(small_out[0], [b[1] for b in big_out])
    m_out = assemble(small_out[1], [b[2] for b in big_out])
    v_out = assemble(small_out[2], [b[3] for b in big_out])
    return (loss, grad_x, *grads_out, *delta_out, *m_out, *v_out)
```

```python
import functools

import jax
import jax.numpy as jnp
from jax import lax
from jax.experimental import pallas as pl
from jax.experimental.pallas import tpu as pltpu

F32 = jnp.float32
BF16 = jnp.bfloat16

EPS = 1e-6
N_META = 16
HEAD_DIM = 64
N_SHARD = 4
N_DEV = 8
HALO = 16
LANES = 128
SMALL_ROWS = 32
VMEM_LIMIT_V7X = 56 * 1024 * 1024
NEG = -1e30
ATTN_BANDS = 2

ADAM_LR = 0.001
ADAM_B1 = 0.9
ADAM_B2 = 0.999
ADAM_EPS = 1e-08
ADAM_WD = 0.01
ADAM_STEP = 10

MESH = pl.DeviceIdType.MESH
ANY = pl.BlockSpec(memory_space=pl.ANY)
NT_DIMS = (((1,), (1,)), ((), ()))
TN_DIMS = (((0,), (0,)), ((), ()))


def _params(*sem):
    return pltpu.CompilerParams(dimension_semantics=sem, vmem_limit_bytes=VMEM_LIMIT_V7X)


class _Comm:
    def __init__(self, ins, out_shapes, sems, start, finish, aliases=None):
        self.ins, self.out_shapes, self.sems = list(ins), list(out_shapes), list(sems)
        self.start, self.finish, self.aliases = start, finish, dict(aliases or {})


def _join(a, b):
    ni, no, ns = len(a.ins), len(a.out_shapes), len(a.sems)

    def start(ins, outs, sems):
        a.start(ins[:ni], outs[:no], sems[:ns])
        b.start(ins[ni:], outs[no:], sems[ns:])

    def finish(ins, outs, sems):
        a.finish(ins[:ni], outs[:no], sems[:ns])
        b.finish(ins[ni:], outs[no:], sems[ns:])

    aliases = dict(a.aliases)
    aliases.update({ni + i: no + j for i, j in b.aliases.items()})
    return _Comm(a.ins + b.ins, a.out_shapes + b.out_shapes, a.sems + b.sems, start, finish, aliases)


def _launch(body, *, name, grid, in_specs, out_specs, out_shape, args, scratch_shapes=(), comm=None, prefetch=(),
            aliases=None):
    single = not isinstance(out_shape, (list, tuple))
    out_specs = [out_specs] if single else list(out_specs)
    out_shape = [out_shape] if single else list(out_shape)
    in_specs, scratch_shapes, prefetch = list(in_specs), list(scratch_shapes), list(prefetch)
    params = _params(*(("arbitrary",) * len(grid)))
    n_pf, n_in, n_out, n_scr = len(prefetch), len(in_specs), len(out_specs), len(scratch_shapes)
    c_ins = comm.ins if comm else []
    c_shapes = comm.out_shapes if comm else []
    c_sems = comm.sems if comm else []
    c_in, c_out = len(c_ins), len(c_shapes)

    def carrier(*refs):
        p = 0
        pf = refs[p:p + n_pf]; p += n_pf
        a = refs[p:p + n_in]; p += n_in
        ci = refs[p:p + c_in]; p += c_in
        o = refs[p:p + n_out]; p += n_out
        co = refs[p:p + c_out]; p += c_out
        s = refs[p:p + n_scr]; p += n_scr
        cs = refs[p:]
        if comm:
            first = functools.reduce(lambda u, v: u & v, [pl.program_id(k) == 0 for k in range(len(grid))])

            @pl.when(first)
            def _():
                comm.start(ci, co, cs)

        body(*pf, *a, *o, *s)

        if comm:
            last = functools.reduce(lambda u, v: u & v, [pl.program_id(k) == grid[k] - 1 for k in range(len(grid))])

            @pl.when(last)
            def _():
                comm.finish(ci, co, cs)

    io_aliases = {n_pf + i: j for i, j in (aliases or {}).items()}
    if comm:
        io_aliases.update({n_pf + n_in + i: n_out + j for i, j in comm.aliases.items()})
    all_in, all_out = in_specs + [ANY] * c_in, out_specs + [ANY] * c_out
    all_scratch = scratch_shapes + [pltpu.SemaphoreType.DMA((k,)) for k in c_sems]
    if n_pf:
        spec = dict(grid_spec=pltpu.PrefetchScalarGridSpec(
            num_scalar_prefetch=n_pf, grid=grid, in_specs=all_in, out_specs=all_out, scratch_shapes=all_scratch))
    else:
        spec = dict(grid=grid, in_specs=all_in, out_specs=all_out, scratch_shapes=all_scratch)
    res = pl.pallas_call(carrier, name=name, out_shape=out_shape + c_shapes, input_output_aliases=io_aliases,
                         compiler_params=params, **spec)(*prefetch, *args, *c_ins)
    main = list(res[:n_out])
    return (main[0] if single else main), (list(res[n_out:]) if comm else None)


def _run_comm(comm, *, name):
    c_in, c_out = len(comm.ins), len(comm.out_shapes)

    def body(*refs):
        ci, co, cs = refs[:c_in], refs[c_in:c_in + c_out], refs[c_in + c_out:]
        comm.start(ci, co, cs)
        comm.finish(ci, co, cs)

    return list(pl.pallas_call(
        body, name=name, in_specs=[ANY] * c_in, out_specs=[ANY] * c_out, out_shape=comm.out_shapes,
        scratch_shapes=[pltpu.SemaphoreType.DMA((k,)) for k in comm.sems],
        input_output_aliases=comm.aliases)(*comm.ins))


def _chunks(width, step=512):
    out, c0 = [], 0
    while c0 < width:
        cw = min(step, width - c0)
        out.append((c0, cw))
        c0 += cw
    return out


def _split2(v):
    hi = v.astype(BF16)
    lo = (v - hi.astype(F32)).astype(BF16)
    return hi, lo


def _split3(v):
    hi = v.astype(BF16)
    r = v - hi.astype(F32)
    mid = r.astype(BF16)
    lo = (r - mid.astype(F32)).astype(BF16)
    return hi, mid, lo


def _dot(a, b):
    return jnp.dot(a, b, preferred_element_type=F32)


def _dot_nt(a, b):
    return lax.dot_general(a, b, NT_DIMS, preferred_element_type=F32)


def _dot_tn(a, b):
    return lax.dot_general(a, b, TN_DIMS, preferred_element_type=F32)


def _silu_mul(g, u):
    return g * jax.nn.sigmoid(g) * u


def _rms_bwd(dn, h, gain, dres):
    r = lax.rsqrt(jnp.mean(h * h, axis=-1, keepdims=True) + EPS)
    y = h * r
    dgain = jnp.sum(dn * y, axis=0, keepdims=True)
    dy = dn * gain
    dh = dres + r * (dy - y * jnp.mean(dy * y, axis=-1, keepdims=True))
    return dh, dgain


def _group_mean(v, p):
    hi, lo = _split2(v)
    return _dot(hi, p) + _dot(lo, p)


def _row_of(a, k):
    rows = lax.broadcasted_iota(jnp.int32, a.shape, 0)
    return jnp.sum(jnp.where(rows == k, a, 0.0), axis=0, keepdims=True)


def _causal_conv(u, prev, w):
    rows = lax.broadcasted_iota(jnp.int32, u.shape, 0)
    p1 = _row_of(prev, HALO - 1)
    p2 = _row_of(prev, HALO - 2)
    u1 = jnp.where(rows == 0, p1, pltpu.roll(u, 1, 0))
    u2 = jnp.where(rows == 0, p2, jnp.where(rows == 1, p1, pltpu.roll(u, 2, 0)))
    return w[2:3, :] * u + w[1:2, :] * u1 + w[0:1, :] * u2, u1, u2


def _rms(x, gain):
    return (x * lax.rsqrt(jnp.mean(x * x, axis=-1, keepdims=True) + EPS) * gain).astype(BF16)


def _embed_norm(x, meta, g, *, tm, name, comm=None):
    B, S, D = x.shape
    L = S + N_META
    per_seq = L // tm
    nt = B * per_seq
    body_rows = tm - N_META

    def body(meta_ref, g_ref, x_hbm, h_ref, n_ref, buf, sems):
        i = pl.program_id(0)

        def fetch(k, fn):
            slot, b, t = k % 2, k // per_seq, k % per_seq

            @pl.when(t == 0)
            def _():
                fn(pltpu.make_async_copy(x_hbm.at[b, pl.ds(0, body_rows)],
                                         buf.at[slot, pl.ds(N_META, body_rows)], sems.at[slot]))

            @pl.when(t != 0)
            def _():
                fn(pltpu.make_async_copy(x_hbm.at[b, pl.ds(pl.multiple_of(t * tm - N_META, 8), tm)],
                                         buf.at[slot], sems.at[slot]))

        @pl.when(i == 0)
        def _():
            fetch(i, lambda cp: cp.start())

        @pl.when(i + 1 < nt)
        def _():
            fetch(i + 1, lambda cp: cp.start())

        fetch(i, lambda cp: cp.wait())
        slot = i % 2

        @pl.when(i % per_seq == 0)
        def _():
            buf[slot, 0:N_META, :] = meta_ref[...]

        hv = buf[slot]
        h_ref[...] = hv
        n_ref[...] = _rms(hv, g_ref[...])

    row = pl.BlockSpec((tm, D), lambda i: (i, 0))
    return _launch(
        body, name=name, grid=(nt,),
        in_specs=[pl.BlockSpec((N_META, D), lambda i: (0, 0)), pl.BlockSpec((1, D), lambda i: (0, 0)), ANY],
        out_specs=[row, row],
        out_shape=[jax.ShapeDtypeStruct((B * L, D), F32), jax.ShapeDtypeStruct((B * L, D), BF16)],
        scratch_shapes=[pltpu.VMEM((2, tm, D), F32), pltpu.SemaphoreType.DMA((2,))],
        args=(meta, g, x), comm=comm)


def _ffn_up(n, wgu, sid, gu_prev, *, tm, first, count, name, tiles=None, comm=None):
    T, D = n.shape
    ns, _, guc = wgu.shape
    ff = N_SHARD * guc // 2
    t0, nt = tiles if tiles else (0, T // tm)

    def body(sid_ref, x_ref, w_ref, *rest):
        rest[-1][...] = _dot(x_ref[...], w_ref[...]).astype(BF16)

    where = lambda s, sid: sid[first + s]
    w_at = (lambda s, sid: 0) if ns == 1 else where
    return _launch(
        body, name=name, grid=(count, nt), prefetch=(sid,),
        in_specs=[pl.BlockSpec((tm, D), lambda s, i, sid: (t0 + i, 0)),
                  pl.BlockSpec((None, D, guc), lambda s, i, sid: (w_at(s, sid), 0, 0))]
                 + ([] if gu_prev is None else [ANY]),
        out_specs=pl.BlockSpec((None, tm, guc), lambda s, i, sid: (where(s, sid) // 2, t0 + i, where(s, sid) % 2)),
        out_shape=jax.ShapeDtypeStruct((2, T, ff), BF16),
        args=(n, wgu) + (() if gu_prev is None else (gu_prev,)),
        aliases=None if gu_prev is None else {2: 0}, comm=comm)


def _matmul_nn(x, w, *, tm, nb, w_spec, out_shape, out_spec, name, comm=None):
    T, K = x.shape

    def body(x_ref, w_ref, o_ref):
        o_ref[...] = _dot(x_ref[...], w_ref[...]).astype(o_ref.dtype)

    return _launch(
        body, name=name, grid=(nb, T // tm),
        in_specs=[pl.BlockSpec((tm, K), lambda s, i: (i, 0)), w_spec],
        out_specs=out_spec, out_shape=out_shape, args=(x, w), comm=comm)


def _ffn_down(gu, wd, h, next_gain, *, tm, name, comm=None):
    _, T, ff = gu.shape
    D = h.shape[1]
    chunks = _chunks(ff)

    def body(g_ref, u_ref, wd_hbm, h_ref, ng_ref, o_ref, n_ref, wd_v, a_v, sem):
        @pl.when(pl.program_id(0) == 0)
        def _():
            cp = pltpu.make_async_copy(wd_hbm, wd_v, sem)
            cp.start()
            cp.wait()

        for c0, cw in chunks:
            a = _silu_mul(g_ref[:, c0:c0 + cw].astype(F32), u_ref[:, c0:c0 + cw].astype(F32))
            a_v[:, c0:c0 + cw] = a.astype(BF16)
        acc = _dot(a_v[...], wd_v[...])
        out = h_ref[...] + 0.5 * acc
        o_ref[...] = out
        n_ref[...] = _rms(out, ng_ref[...])

    return _launch(
        body, name=name, grid=(T // tm,),
        in_specs=[pl.BlockSpec((None, tm, ff), lambda i: (0, i, 0)),
                  pl.BlockSpec((None, tm, ff), lambda i: (1, i, 0)),
                  ANY,
                  pl.BlockSpec((tm, D), lambda i: (i, 0)),
                  pl.BlockSpec((1, D), lambda i: (0, 0))],
        out_specs=[pl.BlockSpec((tm, D), lambda i: (i, 0)), pl.BlockSpec((tm, D), lambda i: (i, 0))],
        out_shape=[jax.ShapeDtypeStruct((T, D), F32), jax.ShapeDtypeStruct((T, D), BF16)],
        scratch_shapes=[pltpu.VMEM((ff, D), BF16), pltpu.VMEM((tm, ff), BF16), pltpu.SemaphoreType.DMA],
        args=(gu, gu, wd, h, next_gain), comm=comm)


def _ffn_down_loss(gu, wd, h, gf, tgt, *, tm, name, comm=None):
    _, T, ff = gu.shape
    D = h.shape[1]
    B, S, _ = tgt.shape
    per_seq = (S + N_META) // tm
    body_rows = tm - N_META
    chunks = _chunks(ff)

    def body(g_ref, u_ref, wd_hbm, h_ref, gf_ref, tgt_hbm, dh_ref, dhb_ref, dg_ref, loss_ref, wd_v, a_v, tg_v, sem,
             tsem):
        i = pl.program_id(0)
        b, t = i // per_seq, i % per_seq

        @pl.when(i == 0)
        def _():
            cp = pltpu.make_async_copy(wd_hbm, wd_v, sem)
            cp.start()
            cp.wait()
            dg_ref[...] = jnp.zeros_like(dg_ref)
            loss_ref[...] = jnp.zeros_like(loss_ref)
            tg_v[0:N_META, :] = jnp.zeros((N_META, D), F32)

        def fetch(fn):
            @pl.when(t == 0)
            def _():
                fn(pltpu.make_async_copy(tgt_hbm.at[b, pl.ds(0, body_rows)], tg_v.at[pl.ds(N_META, body_rows)], tsem))

            @pl.when(t != 0)
            def _():
                fn(pltpu.make_async_copy(tgt_hbm.at[b, pl.ds(pl.multiple_of(t * tm - N_META, 8), tm)], tg_v, tsem))

        fetch(lambda cp: cp.start())
        for c0, cw in chunks:
            a = _silu_mul(g_ref[:, c0:c0 + cw].astype(F32), u_ref[:, c0:c0 + cw].astype(F32))
            a_v[:, c0:c0 + cw] = a.astype(BF16)
        acc = _dot(a_v[...], wd_v[...])
        x = h_ref[...] + 0.5 * acc
        fetch(lambda cp: cp.wait())

        gain = gf_ref[...]
        r = lax.rsqrt(jnp.mean(x * x, axis=-1, keepdims=True) + EPS)
        y = x * r
        pos = t * tm + lax.broadcasted_iota(jnp.int32, (tm, 1), 0)
        err = jnp.where(pos >= N_META, y * gain - tg_v[...], 0.0)
        loss_ref[...] += 0.5 * jnp.sum(jnp.mean(err * err, axis=-1, keepdims=True))
        dout = err / D
        dg_ref[...] += jnp.sum(dout * y, axis=0, keepdims=True)
        dy = dout * gain
        dh = r * (dy - y * jnp.mean(dy * y, axis=-1, keepdims=True))
        dh_ref[...] = dh
        dhb_ref[...] = (0.5 * dh).astype(BF16)

    row = pl.BlockSpec((tm, D), lambda i: (i, 0))
    const = lambda i: (0, 0)
    return _launch(
        body, name=name, grid=(T // tm,),
        in_specs=[pl.BlockSpec((None, tm, ff), lambda i: (0, i, 0)),
                  pl.BlockSpec((None, tm, ff), lambda i: (1, i, 0)),
                  ANY, row, pl.BlockSpec((1, D), const), ANY],
        out_specs=[row, row, pl.BlockSpec((1, D), const), pl.BlockSpec((1, LANES), const)],
        out_shape=[jax.ShapeDtypeStruct((T, D), F32), jax.ShapeDtypeStruct((T, D), BF16),
                   jax.ShapeDtypeStruct((1, D), F32), jax.ShapeDtypeStruct((1, LANES), F32)],
        scratch_shapes=[pltpu.VMEM((ff, D), BF16), pltpu.VMEM((tm, ff), BF16), pltpu.VMEM((tm, D), F32),
                        pltpu.SemaphoreType.DMA, pltpu.SemaphoreType.DMA],
        args=(gu, gu, wd, h, gf, tgt), comm=comm)


def _ffn_bwd_act(df, gu, wd, *, tm, guc, name, comm=None):
    _, T, ff = gu.shape
    D = df.shape[1]
    nj = ff // guc
    chunks = _chunks(guc)

    def body(df_ref, g_ref, u_ref, wd_ref, o_ref, dwd_ref):
        @pl.when(pl.program_id(1) == 0)
        def _():
            dwd_ref[...] = jnp.zeros_like(dwd_ref)

        dfv = df_ref[...]
        for c0, cw in chunks:
            da = _dot_nt(dfv, wd_ref[c0:c0 + cw, :])
            g = g_ref[:, c0:c0 + cw].astype(F32)
            u = u_ref[:, c0:c0 + cw].astype(F32)
            sg = jax.nn.sigmoid(g)
            silu = g * sg
            o_ref[0, :, c0:c0 + cw] = (da * u * (sg * (1.0 + g * (1.0 - sg)))).astype(BF16)
            o_ref[1, :, c0:c0 + cw] = (da * silu).astype(BF16)
            dwd_ref[c0:c0 + cw, :] += _dot_tn((silu * u).astype(BF16), dfv)

    return _launch(
        body, name=name, grid=(nj, T // tm),
        in_specs=[pl.BlockSpec((tm, D), lambda j, i: (i, 0)),
                  pl.BlockSpec((None, tm, guc), lambda j, i: (0, i, j)),
                  pl.BlockSpec((None, tm, guc), lambda j, i: (1, i, j)),
                  pl.BlockSpec((guc, D), lambda j, i: (j, 0))],
        out_specs=[pl.BlockSpec((2, tm, guc), lambda j, i: (0, i, j)), pl.BlockSpec((guc, D), lambda j, i: (j, 0))],
        out_shape=[jax.ShapeDtypeStruct((2, T, ff), BF16), jax.ShapeDtypeStruct((ff, D), F32)],
        args=(df, gu, gu, wd), comm=comm)


def _ffn_bwd_in(dgu, wgu, h, g, dres, *, tm, scale, name, comm=None):
    _, T, ff = dgu.shape
    ns, D, guc = wgu.shape
    nj = ff // guc
    chunks = _chunks(guc)

    def body(dgu_ref, w_hbm, h_ref, g_ref, dres_ref, dh_ref, dhb_ref, dg_ref, w_v, acc, sem):
        i, j = pl.program_id(0), pl.program_id(1)

        @pl.when((i == 0) & (j == 0))
        def _():
            cp = pltpu.make_async_copy(w_hbm, w_v, sem)
            cp.start()
            cp.wait()
            dg_ref[...] = jnp.zeros_like(dg_ref)

        part = _dot_nt(dgu_ref[0], w_v[j]) + _dot_nt(dgu_ref[1], w_v[nj + j])

        @pl.when(j == 0)
        def _():
            acc[...] = part

        @pl.when(j > 0)
        def _():
            acc[...] += part

        @pl.when(j == nj - 1)
        def _():
            dh, dgain = _rms_bwd(acc[...], h_ref[...], g_ref[...], dres_ref[...])
            dh_ref[...] = dh
            dhb_ref[...] = (scale * dh).astype(BF16)
            dg_ref[...] += dgain

    return _launch(
        body, name=name, grid=(T // tm, nj),
        in_specs=[pl.BlockSpec((2, tm, guc), lambda i, j: (0, i, j)),
                  ANY,
                  pl.BlockSpec((tm, D), lambda i, j: (i, 0)),
                  pl.BlockSpec((1, D), lambda i, j: (0, 0)),
                  pl.BlockSpec((tm, D), lambda i, j: (i, 0))],
        out_specs=[pl.BlockSpec((tm, D), lambda i, j: (i, 0)),
                   pl.BlockSpec((tm, D), lambda i, j: (i, 0)),
                   pl.BlockSpec((1, D), lambda i, j: (0, 0))],
        out_shape=[jax.ShapeDtypeStruct((T, D), F32), jax.ShapeDtypeStruct((T, D), BF16),
                   jax.ShapeDtypeStruct((1, D), F32)],
        scratch_shapes=[pltpu.VMEM((ns, D, guc), BF16), pltpu.VMEM((tm, D), F32), pltpu.SemaphoreType.DMA],
        args=(dgu, wgu, h, g, dres), comm=comm)


def _ffn_bwd_in_first(dgu, wgu, h, g, dres, *, tm, batch, name, comm=None):
    _, T, ff = dgu.shape
    ns, D, guc = wgu.shape
    nj = ff // guc
    nt = T // tm
    L = T // batch
    per_seq = L // tm
    body_rows = tm - N_META
    chunks = _chunks(guc)

    def body(dgu_ref, w_hbm, h_ref, g_ref, dres_ref, dx_hbm, dmeta_ref, dg_ref, w_v, acc, dh_v, sem, osem):
        i, j = pl.program_id(0), pl.program_id(1)

        @pl.when((i == 0) & (j == 0))
        def _():
            cp = pltpu.make_async_copy(w_hbm, w_v, sem)
            cp.start()
            cp.wait()
            dg_ref[...] = jnp.zeros_like(dg_ref)
            dmeta_ref[...] = jnp.zeros_like(dmeta_ref)

        part = _dot_nt(dgu_ref[0], w_v[j]) + _dot_nt(dgu_ref[1], w_v[nj + j])

        @pl.when(j == 0)
        def _():
            acc[...] = part

        @pl.when(j > 0)
        def _():
            acc[...] += part

        def head_copy(b):
            return pltpu.make_async_copy(dh_v.at[pl.ds(N_META, body_rows)], dx_hbm.at[b, pl.ds(0, body_rows)], osem)

        def tail_copy(b, t):
            return pltpu.make_async_copy(dh_v, dx_hbm.at[b, pl.ds(pl.multiple_of(t * tm - N_META, 8), tm)], osem)

        def on_tile(k, head_fn, tail_fn):
            @pl.when(k % per_seq == 0)
            def _():
                head_fn(head_copy(k // per_seq))

            @pl.when(k % per_seq != 0)
            def _():
                tail_fn(tail_copy(k // per_seq, k % per_seq))

        @pl.when(j == nj - 1)
        def _():
            dh, dgain = _rms_bwd(acc[...], h_ref[...], g_ref[...], dres_ref[...])
            dg_ref[...] += dgain

            @pl.when(i > 0)
            def _():
                on_tile(i - 1, lambda cp: cp.wait(), lambda cp: cp.wait())

            dh_v[...] = dh

            @pl.when(i % per_seq == 0)
            def _():
                dmeta_ref[...] += dh[0:N_META, :]

            on_tile(i, lambda cp: cp.start(), lambda cp: cp.start())

            @pl.when(i == nt - 1)
            def _():
                on_tile(i, lambda cp: cp.wait(), lambda cp: cp.wait())

    return _launch(
        body, name=name, grid=(nt, nj),
        in_specs=[pl.BlockSpec((2, tm, guc), lambda i, j: (0, i, j)),
                  ANY,
                  pl.BlockSpec((tm, D), lambda i, j: (i, 0)),
                  pl.BlockSpec((1, D), lambda i, j: (0, 0)),
                  pl.BlockSpec((tm, D), lambda i, j: (i, 0))],
        out_specs=[ANY, pl.BlockSpec((N_META, D), lambda i, j: (0, 0)), pl.BlockSpec((1, D), lambda i, j: (0, 0))],
        out_shape=[jax.ShapeDtypeStruct((batch, L - N_META, D), F32), jax.ShapeDtypeStruct((N_META, D), F32),
                   jax.ShapeDtypeStruct((1, D), F32)],
        scratch_shapes=[pltpu.VMEM((ns, D, guc), BF16), pltpu.VMEM((tm, D), F32), pltpu.VMEM((tm, D), F32),
                        pltpu.SemaphoreType.DMA, pltpu.SemaphoreType.DMA],
        args=(dgu, wgu, h, g, dres), comm=comm)


def _mix_bwd_in(parts, w_main, w_fg, h, g, dres, *, tm, scale, name, comm=None):
    T, D = h.shape
    widths = [p.shape[1] for p in parts]
    offs = [sum(widths[:k]) for k in range(len(widths))]
    npart = len(parts)
    wide = sum(widths)

    def body(*refs):
        p_refs = refs[:npart]
        wm_ref, wf_ref, h_ref, g_ref, dres_ref, dh_ref, dhb_ref, dg_ref, all_ref = refs[npart:]

        @pl.when(pl.program_id(0) == 0)
        def _():
            dg_ref[...] = jnp.zeros_like(dg_ref)

        for p_ref, off, wd_ in zip(p_refs, offs, widths):
            for c0, cw in _chunks(wd_):
                all_ref[:, off + c0:off + c0 + cw] = p_ref[:, c0:c0 + cw].astype(BF16)
        n_main = offs[-1]
        dn = _dot_nt(all_ref[:, :n_main], wm_ref[...]) + _dot_nt(all_ref[:, n_main:], wf_ref[...])
        dh, dgain = _rms_bwd(dn, h_ref[...], g_ref[...], dres_ref[...])
        dh_ref[...] = dh
        dhb_ref[...] = (scale * dh).astype(BF16)
        dg_ref[...] += dgain

    row = lambda i: (i, 0)
    const = lambda i: (0, 0)
    return _launch(
        body, name=name, grid=(T // tm,),
        in_specs=[pl.BlockSpec((tm, p.shape[1]), row) for p in parts]
                 + [pl.BlockSpec(w_main.shape, const), pl.BlockSpec(w_fg.shape, const),
                    pl.BlockSpec((tm, D), row), pl.BlockSpec((1, D), const), pl.BlockSpec((tm, D), row)],
        out_specs=[pl.BlockSpec((tm, D), row), pl.BlockSpec((tm, D), row), pl.BlockSpec((1, D), const),
                   pl.BlockSpec((tm, wide), row)],
        out_shape=[jax.ShapeDtypeStruct((T, D), F32), jax.ShapeDtypeStruct((T, D), BF16),
                   jax.ShapeDtypeStruct((1, D), F32), jax.ShapeDtypeStruct((T, wide), BF16)],
        args=(*parts, w_main, w_fg, h, g, dres), comm=comm)


def _matmul_tn(x, y, *, tm, nb, x_spec, y_spec, out_shape, out_spec, kb, name, comm=None):
    T = y.shape[-2]
    chunks = _chunks(kb)

    def body(x_ref, y_ref, o_ref):
        @pl.when(pl.program_id(1) == 0)
        def _():
            o_ref[...] = jnp.zeros_like(o_ref)

        yv = y_ref[...].astype(BF16)
        for c0, cw in chunks:
            o_ref[c0:c0 + cw, :] += _dot_tn(x_ref[:, c0:c0 + cw], yv)

    return _launch(
        body, name=name, grid=(nb, T // tm),
        in_specs=[x_spec, y_spec], out_specs=out_spec, out_shape=out_shape, args=(x, y), comm=comm)


def _tri(n, lower):
    r = lax.broadcasted_iota(jnp.int32, (n, n), 0)
    c = lax.broadcasted_iota(jnp.int32, (n, n), 1)
    return jnp.where((r >= c) if lower else (r <= c), 1.0, 0.0).astype(BF16)


def _tri_dot(tri, v):
    hi, mid, lo = _split3(v)
    return _dot(tri, hi) + _dot(tri, mid) + _dot(tri, lo)


def _fcum(fg, bf, *, ch, name):
    B, L, W = fg.shape
    nch = L // ch

    def body(fg_ref, bf_ref, f_ref):
        tri = _tri(ch, True)
        carry = jnp.zeros((1, W), F32)
        for c in range(nch):
            x = fg_ref[c * ch:(c + 1) * ch, :] + bf_ref[...]
            lf = jnp.minimum(x, 0.0) - jnp.log(1.0 + jnp.exp(-jnp.abs(x)))
            f_ref[c * ch:(c + 1) * ch, :] = _tri_dot(tri, lf) + carry
            carry = carry + jnp.sum(lf, axis=0, keepdims=True)

    return pl.pallas_call(
        body, name=name, grid=(B,),
        in_specs=[pl.BlockSpec((None, L, W), lambda b: (b, 0, 0)), pl.BlockSpec((1, W), lambda b: (0, 0))],
        out_specs=pl.BlockSpec((None, L, W), lambda b: (b, 0, 0)),
        out_shape=jax.ShapeDtypeStruct((B, L, W), F32),
        compiler_params=_params("arbitrary"),
    )(fg, bf)


def _fcum_bwd(dF, fg, bf, *, ch, name):
    B, L, W = fg.shape
    nch = L // ch

    def body(df_ref, fg_ref, bf_ref, dfg_ref, db_ref):
        @pl.when(pl.program_id(0) == 0)
        def _():
            db_ref[...] = jnp.zeros_like(db_ref)

        tri = _tri(ch, False)
        carry = jnp.zeros((1, W), F32)
        dbs = jnp.zeros((1, W), F32)
        for c in reversed(range(nch)):
            d = df_ref[c * ch:(c + 1) * ch, :]
            dlf = _tri_dot(tri, d) + carry
            carry = carry + jnp.sum(d, axis=0, keepdims=True)
            x = fg_ref[c * ch:(c + 1) * ch, :] + bf_ref[...]
            dfg = dlf * jax.nn.sigmoid(-x)
            dfg_ref[c * ch:(c + 1) * ch, :] = dfg.astype(BF16)
            dbs = dbs + jnp.sum(dfg, axis=0, keepdims=True)
        db_ref[...] += dbs

    blk = pl.BlockSpec((None, L, W), lambda b: (b, 0, 0))
    return pl.pallas_call(
        body, name=name, grid=(B,),
        in_specs=[blk, blk, pl.BlockSpec((1, W), lambda b: (0, 0))],
        out_specs=[blk, pl.BlockSpec((1, W), lambda b: (0, 0))],
        out_shape=[jax.ShapeDtypeStruct((B, L, W), BF16), jax.ShapeDtypeStruct((1, W), F32)],
        compiler_params=_params("arbitrary"),
    )(dF, fg, bf)


def _band_edges(tq):
    return sorted({min(tq, (k * tq // ATTN_BANDS + HALO - 1) // HALO * HALO) for k in range(ATTN_BANDS + 1)})


def _pair(h):
    return slice((h // 2) * 2 * HEAD_DIM, (h // 2 + 1) * 2 * HEAD_DIM)


def _own_lanes(a, h):
    low = lax.broadcasted_iota(jnp.int32, a.shape, 1) < HEAD_DIM
    return jnp.where(low if h % 2 == 0 else jnp.logical_not(low), a, jnp.zeros_like(a))


def _sum_lane(h):
    return HEAD_DIM if h % 2 == 0 else 0


def _own_lanes_and_ones(a, h):
    lane = lax.broadcasted_iota(jnp.int32, a.shape, 1)
    low = lane < HEAD_DIM
    return jnp.where(low if h % 2 == 0 else jnp.logical_not(low), a,
                     jnp.where(lane == _sum_lane(h), jnp.ones_like(a), jnp.zeros_like(a)))


def _attn_fwd(proj, fc, fr, *, tq, n_heads, name, comm=None):
    B, L, _ = proj.shape
    AD = n_heads * HEAD_DIM
    nq = L // tq
    W = fc.shape[-1]
    scale = HEAD_DIM ** -0.5
    edges = _band_edges(tq)

    v_ones, sum_lane = _own_lanes_and_ones, _sum_lane

    def body(q_ref, k_ref, v_ref, fr_ref, o_ref, lse_ref, m_s, acc_s):
        qi, ki = pl.program_id(1), pl.program_id(2)

        @pl.when(ki == 0)
        def _():
            m_s[...] = jnp.full_like(m_s, NEG)
            acc_s[...] = jnp.zeros_like(acc_s)

        def tile(diagonal):
            lane = lax.broadcasted_iota(jnp.int32, (tq, W), 1)
            m_all = m_s[...]
            m_out = m_all
            bands = [(r0, r1, r1 if diagonal else tq) for r0, r1 in zip(edges[:-1], edges[1:])]
            if diagonal:
                masks = {r0: (lax.broadcasted_iota(jnp.int32, (r1 - r0, c1), 1)
                              <= r0 + lax.broadcasted_iota(jnp.int32, (r1 - r0, c1), 0)) for r0, r1, c1 in bands}

            def scores(h, band):
                r0, r1, c1 = band
                sl = slice(h * HEAD_DIM, (h + 1) * HEAD_DIM)
                return _dot_nt(q_ref[r0:r1, sl] * scale, k_ref[0:c1, sl])

            work = [(h, band) for h in range(n_heads) for band in bands]
            nxt = scores(*work[0])
            for w, (h, band) in enumerate(work):
                r0, r1, c1 = band
                sl = slice(h * HEAD_DIM, (h + 1) * HEAD_DIM)
                s = nxt - fr_ref[h:h + 1, 0:c1]
                if w + 1 < len(work):
                    nxt = scores(*work[w + 1])
                if diagonal:
                    s = jnp.where(masks[r0], s, NEG)
                m_old = m_all[r0:r1, h:h + 1]
                m_new = jnp.maximum(m_old, jnp.max(s, axis=1, keepdims=True))
                alpha = jnp.exp(m_old - m_new)
                p = jnp.exp(s - m_new)
                own = slice(h * 2 * HEAD_DIM, (h + 1) * 2 * HEAD_DIM)
                acc_s[r0:r1, own] = alpha * acc_s[r0:r1, own] + _dot(p.astype(BF16), v_ones(v_ref[0:c1, _pair(h)], h))
                if r0 == 0:
                    m_parts = []
                m_parts.append(m_new)
                if r1 == tq:
                    m_out = jnp.where(lane == h, jnp.concatenate(m_parts, axis=0), m_out)
            m_s[...] = m_out

        @pl.when(ki < qi)
        def _():
            tile(False)

        @pl.when(ki == qi)
        def _():
            tile(True)
            lane = lax.broadcasted_iota(jnp.int32, (tq, W), 1)
            low = lax.broadcasted_iota(jnp.int32, (tq, 2 * HEAD_DIM), 1) < HEAD_DIM
            l_all = jnp.ones((tq, W), F32)
            for h in range(0, n_heads, 2):
                even = acc_s[:, h * 2 * HEAD_DIM:(h + 1) * 2 * HEAD_DIM]
                odd = acc_s[:, (h + 1) * 2 * HEAD_DIM:(h + 2) * 2 * HEAD_DIM]
                l_even = even[:, sum_lane(h):sum_lane(h) + 1]
                l_odd = odd[:, sum_lane(h + 1):sum_lane(h + 1) + 1]
                o_ref[:, _pair(h)] = jnp.where(low, even / l_even, odd / l_odd)
                l_all = jnp.where(lane == h, l_even, jnp.where(lane == h + 1, l_odd, l_all))
            lse_ref[...] = jnp.where(lane < n_heads, m_s[...] + jnp.log(l_all), 0.0)

    kv = lambda b, qi, ki: jnp.minimum(ki, qi)
    return _launch(
        body, name=name, grid=(B, nq, nq), args=(proj, proj, proj, fr), comm=comm,
        in_specs=[pl.BlockSpec((None, tq, AD), lambda b, qi, ki: (b, qi, 3)),
                  pl.BlockSpec((None, tq, AD), lambda b, qi, ki: (b, kv(b, qi, ki), 4)),
                  pl.BlockSpec((None, tq, AD), lambda b, qi, ki: (b, kv(b, qi, ki), 5)),
                  pl.BlockSpec((None, None, n_heads, tq), lambda b, qi, ki: (b, kv(b, qi, ki), 0, 0))],
        out_specs=[pl.BlockSpec((None, tq, AD), lambda b, qi, ki: (b, qi, 0)),
                   pl.BlockSpec((None, tq, W), lambda b, qi, ki: (b, qi, 0))],
        out_shape=[jax.ShapeDtypeStruct((B, L, AD), F32), jax.ShapeDtypeStruct((B, L, W), F32)],
        scratch_shapes=[pltpu.VMEM((tq, W), F32), pltpu.VMEM((tq, n_heads * 2 * HEAD_DIM), F32)])


def _attn_bwd(proj, o, do, lse, fc, fr, *, tq, n_heads, name, comm=None):
    B, L, _ = proj.shape
    AD = n_heads * HEAD_DIM
    nq = L // tq
    W = fc.shape[-1]
    HW = 2 * HEAD_DIM
    scale = HEAD_DIM ** -0.5
    edges = _band_edges(tq)

    def body(q_ref, k_ref, v_ref, o_ref, do_ref, lse_ref, fr_ref,
             dq_ref, dk_ref, dv_ref, dfk_ref, dfq_ref, dq_s, dk_s, dv_s):
        kj, qi = pl.program_id(1), pl.program_id(2)

        @pl.when((kj == 0) & (qi == 0))
        def _():
            dq_s[...] = jnp.zeros_like(dq_s)

        @pl.when(qi == kj)
        def _():
            dk_s[...] = jnp.zeros_like(dk_s)
            dv_s[...] = jnp.zeros_like(dv_s)

        def tile(diagonal):
            bands = [(r0, r1, r1) for r0, r1 in zip(edges[:-1], edges[1:])] if diagonal else [(0, tq, tq)]
            lse = lse_ref[...]
            for r0, r1, c1 in bands:
                nr = r1 - r0
                rows = pl.ds(pl.multiple_of(qi * tq + r0, 8), nr)
                if diagonal:
                    mask = (lax.broadcasted_iota(jnp.int32, (nr, c1), 1)
                            <= r0 + lax.broadcasted_iota(jnp.int32, (nr, c1), 0))
                for h in range(n_heads):
                    ps = _pair(h)
                    own = slice(h * HW, (h + 1) * HW)
                    k, v = k_ref[0:c1, ps], v_ref[0:c1, ps]
                    qs = q_ref[r0:r1, ps] * scale
                    dov = _own_lanes(do_ref[r0:r1, ps], h)
                    s = _dot_nt(_own_lanes(qs, h), k) - fr_ref[h:h + 1, 0:c1]
                    if diagonal:
                        s = jnp.where(mask, s, NEG)
                    p = jnp.exp(s - lse[r0:r1, h:h + 1])
                    dp = _dot_nt(dov, v)
                    dsum = jnp.sum(dov.astype(F32) * o_ref[r0:r1, ps], axis=1, keepdims=True)
                    dsb = (p * (dp - dsum)).astype(BF16)
                    dv = _dot_tn(p.astype(BF16), dov)
                    dk_s[0:c1, own] += _dot_tn(dsb, _own_lanes_and_ones(qs, h))
                    dq_s[rows, own] += _dot(dsb, _own_lanes_and_ones(k, h))
                    if h % 2 == 0:
                        dv_even = dv
                    else:
                        dv_s[0:c1, ps] += dv_even + dv

        def compact(acc, data_scale):
            rows = acc.shape[0]
            low = lax.broadcasted_iota(jnp.int32, (rows, HW), 1) < HEAD_DIM
            lane = lax.broadcasted_iota(jnp.int32, (rows, W), 1)
            vals, sums = [], jnp.zeros((rows, W), F32)
            for h in range(0, n_heads, 2):
                even, odd = acc[:, h * HW:(h + 1) * HW], acc[:, (h + 1) * HW:(h + 2) * HW]
                vals.append(jnp.where(low, even, odd) * data_scale)
                sums = jnp.where(lane == h, even[:, _sum_lane(h):_sum_lane(h) + 1],
                                 jnp.where(lane == h + 1, odd[:, _sum_lane(h + 1):_sum_lane(h + 1) + 1], sums))
            return vals, sums

        @pl.when(qi > kj)
        def _():
            tile(False)

        @pl.when(qi == kj)
        def _():
            tile(True)
            rows = pl.ds(pl.multiple_of(qi * tq, 8), tq)
            vals, sums = compact(dq_s[rows, :], scale)
            for h in range(0, n_heads, 2):
                dq_ref[rows, _pair(h)] = vals[h // 2]
            dfq_ref[rows, :] = sums

        @pl.when(qi == nq - 1)
        def _():
            vals, sums = compact(dk_s[...], 1.0)
            for h in range(0, n_heads, 2):
                dk_ref[:, _pair(h)] = vals[h // 2].astype(BF16)
            dfk_ref[...] = sums
            dv_ref[...] = dv_s[...].astype(BF16)

    qq = lambda b, kj, qi: jnp.maximum(qi, kj)
    qblk = lambda w, cb: pl.BlockSpec((None, tq, w), lambda b, kj, qi: (b, qq(b, kj, qi), cb))
    kblk = lambda w, cb: pl.BlockSpec((None, tq, w), lambda b, kj, qi: (b, kj, cb))
    return _launch(
        body, name=name, grid=(B, nq, nq), args=(proj, proj, proj, o, do, lse, fr), comm=comm,
        in_specs=[qblk(AD, 3), kblk(AD, 4), kblk(AD, 5), qblk(AD, 0), qblk(AD, 0), qblk(W, 0),
                  pl.BlockSpec((None, None, n_heads, tq), lambda b, kj, qi: (b, kj, 0, 0))],
        out_specs=[pl.BlockSpec((None, L, AD), lambda b, kj, qi: (b, 0, 0)),
                   kblk(AD, 0), kblk(AD, 0), kblk(W, 0),
                   pl.BlockSpec((None, L, W), lambda b, kj, qi: (b, 0, 0))],
        out_shape=[jax.ShapeDtypeStruct((B, L, AD), F32), jax.ShapeDtypeStruct((B, L, AD), BF16),
                   jax.ShapeDtypeStruct((B, L, AD), BF16), jax.ShapeDtypeStruct((B, L, W), F32),
                   jax.ShapeDtypeStruct((B, L, W), F32)],
        scratch_shapes=[pltpu.VMEM((L, n_heads * HW), F32), pltpu.VMEM((tq, n_heads * HW), F32),
                        pltpu.VMEM((tq, AD), F32)])


def _mix_gather(refs, first):
    b_ref, c_ref, hc_ref, cp_ref, hcp_ref, o_ref, cw_ref, p_ref = refs
    bg = b_ref[...].astype(F32)
    u = c_ref[...].astype(F32) * hc_ref[...].astype(F32)
    prev = cp_ref[...].astype(F32) * hcp_ref[...].astype(F32)
    prev = jnp.where(first, 0.0, prev)
    cv, u1, u2 = _causal_conv(u, prev, cw_ref[...])
    yc = bg * cv
    p = p_ref[...]
    rc = lax.rsqrt(_group_mean(yc * yc, p) + EPS)
    ya = o_ref[...].astype(F32)
    ra = lax.rsqrt(_group_mean(ya * ya, p) + EPS)
    return bg, (u, u1, u2), cv, yc * rc, rc, ya * ra, ra


def _mix_specs(tm, CD, D, grid_rank_fn):
    per = tm // HALO
    cur = lambda cb: pl.BlockSpec((None, tm, CD), lambda b, i: (b, i, cb))
    prev = lambda cb: pl.BlockSpec((None, HALO, CD), lambda b, i: (b, jnp.maximum(i * per - 1, 0), cb))
    return [cur(0), cur(1), cur(2), prev(1), prev(2), cur(0)]


def _mix_out(proj, o, cw, gc, ga, wout, h, pmat, next_gain, *, tm, name, comm=None):
    B, L, D = h.shape
    CD = o.shape[-1]
    const = lambda b, i: (0, 0)

    def body(b_ref, c_ref, hc_ref, cp_ref, hcp_ref, o_ref, cw_ref, p_ref, gc_ref, ga_ref, w_ref, h_ref, ng_ref,
             out_ref, y_ref, n_ref):
        first = pl.program_id(1) == 0
        _, _, _, zc, _, za, _ = _mix_gather((b_ref, c_ref, hc_ref, cp_ref, hcp_ref, o_ref, cw_ref, p_ref), first)
        yc = (zc * gc_ref[...]).astype(BF16)
        ya = (za * ga_ref[...]).astype(BF16)
        y_ref[:, :CD] = yc
        y_ref[:, CD:] = ya
        out = h_ref[...] + _dot(yc, w_ref[:CD, :]) + _dot(ya, w_ref[CD:, :])
        out_ref[...] = out
        n_ref[...] = _rms(out, ng_ref[...])

    tile = pl.BlockSpec((None, tm, D), lambda b, i: (b, i, 0))
    return _launch(
        body, name=name, grid=(B, L // tm),
        in_specs=_mix_specs(tm, CD, D, None)
                 + [pl.BlockSpec(cw.shape, const), pl.BlockSpec(pmat.shape, const),
                    pl.BlockSpec((1, CD), const), pl.BlockSpec((1, CD), const), pl.BlockSpec((D, D), const),
                    tile, pl.BlockSpec((1, D), const)],
        out_specs=[tile, tile, tile],
        out_shape=[jax.ShapeDtypeStruct((B, L, D), F32), jax.ShapeDtypeStruct((B, L, D), BF16),
                   jax.ShapeDtypeStruct((B, L, D), BF16)],
        args=(proj, proj, proj, proj, proj, o, cw, pmat, gc, ga, wout, h, next_gain), comm=comm)


def _mix_out_bwd(dhb, proj, o, cw, gc, ga, wout, pmat, *, tm, name, comm=None):
    B, L, D = dhb.shape
    CD = o.shape[-1]
    const = lambda b, i: (0, 0)

    def body(dh_ref, b_ref, c_ref, hc_ref, cp_ref, hcp_ref, o_ref, cw_ref, p_ref, gc_ref, ga_ref, w_ref,
             db_ref, dcv_ref, do_ref, dgc_ref, dga_ref, dcw_ref):
        first = pl.program_id(1) == 0

        @pl.when((pl.program_id(0) == 0) & first)
        def _():
            dgc_ref[...] = jnp.zeros_like(dgc_ref)
            dga_ref[...] = jnp.zeros_like(dga_ref)
            dcw_ref[...] = jnp.zeros_like(dcw_ref)

        bg, us, cv, zc, rc, za, ra = _mix_gather(
            (b_ref, c_ref, hc_ref, cp_ref, hcp_ref, o_ref, cw_ref, p_ref), first)
        p = p_ref[...]
        dh = dh_ref[...]
        dyc = _dot_nt(dh, w_ref[:CD, :])
        dya = _dot_nt(dh, w_ref[CD:, :])

        dgc_ref[...] += jnp.sum(dyc * zc, axis=0, keepdims=True)
        dz = dyc * gc_ref[...]
        dx = rc * (dz - zc * _group_mean(dz * zc, p))
        db_ref[...] = (dx * cv).astype(BF16)
        dcv = dx * bg
        dcv_ref[...] = dcv.astype(BF16)
        for k in range(3):
            dcw_ref[k:k + 1, :] += jnp.sum(dcv * us[2 - k], axis=0, keepdims=True)

        dga_ref[...] += jnp.sum(dya * za, axis=0, keepdims=True)
        dz = dya * ga_ref[...]
        do_ref[...] = (ra * (dz - za * _group_mean(dz * za, p))).astype(BF16)

    tile = lambda w: pl.BlockSpec((None, tm, w), lambda b, i: (b, i, 0))
    return _launch(
        body, name=name, grid=(B, L // tm), comm=comm,
        args=(dhb, proj, proj, proj, proj, proj, o, cw, pmat, gc, ga, wout),
        in_specs=[tile(D)] + _mix_specs(tm, CD, D, None)
                 + [pl.BlockSpec(cw.shape, const), pl.BlockSpec(pmat.shape, const),
                    pl.BlockSpec((1, CD), const), pl.BlockSpec((1, CD), const), pl.BlockSpec((D, D), const)],
        out_specs=[tile(CD), tile(CD), tile(CD),
                   pl.BlockSpec((1, CD), const), pl.BlockSpec((1, CD), const), pl.BlockSpec((8, CD), const)],
        out_shape=[jax.ShapeDtypeStruct((B, L, CD), BF16)] * 3
                  + [jax.ShapeDtypeStruct((1, CD), F32)] * 2 + [jax.ShapeDtypeStruct((8, CD), F32)])


def _conv_bwd(dcv, proj, cw, *, tm, name):
    B, L, CD = dcv.shape
    per = tm // HALO
    nhalo = L // HALO
    nt = L // tm

    def body(d_ref, dn_ref, c_ref, hc_ref, cw_ref, out_ref):
        last = pl.program_id(1) == nt - 1
        d = d_ref[...].astype(F32)
        nxt = jnp.where(last, 0.0, dn_ref[...].astype(F32))
        n0, n1 = _row_of(nxt, 0), _row_of(nxt, 1)
        rows = lax.broadcasted_iota(jnp.int32, d.shape, 0)
        d1 = jnp.where(rows == tm - 1, n0, pltpu.roll(d, tm - 1, 0))
        d2 = jnp.where(rows == tm - 2, n0, jnp.where(rows == tm - 1, n1, pltpu.roll(d, tm - 2, 0)))
        w = cw_ref[...]
        du = w[2:3, :] * d + w[1:2, :] * d1 + w[0:1, :] * d2
        out_ref[:, :CD] = (du * hc_ref[...].astype(F32)).astype(BF16)
        out_ref[:, CD:] = (du * c_ref[...].astype(F32)).astype(BF16)

    return pl.pallas_call(
        body, name=name, grid=(B, nt),
        in_specs=[pl.BlockSpec((None, tm, CD), lambda b, i: (b, i, 0)),
                  pl.BlockSpec((None, HALO, CD), lambda b, i: (b, jnp.minimum((i + 1) * per, nhalo - 1), 0)),
                  pl.BlockSpec((None, tm, CD), lambda b, i: (b, i, 1)),
                  pl.BlockSpec((None, tm, CD), lambda b, i: (b, i, 2)),
                  pl.BlockSpec(cw.shape, lambda b, i: (0, 0))],
        out_specs=pl.BlockSpec((None, tm, 2 * CD), lambda b, i: (b, i, 0)),
        out_shape=jax.ShapeDtypeStruct((B, L, 2 * CD), BF16),
        compiler_params=_params("arbitrary", "arbitrary"),
    )(dcv, dcv, proj, proj, cw)


def _place():
    x, y, c = lax.axis_index("x"), lax.axis_index("y"), lax.axis_index("c")
    others = [(1 - x, y), (x, 1 - y), (1 - x, 1 - y)]
    return x, y, c, others


def _all_gather_shards(shards, *, name):
    n = len(shards)

    def body(*refs):
        ins, outs = refs[:n], refs[n:2 * n]
        send, recv, fsend, frecv, lsem = refs[2 * n:]
        x, y, c, others = _place()
        me = 2 * x + y
        local = [pltpu.make_async_copy(ins[t], outs[t].at[me], lsem.at[t]) for t in range(n)]
        for cp in local:
            cp.start()

        def half(t, k):
            hr = shards[t].shape[0] // 2
            return pl.ds(pl.multiple_of(k * hr, HALO), hr)

        def ici(t, j, src_chip, to):
            src = ins[t].at[half(t, c)] if to is not None else outs[t].at[src_chip, half(t, c)]
            return pltpu.make_async_remote_copy(
                src_ref=src, dst_ref=outs[t].at[src_chip, half(t, c)],
                send_sem=send.at[3 * t + j], recv_sem=recv.at[3 * t + j],
                device_id=(x, y, c) if to is None else to, device_id_type=MESH)

        def d2d(t, j, src_chip, k):
            return pltpu.make_async_remote_copy(
                src_ref=outs[t].at[src_chip, half(t, k)], dst_ref=outs[t].at[src_chip, half(t, k)],
                send_sem=fsend.at[3 * t + j], recv_sem=frecv.at[3 * t + j],
                device_id=(x, y, 1 - c), device_id_type=MESH)

        firsts = [ici(t, j, me, (ox, oy, c)) for t in range(n) for j, (ox, oy) in enumerate(others)]
        for cp in firsts:
            cp.start()
        passed = []
        for t in range(n):
            for j, (ox, oy) in enumerate(others):
                ici(t, j, 2 * ox + oy, None).wait_recv()
                cp = d2d(t, j, 2 * ox + oy, c)
                cp.start()
                passed.append(cp)
        for t in range(n):
            for j, (ox, oy) in enumerate(others):
                d2d(t, j, 2 * ox + oy, 1 - c).wait_recv()
        for cp in firsts + passed:
            cp.wait_send()
        for cp in local:
            cp.wait()

    return pl.pallas_call(
        body, name=name,
        in_specs=[ANY] * n, out_specs=[ANY] * n,
        out_shape=[jax.ShapeDtypeStruct((N_SHARD,) + s.shape, s.dtype) for s in shards],
        scratch_shapes=[pltpu.SemaphoreType.DMA((3 * n,))] * 4 + [pltpu.SemaphoreType.DMA((n,))],
    )(*shards)


def _all_reduce_small(slab, *, name):
    def body(in_ref, out_ref, gath, send, recv):
        x, y, c, _ = _place()
        me = 4 * x + 2 * y + c
        gath[me] = in_ref[...]
        copies, peers = [], []
        for m in range(1, N_DEV):
            px = jnp.where((m >> 2) & 1, 1 - x, x)
            py = jnp.where((m >> 1) & 1, 1 - y, y)
            pc = jnp.where(m & 1, 1 - c, c)
            cp = pltpu.make_async_remote_copy(
                src_ref=in_ref, dst_ref=gath.at[me], send_sem=send.at[m - 1], recv_sem=recv.at[m - 1],
                device_id=(px, py, pc), device_id_type=MESH)
            cp.start()
            copies.append(cp)
            peers.append(4 * px + 2 * py + pc)
        for m in range(1, N_DEV):
            pltpu.make_async_remote_copy(
                src_ref=in_ref, dst_ref=gath.at[peers[m - 1]], send_sem=send.at[m - 1], recv_sem=recv.at[m - 1],
                device_id=(x, y, c), device_id_type=MESH).wait_recv()
        for cp in copies:
            cp.wait_send()
        acc = gath[0]
        for k in range(1, N_DEV):
            acc = acc + gath[k]
        out_ref[...] = acc

    vm = pl.BlockSpec(memory_space=pltpu.VMEM)
    return pl.pallas_call(
        body, name=name, in_specs=[vm], out_specs=vm,
        out_shape=jax.ShapeDtypeStruct(slab.shape, slab.dtype),
        scratch_shapes=[pltpu.VMEM((N_DEV,) + slab.shape, slab.dtype),
                        pltpu.SemaphoreType.DMA((N_DEV - 1,)), pltpu.SemaphoreType.DMA((N_DEV - 1,))],
    )(slab)


def _gather_stage(shards, into, *, ici=(), d2d=()):
    n = len(shards) if into is None else len(into)
    ns = len(shards) if ici else 0
    ni, nd = max(len(ici), 1), max(len(d2d), 1)
    shapes = [s.shape for s in shards] if into is None else [p.shape[1:] for p in into]
    dtypes = [s.dtype for s in shards] if into is None else [p.dtype for p in into]

    def copies(ins, outs, sems, sending):
        x, y, c, others = _place()
        me = 2 * x + y
        out = []
        for t in range(n):
            hr = shapes[t][0] // 2
            mine = pl.ds(pl.multiple_of(c * hr, HALO), hr)
            theirs = pl.ds(pl.multiple_of((1 - c) * hr, HALO), hr)
            for a, j in enumerate(ici):
                ox, oy = others[j]
                src_chip = me if sending else 2 * ox + oy
                out.append(pltpu.make_async_remote_copy(
                    src_ref=ins[t].at[mine], dst_ref=outs[t].at[src_chip, mine],
                    send_sem=sems[0].at[ni * t + a], recv_sem=sems[1].at[ni * t + a],
                    device_id=(ox, oy, c) if sending else (x, y, c), device_id_type=MESH))
            for a, j in enumerate(d2d):
                ox, oy = others[j]
                blk = outs[t].at[2 * ox + oy, mine if sending else theirs]
                out.append(pltpu.make_async_remote_copy(
                    src_ref=blk, dst_ref=blk, send_sem=sems[2].at[nd * t + a], recv_sem=sems[3].at[nd * t + a],
                    device_id=(x, y, 1 - c) if sending else (x, y, c), device_id_type=MESH))
        return out

    def local(ins, outs, sems):
        if into is not None:
            return []
        x, y, _, _ = _place()
        return [pltpu.make_async_copy(ins[t], outs[t].at[2 * x + y], sems[4].at[t]) for t in range(n)]

    def start(ins, outs, sems):
        for cp in local(ins, outs, sems) + copies(ins, outs, sems, True):
            cp.start()

    def finish(ins, outs, sems):
        for cp in copies(ins, outs, sems, False):
            cp.wait_recv()
        for cp in copies(ins, outs, sems, True):
            cp.wait_send()
        for cp in local(ins, outs, sems):
            cp.wait()

    return _Comm((list(shards) if ici or into is None else []) + (list(into) if into is not None else []),
                 [jax.ShapeDtypeStruct((N_SHARD,) + tuple(sh), dt) for sh, dt in zip(shapes, dtypes)],
                 [ni * n, ni * n, nd * n, nd * n, n], start, finish,
                 aliases=None if into is None else {ns + t: t for t in range(n)})


def _gather_ici(shards):
    return _gather_stage(shards, None, ici=(0, 1, 2))


def _gather_d2d(parts):
    return _gather_stage((), parts, d2d=(0, 1, 2))


def _swap_halves(grads):
    n = len(grads)

    def copies(ins, outs, sems):
        x, y, c, _ = _place()
        out = []
        for t in range(n):
            hr = grads[t].shape[1] // 2
            rows = pl.ds(pl.multiple_of((1 - c) * hr, 8), hr)
            out.append(pltpu.make_async_remote_copy(
                src_ref=ins[t].at[:, rows, :], dst_ref=outs[t], send_sem=sems[0].at[t], recv_sem=sems[1].at[t],
                device_id=(x, y, 1 - c), device_id_type=MESH))
        return out

    def start(ins, outs, sems):
        for cp in copies(ins, outs, sems):
            cp.start()

    def finish(ins, outs, sems):
        for cp in copies(ins, outs, sems):
            cp.wait()

    return _Comm(grads, [jax.ShapeDtypeStruct((N_SHARD, g.shape[1] // 2, g.shape[2]), g.dtype) for g in grads],
                 [n, n], start, finish)


def _pair_sum(g, got, c, *, name):
    ns, R, C = g.shape
    hr = R // 2

    def body(c_ref, g_ref, r_ref, o_ref):
        o_ref[...] = (g_ref[...] + r_ref[...]).astype(BF16)

    return pl.pallas_call(
        body, name=name,
        grid_spec=pltpu.PrefetchScalarGridSpec(
            num_scalar_prefetch=1, grid=(ns,),
            in_specs=[pl.BlockSpec((None, hr, C), lambda s, cr: (s, cr[0], 0)),
                      pl.BlockSpec((None, hr, C), lambda s, cr: (s, 0, 0))],
            out_specs=pl.BlockSpec((None, hr, C), lambda s, cr: (s, 0, 0))),
        out_shape=jax.ShapeDtypeStruct((ns, hr, C), BF16),
        compiler_params=_params("arbitrary"),
    )(c, g, got)


def _scatter_chips(sums):
    n = len(sums)

    def copies(ins, outs, sems, sending):
        x, y, c, others = _place()
        me = 2 * x + y
        out = []
        for t in range(n):
            for j, (ox, oy) in enumerate(others):
                there = 2 * ox + oy
                out.append(pltpu.make_async_remote_copy(
                    src_ref=ins[t].at[there if sending else me], dst_ref=outs[t].at[me if sending else there],
                    send_sem=sems[0].at[3 * t + j], recv_sem=sems[1].at[3 * t + j],
                    device_id=(ox, oy, c) if sending else (x, y, c), device_id_type=MESH))
        return out

    def start(ins, outs, sems):
        for cp in copies(ins, outs, sems, True):
            cp.start()

    def finish(ins, outs, sems):
        for cp in copies(ins, outs, sems, False):
            cp.wait_recv()
        for cp in copies(ins, outs, sems, True):
            cp.wait_send()

    return _Comm(sums, [jax.ShapeDtypeStruct(s.shape, s.dtype) for s in sums], [3 * n, 3 * n], start, finish)


def _chip_sum(g, got, landed, idx, *, name):
    ns, R, C = g.shape
    hr = R // 2

    def body(i_ref, g_ref, r_ref, a_ref, b_ref, c_ref, o_ref):
        acc = g_ref[...] + r_ref[...]
        for ref in (a_ref, b_ref, c_ref):
            acc = acc + ref[...].astype(F32)
        o_ref[...] = acc

    other = lambda k: pl.BlockSpec((None, hr, C), lambda s, ir: (ir[2 + k], 0, 0))
    return pl.pallas_call(
        body, name=name,
        grid_spec=pltpu.PrefetchScalarGridSpec(
            num_scalar_prefetch=1, grid=(1,),
            in_specs=[pl.BlockSpec((None, hr, C), lambda s, ir: (ir[0], ir[1], 0)),
                      pl.BlockSpec((None, hr, C), lambda s, ir: (ir[0], 0, 0)),
                      other(0), other(1), other(2)],
            out_specs=pl.BlockSpec((hr, C), lambda s, ir: (ir[1], 0))),
        out_shape=jax.ShapeDtypeStruct((R, C), F32),
        compiler_params=_params("arbitrary"),
    )(idx, g, got, landed, landed, landed)


def _share_halves(halves):
    n = len(halves)

    def copies(outs, sems, sending):
        x, y, c, _ = _place()
        out = []
        for t in range(n):
            hr = halves[t].shape[0] // 2
            rows = pl.ds(pl.multiple_of((c if sending else 1 - c) * hr, 8), hr)
            out.append(pltpu.make_async_remote_copy(
                src_ref=outs[t].at[rows, :], dst_ref=outs[t].at[rows, :], send_sem=sems[0].at[t],
                recv_sem=sems[1].at[t], device_id=(x, y, 1 - c) if sending else (x, y, c), device_id_type=MESH))
        return out

    def start(ins, outs, sems):
        for cp in copies(outs, sems, True):
            cp.start()

    def finish(ins, outs, sems):
        for cp in copies(outs, sems, False):
            cp.wait_recv()
        for cp in copies(outs, sems, True):
            cp.wait_send()

    return _Comm(halves, [jax.ShapeDtypeStruct(h.shape, h.dtype) for h in halves], [n, n], start, finish,
                 aliases={t: t for t in range(n)})


def _adamw(w, gs, m, v, *, name):
    R, C = w.shape
    parts = len(gs)
    tr = R // parts
    for cand in (256, 128, 64, 32, 16, 8):
        if (R // parts) % cand == 0:
            tr = cand
            break
    per = R // parts // tr

    def body(*refs):
        g_refs = refs[:parts]
        w_ref, m_ref, v_ref, go_ref, d_ref, mo_ref, vo_ref = refs[parts:]
        i = pl.program_id(0)

        def update(gv):
            go_ref[...] = gv
            mn = ADAM_B1 * m_ref[...] + (1.0 - ADAM_B1) * gv
            vn = ADAM_B2 * v_ref[...] + (1.0 - ADAM_B2) * (gv * gv)
            m_hat = mn / (1.0 - ADAM_B1 ** ADAM_STEP)
            v_hat = vn / (1.0 - ADAM_B2 ** ADAM_STEP)
            d_ref[...] = -ADAM_LR * (m_hat / (jnp.sqrt(v_hat) + ADAM_EPS) + ADAM_WD * w_ref[...])
            mo_ref[...] = mn
            vo_ref[...] = vn

        for k in range(parts):
            @pl.when((i >= k * per) & (i < (k + 1) * per))
            def _():
                update(g_refs[k][...])

    blk = pl.BlockSpec((tr, C), lambda i: (i, 0))
    g_blk = lambda k: pl.BlockSpec((tr, C), lambda i: (jnp.clip(i - k * per, 0, per - 1), 0))
    return pl.pallas_call(
        body, name=name, grid=(R // tr,), in_specs=[g_blk(k) for k in range(parts)] + [blk] * 3,
        out_specs=[blk] * 4, out_shape=[jax.ShapeDtypeStruct((R, C), F32)] * 4,
        compiler_params=_params("arbitrary"),
    )(*gs, w, m, v)


def _pack_small(D, meta, n1, nm, n3, nf, gc, ga, bf, cw):
    def row(a):
        a = a.reshape(-1, a.shape[-1])
        return jnp.pad(a, ((0, 0), (0, D - a.shape[-1])))
    rows = [row(meta), row(n1), row(nm), row(n3), row(nf), row(jnp.concatenate([gc, ga], axis=-1)), row(bf), row(cw)]
    slab = jnp.concatenate(rows, axis=0)
    return jnp.pad(slab, ((0, SMALL_ROWS - slab.shape[0]), (0, 0)))


def _unpack_small(slab, like):
    meta, n1, nm, n3, nf, gc, ga, bf, cw = like
    nmeta, mc = meta.shape
    out = [slab[:nmeta, :mc].reshape(meta.shape)]
    r = nmeta
    for a in (n1, nm, n3, nf):
        out.append(slab[r, :a.shape[-1]].reshape(a.shape))
        r += 1
    cd = gc.shape[-1]
    out.append(slab[r, :cd].reshape(gc.shape))
    out.append(slab[r, cd:cd + ga.shape[-1]].reshape(ga.shape))
    r += 1
    out.append(slab[r, :bf.shape[-1]].reshape(bf.shape))
    r += 1
    out.append(slab[r:r + 3, :cw.shape[-1]].reshape(cw.shape))
    return out


def kernel(x, meta_tokens, ffn1_norm, ffn1_w_gu, ffn1_w_down, mix_norm, w_in, conv_w, b_f, out_norm_conv, out_norm_attn, w_out, ffn2_norm, ffn2_w_gu, ffn2_w_down, final_norm, loss_target, m_meta_tokens, m_ffn1_norm, m_ffn1_w_gu, m_ffn1_w_down, m_mix_norm, m_w_in, m_conv_w, m_b_f, m_out_norm_conv, m_out_norm_attn, m_w_out, m_ffn2_norm, m_ffn2_w_gu, m_ffn2_w_down, m_final_norm, v_meta_tokens, v_ffn1_norm, v_ffn1_w_gu, v_ffn1_w_down, v_mix_norm, v_w_in, v_conv_w, v_b_f, v_out_norm_conv, v_out_norm_attn, v_w_out, v_ffn2_norm, v_ffn2_w_gu, v_ffn2_w_down, v_final_norm):
    B, S, D = x.shape
    L = S + N_META
    T = B * L
    tm = L // 3
    assert tm * 3 == L and tm % HALO == 0
    guc = ffn1_w_gu.shape[-1]
    ff = N_SHARD * guc // 2
    H = b_f.shape[-1]
    AD = H * HEAD_DIM
    CD = conv_w.shape[-1] * N_SHARD
    assert CD == AD and CD + AD == D and CD % LANES == 0
    n_main = 3 * CD + 3 * AD
    ins = w_in.shape[-1]

    xi, yi, ci = lax.axis_index("x"), lax.axis_index("y"), lax.axis_index("c")
    chip = 2 * xi + yi

    small_shard = jnp.zeros((2 * HALO, meta_tokens.shape[-1]), F32)
    small_shard = small_shard.at[:N_META].set(meta_tokens)
    small_shard = small_shard.at[N_META:N_META + 3, :conv_w.shape[-1]].set(conv_w[0])
    big = [ffn1_w_gu[0], ffn1_w_down[0], w_in[0], w_out[0], ffn2_w_gu[0], ffn2_w_down[0]]
    wgu1_s, wd1_s, win_s, wout_s, wgu2_s, wd2_s = [w.astype(BF16) for w in big]
    small_g, = _all_gather_shards([small_shard], name="gather_small")
    meta_f = jnp.moveaxis(small_g[:, :N_META], 0, 1).reshape(N_META, D)
    cw_f = jnp.moveaxis(small_g[:, N_META:N_META + 3, :conv_w.shape[-1]], 0, 1).reshape(3, CD)
    cw8 = jnp.pad(cw_f, ((0, 5), (0, 0)))
    bf_p = jnp.pad(b_f, ((0, 0), (0, LANES - H)))
    gid = jnp.arange(CD) // HEAD_DIM
    pmat = jnp.where(gid[:, None] == gid[None, :], 1.0 / HEAD_DIM, 0.0).astype(BF16)

    gu_shape = jax.ShapeDtypeStruct((2, T, ff), BF16)
    gu_w_spec = pl.BlockSpec((None, D, guc), lambda s, i: (s, 0, 0))
    gu_o_spec = pl.BlockSpec((None, tm, guc), lambda s, i: (s // 2, i, s % 2))

    sid = jnp.bitwise_xor(chip, jnp.array([0, 2, 1, 3], jnp.int32)).astype(jnp.int32)
    (h0, n1), _ = _embed_norm(x, meta_f, ffn1_norm, tm=tm, name="embed_norm")
    half = (T // tm) // 2
    gu1, wgu1_h = _ffn_up(n1, wgu1_s[None], sid, None, tm=tm, first=0, count=1, tiles=(0, half),
                          name="ffn1_up_own_a", comm=_gather_stage([wgu1_s], None, ici=(0, 1)))
    gu1, wgu1_h = _ffn_up(n1, wgu1_s[None], sid, gu1, tm=tm, first=0, count=1, tiles=(half, T // tm - half),
                          name="ffn1_up_own_b", comm=_gather_stage([wgu1_s], wgu1_h, ici=(2,), d2d=(0, 1)))
    gu1, out = _ffn_up(n1, wgu1_h[0], sid, gu1, tm=tm, first=1, count=2, name="ffn1_up_near",
                       comm=_join(_gather_stage((), wgu1_h, d2d=(2,)), _gather_ici([wd1_s, wout_s])))
    wgu1, down_w = out[0], out[1:]
    gu1, (wd1, wout_g) = _ffn_up(n1, wgu1, sid, gu1, tm=tm, first=3, count=1, name="ffn1_up_far",
                                 comm=_gather_d2d(down_w))
    wd1 = wd1.reshape(ff, D)
    (h1, n2), win_h = _ffn_down(gu1, wd1, h0, mix_norm, tm=tm, name="ffn1_down", comm=_gather_ici([win_s]))
    win_g, = _run_comm(_gather_d2d(win_h), name="gather_w_in")
    wout_f = wout_g.reshape(D, D)
    win_f = jnp.moveaxis(win_g, 0, 1).reshape(D, N_SHARD * ins)
    win_main = win_f[:, :n_main]
    win_fg = jnp.pad(win_f[:, n_main:], ((0, 0), (0, LANES - H)))

    proj, _ = _matmul_nn(n2, win_main, tm=tm, nb=n_main // (3 * CD),
                         w_spec=pl.BlockSpec((D, 3 * CD), lambda s, i: (0, s)),
                         out_shape=jax.ShapeDtypeStruct((T, n_main), BF16),
                         out_spec=pl.BlockSpec((tm, 3 * CD), lambda s, i: (i, s)), name="mix_in")
    fg, _ = _matmul_nn(n2, win_fg, tm=tm, nb=1, w_spec=pl.BlockSpec((D, LANES), lambda s, i: (0, 0)),
                       out_shape=jax.ShapeDtypeStruct((T, LANES), F32),
                       out_spec=pl.BlockSpec((tm, LANES), lambda s, i: (i, 0)), name="mix_in_fg")
    proj3 = proj.reshape(B, L, n_main)
    fg3 = fg.reshape(B, L, LANES)
    fc = _fcum(fg3, bf_p, ch=tm, name="forget_cumsum")
    fr = fc[:, :, :H].reshape(B, L // tm, tm, H).transpose(0, 1, 3, 2)
    (o, lse), ffn2_w = _attn_fwd(proj3, fc, fr, tq=tm, n_heads=H, name="attn_fwd",
                                 comm=_gather_ici([wgu2_s, wd2_s]))
    (h2, ymix, n3), (wgu2, wd2) = _mix_out(
        proj3, o, cw8, out_norm_conv, out_norm_attn, wout_f, h1.reshape(B, L, D), pmat, ffn2_norm,
        tm=tm, name="mix_out", comm=_gather_d2d(ffn2_w))
    wd2 = wd2.reshape(ff, D)
    h2 = h2.reshape(T, D)
    n3 = n3.reshape(T, D)

    gu2, _ = _matmul_nn(n3, wgu2, tm=tm, nb=N_SHARD, w_spec=gu_w_spec, out_shape=gu_shape, out_spec=gu_o_spec,
                        name="ffn2_up")
    (dh3f, dh3b, d_gf, loss_part), _ = _ffn_down_loss(gu2, wd2, h2, final_norm.reshape(1, D), loss_target,
                                                      tm=tm, name="ffn2_down_loss")

    c_arr = jnp.reshape(ci, (1,)).astype(jnp.int32)
    ks = jnp.arange(N_SHARD - 1, dtype=jnp.int32)
    idx = jnp.concatenate([jnp.stack([chip, ci]).astype(jnp.int32), ks + (ks >= chip).astype(jnp.int32)])

    def pair_sums(grads, got, names):
        return [_pair_sum(g, r, c_arr, name="pair_sum_" + nm) for g, r, nm in zip(grads, got, names)]

    def chip_sums(grads, got, landed, names):
        return [_chip_sum(g, r, l, idx, name="chip_sum_" + nm) for g, r, l, nm in zip(grads, got, landed, names)]

    def dw_up(n, dgu, name, comm=None):
        return _matmul_tn(
            n, dgu, tm=tm, nb=N_SHARD, kb=D, x_spec=pl.BlockSpec((tm, D), lambda s, i: (i, 0)),
            y_spec=pl.BlockSpec((None, tm, guc), lambda s, i: (s // 2, i, s % 2)),
            out_shape=jax.ShapeDtypeStruct((N_SHARD, D, guc), F32),
            out_spec=pl.BlockSpec((None, D, guc), lambda s, i: (s, 0, 0)), name=name, comm=comm)

    (dgu2, d_wd2), _ = _ffn_bwd_act(dh3b, gu2, wd2, tm=tm, guc=guc, name="ffn2_bwd_act")
    (dh2, dh2b, d_g3), _ = _ffn_bwd_in(dgu2, wgu2, h2, ffn2_norm, dh3f, tm=tm, scale=1.0, name="ffn2_bwd_in")
    d_wgu2, _ = dw_up(n3, dgu2, "ffn2_dw_up")
    grads_f2 = [d_wgu2, d_wd2.reshape(N_SHARD, ff // N_SHARD, D)]
    names_f2 = ["wgu2", "wd2"]

    dh2b3 = dh2b.reshape(B, L, D)
    (d_bg, d_cv, d_o, d_gc, d_ga, d_cw), got_f2 = _mix_out_bwd(
        dh2b3, proj3, o, cw8, out_norm_conv, out_norm_attn, wout_f, pmat, tm=tm, name="mix_out_bwd",
        comm=_swap_halves(grads_f2))
    sums_f2 = pair_sums(grads_f2, got_f2, names_f2)
    d_wout, _ = _matmul_tn(
        ymix.reshape(T, D), dh2b, tm=tm, nb=1, kb=D,
        x_spec=pl.BlockSpec((tm, D), lambda s, i: (i, 0)), y_spec=pl.BlockSpec((tm, D), lambda s, i: (i, 0)),
        out_shape=jax.ShapeDtypeStruct((D, D), F32), out_spec=pl.BlockSpec((D, D), lambda s, i: (0, 0)),
        name="dw_out")
    d_cc = _conv_bwd(d_cv, proj3, cw8, tm=tm, name="conv_bwd")
    (d_q, d_k, d_v, d_fk, d_fq), landed_f2 = _attn_bwd(proj3, o, d_o, lse, fc, fr, tq=tm, n_heads=H, name="attn_bwd",
                                                       comm=_scatter_chips(sums_f2))
    halves_f2 = chip_sums(grads_f2, got_f2, landed_f2, names_f2)
    d_fc = d_fq - d_fk
    d_fg, d_bf = _fcum_bwd(d_fc, fg3, bf_p, ch=tm, name="forget_cumsum_bwd")

    parts = [d_bg.reshape(T, CD), d_cc.reshape(T, 2 * CD), d_q.reshape(T, AD), d_k.reshape(T, AD),
             d_v.reshape(T, AD), d_fg.reshape(T, LANES)]
    (dh1, dh1b, d_gm, d_proj), g_f2 = _mix_bwd_in(parts, win_main, win_fg, h1, mix_norm, dh2, tm=tm, scale=0.5,
                                                  name="mix_bwd_in", comm=_share_halves(halves_f2))
    wide = d_proj.shape[1]
    bw = next(c for c in (768, 640, 512, 384, 256, LANES) if wide % c == 0)
    d_win_nat, _ = _matmul_tn(
        n2, d_proj, tm=tm, nb=wide // bw, kb=D,
        x_spec=pl.BlockSpec((tm, D), lambda s, i: (i, 0)), y_spec=pl.BlockSpec((tm, bw), lambda s, i: (i, s)),
        out_shape=jax.ShapeDtypeStruct((D, wide), F32), out_spec=pl.BlockSpec((D, bw), lambda s, i: (0, s)),
        name="dw_in")
    d_win = jnp.moveaxis(d_win_nat[:, :N_SHARD * ins].reshape(D, N_SHARD, ins), 1, 0)
    grads_mx = [d_win, d_wout.reshape(N_SHARD, D // N_SHARD, D)]
    names_mx = ["win", "wout"]

    (dgu1, d_wd1), got_mx = _ffn_bwd_act(dh1b, gu1, wd1, tm=tm, guc=guc, name="ffn1_bwd_act",
                                         comm=_swap_halves(grads_mx))
    sums_mx = pair_sums(grads_mx, got_mx, names_mx)
    def dw_up1(part, name, comm):
        return _matmul_tn(
            n1, dgu1, tm=tm, nb=N_SHARD, kb=D // 2, x_spec=pl.BlockSpec((tm, D // 2), lambda s, i: (i, part)),
            y_spec=pl.BlockSpec((None, tm, guc), lambda s, i: (s // 2, i, s % 2)),
            out_shape=jax.ShapeDtypeStruct((N_SHARD, D // 2, guc), F32),
            out_spec=pl.BlockSpec((None, D // 2, guc), lambda s, i: (s, 0, 0)), name=name, comm=comm)

    grads_d1 = [d_wd1.reshape(N_SHARD, ff // N_SHARD, D)]
    d_ua, out = dw_up1(0, "ffn1_dw_up_a", _join(_scatter_chips(sums_mx), _swap_halves(grads_d1)))
    landed_mx, got_d1 = out[:2], out[2:]
    halves_mx = chip_sums(grads_mx, got_mx, landed_mx, names_mx)
    sums_d1 = pair_sums(grads_d1, got_d1, ["wd1"])
    d_ub, out = dw_up1(1, "ffn1_dw_up_b",
                       _join(_join(_share_halves(halves_mx), _scatter_chips(sums_d1)), _swap_halves([d_ua])))
    g_mx, landed_d1, got_ua = out[:2], out[2:3], out[3:]
    halves_d1 = chip_sums(grads_d1, got_d1, landed_d1, ["wd1"])
    sums_ua = pair_sums([d_ua], got_ua, ["wgu1a"])
    (grad_x, d_meta, d_g1), out = _ffn_bwd_in_first(
        dgu1, wgu1, h0, ffn1_norm, dh1, tm=tm, batch=B, name="ffn1_bwd_in",
        comm=_join(_join(_share_halves(halves_d1), _scatter_chips(sums_ua)), _swap_halves([d_ub])))
    g_d1, landed_ua, got_ub = out[:1], out[1:2], out[2:]
    halves_ua = chip_sums([d_ua], got_ua, landed_ua, ["wgu1a"])
    sums_ub = pair_sums([d_ub], got_ub, ["wgu1b"])
    out = _run_comm(_join(_share_halves(halves_ua), _scatter_chips(sums_ub)), name="scatter_ffn1")
    g_ua, landed_ub = out[:1], out[1:]
    halves_ub = chip_sums([d_ub], got_ub, landed_ub, ["wgu1b"])
    g_ub = _run_comm(_share_halves(halves_ub), name="share_ffn1")
    g_big = [[g_ua[0], g_ub[0]], [g_d1[0]], [g_mx[0]], [g_mx[1]], [g_f2[0]], [g_f2[1]]]

    loss_row = jnp.zeros((1, D), F32).at[0, 0].set(loss_part[0, 0])
    slab = _pack_small(D, d_meta, d_g1, d_gm, d_g3, d_gf, d_gc, d_ga, d_bf[:, :H], d_cw[:3])
    slab = slab.at[SMALL_ROWS - 1].set(loss_row[0])
    total = _all_reduce_small(slab, name="reduce_small")
    loss = total[SMALL_ROWS - 1, 0]
    mcols = meta_tokens.shape[-1]
    ccols = conv_w.shape[-1]
    full_like = (jnp.zeros((N_META, D)), ffn1_norm, mix_norm, ffn2_norm, final_norm.reshape(1, D), out_norm_conv,
                 out_norm_attn, b_f, jnp.zeros((1, 3, CD)))
    g_small = _unpack_small(total, full_like)
    g_small[0] = lax.dynamic_slice_in_dim(g_small[0], chip * mcols, mcols, axis=1)
    g_small[8] = lax.dynamic_slice_in_dim(g_small[8], chip * ccols, ccols, axis=2)

    def small_slab(meta, a1, am, a3, af, gc, ga, bf, cw):
        return _pack_small(D, meta, a1, am, a3, af.reshape(1, D), gc, ga, bf, cw[0])

    w_small = small_slab(meta_tokens, ffn1_norm, mix_norm, ffn2_norm, final_norm, out_norm_conv, out_norm_attn, b_f, conv_w)
    m_small = small_slab(m_meta_tokens, m_ffn1_norm, m_mix_norm, m_ffn2_norm, m_final_norm, m_out_norm_conv,
                         m_out_norm_attn, m_b_f, m_conv_w)
    v_small = small_slab(v_meta_tokens, v_ffn1_norm, v_mix_norm, v_ffn2_norm, v_final_norm, v_out_norm_conv,
                         v_out_norm_attn, v_b_f, v_conv_w)
    gs = list(g_small)
    gs[4] = gs[4].reshape(final_norm.shape)
    g_slab = small_slab(gs[0], gs[1], gs[2], gs[3], gs[4], gs[5], gs[6], gs[7], gs[8])
    local_like = (meta_tokens, ffn1_norm, mix_norm, ffn2_norm, final_norm.reshape(1, D), out_norm_conv, out_norm_attn,
                  b_f, conv_w)
    small_out = [_unpack_small(s, local_like)
                 for s in _adamw(w_small, [g_slab], m_small, v_small, name="adamw_small")[1:]]
    for lst in small_out:
        lst[4] = lst[4].reshape(final_norm.shape)

    names = ["wgu1", "wd1", "win", "wout", "wgu2", "wd2"]
    w_big = big
    m_big = [m_ffn1_w_gu[0], m_ffn1_w_down[0], m_w_in[0], m_w_out[0], m_ffn2_w_gu[0], m_ffn2_w_down[0]]
    v_big = [v_ffn1_w_gu[0], v_ffn1_w_down[0], v_w_in[0], v_w_out[0], v_ffn2_w_gu[0], v_ffn2_w_down[0]]
    big_out = [_adamw(w, g, m, v, name="adamw_" + nm) for w, g, m, v, nm in zip(w_big, g_big, m_big, v_big, names)]

    def assemble(small, bigs):
        meta, a1, am, a3, af, gc, ga, bf, cw = small
        gu1_, d1_, win_, wout_, gu2_, d2_ = [b[None] for b in bigs]
        return [meta, a1, gu1_, d1_, am, win_, cw, bf, gc, ga, wout_, a3, gu2_, d2_, af]

    gs_out = list(g_small)
    gs_out[4] = gs_out[4].reshape(final_norm.shape)
    grads_out = assemble(gs_out, [b[0] for b in big_out])
    delta_out = assemble(small_out[0], [b[1] for b in big_out])
    m_out = assemble(small_out[1], [b[2] for b in big_out])
    v_out = assemble(small_out[2], [b[3] for b in big_out])
    return (loss, grad_x, *grads_out, *delta_out, *m_out, *v_out)
```

```python
import functools

import jax
import jax.numpy as jnp
from jax import lax
from jax.experimental import pallas as pl
from jax.experimental.pallas import tpu as pltpu

F32 = jnp.float32
BF16 = jnp.bfloat16

EPS = 1e-6
N_META = 16
HEAD_DIM = 64
N_SHARD = 4
N_DEV = 8
HALO = 16
LANES = 128
SMALL_ROWS = 32
VMEM_LIMIT_V7X = 56 * 1024 * 1024
NEG = -1e30
ATTN_BANDS = 2

ADAM_LR = 0.001
ADAM_B1 = 0.9
ADAM_B2 = 0.999
ADAM_EPS = 1e-08
ADAM_WD = 0.01
ADAM_STEP = 10

MESH = pl.DeviceIdType.MESH
ANY = pl.BlockSpec(memory_space=pl.ANY)
NT_DIMS = (((1,), (1,)), ((), ()))
TN_DIMS = (((0,), (0,)), ((), ()))


def _params(*sem):
    return pltpu.CompilerParams(dimension_semantics=sem, vmem_limit_bytes=VMEM_LIMIT_V7X)


class _Comm:
    def __init__(self, ins, out_shapes, sems, start, finish, aliases=None):
        self.ins, self.out_shapes, self.sems = list(ins), list(out_shapes), list(sems)
        self.start, self.finish, self.aliases = start, finish, dict(aliases or {})


def _join(a, b):
    ni, no, ns = len(a.ins), len(a.out_shapes), len(a.sems)

    def start(ins, outs, sems):
        a.start(ins[:ni], outs[:no], sems[:ns])
        b.start(ins[ni:], outs[no:], sems[ns:])

    def finish(ins, outs, sems):
        a.finish(ins[:ni], outs[:no], sems[:ns])
        b.finish(ins[ni:], outs[no:], sems[ns:])

    aliases = dict(a.aliases)
    aliases.update({ni + i: no + j for i, j in b.aliases.items()})
    return _Comm(a.ins + b.ins, a.out_shapes + b.out_shapes, a.sems + b.sems, start, finish, aliases)


def _launch(body, *, name, grid, in_specs, out_specs, out_shape, args, scratch_shapes=(), comm=None, prefetch=(),
            aliases=None):
    single = not isinstance(out_shape, (list, tuple))
    out_specs = [out_specs] if single else list(out_specs)
    out_shape = [out_shape] if single else list(out_shape)
    in_specs, scratch_shapes, prefetch = list(in_specs), list(scratch_shapes), list(prefetch)
    params = _params(*(("arbitrary",) * len(grid)))
    n_pf, n_in, n_out, n_scr = len(prefetch), len(in_specs), len(out_specs), len(scratch_shapes)
    c_ins = comm.ins if comm else []
    c_shapes = comm.out_shapes if comm else []
    c_sems = comm.sems if comm else []
    c_in, c_out = len(c_ins), len(c_shapes)

    def carrier(*refs):
        p = 0
        pf = refs[p:p + n_pf]; p += n_pf
        a = refs[p:p + n_in]; p += n_in
        ci = refs[p:p + c_in]; p += c_in
        o = refs[p:p + n_out]; p += n_out
        co = refs[p:p + c_out]; p += c_out
        s = refs[p:p + n_scr]; p += n_scr
        cs = refs[p:]
        if comm:
            first = functools.reduce(lambda u, v: u & v, [pl.program_id(k) == 0 for k in range(len(grid))])

            @pl.when(first)
            def _():
                comm.start(ci, co, cs)

        body(*pf, *a, *o, *s)

        if comm:
            last = functools.reduce(lambda u, v: u & v, [pl.program_id(k) == grid[k] - 1 for k in range(len(grid))])

            @pl.when(last)
            def _():
                comm.finish(ci, co, cs)

    io_aliases = {n_pf + i: j for i, j in (aliases or {}).items()}
    if comm:
        io_aliases.update({n_pf + n_in + i: n_out + j for i, j in comm.aliases.items()})
    all_in, all_out = in_specs + [ANY] * c_in, out_specs + [ANY] * c_out
    all_scratch = scratch_shapes + [pltpu.SemaphoreType.DMA((k,)) for k in c_sems]
    if n_pf:
        spec = dict(grid_spec=pltpu.PrefetchScalarGridSpec(
            num_scalar_prefetch=n_pf, grid=grid, in_specs=all_in, out_specs=all_out, scratch_shapes=all_scratch))
    else:
        spec = dict(grid=grid, in_specs=all_in, out_specs=all_out, scratch_shapes=all_scratch)
    res = pl.pallas_call(carrier, name=name, out_shape=out_shape + c_shapes, input_output_aliases=io_aliases,
                         compiler_params=params, **spec)(*prefetch, *args, *c_ins)
    main = list(res[:n_out])
    return (main[0] if single else main), (list(res[n_out:]) if comm else None)


def _run_comm(comm, *, name):
    c_in, c_out = len(comm.ins), len(comm.out_shapes)

    def body(*refs):
        ci, co, cs = refs[:c_in], refs[c_in:c_in + c_out], refs[c_in + c_out:]
        comm.start(ci, co, cs)
        comm.finish(ci, co, cs)

    return list(pl.pallas_call(
        body, name=name, in_specs=[ANY] * c_in, out_specs=[ANY] * c_out, out_shape=comm.out_shapes,
        scratch_shapes=[pltpu.SemaphoreType.DMA((k,)) for k in comm.sems],
        input_output_aliases=comm.aliases)(*comm.ins))


def _chunks(width, step=512):
    out, c0 = [], 0
    while c0 < width:
        cw = min(step, width - c0)
        out.append((c0, cw))
        c0 += cw
    return out


def _split2(v):
    hi = v.astype(BF16)
    lo = (v - hi.astype(F32)).astype(BF16)
    return hi, lo


def _split3(v):
    hi = v.astype(BF16)
    r = v - hi.astype(F32)
    mid = r.astype(BF16)
    lo = (r - mid.astype(F32)).astype(BF16)
    return hi, mid, lo


def _dot(a, b):
    return jnp.dot(a, b, preferred_element_type=F32)


def _dot_nt(a, b):
    return lax.dot_general(a, b, NT_DIMS, preferred_element_type=F32)


def _dot_tn(a, b):
    return lax.dot_general(a, b, TN_DIMS, preferred_element_type=F32)


def _silu_mul(g, u):
    return g * jax.nn.sigmoid(g) * u


def _rms_bwd(dn, h, gain, dres):
    r = lax.rsqrt(jnp.mean(h * h, axis=-1, keepdims=True) + EPS)
    y = h * r
    dgain = jnp.sum(dn * y, axis=0, keepdims=True)
    dy = dn * gain
    dh = dres + r * (dy - y * jnp.mean(dy * y, axis=-1, keepdims=True))
    return dh, dgain


def _group_mean(v, p):
    hi, lo = _split2(v)
    return _dot(hi, p) + _dot(lo, p)


def _row_of(a, k):
    rows = lax.broadcasted_iota(jnp.int32, a.shape, 0)
    return jnp.sum(jnp.where(rows == k, a, 0.0), axis=0, keepdims=True)


def _causal_conv(u, prev, w):
    rows = lax.broadcasted_iota(jnp.int32, u.shape, 0)
    p1 = _row_of(prev, HALO - 1)
    p2 = _row_of(prev, HALO - 2)
    u1 = jnp.where(rows == 0, p1, pltpu.roll(u, 1, 0))
    u2 = jnp.where(rows == 0, p2, jnp.where(rows == 1, p1, pltpu.roll(u, 2, 0)))
    return w[2:3, :] * u + w[1:2, :] * u1 + w[0:1, :] * u2, u1, u2


def _rms(x, gain):
    return (x * lax.rsqrt(jnp.mean(x * x, axis=-1, keepdims=True) + EPS) * gain).astype(BF16)


def _embed_norm(x, meta, g, *, tm, name, comm=None):
    B, S, D = x.shape
    L = S + N_META
    per_seq = L // tm
    nt = B * per_seq
    body_rows = tm - N_META

    def body(meta_ref, g_ref, x_hbm, h_ref, n_ref, buf, sems):
        i = pl.program_id(0)

        def fetch(k, fn):
            slot, b, t = k % 2, k // per_seq, k % per_seq

            @pl.when(t == 0)
            def _():
                fn(pltpu.make_async_copy(x_hbm.at[b, pl.ds(0, body_rows)],
                                         buf.at[slot, pl.ds(N_META, body_rows)], sems.at[slot]))

            @pl.when(t != 0)
            def _():
                fn(pltpu.make_async_copy(x_hbm.at[b, pl.ds(pl.multiple_of(t * tm - N_META, 8), tm)],
                                         buf.at[slot], sems.at[slot]))

        @pl.when(i == 0)
        def _():
            fetch(i, lambda cp: cp.start())

        @pl.when(i + 1 < nt)
        def _():
            fetch(i + 1, lambda cp: cp.start())

        fetch(i, lambda cp: cp.wait())
        slot = i % 2

        @pl.when(i % per_seq == 0)
        def _():
            buf[slot, 0:N_META, :] = meta_ref[...]

        hv = buf[slot]
        h_ref[...] = hv
        n_ref[...] = _rms(hv, g_ref[...])

    row = pl.BlockSpec((tm, D), lambda i: (i, 0))
    return _launch(
        body, name=name, grid=(nt,),
        in_specs=[pl.BlockSpec((N_META, D), lambda i: (0, 0)), pl.BlockSpec((1, D), lambda i: (0, 0)), ANY],
        out_specs=[row, row],
        out_shape=[jax.ShapeDtypeStruct((B * L, D), F32), jax.ShapeDtypeStruct((B * L, D), BF16)],
        scratch_shapes=[pltpu.VMEM((2, tm, D), F32), pltpu.SemaphoreType.DMA((2,))],
        args=(meta, g, x), comm=comm)


def _ffn_up(n, wgu, sid, gu_prev, *, tm, first, count, name, comm=None):
    T, D = n.shape
    ns, _, guc = wgu.shape
    ff = N_SHARD * guc // 2

    def body(sid_ref, x_ref, w_ref, *rest):
        rest[-1][...] = _dot(x_ref[...], w_ref[...]).astype(BF16)

    where = lambda s, sid: sid[first + s]
    w_at = (lambda s, sid: 0) if ns == 1 else where
    return _launch(
        body, name=name, grid=(count, T // tm), prefetch=(sid,),
        in_specs=[pl.BlockSpec((tm, D), lambda s, i, sid: (i, 0)),
                  pl.BlockSpec((None, D, guc), lambda s, i, sid: (w_at(s, sid), 0, 0))]
                 + ([] if gu_prev is None else [ANY]),
        out_specs=pl.BlockSpec((None, tm, guc), lambda s, i, sid: (where(s, sid) // 2, i, where(s, sid) % 2)),
        out_shape=jax.ShapeDtypeStruct((2, T, ff), BF16),
        args=(n, wgu) + (() if gu_prev is None else (gu_prev,)),
        aliases=None if gu_prev is None else {2: 0}, comm=comm)


def _matmul_nn(x, w, *, tm, nb, w_spec, out_shape, out_spec, name, comm=None):
    T, K = x.shape

    def body(x_ref, w_ref, o_ref):
        o_ref[...] = _dot(x_ref[...], w_ref[...]).astype(o_ref.dtype)

    return _launch(
        body, name=name, grid=(nb, T // tm),
        in_specs=[pl.BlockSpec((tm, K), lambda s, i: (i, 0)), w_spec],
        out_specs=out_spec, out_shape=out_shape, args=(x, w), comm=comm)


def _ffn_down(gu, wd, h, next_gain, *, tm, name, comm=None):
    _, T, ff = gu.shape
    D = h.shape[1]
    chunks = _chunks(ff)

    def body(g_ref, u_ref, wd_hbm, h_ref, ng_ref, o_ref, n_ref, wd_v, a_v, sem):
        @pl.when(pl.program_id(0) == 0)
        def _():
            cp = pltpu.make_async_copy(wd_hbm, wd_v, sem)
            cp.start()
            cp.wait()

        for c0, cw in chunks:
            a = _silu_mul(g_ref[:, c0:c0 + cw].astype(F32), u_ref[:, c0:c0 + cw].astype(F32))
            a_v[:, c0:c0 + cw] = a.astype(BF16)
        acc = _dot(a_v[...], wd_v[...])
        out = h_ref[...] + 0.5 * acc
        o_ref[...] = out
        n_ref[...] = _rms(out, ng_ref[...])

    return _launch(
        body, name=name, grid=(T // tm,),
        in_specs=[pl.BlockSpec((None, tm, ff), lambda i: (0, i, 0)),
                  pl.BlockSpec((None, tm, ff), lambda i: (1, i, 0)),
                  ANY,
                  pl.BlockSpec((tm, D), lambda i: (i, 0)),
                  pl.BlockSpec((1, D), lambda i: (0, 0))],
        out_specs=[pl.BlockSpec((tm, D), lambda i: (i, 0)), pl.BlockSpec((tm, D), lambda i: (i, 0))],
        out_shape=[jax.ShapeDtypeStruct((T, D), F32), jax.ShapeDtypeStruct((T, D), BF16)],
        scratch_shapes=[pltpu.VMEM((ff, D), BF16), pltpu.VMEM((tm, ff), BF16), pltpu.SemaphoreType.DMA],
        args=(gu, gu, wd, h, next_gain), comm=comm)


def _ffn_down_loss(gu, wd, h, gf, tgt, *, tm, name, comm=None):
    _, T, ff = gu.shape
    D = h.shape[1]
    B, S, _ = tgt.shape
    per_seq = (S + N_META) // tm
    body_rows = tm - N_META
    chunks = _chunks(ff)

    def body(g_ref, u_ref, wd_hbm, h_ref, gf_ref, tgt_hbm, dh_ref, dhb_ref, dg_ref, loss_ref, wd_v, a_v, tg_v, sem,
             tsem):
        i = pl.program_id(0)
        b, t = i // per_seq, i % per_seq

        @pl.when(i == 0)
        def _():
            cp = pltpu.make_async_copy(wd_hbm, wd_v, sem)
            cp.start()
            cp.wait()
            dg_ref[...] = jnp.zeros_like(dg_ref)
            loss_ref[...] = jnp.zeros_like(loss_ref)
            tg_v[0:N_META, :] = jnp.zeros((N_META, D), F32)

        def fetch(fn):
            @pl.when(t == 0)
            def _():
                fn(pltpu.make_async_copy(tgt_hbm.at[b, pl.ds(0, body_rows)], tg_v.at[pl.ds(N_META, body_rows)], tsem))

            @pl.when(t != 0)
            def _():
                fn(pltpu.make_async_copy(tgt_hbm.at[b, pl.ds(pl.multiple_of(t * tm - N_META, 8), tm)], tg_v, tsem))

        fetch(lambda cp: cp.start())
        for c0, cw in chunks:
            a = _silu_mul(g_ref[:, c0:c0 + cw].astype(F32), u_ref[:, c0:c0 + cw].astype(F32))
            a_v[:, c0:c0 + cw] = a.astype(BF16)
        acc = _dot(a_v[...], wd_v[...])
        x = h_ref[...] + 0.5 * acc
        fetch(lambda cp: cp.wait())

        gain = gf_ref[...]
        r = lax.rsqrt(jnp.mean(x * x, axis=-1, keepdims=True) + EPS)
        y = x * r
        pos = t * tm + lax.broadcasted_iota(jnp.int32, (tm, 1), 0)
        err = jnp.where(pos >= N_META, y * gain - tg_v[...], 0.0)
        loss_ref[...] += 0.5 * jnp.sum(jnp.mean(err * err, axis=-1, keepdims=True))
        dout = err / D
        dg_ref[...] += jnp.sum(dout * y, axis=0, keepdims=True)
        dy = dout * gain
        dh = r * (dy - y * jnp.mean(dy * y, axis=-1, keepdims=True))
        dh_ref[...] = dh
        dhb_ref[...] = (0.5 * dh).astype(BF16)

    row = pl.BlockSpec((tm, D), lambda i: (i, 0))
    const = lambda i: (0, 0)
    return _launch(
        body, name=name, grid=(T // tm,),
        in_specs=[pl.BlockSpec((None, tm, ff), lambda i: (0, i, 0)),
                  pl.BlockSpec((None, tm, ff), lambda i: (1, i, 0)),
                  ANY, row, pl.BlockSpec((1, D), const), ANY],
        out_specs=[row, row, pl.BlockSpec((1, D), const), pl.BlockSpec((1, LANES), const)],
        out_shape=[jax.ShapeDtypeStruct((T, D), F32), jax.ShapeDtypeStruct((T, D), BF16),
                   jax.ShapeDtypeStruct((1, D), F32), jax.ShapeDtypeStruct((1, LANES), F32)],
        scratch_shapes=[pltpu.VMEM((ff, D), BF16), pltpu.VMEM((tm, ff), BF16), pltpu.VMEM((tm, D), F32),
                        pltpu.SemaphoreType.DMA, pltpu.SemaphoreType.DMA],
        args=(gu, gu, wd, h, gf, tgt), comm=comm)


def _ffn_bwd_act(df, gu, wd, *, tm, guc, name, comm=None):
    _, T, ff = gu.shape
    D = df.shape[1]
    nj = ff // guc
    chunks = _chunks(guc)

    def body(df_ref, g_ref, u_ref, wd_ref, o_ref, dwd_ref):
        @pl.when(pl.program_id(1) == 0)
        def _():
            dwd_ref[...] = jnp.zeros_like(dwd_ref)

        dfv = df_ref[...]
        nxt = _dot_nt(dfv, wd_ref[chunks[0][0]:chunks[0][0] + chunks[0][1], :])
        for k, (c0, cw) in enumerate(chunks):
            da = nxt
            if k + 1 < len(chunks):
                n0, nw = chunks[k + 1]
                nxt = _dot_nt(dfv, wd_ref[n0:n0 + nw, :])
            g = g_ref[:, c0:c0 + cw].astype(F32)
            u = u_ref[:, c0:c0 + cw].astype(F32)
            sg = jax.nn.sigmoid(g)
            silu = g * sg
            o_ref[0, :, c0:c0 + cw] = (da * u * (sg * (1.0 + g * (1.0 - sg)))).astype(BF16)
            o_ref[1, :, c0:c0 + cw] = (da * silu).astype(BF16)
            dwd_ref[c0:c0 + cw, :] += _dot_tn((silu * u).astype(BF16), dfv)

    return _launch(
        body, name=name, grid=(nj, T // tm),
        in_specs=[pl.BlockSpec((tm, D), lambda j, i: (i, 0)),
                  pl.BlockSpec((None, tm, guc), lambda j, i: (0, i, j)),
                  pl.BlockSpec((None, tm, guc), lambda j, i: (1, i, j)),
                  pl.BlockSpec((guc, D), lambda j, i: (j, 0))],
        out_specs=[pl.BlockSpec((2, tm, guc), lambda j, i: (0, i, j)), pl.BlockSpec((guc, D), lambda j, i: (j, 0))],
        out_shape=[jax.ShapeDtypeStruct((2, T, ff), BF16), jax.ShapeDtypeStruct((ff, D), F32)],
        args=(df, gu, gu, wd), comm=comm)


def _ffn_bwd_in(dgu, wgu, h, g, dres, *, tm, scale, name, comm=None):
    _, T, ff = dgu.shape
    ns, D, guc = wgu.shape
    nj = ff // guc
    chunks = _chunks(guc)

    def body(dgu_ref, w_hbm, h_ref, g_ref, dres_ref, dh_ref, dhb_ref, dg_ref, w_v, acc, sem):
        i, j = pl.program_id(0), pl.program_id(1)

        @pl.when((i == 0) & (j == 0))
        def _():
            cp = pltpu.make_async_copy(w_hbm, w_v, sem)
            cp.start()
            cp.wait()
            dg_ref[...] = jnp.zeros_like(dg_ref)

        part = _dot_nt(dgu_ref[0], w_v[j]) + _dot_nt(dgu_ref[1], w_v[nj + j])

        @pl.when(j == 0)
        def _():
            acc[...] = part

        @pl.when(j > 0)
        def _():
            acc[...] += part

        @pl.when(j == nj - 1)
        def _():
            dh, dgain = _rms_bwd(acc[...], h_ref[...], g_ref[...], dres_ref[...])
            dh_ref[...] = dh
            dhb_ref[...] = (scale * dh).astype(BF16)
            dg_ref[...] += dgain

    return _launch(
        body, name=name, grid=(T // tm, nj),
        in_specs=[pl.BlockSpec((2, tm, guc), lambda i, j: (0, i, j)),
                  ANY,
                  pl.BlockSpec((tm, D), lambda i, j: (i, 0)),
                  pl.BlockSpec((1, D), lambda i, j: (0, 0)),
                  pl.BlockSpec((tm, D), lambda i, j: (i, 0))],
        out_specs=[pl.BlockSpec((tm, D), lambda i, j: (i, 0)),
                   pl.BlockSpec((tm, D), lambda i, j: (i, 0)),
                   pl.BlockSpec((1, D), lambda i, j: (0, 0))],
        out_shape=[jax.ShapeDtypeStruct((T, D), F32), jax.ShapeDtypeStruct((T, D), BF16),
                   jax.ShapeDtypeStruct((1, D), F32)],
        scratch_shapes=[pltpu.VMEM((ns, D, guc), BF16), pltpu.VMEM((tm, D), F32), pltpu.SemaphoreType.DMA],
        args=(dgu, wgu, h, g, dres), comm=comm)


def _ffn_bwd_in_first(dgu, wgu, h, g, dres, *, tm, batch, name, comm=None):
    _, T, ff = dgu.shape
    ns, D, guc = wgu.shape
    nj = ff // guc
    nt = T // tm
    L = T // batch
    per_seq = L // tm
    body_rows = tm - N_META
    chunks = _chunks(guc)

    def body(dgu_ref, w_hbm, h_ref, g_ref, dres_ref, dx_hbm, dmeta_ref, dg_ref, w_v, acc, dh_v, sem, osem):
        i, j = pl.program_id(0), pl.program_id(1)

        @pl.when((i == 0) & (j == 0))
        def _():
            cp = pltpu.make_async_copy(w_hbm, w_v, sem)
            cp.start()
            cp.wait()
            dg_ref[...] = jnp.zeros_like(dg_ref)
            dmeta_ref[...] = jnp.zeros_like(dmeta_ref)

        part = _dot_nt(dgu_ref[0], w_v[j]) + _dot_nt(dgu_ref[1], w_v[nj + j])

        @pl.when(j == 0)
        def _():
            acc[...] = part

        @pl.when(j > 0)
        def _():
            acc[...] += part

        def head_copy(b):
            return pltpu.make_async_copy(dh_v.at[pl.ds(N_META, body_rows)], dx_hbm.at[b, pl.ds(0, body_rows)], osem)

        def tail_copy(b, t):
            return pltpu.make_async_copy(dh_v, dx_hbm.at[b, pl.ds(pl.multiple_of(t * tm - N_META, 8), tm)], osem)

        def on_tile(k, head_fn, tail_fn):
            @pl.when(k % per_seq == 0)
            def _():
                head_fn(head_copy(k // per_seq))

            @pl.when(k % per_seq != 0)
            def _():
                tail_fn(tail_copy(k // per_seq, k % per_seq))

        @pl.when(j == nj - 1)
        def _():
            dh, dgain = _rms_bwd(acc[...], h_ref[...], g_ref[...], dres_ref[...])
            dg_ref[...] += dgain

            @pl.when(i > 0)
            def _():
                on_tile(i - 1, lambda cp: cp.wait(), lambda cp: cp.wait())

            dh_v[...] = dh

            @pl.when(i % per_seq == 0)
            def _():
                dmeta_ref[...] += dh[0:N_META, :]

            on_tile(i, lambda cp: cp.start(), lambda cp: cp.start())

            @pl.when(i == nt - 1)
            def _():
                on_tile(i, lambda cp: cp.wait(), lambda cp: cp.wait())

    return _launch(
        body, name=name, grid=(nt, nj),
        in_specs=[pl.BlockSpec((2, tm, guc), lambda i, j: (0, i, j)),
                  ANY,
                  pl.BlockSpec((tm, D), lambda i, j: (i, 0)),
                  pl.BlockSpec((1, D), lambda i, j: (0, 0)),
                  pl.BlockSpec((tm, D), lambda i, j: (i, 0))],
        out_specs=[ANY, pl.BlockSpec((N_META, D), lambda i, j: (0, 0)), pl.BlockSpec((1, D), lambda i, j: (0, 0))],
        out_shape=[jax.ShapeDtypeStruct((batch, L - N_META, D), F32), jax.ShapeDtypeStruct((N_META, D), F32),
                   jax.ShapeDtypeStruct((1, D), F32)],
        scratch_shapes=[pltpu.VMEM((ns, D, guc), BF16), pltpu.VMEM((tm, D), F32), pltpu.VMEM((tm, D), F32),
                        pltpu.SemaphoreType.DMA, pltpu.SemaphoreType.DMA],
        args=(dgu, wgu, h, g, dres), comm=comm)


def _mix_bwd_in(parts, w_main, w_fg, h, g, dres, *, tm, scale, name, comm=None):
    T, D = h.shape
    widths = [p.shape[1] for p in parts]
    offs = [sum(widths[:k]) for k in range(len(widths))]
    npart = len(parts)
    wide = sum(widths)

    def body(*refs):
        p_refs = refs[:npart]
        wm_ref, wf_ref, h_ref, g_ref, dres_ref, dh_ref, dhb_ref, dg_ref, all_ref = refs[npart:]

        @pl.when(pl.program_id(0) == 0)
        def _():
            dg_ref[...] = jnp.zeros_like(dg_ref)

        for p_ref, off, wd_ in zip(p_refs, offs, widths):
            for c0, cw in _chunks(wd_):
                all_ref[:, off + c0:off + c0 + cw] = p_ref[:, c0:c0 + cw].astype(BF16)
        n_main = offs[-1]
        dn = _dot_nt(all_ref[:, :n_main], wm_ref[...]) + _dot_nt(all_ref[:, n_main:], wf_ref[...])
        dh, dgain = _rms_bwd(dn, h_ref[...], g_ref[...], dres_ref[...])
        dh_ref[...] = dh
        dhb_ref[...] = (scale * dh).astype(BF16)
        dg_ref[...] += dgain

    row = lambda i: (i, 0)
    const = lambda i: (0, 0)
    return _launch(
        body, name=name, grid=(T // tm,),
        in_specs=[pl.BlockSpec((tm, p.shape[1]), row) for p in parts]
                 + [pl.BlockSpec(w_main.shape, const), pl.BlockSpec(w_fg.shape, const),
                    pl.BlockSpec((tm, D), row), pl.BlockSpec((1, D), const), pl.BlockSpec((tm, D), row)],
        out_specs=[pl.BlockSpec((tm, D), row), pl.BlockSpec((tm, D), row), pl.BlockSpec((1, D), const),
                   pl.BlockSpec((tm, wide), row)],
        out_shape=[jax.ShapeDtypeStruct((T, D), F32), jax.ShapeDtypeStruct((T, D), BF16),
                   jax.ShapeDtypeStruct((1, D), F32), jax.ShapeDtypeStruct((T, wide), BF16)],
        args=(*parts, w_main, w_fg, h, g, dres), comm=comm)


def _matmul_tn(x, y, *, tm, nb, x_spec, y_spec, out_shape, out_spec, kb, name, comm=None):
    T = y.shape[-2]
    chunks = _chunks(kb)

    def body(x_ref, y_ref, o_ref):
        @pl.when(pl.program_id(1) == 0)
        def _():
            o_ref[...] = jnp.zeros_like(o_ref)

        yv = y_ref[...].astype(BF16)
        for c0, cw in chunks:
            o_ref[c0:c0 + cw, :] += _dot_tn(x_ref[:, c0:c0 + cw], yv)

    return _launch(
        body, name=name, grid=(nb, T // tm),
        in_specs=[x_spec, y_spec], out_specs=out_spec, out_shape=out_shape, args=(x, y), comm=comm)


def _tri(n, lower):
    r = lax.broadcasted_iota(jnp.int32, (n, n), 0)
    c = lax.broadcasted_iota(jnp.int32, (n, n), 1)
    return jnp.where((r >= c) if lower else (r <= c), 1.0, 0.0).astype(BF16)


def _tri_dot(tri, v):
    hi, mid, lo = _split3(v)
    return _dot(tri, hi) + _dot(tri, mid) + _dot(tri, lo)


def _fcum(fg, bf, *, ch, name):
    B, L, W = fg.shape
    nch = L // ch

    def body(fg_ref, bf_ref, f_ref):
        tri = _tri(ch, True)
        carry = jnp.zeros((1, W), F32)
        for c in range(nch):
            x = fg_ref[c * ch:(c + 1) * ch, :] + bf_ref[...]
            lf = jnp.minimum(x, 0.0) - jnp.log(1.0 + jnp.exp(-jnp.abs(x)))
            f_ref[c * ch:(c + 1) * ch, :] = _tri_dot(tri, lf) + carry
            carry = carry + jnp.sum(lf, axis=0, keepdims=True)

    return pl.pallas_call(
        body, name=name, grid=(B,),
        in_specs=[pl.BlockSpec((None, L, W), lambda b: (b, 0, 0)), pl.BlockSpec((1, W), lambda b: (0, 0))],
        out_specs=pl.BlockSpec((None, L, W), lambda b: (b, 0, 0)),
        out_shape=jax.ShapeDtypeStruct((B, L, W), F32),
        compiler_params=_params("arbitrary"),
    )(fg, bf)


def _fcum_bwd(dF, fg, bf, *, ch, name):
    B, L, W = fg.shape
    nch = L // ch

    def body(df_ref, fg_ref, bf_ref, dfg_ref, db_ref):
        @pl.when(pl.program_id(0) == 0)
        def _():
            db_ref[...] = jnp.zeros_like(db_ref)

        tri = _tri(ch, False)
        carry = jnp.zeros((1, W), F32)
        dbs = jnp.zeros((1, W), F32)
        for c in reversed(range(nch)):
            d = df_ref[c * ch:(c + 1) * ch, :]
            dlf = _tri_dot(tri, d) + carry
            carry = carry + jnp.sum(d, axis=0, keepdims=True)
            x = fg_ref[c * ch:(c + 1) * ch, :] + bf_ref[...]
            dfg = dlf * jax.nn.sigmoid(-x)
            dfg_ref[c * ch:(c + 1) * ch, :] = dfg.astype(BF16)
            dbs = dbs + jnp.sum(dfg, axis=0, keepdims=True)
        db_ref[...] += dbs

    blk = pl.BlockSpec((None, L, W), lambda b: (b, 0, 0))
    return pl.pallas_call(
        body, name=name, grid=(B,),
        in_specs=[blk, blk, pl.BlockSpec((1, W), lambda b: (0, 0))],
        out_specs=[blk, pl.BlockSpec((1, W), lambda b: (0, 0))],
        out_shape=[jax.ShapeDtypeStruct((B, L, W), BF16), jax.ShapeDtypeStruct((1, W), F32)],
        compiler_params=_params("arbitrary"),
    )(dF, fg, bf)


def _band_edges(tq):
    return sorted({min(tq, (k * tq // ATTN_BANDS + HALO - 1) // HALO * HALO) for k in range(ATTN_BANDS + 1)})


def _pair(h):
    return slice((h // 2) * 2 * HEAD_DIM, (h // 2 + 1) * 2 * HEAD_DIM)


def _own_lanes(a, h):
    low = lax.broadcasted_iota(jnp.int32, a.shape, 1) < HEAD_DIM
    return jnp.where(low if h % 2 == 0 else jnp.logical_not(low), a, jnp.zeros_like(a))


def _sum_lane(h):
    return HEAD_DIM if h % 2 == 0 else 0


def _own_lanes_and_ones(a, h):
    lane = lax.broadcasted_iota(jnp.int32, a.shape, 1)
    low = lane < HEAD_DIM
    return jnp.where(low if h % 2 == 0 else jnp.logical_not(low), a,
                     jnp.where(lane == _sum_lane(h), jnp.ones_like(a), jnp.zeros_like(a)))


def _attn_fwd(proj, fc, fr, *, tq, n_heads, name, comm=None):
    B, L, _ = proj.shape
    AD = n_heads * HEAD_DIM
    nq = L // tq
    W = fc.shape[-1]
    scale = HEAD_DIM ** -0.5
    edges = _band_edges(tq)

    v_ones, sum_lane = _own_lanes_and_ones, _sum_lane

    def body(q_ref, k_ref, v_ref, fr_ref, o_ref, lse_ref, m_s, acc_s):
        qi, ki = pl.program_id(1), pl.program_id(2)

        @pl.when(ki == 0)
        def _():
            m_s[...] = jnp.full_like(m_s, NEG)
            acc_s[...] = jnp.zeros_like(acc_s)

        def tile(diagonal):
            lane = lax.broadcasted_iota(jnp.int32, (tq, W), 1)
            m_all = m_s[...]
            m_out = m_all
            bands = [(r0, r1, r1 if diagonal else tq) for r0, r1 in zip(edges[:-1], edges[1:])]
            if diagonal:
                masks = {r0: (lax.broadcasted_iota(jnp.int32, (r1 - r0, c1), 1)
                              <= r0 + lax.broadcasted_iota(jnp.int32, (r1 - r0, c1), 0)) for r0, r1, c1 in bands}

            def scores(h, band):
                r0, r1, c1 = band
                sl = slice(h * HEAD_DIM, (h + 1) * HEAD_DIM)
                return _dot_nt(q_ref[r0:r1, sl] * scale, k_ref[0:c1, sl])

            work = [(h, band) for h in range(n_heads) for band in bands]
            nxt = scores(*work[0])
            for w, (h, band) in enumerate(work):
                r0, r1, c1 = band
                sl = slice(h * HEAD_DIM, (h + 1) * HEAD_DIM)
                s = nxt - fr_ref[h:h + 1, 0:c1]
                if w + 1 < len(work):
                    nxt = scores(*work[w + 1])
                if diagonal:
                    s = jnp.where(masks[r0], s, NEG)
                m_old = m_all[r0:r1, h:h + 1]
                m_new = jnp.maximum(m_old, jnp.max(s, axis=1, keepdims=True))
                alpha = jnp.exp(m_old - m_new)
                p = jnp.exp(s - m_new)
                own = slice(h * 2 * HEAD_DIM, (h + 1) * 2 * HEAD_DIM)
                acc_s[r0:r1, own] = alpha * acc_s[r0:r1, own] + _dot(p.astype(BF16), v_ones(v_ref[0:c1, _pair(h)], h))
                if r0 == 0:
                    m_parts = []
                m_parts.append(m_new)
                if r1 == tq:
                    m_out = jnp.where(lane == h, jnp.concatenate(m_parts, axis=0), m_out)
            m_s[...] = m_out

        @pl.when(ki < qi)
        def _():
            tile(False)

        @pl.when(ki == qi)
        def _():
            tile(True)
            lane = lax.broadcasted_iota(jnp.int32, (tq, W), 1)
            low = lax.broadcasted_iota(jnp.int32, (tq, 2 * HEAD_DIM), 1) < HEAD_DIM
            l_all = jnp.ones((tq, W), F32)
            for h in range(0, n_heads, 2):
                even = acc_s[:, h * 2 * HEAD_DIM:(h + 1) * 2 * HEAD_DIM]
                odd = acc_s[:, (h + 1) * 2 * HEAD_DIM:(h + 2) * 2 * HEAD_DIM]
                l_even = even[:, sum_lane(h):sum_lane(h) + 1]
                l_odd = odd[:, sum_lane(h + 1):sum_lane(h + 1) + 1]
                o_ref[:, _pair(h)] = jnp.where(low, even / l_even, odd / l_odd)
                l_all = jnp.where(lane == h, l_even, jnp.where(lane == h + 1, l_odd, l_all))
            lse_ref[...] = jnp.where(lane < n_heads, m_s[...] + jnp.log(l_all), 0.0)

    kv = lambda b, qi, ki: jnp.minimum(ki, qi)
    return _launch(
        body, name=name, grid=(B, nq, nq), args=(proj, proj, proj, fr), comm=comm,
        in_specs=[pl.BlockSpec((None, tq, AD), lambda b, qi, ki: (b, qi, 3)),
                  pl.BlockSpec((None, tq, AD), lambda b, qi, ki: (b, kv(b, qi, ki), 4)),
                  pl.BlockSpec((None, tq, AD), lambda b, qi, ki: (b, kv(b, qi, ki), 5)),
                  pl.BlockSpec((None, None, n_heads, tq), lambda b, qi, ki: (b, kv(b, qi, ki), 0, 0))],
        out_specs=[pl.BlockSpec((None, tq, AD), lambda b, qi, ki: (b, qi, 0)),
                   pl.BlockSpec((None, tq, W), lambda b, qi, ki: (b, qi, 0))],
        out_shape=[jax.ShapeDtypeStruct((B, L, AD), F32), jax.ShapeDtypeStruct((B, L, W), F32)],
        scratch_shapes=[pltpu.VMEM((tq, W), F32), pltpu.VMEM((tq, n_heads * 2 * HEAD_DIM), F32)])


def _attn_bwd(proj, o, do, lse, fc, fr, *, tq, n_heads, name, comm=None):
    B, L, _ = proj.shape
    AD = n_heads * HEAD_DIM
    nq = L // tq
    W = fc.shape[-1]
    HW = 2 * HEAD_DIM
    scale = HEAD_DIM ** -0.5
    edges = _band_edges(tq)

    def body(q_ref, k_ref, v_ref, o_ref, do_ref, lse_ref, fr_ref,
             dq_ref, dk_ref, dv_ref, dfk_ref, dfq_ref, dq_s, dk_s, dv_s):
        kj, qi = pl.program_id(1), pl.program_id(2)

        @pl.when((kj == 0) & (qi == 0))
        def _():
            dq_s[...] = jnp.zeros_like(dq_s)

        @pl.when(qi == kj)
        def _():
            dk_s[...] = jnp.zeros_like(dk_s)
            dv_s[...] = jnp.zeros_like(dv_s)

        def tile(diagonal):
            bands = [(r0, r1, r1) for r0, r1 in zip(edges[:-1], edges[1:])] if diagonal else [(0, tq, tq)]
            lse = lse_ref[...]
            for r0, r1, c1 in bands:
                nr = r1 - r0
                rows = pl.ds(pl.multiple_of(qi * tq + r0, 8), nr)
                if diagonal:
                    mask = (lax.broadcasted_iota(jnp.int32, (nr, c1), 1)
                            <= r0 + lax.broadcasted_iota(jnp.int32, (nr, c1), 0))
                for h in range(n_heads):
                    ps = _pair(h)
                    own = slice(h * HW, (h + 1) * HW)
                    k, v = k_ref[0:c1, ps], v_ref[0:c1, ps]
                    qs = q_ref[r0:r1, ps] * scale
                    dov = _own_lanes(do_ref[r0:r1, ps], h)
                    s = _dot_nt(_own_lanes(qs, h), k) - fr_ref[h:h + 1, 0:c1]
                    if diagonal:
                        s = jnp.where(mask, s, NEG)
                    p = jnp.exp(s - lse[r0:r1, h:h + 1])
                    dp = _dot_nt(dov, v)
                    dsum = jnp.sum(dov.astype(F32) * o_ref[r0:r1, ps], axis=1, keepdims=True)
                    dsb = (p * (dp - dsum)).astype(BF16)
                    dv = _dot_tn(p.astype(BF16), dov)
                    dk_s[0:c1, own] += _dot_tn(dsb, _own_lanes_and_ones(qs, h))
                    dq_s[rows, own] += _dot(dsb, _own_lanes_and_ones(k, h))
                    if h % 2 == 0:
                        dv_even = dv
                    else:
                        dv_s[0:c1, ps] += dv_even + dv

        def compact(acc, data_scale):
            rows = acc.shape[0]
            low = lax.broadcasted_iota(jnp.int32, (rows, HW), 1) < HEAD_DIM
            lane = lax.broadcasted_iota(jnp.int32, (rows, W), 1)
            vals, sums = [], jnp.zeros((rows, W), F32)
            for h in range(0, n_heads, 2):
                even, odd = acc[:, h * HW:(h + 1) * HW], acc[:, (h + 1) * HW:(h + 2) * HW]
                vals.append(jnp.where(low, even, odd) * data_scale)
                sums = jnp.where(lane == h, even[:, _sum_lane(h):_sum_lane(h) + 1],
                                 jnp.where(lane == h + 1, odd[:, _sum_lane(h + 1):_sum_lane(h + 1) + 1], sums))
            return vals, sums

        @pl.when(qi > kj)
        def _():
            tile(False)

        @pl.when(qi == kj)
        def _():
            tile(True)
            rows = pl.ds(pl.multiple_of(qi * tq, 8), tq)
            vals, sums = compact(dq_s[rows, :], scale)
            for h in range(0, n_heads, 2):
                dq_ref[rows, _pair(h)] = vals[h // 2]
            dfq_ref[rows, :] = sums

        @pl.when(qi == nq - 1)
        def _():
            vals, sums = compact(dk_s[...], 1.0)
            for h in range(0, n_heads, 2):
                dk_ref[:, _pair(h)] = vals[h // 2].astype(BF16)
            dfk_ref[...] = sums
            dv_ref[...] = dv_s[...].astype(BF16)

    qq = lambda b, kj, qi: jnp.maximum(qi, kj)
    qblk = lambda w, cb: pl.BlockSpec((None, tq, w), lambda b, kj, qi: (b, qq(b, kj, qi), cb))
    kblk = lambda w, cb: pl.BlockSpec((None, tq, w), lambda b, kj, qi: (b, kj, cb))
    return _launch(
        body, name=name, grid=(B, nq, nq), args=(proj, proj, proj, o, do, lse, fr), comm=comm,
        in_specs=[qblk(AD, 3), kblk(AD, 4), kblk(AD, 5), qblk(AD, 0), qblk(AD, 0), qblk(W, 0),
                  pl.BlockSpec((None, None, n_heads, tq), lambda b, kj, qi: (b, kj, 0, 0))],
        out_specs=[pl.BlockSpec((None, L, AD), lambda b, kj, qi: (b, 0, 0)),
                   kblk(AD, 0), kblk(AD, 0), kblk(W, 0),
                   pl.BlockSpec((None, L, W), lambda b, kj, qi: (b, 0, 0))],
        out_shape=[jax.ShapeDtypeStruct((B, L, AD), F32), jax.ShapeDtypeStruct((B, L, AD), BF16),
                   jax.ShapeDtypeStruct((B, L, AD), BF16), jax.ShapeDtypeStruct((B, L, W), F32),
                   jax.ShapeDtypeStruct((B, L, W), F32)],
        scratch_shapes=[pltpu.VMEM((L, n_heads * HW), F32), pltpu.VMEM((tq, n_heads * HW), F32),
                        pltpu.VMEM((tq, AD), F32)])


def _mix_gather(refs, first):
    b_ref, c_ref, hc_ref, cp_ref, hcp_ref, o_ref, cw_ref, p_ref = refs
    bg = b_ref[...].astype(F32)
    u = c_ref[...].astype(F32) * hc_ref[...].astype(F32)
    prev = cp_ref[...].astype(F32) * hcp_ref[...].astype(F32)
    prev = jnp.where(first, 0.0, prev)
    cv, u1, u2 = _causal_conv(u, prev, cw_ref[...])
    yc = bg * cv
    p = p_ref[...]
    rc = lax.rsqrt(_group_mean(yc * yc, p) + EPS)
    ya = o_ref[...].astype(F32)
    ra = lax.rsqrt(_group_mean(ya * ya, p) + EPS)
    return bg, (u, u1, u2), cv, yc * rc, rc, ya * ra, ra


def _mix_specs(tm, CD, D, grid_rank_fn):
    per = tm // HALO
    cur = lambda cb: pl.BlockSpec((None, tm, CD), lambda b, i: (b, i, cb))
    prev = lambda cb: pl.BlockSpec((None, HALO, CD), lambda b, i: (b, jnp.maximum(i * per - 1, 0), cb))
    return [cur(0), cur(1), cur(2), prev(1), prev(2), cur(0)]


def _mix_out(proj, o, cw, gc, ga, wout, h, pmat, next_gain, *, tm, name, comm=None):
    B, L, D = h.shape
    CD = o.shape[-1]
    const = lambda b, i: (0, 0)

    def body(b_ref, c_ref, hc_ref, cp_ref, hcp_ref, o_ref, cw_ref, p_ref, gc_ref, ga_ref, w_ref, h_ref, ng_ref,
             out_ref, y_ref, n_ref):
        first = pl.program_id(1) == 0
        _, _, _, zc, _, za, _ = _mix_gather((b_ref, c_ref, hc_ref, cp_ref, hcp_ref, o_ref, cw_ref, p_ref), first)
        yc = (zc * gc_ref[...]).astype(BF16)
        ya = (za * ga_ref[...]).astype(BF16)
        y_ref[:, :CD] = yc
        y_ref[:, CD:] = ya
        out = h_ref[...] + _dot(yc, w_ref[:CD, :]) + _dot(ya, w_ref[CD:, :])
        out_ref[...] = out
        n_ref[...] = _rms(out, ng_ref[...])

    tile = pl.BlockSpec((None, tm, D), lambda b, i: (b, i, 0))
    return _launch(
        body, name=name, grid=(B, L // tm),
        in_specs=_mix_specs(tm, CD, D, None)
                 + [pl.BlockSpec(cw.shape, const), pl.BlockSpec(pmat.shape, const),
                    pl.BlockSpec((1, CD), const), pl.BlockSpec((1, CD), const), pl.BlockSpec((D, D), const),
                    tile, pl.BlockSpec((1, D), const)],
        out_specs=[tile, tile, tile],
        out_shape=[jax.ShapeDtypeStruct((B, L, D), F32), jax.ShapeDtypeStruct((B, L, D), BF16),
                   jax.ShapeDtypeStruct((B, L, D), BF16)],
        args=(proj, proj, proj, proj, proj, o, cw, pmat, gc, ga, wout, h, next_gain), comm=comm)


def _mix_out_bwd(dhb, proj, o, cw, gc, ga, wout, pmat, *, tm, name, comm=None):
    B, L, D = dhb.shape
    CD = o.shape[-1]
    const = lambda b, i: (0, 0)

    def body(dh_ref, b_ref, c_ref, hc_ref, cp_ref, hcp_ref, o_ref, cw_ref, p_ref, gc_ref, ga_ref, w_ref,
             db_ref, dcv_ref, do_ref, dgc_ref, dga_ref, dcw_ref):
        first = pl.program_id(1) == 0

        @pl.when((pl.program_id(0) == 0) & first)
        def _():
            dgc_ref[...] = jnp.zeros_like(dgc_ref)
            dga_ref[...] = jnp.zeros_like(dga_ref)
            dcw_ref[...] = jnp.zeros_like(dcw_ref)

        bg, us, cv, zc, rc, za, ra = _mix_gather(
            (b_ref, c_ref, hc_ref, cp_ref, hcp_ref, o_ref, cw_ref, p_ref), first)
        p = p_ref[...]
        dh = dh_ref[...]
        dyc = _dot_nt(dh, w_ref[:CD, :])
        dya = _dot_nt(dh, w_ref[CD:, :])

        dgc_ref[...] += jnp.sum(dyc * zc, axis=0, keepdims=True)
        dz = dyc * gc_ref[...]
        dx = rc * (dz - zc * _group_mean(dz * zc, p))
        db_ref[...] = (dx * cv).astype(BF16)
        dcv = dx * bg
        dcv_ref[...] = dcv.astype(BF16)
        for k in range(3):
            dcw_ref[k:k + 1, :] += jnp.sum(dcv * us[2 - k], axis=0, keepdims=True)

        dga_ref[...] += jnp.sum(dya * za, axis=0, keepdims=True)
        dz = dya * ga_ref[...]
        do_ref[...] = (ra * (dz - za * _group_mean(dz * za, p))).astype(BF16)

    tile = lambda w: pl.BlockSpec((None, tm, w), lambda b, i: (b, i, 0))
    return _launch(
        body, name=name, grid=(B, L // tm), comm=comm,
        args=(dhb, proj, proj, proj, proj, proj, o, cw, pmat, gc, ga, wout),
        in_specs=[tile(D)] + _mix_specs(tm, CD, D, None)
                 + [pl.BlockSpec(cw.shape, const), pl.BlockSpec(pmat.shape, const),
                    pl.BlockSpec((1, CD), const), pl.BlockSpec((1, CD), const), pl.BlockSpec((D, D), const)],
        out_specs=[tile(CD), tile(CD), tile(CD),
                   pl.BlockSpec((1, CD), const), pl.BlockSpec((1, CD), const), pl.BlockSpec((8, CD), const)],
        out_shape=[jax.ShapeDtypeStruct((B, L, CD), BF16)] * 3
                  + [jax.ShapeDtypeStruct((1, CD), F32)] * 2 + [jax.ShapeDtypeStruct((8, CD), F32)])


def _conv_bwd(dcv, proj, cw, *, tm, name):
    B, L, CD = dcv.shape
    per = tm // HALO
    nhalo = L // HALO
    nt = L // tm

    def body(d_ref, dn_ref, c_ref, hc_ref, cw_ref, out_ref):
        last = pl.program_id(1) == nt - 1
        d = d_ref[...].astype(F32)
        nxt = jnp.where(last, 0.0, dn_ref[...].astype(F32))
        n0, n1 = _row_of(nxt, 0), _row_of(nxt, 1)
        rows = lax.broadcasted_iota(jnp.int32, d.shape, 0)
        d1 = jnp.where(rows == tm - 1, n0, pltpu.roll(d, tm - 1, 0))
        d2 = jnp.where(rows == tm - 2, n0, jnp.where(rows == tm - 1, n1, pltpu.roll(d, tm - 2, 0)))
        w = cw_ref[...]
        du = w[2:3, :] * d + w[1:2, :] * d1 + w[0:1, :] * d2
        out_ref[:, :CD] = (du * hc_ref[...].astype(F32)).astype(BF16)
        out_ref[:, CD:] = (du * c_ref[...].astype(F32)).astype(BF16)

    return pl.pallas_call(
        body, name=name, grid=(B, nt),
        in_specs=[pl.BlockSpec((None, tm, CD), lambda b, i: (b, i, 0)),
                  pl.BlockSpec((None, HALO, CD), lambda b, i: (b, jnp.minimum((i + 1) * per, nhalo - 1), 0)),
                  pl.BlockSpec((None, tm, CD), lambda b, i: (b, i, 1)),
                  pl.BlockSpec((None, tm, CD), lambda b, i: (b, i, 2)),
                  pl.BlockSpec(cw.shape, lambda b, i: (0, 0))],
        out_specs=pl.BlockSpec((None, tm, 2 * CD), lambda b, i: (b, i, 0)),
        out_shape=jax.ShapeDtypeStruct((B, L, 2 * CD), BF16),
        compiler_params=_params("arbitrary", "arbitrary"),
    )(dcv, dcv, proj, proj, cw)


def _place():
    x, y, c = lax.axis_index("x"), lax.axis_index("y"), lax.axis_index("c")
    others = [(1 - x, y), (x, 1 - y), (1 - x, 1 - y)]
    return x, y, c, others


def _all_gather_shards(shards, *, name):
    n = len(shards)

    def body(*refs):
        ins, outs = refs[:n], refs[n:2 * n]
        send, recv, fsend, frecv, lsem = refs[2 * n:]
        x, y, c, others = _place()
        me = 2 * x + y
        local = [pltpu.make_async_copy(ins[t], outs[t].at[me], lsem.at[t]) for t in range(n)]
        for cp in local:
            cp.start()

        def half(t, k):
            hr = shards[t].shape[0] // 2
            return pl.ds(pl.multiple_of(k * hr, HALO), hr)

        def ici(t, j, src_chip, to):
            src = ins[t].at[half(t, c)] if to is not None else outs[t].at[src_chip, half(t, c)]
            return pltpu.make_async_remote_copy(
                src_ref=src, dst_ref=outs[t].at[src_chip, half(t, c)],
                send_sem=send.at[3 * t + j], recv_sem=recv.at[3 * t + j],
                device_id=(x, y, c) if to is None else to, device_id_type=MESH)

        def d2d(t, j, src_chip, k):
            return pltpu.make_async_remote_copy(
                src_ref=outs[t].at[src_chip, half(t, k)], dst_ref=outs[t].at[src_chip, half(t, k)],
                send_sem=fsend.at[3 * t + j], recv_sem=frecv.at[3 * t + j],
                device_id=(x, y, 1 - c), device_id_type=MESH)

        firsts = [ici(t, j, me, (ox, oy, c)) for t in range(n) for j, (ox, oy) in enumerate(others)]
        for cp in firsts:
            cp.start()
        passed = []
        for t in range(n):
            for j, (ox, oy) in enumerate(others):
                ici(t, j, 2 * ox + oy, None).wait_recv()
                cp = d2d(t, j, 2 * ox + oy, c)
                cp.start()
                passed.append(cp)
        for t in range(n):
            for j, (ox, oy) in enumerate(others):
                d2d(t, j, 2 * ox + oy, 1 - c).wait_recv()
        for cp in firsts + passed:
            cp.wait_send()
        for cp in local:
            cp.wait()

    return pl.pallas_call(
        body, name=name,
        in_specs=[ANY] * n, out_specs=[ANY] * n,
        out_shape=[jax.ShapeDtypeStruct((N_SHARD,) + s.shape, s.dtype) for s in shards],
        scratch_shapes=[pltpu.SemaphoreType.DMA((3 * n,))] * 4 + [pltpu.SemaphoreType.DMA((n,))],
    )(*shards)


def _all_reduce_small(slab, *, name):
    def body(in_ref, out_ref, gath, send, recv):
        x, y, c, _ = _place()
        me = 4 * x + 2 * y + c
        gath[me] = in_ref[...]
        copies, peers = [], []
        for m in range(1, N_DEV):
            px = jnp.where((m >> 2) & 1, 1 - x, x)
            py = jnp.where((m >> 1) & 1, 1 - y, y)
            pc = jnp.where(m & 1, 1 - c, c)
            cp = pltpu.make_async_remote_copy(
                src_ref=in_ref, dst_ref=gath.at[me], send_sem=send.at[m - 1], recv_sem=recv.at[m - 1],
                device_id=(px, py, pc), device_id_type=MESH)
            cp.start()
            copies.append(cp)
            peers.append(4 * px + 2 * py + pc)
        for m in range(1, N_DEV):
            pltpu.make_async_remote_copy(
                src_ref=in_ref, dst_ref=gath.at[peers[m - 1]], send_sem=send.at[m - 1], recv_sem=recv.at[m - 1],
                device_id=(x, y, c), device_id_type=MESH).wait_recv()
        for cp in copies:
            cp.wait_send()
        acc = gath[0]
        for k in range(1, N_DEV):
            acc = acc + gath[k]
        out_ref[...] = acc

    vm = pl.BlockSpec(memory_space=pltpu.VMEM)
    return pl.pallas_call(
        body, name=name, in_specs=[vm], out_specs=vm,
        out_shape=jax.ShapeDtypeStruct(slab.shape, slab.dtype),
        scratch_shapes=[pltpu.VMEM((N_DEV,) + slab.shape, slab.dtype),
                        pltpu.SemaphoreType.DMA((N_DEV - 1,)), pltpu.SemaphoreType.DMA((N_DEV - 1,))],
    )(slab)


def _gather_stage(shards, into, *, ici=(), d2d=(), whole=False):
    n = len(shards) if into is None else len(into)
    ns = len(shards) if ici else 0
    ni, nd = max(len(ici), 1), max(len(d2d), 1)
    shapes = [s.shape for s in shards] if into is None else [p.shape[1:] for p in into]
    dtypes = [s.dtype for s in shards] if into is None else [p.dtype for p in into]

    def copies(ins, outs, sems, sending):
        x, y, c, others = _place()
        me = 2 * x + y
        out = []
        for t in range(n):
            rows_all = shapes[t][0]
            hr = rows_all // 2
            mine = pl.ds(pl.multiple_of(c * hr, HALO), hr)
            theirs = pl.ds(pl.multiple_of((1 - c) * hr, HALO), hr)
            for a, j in enumerate(ici):
                ox, oy = others[j]
                src_chip = me if sending else 2 * ox + oy
                lone = whole and j < 2
                rows = pl.ds(0, rows_all) if lone else mine
                out.append(((c == 1 - j) if lone else None, pltpu.make_async_remote_copy(
                    src_ref=ins[t].at[rows], dst_ref=outs[t].at[src_chip, rows],
                    send_sem=sems[0].at[ni * t + a], recv_sem=sems[1].at[ni * t + a],
                    device_id=(ox, oy, c) if sending else (x, y, c), device_id_type=MESH)))
            for a, j in enumerate(d2d):
                ox, oy = others[j]
                lone = whole and j < 2
                if lone:
                    cond = (c == 1 - j) if sending else (c != 1 - j)
                    blk = outs[t].at[2 * ox + oy]
                else:
                    cond = None
                    blk = outs[t].at[2 * ox + oy, mine if sending else theirs]
                out.append((cond, pltpu.make_async_remote_copy(
                    src_ref=blk, dst_ref=blk, send_sem=sems[2].at[nd * t + a], recv_sem=sems[3].at[nd * t + a],
                    device_id=(x, y, 1 - c) if sending else (x, y, c), device_id_type=MESH)))
        return out

    def local(ins, outs, sems):
        if into is not None:
            return []
        x, y, _, _ = _place()
        return [(None, pltpu.make_async_copy(ins[t], outs[t].at[2 * x + y], sems[4].at[t])) for t in range(n)]

    def each(pairs, fn):
        for cond, cp in pairs:
            if cond is None:
                fn(cp)
            else:
                pl.when(cond)(functools.partial(fn, cp))

    def start(ins, outs, sems):
        each(local(ins, outs, sems) + copies(ins, outs, sems, True), lambda cp: cp.start())

    def finish(ins, outs, sems):
        each(copies(ins, outs, sems, False), lambda cp: cp.wait_recv())
        each(copies(ins, outs, sems, True), lambda cp: cp.wait_send())
        each(local(ins, outs, sems), lambda cp: cp.wait())

    return _Comm((list(shards) if ici or into is None else []) + (list(into) if into is not None else []),
                 [jax.ShapeDtypeStruct((N_SHARD,) + tuple(sh), dt) for sh, dt in zip(shapes, dtypes)],
                 [ni * n, ni * n, nd * n, nd * n, n], start, finish,
                 aliases=None if into is None else {ns + t: t for t in range(n)})


def _gather_ici(shards):
    return _gather_stage(shards, None, ici=(0, 1, 2))


def _gather_d2d(parts):
    return _gather_stage((), parts, d2d=(0, 1, 2))


def _swap_halves(grads):
    n = len(grads)

    def copies(ins, outs, sems):
        x, y, c, _ = _place()
        out = []
        for t in range(n):
            hr = grads[t].shape[1] // 2
            rows = pl.ds(pl.multiple_of((1 - c) * hr, 8), hr)
            out.append(pltpu.make_async_remote_copy(
                src_ref=ins[t].at[:, rows, :], dst_ref=outs[t], send_sem=sems[0].at[t], recv_sem=sems[1].at[t],
                device_id=(x, y, 1 - c), device_id_type=MESH))
        return out

    def start(ins, outs, sems):
        for cp in copies(ins, outs, sems):
            cp.start()

    def finish(ins, outs, sems):
        for cp in copies(ins, outs, sems):
            cp.wait()

    return _Comm(grads, [jax.ShapeDtypeStruct((N_SHARD, g.shape[1] // 2, g.shape[2]), g.dtype) for g in grads],
                 [n, n], start, finish)


def _pair_sum(g, got, c, *, name):
    ns, R, C = g.shape
    hr = R // 2

    def body(c_ref, g_ref, r_ref, o_ref):
        o_ref[...] = (g_ref[...] + r_ref[...]).astype(BF16)

    return pl.pallas_call(
        body, name=name,
        grid_spec=pltpu.PrefetchScalarGridSpec(
            num_scalar_prefetch=1, grid=(ns,),
            in_specs=[pl.BlockSpec((None, hr, C), lambda s, cr: (s, cr[0], 0)),
                      pl.BlockSpec((None, hr, C), lambda s, cr: (s, 0, 0))],
            out_specs=pl.BlockSpec((None, hr, C), lambda s, cr: (s, 0, 0))),
        out_shape=jax.ShapeDtypeStruct((ns, hr, C), BF16),
        compiler_params=_params("arbitrary"),
    )(c, g, got)


def _scatter_chips(sums):
    n = len(sums)

    def copies(ins, outs, sems, sending):
        x, y, c, others = _place()
        me = 2 * x + y
        out = []
        for t in range(n):
            for j, (ox, oy) in enumerate(others):
                there = 2 * ox + oy
                out.append(pltpu.make_async_remote_copy(
                    src_ref=ins[t].at[there if sending else me], dst_ref=outs[t].at[me if sending else there],
                    send_sem=sems[0].at[3 * t + j], recv_sem=sems[1].at[3 * t + j],
                    device_id=(ox, oy, c) if sending else (x, y, c), device_id_type=MESH))
        return out

    def start(ins, outs, sems):
        for cp in copies(ins, outs, sems, True):
            cp.start()

    def finish(ins, outs, sems):
        for cp in copies(ins, outs, sems, False):
            cp.wait_recv()
        for cp in copies(ins, outs, sems, True):
            cp.wait_send()

    return _Comm(sums, [jax.ShapeDtypeStruct(s.shape, s.dtype) for s in sums], [3 * n, 3 * n], start, finish)


def _chip_sum(g, got, landed, idx, *, name):
    ns, R, C = g.shape
    hr = R // 2

    def body(i_ref, g_ref, r_ref, a_ref, b_ref, c_ref, o_ref):
        acc = g_ref[...] + r_ref[...]
        for ref in (a_ref, b_ref, c_ref):
            acc = acc + ref[...].astype(F32)
        o_ref[...] = acc

    other = lambda k: pl.BlockSpec((None, hr, C), lambda s, ir: (ir[2 + k], 0, 0))
    return pl.pallas_call(
        body, name=name,
        grid_spec=pltpu.PrefetchScalarGridSpec(
            num_scalar_prefetch=1, grid=(1,),
            in_specs=[pl.BlockSpec((None, hr, C), lambda s, ir: (ir[0], ir[1], 0)),
                      pl.BlockSpec((None, hr, C), lambda s, ir: (ir[0], 0, 0)),
                      other(0), other(1), other(2)],
            out_specs=pl.BlockSpec((hr, C), lambda s, ir: (ir[1], 0))),
        out_shape=jax.ShapeDtypeStruct((R, C), F32),
        compiler_params=_params("arbitrary"),
    )(idx, g, got, landed, landed, landed)


def _share_halves(halves):
    n = len(halves)

    def copies(outs, sems, sending):
        x, y, c, _ = _place()
        out = []
        for t in range(n):
            hr = halves[t].shape[0] // 2
            rows = pl.ds(pl.multiple_of((c if sending else 1 - c) * hr, 8), hr)
            out.append(pltpu.make_async_remote_copy(
                src_ref=outs[t].at[rows, :], dst_ref=outs[t].at[rows, :], send_sem=sems[0].at[t],
                recv_sem=sems[1].at[t], device_id=(x, y, 1 - c) if sending else (x, y, c), device_id_type=MESH))
        return out

    def start(ins, outs, sems):
        for cp in copies(outs, sems, True):
            cp.start()

    def finish(ins, outs, sems):
        for cp in copies(outs, sems, False):
            cp.wait_recv()
        for cp in copies(outs, sems, True):
            cp.wait_send()

    return _Comm(halves, [jax.ShapeDtypeStruct(h.shape, h.dtype) for h in halves], [n, n], start, finish,
                 aliases={t: t for t in range(n)})


def _adamw(w, g, m, v, *, name):
    R, C = w.shape
    tr = R
    for cand in (256, 128, 64, 32, 16, 8):
        if R % cand == 0:
            tr = cand
            break

    def body(w_ref, g_ref, m_ref, v_ref, go_ref, d_ref, mo_ref, vo_ref):
        gv = g_ref[...]
        go_ref[...] = gv
        mn = ADAM_B1 * m_ref[...] + (1.0 - ADAM_B1) * gv
        vn = ADAM_B2 * v_ref[...] + (1.0 - ADAM_B2) * (gv * gv)
        m_hat = mn / (1.0 - ADAM_B1 ** ADAM_STEP)
        v_hat = vn / (1.0 - ADAM_B2 ** ADAM_STEP)
        d_ref[...] = -ADAM_LR * (m_hat / (jnp.sqrt(v_hat) + ADAM_EPS) + ADAM_WD * w_ref[...])
        mo_ref[...] = mn
        vo_ref[...] = vn

    blk = pl.BlockSpec((tr, C), lambda i: (i, 0))
    return pl.pallas_call(
        body, name=name, grid=(R // tr,), in_specs=[blk] * 4, out_specs=[blk] * 4,
        out_shape=[jax.ShapeDtypeStruct((R, C), F32)] * 4,
        compiler_params=_params("arbitrary"),
    )(w, g, m, v)


def _pack_small(D, meta, n1, nm, n3, nf, gc, ga, bf, cw):
    def row(a):
        a = a.reshape(-1, a.shape[-1])
        return jnp.pad(a, ((0, 0), (0, D - a.shape[-1])))
    rows = [row(meta), row(n1), row(nm), row(n3), row(nf), row(jnp.concatenate([gc, ga], axis=-1)), row(bf), row(cw)]
    slab = jnp.concatenate(rows, axis=0)
    return jnp.pad(slab, ((0, SMALL_ROWS - slab.shape[0]), (0, 0)))


def _unpack_small(slab, like):
    meta, n1, nm, n3, nf, gc, ga, bf, cw = like
    nmeta, mc = meta.shape
    out = [slab[:nmeta, :mc].reshape(meta.shape)]
    r = nmeta
    for a in (n1, nm, n3, nf):
        out.append(slab[r, :a.shape[-1]].reshape(a.shape))
        r += 1
    cd = gc.shape[-1]
    out.append(slab[r, :cd].reshape(gc.shape))
    out.append(slab[r, cd:cd + ga.shape[-1]].reshape(ga.shape))
    r += 1
    out.append(slab[r, :bf.shape[-1]].reshape(bf.shape))
    r += 1
    out.append(slab[r:r + 3, :cw.shape[-1]].reshape(cw.shape))
    return out


def kernel(x, meta_tokens, ffn1_norm, ffn1_w_gu, ffn1_w_down, mix_norm, w_in, conv_w, b_f, out_norm_conv, out_norm_attn, w_out, ffn2_norm, ffn2_w_gu, ffn2_w_down, final_norm, loss_target, m_meta_tokens, m_ffn1_norm, m_ffn1_w_gu, m_ffn1_w_down, m_mix_norm, m_w_in, m_conv_w, m_b_f, m_out_norm_conv, m_out_norm_attn, m_w_out, m_ffn2_norm, m_ffn2_w_gu, m_ffn2_w_down, m_final_norm, v_meta_tokens, v_ffn1_norm, v_ffn1_w_gu, v_ffn1_w_down, v_mix_norm, v_w_in, v_conv_w, v_b_f, v_out_norm_conv, v_out_norm_attn, v_w_out, v_ffn2_norm, v_ffn2_w_gu, v_ffn2_w_down, v_final_norm):
    B, S, D = x.shape
    L = S + N_META
    T = B * L
    tm = L // 3
    assert tm * 3 == L and tm % HALO == 0
    guc = ffn1_w_gu.shape[-1]
    ff = N_SHARD * guc // 2
    H = b_f.shape[-1]
    AD = H * HEAD_DIM
    CD = conv_w.shape[-1] * N_SHARD
    assert CD == AD and CD + AD == D and CD % LANES == 0
    n_main = 3 * CD + 3 * AD
    ins = w_in.shape[-1]

    xi, yi, ci = lax.axis_index("x"), lax.axis_index("y"), lax.axis_index("c")
    chip = 2 * xi + yi

    small_shard = jnp.zeros((2 * HALO, meta_tokens.shape[-1]), F32)
    small_shard = small_shard.at[:N_META].set(meta_tokens)
    small_shard = small_shard.at[N_META:N_META + 3, :conv_w.shape[-1]].set(conv_w[0])
    big = [ffn1_w_gu[0], ffn1_w_down[0], w_in[0], w_out[0], ffn2_w_gu[0], ffn2_w_down[0]]
    wgu1_s, wd1_s, win_s, wout_s, wgu2_s, wd2_s = [w.astype(BF16) for w in big]
    small_g, = _all_gather_shards([small_shard], name="gather_small")
    meta_f = jnp.moveaxis(small_g[:, :N_META], 0, 1).reshape(N_META, D)
    cw_f = jnp.moveaxis(small_g[:, N_META:N_META + 3, :conv_w.shape[-1]], 0, 1).reshape(3, CD)
    cw8 = jnp.pad(cw_f, ((0, 5), (0, 0)))
    bf_p = jnp.pad(b_f, ((0, 0), (0, LANES - H)))
    gid = jnp.arange(CD) // HEAD_DIM
    pmat = jnp.where(gid[:, None] == gid[None, :], 1.0 / HEAD_DIM, 0.0).astype(BF16)

    gu_shape = jax.ShapeDtypeStruct((2, T, ff), BF16)
    gu_w_spec = pl.BlockSpec((None, D, guc), lambda s, i: (s, 0, 0))
    gu_o_spec = pl.BlockSpec((None, tm, guc), lambda s, i: (s // 2, i, s % 2))

    sid = jnp.bitwise_xor(chip, jnp.array([0, 2, 1, 3], jnp.int32)).astype(jnp.int32)
    (h0, n1), wgu1_h = _embed_norm(x, meta_f, ffn1_norm, tm=tm, name="embed_norm",
                                   comm=_gather_stage([wgu1_s], None, ici=(0, 1), whole=True))
    gu1, wgu1_h = _ffn_up(n1, wgu1_s[None], sid, None, tm=tm, first=0, count=1, name="ffn1_up_own",
                          comm=_gather_stage([wgu1_s], wgu1_h, ici=(2,), d2d=(0, 1), whole=True))
    gu1, out = _ffn_up(n1, wgu1_h[0], sid, gu1, tm=tm, first=1, count=2, name="ffn1_up_near",
                       comm=_join(_gather_stage((), wgu1_h, d2d=(2,)), _gather_ici([wd1_s, wout_s])))
    wgu1, down_w = out[0], out[1:]
    gu1, (wd1, wout_g) = _ffn_up(n1, wgu1, sid, gu1, tm=tm, first=3, count=1, name="ffn1_up_far",
                                 comm=_gather_d2d(down_w))
    wd1 = wd1.reshape(ff, D)
    (h1, n2), win_h = _ffn_down(gu1, wd1, h0, mix_norm, tm=tm, name="ffn1_down", comm=_gather_ici([win_s]))
    win_g, = _run_comm(_gather_d2d(win_h), name="gather_w_in")
    wout_f = wout_g.reshape(D, D)
    win_f = jnp.moveaxis(win_g, 0, 1).reshape(D, N_SHARD * ins)
    win_main = win_f[:, :n_main]
    win_fg = jnp.pad(win_f[:, n_main:], ((0, 0), (0, LANES - H)))

    proj, _ = _matmul_nn(n2, win_main, tm=tm, nb=n_main // (3 * CD),
                         w_spec=pl.BlockSpec((D, 3 * CD), lambda s, i: (0, s)),
                         out_shape=jax.ShapeDtypeStruct((T, n_main), BF16),
                         out_spec=pl.BlockSpec((tm, 3 * CD), lambda s, i: (i, s)), name="mix_in")
    fg, _ = _matmul_nn(n2, win_fg, tm=tm, nb=1, w_spec=pl.BlockSpec((D, LANES), lambda s, i: (0, 0)),
                       out_shape=jax.ShapeDtypeStruct((T, LANES), F32),
                       out_spec=pl.BlockSpec((tm, LANES), lambda s, i: (i, 0)), name="mix_in_fg")
    proj3 = proj.reshape(B, L, n_main)
    fg3 = fg.reshape(B, L, LANES)
    fc = _fcum(fg3, bf_p, ch=tm, name="forget_cumsum")
    fr = fc[:, :, :H].reshape(B, L // tm, tm, H).transpose(0, 1, 3, 2)
    (o, lse), ffn2_w = _attn_fwd(proj3, fc, fr, tq=tm, n_heads=H, name="attn_fwd",
                                 comm=_gather_ici([wgu2_s, wd2_s]))
    (h2, ymix, n3), (wgu2, wd2) = _mix_out(
        proj3, o, cw8, out_norm_conv, out_norm_attn, wout_f, h1.reshape(B, L, D), pmat, ffn2_norm,
        tm=tm, name="mix_out", comm=_gather_d2d(ffn2_w))
    wd2 = wd2.reshape(ff, D)
    h2 = h2.reshape(T, D)
    n3 = n3.reshape(T, D)

    gu2, _ = _matmul_nn(n3, wgu2, tm=tm, nb=N_SHARD, w_spec=gu_w_spec, out_shape=gu_shape, out_spec=gu_o_spec,
                        name="ffn2_up")
    (dh3f, dh3b, d_gf, loss_part), _ = _ffn_down_loss(gu2, wd2, h2, final_norm.reshape(1, D), loss_target,
                                                      tm=tm, name="ffn2_down_loss")

    c_arr = jnp.reshape(ci, (1,)).astype(jnp.int32)
    ks = jnp.arange(N_SHARD - 1, dtype=jnp.int32)
    idx = jnp.concatenate([jnp.stack([chip, ci]).astype(jnp.int32), ks + (ks >= chip).astype(jnp.int32)])

    def pair_sums(grads, got, names):
        return [_pair_sum(g, r, c_arr, name="pair_sum_" + nm) for g, r, nm in zip(grads, got, names)]

    def chip_sums(grads, got, landed, names):
        return [_chip_sum(g, r, l, idx, name="chip_sum_" + nm) for g, r, l, nm in zip(grads, got, landed, names)]

    def dw_up(n, dgu, name, comm=None):
        return _matmul_tn(
            n, dgu, tm=tm, nb=N_SHARD, kb=D, x_spec=pl.BlockSpec((tm, D), lambda s, i: (i, 0)),
            y_spec=pl.BlockSpec((None, tm, guc), lambda s, i: (s // 2, i, s % 2)),
            out_shape=jax.ShapeDtypeStruct((N_SHARD, D, guc), F32),
            out_spec=pl.BlockSpec((None, D, guc), lambda s, i: (s, 0, 0)), name=name, comm=comm)

    (dgu2, d_wd2), _ = _ffn_bwd_act(dh3b, gu2, wd2, tm=tm, guc=guc, name="ffn2_bwd_act")
    (dh2, dh2b, d_g3), _ = _ffn_bwd_in(dgu2, wgu2, h2, ffn2_norm, dh3f, tm=tm, scale=1.0, name="ffn2_bwd_in")
    d_wgu2, _ = dw_up(n3, dgu2, "ffn2_dw_up")
    grads_f2 = [d_wgu2, d_wd2.reshape(N_SHARD, ff // N_SHARD, D)]
    names_f2 = ["wgu2", "wd2"]

    dh2b3 = dh2b.reshape(B, L, D)
    (d_bg, d_cv, d_o, d_gc, d_ga, d_cw), got_f2 = _mix_out_bwd(
        dh2b3, proj3, o, cw8, out_norm_conv, out_norm_attn, wout_f, pmat, tm=tm, name="mix_out_bwd",
        comm=_swap_halves(grads_f2))
    sums_f2 = pair_sums(grads_f2, got_f2, names_f2)
    d_wout, _ = _matmul_tn(
        ymix.reshape(T, D), dh2b, tm=tm, nb=1, kb=D,
        x_spec=pl.BlockSpec((tm, D), lambda s, i: (i, 0)), y_spec=pl.BlockSpec((tm, D), lambda s, i: (i, 0)),
        out_shape=jax.ShapeDtypeStruct((D, D), F32), out_spec=pl.BlockSpec((D, D), lambda s, i: (0, 0)),
        name="dw_out")
    d_cc = _conv_bwd(d_cv, proj3, cw8, tm=tm, name="conv_bwd")
    (d_q, d_k, d_v, d_fk, d_fq), landed_f2 = _attn_bwd(proj3, o, d_o, lse, fc, fr, tq=tm, n_heads=H, name="attn_bwd",
                                                       comm=_scatter_chips(sums_f2))
    halves_f2 = chip_sums(grads_f2, got_f2, landed_f2, names_f2)
    d_fc = d_fq - d_fk
    d_fg, d_bf = _fcum_bwd(d_fc, fg3, bf_p, ch=tm, name="forget_cumsum_bwd")

    parts = [d_bg.reshape(T, CD), d_cc.reshape(T, 2 * CD), d_q.reshape(T, AD), d_k.reshape(T, AD),
             d_v.reshape(T, AD), d_fg.reshape(T, LANES)]
    (dh1, dh1b, d_gm, d_proj), g_f2 = _mix_bwd_in(parts, win_main, win_fg, h1, mix_norm, dh2, tm=tm, scale=0.5,
                                                  name="mix_bwd_in", comm=_share_halves(halves_f2))
    wide = d_proj.shape[1]
    d_win_nat, _ = _matmul_tn(
        n2, d_proj, tm=tm, nb=1, kb=D,
        x_spec=pl.BlockSpec((tm, D), lambda s, i: (i, 0)), y_spec=pl.BlockSpec((tm, wide), lambda s, i: (i, 0)),
        out_shape=jax.ShapeDtypeStruct((D, wide), F32), out_spec=pl.BlockSpec((D, wide), lambda s, i: (0, 0)),
        name="dw_in")
    d_win = jnp.moveaxis(d_win_nat[:, :N_SHARD * ins].reshape(D, N_SHARD, ins), 1, 0)
    grads_mx = [d_win, d_wout.reshape(N_SHARD, D // N_SHARD, D)]
    names_mx = ["win", "wout"]

    (dgu1, d_wd1), got_mx = _ffn_bwd_act(dh1b, gu1, wd1, tm=tm, guc=guc, name="ffn1_bwd_act",
                                         comm=_swap_halves(grads_mx))
    sums_mx = pair_sums(grads_mx, got_mx, names_mx)
    grads_d1 = [d_wd1.reshape(N_SHARD, ff // N_SHARD, D)]
    d_wgu1, out = dw_up(n1, dgu1, "ffn1_dw_up", comm=_join(_scatter_chips(sums_mx), _swap_halves(grads_d1)))
    landed_mx, got_d1 = out[:2], out[2:]
    halves_mx = chip_sums(grads_mx, got_mx, landed_mx, names_mx)
    sums_d1 = pair_sums(grads_d1, got_d1, ["wd1"])
    grads_u1 = [d_wgu1]
    (grad_x, d_meta, d_g1), out = _ffn_bwd_in_first(
        dgu1, wgu1, h0, ffn1_norm, dh1, tm=tm, batch=B, name="ffn1_bwd_in",
        comm=_join(_join(_share_halves(halves_mx), _scatter_chips(sums_d1)), _swap_halves(grads_u1)))
    g_mx, landed_d1, got_u1 = out[:2], out[2:3], out[3:]
    halves_d1 = chip_sums(grads_d1, got_d1, landed_d1, ["wd1"])
    sums_u1 = pair_sums(grads_u1, got_u1, ["wgu1"])
    out = _run_comm(_join(_share_halves(halves_d1), _scatter_chips(sums_u1)), name="scatter_ffn1")
    g_d1, landed_u1 = out[:1], out[1:]
    halves_u1 = chip_sums(grads_u1, got_u1, landed_u1, ["wgu1"])
    g_u1 = _run_comm(_share_halves(halves_u1), name="share_ffn1")
    g_big = [g_u1[0], g_d1[0], g_mx[0], g_mx[1], g_f2[0], g_f2[1]]

    loss_row = jnp.zeros((1, D), F32).at[0, 0].set(loss_part[0, 0])
    slab = _pack_small(D, d_meta, d_g1, d_gm, d_g3, d_gf, d_gc, d_ga, d_bf[:, :H], d_cw[:3])
    slab = slab.at[SMALL_ROWS - 1].set(loss_row[0])
    total = _all_reduce_small(slab, name="reduce_small")
    loss = total[SMALL_ROWS - 1, 0]
    mcols = meta_tokens.shape[-1]
    ccols = conv_w.shape[-1]
    full_like = (jnp.zeros((N_META, D)), ffn1_norm, mix_norm, ffn2_norm, final_norm.reshape(1, D), out_norm_conv,
                 out_norm_attn, b_f, jnp.zeros((1, 3, CD)))
    g_small = _unpack_small(total, full_like)
    g_small[0] = lax.dynamic_slice_in_dim(g_small[0], chip * mcols, mcols, axis=1)
    g_small[8] = lax.dynamic_slice_in_dim(g_small[8], chip * ccols, ccols, axis=2)

    def small_slab(meta, a1, am, a3, af, gc, ga, bf, cw):
        return _pack_small(D, meta, a1, am, a3, af.reshape(1, D), gc, ga, bf, cw[0])

    w_small = small_slab(meta_tokens, ffn1_norm, mix_norm, ffn2_norm, final_norm, out_norm_conv, out_norm_attn, b_f, conv_w)
    m_small = small_slab(m_meta_tokens, m_ffn1_norm, m_mix_norm, m_ffn2_norm, m_final_norm, m_out_norm_conv,
                         m_out_norm_attn, m_b_f, m_conv_w)
    v_small = small_slab(v_meta_tokens, v_ffn1_norm, v_mix_norm, v_ffn2_norm, v_final_norm, v_out_norm_conv,
                         v_out_norm_attn, v_b_f, v_conv_w)
    gs = list(g_small)
    gs[4] = gs[4].reshape(final_norm.shape)
    g_slab = small_slab(gs[0], gs[1], gs[2], gs[3], gs[4], gs[5], gs[6], gs[7], gs[8])
    local_like = (meta_tokens, ffn1_norm, mix_norm, ffn2_norm, final_norm.reshape(1, D), out_norm_conv, out_norm_attn,
                  b_f, conv_w)
    small_out = [_unpack_small(s, local_like)
                 for s in _adamw(w_small, g_slab, m_small, v_small, name="adamw_small")[1:]]
    for lst in small_out:
        lst[4] = lst[4].reshape(final_norm.shape)

    names = ["wgu1", "wd1", "win", "wout", "wgu2", "wd2"]
    w_big = big
    m_big = [m_ffn1_w_gu[0], m_ffn1_w_down[0], m_w_in[0], m_w_out[0], m_ffn2_w_gu[0], m_ffn2_w_down[0]]
    v_big = [v_ffn1_w_gu[0], v_ffn1_w_down[0], v_w_in[0], v_w_out[0], v_ffn2_w_gu[0], v_ffn2_w_down[0]]
    big_out = [_adamw(w, g, m, v, name="adamw_" + nm) for w, g, m, v, nm in zip(w_big, g_big, m_big, v_big, names)]

    def assemble(small, bigs):
        meta, a1, am, a3, af, gc, ga, bf, cw = small
        gu1_, d1_, win_, wout_, gu2_, d2_ = [b[None] for b in bigs]
        return [meta, a1, gu1_, d1_, am, win_, cw, bf, gc, ga, wout_, a3, gu2_, d2_, af]

    gs_out = list(g_small)
    gs_out[4] = gs_out[4].reshape(final_norm.shape)
    grads_out = assemble(gs_out, [b[0] for b in big_out])
    delta_out = assemble(small_out[0], [b[1] for b in big_out])
    m_out = assemble(small_out[1], [b[2] for b in big_out])
    v_out = assemble(small_out[2], [b[3] for b in big_out])
    return (loss, grad_x, *grads_out, *delta_out, *m_out, *v_out)
```

```python
import functools

import jax
import jax.numpy as jnp
from jax import lax
from jax.experimental import pallas as pl
from jax.experimental.pallas import tpu as pltpu

F32 = jnp.float32
BF16 = jnp.bfloat16

EPS = 1e-6
N_META = 16
HEAD_DIM = 64
N_SHARD = 4
N_DEV = 8
HALO = 16
LANES = 128
SMALL_ROWS = 32
VMEM_LIMIT_V7X = 56 * 1024 * 1024
NEG = -1e30
ATTN_BANDS = 2

ADAM_LR = 0.001
ADAM_B1 = 0.9
ADAM_B2 = 0.999
ADAM_EPS = 1e-08
ADAM_WD = 0.01
ADAM_STEP = 10

MESH = pl.DeviceIdType.MESH
ANY = pl.BlockSpec(memory_space=pl.ANY)
NT_DIMS = (((1,), (1,)), ((), ()))
TN_DIMS = (((0,), (0,)), ((), ()))


def _params(*sem):
    return pltpu.CompilerParams(dimension_semantics=sem, vmem_limit_bytes=VMEM_LIMIT_V7X)


class _Comm:
    def __init__(self, ins, out_shapes, sems, start, finish, aliases=None):
        self.ins, self.out_shapes, self.sems = list(ins), list(out_shapes), list(sems)
        self.start, self.finish, self.aliases = start, finish, dict(aliases or {})


def _join(a, b):
    ni, no, ns = len(a.ins), len(a.out_shapes), len(a.sems)

    def start(ins, outs, sems):
        a.start(ins[:ni], outs[:no], sems[:ns])
        b.start(ins[ni:], outs[no:], sems[ns:])

    def finish(ins, outs, sems):
        a.finish(ins[:ni], outs[:no], sems[:ns])
        b.finish(ins[ni:], outs[no:], sems[ns:])

    aliases = dict(a.aliases)
    aliases.update({ni + i: no + j for i, j in b.aliases.items()})
    return _Comm(a.ins + b.ins, a.out_shapes + b.out_shapes, a.sems + b.sems, start, finish, aliases)


def _launch(body, *, name, grid, in_specs, out_specs, out_shape, args, scratch_shapes=(), comm=None, prefetch=(),
            aliases=None):
    single = not isinstance(out_shape, (list, tuple))
    out_specs = [out_specs] if single else list(out_specs)
    out_shape = [out_shape] if single else list(out_shape)
    in_specs, scratch_shapes, prefetch = list(in_specs), list(scratch_shapes), list(prefetch)
    params = _params(*(("arbitrary",) * len(grid)))
    n_pf, n_in, n_out, n_scr = len(prefetch), len(in_specs), len(out_specs), len(scratch_shapes)
    c_ins = comm.ins if comm else []
    c_shapes = comm.out_shapes if comm else []
    c_sems = comm.sems if comm else []
    c_in, c_out = len(c_ins), len(c_shapes)

    def carrier(*refs):
        p = 0
        pf = refs[p:p + n_pf]; p += n_pf
        a = refs[p:p + n_in]; p += n_in
        ci = refs[p:p + c_in]; p += c_in
        o = refs[p:p + n_out]; p += n_out
        co = refs[p:p + c_out]; p += c_out
        s = refs[p:p + n_scr]; p += n_scr
        cs = refs[p:]
        if comm:
            first = functools.reduce(lambda u, v: u & v, [pl.program_id(k) == 0 for k in range(len(grid))])

            @pl.when(first)
            def _():
                comm.start(ci, co, cs)

        body(*pf, *a, *o, *s)

        if comm:
            last = functools.reduce(lambda u, v: u & v, [pl.program_id(k) == grid[k] - 1 for k in range(len(grid))])

            @pl.when(last)
            def _():
                comm.finish(ci, co, cs)

    io_aliases = {n_pf + i: j for i, j in (aliases or {}).items()}
    if comm:
        io_aliases.update({n_pf + n_in + i: n_out + j for i, j in comm.aliases.items()})
    all_in, all_out = in_specs + [ANY] * c_in, out_specs + [ANY] * c_out
    all_scratch = scratch_shapes + [pltpu.SemaphoreType.DMA((k,)) for k in c_sems]
    if n_pf:
        spec = dict(grid_spec=pltpu.PrefetchScalarGridSpec(
            num_scalar_prefetch=n_pf, grid=grid, in_specs=all_in, out_specs=all_out, scratch_shapes=all_scratch))
    else:
        spec = dict(grid=grid, in_specs=all_in, out_specs=all_out, scratch_shapes=all_scratch)
    res = pl.pallas_call(carrier, name=name, out_shape=out_shape + c_shapes, input_output_aliases=io_aliases,
                         compiler_params=params, **spec)(*prefetch, *args, *c_ins)
    main = list(res[:n_out])
    return (main[0] if single else main), (list(res[n_out:]) if comm else None)


def _run_comm(comm, *, name):
    c_in, c_out = len(comm.ins), len(comm.out_shapes)

    def body(*refs):
        ci, co, cs = refs[:c_in], refs[c_in:c_in + c_out], refs[c_in + c_out:]
        comm.start(ci, co, cs)
        comm.finish(ci, co, cs)

    return list(pl.pallas_call(
        body, name=name, in_specs=[ANY] * c_in, out_specs=[ANY] * c_out, out_shape=comm.out_shapes,
        scratch_shapes=[pltpu.SemaphoreType.DMA((k,)) for k in comm.sems],
        input_output_aliases=comm.aliases)(*comm.ins))


def _chunks(width, step=512):
    out, c0 = [], 0
    while c0 < width:
        cw = min(step, width - c0)
        out.append((c0, cw))
        c0 += cw
    return out


def _split2(v):
    hi = v.astype(BF16)
    lo = (v - hi.astype(F32)).astype(BF16)
    return hi, lo


def _split3(v):
    hi = v.astype(BF16)
    r = v - hi.astype(F32)
    mid = r.astype(BF16)
    lo = (r - mid.astype(F32)).astype(BF16)
    return hi, mid, lo


def _dot(a, b):
    return jnp.dot(a, b, preferred_element_type=F32)


def _dot_nt(a, b):
    return lax.dot_general(a, b, NT_DIMS, preferred_element_type=F32)


def _dot_tn(a, b):
    return lax.dot_general(a, b, TN_DIMS, preferred_element_type=F32)


def _silu_mul(g, u):
    return g * jax.nn.sigmoid(g) * u


def _rms_bwd(dn, h, gain, dres):
    r = lax.rsqrt(jnp.mean(h * h, axis=-1, keepdims=True) + EPS)
    y = h * r
    dgain = jnp.sum(dn * y, axis=0, keepdims=True)
    dy = dn * gain
    dh = dres + r * (dy - y * jnp.mean(dy * y, axis=-1, keepdims=True))
    return dh, dgain


def _group_mean(v, p):
    hi, lo = _split2(v)
    return _dot(hi, p) + _dot(lo, p)


def _row_of(a, k):
    rows = lax.broadcasted_iota(jnp.int32, a.shape, 0)
    return jnp.sum(jnp.where(rows == k, a, 0.0), axis=0, keepdims=True)


def _causal_conv(u, prev, w):
    rows = lax.broadcasted_iota(jnp.int32, u.shape, 0)
    p1 = _row_of(prev, HALO - 1)
    p2 = _row_of(prev, HALO - 2)
    u1 = jnp.where(rows == 0, p1, pltpu.roll(u, 1, 0))
    u2 = jnp.where(rows == 0, p2, jnp.where(rows == 1, p1, pltpu.roll(u, 2, 0)))
    return w[2:3, :] * u + w[1:2, :] * u1 + w[0:1, :] * u2, u1, u2


def _rms(x, gain):
    return (x * lax.rsqrt(jnp.mean(x * x, axis=-1, keepdims=True) + EPS) * gain).astype(BF16)


def _embed_norm(x, meta, g, *, tm, name, comm=None):
    B, S, D = x.shape
    L = S + N_META
    per_seq = L // tm
    nt = B * per_seq
    body_rows = tm - N_META

    def body(meta_ref, g_ref, x_hbm, h_ref, n_ref, buf, sems):
        i = pl.program_id(0)

        def fetch(k, fn):
            slot, b, t = k % 2, k // per_seq, k % per_seq

            @pl.when(t == 0)
            def _():
                fn(pltpu.make_async_copy(x_hbm.at[b, pl.ds(0, body_rows)],
                                         buf.at[slot, pl.ds(N_META, body_rows)], sems.at[slot]))

            @pl.when(t != 0)
            def _():
                fn(pltpu.make_async_copy(x_hbm.at[b, pl.ds(pl.multiple_of(t * tm - N_META, 8), tm)],
                                         buf.at[slot], sems.at[slot]))

        @pl.when(i == 0)
        def _():
            fetch(i, lambda cp: cp.start())

        @pl.when(i + 1 < nt)
        def _():
            fetch(i + 1, lambda cp: cp.start())

        fetch(i, lambda cp: cp.wait())
        slot = i % 2

        @pl.when(i % per_seq == 0)
        def _():
            buf[slot, 0:N_META, :] = meta_ref[...]

        hv = buf[slot]
        h_ref[...] = hv
        n_ref[...] = _rms(hv, g_ref[...])

    row = pl.BlockSpec((tm, D), lambda i: (i, 0))
    return _launch(
        body, name=name, grid=(nt,),
        in_specs=[pl.BlockSpec((N_META, D), lambda i: (0, 0)), pl.BlockSpec((1, D), lambda i: (0, 0)), ANY],
        out_specs=[row, row],
        out_shape=[jax.ShapeDtypeStruct((B * L, D), F32), jax.ShapeDtypeStruct((B * L, D), BF16)],
        scratch_shapes=[pltpu.VMEM((2, tm, D), F32), pltpu.SemaphoreType.DMA((2,))],
        args=(meta, g, x), comm=comm)


def _ffn_up(n, wgu, sid, gu_prev, *, tm, first, count, name, comm=None):
    T, D = n.shape
    ns, _, guc = wgu.shape
    ff = N_SHARD * guc // 2

    def body(sid_ref, x_ref, w_ref, *rest):
        rest[-1][...] = _dot(x_ref[...], w_ref[...]).astype(BF16)

    where = lambda s, sid: sid[first + s]
    w_at = (lambda s, sid: 0) if ns == 1 else where
    return _launch(
        body, name=name, grid=(count, T // tm), prefetch=(sid,),
        in_specs=[pl.BlockSpec((tm, D), lambda s, i, sid: (i, 0)),
                  pl.BlockSpec((None, D, guc), lambda s, i, sid: (w_at(s, sid), 0, 0))]
                 + ([] if gu_prev is None else [ANY]),
        out_specs=pl.BlockSpec((None, tm, guc), lambda s, i, sid: (where(s, sid) // 2, i, where(s, sid) % 2)),
        out_shape=jax.ShapeDtypeStruct((2, T, ff), BF16),
        args=(n, wgu) + (() if gu_prev is None else (gu_prev,)),
        aliases=None if gu_prev is None else {2: 0}, comm=comm)


def _matmul_nn(x, w, *, tm, nb, w_spec, out_shape, out_spec, name, comm=None):
    T, K = x.shape

    def body(x_ref, w_ref, o_ref):
        o_ref[...] = _dot(x_ref[...], w_ref[...]).astype(o_ref.dtype)

    return _launch(
        body, name=name, grid=(nb, T // tm),
        in_specs=[pl.BlockSpec((tm, K), lambda s, i: (i, 0)), w_spec],
        out_specs=out_spec, out_shape=out_shape, args=(x, w), comm=comm)


def _ffn_down(gu, wd, h, next_gain, *, tm, name, comm=None):
    _, T, ff = gu.shape
    D = h.shape[1]
    chunks = _chunks(ff)

    def body(g_ref, u_ref, wd_hbm, h_ref, ng_ref, o_ref, n_ref, wd_v, sem):
        @pl.when(pl.program_id(0) == 0)
        def _():
            cp = pltpu.make_async_copy(wd_hbm, wd_v, sem)
            cp.start()
            cp.wait()

        def act(c0, cw):
            return _silu_mul(g_ref[:, c0:c0 + cw].astype(F32), u_ref[:, c0:c0 + cw].astype(F32)).astype(BF16)

        acc = jnp.zeros((tm, D), F32)
        nxt = act(*chunks[0])
        for k, (c0, cw) in enumerate(chunks):
            a = nxt
            if k + 1 < len(chunks):
                nxt = act(*chunks[k + 1])
            acc = acc + _dot(a, wd_v[c0:c0 + cw, :])
        out = h_ref[...] + 0.5 * acc
        o_ref[...] = out
        n_ref[...] = _rms(out, ng_ref[...])

    return _launch(
        body, name=name, grid=(T // tm,),
        in_specs=[pl.BlockSpec((None, tm, ff), lambda i: (0, i, 0)),
                  pl.BlockSpec((None, tm, ff), lambda i: (1, i, 0)),
                  ANY,
                  pl.BlockSpec((tm, D), lambda i: (i, 0)),
                  pl.BlockSpec((1, D), lambda i: (0, 0))],
        out_specs=[pl.BlockSpec((tm, D), lambda i: (i, 0)), pl.BlockSpec((tm, D), lambda i: (i, 0))],
        out_shape=[jax.ShapeDtypeStruct((T, D), F32), jax.ShapeDtypeStruct((T, D), BF16)],
        scratch_shapes=[pltpu.VMEM((ff, D), BF16), pltpu.SemaphoreType.DMA],
        args=(gu, gu, wd, h, next_gain), comm=comm)


def _ffn_down_loss(gu, wd, h, gf, tgt, *, tm, name, comm=None):
    _, T, ff = gu.shape
    D = h.shape[1]
    B, S, _ = tgt.shape
    per_seq = (S + N_META) // tm
    body_rows = tm - N_META
    chunks = _chunks(ff)

    def body(g_ref, u_ref, wd_hbm, h_ref, gf_ref, tgt_hbm, dh_ref, dhb_ref, dg_ref, loss_ref, wd_v, tg_v, sem, tsem):
        i = pl.program_id(0)
        b, t = i // per_seq, i % per_seq

        @pl.when(i == 0)
        def _():
            cp = pltpu.make_async_copy(wd_hbm, wd_v, sem)
            cp.start()
            cp.wait()
            dg_ref[...] = jnp.zeros_like(dg_ref)
            loss_ref[...] = jnp.zeros_like(loss_ref)
            tg_v[0:N_META, :] = jnp.zeros((N_META, D), F32)

        def fetch(fn):
            @pl.when(t == 0)
            def _():
                fn(pltpu.make_async_copy(tgt_hbm.at[b, pl.ds(0, body_rows)], tg_v.at[pl.ds(N_META, body_rows)], tsem))

            @pl.when(t != 0)
            def _():
                fn(pltpu.make_async_copy(tgt_hbm.at[b, pl.ds(pl.multiple_of(t * tm - N_META, 8), tm)], tg_v, tsem))

        fetch(lambda cp: cp.start())
        def act(c0, cw):
            return _silu_mul(g_ref[:, c0:c0 + cw].astype(F32), u_ref[:, c0:c0 + cw].astype(F32)).astype(BF16)

        acc = jnp.zeros((tm, D), F32)
        nxt = act(*chunks[0])
        for k, (c0, cw) in enumerate(chunks):
            a = nxt
            if k + 1 < len(chunks):
                nxt = act(*chunks[k + 1])
            acc = acc + _dot(a, wd_v[c0:c0 + cw, :])
        x = h_ref[...] + 0.5 * acc
        fetch(lambda cp: cp.wait())

        gain = gf_ref[...]
        r = lax.rsqrt(jnp.mean(x * x, axis=-1, keepdims=True) + EPS)
        y = x * r
        pos = t * tm + lax.broadcasted_iota(jnp.int32, (tm, 1), 0)
        err = jnp.where(pos >= N_META, y * gain - tg_v[...], 0.0)
        loss_ref[...] += 0.5 * jnp.sum(jnp.mean(err * err, axis=-1, keepdims=True))
        dout = err / D
        dg_ref[...] += jnp.sum(dout * y, axis=0, keepdims=True)
        dy = dout * gain
        dh = r * (dy - y * jnp.mean(dy * y, axis=-1, keepdims=True))
        dh_ref[...] = dh
        dhb_ref[...] = (0.5 * dh).astype(BF16)

    row = pl.BlockSpec((tm, D), lambda i: (i, 0))
    const = lambda i: (0, 0)
    return _launch(
        body, name=name, grid=(T // tm,),
        in_specs=[pl.BlockSpec((None, tm, ff), lambda i: (0, i, 0)),
                  pl.BlockSpec((None, tm, ff), lambda i: (1, i, 0)),
                  ANY, row, pl.BlockSpec((1, D), const), ANY],
        out_specs=[row, row, pl.BlockSpec((1, D), const), pl.BlockSpec((1, LANES), const)],
        out_shape=[jax.ShapeDtypeStruct((T, D), F32), jax.ShapeDtypeStruct((T, D), BF16),
                   jax.ShapeDtypeStruct((1, D), F32), jax.ShapeDtypeStruct((1, LANES), F32)],
        scratch_shapes=[pltpu.VMEM((ff, D), BF16), pltpu.VMEM((tm, D), F32), pltpu.SemaphoreType.DMA,
                        pltpu.SemaphoreType.DMA],
        args=(gu, gu, wd, h, gf, tgt), comm=comm)


def _ffn_bwd_act(df, gu, wd, *, tm, guc, name, comm=None):
    _, T, ff = gu.shape
    D = df.shape[1]
    nj = ff // guc
    chunks = _chunks(guc)

    def body(df_ref, g_ref, u_ref, wd_ref, o_ref, dwd_ref):
        @pl.when(pl.program_id(1) == 0)
        def _():
            dwd_ref[...] = jnp.zeros_like(dwd_ref)

        dfv = df_ref[...]
        nxt = _dot_nt(dfv, wd_ref[chunks[0][0]:chunks[0][0] + chunks[0][1], :])
        for k, (c0, cw) in enumerate(chunks):
            da = nxt
            if k + 1 < len(chunks):
                n0, nw = chunks[k + 1]
                nxt = _dot_nt(dfv, wd_ref[n0:n0 + nw, :])
            g = g_ref[:, c0:c0 + cw].astype(F32)
            u = u_ref[:, c0:c0 + cw].astype(F32)
            sg = jax.nn.sigmoid(g)
            silu = g * sg
            o_ref[0, :, c0:c0 + cw] = (da * u * (sg * (1.0 + g * (1.0 - sg)))).astype(BF16)
            o_ref[1, :, c0:c0 + cw] = (da * silu).astype(BF16)
            dwd_ref[c0:c0 + cw, :] += _dot_tn((silu * u).astype(BF16), dfv)

    return _launch(
        body, name=name, grid=(nj, T // tm),
        in_specs=[pl.BlockSpec((tm, D), lambda j, i: (i, 0)),
                  pl.BlockSpec((None, tm, guc), lambda j, i: (0, i, j)),
                  pl.BlockSpec((None, tm, guc), lambda j, i: (1, i, j)),
                  pl.BlockSpec((guc, D), lambda j, i: (j, 0))],
        out_specs=[pl.BlockSpec((2, tm, guc), lambda j, i: (0, i, j)), pl.BlockSpec((guc, D), lambda j, i: (j, 0))],
        out_shape=[jax.ShapeDtypeStruct((2, T, ff), BF16), jax.ShapeDtypeStruct((ff, D), F32)],
        args=(df, gu, gu, wd), comm=comm)


def _ffn_bwd_in(dgu, wgu, h, g, dres, *, tm, scale, name, comm=None):
    _, T, ff = dgu.shape
    ns, D, guc = wgu.shape
    nj = ff // guc
    edges = _band_edges(tm)

    def body(dgu_ref, w_hbm, h_ref, g_ref, dres_ref, dh_ref, dhb_ref, dg_ref, w_v, acc, sem):
        i, j = pl.program_id(0), pl.program_id(1)

        @pl.when((i == 0) & (j == 0))
        def _():
            cp = pltpu.make_async_copy(w_hbm, w_v, sem)
            cp.start()
            cp.wait()
            dg_ref[...] = jnp.zeros_like(dg_ref)

        def dots(rows):
            return _dot_nt(dgu_ref[0, rows, :], w_v[j]) + _dot_nt(dgu_ref[1, rows, :], w_v[nj + j])

        @pl.when(j < nj - 1)
        def _():
            part = dots(slice(None))

            @pl.when(j == 0)
            def _():
                acc[...] = part

            @pl.when(j > 0)
            def _():
                acc[...] += part

        @pl.when(j == nj - 1)
        def _():
            bands = [slice(r0, r1) for r0, r1 in zip(edges[:-1], edges[1:])]
            nxt = dots(bands[0])
            for b, rows in enumerate(bands):
                dn = nxt if nj == 1 else acc[rows, :] + nxt
                if b + 1 < len(bands):
                    nxt = dots(bands[b + 1])
                dh, dgain = _rms_bwd(dn, h_ref[rows, :], g_ref[...], dres_ref[rows, :])
                dh_ref[rows, :] = dh
                dhb_ref[rows, :] = (scale * dh).astype(BF16)
                dg_ref[...] += dgain

    return _launch(
        body, name=name, grid=(T // tm, nj),
        in_specs=[pl.BlockSpec((2, tm, guc), lambda i, j: (0, i, j)),
                  ANY,
                  pl.BlockSpec((tm, D), lambda i, j: (i, 0)),
                  pl.BlockSpec((1, D), lambda i, j: (0, 0)),
                  pl.BlockSpec((tm, D), lambda i, j: (i, 0))],
        out_specs=[pl.BlockSpec((tm, D), lambda i, j: (i, 0)),
                   pl.BlockSpec((tm, D), lambda i, j: (i, 0)),
                   pl.BlockSpec((1, D), lambda i, j: (0, 0))],
        out_shape=[jax.ShapeDtypeStruct((T, D), F32), jax.ShapeDtypeStruct((T, D), BF16),
                   jax.ShapeDtypeStruct((1, D), F32)],
        scratch_shapes=[pltpu.VMEM((ns, D, guc), BF16), pltpu.VMEM((tm, D), F32), pltpu.SemaphoreType.DMA],
        args=(dgu, wgu, h, g, dres), comm=comm)


def _ffn_bwd_in_first(dgu, wgu, h, g, dres, *, tm, batch, name, comm=None):
    _, T, ff = dgu.shape
    ns, D, guc = wgu.shape
    nj = ff // guc
    nt = T // tm
    L = T // batch
    per_seq = L // tm
    body_rows = tm - N_META
    chunks = _chunks(guc)

    def body(dgu_ref, w_hbm, h_ref, g_ref, dres_ref, dx_hbm, dmeta_ref, dg_ref, w_v, acc, dh_v, sem, osem):
        i, j = pl.program_id(0), pl.program_id(1)

        @pl.when((i == 0) & (j == 0))
        def _():
            cp = pltpu.make_async_copy(w_hbm, w_v, sem)
            cp.start()
            cp.wait()
            dg_ref[...] = jnp.zeros_like(dg_ref)
            dmeta_ref[...] = jnp.zeros_like(dmeta_ref)

        part = _dot_nt(dgu_ref[0], w_v[j]) + _dot_nt(dgu_ref[1], w_v[nj + j])

        @pl.when(j == 0)
        def _():
            acc[...] = part

        @pl.when(j > 0)
        def _():
            acc[...] += part

        def head_copy(b):
            return pltpu.make_async_copy(dh_v.at[pl.ds(N_META, body_rows)], dx_hbm.at[b, pl.ds(0, body_rows)], osem)

        def tail_copy(b, t):
            return pltpu.make_async_copy(dh_v, dx_hbm.at[b, pl.ds(pl.multiple_of(t * tm - N_META, 8), tm)], osem)

        def on_tile(k, head_fn, tail_fn):
            @pl.when(k % per_seq == 0)
            def _():
                head_fn(head_copy(k // per_seq))

            @pl.when(k % per_seq != 0)
            def _():
                tail_fn(tail_copy(k // per_seq, k % per_seq))

        @pl.when(j == nj - 1)
        def _():
            dh, dgain = _rms_bwd(acc[...], h_ref[...], g_ref[...], dres_ref[...])
            dg_ref[...] += dgain

            @pl.when(i > 0)
            def _():
                on_tile(i - 1, lambda cp: cp.wait(), lambda cp: cp.wait())

            dh_v[...] = dh

            @pl.when(i % per_seq == 0)
            def _():
                dmeta_ref[...] += dh[0:N_META, :]

            on_tile(i, lambda cp: cp.start(), lambda cp: cp.start())

            @pl.when(i == nt - 1)
            def _():
                on_tile(i, lambda cp: cp.wait(), lambda cp: cp.wait())

    return _launch(
        body, name=name, grid=(nt, nj),
        in_specs=[pl.BlockSpec((2, tm, guc), lambda i, j: (0, i, j)),
                  ANY,
                  pl.BlockSpec((tm, D), lambda i, j: (i, 0)),
                  pl.BlockSpec((1, D), lambda i, j: (0, 0)),
                  pl.BlockSpec((tm, D), lambda i, j: (i, 0))],
        out_specs=[ANY, pl.BlockSpec((N_META, D), lambda i, j: (0, 0)), pl.BlockSpec((1, D), lambda i, j: (0, 0))],
        out_shape=[jax.ShapeDtypeStruct((batch, L - N_META, D), F32), jax.ShapeDtypeStruct((N_META, D), F32),
                   jax.ShapeDtypeStruct((1, D), F32)],
        scratch_shapes=[pltpu.VMEM((ns, D, guc), BF16), pltpu.VMEM((tm, D), F32), pltpu.VMEM((tm, D), F32),
                        pltpu.SemaphoreType.DMA, pltpu.SemaphoreType.DMA],
        args=(dgu, wgu, h, g, dres), comm=comm)


def _mix_bwd_in(parts, w_main, w_fg, h, g, dres, *, tm, scale, name, comm=None):
    T, D = h.shape
    widths = [p.shape[1] for p in parts]
    offs = [sum(widths[:k]) for k in range(len(widths))]
    npart = len(parts)
    wide = sum(widths)
    edges = _band_edges(tm)

    def body(*refs):
        p_refs = refs[:npart]
        wm_ref, wf_ref, h_ref, g_ref, dres_ref, dh_ref, dhb_ref, dg_ref, all_ref = refs[npart:]

        @pl.when(pl.program_id(0) == 0)
        def _():
            dg_ref[...] = jnp.zeros_like(dg_ref)

        for p_ref, off, wd_ in zip(p_refs, offs, widths):
            for c0, cw in _chunks(wd_):
                all_ref[:, off + c0:off + c0 + cw] = p_ref[:, c0:c0 + cw].astype(BF16)
        n_main = offs[-1]

        def dots(rows):
            return _dot_nt(all_ref[rows, :n_main], wm_ref[...]) + _dot_nt(all_ref[rows, n_main:], wf_ref[...])

        bands = [slice(r0, r1) for r0, r1 in zip(edges[:-1], edges[1:])]
        nxt = dots(bands[0])
        for b, rows in enumerate(bands):
            dn = nxt
            if b + 1 < len(bands):
                nxt = dots(bands[b + 1])
            dh, dgain = _rms_bwd(dn, h_ref[rows, :], g_ref[...], dres_ref[rows, :])
            dh_ref[rows, :] = dh
            dhb_ref[rows, :] = (scale * dh).astype(BF16)
            dg_ref[...] += dgain

    row = lambda i: (i, 0)
    const = lambda i: (0, 0)
    return _launch(
        body, name=name, grid=(T // tm,),
        in_specs=[pl.BlockSpec((tm, p.shape[1]), row) for p in parts]
                 + [pl.BlockSpec(w_main.shape, const), pl.BlockSpec(w_fg.shape, const),
                    pl.BlockSpec((tm, D), row), pl.BlockSpec((1, D), const), pl.BlockSpec((tm, D), row)],
        out_specs=[pl.BlockSpec((tm, D), row), pl.BlockSpec((tm, D), row), pl.BlockSpec((1, D), const),
                   pl.BlockSpec((tm, wide), row)],
        out_shape=[jax.ShapeDtypeStruct((T, D), F32), jax.ShapeDtypeStruct((T, D), BF16),
                   jax.ShapeDtypeStruct((1, D), F32), jax.ShapeDtypeStruct((T, wide), BF16)],
        args=(*parts, w_main, w_fg, h, g, dres), comm=comm)


def _matmul_tn(x, y, *, tm, nb, x_spec, y_spec, out_shape, out_spec, kb, name, comm=None):
    T = y.shape[-2]
    chunks = _chunks(kb)

    def body(x_ref, y_ref, o_ref):
        @pl.when(pl.program_id(1) == 0)
        def _():
            o_ref[...] = jnp.zeros_like(o_ref)

        yv = y_ref[...].astype(BF16)
        nxt = _dot_tn(x_ref[:, chunks[0][0]:chunks[0][0] + chunks[0][1]], yv)
        for k, (c0, cw) in enumerate(chunks):
            cur = nxt
            if k + 1 < len(chunks):
                n0, nw = chunks[k + 1]
                nxt = _dot_tn(x_ref[:, n0:n0 + nw], yv)
            o_ref[c0:c0 + cw, :] += cur

    return _launch(
        body, name=name, grid=(nb, T // tm),
        in_specs=[x_spec, y_spec], out_specs=out_spec, out_shape=out_shape, args=(x, y), comm=comm)


def _tri(n, lower):
    r = lax.broadcasted_iota(jnp.int32, (n, n), 0)
    c = lax.broadcasted_iota(jnp.int32, (n, n), 1)
    return jnp.where((r >= c) if lower else (r <= c), 1.0, 0.0).astype(BF16)


def _tri_dot(tri, v):
    hi, mid, lo = _split3(v)
    return _dot(tri, hi) + _dot(tri, mid) + _dot(tri, lo)


def _fcum(fg, bf, *, ch, name):
    B, L, W = fg.shape
    nch = L // ch

    def body(fg_ref, bf_ref, f_ref):
        tri = _tri(ch, True)
        carry = jnp.zeros((1, W), F32)
        for c in range(nch):
            x = fg_ref[c * ch:(c + 1) * ch, :] + bf_ref[...]
            lf = jnp.minimum(x, 0.0) - jnp.log(1.0 + jnp.exp(-jnp.abs(x)))
            f_ref[c * ch:(c + 1) * ch, :] = _tri_dot(tri, lf) + carry
            carry = carry + jnp.sum(lf, axis=0, keepdims=True)

    return pl.pallas_call(
        body, name=name, grid=(B,),
        in_specs=[pl.BlockSpec((None, L, W), lambda b: (b, 0, 0)), pl.BlockSpec((1, W), lambda b: (0, 0))],
        out_specs=pl.BlockSpec((None, L, W), lambda b: (b, 0, 0)),
        out_shape=jax.ShapeDtypeStruct((B, L, W), F32),
        compiler_params=_params("arbitrary"),
    )(fg, bf)


def _fcum_bwd(dF, fg, bf, *, ch, name):
    B, L, W = fg.shape
    nch = L // ch

    def body(df_ref, fg_ref, bf_ref, dfg_ref, db_ref):
        @pl.when(pl.program_id(0) == 0)
        def _():
            db_ref[...] = jnp.zeros_like(db_ref)

        tri = _tri(ch, False)
        carry = jnp.zeros((1, W), F32)
        dbs = jnp.zeros((1, W), F32)
        for c in reversed(range(nch)):
            d = df_ref[c * ch:(c + 1) * ch, :]
            dlf = _tri_dot(tri, d) + carry
            carry = carry + jnp.sum(d, axis=0, keepdims=True)
            x = fg_ref[c * ch:(c + 1) * ch, :] + bf_ref[...]
            dfg = dlf * jax.nn.sigmoid(-x)
            dfg_ref[c * ch:(c + 1) * ch, :] = dfg.astype(BF16)
            dbs = dbs + jnp.sum(dfg, axis=0, keepdims=True)
        db_ref[...] += dbs

    blk = pl.BlockSpec((None, L, W), lambda b: (b, 0, 0))
    return pl.pallas_call(
        body, name=name, grid=(B,),
        in_specs=[blk, blk, pl.BlockSpec((1, W), lambda b: (0, 0))],
        out_specs=[blk, pl.BlockSpec((1, W), lambda b: (0, 0))],
        out_shape=[jax.ShapeDtypeStruct((B, L, W), BF16), jax.ShapeDtypeStruct((1, W), F32)],
        compiler_params=_params("arbitrary"),
    )(dF, fg, bf)


def _band_edges(tq):
    return sorted({min(tq, (k * tq // ATTN_BANDS + HALO - 1) // HALO * HALO) for k in range(ATTN_BANDS + 1)})


def _pair(h):
    return slice((h // 2) * 2 * HEAD_DIM, (h // 2 + 1) * 2 * HEAD_DIM)


def _own_lanes(a, h):
    low = lax.broadcasted_iota(jnp.int32, a.shape, 1) < HEAD_DIM
    return jnp.where(low if h % 2 == 0 else jnp.logical_not(low), a, jnp.zeros_like(a))


def _sum_lane(h):
    return HEAD_DIM if h % 2 == 0 else 0


def _own_lanes_and_ones(a, h):
    lane = lax.broadcasted_iota(jnp.int32, a.shape, 1)
    low = lane < HEAD_DIM
    return jnp.where(low if h % 2 == 0 else jnp.logical_not(low), a,
                     jnp.where(lane == _sum_lane(h), jnp.ones_like(a), jnp.zeros_like(a)))


def _attn_fwd(proj, fc, fr, *, tq, n_heads, name, comm=None):
    B, L, _ = proj.shape
    AD = n_heads * HEAD_DIM
    nq = L // tq
    W = fc.shape[-1]
    scale = HEAD_DIM ** -0.5
    edges = _band_edges(tq)

    v_ones, sum_lane = _own_lanes_and_ones, _sum_lane

    def body(q_ref, k_ref, v_ref, fr_ref, o_ref, lse_ref, m_s, acc_s):
        qi, ki = pl.program_id(1), pl.program_id(2)

        @pl.when(ki == 0)
        def _():
            m_s[...] = jnp.full_like(m_s, NEG)
            acc_s[...] = jnp.zeros_like(acc_s)

        def tile(diagonal):
            lane = lax.broadcasted_iota(jnp.int32, (tq, W), 1)
            m_all = m_s[...]
            m_out = m_all
            bands = [(r0, r1, r1 if diagonal else tq) for r0, r1 in zip(edges[:-1], edges[1:])]
            if diagonal:
                masks = {r0: (lax.broadcasted_iota(jnp.int32, (r1 - r0, c1), 1)
                              <= r0 + lax.broadcasted_iota(jnp.int32, (r1 - r0, c1), 0)) for r0, r1, c1 in bands}

            def scores(h, band):
                r0, r1, c1 = band
                sl = slice(h * HEAD_DIM, (h + 1) * HEAD_DIM)
                return _dot_nt(q_ref[r0:r1, sl] * scale, k_ref[0:c1, sl])

            work = [(h, band) for h in range(n_heads) for band in bands]
            nxt = scores(*work[0])
            for w, (h, band) in enumerate(work):
                r0, r1, c1 = band
                sl = slice(h * HEAD_DIM, (h + 1) * HEAD_DIM)
                s = nxt - fr_ref[h:h + 1, 0:c1]
                if w + 1 < len(work):
                    nxt = scores(*work[w + 1])
                if diagonal:
                    s = jnp.where(masks[r0], s, NEG)
                m_old = m_all[r0:r1, h:h + 1]
                m_new = jnp.maximum(m_old, jnp.max(s, axis=1, keepdims=True))
                alpha = jnp.exp(m_old - m_new)
                p = jnp.exp(s - m_new)
                own = slice(h * 2 * HEAD_DIM, (h + 1) * 2 * HEAD_DIM)
                acc_s[r0:r1, own] = alpha * acc_s[r0:r1, own] + _dot(p.astype(BF16), v_ones(v_ref[0:c1, _pair(h)], h))
                if r0 == 0:
                    m_parts = []
                m_parts.append(m_new)
                if r1 == tq:
                    m_out = jnp.where(lane == h, jnp.concatenate(m_parts, axis=0), m_out)
            m_s[...] = m_out

        @pl.when(ki < qi)
        def _():
            tile(False)

        @pl.when(ki == qi)
        def _():
            tile(True)
            lane = lax.broadcasted_iota(jnp.int32, (tq, W), 1)
            low = lax.broadcasted_iota(jnp.int32, (tq, 2 * HEAD_DIM), 1) < HEAD_DIM
            l_all = jnp.ones((tq, W), F32)
            for h in range(0, n_heads, 2):
                even = acc_s[:, h * 2 * HEAD_DIM:(h + 1) * 2 * HEAD_DIM]
                odd = acc_s[:, (h + 1) * 2 * HEAD_DIM:(h + 2) * 2 * HEAD_DIM]
                l_even = even[:, sum_lane(h):sum_lane(h) + 1]
                l_odd = odd[:, sum_lane(h + 1):sum_lane(h + 1) + 1]
                o_ref[:, _pair(h)] = jnp.where(low, even / l_even, odd / l_odd)
                l_all = jnp.where(lane == h, l_even, jnp.where(lane == h + 1, l_odd, l_all))
            lse_ref[...] = jnp.where(lane < n_heads, m_s[...] + jnp.log(l_all), 0.0)

    kv = lambda b, qi, ki: jnp.minimum(ki, qi)
    return _launch(
        body, name=name, grid=(B, nq, nq), args=(proj, proj, proj, fr), comm=comm,
        in_specs=[pl.BlockSpec((None, tq, AD), lambda b, qi, ki: (b, qi, 3)),
                  pl.BlockSpec((None, tq, AD), lambda b, qi, ki: (b, kv(b, qi, ki), 4)),
                  pl.BlockSpec((None, tq, AD), lambda b, qi, ki: (b, kv(b, qi, ki), 5)),
                  pl.BlockSpec((None, None, n_heads, tq), lambda b, qi, ki: (b, kv(b, qi, ki), 0, 0))],
        out_specs=[pl.BlockSpec((None, tq, AD), lambda b, qi, ki: (b, qi, 0)),
                   pl.BlockSpec((None, tq, W), lambda b, qi, ki: (b, qi, 0))],
        out_shape=[jax.ShapeDtypeStruct((B, L, AD), F32), jax.ShapeDtypeStruct((B, L, W), F32)],
        scratch_shapes=[pltpu.VMEM((tq, W), F32), pltpu.VMEM((tq, n_heads * 2 * HEAD_DIM), F32)])


def _attn_bwd(proj, o, do, lse, fc, fr, *, tq, n_heads, name, comm=None):
    B, L, _ = proj.shape
    AD = n_heads * HEAD_DIM
    nq = L // tq
    W = fc.shape[-1]
    HW = 2 * HEAD_DIM
    scale = HEAD_DIM ** -0.5
    edges = _band_edges(tq)

    def body(q_ref, k_ref, v_ref, o_ref, do_ref, lse_ref, fr_ref,
             dq_ref, dk_ref, dv_ref, dfk_ref, dfq_ref, dq_s, dk_s, dv_s):
        kj, qi = pl.program_id(1), pl.program_id(2)

        @pl.when((kj == 0) & (qi == 0))
        def _():
            dq_s[...] = jnp.zeros_like(dq_s)

        @pl.when(qi == kj)
        def _():
            dk_s[...] = jnp.zeros_like(dk_s)
            dv_s[...] = jnp.zeros_like(dv_s)

        def tile(diagonal):
            bands = [(r0, r1, r1) for r0, r1 in zip(edges[:-1], edges[1:])] if diagonal else [(0, tq, tq)]
            lse = lse_ref[...]
            for r0, r1, c1 in bands:
                nr = r1 - r0
                rows = pl.ds(pl.multiple_of(qi * tq + r0, 8), nr)
                if diagonal:
                    mask = (lax.broadcasted_iota(jnp.int32, (nr, c1), 1)
                            <= r0 + lax.broadcasted_iota(jnp.int32, (nr, c1), 0))
                def scores(h):
                    ps = _pair(h)
                    k = k_ref[0:c1, ps]
                    qs = q_ref[r0:r1, ps] * scale
                    dov = _own_lanes(do_ref[r0:r1, ps], h)
                    return _dot_nt(_own_lanes(qs, h), k), _dot_nt(dov, v_ref[0:c1, ps]), k, qs, dov

                nxt = scores(0)
                for h in range(n_heads):
                    ps = _pair(h)
                    own = slice(h * HW, (h + 1) * HW)
                    s, dp, k, qs, dov = nxt
                    if h + 1 < n_heads:
                        nxt = scores(h + 1)
                    s = s - fr_ref[h:h + 1, 0:c1]
                    if diagonal:
                        s = jnp.where(mask, s, NEG)
                    p = jnp.exp(s - lse[r0:r1, h:h + 1])
                    dsum = jnp.sum(dov.astype(F32) * o_ref[r0:r1, ps], axis=1, keepdims=True)
                    dsb = (p * (dp - dsum)).astype(BF16)
                    dv = _dot_tn(p.astype(BF16), dov)
                    dk_s[0:c1, own] += _dot_tn(dsb, _own_lanes_and_ones(qs, h))
                    dq_s[rows, own] += _dot(dsb, _own_lanes_and_ones(k, h))
                    if h % 2 == 0:
                        dv_even = dv
                    else:
                        dv_s[0:c1, ps] += dv_even + dv

        def compact(acc, data_scale):
            rows = acc.shape[0]
            low = lax.broadcasted_iota(jnp.int32, (rows, HW), 1) < HEAD_DIM
            lane = lax.broadcasted_iota(jnp.int32, (rows, W), 1)
            vals, sums = [], jnp.zeros((rows, W), F32)
            for h in range(0, n_heads, 2):
                even, odd = acc[:, h * HW:(h + 1) * HW], acc[:, (h + 1) * HW:(h + 2) * HW]
                vals.append(jnp.where(low, even, odd) * data_scale)
                sums = jnp.where(lane == h, even[:, _sum_lane(h):_sum_lane(h) + 1],
                                 jnp.where(lane == h + 1, odd[:, _sum_lane(h + 1):_sum_lane(h + 1) + 1], sums))
            return vals, sums

        @pl.when(qi > kj)
        def _():
            tile(False)

        @pl.when(qi == kj)
        def _():
            tile(True)
            rows = pl.ds(pl.multiple_of(qi * tq, 8), tq)
            vals, sums = compact(dq_s[rows, :], scale)
            for h in range(0, n_heads, 2):
                dq_ref[rows, _pair(h)] = vals[h // 2]
            dfq_ref[rows, :] = sums

        @pl.when(qi == nq - 1)
        def _():
            vals, sums = compact(dk_s[...], 1.0)
            for h in range(0, n_heads, 2):
                dk_ref[:, _pair(h)] = vals[h // 2].astype(BF16)
            dfk_ref[...] = sums
            dv_ref[...] = dv_s[...].astype(BF16)

    qq = lambda b, kj, qi: jnp.maximum(qi, kj)
    qblk = lambda w, cb: pl.BlockSpec((None, tq, w), lambda b, kj, qi: (b, qq(b, kj, qi), cb))
    kblk = lambda w, cb: pl.BlockSpec((None, tq, w), lambda b, kj, qi: (b, kj, cb))
    return _launch(
        body, name=name, grid=(B, nq, nq), args=(proj, proj, proj, o, do, lse, fr), comm=comm,
        in_specs=[qblk(AD, 3), kblk(AD, 4), kblk(AD, 5), qblk(AD, 0), qblk(AD, 0), qblk(W, 0),
                  pl.BlockSpec((None, None, n_heads, tq), lambda b, kj, qi: (b, kj, 0, 0))],
        out_specs=[pl.BlockSpec((None, L, AD), lambda b, kj, qi: (b, 0, 0)),
                   kblk(AD, 0), kblk(AD, 0), kblk(W, 0),
                   pl.BlockSpec((None, L, W), lambda b, kj, qi: (b, 0, 0))],
        out_shape=[jax.ShapeDtypeStruct((B, L, AD), F32), jax.ShapeDtypeStruct((B, L, AD), BF16),
                   jax.ShapeDtypeStruct((B, L, AD), BF16), jax.ShapeDtypeStruct((B, L, W), F32),
                   jax.ShapeDtypeStruct((B, L, W), F32)],
        scratch_shapes=[pltpu.VMEM((L, n_heads * HW), F32), pltpu.VMEM((tq, n_heads * HW), F32),
                        pltpu.VMEM((tq, AD), F32)])


def _mix_gather(refs, first):
    b_ref, c_ref, hc_ref, cp_ref, hcp_ref, o_ref, cw_ref, p_ref = refs
    bg = b_ref[...].astype(F32)
    u = c_ref[...].astype(F32) * hc_ref[...].astype(F32)
    prev = cp_ref[...].astype(F32) * hcp_ref[...].astype(F32)
    prev = jnp.where(first, 0.0, prev)
    cv, u1, u2 = _causal_conv(u, prev, cw_ref[...])
    yc = bg * cv
    p = p_ref[...]
    rc = lax.rsqrt(_group_mean(yc * yc, p) + EPS)
    ya = o_ref[...].astype(F32)
    ra = lax.rsqrt(_group_mean(ya * ya, p) + EPS)
    return bg, (u, u1, u2), cv, yc * rc, rc, ya * ra, ra


def _mix_specs(tm, CD, D, grid_rank_fn):
    per = tm // HALO
    cur = lambda cb: pl.BlockSpec((None, tm, CD), lambda b, i: (b, i, cb))
    prev = lambda cb: pl.BlockSpec((None, HALO, CD), lambda b, i: (b, jnp.maximum(i * per - 1, 0), cb))
    return [cur(0), cur(1), cur(2), prev(1), prev(2), cur(0)]


def _mix_out(proj, o, cw, gc, ga, wout, h, pmat, next_gain, *, tm, name, comm=None):
    B, L, D = h.shape
    CD = o.shape[-1]
    const = lambda b, i: (0, 0)

    def body(b_ref, c_ref, hc_ref, cp_ref, hcp_ref, o_ref, cw_ref, p_ref, gc_ref, ga_ref, w_ref, h_ref, ng_ref,
             out_ref, y_ref, n_ref):
        first = pl.program_id(1) == 0
        _, _, _, zc, _, za, _ = _mix_gather((b_ref, c_ref, hc_ref, cp_ref, hcp_ref, o_ref, cw_ref, p_ref), first)
        yc = (zc * gc_ref[...]).astype(BF16)
        ya = (za * ga_ref[...]).astype(BF16)
        y_ref[:, :CD] = yc
        y_ref[:, CD:] = ya
        out = h_ref[...] + _dot(yc, w_ref[:CD, :]) + _dot(ya, w_ref[CD:, :])
        out_ref[...] = out
        n_ref[...] = _rms(out, ng_ref[...])

    tile = pl.BlockSpec((None, tm, D), lambda b, i: (b, i, 0))
    return _launch(
        body, name=name, grid=(B, L // tm),
        in_specs=_mix_specs(tm, CD, D, None)
                 + [pl.BlockSpec(cw.shape, const), pl.BlockSpec(pmat.shape, const),
                    pl.BlockSpec((1, CD), const), pl.BlockSpec((1, CD), const), pl.BlockSpec((D, D), const),
                    tile, pl.BlockSpec((1, D), const)],
        out_specs=[tile, tile, tile],
        out_shape=[jax.ShapeDtypeStruct((B, L, D), F32), jax.ShapeDtypeStruct((B, L, D), BF16),
                   jax.ShapeDtypeStruct((B, L, D), BF16)],
        args=(proj, proj, proj, proj, proj, o, cw, pmat, gc, ga, wout, h, next_gain), comm=comm)


def _mix_out_bwd(dhb, proj, o, cw, gc, ga, wout, pmat, *, tm, name, comm=None):
    B, L, D = dhb.shape
    CD = o.shape[-1]
    const = lambda b, i: (0, 0)

    def body(dh_ref, b_ref, c_ref, hc_ref, cp_ref, hcp_ref, o_ref, cw_ref, p_ref, gc_ref, ga_ref, w_ref,
             db_ref, dcv_ref, do_ref, dgc_ref, dga_ref, dcw_ref):
        first = pl.program_id(1) == 0

        @pl.when((pl.program_id(0) == 0) & first)
        def _():
            dgc_ref[...] = jnp.zeros_like(dgc_ref)
            dga_ref[...] = jnp.zeros_like(dga_ref)
            dcw_ref[...] = jnp.zeros_like(dcw_ref)

        bg, us, cv, zc, rc, za, ra = _mix_gather(
            (b_ref, c_ref, hc_ref, cp_ref, hcp_ref, o_ref, cw_ref, p_ref), first)
        p = p_ref[...]
        dh = dh_ref[...]
        dyc = _dot_nt(dh, w_ref[:CD, :])
        dya = _dot_nt(dh, w_ref[CD:, :])

        dgc_ref[...] += jnp.sum(dyc * zc, axis=0, keepdims=True)
        dz = dyc * gc_ref[...]
        dx = rc * (dz - zc * _group_mean(dz * zc, p))
        db_ref[...] = (dx * cv).astype(BF16)
        dcv = dx * bg
        dcv_ref[...] = dcv.astype(BF16)
        for k in range(3):
            dcw_ref[k:k + 1, :] += jnp.sum(dcv * us[2 - k], axis=0, keepdims=True)

        dga_ref[...] += jnp.sum(dya * za, axis=0, keepdims=True)
        dz = dya * ga_ref[...]
        do_ref[...] = (ra * (dz - za * _group_mean(dz * za, p))).astype(BF16)

    tile = lambda w: pl.BlockSpec((None, tm, w), lambda b, i: (b, i, 0))
    return _launch(
        body, name=name, grid=(B, L // tm), comm=comm,
        args=(dhb, proj, proj, proj, proj, proj, o, cw, pmat, gc, ga, wout),
        in_specs=[tile(D)] + _mix_specs(tm, CD, D, None)
                 + [pl.BlockSpec(cw.shape, const), pl.BlockSpec(pmat.shape, const),
                    pl.BlockSpec((1, CD), const), pl.BlockSpec((1, CD), const), pl.BlockSpec((D, D), const)],
        out_specs=[tile(CD), tile(CD), tile(CD),
                   pl.BlockSpec((1, CD), const), pl.BlockSpec((1, CD), const), pl.BlockSpec((8, CD), const)],
        out_shape=[jax.ShapeDtypeStruct((B, L, CD), BF16)] * 3
                  + [jax.ShapeDtypeStruct((1, CD), F32)] * 2 + [jax.ShapeDtypeStruct((8, CD), F32)])


def _conv_bwd(dcv, proj, cw, *, tm, name):
    B, L, CD = dcv.shape
    per = tm // HALO
    nhalo = L // HALO
    nt = L // tm

    def body(d_ref, dn_ref, c_ref, hc_ref, cw_ref, out_ref):
        last = pl.program_id(1) == nt - 1
        d = d_ref[...].astype(F32)
        nxt = jnp.where(last, 0.0, dn_ref[...].astype(F32))
        n0, n1 = _row_of(nxt, 0), _row_of(nxt, 1)
        rows = lax.broadcasted_iota(jnp.int32, d.shape, 0)
        d1 = jnp.where(rows == tm - 1, n0, pltpu.roll(d, tm - 1, 0))
        d2 = jnp.where(rows == tm - 2, n0, jnp.where(rows == tm - 1, n1, pltpu.roll(d, tm - 2, 0)))
        w = cw_ref[...]
        du = w[2:3, :] * d + w[1:2, :] * d1 + w[0:1, :] * d2
        out_ref[:, :CD] = (du * hc_ref[...].astype(F32)).astype(BF16)
        out_ref[:, CD:] = (du * c_ref[...].astype(F32)).astype(BF16)

    return pl.pallas_call(
        body, name=name, grid=(B, nt),
        in_specs=[pl.BlockSpec((None, tm, CD), lambda b, i: (b, i, 0)),
                  pl.BlockSpec((None, HALO, CD), lambda b, i: (b, jnp.minimum((i + 1) * per, nhalo - 1), 0)),
                  pl.BlockSpec((None, tm, CD), lambda b, i: (b, i, 1)),
                  pl.BlockSpec((None, tm, CD), lambda b, i: (b, i, 2)),
                  pl.BlockSpec(cw.shape, lambda b, i: (0, 0))],
        out_specs=pl.BlockSpec((None, tm, 2 * CD), lambda b, i: (b, i, 0)),
        out_shape=jax.ShapeDtypeStruct((B, L, 2 * CD), BF16),
        compiler_params=_params("arbitrary", "arbitrary"),
    )(dcv, dcv, proj, proj, cw)


def _place():
    x, y, c = lax.axis_index("x"), lax.axis_index("y"), lax.axis_index("c")
    others = [(1 - x, y), (x, 1 - y), (1 - x, 1 - y)]
    return x, y, c, others


def _all_gather_shards(shards, *, name):
    n = len(shards)

    def body(*refs):
        ins, outs = refs[:n], refs[n:2 * n]
        send, recv, fsend, frecv, lsem = refs[2 * n:]
        x, y, c, others = _place()
        me = 2 * x + y
        local = [pltpu.make_async_copy(ins[t], outs[t].at[me], lsem.at[t]) for t in range(n)]
        for cp in local:
            cp.start()

        def half(t, k):
            hr = shards[t].shape[0] // 2
            return pl.ds(pl.multiple_of(k * hr, HALO), hr)

        def ici(t, j, src_chip, to):
            src = ins[t].at[half(t, c)] if to is not None else outs[t].at[src_chip, half(t, c)]
            return pltpu.make_async_remote_copy(
                src_ref=src, dst_ref=outs[t].at[src_chip, half(t, c)],
                send_sem=send.at[3 * t + j], recv_sem=recv.at[3 * t + j],
                device_id=(x, y, c) if to is None else to, device_id_type=MESH)

        def d2d(t, j, src_chip, k):
            return pltpu.make_async_remote_copy(
                src_ref=outs[t].at[src_chip, half(t, k)], dst_ref=outs[t].at[src_chip, half(t, k)],
                send_sem=fsend.at[3 * t + j], recv_sem=frecv.at[3 * t + j],
                device_id=(x, y, 1 - c), device_id_type=MESH)

        firsts = [ici(t, j, me, (ox, oy, c)) for t in range(n) for j, (ox, oy) in enumerate(others)]
        for cp in firsts:
            cp.start()
        passed = []
        for t in range(n):
            for j, (ox, oy) in enumerate(others):
                ici(t, j, 2 * ox + oy, None).wait_recv()
                cp = d2d(t, j, 2 * ox + oy, c)
                cp.start()
                passed.append(cp)
        for t in range(n):
            for j, (ox, oy) in enumerate(others):
                d2d(t, j, 2 * ox + oy, 1 - c).wait_recv()
        for cp in firsts + passed:
            cp.wait_send()
        for cp in local:
            cp.wait()

    return pl.pallas_call(
        body, name=name,
        in_specs=[ANY] * n, out_specs=[ANY] * n,
        out_shape=[jax.ShapeDtypeStruct((N_SHARD,) + s.shape, s.dtype) for s in shards],
        scratch_shapes=[pltpu.SemaphoreType.DMA((3 * n,))] * 4 + [pltpu.SemaphoreType.DMA((n,))],
    )(*shards)


def _all_reduce_small(slab, *, name):
    def body(in_ref, out_ref, gath, send, recv):
        x, y, c, _ = _place()
        me = 4 * x + 2 * y + c
        gath[me] = in_ref[...]
        copies, peers = [], []
        for m in range(1, N_DEV):
            px = jnp.where((m >> 2) & 1, 1 - x, x)
            py = jnp.where((m >> 1) & 1, 1 - y, y)
            pc = jnp.where(m & 1, 1 - c, c)
            cp = pltpu.make_async_remote_copy(
                src_ref=in_ref, dst_ref=gath.at[me], send_sem=send.at[m - 1], recv_sem=recv.at[m - 1],
                device_id=(px, py, pc), device_id_type=MESH)
            cp.start()
            copies.append(cp)
            peers.append(4 * px + 2 * py + pc)
        for m in range(1, N_DEV):
            pltpu.make_async_remote_copy(
                src_ref=in_ref, dst_ref=gath.at[peers[m - 1]], send_sem=send.at[m - 1], recv_sem=recv.at[m - 1],
                device_id=(x, y, c), device_id_type=MESH).wait_recv()
        for cp in copies:
            cp.wait_send()
        acc = gath[0]
        for k in range(1, N_DEV):
            acc = acc + gath[k]
        out_ref[...] = acc

    vm = pl.BlockSpec(memory_space=pltpu.VMEM)
    return pl.pallas_call(
        body, name=name, in_specs=[vm], out_specs=vm,
        out_shape=jax.ShapeDtypeStruct(slab.shape, slab.dtype),
        scratch_shapes=[pltpu.VMEM((N_DEV,) + slab.shape, slab.dtype),
                        pltpu.SemaphoreType.DMA((N_DEV - 1,)), pltpu.SemaphoreType.DMA((N_DEV - 1,))],
    )(slab)


def _gather_stage(shards, into, *, ici=(), d2d=(), whole=False):
    n = len(shards) if into is None else len(into)
    ns = len(shards) if ici else 0
    ni, nd = max(len(ici), 1), max(len(d2d), 1)
    shapes = [s.shape for s in shards] if into is None else [p.shape[1:] for p in into]
    dtypes = [s.dtype for s in shards] if into is None else [p.dtype for p in into]

    def copies(ins, outs, sems, sending):
        x, y, c, others = _place()
        me = 2 * x + y
        out = []
        for t in range(n):
            rows_all = shapes[t][0]
            hr = rows_all // 2
            mine = pl.ds(pl.multiple_of(c * hr, HALO), hr)
            theirs = pl.ds(pl.multiple_of((1 - c) * hr, HALO), hr)
            for a, j in enumerate(ici):
                ox, oy = others[j]
                src_chip = me if sending else 2 * ox + oy
                lone = whole and j < 2
                rows = pl.ds(0, rows_all) if lone else mine
                out.append(((c == 1 - j) if lone else None, pltpu.make_async_remote_copy(
                    src_ref=ins[t].at[rows], dst_ref=outs[t].at[src_chip, rows],
                    send_sem=sems[0].at[ni * t + a], recv_sem=sems[1].at[ni * t + a],
                    device_id=(ox, oy, c) if sending else (x, y, c), device_id_type=MESH)))
            for a, j in enumerate(d2d):
                ox, oy = others[j]
                lone = whole and j < 2
                if lone:
                    cond = (c == 1 - j) if sending else (c != 1 - j)
                    blk = outs[t].at[2 * ox + oy]
                else:
                    cond = None
                    blk = outs[t].at[2 * ox + oy, mine if sending else theirs]
                out.append((cond, pltpu.make_async_remote_copy(
                    src_ref=blk, dst_ref=blk, send_sem=sems[2].at[nd * t + a], recv_sem=sems[3].at[nd * t + a],
                    device_id=(x, y, 1 - c) if sending else (x, y, c), device_id_type=MESH)))
        return out

    def local(ins, outs, sems):
        if into is not None:
            return []
        x, y, _, _ = _place()
        return [(None, pltpu.make_async_copy(ins[t], outs[t].at[2 * x + y], sems[4].at[t])) for t in range(n)]

    def each(pairs, fn):
        for cond, cp in pairs:
            if cond is None:
                fn(cp)
            else:
                pl.when(cond)(functools.partial(fn, cp))

    def start(ins, outs, sems):
        each(local(ins, outs, sems) + copies(ins, outs, sems, True), lambda cp: cp.start())

    def finish(ins, outs, sems):
        each(copies(ins, outs, sems, False), lambda cp: cp.wait_recv())
        each(copies(ins, outs, sems, True), lambda cp: cp.wait_send())
        each(local(ins, outs, sems), lambda cp: cp.wait())

    return _Comm((list(shards) if ici or into is None else []) + (list(into) if into is not None else []),
                 [jax.ShapeDtypeStruct((N_SHARD,) + tuple(sh), dt) for sh, dt in zip(shapes, dtypes)],
                 [ni * n, ni * n, nd * n, nd * n, n], start, finish,
                 aliases=None if into is None else {ns + t: t for t in range(n)})


def _gather_ici(shards):
    return _gather_stage(shards, None, ici=(0, 1, 2))


def _gather_d2d(parts):
    return _gather_stage((), parts, d2d=(0, 1, 2))


def _swap_halves(grads):
    n = len(grads)

    def copies(ins, outs, sems):
        x, y, c, _ = _place()
        out = []
        for t in range(n):
            hr = grads[t].shape[1] // 2
            rows = pl.ds(pl.multiple_of((1 - c) * hr, 8), hr)
            out.append(pltpu.make_async_remote_copy(
                src_ref=ins[t].at[:, rows, :], dst_ref=outs[t], send_sem=sems[0].at[t], recv_sem=sems[1].at[t],
                device_id=(x, y, 1 - c), device_id_type=MESH))
        return out

    def start(ins, outs, sems):
        for cp in copies(ins, outs, sems):
            cp.start()

    def finish(ins, outs, sems):
        for cp in copies(ins, outs, sems):
            cp.wait()

    return _Comm(grads, [jax.ShapeDtypeStruct((N_SHARD, g.shape[1] // 2, g.shape[2]), g.dtype) for g in grads],
                 [n, n], start, finish)


def _pair_sum(g, got, c, *, name):
    ns, R, C = g.shape
    hr = R // 2

    def body(c_ref, g_ref, r_ref, o_ref):
        o_ref[...] = (g_ref[...] + r_ref[...]).astype(BF16)

    return pl.pallas_call(
        body, name=name,
        grid_spec=pltpu.PrefetchScalarGridSpec(
            num_scalar_prefetch=1, grid=(ns,),
            in_specs=[pl.BlockSpec((None, hr, C), lambda s, cr: (s, cr[0], 0)),
                      pl.BlockSpec((None, hr, C), lambda s, cr: (s, 0, 0))],
            out_specs=pl.BlockSpec((None, hr, C), lambda s, cr: (s, 0, 0))),
        out_shape=jax.ShapeDtypeStruct((ns, hr, C), BF16),
        compiler_params=_params("arbitrary"),
    )(c, g, got)


def _scatter_chips(sums):
    n = len(sums)

    def copies(ins, outs, sems, sending):
        x, y, c, others = _place()
        me = 2 * x + y
        out = []
        for t in range(n):
            for j, (ox, oy) in enumerate(others):
                there = 2 * ox + oy
                out.append(pltpu.make_async_remote_copy(
                    src_ref=ins[t].at[there if sending else me], dst_ref=outs[t].at[me if sending else there],
                    send_sem=sems[0].at[3 * t + j], recv_sem=sems[1].at[3 * t + j],
                    device_id=(ox, oy, c) if sending else (x, y, c), device_id_type=MESH))
        return out

    def start(ins, outs, sems):
        for cp in copies(ins, outs, sems, True):
            cp.start()

    def finish(ins, outs, sems):
        for cp in copies(ins, outs, sems, False):
            cp.wait_recv()
        for cp in copies(ins, outs, sems, True):
            cp.wait_send()

    return _Comm(sums, [jax.ShapeDtypeStruct(s.shape, s.dtype) for s in sums], [3 * n, 3 * n], start, finish)


def _chip_sum(g, got, landed, idx, *, name):
    ns, R, C = g.shape
    hr = R // 2

    def body(i_ref, g_ref, r_ref, a_ref, b_ref, c_ref, o_ref):
        acc = g_ref[...] + r_ref[...]
        for ref in (a_ref, b_ref, c_ref):
            acc = acc + ref[...].astype(F32)
        o_ref[...] = acc

    other = lambda k: pl.BlockSpec((None, hr, C), lambda s, ir: (ir[2 + k], 0, 0))
    return pl.pallas_call(
        body, name=name,
        grid_spec=pltpu.PrefetchScalarGridSpec(
            num_scalar_prefetch=1, grid=(1,),
            in_specs=[pl.BlockSpec((None, hr, C), lambda s, ir: (ir[0], ir[1], 0)),
                      pl.BlockSpec((None, hr, C), lambda s, ir: (ir[0], 0, 0)),
                      other(0), other(1), other(2)],
            out_specs=pl.BlockSpec((hr, C), lambda s, ir: (ir[1], 0))),
        out_shape=jax.ShapeDtypeStruct((R, C), F32),
        compiler_params=_params("arbitrary"),
    )(idx, g, got, landed, landed, landed)


def _share_halves(halves):
    n = len(halves)

    def copies(outs, sems, sending):
        x, y, c, _ = _place()
        out = []
        for t in range(n):
            hr = halves[t].shape[0] // 2
            rows = pl.ds(pl.multiple_of((c if sending else 1 - c) * hr, 8), hr)
            out.append(pltpu.make_async_remote_copy(
                src_ref=outs[t].at[rows, :], dst_ref=outs[t].at[rows, :], send_sem=sems[0].at[t],
                recv_sem=sems[1].at[t], device_id=(x, y, 1 - c) if sending else (x, y, c), device_id_type=MESH))
        return out

    def start(ins, outs, sems):
        for cp in copies(outs, sems, True):
            cp.start()

    def finish(ins, outs, sems):
        for cp in copies(outs, sems, False):
            cp.wait_recv()
        for cp in copies(outs, sems, True):
            cp.wait_send()

    return _Comm(halves, [jax.ShapeDtypeStruct(h.shape, h.dtype) for h in halves], [n, n], start, finish,
                 aliases={t: t for t in range(n)})


def _adamw(w, g, m, v, *, name):
    R, C = w.shape
    tr = R
    for cand in (256, 128, 64, 32, 16, 8):
        if R % cand == 0:
            tr = cand
            break

    def body(w_ref, g_ref, m_ref, v_ref, go_ref, d_ref, mo_ref, vo_ref):
        gv = g_ref[...]
        go_ref[...] = gv
        mn = ADAM_B1 * m_ref[...] + (1.0 - ADAM_B1) * gv
        vn = ADAM_B2 * v_ref[...] + (1.0 - ADAM_B2) * (gv * gv)
        m_hat = mn / (1.0 - ADAM_B1 ** ADAM_STEP)
        v_hat = vn / (1.0 - ADAM_B2 ** ADAM_STEP)
        d_ref[...] = -ADAM_LR * (m_hat / (jnp.sqrt(v_hat) + ADAM_EPS) + ADAM_WD * w_ref[...])
        mo_ref[...] = mn
        vo_ref[...] = vn

    blk = pl.BlockSpec((tr, C), lambda i: (i, 0))
    return pl.pallas_call(
        body, name=name, grid=(R // tr,), in_specs=[blk] * 4, out_specs=[blk] * 4,
        out_shape=[jax.ShapeDtypeStruct((R, C), F32)] * 4,
        compiler_params=_params("arbitrary"),
    )(w, g, m, v)


def _pack_small(D, meta, n1, nm, n3, nf, gc, ga, bf, cw):
    def row(a):
        a = a.reshape(-1, a.shape[-1])
        return jnp.pad(a, ((0, 0), (0, D - a.shape[-1])))
    rows = [row(meta), row(n1), row(nm), row(n3), row(nf), row(jnp.concatenate([gc, ga], axis=-1)), row(bf), row(cw)]
    slab = jnp.concatenate(rows, axis=0)
    return jnp.pad(slab, ((0, SMALL_ROWS - slab.shape[0]), (0, 0)))


def _unpack_small(slab, like):
    meta, n1, nm, n3, nf, gc, ga, bf, cw = like
    nmeta, mc = meta.shape
    out = [slab[:nmeta, :mc].reshape(meta.shape)]
    r = nmeta
    for a in (n1, nm, n3, nf):
        out.append(slab[r, :a.shape[-1]].reshape(a.shape))
        r += 1
    cd = gc.shape[-1]
    out.append(slab[r, :cd].reshape(gc.shape))
    out.append(slab[r, cd:cd + ga.shape[-1]].reshape(ga.shape))
    r += 1
    out.append(slab[r, :bf.shape[-1]].reshape(bf.shape))
    r += 1
    out.append(slab[r:r + 3, :cw.shape[-1]].reshape(cw.shape))
    return out


def kernel(x, meta_tokens, ffn1_norm, ffn1_w_gu, ffn1_w_down, mix_norm, w_in, conv_w, b_f, out_norm_conv, out_norm_attn, w_out, ffn2_norm, ffn2_w_gu, ffn2_w_down, final_norm, loss_target, m_meta_tokens, m_ffn1_norm, m_ffn1_w_gu, m_ffn1_w_down, m_mix_norm, m_w_in, m_conv_w, m_b_f, m_out_norm_conv, m_out_norm_attn, m_w_out, m_ffn2_norm, m_ffn2_w_gu, m_ffn2_w_down, m_final_norm, v_meta_tokens, v_ffn1_norm, v_ffn1_w_gu, v_ffn1_w_down, v_mix_norm, v_w_in, v_conv_w, v_b_f, v_out_norm_conv, v_out_norm_attn, v_w_out, v_ffn2_norm, v_ffn2_w_gu, v_ffn2_w_down, v_final_norm):
    B, S, D = x.shape
    L = S + N_META
    T = B * L
    tm = L // 3
    assert tm * 3 == L and tm % HALO == 0
    guc = ffn1_w_gu.shape[-1]
    ff = N_SHARD * guc // 2
    H = b_f.shape[-1]
    AD = H * HEAD_DIM
    CD = conv_w.shape[-1] * N_SHARD
    assert CD == AD and CD + AD == D and CD % LANES == 0
    n_main = 3 * CD + 3 * AD
    ins = w_in.shape[-1]

    xi, yi, ci = lax.axis_index("x"), lax.axis_index("y"), lax.axis_index("c")
    chip = 2 * xi + yi

    small_shard = jnp.zeros((2 * HALO, meta_tokens.shape[-1]), F32)
    small_shard = small_shard.at[:N_META].set(meta_tokens)
    small_shard = small_shard.at[N_META:N_META + 3, :conv_w.shape[-1]].set(conv_w[0])
    big = [ffn1_w_gu[0], ffn1_w_down[0], w_in[0], w_out[0], ffn2_w_gu[0], ffn2_w_down[0]]
    wgu1_s, wd1_s, win_s, wout_s, wgu2_s, wd2_s = [w.astype(BF16) for w in big]
    small_g, = _all_gather_shards([small_shard], name="gather_small")
    meta_f = jnp.moveaxis(small_g[:, :N_META], 0, 1).reshape(N_META, D)
    cw_f = jnp.moveaxis(small_g[:, N_META:N_META + 3, :conv_w.shape[-1]], 0, 1).reshape(3, CD)
    cw8 = jnp.pad(cw_f, ((0, 5), (0, 0)))
    bf_p = jnp.pad(b_f, ((0, 0), (0, LANES - H)))
    gid = jnp.arange(CD) // HEAD_DIM
    pmat = jnp.where(gid[:, None] == gid[None, :], 1.0 / HEAD_DIM, 0.0).astype(BF16)

    gu_shape = jax.ShapeDtypeStruct((2, T, ff), BF16)
    gu_w_spec = pl.BlockSpec((None, D, guc), lambda s, i: (s, 0, 0))
    gu_o_spec = pl.BlockSpec((None, tm, guc), lambda s, i: (s // 2, i, s % 2))

    sid = jnp.bitwise_xor(chip, jnp.array([0, 2, 1, 3], jnp.int32)).astype(jnp.int32)
    (h0, n1), wgu1_h = _embed_norm(x, meta_f, ffn1_norm, tm=tm, name="embed_norm",
                                   comm=_gather_stage([wgu1_s], None, ici=(0, 1), whole=True))
    gu1, wgu1_h = _ffn_up(n1, wgu1_s[None], sid, None, tm=tm, first=0, count=1, name="ffn1_up_own",
                          comm=_gather_stage([wgu1_s], wgu1_h, ici=(2,), d2d=(0, 1), whole=True))
    gu1, out = _ffn_up(n1, wgu1_h[0], sid, gu1, tm=tm, first=1, count=2, name="ffn1_up_near",
                       comm=_join(_gather_stage((), wgu1_h, d2d=(2,)), _gather_ici([wd1_s, wout_s])))
    wgu1, down_w = out[0], out[1:]
    gu1, (wd1, wout_g) = _ffn_up(n1, wgu1, sid, gu1, tm=tm, first=3, count=1, name="ffn1_up_far",
                                 comm=_gather_d2d(down_w))
    wd1 = wd1.reshape(ff, D)
    (h1, n2), win_h = _ffn_down(gu1, wd1, h0, mix_norm, tm=tm, name="ffn1_down", comm=_gather_ici([win_s]))
    win_g, = _run_comm(_gather_d2d(win_h), name="gather_w_in")
    wout_f = wout_g.reshape(D, D)
    win_f = jnp.moveaxis(win_g, 0, 1).reshape(D, N_SHARD * ins)
    win_main = win_f[:, :n_main]
    win_fg = jnp.pad(win_f[:, n_main:], ((0, 0), (0, LANES - H)))

    proj, _ = _matmul_nn(n2, win_main, tm=tm, nb=n_main // (3 * CD),
                         w_spec=pl.BlockSpec((D, 3 * CD), lambda s, i: (0, s)),
                         out_shape=jax.ShapeDtypeStruct((T, n_main), BF16),
                         out_spec=pl.BlockSpec((tm, 3 * CD), lambda s, i: (i, s)), name="mix_in")
    fg, _ = _matmul_nn(n2, win_fg, tm=tm, nb=1, w_spec=pl.BlockSpec((D, LANES), lambda s, i: (0, 0)),
                       out_shape=jax.ShapeDtypeStruct((T, LANES), F32),
                       out_spec=pl.BlockSpec((tm, LANES), lambda s, i: (i, 0)), name="mix_in_fg")
    proj3 = proj.reshape(B, L, n_main)
    fg3 = fg.reshape(B, L, LANES)
    fc = _fcum(fg3, bf_p, ch=tm, name="forget_cumsum")
    fr = fc[:, :, :H].reshape(B, L // tm, tm, H).transpose(0, 1, 3, 2)
    (o, lse), ffn2_w = _attn_fwd(proj3, fc, fr, tq=tm, n_heads=H, name="attn_fwd",
                                 comm=_gather_ici([wgu2_s, wd2_s]))
    (h2, ymix, n3), (wgu2, wd2) = _mix_out(
        proj3, o, cw8, out_norm_conv, out_norm_attn, wout_f, h1.reshape(B, L, D), pmat, ffn2_norm,
        tm=tm, name="mix_out", comm=_gather_d2d(ffn2_w))
    wd2 = wd2.reshape(ff, D)
    h2 = h2.reshape(T, D)
    n3 = n3.reshape(T, D)

    gu2, _ = _matmul_nn(n3, wgu2, tm=tm, nb=N_SHARD, w_spec=gu_w_spec, out_shape=gu_shape, out_spec=gu_o_spec,
                        name="ffn2_up")
    (dh3f, dh3b, d_gf, loss_part), _ = _ffn_down_loss(gu2, wd2, h2, final_norm.reshape(1, D), loss_target,
                                                      tm=tm, name="ffn2_down_loss")

    c_arr = jnp.reshape(ci, (1,)).astype(jnp.int32)
    ks = jnp.arange(N_SHARD - 1, dtype=jnp.int32)
    idx = jnp.concatenate([jnp.stack([chip, ci]).astype(jnp.int32), ks + (ks >= chip).astype(jnp.int32)])

    def pair_sums(grads, got, names):
        return [_pair_sum(g, r, c_arr, name="pair_sum_" + nm) for g, r, nm in zip(grads, got, names)]

    def chip_sums(grads, got, landed, names):
        return [_chip_sum(g, r, l, idx, name="chip_sum_" + nm) for g, r, l, nm in zip(grads, got, landed, names)]

    def dw_up(n, dgu, name, comm=None):
        return _matmul_tn(
            n, dgu, tm=tm, nb=N_SHARD, kb=D, x_spec=pl.BlockSpec((tm, D), lambda s, i: (i, 0)),
            y_spec=pl.BlockSpec((None, tm, guc), lambda s, i: (s // 2, i, s % 2)),
            out_shape=jax.ShapeDtypeStruct((N_SHARD, D, guc), F32),
            out_spec=pl.BlockSpec((None, D, guc), lambda s, i: (s, 0, 0)), name=name, comm=comm)

    (dgu2, d_wd2), _ = _ffn_bwd_act(dh3b, gu2, wd2, tm=tm, guc=guc, name="ffn2_bwd_act")
    (dh2, dh2b, d_g3), _ = _ffn_bwd_in(dgu2, wgu2, h2, ffn2_norm, dh3f, tm=tm, scale=1.0, name="ffn2_bwd_in")
    d_wgu2, _ = dw_up(n3, dgu2, "ffn2_dw_up")
    grads_f2 = [d_wgu2, d_wd2.reshape(N_SHARD, ff // N_SHARD, D)]
    names_f2 = ["wgu2", "wd2"]

    dh2b3 = dh2b.reshape(B, L, D)
    (d_bg, d_cv, d_o, d_gc, d_ga, d_cw), got_f2 = _mix_out_bwd(
        dh2b3, proj3, o, cw8, out_norm_conv, out_norm_attn, wout_f, pmat, tm=tm, name="mix_out_bwd",
        comm=_swap_halves(grads_f2))
    sums_f2 = pair_sums(grads_f2, got_f2, names_f2)
    d_wout, _ = _matmul_tn(
        ymix.reshape(T, D), dh2b, tm=tm, nb=1, kb=D,
        x_spec=pl.BlockSpec((tm, D), lambda s, i: (i, 0)), y_spec=pl.BlockSpec((tm, D), lambda s, i: (i, 0)),
        out_shape=jax.ShapeDtypeStruct((D, D), F32), out_spec=pl.BlockSpec((D, D), lambda s, i: (0, 0)),
        name="dw_out")
    d_cc = _conv_bwd(d_cv, proj3, cw8, tm=tm, name="conv_bwd")
    (d_q, d_k, d_v, d_fk, d_fq), landed_f2 = _attn_bwd(proj3, o, d_o, lse, fc, fr, tq=tm, n_heads=H, name="attn_bwd",
                                                       comm=_scatter_chips(sums_f2))
    halves_f2 = chip_sums(grads_f2, got_f2, landed_f2, names_f2)
    d_fc = d_fq - d_fk
    d_fg, d_bf = _fcum_bwd(d_fc, fg3, bf_p, ch=tm, name="forget_cumsum_bwd")

    parts = [d_bg.reshape(T, CD), d_cc.reshape(T, 2 * CD), d_q.reshape(T, AD), d_k.reshape(T, AD),
             d_v.reshape(T, AD), d_fg.reshape(T, LANES)]
    (dh1, dh1b, d_gm, d_proj), g_f2 = _mix_bwd_in(parts, win_main, win_fg, h1, mix_norm, dh2, tm=tm, scale=0.5,
                                                  name="mix_bwd_in", comm=_share_halves(halves_f2))
    wide = d_proj.shape[1]
    d_win_nat, _ = _matmul_tn(
        n2, d_proj, tm=tm, nb=1, kb=D,
        x_spec=pl.BlockSpec((tm, D), lambda s, i: (i, 0)), y_spec=pl.BlockSpec((tm, wide), lambda s, i: (i, 0)),
        out_shape=jax.ShapeDtypeStruct((D, wide), F32), out_spec=pl.BlockSpec((D, wide), lambda s, i: (0, 0)),
        name="dw_in")
    d_win = jnp.moveaxis(d_win_nat[:, :N_SHARD * ins].reshape(D, N_SHARD, ins), 1, 0)
    grads_mx = [d_win, d_wout.reshape(N_SHARD, D // N_SHARD, D)]
    names_mx = ["win", "wout"]

    (dgu1, d_wd1), got_mx = _ffn_bwd_act(dh1b, gu1, wd1, tm=tm, guc=guc, name="ffn1_bwd_act",
                                         comm=_swap_halves(grads_mx))
    sums_mx = pair_sums(grads_mx, got_mx, names_mx)
    grads_d1 = [d_wd1.reshape(N_SHARD, ff // N_SHARD, D)]
    d_wgu1, out = dw_up(n1, dgu1, "ffn1_dw_up", comm=_join(_scatter_chips(sums_mx), _swap_halves(grads_d1)))
    landed_mx, got_d1 = out[:2], out[2:]
    halves_mx = chip_sums(grads_mx, got_mx, landed_mx, names_mx)
    sums_d1 = pair_sums(grads_d1, got_d1, ["wd1"])
    grads_u1 = [d_wgu1]
    (grad_x, d_meta, d_g1), out = _ffn_bwd_in_first(
        dgu1, wgu1, h0, ffn1_norm, dh1, tm=tm, batch=B, name="ffn1_bwd_in",
        comm=_join(_join(_share_halves(halves_mx), _scatter_chips(sums_d1)), _swap_halves(grads_u1)))
    g_mx, landed_d1, got_u1 = out[:2], out[2:3], out[3:]
    halves_d1 = chip_sums(grads_d1, got_d1, landed_d1, ["wd1"])
    sums_u1 = pair_sums(grads_u1, got_u1, ["wgu1"])
    out = _run_comm(_join(_share_halves(halves_d1), _scatter_chips(sums_u1)), name="scatter_ffn1")
    g_d1, landed_u1 = out[:1], out[1:]
    halves_u1 = chip_sums(grads_u1, got_u1, landed_u1, ["wgu1"])
    g_u1 = _run_comm(_share_halves(halves_u1), name="share_ffn1")
    g_big = [g_u1[0], g_d1[0], g_mx[0], g_mx[1], g_f2[0], g_f2[1]]

    loss_row = jnp.zeros((1, D), F32).at[0, 0].set(loss_part[0, 0])
    slab = _pack_small(D, d_meta, d_g1, d_gm, d_g3, d_gf, d_gc, d_ga, d_bf[:, :H], d_cw[:3])
    slab = slab.at[SMALL_ROWS - 1].set(loss_row[0])
    total = _all_reduce_small(slab, name="reduce_small")
    loss = total[SMALL_ROWS - 1, 0]
    mcols = meta_tokens.shape[-1]
    ccols = conv_w.shape[-1]
    full_like = (jnp.zeros((N_META, D)), ffn1_norm, mix_norm, ffn2_norm, final_norm.reshape(1, D), out_norm_conv,
                 out_norm_attn, b_f, jnp.zeros((1, 3, CD)))
    g_small = _unpack_small(total, full_like)
    g_small[0] = lax.dynamic_slice_in_dim(g_small[0], chip * mcols, mcols, axis=1)
    g_small[8] = lax.dynamic_slice_in_dim(g_small[8], chip * ccols, ccols, axis=2)

    def small_slab(meta, a1, am, a3, af, gc, ga, bf, cw):
        return _pack_small(D, meta, a1, am, a3, af.reshape(1, D), gc, ga, bf, cw[0])

    w_small = small_slab(meta_tokens, ffn1_norm, mix_norm, ffn2_norm, final_norm, out_norm_conv, out_norm_attn, b_f, conv_w)
    m_small = small_slab(m_meta_tokens, m_ffn1_norm, m_mix_norm, m_ffn2_norm, m_final_norm, m_out_norm_conv,
                         m_out_norm_attn, m_b_f, m_conv_w)
    v_small = small_slab(v_meta_tokens, v_ffn1_norm, v_mix_norm, v_ffn2_norm, v_final_norm, v_out_norm_conv,
                         v_out_norm_attn, v_b_f, v_conv_w)
    gs = list(g_small)
    gs[4] = gs[4].reshape(final_norm.shape)
    g_slab = small_slab(gs[0], gs[1], gs[2], gs[3], gs[4], gs[5], gs[6], gs[7], gs[8])
    local_like = (meta_tokens, ffn1_norm, mix_norm, ffn2_norm, final_norm.reshape(1, D), out_norm_conv, out_norm_attn,
                  b_f, conv_w)
    small_out = [_unpack_small(s, local_like)
                 for s in _adamw(w_small, g_slab, m_small, v_small, name="adamw_small")[1:]]
    for lst in small_out:
        lst[4] = lst[4].reshape(final_norm.shape)

    names = ["wgu1", "wd1", "win", "wout", "wgu2", "wd2"]
    w_big = big
    m_big = [m_ffn1_w_gu[0], m_ffn1_w_down[0], m_w_in[0], m_w_out[0], m_ffn2_w_gu[0], m_ffn2_w_down[0]]
    v_big = [v_ffn1_w_gu[0], v_ffn1_w_down[0], v_w_in[0], v_w_out[0], v_ffn2_w_gu[0], v_ffn2_w_down[0]]
    big_out = [_adamw(w, g, m, v, name="adamw_" + nm) for w, g, m, v, nm in zip(w_big, g_big, m_big, v_big, names)]

    def assemble(small, bigs):
        meta, a1, am, a3, af, gc, ga, bf, cw = small
        gu1_, d1_, win_, wout_, gu2_, d2_ = [b[None] for b in bigs]
        return [meta, a1, gu1_, d1_, am, win_, cw, bf, gc, ga, wout_, a3, gu2_, d2_, af]

    gs_out = list(g_small)
    gs_out[4] = gs_out[4].reshape(final_norm.shape)
    grads_out = assemble(gs_out, [b[0] for b in big_out])
    delta_out = assemble(small_out[0], [b[1] for b in big_out])
    m_out = assemble(small_out[1], [b[2] for b in big_out])
    v_out = assemble(small_out[2], [b[3] for b in big_out])
    return (loss, grad_x, *grads_out, *delta_out, *m_out, *v_out)
```

```python
import functools

import jax
import jax.numpy as jnp
from jax import lax
from jax.experimental import pallas as pl
from jax.experimental.pallas import tpu as pltpu

F32 = jnp.float32
BF16 = jnp.bfloat16

EPS = 1e-6
N_META = 16
HEAD_DIM = 64
N_SHARD = 4
N_DEV = 8
HALO = 16
LANES = 128
SMALL_ROWS = 32
VMEM_LIMIT_V7X = 56 * 1024 * 1024
NEG = -1e30
ATTN_BANDS = 2

ADAM_LR = 0.001
ADAM_B1 = 0.9
ADAM_B2 = 0.999
ADAM_EPS = 1e-08
ADAM_WD = 0.01
ADAM_STEP = 10

MESH = pl.DeviceIdType.MESH
ANY = pl.BlockSpec(memory_space=pl.ANY)
NT_DIMS = (((1,), (1,)), ((), ()))
TN_DIMS = (((0,), (0,)), ((), ()))


def _params(*sem):
    return pltpu.CompilerParams(dimension_semantics=sem, vmem_limit_bytes=VMEM_LIMIT_V7X)


class _Comm:
    def __init__(self, ins, out_shapes, sems, start, finish, aliases=None):
        self.ins, self.out_shapes, self.sems = list(ins), list(out_shapes), list(sems)
        self.start, self.finish, self.aliases = start, finish, dict(aliases or {})


def _join(a, b):
    ni, no, ns = len(a.ins), len(a.out_shapes), len(a.sems)

    def start(ins, outs, sems):
        a.start(ins[:ni], outs[:no], sems[:ns])
        b.start(ins[ni:], outs[no:], sems[ns:])

    def finish(ins, outs, sems):
        a.finish(ins[:ni], outs[:no], sems[:ns])
        b.finish(ins[ni:], outs[no:], sems[ns:])

    aliases = dict(a.aliases)
    aliases.update({ni + i: no + j for i, j in b.aliases.items()})
    return _Comm(a.ins + b.ins, a.out_shapes + b.out_shapes, a.sems + b.sems, start, finish, aliases)


def _launch(body, *, name, grid, in_specs, out_specs, out_shape, args, scratch_shapes=(), comm=None, prefetch=(),
            aliases=None):
    single = not isinstance(out_shape, (list, tuple))
    out_specs = [out_specs] if single else list(out_specs)
    out_shape = [out_shape] if single else list(out_shape)
    in_specs, scratch_shapes, prefetch = list(in_specs), list(scratch_shapes), list(prefetch)
    params = _params(*(("arbitrary",) * len(grid)))
    n_pf, n_in, n_out, n_scr = len(prefetch), len(in_specs), len(out_specs), len(scratch_shapes)
    c_ins = comm.ins if comm else []
    c_shapes = comm.out_shapes if comm else []
    c_sems = comm.sems if comm else []
    c_in, c_out = len(c_ins), len(c_shapes)

    def carrier(*refs):
        p = 0
        pf = refs[p:p + n_pf]; p += n_pf
        a = refs[p:p + n_in]; p += n_in
        ci = refs[p:p + c_in]; p += c_in
        o = refs[p:p + n_out]; p += n_out
        co = refs[p:p + c_out]; p += c_out
        s = refs[p:p + n_scr]; p += n_scr
        cs = refs[p:]
        if comm:
            first = functools.reduce(lambda u, v: u & v, [pl.program_id(k) == 0 for k in range(len(grid))])

            @pl.when(first)
            def _():
                comm.start(ci, co, cs)

        body(*pf, *a, *o, *s)

        if comm:
            last = functools.reduce(lambda u, v: u & v, [pl.program_id(k) == grid[k] - 1 for k in range(len(grid))])

            @pl.when(last)
            def _():
                comm.finish(ci, co, cs)

    io_aliases = {n_pf + i: j for i, j in (aliases or {}).items()}
    if comm:
        io_aliases.update({n_pf + n_in + i: n_out + j for i, j in comm.aliases.items()})
    all_in, all_out = in_specs + [ANY] * c_in, out_specs + [ANY] * c_out
    all_scratch = scratch_shapes + [pltpu.SemaphoreType.DMA((k,)) for k in c_sems]
    if n_pf:
        spec = dict(grid_spec=pltpu.PrefetchScalarGridSpec(
            num_scalar_prefetch=n_pf, grid=grid, in_specs=all_in, out_specs=all_out, scratch_shapes=all_scratch))
    else:
        spec = dict(grid=grid, in_specs=all_in, out_specs=all_out, scratch_shapes=all_scratch)
    res = pl.pallas_call(carrier, name=name, out_shape=out_shape + c_shapes, input_output_aliases=io_aliases,
                         compiler_params=params, **spec)(*prefetch, *args, *c_ins)
    main = list(res[:n_out])
    return (main[0] if single else main), (list(res[n_out:]) if comm else None)


def _run_comm(comm, *, name):
    c_in, c_out = len(comm.ins), len(comm.out_shapes)

    def body(*refs):
        ci, co, cs = refs[:c_in], refs[c_in:c_in + c_out], refs[c_in + c_out:]
        comm.start(ci, co, cs)
        comm.finish(ci, co, cs)

    return list(pl.pallas_call(
        body, name=name, in_specs=[ANY] * c_in, out_specs=[ANY] * c_out, out_shape=comm.out_shapes,
        scratch_shapes=[pltpu.SemaphoreType.DMA((k,)) for k in comm.sems],
        input_output_aliases=comm.aliases)(*comm.ins))


def _chunks(width, step=512):
    out, c0 = [], 0
    while c0 < width:
        cw = min(step, width - c0)
        out.append((c0, cw))
        c0 += cw
    return out


def _split2(v):
    hi = v.astype(BF16)
    lo = (v - hi.astype(F32)).astype(BF16)
    return hi, lo


def _split3(v):
    hi = v.astype(BF16)
    r = v - hi.astype(F32)
    mid = r.astype(BF16)
    lo = (r - mid.astype(F32)).astype(BF16)
    return hi, mid, lo


def _dot(a, b):
    return jnp.dot(a, b, preferred_element_type=F32)


def _dot_nt(a, b):
    return lax.dot_general(a, b, NT_DIMS, preferred_element_type=F32)


def _dot_tn(a, b):
    return lax.dot_general(a, b, TN_DIMS, preferred_element_type=F32)


def _silu_mul(g, u):
    return g * jax.nn.sigmoid(g) * u


def _rms_bwd(dn, h, gain, dres):
    r = lax.rsqrt(jnp.mean(h * h, axis=-1, keepdims=True) + EPS)
    y = h * r
    dgain = jnp.sum(dn * y, axis=0, keepdims=True)
    dy = dn * gain
    dh = dres + r * (dy - y * jnp.mean(dy * y, axis=-1, keepdims=True))
    return dh, dgain


def _group_mean(v, p):
    hi, lo = _split2(v)
    return _dot(hi, p) + _dot(lo, p)


def _row_of(a, k):
    rows = lax.broadcasted_iota(jnp.int32, a.shape, 0)
    return jnp.sum(jnp.where(rows == k, a, 0.0), axis=0, keepdims=True)


def _causal_conv(u, prev, w):
    rows = lax.broadcasted_iota(jnp.int32, u.shape, 0)
    p1 = _row_of(prev, HALO - 1)
    p2 = _row_of(prev, HALO - 2)
    u1 = jnp.where(rows == 0, p1, pltpu.roll(u, 1, 0))
    u2 = jnp.where(rows == 0, p2, jnp.where(rows == 1, p1, pltpu.roll(u, 2, 0)))
    return w[2:3, :] * u + w[1:2, :] * u1 + w[0:1, :] * u2, u1, u2


def _rms(x, gain):
    return (x * lax.rsqrt(jnp.mean(x * x, axis=-1, keepdims=True) + EPS) * gain).astype(BF16)


def _embed_norm(x, meta, g, *, tm, name, comm=None):
    B, S, D = x.shape
    L = S + N_META
    per_seq = L // tm
    nt = B * per_seq
    body_rows = tm - N_META

    def body(meta_ref, g_ref, x_hbm, h_ref, n_ref, buf, sems):
        i = pl.program_id(0)

        def fetch(k, fn):
            slot, b, t = k % 2, k // per_seq, k % per_seq

            @pl.when(t == 0)
            def _():
                fn(pltpu.make_async_copy(x_hbm.at[b, pl.ds(0, body_rows)],
                                         buf.at[slot, pl.ds(N_META, body_rows)], sems.at[slot]))

            @pl.when(t != 0)
            def _():
                fn(pltpu.make_async_copy(x_hbm.at[b, pl.ds(pl.multiple_of(t * tm - N_META, 8), tm)],
                                         buf.at[slot], sems.at[slot]))

        @pl.when(i == 0)
        def _():
            fetch(i, lambda cp: cp.start())

        @pl.when(i + 1 < nt)
        def _():
            fetch(i + 1, lambda cp: cp.start())

        fetch(i, lambda cp: cp.wait())
        slot = i % 2

        @pl.when(i % per_seq == 0)
        def _():
            buf[slot, 0:N_META, :] = meta_ref[...]

        hv = buf[slot]
        h_ref[...] = hv
        n_ref[...] = _rms(hv, g_ref[...])

    row = pl.BlockSpec((tm, D), lambda i: (i, 0))
    return _launch(
        body, name=name, grid=(nt,),
        in_specs=[pl.BlockSpec((N_META, D), lambda i: (0, 0)), pl.BlockSpec((1, D), lambda i: (0, 0)), ANY],
        out_specs=[row, row],
        out_shape=[jax.ShapeDtypeStruct((B * L, D), F32), jax.ShapeDtypeStruct((B * L, D), BF16)],
        scratch_shapes=[pltpu.VMEM((2, tm, D), F32), pltpu.SemaphoreType.DMA((2,))],
        args=(meta, g, x), comm=comm)


def _ffn_up(n, wgu, sid, gu_prev, *, tm, first, count, name, comm=None):
    T, D = n.shape
    ns, _, guc = wgu.shape
    ff = N_SHARD * guc // 2

    def body(sid_ref, x_ref, w_ref, *rest):
        rest[-1][...] = _dot(x_ref[...], w_ref[...]).astype(BF16)

    where = lambda s, sid: sid[first + s]
    w_at = (lambda s, sid: 0) if ns == 1 else where
    return _launch(
        body, name=name, grid=(count, T // tm), prefetch=(sid,),
        in_specs=[pl.BlockSpec((tm, D), lambda s, i, sid: (i, 0)),
                  pl.BlockSpec((None, D, guc), lambda s, i, sid: (w_at(s, sid), 0, 0))]
                 + ([] if gu_prev is None else [ANY]),
        out_specs=pl.BlockSpec((None, tm, guc), lambda s, i, sid: (where(s, sid) // 2, i, where(s, sid) % 2)),
        out_shape=jax.ShapeDtypeStruct((2, T, ff), BF16),
        args=(n, wgu) + (() if gu_prev is None else (gu_prev,)),
        aliases=None if gu_prev is None else {2: 0}, comm=comm)


def _matmul_nn(x, w, *, tm, nb, w_spec, out_shape, out_spec, name, comm=None):
    T, K = x.shape

    def body(x_ref, w_ref, o_ref):
        o_ref[...] = _dot(x_ref[...], w_ref[...]).astype(o_ref.dtype)

    return _launch(
        body, name=name, grid=(nb, T // tm),
        in_specs=[pl.BlockSpec((tm, K), lambda s, i: (i, 0)), w_spec],
        out_specs=out_spec, out_shape=out_shape, args=(x, w), comm=comm)


def _ffn_down(gu, wd, h, next_gain, *, tm, name, comm=None):
    _, T, ff = gu.shape
    D = h.shape[1]
    chunks = _chunks(ff)

    def body(g_ref, u_ref, wd_hbm, h_ref, ng_ref, o_ref, n_ref, wd_v, sem):
        @pl.when(pl.program_id(0) == 0)
        def _():
            cp = pltpu.make_async_copy(wd_hbm, wd_v, sem)
            cp.start()
            cp.wait()

        def act(c0, cw):
            return _silu_mul(g_ref[:, c0:c0 + cw].astype(F32), u_ref[:, c0:c0 + cw].astype(F32)).astype(BF16)

        acc = jnp.zeros((tm, D), F32)
        nxt = act(*chunks[0])
        for k, (c0, cw) in enumerate(chunks):
            a = nxt
            if k + 1 < len(chunks):
                nxt = act(*chunks[k + 1])
            acc = acc + _dot(a, wd_v[c0:c0 + cw, :])
        out = h_ref[...] + 0.5 * acc
        o_ref[...] = out
        n_ref[...] = _rms(out, ng_ref[...])

    return _launch(
        body, name=name, grid=(T // tm,),
        in_specs=[pl.BlockSpec((None, tm, ff), lambda i: (0, i, 0)),
                  pl.BlockSpec((None, tm, ff), lambda i: (1, i, 0)),
                  ANY,
                  pl.BlockSpec((tm, D), lambda i: (i, 0)),
                  pl.BlockSpec((1, D), lambda i: (0, 0))],
        out_specs=[pl.BlockSpec((tm, D), lambda i: (i, 0)), pl.BlockSpec((tm, D), lambda i: (i, 0))],
        out_shape=[jax.ShapeDtypeStruct((T, D), F32), jax.ShapeDtypeStruct((T, D), BF16)],
        scratch_shapes=[pltpu.VMEM((ff, D), BF16), pltpu.SemaphoreType.DMA],
        args=(gu, gu, wd, h, next_gain), comm=comm)


def _ffn_down_loss(gu, wd, h, gf, tgt, *, tm, name, comm=None):
    _, T, ff = gu.shape
    D = h.shape[1]
    B, S, _ = tgt.shape
    per_seq = (S + N_META) // tm
    body_rows = tm - N_META
    chunks = _chunks(ff)

    def body(g_ref, u_ref, wd_hbm, h_ref, gf_ref, tgt_hbm, dh_ref, dhb_ref, dg_ref, loss_ref, wd_v, tg_v, sem, tsem):
        i = pl.program_id(0)
        b, t = i // per_seq, i % per_seq

        @pl.when(i == 0)
        def _():
            cp = pltpu.make_async_copy(wd_hbm, wd_v, sem)
            cp.start()
            cp.wait()
            dg_ref[...] = jnp.zeros_like(dg_ref)
            loss_ref[...] = jnp.zeros_like(loss_ref)
            tg_v[0:N_META, :] = jnp.zeros((N_META, D), F32)

        def fetch(fn):
            @pl.when(t == 0)
            def _():
                fn(pltpu.make_async_copy(tgt_hbm.at[b, pl.ds(0, body_rows)], tg_v.at[pl.ds(N_META, body_rows)], tsem))

            @pl.when(t != 0)
            def _():
                fn(pltpu.make_async_copy(tgt_hbm.at[b, pl.ds(pl.multiple_of(t * tm - N_META, 8), tm)], tg_v, tsem))

        fetch(lambda cp: cp.start())
        def act(c0, cw):
            return _silu_mul(g_ref[:, c0:c0 + cw].astype(F32), u_ref[:, c0:c0 + cw].astype(F32)).astype(BF16)

        acc = jnp.zeros((tm, D), F32)
        nxt = act(*chunks[0])
        for k, (c0, cw) in enumerate(chunks):
            a = nxt
            if k + 1 < len(chunks):
                nxt = act(*chunks[k + 1])
            acc = acc + _dot(a, wd_v[c0:c0 + cw, :])
        x = h_ref[...] + 0.5 * acc
        fetch(lambda cp: cp.wait())

        gain = gf_ref[...]
        r = lax.rsqrt(jnp.mean(x * x, axis=-1, keepdims=True) + EPS)
        y = x * r
        pos = t * tm + lax.broadcasted_iota(jnp.int32, (tm, 1), 0)
        err = jnp.where(pos >= N_META, y * gain - tg_v[...], 0.0)
        loss_ref[...] += 0.5 * jnp.sum(jnp.mean(err * err, axis=-1, keepdims=True))
        dout = err / D
        dg_ref[...] += jnp.sum(dout * y, axis=0, keepdims=True)
        dy = dout * gain
        dh = r * (dy - y * jnp.mean(dy * y, axis=-1, keepdims=True))
        dh_ref[...] = dh
        dhb_ref[...] = (0.5 * dh).astype(BF16)

    row = pl.BlockSpec((tm, D), lambda i: (i, 0))
    const = lambda i: (0, 0)
    return _launch(
        body, name=name, grid=(T // tm,),
        in_specs=[pl.BlockSpec((None, tm, ff), lambda i: (0, i, 0)),
                  pl.BlockSpec((None, tm, ff), lambda i: (1, i, 0)),
                  ANY, row, pl.BlockSpec((1, D), const), ANY],
        out_specs=[row, row, pl.BlockSpec((1, D), const), pl.BlockSpec((1, LANES), const)],
        out_shape=[jax.ShapeDtypeStruct((T, D), F32), jax.ShapeDtypeStruct((T, D), BF16),
                   jax.ShapeDtypeStruct((1, D), F32), jax.ShapeDtypeStruct((1, LANES), F32)],
        scratch_shapes=[pltpu.VMEM((ff, D), BF16), pltpu.VMEM((tm, D), F32), pltpu.SemaphoreType.DMA,
                        pltpu.SemaphoreType.DMA],
        args=(gu, gu, wd, h, gf, tgt), comm=comm)


def _ffn_bwd_act(df, gu, wd, *, tm, guc, name, comm=None):
    _, T, ff = gu.shape
    D = df.shape[1]
    nj = ff // guc
    chunks = _chunks(guc)

    def body(df_ref, g_ref, u_ref, wd_ref, o_ref, dwd_ref):
        @pl.when(pl.program_id(1) == 0)
        def _():
            dwd_ref[...] = jnp.zeros_like(dwd_ref)

        dfv = df_ref[...]
        nxt = _dot_nt(dfv, wd_ref[chunks[0][0]:chunks[0][0] + chunks[0][1], :])
        for k, (c0, cw) in enumerate(chunks):
            da = nxt
            if k + 1 < len(chunks):
                n0, nw = chunks[k + 1]
                nxt = _dot_nt(dfv, wd_ref[n0:n0 + nw, :])
            g = g_ref[:, c0:c0 + cw].astype(F32)
            u = u_ref[:, c0:c0 + cw].astype(F32)
            sg = jax.nn.sigmoid(g)
            silu = g * sg
            o_ref[0, :, c0:c0 + cw] = (da * u * (sg * (1.0 + g * (1.0 - sg)))).astype(BF16)
            o_ref[1, :, c0:c0 + cw] = (da * silu).astype(BF16)
            dwd_ref[c0:c0 + cw, :] += _dot_tn((silu * u).astype(BF16), dfv)

    return _launch(
        body, name=name, grid=(nj, T // tm),
        in_specs=[pl.BlockSpec((tm, D), lambda j, i: (i, 0)),
                  pl.BlockSpec((None, tm, guc), lambda j, i: (0, i, j)),
                  pl.BlockSpec((None, tm, guc), lambda j, i: (1, i, j)),
                  pl.BlockSpec((guc, D), lambda j, i: (j, 0))],
        out_specs=[pl.BlockSpec((2, tm, guc), lambda j, i: (0, i, j)), pl.BlockSpec((guc, D), lambda j, i: (j, 0))],
        out_shape=[jax.ShapeDtypeStruct((2, T, ff), BF16), jax.ShapeDtypeStruct((ff, D), F32)],
        args=(df, gu, gu, wd), comm=comm)


def _ffn_bwd_in(dgu, wgu, h, g, dres, *, tm, scale, name, comm=None):
    _, T, ff = dgu.shape
    ns, D, guc = wgu.shape
    nj = ff // guc
    edges = _band_edges(tm)

    def body(dgu_ref, w_hbm, h_ref, g_ref, dres_ref, dh_ref, dhb_ref, dg_ref, w_v, acc, sem):
        i, j = pl.program_id(0), pl.program_id(1)

        @pl.when((i == 0) & (j == 0))
        def _():
            cp = pltpu.make_async_copy(w_hbm, w_v, sem)
            cp.start()
            cp.wait()
            dg_ref[...] = jnp.zeros_like(dg_ref)

        def dots(rows):
            return _dot_nt(dgu_ref[0, rows, :], w_v[j]) + _dot_nt(dgu_ref[1, rows, :], w_v[nj + j])

        @pl.when(j < nj - 1)
        def _():
            part = dots(slice(None))

            @pl.when(j == 0)
            def _():
                acc[...] = part

            @pl.when(j > 0)
            def _():
                acc[...] += part

        @pl.when(j == nj - 1)
        def _():
            bands = [slice(r0, r1) for r0, r1 in zip(edges[:-1], edges[1:])]
            nxt = dots(bands[0])
            for b, rows in enumerate(bands):
                dn = nxt if nj == 1 else acc[rows, :] + nxt
                if b + 1 < len(bands):
                    nxt = dots(bands[b + 1])
                dh, dgain = _rms_bwd(dn, h_ref[rows, :], g_ref[...], dres_ref[rows, :])
                dh_ref[rows, :] = dh
                dhb_ref[rows, :] = (scale * dh).astype(BF16)
                dg_ref[...] += dgain

    return _launch(
        body, name=name, grid=(T // tm, nj),
        in_specs=[pl.BlockSpec((2, tm, guc), lambda i, j: (0, i, j)),
                  ANY,
                  pl.BlockSpec((tm, D), lambda i, j: (i, 0)),
                  pl.BlockSpec((1, D), lambda i, j: (0, 0)),
                  pl.BlockSpec((tm, D), lambda i, j: (i, 0))],
        out_specs=[pl.BlockSpec((tm, D), lambda i, j: (i, 0)),
                   pl.BlockSpec((tm, D), lambda i, j: (i, 0)),
                   pl.BlockSpec((1, D), lambda i, j: (0, 0))],
        out_shape=[jax.ShapeDtypeStruct((T, D), F32), jax.ShapeDtypeStruct((T, D), BF16),
                   jax.ShapeDtypeStruct((1, D), F32)],
        scratch_shapes=[pltpu.VMEM((ns, D, guc), BF16), pltpu.VMEM((tm, D), F32), pltpu.SemaphoreType.DMA],
        args=(dgu, wgu, h, g, dres), comm=comm)


def _ffn_bwd_in_first(dgu, wgu, h, g, dres, *, tm, batch, name, comm=None):
    _, T, ff = dgu.shape
    ns, D, guc = wgu.shape
    nj = ff // guc
    nt = T // tm
    L = T // batch
    per_seq = L // tm
    body_rows = tm - N_META
    edges = _band_edges(tm)

    def body(dgu_ref, w_hbm, h_ref, g_ref, dres_ref, dx_hbm, dmeta_ref, dg_ref, w_v, acc, dh_v, sem, osem):
        i, j = pl.program_id(0), pl.program_id(1)

        @pl.when((i == 0) & (j == 0))
        def _():
            cp = pltpu.make_async_copy(w_hbm, w_v, sem)
            cp.start()
            cp.wait()
            dg_ref[...] = jnp.zeros_like(dg_ref)
            dmeta_ref[...] = jnp.zeros_like(dmeta_ref)

        def dots(rows):
            return _dot_nt(dgu_ref[0, rows, :], w_v[j]) + _dot_nt(dgu_ref[1, rows, :], w_v[nj + j])

        @pl.when(j < nj - 1)
        def _():
            part = dots(slice(None))

            @pl.when(j == 0)
            def _():
                acc[...] = part

            @pl.when(j > 0)
            def _():
                acc[...] += part

        def head_copy(b):
            return pltpu.make_async_copy(dh_v.at[pl.ds(N_META, body_rows)], dx_hbm.at[b, pl.ds(0, body_rows)], osem)

        def tail_copy(b, t):
            return pltpu.make_async_copy(dh_v, dx_hbm.at[b, pl.ds(pl.multiple_of(t * tm - N_META, 8), tm)], osem)

        def on_tile(k, head_fn, tail_fn):
            @pl.when(k % per_seq == 0)
            def _():
                head_fn(head_copy(k // per_seq))

            @pl.when(k % per_seq != 0)
            def _():
                tail_fn(tail_copy(k // per_seq, k % per_seq))

        @pl.when(j == nj - 1)
        def _():
            @pl.when(i > 0)
            def _():
                on_tile(i - 1, lambda cp: cp.wait(), lambda cp: cp.wait())

            bands = [slice(r0, r1) for r0, r1 in zip(edges[:-1], edges[1:])]
            nxt = dots(bands[0])
            for b, rows in enumerate(bands):
                dn = nxt if nj == 1 else acc[rows, :] + nxt
                if b + 1 < len(bands):
                    nxt = dots(bands[b + 1])
                dh, dgain = _rms_bwd(dn, h_ref[rows, :], g_ref[...], dres_ref[rows, :])
                dg_ref[...] += dgain
                dh_v[rows, :] = dh
                if b == 0:
                    @pl.when(i % per_seq == 0)
                    def _():
                        dmeta_ref[...] += dh[0:N_META, :]

            on_tile(i, lambda cp: cp.start(), lambda cp: cp.start())

            @pl.when(i == nt - 1)
            def _():
                on_tile(i, lambda cp: cp.wait(), lambda cp: cp.wait())

    return _launch(
        body, name=name, grid=(nt, nj),
        in_specs=[pl.BlockSpec((2, tm, guc), lambda i, j: (0, i, j)),
                  ANY,
                  pl.BlockSpec((tm, D), lambda i, j: (i, 0)),
                  pl.BlockSpec((1, D), lambda i, j: (0, 0)),
                  pl.BlockSpec((tm, D), lambda i, j: (i, 0))],
        out_specs=[ANY, pl.BlockSpec((N_META, D), lambda i, j: (0, 0)), pl.BlockSpec((1, D), lambda i, j: (0, 0))],
        out_shape=[jax.ShapeDtypeStruct((batch, L - N_META, D), F32), jax.ShapeDtypeStruct((N_META, D), F32),
                   jax.ShapeDtypeStruct((1, D), F32)],
        scratch_shapes=[pltpu.VMEM((ns, D, guc), BF16), pltpu.VMEM((tm, D), F32), pltpu.VMEM((tm, D), F32),
                        pltpu.SemaphoreType.DMA, pltpu.SemaphoreType.DMA],
        args=(dgu, wgu, h, g, dres), comm=comm)


def _mix_bwd_in(parts, w_main, w_fg, h, g, dres, *, tm, scale, name, comm=None):
    T, D = h.shape
    widths = [p.shape[1] for p in parts]
    offs = [sum(widths[:k]) for k in range(len(widths))]
    npart = len(parts)
    wide = sum(widths)
    edges = _band_edges(tm)

    def body(*refs):
        p_refs = refs[:npart]
        wm_ref, wf_ref, h_ref, g_ref, dres_ref, dh_ref, dhb_ref, dg_ref, all_ref = refs[npart:]

        @pl.when(pl.program_id(0) == 0)
        def _():
            dg_ref[...] = jnp.zeros_like(dg_ref)

        for p_ref, off, wd_ in zip(p_refs, offs, widths):
            for c0, cw in _chunks(wd_):
                all_ref[:, off + c0:off + c0 + cw] = p_ref[:, c0:c0 + cw].astype(BF16)
        n_main = offs[-1]

        def dots(rows):
            return _dot_nt(all_ref[rows, :n_main], wm_ref[...]) + _dot_nt(all_ref[rows, n_main:], wf_ref[...])

        bands = [slice(r0, r1) for r0, r1 in zip(edges[:-1], edges[1:])]
        nxt = dots(bands[0])
        for b, rows in enumerate(bands):
            dn = nxt
            if b + 1 < len(bands):
                nxt = dots(bands[b + 1])
            dh, dgain = _rms_bwd(dn, h_ref[rows, :], g_ref[...], dres_ref[rows, :])
            dh_ref[rows, :] = dh
            dhb_ref[rows, :] = (scale * dh).astype(BF16)
            dg_ref[...] += dgain

    row = lambda i: (i, 0)
    const = lambda i: (0, 0)
    return _launch(
        body, name=name, grid=(T // tm,),
        in_specs=[pl.BlockSpec((tm, p.shape[1]), row) for p in parts]
                 + [pl.BlockSpec(w_main.shape, const), pl.BlockSpec(w_fg.shape, const),
                    pl.BlockSpec((tm, D), row), pl.BlockSpec((1, D), const), pl.BlockSpec((tm, D), row)],
        out_specs=[pl.BlockSpec((tm, D), row), pl.BlockSpec((tm, D), row), pl.BlockSpec((1, D), const),
                   pl.BlockSpec((tm, wide), row)],
        out_shape=[jax.ShapeDtypeStruct((T, D), F32), jax.ShapeDtypeStruct((T, D), BF16),
                   jax.ShapeDtypeStruct((1, D), F32), jax.ShapeDtypeStruct((T, wide), BF16)],
        args=(*parts, w_main, w_fg, h, g, dres), comm=comm)


def _matmul_tn(x, y, *, tm, nb, x_spec, y_spec, out_shape, out_spec, kb, name, comm=None):
    T = y.shape[-2]
    chunks = _chunks(kb)

    def body(x_ref, y_ref, o_ref):
        @pl.when(pl.program_id(1) == 0)
        def _():
            o_ref[...] = jnp.zeros_like(o_ref)

        yv = y_ref[...].astype(BF16)
        nxt = _dot_tn(x_ref[:, chunks[0][0]:chunks[0][0] + chunks[0][1]], yv)
        for k, (c0, cw) in enumerate(chunks):
            cur = nxt
            if k + 1 < len(chunks):
                n0, nw = chunks[k + 1]
                nxt = _dot_tn(x_ref[:, n0:n0 + nw], yv)
            o_ref[c0:c0 + cw, :] += cur

    return _launch(
        body, name=name, grid=(nb, T // tm),
        in_specs=[x_spec, y_spec], out_specs=out_spec, out_shape=out_shape, args=(x, y), comm=comm)


def _tri(n, lower):
    r = lax.broadcasted_iota(jnp.int32, (n, n), 0)
    c = lax.broadcasted_iota(jnp.int32, (n, n), 1)
    return jnp.where((r >= c) if lower else (r <= c), 1.0, 0.0).astype(BF16)


def _tri_dot(tri, v):
    hi, mid, lo = _split3(v)
    return _dot(tri, hi) + _dot(tri, mid) + _dot(tri, lo)


def _fcum(fg, bf, *, ch, name):
    B, L, W = fg.shape
    nch = L // ch

    def body(fg_ref, bf_ref, f_ref):
        tri = _tri(ch, True)
        carry = jnp.zeros((1, W), F32)
        for c in range(nch):
            x = fg_ref[c * ch:(c + 1) * ch, :] + bf_ref[...]
            lf = jnp.minimum(x, 0.0) - jnp.log(1.0 + jnp.exp(-jnp.abs(x)))
            f_ref[c * ch:(c + 1) * ch, :] = _tri_dot(tri, lf) + carry
            carry = carry + jnp.sum(lf, axis=0, keepdims=True)

    return pl.pallas_call(
        body, name=name, grid=(B,),
        in_specs=[pl.BlockSpec((None, L, W), lambda b: (b, 0, 0)), pl.BlockSpec((1, W), lambda b: (0, 0))],
        out_specs=pl.BlockSpec((None, L, W), lambda b: (b, 0, 0)),
        out_shape=jax.ShapeDtypeStruct((B, L, W), F32),
        compiler_params=_params("arbitrary"),
    )(fg, bf)


def _fcum_bwd(dF, fg, bf, *, ch, name):
    B, L, W = fg.shape
    nch = L // ch

    def body(df_ref, fg_ref, bf_ref, dfg_ref, db_ref):
        @pl.when(pl.program_id(0) == 0)
        def _():
            db_ref[...] = jnp.zeros_like(db_ref)

        tri = _tri(ch, False)
        carry = jnp.zeros((1, W), F32)
        dbs = jnp.zeros((1, W), F32)
        for c in reversed(range(nch)):
            d = df_ref[c * ch:(c + 1) * ch, :]
            dlf = _tri_dot(tri, d) + carry
            carry = carry + jnp.sum(d, axis=0, keepdims=True)
            x = fg_ref[c * ch:(c + 1) * ch, :] + bf_ref[...]
            dfg = dlf * jax.nn.sigmoid(-x)
            dfg_ref[c * ch:(c + 1) * ch, :] = dfg.astype(BF16)
            dbs = dbs + jnp.sum(dfg, axis=0, keepdims=True)
        db_ref[...] += dbs

    blk = pl.BlockSpec((None, L, W), lambda b: (b, 0, 0))
    return pl.pallas_call(
        body, name=name, grid=(B,),
        in_specs=[blk, blk, pl.BlockSpec((1, W), lambda b: (0, 0))],
        out_specs=[blk, pl.BlockSpec((1, W), lambda b: (0, 0))],
        out_shape=[jax.ShapeDtypeStruct((B, L, W), BF16), jax.ShapeDtypeStruct((1, W), F32)],
        compiler_params=_params("arbitrary"),
    )(dF, fg, bf)


def _band_edges(tq):
    return sorted({min(tq, (k * tq // ATTN_BANDS + HALO - 1) // HALO * HALO) for k in range(ATTN_BANDS + 1)})


def _pair(h):
    return slice((h // 2) * 2 * HEAD_DIM, (h // 2 + 1) * 2 * HEAD_DIM)


def _own_lanes(a, h):
    low = lax.broadcasted_iota(jnp.int32, a.shape, 1) < HEAD_DIM
    return jnp.where(low if h % 2 == 0 else jnp.logical_not(low), a, jnp.zeros_like(a))


def _sum_lane(h):
    return HEAD_DIM if h % 2 == 0 else 0


def _own_lanes_and_ones(a, h):
    lane = lax.broadcasted_iota(jnp.int32, a.shape, 1)
    low = lane < HEAD_DIM
    return jnp.where(low if h % 2 == 0 else jnp.logical_not(low), a,
                     jnp.where(lane == _sum_lane(h), jnp.ones_like(a), jnp.zeros_like(a)))


def _attn_fwd(proj, fr, *, tq, n_heads, name, comm=None):
    B, L, _ = proj.shape
    AD = n_heads * HEAD_DIM
    nq = L // tq
    W = LANES
    scale = HEAD_DIM ** -0.5
    edges = _band_edges(tq)

    v_ones, sum_lane = _own_lanes_and_ones, _sum_lane

    def body(q_ref, k_ref, v_ref, fr_ref, o_ref, lse_ref, m_s, acc_s):
        qi, ki = pl.program_id(1), pl.program_id(2)

        @pl.when(ki == 0)
        def _():
            m_s[...] = jnp.full_like(m_s, NEG)
            acc_s[...] = jnp.zeros_like(acc_s)

        def tile(diagonal):
            lane = lax.broadcasted_iota(jnp.int32, (tq, W), 1)
            m_all = m_s[...]
            m_out = m_all
            bands = [(r0, r1, r1 if diagonal else tq) for r0, r1 in zip(edges[:-1], edges[1:])]
            if diagonal:
                masks = {r0: (lax.broadcasted_iota(jnp.int32, (r1 - r0, c1), 1)
                              <= r0 + lax.broadcasted_iota(jnp.int32, (r1 - r0, c1), 0)) for r0, r1, c1 in bands}

            def scores(h, band):
                r0, r1, c1 = band
                sl = slice(h * HEAD_DIM, (h + 1) * HEAD_DIM)
                return _dot_nt(q_ref[r0:r1, sl] * scale, k_ref[0:c1, sl])

            work = [(h, band) for h in range(n_heads) for band in bands]
            nxt = scores(*work[0])
            for w, (h, band) in enumerate(work):
                r0, r1, c1 = band
                sl = slice(h * HEAD_DIM, (h + 1) * HEAD_DIM)
                s = nxt - fr_ref[h:h + 1, 0:c1]
                if w + 1 < len(work):
                    nxt = scores(*work[w + 1])
                if diagonal:
                    s = jnp.where(masks[r0], s, NEG)
                m_old = m_all[r0:r1, h:h + 1]
                m_new = jnp.maximum(m_old, jnp.max(s, axis=1, keepdims=True))
                alpha = jnp.exp(m_old - m_new)
                p = jnp.exp(s - m_new)
                own = slice(h * 2 * HEAD_DIM, (h + 1) * 2 * HEAD_DIM)
                acc_s[r0:r1, own] = alpha * acc_s[r0:r1, own] + _dot(p.astype(BF16), v_ones(v_ref[0:c1, _pair(h)], h))
                if r0 == 0:
                    m_parts = []
                m_parts.append(m_new)
                if r1 == tq:
                    m_out = jnp.where(lane == h, jnp.concatenate(m_parts, axis=0), m_out)
            m_s[...] = m_out

        @pl.when(ki < qi)
        def _():
            tile(False)

        @pl.when(ki == qi)
        def _():
            tile(True)
            lane = lax.broadcasted_iota(jnp.int32, (tq, W), 1)
            low = lax.broadcasted_iota(jnp.int32, (tq, 2 * HEAD_DIM), 1) < HEAD_DIM
            l_all = jnp.ones((tq, W), F32)
            for h in range(0, n_heads, 2):
                even = acc_s[:, h * 2 * HEAD_DIM:(h + 1) * 2 * HEAD_DIM]
                odd = acc_s[:, (h + 1) * 2 * HEAD_DIM:(h + 2) * 2 * HEAD_DIM]
                l_even = even[:, sum_lane(h):sum_lane(h) + 1]
                l_odd = odd[:, sum_lane(h + 1):sum_lane(h + 1) + 1]
                o_ref[:, _pair(h)] = jnp.where(low, even / l_even, odd / l_odd)
                l_all = jnp.where(lane == h, l_even, jnp.where(lane == h + 1, l_odd, l_all))
            lse_ref[...] = jnp.where(lane < n_heads, m_s[...] + jnp.log(l_all), 0.0)

    kv = lambda b, qi, ki: jnp.minimum(ki, qi)
    return _launch(
        body, name=name, grid=(B, nq, nq), args=(proj, proj, proj, fr), comm=comm,
        in_specs=[pl.BlockSpec((None, tq, AD), lambda b, qi, ki: (b, qi, 3)),
                  pl.BlockSpec((None, tq, AD), lambda b, qi, ki: (b, kv(b, qi, ki), 4)),
                  pl.BlockSpec((None, tq, AD), lambda b, qi, ki: (b, kv(b, qi, ki), 5)),
                  pl.BlockSpec((None, None, n_heads, tq), lambda b, qi, ki: (b, kv(b, qi, ki), 0, 0))],
        out_specs=[pl.BlockSpec((None, tq, AD), lambda b, qi, ki: (b, qi, 0)),
                   pl.BlockSpec((None, tq, W), lambda b, qi, ki: (b, qi, 0))],
        out_shape=[jax.ShapeDtypeStruct((B, L, AD), F32), jax.ShapeDtypeStruct((B, L, W), F32)],
        scratch_shapes=[pltpu.VMEM((tq, W), F32), pltpu.VMEM((tq, n_heads * 2 * HEAD_DIM), F32)])


def _attn_bwd(proj, o, do, lse, fr, *, tq, n_heads, name, comm=None):
    B, L, _ = proj.shape
    AD = n_heads * HEAD_DIM
    nq = L // tq
    W = LANES
    HW = 2 * HEAD_DIM
    scale = HEAD_DIM ** -0.5
    edges = _band_edges(tq)

    def body(q_ref, k_ref, v_ref, o_ref, do_ref, lse_ref, fr_ref,
             dq_ref, dk_ref, dv_ref, dfk_ref, dfq_ref, dq_s, dk_s, dv_s):
        kj, qi = pl.program_id(1), pl.program_id(2)

        @pl.when((kj == 0) & (qi == 0))
        def _():
            dq_s[...] = jnp.zeros_like(dq_s)

        @pl.when(qi == kj)
        def _():
            dk_s[...] = jnp.zeros_like(dk_s)
            dv_s[...] = jnp.zeros_like(dv_s)

        def tile(diagonal):
            bands = [(r0, r1, r1) for r0, r1 in zip(edges[:-1], edges[1:])] if diagonal else [(0, tq, tq)]
            lse = lse_ref[...]
            for r0, r1, c1 in bands:
                nr = r1 - r0
                rows = pl.ds(pl.multiple_of(qi * tq + r0, 8), nr)
                if diagonal:
                    mask = (lax.broadcasted_iota(jnp.int32, (nr, c1), 1)
                            <= r0 + lax.broadcasted_iota(jnp.int32, (nr, c1), 0))
                def scores(h):
                    ps = _pair(h)
                    k = k_ref[0:c1, ps]
                    qs = q_ref[r0:r1, ps] * scale
                    dov = _own_lanes(do_ref[r0:r1, ps], h)
                    return _dot_nt(_own_lanes(qs, h), k), _dot_nt(dov, v_ref[0:c1, ps]), k, qs, dov

                nxt = scores(0)
                for h in range(n_heads):
                    ps = _pair(h)
                    own = slice(h * HW, (h + 1) * HW)
                    s, dp, k, qs, dov = nxt
                    if h + 1 < n_heads:
                        nxt = scores(h + 1)
                    s = s - fr_ref[h:h + 1, 0:c1]
                    if diagonal:
                        s = jnp.where(mask, s, NEG)
                    p = jnp.exp(s - lse[r0:r1, h:h + 1])
                    dsum = jnp.sum(dov.astype(F32) * o_ref[r0:r1, ps], axis=1, keepdims=True)
                    dsb = (p * (dp - dsum)).astype(BF16)
                    dv = _dot_tn(p.astype(BF16), dov)
                    dk_s[0:c1, own] += _dot_tn(dsb, _own_lanes_and_ones(qs, h))
                    dq_s[rows, own] += _dot(dsb, _own_lanes_and_ones(k, h))
                    if h % 2 == 0:
                        dv_even = dv
                    else:
                        dv_s[0:c1, ps] += dv_even + dv

        def compact(acc, data_scale):
            rows = acc.shape[0]
            low = lax.broadcasted_iota(jnp.int32, (rows, HW), 1) < HEAD_DIM
            lane = lax.broadcasted_iota(jnp.int32, (rows, W), 1)
            vals, sums = [], jnp.zeros((rows, W), F32)
            for h in range(0, n_heads, 2):
                even, odd = acc[:, h * HW:(h + 1) * HW], acc[:, (h + 1) * HW:(h + 2) * HW]
                vals.append(jnp.where(low, even, odd) * data_scale)
                sums = jnp.where(lane == h, even[:, _sum_lane(h):_sum_lane(h) + 1],
                                 jnp.where(lane == h + 1, odd[:, _sum_lane(h + 1):_sum_lane(h + 1) + 1], sums))
            return vals, sums

        @pl.when(qi > kj)
        def _():
            tile(False)

        @pl.when(qi == kj)
        def _():
            tile(True)
            rows = pl.ds(pl.multiple_of(qi * tq, 8), tq)
            vals, sums = compact(dq_s[rows, :], scale)
            for h in range(0, n_heads, 2):
                dq_ref[rows, _pair(h)] = vals[h // 2]
            dfq_ref[rows, :] = sums

        @pl.when(qi == nq - 1)
        def _():
            vals, sums = compact(dk_s[...], 1.0)
            for h in range(0, n_heads, 2):
                dk_ref[:, _pair(h)] = vals[h // 2].astype(BF16)
            dfk_ref[...] = sums
            dv_ref[...] = dv_s[...].astype(BF16)

    qq = lambda b, kj, qi: jnp.maximum(qi, kj)
    qblk = lambda w, cb: pl.BlockSpec((None, tq, w), lambda b, kj, qi: (b, qq(b, kj, qi), cb))
    kblk = lambda w, cb: pl.BlockSpec((None, tq, w), lambda b, kj, qi: (b, kj, cb))
    return _launch(
        body, name=name, grid=(B, nq, nq), args=(proj, proj, proj, o, do, lse, fr), comm=comm,
        in_specs=[qblk(AD, 3), kblk(AD, 4), kblk(AD, 5), qblk(AD, 0), qblk(AD, 0), qblk(W, 0),
                  pl.BlockSpec((None, None, n_heads, tq), lambda b, kj, qi: (b, kj, 0, 0))],
        out_specs=[pl.BlockSpec((None, L, AD), lambda b, kj, qi: (b, 0, 0)),
                   kblk(AD, 0), kblk(AD, 0), kblk(W, 0),
                   pl.BlockSpec((None, L, W), lambda b, kj, qi: (b, 0, 0))],
        out_shape=[jax.ShapeDtypeStruct((B, L, AD), F32), jax.ShapeDtypeStruct((B, L, AD), BF16),
                   jax.ShapeDtypeStruct((B, L, AD), BF16), jax.ShapeDtypeStruct((B, L, W), F32),
                   jax.ShapeDtypeStruct((B, L, W), F32)],
        scratch_shapes=[pltpu.VMEM((L, n_heads * HW), F32), pltpu.VMEM((tq, n_heads * HW), F32),
                        pltpu.VMEM((tq, AD), F32)])


def _mix_gather(refs, first):
    b_ref, c_ref, hc_ref, cp_ref, hcp_ref, o_ref, cw_ref, p_ref = refs
    bg = b_ref[...].astype(F32)
    u = c_ref[...].astype(F32) * hc_ref[...].astype(F32)
    prev = cp_ref[...].astype(F32) * hcp_ref[...].astype(F32)
    prev = jnp.where(first, 0.0, prev)
    cv, u1, u2 = _causal_conv(u, prev, cw_ref[...])
    yc = bg * cv
    p = p_ref[...]
    rc = lax.rsqrt(_group_mean(yc * yc, p) + EPS)
    ya = o_ref[...].astype(F32)
    ra = lax.rsqrt(_group_mean(ya * ya, p) + EPS)
    return bg, (u, u1, u2), cv, yc * rc, rc, ya * ra, ra


def _mix_specs(tm, CD):
    per = tm // HALO
    cur = lambda cb: pl.BlockSpec((None, tm, CD), lambda b, i: (b, i, cb))
    prev = lambda cb: pl.BlockSpec((None, HALO, CD), lambda b, i: (b, jnp.maximum(i * per - 1, 0), cb))
    return [cur(0), cur(1), cur(2), prev(1), prev(2), cur(0)]


def _mix_out(proj, o, cw, gc, ga, wout, h, pmat, next_gain, *, tm, name, comm=None):
    B, L, D = h.shape
    CD = o.shape[-1]
    const = lambda b, i: (0, 0)

    def body(b_ref, c_ref, hc_ref, cp_ref, hcp_ref, o_ref, cw_ref, p_ref, gc_ref, ga_ref, w_ref, h_ref, ng_ref,
             out_ref, y_ref, n_ref):
        first = pl.program_id(1) == 0
        _, _, _, zc, _, za, _ = _mix_gather((b_ref, c_ref, hc_ref, cp_ref, hcp_ref, o_ref, cw_ref, p_ref), first)
        yc = (zc * gc_ref[...]).astype(BF16)
        ya = (za * ga_ref[...]).astype(BF16)
        y_ref[:, :CD] = yc
        y_ref[:, CD:] = ya
        out = h_ref[...] + _dot(yc, w_ref[:CD, :]) + _dot(ya, w_ref[CD:, :])
        out_ref[...] = out
        n_ref[...] = _rms(out, ng_ref[...])

    tile = pl.BlockSpec((None, tm, D), lambda b, i: (b, i, 0))
    return _launch(
        body, name=name, grid=(B, L // tm),
        in_specs=_mix_specs(tm, CD)
                 + [pl.BlockSpec(cw.shape, const), pl.BlockSpec(pmat.shape, const),
                    pl.BlockSpec((1, CD), const), pl.BlockSpec((1, CD), const), pl.BlockSpec((D, D), const),
                    tile, pl.BlockSpec((1, D), const)],
        out_specs=[tile, tile, tile],
        out_shape=[jax.ShapeDtypeStruct((B, L, D), F32), jax.ShapeDtypeStruct((B, L, D), BF16),
                   jax.ShapeDtypeStruct((B, L, D), BF16)],
        args=(proj, proj, proj, proj, proj, o, cw, pmat, gc, ga, wout, h, next_gain), comm=comm)


def _mix_out_bwd(dhb, proj, o, cw, gc, ga, wout, pmat, *, tm, name, comm=None):
    B, L, D = dhb.shape
    CD = o.shape[-1]
    const = lambda b, i: (0, 0)

    def body(dh_ref, b_ref, c_ref, hc_ref, cp_ref, hcp_ref, o_ref, cw_ref, p_ref, gc_ref, ga_ref, w_ref,
             db_ref, dcv_ref, do_ref, dgc_ref, dga_ref, dcw_ref):
        first = pl.program_id(1) == 0

        @pl.when((pl.program_id(0) == 0) & first)
        def _():
            dgc_ref[...] = jnp.zeros_like(dgc_ref)
            dga_ref[...] = jnp.zeros_like(dga_ref)
            dcw_ref[...] = jnp.zeros_like(dcw_ref)

        bg, us, cv, zc, rc, za, ra = _mix_gather(
            (b_ref, c_ref, hc_ref, cp_ref, hcp_ref, o_ref, cw_ref, p_ref), first)
        p = p_ref[...]
        dh = dh_ref[...]
        dyc = _dot_nt(dh, w_ref[:CD, :])
        dya = _dot_nt(dh, w_ref[CD:, :])

        dgc_ref[...] += jnp.sum(dyc * zc, axis=0, keepdims=True)
        dz = dyc * gc_ref[...]
        dx = rc * (dz - zc * _group_mean(dz * zc, p))
        db_ref[...] = (dx * cv).astype(BF16)
        dcv = dx * bg
        dcv_ref[...] = dcv.astype(BF16)
        for k in range(3):
            dcw_ref[k:k + 1, :] += jnp.sum(dcv * us[2 - k], axis=0, keepdims=True)

        dga_ref[...] += jnp.sum(dya * za, axis=0, keepdims=True)
        dz = dya * ga_ref[...]
        do_ref[...] = (ra * (dz - za * _group_mean(dz * za, p))).astype(BF16)

    tile = lambda w: pl.BlockSpec((None, tm, w), lambda b, i: (b, i, 0))
    return _launch(
        body, name=name, grid=(B, L // tm), comm=comm,
        args=(dhb, proj, proj, proj, proj, proj, o, cw, pmat, gc, ga, wout),
        in_specs=[tile(D)] + _mix_specs(tm, CD)
                 + [pl.BlockSpec(cw.shape, const), pl.BlockSpec(pmat.shape, const),
                    pl.BlockSpec((1, CD), const), pl.BlockSpec((1, CD), const), pl.BlockSpec((D, D), const)],
        out_specs=[tile(CD), tile(CD), tile(CD),
                   pl.BlockSpec((1, CD), const), pl.BlockSpec((1, CD), const), pl.BlockSpec((8, CD), const)],
        out_shape=[jax.ShapeDtypeStruct((B, L, CD), BF16)] * 3
                  + [jax.ShapeDtypeStruct((1, CD), F32)] * 2 + [jax.ShapeDtypeStruct((8, CD), F32)])


def _conv_bwd(dcv, proj, cw, *, tm, name):
    B, L, CD = dcv.shape
    per = tm // HALO
    nhalo = L // HALO
    nt = L // tm

    def body(d_ref, dn_ref, c_ref, hc_ref, cw_ref, out_ref):
        last = pl.program_id(1) == nt - 1
        d = d_ref[...].astype(F32)
        nxt = jnp.where(last, 0.0, dn_ref[...].astype(F32))
        n0, n1 = _row_of(nxt, 0), _row_of(nxt, 1)
        rows = lax.broadcasted_iota(jnp.int32, d.shape, 0)
        d1 = jnp.where(rows == tm - 1, n0, pltpu.roll(d, tm - 1, 0))
        d2 = jnp.where(rows == tm - 2, n0, jnp.where(rows == tm - 1, n1, pltpu.roll(d, tm - 2, 0)))
        w = cw_ref[...]
        du = w[2:3, :] * d + w[1:2, :] * d1 + w[0:1, :] * d2
        out_ref[:, :CD] = (du * hc_ref[...].astype(F32)).astype(BF16)
        out_ref[:, CD:] = (du * c_ref[...].astype(F32)).astype(BF16)

    return pl.pallas_call(
        body, name=name, grid=(B, nt),
        in_specs=[pl.BlockSpec((None, tm, CD), lambda b, i: (b, i, 0)),
                  pl.BlockSpec((None, HALO, CD), lambda b, i: (b, jnp.minimum((i + 1) * per, nhalo - 1), 0)),
                  pl.BlockSpec((None, tm, CD), lambda b, i: (b, i, 1)),
                  pl.BlockSpec((None, tm, CD), lambda b, i: (b, i, 2)),
                  pl.BlockSpec(cw.shape, lambda b, i: (0, 0))],
        out_specs=pl.BlockSpec((None, tm, 2 * CD), lambda b, i: (b, i, 0)),
        out_shape=jax.ShapeDtypeStruct((B, L, 2 * CD), BF16),
        compiler_params=_params("arbitrary", "arbitrary"),
    )(dcv, dcv, proj, proj, cw)


def _place():
    x, y, c = lax.axis_index("x"), lax.axis_index("y"), lax.axis_index("c")
    others = [(1 - x, y), (x, 1 - y), (1 - x, 1 - y)]
    return x, y, c, others


def _all_gather_shards(shards, *, name):
    n = len(shards)

    def body(*refs):
        ins, outs = refs[:n], refs[n:2 * n]
        send, recv, fsend, frecv, lsem = refs[2 * n:]
        x, y, c, others = _place()
        me = 2 * x + y
        local = [pltpu.make_async_copy(ins[t], outs[t].at[me], lsem.at[t]) for t in range(n)]
        for cp in local:
            cp.start()

        def half(t, k):
            hr = shards[t].shape[0] // 2
            return pl.ds(pl.multiple_of(k * hr, HALO), hr)

        def ici(t, j, src_chip, to):
            src = ins[t].at[half(t, c)] if to is not None else outs[t].at[src_chip, half(t, c)]
            return pltpu.make_async_remote_copy(
                src_ref=src, dst_ref=outs[t].at[src_chip, half(t, c)],
                send_sem=send.at[3 * t + j], recv_sem=recv.at[3 * t + j],
                device_id=(x, y, c) if to is None else to, device_id_type=MESH)

        def d2d(t, j, src_chip, k):
            return pltpu.make_async_remote_copy(
                src_ref=outs[t].at[src_chip, half(t, k)], dst_ref=outs[t].at[src_chip, half(t, k)],
                send_sem=fsend.at[3 * t + j], recv_sem=frecv.at[3 * t + j],
                device_id=(x, y, 1 - c), device_id_type=MESH)

        firsts = [ici(t, j, me, (ox, oy, c)) for t in range(n) for j, (ox, oy) in enumerate(others)]
        for cp in firsts:
            cp.start()
        passed = []
        for t in range(n):
            for j, (ox, oy) in enumerate(others):
                ici(t, j, 2 * ox + oy, None).wait_recv()
                cp = d2d(t, j, 2 * ox + oy, c)
                cp.start()
                passed.append(cp)
        for t in range(n):
            for j, (ox, oy) in enumerate(others):
                d2d(t, j, 2 * ox + oy, 1 - c).wait_recv()
        for cp in firsts + passed:
            cp.wait_send()
        for cp in local:
            cp.wait()

    return pl.pallas_call(
        body, name=name,
        in_specs=[ANY] * n, out_specs=[ANY] * n,
        out_shape=[jax.ShapeDtypeStruct((N_SHARD,) + s.shape, s.dtype) for s in shards],
        scratch_shapes=[pltpu.SemaphoreType.DMA((3 * n,))] * 4 + [pltpu.SemaphoreType.DMA((n,))],
    )(*shards)


def _all_reduce_small(slab, *, name):
    def body(in_ref, out_ref, gath, send, recv):
        x, y, c, _ = _place()
        me = 4 * x + 2 * y + c
        gath[me] = in_ref[...]
        copies, peers = [], []
        for m in range(1, N_DEV):
            px = jnp.where((m >> 2) & 1, 1 - x, x)
            py = jnp.where((m >> 1) & 1, 1 - y, y)
            pc = jnp.where(m & 1, 1 - c, c)
            cp = pltpu.make_async_remote_copy(
                src_ref=in_ref, dst_ref=gath.at[me], send_sem=send.at[m - 1], recv_sem=recv.at[m - 1],
                device_id=(px, py, pc), device_id_type=MESH)
            cp.start()
            copies.append(cp)
            peers.append(4 * px + 2 * py + pc)
        for m in range(1, N_DEV):
            pltpu.make_async_remote_copy(
                src_ref=in_ref, dst_ref=gath.at[peers[m - 1]], send_sem=send.at[m - 1], recv_sem=recv.at[m - 1],
                device_id=(x, y, c), device_id_type=MESH).wait_recv()
        for cp in copies:
            cp.wait_send()
        acc = gath[0]
        for k in range(1, N_DEV):
            acc = acc + gath[k]
        out_ref[...] = acc

    vm = pl.BlockSpec(memory_space=pltpu.VMEM)
    return pl.pallas_call(
        body, name=name, in_specs=[vm], out_specs=vm,
        out_shape=jax.ShapeDtypeStruct(slab.shape, slab.dtype),
        scratch_shapes=[pltpu.VMEM((N_DEV,) + slab.shape, slab.dtype),
                        pltpu.SemaphoreType.DMA((N_DEV - 1,)), pltpu.SemaphoreType.DMA((N_DEV - 1,))],
    )(slab)


def _gather_stage(shards, into, *, ici=(), d2d=()):
    n = len(shards) if into is None else len(into)
    ns = len(shards) if ici else 0
    ni, nd = max(len(ici), 1), max(len(d2d), 1)
    shapes = [s.shape for s in shards] if into is None else [p.shape[1:] for p in into]
    dtypes = [s.dtype for s in shards] if into is None else [p.dtype for p in into]

    def copies(ins, outs, sems, sending):
        x, y, c, others = _place()
        me = 2 * x + y
        out = []
        for t in range(n):
            hr = shapes[t][0] // 2
            mine = pl.ds(pl.multiple_of(c * hr, HALO), hr)
            theirs = pl.ds(pl.multiple_of((1 - c) * hr, HALO), hr)
            for a, j in enumerate(ici):
                ox, oy = others[j]
                src_chip = me if sending else 2 * ox + oy
                out.append(pltpu.make_async_remote_copy(
                    src_ref=ins[t].at[mine], dst_ref=outs[t].at[src_chip, mine],
                    send_sem=sems[0].at[ni * t + a], recv_sem=sems[1].at[ni * t + a],
                    device_id=(ox, oy, c) if sending else (x, y, c), device_id_type=MESH))
            for a, j in enumerate(d2d):
                ox, oy = others[j]
                blk = outs[t].at[2 * ox + oy, mine if sending else theirs]
                out.append(pltpu.make_async_remote_copy(
                    src_ref=blk, dst_ref=blk, send_sem=sems[2].at[nd * t + a], recv_sem=sems[3].at[nd * t + a],
                    device_id=(x, y, 1 - c) if sending else (x, y, c), device_id_type=MESH))
        return out

    def local(ins, outs, sems):
        if into is not None:
            return []
        x, y, _, _ = _place()
        return [pltpu.make_async_copy(ins[t], outs[t].at[2 * x + y], sems[4].at[t]) for t in range(n)]

    def start(ins, outs, sems):
        for cp in local(ins, outs, sems) + copies(ins, outs, sems, True):
            cp.start()

    def finish(ins, outs, sems):
        for cp in copies(ins, outs, sems, False):
            cp.wait_recv()
        for cp in copies(ins, outs, sems, True):
            cp.wait_send()
        for cp in local(ins, outs, sems):
            cp.wait()

    return _Comm((list(shards) if ici or into is None else []) + (list(into) if into is not None else []),
                 [jax.ShapeDtypeStruct((N_SHARD,) + tuple(sh), dt) for sh, dt in zip(shapes, dtypes)],
                 [ni * n, ni * n, nd * n, nd * n, n], start, finish,
                 aliases=None if into is None else {ns + t: t for t in range(n)})


def _gather_ici(shards):
    return _gather_stage(shards, None, ici=(0, 1, 2))


def _gather_d2d(parts):
    return _gather_stage((), parts, d2d=(0, 1, 2))


def _swap_halves(grads):
    n = len(grads)

    def copies(ins, outs, sems):
        x, y, c, _ = _place()
        out = []
        for t in range(n):
            hr = grads[t].shape[1] // 2
            rows = pl.ds(pl.multiple_of((1 - c) * hr, 8), hr)
            out.append(pltpu.make_async_remote_copy(
                src_ref=ins[t].at[:, rows, :], dst_ref=outs[t], send_sem=sems[0].at[t], recv_sem=sems[1].at[t],
                device_id=(x, y, 1 - c), device_id_type=MESH))
        return out

    def start(ins, outs, sems):
        for cp in copies(ins, outs, sems):
            cp.start()

    def finish(ins, outs, sems):
        for cp in copies(ins, outs, sems):
            cp.wait()

    return _Comm(grads, [jax.ShapeDtypeStruct((N_SHARD, g.shape[1] // 2, g.shape[2]), g.dtype) for g in grads],
                 [n, n], start, finish)


def _pair_sum(g, got, c, *, name):
    ns, R, C = g.shape
    hr = R // 2

    def body(c_ref, g_ref, r_ref, o_ref):
        o_ref[...] = (g_ref[...] + r_ref[...]).astype(BF16)

    return pl.pallas_call(
        body, name=name,
        grid_spec=pltpu.PrefetchScalarGridSpec(
            num_scalar_prefetch=1, grid=(ns,),
            in_specs=[pl.BlockSpec((None, hr, C), lambda s, cr: (s, cr[0], 0)),
                      pl.BlockSpec((None, hr, C), lambda s, cr: (s, 0, 0))],
            out_specs=pl.BlockSpec((None, hr, C), lambda s, cr: (s, 0, 0))),
        out_shape=jax.ShapeDtypeStruct((ns, hr, C), BF16),
        compiler_params=_params("arbitrary"),
    )(c, g, got)


def _scatter_chips(sums):
    n = len(sums)

    def copies(ins, outs, sems, sending):
        x, y, c, others = _place()
        me = 2 * x + y
        out = []
        for t in range(n):
            for j, (ox, oy) in enumerate(others):
                there = 2 * ox + oy
                out.append(pltpu.make_async_remote_copy(
                    src_ref=ins[t].at[there if sending else me], dst_ref=outs[t].at[me if sending else there],
                    send_sem=sems[0].at[3 * t + j], recv_sem=sems[1].at[3 * t + j],
                    device_id=(ox, oy, c) if sending else (x, y, c), device_id_type=MESH))
        return out

    def start(ins, outs, sems):
        for cp in copies(ins, outs, sems, True):
            cp.start()

    def finish(ins, outs, sems):
        for cp in copies(ins, outs, sems, False):
            cp.wait_recv()
        for cp in copies(ins, outs, sems, True):
            cp.wait_send()

    return _Comm(sums, [jax.ShapeDtypeStruct(s.shape, s.dtype) for s in sums], [3 * n, 3 * n], start, finish)


def _chip_sum(g, got, landed, idx, *, name):
    ns, R, C = g.shape
    hr = R // 2

    def body(i_ref, g_ref, r_ref, a_ref, b_ref, c_ref, o_ref):
        acc = g_ref[...] + r_ref[...]
        for ref in (a_ref, b_ref, c_ref):
            acc = acc + ref[...].astype(F32)
        o_ref[...] = acc

    other = lambda k: pl.BlockSpec((None, hr, C), lambda s, ir: (ir[2 + k], 0, 0))
    return pl.pallas_call(
        body, name=name,
        grid_spec=pltpu.PrefetchScalarGridSpec(
            num_scalar_prefetch=1, grid=(1,),
            in_specs=[pl.BlockSpec((None, hr, C), lambda s, ir: (ir[0], ir[1], 0)),
                      pl.BlockSpec((None, hr, C), lambda s, ir: (ir[0], 0, 0)),
                      other(0), other(1), other(2)],
            out_specs=pl.BlockSpec((hr, C), lambda s, ir: (ir[1], 0))),
        out_shape=jax.ShapeDtypeStruct((R, C), F32),
        compiler_params=_params("arbitrary"),
    )(idx, g, got, landed, landed, landed)


def _share_halves(halves):
    n = len(halves)

    def copies(outs, sems, sending):
        x, y, c, _ = _place()
        out = []
        for t in range(n):
            hr = halves[t].shape[0] // 2
            rows = pl.ds(pl.multiple_of((c if sending else 1 - c) * hr, 8), hr)
            out.append(pltpu.make_async_remote_copy(
                src_ref=outs[t].at[rows, :], dst_ref=outs[t].at[rows, :], send_sem=sems[0].at[t],
                recv_sem=sems[1].at[t], device_id=(x, y, 1 - c) if sending else (x, y, c), device_id_type=MESH))
        return out

    def start(ins, outs, sems):
        for cp in copies(outs, sems, True):
            cp.start()

    def finish(ins, outs, sems):
        for cp in copies(outs, sems, False):
            cp.wait_recv()
        for cp in copies(outs, sems, True):
            cp.wait_send()

    return _Comm(halves, [jax.ShapeDtypeStruct(h.shape, h.dtype) for h in halves], [n, n], start, finish,
                 aliases={t: t for t in range(n)})


def _adamw(w, g, m, v, *, name):
    R, C = w.shape
    tr = R
    for cand in (256, 128, 64, 32, 16, 8):
        if R % cand == 0:
            tr = cand
            break

    def body(w_ref, g_ref, m_ref, v_ref, go_ref, d_ref, mo_ref, vo_ref):
        gv = g_ref[...]
        go_ref[...] = gv
        mn = ADAM_B1 * m_ref[...] + (1.0 - ADAM_B1) * gv
        vn = ADAM_B2 * v_ref[...] + (1.0 - ADAM_B2) * (gv * gv)
        m_hat = mn / (1.0 - ADAM_B1 ** ADAM_STEP)
        v_hat = vn / (1.0 - ADAM_B2 ** ADAM_STEP)
        d_ref[...] = -ADAM_LR * (m_hat / (jnp.sqrt(v_hat) + ADAM_EPS) + ADAM_WD * w_ref[...])
        mo_ref[...] = mn
        vo_ref[...] = vn

    blk = pl.BlockSpec((tr, C), lambda i: (i, 0))
    return pl.pallas_call(
        body, name=name, grid=(R // tr,), in_specs=[blk] * 4, out_specs=[blk] * 4,
        out_shape=[jax.ShapeDtypeStruct((R, C), F32)] * 4,
        compiler_params=_params("arbitrary"),
    )(w, g, m, v)


def _pack_small(D, meta, n1, nm, n3, nf, gc, ga, bf, cw):
    def row(a):
        a = a.reshape(-1, a.shape[-1])
        return jnp.pad(a, ((0, 0), (0, D - a.shape[-1])))
    rows = [row(meta), row(n1), row(nm), row(n3), row(nf), row(jnp.concatenate([gc, ga], axis=-1)), row(bf), row(cw)]
    slab = jnp.concatenate(rows, axis=0)
    return jnp.pad(slab, ((0, SMALL_ROWS - slab.shape[0]), (0, 0)))


def _unpack_small(slab, like):
    meta, n1, nm, n3, nf, gc, ga, bf, cw = like
    nmeta, mc = meta.shape
    out = [slab[:nmeta, :mc].reshape(meta.shape)]
    r = nmeta
    for a in (n1, nm, n3, nf):
        out.append(slab[r, :a.shape[-1]].reshape(a.shape))
        r += 1
    cd = gc.shape[-1]
    out.append(slab[r, :cd].reshape(gc.shape))
    out.append(slab[r, cd:cd + ga.shape[-1]].reshape(ga.shape))
    r += 1
    out.append(slab[r, :bf.shape[-1]].reshape(bf.shape))
    r += 1
    out.append(slab[r:r + 3, :cw.shape[-1]].reshape(cw.shape))
    return out


def kernel(x, meta_tokens, ffn1_norm, ffn1_w_gu, ffn1_w_down, mix_norm, w_in, conv_w, b_f, out_norm_conv, out_norm_attn, w_out, ffn2_norm, ffn2_w_gu, ffn2_w_down, final_norm, loss_target, m_meta_tokens, m_ffn1_norm, m_ffn1_w_gu, m_ffn1_w_down, m_mix_norm, m_w_in, m_conv_w, m_b_f, m_out_norm_conv, m_out_norm_attn, m_w_out, m_ffn2_norm, m_ffn2_w_gu, m_ffn2_w_down, m_final_norm, v_meta_tokens, v_ffn1_norm, v_ffn1_w_gu, v_ffn1_w_down, v_mix_norm, v_w_in, v_conv_w, v_b_f, v_out_norm_conv, v_out_norm_attn, v_w_out, v_ffn2_norm, v_ffn2_w_gu, v_ffn2_w_down, v_final_norm):
    B, S, D = x.shape
    L = S + N_META
    T = B * L
    tm = L // 3
    assert tm * 3 == L and tm % HALO == 0
    guc = ffn1_w_gu.shape[-1]
    ff = N_SHARD * guc // 2
    H = b_f.shape[-1]
    AD = H * HEAD_DIM
    CD = conv_w.shape[-1] * N_SHARD
    assert CD == AD and CD + AD == D and CD % LANES == 0
    n_main = 3 * CD + 3 * AD
    ins = w_in.shape[-1]

    xi, yi, ci = lax.axis_index("x"), lax.axis_index("y"), lax.axis_index("c")
    chip = 2 * xi + yi

    small_shard = jnp.zeros((2 * HALO, meta_tokens.shape[-1]), F32)
    small_shard = small_shard.at[:N_META].set(meta_tokens)
    small_shard = small_shard.at[N_META:N_META + 3, :conv_w.shape[-1]].set(conv_w[0])
    big = [ffn1_w_gu[0], ffn1_w_down[0], w_in[0], w_out[0], ffn2_w_gu[0], ffn2_w_down[0]]
    wgu1_s, wd1_s, win_s, wout_s, wgu2_s, wd2_s = [w.astype(BF16) for w in big]
    small_g, = _all_gather_shards([small_shard], name="gather_small")
    meta_f = jnp.moveaxis(small_g[:, :N_META], 0, 1).reshape(N_META, D)
    cw_f = jnp.moveaxis(small_g[:, N_META:N_META + 3, :conv_w.shape[-1]], 0, 1).reshape(3, CD)
    cw8 = jnp.pad(cw_f, ((0, 5), (0, 0)))
    bf_p = jnp.pad(b_f, ((0, 0), (0, LANES - H)))
    gid = jnp.arange(CD) // HEAD_DIM
    pmat = jnp.where(gid[:, None] == gid[None, :], 1.0 / HEAD_DIM, 0.0).astype(BF16)

    gu_shape = jax.ShapeDtypeStruct((2, T, ff), BF16)
    gu_w_spec = pl.BlockSpec((None, D, guc), lambda s, i: (s, 0, 0))
    gu_o_spec = pl.BlockSpec((None, tm, guc), lambda s, i: (s // 2, i, s % 2))

    sid = jnp.bitwise_xor(chip, jnp.array([0, 2, 1, 3], jnp.int32)).astype(jnp.int32)
    (h0, n1), wgu1_h = _embed_norm(x, meta_f, ffn1_norm, tm=tm, name="embed_norm",
                                   comm=_gather_stage([wgu1_s], None, ici=(0, 1)))
    gu1, wgu1_h = _ffn_up(n1, wgu1_s[None], sid, None, tm=tm, first=0, count=1, name="ffn1_up_own",
                          comm=_gather_stage([wgu1_s], wgu1_h, ici=(2,), d2d=(0, 1)))
    gu1, out = _ffn_up(n1, wgu1_h[0], sid, gu1, tm=tm, first=1, count=2, name="ffn1_up_near",
                       comm=_join(_gather_stage((), wgu1_h, d2d=(2,)), _gather_ici([wd1_s, wout_s])))
    wgu1, down_w = out[0], out[1:]
    gu1, (wd1, wout_g) = _ffn_up(n1, wgu1, sid, gu1, tm=tm, first=3, count=1, name="ffn1_up_far",
                                 comm=_gather_d2d(down_w))
    wd1 = wd1.reshape(ff, D)
    (h1, n2), win_h = _ffn_down(gu1, wd1, h0, mix_norm, tm=tm, name="ffn1_down", comm=_gather_ici([win_s]))
    win_g, = _run_comm(_gather_d2d(win_h), name="gather_w_in")
    wout_f = wout_g.reshape(D, D)
    win_f = jnp.moveaxis(win_g, 0, 1).reshape(D, N_SHARD * ins)
    win_main = win_f[:, :n_main]
    win_fg = jnp.pad(win_f[:, n_main:], ((0, 0), (0, LANES - H)))

    proj, _ = _matmul_nn(n2, win_main, tm=tm, nb=n_main // (3 * CD),
                         w_spec=pl.BlockSpec((D, 3 * CD), lambda s, i: (0, s)),
                         out_shape=jax.ShapeDtypeStruct((T, n_main), BF16),
                         out_spec=pl.BlockSpec((tm, 3 * CD), lambda s, i: (i, s)), name="mix_in")
    fg, _ = _matmul_nn(n2, win_fg, tm=tm, nb=1, w_spec=pl.BlockSpec((D, LANES), lambda s, i: (0, 0)),
                       out_shape=jax.ShapeDtypeStruct((T, LANES), F32),
                       out_spec=pl.BlockSpec((tm, LANES), lambda s, i: (i, 0)), name="mix_in_fg")
    proj3 = proj.reshape(B, L, n_main)
    fg3 = fg.reshape(B, L, LANES)
    fc = _fcum(fg3, bf_p, ch=tm, name="forget_cumsum")
    fr = fc[:, :, :H].reshape(B, L // tm, tm, H).transpose(0, 1, 3, 2)
    (o, lse), ffn2_w = _attn_fwd(proj3, fr, tq=tm, n_heads=H, name="attn_fwd",
                                 comm=_gather_ici([wgu2_s, wd2_s]))
    (h2, ymix, n3), (wgu2, wd2) = _mix_out(
        proj3, o, cw8, out_norm_conv, out_norm_attn, wout_f, h1.reshape(B, L, D), pmat, ffn2_norm,
        tm=tm, name="mix_out", comm=_gather_d2d(ffn2_w))
    wd2 = wd2.reshape(ff, D)
    h2 = h2.reshape(T, D)
    n3 = n3.reshape(T, D)

    gu2, _ = _matmul_nn(n3, wgu2, tm=tm, nb=N_SHARD, w_spec=gu_w_spec, out_shape=gu_shape, out_spec=gu_o_spec,
                        name="ffn2_up")
    (dh3f, dh3b, d_gf, loss_part), _ = _ffn_down_loss(gu2, wd2, h2, final_norm.reshape(1, D), loss_target,
                                                      tm=tm, name="ffn2_down_loss")

    c_arr = jnp.reshape(ci, (1,)).astype(jnp.int32)
    ks = jnp.arange(N_SHARD - 1, dtype=jnp.int32)
    idx = jnp.concatenate([jnp.stack([chip, ci]).astype(jnp.int32), ks + (ks >= chip).astype(jnp.int32)])

    def pair_sums(grads, got, names):
        return [_pair_sum(g, r, c_arr, name="pair_sum_" + nm) for g, r, nm in zip(grads, got, names)]

    def chip_sums(grads, got, landed, names):
        return [_chip_sum(g, r, l, idx, name="chip_sum_" + nm) for g, r, l, nm in zip(grads, got, landed, names)]

    def dw_up(n, dgu, name, comm=None):
        return _matmul_tn(
            n, dgu, tm=tm, nb=N_SHARD, kb=D, x_spec=pl.BlockSpec((tm, D), lambda s, i: (i, 0)),
            y_spec=pl.BlockSpec((None, tm, guc), lambda s, i: (s // 2, i, s % 2)),
            out_shape=jax.ShapeDtypeStruct((N_SHARD, D, guc), F32),
            out_spec=pl.BlockSpec((None, D, guc), lambda s, i: (s, 0, 0)), name=name, comm=comm)

    (dgu2, d_wd2), _ = _ffn_bwd_act(dh3b, gu2, wd2, tm=tm, guc=guc, name="ffn2_bwd_act")
    (dh2, dh2b, d_g3), _ = _ffn_bwd_in(dgu2, wgu2, h2, ffn2_norm, dh3f, tm=tm, scale=1.0, name="ffn2_bwd_in")
    d_wgu2, _ = dw_up(n3, dgu2, "ffn2_dw_up")
    grads_f2 = [d_wgu2, d_wd2.reshape(N_SHARD, ff // N_SHARD, D)]
    names_f2 = ["wgu2", "wd2"]

    dh2b3 = dh2b.reshape(B, L, D)
    (d_bg, d_cv, d_o, d_gc, d_ga, d_cw), got_f2 = _mix_out_bwd(
        dh2b3, proj3, o, cw8, out_norm_conv, out_norm_attn, wout_f, pmat, tm=tm, name="mix_out_bwd",
        comm=_swap_halves(grads_f2))
    sums_f2 = pair_sums(grads_f2, got_f2, names_f2)
    d_wout, _ = _matmul_tn(
        ymix.reshape(T, D), dh2b, tm=tm, nb=1, kb=D,
        x_spec=pl.BlockSpec((tm, D), lambda s, i: (i, 0)), y_spec=pl.BlockSpec((tm, D), lambda s, i: (i, 0)),
        out_shape=jax.ShapeDtypeStruct((D, D), F32), out_spec=pl.BlockSpec((D, D), lambda s, i: (0, 0)),
        name="dw_out")
    d_cc = _conv_bwd(d_cv, proj3, cw8, tm=tm, name="conv_bwd")
    (d_q, d_k, d_v, d_fk, d_fq), landed_f2 = _attn_bwd(proj3, o, d_o, lse, fr, tq=tm, n_heads=H, name="attn_bwd",
                                                       comm=_scatter_chips(sums_f2))
    halves_f2 = chip_sums(grads_f2, got_f2, landed_f2, names_f2)
    d_fc = d_fq - d_fk
    d_fg, d_bf = _fcum_bwd(d_fc, fg3, bf_p, ch=tm, name="forget_cumsum_bwd")

    parts = [d_bg.reshape(T, CD), d_cc.reshape(T, 2 * CD), d_q.reshape(T, AD), d_k.reshape(T, AD),
             d_v.reshape(T, AD), d_fg.reshape(T, LANES)]
    (dh1, dh1b, d_gm, d_proj), g_f2 = _mix_bwd_in(parts, win_main, win_fg, h1, mix_norm, dh2, tm=tm, scale=0.5,
                                                  name="mix_bwd_in", comm=_share_halves(halves_f2))
    wide = d_proj.shape[1]
    d_win_nat, _ = _matmul_tn(
        n2, d_proj, tm=tm, nb=1, kb=D,
        x_spec=pl.BlockSpec((tm, D), lambda s, i: (i, 0)), y_spec=pl.BlockSpec((tm, wide), lambda s, i: (i, 0)),
        out_shape=jax.ShapeDtypeStruct((D, wide), F32), out_spec=pl.BlockSpec((D, wide), lambda s, i: (0, 0)),
        name="dw_in")
    d_win = jnp.moveaxis(d_win_nat[:, :N_SHARD * ins].reshape(D, N_SHARD, ins), 1, 0)
    grads_mx = [d_win, d_wout.reshape(N_SHARD, D // N_SHARD, D)]
    names_mx = ["win", "wout"]

    (dgu1, d_wd1), got_mx = _ffn_bwd_act(dh1b, gu1, wd1, tm=tm, guc=guc, name="ffn1_bwd_act",
                                         comm=_swap_halves(grads_mx))
    sums_mx = pair_sums(grads_mx, got_mx, names_mx)
    grads_d1 = [d_wd1.reshape(N_SHARD, ff // N_SHARD, D)]
    d_wgu1, out = dw_up(n1, dgu1, "ffn1_dw_up", comm=_join(_scatter_chips(sums_mx), _swap_halves(grads_d1)))
    landed_mx, got_d1 = out[:2], out[2:]
    halves_mx = chip_sums(grads_mx, got_mx, landed_mx, names_mx)
    sums_d1 = pair_sums(grads_d1, got_d1, ["wd1"])
    grads_u1 = [d_wgu1]
    (grad_x, d_meta, d_g1), out = _ffn_bwd_in_first(
        dgu1, wgu1, h0, ffn1_norm, dh1, tm=tm, batch=B, name="ffn1_bwd_in",
        comm=_join(_join(_share_halves(halves_mx), _scatter_chips(sums_d1)), _swap_halves(grads_u1)))
    g_mx, landed_d1, got_u1 = out[:2], out[2:3], out[3:]
    halves_d1 = chip_sums(grads_d1, got_d1, landed_d1, ["wd1"])
    sums_u1 = pair_sums(grads_u1, got_u1, ["wgu1"])
    out = _run_comm(_join(_share_halves(halves_d1), _scatter_chips(sums_u1)), name="scatter_ffn1")
    g_d1, landed_u1 = out[:1], out[1:]
    halves_u1 = chip_sums(grads_u1, got_u1, landed_u1, ["wgu1"])
    g_u1 = _run_comm(_share_halves(halves_u1), name="share_ffn1")
    g_big = [g_u1[0], g_d1[0], g_mx[0], g_mx[1], g_f2[0], g_f2[1]]

    loss_row = jnp.zeros((1, D), F32).at[0, 0].set(loss_part[0, 0])
    slab = _pack_small(D, d_meta, d_g1, d_gm, d_g3, d_gf, d_gc, d_ga, d_bf[:, :H], d_cw[:3])
    slab = slab.at[SMALL_ROWS - 1].set(loss_row[0])
    total = _all_reduce_small(slab, name="reduce_small")
    loss = total[SMALL_ROWS - 1, 0]
    mcols = meta_tokens.shape[-1]
    ccols = conv_w.shape[-1]
    full_like = (jnp.zeros((N_META, D)), ffn1_norm, mix_norm, ffn2_norm, final_norm.reshape(1, D), out_norm_conv,
                 out_norm_attn, b_f, jnp.zeros((1, 3, CD)))
    g_small = _unpack_small(total, full_like)
    g_small[0] = lax.dynamic_slice_in_dim(g_small[0], chip * mcols, mcols, axis=1)
    g_small[8] = lax.dynamic_slice_in_dim(g_small[8], chip * ccols, ccols, axis=2)

    def small_slab(meta, a1, am, a3, af, gc, ga, bf, cw):
        return _pack_small(D, meta, a1, am, a3, af.reshape(1, D), gc, ga, bf, cw[0])

    w_small = small_slab(meta_tokens, ffn1_norm, mix_norm, ffn2_norm, final_norm, out_norm_conv, out_norm_attn, b_f, conv_w)
    m_small = small_slab(m_meta_tokens, m_ffn1_norm, m_mix_norm, m_ffn2_norm, m_final_norm, m_out_norm_conv,
                         m_out_norm_attn, m_b_f, m_conv_w)
    v_small = small_slab(v_meta_tokens, v_ffn1_norm, v_mix_norm, v_ffn2_norm, v_final_norm, v_out_norm_conv,
                         v_out_norm_attn, v_b_f, v_conv_w)
    gs = list(g_small)
    gs[4] = gs[4].reshape(final_norm.shape)
    g_slab = small_slab(gs[0], gs[1], gs[2], gs[3], gs[4], gs[5], gs[6], gs[7], gs[8])
    local_like = (meta_tokens, ffn1_norm, mix_norm, ffn2_norm, final_norm.reshape(1, D), out_norm_conv, out_norm_attn,
                  b_f, conv_w)
    small_out = [_unpack_small(s, local_like)
                 for s in _adamw(w_small, g_slab, m_small, v_small, name="adamw_small")[1:]]
    for lst in small_out:
        lst[4] = lst[4].reshape(final_norm.shape)

    names = ["wgu1", "wd1", "win", "wout", "wgu2", "wd2"]
    w_big = big
    m_big = [m_ffn1_w_gu[0], m_ffn1_w_down[0], m_w_in[0], m_w_out[0], m_ffn2_w_gu[0], m_ffn2_w_down[0]]
    v_big = [v_ffn1_w_gu[0], v_ffn1_w_down[0], v_w_in[0], v_w_out[0], v_ffn2_w_gu[0], v_ffn2_w_down[0]]
    big_out = [_adamw(w, g, m, v, name="adamw_" + nm) for w, g, m, v, nm in zip(w_big, g_big, m_big, v_big, names)]

    def assemble(small, bigs):
        meta, a1, am, a3, af, gc, ga, bf, cw = small
        gu1_, d1_, win_, wout_, gu2_, d2_ = [b[None] for b in bigs]
        return [meta, a1, gu1_, d1_, am, win_, cw, bf, gc, ga, wout_, a3, gu2_, d2_, af]

    gs_out = list(g_small)
    gs_out[4] = gs_out[4].reshape(final_norm.shape)
    grads_out = assemble(gs_out, [b[0] for b in big_out])
    delta_out = assemble(small_out[0], [b[1] for b in big_out])
    m_out = assemble(small_out[1], [b[2] for b in big_out])
    v_out = assemble(small_out[2], [b[3] for b in big_out])
    return (loss, grad_x, *grads_out, *delta_out, *m_out, *v_out)
```

```python
import functools

import jax
import jax.numpy as jnp
from jax import lax
from jax.experimental import pallas as pl
from jax.experimental.pallas import tpu as pltpu

F32 = jnp.float32
BF16 = jnp.bfloat16

EPS = 1e-6
N_META = 16
HEAD_DIM = 64
N_SHARD = 4
N_DEV = 8
HALO = 16
LANES = 128
SMALL_ROWS = 32
VMEM_LIMIT_V7X = 56 * 1024 * 1024
NEG = -1e30
ATTN_BANDS = 2

ADAM_LR = 0.001
ADAM_B1 = 0.9
ADAM_B2 = 0.999
ADAM_EPS = 1e-08
ADAM_WD = 0.01
ADAM_STEP = 10

MESH = pl.DeviceIdType.MESH
ANY = pl.BlockSpec(memory_space=pl.ANY)
NT_DIMS = (((1,), (1,)), ((), ()))
TN_DIMS = (((0,), (0,)), ((), ()))


def _params(*sem):
    return pltpu.CompilerParams(dimension_semantics=sem, vmem_limit_bytes=VMEM_LIMIT_V7X)


class _Comm:
    def __init__(self, ins, out_shapes, sems, start, finish, aliases=None):
        self.ins, self.out_shapes, self.sems = list(ins), list(out_shapes), list(sems)
        self.start, self.finish, self.aliases = start, finish, dict(aliases or {})


def _join(a, b):
    ni, no, ns = len(a.ins), len(a.out_shapes), len(a.sems)

    def start(ins, outs, sems):
        a.start(ins[:ni], outs[:no], sems[:ns])
        b.start(ins[ni:], outs[no:], sems[ns:])

    def finish(ins, outs, sems):
        a.finish(ins[:ni], outs[:no], sems[:ns])
        b.finish(ins[ni:], outs[no:], sems[ns:])

    aliases = dict(a.aliases)
    aliases.update({ni + i: no + j for i, j in b.aliases.items()})
    return _Comm(a.ins + b.ins, a.out_shapes + b.out_shapes, a.sems + b.sems, start, finish, aliases)


def _launch(body, *, name, grid, in_specs, out_specs, out_shape, args, scratch_shapes=(), comm=None, prefetch=(),
            aliases=None):
    single = not isinstance(out_shape, (list, tuple))
    out_specs = [out_specs] if single else list(out_specs)
    out_shape = [out_shape] if single else list(out_shape)
    in_specs, scratch_shapes, prefetch = list(in_specs), list(scratch_shapes), list(prefetch)
    params = _params(*(("arbitrary",) * len(grid)))
    n_pf, n_in, n_out, n_scr = len(prefetch), len(in_specs), len(out_specs), len(scratch_shapes)
    c_ins = comm.ins if comm else []
    c_shapes = comm.out_shapes if comm else []
    c_sems = comm.sems if comm else []
    c_in, c_out = len(c_ins), len(c_shapes)

    def carrier(*refs):
        p = 0
        pf = refs[p:p + n_pf]; p += n_pf
        a = refs[p:p + n_in]; p += n_in
        ci = refs[p:p + c_in]; p += c_in
        o = refs[p:p + n_out]; p += n_out
        co = refs[p:p + c_out]; p += c_out
        s = refs[p:p + n_scr]; p += n_scr
        cs = refs[p:]
        if comm:
            first = functools.reduce(lambda u, v: u & v, [pl.program_id(k) == 0 for k in range(len(grid))])

            @pl.when(first)
            def _():
                comm.start(ci, co, cs)

        body(*pf, *a, *o, *s)

        if comm:
            last = functools.reduce(lambda u, v: u & v, [pl.program_id(k) == grid[k] - 1 for k in range(len(grid))])

            @pl.when(last)
            def _():
                comm.finish(ci, co, cs)

    io_aliases = {n_pf + i: j for i, j in (aliases or {}).items()}
    if comm:
        io_aliases.update({n_pf + n_in + i: n_out + j for i, j in comm.aliases.items()})
    all_in, all_out = in_specs + [ANY] * c_in, out_specs + [ANY] * c_out
    all_scratch = scratch_shapes + [pltpu.SemaphoreType.DMA((k,)) for k in c_sems]
    if n_pf:
        spec = dict(grid_spec=pltpu.PrefetchScalarGridSpec(
            num_scalar_prefetch=n_pf, grid=grid, in_specs=all_in, out_specs=all_out, scratch_shapes=all_scratch))
    else:
        spec = dict(grid=grid, in_specs=all_in, out_specs=all_out, scratch_shapes=all_scratch)
    res = pl.pallas_call(carrier, name=name, out_shape=out_shape + c_shapes, input_output_aliases=io_aliases,
                         compiler_params=params, **spec)(*prefetch, *args, *c_ins)
    main = list(res[:n_out])
    return (main[0] if single else main), (list(res[n_out:]) if comm else None)


def _run_comm(comm, *, name):
    c_in, c_out = len(comm.ins), len(comm.out_shapes)

    def body(*refs):
        ci, co, cs = refs[:c_in], refs[c_in:c_in + c_out], refs[c_in + c_out:]
        comm.start(ci, co, cs)
        comm.finish(ci, co, cs)

    return list(pl.pallas_call(
        body, name=name, in_specs=[ANY] * c_in, out_specs=[ANY] * c_out, out_shape=comm.out_shapes,
        scratch_shapes=[pltpu.SemaphoreType.DMA((k,)) for k in comm.sems],
        input_output_aliases=comm.aliases)(*comm.ins))


def _chunks(width, step=512):
    out, c0 = [], 0
    while c0 < width:
        cw = min(step, width - c0)
        out.append((c0, cw))
        c0 += cw
    return out


def _split2(v):
    hi = v.astype(BF16)
    lo = (v - hi.astype(F32)).astype(BF16)
    return hi, lo


def _split3(v):
    hi = v.astype(BF16)
    r = v - hi.astype(F32)
    mid = r.astype(BF16)
    lo = (r - mid.astype(F32)).astype(BF16)
    return hi, mid, lo


def _dot(a, b):
    return jnp.dot(a, b, preferred_element_type=F32)


def _dot_nt(a, b):
    return lax.dot_general(a, b, NT_DIMS, preferred_element_type=F32)


def _dot_tn(a, b):
    return lax.dot_general(a, b, TN_DIMS, preferred_element_type=F32)


def _silu_mul(g, u):
    return g * jax.nn.sigmoid(g) * u


def _rms_bwd(dn, h, gain, dres):
    r = lax.rsqrt(jnp.mean(h * h, axis=-1, keepdims=True) + EPS)
    y = h * r
    dgain = jnp.sum(dn * y, axis=0, keepdims=True)
    dy = dn * gain
    dh = dres + r * (dy - y * jnp.mean(dy * y, axis=-1, keepdims=True))
    return dh, dgain


def _group_mean(v, p):
    hi, lo = _split2(v)
    return _dot(hi, p) + _dot(lo, p)


def _row_of(a, k):
    rows = lax.broadcasted_iota(jnp.int32, a.shape, 0)
    return jnp.sum(jnp.where(rows == k, a, 0.0), axis=0, keepdims=True)


def _causal_conv(u, prev, w):
    rows = lax.broadcasted_iota(jnp.int32, u.shape, 0)
    p1 = _row_of(prev, HALO - 1)
    p2 = _row_of(prev, HALO - 2)
    u1 = jnp.where(rows == 0, p1, pltpu.roll(u, 1, 0))
    u2 = jnp.where(rows == 0, p2, jnp.where(rows == 1, p1, pltpu.roll(u, 2, 0)))
    return w[2:3, :] * u + w[1:2, :] * u1 + w[0:1, :] * u2, u1, u2


def _rms(x, gain):
    return (x * lax.rsqrt(jnp.mean(x * x, axis=-1, keepdims=True) + EPS) * gain).astype(BF16)


def _embed_norm(x, meta, g, *, tm, name, comm=None):
    B, S, D = x.shape
    L = S + N_META
    per_seq = L // tm
    nt = B * per_seq
    body_rows = tm - N_META

    def body(meta_ref, g_ref, x_hbm, h_ref, n_ref, buf, sems):
        i = pl.program_id(0)

        def fetch(k, fn):
            slot, b, t = k % 2, k // per_seq, k % per_seq

            @pl.when(t == 0)
            def _():
                fn(pltpu.make_async_copy(x_hbm.at[b, pl.ds(0, body_rows)],
                                         buf.at[slot, pl.ds(N_META, body_rows)], sems.at[slot]))

            @pl.when(t != 0)
            def _():
                fn(pltpu.make_async_copy(x_hbm.at[b, pl.ds(pl.multiple_of(t * tm - N_META, 8), tm)],
                                         buf.at[slot], sems.at[slot]))

        @pl.when(i == 0)
        def _():
            fetch(i, lambda cp: cp.start())

        @pl.when(i + 1 < nt)
        def _():
            fetch(i + 1, lambda cp: cp.start())

        fetch(i, lambda cp: cp.wait())
        slot = i % 2

        @pl.when(i % per_seq == 0)
        def _():
            buf[slot, 0:N_META, :] = meta_ref[...]

        hv = buf[slot]
        h_ref[...] = hv
        n_ref[...] = _rms(hv, g_ref[...])

    row = pl.BlockSpec((tm, D), lambda i: (i, 0))
    return _launch(
        body, name=name, grid=(nt,),
        in_specs=[pl.BlockSpec((N_META, D), lambda i: (0, 0)), pl.BlockSpec((1, D), lambda i: (0, 0)), ANY],
        out_specs=[row, row],
        out_shape=[jax.ShapeDtypeStruct((B * L, D), F32), jax.ShapeDtypeStruct((B * L, D), BF16)],
        scratch_shapes=[pltpu.VMEM((2, tm, D), F32), pltpu.SemaphoreType.DMA((2,))],
        args=(meta, g, x), comm=comm)


def _ffn_up(n, wgu, sid, gu_prev, *, tm, first, count, name, comm=None):
    T, D = n.shape
    ns, _, guc = wgu.shape
    ff = N_SHARD * guc // 2

    def body(sid_ref, x_ref, w_ref, *rest):
        rest[-1][...] = _dot(x_ref[...], w_ref[...]).astype(BF16)

    where = lambda s, sid: sid[first + s]
    w_at = (lambda s, sid: 0) if ns == 1 else where
    return _launch(
        body, name=name, grid=(count, T // tm), prefetch=(sid,),
        in_specs=[pl.BlockSpec((tm, D), lambda s, i, sid: (i, 0)),
                  pl.BlockSpec((None, D, guc), lambda s, i, sid: (w_at(s, sid), 0, 0))]
                 + ([] if gu_prev is None else [ANY]),
        out_specs=pl.BlockSpec((None, tm, guc), lambda s, i, sid: (where(s, sid) // 2, i, where(s, sid) % 2)),
        out_shape=jax.ShapeDtypeStruct((2, T, ff), BF16),
        args=(n, wgu) + (() if gu_prev is None else (gu_prev,)),
        aliases=None if gu_prev is None else {2: 0}, comm=comm)


def _matmul_nn(x, w, *, tm, nb, w_spec, out_shape, out_spec, name, comm=None):
    T, K = x.shape

    def body(x_ref, w_ref, o_ref):
        o_ref[...] = _dot(x_ref[...], w_ref[...]).astype(o_ref.dtype)

    return _launch(
        body, name=name, grid=(nb, T // tm),
        in_specs=[pl.BlockSpec((tm, K), lambda s, i: (i, 0)), w_spec],
        out_specs=out_spec, out_shape=out_shape, args=(x, w), comm=comm)


def _ffn_down(gu, wd, h, next_gain, *, tm, name, comm=None):
    _, T, ff = gu.shape
    D = h.shape[1]
    chunks = _chunks(ff)

    def body(g_ref, u_ref, wd_hbm, h_ref, ng_ref, o_ref, n_ref, wd_v, sem):
        @pl.when(pl.program_id(0) == 0)
        def _():
            cp = pltpu.make_async_copy(wd_hbm, wd_v, sem)
            cp.start()
            cp.wait()

        def act(c0, cw):
            return _silu_mul(g_ref[:, c0:c0 + cw].astype(F32), u_ref[:, c0:c0 + cw].astype(F32)).astype(BF16)

        acc = jnp.zeros((tm, D), F32)
        nxt = act(*chunks[0])
        for k, (c0, cw) in enumerate(chunks):
            a = nxt
            if k + 1 < len(chunks):
                nxt = act(*chunks[k + 1])
            acc = acc + _dot(a, wd_v[c0:c0 + cw, :])
        out = h_ref[...] + 0.5 * acc
        o_ref[...] = out
        n_ref[...] = _rms(out, ng_ref[...])

    return _launch(
        body, name=name, grid=(T // tm,),
        in_specs=[pl.BlockSpec((None, tm, ff), lambda i: (0, i, 0)),
                  pl.BlockSpec((None, tm, ff), lambda i: (1, i, 0)),
                  ANY,
                  pl.BlockSpec((tm, D), lambda i: (i, 0)),
                  pl.BlockSpec((1, D), lambda i: (0, 0))],
        out_specs=[pl.BlockSpec((tm, D), lambda i: (i, 0)), pl.BlockSpec((tm, D), lambda i: (i, 0))],
        out_shape=[jax.ShapeDtypeStruct((T, D), F32), jax.ShapeDtypeStruct((T, D), BF16)],
        scratch_shapes=[pltpu.VMEM((ff, D), BF16), pltpu.SemaphoreType.DMA],
        args=(gu, gu, wd, h, next_gain), comm=comm)


def _ffn_down_loss(gu, wd, h, gf, tgt, *, tm, name, comm=None):
    _, T, ff = gu.shape
    D = h.shape[1]
    B, S, _ = tgt.shape
    per_seq = (S + N_META) // tm
    body_rows = tm - N_META
    chunks = _chunks(ff)

    def body(g_ref, u_ref, wd_hbm, h_ref, gf_ref, tgt_hbm, dh_ref, dhb_ref, dg_ref, loss_ref, wd_v, tg_v, sem, tsem):
        i = pl.program_id(0)
        b, t = i // per_seq, i % per_seq

        @pl.when(i == 0)
        def _():
            cp = pltpu.make_async_copy(wd_hbm, wd_v, sem)
            cp.start()
            cp.wait()
            dg_ref[...] = jnp.zeros_like(dg_ref)
            loss_ref[...] = jnp.zeros_like(loss_ref)
            tg_v[0:N_META, :] = jnp.zeros((N_META, D), F32)

        def fetch(fn):
            @pl.when(t == 0)
            def _():
                fn(pltpu.make_async_copy(tgt_hbm.at[b, pl.ds(0, body_rows)], tg_v.at[pl.ds(N_META, body_rows)], tsem))

            @pl.when(t != 0)
            def _():
                fn(pltpu.make_async_copy(tgt_hbm.at[b, pl.ds(pl.multiple_of(t * tm - N_META, 8), tm)], tg_v, tsem))

        fetch(lambda cp: cp.start())
        def act(c0, cw):
            return _silu_mul(g_ref[:, c0:c0 + cw].astype(F32), u_ref[:, c0:c0 + cw].astype(F32)).astype(BF16)

        acc = jnp.zeros((tm, D), F32)
        nxt = act(*chunks[0])
        for k, (c0, cw) in enumerate(chunks):
            a = nxt
            if k + 1 < len(chunks):
                nxt = act(*chunks[k + 1])
            acc = acc + _dot(a, wd_v[c0:c0 + cw, :])
        x = h_ref[...] + 0.5 * acc
        fetch(lambda cp: cp.wait())

        gain = gf_ref[...]
        r = lax.rsqrt(jnp.mean(x * x, axis=-1, keepdims=True) + EPS)
        y = x * r
        pos = t * tm + lax.broadcasted_iota(jnp.int32, (tm, 1), 0)
        err = jnp.where(pos >= N_META, y * gain - tg_v[...], 0.0)
        loss_ref[...] += 0.5 * jnp.sum(jnp.mean(err * err, axis=-1, keepdims=True))
        dout = err / D
        dg_ref[...] += jnp.sum(dout * y, axis=0, keepdims=True)
        dy = dout * gain
        dh = r * (dy - y * jnp.mean(dy * y, axis=-1, keepdims=True))
        dh_ref[...] = dh
        dhb_ref[...] = (0.5 * dh).astype(BF16)

    row = pl.BlockSpec((tm, D), lambda i: (i, 0))
    const = lambda i: (0, 0)
    return _launch(
        body, name=name, grid=(T // tm,),
        in_specs=[pl.BlockSpec((None, tm, ff), lambda i: (0, i, 0)),
                  pl.BlockSpec((None, tm, ff), lambda i: (1, i, 0)),
                  ANY, row, pl.BlockSpec((1, D), const), ANY],
        out_specs=[row, row, pl.BlockSpec((1, D), const), pl.BlockSpec((1, LANES), const)],
        out_shape=[jax.ShapeDtypeStruct((T, D), F32), jax.ShapeDtypeStruct((T, D), BF16),
                   jax.ShapeDtypeStruct((1, D), F32), jax.ShapeDtypeStruct((1, LANES), F32)],
        scratch_shapes=[pltpu.VMEM((ff, D), BF16), pltpu.VMEM((tm, D), F32), pltpu.SemaphoreType.DMA,
                        pltpu.SemaphoreType.DMA],
        args=(gu, gu, wd, h, gf, tgt), comm=comm)


def _ffn_bwd_act(df, gu, wd, *, tm, guc, name, comm=None):
    _, T, ff = gu.shape
    D = df.shape[1]
    nj = ff // guc
    chunks = _chunks(guc)

    def body(df_ref, g_ref, u_ref, wd_ref, o_ref, dwd_ref):
        @pl.when(pl.program_id(1) == 0)
        def _():
            dwd_ref[...] = jnp.zeros_like(dwd_ref)

        dfv = df_ref[...]
        nxt = _dot_nt(dfv, wd_ref[chunks[0][0]:chunks[0][0] + chunks[0][1], :])
        for k, (c0, cw) in enumerate(chunks):
            da = nxt
            if k + 1 < len(chunks):
                n0, nw = chunks[k + 1]
                nxt = _dot_nt(dfv, wd_ref[n0:n0 + nw, :])
            g = g_ref[:, c0:c0 + cw].astype(F32)
            u = u_ref[:, c0:c0 + cw].astype(F32)
            sg = jax.nn.sigmoid(g)
            silu = g * sg
            o_ref[0, :, c0:c0 + cw] = (da * u * (sg * (1.0 + g * (1.0 - sg)))).astype(BF16)
            o_ref[1, :, c0:c0 + cw] = (da * silu).astype(BF16)
            dwd_ref[c0:c0 + cw, :] += _dot_tn((silu * u).astype(BF16), dfv)

    return _launch(
        body, name=name, grid=(nj, T // tm),
        in_specs=[pl.BlockSpec((tm, D), lambda j, i: (i, 0)),
                  pl.BlockSpec((None, tm, guc), lambda j, i: (0, i, j)),
                  pl.BlockSpec((None, tm, guc), lambda j, i: (1, i, j)),
                  pl.BlockSpec((guc, D), lambda j, i: (j, 0))],
        out_specs=[pl.BlockSpec((2, tm, guc), lambda j, i: (0, i, j)), pl.BlockSpec((guc, D), lambda j, i: (j, 0))],
        out_shape=[jax.ShapeDtypeStruct((2, T, ff), BF16), jax.ShapeDtypeStruct((ff, D), F32)],
        args=(df, gu, gu, wd), comm=comm)


def _ffn_bwd_in(dgu, wgu, h, g, dres, *, tm, scale, name, comm=None):
    _, T, ff = dgu.shape
    ns, D, guc = wgu.shape
    nj = ff // guc
    edges = _band_edges(tm)

    def body(dgu_ref, w_hbm, h_ref, g_ref, dres_ref, dh_ref, dhb_ref, dg_ref, w_v, acc, sem):
        i, j = pl.program_id(0), pl.program_id(1)

        @pl.when((i == 0) & (j == 0))
        def _():
            cp = pltpu.make_async_copy(w_hbm, w_v, sem)
            cp.start()
            cp.wait()
            dg_ref[...] = jnp.zeros_like(dg_ref)

        def dots(rows):
            return _dot_nt(dgu_ref[0, rows, :], w_v[j]) + _dot_nt(dgu_ref[1, rows, :], w_v[nj + j])

        @pl.when(j < nj - 1)
        def _():
            part = dots(slice(None))

            @pl.when(j == 0)
            def _():
                acc[...] = part

            @pl.when(j > 0)
            def _():
                acc[...] += part

        @pl.when(j == nj - 1)
        def _():
            bands = [slice(r0, r1) for r0, r1 in zip(edges[:-1], edges[1:])]
            nxt = dots(bands[0])
            for b, rows in enumerate(bands):
                dn = nxt if nj == 1 else acc[rows, :] + nxt
                if b + 1 < len(bands):
                    nxt = dots(bands[b + 1])
                dh, dgain = _rms_bwd(dn, h_ref[rows, :], g_ref[...], dres_ref[rows, :])
                dh_ref[rows, :] = dh
                dhb_ref[rows, :] = (scale * dh).astype(BF16)
                dg_ref[...] += dgain

    return _launch(
        body, name=name, grid=(T // tm, nj),
        in_specs=[pl.BlockSpec((2, tm, guc), lambda i, j: (0, i, j)),
                  ANY,
                  pl.BlockSpec((tm, D), lambda i, j: (i, 0)),
                  pl.BlockSpec((1, D), lambda i, j: (0, 0)),
                  pl.BlockSpec((tm, D), lambda i, j: (i, 0))],
        out_specs=[pl.BlockSpec((tm, D), lambda i, j: (i, 0)),
                   pl.BlockSpec((tm, D), lambda i, j: (i, 0)),
                   pl.BlockSpec((1, D), lambda i, j: (0, 0))],
        out_shape=[jax.ShapeDtypeStruct((T, D), F32), jax.ShapeDtypeStruct((T, D), BF16),
                   jax.ShapeDtypeStruct((1, D), F32)],
        scratch_shapes=[pltpu.VMEM((ns, D, guc), BF16), pltpu.VMEM((tm, D), F32), pltpu.SemaphoreType.DMA],
        args=(dgu, wgu, h, g, dres), comm=comm)


def _ffn_bwd_in_first(dgu, wgu, h, g, dres, *, tm, batch, name, comm=None):
    _, T, ff = dgu.shape
    ns, D, guc = wgu.shape
    nj = ff // guc
    nt = T // tm
    L = T // batch
    per_seq = L // tm
    body_rows = tm - N_META
    edges = _band_edges(tm)

    def body(dgu_ref, w_hbm, h_ref, g_ref, dres_ref, dx_hbm, dmeta_ref, dg_ref, w_v, acc, dh_v, sem, osem):
        i, j = pl.program_id(0), pl.program_id(1)

        @pl.when((i == 0) & (j == 0))
        def _():
            cp = pltpu.make_async_copy(w_hbm, w_v, sem)
            cp.start()
            cp.wait()
            dg_ref[...] = jnp.zeros_like(dg_ref)
            dmeta_ref[...] = jnp.zeros_like(dmeta_ref)

        def dots(rows):
            return _dot_nt(dgu_ref[0, rows, :], w_v[j]) + _dot_nt(dgu_ref[1, rows, :], w_v[nj + j])

        @pl.when(j < nj - 1)
        def _():
            part = dots(slice(None))

            @pl.when(j == 0)
            def _():
                acc[...] = part

            @pl.when(j > 0)
            def _():
                acc[...] += part

        def head_copy(b):
            return pltpu.make_async_copy(dh_v.at[pl.ds(N_META, body_rows)], dx_hbm.at[b, pl.ds(0, body_rows)], osem)

        def tail_copy(b, t):
            return pltpu.make_async_copy(dh_v, dx_hbm.at[b, pl.ds(pl.multiple_of(t * tm - N_META, 8), tm)], osem)

        def on_tile(k, head_fn, tail_fn):
            @pl.when(k % per_seq == 0)
            def _():
                head_fn(head_copy(k // per_seq))

            @pl.when(k % per_seq != 0)
            def _():
                tail_fn(tail_copy(k // per_seq, k % per_seq))

        @pl.when(j == nj - 1)
        def _():
            @pl.when(i > 0)
            def _():
                on_tile(i - 1, lambda cp: cp.wait(), lambda cp: cp.wait())

            bands = [slice(r0, r1) for r0, r1 in zip(edges[:-1], edges[1:])]
            nxt = dots(bands[0])
            for b, rows in enumerate(bands):
                dn = nxt if nj == 1 else acc[rows, :] + nxt
                if b + 1 < len(bands):
                    nxt = dots(bands[b + 1])
                dh, dgain = _rms_bwd(dn, h_ref[rows, :], g_ref[...], dres_ref[rows, :])
                dg_ref[...] += dgain
                dh_v[rows, :] = dh
                if b == 0:
                    @pl.when(i % per_seq == 0)
                    def _():
                        dmeta_ref[...] += dh[0:N_META, :]

            on_tile(i, lambda cp: cp.start(), lambda cp: cp.start())

            @pl.when(i == nt - 1)
            def _():
                on_tile(i, lambda cp: cp.wait(), lambda cp: cp.wait())

    return _launch(
        body, name=name, grid=(nt, nj),
        in_specs=[pl.BlockSpec((2, tm, guc), lambda i, j: (0, i, j)),
                  ANY,
                  pl.BlockSpec((tm, D), lambda i, j: (i, 0)),
                  pl.BlockSpec((1, D), lambda i, j: (0, 0)),
                  pl.BlockSpec((tm, D), lambda i, j: (i, 0))],
        out_specs=[ANY, pl.BlockSpec((N_META, D), lambda i, j: (0, 0)), pl.BlockSpec((1, D), lambda i, j: (0, 0))],
        out_shape=[jax.ShapeDtypeStruct((batch, L - N_META, D), F32), jax.ShapeDtypeStruct((N_META, D), F32),
                   jax.ShapeDtypeStruct((1, D), F32)],
        scratch_shapes=[pltpu.VMEM((ns, D, guc), BF16), pltpu.VMEM((tm, D), F32), pltpu.VMEM((tm, D), F32),
                        pltpu.SemaphoreType.DMA, pltpu.SemaphoreType.DMA],
        args=(dgu, wgu, h, g, dres), comm=comm)


def _mix_bwd_in(parts, w_main, w_fg, h, g, dres, *, tm, scale, name, comm=None):
    T, D = h.shape
    widths = [p.shape[1] for p in parts]
    offs = [sum(widths[:k]) for k in range(len(widths))]
    npart = len(parts)
    wide = sum(widths)
    edges = _band_edges(tm)

    def body(*refs):
        p_refs = refs[:npart]
        wm_ref, wf_ref, h_ref, g_ref, dres_ref, dh_ref, dhb_ref, dg_ref, all_ref = refs[npart:]

        @pl.when(pl.program_id(0) == 0)
        def _():
            dg_ref[...] = jnp.zeros_like(dg_ref)

        for p_ref, off, wd_ in zip(p_refs, offs, widths):
            for c0, cw in _chunks(wd_):
                all_ref[:, off + c0:off + c0 + cw] = p_ref[:, c0:c0 + cw].astype(BF16)
        n_main = offs[-1]

        def dots(rows):
            return _dot_nt(all_ref[rows, :n_main], wm_ref[...]) + _dot_nt(all_ref[rows, n_main:], wf_ref[...])

        bands = [slice(r0, r1) for r0, r1 in zip(edges[:-1], edges[1:])]
        nxt = dots(bands[0])
        for b, rows in enumerate(bands):
            dn = nxt
            if b + 1 < len(bands):
                nxt = dots(bands[b + 1])
            dh, dgain = _rms_bwd(dn, h_ref[rows, :], g_ref[...], dres_ref[rows, :])
            dh_ref[rows, :] = dh
            dhb_ref[rows, :] = (scale * dh).astype(BF16)
            dg_ref[...] += dgain

    row = lambda i: (i, 0)
    const = lambda i: (0, 0)
    return _launch(
        body, name=name, grid=(T // tm,),
        in_specs=[pl.BlockSpec((tm, p.shape[1]), row) for p in parts]
                 + [pl.BlockSpec(w_main.shape, const), pl.BlockSpec(w_fg.shape, const),
                    pl.BlockSpec((tm, D), row), pl.BlockSpec((1, D), const), pl.BlockSpec((tm, D), row)],
        out_specs=[pl.BlockSpec((tm, D), row), pl.BlockSpec((tm, D), row), pl.BlockSpec((1, D), const),
                   pl.BlockSpec((tm, wide), row)],
        out_shape=[jax.ShapeDtypeStruct((T, D), F32), jax.ShapeDtypeStruct((T, D), BF16),
                   jax.ShapeDtypeStruct((1, D), F32), jax.ShapeDtypeStruct((T, wide), BF16)],
        args=(*parts, w_main, w_fg, h, g, dres), comm=comm)


def _matmul_tn(x, y, *, tm, nb, x_spec, y_spec, out_shape, out_spec, kb, name, comm=None):
    T = y.shape[-2]
    chunks = _chunks(kb)

    def body(x_ref, y_ref, o_ref):
        @pl.when(pl.program_id(1) == 0)
        def _():
            o_ref[...] = jnp.zeros_like(o_ref)

        yv = y_ref[...].astype(BF16)
        nxt = _dot_tn(x_ref[:, chunks[0][0]:chunks[0][0] + chunks[0][1]], yv)
        for k, (c0, cw) in enumerate(chunks):
            cur = nxt
            if k + 1 < len(chunks):
                n0, nw = chunks[k + 1]
                nxt = _dot_tn(x_ref[:, n0:n0 + nw], yv)
            o_ref[c0:c0 + cw, :] += cur

    return _launch(
        body, name=name, grid=(nb, T // tm),
        in_specs=[x_spec, y_spec], out_specs=out_spec, out_shape=out_shape, args=(x, y), comm=comm)


def _tri(n, lower):
    r = lax.broadcasted_iota(jnp.int32, (n, n), 0)
    c = lax.broadcasted_iota(jnp.int32, (n, n), 1)
    return jnp.where((r >= c) if lower else (r <= c), 1.0, 0.0).astype(BF16)


def _tri_dot(tri, v):
    hi, mid, lo = _split3(v)
    return _dot(tri, hi) + _dot(tri, mid) + _dot(tri, lo)


def _fcum(fg, bf, *, ch, name):
    B, L, W = fg.shape
    nch = L // ch

    def body(fg_ref, bf_ref, f_ref):
        tri = _tri(ch, True)
        carry = jnp.zeros((1, W), F32)
        for c in range(nch):
            x = fg_ref[c * ch:(c + 1) * ch, :] + bf_ref[...]
            lf = jnp.minimum(x, 0.0) - jnp.log(1.0 + jnp.exp(-jnp.abs(x)))
            f_ref[c * ch:(c + 1) * ch, :] = _tri_dot(tri, lf) + carry
            carry = carry + jnp.sum(lf, axis=0, keepdims=True)

    return pl.pallas_call(
        body, name=name, grid=(B,),
        in_specs=[pl.BlockSpec((None, L, W), lambda b: (b, 0, 0)), pl.BlockSpec((1, W), lambda b: (0, 0))],
        out_specs=pl.BlockSpec((None, L, W), lambda b: (b, 0, 0)),
        out_shape=jax.ShapeDtypeStruct((B, L, W), F32),
        compiler_params=_params("arbitrary"),
    )(fg, bf)


def _fcum_bwd(dF, fg, bf, *, ch, name):
    B, L, W = fg.shape
    nch = L // ch

    def body(df_ref, fg_ref, bf_ref, dfg_ref, db_ref):
        @pl.when(pl.program_id(0) == 0)
        def _():
            db_ref[...] = jnp.zeros_like(db_ref)

        tri = _tri(ch, False)
        carry = jnp.zeros((1, W), F32)
        dbs = jnp.zeros((1, W), F32)
        for c in reversed(range(nch)):
            d = df_ref[c * ch:(c + 1) * ch, :]
            dlf = _tri_dot(tri, d) + carry
            carry = carry + jnp.sum(d, axis=0, keepdims=True)
            x = fg_ref[c * ch:(c + 1) * ch, :] + bf_ref[...]
            dfg = dlf * jax.nn.sigmoid(-x)
            dfg_ref[c * ch:(c + 1) * ch, :] = dfg.astype(BF16)
            dbs = dbs + jnp.sum(dfg, axis=0, keepdims=True)
        db_ref[...] += dbs

    blk = pl.BlockSpec((None, L, W), lambda b: (b, 0, 0))
    return pl.pallas_call(
        body, name=name, grid=(B,),
        in_specs=[blk, blk, pl.BlockSpec((1, W), lambda b: (0, 0))],
        out_specs=[blk, pl.BlockSpec((1, W), lambda b: (0, 0))],
        out_shape=[jax.ShapeDtypeStruct((B, L, W), BF16), jax.ShapeDtypeStruct((1, W), F32)],
        compiler_params=_params("arbitrary"),
    )(dF, fg, bf)


def _band_edges(tq):
    return sorted({min(tq, (k * tq // ATTN_BANDS + HALO - 1) // HALO * HALO) for k in range(ATTN_BANDS + 1)})


def _pair(h):
    return slice((h // 2) * 2 * HEAD_DIM, (h // 2 + 1) * 2 * HEAD_DIM)


def _own_lanes(a, h):
    low = lax.broadcasted_iota(jnp.int32, a.shape, 1) < HEAD_DIM
    return jnp.where(low if h % 2 == 0 else jnp.logical_not(low), a, jnp.zeros_like(a))


def _sum_lane(h):
    return HEAD_DIM if h % 2 == 0 else 0


def _own_lanes_and_ones(a, h):
    lane = lax.broadcasted_iota(jnp.int32, a.shape, 1)
    low = lane < HEAD_DIM
    return jnp.where(low if h % 2 == 0 else jnp.logical_not(low), a,
                     jnp.where(lane == _sum_lane(h), jnp.ones_like(a), jnp.zeros_like(a)))


def _attn_fwd(proj, fr, *, tq, n_heads, name, comm=None):
    B, L, _ = proj.shape
    AD = n_heads * HEAD_DIM
    nq = L // tq
    W = LANES
    scale = HEAD_DIM ** -0.5
    edges = _band_edges(tq)

    v_ones, sum_lane = _own_lanes_and_ones, _sum_lane

    def body(q_ref, k_ref, v_ref, fr_ref, o_ref, lse_ref, m_s, acc_s):
        qi, ki = pl.program_id(1), pl.program_id(2)

        @pl.when(ki == 0)
        def _():
            m_s[...] = jnp.full_like(m_s, NEG)
            acc_s[...] = jnp.zeros_like(acc_s)

        def tile(diagonal):
            lane = lax.broadcasted_iota(jnp.int32, (tq, W), 1)
            m_all = m_s[...]
            m_out = m_all
            bands = [(r0, r1, r1 if diagonal else tq) for r0, r1 in zip(edges[:-1], edges[1:])]
            if diagonal:
                masks = {r0: (lax.broadcasted_iota(jnp.int32, (r1 - r0, c1), 1)
                              <= r0 + lax.broadcasted_iota(jnp.int32, (r1 - r0, c1), 0)) for r0, r1, c1 in bands}

            def scores(h, band):
                r0, r1, c1 = band
                sl = slice(h * HEAD_DIM, (h + 1) * HEAD_DIM)
                return _dot_nt(q_ref[r0:r1, sl] * scale, k_ref[0:c1, sl])

            work = [(h, band) for h in range(n_heads) for band in bands]
            nxt = scores(*work[0])
            for w, (h, band) in enumerate(work):
                r0, r1, c1 = band
                sl = slice(h * HEAD_DIM, (h + 1) * HEAD_DIM)
                s = nxt - fr_ref[h:h + 1, 0:c1]
                if w + 1 < len(work):
                    nxt = scores(*work[w + 1])
                if diagonal:
                    s = jnp.where(masks[r0], s, NEG)
                m_old = m_all[r0:r1, h:h + 1]
                m_new = jnp.maximum(m_old, jnp.max(s, axis=1, keepdims=True))
                alpha = jnp.exp(m_old - m_new)
                p = jnp.exp(s - m_new)
                own = slice(h * 2 * HEAD_DIM, (h + 1) * 2 * HEAD_DIM)
                acc_s[r0:r1, own] = alpha * acc_s[r0:r1, own] + _dot(p.astype(BF16), v_ones(v_ref[0:c1, _pair(h)], h))
                if r0 == 0:
                    m_parts = []
                m_parts.append(m_new)
                if r1 == tq:
                    m_out = jnp.where(lane == h, jnp.concatenate(m_parts, axis=0), m_out)
            m_s[...] = m_out

        @pl.when(ki < qi)
        def _():
            tile(False)

        @pl.when(ki == qi)
        def _():
            tile(True)
            lane = lax.broadcasted_iota(jnp.int32, (tq, W), 1)
            low = lax.broadcasted_iota(jnp.int32, (tq, 2 * HEAD_DIM), 1) < HEAD_DIM
            l_all = jnp.ones((tq, W), F32)
            for h in range(0, n_heads, 2):
                even = acc_s[:, h * 2 * HEAD_DIM:(h + 1) * 2 * HEAD_DIM]
                odd = acc_s[:, (h + 1) * 2 * HEAD_DIM:(h + 2) * 2 * HEAD_DIM]
                l_even = even[:, sum_lane(h):sum_lane(h) + 1]
                l_odd = odd[:, sum_lane(h + 1):sum_lane(h + 1) + 1]
                o_ref[:, _pair(h)] = jnp.where(low, even / l_even, odd / l_odd)
                l_all = jnp.where(lane == h, l_even, jnp.where(lane == h + 1, l_odd, l_all))
            lse_ref[...] = jnp.where(lane < n_heads, m_s[...] + jnp.log(l_all), 0.0)

    kv = lambda b, qi, ki: jnp.minimum(ki, qi)
    return _launch(
        body, name=name, grid=(B, nq, nq), args=(proj, proj, proj, fr), comm=comm,
        in_specs=[pl.BlockSpec((None, tq, AD), lambda b, qi, ki: (b, qi, 3)),
                  pl.BlockSpec((None, tq, AD), lambda b, qi, ki: (b, kv(b, qi, ki), 4)),
                  pl.BlockSpec((None, tq, AD), lambda b, qi, ki: (b, kv(b, qi, ki), 5)),
                  pl.BlockSpec((None, None, n_heads, tq), lambda b, qi, ki: (b, kv(b, qi, ki), 0, 0))],
        out_specs=[pl.BlockSpec((None, tq, AD), lambda b, qi, ki: (b, qi, 0)),
                   pl.BlockSpec((None, tq, W), lambda b, qi, ki: (b, qi, 0))],
        out_shape=[jax.ShapeDtypeStruct((B, L, AD), F32), jax.ShapeDtypeStruct((B, L, W), F32)],
        scratch_shapes=[pltpu.VMEM((tq, W), F32), pltpu.VMEM((tq, n_heads * 2 * HEAD_DIM), F32)])


def _attn_bwd(proj, o, do, lse, fr, *, tq, n_heads, name, comm=None):
    B, L, _ = proj.shape
    AD = n_heads * HEAD_DIM
    nq = L // tq
    W = LANES
    HW = 2 * HEAD_DIM
    scale = HEAD_DIM ** -0.5
    edges = _band_edges(tq)

    def body(q_ref, k_ref, v_ref, o_ref, do_ref, lse_ref, fr_ref,
             dq_ref, dk_ref, dv_ref, dfk_ref, dfq_ref, dq_s, dk_s, dv_s):
        kj, qi = pl.program_id(1), pl.program_id(2)

        @pl.when((kj == 0) & (qi == 0))
        def _():
            dq_s[...] = jnp.zeros_like(dq_s)

        @pl.when(qi == kj)
        def _():
            dk_s[...] = jnp.zeros_like(dk_s)
            dv_s[...] = jnp.zeros_like(dv_s)

        def tile(diagonal):
            bands = [(r0, r1, r1) for r0, r1 in zip(edges[:-1], edges[1:])] if diagonal else [(0, tq, tq)]
            lse = lse_ref[...]
            for r0, r1, c1 in bands:
                nr = r1 - r0
                rows = pl.ds(pl.multiple_of(qi * tq + r0, 8), nr)
                if diagonal:
                    mask = (lax.broadcasted_iota(jnp.int32, (nr, c1), 1)
                            <= r0 + lax.broadcasted_iota(jnp.int32, (nr, c1), 0))
                def scores(h):
                    ps = _pair(h)
                    k = k_ref[0:c1, ps]
                    qs = q_ref[r0:r1, ps] * scale
                    dov = _own_lanes(do_ref[r0:r1, ps], h)
                    return _dot_nt(_own_lanes(qs, h), k), _dot_nt(dov, v_ref[0:c1, ps]), k, qs, dov

                nxt = scores(0)
                for h in range(n_heads):
                    ps = _pair(h)
                    own = slice(h * HW, (h + 1) * HW)
                    s, dp, k, qs, dov = nxt
                    if h + 1 < n_heads:
                        nxt = scores(h + 1)
                    s = s - fr_ref[h:h + 1, 0:c1]
                    if diagonal:
                        s = jnp.where(mask, s, NEG)
                    p = jnp.exp(s - lse[r0:r1, h:h + 1])
                    dsum = jnp.sum(dov.astype(F32) * o_ref[r0:r1, ps], axis=1, keepdims=True)
                    dsb = (p * (dp - dsum)).astype(BF16)
                    dv = _dot_tn(p.astype(BF16), dov)
                    dk_s[0:c1, own] += _dot_tn(dsb, _own_lanes_and_ones(qs, h))
                    dq_s[rows, own] += _dot(dsb, _own_lanes_and_ones(k, h))
                    if h % 2 == 0:
                        dv_even = dv
                    else:
                        dv_s[0:c1, ps] += dv_even + dv

        def compact(acc, data_scale):
            rows = acc.shape[0]
            low = lax.broadcasted_iota(jnp.int32, (rows, HW), 1) < HEAD_DIM
            lane = lax.broadcasted_iota(jnp.int32, (rows, W), 1)
            vals, sums = [], jnp.zeros((rows, W), F32)
            for h in range(0, n_heads, 2):
                even, odd = acc[:, h * HW:(h + 1) * HW], acc[:, (h + 1) * HW:(h + 2) * HW]
                vals.append(jnp.where(low, even, odd) * data_scale)
                sums = jnp.where(lane == h, even[:, _sum_lane(h):_sum_lane(h) + 1],
                                 jnp.where(lane == h + 1, odd[:, _sum_lane(h + 1):_sum_lane(h + 1) + 1], sums))
            return vals, sums

        @pl.when(qi > kj)
        def _():
            tile(False)

        @pl.when(qi == kj)
        def _():
            tile(True)
            rows = pl.ds(pl.multiple_of(qi * tq, 8), tq)
            vals, sums = compact(dq_s[rows, :], scale)
            for h in range(0, n_heads, 2):
                dq_ref[rows, _pair(h)] = vals[h // 2]
            dfq_ref[rows, :] = sums

        @pl.when(qi == nq - 1)
        def _():
            vals, sums = compact(dk_s[...], 1.0)
            for h in range(0, n_heads, 2):
                dk_ref[:, _pair(h)] = vals[h // 2].astype(BF16)
            dfk_ref[...] = sums
            dv_ref[...] = dv_s[...].astype(BF16)

    qq = lambda b, kj, qi: jnp.maximum(qi, kj)
    qblk = lambda w, cb: pl.BlockSpec((None, tq, w), lambda b, kj, qi: (b, qq(b, kj, qi), cb))
    kblk = lambda w, cb: pl.BlockSpec((None, tq, w), lambda b, kj, qi: (b, kj, cb))
    return _launch(
        body, name=name, grid=(B, nq, nq), args=(proj, proj, proj, o, do, lse, fr), comm=comm,
        in_specs=[qblk(AD, 3), kblk(AD, 4), kblk(AD, 5), qblk(AD, 0), qblk(AD, 0), qblk(W, 0),
                  pl.BlockSpec((None, None, n_heads, tq), lambda b, kj, qi: (b, kj, 0, 0))],
        out_specs=[pl.BlockSpec((None, L, AD), lambda b, kj, qi: (b, 0, 0)),
                   kblk(AD, 0), kblk(AD, 0), kblk(W, 0),
                   pl.BlockSpec((None, L, W), lambda b, kj, qi: (b, 0, 0))],
        out_shape=[jax.ShapeDtypeStruct((B, L, AD), F32), jax.ShapeDtypeStruct((B, L, AD), BF16),
                   jax.ShapeDtypeStruct((B, L, AD), BF16), jax.ShapeDtypeStruct((B, L, W), F32),
                   jax.ShapeDtypeStruct((B, L, W), F32)],
        scratch_shapes=[pltpu.VMEM((L, n_heads * HW), F32), pltpu.VMEM((tq, n_heads * HW), F32),
                        pltpu.VMEM((tq, AD), F32)])


def _mix_gather(refs, first):
    b_ref, c_ref, hc_ref, cp_ref, hcp_ref, o_ref, cw_ref, p_ref = refs
    bg = b_ref[...].astype(F32)
    u = c_ref[...].astype(F32) * hc_ref[...].astype(F32)
    prev = cp_ref[...].astype(F32) * hcp_ref[...].astype(F32)
    prev = jnp.where(first, 0.0, prev)
    cv, u1, u2 = _causal_conv(u, prev, cw_ref[...])
    yc = bg * cv
    p = p_ref[...]
    rc = lax.rsqrt(_group_mean(yc * yc, p) + EPS)
    ya = o_ref[...].astype(F32)
    ra = lax.rsqrt(_group_mean(ya * ya, p) + EPS)
    return bg, (u, u1, u2), cv, yc * rc, rc, ya * ra, ra


def _mix_specs(tm, CD):
    per = tm // HALO
    cur = lambda cb: pl.BlockSpec((None, tm, CD), lambda b, i: (b, i, cb))
    prev = lambda cb: pl.BlockSpec((None, HALO, CD), lambda b, i: (b, jnp.maximum(i * per - 1, 0), cb))
    return [cur(0), cur(1), cur(2), prev(1), prev(2), cur(0)]


def _mix_out(proj, o, cw, gc, ga, wout, h, pmat, next_gain, *, tm, name, comm=None):
    B, L, D = h.shape
    CD = o.shape[-1]
    const = lambda b, i: (0, 0)

    def body(b_ref, c_ref, hc_ref, cp_ref, hcp_ref, o_ref, cw_ref, p_ref, gc_ref, ga_ref, w_ref, h_ref, ng_ref,
             out_ref, y_ref, n_ref):
        first = pl.program_id(1) == 0
        _, _, _, zc, _, za, _ = _mix_gather((b_ref, c_ref, hc_ref, cp_ref, hcp_ref, o_ref, cw_ref, p_ref), first)
        yc = (zc * gc_ref[...]).astype(BF16)
        ya = (za * ga_ref[...]).astype(BF16)
        y_ref[:, :CD] = yc
        y_ref[:, CD:] = ya
        out = h_ref[...] + _dot(yc, w_ref[:CD, :]) + _dot(ya, w_ref[CD:, :])
        out_ref[...] = out
        n_ref[...] = _rms(out, ng_ref[...])

    tile = pl.BlockSpec((None, tm, D), lambda b, i: (b, i, 0))
    return _launch(
        body, name=name, grid=(B, L // tm),
        in_specs=_mix_specs(tm, CD)
                 + [pl.BlockSpec(cw.shape, const), pl.BlockSpec(pmat.shape, const),
                    pl.BlockSpec((1, CD), const), pl.BlockSpec((1, CD), const), pl.BlockSpec((D, D), const),
                    tile, pl.BlockSpec((1, D), const)],
        out_specs=[tile, tile, tile],
        out_shape=[jax.ShapeDtypeStruct((B, L, D), F32), jax.ShapeDtypeStruct((B, L, D), BF16),
                   jax.ShapeDtypeStruct((B, L, D), BF16)],
        args=(proj, proj, proj, proj, proj, o, cw, pmat, gc, ga, wout, h, next_gain), comm=comm)


def _mix_out_bwd(dhb, proj, o, cw, gc, ga, wout, pmat, *, tm, name, comm=None):
    B, L, D = dhb.shape
    CD = o.shape[-1]
    const = lambda b, i: (0, 0)

    def body(dh_ref, b_ref, c_ref, hc_ref, cp_ref, hcp_ref, o_ref, cw_ref, p_ref, gc_ref, ga_ref, w_ref,
             db_ref, dcv_ref, do_ref, dgc_ref, dga_ref, dcw_ref):
        first = pl.program_id(1) == 0

        @pl.when((pl.program_id(0) == 0) & first)
        def _():
            dgc_ref[...] = jnp.zeros_like(dgc_ref)
            dga_ref[...] = jnp.zeros_like(dga_ref)
            dcw_ref[...] = jnp.zeros_like(dcw_ref)

        bg, us, cv, zc, rc, za, ra = _mix_gather(
            (b_ref, c_ref, hc_ref, cp_ref, hcp_ref, o_ref, cw_ref, p_ref), first)
        p = p_ref[...]
        dh = dh_ref[...]
        dyc = _dot_nt(dh, w_ref[:CD, :])
        dya = _dot_nt(dh, w_ref[CD:, :])

        dgc_ref[...] += jnp.sum(dyc * zc, axis=0, keepdims=True)
        dz = dyc * gc_ref[...]
        dx = rc * (dz - zc * _group_mean(dz * zc, p))
        db_ref[...] = (dx * cv).astype(BF16)
        dcv = dx * bg
        dcv_ref[...] = dcv.astype(BF16)
        for k in range(3):
            dcw_ref[k:k + 1, :] += jnp.sum(dcv * us[2 - k], axis=0, keepdims=True)

        dga_ref[...] += jnp.sum(dya * za, axis=0, keepdims=True)
        dz = dya * ga_ref[...]
        do_ref[...] = (ra * (dz - za * _group_mean(dz * za, p))).astype(BF16)

    tile = lambda w: pl.BlockSpec((None, tm, w), lambda b, i: (b, i, 0))
    return _launch(
        body, name=name, grid=(B, L // tm), comm=comm,
        args=(dhb, proj, proj, proj, proj, proj, o, cw, pmat, gc, ga, wout),
        in_specs=[tile(D)] + _mix_specs(tm, CD)
                 + [pl.BlockSpec(cw.shape, const), pl.BlockSpec(pmat.shape, const),
                    pl.BlockSpec((1, CD), const), pl.BlockSpec((1, CD), const), pl.BlockSpec((D, D), const)],
        out_specs=[tile(CD), tile(CD), tile(CD),
                   pl.BlockSpec((1, CD), const), pl.BlockSpec((1, CD), const), pl.BlockSpec((8, CD), const)],
        out_shape=[jax.ShapeDtypeStruct((B, L, CD), BF16)] * 3
                  + [jax.ShapeDtypeStruct((1, CD), F32)] * 2 + [jax.ShapeDtypeStruct((8, CD), F32)])


def _conv_bwd(dcv, proj, cw, *, tm, name):
    B, L, CD = dcv.shape
    per = tm // HALO
    nhalo = L // HALO
    nt = L // tm

    def body(d_ref, dn_ref, c_ref, hc_ref, cw_ref, out_ref):
        last = pl.program_id(1) == nt - 1
        d = d_ref[...].astype(F32)
        nxt = jnp.where(last, 0.0, dn_ref[...].astype(F32))
        n0, n1 = _row_of(nxt, 0), _row_of(nxt, 1)
        rows = lax.broadcasted_iota(jnp.int32, d.shape, 0)
        d1 = jnp.where(rows == tm - 1, n0, pltpu.roll(d, tm - 1, 0))
        d2 = jnp.where(rows == tm - 2, n0, jnp.where(rows == tm - 1, n1, pltpu.roll(d, tm - 2, 0)))
        w = cw_ref[...]
        du = w[2:3, :] * d + w[1:2, :] * d1 + w[0:1, :] * d2
        out_ref[:, :CD] = (du * hc_ref[...].astype(F32)).astype(BF16)
        out_ref[:, CD:] = (du * c_ref[...].astype(F32)).astype(BF16)

    return pl.pallas_call(
        body, name=name, grid=(B, nt),
        in_specs=[pl.BlockSpec((None, tm, CD), lambda b, i: (b, i, 0)),
                  pl.BlockSpec((None, HALO, CD), lambda b, i: (b, jnp.minimum((i + 1) * per, nhalo - 1), 0)),
                  pl.BlockSpec((None, tm, CD), lambda b, i: (b, i, 1)),
                  pl.BlockSpec((None, tm, CD), lambda b, i: (b, i, 2)),
                  pl.BlockSpec(cw.shape, lambda b, i: (0, 0))],
        out_specs=pl.BlockSpec((None, tm, 2 * CD), lambda b, i: (b, i, 0)),
        out_shape=jax.ShapeDtypeStruct((B, L, 2 * CD), BF16),
        compiler_params=_params("arbitrary", "arbitrary"),
    )(dcv, dcv, proj, proj, cw)


def _place():
    x, y, c = lax.axis_index("x"), lax.axis_index("y"), lax.axis_index("c")
    others = [(1 - x, y), (x, 1 - y), (1 - x, 1 - y)]
    return x, y, c, others


def _all_gather_shards(shards, *, name):
    n = len(shards)

    def body(*refs):
        ins, outs = refs[:n], refs[n:2 * n]
        send, recv, fsend, frecv, lsem = refs[2 * n:]
        x, y, c, others = _place()
        me = 2 * x + y
        local = [pltpu.make_async_copy(ins[t], outs[t].at[me], lsem.at[t]) for t in range(n)]
        for cp in local:
            cp.start()

        def half(t, k):
            hr = shards[t].shape[0] // 2
            return pl.ds(pl.multiple_of(k * hr, HALO), hr)

        def ici(t, j, src_chip, to):
            src = ins[t].at[half(t, c)] if to is not None else outs[t].at[src_chip, half(t, c)]
            return pltpu.make_async_remote_copy(
                src_ref=src, dst_ref=outs[t].at[src_chip, half(t, c)],
                send_sem=send.at[3 * t + j], recv_sem=recv.at[3 * t + j],
                device_id=(x, y, c) if to is None else to, device_id_type=MESH)

        def d2d(t, j, src_chip, k):
            return pltpu.make_async_remote_copy(
                src_ref=outs[t].at[src_chip, half(t, k)], dst_ref=outs[t].at[src_chip, half(t, k)],
                send_sem=fsend.at[3 * t + j], recv_sem=frecv.at[3 * t + j],
                device_id=(x, y, 1 - c), device_id_type=MESH)

        firsts = [ici(t, j, me, (ox, oy, c)) for t in range(n) for j, (ox, oy) in enumerate(others)]
        for cp in firsts:
            cp.start()
        passed = []
        for t in range(n):
            for j, (ox, oy) in enumerate(others):
                ici(t, j, 2 * ox + oy, None).wait_recv()
                cp = d2d(t, j, 2 * ox + oy, c)
                cp.start()
                passed.append(cp)
        for t in range(n):
            for j, (ox, oy) in enumerate(others):
                d2d(t, j, 2 * ox + oy, 1 - c).wait_recv()
        for cp in firsts + passed:
            cp.wait_send()
        for cp in local:
            cp.wait()

    return pl.pallas_call(
        body, name=name,
        in_specs=[ANY] * n, out_specs=[ANY] * n,
        out_shape=[jax.ShapeDtypeStruct((N_SHARD,) + s.shape, s.dtype) for s in shards],
        scratch_shapes=[pltpu.SemaphoreType.DMA((3 * n,))] * 4 + [pltpu.SemaphoreType.DMA((n,))],
    )(*shards)


def _all_reduce_small(slab, *, name):
    def body(in_ref, out_ref, gath, send, recv):
        x, y, c, _ = _place()
        me = 4 * x + 2 * y + c
        gath[me] = in_ref[...]
        copies, peers = [], []
        for m in range(1, N_DEV):
            px = jnp.where((m >> 2) & 1, 1 - x, x)
            py = jnp.where((m >> 1) & 1, 1 - y, y)
            pc = jnp.where(m & 1, 1 - c, c)
            cp = pltpu.make_async_remote_copy(
                src_ref=in_ref, dst_ref=gath.at[me], send_sem=send.at[m - 1], recv_sem=recv.at[m - 1],
                device_id=(px, py, pc), device_id_type=MESH)
            cp.start()
            copies.append(cp)
            peers.append(4 * px + 2 * py + pc)
        for m in range(1, N_DEV):
            pltpu.make_async_remote_copy(
                src_ref=in_ref, dst_ref=gath.at[peers[m - 1]], send_sem=send.at[m - 1], recv_sem=recv.at[m - 1],
                device_id=(x, y, c), device_id_type=MESH).wait_recv()
        for cp in copies:
            cp.wait_send()
        acc = gath[0]
        for k in range(1, N_DEV):
            acc = acc + gath[k]
        out_ref[...] = acc

    vm = pl.BlockSpec(memory_space=pltpu.VMEM)
    return pl.pallas_call(
        body, name=name, in_specs=[vm], out_specs=vm,
        out_shape=jax.ShapeDtypeStruct(slab.shape, slab.dtype),
        scratch_shapes=[pltpu.VMEM((N_DEV,) + slab.shape, slab.dtype),
                        pltpu.SemaphoreType.DMA((N_DEV - 1,)), pltpu.SemaphoreType.DMA((N_DEV - 1,))],
    )(slab)


def _gather_stage(shards, into, *, ici=(), d2d=()):
    n = len(shards) if into is None else len(into)
    ns = len(shards) if ici else 0
    ni, nd = max(len(ici), 1), max(len(d2d), 1)
    shapes = [s.shape for s in shards] if into is None else [p.shape[1:] for p in into]
    dtypes = [s.dtype for s in shards] if into is None else [p.dtype for p in into]

    def copies(ins, outs, sems, sending):
        x, y, c, others = _place()
        me = 2 * x + y
        out = []
        for t in range(n):
            hr = shapes[t][0] // 2
            mine = pl.ds(pl.multiple_of(c * hr, HALO), hr)
            theirs = pl.ds(pl.multiple_of((1 - c) * hr, HALO), hr)
            for a, j in enumerate(ici):
                ox, oy = others[j]
                src_chip = me if sending else 2 * ox + oy
                out.append(pltpu.make_async_remote_copy(
                    src_ref=ins[t].at[mine], dst_ref=outs[t].at[src_chip, mine],
                    send_sem=sems[0].at[ni * t + a], recv_sem=sems[1].at[ni * t + a],
                    device_id=(ox, oy, c) if sending else (x, y, c), device_id_type=MESH))
            for a, j in enumerate(d2d):
                ox, oy = others[j]
                blk = outs[t].at[2 * ox + oy, mine if sending else theirs]
                out.append(pltpu.make_async_remote_copy(
                    src_ref=blk, dst_ref=blk, send_sem=sems[2].at[nd * t + a], recv_sem=sems[3].at[nd * t + a],
                    device_id=(x, y, 1 - c) if sending else (x, y, c), device_id_type=MESH))
        return out

    def local(ins, outs, sems):
        if into is not None:
            return []
        x, y, _, _ = _place()
        return [pltpu.make_async_copy(ins[t], outs[t].at[2 * x + y], sems[4].at[t]) for t in range(n)]

    def start(ins, outs, sems):
        for cp in local(ins, outs, sems) + copies(ins, outs, sems, True):
            cp.start()

    def finish(ins, outs, sems):
        for cp in copies(ins, outs, sems, False):
            cp.wait_recv()
        for cp in copies(ins, outs, sems, True):
            cp.wait_send()
        for cp in local(ins, outs, sems):
            cp.wait()

    return _Comm((list(shards) if ici or into is None else []) + (list(into) if into is not None else []),
                 [jax.ShapeDtypeStruct((N_SHARD,) + tuple(sh), dt) for sh, dt in zip(shapes, dtypes)],
                 [ni * n, ni * n, nd * n, nd * n, n], start, finish,
                 aliases=None if into is None else {ns + t: t for t in range(n)})


def _gather_ici(shards):
    return _gather_stage(shards, None, ici=(0, 1, 2))


def _gather_d2d(parts):
    return _gather_stage((), parts, d2d=(0, 1, 2))


def _swap_halves(grads):
    n = len(grads)

    def copies(ins, outs, sems):
        x, y, c, _ = _place()
        out = []
        for t in range(n):
            hr = grads[t].shape[1] // 2
            rows = pl.ds(pl.multiple_of((1 - c) * hr, 8), hr)
            out.append(pltpu.make_async_remote_copy(
                src_ref=ins[t].at[:, rows, :], dst_ref=outs[t], send_sem=sems[0].at[t], recv_sem=sems[1].at[t],
                device_id=(x, y, 1 - c), device_id_type=MESH))
        return out

    def start(ins, outs, sems):
        for cp in copies(ins, outs, sems):
            cp.start()

    def finish(ins, outs, sems):
        for cp in copies(ins, outs, sems):
            cp.wait()

    return _Comm(grads, [jax.ShapeDtypeStruct((N_SHARD, g.shape[1] // 2, g.shape[2]), g.dtype) for g in grads],
                 [n, n], start, finish)


def _pair_sum(g, got, c, *, name):
    ns, R, C = g.shape
    hr = R // 2

    def body(c_ref, g_ref, r_ref, o_ref):
        o_ref[...] = (g_ref[...] + r_ref[...]).astype(BF16)

    return pl.pallas_call(
        body, name=name,
        grid_spec=pltpu.PrefetchScalarGridSpec(
            num_scalar_prefetch=1, grid=(ns,),
            in_specs=[pl.BlockSpec((None, hr, C), lambda s, cr: (s, cr[0], 0)),
                      pl.BlockSpec((None, hr, C), lambda s, cr: (s, 0, 0))],
            out_specs=pl.BlockSpec((None, hr, C), lambda s, cr: (s, 0, 0))),
        out_shape=jax.ShapeDtypeStruct((ns, hr, C), BF16),
        compiler_params=_params("arbitrary"),
    )(c, g, got)


def _scatter_chips(sums):
    n = len(sums)

    def copies(ins, outs, sems, sending):
        x, y, c, others = _place()
        me = 2 * x + y
        out = []
        for t in range(n):
            for j, (ox, oy) in enumerate(others):
                there = 2 * ox + oy
                out.append(pltpu.make_async_remote_copy(
                    src_ref=ins[t].at[there if sending else me], dst_ref=outs[t].at[me if sending else there],
                    send_sem=sems[0].at[3 * t + j], recv_sem=sems[1].at[3 * t + j],
                    device_id=(ox, oy, c) if sending else (x, y, c), device_id_type=MESH))
        return out

    def start(ins, outs, sems):
        for cp in copies(ins, outs, sems, True):
            cp.start()

    def finish(ins, outs, sems):
        for cp in copies(ins, outs, sems, False):
            cp.wait_recv()
        for cp in copies(ins, outs, sems, True):
            cp.wait_send()

    return _Comm(sums, [jax.ShapeDtypeStruct(s.shape, s.dtype) for s in sums], [3 * n, 3 * n], start, finish)


def _chip_sum(g, got, landed, idx, *, name):
    ns, R, C = g.shape
    hr = R // 2

    def body(i_ref, g_ref, r_ref, a_ref, b_ref, c_ref, o_ref):
        acc = g_ref[...] + r_ref[...]
        for ref in (a_ref, b_ref, c_ref):
            acc = acc + ref[...].astype(F32)
        o_ref[...] = acc

    other = lambda k: pl.BlockSpec((None, hr, C), lambda s, ir: (ir[2 + k], 0, 0))
    return pl.pallas_call(
        body, name=name,
        grid_spec=pltpu.PrefetchScalarGridSpec(
            num_scalar_prefetch=1, grid=(1,),
            in_specs=[pl.BlockSpec((None, hr, C), lambda s, ir: (ir[0], ir[1], 0)),
                      pl.BlockSpec((None, hr, C), lambda s, ir: (ir[0], 0, 0)),
                      other(0), other(1), other(2)],
            out_specs=pl.BlockSpec((hr, C), lambda s, ir: (ir[1], 0))),
        out_shape=jax.ShapeDtypeStruct((R, C), F32),
        compiler_params=_params("arbitrary"),
    )(idx, g, got, landed, landed, landed)


def _share_halves(halves):
    n = len(halves)

    def copies(outs, sems, sending):
        x, y, c, _ = _place()
        out = []
        for t in range(n):
            hr = halves[t].shape[0] // 2
            rows = pl.ds(pl.multiple_of((c if sending else 1 - c) * hr, 8), hr)
            out.append(pltpu.make_async_remote_copy(
                src_ref=outs[t].at[rows, :], dst_ref=outs[t].at[rows, :], send_sem=sems[0].at[t],
                recv_sem=sems[1].at[t], device_id=(x, y, 1 - c) if sending else (x, y, c), device_id_type=MESH))
        return out

    def start(ins, outs, sems):
        for cp in copies(outs, sems, True):
            cp.start()

    def finish(ins, outs, sems):
        for cp in copies(outs, sems, False):
            cp.wait_recv()
        for cp in copies(outs, sems, True):
            cp.wait_send()

    return _Comm(halves, [jax.ShapeDtypeStruct(h.shape, h.dtype) for h in halves], [n, n], start, finish,
                 aliases={t: t for t in range(n)})


def _adamw(w, g, m, v, *, name):
    R, C = w.shape
    tr = R
    for cand in (256, 128, 64, 32, 16, 8):
        if R % cand == 0:
            tr = cand
            break

    def body(w_ref, g_ref, m_ref, v_ref, go_ref, d_ref, mo_ref, vo_ref):
        gv = g_ref[...]
        go_ref[...] = gv
        mn = ADAM_B1 * m_ref[...] + (1.0 - ADAM_B1) * gv
        vn = ADAM_B2 * v_ref[...] + (1.0 - ADAM_B2) * (gv * gv)
        m_hat = mn / (1.0 - ADAM_B1 ** ADAM_STEP)
        v_hat = vn / (1.0 - ADAM_B2 ** ADAM_STEP)
        d_ref[...] = -ADAM_LR * (m_hat / (jnp.sqrt(v_hat) + ADAM_EPS) + ADAM_WD * w_ref[...])
        mo_ref[...] = mn
        vo_ref[...] = vn

    blk = pl.BlockSpec((tr, C), lambda i: (i, 0))
    return pl.pallas_call(
        body, name=name, grid=(R // tr,), in_specs=[blk] * 4, out_specs=[blk] * 4,
        out_shape=[jax.ShapeDtypeStruct((R, C), F32)] * 4,
        compiler_params=_params("arbitrary"),
    )(w, g, m, v)


def _pack_small(D, meta, n1, nm, n3, nf, gc, ga, bf, cw):
    def row(a):
        a = a.reshape(-1, a.shape[-1])
        return jnp.pad(a, ((0, 0), (0, D - a.shape[-1])))
    rows = [row(meta), row(n1), row(nm), row(n3), row(nf), row(jnp.concatenate([gc, ga], axis=-1)), row(bf), row(cw)]
    slab = jnp.concatenate(rows, axis=0)
    return jnp.pad(slab, ((0, SMALL_ROWS - slab.shape[0]), (0, 0)))


def _unpack_small(slab, like):
    meta, n1, nm, n3, nf, gc, ga, bf, cw = like
    nmeta, mc = meta.shape
    out = [slab[:nmeta, :mc].reshape(meta.shape)]
    r = nmeta
    for a in (n1, nm, n3, nf):
        out.append(slab[r, :a.shape[-1]].reshape(a.shape))
        r += 1
    cd = gc.shape[-1]
    out.append(slab[r, :cd].reshape(gc.shape))
    out.append(slab[r, cd:cd + ga.shape[-1]].reshape(ga.shape))
    r += 1
    out.append(slab[r, :bf.shape[-1]].reshape(bf.shape))
    r += 1
    out.append(slab[r:r + 3, :cw.shape[-1]].reshape(cw.shape))
    return out


def kernel(x, meta_tokens, ffn1_norm, ffn1_w_gu, ffn1_w_down, mix_norm, w_in, conv_w, b_f, out_norm_conv, out_norm_attn, w_out, ffn2_norm, ffn2_w_gu, ffn2_w_down, final_norm, loss_target, m_meta_tokens, m_ffn1_norm, m_ffn1_w_gu, m_ffn1_w_down, m_mix_norm, m_w_in, m_conv_w, m_b_f, m_out_norm_conv, m_out_norm_attn, m_w_out, m_ffn2_norm, m_ffn2_w_gu, m_ffn2_w_down, m_final_norm, v_meta_tokens, v_ffn1_norm, v_ffn1_w_gu, v_ffn1_w_down, v_mix_norm, v_w_in, v_conv_w, v_b_f, v_out_norm_conv, v_out_norm_attn, v_w_out, v_ffn2_norm, v_ffn2_w_gu, v_ffn2_w_down, v_final_norm):
    B, S, D = x.shape
    L = S + N_META
    T = B * L
    tm = L // 3
    assert tm * 3 == L and tm % HALO == 0
    tm2 = 2 * tm
    assert T % tm2 == 0
    guc = ffn1_w_gu.shape[-1]
    ff = N_SHARD * guc // 2
    H = b_f.shape[-1]
    AD = H * HEAD_DIM
    CD = conv_w.shape[-1] * N_SHARD
    assert CD == AD and CD + AD == D and CD % LANES == 0
    n_main = 3 * CD + 3 * AD
    ins = w_in.shape[-1]

    xi, yi, ci = lax.axis_index("x"), lax.axis_index("y"), lax.axis_index("c")
    chip = 2 * xi + yi

    small_shard = jnp.zeros((2 * HALO, meta_tokens.shape[-1]), F32)
    small_shard = small_shard.at[:N_META].set(meta_tokens)
    small_shard = small_shard.at[N_META:N_META + 3, :conv_w.shape[-1]].set(conv_w[0])
    big = [ffn1_w_gu[0], ffn1_w_down[0], w_in[0], w_out[0], ffn2_w_gu[0], ffn2_w_down[0]]
    wgu1_s, wd1_s, win_s, wout_s, wgu2_s, wd2_s = [w.astype(BF16) for w in big]
    small_g, = _all_gather_shards([small_shard], name="gather_small")
    meta_f = jnp.moveaxis(small_g[:, :N_META], 0, 1).reshape(N_META, D)
    cw_f = jnp.moveaxis(small_g[:, N_META:N_META + 3, :conv_w.shape[-1]], 0, 1).reshape(3, CD)
    cw8 = jnp.pad(cw_f, ((0, 5), (0, 0)))
    bf_p = jnp.pad(b_f, ((0, 0), (0, LANES - H)))
    gid = jnp.arange(CD) // HEAD_DIM
    pmat = jnp.where(gid[:, None] == gid[None, :], 1.0 / HEAD_DIM, 0.0).astype(BF16)

    gu_shape = jax.ShapeDtypeStruct((2, T, ff), BF16)
    gu_w_spec = pl.BlockSpec((None, D, guc), lambda s, i: (s, 0, 0))
    gu_o_spec = pl.BlockSpec((None, tm2, guc), lambda s, i: (s // 2, i, s % 2))

    sid = jnp.bitwise_xor(chip, jnp.array([0, 2, 1, 3], jnp.int32)).astype(jnp.int32)
    (h0, n1), wgu1_h = _embed_norm(x, meta_f, ffn1_norm, tm=tm, name="embed_norm",
                                   comm=_gather_stage([wgu1_s], None, ici=(0, 1)))
    gu1, wgu1_h = _ffn_up(n1, wgu1_s[None], sid, None, tm=tm2, first=0, count=1, name="ffn1_up_own",
                          comm=_gather_stage([wgu1_s], wgu1_h, ici=(2,), d2d=(0, 1)))
    gu1, out = _ffn_up(n1, wgu1_h[0], sid, gu1, tm=tm2, first=1, count=2, name="ffn1_up_near",
                       comm=_join(_gather_stage((), wgu1_h, d2d=(2,)), _gather_ici([wd1_s, wout_s])))
    wgu1, down_w = out[0], out[1:]
    gu1, (wd1, wout_g) = _ffn_up(n1, wgu1, sid, gu1, tm=tm2, first=3, count=1, name="ffn1_up_far",
                                 comm=_gather_d2d(down_w))
    wd1 = wd1.reshape(ff, D)
    (h1, n2), win_h = _ffn_down(gu1, wd1, h0, mix_norm, tm=tm, name="ffn1_down", comm=_gather_ici([win_s]))
    win_g, = _run_comm(_gather_d2d(win_h), name="gather_w_in")
    wout_f = wout_g.reshape(D, D)
    win_f = jnp.moveaxis(win_g, 0, 1).reshape(D, N_SHARD * ins)
    win_main = win_f[:, :n_main]
    win_fg = jnp.pad(win_f[:, n_main:], ((0, 0), (0, LANES - H)))

    proj, _ = _matmul_nn(n2, win_main, tm=tm2, nb=n_main // (3 * CD),
                         w_spec=pl.BlockSpec((D, 3 * CD), lambda s, i: (0, s)),
                         out_shape=jax.ShapeDtypeStruct((T, n_main), BF16),
                         out_spec=pl.BlockSpec((tm2, 3 * CD), lambda s, i: (i, s)), name="mix_in")
    fg, _ = _matmul_nn(n2, win_fg, tm=tm, nb=1, w_spec=pl.BlockSpec((D, LANES), lambda s, i: (0, 0)),
                       out_shape=jax.ShapeDtypeStruct((T, LANES), F32),
                       out_spec=pl.BlockSpec((tm, LANES), lambda s, i: (i, 0)), name="mix_in_fg")
    proj3 = proj.reshape(B, L, n_main)
    fg3 = fg.reshape(B, L, LANES)
    fc = _fcum(fg3, bf_p, ch=tm, name="forget_cumsum")
    fr = fc[:, :, :H].reshape(B, L // tm, tm, H).transpose(0, 1, 3, 2)
    (o, lse), ffn2_w = _attn_fwd(proj3, fr, tq=tm, n_heads=H, name="attn_fwd",
                                 comm=_gather_ici([wgu2_s, wd2_s]))
    (h2, ymix, n3), (wgu2, wd2) = _mix_out(
        proj3, o, cw8, out_norm_conv, out_norm_attn, wout_f, h1.reshape(B, L, D), pmat, ffn2_norm,
        tm=tm, name="mix_out", comm=_gather_d2d(ffn2_w))
    wd2 = wd2.reshape(ff, D)
    h2 = h2.reshape(T, D)
    n3 = n3.reshape(T, D)

    gu2, _ = _matmul_nn(n3, wgu2, tm=tm2, nb=N_SHARD, w_spec=gu_w_spec, out_shape=gu_shape, out_spec=gu_o_spec,
                        name="ffn2_up")
    (dh3f, dh3b, d_gf, loss_part), _ = _ffn_down_loss(gu2, wd2, h2, final_norm.reshape(1, D), loss_target,
                                                      tm=tm, name="ffn2_down_loss")

    c_arr = jnp.reshape(ci, (1,)).astype(jnp.int32)
    ks = jnp.arange(N_SHARD - 1, dtype=jnp.int32)
    idx = jnp.concatenate([jnp.stack([chip, ci]).astype(jnp.int32), ks + (ks >= chip).astype(jnp.int32)])

    def pair_sums(grads, got, names):
        return [_pair_sum(g, r, c_arr, name="pair_sum_" + nm) for g, r, nm in zip(grads, got, names)]

    def chip_sums(grads, got, landed, names):
        return [_chip_sum(g, r, l, idx, name="chip_sum_" + nm) for g, r, l, nm in zip(grads, got, landed, names)]

    def dw_up(n, dgu, name, comm=None):
        return _matmul_tn(
            n, dgu, tm=tm2, nb=N_SHARD, kb=D, x_spec=pl.BlockSpec((tm2, D), lambda s, i: (i, 0)),
            y_spec=pl.BlockSpec((None, tm2, guc), lambda s, i: (s // 2, i, s % 2)),
            out_shape=jax.ShapeDtypeStruct((N_SHARD, D, guc), F32),
            out_spec=pl.BlockSpec((None, D, guc), lambda s, i: (s, 0, 0)), name=name, comm=comm)

    (dgu2, d_wd2), _ = _ffn_bwd_act(dh3b, gu2, wd2, tm=tm, guc=guc, name="ffn2_bwd_act")
    (dh2, dh2b, d_g3), _ = _ffn_bwd_in(dgu2, wgu2, h2, ffn2_norm, dh3f, tm=tm, scale=1.0, name="ffn2_bwd_in")
    d_wgu2, _ = dw_up(n3, dgu2, "ffn2_dw_up")
    grads_f2 = [d_wgu2, d_wd2.reshape(N_SHARD, ff // N_SHARD, D)]
    names_f2 = ["wgu2", "wd2"]

    dh2b3 = dh2b.reshape(B, L, D)
    (d_bg, d_cv, d_o, d_gc, d_ga, d_cw), got_f2 = _mix_out_bwd(
        dh2b3, proj3, o, cw8, out_norm_conv, out_norm_attn, wout_f, pmat, tm=tm, name="mix_out_bwd",
        comm=_swap_halves(grads_f2))
    sums_f2 = pair_sums(grads_f2, got_f2, names_f2)
    d_wout, _ = _matmul_tn(
        ymix.reshape(T, D), dh2b, tm=tm2, nb=1, kb=D,
        x_spec=pl.BlockSpec((tm2, D), lambda s, i: (i, 0)), y_spec=pl.BlockSpec((tm2, D), lambda s, i: (i, 0)),
        out_shape=jax.ShapeDtypeStruct((D, D), F32), out_spec=pl.BlockSpec((D, D), lambda s, i: (0, 0)),
        name="dw_out")
    d_cc = _conv_bwd(d_cv, proj3, cw8, tm=tm, name="conv_bwd")
    (d_q, d_k, d_v, d_fk, d_fq), landed_f2 = _attn_bwd(proj3, o, d_o, lse, fr, tq=tm, n_heads=H, name="attn_bwd",
                                                       comm=_scatter_chips(sums_f2))
    halves_f2 = chip_sums(grads_f2, got_f2, landed_f2, names_f2)
    d_fc = d_fq - d_fk
    d_fg, d_bf = _fcum_bwd(d_fc, fg3, bf_p, ch=tm, name="forget_cumsum_bwd")

    parts = [d_bg.reshape(T, CD), d_cc.reshape(T, 2 * CD), d_q.reshape(T, AD), d_k.reshape(T, AD),
             d_v.reshape(T, AD), d_fg.reshape(T, LANES)]
    (dh1, dh1b, d_gm, d_proj), g_f2 = _mix_bwd_in(parts, win_main, win_fg, h1, mix_norm, dh2, tm=tm, scale=0.5,
                                                  name="mix_bwd_in", comm=_share_halves(halves_f2))
    wide = d_proj.shape[1]
    d_win_nat, _ = _matmul_tn(
        n2, d_proj, tm=tm, nb=1, kb=D,
        x_spec=pl.BlockSpec((tm, D), lambda s, i: (i, 0)), y_spec=pl.BlockSpec((tm, wide), lambda s, i: (i, 0)),
        out_shape=jax.ShapeDtypeStruct((D, wide), F32), out_spec=pl.BlockSpec((D, wide), lambda s, i: (0, 0)),
        name="dw_in")
    d_win = jnp.moveaxis(d_win_nat[:, :N_SHARD * ins].reshape(D, N_SHARD, ins), 1, 0)
    grads_mx = [d_win, d_wout.reshape(N_SHARD, D // N_SHARD, D)]
    names_mx = ["win", "wout"]

    (dgu1, d_wd1), got_mx = _ffn_bwd_act(dh1b, gu1, wd1, tm=tm, guc=guc, name="ffn1_bwd_act",
                                         comm=_swap_halves(grads_mx))
    sums_mx = pair_sums(grads_mx, got_mx, names_mx)
    grads_d1 = [d_wd1.reshape(N_SHARD, ff // N_SHARD, D)]
    d_wgu1, out = dw_up(n1, dgu1, "ffn1_dw_up", comm=_join(_scatter_chips(sums_mx), _swap_halves(grads_d1)))
    landed_mx, got_d1 = out[:2], out[2:]
    halves_mx = chip_sums(grads_mx, got_mx, landed_mx, names_mx)
    sums_d1 = pair_sums(grads_d1, got_d1, ["wd1"])
    grads_u1 = [d_wgu1]
    (grad_x, d_meta, d_g1), out = _ffn_bwd_in_first(
        dgu1, wgu1, h0, ffn1_norm, dh1, tm=tm, batch=B, name="ffn1_bwd_in",
        comm=_join(_join(_share_halves(halves_mx), _scatter_chips(sums_d1)), _swap_halves(grads_u1)))
    g_mx, landed_d1, got_u1 = out[:2], out[2:3], out[3:]
    halves_d1 = chip_sums(grads_d1, got_d1, landed_d1, ["wd1"])
    sums_u1 = pair_sums(grads_u1, got_u1, ["wgu1"])
    out = _run_comm(_join(_share_halves(halves_d1), _scatter_chips(sums_u1)), name="scatter_ffn1")
    g_d1, landed_u1 = out[:1], out[1:]
    halves_u1 = chip_sums(grads_u1, got_u1, landed_u1, ["wgu1"])
    g_u1 = _run_comm(_share_halves(halves_u1), name="share_ffn1")
    g_big = [g_u1[0], g_d1[0], g_mx[0], g_mx[1], g_f2[0], g_f2[1]]

    loss_row = jnp.zeros((1, D), F32).at[0, 0].set(loss_part[0, 0])
    slab = _pack_small(D, d_meta, d_g1, d_gm, d_g3, d_gf, d_gc, d_ga, d_bf[:, :H], d_cw[:3])
    slab = slab.at[SMALL_ROWS - 1].set(loss_row[0])
    total = _all_reduce_small(slab, name="reduce_small")
    loss = total[SMALL_ROWS - 1, 0]
    mcols = meta_tokens.shape[-1]
    ccols = conv_w.shape[-1]
    full_like = (jnp.zeros((N_META, D)), ffn1_norm, mix_norm, ffn2_norm, final_norm.reshape(1, D), out_norm_conv,
                 out_norm_attn, b_f, jnp.zeros((1, 3, CD)))
    g_small = _unpack_small(total, full_like)
    g_small[0] = lax.dynamic_slice_in_dim(g_small[0], chip * mcols, mcols, axis=1)
    g_small[8] = lax.dynamic_slice_in_dim(g_small[8], chip * ccols, ccols, axis=2)

    def small_slab(meta, a1, am, a3, af, gc, ga, bf, cw):
        return _pack_small(D, meta, a1, am, a3, af.reshape(1, D), gc, ga, bf, cw[0])

    w_small = small_slab(meta_tokens, ffn1_norm, mix_norm, ffn2_norm, final_norm, out_norm_conv, out_norm_attn, b_f, conv_w)
    m_small = small_slab(m_meta_tokens, m_ffn1_norm, m_mix_norm, m_ffn2_norm, m_final_norm, m_out_norm_conv,
                         m_out_norm_attn, m_b_f, m_conv_w)
    v_small = small_slab(v_meta_tokens, v_ffn1_norm, v_mix_norm, v_ffn2_norm, v_final_norm, v_out_norm_conv,
                         v_out_norm_attn, v_b_f, v_conv_w)
    gs = list(g_small)
    gs[4] = gs[4].reshape(final_norm.shape)
    g_slab = small_slab(gs[0], gs[1], gs[2], gs[3], gs[4], gs[5], gs[6], gs[7], gs[8])
    local_like = (meta_tokens, ffn1_norm, mix_norm, ffn2_norm, final_norm.reshape(1, D), out_norm_conv, out_norm_attn,
                  b_f, conv_w)
    small_out = [_unpack_small(s, local_like)
                 for s in _adamw(w_small, g_slab, m_small, v_small, name="adamw_small")[1:]]
    for lst in small_out:
        lst[4] = lst[4].reshape(final_norm.shape)

    names = ["wgu1", "wd1", "win", "wout", "wgu2", "wd2"]
    w_big = big
    m_big = [m_ffn1_w_gu[0], m_ffn1_w_down[0], m_w_in[0], m_w_out[0], m_ffn2_w_gu[0], m_ffn2_w_down[0]]
    v_big = [v_ffn1_w_gu[0], v_ffn1_w_down[0], v_w_in[0], v_w_out[0], v_ffn2_w_gu[0], v_ffn2_w_down[0]]
    big_out = [_adamw(w, g, m, v, name="adamw_" + nm) for w, g, m, v, nm in zip(w_big, g_big, m_big, v_big, names)]

    def assemble(small, bigs):
        meta, a1, am, a3, af, gc, ga, bf, cw = small
        gu1_, d1_, win_, wout_, gu2_, d2_ = [b[None] for b in bigs]
        return [meta, a1, gu1_, d1_, am, win_, cw, bf, gc, ga, wout_, a3, gu2_, d2_, af]

    gs_out = list(g_small)
    gs_out[4] = gs_out[4].reshape(final_norm.shape)
    grads_out = assemble(gs_out, [b[0] for b in big_out])
    delta_out = assemble(small_out[0], [b[1] for b in big_out])
    m_out = assemble(small_out[1], [b[2] for b in big_out])
    v_out = assemble(small_out[2], [b[3] for b in big_out])
    return (loss, grad_x, *grads_out, *delta_out, *m_out, *v_out)
```

```python
import functools

import jax
import jax.numpy as jnp
from jax import lax
from jax.experimental import pallas as pl
from jax.experimental.pallas import tpu as pltpu

F32 = jnp.float32
BF16 = jnp.bfloat16

EPS = 1e-6
N_META = 16
HEAD_DIM = 64
N_SHARD = 4
N_DEV = 8
HALO = 16
LANES = 128
SMALL_ROWS = 32
VMEM_LIMIT_V7X = 56 * 1024 * 1024
NEG = -1e30
ATTN_BANDS = 2

ADAM_LR = 0.001
ADAM_B1 = 0.9
ADAM_B2 = 0.999
ADAM_EPS = 1e-08
ADAM_WD = 0.01
ADAM_STEP = 10

MESH = pl.DeviceIdType.MESH
ANY = pl.BlockSpec(memory_space=pl.ANY)
NT_DIMS = (((1,), (1,)), ((), ()))
TN_DIMS = (((0,), (0,)), ((), ()))


def _params(*sem):
    return pltpu.CompilerParams(dimension_semantics=sem, vmem_limit_bytes=VMEM_LIMIT_V7X)


class _Comm:
    def __init__(self, ins, out_shapes, sems, start, finish, aliases=None):
        self.ins, self.out_shapes, self.sems = list(ins), list(out_shapes), list(sems)
        self.start, self.finish, self.aliases = start, finish, dict(aliases or {})


def _join(a, b):
    ni, no, ns = len(a.ins), len(a.out_shapes), len(a.sems)

    def start(ins, outs, sems):
        a.start(ins[:ni], outs[:no], sems[:ns])
        b.start(ins[ni:], outs[no:], sems[ns:])

    def finish(ins, outs, sems):
        a.finish(ins[:ni], outs[:no], sems[:ns])
        b.finish(ins[ni:], outs[no:], sems[ns:])

    aliases = dict(a.aliases)
    aliases.update({ni + i: no + j for i, j in b.aliases.items()})
    return _Comm(a.ins + b.ins, a.out_shapes + b.out_shapes, a.sems + b.sems, start, finish, aliases)


def _launch(body, *, name, grid, in_specs, out_specs, out_shape, args, scratch_shapes=(), comm=None, prefetch=(),
            aliases=None):
    single = not isinstance(out_shape, (list, tuple))
    out_specs = [out_specs] if single else list(out_specs)
    out_shape = [out_shape] if single else list(out_shape)
    in_specs, scratch_shapes, prefetch = list(in_specs), list(scratch_shapes), list(prefetch)
    params = _params(*(("arbitrary",) * len(grid)))
    n_pf, n_in, n_out, n_scr = len(prefetch), len(in_specs), len(out_specs), len(scratch_shapes)
    c_ins = comm.ins if comm else []
    c_shapes = comm.out_shapes if comm else []
    c_sems = comm.sems if comm else []
    c_in, c_out = len(c_ins), len(c_shapes)

    def carrier(*refs):
        p = 0
        pf = refs[p:p + n_pf]; p += n_pf
        a = refs[p:p + n_in]; p += n_in
        ci = refs[p:p + c_in]; p += c_in
        o = refs[p:p + n_out]; p += n_out
        co = refs[p:p + c_out]; p += c_out
        s = refs[p:p + n_scr]; p += n_scr
        cs = refs[p:]
        if comm:
            first = functools.reduce(lambda u, v: u & v, [pl.program_id(k) == 0 for k in range(len(grid))])

            @pl.when(first)
            def _():
                comm.start(ci, co, cs)

        body(*pf, *a, *o, *s)

        if comm:
            last = functools.reduce(lambda u, v: u & v, [pl.program_id(k) == grid[k] - 1 for k in range(len(grid))])

            @pl.when(last)
            def _():
                comm.finish(ci, co, cs)

    io_aliases = {n_pf + i: j for i, j in (aliases or {}).items()}
    if comm:
        io_aliases.update({n_pf + n_in + i: n_out + j for i, j in comm.aliases.items()})
    all_in, all_out = in_specs + [ANY] * c_in, out_specs + [ANY] * c_out
    all_scratch = scratch_shapes + [pltpu.SemaphoreType.DMA((k,)) for k in c_sems]
    if n_pf:
        spec = dict(grid_spec=pltpu.PrefetchScalarGridSpec(
            num_scalar_prefetch=n_pf, grid=grid, in_specs=all_in, out_specs=all_out, scratch_shapes=all_scratch))
    else:
        spec = dict(grid=grid, in_specs=all_in, out_specs=all_out, scratch_shapes=all_scratch)
    res = pl.pallas_call(carrier, name=name, out_shape=out_shape + c_shapes, input_output_aliases=io_aliases,
                         compiler_params=params, **spec)(*prefetch, *args, *c_ins)
    main = list(res[:n_out])
    return (main[0] if single else main), (list(res[n_out:]) if comm else None)


def _run_comm(comm, *, name):
    c_in, c_out = len(comm.ins), len(comm.out_shapes)

    def body(*refs):
        ci, co, cs = refs[:c_in], refs[c_in:c_in + c_out], refs[c_in + c_out:]
        comm.start(ci, co, cs)
        comm.finish(ci, co, cs)

    return list(pl.pallas_call(
        body, name=name, in_specs=[ANY] * c_in, out_specs=[ANY] * c_out, out_shape=comm.out_shapes,
        scratch_shapes=[pltpu.SemaphoreType.DMA((k,)) for k in comm.sems],
        input_output_aliases=comm.aliases)(*comm.ins))


def _chunks(width, step=512):
    out, c0 = [], 0
    while c0 < width:
        cw = min(step, width - c0)
        out.append((c0, cw))
        c0 += cw
    return out


def _split2(v):
    hi = v.astype(BF16)
    lo = (v - hi.astype(F32)).astype(BF16)
    return hi, lo


def _split3(v):
    hi = v.astype(BF16)
    r = v - hi.astype(F32)
    mid = r.astype(BF16)
    lo = (r - mid.astype(F32)).astype(BF16)
    return hi, mid, lo


def _dot(a, b):
    return jnp.dot(a, b, preferred_element_type=F32)


def _dot_nt(a, b):
    return lax.dot_general(a, b, NT_DIMS, preferred_element_type=F32)


def _dot_tn(a, b):
    return lax.dot_general(a, b, TN_DIMS, preferred_element_type=F32)


def _silu_mul(g, u):
    return g * jax.nn.sigmoid(g) * u


def _rms_bwd(dn, h, gain, dres):
    r = lax.rsqrt(jnp.mean(h * h, axis=-1, keepdims=True) + EPS)
    y = h * r
    dgain = jnp.sum(dn * y, axis=0, keepdims=True)
    dy = dn * gain
    dh = dres + r * (dy - y * jnp.mean(dy * y, axis=-1, keepdims=True))
    return dh, dgain


def _group_mean(v, p):
    hi, lo = _split2(v)
    return _dot(hi, p) + _dot(lo, p)


def _row_of(a, k):
    rows = lax.broadcasted_iota(jnp.int32, a.shape, 0)
    return jnp.sum(jnp.where(rows == k, a, 0.0), axis=0, keepdims=True)


def _causal_conv(u, prev, w):
    rows = lax.broadcasted_iota(jnp.int32, u.shape, 0)
    p1 = _row_of(prev, HALO - 1)
    p2 = _row_of(prev, HALO - 2)
    u1 = jnp.where(rows == 0, p1, pltpu.roll(u, 1, 0))
    u2 = jnp.where(rows == 0, p2, jnp.where(rows == 1, p1, pltpu.roll(u, 2, 0)))
    return w[2:3, :] * u + w[1:2, :] * u1 + w[0:1, :] * u2, u1, u2


def _rms(x, gain):
    return (x * lax.rsqrt(jnp.mean(x * x, axis=-1, keepdims=True) + EPS) * gain).astype(BF16)


def _embed_norm(x, meta, g, *, tm, name, comm=None):
    B, S, D = x.shape
    L = S + N_META
    per_seq = L // tm
    nt = B * per_seq
    body_rows = tm - N_META

    def body(meta_ref, g_ref, x_hbm, h_ref, n_ref, buf, sems):
        i = pl.program_id(0)

        def fetch(k, fn):
            slot, b, t = k % 2, k // per_seq, k % per_seq

            @pl.when(t == 0)
            def _():
                fn(pltpu.make_async_copy(x_hbm.at[b, pl.ds(0, body_rows)],
                                         buf.at[slot, pl.ds(N_META, body_rows)], sems.at[slot]))

            @pl.when(t != 0)
            def _():
                fn(pltpu.make_async_copy(x_hbm.at[b, pl.ds(pl.multiple_of(t * tm - N_META, 8), tm)],
                                         buf.at[slot], sems.at[slot]))

        @pl.when(i == 0)
        def _():
            fetch(i, lambda cp: cp.start())

        @pl.when(i + 1 < nt)
        def _():
            fetch(i + 1, lambda cp: cp.start())

        fetch(i, lambda cp: cp.wait())
        slot = i % 2

        @pl.when(i % per_seq == 0)
        def _():
            buf[slot, 0:N_META, :] = meta_ref[...]

        hv = buf[slot]
        h_ref[...] = hv
        n_ref[...] = _rms(hv, g_ref[...])

    row = pl.BlockSpec((tm, D), lambda i: (i, 0))
    return _launch(
        body, name=name, grid=(nt,),
        in_specs=[pl.BlockSpec((N_META, D), lambda i: (0, 0)), pl.BlockSpec((1, D), lambda i: (0, 0)), ANY],
        out_specs=[row, row],
        out_shape=[jax.ShapeDtypeStruct((B * L, D), F32), jax.ShapeDtypeStruct((B * L, D), BF16)],
        scratch_shapes=[pltpu.VMEM((2, tm, D), F32), pltpu.SemaphoreType.DMA((2,))],
        args=(meta, g, x), comm=comm)


def _ffn_up(n, wgu, sid, gu_prev, *, tm, first, count, name, comm=None):
    T, D = n.shape
    ns, _, guc = wgu.shape
    ff = N_SHARD * guc // 2

    def body(sid_ref, x_ref, w_ref, *rest):
        rest[-1][...] = _dot(x_ref[...], w_ref[...]).astype(BF16)

    where = lambda s, sid: sid[first + s]
    w_at = (lambda s, sid: 0) if ns == 1 else where
    return _launch(
        body, name=name, grid=(count, T // tm), prefetch=(sid,),
        in_specs=[pl.BlockSpec((tm, D), lambda s, i, sid: (i, 0)),
                  pl.BlockSpec((None, D, guc), lambda s, i, sid: (w_at(s, sid), 0, 0))]
                 + ([] if gu_prev is None else [ANY]),
        out_specs=pl.BlockSpec((None, tm, guc), lambda s, i, sid: (where(s, sid) // 2, i, where(s, sid) % 2)),
        out_shape=jax.ShapeDtypeStruct((2, T, ff), BF16),
        args=(n, wgu) + (() if gu_prev is None else (gu_prev,)),
        aliases=None if gu_prev is None else {2: 0}, comm=comm)


def _matmul_nn(x, w, *, tm, nb, w_spec, out_shape, out_spec, name, comm=None):
    T, K = x.shape

    def body(x_ref, w_ref, o_ref):
        o_ref[...] = _dot(x_ref[...], w_ref[...]).astype(o_ref.dtype)

    return _launch(
        body, name=name, grid=(nb, T // tm),
        in_specs=[pl.BlockSpec((tm, K), lambda s, i: (i, 0)), w_spec],
        out_specs=out_spec, out_shape=out_shape, args=(x, w), comm=comm)


def _down_in_bands(g_ref, u_ref, wd_v, edges, chunks, finish):
    def down(rows):
        def act(c0, cw):
            return _silu_mul(g_ref[rows, c0:c0 + cw].astype(F32), u_ref[rows, c0:c0 + cw].astype(F32)).astype(BF16)

        acc = None
        nxt = act(*chunks[0])
        for k, (c0, cw) in enumerate(chunks):
            a = nxt
            if k + 1 < len(chunks):
                nxt = act(*chunks[k + 1])
            d = _dot(a, wd_v[c0:c0 + cw, :])
            acc = d if acc is None else acc + d
        return acc

    bands = [slice(r0, r1) for r0, r1 in zip(edges[:-1], edges[1:])]
    nxt = down(bands[0])
    for b, rows in enumerate(bands):
        acc = nxt
        if b + 1 < len(bands):
            nxt = down(bands[b + 1])
        finish(rows, acc)


def _ffn_down(gu, wd, h, next_gain, *, tm, name, comm=None):
    _, T, ff = gu.shape
    D = h.shape[1]
    chunks = _chunks(ff)

    def body(g_ref, u_ref, wd_hbm, h_ref, ng_ref, o_ref, n_ref, wd_v, sem):
        @pl.when(pl.program_id(0) == 0)
        def _():
            cp = pltpu.make_async_copy(wd_hbm, wd_v, sem)
            cp.start()
            cp.wait()

        def finish(rows, acc):
            out = h_ref[rows, :] + 0.5 * acc
            o_ref[rows, :] = out
            n_ref[rows, :] = _rms(out, ng_ref[...])

        _down_in_bands(g_ref, u_ref, wd_v, _band_edges(tm), chunks, finish)

    return _launch(
        body, name=name, grid=(T // tm,),
        in_specs=[pl.BlockSpec((None, tm, ff), lambda i: (0, i, 0)),
                  pl.BlockSpec((None, tm, ff), lambda i: (1, i, 0)),
                  ANY,
                  pl.BlockSpec((tm, D), lambda i: (i, 0)),
                  pl.BlockSpec((1, D), lambda i: (0, 0))],
        out_specs=[pl.BlockSpec((tm, D), lambda i: (i, 0)), pl.BlockSpec((tm, D), lambda i: (i, 0))],
        out_shape=[jax.ShapeDtypeStruct((T, D), F32), jax.ShapeDtypeStruct((T, D), BF16)],
        scratch_shapes=[pltpu.VMEM((ff, D), BF16), pltpu.SemaphoreType.DMA],
        args=(gu, gu, wd, h, next_gain), comm=comm)


def _ffn_down_loss(gu, wd, h, gf, tgt, *, tm, name, comm=None):
    _, T, ff = gu.shape
    D = h.shape[1]
    B, S, _ = tgt.shape
    per_seq = (S + N_META) // tm
    body_rows = tm - N_META
    chunks = _chunks(ff)

    def body(g_ref, u_ref, wd_hbm, h_ref, gf_ref, tgt_hbm, dh_ref, dhb_ref, dg_ref, loss_ref, wd_v, tg_v, sem, tsem):
        i = pl.program_id(0)
        b, t = i // per_seq, i % per_seq

        @pl.when(i == 0)
        def _():
            cp = pltpu.make_async_copy(wd_hbm, wd_v, sem)
            cp.start()
            cp.wait()
            dg_ref[...] = jnp.zeros_like(dg_ref)
            loss_ref[...] = jnp.zeros_like(loss_ref)
            tg_v[0:N_META, :] = jnp.zeros((N_META, D), F32)

        def fetch(fn):
            @pl.when(t == 0)
            def _():
                fn(pltpu.make_async_copy(tgt_hbm.at[b, pl.ds(0, body_rows)], tg_v.at[pl.ds(N_META, body_rows)], tsem))

            @pl.when(t != 0)
            def _():
                fn(pltpu.make_async_copy(tgt_hbm.at[b, pl.ds(pl.multiple_of(t * tm - N_META, 8), tm)], tg_v, tsem))

        fetch(lambda cp: cp.start())

        def finish(rows, acc):
            if rows.start == 0:
                fetch(lambda cp: cp.wait())
            x = h_ref[rows, :] + 0.5 * acc
            gain = gf_ref[...]
            r = lax.rsqrt(jnp.mean(x * x, axis=-1, keepdims=True) + EPS)
            y = x * r
            pos = t * tm + rows.start + lax.broadcasted_iota(jnp.int32, (rows.stop - rows.start, 1), 0)
            err = jnp.where(pos >= N_META, y * gain - tg_v[rows, :], 0.0)
            loss_ref[...] += 0.5 * jnp.sum(jnp.mean(err * err, axis=-1, keepdims=True))
            dout = err / D
            dg_ref[...] += jnp.sum(dout * y, axis=0, keepdims=True)
            dy = dout * gain
            dh = r * (dy - y * jnp.mean(dy * y, axis=-1, keepdims=True))
            dh_ref[rows, :] = dh
            dhb_ref[rows, :] = (0.5 * dh).astype(BF16)

        _down_in_bands(g_ref, u_ref, wd_v, _band_edges(tm), chunks, finish)

    row = pl.BlockSpec((tm, D), lambda i: (i, 0))
    const = lambda i: (0, 0)
    return _launch(
        body, name=name, grid=(T // tm,),
        in_specs=[pl.BlockSpec((None, tm, ff), lambda i: (0, i, 0)),
                  pl.BlockSpec((None, tm, ff), lambda i: (1, i, 0)),
                  ANY, row, pl.BlockSpec((1, D), const), ANY],
        out_specs=[row, row, pl.BlockSpec((1, D), const), pl.BlockSpec((1, LANES), const)],
        out_shape=[jax.ShapeDtypeStruct((T, D), F32), jax.ShapeDtypeStruct((T, D), BF16),
                   jax.ShapeDtypeStruct((1, D), F32), jax.ShapeDtypeStruct((1, LANES), F32)],
        scratch_shapes=[pltpu.VMEM((ff, D), BF16), pltpu.VMEM((tm, D), F32), pltpu.SemaphoreType.DMA,
                        pltpu.SemaphoreType.DMA],
        args=(gu, gu, wd, h, gf, tgt), comm=comm)


def _ffn_bwd_act(df, gu, wd, *, tm, guc, name, comm=None):
    _, T, ff = gu.shape
    D = df.shape[1]
    nj = ff // guc
    chunks = _chunks(guc)

    def body(df_ref, g_ref, u_ref, wd_ref, o_ref, dwd_ref):
        @pl.when(pl.program_id(1) == 0)
        def _():
            dwd_ref[...] = jnp.zeros_like(dwd_ref)

        dfv = df_ref[...]
        nxt = _dot_nt(dfv, wd_ref[chunks[0][0]:chunks[0][0] + chunks[0][1], :])
        for k, (c0, cw) in enumerate(chunks):
            da = nxt
            if k + 1 < len(chunks):
                n0, nw = chunks[k + 1]
                nxt = _dot_nt(dfv, wd_ref[n0:n0 + nw, :])
            g = g_ref[:, c0:c0 + cw].astype(F32)
            u = u_ref[:, c0:c0 + cw].astype(F32)
            sg = jax.nn.sigmoid(g)
            silu = g * sg
            o_ref[0, :, c0:c0 + cw] = (da * u * (sg * (1.0 + g * (1.0 - sg)))).astype(BF16)
            o_ref[1, :, c0:c0 + cw] = (da * silu).astype(BF16)
            dwd_ref[c0:c0 + cw, :] += _dot_tn((silu * u).astype(BF16), dfv)

    return _launch(
        body, name=name, grid=(nj, T // tm),
        in_specs=[pl.BlockSpec((tm, D), lambda j, i: (i, 0)),
                  pl.BlockSpec((None, tm, guc), lambda j, i: (0, i, j)),
                  pl.BlockSpec((None, tm, guc), lambda j, i: (1, i, j)),
                  pl.BlockSpec((guc, D), lambda j, i: (j, 0))],
        out_specs=[pl.BlockSpec((2, tm, guc), lambda j, i: (0, i, j)), pl.BlockSpec((guc, D), lambda j, i: (j, 0))],
        out_shape=[jax.ShapeDtypeStruct((2, T, ff), BF16), jax.ShapeDtypeStruct((ff, D), F32)],
        args=(df, gu, gu, wd), comm=comm)


def _ffn_bwd_in(dgu, wgu, h, g, dres, *, tm, scale, name, comm=None):
    _, T, ff = dgu.shape
    ns, D, guc = wgu.shape
    nj = ff // guc
    edges = _band_edges(tm)

    def body(dgu_ref, w_hbm, h_ref, g_ref, dres_ref, dh_ref, dhb_ref, dg_ref, w_v, acc, sem):
        i, j = pl.program_id(0), pl.program_id(1)

        @pl.when((i == 0) & (j == 0))
        def _():
            cp = pltpu.make_async_copy(w_hbm, w_v, sem)
            cp.start()
            cp.wait()
            dg_ref[...] = jnp.zeros_like(dg_ref)

        def dots(rows):
            return _dot_nt(dgu_ref[0, rows, :], w_v[j]) + _dot_nt(dgu_ref[1, rows, :], w_v[nj + j])

        @pl.when(j < nj - 1)
        def _():
            part = dots(slice(None))

            @pl.when(j == 0)
            def _():
                acc[...] = part

            @pl.when(j > 0)
            def _():
                acc[...] += part

        @pl.when(j == nj - 1)
        def _():
            bands = [slice(r0, r1) for r0, r1 in zip(edges[:-1], edges[1:])]
            nxt = dots(bands[0])
            for b, rows in enumerate(bands):
                dn = nxt if nj == 1 else acc[rows, :] + nxt
                if b + 1 < len(bands):
                    nxt = dots(bands[b + 1])
                dh, dgain = _rms_bwd(dn, h_ref[rows, :], g_ref[...], dres_ref[rows, :])
                dh_ref[rows, :] = dh
                dhb_ref[rows, :] = (scale * dh).astype(BF16)
                dg_ref[...] += dgain

    return _launch(
        body, name=name, grid=(T // tm, nj),
        in_specs=[pl.BlockSpec((2, tm, guc), lambda i, j: (0, i, j)),
                  ANY,
                  pl.BlockSpec((tm, D), lambda i, j: (i, 0)),
                  pl.BlockSpec((1, D), lambda i, j: (0, 0)),
                  pl.BlockSpec((tm, D), lambda i, j: (i, 0))],
        out_specs=[pl.BlockSpec((tm, D), lambda i, j: (i, 0)),
                   pl.BlockSpec((tm, D), lambda i, j: (i, 0)),
                   pl.BlockSpec((1, D), lambda i, j: (0, 0))],
        out_shape=[jax.ShapeDtypeStruct((T, D), F32), jax.ShapeDtypeStruct((T, D), BF16),
                   jax.ShapeDtypeStruct((1, D), F32)],
        scratch_shapes=[pltpu.VMEM((ns, D, guc), BF16), pltpu.VMEM((tm, D), F32), pltpu.SemaphoreType.DMA],
        args=(dgu, wgu, h, g, dres), comm=comm)


def _ffn_bwd_in_first(dgu, wgu, h, g, dres, *, tm, batch, name, comm=None):
    _, T, ff = dgu.shape
    ns, D, guc = wgu.shape
    nj = ff // guc
    nt = T // tm
    L = T // batch
    per_seq = L // tm
    body_rows = tm - N_META
    edges = _band_edges(tm)

    def body(dgu_ref, w_hbm, h_ref, g_ref, dres_ref, dx_hbm, dmeta_ref, dg_ref, w_v, acc, dh_v, sem, osem):
        i, j = pl.program_id(0), pl.program_id(1)

        @pl.when((i == 0) & (j == 0))
        def _():
            cp = pltpu.make_async_copy(w_hbm, w_v, sem)
            cp.start()
            cp.wait()
            dg_ref[...] = jnp.zeros_like(dg_ref)
            dmeta_ref[...] = jnp.zeros_like(dmeta_ref)

        def dots(rows):
            return _dot_nt(dgu_ref[0, rows, :], w_v[j]) + _dot_nt(dgu_ref[1, rows, :], w_v[nj + j])

        @pl.when(j < nj - 1)
        def _():
            part = dots(slice(None))

            @pl.when(j == 0)
            def _():
                acc[...] = part

            @pl.when(j > 0)
            def _():
                acc[...] += part

        def head_copy(b):
            return pltpu.make_async_copy(dh_v.at[pl.ds(N_META, body_rows)], dx_hbm.at[b, pl.ds(0, body_rows)], osem)

        def tail_copy(b, t):
            return pltpu.make_async_copy(dh_v, dx_hbm.at[b, pl.ds(pl.multiple_of(t * tm - N_META, 8), tm)], osem)

        def on_tile(k, head_fn, tail_fn):
            @pl.when(k % per_seq == 0)
            def _():
                head_fn(head_copy(k // per_seq))

            @pl.when(k % per_seq != 0)
            def _():
                tail_fn(tail_copy(k // per_seq, k % per_seq))

        @pl.when(j == nj - 1)
        def _():
            @pl.when(i > 0)
            def _():
                on_tile(i - 1, lambda cp: cp.wait(), lambda cp: cp.wait())

            bands = [slice(r0, r1) for r0, r1 in zip(edges[:-1], edges[1:])]
            nxt = dots(bands[0])
            for b, rows in enumerate(bands):
                dn = nxt if nj == 1 else acc[rows, :] + nxt
                if b + 1 < len(bands):
                    nxt = dots(bands[b + 1])
                dh, dgain = _rms_bwd(dn, h_ref[rows, :], g_ref[...], dres_ref[rows, :])
                dg_ref[...] += dgain
                dh_v[rows, :] = dh
                if b == 0:
                    @pl.when(i % per_seq == 0)
                    def _():
                        dmeta_ref[...] += dh[0:N_META, :]

            on_tile(i, lambda cp: cp.start(), lambda cp: cp.start())

            @pl.when(i == nt - 1)
            def _():
                on_tile(i, lambda cp: cp.wait(), lambda cp: cp.wait())

    return _launch(
        body, name=name, grid=(nt, nj),
        in_specs=[pl.BlockSpec((2, tm, guc), lambda i, j: (0, i, j)),
                  ANY,
                  pl.BlockSpec((tm, D), lambda i, j: (i, 0)),
                  pl.BlockSpec((1, D), lambda i, j: (0, 0)),
                  pl.BlockSpec((tm, D), lambda i, j: (i, 0))],
        out_specs=[ANY, pl.BlockSpec((N_META, D), lambda i, j: (0, 0)), pl.BlockSpec((1, D), lambda i, j: (0, 0))],
        out_shape=[jax.ShapeDtypeStruct((batch, L - N_META, D), F32), jax.ShapeDtypeStruct((N_META, D), F32),
                   jax.ShapeDtypeStruct((1, D), F32)],
        scratch_shapes=[pltpu.VMEM((ns, D, guc), BF16), pltpu.VMEM((tm, D), F32), pltpu.VMEM((tm, D), F32),
                        pltpu.SemaphoreType.DMA, pltpu.SemaphoreType.DMA],
        args=(dgu, wgu, h, g, dres), comm=comm)


def _mix_bwd_in(parts, w_main, w_fg, h, g, dres, *, tm, scale, name, comm=None):
    T, D = h.shape
    widths = [p.shape[1] for p in parts]
    offs = [sum(widths[:k]) for k in range(len(widths))]
    npart = len(parts)
    wide = sum(widths)
    edges = _band_edges(tm)

    def body(*refs):
        p_refs = refs[:npart]
        wm_ref, wf_ref, h_ref, g_ref, dres_ref, dh_ref, dhb_ref, dg_ref, all_ref = refs[npart:]

        @pl.when(pl.program_id(0) == 0)
        def _():
            dg_ref[...] = jnp.zeros_like(dg_ref)

        for p_ref, off, wd_ in zip(p_refs, offs, widths):
            for c0, cw in _chunks(wd_):
                all_ref[:, off + c0:off + c0 + cw] = p_ref[:, c0:c0 + cw].astype(BF16)
        n_main = offs[-1]

        def dots(rows):
            return _dot_nt(all_ref[rows, :n_main], wm_ref[...]) + _dot_nt(all_ref[rows, n_main:], wf_ref[...])

        bands = [slice(r0, r1) for r0, r1 in zip(edges[:-1], edges[1:])]
        nxt = dots(bands[0])
        for b, rows in enumerate(bands):
            dn = nxt
            if b + 1 < len(bands):
                nxt = dots(bands[b + 1])
            dh, dgain = _rms_bwd(dn, h_ref[rows, :], g_ref[...], dres_ref[rows, :])
            dh_ref[rows, :] = dh
            dhb_ref[rows, :] = (scale * dh).astype(BF16)
            dg_ref[...] += dgain

    row = lambda i: (i, 0)
    const = lambda i: (0, 0)
    return _launch(
        body, name=name, grid=(T // tm,),
        in_specs=[pl.BlockSpec((tm, p.shape[1]), row) for p in parts]
                 + [pl.BlockSpec(w_main.shape, const), pl.BlockSpec(w_fg.shape, const),
                    pl.BlockSpec((tm, D), row), pl.BlockSpec((1, D), const), pl.BlockSpec((tm, D), row)],
        out_specs=[pl.BlockSpec((tm, D), row), pl.BlockSpec((tm, D), row), pl.BlockSpec((1, D), const),
                   pl.BlockSpec((tm, wide), row)],
        out_shape=[jax.ShapeDtypeStruct((T, D), F32), jax.ShapeDtypeStruct((T, D), BF16),
                   jax.ShapeDtypeStruct((1, D), F32), jax.ShapeDtypeStruct((T, wide), BF16)],
        args=(*parts, w_main, w_fg, h, g, dres), comm=comm)


def _matmul_tn(x, y, *, tm, nb, x_spec, y_spec, out_shape, out_spec, kb, name, comm=None):
    T = y.shape[-2]
    chunks = _chunks(kb)

    def body(x_ref, y_ref, o_ref):
        @pl.when(pl.program_id(1) == 0)
        def _():
            o_ref[...] = jnp.zeros_like(o_ref)

        yv = y_ref[...].astype(BF16)
        nxt = _dot_tn(x_ref[:, chunks[0][0]:chunks[0][0] + chunks[0][1]], yv)
        for k, (c0, cw) in enumerate(chunks):
            cur = nxt
            if k + 1 < len(chunks):
                n0, nw = chunks[k + 1]
                nxt = _dot_tn(x_ref[:, n0:n0 + nw], yv)
            o_ref[c0:c0 + cw, :] += cur

    return _launch(
        body, name=name, grid=(nb, T // tm),
        in_specs=[x_spec, y_spec], out_specs=out_spec, out_shape=out_shape, args=(x, y), comm=comm)


def _tri(n, lower):
    r = lax.broadcasted_iota(jnp.int32, (n, n), 0)
    c = lax.broadcasted_iota(jnp.int32, (n, n), 1)
    return jnp.where((r >= c) if lower else (r <= c), 1.0, 0.0).astype(BF16)


def _tri_dot(tri, v):
    hi, mid, lo = _split3(v)
    return _dot(tri, hi) + _dot(tri, mid) + _dot(tri, lo)


def _fcum(fg, bf, *, ch, name):
    B, L, W = fg.shape
    nch = L // ch

    def body(fg_ref, bf_ref, f_ref):
        tri = _tri(ch, True)
        carry = jnp.zeros((1, W), F32)
        for c in range(nch):
            x = fg_ref[c * ch:(c + 1) * ch, :] + bf_ref[...]
            lf = jnp.minimum(x, 0.0) - jnp.log(1.0 + jnp.exp(-jnp.abs(x)))
            f_ref[c * ch:(c + 1) * ch, :] = _tri_dot(tri, lf) + carry
            carry = carry + jnp.sum(lf, axis=0, keepdims=True)

    return pl.pallas_call(
        body, name=name, grid=(B,),
        in_specs=[pl.BlockSpec((None, L, W), lambda b: (b, 0, 0)), pl.BlockSpec((1, W), lambda b: (0, 0))],
        out_specs=pl.BlockSpec((None, L, W), lambda b: (b, 0, 0)),
        out_shape=jax.ShapeDtypeStruct((B, L, W), F32),
        compiler_params=_params("arbitrary"),
    )(fg, bf)


def _fcum_bwd(dF, fg, bf, *, ch, name):
    B, L, W = fg.shape
    nch = L // ch

    def body(df_ref, fg_ref, bf_ref, dfg_ref, db_ref):
        @pl.when(pl.program_id(0) == 0)
        def _():
            db_ref[...] = jnp.zeros_like(db_ref)

        tri = _tri(ch, False)
        carry = jnp.zeros((1, W), F32)
        dbs = jnp.zeros((1, W), F32)
        for c in reversed(range(nch)):
            d = df_ref[c * ch:(c + 1) * ch, :]
            dlf = _tri_dot(tri, d) + carry
            carry = carry + jnp.sum(d, axis=0, keepdims=True)
            x = fg_ref[c * ch:(c + 1) * ch, :] + bf_ref[...]
            dfg = dlf * jax.nn.sigmoid(-x)
            dfg_ref[c * ch:(c + 1) * ch, :] = dfg.astype(BF16)
            dbs = dbs + jnp.sum(dfg, axis=0, keepdims=True)
        db_ref[...] += dbs

    blk = pl.BlockSpec((None, L, W), lambda b: (b, 0, 0))
    return pl.pallas_call(
        body, name=name, grid=(B,),
        in_specs=[blk, blk, pl.BlockSpec((1, W), lambda b: (0, 0))],
        out_specs=[blk, pl.BlockSpec((1, W), lambda b: (0, 0))],
        out_shape=[jax.ShapeDtypeStruct((B, L, W), BF16), jax.ShapeDtypeStruct((1, W), F32)],
        compiler_params=_params("arbitrary"),
    )(dF, fg, bf)


def _band_edges(tq):
    return sorted({min(tq, (k * tq // ATTN_BANDS + HALO - 1) // HALO * HALO) for k in range(ATTN_BANDS + 1)})


def _pair(h):
    return slice((h // 2) * 2 * HEAD_DIM, (h // 2 + 1) * 2 * HEAD_DIM)


def _own_lanes(a, h):
    low = lax.broadcasted_iota(jnp.int32, a.shape, 1) < HEAD_DIM
    return jnp.where(low if h % 2 == 0 else jnp.logical_not(low), a, jnp.zeros_like(a))


def _sum_lane(h):
    return HEAD_DIM if h % 2 == 0 else 0


def _own_lanes_and_ones(a, h):
    lane = lax.broadcasted_iota(jnp.int32, a.shape, 1)
    low = lane < HEAD_DIM
    return jnp.where(low if h % 2 == 0 else jnp.logical_not(low), a,
                     jnp.where(lane == _sum_lane(h), jnp.ones_like(a), jnp.zeros_like(a)))


def _attn_fwd(proj, fr, *, tq, n_heads, name, comm=None):
    B, L, _ = proj.shape
    AD = n_heads * HEAD_DIM
    nq = L // tq
    W = LANES
    scale = HEAD_DIM ** -0.5
    edges = _band_edges(tq)

    v_ones, sum_lane = _own_lanes_and_ones, _sum_lane

    def body(q_ref, k_ref, v_ref, fr_ref, o_ref, lse_ref, m_s, acc_s):
        qi, ki = pl.program_id(1), pl.program_id(2)

        @pl.when(ki == 0)
        def _():
            m_s[...] = jnp.full_like(m_s, NEG)
            acc_s[...] = jnp.zeros_like(acc_s)

        def tile(diagonal):
            lane = lax.broadcasted_iota(jnp.int32, (tq, W), 1)
            m_all = m_s[...]
            m_out = m_all
            bands = [(r0, r1, r1 if diagonal else tq) for r0, r1 in zip(edges[:-1], edges[1:])]
            if diagonal:
                masks = {r0: (lax.broadcasted_iota(jnp.int32, (r1 - r0, c1), 1)
                              <= r0 + lax.broadcasted_iota(jnp.int32, (r1 - r0, c1), 0)) for r0, r1, c1 in bands}

            def scores(h, band):
                r0, r1, c1 = band
                sl = slice(h * HEAD_DIM, (h + 1) * HEAD_DIM)
                return _dot_nt(q_ref[r0:r1, sl] * scale, k_ref[0:c1, sl])

            work = [(h, band) for h in range(n_heads) for band in bands]
            nxt = scores(*work[0])
            for w, (h, band) in enumerate(work):
                r0, r1, c1 = band
                sl = slice(h * HEAD_DIM, (h + 1) * HEAD_DIM)
                s = nxt - fr_ref[h:h + 1, 0:c1]
                if w + 1 < len(work):
                    nxt = scores(*work[w + 1])
                if diagonal:
                    s = jnp.where(masks[r0], s, NEG)
                m_old = m_all[r0:r1, h:h + 1]
                m_new = jnp.maximum(m_old, jnp.max(s, axis=1, keepdims=True))
                alpha = jnp.exp(m_old - m_new)
                p = jnp.exp(s - m_new)
                own = slice(h * 2 * HEAD_DIM, (h + 1) * 2 * HEAD_DIM)
                acc_s[r0:r1, own] = alpha * acc_s[r0:r1, own] + _dot(p.astype(BF16), v_ones(v_ref[0:c1, _pair(h)], h))
                if r0 == 0:
                    m_parts = []
                m_parts.append(m_new)
                if r1 == tq:
                    m_out = jnp.where(lane == h, jnp.concatenate(m_parts, axis=0), m_out)
            m_s[...] = m_out

        @pl.when(ki < qi)
        def _():
            tile(False)

        @pl.when(ki == qi)
        def _():
            tile(True)
            lane = lax.broadcasted_iota(jnp.int32, (tq, W), 1)
            low = lax.broadcasted_iota(jnp.int32, (tq, 2 * HEAD_DIM), 1) < HEAD_DIM
            l_all = jnp.ones((tq, W), F32)
            for h in range(0, n_heads, 2):
                even = acc_s[:, h * 2 * HEAD_DIM:(h + 1) * 2 * HEAD_DIM]
                odd = acc_s[:, (h + 1) * 2 * HEAD_DIM:(h + 2) * 2 * HEAD_DIM]
                l_even = even[:, sum_lane(h):sum_lane(h) + 1]
                l_odd = odd[:, sum_lane(h + 1):sum_lane(h + 1) + 1]
                o_ref[:, _pair(h)] = jnp.where(low, even / l_even, odd / l_odd)
                l_all = jnp.where(lane == h, l_even, jnp.where(lane == h + 1, l_odd, l_all))
            lse_ref[...] = jnp.where(lane < n_heads, m_s[...] + jnp.log(l_all), 0.0)

    kv = lambda b, qi, ki: jnp.minimum(ki, qi)
    return _launch(
        body, name=name, grid=(B, nq, nq), args=(proj, proj, proj, fr), comm=comm,
        in_specs=[pl.BlockSpec((None, tq, AD), lambda b, qi, ki: (b, qi, 3)),
                  pl.BlockSpec((None, tq, AD), lambda b, qi, ki: (b, kv(b, qi, ki), 4)),
                  pl.BlockSpec((None, tq, AD), lambda b, qi, ki: (b, kv(b, qi, ki), 5)),
                  pl.BlockSpec((None, None, n_heads, tq), lambda b, qi, ki: (b, kv(b, qi, ki), 0, 0))],
        out_specs=[pl.BlockSpec((None, tq, AD), lambda b, qi, ki: (b, qi, 0)),
                   pl.BlockSpec((None, tq, W), lambda b, qi, ki: (b, qi, 0))],
        out_shape=[jax.ShapeDtypeStruct((B, L, AD), F32), jax.ShapeDtypeStruct((B, L, W), F32)],
        scratch_shapes=[pltpu.VMEM((tq, W), F32), pltpu.VMEM((tq, n_heads * 2 * HEAD_DIM), F32)])


def _attn_bwd(proj, o, do, lse, fr, *, tq, n_heads, name, comm=None):
    B, L, _ = proj.shape
    AD = n_heads * HEAD_DIM
    nq = L // tq
    W = LANES
    HW = 2 * HEAD_DIM
    scale = HEAD_DIM ** -0.5
    edges = _band_edges(tq)

    def body(q_ref, k_ref, v_ref, o_ref, do_ref, lse_ref, fr_ref,
             dq_ref, dk_ref, dv_ref, dfk_ref, dfq_ref, dq_s, dk_s, dv_s):
        kj, qi = pl.program_id(1), pl.program_id(2)

        @pl.when((kj == 0) & (qi == 0))
        def _():
            dq_s[...] = jnp.zeros_like(dq_s)

        @pl.when(qi == kj)
        def _():
            dk_s[...] = jnp.zeros_like(dk_s)
            dv_s[...] = jnp.zeros_like(dv_s)

        def tile(diagonal):
            bands = [(r0, r1, r1) for r0, r1 in zip(edges[:-1], edges[1:])] if diagonal else [(0, tq, tq)]
            lse = lse_ref[...]
            for r0, r1, c1 in bands:
                nr = r1 - r0
                rows = pl.ds(pl.multiple_of(qi * tq + r0, 8), nr)
                if diagonal:
                    mask = (lax.broadcasted_iota(jnp.int32, (nr, c1), 1)
                            <= r0 + lax.broadcasted_iota(jnp.int32, (nr, c1), 0))
                def scores(h):
                    ps = _pair(h)
                    k = k_ref[0:c1, ps]
                    qs = q_ref[r0:r1, ps] * scale
                    dov = _own_lanes(do_ref[r0:r1, ps], h)
                    return _dot_nt(_own_lanes(qs, h), k), _dot_nt(dov, v_ref[0:c1, ps]), k, qs, dov

                nxt = scores(0)
                for h in range(n_heads):
                    ps = _pair(h)
                    own = slice(h * HW, (h + 1) * HW)
                    s, dp, k, qs, dov = nxt
                    if h + 1 < n_heads:
                        nxt = scores(h + 1)
                    s = s - fr_ref[h:h + 1, 0:c1]
                    if diagonal:
                        s = jnp.where(mask, s, NEG)
                    p = jnp.exp(s - lse[r0:r1, h:h + 1])
                    dsum = jnp.sum(dov.astype(F32) * o_ref[r0:r1, ps], axis=1, keepdims=True)
                    dsb = (p * (dp - dsum)).astype(BF16)
                    dv = _dot_tn(p.astype(BF16), dov)
                    dk_s[0:c1, own] += _dot_tn(dsb, _own_lanes_and_ones(qs, h))
                    dq_s[rows, own] += _dot(dsb, _own_lanes_and_ones(k, h))
                    if h % 2 == 0:
                        dv_even = dv
                    else:
                        dv_s[0:c1, ps] += dv_even + dv

        def compact(acc, data_scale):
            rows = acc.shape[0]
            low = lax.broadcasted_iota(jnp.int32, (rows, HW), 1) < HEAD_DIM
            lane = lax.broadcasted_iota(jnp.int32, (rows, W), 1)
            vals, sums = [], jnp.zeros((rows, W), F32)
            for h in range(0, n_heads, 2):
                even, odd = acc[:, h * HW:(h + 1) * HW], acc[:, (h + 1) * HW:(h + 2) * HW]
                vals.append(jnp.where(low, even, odd) * data_scale)
                sums = jnp.where(lane == h, even[:, _sum_lane(h):_sum_lane(h) + 1],
                                 jnp.where(lane == h + 1, odd[:, _sum_lane(h + 1):_sum_lane(h + 1) + 1], sums))
            return vals, sums

        @pl.when(qi > kj)
        def _():
            tile(False)

        @pl.when(qi == kj)
        def _():
            tile(True)
            rows = pl.ds(pl.multiple_of(qi * tq, 8), tq)
            vals, sums = compact(dq_s[rows, :], scale)
            for h in range(0, n_heads, 2):
                dq_ref[rows, _pair(h)] = vals[h // 2]
            dfq_ref[rows, :] = sums

        @pl.when(qi == nq - 1)
        def _():
            vals, sums = compact(dk_s[...], 1.0)
            for h in range(0, n_heads, 2):
                dk_ref[:, _pair(h)] = vals[h // 2].astype(BF16)
            dfk_ref[...] = sums
            dv_ref[...] = dv_s[...].astype(BF16)

    qq = lambda b, kj, qi: jnp.maximum(qi, kj)
    qblk = lambda w, cb: pl.BlockSpec((None, tq, w), lambda b, kj, qi: (b, qq(b, kj, qi), cb))
    kblk = lambda w, cb: pl.BlockSpec((None, tq, w), lambda b, kj, qi: (b, kj, cb))
    return _launch(
        body, name=name, grid=(B, nq, nq), args=(proj, proj, proj, o, do, lse, fr), comm=comm,
        in_specs=[qblk(AD, 3), kblk(AD, 4), kblk(AD, 5), qblk(AD, 0), qblk(AD, 0), qblk(W, 0),
                  pl.BlockSpec((None, None, n_heads, tq), lambda b, kj, qi: (b, kj, 0, 0))],
        out_specs=[pl.BlockSpec((None, L, AD), lambda b, kj, qi: (b, 0, 0)),
                   kblk(AD, 0), kblk(AD, 0), kblk(W, 0),
                   pl.BlockSpec((None, L, W), lambda b, kj, qi: (b, 0, 0))],
        out_shape=[jax.ShapeDtypeStruct((B, L, AD), F32), jax.ShapeDtypeStruct((B, L, AD), BF16),
                   jax.ShapeDtypeStruct((B, L, AD), BF16), jax.ShapeDtypeStruct((B, L, W), F32),
                   jax.ShapeDtypeStruct((B, L, W), F32)],
        scratch_shapes=[pltpu.VMEM((L, n_heads * HW), F32), pltpu.VMEM((tq, n_heads * HW), F32),
                        pltpu.VMEM((tq, AD), F32)])


def _mix_gather(refs, first):
    b_ref, c_ref, hc_ref, cp_ref, hcp_ref, o_ref, cw_ref, p_ref = refs
    bg = b_ref[...].astype(F32)
    u = c_ref[...].astype(F32) * hc_ref[...].astype(F32)
    prev = cp_ref[...].astype(F32) * hcp_ref[...].astype(F32)
    prev = jnp.where(first, 0.0, prev)
    cv, u1, u2 = _causal_conv(u, prev, cw_ref[...])
    yc = bg * cv
    p = p_ref[...]
    rc = lax.rsqrt(_group_mean(yc * yc, p) + EPS)
    ya = o_ref[...].astype(F32)
    ra = lax.rsqrt(_group_mean(ya * ya, p) + EPS)
    return bg, (u, u1, u2), cv, yc * rc, rc, ya * ra, ra


def _mix_specs(tm, CD):
    per = tm // HALO
    cur = lambda cb: pl.BlockSpec((None, tm, CD), lambda b, i: (b, i, cb))
    prev = lambda cb: pl.BlockSpec((None, HALO, CD), lambda b, i: (b, jnp.maximum(i * per - 1, 0), cb))
    return [cur(0), cur(1), cur(2), prev(1), prev(2), cur(0)]


def _mix_out(proj, o, cw, gc, ga, wout, h, pmat, next_gain, *, tm, name, comm=None):
    B, L, D = h.shape
    CD = o.shape[-1]
    const = lambda b, i: (0, 0)

    def body(b_ref, c_ref, hc_ref, cp_ref, hcp_ref, o_ref, cw_ref, p_ref, gc_ref, ga_ref, w_ref, h_ref, ng_ref,
             out_ref, y_ref, n_ref):
        first = pl.program_id(1) == 0
        _, _, _, zc, _, za, _ = _mix_gather((b_ref, c_ref, hc_ref, cp_ref, hcp_ref, o_ref, cw_ref, p_ref), first)
        yc = (zc * gc_ref[...]).astype(BF16)
        ya = (za * ga_ref[...]).astype(BF16)
        y_ref[:, :CD] = yc
        y_ref[:, CD:] = ya
        out = h_ref[...] + _dot(yc, w_ref[:CD, :]) + _dot(ya, w_ref[CD:, :])
        out_ref[...] = out
        n_ref[...] = _rms(out, ng_ref[...])

    tile = pl.BlockSpec((None, tm, D), lambda b, i: (b, i, 0))
    return _launch(
        body, name=name, grid=(B, L // tm),
        in_specs=_mix_specs(tm, CD)
                 + [pl.BlockSpec(cw.shape, const), pl.BlockSpec(pmat.shape, const),
                    pl.BlockSpec((1, CD), const), pl.BlockSpec((1, CD), const), pl.BlockSpec((D, D), const),
                    tile, pl.BlockSpec((1, D), const)],
        out_specs=[tile, tile, tile],
        out_shape=[jax.ShapeDtypeStruct((B, L, D), F32), jax.ShapeDtypeStruct((B, L, D), BF16),
                   jax.ShapeDtypeStruct((B, L, D), BF16)],
        args=(proj, proj, proj, proj, proj, o, cw, pmat, gc, ga, wout, h, next_gain), comm=comm)


def _mix_out_bwd(dhb, proj, o, cw, gc, ga, wout, pmat, *, tm, name, comm=None):
    B, L, D = dhb.shape
    CD = o.shape[-1]
    const = lambda b, i: (0, 0)

    def body(dh_ref, b_ref, c_ref, hc_ref, cp_ref, hcp_ref, o_ref, cw_ref, p_ref, gc_ref, ga_ref, w_ref,
             db_ref, dcv_ref, do_ref, dgc_ref, dga_ref, dcw_ref):
        first = pl.program_id(1) == 0

        @pl.when((pl.program_id(0) == 0) & first)
        def _():
            dgc_ref[...] = jnp.zeros_like(dgc_ref)
            dga_ref[...] = jnp.zeros_like(dga_ref)
            dcw_ref[...] = jnp.zeros_like(dcw_ref)

        bg, us, cv, zc, rc, za, ra = _mix_gather(
            (b_ref, c_ref, hc_ref, cp_ref, hcp_ref, o_ref, cw_ref, p_ref), first)
        p = p_ref[...]
        dh = dh_ref[...]
        dyc = _dot_nt(dh, w_ref[:CD, :])
        dya = _dot_nt(dh, w_ref[CD:, :])

        dgc_ref[...] += jnp.sum(dyc * zc, axis=0, keepdims=True)
        dz = dyc * gc_ref[...]
        dx = rc * (dz - zc * _group_mean(dz * zc, p))
        db_ref[...] = (dx * cv).astype(BF16)
        dcv = dx * bg
        dcv_ref[...] = dcv.astype(BF16)
        for k in range(3):
            dcw_ref[k:k + 1, :] += jnp.sum(dcv * us[2 - k], axis=0, keepdims=True)

        dga_ref[...] += jnp.sum(dya * za, axis=0, keepdims=True)
        dz = dya * ga_ref[...]
        do_ref[...] = (ra * (dz - za * _group_mean(dz * za, p))).astype(BF16)

    tile = lambda w: pl.BlockSpec((None, tm, w), lambda b, i: (b, i, 0))
    return _launch(
        body, name=name, grid=(B, L // tm), comm=comm,
        args=(dhb, proj, proj, proj, proj, proj, o, cw, pmat, gc, ga, wout),
        in_specs=[tile(D)] + _mix_specs(tm, CD)
                 + [pl.BlockSpec(cw.shape, const), pl.BlockSpec(pmat.shape, const),
                    pl.BlockSpec((1, CD), const), pl.BlockSpec((1, CD), const), pl.BlockSpec((D, D), const)],
        out_specs=[tile(CD), tile(CD), tile(CD),
                   pl.BlockSpec((1, CD), const), pl.BlockSpec((1, CD), const), pl.BlockSpec((8, CD), const)],
        out_shape=[jax.ShapeDtypeStruct((B, L, CD), BF16)] * 3
                  + [jax.ShapeDtypeStruct((1, CD), F32)] * 2 + [jax.ShapeDtypeStruct((8, CD), F32)])


def _conv_bwd(dcv, proj, cw, *, tm, name):
    B, L, CD = dcv.shape
    per = tm // HALO
    nhalo = L // HALO
    nt = L // tm

    def body(d_ref, dn_ref, c_ref, hc_ref, cw_ref, out_ref):
        last = pl.program_id(1) == nt - 1
        d = d_ref[...].astype(F32)
        nxt = jnp.where(last, 0.0, dn_ref[...].astype(F32))
        n0, n1 = _row_of(nxt, 0), _row_of(nxt, 1)
        rows = lax.broadcasted_iota(jnp.int32, d.shape, 0)
        d1 = jnp.where(rows == tm - 1, n0, pltpu.roll(d, tm - 1, 0))
        d2 = jnp.where(rows == tm - 2, n0, jnp.where(rows == tm - 1, n1, pltpu.roll(d, tm - 2, 0)))
        w = cw_ref[...]
        du = w[2:3, :] * d + w[1:2, :] * d1 + w[0:1, :] * d2
        out_ref[:, :CD] = (du * hc_ref[...].astype(F32)).astype(BF16)
        out_ref[:, CD:] = (du * c_ref[...].astype(F32)).astype(BF16)

    return pl.pallas_call(
        body, name=name, grid=(B, nt),
        in_specs=[pl.BlockSpec((None, tm, CD), lambda b, i: (b, i, 0)),
                  pl.BlockSpec((None, HALO, CD), lambda b, i: (b, jnp.minimum((i + 1) * per, nhalo - 1), 0)),
                  pl.BlockSpec((None, tm, CD), lambda b, i: (b, i, 1)),
                  pl.BlockSpec((None, tm, CD), lambda b, i: (b, i, 2)),
                  pl.BlockSpec(cw.shape, lambda b, i: (0, 0))],
        out_specs=pl.BlockSpec((None, tm, 2 * CD), lambda b, i: (b, i, 0)),
        out_shape=jax.ShapeDtypeStruct((B, L, 2 * CD), BF16),
        compiler_params=_params("arbitrary", "arbitrary"),
    )(dcv, dcv, proj, proj, cw)


def _place():
    x, y, c = lax.axis_index("x"), lax.axis_index("y"), lax.axis_index("c")
    others = [(1 - x, y), (x, 1 - y), (1 - x, 1 - y)]
    return x, y, c, others


def _all_gather_shards(shards, *, name):
    n = len(shards)

    def body(*refs):
        ins, outs = refs[:n], refs[n:2 * n]
        send, recv, fsend, frecv, lsem = refs[2 * n:]
        x, y, c, others = _place()
        me = 2 * x + y
        local = [pltpu.make_async_copy(ins[t], outs[t].at[me], lsem.at[t]) for t in range(n)]
        for cp in local:
            cp.start()

        def half(t, k):
            hr = shards[t].shape[0] // 2
            return pl.ds(pl.multiple_of(k * hr, HALO), hr)

        def ici(t, j, src_chip, to):
            src = ins[t].at[half(t, c)] if to is not None else outs[t].at[src_chip, half(t, c)]
            return pltpu.make_async_remote_copy(
                src_ref=src, dst_ref=outs[t].at[src_chip, half(t, c)],
                send_sem=send.at[3 * t + j], recv_sem=recv.at[3 * t + j],
                device_id=(x, y, c) if to is None else to, device_id_type=MESH)

        def d2d(t, j, src_chip, k):
            return pltpu.make_async_remote_copy(
                src_ref=outs[t].at[src_chip, half(t, k)], dst_ref=outs[t].at[src_chip, half(t, k)],
                send_sem=fsend.at[3 * t + j], recv_sem=frecv.at[3 * t + j],
                device_id=(x, y, 1 - c), device_id_type=MESH)

        firsts = [ici(t, j, me, (ox, oy, c)) for t in range(n) for j, (ox, oy) in enumerate(others)]
        for cp in firsts:
            cp.start()
        passed = []
        for t in range(n):
            for j, (ox, oy) in enumerate(others):
                ici(t, j, 2 * ox + oy, None).wait_recv()
                cp = d2d(t, j, 2 * ox + oy, c)
                cp.start()
                passed.append(cp)
        for t in range(n):
            for j, (ox, oy) in enumerate(others):
                d2d(t, j, 2 * ox + oy, 1 - c).wait_recv()
        for cp in firsts + passed:
            cp.wait_send()
        for cp in local:
            cp.wait()

    return pl.pallas_call(
        body, name=name,
        in_specs=[ANY] * n, out_specs=[ANY] * n,
        out_shape=[jax.ShapeDtypeStruct((N_SHARD,) + s.shape, s.dtype) for s in shards],
        scratch_shapes=[pltpu.SemaphoreType.DMA((3 * n,))] * 4 + [pltpu.SemaphoreType.DMA((n,))],
    )(*shards)


def _all_reduce_small(slab, *, name):
    def body(in_ref, out_ref, gath, send, recv):
        x, y, c, _ = _place()
        me = 4 * x + 2 * y + c
        gath[me] = in_ref[...]
        copies, peers = [], []
        for m in range(1, N_DEV):
            px = jnp.where((m >> 2) & 1, 1 - x, x)
            py = jnp.where((m >> 1) & 1, 1 - y, y)
            pc = jnp.where(m & 1, 1 - c, c)
            cp = pltpu.make_async_remote_copy(
                src_ref=in_ref, dst_ref=gath.at[me], send_sem=send.at[m - 1], recv_sem=recv.at[m - 1],
                device_id=(px, py, pc), device_id_type=MESH)
            cp.start()
            copies.append(cp)
            peers.append(4 * px + 2 * py + pc)
        for m in range(1, N_DEV):
            pltpu.make_async_remote_copy(
                src_ref=in_ref, dst_ref=gath.at[peers[m - 1]], send_sem=send.at[m - 1], recv_sem=recv.at[m - 1],
                device_id=(x, y, c), device_id_type=MESH).wait_recv()
        for cp in copies:
            cp.wait_send()
        acc = gath[0]
        for k in range(1, N_DEV):
            acc = acc + gath[k]
        out_ref[...] = acc

    vm = pl.BlockSpec(memory_space=pltpu.VMEM)
    return pl.pallas_call(
        body, name=name, in_specs=[vm], out_specs=vm,
        out_shape=jax.ShapeDtypeStruct(slab.shape, slab.dtype),
        scratch_shapes=[pltpu.VMEM((N_DEV,) + slab.shape, slab.dtype),
                        pltpu.SemaphoreType.DMA((N_DEV - 1,)), pltpu.SemaphoreType.DMA((N_DEV - 1,))],
    )(slab)


def _gather_stage(shards, into, *, ici=(), d2d=()):
    n = len(shards) if into is None else len(into)
    ns = len(shards) if ici else 0
    ni, nd = max(len(ici), 1), max(len(d2d), 1)
    shapes = [s.shape for s in shards] if into is None else [p.shape[1:] for p in into]
    dtypes = [s.dtype for s in shards] if into is None else [p.dtype for p in into]

    def copies(ins, outs, sems, sending):
        x, y, c, others = _place()
        me = 2 * x + y
        out = []
        for t in range(n):
            hr = shapes[t][0] // 2
            mine = pl.ds(pl.multiple_of(c * hr, HALO), hr)
            theirs = pl.ds(pl.multiple_of((1 - c) * hr, HALO), hr)
            for a, j in enumerate(ici):
                ox, oy = others[j]
                src_chip = me if sending else 2 * ox + oy
                out.append(pltpu.make_async_remote_copy(
                    src_ref=ins[t].at[mine], dst_ref=outs[t].at[src_chip, mine],
                    send_sem=sems[0].at[ni * t + a], recv_sem=sems[1].at[ni * t + a],
                    device_id=(ox, oy, c) if sending else (x, y, c), device_id_type=MESH))
            for a, j in enumerate(d2d):
                ox, oy = others[j]
                blk = outs[t].at[2 * ox + oy, mine if sending else theirs]
                out.append(pltpu.make_async_remote_copy(
                    src_ref=blk, dst_ref=blk, send_sem=sems[2].at[nd * t + a], recv_sem=sems[3].at[nd * t + a],
                    device_id=(x, y, 1 - c) if sending else (x, y, c), device_id_type=MESH))
        return out

    def local(ins, outs, sems):
        if into is not None:
            return []
        x, y, _, _ = _place()
        return [pltpu.make_async_copy(ins[t], outs[t].at[2 * x + y], sems[4].at[t]) for t in range(n)]

    def start(ins, outs, sems):
        for cp in local(ins, outs, sems) + copies(ins, outs, sems, True):
            cp.start()

    def finish(ins, outs, sems):
        for cp in copies(ins, outs, sems, False):
            cp.wait_recv()
        for cp in copies(ins, outs, sems, True):
            cp.wait_send()
        for cp in local(ins, outs, sems):
            cp.wait()

    return _Comm((list(shards) if ici or into is None else []) + (list(into) if into is not None else []),
                 [jax.ShapeDtypeStruct((N_SHARD,) + tuple(sh), dt) for sh, dt in zip(shapes, dtypes)],
                 [ni * n, ni * n, nd * n, nd * n, n], start, finish,
                 aliases=None if into is None else {ns + t: t for t in range(n)})


def _gather_ici(shards):
    return _gather_stage(shards, None, ici=(0, 1, 2))


def _gather_d2d(parts):
    return _gather_stage((), parts, d2d=(0, 1, 2))


def _swap_halves(grads):
    n = len(grads)

    def copies(ins, outs, sems):
        x, y, c, _ = _place()
        out = []
        for t in range(n):
            hr = grads[t].shape[1] // 2
            rows = pl.ds(pl.multiple_of((1 - c) * hr, 8), hr)
            out.append(pltpu.make_async_remote_copy(
                src_ref=ins[t].at[:, rows, :], dst_ref=outs[t], send_sem=sems[0].at[t], recv_sem=sems[1].at[t],
                device_id=(x, y, 1 - c), device_id_type=MESH))
        return out

    def start(ins, outs, sems):
        for cp in copies(ins, outs, sems):
            cp.start()

    def finish(ins, outs, sems):
        for cp in copies(ins, outs, sems):
            cp.wait()

    return _Comm(grads, [jax.ShapeDtypeStruct((N_SHARD, g.shape[1] // 2, g.shape[2]), g.dtype) for g in grads],
                 [n, n], start, finish)


def _pair_sum(g, got, c, *, name):
    ns, R, C = g.shape
    hr = R // 2

    def body(c_ref, g_ref, r_ref, o_ref):
        o_ref[...] = (g_ref[...] + r_ref[...]).astype(BF16)

    return pl.pallas_call(
        body, name=name,
        grid_spec=pltpu.PrefetchScalarGridSpec(
            num_scalar_prefetch=1, grid=(ns,),
            in_specs=[pl.BlockSpec((None, hr, C), lambda s, cr: (s, cr[0], 0)),
                      pl.BlockSpec((None, hr, C), lambda s, cr: (s, 0, 0))],
            out_specs=pl.BlockSpec((None, hr, C), lambda s, cr: (s, 0, 0))),
        out_shape=jax.ShapeDtypeStruct((ns, hr, C), BF16),
        compiler_params=_params("arbitrary"),
    )(c, g, got)


def _scatter_chips(sums):
    n = len(sums)

    def copies(ins, outs, sems, sending):
        x, y, c, others = _place()
        me = 2 * x + y
        out = []
        for t in range(n):
            for j, (ox, oy) in enumerate(others):
                there = 2 * ox + oy
                out.append(pltpu.make_async_remote_copy(
                    src_ref=ins[t].at[there if sending else me], dst_ref=outs[t].at[me if sending else there],
                    send_sem=sems[0].at[3 * t + j], recv_sem=sems[1].at[3 * t + j],
                    device_id=(ox, oy, c) if sending else (x, y, c), device_id_type=MESH))
        return out

    def start(ins, outs, sems):
        for cp in copies(ins, outs, sems, True):
            cp.start()

    def finish(ins, outs, sems):
        for cp in copies(ins, outs, sems, False):
            cp.wait_recv()
        for cp in copies(ins, outs, sems, True):
            cp.wait_send()

    return _Comm(sums, [jax.ShapeDtypeStruct(s.shape, s.dtype) for s in sums], [3 * n, 3 * n], start, finish)


def _chip_sum(g, got, landed, idx, *, name):
    ns, R, C = g.shape
    hr = R // 2

    def body(i_ref, g_ref, r_ref, a_ref, b_ref, c_ref, o_ref):
        acc = g_ref[...] + r_ref[...]
        for ref in (a_ref, b_ref, c_ref):
            acc = acc + ref[...].astype(F32)
        o_ref[...] = acc

    other = lambda k: pl.BlockSpec((None, hr, C), lambda s, ir: (ir[2 + k], 0, 0))
    return pl.pallas_call(
        body, name=name,
        grid_spec=pltpu.PrefetchScalarGridSpec(
            num_scalar_prefetch=1, grid=(1,),
            in_specs=[pl.BlockSpec((None, hr, C), lambda s, ir: (ir[0], ir[1], 0)),
                      pl.BlockSpec((None, hr, C), lambda s, ir: (ir[0], 0, 0)),
                      other(0), other(1), other(2)],
            out_specs=pl.BlockSpec((hr, C), lambda s, ir: (ir[1], 0))),
        out_shape=jax.ShapeDtypeStruct((R, C), F32),
        compiler_params=_params("arbitrary"),
    )(idx, g, got, landed, landed, landed)


def _share_halves(halves):
    n = len(halves)

    def copies(outs, sems, sending):
        x, y, c, _ = _place()
        out = []
        for t in range(n):
            hr = halves[t].shape[0] // 2
            rows = pl.ds(pl.multiple_of((c if sending else 1 - c) * hr, 8), hr)
            out.append(pltpu.make_async_remote_copy(
                src_ref=outs[t].at[rows, :], dst_ref=outs[t].at[rows, :], send_sem=sems[0].at[t],
                recv_sem=sems[1].at[t], device_id=(x, y, 1 - c) if sending else (x, y, c), device_id_type=MESH))
        return out

    def start(ins, outs, sems):
        for cp in copies(outs, sems, True):
            cp.start()

    def finish(ins, outs, sems):
        for cp in copies(outs, sems, False):
            cp.wait_recv()
        for cp in copies(outs, sems, True):
            cp.wait_send()

    return _Comm(halves, [jax.ShapeDtypeStruct(h.shape, h.dtype) for h in halves], [n, n], start, finish,
                 aliases={t: t for t in range(n)})


def _adamw(w, g, m, v, *, name):
    R, C = w.shape
    tr = R
    for cand in (256, 128, 64, 32, 16, 8):
        if R % cand == 0:
            tr = cand
            break

    def body(w_ref, g_ref, m_ref, v_ref, go_ref, d_ref, mo_ref, vo_ref):
        gv = g_ref[...]
        go_ref[...] = gv
        mn = ADAM_B1 * m_ref[...] + (1.0 - ADAM_B1) * gv
        vn = ADAM_B2 * v_ref[...] + (1.0 - ADAM_B2) * (gv * gv)
        m_hat = mn / (1.0 - ADAM_B1 ** ADAM_STEP)
        v_hat = vn / (1.0 - ADAM_B2 ** ADAM_STEP)
        d_ref[...] = -ADAM_LR * (m_hat / (jnp.sqrt(v_hat) + ADAM_EPS) + ADAM_WD * w_ref[...])
        mo_ref[...] = mn
        vo_ref[...] = vn

    blk = pl.BlockSpec((tr, C), lambda i: (i, 0))
    return pl.pallas_call(
        body, name=name, grid=(R // tr,), in_specs=[blk] * 4, out_specs=[blk] * 4,
        out_shape=[jax.ShapeDtypeStruct((R, C), F32)] * 4,
        compiler_params=_params("arbitrary"),
    )(w, g, m, v)


def _pack_small(D, meta, n1, nm, n3, nf, gc, ga, bf, cw):
    def row(a):
        a = a.reshape(-1, a.shape[-1])
        return jnp.pad(a, ((0, 0), (0, D - a.shape[-1])))
    rows = [row(meta), row(n1), row(nm), row(n3), row(nf), row(jnp.concatenate([gc, ga], axis=-1)), row(bf), row(cw)]
    slab = jnp.concatenate(rows, axis=0)
    return jnp.pad(slab, ((0, SMALL_ROWS - slab.shape[0]), (0, 0)))


def _unpack_small(slab, like):
    meta, n1, nm, n3, nf, gc, ga, bf, cw = like
    nmeta, mc = meta.shape
    out = [slab[:nmeta, :mc].reshape(meta.shape)]
    r = nmeta
    for a in (n1, nm, n3, nf):
        out.append(slab[r, :a.shape[-1]].reshape(a.shape))
        r += 1
    cd = gc.shape[-1]
    out.append(slab[r, :cd].reshape(gc.shape))
    out.append(slab[r, cd:cd + ga.shape[-1]].reshape(ga.shape))
    r += 1
    out.append(slab[r, :bf.shape[-1]].reshape(bf.shape))
    r += 1
    out.append(slab[r:r + 3, :cw.shape[-1]].reshape(cw.shape))
    return out


def kernel(x, meta_tokens, ffn1_norm, ffn1_w_gu, ffn1_w_down, mix_norm, w_in, conv_w, b_f, out_norm_conv, out_norm_attn, w_out, ffn2_norm, ffn2_w_gu, ffn2_w_down, final_norm, loss_target, m_meta_tokens, m_ffn1_norm, m_ffn1_w_gu, m_ffn1_w_down, m_mix_norm, m_w_in, m_conv_w, m_b_f, m_out_norm_conv, m_out_norm_attn, m_w_out, m_ffn2_norm, m_ffn2_w_gu, m_ffn2_w_down, m_final_norm, v_meta_tokens, v_ffn1_norm, v_ffn1_w_gu, v_ffn1_w_down, v_mix_norm, v_w_in, v_conv_w, v_b_f, v_out_norm_conv, v_out_norm_attn, v_w_out, v_ffn2_norm, v_ffn2_w_gu, v_ffn2_w_down, v_final_norm):
    B, S, D = x.shape
    L = S + N_META
    T = B * L
    tm = L // 3
    assert tm * 3 == L and tm % HALO == 0
    tm2 = 2 * tm
    assert T % tm2 == 0
    guc = ffn1_w_gu.shape[-1]
    ff = N_SHARD * guc // 2
    H = b_f.shape[-1]
    AD = H * HEAD_DIM
    CD = conv_w.shape[-1] * N_SHARD
    assert CD == AD and CD + AD == D and CD % LANES == 0
    n_main = 3 * CD + 3 * AD
    ins = w_in.shape[-1]

    xi, yi, ci = lax.axis_index("x"), lax.axis_index("y"), lax.axis_index("c")
    chip = 2 * xi + yi

    small_shard = jnp.zeros((2 * HALO, meta_tokens.shape[-1]), F32)
    small_shard = small_shard.at[:N_META].set(meta_tokens)
    small_shard = small_shard.at[N_META:N_META + 3, :conv_w.shape[-1]].set(conv_w[0])
    big = [ffn1_w_gu[0], ffn1_w_down[0], w_in[0], w_out[0], ffn2_w_gu[0], ffn2_w_down[0]]
    wgu1_s, wd1_s, win_s, wout_s, wgu2_s, wd2_s = [w.astype(BF16) for w in big]
    small_g, = _all_gather_shards([small_shard], name="gather_small")
    meta_f = jnp.moveaxis(small_g[:, :N_META], 0, 1).reshape(N_META, D)
    cw_f = jnp.moveaxis(small_g[:, N_META:N_META + 3, :conv_w.shape[-1]], 0, 1).reshape(3, CD)
    cw8 = jnp.pad(cw_f, ((0, 5), (0, 0)))
    bf_p = jnp.pad(b_f, ((0, 0), (0, LANES - H)))
    gid = jnp.arange(CD) // HEAD_DIM
    pmat = jnp.where(gid[:, None] == gid[None, :], 1.0 / HEAD_DIM, 0.0).astype(BF16)

    gu_shape = jax.ShapeDtypeStruct((2, T, ff), BF16)
    gu_w_spec = pl.BlockSpec((None, D, guc), lambda s, i: (s, 0, 0))
    gu_o_spec = pl.BlockSpec((None, tm2, guc), lambda s, i: (s // 2, i, s % 2))

    sid = jnp.bitwise_xor(chip, jnp.array([0, 2, 1, 3], jnp.int32)).astype(jnp.int32)
    (h0, n1), wgu1_h = _embed_norm(x, meta_f, ffn1_norm, tm=tm, name="embed_norm",
                                   comm=_gather_stage([wgu1_s], None, ici=(0, 1)))
    gu1, wgu1_h = _ffn_up(n1, wgu1_s[None], sid, None, tm=tm2, first=0, count=1, name="ffn1_up_own",
                          comm=_gather_stage([wgu1_s], wgu1_h, ici=(2,), d2d=(0, 1)))
    gu1, out = _ffn_up(n1, wgu1_h[0], sid, gu1, tm=tm2, first=1, count=2, name="ffn1_up_near",
                       comm=_join(_gather_stage((), wgu1_h, d2d=(2,)), _gather_ici([wd1_s, wout_s])))
    wgu1, down_w = out[0], out[1:]
    gu1, (wd1, wout_g) = _ffn_up(n1, wgu1, sid, gu1, tm=tm2, first=3, count=1, name="ffn1_up_far",
                                 comm=_gather_d2d(down_w))
    wd1 = wd1.reshape(ff, D)
    (h1, n2), win_h = _ffn_down(gu1, wd1, h0, mix_norm, tm=tm, name="ffn1_down", comm=_gather_ici([win_s]))
    win_g, = _run_comm(_gather_d2d(win_h), name="gather_w_in")
    wout_f = wout_g.reshape(D, D)
    win_f = jnp.moveaxis(win_g, 0, 1).reshape(D, N_SHARD * ins)
    win_main = win_f[:, :n_main]
    win_fg = jnp.pad(win_f[:, n_main:], ((0, 0), (0, LANES - H)))

    proj, _ = _matmul_nn(n2, win_main, tm=tm2, nb=n_main // (3 * CD),
                         w_spec=pl.BlockSpec((D, 3 * CD), lambda s, i: (0, s)),
                         out_shape=jax.ShapeDtypeStruct((T, n_main), BF16),
                         out_spec=pl.BlockSpec((tm2, 3 * CD), lambda s, i: (i, s)), name="mix_in")
    fg, _ = _matmul_nn(n2, win_fg, tm=tm, nb=1, w_spec=pl.BlockSpec((D, LANES), lambda s, i: (0, 0)),
                       out_shape=jax.ShapeDtypeStruct((T, LANES), F32),
                       out_spec=pl.BlockSpec((tm, LANES), lambda s, i: (i, 0)), name="mix_in_fg")
    proj3 = proj.reshape(B, L, n_main)
    fg3 = fg.reshape(B, L, LANES)
    fc = _fcum(fg3, bf_p, ch=tm, name="forget_cumsum")
    fr = fc[:, :, :H].reshape(B, L // tm, tm, H).transpose(0, 1, 3, 2)
    (o, lse), ffn2_w = _attn_fwd(proj3, fr, tq=tm, n_heads=H, name="attn_fwd",
                                 comm=_gather_ici([wgu2_s, wd2_s]))
    (h2, ymix, n3), (wgu2, wd2) = _mix_out(
        proj3, o, cw8, out_norm_conv, out_norm_attn, wout_f, h1.reshape(B, L, D), pmat, ffn2_norm,
        tm=tm, name="mix_out", comm=_gather_d2d(ffn2_w))
    wd2 = wd2.reshape(ff, D)
    h2 = h2.reshape(T, D)
    n3 = n3.reshape(T, D)

    gu2, _ = _matmul_nn(n3, wgu2, tm=tm2, nb=N_SHARD, w_spec=gu_w_spec, out_shape=gu_shape, out_spec=gu_o_spec,
                        name="ffn2_up")
    (dh3f, dh3b, d_gf, loss_part), _ = _ffn_down_loss(gu2, wd2, h2, final_norm.reshape(1, D), loss_target,
                                                      tm=tm, name="ffn2_down_loss")

    c_arr = jnp.reshape(ci, (1,)).astype(jnp.int32)
    ks = jnp.arange(N_SHARD - 1, dtype=jnp.int32)
    idx = jnp.concatenate([jnp.stack([chip, ci]).astype(jnp.int32), ks + (ks >= chip).astype(jnp.int32)])

    def pair_sums(grads, got, names):
        return [_pair_sum(g, r, c_arr, name="pair_sum_" + nm) for g, r, nm in zip(grads, got, names)]

    def chip_sums(grads, got, landed, names):
        return [_chip_sum(g, r, l, idx, name="chip_sum_" + nm) for g, r, l, nm in zip(grads, got, landed, names)]

    def dw_up(n, dgu, name, comm=None):
        return _matmul_tn(
            n, dgu, tm=L, nb=N_SHARD, kb=D, x_spec=pl.BlockSpec((L, D), lambda s, i: (i, 0)),
            y_spec=pl.BlockSpec((None, L, guc), lambda s, i: (s // 2, i, s % 2)),
            out_shape=jax.ShapeDtypeStruct((N_SHARD, D, guc), F32),
            out_spec=pl.BlockSpec((None, D, guc), lambda s, i: (s, 0, 0)), name=name, comm=comm)

    (dgu2, d_wd2), _ = _ffn_bwd_act(dh3b, gu2, wd2, tm=tm, guc=guc, name="ffn2_bwd_act")
    (dh2, dh2b, d_g3), _ = _ffn_bwd_in(dgu2, wgu2, h2, ffn2_norm, dh3f, tm=tm, scale=1.0, name="ffn2_bwd_in")
    d_wgu2, _ = dw_up(n3, dgu2, "ffn2_dw_up")
    grads_f2 = [d_wgu2, d_wd2.reshape(N_SHARD, ff // N_SHARD, D)]
    names_f2 = ["wgu2", "wd2"]

    dh2b3 = dh2b.reshape(B, L, D)
    (d_bg, d_cv, d_o, d_gc, d_ga, d_cw), got_f2 = _mix_out_bwd(
        dh2b3, proj3, o, cw8, out_norm_conv, out_norm_attn, wout_f, pmat, tm=tm, name="mix_out_bwd",
        comm=_swap_halves(grads_f2))
    sums_f2 = pair_sums(grads_f2, got_f2, names_f2)
    d_wout, _ = _matmul_tn(
        ymix.reshape(T, D), dh2b, tm=tm2, nb=1, kb=D,
        x_spec=pl.BlockSpec((tm2, D), lambda s, i: (i, 0)), y_spec=pl.BlockSpec((tm2, D), lambda s, i: (i, 0)),
        out_shape=jax.ShapeDtypeStruct((D, D), F32), out_spec=pl.BlockSpec((D, D), lambda s, i: (0, 0)),
        name="dw_out")
    d_cc = _conv_bwd(d_cv, proj3, cw8, tm=tm, name="conv_bwd")
    (d_q, d_k, d_v, d_fk, d_fq), landed_f2 = _attn_bwd(proj3, o, d_o, lse, fr, tq=tm, n_heads=H, name="attn_bwd",
                                                       comm=_scatter_chips(sums_f2))
    halves_f2 = chip_sums(grads_f2, got_f2, landed_f2, names_f2)
    d_fc = d_fq - d_fk
    d_fg, d_bf = _fcum_bwd(d_fc, fg3, bf_p, ch=tm, name="forget_cumsum_bwd")

    parts = [d_bg.reshape(T, CD), d_cc.reshape(T, 2 * CD), d_q.reshape(T, AD), d_k.reshape(T, AD),
             d_v.reshape(T, AD), d_fg.reshape(T, LANES)]
    (dh1, dh1b, d_gm, d_proj), g_f2 = _mix_bwd_in(parts, win_main, win_fg, h1, mix_norm, dh2, tm=tm, scale=0.5,
                                                  name="mix_bwd_in", comm=_share_halves(halves_f2))
    wide = d_proj.shape[1]
    d_win_nat, _ = _matmul_tn(
        n2, d_proj, tm=tm, nb=1, kb=D,
        x_spec=pl.BlockSpec((tm, D), lambda s, i: (i, 0)), y_spec=pl.BlockSpec((tm, wide), lambda s, i: (i, 0)),
        out_shape=jax.ShapeDtypeStruct((D, wide), F32), out_spec=pl.BlockSpec((D, wide), lambda s, i: (0, 0)),
        name="dw_in")
    d_win = jnp.moveaxis(d_win_nat[:, :N_SHARD * ins].reshape(D, N_SHARD, ins), 1, 0)
    grads_mx = [d_win, d_wout.reshape(N_SHARD, D // N_SHARD, D)]
    names_mx = ["win", "wout"]

    (dgu1, d_wd1), got_mx = _ffn_bwd_act(dh1b, gu1, wd1, tm=tm, guc=guc, name="ffn1_bwd_act",
                                         comm=_swap_halves(grads_mx))
    sums_mx = pair_sums(grads_mx, got_mx, names_mx)
    grads_d1 = [d_wd1.reshape(N_SHARD, ff // N_SHARD, D)]
    d_wgu1, out = dw_up(n1, dgu1, "ffn1_dw_up", comm=_join(_scatter_chips(sums_mx), _swap_halves(grads_d1)))
    landed_mx, got_d1 = out[:2], out[2:]
    halves_mx = chip_sums(grads_mx, got_mx, landed_mx, names_mx)
    sums_d1 = pair_sums(grads_d1, got_d1, ["wd1"])
    grads_u1 = [d_wgu1]
    (grad_x, d_meta, d_g1), out = _ffn_bwd_in_first(
        dgu1, wgu1, h0, ffn1_norm, dh1, tm=tm, batch=B, name="ffn1_bwd_in",
        comm=_join(_join(_share_halves(halves_mx), _scatter_chips(sums_d1)), _swap_halves(grads_u1)))
    g_mx, landed_d1, got_u1 = out[:2], out[2:3], out[3:]
    halves_d1 = chip_sums(grads_d1, got_d1, landed_d1, ["wd1"])
    sums_u1 = pair_sums(grads_u1, got_u1, ["wgu1"])
    out = _run_comm(_join(_share_halves(halves_d1), _scatter_chips(sums_u1)), name="scatter_ffn1")
    g_d1, landed_u1 = out[:1], out[1:]
    halves_u1 = chip_sums(grads_u1, got_u1, landed_u1, ["wgu1"])
    g_u1 = _run_comm(_share_halves(halves_u1), name="share_ffn1")
    g_big = [g_u1[0], g_d1[0], g_mx[0], g_mx[1], g_f2[0], g_f2[1]]

    loss_row = jnp.zeros((1, D), F32).at[0, 0].set(loss_part[0, 0])
    slab = _pack_small(D, d_meta, d_g1, d_gm, d_g3, d_gf, d_gc, d_ga, d_bf[:, :H], d_cw[:3])
    slab = slab.at[SMALL_ROWS - 1].set(loss_row[0])
    total = _all_reduce_small(slab, name="reduce_small")
    loss = total[SMALL_ROWS - 1, 0]
    mcols = meta_tokens.shape[-1]
    ccols = conv_w.shape[-1]
    full_like = (jnp.zeros((N_META, D)), ffn1_norm, mix_norm, ffn2_norm, final_norm.reshape(1, D), out_norm_conv,
                 out_norm_attn, b_f, jnp.zeros((1, 3, CD)))
    g_small = _unpack_small(total, full_like)
    g_small[0] = lax.dynamic_slice_in_dim(g_small[0], chip * mcols, mcols, axis=1)
    g_small[8] = lax.dynamic_slice_in_dim(g_small[8], chip * ccols, ccols, axis=2)

    def small_slab(meta, a1, am, a3, af, gc, ga, bf, cw):
        return _pack_small(D, meta, a1, am, a3, af.reshape(1, D), gc, ga, bf, cw[0])

    w_small = small_slab(meta_tokens, ffn1_norm, mix_norm, ffn2_norm, final_norm, out_norm_conv, out_norm_attn, b_f, conv_w)
    m_small = small_slab(m_meta_tokens, m_ffn1_norm, m_mix_norm, m_ffn2_norm, m_final_norm, m_out_norm_conv,
                         m_out_norm_attn, m_b_f, m_conv_w)
    v_small = small_slab(v_meta_tokens, v_ffn1_norm, v_mix_norm, v_ffn2_norm, v_final_norm, v_out_norm_conv,
                         v_out_norm_attn, v_b_f, v_conv_w)
    gs = list(g_small)
    gs[4] = gs[4].reshape(final_norm.shape)
    g_slab = small_slab(gs[0], gs[1], gs[2], gs[3], gs[4], gs[5], gs[6], gs[7], gs[8])
    local_like = (meta_tokens, ffn1_norm, mix_norm, ffn2_norm, final_norm.reshape(1, D), out_norm_conv, out_norm_attn,
                  b_f, conv_w)
    small_out = [_unpack_small(s, local_like)
                 for s in _adamw(w_small, g_slab, m_small, v_small, name="adamw_small")[1:]]
    for lst in small_out:
        lst[4] = lst[4].reshape(final_norm.shape)

    names = ["wgu1", "wd1", "win", "wout", "wgu2", "wd2"]
    w_big = big
    m_big = [m_ffn1_w_gu[0], m_ffn1_w_down[0], m_w_in[0], m_w_out[0], m_ffn2_w_gu[0], m_ffn2_w_down[0]]
    v_big = [v_ffn1_w_gu[0], v_ffn1_w_down[0], v_w_in[0], v_w_out[0], v_ffn2_w_gu[0], v_ffn2_w_down[0]]
    big_out = [_adamw(w, g, m, v, name="adamw_" + nm) for w, g, m, v, nm in zip(w_big, g_big, m_big, v_big, names)]

    def assemble(small, bigs):
        meta, a1, am, a3, af, gc, ga, bf, cw = small
        gu1_, d1_, win_, wout_, gu2_, d2_ = [b[None] for b in bigs]
        return [meta, a1, gu1_, d1_, am, win_, cw, bf, gc, ga, wout_, a3, gu2_, d2_, af]

    gs_out = list(g_small)
    gs_out[4] = gs_out[4].reshape(final_norm.shape)
    grads_out = assemble(gs_out, [b[0] for b in big_out])
    delta_out = assemble(small_out[0], [b[1] for b in big_out])
    m_out = assemble(small_out[1], [b[2] for b in big_out])
    v_out = assemble(small_out[2], [b[3] for b in big_out])
    return (loss, grad_x, *grads_out, *delta_out, *m_out, *v_out)
```

```python
import functools

import jax
import jax.numpy as jnp
from jax import lax
from jax.experimental import pallas as pl
from jax.experimental.pallas import tpu as pltpu

F32 = jnp.float32
BF16 = jnp.bfloat16

EPS = 1e-6
N_META = 16
HEAD_DIM = 64
N_SHARD = 4
N_DEV = 8
HALO = 16
LANES = 128
SMALL_ROWS = 32
VMEM_LIMIT_V7X = 56 * 1024 * 1024
NEG = -1e30
ATTN_BANDS = 2

ADAM_LR = 0.001
ADAM_B1 = 0.9
ADAM_B2 = 0.999
ADAM_EPS = 1e-08
ADAM_WD = 0.01
ADAM_STEP = 10

MESH = pl.DeviceIdType.MESH
ANY = pl.BlockSpec(memory_space=pl.ANY)
NT_DIMS = (((1,), (1,)), ((), ()))
TN_DIMS = (((0,), (0,)), ((), ()))


def _params(*sem):
    return pltpu.CompilerParams(dimension_semantics=sem, vmem_limit_bytes=VMEM_LIMIT_V7X)


class _Comm:
    def __init__(self, ins, out_shapes, sems, start, finish, aliases=None):
        self.ins, self.out_shapes, self.sems = list(ins), list(out_shapes), list(sems)
        self.start, self.finish, self.aliases = start, finish, dict(aliases or {})


def _join(a, b):
    ni, no, ns = len(a.ins), len(a.out_shapes), len(a.sems)

    def start(ins, outs, sems):
        a.start(ins[:ni], outs[:no], sems[:ns])
        b.start(ins[ni:], outs[no:], sems[ns:])

    def finish(ins, outs, sems):
        a.finish(ins[:ni], outs[:no], sems[:ns])
        b.finish(ins[ni:], outs[no:], sems[ns:])

    aliases = dict(a.aliases)
    aliases.update({ni + i: no + j for i, j in b.aliases.items()})
    return _Comm(a.ins + b.ins, a.out_shapes + b.out_shapes, a.sems + b.sems, start, finish, aliases)


def _launch(body, *, name, grid, in_specs, out_specs, out_shape, args, scratch_shapes=(), comm=None, prefetch=(),
            aliases=None):
    single = not isinstance(out_shape, (list, tuple))
    out_specs = [out_specs] if single else list(out_specs)
    out_shape = [out_shape] if single else list(out_shape)
    in_specs, scratch_shapes, prefetch = list(in_specs), list(scratch_shapes), list(prefetch)
    params = _params(*(("arbitrary",) * len(grid)))
    n_pf, n_in, n_out, n_scr = len(prefetch), len(in_specs), len(out_specs), len(scratch_shapes)
    c_ins = comm.ins if comm else []
    c_shapes = comm.out_shapes if comm else []
    c_sems = comm.sems if comm else []
    c_in, c_out = len(c_ins), len(c_shapes)

    def carrier(*refs):
        p = 0
        pf = refs[p:p + n_pf]; p += n_pf
        a = refs[p:p + n_in]; p += n_in
        ci = refs[p:p + c_in]; p += c_in
        o = refs[p:p + n_out]; p += n_out
        co = refs[p:p + c_out]; p += c_out
        s = refs[p:p + n_scr]; p += n_scr
        cs = refs[p:]
        if comm:
            first = functools.reduce(lambda u, v: u & v, [pl.program_id(k) == 0 for k in range(len(grid))])

            @pl.when(first)
            def _():
                comm.start(ci, co, cs)

        body(*pf, *a, *o, *s)

        if comm:
            last = functools.reduce(lambda u, v: u & v, [pl.program_id(k) == grid[k] - 1 for k in range(len(grid))])

            @pl.when(last)
            def _():
                comm.finish(ci, co, cs)

    io_aliases = {n_pf + i: j for i, j in (aliases or {}).items()}
    if comm:
        io_aliases.update({n_pf + n_in + i: n_out + j for i, j in comm.aliases.items()})
    all_in, all_out = in_specs + [ANY] * c_in, out_specs + [ANY] * c_out
    all_scratch = scratch_shapes + [pltpu.SemaphoreType.DMA((k,)) for k in c_sems]
    if n_pf:
        spec = dict(grid_spec=pltpu.PrefetchScalarGridSpec(
            num_scalar_prefetch=n_pf, grid=grid, in_specs=all_in, out_specs=all_out, scratch_shapes=all_scratch))
    else:
        spec = dict(grid=grid, in_specs=all_in, out_specs=all_out, scratch_shapes=all_scratch)
    res = pl.pallas_call(carrier, name=name, out_shape=out_shape + c_shapes, input_output_aliases=io_aliases,
                         compiler_params=params, **spec)(*prefetch, *args, *c_ins)
    main = list(res[:n_out])
    return (main[0] if single else main), (list(res[n_out:]) if comm else None)


def _run_comm(comm, *, name):
    c_in, c_out = len(comm.ins), len(comm.out_shapes)

    def body(*refs):
        ci, co, cs = refs[:c_in], refs[c_in:c_in + c_out], refs[c_in + c_out:]
        comm.start(ci, co, cs)
        comm.finish(ci, co, cs)

    return list(pl.pallas_call(
        body, name=name, in_specs=[ANY] * c_in, out_specs=[ANY] * c_out, out_shape=comm.out_shapes,
        scratch_shapes=[pltpu.SemaphoreType.DMA((k,)) for k in comm.sems],
        input_output_aliases=comm.aliases)(*comm.ins))


def _chunks(width, step=512):
    out, c0 = [], 0
    while c0 < width:
        cw = min(step, width - c0)
        out.append((c0, cw))
        c0 += cw
    return out


def _split2(v):
    hi = v.astype(BF16)
    lo = (v - hi.astype(F32)).astype(BF16)
    return hi, lo


def _split3(v):
    hi = v.astype(BF16)
    r = v - hi.astype(F32)
    mid = r.astype(BF16)
    lo = (r - mid.astype(F32)).astype(BF16)
    return hi, mid, lo


def _dot(a, b):
    return jnp.dot(a, b, preferred_element_type=F32)


def _dot_nt(a, b):
    return lax.dot_general(a, b, NT_DIMS, preferred_element_type=F32)


def _dot_tn(a, b):
    return lax.dot_general(a, b, TN_DIMS, preferred_element_type=F32)


def _silu_mul(g, u):
    return g * jax.nn.sigmoid(g) * u


def _rms_bwd(dn, h, gain, dres):
    r = lax.rsqrt(jnp.mean(h * h, axis=-1, keepdims=True) + EPS)
    y = h * r
    dgain = jnp.sum(dn * y, axis=0, keepdims=True)
    dy = dn * gain
    dh = dres + r * (dy - y * jnp.mean(dy * y, axis=-1, keepdims=True))
    return dh, dgain


def _group_mean(v, p):
    hi, lo = _split2(v)
    return _dot(hi, p) + _dot(lo, p)


def _row_of(a, k):
    rows = lax.broadcasted_iota(jnp.int32, a.shape, 0)
    return jnp.sum(jnp.where(rows == k, a, 0.0), axis=0, keepdims=True)


def _causal_conv(u, prev, w):
    rows = lax.broadcasted_iota(jnp.int32, u.shape, 0)
    p1 = _row_of(prev, HALO - 1)
    p2 = _row_of(prev, HALO - 2)
    u1 = jnp.where(rows == 0, p1, pltpu.roll(u, 1, 0))
    u2 = jnp.where(rows == 0, p2, jnp.where(rows == 1, p1, pltpu.roll(u, 2, 0)))
    return w[2:3, :] * u + w[1:2, :] * u1 + w[0:1, :] * u2, u1, u2


def _rms(x, gain):
    return (x * lax.rsqrt(jnp.mean(x * x, axis=-1, keepdims=True) + EPS) * gain).astype(BF16)


def _embed_norm(x, meta, g, *, tm, name, comm=None):
    B, S, D = x.shape
    L = S + N_META
    per_seq = L // tm
    nt = B * per_seq
    body_rows = tm - N_META

    def body(meta_ref, g_ref, x_hbm, h_ref, n_ref, buf, sems):
        i = pl.program_id(0)

        def fetch(k, fn):
            slot, b, t = k % 2, k // per_seq, k % per_seq

            @pl.when(t == 0)
            def _():
                fn(pltpu.make_async_copy(x_hbm.at[b, pl.ds(0, body_rows)],
                                         buf.at[slot, pl.ds(N_META, body_rows)], sems.at[slot]))

            @pl.when(t != 0)
            def _():
                fn(pltpu.make_async_copy(x_hbm.at[b, pl.ds(pl.multiple_of(t * tm - N_META, 8), tm)],
                                         buf.at[slot], sems.at[slot]))

        @pl.when(i == 0)
        def _():
            fetch(i, lambda cp: cp.start())

        @pl.when(i + 1 < nt)
        def _():
            fetch(i + 1, lambda cp: cp.start())

        fetch(i, lambda cp: cp.wait())
        slot = i % 2

        @pl.when(i % per_seq == 0)
        def _():
            buf[slot, 0:N_META, :] = meta_ref[...]

        hv = buf[slot]
        h_ref[...] = hv
        n_ref[...] = _rms(hv, g_ref[...])

    row = pl.BlockSpec((tm, D), lambda i: (i, 0))
    return _launch(
        body, name=name, grid=(nt,),
        in_specs=[pl.BlockSpec((N_META, D), lambda i: (0, 0)), pl.BlockSpec((1, D), lambda i: (0, 0)), ANY],
        out_specs=[row, row],
        out_shape=[jax.ShapeDtypeStruct((B * L, D), F32), jax.ShapeDtypeStruct((B * L, D), BF16)],
        scratch_shapes=[pltpu.VMEM((2, tm, D), F32), pltpu.SemaphoreType.DMA((2,))],
        args=(meta, g, x), comm=comm)


def _ffn_up(n, wgu, sid, gu_prev, *, tm, first, count, name, comm=None):
    T, D = n.shape
    ns, _, guc = wgu.shape
    ff = N_SHARD * guc // 2

    def body(sid_ref, x_ref, w_ref, *rest):
        rest[-1][...] = _dot(x_ref[...], w_ref[...]).astype(BF16)

    where = lambda s, sid: sid[first + s]
    w_at = (lambda s, sid: 0) if ns == 1 else where
    return _launch(
        body, name=name, grid=(count, T // tm), prefetch=(sid,),
        in_specs=[pl.BlockSpec((tm, D), lambda s, i, sid: (i, 0)),
                  pl.BlockSpec((None, D, guc), lambda s, i, sid: (w_at(s, sid), 0, 0))]
                 + ([] if gu_prev is None else [ANY]),
        out_specs=pl.BlockSpec((None, tm, guc), lambda s, i, sid: (where(s, sid) // 2, i, where(s, sid) % 2)),
        out_shape=jax.ShapeDtypeStruct((2, T, ff), BF16),
        args=(n, wgu) + (() if gu_prev is None else (gu_prev,)),
        aliases=None if gu_prev is None else {2: 0}, comm=comm)


def _matmul_nn(x, w, *, tm, nb, w_spec, out_shape, out_spec, name, comm=None):
    T, K = x.shape

    def body(x_ref, w_ref, o_ref):
        o_ref[...] = _dot(x_ref[...], w_ref[...]).astype(o_ref.dtype)

    return _launch(
        body, name=name, grid=(nb, T // tm),
        in_specs=[pl.BlockSpec((tm, K), lambda s, i: (i, 0)), w_spec],
        out_specs=out_spec, out_shape=out_shape, args=(x, w), comm=comm)


def _mix_in(n, w_main, w_fg, *, tm, nb, name):
    T, D = n.shape
    n_main = w_main.shape[1]
    bw = n_main // nb
    W = w_fg.shape[1]

    def body(x_ref, w_ref, wf_ref, o_ref, fg_ref):
        x = x_ref[...]
        o_ref[...] = _dot(x, w_ref[...]).astype(BF16)

        @pl.when(pl.program_id(1) == 0)
        def _():
            fg_ref[...] = _dot(x, wf_ref[...])

    res, _ = _launch(
        body, name=name, grid=(T // tm, nb),
        in_specs=[pl.BlockSpec((tm, D), lambda i, s: (i, 0)), pl.BlockSpec((D, bw), lambda i, s: (0, s)),
                  pl.BlockSpec((D, W), lambda i, s: (0, 0))],
        out_specs=[pl.BlockSpec((tm, bw), lambda i, s: (i, s)), pl.BlockSpec((tm, W), lambda i, s: (i, 0))],
        out_shape=[jax.ShapeDtypeStruct((T, n_main), BF16), jax.ShapeDtypeStruct((T, W), F32)],
        args=(n, w_main, w_fg))
    return res


def _down_in_bands(g_ref, u_ref, wd_v, edges, chunks, finish):
    def down(rows):
        def act(c0, cw):
            return _silu_mul(g_ref[rows, c0:c0 + cw].astype(F32), u_ref[rows, c0:c0 + cw].astype(F32)).astype(BF16)

        acc = None
        nxt = act(*chunks[0])
        for k, (c0, cw) in enumerate(chunks):
            a = nxt
            if k + 1 < len(chunks):
                nxt = act(*chunks[k + 1])
            d = _dot(a, wd_v[c0:c0 + cw, :])
            acc = d if acc is None else acc + d
        return acc

    bands = [slice(r0, r1) for r0, r1 in zip(edges[:-1], edges[1:])]
    nxt = down(bands[0])
    for b, rows in enumerate(bands):
        acc = nxt
        if b + 1 < len(bands):
            nxt = down(bands[b + 1])
        finish(rows, acc)


def _ffn_down(gu, wd, h, next_gain, *, tm, name, comm=None):
    _, T, ff = gu.shape
    D = h.shape[1]
    chunks = _chunks(ff)

    def body(g_ref, u_ref, wd_hbm, h_ref, ng_ref, o_ref, n_ref, wd_v, sem):
        @pl.when(pl.program_id(0) == 0)
        def _():
            cp = pltpu.make_async_copy(wd_hbm, wd_v, sem)
            cp.start()
            cp.wait()

        def finish(rows, acc):
            out = h_ref[rows, :] + 0.5 * acc
            o_ref[rows, :] = out
            n_ref[rows, :] = _rms(out, ng_ref[...])

        _down_in_bands(g_ref, u_ref, wd_v, _band_edges(tm), chunks, finish)

    return _launch(
        body, name=name, grid=(T // tm,),
        in_specs=[pl.BlockSpec((None, tm, ff), lambda i: (0, i, 0)),
                  pl.BlockSpec((None, tm, ff), lambda i: (1, i, 0)),
                  ANY,
                  pl.BlockSpec((tm, D), lambda i: (i, 0)),
                  pl.BlockSpec((1, D), lambda i: (0, 0))],
        out_specs=[pl.BlockSpec((tm, D), lambda i: (i, 0)), pl.BlockSpec((tm, D), lambda i: (i, 0))],
        out_shape=[jax.ShapeDtypeStruct((T, D), F32), jax.ShapeDtypeStruct((T, D), BF16)],
        scratch_shapes=[pltpu.VMEM((ff, D), BF16), pltpu.SemaphoreType.DMA],
        args=(gu, gu, wd, h, next_gain), comm=comm)


def _ffn_down_loss(gu, wd, h, gf, tgt, *, tm, name, comm=None):
    _, T, ff = gu.shape
    D = h.shape[1]
    B, S, _ = tgt.shape
    per_seq = (S + N_META) // tm
    body_rows = tm - N_META
    chunks = _chunks(ff)

    def body(g_ref, u_ref, wd_hbm, h_ref, gf_ref, tgt_hbm, dh_ref, dhb_ref, dg_ref, loss_ref, wd_v, tg_v, sem, tsem):
        i = pl.program_id(0)
        b, t = i // per_seq, i % per_seq

        @pl.when(i == 0)
        def _():
            cp = pltpu.make_async_copy(wd_hbm, wd_v, sem)
            cp.start()
            cp.wait()
            dg_ref[...] = jnp.zeros_like(dg_ref)
            loss_ref[...] = jnp.zeros_like(loss_ref)
            tg_v[0:N_META, :] = jnp.zeros((N_META, D), F32)

        def fetch(fn):
            @pl.when(t == 0)
            def _():
                fn(pltpu.make_async_copy(tgt_hbm.at[b, pl.ds(0, body_rows)], tg_v.at[pl.ds(N_META, body_rows)], tsem))

            @pl.when(t != 0)
            def _():
                fn(pltpu.make_async_copy(tgt_hbm.at[b, pl.ds(pl.multiple_of(t * tm - N_META, 8), tm)], tg_v, tsem))

        fetch(lambda cp: cp.start())

        def finish(rows, acc):
            if rows.start == 0:
                fetch(lambda cp: cp.wait())
            x = h_ref[rows, :] + 0.5 * acc
            gain = gf_ref[...]
            r = lax.rsqrt(jnp.mean(x * x, axis=-1, keepdims=True) + EPS)
            y = x * r
            pos = t * tm + rows.start + lax.broadcasted_iota(jnp.int32, (rows.stop - rows.start, 1), 0)
            err = jnp.where(pos >= N_META, y * gain - tg_v[rows, :], 0.0)
            loss_ref[...] += 0.5 * jnp.sum(jnp.mean(err * err, axis=-1, keepdims=True))
            dout = err / D
            dg_ref[...] += jnp.sum(dout * y, axis=0, keepdims=True)
            dy = dout * gain
            dh = r * (dy - y * jnp.mean(dy * y, axis=-1, keepdims=True))
            dh_ref[rows, :] = dh
            dhb_ref[rows, :] = (0.5 * dh).astype(BF16)

        _down_in_bands(g_ref, u_ref, wd_v, _band_edges(tm), chunks, finish)

    row = pl.BlockSpec((tm, D), lambda i: (i, 0))
    const = lambda i: (0, 0)
    return _launch(
        body, name=name, grid=(T // tm,),
        in_specs=[pl.BlockSpec((None, tm, ff), lambda i: (0, i, 0)),
                  pl.BlockSpec((None, tm, ff), lambda i: (1, i, 0)),
                  ANY, row, pl.BlockSpec((1, D), const), ANY],
        out_specs=[row, row, pl.BlockSpec((1, D), const), pl.BlockSpec((1, LANES), const)],
        out_shape=[jax.ShapeDtypeStruct((T, D), F32), jax.ShapeDtypeStruct((T, D), BF16),
                   jax.ShapeDtypeStruct((1, D), F32), jax.ShapeDtypeStruct((1, LANES), F32)],
        scratch_shapes=[pltpu.VMEM((ff, D), BF16), pltpu.VMEM((tm, D), F32), pltpu.SemaphoreType.DMA,
                        pltpu.SemaphoreType.DMA],
        args=(gu, gu, wd, h, gf, tgt), comm=comm)


def _ffn_bwd_act(df, gu, wd, *, tm, guc, name, comm=None):
    _, T, ff = gu.shape
    D = df.shape[1]
    nj = ff // guc
    chunks = _chunks(guc)

    def body(df_ref, g_ref, u_ref, wd_ref, o_ref, dwd_ref):
        @pl.when(pl.program_id(1) == 0)
        def _():
            dwd_ref[...] = jnp.zeros_like(dwd_ref)

        dfv = df_ref[...]
        nxt = _dot_nt(dfv, wd_ref[chunks[0][0]:chunks[0][0] + chunks[0][1], :])
        for k, (c0, cw) in enumerate(chunks):
            da = nxt
            if k + 1 < len(chunks):
                n0, nw = chunks[k + 1]
                nxt = _dot_nt(dfv, wd_ref[n0:n0 + nw, :])
            g = g_ref[:, c0:c0 + cw].astype(F32)
            u = u_ref[:, c0:c0 + cw].astype(F32)
            sg = jax.nn.sigmoid(g)
            silu = g * sg
            o_ref[0, :, c0:c0 + cw] = (da * u * (sg * (1.0 + g * (1.0 - sg)))).astype(BF16)
            o_ref[1, :, c0:c0 + cw] = (da * silu).astype(BF16)
            dwd_ref[c0:c0 + cw, :] += _dot_tn((silu * u).astype(BF16), dfv)

    return _launch(
        body, name=name, grid=(nj, T // tm),
        in_specs=[pl.BlockSpec((tm, D), lambda j, i: (i, 0)),
                  pl.BlockSpec((None, tm, guc), lambda j, i: (0, i, j)),
                  pl.BlockSpec((None, tm, guc), lambda j, i: (1, i, j)),
                  pl.BlockSpec((guc, D), lambda j, i: (j, 0))],
        out_specs=[pl.BlockSpec((2, tm, guc), lambda j, i: (0, i, j)), pl.BlockSpec((guc, D), lambda j, i: (j, 0))],
        out_shape=[jax.ShapeDtypeStruct((2, T, ff), BF16), jax.ShapeDtypeStruct((ff, D), F32)],
        args=(df, gu, gu, wd), comm=comm)


def _ffn_bwd_in(dgu, wgu, h, g, dres, *, tm, scale, name, comm=None):
    _, T, ff = dgu.shape
    ns, D, guc = wgu.shape
    nj = ff // guc
    edges = _band_edges(tm)

    def body(dgu_ref, w_hbm, h_ref, g_ref, dres_ref, dh_ref, dhb_ref, dg_ref, w_v, acc, sem):
        i, j = pl.program_id(0), pl.program_id(1)

        @pl.when((i == 0) & (j == 0))
        def _():
            cp = pltpu.make_async_copy(w_hbm, w_v, sem)
            cp.start()
            cp.wait()
            dg_ref[...] = jnp.zeros_like(dg_ref)

        def dots(rows):
            return _dot_nt(dgu_ref[0, rows, :], w_v[j]) + _dot_nt(dgu_ref[1, rows, :], w_v[nj + j])

        @pl.when(j < nj - 1)
        def _():
            part = dots(slice(None))

            @pl.when(j == 0)
            def _():
                acc[...] = part

            @pl.when(j > 0)
            def _():
                acc[...] += part

        @pl.when(j == nj - 1)
        def _():
            bands = [slice(r0, r1) for r0, r1 in zip(edges[:-1], edges[1:])]
            nxt = dots(bands[0])
            for b, rows in enumerate(bands):
                dn = nxt if nj == 1 else acc[rows, :] + nxt
                if b + 1 < len(bands):
                    nxt = dots(bands[b + 1])
                dh, dgain = _rms_bwd(dn, h_ref[rows, :], g_ref[...], dres_ref[rows, :])
                dh_ref[rows, :] = dh
                dhb_ref[rows, :] = (scale * dh).astype(BF16)
                dg_ref[...] += dgain

    return _launch(
        body, name=name, grid=(T // tm, nj),
        in_specs=[pl.BlockSpec((2, tm, guc), lambda i, j: (0, i, j)),
                  ANY,
                  pl.BlockSpec((tm, D), lambda i, j: (i, 0)),
                  pl.BlockSpec((1, D), lambda i, j: (0, 0)),
                  pl.BlockSpec((tm, D), lambda i, j: (i, 0))],
        out_specs=[pl.BlockSpec((tm, D), lambda i, j: (i, 0)),
                   pl.BlockSpec((tm, D), lambda i, j: (i, 0)),
                   pl.BlockSpec((1, D), lambda i, j: (0, 0))],
        out_shape=[jax.ShapeDtypeStruct((T, D), F32), jax.ShapeDtypeStruct((T, D), BF16),
                   jax.ShapeDtypeStruct((1, D), F32)],
        scratch_shapes=[pltpu.VMEM((ns, D, guc), BF16), pltpu.VMEM((tm, D), F32), pltpu.SemaphoreType.DMA],
        args=(dgu, wgu, h, g, dres), comm=comm)


def _ffn_bwd_in_first(dgu, wgu, h, g, dres, *, tm, batch, name, comm=None):
    _, T, ff = dgu.shape
    ns, D, guc = wgu.shape
    nj = ff // guc
    nt = T // tm
    L = T // batch
    per_seq = L // tm
    body_rows = tm - N_META
    edges = _band_edges(tm)

    def body(dgu_ref, w_hbm, h_ref, g_ref, dres_ref, dx_hbm, dmeta_ref, dg_ref, w_v, acc, dh_v, sem, osem):
        i, j = pl.program_id(0), pl.program_id(1)

        @pl.when((i == 0) & (j == 0))
        def _():
            cp = pltpu.make_async_copy(w_hbm, w_v, sem)
            cp.start()
            cp.wait()
            dg_ref[...] = jnp.zeros_like(dg_ref)
            dmeta_ref[...] = jnp.zeros_like(dmeta_ref)

        def dots(rows):
            return _dot_nt(dgu_ref[0, rows, :], w_v[j]) + _dot_nt(dgu_ref[1, rows, :], w_v[nj + j])

        @pl.when(j < nj - 1)
        def _():
            part = dots(slice(None))

            @pl.when(j == 0)
            def _():
                acc[...] = part

            @pl.when(j > 0)
            def _():
                acc[...] += part

        def head_copy(b):
            return pltpu.make_async_copy(dh_v.at[pl.ds(N_META, body_rows)], dx_hbm.at[b, pl.ds(0, body_rows)], osem)

        def tail_copy(b, t):
            return pltpu.make_async_copy(dh_v, dx_hbm.at[b, pl.ds(pl.multiple_of(t * tm - N_META, 8), tm)], osem)

        def on_tile(k, head_fn, tail_fn):
            @pl.when(k % per_seq == 0)
            def _():
                head_fn(head_copy(k // per_seq))

            @pl.when(k % per_seq != 0)
            def _():
                tail_fn(tail_copy(k // per_seq, k % per_seq))

        @pl.when(j == nj - 1)
        def _():
            @pl.when(i > 0)
            def _():
                on_tile(i - 1, lambda cp: cp.wait(), lambda cp: cp.wait())

            bands = [slice(r0, r1) for r0, r1 in zip(edges[:-1], edges[1:])]
            nxt = dots(bands[0])
            for b, rows in enumerate(bands):
                dn = nxt if nj == 1 else acc[rows, :] + nxt
                if b + 1 < len(bands):
                    nxt = dots(bands[b + 1])
                dh, dgain = _rms_bwd(dn, h_ref[rows, :], g_ref[...], dres_ref[rows, :])
                dg_ref[...] += dgain
                dh_v[rows, :] = dh
                if b == 0:
                    @pl.when(i % per_seq == 0)
                    def _():
                        dmeta_ref[...] += dh[0:N_META, :]

            on_tile(i, lambda cp: cp.start(), lambda cp: cp.start())

            @pl.when(i == nt - 1)
            def _():
                on_tile(i, lambda cp: cp.wait(), lambda cp: cp.wait())

    return _launch(
        body, name=name, grid=(nt, nj),
        in_specs=[pl.BlockSpec((2, tm, guc), lambda i, j: (0, i, j)),
                  ANY,
                  pl.BlockSpec((tm, D), lambda i, j: (i, 0)),
                  pl.BlockSpec((1, D), lambda i, j: (0, 0)),
                  pl.BlockSpec((tm, D), lambda i, j: (i, 0))],
        out_specs=[ANY, pl.BlockSpec((N_META, D), lambda i, j: (0, 0)), pl.BlockSpec((1, D), lambda i, j: (0, 0))],
        out_shape=[jax.ShapeDtypeStruct((batch, L - N_META, D), F32), jax.ShapeDtypeStruct((N_META, D), F32),
                   jax.ShapeDtypeStruct((1, D), F32)],
        scratch_shapes=[pltpu.VMEM((ns, D, guc), BF16), pltpu.VMEM((tm, D), F32), pltpu.VMEM((tm, D), F32),
                        pltpu.SemaphoreType.DMA, pltpu.SemaphoreType.DMA],
        args=(dgu, wgu, h, g, dres), comm=comm)


def _mix_bwd_in(parts, w_main, w_fg, h, g, dres, *, tm, scale, name, comm=None):
    T, D = h.shape
    widths = [p.shape[1] for p in parts]
    offs = [sum(widths[:k]) for k in range(len(widths))]
    npart = len(parts)
    wide = sum(widths)
    edges = _band_edges(tm)

    def body(*refs):
        p_refs = refs[:npart]
        wm_ref, wf_ref, h_ref, g_ref, dres_ref, dh_ref, dhb_ref, dg_ref, all_ref = refs[npart:]

        @pl.when(pl.program_id(0) == 0)
        def _():
            dg_ref[...] = jnp.zeros_like(dg_ref)

        for p_ref, off, wd_ in zip(p_refs, offs, widths):
            for c0, cw in _chunks(wd_):
                all_ref[:, off + c0:off + c0 + cw] = p_ref[:, c0:c0 + cw].astype(BF16)
        n_main = offs[-1]

        def dots(rows):
            return _dot_nt(all_ref[rows, :n_main], wm_ref[...]) + _dot_nt(all_ref[rows, n_main:], wf_ref[...])

        bands = [slice(r0, r1) for r0, r1 in zip(edges[:-1], edges[1:])]
        nxt = dots(bands[0])
        for b, rows in enumerate(bands):
            dn = nxt
            if b + 1 < len(bands):
                nxt = dots(bands[b + 1])
            dh, dgain = _rms_bwd(dn, h_ref[rows, :], g_ref[...], dres_ref[rows, :])
            dh_ref[rows, :] = dh
            dhb_ref[rows, :] = (scale * dh).astype(BF16)
            dg_ref[...] += dgain

    row = lambda i: (i, 0)
    const = lambda i: (0, 0)
    return _launch(
        body, name=name, grid=(T // tm,),
        in_specs=[pl.BlockSpec((tm, p.shape[1]), row) for p in parts]
                 + [pl.BlockSpec(w_main.shape, const), pl.BlockSpec(w_fg.shape, const),
                    pl.BlockSpec((tm, D), row), pl.BlockSpec((1, D), const), pl.BlockSpec((tm, D), row)],
        out_specs=[pl.BlockSpec((tm, D), row), pl.BlockSpec((tm, D), row), pl.BlockSpec((1, D), const),
                   pl.BlockSpec((tm, wide), row)],
        out_shape=[jax.ShapeDtypeStruct((T, D), F32), jax.ShapeDtypeStruct((T, D), BF16),
                   jax.ShapeDtypeStruct((1, D), F32), jax.ShapeDtypeStruct((T, wide), BF16)],
        args=(*parts, w_main, w_fg, h, g, dres), comm=comm)


def _matmul_tn(x, y, *, tm, nb, x_spec, y_spec, out_shape, out_spec, kb, name, comm=None):
    T = y.shape[-2]
    chunks = _chunks(kb)

    def body(x_ref, y_ref, o_ref):
        @pl.when(pl.program_id(1) == 0)
        def _():
            o_ref[...] = jnp.zeros_like(o_ref)

        yv = y_ref[...].astype(BF16)
        nxt = _dot_tn(x_ref[:, chunks[0][0]:chunks[0][0] + chunks[0][1]], yv)
        for k, (c0, cw) in enumerate(chunks):
            cur = nxt
            if k + 1 < len(chunks):
                n0, nw = chunks[k + 1]
                nxt = _dot_tn(x_ref[:, n0:n0 + nw], yv)
            o_ref[c0:c0 + cw, :] += cur

    return _launch(
        body, name=name, grid=(nb, T // tm),
        in_specs=[x_spec, y_spec], out_specs=out_spec, out_shape=out_shape, args=(x, y), comm=comm)


def _tri(n, lower):
    r = lax.broadcasted_iota(jnp.int32, (n, n), 0)
    c = lax.broadcasted_iota(jnp.int32, (n, n), 1)
    return jnp.where((r >= c) if lower else (r <= c), 1.0, 0.0).astype(BF16)


def _tri_dot(tri, v):
    hi, mid, lo = _split3(v)
    return _dot(tri, hi) + _dot(tri, mid) + _dot(tri, lo)


def _fcum(fg, bf, *, ch, name):
    B, L, W = fg.shape
    nch = L // ch

    def body(fg_ref, bf_ref, f_ref):
        tri = _tri(ch, True)
        carry = jnp.zeros((1, W), F32)
        for c in range(nch):
            x = fg_ref[c * ch:(c + 1) * ch, :] + bf_ref[...]
            lf = jnp.minimum(x, 0.0) - jnp.log(1.0 + jnp.exp(-jnp.abs(x)))
            f_ref[c * ch:(c + 1) * ch, :] = _tri_dot(tri, lf) + carry
            carry = carry + jnp.sum(lf, axis=0, keepdims=True)

    return pl.pallas_call(
        body, name=name, grid=(B,),
        in_specs=[pl.BlockSpec((None, L, W), lambda b: (b, 0, 0)), pl.BlockSpec((1, W), lambda b: (0, 0))],
        out_specs=pl.BlockSpec((None, L, W), lambda b: (b, 0, 0)),
        out_shape=jax.ShapeDtypeStruct((B, L, W), F32),
        compiler_params=_params("arbitrary"),
    )(fg, bf)


def _fcum_bwd(dF_rows, dF_cols, fg, bf, *, ch, name):
    B, L, W = fg.shape
    nch = L // ch

    def body(dfr_ref, dfc_ref, fg_ref, bf_ref, dfg_ref, db_ref):
        @pl.when(pl.program_id(0) == 0)
        def _():
            db_ref[...] = jnp.zeros_like(db_ref)

        tri = _tri(ch, False)
        carry = jnp.zeros((1, W), F32)
        dbs = jnp.zeros((1, W), F32)
        for c in reversed(range(nch)):
            d = dfr_ref[c * ch:(c + 1) * ch, :] - dfc_ref[c * ch:(c + 1) * ch, :]
            dlf = _tri_dot(tri, d) + carry
            carry = carry + jnp.sum(d, axis=0, keepdims=True)
            x = fg_ref[c * ch:(c + 1) * ch, :] + bf_ref[...]
            dfg = dlf * jax.nn.sigmoid(-x)
            dfg_ref[c * ch:(c + 1) * ch, :] = dfg.astype(BF16)
            dbs = dbs + jnp.sum(dfg, axis=0, keepdims=True)
        db_ref[...] += dbs

    blk = pl.BlockSpec((None, L, W), lambda b: (b, 0, 0))
    return pl.pallas_call(
        body, name=name, grid=(B,),
        in_specs=[blk, blk, blk, pl.BlockSpec((1, W), lambda b: (0, 0))],
        out_specs=[blk, pl.BlockSpec((1, W), lambda b: (0, 0))],
        out_shape=[jax.ShapeDtypeStruct((B, L, W), BF16), jax.ShapeDtypeStruct((1, W), F32)],
        compiler_params=_params("arbitrary"),
    )(dF_rows, dF_cols, fg, bf)


def _band_edges(tq):
    return sorted({min(tq, (k * tq // ATTN_BANDS + HALO - 1) // HALO * HALO) for k in range(ATTN_BANDS + 1)})


def _pair(h):
    return slice((h // 2) * 2 * HEAD_DIM, (h // 2 + 1) * 2 * HEAD_DIM)


def _own_lanes(a, h):
    low = lax.broadcasted_iota(jnp.int32, a.shape, 1) < HEAD_DIM
    return jnp.where(low if h % 2 == 0 else jnp.logical_not(low), a, jnp.zeros_like(a))


def _sum_lane(h):
    return HEAD_DIM if h % 2 == 0 else 0


def _own_lanes_and_ones(a, h):
    lane = lax.broadcasted_iota(jnp.int32, a.shape, 1)
    low = lane < HEAD_DIM
    return jnp.where(low if h % 2 == 0 else jnp.logical_not(low), a,
                     jnp.where(lane == _sum_lane(h), jnp.ones_like(a), jnp.zeros_like(a)))


def _attn_fwd(proj, fr, *, tq, n_heads, name, comm=None):
    B, L, _ = proj.shape
    AD = n_heads * HEAD_DIM
    nq = L // tq
    W = LANES
    scale = HEAD_DIM ** -0.5
    edges = _band_edges(tq)

    v_ones, sum_lane = _own_lanes_and_ones, _sum_lane

    def body(q_ref, k_ref, v_ref, fr_ref, o_ref, lse_ref, m_s, acc_s):
        qi, ki = pl.program_id(1), pl.program_id(2)

        @pl.when(ki == 0)
        def _():
            m_s[...] = jnp.full_like(m_s, NEG)
            acc_s[...] = jnp.zeros_like(acc_s)

        def tile(diagonal):
            lane = lax.broadcasted_iota(jnp.int32, (tq, W), 1)
            m_all = m_s[...]
            m_out = m_all
            bands = [(r0, r1, r1 if diagonal else tq) for r0, r1 in zip(edges[:-1], edges[1:])]
            if diagonal:
                masks = {r0: (lax.broadcasted_iota(jnp.int32, (r1 - r0, c1), 1)
                              <= r0 + lax.broadcasted_iota(jnp.int32, (r1 - r0, c1), 0)) for r0, r1, c1 in bands}

            def scores(h, band):
                r0, r1, c1 = band
                sl = slice(h * HEAD_DIM, (h + 1) * HEAD_DIM)
                return _dot_nt(q_ref[r0:r1, sl] * scale, k_ref[0:c1, sl])

            work = [(h, band) for h in range(n_heads) for band in bands]
            nxt = scores(*work[0])
            for w, (h, band) in enumerate(work):
                r0, r1, c1 = band
                sl = slice(h * HEAD_DIM, (h + 1) * HEAD_DIM)
                s = nxt - fr_ref[h:h + 1, 0:c1]
                if w + 1 < len(work):
                    nxt = scores(*work[w + 1])
                if diagonal:
                    s = jnp.where(masks[r0], s, NEG)
                m_old = m_all[r0:r1, h:h + 1]
                m_new = jnp.maximum(m_old, jnp.max(s, axis=1, keepdims=True))
                alpha = jnp.exp(m_old - m_new)
                p = jnp.exp(s - m_new)
                own = slice(h * 2 * HEAD_DIM, (h + 1) * 2 * HEAD_DIM)
                acc_s[r0:r1, own] = alpha * acc_s[r0:r1, own] + _dot(p.astype(BF16), v_ones(v_ref[0:c1, _pair(h)], h))
                if r0 == 0:
                    m_parts = []
                m_parts.append(m_new)
                if r1 == tq:
                    m_out = jnp.where(lane == h, jnp.concatenate(m_parts, axis=0), m_out)
            m_s[...] = m_out

        @pl.when(ki < qi)
        def _():
            tile(False)

        @pl.when(ki == qi)
        def _():
            tile(True)
            lane = lax.broadcasted_iota(jnp.int32, (tq, W), 1)
            low = lax.broadcasted_iota(jnp.int32, (tq, 2 * HEAD_DIM), 1) < HEAD_DIM
            l_all = jnp.ones((tq, W), F32)
            for h in range(0, n_heads, 2):
                even = acc_s[:, h * 2 * HEAD_DIM:(h + 1) * 2 * HEAD_DIM]
                odd = acc_s[:, (h + 1) * 2 * HEAD_DIM:(h + 2) * 2 * HEAD_DIM]
                l_even = even[:, sum_lane(h):sum_lane(h) + 1]
                l_odd = odd[:, sum_lane(h + 1):sum_lane(h + 1) + 1]
                o_ref[:, _pair(h)] = jnp.where(low, even / l_even, odd / l_odd)
                l_all = jnp.where(lane == h, l_even, jnp.where(lane == h + 1, l_odd, l_all))
            lse_ref[...] = jnp.where(lane < n_heads, m_s[...] + jnp.log(l_all), 0.0)

    kv = lambda b, qi, ki: jnp.minimum(ki, qi)
    return _launch(
        body, name=name, grid=(B, nq, nq), args=(proj, proj, proj, fr), comm=comm,
        in_specs=[pl.BlockSpec((None, tq, AD), lambda b, qi, ki: (b, qi, 3)),
                  pl.BlockSpec((None, tq, AD), lambda b, qi, ki: (b, kv(b, qi, ki), 4)),
                  pl.BlockSpec((None, tq, AD), lambda b, qi, ki: (b, kv(b, qi, ki), 5)),
                  pl.BlockSpec((None, None, n_heads, tq), lambda b, qi, ki: (b, kv(b, qi, ki), 0, 0))],
        out_specs=[pl.BlockSpec((None, tq, AD), lambda b, qi, ki: (b, qi, 0)),
                   pl.BlockSpec((None, tq, W), lambda b, qi, ki: (b, qi, 0))],
        out_shape=[jax.ShapeDtypeStruct((B, L, AD), F32), jax.ShapeDtypeStruct((B, L, W), F32)],
        scratch_shapes=[pltpu.VMEM((tq, W), F32), pltpu.VMEM((tq, n_heads * 2 * HEAD_DIM), F32)])


def _attn_bwd(proj, o, do, lse, fr, *, tq, n_heads, name, comm=None):
    B, L, _ = proj.shape
    AD = n_heads * HEAD_DIM
    nq = L // tq
    W = LANES
    HW = 2 * HEAD_DIM
    scale = HEAD_DIM ** -0.5
    edges = _band_edges(tq)

    def body(q_ref, k_ref, v_ref, o_ref, do_ref, lse_ref, fr_ref,
             dq_ref, dk_ref, dv_ref, dfk_ref, dfq_ref, dq_s, dk_s, dv_s):
        kj, qi = pl.program_id(1), pl.program_id(2)

        @pl.when((kj == 0) & (qi == 0))
        def _():
            dq_s[...] = jnp.zeros_like(dq_s)

        @pl.when(qi == kj)
        def _():
            dk_s[...] = jnp.zeros_like(dk_s)
            dv_s[...] = jnp.zeros_like(dv_s)

        def tile(diagonal):
            bands = [(r0, r1, r1) for r0, r1 in zip(edges[:-1], edges[1:])] if diagonal else [(0, tq, tq)]
            lse = lse_ref[...]
            for r0, r1, c1 in bands:
                nr = r1 - r0
                rows = pl.ds(pl.multiple_of(qi * tq + r0, 8), nr)
                if diagonal:
                    mask = (lax.broadcasted_iota(jnp.int32, (nr, c1), 1)
                            <= r0 + lax.broadcasted_iota(jnp.int32, (nr, c1), 0))
                def scores(h):
                    ps = _pair(h)
                    k = k_ref[0:c1, ps]
                    qs = q_ref[r0:r1, ps] * scale
                    dov = _own_lanes(do_ref[r0:r1, ps], h)
                    return _dot_nt(_own_lanes(qs, h), k), _dot_nt(dov, v_ref[0:c1, ps]), k, qs, dov

                nxt = scores(0)
                for h in range(n_heads):
                    ps = _pair(h)
                    own = slice(h * HW, (h + 1) * HW)
                    s, dp, k, qs, dov = nxt
                    if h + 1 < n_heads:
                        nxt = scores(h + 1)
                    s = s - fr_ref[h:h + 1, 0:c1]
                    if diagonal:
                        s = jnp.where(mask, s, NEG)
                    p = jnp.exp(s - lse[r0:r1, h:h + 1])
                    dsum = jnp.sum(dov.astype(F32) * o_ref[r0:r1, ps], axis=1, keepdims=True)
                    dsb = (p * (dp - dsum)).astype(BF16)
                    dv = _dot_tn(p.astype(BF16), dov)
                    dk_s[0:c1, own] += _dot_tn(dsb, _own_lanes_and_ones(qs, h))
                    dq_s[rows, own] += _dot(dsb, _own_lanes_and_ones(k, h))
                    if h % 2 == 0:
                        dv_even = dv
                    else:
                        dv_s[0:c1, ps] += dv_even + dv

        def compact(acc, data_scale):
            rows = acc.shape[0]
            low = lax.broadcasted_iota(jnp.int32, (rows, HW), 1) < HEAD_DIM
            lane = lax.broadcasted_iota(jnp.int32, (rows, W), 1)
            vals, sums = [], jnp.zeros((rows, W), F32)
            for h in range(0, n_heads, 2):
                even, odd = acc[:, h * HW:(h + 1) * HW], acc[:, (h + 1) * HW:(h + 2) * HW]
                vals.append(jnp.where(low, even, odd) * data_scale)
                sums = jnp.where(lane == h, even[:, _sum_lane(h):_sum_lane(h) + 1],
                                 jnp.where(lane == h + 1, odd[:, _sum_lane(h + 1):_sum_lane(h + 1) + 1], sums))
            return vals, sums

        @pl.when(qi > kj)
        def _():
            tile(False)

        @pl.when(qi == kj)
        def _():
            tile(True)
            rows = pl.ds(pl.multiple_of(qi * tq, 8), tq)
            vals, sums = compact(dq_s[rows, :], scale)
            for h in range(0, n_heads, 2):
                dq_ref[rows, _pair(h)] = vals[h // 2]
            dfq_ref[rows, :] = sums

        @pl.when(qi == nq - 1)
        def _():
            vals, sums = compact(dk_s[...], 1.0)
            for h in range(0, n_heads, 2):
                dk_ref[:, _pair(h)] = vals[h // 2].astype(BF16)
            dfk_ref[...] = sums
            dv_ref[...] = dv_s[...].astype(BF16)

    qq = lambda b, kj, qi: jnp.maximum(qi, kj)
    qblk = lambda w, cb: pl.BlockSpec((None, tq, w), lambda b, kj, qi: (b, qq(b, kj, qi), cb))
    kblk = lambda w, cb: pl.BlockSpec((None, tq, w), lambda b, kj, qi: (b, kj, cb))
    return _launch(
        body, name=name, grid=(B, nq, nq), args=(proj, proj, proj, o, do, lse, fr), comm=comm,
        in_specs=[qblk(AD, 3), kblk(AD, 4), kblk(AD, 5), qblk(AD, 0), qblk(AD, 0), qblk(W, 0),
                  pl.BlockSpec((None, None, n_heads, tq), lambda b, kj, qi: (b, kj, 0, 0))],
        out_specs=[pl.BlockSpec((None, L, AD), lambda b, kj, qi: (b, 0, 0)),
                   kblk(AD, 0), kblk(AD, 0), kblk(W, 0),
                   pl.BlockSpec((None, L, W), lambda b, kj, qi: (b, 0, 0))],
        out_shape=[jax.ShapeDtypeStruct((B, L, AD), F32), jax.ShapeDtypeStruct((B, L, AD), BF16),
                   jax.ShapeDtypeStruct((B, L, AD), BF16), jax.ShapeDtypeStruct((B, L, W), F32),
                   jax.ShapeDtypeStruct((B, L, W), F32)],
        scratch_shapes=[pltpu.VMEM((L, n_heads * HW), F32), pltpu.VMEM((tq, n_heads * HW), F32),
                        pltpu.VMEM((tq, AD), F32)])


def _mix_gather(refs, first):
    b_ref, c_ref, hc_ref, cp_ref, hcp_ref, o_ref, cw_ref, p_ref = refs
    bg = b_ref[...].astype(F32)
    u = c_ref[...].astype(F32) * hc_ref[...].astype(F32)
    prev = cp_ref[...].astype(F32) * hcp_ref[...].astype(F32)
    prev = jnp.where(first, 0.0, prev)
    cv, u1, u2 = _causal_conv(u, prev, cw_ref[...])
    yc = bg * cv
    p = p_ref[...]
    rc = lax.rsqrt(_group_mean(yc * yc, p) + EPS)
    ya = o_ref[...].astype(F32)
    ra = lax.rsqrt(_group_mean(ya * ya, p) + EPS)
    return bg, (u, u1, u2), cv, yc * rc, rc, ya * ra, ra


def _mix_specs(tm, CD):
    per = tm // HALO
    cur = lambda cb: pl.BlockSpec((None, tm, CD), lambda b, i: (b, i, cb))
    prev = lambda cb: pl.BlockSpec((None, HALO, CD), lambda b, i: (b, jnp.maximum(i * per - 1, 0), cb))
    return [cur(0), cur(1), cur(2), prev(1), prev(2), cur(0)]


def _mix_out(proj, o, cw, gc, ga, wout, h, pmat, next_gain, *, tm, name, comm=None):
    B, L, D = h.shape
    CD = o.shape[-1]
    const = lambda b, i: (0, 0)

    def body(b_ref, c_ref, hc_ref, cp_ref, hcp_ref, o_ref, cw_ref, p_ref, gc_ref, ga_ref, w_ref, h_ref, ng_ref,
             out_ref, y_ref, n_ref):
        first = pl.program_id(1) == 0
        _, _, _, zc, _, za, _ = _mix_gather((b_ref, c_ref, hc_ref, cp_ref, hcp_ref, o_ref, cw_ref, p_ref), first)
        yc = (zc * gc_ref[...]).astype(BF16)
        ya = (za * ga_ref[...]).astype(BF16)
        y_ref[:, :CD] = yc
        y_ref[:, CD:] = ya
        out = h_ref[...] + _dot(yc, w_ref[:CD, :]) + _dot(ya, w_ref[CD:, :])
        out_ref[...] = out
        n_ref[...] = _rms(out, ng_ref[...])

    tile = pl.BlockSpec((None, tm, D), lambda b, i: (b, i, 0))
    return _launch(
        body, name=name, grid=(B, L // tm),
        in_specs=_mix_specs(tm, CD)
                 + [pl.BlockSpec(cw.shape, const), pl.BlockSpec(pmat.shape, const),
                    pl.BlockSpec((1, CD), const), pl.BlockSpec((1, CD), const), pl.BlockSpec((D, D), const),
                    tile, pl.BlockSpec((1, D), const)],
        out_specs=[tile, tile, tile],
        out_shape=[jax.ShapeDtypeStruct((B, L, D), F32), jax.ShapeDtypeStruct((B, L, D), BF16),
                   jax.ShapeDtypeStruct((B, L, D), BF16)],
        args=(proj, proj, proj, proj, proj, o, cw, pmat, gc, ga, wout, h, next_gain), comm=comm)


def _mix_out_bwd(dhb, proj, o, cw, gc, ga, wout, pmat, *, tm, name, comm=None):
    B, L, D = dhb.shape
    CD = o.shape[-1]
    const = lambda b, i: (0, 0)

    def body(dh_ref, b_ref, c_ref, hc_ref, cp_ref, hcp_ref, o_ref, cw_ref, p_ref, gc_ref, ga_ref, w_ref,
             db_ref, dcv_ref, do_ref, dgc_ref, dga_ref, dcw_ref):
        first = pl.program_id(1) == 0

        @pl.when((pl.program_id(0) == 0) & first)
        def _():
            dgc_ref[...] = jnp.zeros_like(dgc_ref)
            dga_ref[...] = jnp.zeros_like(dga_ref)
            dcw_ref[...] = jnp.zeros_like(dcw_ref)

        bg, us, cv, zc, rc, za, ra = _mix_gather(
            (b_ref, c_ref, hc_ref, cp_ref, hcp_ref, o_ref, cw_ref, p_ref), first)
        p = p_ref[...]
        dh = dh_ref[...]
        dyc = _dot_nt(dh, w_ref[:CD, :])
        dya = _dot_nt(dh, w_ref[CD:, :])

        dgc_ref[...] += jnp.sum(dyc * zc, axis=0, keepdims=True)
        dz = dyc * gc_ref[...]
        dx = rc * (dz - zc * _group_mean(dz * zc, p))
        db_ref[...] = (dx * cv).astype(BF16)
        dcv = dx * bg
        dcv_ref[...] = dcv.astype(BF16)
        for k in range(3):
            dcw_ref[k:k + 1, :] += jnp.sum(dcv * us[2 - k], axis=0, keepdims=True)

        dga_ref[...] += jnp.sum(dya * za, axis=0, keepdims=True)
        dz = dya * ga_ref[...]
        do_ref[...] = (ra * (dz - za * _group_mean(dz * za, p))).astype(BF16)

    tile = lambda w: pl.BlockSpec((None, tm, w), lambda b, i: (b, i, 0))
    return _launch(
        body, name=name, grid=(B, L // tm), comm=comm,
        args=(dhb, proj, proj, proj, proj, proj, o, cw, pmat, gc, ga, wout),
        in_specs=[tile(D)] + _mix_specs(tm, CD)
                 + [pl.BlockSpec(cw.shape, const), pl.BlockSpec(pmat.shape, const),
                    pl.BlockSpec((1, CD), const), pl.BlockSpec((1, CD), const), pl.BlockSpec((D, D), const)],
        out_specs=[tile(CD), tile(CD), tile(CD),
                   pl.BlockSpec((1, CD), const), pl.BlockSpec((1, CD), const), pl.BlockSpec((8, CD), const)],
        out_shape=[jax.ShapeDtypeStruct((B, L, CD), BF16)] * 3
                  + [jax.ShapeDtypeStruct((1, CD), F32)] * 2 + [jax.ShapeDtypeStruct((8, CD), F32)])


def _conv_bwd(dcv, proj, cw, *, tm, name):
    B, L, CD = dcv.shape
    per = tm // HALO
    nhalo = L // HALO
    nt = L // tm

    def body(d_ref, dn_ref, c_ref, hc_ref, cw_ref, out_ref):
        last = pl.program_id(1) == nt - 1
        d = d_ref[...].astype(F32)
        nxt = jnp.where(last, 0.0, dn_ref[...].astype(F32))
        n0, n1 = _row_of(nxt, 0), _row_of(nxt, 1)
        rows = lax.broadcasted_iota(jnp.int32, d.shape, 0)
        d1 = jnp.where(rows == tm - 1, n0, pltpu.roll(d, tm - 1, 0))
        d2 = jnp.where(rows == tm - 2, n0, jnp.where(rows == tm - 1, n1, pltpu.roll(d, tm - 2, 0)))
        w = cw_ref[...]
        du = w[2:3, :] * d + w[1:2, :] * d1 + w[0:1, :] * d2
        out_ref[:, :CD] = (du * hc_ref[...].astype(F32)).astype(BF16)
        out_ref[:, CD:] = (du * c_ref[...].astype(F32)).astype(BF16)

    return pl.pallas_call(
        body, name=name, grid=(B, nt),
        in_specs=[pl.BlockSpec((None, tm, CD), lambda b, i: (b, i, 0)),
                  pl.BlockSpec((None, HALO, CD), lambda b, i: (b, jnp.minimum((i + 1) * per, nhalo - 1), 0)),
                  pl.BlockSpec((None, tm, CD), lambda b, i: (b, i, 1)),
                  pl.BlockSpec((None, tm, CD), lambda b, i: (b, i, 2)),
                  pl.BlockSpec(cw.shape, lambda b, i: (0, 0))],
        out_specs=pl.BlockSpec((None, tm, 2 * CD), lambda b, i: (b, i, 0)),
        out_shape=jax.ShapeDtypeStruct((B, L, 2 * CD), BF16),
        compiler_params=_params("arbitrary", "arbitrary"),
    )(dcv, dcv, proj, proj, cw)


def _place():
    x, y, c = lax.axis_index("x"), lax.axis_index("y"), lax.axis_index("c")
    others = [(1 - x, y), (x, 1 - y), (1 - x, 1 - y)]
    return x, y, c, others


def _all_gather_shards(shards, *, name):
    n = len(shards)

    def body(*refs):
        ins, outs = refs[:n], refs[n:2 * n]
        send, recv, fsend, frecv, lsem = refs[2 * n:]
        x, y, c, others = _place()
        me = 2 * x + y
        local = [pltpu.make_async_copy(ins[t], outs[t].at[me], lsem.at[t]) for t in range(n)]
        for cp in local:
            cp.start()

        def half(t, k):
            hr = shards[t].shape[0] // 2
            return pl.ds(pl.multiple_of(k * hr, HALO), hr)

        def ici(t, j, src_chip, to):
            src = ins[t].at[half(t, c)] if to is not None else outs[t].at[src_chip, half(t, c)]
            return pltpu.make_async_remote_copy(
                src_ref=src, dst_ref=outs[t].at[src_chip, half(t, c)],
                send_sem=send.at[3 * t + j], recv_sem=recv.at[3 * t + j],
                device_id=(x, y, c) if to is None else to, device_id_type=MESH)

        def d2d(t, j, src_chip, k):
            return pltpu.make_async_remote_copy(
                src_ref=outs[t].at[src_chip, half(t, k)], dst_ref=outs[t].at[src_chip, half(t, k)],
                send_sem=fsend.at[3 * t + j], recv_sem=frecv.at[3 * t + j],
                device_id=(x, y, 1 - c), device_id_type=MESH)

        firsts = [ici(t, j, me, (ox, oy, c)) for t in range(n) for j, (ox, oy) in enumerate(others)]
        for cp in firsts:
            cp.start()
        passed = []
        for t in range(n):
            for j, (ox, oy) in enumerate(others):
                ici(t, j, 2 * ox + oy, None).wait_recv()
                cp = d2d(t, j, 2 * ox + oy, c)
                cp.start()
                passed.append(cp)
        for t in range(n):
            for j, (ox, oy) in enumerate(others):
                d2d(t, j, 2 * ox + oy, 1 - c).wait_recv()
        for cp in firsts + passed:
            cp.wait_send()
        for cp in local:
            cp.wait()

    return pl.pallas_call(
        body, name=name,
        in_specs=[ANY] * n, out_specs=[ANY] * n,
        out_shape=[jax.ShapeDtypeStruct((N_SHARD,) + s.shape, s.dtype) for s in shards],
        scratch_shapes=[pltpu.SemaphoreType.DMA((3 * n,))] * 4 + [pltpu.SemaphoreType.DMA((n,))],
    )(*shards)


def _all_reduce_small(slab, *, name):
    def body(in_ref, out_ref, gath, send, recv):
        x, y, c, _ = _place()
        me = 4 * x + 2 * y + c
        gath[me] = in_ref[...]
        copies, peers = [], []
        for m in range(1, N_DEV):
            px = jnp.where((m >> 2) & 1, 1 - x, x)
            py = jnp.where((m >> 1) & 1, 1 - y, y)
            pc = jnp.where(m & 1, 1 - c, c)
            cp = pltpu.make_async_remote_copy(
                src_ref=in_ref, dst_ref=gath.at[me], send_sem=send.at[m - 1], recv_sem=recv.at[m - 1],
                device_id=(px, py, pc), device_id_type=MESH)
            cp.start()
            copies.append(cp)
            peers.append(4 * px + 2 * py + pc)
        for m in range(1, N_DEV):
            pltpu.make_async_remote_copy(
                src_ref=in_ref, dst_ref=gath.at[peers[m - 1]], send_sem=send.at[m - 1], recv_sem=recv.at[m - 1],
                device_id=(x, y, c), device_id_type=MESH).wait_recv()
        for cp in copies:
            cp.wait_send()
        acc = gath[0]
        for k in range(1, N_DEV):
            acc = acc + gath[k]
        out_ref[...] = acc

    vm = pl.BlockSpec(memory_space=pltpu.VMEM)
    return pl.pallas_call(
        body, name=name, in_specs=[vm], out_specs=vm,
        out_shape=jax.ShapeDtypeStruct(slab.shape, slab.dtype),
        scratch_shapes=[pltpu.VMEM((N_DEV,) + slab.shape, slab.dtype),
                        pltpu.SemaphoreType.DMA((N_DEV - 1,)), pltpu.SemaphoreType.DMA((N_DEV - 1,))],
    )(slab)


def _gather_stage(shards, into, *, ici=(), d2d=()):
    n = len(shards) if into is None else len(into)
    ns = len(shards) if ici else 0
    ni, nd = max(len(ici), 1), max(len(d2d), 1)
    shapes = [s.shape for s in shards] if into is None else [p.shape[1:] for p in into]
    dtypes = [s.dtype for s in shards] if into is None else [p.dtype for p in into]

    def copies(ins, outs, sems, sending):
        x, y, c, others = _place()
        me = 2 * x + y
        out = []
        for t in range(n):
            hr = shapes[t][0] // 2
            mine = pl.ds(pl.multiple_of(c * hr, HALO), hr)
            theirs = pl.ds(pl.multiple_of((1 - c) * hr, HALO), hr)
            for a, j in enumerate(ici):
                ox, oy = others[j]
                src_chip = me if sending else 2 * ox + oy
                out.append(pltpu.make_async_remote_copy(
                    src_ref=ins[t].at[mine], dst_ref=outs[t].at[src_chip, mine],
                    send_sem=sems[0].at[ni * t + a], recv_sem=sems[1].at[ni * t + a],
                    device_id=(ox, oy, c) if sending else (x, y, c), device_id_type=MESH))
            for a, j in enumerate(d2d):
                ox, oy = others[j]
                blk = outs[t].at[2 * ox + oy, mine if sending else theirs]
                out.append(pltpu.make_async_remote_copy(
                    src_ref=blk, dst_ref=blk, send_sem=sems[2].at[nd * t + a], recv_sem=sems[3].at[nd * t + a],
                    device_id=(x, y, 1 - c) if sending else (x, y, c), device_id_type=MESH))
        return out

    def local(ins, outs, sems):
        if into is not None:
            return []
        x, y, _, _ = _place()
        return [pltpu.make_async_copy(ins[t], outs[t].at[2 * x + y], sems[4].at[t]) for t in range(n)]

    def start(ins, outs, sems):
        for cp in local(ins, outs, sems) + copies(ins, outs, sems, True):
            cp.start()

    def finish(ins, outs, sems):
        for cp in copies(ins, outs, sems, False):
            cp.wait_recv()
        for cp in copies(ins, outs, sems, True):
            cp.wait_send()
        for cp in local(ins, outs, sems):
            cp.wait()

    return _Comm((list(shards) if ici or into is None else []) + (list(into) if into is not None else []),
                 [jax.ShapeDtypeStruct((N_SHARD,) + tuple(sh), dt) for sh, dt in zip(shapes, dtypes)],
                 [ni * n, ni * n, nd * n, nd * n, n], start, finish,
                 aliases=None if into is None else {ns + t: t for t in range(n)})


def _gather_ici(shards):
    return _gather_stage(shards, None, ici=(0, 1, 2))


def _gather_d2d(parts):
    return _gather_stage((), parts, d2d=(0, 1, 2))


def _swap_halves(grads):
    n = len(grads)

    def copies(ins, outs, sems):
        x, y, c, _ = _place()
        out = []
        for t in range(n):
            hr = grads[t].shape[1] // 2
            rows = pl.ds(pl.multiple_of((1 - c) * hr, 8), hr)
            out.append(pltpu.make_async_remote_copy(
                src_ref=ins[t].at[:, rows, :], dst_ref=outs[t], send_sem=sems[0].at[t], recv_sem=sems[1].at[t],
                device_id=(x, y, 1 - c), device_id_type=MESH))
        return out

    def start(ins, outs, sems):
        for cp in copies(ins, outs, sems):
            cp.start()

    def finish(ins, outs, sems):
        for cp in copies(ins, outs, sems):
            cp.wait()

    return _Comm(grads, [jax.ShapeDtypeStruct((N_SHARD, g.shape[1] // 2, g.shape[2]), g.dtype) for g in grads],
                 [n, n], start, finish)


def _pair_sum(g, got, c, *, name):
    ns, R, C = g.shape
    hr = R // 2

    def body(c_ref, g_ref, r_ref, o_ref):
        o_ref[...] = (g_ref[...] + r_ref[...]).astype(BF16)

    return pl.pallas_call(
        body, name=name,
        grid_spec=pltpu.PrefetchScalarGridSpec(
            num_scalar_prefetch=1, grid=(ns,),
            in_specs=[pl.BlockSpec((None, hr, C), lambda s, cr: (s, cr[0], 0)),
                      pl.BlockSpec((None, hr, C), lambda s, cr: (s, 0, 0))],
            out_specs=pl.BlockSpec((None, hr, C), lambda s, cr: (s, 0, 0))),
        out_shape=jax.ShapeDtypeStruct((ns, hr, C), BF16),
        compiler_params=_params("arbitrary"),
    )(c, g, got)


def _scatter_chips(sums):
    n = len(sums)

    def copies(ins, outs, sems, sending):
        x, y, c, others = _place()
        me = 2 * x + y
        out = []
        for t in range(n):
            for j, (ox, oy) in enumerate(others):
                there = 2 * ox + oy
                out.append(pltpu.make_async_remote_copy(
                    src_ref=ins[t].at[there if sending else me], dst_ref=outs[t].at[me if sending else there],
                    send_sem=sems[0].at[3 * t + j], recv_sem=sems[1].at[3 * t + j],
                    device_id=(ox, oy, c) if sending else (x, y, c), device_id_type=MESH))
        return out

    def start(ins, outs, sems):
        for cp in copies(ins, outs, sems, True):
            cp.start()

    def finish(ins, outs, sems):
        for cp in copies(ins, outs, sems, False):
            cp.wait_recv()
        for cp in copies(ins, outs, sems, True):
            cp.wait_send()

    return _Comm(sums, [jax.ShapeDtypeStruct(s.shape, s.dtype) for s in sums], [3 * n, 3 * n], start, finish)


def _chip_sum(g, got, landed, idx, *, name):
    ns, R, C = g.shape
    hr = R // 2

    def body(i_ref, g_ref, r_ref, a_ref, b_ref, c_ref, o_ref):
        acc = g_ref[...] + r_ref[...]
        for ref in (a_ref, b_ref, c_ref):
            acc = acc + ref[...].astype(F32)
        o_ref[...] = acc

    other = lambda k: pl.BlockSpec((None, hr, C), lambda s, ir: (ir[2 + k], 0, 0))
    return pl.pallas_call(
        body, name=name,
        grid_spec=pltpu.PrefetchScalarGridSpec(
            num_scalar_prefetch=1, grid=(1,),
            in_specs=[pl.BlockSpec((None, hr, C), lambda s, ir: (ir[0], ir[1], 0)),
                      pl.BlockSpec((None, hr, C), lambda s, ir: (ir[0], 0, 0)),
                      other(0), other(1), other(2)],
            out_specs=pl.BlockSpec((hr, C), lambda s, ir: (ir[1], 0))),
        out_shape=jax.ShapeDtypeStruct((R, C), F32),
        compiler_params=_params("arbitrary"),
    )(idx, g, got, landed, landed, landed)


def _share_halves(halves):
    n = len(halves)

    def copies(outs, sems, sending):
        x, y, c, _ = _place()
        out = []
        for t in range(n):
            hr = halves[t].shape[0] // 2
            rows = pl.ds(pl.multiple_of((c if sending else 1 - c) * hr, 8), hr)
            out.append(pltpu.make_async_remote_copy(
                src_ref=outs[t].at[rows, :], dst_ref=outs[t].at[rows, :], send_sem=sems[0].at[t],
                recv_sem=sems[1].at[t], device_id=(x, y, 1 - c) if sending else (x, y, c), device_id_type=MESH))
        return out

    def start(ins, outs, sems):
        for cp in copies(outs, sems, True):
            cp.start()

    def finish(ins, outs, sems):
        for cp in copies(outs, sems, False):
            cp.wait_recv()
        for cp in copies(outs, sems, True):
            cp.wait_send()

    return _Comm(halves, [jax.ShapeDtypeStruct(h.shape, h.dtype) for h in halves], [n, n], start, finish,
                 aliases={t: t for t in range(n)})


def _adamw(w, g, m, v, *, name):
    R, C = w.shape
    tr = R
    for cand in (256, 128, 64, 32, 16, 8):
        if R % cand == 0:
            tr = cand
            break

    def body(w_ref, g_ref, m_ref, v_ref, go_ref, d_ref, mo_ref, vo_ref):
        gv = g_ref[...]
        go_ref[...] = gv
        mn = ADAM_B1 * m_ref[...] + (1.0 - ADAM_B1) * gv
        vn = ADAM_B2 * v_ref[...] + (1.0 - ADAM_B2) * (gv * gv)
        m_hat = mn / (1.0 - ADAM_B1 ** ADAM_STEP)
        v_hat = vn / (1.0 - ADAM_B2 ** ADAM_STEP)
        d_ref[...] = -ADAM_LR * (m_hat / (jnp.sqrt(v_hat) + ADAM_EPS) + ADAM_WD * w_ref[...])
        mo_ref[...] = mn
        vo_ref[...] = vn

    blk = pl.BlockSpec((tr, C), lambda i: (i, 0))
    return pl.pallas_call(
        body, name=name, grid=(R // tr,), in_specs=[blk] * 4, out_specs=[blk] * 4,
        out_shape=[jax.ShapeDtypeStruct((R, C), F32)] * 4,
        compiler_params=_params("arbitrary"),
    )(w, g, m, v)


def _pack_small(D, meta, n1, nm, n3, nf, gc, ga, bf, cw):
    def row(a):
        a = a.reshape(-1, a.shape[-1])
        return jnp.pad(a, ((0, 0), (0, D - a.shape[-1])))
    rows = [row(meta), row(n1), row(nm), row(n3), row(nf), row(jnp.concatenate([gc, ga], axis=-1)), row(bf), row(cw)]
    slab = jnp.concatenate(rows, axis=0)
    return jnp.pad(slab, ((0, SMALL_ROWS - slab.shape[0]), (0, 0)))


def _unpack_small(slab, like):
    meta, n1, nm, n3, nf, gc, ga, bf, cw = like
    nmeta, mc = meta.shape
    out = [slab[:nmeta, :mc].reshape(meta.shape)]
    r = nmeta
    for a in (n1, nm, n3, nf):
        out.append(slab[r, :a.shape[-1]].reshape(a.shape))
        r += 1
    cd = gc.shape[-1]
    out.append(slab[r, :cd].reshape(gc.shape))
    out.append(slab[r, cd:cd + ga.shape[-1]].reshape(ga.shape))
    r += 1
    out.append(slab[r, :bf.shape[-1]].reshape(bf.shape))
    r += 1
    out.append(slab[r:r + 3, :cw.shape[-1]].reshape(cw.shape))
    return out


def kernel(x, meta_tokens, ffn1_norm, ffn1_w_gu, ffn1_w_down, mix_norm, w_in, conv_w, b_f, out_norm_conv, out_norm_attn, w_out, ffn2_norm, ffn2_w_gu, ffn2_w_down, final_norm, loss_target, m_meta_tokens, m_ffn1_norm, m_ffn1_w_gu, m_ffn1_w_down, m_mix_norm, m_w_in, m_conv_w, m_b_f, m_out_norm_conv, m_out_norm_attn, m_w_out, m_ffn2_norm, m_ffn2_w_gu, m_ffn2_w_down, m_final_norm, v_meta_tokens, v_ffn1_norm, v_ffn1_w_gu, v_ffn1_w_down, v_mix_norm, v_w_in, v_conv_w, v_b_f, v_out_norm_conv, v_out_norm_attn, v_w_out, v_ffn2_norm, v_ffn2_w_gu, v_ffn2_w_down, v_final_norm):
    B, S, D = x.shape
    L = S + N_META
    T = B * L
    tm = L // 3
    assert tm * 3 == L and tm % HALO == 0
    tm2 = 2 * tm
    assert T % tm2 == 0
    guc = ffn1_w_gu.shape[-1]
    ff = N_SHARD * guc // 2
    H = b_f.shape[-1]
    AD = H * HEAD_DIM
    CD = conv_w.shape[-1] * N_SHARD
    assert CD == AD and CD + AD == D and CD % LANES == 0
    n_main = 3 * CD + 3 * AD
    ins = w_in.shape[-1]

    xi, yi, ci = lax.axis_index("x"), lax.axis_index("y"), lax.axis_index("c")
    chip = 2 * xi + yi

    small_shard = jnp.zeros((2 * HALO, meta_tokens.shape[-1]), F32)
    small_shard = small_shard.at[:N_META].set(meta_tokens)
    small_shard = small_shard.at[N_META:N_META + 3, :conv_w.shape[-1]].set(conv_w[0])
    big = [ffn1_w_gu[0], ffn1_w_down[0], w_in[0], w_out[0], ffn2_w_gu[0], ffn2_w_down[0]]
    wgu1_s, wd1_s, win_s, wout_s, wgu2_s, wd2_s = [w.astype(BF16) for w in big]
    small_g, = _all_gather_shards([small_shard], name="gather_small")
    meta_f = jnp.moveaxis(small_g[:, :N_META], 0, 1).reshape(N_META, D)
    cw_f = jnp.moveaxis(small_g[:, N_META:N_META + 3, :conv_w.shape[-1]], 0, 1).reshape(3, CD)
    cw8 = jnp.pad(cw_f, ((0, 5), (0, 0)))
    bf_p = jnp.pad(b_f, ((0, 0), (0, LANES - H)))
    gid = jnp.arange(CD) // HEAD_DIM
    pmat = jnp.where(gid[:, None] == gid[None, :], 1.0 / HEAD_DIM, 0.0).astype(BF16)

    gu_shape = jax.ShapeDtypeStruct((2, T, ff), BF16)
    gu_w_spec = pl.BlockSpec((None, D, guc), lambda s, i: (s, 0, 0))
    gu_o_spec = pl.BlockSpec((None, tm2, guc), lambda s, i: (s // 2, i, s % 2))

    sid = jnp.bitwise_xor(chip, jnp.array([0, 2, 1, 3], jnp.int32)).astype(jnp.int32)
    (h0, n1), wgu1_h = _embed_norm(x, meta_f, ffn1_norm, tm=tm, name="embed_norm",
                                   comm=_gather_stage([wgu1_s], None, ici=(0, 1)))
    gu1, wgu1_h = _ffn_up(n1, wgu1_s[None], sid, None, tm=tm2, first=0, count=1, name="ffn1_up_own",
                          comm=_gather_stage([wgu1_s], wgu1_h, ici=(2,), d2d=(0, 1)))
    gu1, out = _ffn_up(n1, wgu1_h[0], sid, gu1, tm=tm2, first=1, count=2, name="ffn1_up_near",
                       comm=_join(_gather_stage((), wgu1_h, d2d=(2,)), _gather_ici([wd1_s, wout_s])))
    wgu1, down_w = out[0], out[1:]
    gu1, (wd1, wout_g) = _ffn_up(n1, wgu1, sid, gu1, tm=tm2, first=3, count=1, name="ffn1_up_far",
                                 comm=_gather_d2d(down_w))
    wd1 = wd1.reshape(ff, D)
    (h1, n2), win_h = _ffn_down(gu1, wd1, h0, mix_norm, tm=tm, name="ffn1_down", comm=_gather_ici([win_s]))
    win_g, = _run_comm(_gather_d2d(win_h), name="gather_w_in")
    wout_f = wout_g.reshape(D, D)
    win_f = jnp.moveaxis(win_g, 0, 1).reshape(D, N_SHARD * ins)
    win_main = win_f[:, :n_main]
    win_fg = jnp.pad(win_f[:, n_main:], ((0, 0), (0, LANES - H)))

    proj, fg = _mix_in(n2, win_main, win_fg, tm=tm2, nb=n_main // (3 * CD), name="mix_in")
    proj3 = proj.reshape(B, L, n_main)
    fg3 = fg.reshape(B, L, LANES)
    fc = _fcum(fg3, bf_p, ch=tm, name="forget_cumsum")
    fr = fc[:, :, :H].reshape(B, L // tm, tm, H).transpose(0, 1, 3, 2)
    (o, lse), ffn2_w = _attn_fwd(proj3, fr, tq=tm, n_heads=H, name="attn_fwd",
                                 comm=_gather_ici([wgu2_s, wd2_s]))
    (h2, ymix, n3), (wgu2, wd2) = _mix_out(
        proj3, o, cw8, out_norm_conv, out_norm_attn, wout_f, h1.reshape(B, L, D), pmat, ffn2_norm,
        tm=tm, name="mix_out", comm=_gather_d2d(ffn2_w))
    wd2 = wd2.reshape(ff, D)
    h2 = h2.reshape(T, D)
    n3 = n3.reshape(T, D)

    gu2, _ = _matmul_nn(n3, wgu2, tm=tm2, nb=N_SHARD, w_spec=gu_w_spec, out_shape=gu_shape, out_spec=gu_o_spec,
                        name="ffn2_up")
    (dh3f, dh3b, d_gf, loss_part), _ = _ffn_down_loss(gu2, wd2, h2, final_norm.reshape(1, D), loss_target,
                                                      tm=tm, name="ffn2_down_loss")

    c_arr = jnp.reshape(ci, (1,)).astype(jnp.int32)
    ks = jnp.arange(N_SHARD - 1, dtype=jnp.int32)
    idx = jnp.concatenate([jnp.stack([chip, ci]).astype(jnp.int32), ks + (ks >= chip).astype(jnp.int32)])

    def pair_sums(grads, got, names):
        return [_pair_sum(g, r, c_arr, name="pair_sum_" + nm) for g, r, nm in zip(grads, got, names)]

    def chip_sums(grads, got, landed, names):
        return [_chip_sum(g, r, l, idx, name="chip_sum_" + nm) for g, r, l, nm in zip(grads, got, landed, names)]

    def dw_up(n, dgu, name, comm=None):
        return _matmul_tn(
            n, dgu, tm=L, nb=N_SHARD, kb=D, x_spec=pl.BlockSpec((L, D), lambda s, i: (i, 0)),
            y_spec=pl.BlockSpec((None, L, guc), lambda s, i: (s // 2, i, s % 2)),
            out_shape=jax.ShapeDtypeStruct((N_SHARD, D, guc), F32),
            out_spec=pl.BlockSpec((None, D, guc), lambda s, i: (s, 0, 0)), name=name, comm=comm)

    (dgu2, d_wd2), _ = _ffn_bwd_act(dh3b, gu2, wd2, tm=tm, guc=guc, name="ffn2_bwd_act")
    (dh2, dh2b, d_g3), _ = _ffn_bwd_in(dgu2, wgu2, h2, ffn2_norm, dh3f, tm=tm, scale=1.0, name="ffn2_bwd_in")
    d_wgu2, _ = dw_up(n3, dgu2, "ffn2_dw_up")
    grads_f2 = [d_wgu2, d_wd2.reshape(N_SHARD, ff // N_SHARD, D)]
    names_f2 = ["wgu2", "wd2"]

    dh2b3 = dh2b.reshape(B, L, D)
    (d_bg, d_cv, d_o, d_gc, d_ga, d_cw), got_f2 = _mix_out_bwd(
        dh2b3, proj3, o, cw8, out_norm_conv, out_norm_attn, wout_f, pmat, tm=tm, name="mix_out_bwd",
        comm=_swap_halves(grads_f2))
    sums_f2 = pair_sums(grads_f2, got_f2, names_f2)
    d_wout, _ = _matmul_tn(
        ymix.reshape(T, D), dh2b, tm=tm2, nb=1, kb=D,
        x_spec=pl.BlockSpec((tm2, D), lambda s, i: (i, 0)), y_spec=pl.BlockSpec((tm2, D), lambda s, i: (i, 0)),
        out_shape=jax.ShapeDtypeStruct((D, D), F32), out_spec=pl.BlockSpec((D, D), lambda s, i: (0, 0)),
        name="dw_out")
    d_cc = _conv_bwd(d_cv, proj3, cw8, tm=tm, name="conv_bwd")
    (d_q, d_k, d_v, d_fk, d_fq), landed_f2 = _attn_bwd(proj3, o, d_o, lse, fr, tq=tm, n_heads=H, name="attn_bwd",
                                                       comm=_scatter_chips(sums_f2))
    halves_f2 = chip_sums(grads_f2, got_f2, landed_f2, names_f2)
    d_fg, d_bf = _fcum_bwd(d_fq, d_fk, fg3, bf_p, ch=tm, name="forget_cumsum_bwd")

    parts = [d_bg.reshape(T, CD), d_cc.reshape(T, 2 * CD), d_q.reshape(T, AD), d_k.reshape(T, AD),
             d_v.reshape(T, AD), d_fg.reshape(T, LANES)]
    (dh1, dh1b, d_gm, d_proj), g_f2 = _mix_bwd_in(parts, win_main, win_fg, h1, mix_norm, dh2, tm=tm, scale=0.5,
                                                  name="mix_bwd_in", comm=_share_halves(halves_f2))
    wide = d_proj.shape[1]
    d_win_nat, _ = _matmul_tn(
        n2, d_proj, tm=tm, nb=1, kb=D,
        x_spec=pl.BlockSpec((tm, D), lambda s, i: (i, 0)), y_spec=pl.BlockSpec((tm, wide), lambda s, i: (i, 0)),
        out_shape=jax.ShapeDtypeStruct((D, wide), F32), out_spec=pl.BlockSpec((D, wide), lambda s, i: (0, 0)),
        name="dw_in")
    d_win = jnp.moveaxis(d_win_nat[:, :N_SHARD * ins].reshape(D, N_SHARD, ins), 1, 0)
    grads_mx = [d_win, d_wout.reshape(N_SHARD, D // N_SHARD, D)]
    names_mx = ["win", "wout"]

    (dgu1, d_wd1), got_mx = _ffn_bwd_act(dh1b, gu1, wd1, tm=tm, guc=guc, name="ffn1_bwd_act",
                                         comm=_swap_halves(grads_mx))
    sums_mx = pair_sums(grads_mx, got_mx, names_mx)
    grads_d1 = [d_wd1.reshape(N_SHARD, ff // N_SHARD, D)]
    d_wgu1, out = dw_up(n1, dgu1, "ffn1_dw_up", comm=_join(_scatter_chips(sums_mx), _swap_halves(grads_d1)))
    landed_mx, got_d1 = out[:2], out[2:]
    halves_mx = chip_sums(grads_mx, got_mx, landed_mx, names_mx)
    sums_d1 = pair_sums(grads_d1, got_d1, ["wd1"])
    grads_u1 = [d_wgu1]
    (grad_x, d_meta, d_g1), out = _ffn_bwd_in_first(
        dgu1, wgu1, h0, ffn1_norm, dh1, tm=tm, batch=B, name="ffn1_bwd_in",
        comm=_join(_join(_share_halves(halves_mx), _scatter_chips(sums_d1)), _swap_halves(grads_u1)))
    g_mx, landed_d1, got_u1 = out[:2], out[2:3], out[3:]
    halves_d1 = chip_sums(grads_d1, got_d1, landed_d1, ["wd1"])
    sums_u1 = pair_sums(grads_u1, got_u1, ["wgu1"])
    out = _run_comm(_join(_share_halves(halves_d1), _scatter_chips(sums_u1)), name="scatter_ffn1")
    g_d1, landed_u1 = out[:1], out[1:]
    halves_u1 = chip_sums(grads_u1, got_u1, landed_u1, ["wgu1"])
    g_u1 = _run_comm(_share_halves(halves_u1), name="share_ffn1")
    g_big = [g_u1[0], g_d1[0], g_mx[0], g_mx[1], g_f2[0], g_f2[1]]

    loss_row = jnp.zeros((1, D), F32).at[0, 0].set(loss_part[0, 0])
    slab = _pack_small(D, d_meta, d_g1, d_gm, d_g3, d_gf, d_gc, d_ga, d_bf[:, :H], d_cw[:3])
    slab = slab.at[SMALL_ROWS - 1].set(loss_row[0])
    total = _all_reduce_small(slab, name="reduce_small")
    loss = total[SMALL_ROWS - 1, 0]
    mcols = meta_tokens.shape[-1]
    ccols = conv_w.shape[-1]
    full_like = (jnp.zeros((N_META, D)), ffn1_norm, mix_norm, ffn2_norm, final_norm.reshape(1, D), out_norm_conv,
                 out_norm_attn, b_f, jnp.zeros((1, 3, CD)))
    g_small = _unpack_small(total, full_like)
    g_small[0] = lax.dynamic_slice_in_dim(g_small[0], chip * mcols, mcols, axis=1)
    g_small[8] = lax.dynamic_slice_in_dim(g_small[8], chip * ccols, ccols, axis=2)

    def small_slab(meta, a1, am, a3, af, gc, ga, bf, cw):
        return _pack_small(D, meta, a1, am, a3, af.reshape(1, D), gc, ga, bf, cw[0])

    w_small = small_slab(meta_tokens, ffn1_norm, mix_norm, ffn2_norm, final_norm, out_norm_conv, out_norm_attn, b_f, conv_w)
    m_small = small_slab(m_meta_tokens, m_ffn1_norm, m_mix_norm, m_ffn2_norm, m_final_norm, m_out_norm_conv,
                         m_out_norm_attn, m_b_f, m_conv_w)
    v_small = small_slab(v_meta_tokens, v_ffn1_norm, v_mix_norm, v_ffn2_norm, v_final_norm, v_out_norm_conv,
                         v_out_norm_attn, v_b_f, v_conv_w)
    gs = list(g_small)
    gs[4] = gs[4].reshape(final_norm.shape)
    g_slab = small_slab(gs[0], gs[1], gs[2], gs[3], gs[4], gs[5], gs[6], gs[7], gs[8])
    local_like = (meta_tokens, ffn1_norm, mix_norm, ffn2_norm, final_norm.reshape(1, D), out_norm_conv, out_norm_attn,
                  b_f, conv_w)
    small_out = [_unpack_small(s, local_like)
                 for s in _adamw(w_small, g_slab, m_small, v_small, name="adamw_small")[1:]]
    for lst in small_out:
        lst[4] = lst[4].reshape(final_norm.shape)

    names = ["wgu1", "wd1", "win", "wout", "wgu2", "wd2"]
    w_big = big
    m_big = [m_ffn1_w_gu[0], m_ffn1_w_down[0], m_w_in[0], m_w_out[0], m_ffn2_w_gu[0], m_ffn2_w_down[0]]
    v_big = [v_ffn1_w_gu[0], v_ffn1_w_down[0], v_w_in[0], v_w_out[0], v_ffn2_w_gu[0], v_ffn2_w_down[0]]
    big_out = [_adamw(w, g, m, v, name="adamw_" + nm) for w, g, m, v, nm in zip(w_big, g_big, m_big, v_big, names)]

    def assemble(small, bigs):
        meta, a1, am, a3, af, gc, ga, bf, cw = small
        gu1_, d1_, win_, wout_, gu2_, d2_ = [b[None] for b in bigs]
        return [meta, a1, gu1_, d1_, am, win_, cw, bf, gc, ga, wout_, a3, gu2_, d2_, af]

    gs_out = list(g_small)
    gs_out[4] = gs_out[4].reshape(final_norm.shape)
    grads_out = assemble(gs_out, [b[0] for b in big_out])
    delta_out = assemble(small_out[0], [b[1] for b in big_out])
    m_out = assemble(small_out[1], [b[2] for b in big_out])
    v_out = assemble(small_out[2], [b[3] for b in big_out])
    return (loss, grad_x, *grads_out, *delta_out, *m_out, *v_out)
```

```python
import functools

import jax
import jax.numpy as jnp
from jax import lax
from jax.experimental import pallas as pl
from jax.experimental.pallas import tpu as pltpu

F32 = jnp.float32
BF16 = jnp.bfloat16

EPS = 1e-6
N_META = 16
HEAD_DIM = 64
N_SHARD = 4
N_DEV = 8
HALO = 16
LANES = 128
SMALL_ROWS = 32
VMEM_LIMIT_V7X = 56 * 1024 * 1024
NEG = -1e30
ROW_BANDS = 2

ADAM_LR = 0.001
ADAM_B1 = 0.9
ADAM_B2 = 0.999
ADAM_EPS = 1e-08
ADAM_WD = 0.01
ADAM_STEP = 10

MESH = pl.DeviceIdType.MESH
ANY = pl.BlockSpec(memory_space=pl.ANY)
NT_DIMS = (((1,), (1,)), ((), ()))
TN_DIMS = (((0,), (0,)), ((), ()))


def _params(*sem):
    return pltpu.CompilerParams(dimension_semantics=sem, vmem_limit_bytes=VMEM_LIMIT_V7X)


class _Comm:
    def __init__(self, ins, out_shapes, sems, start, finish, aliases=None):
        self.ins, self.out_shapes, self.sems = list(ins), list(out_shapes), list(sems)
        self.start, self.finish, self.aliases = start, finish, dict(aliases or {})


def _join(a, b):
    ni, no, ns = len(a.ins), len(a.out_shapes), len(a.sems)

    def start(ins, outs, sems):
        a.start(ins[:ni], outs[:no], sems[:ns])
        b.start(ins[ni:], outs[no:], sems[ns:])

    def finish(ins, outs, sems):
        a.finish(ins[:ni], outs[:no], sems[:ns])
        b.finish(ins[ni:], outs[no:], sems[ns:])

    aliases = dict(a.aliases)
    aliases.update({ni + i: no + j for i, j in b.aliases.items()})
    return _Comm(a.ins + b.ins, a.out_shapes + b.out_shapes, a.sems + b.sems, start, finish, aliases)


def _launch(body, *, name, grid, in_specs, out_specs, out_shape, args, scratch_shapes=(), comm=None, prefetch=(),
            aliases=None):
    single = not isinstance(out_shape, (list, tuple))
    out_specs = [out_specs] if single else list(out_specs)
    out_shape = [out_shape] if single else list(out_shape)
    in_specs, scratch_shapes, prefetch = list(in_specs), list(scratch_shapes), list(prefetch)
    params = _params(*(("arbitrary",) * len(grid)))
    n_pf, n_in, n_out, n_scr = len(prefetch), len(in_specs), len(out_specs), len(scratch_shapes)
    c_ins = comm.ins if comm else []
    c_shapes = comm.out_shapes if comm else []
    c_sems = comm.sems if comm else []
    c_in, c_out = len(c_ins), len(c_shapes)

    def carrier(*refs):
        p = 0
        pf = refs[p:p + n_pf]; p += n_pf
        a = refs[p:p + n_in]; p += n_in
        ci = refs[p:p + c_in]; p += c_in
        o = refs[p:p + n_out]; p += n_out
        co = refs[p:p + c_out]; p += c_out
        s = refs[p:p + n_scr]; p += n_scr
        cs = refs[p:]
        if comm:
            first = functools.reduce(lambda u, v: u & v, [pl.program_id(k) == 0 for k in range(len(grid))])

            @pl.when(first)
            def _():
                comm.start(ci, co, cs)

        body(*pf, *a, *o, *s)

        if comm:
            last = functools.reduce(lambda u, v: u & v, [pl.program_id(k) == grid[k] - 1 for k in range(len(grid))])

            @pl.when(last)
            def _():
                comm.finish(ci, co, cs)

    io_aliases = {n_pf + i: j for i, j in (aliases or {}).items()}
    if comm:
        io_aliases.update({n_pf + n_in + i: n_out + j for i, j in comm.aliases.items()})
    all_in, all_out = in_specs + [ANY] * c_in, out_specs + [ANY] * c_out
    all_scratch = scratch_shapes + [pltpu.SemaphoreType.DMA((k,)) for k in c_sems]
    if n_pf:
        spec = dict(grid_spec=pltpu.PrefetchScalarGridSpec(
            num_scalar_prefetch=n_pf, grid=grid, in_specs=all_in, out_specs=all_out, scratch_shapes=all_scratch))
    else:
        spec = dict(grid=grid, in_specs=all_in, out_specs=all_out, scratch_shapes=all_scratch)
    res = pl.pallas_call(carrier, name=name, out_shape=out_shape + c_shapes, input_output_aliases=io_aliases,
                         compiler_params=params, **spec)(*prefetch, *args, *c_ins)
    main = list(res[:n_out])
    return (main[0] if single else main), (list(res[n_out:]) if comm else None)


def _run_comm(comm, *, name):
    c_in, c_out = len(comm.ins), len(comm.out_shapes)

    def body(*refs):
        ci, co, cs = refs[:c_in], refs[c_in:c_in + c_out], refs[c_in + c_out:]
        comm.start(ci, co, cs)
        comm.finish(ci, co, cs)

    return list(pl.pallas_call(
        body, name=name, in_specs=[ANY] * c_in, out_specs=[ANY] * c_out, out_shape=comm.out_shapes,
        scratch_shapes=[pltpu.SemaphoreType.DMA((k,)) for k in comm.sems],
        input_output_aliases=comm.aliases)(*comm.ins))


def _chunks(width, step=512):
    out, c0 = [], 0
    while c0 < width:
        cw = min(step, width - c0)
        out.append((c0, cw))
        c0 += cw
    return out


def _split2(v):
    hi = v.astype(BF16)
    lo = (v - hi.astype(F32)).astype(BF16)
    return hi, lo


def _split3(v):
    hi = v.astype(BF16)
    r = v - hi.astype(F32)
    mid = r.astype(BF16)
    lo = (r - mid.astype(F32)).astype(BF16)
    return hi, mid, lo


def _dot(a, b):
    return jnp.dot(a, b, preferred_element_type=F32)


def _dot_nt(a, b):
    return lax.dot_general(a, b, NT_DIMS, preferred_element_type=F32)


def _dot_tn(a, b):
    return lax.dot_general(a, b, TN_DIMS, preferred_element_type=F32)


def _silu_mul(g, u):
    return g * jax.nn.sigmoid(g) * u


def _rms_bwd(dn, h, gain, dres):
    r = lax.rsqrt(jnp.mean(h * h, axis=-1, keepdims=True) + EPS)
    y = h * r
    dgain = jnp.sum(dn * y, axis=0, keepdims=True)
    dy = dn * gain
    dh = dres + r * (dy - y * jnp.mean(dy * y, axis=-1, keepdims=True))
    return dh, dgain


def _group_mean(v, p):
    hi, lo = _split2(v)
    return _dot(hi, p) + _dot(lo, p)


def _row_of(a, k):
    rows = lax.broadcasted_iota(jnp.int32, a.shape, 0)
    return jnp.sum(jnp.where(rows == k, a, 0.0), axis=0, keepdims=True)


def _causal_conv(u, prev, w):
    rows = lax.broadcasted_iota(jnp.int32, u.shape, 0)
    p1 = _row_of(prev, HALO - 1)
    p2 = _row_of(prev, HALO - 2)
    u1 = jnp.where(rows == 0, p1, pltpu.roll(u, 1, 0))
    u2 = jnp.where(rows == 0, p2, jnp.where(rows == 1, p1, pltpu.roll(u, 2, 0)))
    return w[2:3, :] * u + w[1:2, :] * u1 + w[0:1, :] * u2, u1, u2


def _rms(x, gain):
    return (x * lax.rsqrt(jnp.mean(x * x, axis=-1, keepdims=True) + EPS) * gain).astype(BF16)


def _embed_norm(x, meta, g, *, tm, name, comm=None):
    B, S, D = x.shape
    L = S + N_META
    per_seq = L // tm
    nt = B * per_seq
    body_rows = tm - N_META

    def body(meta_ref, g_ref, x_hbm, h_ref, n_ref, buf, sems):
        i = pl.program_id(0)

        def fetch(k, fn):
            slot, b, t = k % 2, k // per_seq, k % per_seq

            @pl.when(t == 0)
            def _():
                fn(pltpu.make_async_copy(x_hbm.at[b, pl.ds(0, body_rows)],
                                         buf.at[slot, pl.ds(N_META, body_rows)], sems.at[slot]))

            @pl.when(t != 0)
            def _():
                fn(pltpu.make_async_copy(x_hbm.at[b, pl.ds(pl.multiple_of(t * tm - N_META, 8), tm)],
                                         buf.at[slot], sems.at[slot]))

        @pl.when(i == 0)
        def _():
            fetch(i, lambda cp: cp.start())

        @pl.when(i + 1 < nt)
        def _():
            fetch(i + 1, lambda cp: cp.start())

        fetch(i, lambda cp: cp.wait())
        slot = i % 2

        @pl.when(i % per_seq == 0)
        def _():
            buf[slot, 0:N_META, :] = meta_ref[...]

        hv = buf[slot]
        h_ref[...] = hv
        n_ref[...] = _rms(hv, g_ref[...])

    row = pl.BlockSpec((tm, D), lambda i: (i, 0))
    return _launch(
        body, name=name, grid=(nt,),
        in_specs=[pl.BlockSpec((N_META, D), lambda i: (0, 0)), pl.BlockSpec((1, D), lambda i: (0, 0)), ANY],
        out_specs=[row, row],
        out_shape=[jax.ShapeDtypeStruct((B * L, D), F32), jax.ShapeDtypeStruct((B * L, D), BF16)],
        scratch_shapes=[pltpu.VMEM((2, tm, D), F32), pltpu.SemaphoreType.DMA((2,))],
        args=(meta, g, x), comm=comm)


def _ffn_up(n, wgu, sid, gu_prev, *, tm, first, count, name, comm=None):
    T, D = n.shape
    ns, _, guc = wgu.shape
    ff = N_SHARD * guc // 2

    def body(sid_ref, x_ref, w_ref, *rest):
        rest[-1][...] = _dot(x_ref[...], w_ref[...]).astype(BF16)

    where = lambda s, sid: sid[first + s]
    w_at = (lambda s, sid: 0) if ns == 1 else where
    return _launch(
        body, name=name, grid=(count, T // tm), prefetch=(sid,),
        in_specs=[pl.BlockSpec((tm, D), lambda s, i, sid: (i, 0)),
                  pl.BlockSpec((None, D, guc), lambda s, i, sid: (w_at(s, sid), 0, 0))]
                 + ([] if gu_prev is None else [ANY]),
        out_specs=pl.BlockSpec((None, tm, guc), lambda s, i, sid: (where(s, sid) // 2, i, where(s, sid) % 2)),
        out_shape=jax.ShapeDtypeStruct((2, T, ff), BF16),
        args=(n, wgu) + (() if gu_prev is None else (gu_prev,)),
        aliases=None if gu_prev is None else {2: 0}, comm=comm)


def _matmul_nn(x, w, *, tm, nb, w_spec, out_shape, out_spec, name, comm=None):
    T, K = x.shape

    def body(x_ref, w_ref, o_ref):
        o_ref[...] = _dot(x_ref[...], w_ref[...]).astype(o_ref.dtype)

    return _launch(
        body, name=name, grid=(nb, T // tm),
        in_specs=[pl.BlockSpec((tm, K), lambda s, i: (i, 0)), w_spec],
        out_specs=out_spec, out_shape=out_shape, args=(x, w), comm=comm)


def _mix_in(n, w_main, w_fg, *, tm, nb, name):
    T, D = n.shape
    n_main = w_main.shape[1]
    bw = n_main // nb
    W = w_fg.shape[1]

    def body(x_ref, w_ref, wf_ref, o_ref, fg_ref):
        x = x_ref[...]
        o_ref[...] = _dot(x, w_ref[...]).astype(BF16)

        @pl.when(pl.program_id(1) == 0)
        def _():
            fg_ref[...] = _dot(x, wf_ref[...])

    res, _ = _launch(
        body, name=name, grid=(T // tm, nb),
        in_specs=[pl.BlockSpec((tm, D), lambda i, s: (i, 0)), pl.BlockSpec((D, bw), lambda i, s: (0, s)),
                  pl.BlockSpec((D, W), lambda i, s: (0, 0))],
        out_specs=[pl.BlockSpec((tm, bw), lambda i, s: (i, s)), pl.BlockSpec((tm, W), lambda i, s: (i, 0))],
        out_shape=[jax.ShapeDtypeStruct((T, n_main), BF16), jax.ShapeDtypeStruct((T, W), F32)],
        args=(n, w_main, w_fg))
    return res


def _down_in_bands(g_ref, u_ref, wd_v, edges, chunks, finish):
    def down(rows):
        def act(c0, cw):
            return _silu_mul(g_ref[rows, c0:c0 + cw].astype(F32), u_ref[rows, c0:c0 + cw].astype(F32)).astype(BF16)

        acc = None
        nxt = act(*chunks[0])
        for k, (c0, cw) in enumerate(chunks):
            a = nxt
            if k + 1 < len(chunks):
                nxt = act(*chunks[k + 1])
            d = _dot(a, wd_v[c0:c0 + cw, :])
            acc = d if acc is None else acc + d
        return acc

    bands = [slice(r0, r1) for r0, r1 in zip(edges[:-1], edges[1:])]
    nxt = down(bands[0])
    for b, rows in enumerate(bands):
        acc = nxt
        if b + 1 < len(bands):
            nxt = down(bands[b + 1])
        finish(rows, acc)


def _ffn_down(gu, wd, h, next_gain, *, tm, name, comm=None):
    _, T, ff = gu.shape
    D = h.shape[1]
    chunks = _chunks(ff)

    def body(g_ref, u_ref, wd_hbm, h_ref, ng_ref, o_ref, n_ref, wd_v, sem):
        @pl.when(pl.program_id(0) == 0)
        def _():
            cp = pltpu.make_async_copy(wd_hbm, wd_v, sem)
            cp.start()
            cp.wait()

        def finish(rows, acc):
            out = h_ref[rows, :] + 0.5 * acc
            o_ref[rows, :] = out
            n_ref[rows, :] = _rms(out, ng_ref[...])

        _down_in_bands(g_ref, u_ref, wd_v, _band_edges(tm), chunks, finish)

    return _launch(
        body, name=name, grid=(T // tm,),
        in_specs=[pl.BlockSpec((None, tm, ff), lambda i: (0, i, 0)),
                  pl.BlockSpec((None, tm, ff), lambda i: (1, i, 0)),
                  ANY,
                  pl.BlockSpec((tm, D), lambda i: (i, 0)),
                  pl.BlockSpec((1, D), lambda i: (0, 0))],
        out_specs=[pl.BlockSpec((tm, D), lambda i: (i, 0)), pl.BlockSpec((tm, D), lambda i: (i, 0))],
        out_shape=[jax.ShapeDtypeStruct((T, D), F32), jax.ShapeDtypeStruct((T, D), BF16)],
        scratch_shapes=[pltpu.VMEM((ff, D), BF16), pltpu.SemaphoreType.DMA],
        args=(gu, gu, wd, h, next_gain), comm=comm)


def _ffn_down_loss(gu, wd, h, gf, tgt, *, tm, name, comm=None):
    _, T, ff = gu.shape
    D = h.shape[1]
    B, S, _ = tgt.shape
    per_seq = (S + N_META) // tm
    body_rows = tm - N_META
    chunks = _chunks(ff)

    def body(g_ref, u_ref, wd_hbm, h_ref, gf_ref, tgt_hbm, dh_ref, dhb_ref, dg_ref, loss_ref, wd_v, tg_v, sem, tsem):
        i = pl.program_id(0)
        b, t = i // per_seq, i % per_seq

        @pl.when(i == 0)
        def _():
            cp = pltpu.make_async_copy(wd_hbm, wd_v, sem)
            cp.start()
            cp.wait()
            dg_ref[...] = jnp.zeros_like(dg_ref)
            loss_ref[...] = jnp.zeros_like(loss_ref)
            tg_v[0:N_META, :] = jnp.zeros((N_META, D), F32)

        def fetch(fn):
            @pl.when(t == 0)
            def _():
                fn(pltpu.make_async_copy(tgt_hbm.at[b, pl.ds(0, body_rows)], tg_v.at[pl.ds(N_META, body_rows)], tsem))

            @pl.when(t != 0)
            def _():
                fn(pltpu.make_async_copy(tgt_hbm.at[b, pl.ds(pl.multiple_of(t * tm - N_META, 8), tm)], tg_v, tsem))

        fetch(lambda cp: cp.start())

        def finish(rows, acc):
            if rows.start == 0:
                fetch(lambda cp: cp.wait())
            x = h_ref[rows, :] + 0.5 * acc
            gain = gf_ref[...]
            r = lax.rsqrt(jnp.mean(x * x, axis=-1, keepdims=True) + EPS)
            y = x * r
            pos = t * tm + rows.start + lax.broadcasted_iota(jnp.int32, (rows.stop - rows.start, 1), 0)
            err = jnp.where(pos >= N_META, y * gain - tg_v[rows, :], 0.0)
            loss_ref[...] += 0.5 * jnp.sum(jnp.mean(err * err, axis=-1, keepdims=True))
            dout = err / D
            dg_ref[...] += jnp.sum(dout * y, axis=0, keepdims=True)
            dy = dout * gain
            dh = r * (dy - y * jnp.mean(dy * y, axis=-1, keepdims=True))
            dh_ref[rows, :] = dh
            dhb_ref[rows, :] = (0.5 * dh).astype(BF16)

        _down_in_bands(g_ref, u_ref, wd_v, _band_edges(tm), chunks, finish)

    row = pl.BlockSpec((tm, D), lambda i: (i, 0))
    const = lambda i: (0, 0)
    return _launch(
        body, name=name, grid=(T // tm,),
        in_specs=[pl.BlockSpec((None, tm, ff), lambda i: (0, i, 0)),
                  pl.BlockSpec((None, tm, ff), lambda i: (1, i, 0)),
                  ANY, row, pl.BlockSpec((1, D), const), ANY],
        out_specs=[row, row, pl.BlockSpec((1, D), const), pl.BlockSpec((1, LANES), const)],
        out_shape=[jax.ShapeDtypeStruct((T, D), F32), jax.ShapeDtypeStruct((T, D), BF16),
                   jax.ShapeDtypeStruct((1, D), F32), jax.ShapeDtypeStruct((1, LANES), F32)],
        scratch_shapes=[pltpu.VMEM((ff, D), BF16), pltpu.VMEM((tm, D), F32), pltpu.SemaphoreType.DMA,
                        pltpu.SemaphoreType.DMA],
        args=(gu, gu, wd, h, gf, tgt), comm=comm)


def _ffn_bwd_act(df, gu, wd, *, tm, guc, name, comm=None):
    _, T, ff = gu.shape
    D = df.shape[1]
    nj = ff // guc
    chunks = _chunks(guc)

    def body(df_ref, g_ref, u_ref, wd_ref, o_ref, dwd_ref):
        @pl.when(pl.program_id(1) == 0)
        def _():
            dwd_ref[...] = jnp.zeros_like(dwd_ref)

        dfv = df_ref[...]
        nxt = _dot_nt(dfv, wd_ref[chunks[0][0]:chunks[0][0] + chunks[0][1], :])
        for k, (c0, cw) in enumerate(chunks):
            da = nxt
            if k + 1 < len(chunks):
                n0, nw = chunks[k + 1]
                nxt = _dot_nt(dfv, wd_ref[n0:n0 + nw, :])
            g = g_ref[:, c0:c0 + cw].astype(F32)
            u = u_ref[:, c0:c0 + cw].astype(F32)
            sg = jax.nn.sigmoid(g)
            silu = g * sg
            o_ref[0, :, c0:c0 + cw] = (da * u * (sg * (1.0 + g * (1.0 - sg)))).astype(BF16)
            o_ref[1, :, c0:c0 + cw] = (da * silu).astype(BF16)
            dwd_ref[c0:c0 + cw, :] += _dot_tn((silu * u).astype(BF16), dfv)

    return _launch(
        body, name=name, grid=(nj, T // tm),
        in_specs=[pl.BlockSpec((tm, D), lambda j, i: (i, 0)),
                  pl.BlockSpec((None, tm, guc), lambda j, i: (0, i, j)),
                  pl.BlockSpec((None, tm, guc), lambda j, i: (1, i, j)),
                  pl.BlockSpec((guc, D), lambda j, i: (j, 0))],
        out_specs=[pl.BlockSpec((2, tm, guc), lambda j, i: (0, i, j)), pl.BlockSpec((guc, D), lambda j, i: (j, 0))],
        out_shape=[jax.ShapeDtypeStruct((2, T, ff), BF16), jax.ShapeDtypeStruct((ff, D), F32)],
        args=(df, gu, gu, wd), comm=comm)


def _ffn_bwd_in(dgu, wgu, h, g, dres, *, tm, scale, name, comm=None):
    _, T, ff = dgu.shape
    ns, D, guc = wgu.shape
    nj = ff // guc
    edges = _band_edges(tm)

    def body(dgu_ref, w_hbm, h_ref, g_ref, dres_ref, dh_ref, dhb_ref, dg_ref, w_v, acc, sem):
        i, j = pl.program_id(0), pl.program_id(1)

        @pl.when((i == 0) & (j == 0))
        def _():
            cp = pltpu.make_async_copy(w_hbm, w_v, sem)
            cp.start()
            cp.wait()
            dg_ref[...] = jnp.zeros_like(dg_ref)

        def dots(rows):
            return _dot_nt(dgu_ref[0, rows, :], w_v[j]) + _dot_nt(dgu_ref[1, rows, :], w_v[nj + j])

        @pl.when(j < nj - 1)
        def _():
            part = dots(slice(None))

            @pl.when(j == 0)
            def _():
                acc[...] = part

            @pl.when(j > 0)
            def _():
                acc[...] += part

        @pl.when(j == nj - 1)
        def _():
            bands = [slice(r0, r1) for r0, r1 in zip(edges[:-1], edges[1:])]
            nxt = dots(bands[0])
            for b, rows in enumerate(bands):
                dn = nxt if nj == 1 else acc[rows, :] + nxt
                if b + 1 < len(bands):
                    nxt = dots(bands[b + 1])
                dh, dgain = _rms_bwd(dn, h_ref[rows, :], g_ref[...], dres_ref[rows, :])
                dh_ref[rows, :] = dh
                dhb_ref[rows, :] = (scale * dh).astype(BF16)
                dg_ref[...] += dgain

    return _launch(
        body, name=name, grid=(T // tm, nj),
        in_specs=[pl.BlockSpec((2, tm, guc), lambda i, j: (0, i, j)),
                  ANY,
                  pl.BlockSpec((tm, D), lambda i, j: (i, 0)),
                  pl.BlockSpec((1, D), lambda i, j: (0, 0)),
                  pl.BlockSpec((tm, D), lambda i, j: (i, 0))],
        out_specs=[pl.BlockSpec((tm, D), lambda i, j: (i, 0)),
                   pl.BlockSpec((tm, D), lambda i, j: (i, 0)),
                   pl.BlockSpec((1, D), lambda i, j: (0, 0))],
        out_shape=[jax.ShapeDtypeStruct((T, D), F32), jax.ShapeDtypeStruct((T, D), BF16),
                   jax.ShapeDtypeStruct((1, D), F32)],
        scratch_shapes=[pltpu.VMEM((ns, D, guc), BF16), pltpu.VMEM((tm, D), F32), pltpu.SemaphoreType.DMA],
        args=(dgu, wgu, h, g, dres), comm=comm)


def _ffn_bwd_in_first(dgu, wgu, h, g, dres, *, tm, batch, name, comm=None):
    _, T, ff = dgu.shape
    ns, D, guc = wgu.shape
    nj = ff // guc
    nt = T // tm
    L = T // batch
    per_seq = L // tm
    body_rows = tm - N_META
    edges = _band_edges(tm)

    def body(dgu_ref, w_hbm, h_ref, g_ref, dres_ref, dx_hbm, dmeta_ref, dg_ref, w_v, acc, dh_v, sem, osem):
        i, j = pl.program_id(0), pl.program_id(1)

        @pl.when((i == 0) & (j == 0))
        def _():
            cp = pltpu.make_async_copy(w_hbm, w_v, sem)
            cp.start()
            cp.wait()
            dg_ref[...] = jnp.zeros_like(dg_ref)
            dmeta_ref[...] = jnp.zeros_like(dmeta_ref)

        def dots(rows):
            return _dot_nt(dgu_ref[0, rows, :], w_v[j]) + _dot_nt(dgu_ref[1, rows, :], w_v[nj + j])

        @pl.when(j < nj - 1)
        def _():
            part = dots(slice(None))

            @pl.when(j == 0)
            def _():
                acc[...] = part

            @pl.when(j > 0)
            def _():
                acc[...] += part

        def head_copy(b):
            return pltpu.make_async_copy(dh_v.at[pl.ds(N_META, body_rows)], dx_hbm.at[b, pl.ds(0, body_rows)], osem)

        def tail_copy(b, t):
            return pltpu.make_async_copy(dh_v, dx_hbm.at[b, pl.ds(pl.multiple_of(t * tm - N_META, 8), tm)], osem)

        def on_tile(k, head_fn, tail_fn):
            @pl.when(k % per_seq == 0)
            def _():
                head_fn(head_copy(k // per_seq))

            @pl.when(k % per_seq != 0)
            def _():
                tail_fn(tail_copy(k // per_seq, k % per_seq))

        @pl.when(j == nj - 1)
        def _():
            @pl.when(i > 0)
            def _():
                on_tile(i - 1, lambda cp: cp.wait(), lambda cp: cp.wait())

            bands = [slice(r0, r1) for r0, r1 in zip(edges[:-1], edges[1:])]
            nxt = dots(bands[0])
            for b, rows in enumerate(bands):
                dn = nxt if nj == 1 else acc[rows, :] + nxt
                if b + 1 < len(bands):
                    nxt = dots(bands[b + 1])
                dh, dgain = _rms_bwd(dn, h_ref[rows, :], g_ref[...], dres_ref[rows, :])
                dg_ref[...] += dgain
                dh_v[rows, :] = dh
                if b == 0:
                    @pl.when(i % per_seq == 0)
                    def _():
                        dmeta_ref[...] += dh[0:N_META, :]

            on_tile(i, lambda cp: cp.start(), lambda cp: cp.start())

            @pl.when(i == nt - 1)
            def _():
                on_tile(i, lambda cp: cp.wait(), lambda cp: cp.wait())

    return _launch(
        body, name=name, grid=(nt, nj),
        in_specs=[pl.BlockSpec((2, tm, guc), lambda i, j: (0, i, j)),
                  ANY,
                  pl.BlockSpec((tm, D), lambda i, j: (i, 0)),
                  pl.BlockSpec((1, D), lambda i, j: (0, 0)),
                  pl.BlockSpec((tm, D), lambda i, j: (i, 0))],
        out_specs=[ANY, pl.BlockSpec((N_META, D), lambda i, j: (0, 0)), pl.BlockSpec((1, D), lambda i, j: (0, 0))],
        out_shape=[jax.ShapeDtypeStruct((batch, L - N_META, D), F32), jax.ShapeDtypeStruct((N_META, D), F32),
                   jax.ShapeDtypeStruct((1, D), F32)],
        scratch_shapes=[pltpu.VMEM((ns, D, guc), BF16), pltpu.VMEM((tm, D), F32), pltpu.VMEM((tm, D), F32),
                        pltpu.SemaphoreType.DMA, pltpu.SemaphoreType.DMA],
        args=(dgu, wgu, h, g, dres), comm=comm)


def _mix_bwd_in(parts, w_main, w_fg, h, g, dres, *, tm, scale, name, comm=None):
    T, D = h.shape
    widths = [p.shape[1] for p in parts]
    offs = [sum(widths[:k]) for k in range(len(widths))]
    npart = len(parts)
    wide = sum(widths)
    edges = _band_edges(tm)

    def body(*refs):
        p_refs = refs[:npart]
        wm_ref, wf_ref, h_ref, g_ref, dres_ref, dh_ref, dhb_ref, dg_ref, all_ref = refs[npart:]

        @pl.when(pl.program_id(0) == 0)
        def _():
            dg_ref[...] = jnp.zeros_like(dg_ref)

        for p_ref, off, wd_ in zip(p_refs, offs, widths):
            for c0, cw in _chunks(wd_):
                all_ref[:, off + c0:off + c0 + cw] = p_ref[:, c0:c0 + cw].astype(BF16)
        n_main = offs[-1]

        def dots(rows):
            return _dot_nt(all_ref[rows, :n_main], wm_ref[...]) + _dot_nt(all_ref[rows, n_main:], wf_ref[...])

        bands = [slice(r0, r1) for r0, r1 in zip(edges[:-1], edges[1:])]
        nxt = dots(bands[0])
        for b, rows in enumerate(bands):
            dn = nxt
            if b + 1 < len(bands):
                nxt = dots(bands[b + 1])
            dh, dgain = _rms_bwd(dn, h_ref[rows, :], g_ref[...], dres_ref[rows, :])
            dh_ref[rows, :] = dh
            dhb_ref[rows, :] = (scale * dh).astype(BF16)
            dg_ref[...] += dgain

    row = lambda i: (i, 0)
    const = lambda i: (0, 0)
    return _launch(
        body, name=name, grid=(T // tm,),
        in_specs=[pl.BlockSpec((tm, p.shape[1]), row) for p in parts]
                 + [pl.BlockSpec(w_main.shape, const), pl.BlockSpec(w_fg.shape, const),
                    pl.BlockSpec((tm, D), row), pl.BlockSpec((1, D), const), pl.BlockSpec((tm, D), row)],
        out_specs=[pl.BlockSpec((tm, D), row), pl.BlockSpec((tm, D), row), pl.BlockSpec((1, D), const),
                   pl.BlockSpec((tm, wide), row)],
        out_shape=[jax.ShapeDtypeStruct((T, D), F32), jax.ShapeDtypeStruct((T, D), BF16),
                   jax.ShapeDtypeStruct((1, D), F32), jax.ShapeDtypeStruct((T, wide), BF16)],
        args=(*parts, w_main, w_fg, h, g, dres), comm=comm)


def _matmul_tn(x, y, *, tm, nb, x_spec, y_spec, out_shape, out_spec, kb, name, comm=None):
    T = y.shape[-2]
    chunks = _chunks(kb)

    def body(x_ref, y_ref, o_ref):
        @pl.when(pl.program_id(1) == 0)
        def _():
            o_ref[...] = jnp.zeros_like(o_ref)

        yv = y_ref[...].astype(BF16)
        nxt = _dot_tn(x_ref[:, chunks[0][0]:chunks[0][0] + chunks[0][1]], yv)
        for k, (c0, cw) in enumerate(chunks):
            cur = nxt
            if k + 1 < len(chunks):
                n0, nw = chunks[k + 1]
                nxt = _dot_tn(x_ref[:, n0:n0 + nw], yv)
            o_ref[c0:c0 + cw, :] += cur

    return _launch(
        body, name=name, grid=(nb, T // tm),
        in_specs=[x_spec, y_spec], out_specs=out_spec, out_shape=out_shape, args=(x, y), comm=comm)


def _tri(n, lower):
    r = lax.broadcasted_iota(jnp.int32, (n, n), 0)
    c = lax.broadcasted_iota(jnp.int32, (n, n), 1)
    return jnp.where((r >= c) if lower else (r <= c), 1.0, 0.0).astype(BF16)


def _tri_dot(tri, v):
    hi, mid, lo = _split3(v)
    return _dot(tri, hi) + _dot(tri, mid) + _dot(tri, lo)


def _fcum(fg, bf, *, ch, name):
    B, L, W = fg.shape
    nch = L // ch

    def body(fg_ref, bf_ref, f_ref):
        tri = _tri(ch, True)
        carry = jnp.zeros((1, W), F32)
        for c in range(nch):
            x = fg_ref[c * ch:(c + 1) * ch, :] + bf_ref[...]
            lf = jnp.minimum(x, 0.0) - jnp.log(1.0 + jnp.exp(-jnp.abs(x)))
            f_ref[c * ch:(c + 1) * ch, :] = _tri_dot(tri, lf) + carry
            carry = carry + jnp.sum(lf, axis=0, keepdims=True)

    return pl.pallas_call(
        body, name=name, grid=(B,),
        in_specs=[pl.BlockSpec((None, L, W), lambda b: (b, 0, 0)), pl.BlockSpec((1, W), lambda b: (0, 0))],
        out_specs=pl.BlockSpec((None, L, W), lambda b: (b, 0, 0)),
        out_shape=jax.ShapeDtypeStruct((B, L, W), F32),
        compiler_params=_params("arbitrary"),
    )(fg, bf)


def _fcum_bwd(dF_rows, dF_cols, fg, bf, *, ch, name):
    B, L, W = fg.shape
    nch = L // ch

    def body(dfr_ref, dfc_ref, fg_ref, bf_ref, dfg_ref, db_ref):
        @pl.when(pl.program_id(0) == 0)
        def _():
            db_ref[...] = jnp.zeros_like(db_ref)

        tri = _tri(ch, False)
        carry = jnp.zeros((1, W), F32)
        dbs = jnp.zeros((1, W), F32)
        for c in reversed(range(nch)):
            d = dfr_ref[c * ch:(c + 1) * ch, :] - dfc_ref[c * ch:(c + 1) * ch, :]
            dlf = _tri_dot(tri, d) + carry
            carry = carry + jnp.sum(d, axis=0, keepdims=True)
            x = fg_ref[c * ch:(c + 1) * ch, :] + bf_ref[...]
            dfg = dlf * jax.nn.sigmoid(-x)
            dfg_ref[c * ch:(c + 1) * ch, :] = dfg.astype(BF16)
            dbs = dbs + jnp.sum(dfg, axis=0, keepdims=True)
        db_ref[...] += dbs

    blk = pl.BlockSpec((None, L, W), lambda b: (b, 0, 0))
    return pl.pallas_call(
        body, name=name, grid=(B,),
        in_specs=[blk, blk, blk, pl.BlockSpec((1, W), lambda b: (0, 0))],
        out_specs=[blk, pl.BlockSpec((1, W), lambda b: (0, 0))],
        out_shape=[jax.ShapeDtypeStruct((B, L, W), BF16), jax.ShapeDtypeStruct((1, W), F32)],
        compiler_params=_params("arbitrary"),
    )(dF_rows, dF_cols, fg, bf)


def _band_edges(tq):
    return sorted({min(tq, (k * tq // ROW_BANDS + HALO - 1) // HALO * HALO) for k in range(ROW_BANDS + 1)})


def _pair(h):
    return slice((h // 2) * 2 * HEAD_DIM, (h // 2 + 1) * 2 * HEAD_DIM)


def _own_lanes(a, h):
    low = lax.broadcasted_iota(jnp.int32, a.shape, 1) < HEAD_DIM
    return jnp.where(low if h % 2 == 0 else jnp.logical_not(low), a, jnp.zeros_like(a))


def _sum_lane(h):
    return HEAD_DIM if h % 2 == 0 else 0


def _own_lanes_and_ones(a, h):
    lane = lax.broadcasted_iota(jnp.int32, a.shape, 1)
    low = lane < HEAD_DIM
    return jnp.where(low if h % 2 == 0 else jnp.logical_not(low), a,
                     jnp.where(lane == _sum_lane(h), jnp.ones_like(a), jnp.zeros_like(a)))


def _attn_fwd(proj, fr, *, tq, n_heads, name, comm=None):
    B, L, _ = proj.shape
    AD = n_heads * HEAD_DIM
    nq = L // tq
    W = LANES
    scale = HEAD_DIM ** -0.5
    edges = _band_edges(tq)

    v_ones, sum_lane = _own_lanes_and_ones, _sum_lane

    def body(q_ref, k_ref, v_ref, fr_ref, o_ref, lse_ref, m_s, acc_s):
        qi, ki = pl.program_id(1), pl.program_id(2)

        @pl.when(ki == 0)
        def _():
            m_s[...] = jnp.full_like(m_s, NEG)
            acc_s[...] = jnp.zeros_like(acc_s)

        def tile(diagonal):
            lane = lax.broadcasted_iota(jnp.int32, (tq, W), 1)
            m_all = m_s[...]
            m_out = m_all
            bands = [(r0, r1, r1 if diagonal else tq) for r0, r1 in zip(edges[:-1], edges[1:])]
            if diagonal:
                masks = {r0: (lax.broadcasted_iota(jnp.int32, (r1 - r0, c1), 1)
                              <= r0 + lax.broadcasted_iota(jnp.int32, (r1 - r0, c1), 0)) for r0, r1, c1 in bands}

            def scores(h, band):
                r0, r1, c1 = band
                sl = slice(h * HEAD_DIM, (h + 1) * HEAD_DIM)
                return _dot_nt(q_ref[r0:r1, sl] * scale, k_ref[0:c1, sl])

            work = [(h, band) for h in range(n_heads) for band in bands]
            nxt = scores(*work[0])
            for w, (h, band) in enumerate(work):
                r0, r1, c1 = band
                sl = slice(h * HEAD_DIM, (h + 1) * HEAD_DIM)
                s = nxt - fr_ref[h:h + 1, 0:c1]
                if w + 1 < len(work):
                    nxt = scores(*work[w + 1])
                if diagonal:
                    s = jnp.where(masks[r0], s, NEG)
                m_old = m_all[r0:r1, h:h + 1]
                m_new = jnp.maximum(m_old, jnp.max(s, axis=1, keepdims=True))
                alpha = jnp.exp(m_old - m_new)
                p = jnp.exp(s - m_new)
                own = slice(h * 2 * HEAD_DIM, (h + 1) * 2 * HEAD_DIM)
                acc_s[r0:r1, own] = alpha * acc_s[r0:r1, own] + _dot(p.astype(BF16), v_ones(v_ref[0:c1, _pair(h)], h))
                if r0 == 0:
                    m_parts = []
                m_parts.append(m_new)
                if r1 == tq:
                    m_out = jnp.where(lane == h, jnp.concatenate(m_parts, axis=0), m_out)
            m_s[...] = m_out

        @pl.when(ki < qi)
        def _():
            tile(False)

        @pl.when(ki == qi)
        def _():
            tile(True)
            lane = lax.broadcasted_iota(jnp.int32, (tq, W), 1)
            low = lax.broadcasted_iota(jnp.int32, (tq, 2 * HEAD_DIM), 1) < HEAD_DIM
            l_all = jnp.ones((tq, W), F32)
            for h in range(0, n_heads, 2):
                even = acc_s[:, h * 2 * HEAD_DIM:(h + 1) * 2 * HEAD_DIM]
                odd = acc_s[:, (h + 1) * 2 * HEAD_DIM:(h + 2) * 2 * HEAD_DIM]
                l_even = even[:, sum_lane(h):sum_lane(h) + 1]
                l_odd = odd[:, sum_lane(h + 1):sum_lane(h + 1) + 1]
                o_ref[:, _pair(h)] = jnp.where(low, even / l_even, odd / l_odd)
                l_all = jnp.where(lane == h, l_even, jnp.where(lane == h + 1, l_odd, l_all))
            lse_ref[...] = jnp.where(lane < n_heads, m_s[...] + jnp.log(l_all), 0.0)

    kv = lambda b, qi, ki: jnp.minimum(ki, qi)
    return _launch(
        body, name=name, grid=(B, nq, nq), args=(proj, proj, proj, fr), comm=comm,
        in_specs=[pl.BlockSpec((None, tq, AD), lambda b, qi, ki: (b, qi, 3)),
                  pl.BlockSpec((None, tq, AD), lambda b, qi, ki: (b, kv(b, qi, ki), 4)),
                  pl.BlockSpec((None, tq, AD), lambda b, qi, ki: (b, kv(b, qi, ki), 5)),
                  pl.BlockSpec((None, None, n_heads, tq), lambda b, qi, ki: (b, kv(b, qi, ki), 0, 0))],
        out_specs=[pl.BlockSpec((None, tq, AD), lambda b, qi, ki: (b, qi, 0)),
                   pl.BlockSpec((None, tq, W), lambda b, qi, ki: (b, qi, 0))],
        out_shape=[jax.ShapeDtypeStruct((B, L, AD), F32), jax.ShapeDtypeStruct((B, L, W), F32)],
        scratch_shapes=[pltpu.VMEM((tq, W), F32), pltpu.VMEM((tq, n_heads * 2 * HEAD_DIM), F32)])


def _attn_bwd(proj, o, do, lse, fr, *, tq, n_heads, name, comm=None):
    B, L, _ = proj.shape
    AD = n_heads * HEAD_DIM
    nq = L // tq
    W = LANES
    HW = 2 * HEAD_DIM
    scale = HEAD_DIM ** -0.5
    edges = _band_edges(tq)

    def body(q_ref, k_ref, v_ref, o_ref, do_ref, lse_ref, fr_ref,
             dq_ref, dk_ref, dv_ref, dfk_ref, dfq_ref, dq_s, dk_s, dv_s):
        kj, qi = pl.program_id(1), pl.program_id(2)

        @pl.when((kj == 0) & (qi == 0))
        def _():
            dq_s[...] = jnp.zeros_like(dq_s)

        @pl.when(qi == kj)
        def _():
            dk_s[...] = jnp.zeros_like(dk_s)
            dv_s[...] = jnp.zeros_like(dv_s)

        def tile(diagonal):
            bands = [(r0, r1, r1) for r0, r1 in zip(edges[:-1], edges[1:])] if diagonal else [(0, tq, tq)]
            lse = lse_ref[...]
            for r0, r1, c1 in bands:
                nr = r1 - r0
                rows = pl.ds(pl.multiple_of(qi * tq + r0, 8), nr)
                if diagonal:
                    mask = (lax.broadcasted_iota(jnp.int32, (nr, c1), 1)
                            <= r0 + lax.broadcasted_iota(jnp.int32, (nr, c1), 0))
                def scores(h):
                    ps = _pair(h)
                    k = k_ref[0:c1, ps]
                    qs = q_ref[r0:r1, ps] * scale
                    dov = _own_lanes(do_ref[r0:r1, ps], h)
                    return _dot_nt(_own_lanes(qs, h), k), _dot_nt(dov, v_ref[0:c1, ps]), k, qs, dov

                nxt = scores(0)
                for h in range(n_heads):
                    ps = _pair(h)
                    own = slice(h * HW, (h + 1) * HW)
                    s, dp, k, qs, dov = nxt
                    if h + 1 < n_heads:
                        nxt = scores(h + 1)
                    s = s - fr_ref[h:h + 1, 0:c1]
                    if diagonal:
                        s = jnp.where(mask, s, NEG)
                    p = jnp.exp(s - lse[r0:r1, h:h + 1])
                    dsum = jnp.sum(dov.astype(F32) * o_ref[r0:r1, ps], axis=1, keepdims=True)
                    dsb = (p * (dp - dsum)).astype(BF16)
                    dv = _dot_tn(p.astype(BF16), dov)
                    dk_s[0:c1, own] += _dot_tn(dsb, _own_lanes_and_ones(qs, h))
                    dq_s[rows, own] += _dot(dsb, _own_lanes_and_ones(k, h))
                    if h % 2 == 0:
                        dv_even = dv
                    else:
                        dv_s[0:c1, ps] += dv_even + dv

        def compact(acc, data_scale):
            rows = acc.shape[0]
            low = lax.broadcasted_iota(jnp.int32, (rows, HW), 1) < HEAD_DIM
            lane = lax.broadcasted_iota(jnp.int32, (rows, W), 1)
            vals, sums = [], jnp.zeros((rows, W), F32)
            for h in range(0, n_heads, 2):
                even, odd = acc[:, h * HW:(h + 1) * HW], acc[:, (h + 1) * HW:(h + 2) * HW]
                vals.append(jnp.where(low, even, odd) * data_scale)
                sums = jnp.where(lane == h, even[:, _sum_lane(h):_sum_lane(h) + 1],
                                 jnp.where(lane == h + 1, odd[:, _sum_lane(h + 1):_sum_lane(h + 1) + 1], sums))
            return vals, sums

        @pl.when(qi > kj)
        def _():
            tile(False)

        @pl.when(qi == kj)
        def _():
            tile(True)
            rows = pl.ds(pl.multiple_of(qi * tq, 8), tq)
            vals, sums = compact(dq_s[rows, :], scale)
            for h in range(0, n_heads, 2):
                dq_ref[rows, _pair(h)] = vals[h // 2]
            dfq_ref[rows, :] = sums

        @pl.when(qi == nq - 1)
        def _():
            vals, sums = compact(dk_s[...], 1.0)
            for h in range(0, n_heads, 2):
                dk_ref[:, _pair(h)] = vals[h // 2].astype(BF16)
            dfk_ref[...] = sums
            dv_ref[...] = dv_s[...].astype(BF16)

    qq = lambda b, kj, qi: jnp.maximum(qi, kj)
    qblk = lambda w, cb: pl.BlockSpec((None, tq, w), lambda b, kj, qi: (b, qq(b, kj, qi), cb))
    kblk = lambda w, cb: pl.BlockSpec((None, tq, w), lambda b, kj, qi: (b, kj, cb))
    return _launch(
        body, name=name, grid=(B, nq, nq), args=(proj, proj, proj, o, do, lse, fr), comm=comm,
        in_specs=[qblk(AD, 3), kblk(AD, 4), kblk(AD, 5), qblk(AD, 0), qblk(AD, 0), qblk(W, 0),
                  pl.BlockSpec((None, None, n_heads, tq), lambda b, kj, qi: (b, kj, 0, 0))],
        out_specs=[pl.BlockSpec((None, L, AD), lambda b, kj, qi: (b, 0, 0)),
                   kblk(AD, 0), kblk(AD, 0), kblk(W, 0),
                   pl.BlockSpec((None, L, W), lambda b, kj, qi: (b, 0, 0))],
        out_shape=[jax.ShapeDtypeStruct((B, L, AD), F32), jax.ShapeDtypeStruct((B, L, AD), BF16),
                   jax.ShapeDtypeStruct((B, L, AD), BF16), jax.ShapeDtypeStruct((B, L, W), F32),
                   jax.ShapeDtypeStruct((B, L, W), F32)],
        scratch_shapes=[pltpu.VMEM((L, n_heads * HW), F32), pltpu.VMEM((tq, n_heads * HW), F32),
                        pltpu.VMEM((tq, AD), F32)])


def _mix_gather(refs, first):
    b_ref, c_ref, hc_ref, cp_ref, hcp_ref, o_ref, cw_ref, p_ref = refs
    bg = b_ref[...].astype(F32)
    u = c_ref[...].astype(F32) * hc_ref[...].astype(F32)
    prev = cp_ref[...].astype(F32) * hcp_ref[...].astype(F32)
    prev = jnp.where(first, 0.0, prev)
    cv, u1, u2 = _causal_conv(u, prev, cw_ref[...])
    yc = bg * cv
    p = p_ref[...]
    rc = lax.rsqrt(_group_mean(yc * yc, p) + EPS)
    ya = o_ref[...].astype(F32)
    ra = lax.rsqrt(_group_mean(ya * ya, p) + EPS)
    return bg, (u, u1, u2), cv, yc * rc, rc, ya * ra, ra


def _mix_specs(tm, CD):
    per = tm // HALO
    cur = lambda cb: pl.BlockSpec((None, tm, CD), lambda b, i: (b, i, cb))
    prev = lambda cb: pl.BlockSpec((None, HALO, CD), lambda b, i: (b, jnp.maximum(i * per - 1, 0), cb))
    return [cur(0), cur(1), cur(2), prev(1), prev(2), cur(0)]


def _mix_out(proj, o, cw, gc, ga, wout, h, pmat, next_gain, *, tm, name, comm=None):
    B, L, D = h.shape
    CD = o.shape[-1]
    const = lambda b, i: (0, 0)

    def body(b_ref, c_ref, hc_ref, cp_ref, hcp_ref, o_ref, cw_ref, p_ref, gc_ref, ga_ref, w_ref, h_ref, ng_ref,
             out_ref, y_ref, n_ref):
        first = pl.program_id(1) == 0
        _, _, _, zc, _, za, _ = _mix_gather((b_ref, c_ref, hc_ref, cp_ref, hcp_ref, o_ref, cw_ref, p_ref), first)
        yc = (zc * gc_ref[...]).astype(BF16)
        ya = (za * ga_ref[...]).astype(BF16)
        y_ref[:, :CD] = yc
        y_ref[:, CD:] = ya
        out = h_ref[...] + _dot(yc, w_ref[:CD, :]) + _dot(ya, w_ref[CD:, :])
        out_ref[...] = out
        n_ref[...] = _rms(out, ng_ref[...])

    tile = pl.BlockSpec((None, tm, D), lambda b, i: (b, i, 0))
    return _launch(
        body, name=name, grid=(B, L // tm),
        in_specs=_mix_specs(tm, CD)
                 + [pl.BlockSpec(cw.shape, const), pl.BlockSpec(pmat.shape, const),
                    pl.BlockSpec((1, CD), const), pl.BlockSpec((1, CD), const), pl.BlockSpec((D, D), const),
                    tile, pl.BlockSpec((1, D), const)],
        out_specs=[tile, tile, tile],
        out_shape=[jax.ShapeDtypeStruct((B, L, D), F32), jax.ShapeDtypeStruct((B, L, D), BF16),
                   jax.ShapeDtypeStruct((B, L, D), BF16)],
        args=(proj, proj, proj, proj, proj, o, cw, pmat, gc, ga, wout, h, next_gain), comm=comm)


def _mix_out_bwd(dhb, proj, o, cw, gc, ga, wout, pmat, *, tm, name, comm=None):
    B, L, D = dhb.shape
    CD = o.shape[-1]
    const = lambda b, i: (0, 0)

    def body(dh_ref, b_ref, c_ref, hc_ref, cp_ref, hcp_ref, o_ref, cw_ref, p_ref, gc_ref, ga_ref, w_ref,
             db_ref, dcv_ref, do_ref, dgc_ref, dga_ref, dcw_ref):
        first = pl.program_id(1) == 0

        @pl.when((pl.program_id(0) == 0) & first)
        def _():
            dgc_ref[...] = jnp.zeros_like(dgc_ref)
            dga_ref[...] = jnp.zeros_like(dga_ref)
            dcw_ref[...] = jnp.zeros_like(dcw_ref)

        bg, us, cv, zc, rc, za, ra = _mix_gather(
            (b_ref, c_ref, hc_ref, cp_ref, hcp_ref, o_ref, cw_ref, p_ref), first)
        p = p_ref[...]
        dh = dh_ref[...]
        dyc = _dot_nt(dh, w_ref[:CD, :])
        dya = _dot_nt(dh, w_ref[CD:, :])

        dgc_ref[...] += jnp.sum(dyc * zc, axis=0, keepdims=True)
        dz = dyc * gc_ref[...]
        dx = rc * (dz - zc * _group_mean(dz * zc, p))
        db_ref[...] = (dx * cv).astype(BF16)
        dcv = dx * bg
        dcv_ref[...] = dcv.astype(BF16)
        for k in range(3):
            dcw_ref[k:k + 1, :] += jnp.sum(dcv * us[2 - k], axis=0, keepdims=True)

        dga_ref[...] += jnp.sum(dya * za, axis=0, keepdims=True)
        dz = dya * ga_ref[...]
        do_ref[...] = (ra * (dz - za * _group_mean(dz * za, p))).astype(BF16)

    tile = lambda w: pl.BlockSpec((None, tm, w), lambda b, i: (b, i, 0))
    return _launch(
        body, name=name, grid=(B, L // tm), comm=comm,
        args=(dhb, proj, proj, proj, proj, proj, o, cw, pmat, gc, ga, wout),
        in_specs=[tile(D)] + _mix_specs(tm, CD)
                 + [pl.BlockSpec(cw.shape, const), pl.BlockSpec(pmat.shape, const),
                    pl.BlockSpec((1, CD), const), pl.BlockSpec((1, CD), const), pl.BlockSpec((D, D), const)],
        out_specs=[tile(CD), tile(CD), tile(CD),
                   pl.BlockSpec((1, CD), const), pl.BlockSpec((1, CD), const), pl.BlockSpec((8, CD), const)],
        out_shape=[jax.ShapeDtypeStruct((B, L, CD), BF16)] * 3
                  + [jax.ShapeDtypeStruct((1, CD), F32)] * 2 + [jax.ShapeDtypeStruct((8, CD), F32)])


def _conv_bwd(dcv, proj, cw, *, tm, name):
    B, L, CD = dcv.shape
    per = tm // HALO
    nhalo = L // HALO
    nt = L // tm

    def body(d_ref, dn_ref, c_ref, hc_ref, cw_ref, out_ref):
        last = pl.program_id(1) == nt - 1
        d = d_ref[...].astype(F32)
        nxt = jnp.where(last, 0.0, dn_ref[...].astype(F32))
        n0, n1 = _row_of(nxt, 0), _row_of(nxt, 1)
        rows = lax.broadcasted_iota(jnp.int32, d.shape, 0)
        d1 = jnp.where(rows == tm - 1, n0, pltpu.roll(d, tm - 1, 0))
        d2 = jnp.where(rows == tm - 2, n0, jnp.where(rows == tm - 1, n1, pltpu.roll(d, tm - 2, 0)))
        w = cw_ref[...]
        du = w[2:3, :] * d + w[1:2, :] * d1 + w[0:1, :] * d2
        out_ref[:, :CD] = (du * hc_ref[...].astype(F32)).astype(BF16)
        out_ref[:, CD:] = (du * c_ref[...].astype(F32)).astype(BF16)

    return pl.pallas_call(
        body, name=name, grid=(B, nt),
        in_specs=[pl.BlockSpec((None, tm, CD), lambda b, i: (b, i, 0)),
                  pl.BlockSpec((None, HALO, CD), lambda b, i: (b, jnp.minimum((i + 1) * per, nhalo - 1), 0)),
                  pl.BlockSpec((None, tm, CD), lambda b, i: (b, i, 1)),
                  pl.BlockSpec((None, tm, CD), lambda b, i: (b, i, 2)),
                  pl.BlockSpec(cw.shape, lambda b, i: (0, 0))],
        out_specs=pl.BlockSpec((None, tm, 2 * CD), lambda b, i: (b, i, 0)),
        out_shape=jax.ShapeDtypeStruct((B, L, 2 * CD), BF16),
        compiler_params=_params("arbitrary", "arbitrary"),
    )(dcv, dcv, proj, proj, cw)


def _place():
    x, y, c = lax.axis_index("x"), lax.axis_index("y"), lax.axis_index("c")
    others = [(1 - x, y), (x, 1 - y), (1 - x, 1 - y)]
    return x, y, c, others


def _all_gather_shards(shards, *, name):
    n = len(shards)

    def body(*refs):
        ins, outs = refs[:n], refs[n:2 * n]
        send, recv, fsend, frecv, lsem = refs[2 * n:]
        x, y, c, others = _place()
        me = 2 * x + y
        local = [pltpu.make_async_copy(ins[t], outs[t].at[me], lsem.at[t]) for t in range(n)]
        for cp in local:
            cp.start()

        def half(t, k):
            hr = shards[t].shape[0] // 2
            return pl.ds(pl.multiple_of(k * hr, HALO), hr)

        def ici(t, j, src_chip, to):
            src = ins[t].at[half(t, c)] if to is not None else outs[t].at[src_chip, half(t, c)]
            return pltpu.make_async_remote_copy(
                src_ref=src, dst_ref=outs[t].at[src_chip, half(t, c)],
                send_sem=send.at[3 * t + j], recv_sem=recv.at[3 * t + j],
                device_id=(x, y, c) if to is None else to, device_id_type=MESH)

        def d2d(t, j, src_chip, k):
            return pltpu.make_async_remote_copy(
                src_ref=outs[t].at[src_chip, half(t, k)], dst_ref=outs[t].at[src_chip, half(t, k)],
                send_sem=fsend.at[3 * t + j], recv_sem=frecv.at[3 * t + j],
                device_id=(x, y, 1 - c), device_id_type=MESH)

        firsts = [ici(t, j, me, (ox, oy, c)) for t in range(n) for j, (ox, oy) in enumerate(others)]
        for cp in firsts:
            cp.start()
        passed = []
        for t in range(n):
            for j, (ox, oy) in enumerate(others):
                ici(t, j, 2 * ox + oy, None).wait_recv()
                cp = d2d(t, j, 2 * ox + oy, c)
                cp.start()
                passed.append(cp)
        for t in range(n):
            for j, (ox, oy) in enumerate(others):
                d2d(t, j, 2 * ox + oy, 1 - c).wait_recv()
        for cp in firsts + passed:
            cp.wait_send()
        for cp in local:
            cp.wait()

    return pl.pallas_call(
        body, name=name,
        in_specs=[ANY] * n, out_specs=[ANY] * n,
        out_shape=[jax.ShapeDtypeStruct((N_SHARD,) + s.shape, s.dtype) for s in shards],
        scratch_shapes=[pltpu.SemaphoreType.DMA((3 * n,))] * 4 + [pltpu.SemaphoreType.DMA((n,))],
    )(*shards)


def _all_reduce_small(slab, *, name):
    def body(in_ref, out_ref, gath, send, recv):
        x, y, c, _ = _place()
        me = 4 * x + 2 * y + c
        gath[me] = in_ref[...]
        copies, peers = [], []
        for m in range(1, N_DEV):
            px = jnp.where((m >> 2) & 1, 1 - x, x)
            py = jnp.where((m >> 1) & 1, 1 - y, y)
            pc = jnp.where(m & 1, 1 - c, c)
            cp = pltpu.make_async_remote_copy(
                src_ref=in_ref, dst_ref=gath.at[me], send_sem=send.at[m - 1], recv_sem=recv.at[m - 1],
                device_id=(px, py, pc), device_id_type=MESH)
            cp.start()
            copies.append(cp)
            peers.append(4 * px + 2 * py + pc)
        for m in range(1, N_DEV):
            pltpu.make_async_remote_copy(
                src_ref=in_ref, dst_ref=gath.at[peers[m - 1]], send_sem=send.at[m - 1], recv_sem=recv.at[m - 1],
                device_id=(x, y, c), device_id_type=MESH).wait_recv()
        for cp in copies:
            cp.wait_send()
        acc = gath[0]
        for k in range(1, N_DEV):
            acc = acc + gath[k]
        out_ref[...] = acc

    vm = pl.BlockSpec(memory_space=pltpu.VMEM)
    return pl.pallas_call(
        body, name=name, in_specs=[vm], out_specs=vm,
        out_shape=jax.ShapeDtypeStruct(slab.shape, slab.dtype),
        scratch_shapes=[pltpu.VMEM((N_DEV,) + slab.shape, slab.dtype),
                        pltpu.SemaphoreType.DMA((N_DEV - 1,)), pltpu.SemaphoreType.DMA((N_DEV - 1,))],
    )(slab)


def _gather_stage(shards, into, *, ici=(), d2d=()):
    n = len(shards) if into is None else len(into)
    ns = len(shards) if ici else 0
    ni, nd = max(len(ici), 1), max(len(d2d), 1)
    shapes = [s.shape for s in shards] if into is None else [p.shape[1:] for p in into]
    dtypes = [s.dtype for s in shards] if into is None else [p.dtype for p in into]

    def copies(ins, outs, sems, sending):
        x, y, c, others = _place()
        me = 2 * x + y
        out = []
        for t in range(n):
            hr = shapes[t][0] // 2
            mine = pl.ds(pl.multiple_of(c * hr, HALO), hr)
            theirs = pl.ds(pl.multiple_of((1 - c) * hr, HALO), hr)
            for a, j in enumerate(ici):
                ox, oy = others[j]
                src_chip = me if sending else 2 * ox + oy
                out.append(pltpu.make_async_remote_copy(
                    src_ref=ins[t].at[mine], dst_ref=outs[t].at[src_chip, mine],
                    send_sem=sems[0].at[ni * t + a], recv_sem=sems[1].at[ni * t + a],
                    device_id=(ox, oy, c) if sending else (x, y, c), device_id_type=MESH))
            for a, j in enumerate(d2d):
                ox, oy = others[j]
                blk = outs[t].at[2 * ox + oy, mine if sending else theirs]
                out.append(pltpu.make_async_remote_copy(
                    src_ref=blk, dst_ref=blk, send_sem=sems[2].at[nd * t + a], recv_sem=sems[3].at[nd * t + a],
                    device_id=(x, y, 1 - c) if sending else (x, y, c), device_id_type=MESH))
        return out

    def local(ins, outs, sems):
        if into is not None:
            return []
        x, y, _, _ = _place()
        return [pltpu.make_async_copy(ins[t], outs[t].at[2 * x + y], sems[4].at[t]) for t in range(n)]

    def start(ins, outs, sems):
        for cp in local(ins, outs, sems) + copies(ins, outs, sems, True):
            cp.start()

    def finish(ins, outs, sems):
        for cp in copies(ins, outs, sems, False):
            cp.wait_recv()
        for cp in copies(ins, outs, sems, True):
            cp.wait_send()
        for cp in local(ins, outs, sems):
            cp.wait()

    return _Comm((list(shards) if ici or into is None else []) + (list(into) if into is not None else []),
                 [jax.ShapeDtypeStruct((N_SHARD,) + tuple(sh), dt) for sh, dt in zip(shapes, dtypes)],
                 [ni * n, ni * n, nd * n, nd * n, n], start, finish,
                 aliases=None if into is None else {ns + t: t for t in range(n)})


def _gather_ici(shards):
    return _gather_stage(shards, None, ici=(0, 1, 2))


def _gather_d2d(parts):
    return _gather_stage((), parts, d2d=(0, 1, 2))


def _swap_halves(grads):
    n = len(grads)

    def copies(ins, outs, sems):
        x, y, c, _ = _place()
        out = []
        for t in range(n):
            hr = grads[t].shape[1] // 2
            rows = pl.ds(pl.multiple_of((1 - c) * hr, 8), hr)
            out.append(pltpu.make_async_remote_copy(
                src_ref=ins[t].at[:, rows, :], dst_ref=outs[t], send_sem=sems[0].at[t], recv_sem=sems[1].at[t],
                device_id=(x, y, 1 - c), device_id_type=MESH))
        return out

    def start(ins, outs, sems):
        for cp in copies(ins, outs, sems):
            cp.start()

    def finish(ins, outs, sems):
        for cp in copies(ins, outs, sems):
            cp.wait()

    return _Comm(grads, [jax.ShapeDtypeStruct((N_SHARD, g.shape[1] // 2, g.shape[2]), g.dtype) for g in grads],
                 [n, n], start, finish)


def _pair_sum(g, got, c, *, name):
    ns, R, C = g.shape
    hr = R // 2

    def body(c_ref, g_ref, r_ref, o_ref):
        o_ref[...] = (g_ref[...] + r_ref[...]).astype(BF16)

    return pl.pallas_call(
        body, name=name,
        grid_spec=pltpu.PrefetchScalarGridSpec(
            num_scalar_prefetch=1, grid=(ns,),
            in_specs=[pl.BlockSpec((None, hr, C), lambda s, cr: (s, cr[0], 0)),
                      pl.BlockSpec((None, hr, C), lambda s, cr: (s, 0, 0))],
            out_specs=pl.BlockSpec((None, hr, C), lambda s, cr: (s, 0, 0))),
        out_shape=jax.ShapeDtypeStruct((ns, hr, C), BF16),
        compiler_params=_params("arbitrary"),
    )(c, g, got)


def _scatter_chips(sums):
    n = len(sums)

    def copies(ins, outs, sems, sending):
        x, y, c, others = _place()
        me = 2 * x + y
        out = []
        for t in range(n):
            for j, (ox, oy) in enumerate(others):
                there = 2 * ox + oy
                out.append(pltpu.make_async_remote_copy(
                    src_ref=ins[t].at[there if sending else me], dst_ref=outs[t].at[me if sending else there],
                    send_sem=sems[0].at[3 * t + j], recv_sem=sems[1].at[3 * t + j],
                    device_id=(ox, oy, c) if sending else (x, y, c), device_id_type=MESH))
        return out

    def start(ins, outs, sems):
        for cp in copies(ins, outs, sems, True):
            cp.start()

    def finish(ins, outs, sems):
        for cp in copies(ins, outs, sems, False):
            cp.wait_recv()
        for cp in copies(ins, outs, sems, True):
            cp.wait_send()

    return _Comm(sums, [jax.ShapeDtypeStruct(s.shape, s.dtype) for s in sums], [3 * n, 3 * n], start, finish)


def _chip_sum(g, got, landed, idx, *, name):
    ns, R, C = g.shape
    hr = R // 2
    steps = next(k for k in (4, 2, 1) if hr % (k * HALO) == 0)
    tr = hr // steps

    def body(i_ref, g_ref, r_ref, a_ref, b_ref, c_ref, o_ref):
        acc = g_ref[...] + r_ref[...]
        for ref in (a_ref, b_ref, c_ref):
            acc = acc + ref[...].astype(F32)
        o_ref[...] = acc

    other = lambda k: pl.BlockSpec((None, tr, C), lambda s, ir: (ir[2 + k], s, 0))
    return pl.pallas_call(
        body, name=name,
        grid_spec=pltpu.PrefetchScalarGridSpec(
            num_scalar_prefetch=1, grid=(steps,),
            in_specs=[pl.BlockSpec((None, tr, C), lambda s, ir: (ir[0], ir[1] * steps + s, 0)),
                      pl.BlockSpec((None, tr, C), lambda s, ir: (ir[0], s, 0)),
                      other(0), other(1), other(2)],
            out_specs=pl.BlockSpec((tr, C), lambda s, ir: (ir[1] * steps + s, 0))),
        out_shape=jax.ShapeDtypeStruct((R, C), F32),
        compiler_params=_params("arbitrary"),
    )(idx, g, got, landed, landed, landed)


def _share_halves(halves):
    n = len(halves)

    def copies(outs, sems, sending):
        x, y, c, _ = _place()
        out = []
        for t in range(n):
            hr = halves[t].shape[0] // 2
            rows = pl.ds(pl.multiple_of((c if sending else 1 - c) * hr, 8), hr)
            out.append(pltpu.make_async_remote_copy(
                src_ref=outs[t].at[rows, :], dst_ref=outs[t].at[rows, :], send_sem=sems[0].at[t],
                recv_sem=sems[1].at[t], device_id=(x, y, 1 - c) if sending else (x, y, c), device_id_type=MESH))
        return out

    def start(ins, outs, sems):
        for cp in copies(outs, sems, True):
            cp.start()

    def finish(ins, outs, sems):
        for cp in copies(outs, sems, False):
            cp.wait_recv()
        for cp in copies(outs, sems, True):
            cp.wait_send()

    return _Comm(halves, [jax.ShapeDtypeStruct(h.shape, h.dtype) for h in halves], [n, n], start, finish,
                 aliases={t: t for t in range(n)})


def _adamw(w, g, m, v, *, name):
    R, C = w.shape
    tr = next((k for k in (128, 64, 32, 16, 8) if R % k == 0), R)

    def body(w_ref, g_ref, m_ref, v_ref, go_ref, d_ref, mo_ref, vo_ref):
        gv = g_ref[...]
        go_ref[...] = gv
        mn = ADAM_B1 * m_ref[...] + (1.0 - ADAM_B1) * gv
        vn = ADAM_B2 * v_ref[...] + (1.0 - ADAM_B2) * (gv * gv)
        m_hat = mn / (1.0 - ADAM_B1 ** ADAM_STEP)
        v_hat = vn / (1.0 - ADAM_B2 ** ADAM_STEP)
        d_ref[...] = -ADAM_LR * (m_hat / (jnp.sqrt(v_hat) + ADAM_EPS) + ADAM_WD * w_ref[...])
        mo_ref[...] = mn
        vo_ref[...] = vn

    blk = pl.BlockSpec((tr, C), lambda i: (i, 0))
    return pl.pallas_call(
        body, name=name, grid=(R // tr,), in_specs=[blk] * 4, out_specs=[blk] * 4,
        out_shape=[jax.ShapeDtypeStruct((R, C), F32)] * 4,
        compiler_params=_params("arbitrary"),
    )(w, g, m, v)


def _pack_small(D, meta, n1, nm, n3, nf, gc, ga, bf, cw):
    def row(a):
        a = a.reshape(-1, a.shape[-1])
        return jnp.pad(a, ((0, 0), (0, D - a.shape[-1])))
    rows = [row(meta), row(n1), row(nm), row(n3), row(nf), row(jnp.concatenate([gc, ga], axis=-1)), row(bf), row(cw)]
    slab = jnp.concatenate(rows, axis=0)
    return jnp.pad(slab, ((0, SMALL_ROWS - slab.shape[0]), (0, 0)))


def _unpack_small(slab, like):
    meta, n1, nm, n3, nf, gc, ga, bf, cw = like
    nmeta, mc = meta.shape
    out = [slab[:nmeta, :mc].reshape(meta.shape)]
    r = nmeta
    for a in (n1, nm, n3, nf):
        out.append(slab[r, :a.shape[-1]].reshape(a.shape))
        r += 1
    cd = gc.shape[-1]
    out.append(slab[r, :cd].reshape(gc.shape))
    out.append(slab[r, cd:cd + ga.shape[-1]].reshape(ga.shape))
    r += 1
    out.append(slab[r, :bf.shape[-1]].reshape(bf.shape))
    r += 1
    out.append(slab[r:r + 3, :cw.shape[-1]].reshape(cw.shape))
    return out


def kernel(x, meta_tokens, ffn1_norm, ffn1_w_gu, ffn1_w_down, mix_norm, w_in, conv_w, b_f, out_norm_conv, out_norm_attn, w_out, ffn2_norm, ffn2_w_gu, ffn2_w_down, final_norm, loss_target, m_meta_tokens, m_ffn1_norm, m_ffn1_w_gu, m_ffn1_w_down, m_mix_norm, m_w_in, m_conv_w, m_b_f, m_out_norm_conv, m_out_norm_attn, m_w_out, m_ffn2_norm, m_ffn2_w_gu, m_ffn2_w_down, m_final_norm, v_meta_tokens, v_ffn1_norm, v_ffn1_w_gu, v_ffn1_w_down, v_mix_norm, v_w_in, v_conv_w, v_b_f, v_out_norm_conv, v_out_norm_attn, v_w_out, v_ffn2_norm, v_ffn2_w_gu, v_ffn2_w_down, v_final_norm):
    B, S, D = x.shape
    L = S + N_META
    T = B * L
    tm = L // 3
    assert tm * 3 == L and tm % HALO == 0
    tm2 = 2 * tm
    assert T % tm2 == 0
    guc = ffn1_w_gu.shape[-1]
    ff = N_SHARD * guc // 2
    H = b_f.shape[-1]
    AD = H * HEAD_DIM
    CD = conv_w.shape[-1] * N_SHARD
    assert CD == AD and CD + AD == D and CD % LANES == 0
    n_main = 3 * CD + 3 * AD
    ins = w_in.shape[-1]

    xi, yi, ci = lax.axis_index("x"), lax.axis_index("y"), lax.axis_index("c")
    chip = 2 * xi + yi

    small_shard = jnp.zeros((2 * HALO, meta_tokens.shape[-1]), F32)
    small_shard = small_shard.at[:N_META].set(meta_tokens)
    small_shard = small_shard.at[N_META:N_META + 3, :conv_w.shape[-1]].set(conv_w[0])
    big = [ffn1_w_gu[0], ffn1_w_down[0], w_in[0], w_out[0], ffn2_w_gu[0], ffn2_w_down[0]]
    wgu1_s, wd1_s, win_s, wout_s, wgu2_s, wd2_s = [w.astype(BF16) for w in big]
    small_g, = _all_gather_shards([small_shard], name="gather_small")
    meta_f = jnp.moveaxis(small_g[:, :N_META], 0, 1).reshape(N_META, D)
    cw_f = jnp.moveaxis(small_g[:, N_META:N_META + 3, :conv_w.shape[-1]], 0, 1).reshape(3, CD)
    cw8 = jnp.pad(cw_f, ((0, 5), (0, 0)))
    bf_p = jnp.pad(b_f, ((0, 0), (0, LANES - H)))
    gid = jnp.arange(CD) // HEAD_DIM
    pmat = jnp.where(gid[:, None] == gid[None, :], 1.0 / HEAD_DIM, 0.0).astype(BF16)

    gu_shape = jax.ShapeDtypeStruct((2, T, ff), BF16)
    gu_w_spec = pl.BlockSpec((None, D, guc), lambda s, i: (s, 0, 0))
    gu_o_spec = pl.BlockSpec((None, tm2, guc), lambda s, i: (s // 2, i, s % 2))

    sid = jnp.bitwise_xor(chip, jnp.array([0, 2, 1, 3], jnp.int32)).astype(jnp.int32)
    (h0, n1), wgu1_h = _embed_norm(x, meta_f, ffn1_norm, tm=tm, name="embed_norm",
                                   comm=_gather_stage([wgu1_s], None, ici=(0, 1)))
    gu1, wgu1_h = _ffn_up(n1, wgu1_s[None], sid, None, tm=tm2, first=0, count=1, name="ffn1_up_own",
                          comm=_gather_stage([wgu1_s], wgu1_h, ici=(2,), d2d=(0, 1)))
    gu1, out = _ffn_up(n1, wgu1_h[0], sid, gu1, tm=tm2, first=1, count=2, name="ffn1_up_near",
                       comm=_join(_gather_stage((), wgu1_h, d2d=(2,)), _gather_ici([wd1_s, wout_s])))
    wgu1, down_w = out[0], out[1:]
    gu1, (wd1, wout_g) = _ffn_up(n1, wgu1, sid, gu1, tm=tm2, first=3, count=1, name="ffn1_up_far",
                                 comm=_gather_d2d(down_w))
    wd1 = wd1.reshape(ff, D)
    (h1, n2), win_h = _ffn_down(gu1, wd1, h0, mix_norm, tm=tm, name="ffn1_down", comm=_gather_ici([win_s]))
    win_g, = _run_comm(_gather_d2d(win_h), name="gather_w_in")
    wout_f = wout_g.reshape(D, D)
    win_f = jnp.moveaxis(win_g, 0, 1).reshape(D, N_SHARD * ins)
    win_main = win_f[:, :n_main]
    win_fg = jnp.pad(win_f[:, n_main:], ((0, 0), (0, LANES - H)))

    proj, fg = _mix_in(n2, win_main, win_fg, tm=tm2, nb=n_main // (3 * CD), name="mix_in")
    proj3 = proj.reshape(B, L, n_main)
    fg3 = fg.reshape(B, L, LANES)
    fc = _fcum(fg3, bf_p, ch=tm, name="forget_cumsum")
    fr = fc[:, :, :H].reshape(B, L // tm, tm, H).transpose(0, 1, 3, 2)
    (o, lse), ffn2_w = _attn_fwd(proj3, fr, tq=tm, n_heads=H, name="attn_fwd",
                                 comm=_gather_ici([wgu2_s, wd2_s]))
    (h2, ymix, n3), (wgu2, wd2) = _mix_out(
        proj3, o, cw8, out_norm_conv, out_norm_attn, wout_f, h1.reshape(B, L, D), pmat, ffn2_norm,
        tm=tm, name="mix_out", comm=_gather_d2d(ffn2_w))
    wd2 = wd2.reshape(ff, D)
    h2 = h2.reshape(T, D)
    n3 = n3.reshape(T, D)

    gu2, _ = _matmul_nn(n3, wgu2, tm=tm2, nb=N_SHARD, w_spec=gu_w_spec, out_shape=gu_shape, out_spec=gu_o_spec,
                        name="ffn2_up")
    (dh3f, dh3b, d_gf, loss_part), _ = _ffn_down_loss(gu2, wd2, h2, final_norm.reshape(1, D), loss_target,
                                                      tm=tm, name="ffn2_down_loss")

    c_arr = jnp.reshape(ci, (1,)).astype(jnp.int32)
    ks = jnp.arange(N_SHARD - 1, dtype=jnp.int32)
    idx = jnp.concatenate([jnp.stack([chip, ci]).astype(jnp.int32), ks + (ks >= chip).astype(jnp.int32)])

    def pair_sums(grads, got, names):
        return [_pair_sum(g, r, c_arr, name="pair_sum_" + nm) for g, r, nm in zip(grads, got, names)]

    def chip_sums(grads, got, landed, names):
        return [_chip_sum(g, r, l, idx, name="chip_sum_" + nm) for g, r, l, nm in zip(grads, got, landed, names)]

    def dw_up(n, dgu, name, comm=None):
        return _matmul_tn(
            n, dgu, tm=L, nb=N_SHARD, kb=D, x_spec=pl.BlockSpec((L, D), lambda s, i: (i, 0)),
            y_spec=pl.BlockSpec((None, L, guc), lambda s, i: (s // 2, i, s % 2)),
            out_shape=jax.ShapeDtypeStruct((N_SHARD, D, guc), F32),
            out_spec=pl.BlockSpec((None, D, guc), lambda s, i: (s, 0, 0)), name=name, comm=comm)

    (dgu2, d_wd2), _ = _ffn_bwd_act(dh3b, gu2, wd2, tm=tm, guc=guc, name="ffn2_bwd_act")
    (dh2, dh2b, d_g3), _ = _ffn_bwd_in(dgu2, wgu2, h2, ffn2_norm, dh3f, tm=tm, scale=1.0, name="ffn2_bwd_in")
    d_wgu2, _ = dw_up(n3, dgu2, "ffn2_dw_up")
    grads_f2 = [d_wgu2, d_wd2.reshape(N_SHARD, ff // N_SHARD, D)]
    names_f2 = ["wgu2", "wd2"]

    dh2b3 = dh2b.reshape(B, L, D)
    (d_bg, d_cv, d_o, d_gc, d_ga, d_cw), got_f2 = _mix_out_bwd(
        dh2b3, proj3, o, cw8, out_norm_conv, out_norm_attn, wout_f, pmat, tm=tm, name="mix_out_bwd",
        comm=_swap_halves(grads_f2))
    sums_f2 = pair_sums(grads_f2, got_f2, names_f2)
    d_wout, _ = _matmul_tn(
        ymix.reshape(T, D), dh2b, tm=tm2, nb=1, kb=D,
        x_spec=pl.BlockSpec((tm2, D), lambda s, i: (i, 0)), y_spec=pl.BlockSpec((tm2, D), lambda s, i: (i, 0)),
        out_shape=jax.ShapeDtypeStruct((D, D), F32), out_spec=pl.BlockSpec((D, D), lambda s, i: (0, 0)),
        name="dw_out")
    d_cc = _conv_bwd(d_cv, proj3, cw8, tm=tm, name="conv_bwd")
    (d_q, d_k, d_v, d_fk, d_fq), landed_f2 = _attn_bwd(proj3, o, d_o, lse, fr, tq=tm, n_heads=H, name="attn_bwd",
                                                       comm=_scatter_chips(sums_f2))
    halves_f2 = chip_sums(grads_f2, got_f2, landed_f2, names_f2)
    d_fg, d_bf = _fcum_bwd(d_fq, d_fk, fg3, bf_p, ch=tm, name="forget_cumsum_bwd")

    parts = [d_bg.reshape(T, CD), d_cc.reshape(T, 2 * CD), d_q.reshape(T, AD), d_k.reshape(T, AD),
             d_v.reshape(T, AD), d_fg.reshape(T, LANES)]
    (dh1, dh1b, d_gm, d_proj), g_f2 = _mix_bwd_in(parts, win_main, win_fg, h1, mix_norm, dh2, tm=tm, scale=0.5,
                                                  name="mix_bwd_in", comm=_share_halves(halves_f2))
    wide = d_proj.shape[1]
    d_win_nat, _ = _matmul_tn(
        n2, d_proj, tm=tm, nb=1, kb=D,
        x_spec=pl.BlockSpec((tm, D), lambda s, i: (i, 0)), y_spec=pl.BlockSpec((tm, wide), lambda s, i: (i, 0)),
        out_shape=jax.ShapeDtypeStruct((D, wide), F32), out_spec=pl.BlockSpec((D, wide), lambda s, i: (0, 0)),
        name="dw_in")
    d_win = jnp.moveaxis(d_win_nat[:, :N_SHARD * ins].reshape(D, N_SHARD, ins), 1, 0)
    grads_mx = [d_win, d_wout.reshape(N_SHARD, D // N_SHARD, D)]
    names_mx = ["win", "wout"]

    (dgu1, d_wd1), got_mx = _ffn_bwd_act(dh1b, gu1, wd1, tm=tm, guc=guc, name="ffn1_bwd_act",
                                         comm=_swap_halves(grads_mx))
    sums_mx = pair_sums(grads_mx, got_mx, names_mx)
    grads_d1 = [d_wd1.reshape(N_SHARD, ff // N_SHARD, D)]
    d_wgu1, out = dw_up(n1, dgu1, "ffn1_dw_up", comm=_join(_scatter_chips(sums_mx), _swap_halves(grads_d1)))
    landed_mx, got_d1 = out[:2], out[2:]
    halves_mx = chip_sums(grads_mx, got_mx, landed_mx, names_mx)
    sums_d1 = pair_sums(grads_d1, got_d1, ["wd1"])
    grads_u1 = [d_wgu1]
    (grad_x, d_meta, d_g1), out = _ffn_bwd_in_first(
        dgu1, wgu1, h0, ffn1_norm, dh1, tm=tm, batch=B, name="ffn1_bwd_in",
        comm=_join(_join(_share_halves(halves_mx), _scatter_chips(sums_d1)), _swap_halves(grads_u1)))
    g_mx, landed_d1, got_u1 = out[:2], out[2:3], out[3:]
    halves_d1 = chip_sums(grads_d1, got_d1, landed_d1, ["wd1"])
    sums_u1 = pair_sums(grads_u1, got_u1, ["wgu1"])
    out = _run_comm(_join(_share_halves(halves_d1), _scatter_chips(sums_u1)), name="scatter_ffn1")
    g_d1, landed_u1 = out[:1], out[1:]
    halves_u1 = chip_sums(grads_u1, got_u1, landed_u1, ["wgu1"])
    g_u1 = _run_comm(_share_halves(halves_u1), name="share_ffn1")
    g_big = [g_u1[0], g_d1[0], g_mx[0], g_mx[1], g_f2[0], g_f2[1]]

    loss_row = jnp.zeros((1, D), F32).at[0, 0].set(loss_part[0, 0])
    slab = _pack_small(D, d_meta, d_g1, d_gm, d_g3, d_gf, d_gc, d_ga, d_bf[:, :H], d_cw[:3])
    slab = slab.at[SMALL_ROWS - 1].set(loss_row[0])
    total = _all_reduce_small(slab, name="reduce_small")
    loss = total[SMALL_ROWS - 1, 0]
    mcols = meta_tokens.shape[-1]
    ccols = conv_w.shape[-1]
    full_like = (jnp.zeros((N_META, D)), ffn1_norm, mix_norm, ffn2_norm, final_norm.reshape(1, D), out_norm_conv,
                 out_norm_attn, b_f, jnp.zeros((1, 3, CD)))
    g_small = _unpack_small(total, full_like)
    g_small[0] = lax.dynamic_slice_in_dim(g_small[0], chip * mcols, mcols, axis=1)
    g_small[8] = lax.dynamic_slice_in_dim(g_small[8], chip * ccols, ccols, axis=2)

    def small_slab(meta, a1, am, a3, af, gc, ga, bf, cw):
        return _pack_small(D, meta, a1, am, a3, af.reshape(1, D), gc, ga, bf, cw[0])

    w_small = small_slab(meta_tokens, ffn1_norm, mix_norm, ffn2_norm, final_norm, out_norm_conv, out_norm_attn, b_f, conv_w)
    m_small = small_slab(m_meta_tokens, m_ffn1_norm, m_mix_norm, m_ffn2_norm, m_final_norm, m_out_norm_conv,
                         m_out_norm_attn, m_b_f, m_conv_w)
    v_small = small_slab(v_meta_tokens, v_ffn1_norm, v_mix_norm, v_ffn2_norm, v_final_norm, v_out_norm_conv,
                         v_out_norm_attn, v_b_f, v_conv_w)
    gs = list(g_small)
    gs[4] = gs[4].reshape(final_norm.shape)
    g_slab = small_slab(gs[0], gs[1], gs[2], gs[3], gs[4], gs[5], gs[6], gs[7], gs[8])
    local_like = (meta_tokens, ffn1_norm, mix_norm, ffn2_norm, final_norm.reshape(1, D), out_norm_conv, out_norm_attn,
                  b_f, conv_w)
    small_out = [_unpack_small(s, local_like)
                 for s in _adamw(w_small, g_slab, m_small, v_small, name="adamw_small")[1:]]
    for lst in small_out:
        lst[4] = lst[4].reshape(final_norm.shape)

    names = ["wgu1", "wd1", "win", "wout", "wgu2", "wd2"]
    w_big = big
    m_big = [m_ffn1_w_gu[0], m_ffn1_w_down[0], m_w_in[0], m_w_out[0], m_ffn2_w_gu[0], m_ffn2_w_down[0]]
    v_big = [v_ffn1_w_gu[0], v_ffn1_w_down[0], v_w_in[0], v_w_out[0], v_ffn2_w_gu[0], v_ffn2_w_down[0]]
    big_out = [_adamw(w, g, m, v, name="adamw_" + nm) for w, g, m, v, nm in zip(w_big, g_big, m_big, v_big, names)]

    def assemble(small, bigs):
        meta, a1, am, a3, af, gc, ga, bf, cw = small
        gu1_, d1_, win_, wout_, gu2_, d2_ = [b[None] for b in bigs]
        return [meta, a1, gu1_, d1_, am, win_, cw, bf, gc, ga, wout_, a3, gu2_, d2_, af]

    gs_out = list(g_small)
    gs_out[4] = gs_out[4].reshape(final_norm.shape)
    grads_out = assemble(gs_out, [b[0] for b in big_out])
    delta_out = assemble(small_out[0], [b[1] for b in big_out])
    m_out = assemble(small_out[1], [b[2] for b in big_out])
    v_out = assemble(small_out[2], [b[3] for b in big_out])
    return (loss, grad_x, *grads_out, *delta_out, *m_out, *v_out)
```

```python
import functools

import jax
import jax.numpy as jnp
from jax import lax
from jax.experimental import pallas as pl
from jax.experimental.pallas import tpu as pltpu

F32 = jnp.float32
BF16 = jnp.bfloat16

EPS = 1e-6
N_META = 16
HEAD_DIM = 64
N_SHARD = 4
N_DEV = 8
HALO = 16
LANES = 128
SMALL_ROWS = 32
VMEM_LIMIT_V7X = 56 * 1024 * 1024
NEG = -1e30
ROW_BANDS = 2

ADAM_LR = 0.001
ADAM_B1 = 0.9
ADAM_B2 = 0.999
ADAM_EPS = 1e-08
ADAM_WD = 0.01
ADAM_STEP = 10

MESH = pl.DeviceIdType.MESH
ANY = pl.BlockSpec(memory_space=pl.ANY)
NT_DIMS = (((1,), (1,)), ((), ()))
TN_DIMS = (((0,), (0,)), ((), ()))


def _params(*sem):
    return pltpu.CompilerParams(dimension_semantics=sem, vmem_limit_bytes=VMEM_LIMIT_V7X)


class _Comm:
    def __init__(self, ins, out_shapes, sems, start, finish, aliases=None):
        self.ins, self.out_shapes, self.sems = list(ins), list(out_shapes), list(sems)
        self.start, self.finish, self.aliases = start, finish, dict(aliases or {})


def _join(a, b):
    ni, no, ns = len(a.ins), len(a.out_shapes), len(a.sems)

    def start(ins, outs, sems):
        a.start(ins[:ni], outs[:no], sems[:ns])
        b.start(ins[ni:], outs[no:], sems[ns:])

    def finish(ins, outs, sems):
        a.finish(ins[:ni], outs[:no], sems[:ns])
        b.finish(ins[ni:], outs[no:], sems[ns:])

    aliases = dict(a.aliases)
    aliases.update({ni + i: no + j for i, j in b.aliases.items()})
    return _Comm(a.ins + b.ins, a.out_shapes + b.out_shapes, a.sems + b.sems, start, finish, aliases)


def _launch(body, *, name, grid, in_specs, out_specs, out_shape, args, scratch_shapes=(), comm=None, prefetch=(),
            aliases=None):
    single = not isinstance(out_shape, (list, tuple))
    out_specs = [out_specs] if single else list(out_specs)
    out_shape = [out_shape] if single else list(out_shape)
    in_specs, scratch_shapes, prefetch = list(in_specs), list(scratch_shapes), list(prefetch)
    params = _params(*(("arbitrary",) * len(grid)))
    n_pf, n_in, n_out, n_scr = len(prefetch), len(in_specs), len(out_specs), len(scratch_shapes)
    c_ins = comm.ins if comm else []
    c_shapes = comm.out_shapes if comm else []
    c_sems = comm.sems if comm else []
    c_in, c_out = len(c_ins), len(c_shapes)

    def carrier(*refs):
        p = 0
        pf = refs[p:p + n_pf]; p += n_pf
        a = refs[p:p + n_in]; p += n_in
        ci = refs[p:p + c_in]; p += c_in
        o = refs[p:p + n_out]; p += n_out
        co = refs[p:p + c_out]; p += c_out
        s = refs[p:p + n_scr]; p += n_scr
        cs = refs[p:]
        if comm:
            first = functools.reduce(lambda u, v: u & v, [pl.program_id(k) == 0 for k in range(len(grid))])

            @pl.when(first)
            def _():
                comm.start(ci, co, cs)

        body(*pf, *a, *o, *s)

        if comm:
            last = functools.reduce(lambda u, v: u & v, [pl.program_id(k) == grid[k] - 1 for k in range(len(grid))])

            @pl.when(last)
            def _():
                comm.finish(ci, co, cs)

    io_aliases = {n_pf + i: j for i, j in (aliases or {}).items()}
    if comm:
        io_aliases.update({n_pf + n_in + i: n_out + j for i, j in comm.aliases.items()})
    all_in, all_out = in_specs + [ANY] * c_in, out_specs + [ANY] * c_out
    all_scratch = scratch_shapes + [pltpu.SemaphoreType.DMA((k,)) for k in c_sems]
    if n_pf:
        spec = dict(grid_spec=pltpu.PrefetchScalarGridSpec(
            num_scalar_prefetch=n_pf, grid=grid, in_specs=all_in, out_specs=all_out, scratch_shapes=all_scratch))
    else:
        spec = dict(grid=grid, in_specs=all_in, out_specs=all_out, scratch_shapes=all_scratch)
    res = pl.pallas_call(carrier, name=name, out_shape=out_shape + c_shapes, input_output_aliases=io_aliases,
                         compiler_params=params, **spec)(*prefetch, *args, *c_ins)
    main = list(res[:n_out])
    return (main[0] if single else main), (list(res[n_out:]) if comm else None)


def _run_comm(comm, *, name):
    c_in, c_out = len(comm.ins), len(comm.out_shapes)

    def body(*refs):
        ci, co, cs = refs[:c_in], refs[c_in:c_in + c_out], refs[c_in + c_out:]
        comm.start(ci, co, cs)
        comm.finish(ci, co, cs)

    return list(pl.pallas_call(
        body, name=name, in_specs=[ANY] * c_in, out_specs=[ANY] * c_out, out_shape=comm.out_shapes,
        scratch_shapes=[pltpu.SemaphoreType.DMA((k,)) for k in comm.sems],
        input_output_aliases=comm.aliases)(*comm.ins))


def _chunks(width, step=512):
    out, c0 = [], 0
    while c0 < width:
        cw = min(step, width - c0)
        out.append((c0, cw))
        c0 += cw
    return out


def _split3(v):
    hi = v.astype(BF16)
    r = v - hi.astype(F32)
    mid = r.astype(BF16)
    lo = (r - mid.astype(F32)).astype(BF16)
    return hi, mid, lo


def _dot(a, b):
    return jnp.dot(a, b, preferred_element_type=F32)


def _dot_nt(a, b):
    return lax.dot_general(a, b, NT_DIMS, preferred_element_type=F32)


def _dot_tn(a, b):
    return lax.dot_general(a, b, TN_DIMS, preferred_element_type=F32)


def _silu_mul(g, u):
    return g * jax.nn.sigmoid(g) * u


def _rms_bwd(dn, h, gain, dres):
    r = lax.rsqrt(jnp.mean(h * h, axis=-1, keepdims=True) + EPS)
    y = h * r
    dgain = jnp.sum(dn * y, axis=0, keepdims=True)
    dy = dn * gain
    dh = dres + r * (dy - y * jnp.mean(dy * y, axis=-1, keepdims=True))
    return dh, dgain


def _group_mean(v, p):
    return _dot(v.astype(BF16), p)


def _row_of(a, k):
    rows = lax.broadcasted_iota(jnp.int32, a.shape, 0)
    return jnp.sum(jnp.where(rows == k, a, 0.0), axis=0, keepdims=True)


def _causal_conv(u, prev, w):
    rows = lax.broadcasted_iota(jnp.int32, u.shape, 0)
    p1 = _row_of(prev, HALO - 1)
    p2 = _row_of(prev, HALO - 2)
    u1 = jnp.where(rows == 0, p1, pltpu.roll(u, 1, 0))
    u2 = jnp.where(rows == 0, p2, jnp.where(rows == 1, p1, pltpu.roll(u, 2, 0)))
    return w[2:3, :] * u + w[1:2, :] * u1 + w[0:1, :] * u2, u1, u2


def _rms(x, gain):
    return (x * lax.rsqrt(jnp.mean(x * x, axis=-1, keepdims=True) + EPS) * gain).astype(BF16)


def _embed_norm(x, meta, g, *, tm, name, comm=None):
    B, S, D = x.shape
    L = S + N_META
    per_seq = L // tm
    nt = B * per_seq
    body_rows = tm - N_META

    def body(meta_ref, g_ref, x_hbm, h_ref, n_ref, buf, sems):
        i = pl.program_id(0)

        def fetch(k, fn):
            slot, b, t = k % 2, k // per_seq, k % per_seq

            @pl.when(t == 0)
            def _():
                fn(pltpu.make_async_copy(x_hbm.at[b, pl.ds(0, body_rows)],
                                         buf.at[slot, pl.ds(N_META, body_rows)], sems.at[slot]))

            @pl.when(t != 0)
            def _():
                fn(pltpu.make_async_copy(x_hbm.at[b, pl.ds(pl.multiple_of(t * tm - N_META, 8), tm)],
                                         buf.at[slot], sems.at[slot]))

        @pl.when(i == 0)
        def _():
            fetch(i, lambda cp: cp.start())

        @pl.when(i + 1 < nt)
        def _():
            fetch(i + 1, lambda cp: cp.start())

        fetch(i, lambda cp: cp.wait())
        slot = i % 2

        @pl.when(i % per_seq == 0)
        def _():
            buf[slot, 0:N_META, :] = meta_ref[...]

        hv = buf[slot]
        h_ref[...] = hv
        n_ref[...] = _rms(hv, g_ref[...])

    row = pl.BlockSpec((tm, D), lambda i: (i, 0))
    return _launch(
        body, name=name, grid=(nt,),
        in_specs=[pl.BlockSpec((N_META, D), lambda i: (0, 0)), pl.BlockSpec((1, D), lambda i: (0, 0)), ANY],
        out_specs=[row, row],
        out_shape=[jax.ShapeDtypeStruct((B * L, D), F32), jax.ShapeDtypeStruct((B * L, D), BF16)],
        scratch_shapes=[pltpu.VMEM((2, tm, D), F32), pltpu.SemaphoreType.DMA((2,))],
        args=(meta, g, x), comm=comm)


def _ffn_up(n, wgu, sid, gu_prev, *, tm, first, count, name, comm=None):
    T, D = n.shape
    ns, _, guc = wgu.shape
    ff = N_SHARD * guc // 2

    def body(sid_ref, x_ref, w_ref, *rest):
        rest[-1][...] = _dot(x_ref[...], w_ref[...]).astype(BF16)

    where = lambda s, sid: sid[first + s]
    w_at = (lambda s, sid: 0) if ns == 1 else where
    return _launch(
        body, name=name, grid=(count, T // tm), prefetch=(sid,),
        in_specs=[pl.BlockSpec((tm, D), lambda s, i, sid: (i, 0)),
                  pl.BlockSpec((None, D, guc), lambda s, i, sid: (w_at(s, sid), 0, 0))]
                 + ([] if gu_prev is None else [ANY]),
        out_specs=pl.BlockSpec((None, tm, guc), lambda s, i, sid: (where(s, sid) // 2, i, where(s, sid) % 2)),
        out_shape=jax.ShapeDtypeStruct((2, T, ff), BF16),
        args=(n, wgu) + (() if gu_prev is None else (gu_prev,)),
        aliases=None if gu_prev is None else {2: 0}, comm=comm)


def _matmul_nn(x, w, *, tm, nb, w_spec, out_shape, out_spec, name, comm=None):
    T, K = x.shape

    def body(x_ref, w_ref, o_ref):
        o_ref[...] = _dot(x_ref[...], w_ref[...]).astype(o_ref.dtype)

    return _launch(
        body, name=name, grid=(nb, T // tm),
        in_specs=[pl.BlockSpec((tm, K), lambda s, i: (i, 0)), w_spec],
        out_specs=out_spec, out_shape=out_shape, args=(x, w), comm=comm)


def _mix_in(n, w_main, w_fg, *, tm, nb, name):
    T, D = n.shape
    n_main = w_main.shape[1]
    bw = n_main // nb
    W = w_fg.shape[1]

    def body(x_ref, w_ref, wf_ref, o_ref, fg_ref):
        x = x_ref[...]
        o_ref[...] = _dot(x, w_ref[...]).astype(BF16)

        @pl.when(pl.program_id(1) == 0)
        def _():
            fg_ref[...] = _dot(x, wf_ref[...])

    res, _ = _launch(
        body, name=name, grid=(T // tm, nb),
        in_specs=[pl.BlockSpec((tm, D), lambda i, s: (i, 0)), pl.BlockSpec((D, bw), lambda i, s: (0, s)),
                  pl.BlockSpec((D, W), lambda i, s: (0, 0))],
        out_specs=[pl.BlockSpec((tm, bw), lambda i, s: (i, s)), pl.BlockSpec((tm, W), lambda i, s: (i, 0))],
        out_shape=[jax.ShapeDtypeStruct((T, n_main), BF16), jax.ShapeDtypeStruct((T, W), F32)],
        args=(n, w_main, w_fg))
    return res


def _down_in_bands(g_ref, u_ref, wd_v, edges, chunks, finish):
    def down(rows):
        def act(c0, cw):
            return _silu_mul(g_ref[rows, c0:c0 + cw].astype(F32), u_ref[rows, c0:c0 + cw].astype(F32)).astype(BF16)

        acc = None
        nxt = act(*chunks[0])
        for k, (c0, cw) in enumerate(chunks):
            a = nxt
            if k + 1 < len(chunks):
                nxt = act(*chunks[k + 1])
            d = _dot(a, wd_v[c0:c0 + cw, :])
            acc = d if acc is None else acc + d
        return acc

    bands = [slice(r0, r1) for r0, r1 in zip(edges[:-1], edges[1:])]
    nxt = down(bands[0])
    for b, rows in enumerate(bands):
        acc = nxt
        if b + 1 < len(bands):
            nxt = down(bands[b + 1])
        finish(rows, acc)


def _ffn_down(gu, wd, h, next_gain, *, tm, name, comm=None):
    _, T, ff = gu.shape
    D = h.shape[1]
    chunks = _chunks(ff)

    def body(g_ref, u_ref, wd_hbm, h_ref, ng_ref, o_ref, n_ref, wd_v, sem):
        @pl.when(pl.program_id(0) == 0)
        def _():
            cp = pltpu.make_async_copy(wd_hbm, wd_v, sem)
            cp.start()
            cp.wait()

        def finish(rows, acc):
            out = h_ref[rows, :] + 0.5 * acc
            o_ref[rows, :] = out
            n_ref[rows, :] = _rms(out, ng_ref[...])

        _down_in_bands(g_ref, u_ref, wd_v, _band_edges(tm), chunks, finish)

    return _launch(
        body, name=name, grid=(T // tm,),
        in_specs=[pl.BlockSpec((None, tm, ff), lambda i: (0, i, 0)),
                  pl.BlockSpec((None, tm, ff), lambda i: (1, i, 0)),
                  ANY,
                  pl.BlockSpec((tm, D), lambda i: (i, 0)),
                  pl.BlockSpec((1, D), lambda i: (0, 0))],
        out_specs=[pl.BlockSpec((tm, D), lambda i: (i, 0)), pl.BlockSpec((tm, D), lambda i: (i, 0))],
        out_shape=[jax.ShapeDtypeStruct((T, D), F32), jax.ShapeDtypeStruct((T, D), BF16)],
        scratch_shapes=[pltpu.VMEM((ff, D), BF16), pltpu.SemaphoreType.DMA],
        args=(gu, gu, wd, h, next_gain), comm=comm)


def _ffn_down_loss(gu, wd, h, gf, tgt, *, tm, name, comm=None):
    _, T, ff = gu.shape
    D = h.shape[1]
    B, S, _ = tgt.shape
    per_seq = (S + N_META) // tm
    body_rows = tm - N_META
    chunks = _chunks(ff)

    def body(g_ref, u_ref, wd_hbm, h_ref, gf_ref, tgt_hbm, dh_ref, dhb_ref, dg_ref, loss_ref, wd_v, tg_v, sem, tsem):
        i = pl.program_id(0)
        b, t = i // per_seq, i % per_seq

        @pl.when(i == 0)
        def _():
            cp = pltpu.make_async_copy(wd_hbm, wd_v, sem)
            cp.start()
            cp.wait()
            dg_ref[...] = jnp.zeros_like(dg_ref)
            loss_ref[...] = jnp.zeros_like(loss_ref)
            tg_v[0:N_META, :] = jnp.zeros((N_META, D), F32)

        def fetch(fn):
            @pl.when(t == 0)
            def _():
                fn(pltpu.make_async_copy(tgt_hbm.at[b, pl.ds(0, body_rows)], tg_v.at[pl.ds(N_META, body_rows)], tsem))

            @pl.when(t != 0)
            def _():
                fn(pltpu.make_async_copy(tgt_hbm.at[b, pl.ds(pl.multiple_of(t * tm - N_META, 8), tm)], tg_v, tsem))

        fetch(lambda cp: cp.start())

        def finish(rows, acc):
            if rows.start == 0:
                fetch(lambda cp: cp.wait())
            x = h_ref[rows, :] + 0.5 * acc
            gain = gf_ref[...]
            r = lax.rsqrt(jnp.mean(x * x, axis=-1, keepdims=True) + EPS)
            y = x * r
            pos = t * tm + rows.start + lax.broadcasted_iota(jnp.int32, (rows.stop - rows.start, 1), 0)
            err = jnp.where(pos >= N_META, y * gain - tg_v[rows, :], 0.0)
            loss_ref[...] += 0.5 * jnp.sum(jnp.mean(err * err, axis=-1, keepdims=True))
            dout = err / D
            dg_ref[...] += jnp.sum(dout * y, axis=0, keepdims=True)
            dy = dout * gain
            dh = r * (dy - y * jnp.mean(dy * y, axis=-1, keepdims=True))
            dh_ref[rows, :] = dh
            dhb_ref[rows, :] = (0.5 * dh).astype(BF16)

        _down_in_bands(g_ref, u_ref, wd_v, _band_edges(tm), chunks, finish)

    row = pl.BlockSpec((tm, D), lambda i: (i, 0))
    const = lambda i: (0, 0)
    return _launch(
        body, name=name, grid=(T // tm,),
        in_specs=[pl.BlockSpec((None, tm, ff), lambda i: (0, i, 0)),
                  pl.BlockSpec((None, tm, ff), lambda i: (1, i, 0)),
                  ANY, row, pl.BlockSpec((1, D), const), ANY],
        out_specs=[row, row, pl.BlockSpec((1, D), const), pl.BlockSpec((1, LANES), const)],
        out_shape=[jax.ShapeDtypeStruct((T, D), F32), jax.ShapeDtypeStruct((T, D), BF16),
                   jax.ShapeDtypeStruct((1, D), F32), jax.ShapeDtypeStruct((1, LANES), F32)],
        scratch_shapes=[pltpu.VMEM((ff, D), BF16), pltpu.VMEM((tm, D), F32), pltpu.SemaphoreType.DMA,
                        pltpu.SemaphoreType.DMA],
        args=(gu, gu, wd, h, gf, tgt), comm=comm)


def _ffn_bwd_act(df, gu, wd, *, tm, guc, name, comm=None):
    _, T, ff = gu.shape
    D = df.shape[1]
    nj = ff // guc
    chunks = _chunks(guc)

    def body(df_ref, g_ref, u_ref, wd_ref, o_ref, dwd_ref):
        @pl.when(pl.program_id(1) == 0)
        def _():
            dwd_ref[...] = jnp.zeros_like(dwd_ref)

        dfv = df_ref[...]
        nxt = _dot_nt(dfv, wd_ref[chunks[0][0]:chunks[0][0] + chunks[0][1], :])
        for k, (c0, cw) in enumerate(chunks):
            da = nxt
            if k + 1 < len(chunks):
                n0, nw = chunks[k + 1]
                nxt = _dot_nt(dfv, wd_ref[n0:n0 + nw, :])
            g = g_ref[:, c0:c0 + cw].astype(F32)
            u = u_ref[:, c0:c0 + cw].astype(F32)
            sg = jax.nn.sigmoid(g)
            silu = g * sg
            o_ref[0, :, c0:c0 + cw] = (da * u * (sg * (1.0 + g * (1.0 - sg)))).astype(BF16)
            o_ref[1, :, c0:c0 + cw] = (da * silu).astype(BF16)
            dwd_ref[c0:c0 + cw, :] += _dot_tn((silu * u).astype(BF16), dfv)

    return _launch(
        body, name=name, grid=(nj, T // tm),
        in_specs=[pl.BlockSpec((tm, D), lambda j, i: (i, 0)),
                  pl.BlockSpec((None, tm, guc), lambda j, i: (0, i, j)),
                  pl.BlockSpec((None, tm, guc), lambda j, i: (1, i, j)),
                  pl.BlockSpec((guc, D), lambda j, i: (j, 0))],
        out_specs=[pl.BlockSpec((2, tm, guc), lambda j, i: (0, i, j)), pl.BlockSpec((guc, D), lambda j, i: (j, 0))],
        out_shape=[jax.ShapeDtypeStruct((2, T, ff), BF16), jax.ShapeDtypeStruct((ff, D), F32)],
        args=(df, gu, gu, wd), comm=comm)


def _ffn_bwd_in(dgu, wgu, h, g, dres, *, tm, scale, name, comm=None):
    _, T, ff = dgu.shape
    ns, D, guc = wgu.shape
    nj = ff // guc
    edges = _band_edges(tm)

    def body(dgu_ref, w_hbm, h_ref, g_ref, dres_ref, dh_ref, dhb_ref, dg_ref, w_v, acc, sem):
        i, j = pl.program_id(0), pl.program_id(1)

        @pl.when((i == 0) & (j == 0))
        def _():
            cp = pltpu.make_async_copy(w_hbm, w_v, sem)
            cp.start()
            cp.wait()
            dg_ref[...] = jnp.zeros_like(dg_ref)

        def dots(rows):
            return _dot_nt(dgu_ref[0, rows, :], w_v[j]) + _dot_nt(dgu_ref[1, rows, :], w_v[nj + j])

        @pl.when(j < nj - 1)
        def _():
            part = dots(slice(None))

            @pl.when(j == 0)
            def _():
                acc[...] = part

            @pl.when(j > 0)
            def _():
                acc[...] += part

        @pl.when(j == nj - 1)
        def _():
            bands = [slice(r0, r1) for r0, r1 in zip(edges[:-1], edges[1:])]
            nxt = dots(bands[0])
            for b, rows in enumerate(bands):
                dn = nxt if nj == 1 else acc[rows, :] + nxt
                if b + 1 < len(bands):
                    nxt = dots(bands[b + 1])
                dh, dgain = _rms_bwd(dn, h_ref[rows, :], g_ref[...], dres_ref[rows, :])
                dh_ref[rows, :] = dh
                dhb_ref[rows, :] = (scale * dh).astype(BF16)
                dg_ref[...] += dgain

    return _launch(
        body, name=name, grid=(T // tm, nj),
        in_specs=[pl.BlockSpec((2, tm, guc), lambda i, j: (0, i, j)),
                  ANY,
                  pl.BlockSpec((tm, D), lambda i, j: (i, 0)),
                  pl.BlockSpec((1, D), lambda i, j: (0, 0)),
                  pl.BlockSpec((tm, D), lambda i, j: (i, 0))],
        out_specs=[pl.BlockSpec((tm, D), lambda i, j: (i, 0)),
                   pl.BlockSpec((tm, D), lambda i, j: (i, 0)),
                   pl.BlockSpec((1, D), lambda i, j: (0, 0))],
        out_shape=[jax.ShapeDtypeStruct((T, D), F32), jax.ShapeDtypeStruct((T, D), BF16),
                   jax.ShapeDtypeStruct((1, D), F32)],
        scratch_shapes=[pltpu.VMEM((ns, D, guc), BF16), pltpu.VMEM((tm, D), F32), pltpu.SemaphoreType.DMA],
        args=(dgu, wgu, h, g, dres), comm=comm)


def _ffn_bwd_in_first(dgu, wgu, h, g, dres, *, tm, batch, name, comm=None):
    _, T, ff = dgu.shape
    ns, D, guc = wgu.shape
    nj = ff // guc
    nt = T // tm
    L = T // batch
    per_seq = L // tm
    body_rows = tm - N_META
    edges = _band_edges(tm)

    def body(dgu_ref, w_hbm, h_ref, g_ref, dres_ref, dx_hbm, dmeta_ref, dg_ref, w_v, acc, dh_v, sem, osem):
        i, j = pl.program_id(0), pl.program_id(1)

        @pl.when((i == 0) & (j == 0))
        def _():
            cp = pltpu.make_async_copy(w_hbm, w_v, sem)
            cp.start()
            cp.wait()
            dg_ref[...] = jnp.zeros_like(dg_ref)
            dmeta_ref[...] = jnp.zeros_like(dmeta_ref)

        def dots(rows):
            return _dot_nt(dgu_ref[0, rows, :], w_v[j]) + _dot_nt(dgu_ref[1, rows, :], w_v[nj + j])

        @pl.when(j < nj - 1)
        def _():
            part = dots(slice(None))

            @pl.when(j == 0)
            def _():
                acc[...] = part

            @pl.when(j > 0)
            def _():
                acc[...] += part

        def head_copy(b):
            return pltpu.make_async_copy(dh_v.at[pl.ds(N_META, body_rows)], dx_hbm.at[b, pl.ds(0, body_rows)], osem)

        def tail_copy(b, t):
            return pltpu.make_async_copy(dh_v, dx_hbm.at[b, pl.ds(pl.multiple_of(t * tm - N_META, 8), tm)], osem)

        def on_tile(k, head_fn, tail_fn):
            @pl.when(k % per_seq == 0)
            def _():
                head_fn(head_copy(k // per_seq))

            @pl.when(k % per_seq != 0)
            def _():
                tail_fn(tail_copy(k // per_seq, k % per_seq))

        @pl.when(j == nj - 1)
        def _():
            @pl.when(i > 0)
            def _():
                on_tile(i - 1, lambda cp: cp.wait(), lambda cp: cp.wait())

            bands = [slice(r0, r1) for r0, r1 in zip(edges[:-1], edges[1:])]
            nxt = dots(bands[0])
            for b, rows in enumerate(bands):
                dn = nxt if nj == 1 else acc[rows, :] + nxt
                if b + 1 < len(bands):
                    nxt = dots(bands[b + 1])
                dh, dgain = _rms_bwd(dn, h_ref[rows, :], g_ref[...], dres_ref[rows, :])
                dg_ref[...] += dgain
                dh_v[rows, :] = dh
                if b == 0:
                    @pl.when(i % per_seq == 0)
                    def _():
                        dmeta_ref[...] += dh[0:N_META, :]

            on_tile(i, lambda cp: cp.start(), lambda cp: cp.start())

            @pl.when(i == nt - 1)
            def _():
                on_tile(i, lambda cp: cp.wait(), lambda cp: cp.wait())

    return _launch(
        body, name=name, grid=(nt, nj),
        in_specs=[pl.BlockSpec((2, tm, guc), lambda i, j: (0, i, j)),
                  ANY,
                  pl.BlockSpec((tm, D), lambda i, j: (i, 0)),
                  pl.BlockSpec((1, D), lambda i, j: (0, 0)),
                  pl.BlockSpec((tm, D), lambda i, j: (i, 0))],
        out_specs=[ANY, pl.BlockSpec((N_META, D), lambda i, j: (0, 0)), pl.BlockSpec((1, D), lambda i, j: (0, 0))],
        out_shape=[jax.ShapeDtypeStruct((batch, L - N_META, D), F32), jax.ShapeDtypeStruct((N_META, D), F32),
                   jax.ShapeDtypeStruct((1, D), F32)],
        scratch_shapes=[pltpu.VMEM((ns, D, guc), BF16), pltpu.VMEM((tm, D), F32), pltpu.VMEM((tm, D), F32),
                        pltpu.SemaphoreType.DMA, pltpu.SemaphoreType.DMA],
        args=(dgu, wgu, h, g, dres), comm=comm)


def _mix_bwd_in(parts, w_main, w_fg, h, g, dres, *, tm, scale, name, comm=None):
    T, D = h.shape
    widths = [p.shape[1] for p in parts]
    offs = [sum(widths[:k]) for k in range(len(widths))]
    npart = len(parts)
    wide = sum(widths)
    edges = _band_edges(tm)

    def body(*refs):
        p_refs = refs[:npart]
        wm_ref, wf_ref, h_ref, g_ref, dres_ref, dh_ref, dhb_ref, dg_ref, all_ref = refs[npart:]

        @pl.when(pl.program_id(0) == 0)
        def _():
            dg_ref[...] = jnp.zeros_like(dg_ref)

        for p_ref, off, wd_ in zip(p_refs, offs, widths):
            for c0, cw in _chunks(wd_):
                all_ref[:, off + c0:off + c0 + cw] = p_ref[:, c0:c0 + cw].astype(BF16)
        n_main = offs[-1]

        def dots(rows):
            return _dot_nt(all_ref[rows, :n_main], wm_ref[...]) + _dot_nt(all_ref[rows, n_main:], wf_ref[...])

        bands = [slice(r0, r1) for r0, r1 in zip(edges[:-1], edges[1:])]
        nxt = dots(bands[0])
        for b, rows in enumerate(bands):
            dn = nxt
            if b + 1 < len(bands):
                nxt = dots(bands[b + 1])
            dh, dgain = _rms_bwd(dn, h_ref[rows, :], g_ref[...], dres_ref[rows, :])
            dh_ref[rows, :] = dh
            dhb_ref[rows, :] = (scale * dh).astype(BF16)
            dg_ref[...] += dgain

    row = lambda i: (i, 0)
    const = lambda i: (0, 0)
    return _launch(
        body, name=name, grid=(T // tm,),
        in_specs=[pl.BlockSpec((tm, p.shape[1]), row) for p in parts]
                 + [pl.BlockSpec(w_main.shape, const), pl.BlockSpec(w_fg.shape, const),
                    pl.BlockSpec((tm, D), row), pl.BlockSpec((1, D), const), pl.BlockSpec((tm, D), row)],
        out_specs=[pl.BlockSpec((tm, D), row), pl.BlockSpec((tm, D), row), pl.BlockSpec((1, D), const),
                   pl.BlockSpec((tm, wide), row)],
        out_shape=[jax.ShapeDtypeStruct((T, D), F32), jax.ShapeDtypeStruct((T, D), BF16),
                   jax.ShapeDtypeStruct((1, D), F32), jax.ShapeDtypeStruct((T, wide), BF16)],
        args=(*parts, w_main, w_fg, h, g, dres), comm=comm)


def _matmul_tn(x, y, *, tm, nb, x_spec, y_spec, out_shape, out_spec, kb, name, comm=None):
    T = y.shape[-2]
    chunks = _chunks(kb)

    def body(x_ref, y_ref, o_ref):
        @pl.when(pl.program_id(1) == 0)
        def _():
            o_ref[...] = jnp.zeros_like(o_ref)

        yv = y_ref[...].astype(BF16)
        nxt = _dot_tn(x_ref[:, chunks[0][0]:chunks[0][0] + chunks[0][1]], yv)
        for k, (c0, cw) in enumerate(chunks):
            cur = nxt
            if k + 1 < len(chunks):
                n0, nw = chunks[k + 1]
                nxt = _dot_tn(x_ref[:, n0:n0 + nw], yv)
            o_ref[c0:c0 + cw, :] += cur

    return _launch(
        body, name=name, grid=(nb, T // tm),
        in_specs=[x_spec, y_spec], out_specs=out_spec, out_shape=out_shape, args=(x, y), comm=comm)


def _tri(n, lower):
    r = lax.broadcasted_iota(jnp.int32, (n, n), 0)
    c = lax.broadcasted_iota(jnp.int32, (n, n), 1)
    return jnp.where((r >= c) if lower else (r <= c), 1.0, 0.0).astype(BF16)


def _tri_dot(tri, v):
    hi, mid, lo = _split3(v)
    return _dot(tri, hi) + _dot(tri, mid) + _dot(tri, lo)


def _fcum(fg, bf, *, ch, name):
    B, L, W = fg.shape
    nch = L // ch

    def body(fg_ref, bf_ref, f_ref):
        tri = _tri(ch, True)
        carry = jnp.zeros((1, W), F32)
        for c in range(nch):
            x = fg_ref[c * ch:(c + 1) * ch, :] + bf_ref[...]
            lf = jnp.minimum(x, 0.0) - jnp.log(1.0 + jnp.exp(-jnp.abs(x)))
            f_ref[c * ch:(c + 1) * ch, :] = _tri_dot(tri, lf) + carry
            carry = carry + jnp.sum(lf, axis=0, keepdims=True)

    return pl.pallas_call(
        body, name=name, grid=(B,),
        in_specs=[pl.BlockSpec((None, L, W), lambda b: (b, 0, 0)), pl.BlockSpec((1, W), lambda b: (0, 0))],
        out_specs=pl.BlockSpec((None, L, W), lambda b: (b, 0, 0)),
        out_shape=jax.ShapeDtypeStruct((B, L, W), F32),
        compiler_params=_params("arbitrary"),
    )(fg, bf)


def _fcum_bwd(dF_rows, dF_cols, fg, bf, *, ch, name):
    B, L, W = fg.shape
    nch = L // ch

    def body(dfr_ref, dfc_ref, fg_ref, bf_ref, dfg_ref, db_ref):
        @pl.when(pl.program_id(0) == 0)
        def _():
            db_ref[...] = jnp.zeros_like(db_ref)

        tri = _tri(ch, False)
        carry = jnp.zeros((1, W), F32)
        dbs = jnp.zeros((1, W), F32)
        for c in reversed(range(nch)):
            d = dfr_ref[c * ch:(c + 1) * ch, :] - dfc_ref[c * ch:(c + 1) * ch, :]
            dlf = _tri_dot(tri, d) + carry
            carry = carry + jnp.sum(d, axis=0, keepdims=True)
            x = fg_ref[c * ch:(c + 1) * ch, :] + bf_ref[...]
            dfg = dlf * jax.nn.sigmoid(-x)
            dfg_ref[c * ch:(c + 1) * ch, :] = dfg.astype(BF16)
            dbs = dbs + jnp.sum(dfg, axis=0, keepdims=True)
        db_ref[...] += dbs

    blk = pl.BlockSpec((None, L, W), lambda b: (b, 0, 0))
    return pl.pallas_call(
        body, name=name, grid=(B,),
        in_specs=[blk, blk, blk, pl.BlockSpec((1, W), lambda b: (0, 0))],
        out_specs=[blk, pl.BlockSpec((1, W), lambda b: (0, 0))],
        out_shape=[jax.ShapeDtypeStruct((B, L, W), BF16), jax.ShapeDtypeStruct((1, W), F32)],
        compiler_params=_params("arbitrary"),
    )(dF_rows, dF_cols, fg, bf)


def _band_edges(tq):
    return sorted({min(tq, (k * tq // ROW_BANDS + HALO - 1) // HALO * HALO) for k in range(ROW_BANDS + 1)})


def _pair(h):
    return slice((h // 2) * 2 * HEAD_DIM, (h // 2 + 1) * 2 * HEAD_DIM)


def _own_lanes(a, h):
    low = lax.broadcasted_iota(jnp.int32, a.shape, 1) < HEAD_DIM
    return jnp.where(low if h % 2 == 0 else jnp.logical_not(low), a, jnp.zeros_like(a))


def _sum_lane(h):
    return HEAD_DIM if h % 2 == 0 else 0


def _own_lanes_and_ones(a, h):
    lane = lax.broadcasted_iota(jnp.int32, a.shape, 1)
    low = lane < HEAD_DIM
    return jnp.where(low if h % 2 == 0 else jnp.logical_not(low), a,
                     jnp.where(lane == _sum_lane(h), jnp.ones_like(a), jnp.zeros_like(a)))


def _attn_fwd(proj, fr, *, tq, n_heads, name, comm=None):
    B, L, _ = proj.shape
    AD = n_heads * HEAD_DIM
    nq = L // tq
    W = LANES
    scale = HEAD_DIM ** -0.5
    edges = _band_edges(tq)

    v_ones, sum_lane = _own_lanes_and_ones, _sum_lane

    def body(q_ref, k_ref, v_ref, fr_ref, o_ref, lse_ref, m_s, acc_s):
        qi, ki = pl.program_id(1), pl.program_id(2)

        @pl.when(ki == 0)
        def _():
            m_s[...] = jnp.full_like(m_s, NEG)
            acc_s[...] = jnp.zeros_like(acc_s)

        def tile(diagonal):
            lane = lax.broadcasted_iota(jnp.int32, (tq, W), 1)
            m_all = m_s[...]
            m_out = m_all
            bands = [(r0, r1, r1 if diagonal else tq) for r0, r1 in zip(edges[:-1], edges[1:])]
            if diagonal:
                masks = {r0: (lax.broadcasted_iota(jnp.int32, (r1 - r0, c1), 1)
                              <= r0 + lax.broadcasted_iota(jnp.int32, (r1 - r0, c1), 0)) for r0, r1, c1 in bands}

            def scores(h, band):
                r0, r1, c1 = band
                sl = slice(h * HEAD_DIM, (h + 1) * HEAD_DIM)
                return _dot_nt(q_ref[r0:r1, sl] * scale, k_ref[0:c1, sl])

            work = [(h, band) for h in range(n_heads) for band in bands]
            nxt = scores(*work[0])
            for w, (h, band) in enumerate(work):
                r0, r1, c1 = band
                sl = slice(h * HEAD_DIM, (h + 1) * HEAD_DIM)
                s = nxt - fr_ref[h:h + 1, 0:c1]
                if w + 1 < len(work):
                    nxt = scores(*work[w + 1])
                if diagonal:
                    s = jnp.where(masks[r0], s, NEG)
                m_old = m_all[r0:r1, h:h + 1]
                m_new = jnp.maximum(m_old, jnp.max(s, axis=1, keepdims=True))
                alpha = jnp.exp(m_old - m_new)
                p = jnp.exp(s - m_new)
                own = slice(h * 2 * HEAD_DIM, (h + 1) * 2 * HEAD_DIM)
                acc_s[r0:r1, own] = alpha * acc_s[r0:r1, own] + _dot(p.astype(BF16), v_ones(v_ref[0:c1, _pair(h)], h))
                if r0 == 0:
                    m_parts = []
                m_parts.append(m_new)
                if r1 == tq:
                    m_out = jnp.where(lane == h, jnp.concatenate(m_parts, axis=0), m_out)
            m_s[...] = m_out

        @pl.when(ki < qi)
        def _():
            tile(False)

        @pl.when(ki == qi)
        def _():
            tile(True)
            lane = lax.broadcasted_iota(jnp.int32, (tq, W), 1)
            low = lax.broadcasted_iota(jnp.int32, (tq, 2 * HEAD_DIM), 1) < HEAD_DIM
            l_all = jnp.ones((tq, W), F32)
            for h in range(0, n_heads, 2):
                even = acc_s[:, h * 2 * HEAD_DIM:(h + 1) * 2 * HEAD_DIM]
                odd = acc_s[:, (h + 1) * 2 * HEAD_DIM:(h + 2) * 2 * HEAD_DIM]
                l_even = even[:, sum_lane(h):sum_lane(h) + 1]
                l_odd = odd[:, sum_lane(h + 1):sum_lane(h + 1) + 1]
                o_ref[:, _pair(h)] = jnp.where(low, even / l_even, odd / l_odd)
                l_all = jnp.where(lane == h, l_even, jnp.where(lane == h + 1, l_odd, l_all))
            lse_ref[...] = jnp.where(lane < n_heads, m_s[...] + jnp.log(l_all), 0.0)

    kv = lambda b, qi, ki: jnp.minimum(ki, qi)
    return _launch(
        body, name=name, grid=(B, nq, nq), args=(proj, proj, proj, fr), comm=comm,
        in_specs=[pl.BlockSpec((None, tq, AD), lambda b, qi, ki: (b, qi, 3)),
                  pl.BlockSpec((None, tq, AD), lambda b, qi, ki: (b, kv(b, qi, ki), 4)),
                  pl.BlockSpec((None, tq, AD), lambda b, qi, ki: (b, kv(b, qi, ki), 5)),
                  pl.BlockSpec((None, None, n_heads, tq), lambda b, qi, ki: (b, kv(b, qi, ki), 0, 0))],
        out_specs=[pl.BlockSpec((None, tq, AD), lambda b, qi, ki: (b, qi, 0)),
                   pl.BlockSpec((None, tq, W), lambda b, qi, ki: (b, qi, 0))],
        out_shape=[jax.ShapeDtypeStruct((B, L, AD), F32), jax.ShapeDtypeStruct((B, L, W), F32)],
        scratch_shapes=[pltpu.VMEM((tq, W), F32), pltpu.VMEM((tq, n_heads * 2 * HEAD_DIM), F32)])


def _attn_bwd(proj, o, do, lse, fr, *, tq, n_heads, name, comm=None):
    B, L, _ = proj.shape
    AD = n_heads * HEAD_DIM
    nq = L // tq
    W = LANES
    HW = 2 * HEAD_DIM
    scale = HEAD_DIM ** -0.5
    edges = _band_edges(tq)

    def body(q_ref, k_ref, v_ref, o_ref, do_ref, lse_ref, fr_ref,
             dq_ref, dk_ref, dv_ref, dfk_ref, dfq_ref, dq_s, dk_s, dv_s):
        kj, qi = pl.program_id(1), pl.program_id(2)

        @pl.when((kj == 0) & (qi == 0))
        def _():
            dq_s[...] = jnp.zeros_like(dq_s)

        @pl.when(qi == kj)
        def _():
            dk_s[...] = jnp.zeros_like(dk_s)
            dv_s[...] = jnp.zeros_like(dv_s)

        def tile(diagonal):
            bands = [(r0, r1, r1) for r0, r1 in zip(edges[:-1], edges[1:])] if diagonal else [(0, tq, tq)]
            lse = lse_ref[...]
            for r0, r1, c1 in bands:
                nr = r1 - r0
                rows = pl.ds(pl.multiple_of(qi * tq + r0, 8), nr)
                if diagonal:
                    mask = (lax.broadcasted_iota(jnp.int32, (nr, c1), 1)
                            <= r0 + lax.broadcasted_iota(jnp.int32, (nr, c1), 0))
                def scores(h):
                    ps = _pair(h)
                    k = k_ref[0:c1, ps]
                    qs = q_ref[r0:r1, ps] * scale
                    dov = _own_lanes(do_ref[r0:r1, ps], h)
                    return _dot_nt(_own_lanes(qs, h), k), _dot_nt(dov, v_ref[0:c1, ps]), k, qs, dov

                nxt = scores(0)
                for h in range(n_heads):
                    ps = _pair(h)
                    own = slice(h * HW, (h + 1) * HW)
                    s, dp, k, qs, dov = nxt
                    if h + 1 < n_heads:
                        nxt = scores(h + 1)
                    s = s - fr_ref[h:h + 1, 0:c1]
                    if diagonal:
                        s = jnp.where(mask, s, NEG)
                    p = jnp.exp(s - lse[r0:r1, h:h + 1])
                    dsum = jnp.sum(dov.astype(F32) * o_ref[r0:r1, ps], axis=1, keepdims=True)
                    dsb = (p * (dp - dsum)).astype(BF16)
                    dv = _dot_tn(p.astype(BF16), dov)
                    dk_s[0:c1, own] += _dot_tn(dsb, _own_lanes_and_ones(qs, h))
                    dq_s[rows, own] += _dot(dsb, _own_lanes_and_ones(k, h))
                    if h % 2 == 0:
                        dv_even = dv
                    else:
                        dv_s[0:c1, ps] += dv_even + dv

        def compact(acc, data_scale):
            rows = acc.shape[0]
            low = lax.broadcasted_iota(jnp.int32, (rows, HW), 1) < HEAD_DIM
            lane = lax.broadcasted_iota(jnp.int32, (rows, W), 1)
            vals, sums = [], jnp.zeros((rows, W), F32)
            for h in range(0, n_heads, 2):
                even, odd = acc[:, h * HW:(h + 1) * HW], acc[:, (h + 1) * HW:(h + 2) * HW]
                vals.append(jnp.where(low, even, odd) * data_scale)
                sums = jnp.where(lane == h, even[:, _sum_lane(h):_sum_lane(h) + 1],
                                 jnp.where(lane == h + 1, odd[:, _sum_lane(h + 1):_sum_lane(h + 1) + 1], sums))
            return vals, sums

        @pl.when(qi > kj)
        def _():
            tile(False)

        @pl.when(qi == kj)
        def _():
            tile(True)
            rows = pl.ds(pl.multiple_of(qi * tq, 8), tq)
            vals, sums = compact(dq_s[rows, :], scale)
            for h in range(0, n_heads, 2):
                dq_ref[rows, _pair(h)] = vals[h // 2]
            dfq_ref[rows, :] = sums

        @pl.when(qi == nq - 1)
        def _():
            vals, sums = compact(dk_s[...], 1.0)
            for h in range(0, n_heads, 2):
                dk_ref[:, _pair(h)] = vals[h // 2].astype(BF16)
            dfk_ref[...] = sums
            dv_ref[...] = dv_s[...].astype(BF16)

    qq = lambda b, kj, qi: jnp.maximum(qi, kj)
    qblk = lambda w, cb: pl.BlockSpec((None, tq, w), lambda b, kj, qi: (b, qq(b, kj, qi), cb))
    kblk = lambda w, cb: pl.BlockSpec((None, tq, w), lambda b, kj, qi: (b, kj, cb))
    return _launch(
        body, name=name, grid=(B, nq, nq), args=(proj, proj, proj, o, do, lse, fr), comm=comm,
        in_specs=[qblk(AD, 3), kblk(AD, 4), kblk(AD, 5), qblk(AD, 0), qblk(AD, 0), qblk(W, 0),
                  pl.BlockSpec((None, None, n_heads, tq), lambda b, kj, qi: (b, kj, 0, 0))],
        out_specs=[pl.BlockSpec((None, L, AD), lambda b, kj, qi: (b, 0, 0)),
                   kblk(AD, 0), kblk(AD, 0), kblk(W, 0),
                   pl.BlockSpec((None, L, W), lambda b, kj, qi: (b, 0, 0))],
        out_shape=[jax.ShapeDtypeStruct((B, L, AD), F32), jax.ShapeDtypeStruct((B, L, AD), BF16),
                   jax.ShapeDtypeStruct((B, L, AD), BF16), jax.ShapeDtypeStruct((B, L, W), F32),
                   jax.ShapeDtypeStruct((B, L, W), F32)],
        scratch_shapes=[pltpu.VMEM((L, n_heads * HW), F32), pltpu.VMEM((tq, n_heads * HW), F32),
                        pltpu.VMEM((tq, AD), F32)])


def _mix_gather(refs, first):
    b_ref, c_ref, hc_ref, cp_ref, hcp_ref, o_ref, cw_ref, p_ref = refs
    bg = b_ref[...].astype(F32)
    u = c_ref[...].astype(F32) * hc_ref[...].astype(F32)
    prev = cp_ref[...].astype(F32) * hcp_ref[...].astype(F32)
    prev = jnp.where(first, 0.0, prev)
    cv, u1, u2 = _causal_conv(u, prev, cw_ref[...])
    yc = bg * cv
    p = p_ref[...]
    rc = lax.rsqrt(_group_mean(yc * yc, p) + EPS)
    ya = o_ref[...].astype(F32)
    ra = lax.rsqrt(_group_mean(ya * ya, p) + EPS)
    return bg, (u, u1, u2), cv, yc * rc, rc, ya * ra, ra


def _mix_specs(tm, CD):
    per = tm // HALO
    cur = lambda cb: pl.BlockSpec((None, tm, CD), lambda b, i: (b, i, cb))
    prev = lambda cb: pl.BlockSpec((None, HALO, CD), lambda b, i: (b, jnp.maximum(i * per - 1, 0), cb))
    return [cur(0), cur(1), cur(2), prev(1), prev(2), cur(0)]


def _mix_out(proj, o, cw, gc, ga, wout, h, pmat, next_gain, *, tm, name, comm=None):
    B, L, D = h.shape
    CD = o.shape[-1]
    const = lambda b, i: (0, 0)

    def body(b_ref, c_ref, hc_ref, cp_ref, hcp_ref, o_ref, cw_ref, p_ref, gc_ref, ga_ref, w_ref, h_ref, ng_ref,
             out_ref, y_ref, n_ref):
        first = pl.program_id(1) == 0
        _, _, _, zc, _, za, _ = _mix_gather((b_ref, c_ref, hc_ref, cp_ref, hcp_ref, o_ref, cw_ref, p_ref), first)
        yc = (zc * gc_ref[...]).astype(BF16)
        ya = (za * ga_ref[...]).astype(BF16)
        y_ref[:, :CD] = yc
        y_ref[:, CD:] = ya
        out = h_ref[...] + _dot(yc, w_ref[:CD, :]) + _dot(ya, w_ref[CD:, :])
        out_ref[...] = out
        n_ref[...] = _rms(out, ng_ref[...])

    tile = pl.BlockSpec((None, tm, D), lambda b, i: (b, i, 0))
    return _launch(
        body, name=name, grid=(B, L // tm),
        in_specs=_mix_specs(tm, CD)
                 + [pl.BlockSpec(cw.shape, const), pl.BlockSpec(pmat.shape, const),
                    pl.BlockSpec((1, CD), const), pl.BlockSpec((1, CD), const), pl.BlockSpec((D, D), const),
                    tile, pl.BlockSpec((1, D), const)],
        out_specs=[tile, tile, tile],
        out_shape=[jax.ShapeDtypeStruct((B, L, D), F32), jax.ShapeDtypeStruct((B, L, D), BF16),
                   jax.ShapeDtypeStruct((B, L, D), BF16)],
        args=(proj, proj, proj, proj, proj, o, cw, pmat, gc, ga, wout, h, next_gain), comm=comm)


def _mix_out_bwd(dhb, proj, o, cw, gc, ga, wout, pmat, *, tm, name, comm=None):
    B, L, D = dhb.shape
    CD = o.shape[-1]
    const = lambda b, i: (0, 0)

    def body(dh_ref, b_ref, c_ref, hc_ref, cp_ref, hcp_ref, o_ref, cw_ref, p_ref, gc_ref, ga_ref, w_ref,
             db_ref, dcv_ref, do_ref, dgc_ref, dga_ref, dcw_ref):
        first = pl.program_id(1) == 0

        @pl.when((pl.program_id(0) == 0) & first)
        def _():
            dgc_ref[...] = jnp.zeros_like(dgc_ref)
            dga_ref[...] = jnp.zeros_like(dga_ref)
            dcw_ref[...] = jnp.zeros_like(dcw_ref)

        bg, us, cv, zc, rc, za, ra = _mix_gather(
            (b_ref, c_ref, hc_ref, cp_ref, hcp_ref, o_ref, cw_ref, p_ref), first)
        p = p_ref[...]
        dh = dh_ref[...]
        dyc = _dot_nt(dh, w_ref[:CD, :])
        dya = _dot_nt(dh, w_ref[CD:, :])

        dgc_ref[...] += jnp.sum(dyc * zc, axis=0, keepdims=True)
        dz = dyc * gc_ref[...]
        dx = rc * (dz - zc * _group_mean(dz * zc, p))
        db_ref[...] = (dx * cv).astype(BF16)
        dcv = dx * bg
        dcv_ref[...] = dcv.astype(BF16)
        for k in range(3):
            dcw_ref[k:k + 1, :] += jnp.sum(dcv * us[2 - k], axis=0, keepdims=True)

        dga_ref[...] += jnp.sum(dya * za, axis=0, keepdims=True)
        dz = dya * ga_ref[...]
        do_ref[...] = (ra * (dz - za * _group_mean(dz * za, p))).astype(BF16)

    tile = lambda w: pl.BlockSpec((None, tm, w), lambda b, i: (b, i, 0))
    return _launch(
        body, name=name, grid=(B, L // tm), comm=comm,
        args=(dhb, proj, proj, proj, proj, proj, o, cw, pmat, gc, ga, wout),
        in_specs=[tile(D)] + _mix_specs(tm, CD)
                 + [pl.BlockSpec(cw.shape, const), pl.BlockSpec(pmat.shape, const),
                    pl.BlockSpec((1, CD), const), pl.BlockSpec((1, CD), const), pl.BlockSpec((D, D), const)],
        out_specs=[tile(CD), tile(CD), tile(CD),
                   pl.BlockSpec((1, CD), const), pl.BlockSpec((1, CD), const), pl.BlockSpec((8, CD), const)],
        out_shape=[jax.ShapeDtypeStruct((B, L, CD), BF16)] * 3
                  + [jax.ShapeDtypeStruct((1, CD), F32)] * 2 + [jax.ShapeDtypeStruct((8, CD), F32)])


def _conv_bwd(dcv, proj, cw, *, tm, name):
    B, L, CD = dcv.shape
    per = tm // HALO
    nhalo = L // HALO
    nt = L // tm

    def body(d_ref, dn_ref, c_ref, hc_ref, cw_ref, out_ref):
        last = pl.program_id(1) == nt - 1
        d = d_ref[...].astype(F32)
        nxt = jnp.where(last, 0.0, dn_ref[...].astype(F32))
        n0, n1 = _row_of(nxt, 0), _row_of(nxt, 1)
        rows = lax.broadcasted_iota(jnp.int32, d.shape, 0)
        d1 = jnp.where(rows == tm - 1, n0, pltpu.roll(d, tm - 1, 0))
        d2 = jnp.where(rows == tm - 2, n0, jnp.where(rows == tm - 1, n1, pltpu.roll(d, tm - 2, 0)))
        w = cw_ref[...]
        du = w[2:3, :] * d + w[1:2, :] * d1 + w[0:1, :] * d2
        out_ref[:, :CD] = (du * hc_ref[...].astype(F32)).astype(BF16)
        out_ref[:, CD:] = (du * c_ref[...].astype(F32)).astype(BF16)

    return pl.pallas_call(
        body, name=name, grid=(B, nt),
        in_specs=[pl.BlockSpec((None, tm, CD), lambda b, i: (b, i, 0)),
                  pl.BlockSpec((None, HALO, CD), lambda b, i: (b, jnp.minimum((i + 1) * per, nhalo - 1), 0)),
                  pl.BlockSpec((None, tm, CD), lambda b, i: (b, i, 1)),
                  pl.BlockSpec((None, tm, CD), lambda b, i: (b, i, 2)),
                  pl.BlockSpec(cw.shape, lambda b, i: (0, 0))],
        out_specs=pl.BlockSpec((None, tm, 2 * CD), lambda b, i: (b, i, 0)),
        out_shape=jax.ShapeDtypeStruct((B, L, 2 * CD), BF16),
        compiler_params=_params("arbitrary", "arbitrary"),
    )(dcv, dcv, proj, proj, cw)


def _place():
    x, y, c = lax.axis_index("x"), lax.axis_index("y"), lax.axis_index("c")
    others = [(1 - x, y), (x, 1 - y), (1 - x, 1 - y)]
    return x, y, c, others


def _all_gather_shards(shards, *, name):
    n = len(shards)

    def body(*refs):
        ins, outs = refs[:n], refs[n:2 * n]
        send, recv, fsend, frecv, lsem = refs[2 * n:]
        x, y, c, others = _place()
        me = 2 * x + y
        local = [pltpu.make_async_copy(ins[t], outs[t].at[me], lsem.at[t]) for t in range(n)]
        for cp in local:
            cp.start()

        def half(t, k):
            hr = shards[t].shape[0] // 2
            return pl.ds(pl.multiple_of(k * hr, HALO), hr)

        def ici(t, j, src_chip, to):
            src = ins[t].at[half(t, c)] if to is not None else outs[t].at[src_chip, half(t, c)]
            return pltpu.make_async_remote_copy(
                src_ref=src, dst_ref=outs[t].at[src_chip, half(t, c)],
                send_sem=send.at[3 * t + j], recv_sem=recv.at[3 * t + j],
                device_id=(x, y, c) if to is None else to, device_id_type=MESH)

        def d2d(t, j, src_chip, k):
            return pltpu.make_async_remote_copy(
                src_ref=outs[t].at[src_chip, half(t, k)], dst_ref=outs[t].at[src_chip, half(t, k)],
                send_sem=fsend.at[3 * t + j], recv_sem=frecv.at[3 * t + j],
                device_id=(x, y, 1 - c), device_id_type=MESH)

        firsts = [ici(t, j, me, (ox, oy, c)) for t in range(n) for j, (ox, oy) in enumerate(others)]
        for cp in firsts:
            cp.start()
        passed = []
        for t in range(n):
            for j, (ox, oy) in enumerate(others):
                ici(t, j, 2 * ox + oy, None).wait_recv()
                cp = d2d(t, j, 2 * ox + oy, c)
                cp.start()
                passed.append(cp)
        for t in range(n):
            for j, (ox, oy) in enumerate(others):
                d2d(t, j, 2 * ox + oy, 1 - c).wait_recv()
        for cp in firsts + passed:
            cp.wait_send()
        for cp in local:
            cp.wait()

    return pl.pallas_call(
        body, name=name,
        in_specs=[ANY] * n, out_specs=[ANY] * n,
        out_shape=[jax.ShapeDtypeStruct((N_SHARD,) + s.shape, s.dtype) for s in shards],
        scratch_shapes=[pltpu.SemaphoreType.DMA((3 * n,))] * 4 + [pltpu.SemaphoreType.DMA((n,))],
    )(*shards)


def _all_reduce_small(slab, *, name):
    def body(in_ref, out_ref, gath, send, recv):
        x, y, c, _ = _place()
        me = 4 * x + 2 * y + c
        gath[me] = in_ref[...]
        copies, peers = [], []
        for m in range(1, N_DEV):
            px = jnp.where((m >> 2) & 1, 1 - x, x)
            py = jnp.where((m >> 1) & 1, 1 - y, y)
            pc = jnp.where(m & 1, 1 - c, c)
            cp = pltpu.make_async_remote_copy(
                src_ref=in_ref, dst_ref=gath.at[me], send_sem=send.at[m - 1], recv_sem=recv.at[m - 1],
                device_id=(px, py, pc), device_id_type=MESH)
            cp.start()
            copies.append(cp)
            peers.append(4 * px + 2 * py + pc)
        for m in range(1, N_DEV):
            pltpu.make_async_remote_copy(
                src_ref=in_ref, dst_ref=gath.at[peers[m - 1]], send_sem=send.at[m - 1], recv_sem=recv.at[m - 1],
                device_id=(x, y, c), device_id_type=MESH).wait_recv()
        for cp in copies:
            cp.wait_send()
        acc = gath[0]
        for k in range(1, N_DEV):
            acc = acc + gath[k]
        out_ref[...] = acc

    vm = pl.BlockSpec(memory_space=pltpu.VMEM)
    return pl.pallas_call(
        body, name=name, in_specs=[vm], out_specs=vm,
        out_shape=jax.ShapeDtypeStruct(slab.shape, slab.dtype),
        scratch_shapes=[pltpu.VMEM((N_DEV,) + slab.shape, slab.dtype),
                        pltpu.SemaphoreType.DMA((N_DEV - 1,)), pltpu.SemaphoreType.DMA((N_DEV - 1,))],
    )(slab)


def _gather_stage(shards, into, *, ici=(), d2d=()):
    n = len(shards) if into is None else len(into)
    ns = len(shards) if ici else 0
    ni, nd = max(len(ici), 1), max(len(d2d), 1)
    shapes = [s.shape for s in shards] if into is None else [p.shape[1:] for p in into]
    dtypes = [s.dtype for s in shards] if into is None else [p.dtype for p in into]

    def copies(ins, outs, sems, sending):
        x, y, c, others = _place()
        me = 2 * x + y
        out = []
        for t in range(n):
            hr = shapes[t][0] // 2
            mine = pl.ds(pl.multiple_of(c * hr, HALO), hr)
            theirs = pl.ds(pl.multiple_of((1 - c) * hr, HALO), hr)
            for a, j in enumerate(ici):
                ox, oy = others[j]
                src_chip = me if sending else 2 * ox + oy
                out.append(pltpu.make_async_remote_copy(
                    src_ref=ins[t].at[mine], dst_ref=outs[t].at[src_chip, mine],
                    send_sem=sems[0].at[ni * t + a], recv_sem=sems[1].at[ni * t + a],
                    device_id=(ox, oy, c) if sending else (x, y, c), device_id_type=MESH))
            for a, j in enumerate(d2d):
                ox, oy = others[j]
                blk = outs[t].at[2 * ox + oy, mine if sending else theirs]
                out.append(pltpu.make_async_remote_copy(
                    src_ref=blk, dst_ref=blk, send_sem=sems[2].at[nd * t + a], recv_sem=sems[3].at[nd * t + a],
                    device_id=(x, y, 1 - c) if sending else (x, y, c), device_id_type=MESH))
        return out

    def local(ins, outs, sems):
        if into is not None:
            return []
        x, y, _, _ = _place()
        return [pltpu.make_async_copy(ins[t], outs[t].at[2 * x + y], sems[4].at[t]) for t in range(n)]

    def start(ins, outs, sems):
        for cp in local(ins, outs, sems) + copies(ins, outs, sems, True):
            cp.start()

    def finish(ins, outs, sems):
        for cp in copies(ins, outs, sems, False):
            cp.wait_recv()
        for cp in copies(ins, outs, sems, True):
            cp.wait_send()
        for cp in local(ins, outs, sems):
            cp.wait()

    return _Comm((list(shards) if ici or into is None else []) + (list(into) if into is not None else []),
                 [jax.ShapeDtypeStruct((N_SHARD,) + tuple(sh), dt) for sh, dt in zip(shapes, dtypes)],
                 [ni * n, ni * n, nd * n, nd * n, n], start, finish,
                 aliases=None if into is None else {ns + t: t for t in range(n)})


def _gather_ici(shards):
    return _gather_stage(shards, None, ici=(0, 1, 2))


def _gather_d2d(parts):
    return _gather_stage((), parts, d2d=(0, 1, 2))


def _swap_halves(grads):
    n = len(grads)

    def copies(ins, outs, sems):
        x, y, c, _ = _place()
        out = []
        for t in range(n):
            hr = grads[t].shape[1] // 2
            rows = pl.ds(pl.multiple_of((1 - c) * hr, 8), hr)
            out.append(pltpu.make_async_remote_copy(
                src_ref=ins[t].at[:, rows, :], dst_ref=outs[t], send_sem=sems[0].at[t], recv_sem=sems[1].at[t],
                device_id=(x, y, 1 - c), device_id_type=MESH))
        return out

    def start(ins, outs, sems):
        for cp in copies(ins, outs, sems):
            cp.start()

    def finish(ins, outs, sems):
        for cp in copies(ins, outs, sems):
            cp.wait()

    return _Comm(grads, [jax.ShapeDtypeStruct((N_SHARD, g.shape[1] // 2, g.shape[2]), g.dtype) for g in grads],
                 [n, n], start, finish)


def _pair_sum(g, got, c, *, name):
    ns, R, C = g.shape
    hr = R // 2

    def body(c_ref, g_ref, r_ref, o_ref):
        o_ref[...] = (g_ref[...] + r_ref[...]).astype(BF16)

    return pl.pallas_call(
        body, name=name,
        grid_spec=pltpu.PrefetchScalarGridSpec(
            num_scalar_prefetch=1, grid=(ns,),
            in_specs=[pl.BlockSpec((None, hr, C), lambda s, cr: (s, cr[0], 0)),
                      pl.BlockSpec((None, hr, C), lambda s, cr: (s, 0, 0))],
            out_specs=pl.BlockSpec((None, hr, C), lambda s, cr: (s, 0, 0))),
        out_shape=jax.ShapeDtypeStruct((ns, hr, C), BF16),
        compiler_params=_params("arbitrary"),
    )(c, g, got)


def _scatter_chips(sums):
    n = len(sums)

    def copies(ins, outs, sems, sending):
        x, y, c, others = _place()
        me = 2 * x + y
        out = []
        for t in range(n):
            for j, (ox, oy) in enumerate(others):
                there = 2 * ox + oy
                out.append(pltpu.make_async_remote_copy(
                    src_ref=ins[t].at[there if sending else me], dst_ref=outs[t].at[me if sending else there],
                    send_sem=sems[0].at[3 * t + j], recv_sem=sems[1].at[3 * t + j],
                    device_id=(ox, oy, c) if sending else (x, y, c), device_id_type=MESH))
        return out

    def start(ins, outs, sems):
        for cp in copies(ins, outs, sems, True):
            cp.start()

    def finish(ins, outs, sems):
        for cp in copies(ins, outs, sems, False):
            cp.wait_recv()
        for cp in copies(ins, outs, sems, True):
            cp.wait_send()

    return _Comm(sums, [jax.ShapeDtypeStruct(s.shape, s.dtype) for s in sums], [3 * n, 3 * n], start, finish)


def _chip_sum(g, got, landed, idx, *, name):
    ns, R, C = g.shape
    hr = R // 2
    steps = next(k for k in (4, 2, 1) if hr % (k * HALO) == 0)
    tr = hr // steps

    def body(i_ref, g_ref, r_ref, a_ref, b_ref, c_ref, o_ref):
        acc = g_ref[...] + r_ref[...]
        for ref in (a_ref, b_ref, c_ref):
            acc = acc + ref[...].astype(F32)
        o_ref[...] = acc

    other = lambda k: pl.BlockSpec((None, tr, C), lambda s, ir: (ir[2 + k], s, 0))
    return pl.pallas_call(
        body, name=name,
        grid_spec=pltpu.PrefetchScalarGridSpec(
            num_scalar_prefetch=1, grid=(steps,),
            in_specs=[pl.BlockSpec((None, tr, C), lambda s, ir: (ir[0], ir[1] * steps + s, 0)),
                      pl.BlockSpec((None, tr, C), lambda s, ir: (ir[0], s, 0)),
                      other(0), other(1), other(2)],
            out_specs=pl.BlockSpec((tr, C), lambda s, ir: (ir[1] * steps + s, 0))),
        out_shape=jax.ShapeDtypeStruct((R, C), F32),
        compiler_params=_params("arbitrary"),
    )(idx, g, got, landed, landed, landed)


def _share_halves(halves):
    n = len(halves)

    def copies(outs, sems, sending):
        x, y, c, _ = _place()
        out = []
        for t in range(n):
            hr = halves[t].shape[0] // 2
            rows = pl.ds(pl.multiple_of((c if sending else 1 - c) * hr, 8), hr)
            out.append(pltpu.make_async_remote_copy(
                src_ref=outs[t].at[rows, :], dst_ref=outs[t].at[rows, :], send_sem=sems[0].at[t],
                recv_sem=sems[1].at[t], device_id=(x, y, 1 - c) if sending else (x, y, c), device_id_type=MESH))
        return out

    def start(ins, outs, sems):
        for cp in copies(outs, sems, True):
            cp.start()

    def finish(ins, outs, sems):
        for cp in copies(outs, sems, False):
            cp.wait_recv()
        for cp in copies(outs, sems, True):
            cp.wait_send()

    return _Comm(halves, [jax.ShapeDtypeStruct(h.shape, h.dtype) for h in halves], [n, n], start, finish,
                 aliases={t: t for t in range(n)})


def _adamw(w, g, m, v, *, name):
    R, C = w.shape
    tr = next((k for k in (128, 64, 32, 16, 8) if R % k == 0), R)

    def body(w_ref, g_ref, m_ref, v_ref, go_ref, d_ref, mo_ref, vo_ref):
        gv = g_ref[...]
        go_ref[...] = gv
        mn = ADAM_B1 * m_ref[...] + (1.0 - ADAM_B1) * gv
        vn = ADAM_B2 * v_ref[...] + (1.0 - ADAM_B2) * (gv * gv)
        m_hat = mn / (1.0 - ADAM_B1 ** ADAM_STEP)
        v_hat = vn / (1.0 - ADAM_B2 ** ADAM_STEP)
        d_ref[...] = -ADAM_LR * (m_hat / (jnp.sqrt(v_hat) + ADAM_EPS) + ADAM_WD * w_ref[...])
        mo_ref[...] = mn
        vo_ref[...] = vn

    blk = pl.BlockSpec((tr, C), lambda i: (i, 0))
    return pl.pallas_call(
        body, name=name, grid=(R // tr,), in_specs=[blk] * 4, out_specs=[blk] * 4,
        out_shape=[jax.ShapeDtypeStruct((R, C), F32)] * 4,
        compiler_params=_params("arbitrary"),
    )(w, g, m, v)


def _pack_small(D, meta, n1, nm, n3, nf, gc, ga, bf, cw):
    def row(a):
        a = a.reshape(-1, a.shape[-1])
        return jnp.pad(a, ((0, 0), (0, D - a.shape[-1])))
    rows = [row(meta), row(n1), row(nm), row(n3), row(nf), row(jnp.concatenate([gc, ga], axis=-1)), row(bf), row(cw)]
    slab = jnp.concatenate(rows, axis=0)
    return jnp.pad(slab, ((0, SMALL_ROWS - slab.shape[0]), (0, 0)))


def _unpack_small(slab, like):
    meta, n1, nm, n3, nf, gc, ga, bf, cw = like
    nmeta, mc = meta.shape
    out = [slab[:nmeta, :mc].reshape(meta.shape)]
    r = nmeta
    for a in (n1, nm, n3, nf):
        out.append(slab[r, :a.shape[-1]].reshape(a.shape))
        r += 1
    cd = gc.shape[-1]
    out.append(slab[r, :cd].reshape(gc.shape))
    out.append(slab[r, cd:cd + ga.shape[-1]].reshape(ga.shape))
    r += 1
    out.append(slab[r, :bf.shape[-1]].reshape(bf.shape))
    r += 1
    out.append(slab[r:r + 3, :cw.shape[-1]].reshape(cw.shape))
    return out


def kernel(x, meta_tokens, ffn1_norm, ffn1_w_gu, ffn1_w_down, mix_norm, w_in, conv_w, b_f, out_norm_conv, out_norm_attn, w_out, ffn2_norm, ffn2_w_gu, ffn2_w_down, final_norm, loss_target, m_meta_tokens, m_ffn1_norm, m_ffn1_w_gu, m_ffn1_w_down, m_mix_norm, m_w_in, m_conv_w, m_b_f, m_out_norm_conv, m_out_norm_attn, m_w_out, m_ffn2_norm, m_ffn2_w_gu, m_ffn2_w_down, m_final_norm, v_meta_tokens, v_ffn1_norm, v_ffn1_w_gu, v_ffn1_w_down, v_mix_norm, v_w_in, v_conv_w, v_b_f, v_out_norm_conv, v_out_norm_attn, v_w_out, v_ffn2_norm, v_ffn2_w_gu, v_ffn2_w_down, v_final_norm):
    B, S, D = x.shape
    L = S + N_META
    T = B * L
    tm = L // 3
    assert tm * 3 == L and tm % HALO == 0
    tm2 = 2 * tm
    assert T % tm2 == 0
    guc = ffn1_w_gu.shape[-1]
    ff = N_SHARD * guc // 2
    H = b_f.shape[-1]
    AD = H * HEAD_DIM
    CD = conv_w.shape[-1] * N_SHARD
    assert CD == AD and CD + AD == D and CD % LANES == 0
    n_main = 3 * CD + 3 * AD
    ins = w_in.shape[-1]

    xi, yi, ci = lax.axis_index("x"), lax.axis_index("y"), lax.axis_index("c")
    chip = 2 * xi + yi

    small_shard = jnp.zeros((2 * HALO, meta_tokens.shape[-1]), F32)
    small_shard = small_shard.at[:N_META].set(meta_tokens)
    small_shard = small_shard.at[N_META:N_META + 3, :conv_w.shape[-1]].set(conv_w[0])
    big = [ffn1_w_gu[0], ffn1_w_down[0], w_in[0], w_out[0], ffn2_w_gu[0], ffn2_w_down[0]]
    wgu1_s, wd1_s, win_s, wout_s, wgu2_s, wd2_s = [w.astype(BF16) for w in big]
    small_g, = _all_gather_shards([small_shard], name="gather_small")
    meta_f = jnp.moveaxis(small_g[:, :N_META], 0, 1).reshape(N_META, D)
    cw_f = jnp.moveaxis(small_g[:, N_META:N_META + 3, :conv_w.shape[-1]], 0, 1).reshape(3, CD)
    cw8 = jnp.pad(cw_f, ((0, 5), (0, 0)))
    bf_p = jnp.pad(b_f, ((0, 0), (0, LANES - H)))
    gid = jnp.arange(CD) // HEAD_DIM
    pmat = jnp.where(gid[:, None] == gid[None, :], 1.0 / HEAD_DIM, 0.0).astype(BF16)

    gu_shape = jax.ShapeDtypeStruct((2, T, ff), BF16)
    gu_w_spec = pl.BlockSpec((None, D, guc), lambda s, i: (s, 0, 0))
    gu_o_spec = pl.BlockSpec((None, tm2, guc), lambda s, i: (s // 2, i, s % 2))

    sid = jnp.bitwise_xor(chip, jnp.array([0, 2, 1, 3], jnp.int32)).astype(jnp.int32)
    (h0, n1), wgu1_h = _embed_norm(x, meta_f, ffn1_norm, tm=tm, name="embed_norm",
                                   comm=_gather_stage([wgu1_s], None, ici=(0, 1)))
    gu1, wgu1_h = _ffn_up(n1, wgu1_s[None], sid, None, tm=tm2, first=0, count=1, name="ffn1_up_own",
                          comm=_gather_stage([wgu1_s], wgu1_h, ici=(2,), d2d=(0, 1)))
    gu1, out = _ffn_up(n1, wgu1_h[0], sid, gu1, tm=tm2, first=1, count=2, name="ffn1_up_near",
                       comm=_join(_gather_stage((), wgu1_h, d2d=(2,)), _gather_ici([wd1_s, wout_s])))
    wgu1, down_w = out[0], out[1:]
    gu1, (wd1, wout_g) = _ffn_up(n1, wgu1, sid, gu1, tm=tm2, first=3, count=1, name="ffn1_up_far",
                                 comm=_gather_d2d(down_w))
    wd1 = wd1.reshape(ff, D)
    (h1, n2), win_h = _ffn_down(gu1, wd1, h0, mix_norm, tm=tm, name="ffn1_down", comm=_gather_ici([win_s]))
    win_g, = _run_comm(_gather_d2d(win_h), name="gather_w_in")
    wout_f = wout_g.reshape(D, D)
    win_f = jnp.moveaxis(win_g, 0, 1).reshape(D, N_SHARD * ins)
    win_main = win_f[:, :n_main]
    win_fg = jnp.pad(win_f[:, n_main:], ((0, 0), (0, LANES - H)))

    proj, fg = _mix_in(n2, win_main, win_fg, tm=tm2, nb=n_main // (3 * CD), name="mix_in")
    proj3 = proj.reshape(B, L, n_main)
    fg3 = fg.reshape(B, L, LANES)
    fc = _fcum(fg3, bf_p, ch=tm, name="forget_cumsum")
    fr = fc[:, :, :H].reshape(B, L // tm, tm, H).transpose(0, 1, 3, 2)
    (o, lse), ffn2_w = _attn_fwd(proj3, fr, tq=tm, n_heads=H, name="attn_fwd",
                                 comm=_gather_ici([wgu2_s, wd2_s]))
    (h2, ymix, n3), (wgu2, wd2) = _mix_out(
        proj3, o, cw8, out_norm_conv, out_norm_attn, wout_f, h1.reshape(B, L, D), pmat, ffn2_norm,
        tm=tm, name="mix_out", comm=_gather_d2d(ffn2_w))
    wd2 = wd2.reshape(ff, D)
    h2 = h2.reshape(T, D)
    n3 = n3.reshape(T, D)

    gu2, _ = _matmul_nn(n3, wgu2, tm=tm2, nb=N_SHARD, w_spec=gu_w_spec, out_shape=gu_shape, out_spec=gu_o_spec,
                        name="ffn2_up")
    (dh3f, dh3b, d_gf, loss_part), _ = _ffn_down_loss(gu2, wd2, h2, final_norm.reshape(1, D), loss_target,
                                                      tm=tm, name="ffn2_down_loss")

    c_arr = jnp.reshape(ci, (1,)).astype(jnp.int32)
    ks = jnp.arange(N_SHARD - 1, dtype=jnp.int32)
    idx = jnp.concatenate([jnp.stack([chip, ci]).astype(jnp.int32), ks + (ks >= chip).astype(jnp.int32)])

    def pair_sums(grads, got, names):
        return [_pair_sum(g, r, c_arr, name="pair_sum_" + nm) for g, r, nm in zip(grads, got, names)]

    def chip_sums(grads, got, landed, names):
        return [_chip_sum(g, r, l, idx, name="chip_sum_" + nm) for g, r, l, nm in zip(grads, got, landed, names)]

    def dw_up(n, dgu, name, comm=None):
        return _matmul_tn(
            n, dgu, tm=L, nb=N_SHARD, kb=D, x_spec=pl.BlockSpec((L, D), lambda s, i: (i, 0)),
            y_spec=pl.BlockSpec((None, L, guc), lambda s, i: (s // 2, i, s % 2)),
            out_shape=jax.ShapeDtypeStruct((N_SHARD, D, guc), F32),
            out_spec=pl.BlockSpec((None, D, guc), lambda s, i: (s, 0, 0)), name=name, comm=comm)

    (dgu2, d_wd2), _ = _ffn_bwd_act(dh3b, gu2, wd2, tm=tm, guc=guc, name="ffn2_bwd_act")
    (dh2, dh2b, d_g3), _ = _ffn_bwd_in(dgu2, wgu2, h2, ffn2_norm, dh3f, tm=tm, scale=1.0, name="ffn2_bwd_in")
    d_wgu2, _ = dw_up(n3, dgu2, "ffn2_dw_up")
    grads_f2 = [d_wgu2, d_wd2.reshape(N_SHARD, ff // N_SHARD, D)]
    names_f2 = ["wgu2", "wd2"]

    dh2b3 = dh2b.reshape(B, L, D)
    (d_bg, d_cv, d_o, d_gc, d_ga, d_cw), got_f2 = _mix_out_bwd(
        dh2b3, proj3, o, cw8, out_norm_conv, out_norm_attn, wout_f, pmat, tm=tm, name="mix_out_bwd",
        comm=_swap_halves(grads_f2))
    sums_f2 = pair_sums(grads_f2, got_f2, names_f2)
    d_wout, _ = _matmul_tn(
        ymix.reshape(T, D), dh2b, tm=tm2, nb=1, kb=D,
        x_spec=pl.BlockSpec((tm2, D), lambda s, i: (i, 0)), y_spec=pl.BlockSpec((tm2, D), lambda s, i: (i, 0)),
        out_shape=jax.ShapeDtypeStruct((D, D), F32), out_spec=pl.BlockSpec((D, D), lambda s, i: (0, 0)),
        name="dw_out")
    d_cc = _conv_bwd(d_cv, proj3, cw8, tm=tm, name="conv_bwd")
    (d_q, d_k, d_v, d_fk, d_fq), landed_f2 = _attn_bwd(proj3, o, d_o, lse, fr, tq=tm, n_heads=H, name="attn_bwd",
                                                       comm=_scatter_chips(sums_f2))
    halves_f2 = chip_sums(grads_f2, got_f2, landed_f2, names_f2)
    d_fg, d_bf = _fcum_bwd(d_fq, d_fk, fg3, bf_p, ch=tm, name="forget_cumsum_bwd")

    parts = [d_bg.reshape(T, CD), d_cc.reshape(T, 2 * CD), d_q.reshape(T, AD), d_k.reshape(T, AD),
             d_v.reshape(T, AD), d_fg.reshape(T, LANES)]
    (dh1, dh1b, d_gm, d_proj), g_f2 = _mix_bwd_in(parts, win_main, win_fg, h1, mix_norm, dh2, tm=tm, scale=0.5,
                                                  name="mix_bwd_in", comm=_share_halves(halves_f2))
    wide = d_proj.shape[1]
    d_win_nat, _ = _matmul_tn(
        n2, d_proj, tm=tm, nb=1, kb=D,
        x_spec=pl.BlockSpec((tm, D), lambda s, i: (i, 0)), y_spec=pl.BlockSpec((tm, wide), lambda s, i: (i, 0)),
        out_shape=jax.ShapeDtypeStruct((D, wide), F32), out_spec=pl.BlockSpec((D, wide), lambda s, i: (0, 0)),
        name="dw_in")
    d_win = jnp.moveaxis(d_win_nat[:, :N_SHARD * ins].reshape(D, N_SHARD, ins), 1, 0)
    grads_mx = [d_win, d_wout.reshape(N_SHARD, D // N_SHARD, D)]
    names_mx = ["win", "wout"]

    (dgu1, d_wd1), got_mx = _ffn_bwd_act(dh1b, gu1, wd1, tm=tm, guc=guc, name="ffn1_bwd_act",
                                         comm=_swap_halves(grads_mx))
    sums_mx = pair_sums(grads_mx, got_mx, names_mx)
    grads_d1 = [d_wd1.reshape(N_SHARD, ff // N_SHARD, D)]
    d_wgu1, out = dw_up(n1, dgu1, "ffn1_dw_up", comm=_join(_scatter_chips(sums_mx), _swap_halves(grads_d1)))
    landed_mx, got_d1 = out[:2], out[2:]
    halves_mx = chip_sums(grads_mx, got_mx, landed_mx, names_mx)
    sums_d1 = pair_sums(grads_d1, got_d1, ["wd1"])
    grads_u1 = [d_wgu1]
    (grad_x, d_meta, d_g1), out = _ffn_bwd_in_first(
        dgu1, wgu1, h0, ffn1_norm, dh1, tm=tm, batch=B, name="ffn1_bwd_in",
        comm=_join(_join(_share_halves(halves_mx), _scatter_chips(sums_d1)), _swap_halves(grads_u1)))
    g_mx, landed_d1, got_u1 = out[:2], out[2:3], out[3:]
    halves_d1 = chip_sums(grads_d1, got_d1, landed_d1, ["wd1"])
    sums_u1 = pair_sums(grads_u1, got_u1, ["wgu1"])
    out = _run_comm(_join(_share_halves(halves_d1), _scatter_chips(sums_u1)), name="scatter_ffn1")
    g_d1, landed_u1 = out[:1], out[1:]
    halves_u1 = chip_sums(grads_u1, got_u1, landed_u1, ["wgu1"])
    g_u1 = _run_comm(_share_halves(halves_u1), name="share_ffn1")
    g_big = [g_u1[0], g_d1[0], g_mx[0], g_mx[1], g_f2[0], g_f2[1]]

    loss_row = jnp.zeros((1, D), F32).at[0, 0].set(loss_part[0, 0])
    slab = _pack_small(D, d_meta, d_g1, d_gm, d_g3, d_gf, d_gc, d_ga, d_bf[:, :H], d_cw[:3])
    slab = slab.at[SMALL_ROWS - 1].set(loss_row[0])
    total = _all_reduce_small(slab, name="reduce_small")
    loss = total[SMALL_ROWS - 1, 0]
    mcols = meta_tokens.shape[-1]
    ccols = conv_w.shape[-1]
    full_like = (jnp.zeros((N_META, D)), ffn1_norm, mix_norm, ffn2_norm, final_norm.reshape(1, D), out_norm_conv,
                 out_norm_attn, b_f, jnp.zeros((1, 3, CD)))
    g_small = _unpack_small(total, full_like)
    g_small[0] = lax.dynamic_slice_in_dim(g_small[0], chip * mcols, mcols, axis=1)
    g_small[8] = lax.dynamic_slice_in_dim(g_small[8], chip * ccols, ccols, axis=2)

    def small_slab(meta, a1, am, a3, af, gc, ga, bf, cw):
        return _pack_small(D, meta, a1, am, a3, af.reshape(1, D), gc, ga, bf, cw[0])

    w_small = small_slab(meta_tokens, ffn1_norm, mix_norm, ffn2_norm, final_norm, out_norm_conv, out_norm_attn, b_f, conv_w)
    m_small = small_slab(m_meta_tokens, m_ffn1_norm, m_mix_norm, m_ffn2_norm, m_final_norm, m_out_norm_conv,
                         m_out_norm_attn, m_b_f, m_conv_w)
    v_small = small_slab(v_meta_tokens, v_ffn1_norm, v_mix_norm, v_ffn2_norm, v_final_norm, v_out_norm_conv,
                         v_out_norm_attn, v_b_f, v_conv_w)
    gs = list(g_small)
    gs[4] = gs[4].reshape(final_norm.shape)
    g_slab = small_slab(gs[0], gs[1], gs[2], gs[3], gs[4], gs[5], gs[6], gs[7], gs[8])
    local_like = (meta_tokens, ffn1_norm, mix_norm, ffn2_norm, final_norm.reshape(1, D), out_norm_conv, out_norm_attn,
                  b_f, conv_w)
    small_out = [_unpack_small(s, local_like)
                 for s in _adamw(w_small, g_slab, m_small, v_small, name="adamw_small")[1:]]
    for lst in small_out:
        lst[4] = lst[4].reshape(final_norm.shape)

    names = ["wgu1", "wd1", "win", "wout", "wgu2", "wd2"]
    w_big = big
    m_big = [m_ffn1_w_gu[0], m_ffn1_w_down[0], m_w_in[0], m_w_out[0], m_ffn2_w_gu[0], m_ffn2_w_down[0]]
    v_big = [v_ffn1_w_gu[0], v_ffn1_w_down[0], v_w_in[0], v_w_out[0], v_ffn2_w_gu[0], v_ffn2_w_down[0]]
    big_out = [_adamw(w, g, m, v, name="adamw_" + nm) for w, g, m, v, nm in zip(w_big, g_big, m_big, v_big, names)]

    def assemble(small, bigs):
        meta, a1, am, a3, af, gc, ga, bf, cw = small
        gu1_, d1_, win_, wout_, gu2_, d2_ = [b[None] for b in bigs]
        return [meta, a1, gu1_, d1_, am, win_, cw, bf, gc, ga, wout_, a3, gu2_, d2_, af]

    gs_out = list(g_small)
    gs_out[4] = gs_out[4].reshape(final_norm.shape)
    grads_out = assemble(gs_out, [b[0] for b in big_out])
    delta_out = assemble(small_out[0], [b[1] for b in big_out])
    m_out = assemble(small_out[1], [b[2] for b in big_out])
    v_out = assemble(small_out[2], [b[3] for b in big_out])
    return (loss, grad_x, *grads_out, *delta_out, *m_out, *v_out)
```

```python
import functools

import jax
import jax.numpy as jnp
from jax import lax
from jax.experimental import pallas as pl
from jax.experimental.pallas import tpu as pltpu

F32 = jnp.float32
BF16 = jnp.bfloat16

EPS = 1e-6
N_META = 16
HEAD_DIM = 64
N_SHARD = 4
N_DEV = 8
HALO = 16
LANES = 128
SMALL_ROWS = 32
VMEM_LIMIT_V7X = 56 * 1024 * 1024
NEG = -1e30
ROW_BANDS = 2

ADAM_LR = 0.001
ADAM_B1 = 0.9
ADAM_B2 = 0.999
ADAM_EPS = 1e-08
ADAM_WD = 0.01
ADAM_STEP = 10

MESH = pl.DeviceIdType.MESH
ANY = pl.BlockSpec(memory_space=pl.ANY)
NT_DIMS = (((1,), (1,)), ((), ()))
TN_DIMS = (((0,), (0,)), ((), ()))


def _params(*sem):
    return pltpu.CompilerParams(dimension_semantics=sem, vmem_limit_bytes=VMEM_LIMIT_V7X)


class _Comm:
    def __init__(self, ins, out_shapes, sems, start, finish, aliases=None):
        self.ins, self.out_shapes, self.sems = list(ins), list(out_shapes), list(sems)
        self.start, self.finish, self.aliases = start, finish, dict(aliases or {})


def _join(a, b):
    ni, no, ns = len(a.ins), len(a.out_shapes), len(a.sems)

    def start(ins, outs, sems):
        a.start(ins[:ni], outs[:no], sems[:ns])
        b.start(ins[ni:], outs[no:], sems[ns:])

    def finish(ins, outs, sems):
        a.finish(ins[:ni], outs[:no], sems[:ns])
        b.finish(ins[ni:], outs[no:], sems[ns:])

    aliases = dict(a.aliases)
    aliases.update({ni + i: no + j for i, j in b.aliases.items()})
    return _Comm(a.ins + b.ins, a.out_shapes + b.out_shapes, a.sems + b.sems, start, finish, aliases)


def _launch(body, *, name, grid, in_specs, out_specs, out_shape, args, scratch_shapes=(), comm=None, prefetch=(),
            aliases=None):
    single = not isinstance(out_shape, (list, tuple))
    out_specs = [out_specs] if single else list(out_specs)
    out_shape = [out_shape] if single else list(out_shape)
    in_specs, scratch_shapes, prefetch = list(in_specs), list(scratch_shapes), list(prefetch)
    params = _params(*(("arbitrary",) * len(grid)))
    n_pf, n_in, n_out, n_scr = len(prefetch), len(in_specs), len(out_specs), len(scratch_shapes)
    c_ins = comm.ins if comm else []
    c_shapes = comm.out_shapes if comm else []
    c_sems = comm.sems if comm else []
    c_in, c_out = len(c_ins), len(c_shapes)

    def carrier(*refs):
        p = 0
        pf = refs[p:p + n_pf]; p += n_pf
        a = refs[p:p + n_in]; p += n_in
        ci = refs[p:p + c_in]; p += c_in
        o = refs[p:p + n_out]; p += n_out
        co = refs[p:p + c_out]; p += c_out
        s = refs[p:p + n_scr]; p += n_scr
        cs = refs[p:]
        if comm:
            first = functools.reduce(lambda u, v: u & v, [pl.program_id(k) == 0 for k in range(len(grid))])

            @pl.when(first)
            def _():
                comm.start(ci, co, cs)

        body(*pf, *a, *o, *s)

        if comm:
            last = functools.reduce(lambda u, v: u & v, [pl.program_id(k) == grid[k] - 1 for k in range(len(grid))])

            @pl.when(last)
            def _():
                comm.finish(ci, co, cs)

    io_aliases = {n_pf + i: j for i, j in (aliases or {}).items()}
    if comm:
        io_aliases.update({n_pf + n_in + i: n_out + j for i, j in comm.aliases.items()})
    all_in, all_out = in_specs + [ANY] * c_in, out_specs + [ANY] * c_out
    all_scratch = scratch_shapes + [pltpu.SemaphoreType.DMA((k,)) for k in c_sems]
    if n_pf:
        spec = dict(grid_spec=pltpu.PrefetchScalarGridSpec(
            num_scalar_prefetch=n_pf, grid=grid, in_specs=all_in, out_specs=all_out, scratch_shapes=all_scratch))
    else:
        spec = dict(grid=grid, in_specs=all_in, out_specs=all_out, scratch_shapes=all_scratch)
    res = pl.pallas_call(carrier, name=name, out_shape=out_shape + c_shapes, input_output_aliases=io_aliases,
                         compiler_params=params, **spec)(*prefetch, *args, *c_ins)
    main = list(res[:n_out])
    return (main[0] if single else main), (list(res[n_out:]) if comm else None)


def _run_comm(comm, *, name):
    c_in, c_out = len(comm.ins), len(comm.out_shapes)

    def body(*refs):
        ci, co, cs = refs[:c_in], refs[c_in:c_in + c_out], refs[c_in + c_out:]
        comm.start(ci, co, cs)
        comm.finish(ci, co, cs)

    return list(pl.pallas_call(
        body, name=name, in_specs=[ANY] * c_in, out_specs=[ANY] * c_out, out_shape=comm.out_shapes,
        scratch_shapes=[pltpu.SemaphoreType.DMA((k,)) for k in comm.sems],
        input_output_aliases=comm.aliases)(*comm.ins))


def _chunks(width, step=512):
    out, c0 = [], 0
    while c0 < width:
        cw = min(step, width - c0)
        out.append((c0, cw))
        c0 += cw
    return out


def _split2(v):
    hi = v.astype(BF16)
    lo = (v - hi.astype(F32)).astype(BF16)
    return hi, lo


def _split3(v):
    hi = v.astype(BF16)
    r = v - hi.astype(F32)
    mid = r.astype(BF16)
    lo = (r - mid.astype(F32)).astype(BF16)
    return hi, mid, lo


def _dot(a, b):
    return jnp.dot(a, b, preferred_element_type=F32)


def _dot_nt(a, b):
    return lax.dot_general(a, b, NT_DIMS, preferred_element_type=F32)


def _dot_tn(a, b):
    return lax.dot_general(a, b, TN_DIMS, preferred_element_type=F32)


def _silu_mul(g, u):
    return g * jax.nn.sigmoid(g) * u


def _rms_bwd(dn, h, gain, dres):
    r = lax.rsqrt(jnp.mean(h * h, axis=-1, keepdims=True) + EPS)
    y = h * r
    dgain = jnp.sum(dn * y, axis=0, keepdims=True)
    dy = dn * gain
    dh = dres + r * (dy - y * jnp.mean(dy * y, axis=-1, keepdims=True))
    return dh, dgain


def _group_mean(v, p):
    hi, lo = _split2(v)
    return _dot(hi, p) + _dot(lo, p)


def _row_of(a, k):
    rows = lax.broadcasted_iota(jnp.int32, a.shape, 0)
    return jnp.sum(jnp.where(rows == k, a, 0.0), axis=0, keepdims=True)


def _causal_conv(u, prev, w):
    rows = lax.broadcasted_iota(jnp.int32, u.shape, 0)
    p1 = _row_of(prev, HALO - 1)
    p2 = _row_of(prev, HALO - 2)
    u1 = jnp.where(rows == 0, p1, pltpu.roll(u, 1, 0))
    u2 = jnp.where(rows == 0, p2, jnp.where(rows == 1, p1, pltpu.roll(u, 2, 0)))
    return w[2:3, :] * u + w[1:2, :] * u1 + w[0:1, :] * u2, u1, u2


def _rms(x, gain):
    return (x * lax.rsqrt(jnp.mean(x * x, axis=-1, keepdims=True) + EPS) * gain).astype(BF16)


def _embed_norm(x, meta, g, *, tm, name, comm=None):
    B, S, D = x.shape
    L = S + N_META
    per_seq = L // tm
    nt = B * per_seq
    body_rows = tm - N_META

    def body(meta_ref, g_ref, x_hbm, h_ref, n_ref, buf, sems):
        i = pl.program_id(0)

        def fetch(k, fn):
            slot, b, t = k % 2, k // per_seq, k % per_seq

            @pl.when(t == 0)
            def _():
                fn(pltpu.make_async_copy(x_hbm.at[b, pl.ds(0, body_rows)],
                                         buf.at[slot, pl.ds(N_META, body_rows)], sems.at[slot]))

            @pl.when(t != 0)
            def _():
                fn(pltpu.make_async_copy(x_hbm.at[b, pl.ds(pl.multiple_of(t * tm - N_META, 8), tm)],
                                         buf.at[slot], sems.at[slot]))

        @pl.when(i == 0)
        def _():
            fetch(i, lambda cp: cp.start())

        @pl.when(i + 1 < nt)
        def _():
            fetch(i + 1, lambda cp: cp.start())

        fetch(i, lambda cp: cp.wait())
        slot = i % 2

        @pl.when(i % per_seq == 0)
        def _():
            buf[slot, 0:N_META, :] = meta_ref[...]

        hv = buf[slot]
        h_ref[...] = hv
        n_ref[...] = _rms(hv, g_ref[...])

    row = pl.BlockSpec((tm, D), lambda i: (i, 0))
    return _launch(
        body, name=name, grid=(nt,),
        in_specs=[pl.BlockSpec((N_META, D), lambda i: (0, 0)), pl.BlockSpec((1, D), lambda i: (0, 0)), ANY],
        out_specs=[row, row],
        out_shape=[jax.ShapeDtypeStruct((B * L, D), F32), jax.ShapeDtypeStruct((B * L, D), BF16)],
        scratch_shapes=[pltpu.VMEM((2, tm, D), F32), pltpu.SemaphoreType.DMA((2,))],
        args=(meta, g, x), comm=comm)


def _ffn_up(n, wgu, sid, gu_prev, *, tm, first, count, name, comm=None):
    T, D = n.shape
    ns, _, guc = wgu.shape
    ff = N_SHARD * guc // 2

    def body(sid_ref, x_ref, w_ref, *rest):
        rest[-1][...] = _dot(x_ref[...], w_ref[...]).astype(BF16)

    where = lambda s, sid: sid[first + s]
    w_at = (lambda s, sid: 0) if ns == 1 else where
    return _launch(
        body, name=name, grid=(count, T // tm), prefetch=(sid,),
        in_specs=[pl.BlockSpec((tm, D), lambda s, i, sid: (i, 0)),
                  pl.BlockSpec((None, D, guc), lambda s, i, sid: (w_at(s, sid), 0, 0))]
                 + ([] if gu_prev is None else [ANY]),
        out_specs=pl.BlockSpec((None, tm, guc), lambda s, i, sid: (where(s, sid) // 2, i, where(s, sid) % 2)),
        out_shape=jax.ShapeDtypeStruct((2, T, ff), BF16),
        args=(n, wgu) + (() if gu_prev is None else (gu_prev,)),
        aliases=None if gu_prev is None else {2: 0}, comm=comm)


def _matmul_nn(x, w, *, tm, nb, w_spec, out_shape, out_spec, name, comm=None):
    T, K = x.shape

    def body(x_ref, w_ref, o_ref):
        o_ref[...] = _dot(x_ref[...], w_ref[...]).astype(o_ref.dtype)

    return _launch(
        body, name=name, grid=(nb, T // tm),
        in_specs=[pl.BlockSpec((tm, K), lambda s, i: (i, 0)), w_spec],
        out_specs=out_spec, out_shape=out_shape, args=(x, w), comm=comm)


def _mix_in(n, w_main, w_fg, *, tm, nb, name):
    T, D = n.shape
    n_main = w_main.shape[1]
    bw = n_main // nb
    W = w_fg.shape[1]

    def body(x_ref, w_ref, wf_ref, o_ref, fg_ref):
        x = x_ref[...]
        o_ref[...] = _dot(x, w_ref[...]).astype(BF16)

        @pl.when(pl.program_id(1) == 0)
        def _():
            fg_ref[...] = _dot(x, wf_ref[...])

    res, _ = _launch(
        body, name=name, grid=(T // tm, nb),
        in_specs=[pl.BlockSpec((tm, D), lambda i, s: (i, 0)), pl.BlockSpec((D, bw), lambda i, s: (0, s)),
                  pl.BlockSpec((D, W), lambda i, s: (0, 0))],
        out_specs=[pl.BlockSpec((tm, bw), lambda i, s: (i, s)), pl.BlockSpec((tm, W), lambda i, s: (i, 0))],
        out_shape=[jax.ShapeDtypeStruct((T, n_main), BF16), jax.ShapeDtypeStruct((T, W), F32)],
        args=(n, w_main, w_fg))
    return res


def _down_in_bands(g_ref, u_ref, wd_v, edges, chunks, finish):
    def down(rows):
        def act(c0, cw):
            return _silu_mul(g_ref[rows, c0:c0 + cw].astype(F32), u_ref[rows, c0:c0 + cw].astype(F32)).astype(BF16)

        acc = None
        nxt = act(*chunks[0])
        for k, (c0, cw) in enumerate(chunks):
            a = nxt
            if k + 1 < len(chunks):
                nxt = act(*chunks[k + 1])
            d = _dot(a, wd_v[c0:c0 + cw, :])
            acc = d if acc is None else acc + d
        return acc

    bands = [slice(r0, r1) for r0, r1 in zip(edges[:-1], edges[1:])]
    nxt = down(bands[0])
    for b, rows in enumerate(bands):
        acc = nxt
        if b + 1 < len(bands):
            nxt = down(bands[b + 1])
        finish(rows, acc)


def _ffn_down(gu, wd, h, next_gain, *, tm, name, comm=None):
    _, T, ff = gu.shape
    D = h.shape[1]
    chunks = _chunks(ff)

    def body(g_ref, u_ref, wd_hbm, h_ref, ng_ref, o_ref, n_ref, wd_v, sem):
        @pl.when(pl.program_id(0) == 0)
        def _():
            cp = pltpu.make_async_copy(wd_hbm, wd_v, sem)
            cp.start()
            cp.wait()

        def finish(rows, acc):
            out = h_ref[rows, :] + 0.5 * acc
            o_ref[rows, :] = out
            n_ref[rows, :] = _rms(out, ng_ref[...])

        _down_in_bands(g_ref, u_ref, wd_v, _band_edges(tm), chunks, finish)

    return _launch(
        body, name=name, grid=(T // tm,),
        in_specs=[pl.BlockSpec((None, tm, ff), lambda i: (0, i, 0)),
                  pl.BlockSpec((None, tm, ff), lambda i: (1, i, 0)),
                  ANY,
                  pl.BlockSpec((tm, D), lambda i: (i, 0)),
                  pl.BlockSpec((1, D), lambda i: (0, 0))],
        out_specs=[pl.BlockSpec((tm, D), lambda i: (i, 0)), pl.BlockSpec((tm, D), lambda i: (i, 0))],
        out_shape=[jax.ShapeDtypeStruct((T, D), F32), jax.ShapeDtypeStruct((T, D), BF16)],
        scratch_shapes=[pltpu.VMEM((ff, D), BF16), pltpu.SemaphoreType.DMA],
        args=(gu, gu, wd, h, next_gain), comm=comm)


def _ffn_down_loss(gu, wd, h, gf, tgt, *, tm, name, comm=None):
    _, T, ff = gu.shape
    D = h.shape[1]
    B, S, _ = tgt.shape
    per_seq = (S + N_META) // tm
    body_rows = tm - N_META
    chunks = _chunks(ff)

    def body(g_ref, u_ref, wd_hbm, h_ref, gf_ref, tgt_hbm, dh_ref, dhb_ref, dg_ref, loss_ref, wd_v, tg_v, sem, tsem):
        i = pl.program_id(0)
        b, t = i // per_seq, i % per_seq

        @pl.when(i == 0)
        def _():
            cp = pltpu.make_async_copy(wd_hbm, wd_v, sem)
            cp.start()
            cp.wait()
            dg_ref[...] = jnp.zeros_like(dg_ref)
            loss_ref[...] = jnp.zeros_like(loss_ref)
            tg_v[0:N_META, :] = jnp.zeros((N_META, D), F32)

        def fetch(fn):
            @pl.when(t == 0)
            def _():
                fn(pltpu.make_async_copy(tgt_hbm.at[b, pl.ds(0, body_rows)], tg_v.at[pl.ds(N_META, body_rows)], tsem))

            @pl.when(t != 0)
            def _():
                fn(pltpu.make_async_copy(tgt_hbm.at[b, pl.ds(pl.multiple_of(t * tm - N_META, 8), tm)], tg_v, tsem))

        fetch(lambda cp: cp.start())

        def finish(rows, acc):
            if rows.start == 0:
                fetch(lambda cp: cp.wait())
            x = h_ref[rows, :] + 0.5 * acc
            gain = gf_ref[...]
            r = lax.rsqrt(jnp.mean(x * x, axis=-1, keepdims=True) + EPS)
            y = x * r
            pos = t * tm + rows.start + lax.broadcasted_iota(jnp.int32, (rows.stop - rows.start, 1), 0)
            err = jnp.where(pos >= N_META, y * gain - tg_v[rows, :], 0.0)
            loss_ref[...] += 0.5 * jnp.sum(jnp.mean(err * err, axis=-1, keepdims=True))
            dout = err / D
            dg_ref[...] += jnp.sum(dout * y, axis=0, keepdims=True)
            dy = dout * gain
            dh = r * (dy - y * jnp.mean(dy * y, axis=-1, keepdims=True))
            dh_ref[rows, :] = dh
            dhb_ref[rows, :] = (0.5 * dh).astype(BF16)

        _down_in_bands(g_ref, u_ref, wd_v, _band_edges(tm), chunks, finish)

    row = pl.BlockSpec((tm, D), lambda i: (i, 0))
    const = lambda i: (0, 0)
    return _launch(
        body, name=name, grid=(T // tm,),
        in_specs=[pl.BlockSpec((None, tm, ff), lambda i: (0, i, 0)),
                  pl.BlockSpec((None, tm, ff), lambda i: (1, i, 0)),
                  ANY, row, pl.BlockSpec((1, D), const), ANY],
        out_specs=[row, row, pl.BlockSpec((1, D), const), pl.BlockSpec((1, LANES), const)],
        out_shape=[jax.ShapeDtypeStruct((T, D), F32), jax.ShapeDtypeStruct((T, D), BF16),
                   jax.ShapeDtypeStruct((1, D), F32), jax.ShapeDtypeStruct((1, LANES), F32)],
        scratch_shapes=[pltpu.VMEM((ff, D), BF16), pltpu.VMEM((tm, D), F32), pltpu.SemaphoreType.DMA,
                        pltpu.SemaphoreType.DMA],
        args=(gu, gu, wd, h, gf, tgt), comm=comm)


def _ffn_bwd_act(df, gu, wd, *, tm, guc, name, comm=None):
    _, T, ff = gu.shape
    D = df.shape[1]
    nj = ff // guc
    chunks = _chunks(guc, 768)

    def body(df_ref, g_ref, u_ref, wd_ref, o_ref, dwd_ref):
        @pl.when(pl.program_id(1) == 0)
        def _():
            dwd_ref[...] = jnp.zeros_like(dwd_ref)

        dfv = df_ref[...]
        nxt = _dot_nt(dfv, wd_ref[chunks[0][0]:chunks[0][0] + chunks[0][1], :])
        for k, (c0, cw) in enumerate(chunks):
            da = nxt
            if k + 1 < len(chunks):
                n0, nw = chunks[k + 1]
                nxt = _dot_nt(dfv, wd_ref[n0:n0 + nw, :])
            g = g_ref[:, c0:c0 + cw].astype(F32)
            u = u_ref[:, c0:c0 + cw].astype(F32)
            sg = jax.nn.sigmoid(g)
            silu = g * sg
            o_ref[0, :, c0:c0 + cw] = (da * u * (sg * (1.0 + g * (1.0 - sg)))).astype(BF16)
            o_ref[1, :, c0:c0 + cw] = (da * silu).astype(BF16)
            dwd_ref[c0:c0 + cw, :] += _dot_tn((silu * u).astype(BF16), dfv)

    return _launch(
        body, name=name, grid=(nj, T // tm),
        in_specs=[pl.BlockSpec((tm, D), lambda j, i: (i, 0)),
                  pl.BlockSpec((None, tm, guc), lambda j, i: (0, i, j)),
                  pl.BlockSpec((None, tm, guc), lambda j, i: (1, i, j)),
                  pl.BlockSpec((guc, D), lambda j, i: (j, 0))],
        out_specs=[pl.BlockSpec((2, tm, guc), lambda j, i: (0, i, j)), pl.BlockSpec((guc, D), lambda j, i: (j, 0))],
        out_shape=[jax.ShapeDtypeStruct((2, T, ff), BF16), jax.ShapeDtypeStruct((ff, D), F32)],
        args=(df, gu, gu, wd), comm=comm)


def _ffn_bwd_in(dgu, wgu, h, g, dres, *, tm, scale, name, comm=None):
    _, T, ff = dgu.shape
    ns, D, guc = wgu.shape
    nj = ff // guc
    edges = _band_edges(tm)

    def body(dgu_ref, w_hbm, h_ref, g_ref, dres_ref, dh_ref, dhb_ref, dg_ref, w_v, acc, sem):
        i, j = pl.program_id(0), pl.program_id(1)

        @pl.when((i == 0) & (j == 0))
        def _():
            cp = pltpu.make_async_copy(w_hbm, w_v, sem)
            cp.start()
            cp.wait()
            dg_ref[...] = jnp.zeros_like(dg_ref)

        def dots(rows):
            return _dot_nt(dgu_ref[0, rows, :], w_v[j]) + _dot_nt(dgu_ref[1, rows, :], w_v[nj + j])

        @pl.when(j < nj - 1)
        def _():
            part = dots(slice(None))

            @pl.when(j == 0)
            def _():
                acc[...] = part

            @pl.when(j > 0)
            def _():
                acc[...] += part

        @pl.when(j == nj - 1)
        def _():
            bands = [slice(r0, r1) for r0, r1 in zip(edges[:-1], edges[1:])]
            nxt = dots(bands[0])
            for b, rows in enumerate(bands):
                dn = nxt if nj == 1 else acc[rows, :] + nxt
                if b + 1 < len(bands):
                    nxt = dots(bands[b + 1])
                dh, dgain = _rms_bwd(dn, h_ref[rows, :], g_ref[...], dres_ref[rows, :])
                dh_ref[rows, :] = dh
                dhb_ref[rows, :] = (scale * dh).astype(BF16)
                dg_ref[...] += dgain

    return _launch(
        body, name=name, grid=(T // tm, nj),
        in_specs=[pl.BlockSpec((2, tm, guc), lambda i, j: (0, i, j)),
                  ANY,
                  pl.BlockSpec((tm, D), lambda i, j: (i, 0)),
                  pl.BlockSpec((1, D), lambda i, j: (0, 0)),
                  pl.BlockSpec((tm, D), lambda i, j: (i, 0))],
        out_specs=[pl.BlockSpec((tm, D), lambda i, j: (i, 0)),
                   pl.BlockSpec((tm, D), lambda i, j: (i, 0)),
                   pl.BlockSpec((1, D), lambda i, j: (0, 0))],
        out_shape=[jax.ShapeDtypeStruct((T, D), F32), jax.ShapeDtypeStruct((T, D), BF16),
                   jax.ShapeDtypeStruct((1, D), F32)],
        scratch_shapes=[pltpu.VMEM((ns, D, guc), BF16), pltpu.VMEM((tm, D), F32), pltpu.SemaphoreType.DMA],
        args=(dgu, wgu, h, g, dres), comm=comm)


def _ffn_bwd_in_first(dgu, wgu, h, g, dres, *, tm, batch, name, comm=None):
    _, T, ff = dgu.shape
    ns, D, guc = wgu.shape
    nj = ff // guc
    nt = T // tm
    L = T // batch
    per_seq = L // tm
    body_rows = tm - N_META
    edges = _band_edges(tm)

    def body(dgu_ref, w_hbm, h_ref, g_ref, dres_ref, dx_hbm, dmeta_ref, dg_ref, w_v, acc, dh_v, sem, osem):
        i, j = pl.program_id(0), pl.program_id(1)

        @pl.when((i == 0) & (j == 0))
        def _():
            cp = pltpu.make_async_copy(w_hbm, w_v, sem)
            cp.start()
            cp.wait()
            dg_ref[...] = jnp.zeros_like(dg_ref)
            dmeta_ref[...] = jnp.zeros_like(dmeta_ref)

        def dots(rows):
            return _dot_nt(dgu_ref[0, rows, :], w_v[j]) + _dot_nt(dgu_ref[1, rows, :], w_v[nj + j])

        @pl.when(j < nj - 1)
        def _():
            part = dots(slice(None))

            @pl.when(j == 0)
            def _():
                acc[...] = part

            @pl.when(j > 0)
            def _():
                acc[...] += part

        def head_copy(b):
            return pltpu.make_async_copy(dh_v.at[pl.ds(N_META, body_rows)], dx_hbm.at[b, pl.ds(0, body_rows)], osem)

        def tail_copy(b, t):
            return pltpu.make_async_copy(dh_v, dx_hbm.at[b, pl.ds(pl.multiple_of(t * tm - N_META, 8), tm)], osem)

        def on_tile(k, head_fn, tail_fn):
            @pl.when(k % per_seq == 0)
            def _():
                head_fn(head_copy(k // per_seq))

            @pl.when(k % per_seq != 0)
            def _():
                tail_fn(tail_copy(k // per_seq, k % per_seq))

        @pl.when(j == nj - 1)
        def _():
            @pl.when(i > 0)
            def _():
                on_tile(i - 1, lambda cp: cp.wait(), lambda cp: cp.wait())

            bands = [slice(r0, r1) for r0, r1 in zip(edges[:-1], edges[1:])]
            nxt = dots(bands[0])
            for b, rows in enumerate(bands):
                dn = nxt if nj == 1 else acc[rows, :] + nxt
                if b + 1 < len(bands):
                    nxt = dots(bands[b + 1])
                dh, dgain = _rms_bwd(dn, h_ref[rows, :], g_ref[...], dres_ref[rows, :])
                dg_ref[...] += dgain
                dh_v[rows, :] = dh
                if b == 0:
                    @pl.when(i % per_seq == 0)
                    def _():
                        dmeta_ref[...] += dh[0:N_META, :]

            on_tile(i, lambda cp: cp.start(), lambda cp: cp.start())

            @pl.when(i == nt - 1)
            def _():
                on_tile(i, lambda cp: cp.wait(), lambda cp: cp.wait())

    return _launch(
        body, name=name, grid=(nt, nj),
        in_specs=[pl.BlockSpec((2, tm, guc), lambda i, j: (0, i, j)),
                  ANY,
                  pl.BlockSpec((tm, D), lambda i, j: (i, 0)),
                  pl.BlockSpec((1, D), lambda i, j: (0, 0)),
                  pl.BlockSpec((tm, D), lambda i, j: (i, 0))],
        out_specs=[ANY, pl.BlockSpec((N_META, D), lambda i, j: (0, 0)), pl.BlockSpec((1, D), lambda i, j: (0, 0))],
        out_shape=[jax.ShapeDtypeStruct((batch, L - N_META, D), F32), jax.ShapeDtypeStruct((N_META, D), F32),
                   jax.ShapeDtypeStruct((1, D), F32)],
        scratch_shapes=[pltpu.VMEM((ns, D, guc), BF16), pltpu.VMEM((tm, D), F32), pltpu.VMEM((tm, D), F32),
                        pltpu.SemaphoreType.DMA, pltpu.SemaphoreType.DMA],
        args=(dgu, wgu, h, g, dres), comm=comm)


def _mix_bwd_in(parts, w_main, w_fg, h, g, dres, *, tm, scale, name, comm=None):
    T, D = h.shape
    widths = [p.shape[1] for p in parts]
    offs = [sum(widths[:k]) for k in range(len(widths))]
    npart = len(parts)
    wide = sum(widths)
    edges = _band_edges(tm)

    def body(*refs):
        p_refs = refs[:npart]
        wm_ref, wf_ref, h_ref, g_ref, dres_ref, dh_ref, dhb_ref, dg_ref, all_ref = refs[npart:]

        @pl.when(pl.program_id(0) == 0)
        def _():
            dg_ref[...] = jnp.zeros_like(dg_ref)

        for p_ref, off, wd_ in zip(p_refs, offs, widths):
            for c0, cw in _chunks(wd_):
                all_ref[:, off + c0:off + c0 + cw] = p_ref[:, c0:c0 + cw].astype(BF16)
        n_main = offs[-1]

        def dots(rows):
            return _dot_nt(all_ref[rows, :n_main], wm_ref[...]) + _dot_nt(all_ref[rows, n_main:], wf_ref[...])

        bands = [slice(r0, r1) for r0, r1 in zip(edges[:-1], edges[1:])]
        nxt = dots(bands[0])
        for b, rows in enumerate(bands):
            dn = nxt
            if b + 1 < len(bands):
                nxt = dots(bands[b + 1])
            dh, dgain = _rms_bwd(dn, h_ref[rows, :], g_ref[...], dres_ref[rows, :])
            dh_ref[rows, :] = dh
            dhb_ref[rows, :] = (scale * dh).astype(BF16)
            dg_ref[...] += dgain

    row = lambda i: (i, 0)
    const = lambda i: (0, 0)
    return _launch(
        body, name=name, grid=(T // tm,),
        in_specs=[pl.BlockSpec((tm, p.shape[1]), row) for p in parts]
                 + [pl.BlockSpec(w_main.shape, const), pl.BlockSpec(w_fg.shape, const),
                    pl.BlockSpec((tm, D), row), pl.BlockSpec((1, D), const), pl.BlockSpec((tm, D), row)],
        out_specs=[pl.BlockSpec((tm, D), row), pl.BlockSpec((tm, D), row), pl.BlockSpec((1, D), const),
                   pl.BlockSpec((tm, wide), row)],
        out_shape=[jax.ShapeDtypeStruct((T, D), F32), jax.ShapeDtypeStruct((T, D), BF16),
                   jax.ShapeDtypeStruct((1, D), F32), jax.ShapeDtypeStruct((T, wide), BF16)],
        args=(*parts, w_main, w_fg, h, g, dres), comm=comm)


def _matmul_tn(x, y, *, tm, nb, x_spec, y_spec, out_shape, out_spec, kb, name, comm=None):
    T = y.shape[-2]
    chunks = _chunks(kb)

    def body(x_ref, y_ref, o_ref):
        @pl.when(pl.program_id(1) == 0)
        def _():
            o_ref[...] = jnp.zeros_like(o_ref)

        yv = y_ref[...].astype(BF16)
        nxt = _dot_tn(x_ref[:, chunks[0][0]:chunks[0][0] + chunks[0][1]], yv)
        for k, (c0, cw) in enumerate(chunks):
            cur = nxt
            if k + 1 < len(chunks):
                n0, nw = chunks[k + 1]
                nxt = _dot_tn(x_ref[:, n0:n0 + nw], yv)
            o_ref[c0:c0 + cw, :] += cur

    return _launch(
        body, name=name, grid=(nb, T // tm),
        in_specs=[x_spec, y_spec], out_specs=out_spec, out_shape=out_shape, args=(x, y), comm=comm)


def _tri(n, lower):
    r = lax.broadcasted_iota(jnp.int32, (n, n), 0)
    c = lax.broadcasted_iota(jnp.int32, (n, n), 1)
    return jnp.where((r >= c) if lower else (r <= c), 1.0, 0.0).astype(BF16)


def _tri_dot(tri, v):
    hi, mid, lo = _split3(v)
    return _dot(tri, hi) + _dot(tri, mid) + _dot(tri, lo)


def _fcum(fg, bf, *, ch, name):
    B, L, W = fg.shape
    nch = L // ch

    def body(fg_ref, bf_ref, f_ref):
        tri = _tri(ch, True)
        carry = jnp.zeros((1, W), F32)
        for c in range(nch):
            x = fg_ref[c * ch:(c + 1) * ch, :] + bf_ref[...]
            lf = jnp.minimum(x, 0.0) - jnp.log(1.0 + jnp.exp(-jnp.abs(x)))
            f_ref[c * ch:(c + 1) * ch, :] = _tri_dot(tri, lf) + carry
            carry = carry + jnp.sum(lf, axis=0, keepdims=True)

    return pl.pallas_call(
        body, name=name, grid=(B,),
        in_specs=[pl.BlockSpec((None, L, W), lambda b: (b, 0, 0)), pl.BlockSpec((1, W), lambda b: (0, 0))],
        out_specs=pl.BlockSpec((None, L, W), lambda b: (b, 0, 0)),
        out_shape=jax.ShapeDtypeStruct((B, L, W), F32),
        compiler_params=_params("arbitrary"),
    )(fg, bf)


def _fcum_bwd(dF_rows, dF_cols, fg, bf, *, ch, name):
    B, L, W = fg.shape
    nch = L // ch

    def body(dfr_ref, dfc_ref, fg_ref, bf_ref, dfg_ref, db_ref):
        @pl.when(pl.program_id(0) == 0)
        def _():
            db_ref[...] = jnp.zeros_like(db_ref)

        tri = _tri(ch, False)
        carry = jnp.zeros((1, W), F32)
        dbs = jnp.zeros((1, W), F32)
        for c in reversed(range(nch)):
            d = dfr_ref[c * ch:(c + 1) * ch, :] - dfc_ref[c * ch:(c + 1) * ch, :]
            dlf = _tri_dot(tri, d) + carry
            carry = carry + jnp.sum(d, axis=0, keepdims=True)
            x = fg_ref[c * ch:(c + 1) * ch, :] + bf_ref[...]
            dfg = dlf * jax.nn.sigmoid(-x)
            dfg_ref[c * ch:(c + 1) * ch, :] = dfg.astype(BF16)
            dbs = dbs + jnp.sum(dfg, axis=0, keepdims=True)
        db_ref[...] += dbs

    blk = pl.BlockSpec((None, L, W), lambda b: (b, 0, 0))
    return pl.pallas_call(
        body, name=name, grid=(B,),
        in_specs=[blk, blk, blk, pl.BlockSpec((1, W), lambda b: (0, 0))],
        out_specs=[blk, pl.BlockSpec((1, W), lambda b: (0, 0))],
        out_shape=[jax.ShapeDtypeStruct((B, L, W), BF16), jax.ShapeDtypeStruct((1, W), F32)],
        compiler_params=_params("arbitrary"),
    )(dF_rows, dF_cols, fg, bf)


def _band_edges(tq):
    return sorted({min(tq, (k * tq // ROW_BANDS + HALO - 1) // HALO * HALO) for k in range(ROW_BANDS + 1)})


def _pair(h):
    return slice((h // 2) * 2 * HEAD_DIM, (h // 2 + 1) * 2 * HEAD_DIM)


def _own_lanes(a, h):
    low = lax.broadcasted_iota(jnp.int32, a.shape, 1) < HEAD_DIM
    return jnp.where(low if h % 2 == 0 else jnp.logical_not(low), a, jnp.zeros_like(a))


def _sum_lane(h):
    return HEAD_DIM if h % 2 == 0 else 0


def _own_lanes_and_ones(a, h):
    lane = lax.broadcasted_iota(jnp.int32, a.shape, 1)
    low = lane < HEAD_DIM
    return jnp.where(low if h % 2 == 0 else jnp.logical_not(low), a,
                     jnp.where(lane == _sum_lane(h), jnp.ones_like(a), jnp.zeros_like(a)))


def _attn_fwd(proj, fr, *, tq, n_heads, name, comm=None):
    B, L, _ = proj.shape
    AD = n_heads * HEAD_DIM
    nq = L // tq
    W = LANES
    scale = HEAD_DIM ** -0.5
    edges = _band_edges(tq)

    v_ones, sum_lane = _own_lanes_and_ones, _sum_lane

    def body(q_ref, k_ref, v_ref, fr_ref, o_ref, lse_ref, m_s, acc_s):
        qi, ki = pl.program_id(1), pl.program_id(2)

        @pl.when(ki == 0)
        def _():
            m_s[...] = jnp.full_like(m_s, NEG)
            acc_s[...] = jnp.zeros_like(acc_s)

        def tile(diagonal):
            lane = lax.broadcasted_iota(jnp.int32, (tq, W), 1)
            m_all = m_s[...]
            m_out = m_all
            bands = [(r0, r1, r1 if diagonal else tq) for r0, r1 in zip(edges[:-1], edges[1:])]
            if diagonal:
                masks = {r0: (lax.broadcasted_iota(jnp.int32, (r1 - r0, c1), 1)
                              <= r0 + lax.broadcasted_iota(jnp.int32, (r1 - r0, c1), 0)) for r0, r1, c1 in bands}

            def scores(h, band):
                r0, r1, c1 = band
                sl = slice(h * HEAD_DIM, (h + 1) * HEAD_DIM)
                return _dot_nt(q_ref[r0:r1, sl] * scale, k_ref[0:c1, sl])

            work = [(h, band) for h in range(n_heads) for band in bands]
            nxt = scores(*work[0])
            for w, (h, band) in enumerate(work):
                r0, r1, c1 = band
                sl = slice(h * HEAD_DIM, (h + 1) * HEAD_DIM)
                s = nxt - fr_ref[h:h + 1, 0:c1]
                if w + 1 < len(work):
                    nxt = scores(*work[w + 1])
                if diagonal:
                    s = jnp.where(masks[r0], s, NEG)
                m_old = m_all[r0:r1, h:h + 1]
                m_new = jnp.maximum(m_old, jnp.max(s, axis=1, keepdims=True))
                alpha = jnp.exp(m_old - m_new)
                p = jnp.exp(s - m_new)
                own = slice(h * 2 * HEAD_DIM, (h + 1) * 2 * HEAD_DIM)
                acc_s[r0:r1, own] = alpha * acc_s[r0:r1, own] + _dot(p.astype(BF16), v_ones(v_ref[0:c1, _pair(h)], h))
                if r0 == 0:
                    m_parts = []
                m_parts.append(m_new)
                if r1 == tq:
                    m_out = jnp.where(lane == h, jnp.concatenate(m_parts, axis=0), m_out)
            m_s[...] = m_out

        @pl.when(ki < qi)
        def _():
            tile(False)

        @pl.when(ki == qi)
        def _():
            tile(True)
            lane = lax.broadcasted_iota(jnp.int32, (tq, W), 1)
            low = lax.broadcasted_iota(jnp.int32, (tq, 2 * HEAD_DIM), 1) < HEAD_DIM
            l_all = jnp.ones((tq, W), F32)
            for h in range(0, n_heads, 2):
                even = acc_s[:, h * 2 * HEAD_DIM:(h + 1) * 2 * HEAD_DIM]
                odd = acc_s[:, (h + 1) * 2 * HEAD_DIM:(h + 2) * 2 * HEAD_DIM]
                l_even = even[:, sum_lane(h):sum_lane(h) + 1]
                l_odd = odd[:, sum_lane(h + 1):sum_lane(h + 1) + 1]
                o_ref[:, _pair(h)] = jnp.where(low, even / l_even, odd / l_odd)
                l_all = jnp.where(lane == h, l_even, jnp.where(lane == h + 1, l_odd, l_all))
            lse_ref[...] = jnp.where(lane < n_heads, m_s[...] + jnp.log(l_all), 0.0)

    kv = lambda b, qi, ki: jnp.minimum(ki, qi)
    return _launch(
        body, name=name, grid=(B, nq, nq), args=(proj, proj, proj, fr), comm=comm,
        in_specs=[pl.BlockSpec((None, tq, AD), lambda b, qi, ki: (b, qi, 3)),
                  pl.BlockSpec((None, tq, AD), lambda b, qi, ki: (b, kv(b, qi, ki), 4)),
                  pl.BlockSpec((None, tq, AD), lambda b, qi, ki: (b, kv(b, qi, ki), 5)),
                  pl.BlockSpec((None, None, n_heads, tq), lambda b, qi, ki: (b, kv(b, qi, ki), 0, 0))],
        out_specs=[pl.BlockSpec((None, tq, AD), lambda b, qi, ki: (b, qi, 0)),
                   pl.BlockSpec((None, tq, W), lambda b, qi, ki: (b, qi, 0))],
        out_shape=[jax.ShapeDtypeStruct((B, L, AD), F32), jax.ShapeDtypeStruct((B, L, W), F32)],
        scratch_shapes=[pltpu.VMEM((tq, W), F32), pltpu.VMEM((tq, n_heads * 2 * HEAD_DIM), F32)])


def _attn_bwd(proj, o, do, lse, fr, *, tq, n_heads, name, comm=None):
    B, L, _ = proj.shape
    AD = n_heads * HEAD_DIM
    nq = L // tq
    W = LANES
    HW = 2 * HEAD_DIM
    scale = HEAD_DIM ** -0.5
    edges = _band_edges(tq)

    def body(q_ref, k_ref, v_ref, o_ref, do_ref, lse_ref, fr_ref,
             dq_ref, dk_ref, dv_ref, dfk_ref, dfq_ref, dq_s, dk_s, dv_s):
        kj, qi = pl.program_id(1), pl.program_id(2)

        @pl.when((kj == 0) & (qi == 0))
        def _():
            dq_s[...] = jnp.zeros_like(dq_s)

        @pl.when(qi == kj)
        def _():
            dk_s[...] = jnp.zeros_like(dk_s)
            dv_s[...] = jnp.zeros_like(dv_s)

        def tile(diagonal):
            bands = [(r0, r1, r1) for r0, r1 in zip(edges[:-1], edges[1:])] if diagonal else [(0, tq, tq)]
            lse = lse_ref[...]
            for r0, r1, c1 in bands:
                nr = r1 - r0
                rows = pl.ds(pl.multiple_of(qi * tq + r0, 8), nr)
                if diagonal:
                    mask = (lax.broadcasted_iota(jnp.int32, (nr, c1), 1)
                            <= r0 + lax.broadcasted_iota(jnp.int32, (nr, c1), 0))
                def scores(h):
                    ps = _pair(h)
                    k = k_ref[0:c1, ps]
                    qs = q_ref[r0:r1, ps] * scale
                    dov = _own_lanes(do_ref[r0:r1, ps], h)
                    return _dot_nt(_own_lanes(qs, h), k), _dot_nt(dov, v_ref[0:c1, ps]), k, qs, dov

                nxt = scores(0)
                for h in range(n_heads):
                    ps = _pair(h)
                    own = slice(h * HW, (h + 1) * HW)
                    s, dp, k, qs, dov = nxt
                    if h + 1 < n_heads:
                        nxt = scores(h + 1)
                    s = s - fr_ref[h:h + 1, 0:c1]
                    if diagonal:
                        s = jnp.where(mask, s, NEG)
                    p = jnp.exp(s - lse[r0:r1, h:h + 1])
                    dsum = jnp.sum(dov.astype(F32) * o_ref[r0:r1, ps], axis=1, keepdims=True)
                    dsb = (p * (dp - dsum)).astype(BF16)
                    dv = _dot_tn(p.astype(BF16), dov)
                    dk_s[0:c1, own] += _dot_tn(dsb, _own_lanes_and_ones(qs, h))
                    dq_s[rows, own] += _dot(dsb, _own_lanes_and_ones(k, h))
                    if h % 2 == 0:
                        dv_even = dv
                    else:
                        dv_s[0:c1, ps] += dv_even + dv

        def compact(acc, data_scale):
            rows = acc.shape[0]
            low = lax.broadcasted_iota(jnp.int32, (rows, HW), 1) < HEAD_DIM
            lane = lax.broadcasted_iota(jnp.int32, (rows, W), 1)
            vals, sums = [], jnp.zeros((rows, W), F32)
            for h in range(0, n_heads, 2):
                even, odd = acc[:, h * HW:(h + 1) * HW], acc[:, (h + 1) * HW:(h + 2) * HW]
                vals.append(jnp.where(low, even, odd) * data_scale)
                sums = jnp.where(lane == h, even[:, _sum_lane(h):_sum_lane(h) + 1],
                                 jnp.where(lane == h + 1, odd[:, _sum_lane(h + 1):_sum_lane(h + 1) + 1], sums))
            return vals, sums

        @pl.when(qi > kj)
        def _():
            tile(False)

        @pl.when(qi == kj)
        def _():
            tile(True)
            rows = pl.ds(pl.multiple_of(qi * tq, 8), tq)
            vals, sums = compact(dq_s[rows, :], scale)
            for h in range(0, n_heads, 2):
                dq_ref[rows, _pair(h)] = vals[h // 2]
            dfq_ref[rows, :] = sums

        @pl.when(qi == nq - 1)
        def _():
            vals, sums = compact(dk_s[...], 1.0)
            for h in range(0, n_heads, 2):
                dk_ref[:, _pair(h)] = vals[h // 2].astype(BF16)
            dfk_ref[...] = sums
            dv_ref[...] = dv_s[...].astype(BF16)

    qq = lambda b, kj, qi: jnp.maximum(qi, kj)
    qblk = lambda w, cb: pl.BlockSpec((None, tq, w), lambda b, kj, qi: (b, qq(b, kj, qi), cb))
    kblk = lambda w, cb: pl.BlockSpec((None, tq, w), lambda b, kj, qi: (b, kj, cb))
    return _launch(
        body, name=name, grid=(B, nq, nq), args=(proj, proj, proj, o, do, lse, fr), comm=comm,
        in_specs=[qblk(AD, 3), kblk(AD, 4), kblk(AD, 5), qblk(AD, 0), qblk(AD, 0), qblk(W, 0),
                  pl.BlockSpec((None, None, n_heads, tq), lambda b, kj, qi: (b, kj, 0, 0))],
        out_specs=[pl.BlockSpec((None, L, AD), lambda b, kj, qi: (b, 0, 0)),
                   kblk(AD, 0), kblk(AD, 0), kblk(W, 0),
                   pl.BlockSpec((None, L, W), lambda b, kj, qi: (b, 0, 0))],
        out_shape=[jax.ShapeDtypeStruct((B, L, AD), F32), jax.ShapeDtypeStruct((B, L, AD), BF16),
                   jax.ShapeDtypeStruct((B, L, AD), BF16), jax.ShapeDtypeStruct((B, L, W), F32),
                   jax.ShapeDtypeStruct((B, L, W), F32)],
        scratch_shapes=[pltpu.VMEM((L, n_heads * HW), F32), pltpu.VMEM((tq, n_heads * HW), F32),
                        pltpu.VMEM((tq, AD), F32)])


def _mix_gather(refs, first):
    b_ref, c_ref, hc_ref, cp_ref, hcp_ref, o_ref, cw_ref, p_ref = refs
    bg = b_ref[...].astype(F32)
    u = c_ref[...].astype(F32) * hc_ref[...].astype(F32)
    prev = cp_ref[...].astype(F32) * hcp_ref[...].astype(F32)
    prev = jnp.where(first, 0.0, prev)
    cv, u1, u2 = _causal_conv(u, prev, cw_ref[...])
    yc = bg * cv
    p = p_ref[...]
    rc = lax.rsqrt(_group_mean(yc * yc, p) + EPS)
    ya = o_ref[...].astype(F32)
    ra = lax.rsqrt(_group_mean(ya * ya, p) + EPS)
    return bg, (u, u1, u2), cv, yc * rc, rc, ya * ra, ra


def _mix_specs(tm, CD):
    per = tm // HALO
    cur = lambda cb: pl.BlockSpec((None, tm, CD), lambda b, i: (b, i, cb))
    prev = lambda cb: pl.BlockSpec((None, HALO, CD), lambda b, i: (b, jnp.maximum(i * per - 1, 0), cb))
    return [cur(0), cur(1), cur(2), prev(1), prev(2), cur(0)]


def _mix_out(proj, o, cw, gc, ga, wout, h, pmat, next_gain, *, tm, name, comm=None):
    B, L, D = h.shape
    CD = o.shape[-1]
    const = lambda b, i: (0, 0)

    def body(b_ref, c_ref, hc_ref, cp_ref, hcp_ref, o_ref, cw_ref, p_ref, gc_ref, ga_ref, w_ref, h_ref, ng_ref,
             out_ref, y_ref, n_ref):
        first = pl.program_id(1) == 0
        _, _, _, zc, _, za, _ = _mix_gather((b_ref, c_ref, hc_ref, cp_ref, hcp_ref, o_ref, cw_ref, p_ref), first)
        yc = (zc * gc_ref[...]).astype(BF16)
        ya = (za * ga_ref[...]).astype(BF16)
        y_ref[:, :CD] = yc
        y_ref[:, CD:] = ya
        out = h_ref[...] + _dot(yc, w_ref[:CD, :]) + _dot(ya, w_ref[CD:, :])
        out_ref[...] = out
        n_ref[...] = _rms(out, ng_ref[...])

    tile = pl.BlockSpec((None, tm, D), lambda b, i: (b, i, 0))
    return _launch(
        body, name=name, grid=(B, L // tm),
        in_specs=_mix_specs(tm, CD)
                 + [pl.BlockSpec(cw.shape, const), pl.BlockSpec(pmat.shape, const),
                    pl.BlockSpec((1, CD), const), pl.BlockSpec((1, CD), const), pl.BlockSpec((D, D), const),
                    tile, pl.BlockSpec((1, D), const)],
        out_specs=[tile, tile, tile],
        out_shape=[jax.ShapeDtypeStruct((B, L, D), F32), jax.ShapeDtypeStruct((B, L, D), BF16),
                   jax.ShapeDtypeStruct((B, L, D), BF16)],
        args=(proj, proj, proj, proj, proj, o, cw, pmat, gc, ga, wout, h, next_gain), comm=comm)


def _mix_out_bwd(dhb, proj, o, cw, gc, ga, wout, pmat, *, tm, name, comm=None):
    B, L, D = dhb.shape
    CD = o.shape[-1]
    const = lambda b, i: (0, 0)

    def body(dh_ref, b_ref, c_ref, hc_ref, cp_ref, hcp_ref, o_ref, cw_ref, p_ref, gc_ref, ga_ref, w_ref,
             db_ref, dcv_ref, do_ref, dgc_ref, dga_ref, dcw_ref):
        first = pl.program_id(1) == 0

        @pl.when((pl.program_id(0) == 0) & first)
        def _():
            dgc_ref[...] = jnp.zeros_like(dgc_ref)
            dga_ref[...] = jnp.zeros_like(dga_ref)
            dcw_ref[...] = jnp.zeros_like(dcw_ref)

        bg, us, cv, zc, rc, za, ra = _mix_gather(
            (b_ref, c_ref, hc_ref, cp_ref, hcp_ref, o_ref, cw_ref, p_ref), first)
        p = p_ref[...]
        dh = dh_ref[...]
        dyc = _dot_nt(dh, w_ref[:CD, :])
        dya = _dot_nt(dh, w_ref[CD:, :])

        dgc_ref[...] += jnp.sum(dyc * zc, axis=0, keepdims=True)
        dz = dyc * gc_ref[...]
        dx = rc * (dz - zc * _group_mean(dz * zc, p))
        db_ref[...] = (dx * cv).astype(BF16)
        dcv = dx * bg
        dcv_ref[...] = dcv.astype(BF16)
        for k in range(3):
            dcw_ref[k:k + 1, :] += jnp.sum(dcv * us[2 - k], axis=0, keepdims=True)

        dga_ref[...] += jnp.sum(dya * za, axis=0, keepdims=True)
        dz = dya * ga_ref[...]
        do_ref[...] = (ra * (dz - za * _group_mean(dz * za, p))).astype(BF16)

    tile = lambda w: pl.BlockSpec((None, tm, w), lambda b, i: (b, i, 0))
    return _launch(
        body, name=name, grid=(B, L // tm), comm=comm,
        args=(dhb, proj, proj, proj, proj, proj, o, cw, pmat, gc, ga, wout),
        in_specs=[tile(D)] + _mix_specs(tm, CD)
                 + [pl.BlockSpec(cw.shape, const), pl.BlockSpec(pmat.shape, const),
                    pl.BlockSpec((1, CD), const), pl.BlockSpec((1, CD), const), pl.BlockSpec((D, D), const)],
        out_specs=[tile(CD), tile(CD), tile(CD),
                   pl.BlockSpec((1, CD), const), pl.BlockSpec((1, CD), const), pl.BlockSpec((8, CD), const)],
        out_shape=[jax.ShapeDtypeStruct((B, L, CD), BF16)] * 3
                  + [jax.ShapeDtypeStruct((1, CD), F32)] * 2 + [jax.ShapeDtypeStruct((8, CD), F32)])


def _conv_bwd(dcv, proj, cw, *, tm, name):
    B, L, CD = dcv.shape
    per = tm // HALO
    nhalo = L // HALO
    nt = L // tm

    def body(d_ref, dn_ref, c_ref, hc_ref, cw_ref, out_ref):
        last = pl.program_id(1) == nt - 1
        d = d_ref[...].astype(F32)
        nxt = jnp.where(last, 0.0, dn_ref[...].astype(F32))
        n0, n1 = _row_of(nxt, 0), _row_of(nxt, 1)
        rows = lax.broadcasted_iota(jnp.int32, d.shape, 0)
        d1 = jnp.where(rows == tm - 1, n0, pltpu.roll(d, tm - 1, 0))
        d2 = jnp.where(rows == tm - 2, n0, jnp.where(rows == tm - 1, n1, pltpu.roll(d, tm - 2, 0)))
        w = cw_ref[...]
        du = w[2:3, :] * d + w[1:2, :] * d1 + w[0:1, :] * d2
        out_ref[:, :CD] = (du * hc_ref[...].astype(F32)).astype(BF16)
        out_ref[:, CD:] = (du * c_ref[...].astype(F32)).astype(BF16)

    return pl.pallas_call(
        body, name=name, grid=(B, nt),
        in_specs=[pl.BlockSpec((None, tm, CD), lambda b, i: (b, i, 0)),
                  pl.BlockSpec((None, HALO, CD), lambda b, i: (b, jnp.minimum((i + 1) * per, nhalo - 1), 0)),
                  pl.BlockSpec((None, tm, CD), lambda b, i: (b, i, 1)),
                  pl.BlockSpec((None, tm, CD), lambda b, i: (b, i, 2)),
                  pl.BlockSpec(cw.shape, lambda b, i: (0, 0))],
        out_specs=pl.BlockSpec((None, tm, 2 * CD), lambda b, i: (b, i, 0)),
        out_shape=jax.ShapeDtypeStruct((B, L, 2 * CD), BF16),
        compiler_params=_params("arbitrary", "arbitrary"),
    )(dcv, dcv, proj, proj, cw)


def _place():
    x, y, c = lax.axis_index("x"), lax.axis_index("y"), lax.axis_index("c")
    others = [(1 - x, y), (x, 1 - y), (1 - x, 1 - y)]
    return x, y, c, others


def _all_gather_shards(shards, *, name):
    n = len(shards)

    def body(*refs):
        ins, outs = refs[:n], refs[n:2 * n]
        send, recv, fsend, frecv, lsem = refs[2 * n:]
        x, y, c, others = _place()
        me = 2 * x + y
        local = [pltpu.make_async_copy(ins[t], outs[t].at[me], lsem.at[t]) for t in range(n)]
        for cp in local:
            cp.start()

        def half(t, k):
            hr = shards[t].shape[0] // 2
            return pl.ds(pl.multiple_of(k * hr, HALO), hr)

        def ici(t, j, src_chip, to):
            src = ins[t].at[half(t, c)] if to is not None else outs[t].at[src_chip, half(t, c)]
            return pltpu.make_async_remote_copy(
                src_ref=src, dst_ref=outs[t].at[src_chip, half(t, c)],
                send_sem=send.at[3 * t + j], recv_sem=recv.at[3 * t + j],
                device_id=(x, y, c) if to is None else to, device_id_type=MESH)

        def d2d(t, j, src_chip, k):
            return pltpu.make_async_remote_copy(
                src_ref=outs[t].at[src_chip, half(t, k)], dst_ref=outs[t].at[src_chip, half(t, k)],
                send_sem=fsend.at[3 * t + j], recv_sem=frecv.at[3 * t + j],
                device_id=(x, y, 1 - c), device_id_type=MESH)

        firsts = [ici(t, j, me, (ox, oy, c)) for t in range(n) for j, (ox, oy) in enumerate(others)]
        for cp in firsts:
            cp.start()
        passed = []
        for t in range(n):
            for j, (ox, oy) in enumerate(others):
                ici(t, j, 2 * ox + oy, None).wait_recv()
                cp = d2d(t, j, 2 * ox + oy, c)
                cp.start()
                passed.append(cp)
        for t in range(n):
            for j, (ox, oy) in enumerate(others):
                d2d(t, j, 2 * ox + oy, 1 - c).wait_recv()
        for cp in firsts + passed:
            cp.wait_send()
        for cp in local:
            cp.wait()

    return pl.pallas_call(
        body, name=name,
        in_specs=[ANY] * n, out_specs=[ANY] * n,
        out_shape=[jax.ShapeDtypeStruct((N_SHARD,) + s.shape, s.dtype) for s in shards],
        scratch_shapes=[pltpu.SemaphoreType.DMA((3 * n,))] * 4 + [pltpu.SemaphoreType.DMA((n,))],
    )(*shards)


def _all_reduce_small(slab, *, name):
    def body(in_ref, out_ref, gath, send, recv):
        x, y, c, _ = _place()
        me = 4 * x + 2 * y + c
        gath[me] = in_ref[...]
        copies, peers = [], []
        for m in range(1, N_DEV):
            px = jnp.where((m >> 2) & 1, 1 - x, x)
            py = jnp.where((m >> 1) & 1, 1 - y, y)
            pc = jnp.where(m & 1, 1 - c, c)
            cp = pltpu.make_async_remote_copy(
                src_ref=in_ref, dst_ref=gath.at[me], send_sem=send.at[m - 1], recv_sem=recv.at[m - 1],
                device_id=(px, py, pc), device_id_type=MESH)
            cp.start()
            copies.append(cp)
            peers.append(4 * px + 2 * py + pc)
        for m in range(1, N_DEV):
            pltpu.make_async_remote_copy(
                src_ref=in_ref, dst_ref=gath.at[peers[m - 1]], send_sem=send.at[m - 1], recv_sem=recv.at[m - 1],
                device_id=(x, y, c), device_id_type=MESH).wait_recv()
        for cp in copies:
            cp.wait_send()
        acc = gath[0]
        for k in range(1, N_DEV):
            acc = acc + gath[k]
        out_ref[...] = acc

    vm = pl.BlockSpec(memory_space=pltpu.VMEM)
    return pl.pallas_call(
        body, name=name, in_specs=[vm], out_specs=vm,
        out_shape=jax.ShapeDtypeStruct(slab.shape, slab.dtype),
        scratch_shapes=[pltpu.VMEM((N_DEV,) + slab.shape, slab.dtype),
                        pltpu.SemaphoreType.DMA((N_DEV - 1,)), pltpu.SemaphoreType.DMA((N_DEV - 1,))],
    )(slab)


def _gather_stage(shards, into, *, ici=(), d2d=()):
    n = len(shards) if into is None else len(into)
    ns = len(shards) if ici else 0
    ni, nd = max(len(ici), 1), max(len(d2d), 1)
    shapes = [s.shape for s in shards] if into is None else [p.shape[1:] for p in into]
    dtypes = [s.dtype for s in shards] if into is None else [p.dtype for p in into]

    def copies(ins, outs, sems, sending):
        x, y, c, others = _place()
        me = 2 * x + y
        out = []
        for t in range(n):
            hr = shapes[t][0] // 2
            mine = pl.ds(pl.multiple_of(c * hr, HALO), hr)
            theirs = pl.ds(pl.multiple_of((1 - c) * hr, HALO), hr)
            for a, j in enumerate(ici):
                ox, oy = others[j]
                src_chip = me if sending else 2 * ox + oy
                out.append(pltpu.make_async_remote_copy(
                    src_ref=ins[t].at[mine], dst_ref=outs[t].at[src_chip, mine],
                    send_sem=sems[0].at[ni * t + a], recv_sem=sems[1].at[ni * t + a],
                    device_id=(ox, oy, c) if sending else (x, y, c), device_id_type=MESH))
            for a, j in enumerate(d2d):
                ox, oy = others[j]
                blk = outs[t].at[2 * ox + oy, mine if sending else theirs]
                out.append(pltpu.make_async_remote_copy(
                    src_ref=blk, dst_ref=blk, send_sem=sems[2].at[nd * t + a], recv_sem=sems[3].at[nd * t + a],
                    device_id=(x, y, 1 - c) if sending else (x, y, c), device_id_type=MESH))
        return out

    def local(ins, outs, sems):
        if into is not None:
            return []
        x, y, _, _ = _place()
        return [pltpu.make_async_copy(ins[t], outs[t].at[2 * x + y], sems[4].at[t]) for t in range(n)]

    def start(ins, outs, sems):
        for cp in local(ins, outs, sems) + copies(ins, outs, sems, True):
            cp.start()

    def finish(ins, outs, sems):
        for cp in copies(ins, outs, sems, False):
            cp.wait_recv()
        for cp in copies(ins, outs, sems, True):
            cp.wait_send()
        for cp in local(ins, outs, sems):
            cp.wait()

    return _Comm((list(shards) if ici or into is None else []) + (list(into) if into is not None else []),
                 [jax.ShapeDtypeStruct((N_SHARD,) + tuple(sh), dt) for sh, dt in zip(shapes, dtypes)],
                 [ni * n, ni * n, nd * n, nd * n, n], start, finish,
                 aliases=None if into is None else {ns + t: t for t in range(n)})


def _gather_ici(shards):
    return _gather_stage(shards, None, ici=(0, 1, 2))


def _gather_d2d(parts):
    return _gather_stage((), parts, d2d=(0, 1, 2))


def _swap_halves(grads):
    n = len(grads)

    def copies(ins, outs, sems):
        x, y, c, _ = _place()
        out = []
        for t in range(n):
            hr = grads[t].shape[1] // 2
            rows = pl.ds(pl.multiple_of((1 - c) * hr, 8), hr)
            out.append(pltpu.make_async_remote_copy(
                src_ref=ins[t].at[:, rows, :], dst_ref=outs[t], send_sem=sems[0].at[t], recv_sem=sems[1].at[t],
                device_id=(x, y, 1 - c), device_id_type=MESH))
        return out

    def start(ins, outs, sems):
        for cp in copies(ins, outs, sems):
            cp.start()

    def finish(ins, outs, sems):
        for cp in copies(ins, outs, sems):
            cp.wait()

    return _Comm(grads, [jax.ShapeDtypeStruct((N_SHARD, g.shape[1] // 2, g.shape[2]), g.dtype) for g in grads],
                 [n, n], start, finish)


def _pair_sum(g, got, c, *, name):
    ns, R, C = g.shape
    hr = R // 2

    def body(c_ref, g_ref, r_ref, o_ref):
        o_ref[...] = (g_ref[...] + r_ref[...]).astype(BF16)

    return pl.pallas_call(
        body, name=name,
        grid_spec=pltpu.PrefetchScalarGridSpec(
            num_scalar_prefetch=1, grid=(ns,),
            in_specs=[pl.BlockSpec((None, hr, C), lambda s, cr: (s, cr[0], 0)),
                      pl.BlockSpec((None, hr, C), lambda s, cr: (s, 0, 0))],
            out_specs=pl.BlockSpec((None, hr, C), lambda s, cr: (s, 0, 0))),
        out_shape=jax.ShapeDtypeStruct((ns, hr, C), BF16),
        compiler_params=_params("arbitrary"),
    )(c, g, got)


def _scatter_chips(sums):
    n = len(sums)

    def copies(ins, outs, sems, sending):
        x, y, c, others = _place()
        me = 2 * x + y
        out = []
        for t in range(n):
            for j, (ox, oy) in enumerate(others):
                there = 2 * ox + oy
                out.append(pltpu.make_async_remote_copy(
                    src_ref=ins[t].at[there if sending else me], dst_ref=outs[t].at[me if sending else there],
                    send_sem=sems[0].at[3 * t + j], recv_sem=sems[1].at[3 * t + j],
                    device_id=(ox, oy, c) if sending else (x, y, c), device_id_type=MESH))
        return out

    def start(ins, outs, sems):
        for cp in copies(ins, outs, sems, True):
            cp.start()

    def finish(ins, outs, sems):
        for cp in copies(ins, outs, sems, False):
            cp.wait_recv()
        for cp in copies(ins, outs, sems, True):
            cp.wait_send()

    return _Comm(sums, [jax.ShapeDtypeStruct(s.shape, s.dtype) for s in sums], [3 * n, 3 * n], start, finish)


def _chip_sum(g, got, landed, idx, *, name):
    ns, R, C = g.shape
    hr = R // 2
    steps = next(k for k in (4, 2, 1) if hr % (k * HALO) == 0)
    tr = hr // steps

    def body(i_ref, g_ref, r_ref, a_ref, b_ref, c_ref, o_ref):
        acc = g_ref[...] + r_ref[...]
        for ref in (a_ref, b_ref, c_ref):
            acc = acc + ref[...].astype(F32)
        o_ref[...] = acc

    other = lambda k: pl.BlockSpec((None, tr, C), lambda s, ir: (ir[2 + k], s, 0))
    return pl.pallas_call(
        body, name=name,
        grid_spec=pltpu.PrefetchScalarGridSpec(
            num_scalar_prefetch=1, grid=(steps,),
            in_specs=[pl.BlockSpec((None, tr, C), lambda s, ir: (ir[0], ir[1] * steps + s, 0)),
                      pl.BlockSpec((None, tr, C), lambda s, ir: (ir[0], s, 0)),
                      other(0), other(1), other(2)],
            out_specs=pl.BlockSpec((tr, C), lambda s, ir: (ir[1] * steps + s, 0))),
        out_shape=jax.ShapeDtypeStruct((R, C), F32),
        compiler_params=_params("arbitrary"),
    )(idx, g, got, landed, landed, landed)


def _share_halves(halves):
    n = len(halves)

    def copies(outs, sems, sending):
        x, y, c, _ = _place()
        out = []
        for t in range(n):
            hr = halves[t].shape[0] // 2
            rows = pl.ds(pl.multiple_of((c if sending else 1 - c) * hr, 8), hr)
            out.append(pltpu.make_async_remote_copy(
                src_ref=outs[t].at[rows, :], dst_ref=outs[t].at[rows, :], send_sem=sems[0].at[t],
                recv_sem=sems[1].at[t], device_id=(x, y, 1 - c) if sending else (x, y, c), device_id_type=MESH))
        return out

    def start(ins, outs, sems):
        for cp in copies(outs, sems, True):
            cp.start()

    def finish(ins, outs, sems):
        for cp in copies(outs, sems, False):
            cp.wait_recv()
        for cp in copies(outs, sems, True):
            cp.wait_send()

    return _Comm(halves, [jax.ShapeDtypeStruct(h.shape, h.dtype) for h in halves], [n, n], start, finish,
                 aliases={t: t for t in range(n)})


def _adamw(w, g, m, v, *, name):
    R, C = w.shape
    tr = next((k for k in (128, 64, 32, 16, 8) if R % k == 0), R)

    def body(w_ref, g_ref, m_ref, v_ref, go_ref, d_ref, mo_ref, vo_ref):
        gv = g_ref[...]
        go_ref[...] = gv
        mn = ADAM_B1 * m_ref[...] + (1.0 - ADAM_B1) * gv
        vn = ADAM_B2 * v_ref[...] + (1.0 - ADAM_B2) * (gv * gv)
        m_hat = mn / (1.0 - ADAM_B1 ** ADAM_STEP)
        v_hat = vn / (1.0 - ADAM_B2 ** ADAM_STEP)
        d_ref[...] = -ADAM_LR * (m_hat / (jnp.sqrt(v_hat) + ADAM_EPS) + ADAM_WD * w_ref[...])
        mo_ref[...] = mn
        vo_ref[...] = vn

    blk = pl.BlockSpec((tr, C), lambda i: (i, 0))
    return pl.pallas_call(
        body, name=name, grid=(R // tr,), in_specs=[blk] * 4, out_specs=[blk] * 4,
        out_shape=[jax.ShapeDtypeStruct((R, C), F32)] * 4,
        compiler_params=_params("arbitrary"),
    )(w, g, m, v)


def _pack_small(D, meta, n1, nm, n3, nf, gc, ga, bf, cw):
    def row(a):
        a = a.reshape(-1, a.shape[-1])
        return jnp.pad(a, ((0, 0), (0, D - a.shape[-1])))
    rows = [row(meta), row(n1), row(nm), row(n3), row(nf), row(jnp.concatenate([gc, ga], axis=-1)), row(bf), row(cw)]
    slab = jnp.concatenate(rows, axis=0)
    return jnp.pad(slab, ((0, SMALL_ROWS - slab.shape[0]), (0, 0)))


def _unpack_small(slab, like):
    meta, n1, nm, n3, nf, gc, ga, bf, cw = like
    nmeta, mc = meta.shape
    out = [slab[:nmeta, :mc].reshape(meta.shape)]
    r = nmeta
    for a in (n1, nm, n3, nf):
        out.append(slab[r, :a.shape[-1]].reshape(a.shape))
        r += 1
    cd = gc.shape[-1]
    out.append(slab[r, :cd].reshape(gc.shape))
    out.append(slab[r, cd:cd + ga.shape[-1]].reshape(ga.shape))
    r += 1
    out.append(slab[r, :bf.shape[-1]].reshape(bf.shape))
    r += 1
    out.append(slab[r:r + 3, :cw.shape[-1]].reshape(cw.shape))
    return out


def kernel(x, meta_tokens, ffn1_norm, ffn1_w_gu, ffn1_w_down, mix_norm, w_in, conv_w, b_f, out_norm_conv, out_norm_attn, w_out, ffn2_norm, ffn2_w_gu, ffn2_w_down, final_norm, loss_target, m_meta_tokens, m_ffn1_norm, m_ffn1_w_gu, m_ffn1_w_down, m_mix_norm, m_w_in, m_conv_w, m_b_f, m_out_norm_conv, m_out_norm_attn, m_w_out, m_ffn2_norm, m_ffn2_w_gu, m_ffn2_w_down, m_final_norm, v_meta_tokens, v_ffn1_norm, v_ffn1_w_gu, v_ffn1_w_down, v_mix_norm, v_w_in, v_conv_w, v_b_f, v_out_norm_conv, v_out_norm_attn, v_w_out, v_ffn2_norm, v_ffn2_w_gu, v_ffn2_w_down, v_final_norm):
    B, S, D = x.shape
    L = S + N_META
    T = B * L
    tm = L // 3
    assert tm * 3 == L and tm % HALO == 0
    tm2 = 2 * tm
    assert T % tm2 == 0
    guc = ffn1_w_gu.shape[-1]
    ff = N_SHARD * guc // 2
    H = b_f.shape[-1]
    AD = H * HEAD_DIM
    CD = conv_w.shape[-1] * N_SHARD
    assert CD == AD and CD + AD == D and CD % LANES == 0
    n_main = 3 * CD + 3 * AD
    ins = w_in.shape[-1]

    xi, yi, ci = lax.axis_index("x"), lax.axis_index("y"), lax.axis_index("c")
    chip = 2 * xi + yi

    small_shard = jnp.zeros((2 * HALO, meta_tokens.shape[-1]), F32)
    small_shard = small_shard.at[:N_META].set(meta_tokens)
    small_shard = small_shard.at[N_META:N_META + 3, :conv_w.shape[-1]].set(conv_w[0])
    big = [ffn1_w_gu[0], ffn1_w_down[0], w_in[0], w_out[0], ffn2_w_gu[0], ffn2_w_down[0]]
    wgu1_s, wd1_s, win_s, wout_s, wgu2_s, wd2_s = [w.astype(BF16) for w in big]
    small_g, = _all_gather_shards([small_shard], name="gather_small")
    meta_f = jnp.moveaxis(small_g[:, :N_META], 0, 1).reshape(N_META, D)
    cw_f = jnp.moveaxis(small_g[:, N_META:N_META + 3, :conv_w.shape[-1]], 0, 1).reshape(3, CD)
    cw8 = jnp.pad(cw_f, ((0, 5), (0, 0)))
    bf_p = jnp.pad(b_f, ((0, 0), (0, LANES - H)))
    gid = jnp.arange(CD) // HEAD_DIM
    pmat = jnp.where(gid[:, None] == gid[None, :], 1.0 / HEAD_DIM, 0.0).astype(BF16)

    gu_shape = jax.ShapeDtypeStruct((2, T, ff), BF16)
    gu_w_spec = pl.BlockSpec((None, D, guc), lambda s, i: (s, 0, 0))
    gu_o_spec = pl.BlockSpec((None, tm2, guc), lambda s, i: (s // 2, i, s % 2))

    sid = jnp.bitwise_xor(chip, jnp.array([0, 2, 1, 3], jnp.int32)).astype(jnp.int32)
    (h0, n1), wgu1_h = _embed_norm(x, meta_f, ffn1_norm, tm=tm, name="embed_norm",
                                   comm=_gather_stage([wgu1_s], None, ici=(0, 1)))
    gu1, wgu1_h = _ffn_up(n1, wgu1_s[None], sid, None, tm=tm2, first=0, count=1, name="ffn1_up_own",
                          comm=_gather_stage([wgu1_s], wgu1_h, ici=(2,), d2d=(0, 1)))
    gu1, out = _ffn_up(n1, wgu1_h[0], sid, gu1, tm=tm2, first=1, count=2, name="ffn1_up_near",
                       comm=_join(_gather_stage((), wgu1_h, d2d=(2,)), _gather_ici([wd1_s, wout_s])))
    wgu1, down_w = out[0], out[1:]
    gu1, (wd1, wout_g) = _ffn_up(n1, wgu1, sid, gu1, tm=tm2, first=3, count=1, name="ffn1_up_far",
                                 comm=_gather_d2d(down_w))
    wd1 = wd1.reshape(ff, D)
    (h1, n2), win_h = _ffn_down(gu1, wd1, h0, mix_norm, tm=tm, name="ffn1_down", comm=_gather_ici([win_s]))
    win_g, = _run_comm(_gather_d2d(win_h), name="gather_w_in")
    wout_f = wout_g.reshape(D, D)
    win_f = jnp.moveaxis(win_g, 0, 1).reshape(D, N_SHARD * ins)
    win_main = win_f[:, :n_main]
    win_fg = jnp.pad(win_f[:, n_main:], ((0, 0), (0, LANES - H)))

    proj, fg = _mix_in(n2, win_main, win_fg, tm=tm2, nb=n_main // (3 * CD), name="mix_in")
    proj3 = proj.reshape(B, L, n_main)
    fg3 = fg.reshape(B, L, LANES)
    fc = _fcum(fg3, bf_p, ch=tm, name="forget_cumsum")
    fr = fc[:, :, :H].reshape(B, L // tm, tm, H).transpose(0, 1, 3, 2)
    (o, lse), ffn2_w = _attn_fwd(proj3, fr, tq=tm, n_heads=H, name="attn_fwd",
                                 comm=_gather_ici([wgu2_s, wd2_s]))
    (h2, ymix, n3), (wgu2, wd2) = _mix_out(
        proj3, o, cw8, out_norm_conv, out_norm_attn, wout_f, h1.reshape(B, L, D), pmat, ffn2_norm,
        tm=tm, name="mix_out", comm=_gather_d2d(ffn2_w))
    wd2 = wd2.reshape(ff, D)
    h2 = h2.reshape(T, D)
    n3 = n3.reshape(T, D)

    gu2, _ = _matmul_nn(n3, wgu2, tm=tm2, nb=N_SHARD, w_spec=gu_w_spec, out_shape=gu_shape, out_spec=gu_o_spec,
                        name="ffn2_up")
    (dh3f, dh3b, d_gf, loss_part), _ = _ffn_down_loss(gu2, wd2, h2, final_norm.reshape(1, D), loss_target,
                                                      tm=tm, name="ffn2_down_loss")

    c_arr = jnp.reshape(ci, (1,)).astype(jnp.int32)
    ks = jnp.arange(N_SHARD - 1, dtype=jnp.int32)
    idx = jnp.concatenate([jnp.stack([chip, ci]).astype(jnp.int32), ks + (ks >= chip).astype(jnp.int32)])

    def pair_sums(grads, got, names):
        return [_pair_sum(g, r, c_arr, name="pair_sum_" + nm) for g, r, nm in zip(grads, got, names)]

    def chip_sums(grads, got, landed, names):
        return [_chip_sum(g, r, l, idx, name="chip_sum_" + nm) for g, r, l, nm in zip(grads, got, landed, names)]

    def dw_up(n, dgu, name, comm=None):
        return _matmul_tn(
            n, dgu, tm=L, nb=N_SHARD, kb=D, x_spec=pl.BlockSpec((L, D), lambda s, i: (i, 0)),
            y_spec=pl.BlockSpec((None, L, guc), lambda s, i: (s // 2, i, s % 2)),
            out_shape=jax.ShapeDtypeStruct((N_SHARD, D, guc), F32),
            out_spec=pl.BlockSpec((None, D, guc), lambda s, i: (s, 0, 0)), name=name, comm=comm)

    (dgu2, d_wd2), _ = _ffn_bwd_act(dh3b, gu2, wd2, tm=tm, guc=guc, name="ffn2_bwd_act")
    (dh2, dh2b, d_g3), _ = _ffn_bwd_in(dgu2, wgu2, h2, ffn2_norm, dh3f, tm=tm, scale=1.0, name="ffn2_bwd_in")
    d_wgu2, _ = dw_up(n3, dgu2, "ffn2_dw_up")
    grads_f2 = [d_wgu2, d_wd2.reshape(N_SHARD, ff // N_SHARD, D)]
    names_f2 = ["wgu2", "wd2"]

    dh2b3 = dh2b.reshape(B, L, D)
    (d_bg, d_cv, d_o, d_gc, d_ga, d_cw), got_f2 = _mix_out_bwd(
        dh2b3, proj3, o, cw8, out_norm_conv, out_norm_attn, wout_f, pmat, tm=tm, name="mix_out_bwd",
        comm=_swap_halves(grads_f2))
    sums_f2 = pair_sums(grads_f2, got_f2, names_f2)
    d_wout, _ = _matmul_tn(
        ymix.reshape(T, D), dh2b, tm=tm2, nb=1, kb=D,
        x_spec=pl.BlockSpec((tm2, D), lambda s, i: (i, 0)), y_spec=pl.BlockSpec((tm2, D), lambda s, i: (i, 0)),
        out_shape=jax.ShapeDtypeStruct((D, D), F32), out_spec=pl.BlockSpec((D, D), lambda s, i: (0, 0)),
        name="dw_out")
    d_cc = _conv_bwd(d_cv, proj3, cw8, tm=tm, name="conv_bwd")
    (d_q, d_k, d_v, d_fk, d_fq), landed_f2 = _attn_bwd(proj3, o, d_o, lse, fr, tq=tm, n_heads=H, name="attn_bwd",
                                                       comm=_scatter_chips(sums_f2))
    halves_f2 = chip_sums(grads_f2, got_f2, landed_f2, names_f2)
    d_fg, d_bf = _fcum_bwd(d_fq, d_fk, fg3, bf_p, ch=tm, name="forget_cumsum_bwd")

    parts = [d_bg.reshape(T, CD), d_cc.reshape(T, 2 * CD), d_q.reshape(T, AD), d_k.reshape(T, AD),
             d_v.reshape(T, AD), d_fg.reshape(T, LANES)]
    (dh1, dh1b, d_gm, d_proj), g_f2 = _mix_bwd_in(parts, win_main, win_fg, h1, mix_norm, dh2, tm=tm, scale=0.5,
                                                  name="mix_bwd_in", comm=_share_halves(halves_f2))
    wide = d_proj.shape[1]
    d_win_nat, _ = _matmul_tn(
        n2, d_proj, tm=tm, nb=1, kb=D,
        x_spec=pl.BlockSpec((tm, D), lambda s, i: (i, 0)), y_spec=pl.BlockSpec((tm, wide), lambda s, i: (i, 0)),
        out_shape=jax.ShapeDtypeStruct((D, wide), F32), out_spec=pl.BlockSpec((D, wide), lambda s, i: (0, 0)),
        name="dw_in")
    d_win = jnp.moveaxis(d_win_nat[:, :N_SHARD * ins].reshape(D, N_SHARD, ins), 1, 0)
    grads_mx = [d_win, d_wout.reshape(N_SHARD, D // N_SHARD, D)]
    names_mx = ["win", "wout"]

    (dgu1, d_wd1), got_mx = _ffn_bwd_act(dh1b, gu1, wd1, tm=tm, guc=guc, name="ffn1_bwd_act",
                                         comm=_swap_halves(grads_mx))
    sums_mx = pair_sums(grads_mx, got_mx, names_mx)
    grads_d1 = [d_wd1.reshape(N_SHARD, ff // N_SHARD, D)]
    d_wgu1, out = dw_up(n1, dgu1, "ffn1_dw_up", comm=_join(_scatter_chips(sums_mx), _swap_halves(grads_d1)))
    landed_mx, got_d1 = out[:2], out[2:]
    halves_mx = chip_sums(grads_mx, got_mx, landed_mx, names_mx)
    sums_d1 = pair_sums(grads_d1, got_d1, ["wd1"])
    grads_u1 = [d_wgu1]
    (grad_x, d_meta, d_g1), out = _ffn_bwd_in_first(
        dgu1, wgu1, h0, ffn1_norm, dh1, tm=tm, batch=B, name="ffn1_bwd_in",
        comm=_join(_join(_share_halves(halves_mx), _scatter_chips(sums_d1)), _swap_halves(grads_u1)))
    g_mx, landed_d1, got_u1 = out[:2], out[2:3], out[3:]
    halves_d1 = chip_sums(grads_d1, got_d1, landed_d1, ["wd1"])
    sums_u1 = pair_sums(grads_u1, got_u1, ["wgu1"])
    out = _run_comm(_join(_share_halves(halves_d1), _scatter_chips(sums_u1)), name="scatter_ffn1")
    g_d1, landed_u1 = out[:1], out[1:]
    halves_u1 = chip_sums(grads_u1, got_u1, landed_u1, ["wgu1"])
    g_u1 = _run_comm(_share_halves(halves_u1), name="share_ffn1")
    g_big = [g_u1[0], g_d1[0], g_mx[0], g_mx[1], g_f2[0], g_f2[1]]

    loss_row = jnp.zeros((1, D), F32).at[0, 0].set(loss_part[0, 0])
    slab = _pack_small(D, d_meta, d_g1, d_gm, d_g3, d_gf, d_gc, d_ga, d_bf[:, :H], d_cw[:3])
    slab = slab.at[SMALL_ROWS - 1].set(loss_row[0])
    total = _all_reduce_small(slab, name="reduce_small")
    loss = total[SMALL_ROWS - 1, 0]
    mcols = meta_tokens.shape[-1]
    ccols = conv_w.shape[-1]
    full_like = (jnp.zeros((N_META, D)), ffn1_norm, mix_norm, ffn2_norm, final_norm.reshape(1, D), out_norm_conv,
                 out_norm_attn, b_f, jnp.zeros((1, 3, CD)))
    g_small = _unpack_small(total, full_like)
    g_small[0] = lax.dynamic_slice_in_dim(g_small[0], chip * mcols, mcols, axis=1)
    g_small[8] = lax.dynamic_slice_in_dim(g_small[8], chip * ccols, ccols, axis=2)

    def small_slab(meta, a1, am, a3, af, gc, ga, bf, cw):
        return _pack_small(D, meta, a1, am, a3, af.reshape(1, D), gc, ga, bf, cw[0])

    w_small = small_slab(meta_tokens, ffn1_norm, mix_norm, ffn2_norm, final_norm, out_norm_conv, out_norm_attn, b_f, conv_w)
    m_small = small_slab(m_meta_tokens, m_ffn1_norm, m_mix_norm, m_ffn2_norm, m_final_norm, m_out_norm_conv,
                         m_out_norm_attn, m_b_f, m_conv_w)
    v_small = small_slab(v_meta_tokens, v_ffn1_norm, v_mix_norm, v_ffn2_norm, v_final_norm, v_out_norm_conv,
                         v_out_norm_attn, v_b_f, v_conv_w)
    gs = list(g_small)
    gs[4] = gs[4].reshape(final_norm.shape)
    g_slab = small_slab(gs[0], gs[1], gs[2], gs[3], gs[4], gs[5], gs[6], gs[7], gs[8])
    local_like = (meta_tokens, ffn1_norm, mix_norm, ffn2_norm, final_norm.reshape(1, D), out_norm_conv, out_norm_attn,
                  b_f, conv_w)
    small_out = [_unpack_small(s, local_like)
                 for s in _adamw(w_small, g_slab, m_small, v_small, name="adamw_small")[1:]]
    for lst in small_out:
        lst[4] = lst[4].reshape(final_norm.shape)

    names = ["wgu1", "wd1", "win", "wout", "wgu2", "wd2"]
    w_big = big
    m_big = [m_ffn1_w_gu[0], m_ffn1_w_down[0], m_w_in[0], m_w_out[0], m_ffn2_w_gu[0], m_ffn2_w_down[0]]
    v_big = [v_ffn1_w_gu[0], v_ffn1_w_down[0], v_w_in[0], v_w_out[0], v_ffn2_w_gu[0], v_ffn2_w_down[0]]
    big_out = [_adamw(w, g, m, v, name="adamw_" + nm) for w, g, m, v, nm in zip(w_big, g_big, m_big, v_big, names)]

    def assemble(small, bigs):
        meta, a1, am, a3, af, gc, ga, bf, cw = small
        gu1_, d1_, win_, wout_, gu2_, d2_ = [b[None] for b in bigs]
        return [meta, a1, gu1_, d1_, am, win_, cw, bf, gc, ga, wout_, a3, gu2_, d2_, af]

    gs_out = list(g_small)
    gs_out[4] = gs_out[4].reshape(final_norm.shape)
    grads_out = assemble(gs_out, [b[0] for b in big_out])
    delta_out = assemble(small_out[0], [b[1] for b in big_out])
    m_out = assemble(small_out[1], [b[2] for b in big_out])
    v_out = assemble(small_out[2], [b[3] for b in big_out])
    return (loss, grad_x, *grads_out, *delta_out, *m_out, *v_out)
```

```python
import functools

import jax
import jax.numpy as jnp
from jax import lax
from jax.experimental import pallas as pl
from jax.experimental.pallas import tpu as pltpu

F32 = jnp.float32
BF16 = jnp.bfloat16

EPS = 1e-6
N_META = 16
HEAD_DIM = 64
N_SHARD = 4
N_DEV = 8
HALO = 16
LANES = 128
SMALL_ROWS = 32
VMEM_LIMIT_V7X = 56 * 1024 * 1024
NEG = -1e30
ROW_BANDS = 2

ADAM_LR = 0.001
ADAM_B1 = 0.9
ADAM_B2 = 0.999
ADAM_EPS = 1e-08
ADAM_WD = 0.01
ADAM_STEP = 10

MESH = pl.DeviceIdType.MESH
ANY = pl.BlockSpec(memory_space=pl.ANY)
NT_DIMS = (((1,), (1,)), ((), ()))
TN_DIMS = (((0,), (0,)), ((), ()))


def _params(*sem):
    return pltpu.CompilerParams(dimension_semantics=sem, vmem_limit_bytes=VMEM_LIMIT_V7X)


class _Comm:
    def __init__(self, ins, out_shapes, sems, start, finish, aliases=None):
        self.ins, self.out_shapes, self.sems = list(ins), list(out_shapes), list(sems)
        self.start, self.finish, self.aliases = start, finish, dict(aliases or {})


def _join(a, b):
    ni, no, ns = len(a.ins), len(a.out_shapes), len(a.sems)

    def start(ins, outs, sems):
        a.start(ins[:ni], outs[:no], sems[:ns])
        b.start(ins[ni:], outs[no:], sems[ns:])

    def finish(ins, outs, sems):
        a.finish(ins[:ni], outs[:no], sems[:ns])
        b.finish(ins[ni:], outs[no:], sems[ns:])

    aliases = dict(a.aliases)
    aliases.update({ni + i: no + j for i, j in b.aliases.items()})
    return _Comm(a.ins + b.ins, a.out_shapes + b.out_shapes, a.sems + b.sems, start, finish, aliases)


def _launch(body, *, name, grid, in_specs, out_specs, out_shape, args, scratch_shapes=(), comm=None, prefetch=(),
            aliases=None):
    single = not isinstance(out_shape, (list, tuple))
    out_specs = [out_specs] if single else list(out_specs)
    out_shape = [out_shape] if single else list(out_shape)
    in_specs, scratch_shapes, prefetch = list(in_specs), list(scratch_shapes), list(prefetch)
    params = _params(*(("arbitrary",) * len(grid)))
    n_pf, n_in, n_out, n_scr = len(prefetch), len(in_specs), len(out_specs), len(scratch_shapes)
    c_ins = comm.ins if comm else []
    c_shapes = comm.out_shapes if comm else []
    c_sems = comm.sems if comm else []
    c_in, c_out = len(c_ins), len(c_shapes)

    def carrier(*refs):
        p = 0
        pf = refs[p:p + n_pf]; p += n_pf
        a = refs[p:p + n_in]; p += n_in
        ci = refs[p:p + c_in]; p += c_in
        o = refs[p:p + n_out]; p += n_out
        co = refs[p:p + c_out]; p += c_out
        s = refs[p:p + n_scr]; p += n_scr
        cs = refs[p:]
        if comm:
            first = functools.reduce(lambda u, v: u & v, [pl.program_id(k) == 0 for k in range(len(grid))])

            @pl.when(first)
            def _():
                comm.start(ci, co, cs)

        body(*pf, *a, *o, *s)

        if comm:
            last = functools.reduce(lambda u, v: u & v, [pl.program_id(k) == grid[k] - 1 for k in range(len(grid))])

            @pl.when(last)
            def _():
                comm.finish(ci, co, cs)

    io_aliases = {n_pf + i: j for i, j in (aliases or {}).items()}
    if comm:
        io_aliases.update({n_pf + n_in + i: n_out + j for i, j in comm.aliases.items()})
    all_in, all_out = in_specs + [ANY] * c_in, out_specs + [ANY] * c_out
    all_scratch = scratch_shapes + [pltpu.SemaphoreType.DMA((k,)) for k in c_sems]
    if n_pf:
        spec = dict(grid_spec=pltpu.PrefetchScalarGridSpec(
            num_scalar_prefetch=n_pf, grid=grid, in_specs=all_in, out_specs=all_out, scratch_shapes=all_scratch))
    else:
        spec = dict(grid=grid, in_specs=all_in, out_specs=all_out, scratch_shapes=all_scratch)
    res = pl.pallas_call(carrier, name=name, out_shape=out_shape + c_shapes, input_output_aliases=io_aliases,
                         compiler_params=params, **spec)(*prefetch, *args, *c_ins)
    main = list(res[:n_out])
    return (main[0] if single else main), (list(res[n_out:]) if comm else None)


def _run_comm(comm, *, name):
    c_in, c_out = len(comm.ins), len(comm.out_shapes)

    def body(*refs):
        ci, co, cs = refs[:c_in], refs[c_in:c_in + c_out], refs[c_in + c_out:]
        comm.start(ci, co, cs)
        comm.finish(ci, co, cs)

    return list(pl.pallas_call(
        body, name=name, in_specs=[ANY] * c_in, out_specs=[ANY] * c_out, out_shape=comm.out_shapes,
        scratch_shapes=[pltpu.SemaphoreType.DMA((k,)) for k in comm.sems],
        input_output_aliases=comm.aliases)(*comm.ins))


def _chunks(width, step=512):
    out, c0 = [], 0
    while c0 < width:
        cw = min(step, width - c0)
        out.append((c0, cw))
        c0 += cw
    return out


def _split2(v):
    hi = v.astype(BF16)
    lo = (v - hi.astype(F32)).astype(BF16)
    return hi, lo


def _split3(v):
    hi = v.astype(BF16)
    r = v - hi.astype(F32)
    mid = r.astype(BF16)
    lo = (r - mid.astype(F32)).astype(BF16)
    return hi, mid, lo


def _dot(a, b):
    return jnp.dot(a, b, preferred_element_type=F32)


def _dot_nt(a, b):
    return lax.dot_general(a, b, NT_DIMS, preferred_element_type=F32)


def _dot_tn(a, b):
    return lax.dot_general(a, b, TN_DIMS, preferred_element_type=F32)


def _silu_mul(g, u):
    return g * jax.nn.sigmoid(g) * u


def _rms_bwd(dn, h, gain, dres):
    r = lax.rsqrt(jnp.mean(h * h, axis=-1, keepdims=True) + EPS)
    y = h * r
    dgain = jnp.sum(dn * y, axis=0, keepdims=True)
    dy = dn * gain
    dh = dres + r * (dy - y * jnp.mean(dy * y, axis=-1, keepdims=True))
    return dh, dgain


def _group_mean(v, p):
    hi, lo = _split2(v)
    return _dot(hi, p) + _dot(lo, p)


def _row_of(a, k):
    rows = lax.broadcasted_iota(jnp.int32, a.shape, 0)
    return jnp.sum(jnp.where(rows == k, a, 0.0), axis=0, keepdims=True)


def _causal_conv(u, prev, w):
    rows = lax.broadcasted_iota(jnp.int32, u.shape, 0)
    p1 = _row_of(prev, HALO - 1)
    p2 = _row_of(prev, HALO - 2)
    u1 = jnp.where(rows == 0, p1, pltpu.roll(u, 1, 0))
    u2 = jnp.where(rows == 0, p2, jnp.where(rows == 1, p1, pltpu.roll(u, 2, 0)))
    return w[2:3, :] * u + w[1:2, :] * u1 + w[0:1, :] * u2, u1, u2


def _rms(x, gain):
    return (x * lax.rsqrt(jnp.mean(x * x, axis=-1, keepdims=True) + EPS) * gain).astype(BF16)


def _embed_norm(x, meta, g, *, tm, name, comm=None):
    B, S, D = x.shape
    L = S + N_META
    per_seq = L // tm
    nt = B * per_seq
    body_rows = tm - N_META

    def body(meta_ref, g_ref, x_hbm, h_ref, n_ref, buf, sems):
        i = pl.program_id(0)

        def fetch(k, fn):
            slot, b, t = k % 2, k // per_seq, k % per_seq

            @pl.when(t == 0)
            def _():
                fn(pltpu.make_async_copy(x_hbm.at[b, pl.ds(0, body_rows)],
                                         buf.at[slot, pl.ds(N_META, body_rows)], sems.at[slot]))

            @pl.when(t != 0)
            def _():
                fn(pltpu.make_async_copy(x_hbm.at[b, pl.ds(pl.multiple_of(t * tm - N_META, 8), tm)],
                                         buf.at[slot], sems.at[slot]))

        @pl.when(i == 0)
        def _():
            fetch(i, lambda cp: cp.start(priority=1))

        @pl.when(i + 1 < nt)
        def _():
            fetch(i + 1, lambda cp: cp.start(priority=1))

        fetch(i, lambda cp: cp.wait())
        slot = i % 2

        @pl.when(i % per_seq == 0)
        def _():
            buf[slot, 0:N_META, :] = meta_ref[...]

        hv = buf[slot]
        h_ref[...] = hv
        n_ref[...] = _rms(hv, g_ref[...])

    row = pl.BlockSpec((tm, D), lambda i: (i, 0))
    return _launch(
        body, name=name, grid=(nt,),
        in_specs=[pl.BlockSpec((N_META, D), lambda i: (0, 0)), pl.BlockSpec((1, D), lambda i: (0, 0)), ANY],
        out_specs=[row, row],
        out_shape=[jax.ShapeDtypeStruct((B * L, D), F32), jax.ShapeDtypeStruct((B * L, D), BF16)],
        scratch_shapes=[pltpu.VMEM((2, tm, D), F32), pltpu.SemaphoreType.DMA((2,))],
        args=(meta, g, x), comm=comm)


def _ffn_up(n, wgu, sid, gu_prev, *, tm, first, count, name, comm=None):
    T, D = n.shape
    ns, _, guc = wgu.shape
    ff = N_SHARD * guc // 2

    def body(sid_ref, x_ref, w_ref, *rest):
        rest[-1][...] = _dot(x_ref[...], w_ref[...]).astype(BF16)

    where = lambda s, sid: sid[first + s]
    w_at = (lambda s, sid: 0) if ns == 1 else where
    return _launch(
        body, name=name, grid=(count, T // tm), prefetch=(sid,),
        in_specs=[pl.BlockSpec((tm, D), lambda s, i, sid: (i, 0)),
                  pl.BlockSpec((None, D, guc), lambda s, i, sid: (w_at(s, sid), 0, 0))]
                 + ([] if gu_prev is None else [ANY]),
        out_specs=pl.BlockSpec((None, tm, guc), lambda s, i, sid: (where(s, sid) // 2, i, where(s, sid) % 2)),
        out_shape=jax.ShapeDtypeStruct((2, T, ff), BF16),
        args=(n, wgu) + (() if gu_prev is None else (gu_prev,)),
        aliases=None if gu_prev is None else {2: 0}, comm=comm)


def _matmul_nn(x, w, *, tm, nb, w_spec, out_shape, out_spec, name, comm=None):
    T, K = x.shape

    def body(x_ref, w_ref, o_ref):
        o_ref[...] = _dot(x_ref[...], w_ref[...]).astype(o_ref.dtype)

    return _launch(
        body, name=name, grid=(nb, T // tm),
        in_specs=[pl.BlockSpec((tm, K), lambda s, i: (i, 0)), w_spec],
        out_specs=out_spec, out_shape=out_shape, args=(x, w), comm=comm)


def _mix_in(n, w_main, w_fg, *, tm, nb, name):
    T, D = n.shape
    n_main = w_main.shape[1]
    bw = n_main // nb
    W = w_fg.shape[1]

    def body(x_ref, w_ref, wf_ref, o_ref, fg_ref):
        x = x_ref[...]
        o_ref[...] = _dot(x, w_ref[...]).astype(BF16)

        @pl.when(pl.program_id(1) == 0)
        def _():
            fg_ref[...] = _dot(x, wf_ref[...])

    res, _ = _launch(
        body, name=name, grid=(T // tm, nb),
        in_specs=[pl.BlockSpec((tm, D), lambda i, s: (i, 0)), pl.BlockSpec((D, bw), lambda i, s: (0, s)),
                  pl.BlockSpec((D, W), lambda i, s: (0, 0))],
        out_specs=[pl.BlockSpec((tm, bw), lambda i, s: (i, s)), pl.BlockSpec((tm, W), lambda i, s: (i, 0))],
        out_shape=[jax.ShapeDtypeStruct((T, n_main), BF16), jax.ShapeDtypeStruct((T, W), F32)],
        args=(n, w_main, w_fg))
    return res


def _down_in_bands(g_ref, u_ref, wd_v, edges, chunks, finish):
    def down(rows):
        def act(c0, cw):
            return _silu_mul(g_ref[rows, c0:c0 + cw].astype(F32), u_ref[rows, c0:c0 + cw].astype(F32)).astype(BF16)

        acc = None
        nxt = act(*chunks[0])
        for k, (c0, cw) in enumerate(chunks):
            a = nxt
            if k + 1 < len(chunks):
                nxt = act(*chunks[k + 1])
            d = _dot(a, wd_v[c0:c0 + cw, :])
            acc = d if acc is None else acc + d
        return acc

    bands = [slice(r0, r1) for r0, r1 in zip(edges[:-1], edges[1:])]
    nxt = down(bands[0])
    for b, rows in enumerate(bands):
        acc = nxt
        if b + 1 < len(bands):
            nxt = down(bands[b + 1])
        finish(rows, acc)


def _ffn_down(gu, wd, h, next_gain, *, tm, name, comm=None):
    _, T, ff = gu.shape
    D = h.shape[1]
    chunks = _chunks(ff)

    def body(g_ref, u_ref, wd_hbm, h_ref, ng_ref, o_ref, n_ref, wd_v, sem):
        @pl.when(pl.program_id(0) == 0)
        def _():
            cp = pltpu.make_async_copy(wd_hbm, wd_v, sem)
            cp.start()
            cp.wait()

        def finish(rows, acc):
            out = h_ref[rows, :] + 0.5 * acc
            o_ref[rows, :] = out
            n_ref[rows, :] = _rms(out, ng_ref[...])

        _down_in_bands(g_ref, u_ref, wd_v, _band_edges(tm), chunks, finish)

    return _launch(
        body, name=name, grid=(T // tm,),
        in_specs=[pl.BlockSpec((None, tm, ff), lambda i: (0, i, 0)),
                  pl.BlockSpec((None, tm, ff), lambda i: (1, i, 0)),
                  ANY,
                  pl.BlockSpec((tm, D), lambda i: (i, 0)),
                  pl.BlockSpec((1, D), lambda i: (0, 0))],
        out_specs=[pl.BlockSpec((tm, D), lambda i: (i, 0)), pl.BlockSpec((tm, D), lambda i: (i, 0))],
        out_shape=[jax.ShapeDtypeStruct((T, D), F32), jax.ShapeDtypeStruct((T, D), BF16)],
        scratch_shapes=[pltpu.VMEM((ff, D), BF16), pltpu.SemaphoreType.DMA],
        args=(gu, gu, wd, h, next_gain), comm=comm)


def _ffn_down_loss(gu, wd, h, gf, tgt, *, tm, name, comm=None):
    _, T, ff = gu.shape
    D = h.shape[1]
    B, S, _ = tgt.shape
    per_seq = (S + N_META) // tm
    body_rows = tm - N_META
    chunks = _chunks(ff)

    def body(g_ref, u_ref, wd_hbm, h_ref, gf_ref, tgt_hbm, dh_ref, dhb_ref, dg_ref, loss_ref, wd_v, tg_v, sem, tsem):
        i = pl.program_id(0)
        b, t = i // per_seq, i % per_seq

        @pl.when(i == 0)
        def _():
            cp = pltpu.make_async_copy(wd_hbm, wd_v, sem)
            cp.start()
            cp.wait()
            dg_ref[...] = jnp.zeros_like(dg_ref)
            loss_ref[...] = jnp.zeros_like(loss_ref)
            tg_v[0:N_META, :] = jnp.zeros((N_META, D), F32)

        def fetch(fn):
            @pl.when(t == 0)
            def _():
                fn(pltpu.make_async_copy(tgt_hbm.at[b, pl.ds(0, body_rows)], tg_v.at[pl.ds(N_META, body_rows)], tsem))

            @pl.when(t != 0)
            def _():
                fn(pltpu.make_async_copy(tgt_hbm.at[b, pl.ds(pl.multiple_of(t * tm - N_META, 8), tm)], tg_v, tsem))

        fetch(lambda cp: cp.start())

        def finish(rows, acc):
            if rows.start == 0:
                fetch(lambda cp: cp.wait())
            x = h_ref[rows, :] + 0.5 * acc
            gain = gf_ref[...]
            r = lax.rsqrt(jnp.mean(x * x, axis=-1, keepdims=True) + EPS)
            y = x * r
            pos = t * tm + rows.start + lax.broadcasted_iota(jnp.int32, (rows.stop - rows.start, 1), 0)
            err = jnp.where(pos >= N_META, y * gain - tg_v[rows, :], 0.0)
            loss_ref[...] += 0.5 * jnp.sum(jnp.mean(err * err, axis=-1, keepdims=True))
            dout = err / D
            dg_ref[...] += jnp.sum(dout * y, axis=0, keepdims=True)
            dy = dout * gain
            dh = r * (dy - y * jnp.mean(dy * y, axis=-1, keepdims=True))
            dh_ref[rows, :] = dh
            dhb_ref[rows, :] = (0.5 * dh).astype(BF16)

        _down_in_bands(g_ref, u_ref, wd_v, _band_edges(tm), chunks, finish)

    row = pl.BlockSpec((tm, D), lambda i: (i, 0))
    const = lambda i: (0, 0)
    return _launch(
        body, name=name, grid=(T // tm,),
        in_specs=[pl.BlockSpec((None, tm, ff), lambda i: (0, i, 0)),
                  pl.BlockSpec((None, tm, ff), lambda i: (1, i, 0)),
                  ANY, row, pl.BlockSpec((1, D), const), ANY],
        out_specs=[row, row, pl.BlockSpec((1, D), const), pl.BlockSpec((1, LANES), const)],
        out_shape=[jax.ShapeDtypeStruct((T, D), F32), jax.ShapeDtypeStruct((T, D), BF16),
                   jax.ShapeDtypeStruct((1, D), F32), jax.ShapeDtypeStruct((1, LANES), F32)],
        scratch_shapes=[pltpu.VMEM((ff, D), BF16), pltpu.VMEM((tm, D), F32), pltpu.SemaphoreType.DMA,
                        pltpu.SemaphoreType.DMA],
        args=(gu, gu, wd, h, gf, tgt), comm=comm)


def _ffn_bwd_act(df, gu, wd, *, tm, guc, name, comm=None):
    _, T, ff = gu.shape
    D = df.shape[1]
    nj = ff // guc
    chunks = _chunks(guc, 768)

    def body(df_ref, g_ref, u_ref, wd_ref, o_ref, dwd_ref):
        @pl.when(pl.program_id(1) == 0)
        def _():
            dwd_ref[...] = jnp.zeros_like(dwd_ref)

        dfv = df_ref[...]
        nxt = _dot_nt(dfv, wd_ref[chunks[0][0]:chunks[0][0] + chunks[0][1], :])
        for k, (c0, cw) in enumerate(chunks):
            da = nxt
            if k + 1 < len(chunks):
                n0, nw = chunks[k + 1]
                nxt = _dot_nt(dfv, wd_ref[n0:n0 + nw, :])
            g = g_ref[:, c0:c0 + cw].astype(F32)
            u = u_ref[:, c0:c0 + cw].astype(F32)
            sg = jax.nn.sigmoid(g)
            silu = g * sg
            o_ref[0, :, c0:c0 + cw] = (da * u * (sg * (1.0 + g * (1.0 - sg)))).astype(BF16)
            o_ref[1, :, c0:c0 + cw] = (da * silu).astype(BF16)
            dwd_ref[c0:c0 + cw, :] += _dot_tn((silu * u).astype(BF16), dfv)

    return _launch(
        body, name=name, grid=(nj, T // tm),
        in_specs=[pl.BlockSpec((tm, D), lambda j, i: (i, 0)),
                  pl.BlockSpec((None, tm, guc), lambda j, i: (0, i, j)),
                  pl.BlockSpec((None, tm, guc), lambda j, i: (1, i, j)),
                  pl.BlockSpec((guc, D), lambda j, i: (j, 0))],
        out_specs=[pl.BlockSpec((2, tm, guc), lambda j, i: (0, i, j)), pl.BlockSpec((guc, D), lambda j, i: (j, 0))],
        out_shape=[jax.ShapeDtypeStruct((2, T, ff), BF16), jax.ShapeDtypeStruct((ff, D), F32)],
        args=(df, gu, gu, wd), comm=comm)


def _ffn_bwd_in(dgu, wgu, h, g, dres, *, tm, scale, name, comm=None):
    _, T, ff = dgu.shape
    ns, D, guc = wgu.shape
    nj = ff // guc
    edges = _band_edges(tm)

    def body(dgu_ref, w_hbm, h_ref, g_ref, dres_ref, dh_ref, dhb_ref, dg_ref, w_v, acc, sem):
        i, j = pl.program_id(0), pl.program_id(1)

        @pl.when((i == 0) & (j == 0))
        def _():
            cp = pltpu.make_async_copy(w_hbm, w_v, sem)
            cp.start()
            cp.wait()
            dg_ref[...] = jnp.zeros_like(dg_ref)

        def dots(rows):
            return _dot_nt(dgu_ref[0, rows, :], w_v[j]) + _dot_nt(dgu_ref[1, rows, :], w_v[nj + j])

        @pl.when(j < nj - 1)
        def _():
            part = dots(slice(None))

            @pl.when(j == 0)
            def _():
                acc[...] = part

            @pl.when(j > 0)
            def _():
                acc[...] += part

        @pl.when(j == nj - 1)
        def _():
            bands = [slice(r0, r1) for r0, r1 in zip(edges[:-1], edges[1:])]
            nxt = dots(bands[0])
            for b, rows in enumerate(bands):
                dn = nxt if nj == 1 else acc[rows, :] + nxt
                if b + 1 < len(bands):
                    nxt = dots(bands[b + 1])
                dh, dgain = _rms_bwd(dn, h_ref[rows, :], g_ref[...], dres_ref[rows, :])
                dh_ref[rows, :] = dh
                dhb_ref[rows, :] = (scale * dh).astype(BF16)
                dg_ref[...] += dgain

    return _launch(
        body, name=name, grid=(T // tm, nj),
        in_specs=[pl.BlockSpec((2, tm, guc), lambda i, j: (0, i, j)),
                  ANY,
                  pl.BlockSpec((tm, D), lambda i, j: (i, 0)),
                  pl.BlockSpec((1, D), lambda i, j: (0, 0)),
                  pl.BlockSpec((tm, D), lambda i, j: (i, 0))],
        out_specs=[pl.BlockSpec((tm, D), lambda i, j: (i, 0)),
                   pl.BlockSpec((tm, D), lambda i, j: (i, 0)),
                   pl.BlockSpec((1, D), lambda i, j: (0, 0))],
        out_shape=[jax.ShapeDtypeStruct((T, D), F32), jax.ShapeDtypeStruct((T, D), BF16),
                   jax.ShapeDtypeStruct((1, D), F32)],
        scratch_shapes=[pltpu.VMEM((ns, D, guc), BF16), pltpu.VMEM((tm, D), F32), pltpu.SemaphoreType.DMA],
        args=(dgu, wgu, h, g, dres), comm=comm)


def _ffn_bwd_in_first(dgu, wgu, h, g, dres, *, tm, batch, name, comm=None):
    _, T, ff = dgu.shape
    ns, D, guc = wgu.shape
    nj = ff // guc
    nt = T // tm
    L = T // batch
    per_seq = L // tm
    body_rows = tm - N_META
    edges = _band_edges(tm)

    def body(dgu_ref, w_hbm, h_ref, g_ref, dres_ref, dx_hbm, dmeta_ref, dg_ref, w_v, acc, dh_v, sem, osem):
        i, j = pl.program_id(0), pl.program_id(1)

        @pl.when((i == 0) & (j == 0))
        def _():
            cp = pltpu.make_async_copy(w_hbm, w_v, sem)
            cp.start()
            cp.wait()
            dg_ref[...] = jnp.zeros_like(dg_ref)
            dmeta_ref[...] = jnp.zeros_like(dmeta_ref)

        def dots(rows):
            return _dot_nt(dgu_ref[0, rows, :], w_v[j]) + _dot_nt(dgu_ref[1, rows, :], w_v[nj + j])

        @pl.when(j < nj - 1)
        def _():
            part = dots(slice(None))

            @pl.when(j == 0)
            def _():
                acc[...] = part

            @pl.when(j > 0)
            def _():
                acc[...] += part

        def head_copy(b):
            return pltpu.make_async_copy(dh_v.at[pl.ds(N_META, body_rows)], dx_hbm.at[b, pl.ds(0, body_rows)], osem)

        def tail_copy(b, t):
            return pltpu.make_async_copy(dh_v, dx_hbm.at[b, pl.ds(pl.multiple_of(t * tm - N_META, 8), tm)], osem)

        def on_tile(k, head_fn, tail_fn):
            @pl.when(k % per_seq == 0)
            def _():
                head_fn(head_copy(k // per_seq))

            @pl.when(k % per_seq != 0)
            def _():
                tail_fn(tail_copy(k // per_seq, k % per_seq))

        @pl.when(j == nj - 1)
        def _():
            @pl.when(i > 0)
            def _():
                on_tile(i - 1, lambda cp: cp.wait(), lambda cp: cp.wait())

            bands = [slice(r0, r1) for r0, r1 in zip(edges[:-1], edges[1:])]
            nxt = dots(bands[0])
            for b, rows in enumerate(bands):
                dn = nxt if nj == 1 else acc[rows, :] + nxt
                if b + 1 < len(bands):
                    nxt = dots(bands[b + 1])
                dh, dgain = _rms_bwd(dn, h_ref[rows, :], g_ref[...], dres_ref[rows, :])
                dg_ref[...] += dgain
                dh_v[rows, :] = dh
                if b == 0:
                    @pl.when(i % per_seq == 0)
                    def _():
                        dmeta_ref[...] += dh[0:N_META, :]

            on_tile(i, lambda cp: cp.start(), lambda cp: cp.start())

            @pl.when(i == nt - 1)
            def _():
                on_tile(i, lambda cp: cp.wait(), lambda cp: cp.wait())

    return _launch(
        body, name=name, grid=(nt, nj),
        in_specs=[pl.BlockSpec((2, tm, guc), lambda i, j: (0, i, j)),
                  ANY,
                  pl.BlockSpec((tm, D), lambda i, j: (i, 0)),
                  pl.BlockSpec((1, D), lambda i, j: (0, 0)),
                  pl.BlockSpec((tm, D), lambda i, j: (i, 0))],
        out_specs=[ANY, pl.BlockSpec((N_META, D), lambda i, j: (0, 0)), pl.BlockSpec((1, D), lambda i, j: (0, 0))],
        out_shape=[jax.ShapeDtypeStruct((batch, L - N_META, D), F32), jax.ShapeDtypeStruct((N_META, D), F32),
                   jax.ShapeDtypeStruct((1, D), F32)],
        scratch_shapes=[pltpu.VMEM((ns, D, guc), BF16), pltpu.VMEM((tm, D), F32), pltpu.VMEM((tm, D), F32),
                        pltpu.SemaphoreType.DMA, pltpu.SemaphoreType.DMA],
        args=(dgu, wgu, h, g, dres), comm=comm)


def _mix_bwd_in(parts, w_main, w_fg, h, g, dres, *, tm, scale, name, comm=None):
    T, D = h.shape
    widths = [p.shape[1] for p in parts]
    offs = [sum(widths[:k]) for k in range(len(widths))]
    npart = len(parts)
    wide = sum(widths)
    edges = _band_edges(tm)

    def body(*refs):
        p_refs = refs[:npart]
        wm_ref, wf_ref, h_ref, g_ref, dres_ref, dh_ref, dhb_ref, dg_ref, all_ref = refs[npart:]

        @pl.when(pl.program_id(0) == 0)
        def _():
            dg_ref[...] = jnp.zeros_like(dg_ref)

        for p_ref, off, wd_ in zip(p_refs, offs, widths):
            for c0, cw in _chunks(wd_):
                all_ref[:, off + c0:off + c0 + cw] = p_ref[:, c0:c0 + cw].astype(BF16)
        n_main = offs[-1]

        def dots(rows):
            return _dot_nt(all_ref[rows, :n_main], wm_ref[...]) + _dot_nt(all_ref[rows, n_main:], wf_ref[...])

        bands = [slice(r0, r1) for r0, r1 in zip(edges[:-1], edges[1:])]
        nxt = dots(bands[0])
        for b, rows in enumerate(bands):
            dn = nxt
            if b + 1 < len(bands):
                nxt = dots(bands[b + 1])
            dh, dgain = _rms_bwd(dn, h_ref[rows, :], g_ref[...], dres_ref[rows, :])
            dh_ref[rows, :] = dh
            dhb_ref[rows, :] = (scale * dh).astype(BF16)
            dg_ref[...] += dgain

    row = lambda i: (i, 0)
    const = lambda i: (0, 0)
    return _launch(
        body, name=name, grid=(T // tm,),
        in_specs=[pl.BlockSpec((tm, p.shape[1]), row) for p in parts]
                 + [pl.BlockSpec(w_main.shape, const), pl.BlockSpec(w_fg.shape, const),
                    pl.BlockSpec((tm, D), row), pl.BlockSpec((1, D), const), pl.BlockSpec((tm, D), row)],
        out_specs=[pl.BlockSpec((tm, D), row), pl.BlockSpec((tm, D), row), pl.BlockSpec((1, D), const),
                   pl.BlockSpec((tm, wide), row)],
        out_shape=[jax.ShapeDtypeStruct((T, D), F32), jax.ShapeDtypeStruct((T, D), BF16),
                   jax.ShapeDtypeStruct((1, D), F32), jax.ShapeDtypeStruct((T, wide), BF16)],
        args=(*parts, w_main, w_fg, h, g, dres), comm=comm)


def _matmul_tn(x, y, *, tm, nb, x_spec, y_spec, out_shape, out_spec, kb, name, comm=None):
    T = y.shape[-2]
    chunks = _chunks(kb)

    def body(x_ref, y_ref, o_ref):
        @pl.when(pl.program_id(1) == 0)
        def _():
            o_ref[...] = jnp.zeros_like(o_ref)

        yv = y_ref[...].astype(BF16)
        nxt = _dot_tn(x_ref[:, chunks[0][0]:chunks[0][0] + chunks[0][1]], yv)
        for k, (c0, cw) in enumerate(chunks):
            cur = nxt
            if k + 1 < len(chunks):
                n0, nw = chunks[k + 1]
                nxt = _dot_tn(x_ref[:, n0:n0 + nw], yv)
            o_ref[c0:c0 + cw, :] += cur

    return _launch(
        body, name=name, grid=(nb, T // tm),
        in_specs=[x_spec, y_spec], out_specs=out_spec, out_shape=out_shape, args=(x, y), comm=comm)


def _tri(n, lower):
    r = lax.broadcasted_iota(jnp.int32, (n, n), 0)
    c = lax.broadcasted_iota(jnp.int32, (n, n), 1)
    return jnp.where((r >= c) if lower else (r <= c), 1.0, 0.0).astype(BF16)


def _tri_dot(tri, v):
    hi, mid, lo = _split3(v)
    return _dot(tri, hi) + _dot(tri, mid) + _dot(tri, lo)


def _fcum(fg, bf, *, ch, name):
    B, L, W = fg.shape
    nch = L // ch

    def body(fg_ref, bf_ref, f_ref):
        tri = _tri(ch, True)
        carry = jnp.zeros((1, W), F32)
        for c in range(nch):
            x = fg_ref[c * ch:(c + 1) * ch, :] + bf_ref[...]
            lf = jnp.minimum(x, 0.0) - jnp.log(1.0 + jnp.exp(-jnp.abs(x)))
            f_ref[c * ch:(c + 1) * ch, :] = _tri_dot(tri, lf) + carry
            carry = carry + jnp.sum(lf, axis=0, keepdims=True)

    return pl.pallas_call(
        body, name=name, grid=(B,),
        in_specs=[pl.BlockSpec((None, L, W), lambda b: (b, 0, 0)), pl.BlockSpec((1, W), lambda b: (0, 0))],
        out_specs=pl.BlockSpec((None, L, W), lambda b: (b, 0, 0)),
        out_shape=jax.ShapeDtypeStruct((B, L, W), F32),
        compiler_params=_params("arbitrary"),
    )(fg, bf)


def _fcum_bwd(dF_rows, dF_cols, fg, bf, *, ch, name):
    B, L, W = fg.shape
    nch = L // ch

    def body(dfr_ref, dfc_ref, fg_ref, bf_ref, dfg_ref, db_ref):
        @pl.when(pl.program_id(0) == 0)
        def _():
            db_ref[...] = jnp.zeros_like(db_ref)

        tri = _tri(ch, False)
        carry = jnp.zeros((1, W), F32)
        dbs = jnp.zeros((1, W), F32)
        for c in reversed(range(nch)):
            d = dfr_ref[c * ch:(c + 1) * ch, :] - dfc_ref[c * ch:(c + 1) * ch, :]
            dlf = _tri_dot(tri, d) + carry
            carry = carry + jnp.sum(d, axis=0, keepdims=True)
            x = fg_ref[c * ch:(c + 1) * ch, :] + bf_ref[...]
            dfg = dlf * jax.nn.sigmoid(-x)
            dfg_ref[c * ch:(c + 1) * ch, :] = dfg.astype(BF16)
            dbs = dbs + jnp.sum(dfg, axis=0, keepdims=True)
        db_ref[...] += dbs

    blk = pl.BlockSpec((None, L, W), lambda b: (b, 0, 0))
    return pl.pallas_call(
        body, name=name, grid=(B,),
        in_specs=[blk, blk, blk, pl.BlockSpec((1, W), lambda b: (0, 0))],
        out_specs=[blk, pl.BlockSpec((1, W), lambda b: (0, 0))],
        out_shape=[jax.ShapeDtypeStruct((B, L, W), BF16), jax.ShapeDtypeStruct((1, W), F32)],
        compiler_params=_params("arbitrary"),
    )(dF_rows, dF_cols, fg, bf)


def _band_edges(tq):
    return sorted({min(tq, (k * tq // ROW_BANDS + HALO - 1) // HALO * HALO) for k in range(ROW_BANDS + 1)})


def _pair(h):
    return slice((h // 2) * 2 * HEAD_DIM, (h // 2 + 1) * 2 * HEAD_DIM)


def _own_lanes(a, h):
    low = lax.broadcasted_iota(jnp.int32, a.shape, 1) < HEAD_DIM
    return jnp.where(low if h % 2 == 0 else jnp.logical_not(low), a, jnp.zeros_like(a))


def _sum_lane(h):
    return HEAD_DIM if h % 2 == 0 else 0


def _own_lanes_and_ones(a, h):
    lane = lax.broadcasted_iota(jnp.int32, a.shape, 1)
    low = lane < HEAD_DIM
    return jnp.where(low if h % 2 == 0 else jnp.logical_not(low), a,
                     jnp.where(lane == _sum_lane(h), jnp.ones_like(a), jnp.zeros_like(a)))


def _attn_fwd(proj, fr, *, tq, n_heads, name, comm=None):
    B, L, _ = proj.shape
    AD = n_heads * HEAD_DIM
    nq = L // tq
    W = LANES
    scale = HEAD_DIM ** -0.5
    edges = _band_edges(tq)

    v_ones, sum_lane = _own_lanes_and_ones, _sum_lane

    def body(q_ref, k_ref, v_ref, fr_ref, o_ref, lse_ref, m_s, acc_s):
        qi, ki = pl.program_id(1), pl.program_id(2)

        @pl.when(ki == 0)
        def _():
            m_s[...] = jnp.full_like(m_s, NEG)
            acc_s[...] = jnp.zeros_like(acc_s)

        def tile(diagonal):
            lane = lax.broadcasted_iota(jnp.int32, (tq, W), 1)
            m_all = m_s[...]
            m_out = m_all
            bands = [(r0, r1, r1 if diagonal else tq) for r0, r1 in zip(edges[:-1], edges[1:])]
            if diagonal:
                masks = {r0: (lax.broadcasted_iota(jnp.int32, (r1 - r0, c1), 1)
                              <= r0 + lax.broadcasted_iota(jnp.int32, (r1 - r0, c1), 0)) for r0, r1, c1 in bands}

            def scores(h, band):
                r0, r1, c1 = band
                sl = slice(h * HEAD_DIM, (h + 1) * HEAD_DIM)
                return _dot_nt(q_ref[r0:r1, sl] * scale, k_ref[0:c1, sl])

            work = [(h, band) for h in range(n_heads) for band in bands]
            nxt = scores(*work[0])
            for w, (h, band) in enumerate(work):
                r0, r1, c1 = band
                sl = slice(h * HEAD_DIM, (h + 1) * HEAD_DIM)
                s = nxt - fr_ref[h:h + 1, 0:c1]
                if w + 1 < len(work):
                    nxt = scores(*work[w + 1])
                if diagonal:
                    s = jnp.where(masks[r0], s, NEG)
                m_old = m_all[r0:r1, h:h + 1]
                m_new = jnp.maximum(m_old, jnp.max(s, axis=1, keepdims=True))
                alpha = jnp.exp(m_old - m_new)
                p = jnp.exp(s - m_new)
                own = slice(h * 2 * HEAD_DIM, (h + 1) * 2 * HEAD_DIM)
                acc_s[r0:r1, own] = alpha * acc_s[r0:r1, own] + _dot(p.astype(BF16), v_ones(v_ref[0:c1, _pair(h)], h))
                if r0 == 0:
                    m_parts = []
                m_parts.append(m_new)
                if r1 == tq:
                    m_out = jnp.where(lane == h, jnp.concatenate(m_parts, axis=0), m_out)
            m_s[...] = m_out

        @pl.when(ki < qi)
        def _():
            tile(False)

        @pl.when(ki == qi)
        def _():
            tile(True)
            lane = lax.broadcasted_iota(jnp.int32, (tq, W), 1)
            low = lax.broadcasted_iota(jnp.int32, (tq, 2 * HEAD_DIM), 1) < HEAD_DIM
            l_all = jnp.ones((tq, W), F32)
            for h in range(0, n_heads, 2):
                even = acc_s[:, h * 2 * HEAD_DIM:(h + 1) * 2 * HEAD_DIM]
                odd = acc_s[:, (h + 1) * 2 * HEAD_DIM:(h + 2) * 2 * HEAD_DIM]
                l_even = even[:, sum_lane(h):sum_lane(h) + 1]
                l_odd = odd[:, sum_lane(h + 1):sum_lane(h + 1) + 1]
                o_ref[:, _pair(h)] = jnp.where(low, even / l_even, odd / l_odd)
                l_all = jnp.where(lane == h, l_even, jnp.where(lane == h + 1, l_odd, l_all))
            lse_ref[...] = jnp.where(lane < n_heads, m_s[...] + jnp.log(l_all), 0.0)

    kv = lambda b, qi, ki: jnp.minimum(ki, qi)
    return _launch(
        body, name=name, grid=(B, nq, nq), args=(proj, proj, proj, fr), comm=comm,
        in_specs=[pl.BlockSpec((None, tq, AD), lambda b, qi, ki: (b, qi, 3)),
                  pl.BlockSpec((None, tq, AD), lambda b, qi, ki: (b, kv(b, qi, ki), 4)),
                  pl.BlockSpec((None, tq, AD), lambda b, qi, ki: (b, kv(b, qi, ki), 5)),
                  pl.BlockSpec((None, None, n_heads, tq), lambda b, qi, ki: (b, kv(b, qi, ki), 0, 0))],
        out_specs=[pl.BlockSpec((None, tq, AD), lambda b, qi, ki: (b, qi, 0)),
                   pl.BlockSpec((None, tq, W), lambda b, qi, ki: (b, qi, 0))],
        out_shape=[jax.ShapeDtypeStruct((B, L, AD), F32), jax.ShapeDtypeStruct((B, L, W), F32)],
        scratch_shapes=[pltpu.VMEM((tq, W), F32), pltpu.VMEM((tq, n_heads * 2 * HEAD_DIM), F32)])


def _attn_bwd(proj, o, do, lse, fr, *, tq, n_heads, name, comm=None):
    B, L, _ = proj.shape
    AD = n_heads * HEAD_DIM
    nq = L // tq
    W = LANES
    HW = 2 * HEAD_DIM
    scale = HEAD_DIM ** -0.5
    edges = _band_edges(tq)

    def body(q_ref, k_ref, v_ref, o_ref, do_ref, lse_ref, fr_ref,
             dq_ref, dk_ref, dv_ref, dfk_ref, dfq_ref, dq_s, dk_s, dv_s):
        kj, qi = pl.program_id(1), pl.program_id(2)

        @pl.when((kj == 0) & (qi == 0))
        def _():
            dq_s[...] = jnp.zeros_like(dq_s)

        @pl.when(qi == kj)
        def _():
            dk_s[...] = jnp.zeros_like(dk_s)
            dv_s[...] = jnp.zeros_like(dv_s)

        def tile(diagonal):
            bands = [(r0, r1, r1) for r0, r1 in zip(edges[:-1], edges[1:])] if diagonal else [(0, tq, tq)]
            lse = lse_ref[...]
            for r0, r1, c1 in bands:
                nr = r1 - r0
                rows = pl.ds(pl.multiple_of(qi * tq + r0, 8), nr)
                if diagonal:
                    mask = (lax.broadcasted_iota(jnp.int32, (nr, c1), 1)
                            <= r0 + lax.broadcasted_iota(jnp.int32, (nr, c1), 0))
                def scores(h):
                    ps = _pair(h)
                    k = k_ref[0:c1, ps]
                    qs = q_ref[r0:r1, ps] * scale
                    dov = _own_lanes(do_ref[r0:r1, ps], h)
                    return _dot_nt(_own_lanes(qs, h), k), _dot_nt(dov, v_ref[0:c1, ps]), k, qs, dov

                nxt = scores(0)
                for h in range(n_heads):
                    ps = _pair(h)
                    own = slice(h * HW, (h + 1) * HW)
                    s, dp, k, qs, dov = nxt
                    if h + 1 < n_heads:
                        nxt = scores(h + 1)
                    s = s - fr_ref[h:h + 1, 0:c1]
                    if diagonal:
                        s = jnp.where(mask, s, NEG)
                    p = jnp.exp(s - lse[r0:r1, h:h + 1])
                    dsum = jnp.sum(dov.astype(F32) * o_ref[r0:r1, ps], axis=1, keepdims=True)
                    dsb = (p * (dp - dsum)).astype(BF16)
                    dv = _dot_tn(p.astype(BF16), dov)
                    dk_s[0:c1, own] += _dot_tn(dsb, _own_lanes_and_ones(qs, h))
                    dq_s[rows, own] += _dot(dsb, _own_lanes_and_ones(k, h))
                    if h % 2 == 0:
                        dv_even = dv
                    else:
                        dv_s[0:c1, ps] += dv_even + dv

        def compact(acc, data_scale):
            rows = acc.shape[0]
            low = lax.broadcasted_iota(jnp.int32, (rows, HW), 1) < HEAD_DIM
            lane = lax.broadcasted_iota(jnp.int32, (rows, W), 1)
            vals, sums = [], jnp.zeros((rows, W), F32)
            for h in range(0, n_heads, 2):
                even, odd = acc[:, h * HW:(h + 1) * HW], acc[:, (h + 1) * HW:(h + 2) * HW]
                vals.append(jnp.where(low, even, odd) * data_scale)
                sums = jnp.where(lane == h, even[:, _sum_lane(h):_sum_lane(h) + 1],
                                 jnp.where(lane == h + 1, odd[:, _sum_lane(h + 1):_sum_lane(h + 1) + 1], sums))
            return vals, sums

        @pl.when(qi > kj)
        def _():
            tile(False)

        @pl.when(qi == kj)
        def _():
            tile(True)
            rows = pl.ds(pl.multiple_of(qi * tq, 8), tq)
            vals, sums = compact(dq_s[rows, :], scale)
            for h in range(0, n_heads, 2):
                dq_ref[rows, _pair(h)] = vals[h // 2]
            dfq_ref[rows, :] = sums

        @pl.when(qi == nq - 1)
        def _():
            vals, sums = compact(dk_s[...], 1.0)
            for h in range(0, n_heads, 2):
                dk_ref[:, _pair(h)] = vals[h // 2].astype(BF16)
            dfk_ref[...] = sums
            dv_ref[...] = dv_s[...].astype(BF16)

    qq = lambda b, kj, qi: jnp.maximum(qi, kj)
    qblk = lambda w, cb: pl.BlockSpec((None, tq, w), lambda b, kj, qi: (b, qq(b, kj, qi), cb))
    kblk = lambda w, cb: pl.BlockSpec((None, tq, w), lambda b, kj, qi: (b, kj, cb))
    return _launch(
        body, name=name, grid=(B, nq, nq), args=(proj, proj, proj, o, do, lse, fr), comm=comm,
        in_specs=[qblk(AD, 3), kblk(AD, 4), kblk(AD, 5), qblk(AD, 0), qblk(AD, 0), qblk(W, 0),
                  pl.BlockSpec((None, None, n_heads, tq), lambda b, kj, qi: (b, kj, 0, 0))],
        out_specs=[pl.BlockSpec((None, L, AD), lambda b, kj, qi: (b, 0, 0)),
                   kblk(AD, 0), kblk(AD, 0), kblk(W, 0),
                   pl.BlockSpec((None, L, W), lambda b, kj, qi: (b, 0, 0))],
        out_shape=[jax.ShapeDtypeStruct((B, L, AD), F32), jax.ShapeDtypeStruct((B, L, AD), BF16),
                   jax.ShapeDtypeStruct((B, L, AD), BF16), jax.ShapeDtypeStruct((B, L, W), F32),
                   jax.ShapeDtypeStruct((B, L, W), F32)],
        scratch_shapes=[pltpu.VMEM((L, n_heads * HW), F32), pltpu.VMEM((tq, n_heads * HW), F32),
                        pltpu.VMEM((tq, AD), F32)])


def _mix_gather(refs, first):
    b_ref, c_ref, hc_ref, cp_ref, hcp_ref, o_ref, cw_ref, p_ref = refs
    bg = b_ref[...].astype(F32)
    u = c_ref[...].astype(F32) * hc_ref[...].astype(F32)
    prev = cp_ref[...].astype(F32) * hcp_ref[...].astype(F32)
    prev = jnp.where(first, 0.0, prev)
    cv, u1, u2 = _causal_conv(u, prev, cw_ref[...])
    yc = bg * cv
    p = p_ref[...]
    rc = lax.rsqrt(_group_mean(yc * yc, p) + EPS)
    ya = o_ref[...].astype(F32)
    ra = lax.rsqrt(_group_mean(ya * ya, p) + EPS)
    return bg, (u, u1, u2), cv, yc * rc, rc, ya * ra, ra


def _mix_specs(tm, CD):
    per = tm // HALO
    cur = lambda cb: pl.BlockSpec((None, tm, CD), lambda b, i: (b, i, cb))
    prev = lambda cb: pl.BlockSpec((None, HALO, CD), lambda b, i: (b, jnp.maximum(i * per - 1, 0), cb))
    return [cur(0), cur(1), cur(2), prev(1), prev(2), cur(0)]


def _mix_out(proj, o, cw, gc, ga, wout, h, pmat, next_gain, *, tm, name, comm=None):
    B, L, D = h.shape
    CD = o.shape[-1]
    const = lambda b, i: (0, 0)

    def body(b_ref, c_ref, hc_ref, cp_ref, hcp_ref, o_ref, cw_ref, p_ref, gc_ref, ga_ref, w_ref, h_ref, ng_ref,
             out_ref, y_ref, n_ref):
        first = pl.program_id(1) == 0
        _, _, _, zc, _, za, _ = _mix_gather((b_ref, c_ref, hc_ref, cp_ref, hcp_ref, o_ref, cw_ref, p_ref), first)
        yc = (zc * gc_ref[...]).astype(BF16)
        ya = (za * ga_ref[...]).astype(BF16)
        y_ref[:, :CD] = yc
        y_ref[:, CD:] = ya
        out = h_ref[...] + _dot(yc, w_ref[:CD, :]) + _dot(ya, w_ref[CD:, :])
        out_ref[...] = out
        n_ref[...] = _rms(out, ng_ref[...])

    tile = pl.BlockSpec((None, tm, D), lambda b, i: (b, i, 0))
    return _launch(
        body, name=name, grid=(B, L // tm),
        in_specs=_mix_specs(tm, CD)
                 + [pl.BlockSpec(cw.shape, const), pl.BlockSpec(pmat.shape, const),
                    pl.BlockSpec((1, CD), const), pl.BlockSpec((1, CD), const), pl.BlockSpec((D, D), const),
                    tile, pl.BlockSpec((1, D), const)],
        out_specs=[tile, tile, tile],
        out_shape=[jax.ShapeDtypeStruct((B, L, D), F32), jax.ShapeDtypeStruct((B, L, D), BF16),
                   jax.ShapeDtypeStruct((B, L, D), BF16)],
        args=(proj, proj, proj, proj, proj, o, cw, pmat, gc, ga, wout, h, next_gain), comm=comm)


def _mix_out_bwd(dhb, proj, o, cw, gc, ga, wout, pmat, *, tm, name, comm=None):
    B, L, D = dhb.shape
    CD = o.shape[-1]
    const = lambda b, i: (0, 0)

    def body(dh_ref, b_ref, c_ref, hc_ref, cp_ref, hcp_ref, o_ref, cw_ref, p_ref, gc_ref, ga_ref, w_ref,
             db_ref, dcv_ref, do_ref, dgc_ref, dga_ref, dcw_ref):
        first = pl.program_id(1) == 0

        @pl.when((pl.program_id(0) == 0) & first)
        def _():
            dgc_ref[...] = jnp.zeros_like(dgc_ref)
            dga_ref[...] = jnp.zeros_like(dga_ref)
            dcw_ref[...] = jnp.zeros_like(dcw_ref)

        bg, us, cv, zc, rc, za, ra = _mix_gather(
            (b_ref, c_ref, hc_ref, cp_ref, hcp_ref, o_ref, cw_ref, p_ref), first)
        p = p_ref[...]
        dh = dh_ref[...]
        dyc = _dot_nt(dh, w_ref[:CD, :])
        dya = _dot_nt(dh, w_ref[CD:, :])

        dgc_ref[...] += jnp.sum(dyc * zc, axis=0, keepdims=True)
        dz = dyc * gc_ref[...]
        dx = rc * (dz - zc * _group_mean(dz * zc, p))
        db_ref[...] = (dx * cv).astype(BF16)
        dcv = dx * bg
        dcv_ref[...] = dcv.astype(BF16)
        for k in range(3):
            dcw_ref[k:k + 1, :] += jnp.sum(dcv * us[2 - k], axis=0, keepdims=True)

        dga_ref[...] += jnp.sum(dya * za, axis=0, keepdims=True)
        dz = dya * ga_ref[...]
        do_ref[...] = (ra * (dz - za * _group_mean(dz * za, p))).astype(BF16)

    tile = lambda w: pl.BlockSpec((None, tm, w), lambda b, i: (b, i, 0))
    return _launch(
        body, name=name, grid=(B, L // tm), comm=comm,
        args=(dhb, proj, proj, proj, proj, proj, o, cw, pmat, gc, ga, wout),
        in_specs=[tile(D)] + _mix_specs(tm, CD)
                 + [pl.BlockSpec(cw.shape, const), pl.BlockSpec(pmat.shape, const),
                    pl.BlockSpec((1, CD), const), pl.BlockSpec((1, CD), const), pl.BlockSpec((D, D), const)],
        out_specs=[tile(CD), tile(CD), tile(CD),
                   pl.BlockSpec((1, CD), const), pl.BlockSpec((1, CD), const), pl.BlockSpec((8, CD), const)],
        out_shape=[jax.ShapeDtypeStruct((B, L, CD), BF16)] * 3
                  + [jax.ShapeDtypeStruct((1, CD), F32)] * 2 + [jax.ShapeDtypeStruct((8, CD), F32)])


def _conv_bwd(dcv, proj, cw, *, tm, name):
    B, L, CD = dcv.shape
    per = tm // HALO
    nhalo = L // HALO
    nt = L // tm

    def body(d_ref, dn_ref, c_ref, hc_ref, cw_ref, out_ref):
        last = pl.program_id(1) == nt - 1
        d = d_ref[...].astype(F32)
        nxt = jnp.where(last, 0.0, dn_ref[...].astype(F32))
        n0, n1 = _row_of(nxt, 0), _row_of(nxt, 1)
        rows = lax.broadcasted_iota(jnp.int32, d.shape, 0)
        d1 = jnp.where(rows == tm - 1, n0, pltpu.roll(d, tm - 1, 0))
        d2 = jnp.where(rows == tm - 2, n0, jnp.where(rows == tm - 1, n1, pltpu.roll(d, tm - 2, 0)))
        w = cw_ref[...]
        du = w[2:3, :] * d + w[1:2, :] * d1 + w[0:1, :] * d2
        out_ref[:, :CD] = (du * hc_ref[...].astype(F32)).astype(BF16)
        out_ref[:, CD:] = (du * c_ref[...].astype(F32)).astype(BF16)

    return pl.pallas_call(
        body, name=name, grid=(B, nt),
        in_specs=[pl.BlockSpec((None, tm, CD), lambda b, i: (b, i, 0)),
                  pl.BlockSpec((None, HALO, CD), lambda b, i: (b, jnp.minimum((i + 1) * per, nhalo - 1), 0)),
                  pl.BlockSpec((None, tm, CD), lambda b, i: (b, i, 1)),
                  pl.BlockSpec((None, tm, CD), lambda b, i: (b, i, 2)),
                  pl.BlockSpec(cw.shape, lambda b, i: (0, 0))],
        out_specs=pl.BlockSpec((None, tm, 2 * CD), lambda b, i: (b, i, 0)),
        out_shape=jax.ShapeDtypeStruct((B, L, 2 * CD), BF16),
        compiler_params=_params("arbitrary", "arbitrary"),
    )(dcv, dcv, proj, proj, cw)


def _place():
    x, y, c = lax.axis_index("x"), lax.axis_index("y"), lax.axis_index("c")
    others = [(1 - x, y), (x, 1 - y), (1 - x, 1 - y)]
    return x, y, c, others


def _all_gather_shards(shards, *, name):
    n = len(shards)

    def body(*refs):
        ins, outs = refs[:n], refs[n:2 * n]
        send, recv, fsend, frecv, lsem = refs[2 * n:]
        x, y, c, others = _place()
        me = 2 * x + y
        local = [pltpu.make_async_copy(ins[t], outs[t].at[me], lsem.at[t]) for t in range(n)]
        for cp in local:
            cp.start()

        def half(t, k):
            hr = shards[t].shape[0] // 2
            return pl.ds(pl.multiple_of(k * hr, HALO), hr)

        def ici(t, j, src_chip, to):
            src = ins[t].at[half(t, c)] if to is not None else outs[t].at[src_chip, half(t, c)]
            return pltpu.make_async_remote_copy(
                src_ref=src, dst_ref=outs[t].at[src_chip, half(t, c)],
                send_sem=send.at[3 * t + j], recv_sem=recv.at[3 * t + j],
                device_id=(x, y, c) if to is None else to, device_id_type=MESH)

        def d2d(t, j, src_chip, k):
            return pltpu.make_async_remote_copy(
                src_ref=outs[t].at[src_chip, half(t, k)], dst_ref=outs[t].at[src_chip, half(t, k)],
                send_sem=fsend.at[3 * t + j], recv_sem=frecv.at[3 * t + j],
                device_id=(x, y, 1 - c), device_id_type=MESH)

        firsts = [ici(t, j, me, (ox, oy, c)) for t in range(n) for j, (ox, oy) in enumerate(others)]
        for cp in firsts:
            cp.start()
        passed = []
        for t in range(n):
            for j, (ox, oy) in enumerate(others):
                ici(t, j, 2 * ox + oy, None).wait_recv()
                cp = d2d(t, j, 2 * ox + oy, c)
                cp.start()
                passed.append(cp)
        for t in range(n):
            for j, (ox, oy) in enumerate(others):
                d2d(t, j, 2 * ox + oy, 1 - c).wait_recv()
        for cp in firsts + passed:
            cp.wait_send()
        for cp in local:
            cp.wait()

    return pl.pallas_call(
        body, name=name,
        in_specs=[ANY] * n, out_specs=[ANY] * n,
        out_shape=[jax.ShapeDtypeStruct((N_SHARD,) + s.shape, s.dtype) for s in shards],
        scratch_shapes=[pltpu.SemaphoreType.DMA((3 * n,))] * 4 + [pltpu.SemaphoreType.DMA((n,))],
    )(*shards)


def _all_reduce_small(slab, *, name):
    def body(in_ref, out_ref, gath, send, recv):
        x, y, c, _ = _place()
        me = 4 * x + 2 * y + c
        gath[me] = in_ref[...]
        copies, peers = [], []
        for m in range(1, N_DEV):
            px = jnp.where((m >> 2) & 1, 1 - x, x)
            py = jnp.where((m >> 1) & 1, 1 - y, y)
            pc = jnp.where(m & 1, 1 - c, c)
            cp = pltpu.make_async_remote_copy(
                src_ref=in_ref, dst_ref=gath.at[me], send_sem=send.at[m - 1], recv_sem=recv.at[m - 1],
                device_id=(px, py, pc), device_id_type=MESH)
            cp.start()
            copies.append(cp)
            peers.append(4 * px + 2 * py + pc)
        for m in range(1, N_DEV):
            pltpu.make_async_remote_copy(
                src_ref=in_ref, dst_ref=gath.at[peers[m - 1]], send_sem=send.at[m - 1], recv_sem=recv.at[m - 1],
                device_id=(x, y, c), device_id_type=MESH).wait_recv()
        for cp in copies:
            cp.wait_send()
        acc = gath[0]
        for k in range(1, N_DEV):
            acc = acc + gath[k]
        out_ref[...] = acc

    vm = pl.BlockSpec(memory_space=pltpu.VMEM)
    return pl.pallas_call(
        body, name=name, in_specs=[vm], out_specs=vm,
        out_shape=jax.ShapeDtypeStruct(slab.shape, slab.dtype),
        scratch_shapes=[pltpu.VMEM((N_DEV,) + slab.shape, slab.dtype),
                        pltpu.SemaphoreType.DMA((N_DEV - 1,)), pltpu.SemaphoreType.DMA((N_DEV - 1,))],
    )(slab)


def _gather_stage(shards, into, *, ici=(), d2d=()):
    n = len(shards) if into is None else len(into)
    ns = len(shards) if ici else 0
    ni, nd = max(len(ici), 1), max(len(d2d), 1)
    shapes = [s.shape for s in shards] if into is None else [p.shape[1:] for p in into]
    dtypes = [s.dtype for s in shards] if into is None else [p.dtype for p in into]

    def copies(ins, outs, sems, sending):
        x, y, c, others = _place()
        me = 2 * x + y
        out = []
        for t in range(n):
            hr = shapes[t][0] // 2
            mine = pl.ds(pl.multiple_of(c * hr, HALO), hr)
            theirs = pl.ds(pl.multiple_of((1 - c) * hr, HALO), hr)
            for a, j in enumerate(ici):
                ox, oy = others[j]
                src_chip = me if sending else 2 * ox + oy
                out.append(pltpu.make_async_remote_copy(
                    src_ref=ins[t].at[mine], dst_ref=outs[t].at[src_chip, mine],
                    send_sem=sems[0].at[ni * t + a], recv_sem=sems[1].at[ni * t + a],
                    device_id=(ox, oy, c) if sending else (x, y, c), device_id_type=MESH))
            for a, j in enumerate(d2d):
                ox, oy = others[j]
                blk = outs[t].at[2 * ox + oy, mine if sending else theirs]
                out.append(pltpu.make_async_remote_copy(
                    src_ref=blk, dst_ref=blk, send_sem=sems[2].at[nd * t + a], recv_sem=sems[3].at[nd * t + a],
                    device_id=(x, y, 1 - c) if sending else (x, y, c), device_id_type=MESH))
        return out

    def local(ins, outs, sems):
        if into is not None:
            return []
        x, y, _, _ = _place()
        return [pltpu.make_async_copy(ins[t], outs[t].at[2 * x + y], sems[4].at[t]) for t in range(n)]

    def start(ins, outs, sems):
        for cp in local(ins, outs, sems) + copies(ins, outs, sems, True):
            cp.start()

    def finish(ins, outs, sems):
        for cp in copies(ins, outs, sems, False):
            cp.wait_recv()
        for cp in copies(ins, outs, sems, True):
            cp.wait_send()
        for cp in local(ins, outs, sems):
            cp.wait()

    return _Comm((list(shards) if ici or into is None else []) + (list(into) if into is not None else []),
                 [jax.ShapeDtypeStruct((N_SHARD,) + tuple(sh), dt) for sh, dt in zip(shapes, dtypes)],
                 [ni * n, ni * n, nd * n, nd * n, n], start, finish,
                 aliases=None if into is None else {ns + t: t for t in range(n)})


def _gather_ici(shards):
    return _gather_stage(shards, None, ici=(0, 1, 2))


def _gather_d2d(parts):
    return _gather_stage((), parts, d2d=(0, 1, 2))


def _swap_halves(grads):
    n = len(grads)

    def copies(ins, outs, sems):
        x, y, c, _ = _place()
        out = []
        for t in range(n):
            hr = grads[t].shape[1] // 2
            rows = pl.ds(pl.multiple_of((1 - c) * hr, 8), hr)
            out.append(pltpu.make_async_remote_copy(
                src_ref=ins[t].at[:, rows, :], dst_ref=outs[t], send_sem=sems[0].at[t], recv_sem=sems[1].at[t],
                device_id=(x, y, 1 - c), device_id_type=MESH))
        return out

    def start(ins, outs, sems):
        for cp in copies(ins, outs, sems):
            cp.start()

    def finish(ins, outs, sems):
        for cp in copies(ins, outs, sems):
            cp.wait()

    return _Comm(grads, [jax.ShapeDtypeStruct((N_SHARD, g.shape[1] // 2, g.shape[2]), g.dtype) for g in grads],
                 [n, n], start, finish)


def _pair_sum(g, got, c, *, name):
    ns, R, C = g.shape
    hr = R // 2

    def body(c_ref, g_ref, r_ref, o_ref):
        o_ref[...] = (g_ref[...] + r_ref[...]).astype(BF16)

    return pl.pallas_call(
        body, name=name,
        grid_spec=pltpu.PrefetchScalarGridSpec(
            num_scalar_prefetch=1, grid=(ns,),
            in_specs=[pl.BlockSpec((None, hr, C), lambda s, cr: (s, cr[0], 0)),
                      pl.BlockSpec((None, hr, C), lambda s, cr: (s, 0, 0))],
            out_specs=pl.BlockSpec((None, hr, C), lambda s, cr: (s, 0, 0))),
        out_shape=jax.ShapeDtypeStruct((ns, hr, C), BF16),
        compiler_params=_params("arbitrary"),
    )(c, g, got)


def _scatter_chips(sums):
    n = len(sums)

    def copies(ins, outs, sems, sending):
        x, y, c, others = _place()
        me = 2 * x + y
        out = []
        for t in range(n):
            for j, (ox, oy) in enumerate(others):
                there = 2 * ox + oy
                out.append(pltpu.make_async_remote_copy(
                    src_ref=ins[t].at[there if sending else me], dst_ref=outs[t].at[me if sending else there],
                    send_sem=sems[0].at[3 * t + j], recv_sem=sems[1].at[3 * t + j],
                    device_id=(ox, oy, c) if sending else (x, y, c), device_id_type=MESH))
        return out

    def start(ins, outs, sems):
        for cp in copies(ins, outs, sems, True):
            cp.start()

    def finish(ins, outs, sems):
        for cp in copies(ins, outs, sems, False):
            cp.wait_recv()
        for cp in copies(ins, outs, sems, True):
            cp.wait_send()

    return _Comm(sums, [jax.ShapeDtypeStruct(s.shape, s.dtype) for s in sums], [3 * n, 3 * n], start, finish)


def _chip_sum(g, got, landed, idx, *, name):
    ns, R, C = g.shape
    hr = R // 2
    steps = next(k for k in (4, 2, 1) if hr % (k * HALO) == 0)
    tr = hr // steps

    def body(i_ref, g_ref, r_ref, a_ref, b_ref, c_ref, o_ref):
        acc = g_ref[...] + r_ref[...]
        for ref in (a_ref, b_ref, c_ref):
            acc = acc + ref[...].astype(F32)
        o_ref[...] = acc

    other = lambda k: pl.BlockSpec((None, tr, C), lambda s, ir: (ir[2 + k], s, 0))
    return pl.pallas_call(
        body, name=name,
        grid_spec=pltpu.PrefetchScalarGridSpec(
            num_scalar_prefetch=1, grid=(steps,),
            in_specs=[pl.BlockSpec((None, tr, C), lambda s, ir: (ir[0], ir[1] * steps + s, 0)),
                      pl.BlockSpec((None, tr, C), lambda s, ir: (ir[0], s, 0)),
                      other(0), other(1), other(2)],
            out_specs=pl.BlockSpec((tr, C), lambda s, ir: (ir[1] * steps + s, 0))),
        out_shape=jax.ShapeDtypeStruct((R, C), F32),
        compiler_params=_params("arbitrary"),
    )(idx, g, got, landed, landed, landed)


def _share_halves(halves):
    n = len(halves)

    def copies(outs, sems, sending):
        x, y, c, _ = _place()
        out = []
        for t in range(n):
            hr = halves[t].shape[0] // 2
            rows = pl.ds(pl.multiple_of((c if sending else 1 - c) * hr, 8), hr)
            out.append(pltpu.make_async_remote_copy(
                src_ref=outs[t].at[rows, :], dst_ref=outs[t].at[rows, :], send_sem=sems[0].at[t],
                recv_sem=sems[1].at[t], device_id=(x, y, 1 - c) if sending else (x, y, c), device_id_type=MESH))
        return out

    def start(ins, outs, sems):
        for cp in copies(outs, sems, True):
            cp.start()

    def finish(ins, outs, sems):
        for cp in copies(outs, sems, False):
            cp.wait_recv()
        for cp in copies(outs, sems, True):
            cp.wait_send()

    return _Comm(halves, [jax.ShapeDtypeStruct(h.shape, h.dtype) for h in halves], [n, n], start, finish,
                 aliases={t: t for t in range(n)})


def _adamw(w, g, m, v, *, name):
    R, C = w.shape
    tr = next((k for k in (128, 64, 32, 16, 8) if R % k == 0), R)

    def body(w_ref, g_ref, m_ref, v_ref, go_ref, d_ref, mo_ref, vo_ref):
        gv = g_ref[...]
        go_ref[...] = gv
        mn = ADAM_B1 * m_ref[...] + (1.0 - ADAM_B1) * gv
        vn = ADAM_B2 * v_ref[...] + (1.0 - ADAM_B2) * (gv * gv)
        m_hat = mn / (1.0 - ADAM_B1 ** ADAM_STEP)
        v_hat = vn / (1.0 - ADAM_B2 ** ADAM_STEP)
        d_ref[...] = -ADAM_LR * (m_hat / (jnp.sqrt(v_hat) + ADAM_EPS) + ADAM_WD * w_ref[...])
        mo_ref[...] = mn
        vo_ref[...] = vn

    blk = pl.BlockSpec((tr, C), lambda i: (i, 0))
    return pl.pallas_call(
        body, name=name, grid=(R // tr,), in_specs=[blk] * 4, out_specs=[blk] * 4,
        out_shape=[jax.ShapeDtypeStruct((R, C), F32)] * 4,
        compiler_params=_params("arbitrary"),
    )(w, g, m, v)


def _pack_small(D, meta, n1, nm, n3, nf, gc, ga, bf, cw):
    def row(a):
        a = a.reshape(-1, a.shape[-1])
        return jnp.pad(a, ((0, 0), (0, D - a.shape[-1])))
    rows = [row(meta), row(n1), row(nm), row(n3), row(nf), row(jnp.concatenate([gc, ga], axis=-1)), row(bf), row(cw)]
    slab = jnp.concatenate(rows, axis=0)
    return jnp.pad(slab, ((0, SMALL_ROWS - slab.shape[0]), (0, 0)))


def _unpack_small(slab, like):
    meta, n1, nm, n3, nf, gc, ga, bf, cw = like
    nmeta, mc = meta.shape
    out = [slab[:nmeta, :mc].reshape(meta.shape)]
    r = nmeta
    for a in (n1, nm, n3, nf):
        out.append(slab[r, :a.shape[-1]].reshape(a.shape))
        r += 1
    cd = gc.shape[-1]
    out.append(slab[r, :cd].reshape(gc.shape))
    out.append(slab[r, cd:cd + ga.shape[-1]].reshape(ga.shape))
    r += 1
    out.append(slab[r, :bf.shape[-1]].reshape(bf.shape))
    r += 1
    out.append(slab[r:r + 3, :cw.shape[-1]].reshape(cw.shape))
    return out


def kernel(x, meta_tokens, ffn1_norm, ffn1_w_gu, ffn1_w_down, mix_norm, w_in, conv_w, b_f, out_norm_conv, out_norm_attn, w_out, ffn2_norm, ffn2_w_gu, ffn2_w_down, final_norm, loss_target, m_meta_tokens, m_ffn1_norm, m_ffn1_w_gu, m_ffn1_w_down, m_mix_norm, m_w_in, m_conv_w, m_b_f, m_out_norm_conv, m_out_norm_attn, m_w_out, m_ffn2_norm, m_ffn2_w_gu, m_ffn2_w_down, m_final_norm, v_meta_tokens, v_ffn1_norm, v_ffn1_w_gu, v_ffn1_w_down, v_mix_norm, v_w_in, v_conv_w, v_b_f, v_out_norm_conv, v_out_norm_attn, v_w_out, v_ffn2_norm, v_ffn2_w_gu, v_ffn2_w_down, v_final_norm):
    B, S, D = x.shape
    L = S + N_META
    T = B * L
    tm = L // 3
    assert tm * 3 == L and tm % HALO == 0
    tm2 = 2 * tm
    assert T % tm2 == 0
    guc = ffn1_w_gu.shape[-1]
    ff = N_SHARD * guc // 2
    H = b_f.shape[-1]
    AD = H * HEAD_DIM
    CD = conv_w.shape[-1] * N_SHARD
    assert CD == AD and CD + AD == D and CD % LANES == 0
    n_main = 3 * CD + 3 * AD
    ins = w_in.shape[-1]

    xi, yi, ci = lax.axis_index("x"), lax.axis_index("y"), lax.axis_index("c")
    chip = 2 * xi + yi

    small_shard = jnp.zeros((2 * HALO, meta_tokens.shape[-1]), F32)
    small_shard = small_shard.at[:N_META].set(meta_tokens)
    small_shard = small_shard.at[N_META:N_META + 3, :conv_w.shape[-1]].set(conv_w[0])
    big = [ffn1_w_gu[0], ffn1_w_down[0], w_in[0], w_out[0], ffn2_w_gu[0], ffn2_w_down[0]]
    wgu1_s, wd1_s, win_s, wout_s, wgu2_s, wd2_s = [w.astype(BF16) for w in big]
    small_g, = _all_gather_shards([small_shard], name="gather_small")
    meta_f = jnp.moveaxis(small_g[:, :N_META], 0, 1).reshape(N_META, D)
    cw_f = jnp.moveaxis(small_g[:, N_META:N_META + 3, :conv_w.shape[-1]], 0, 1).reshape(3, CD)
    cw8 = jnp.pad(cw_f, ((0, 5), (0, 0)))
    bf_p = jnp.pad(b_f, ((0, 0), (0, LANES - H)))
    gid = jnp.arange(CD) // HEAD_DIM
    pmat = jnp.where(gid[:, None] == gid[None, :], 1.0 / HEAD_DIM, 0.0).astype(BF16)

    gu_shape = jax.ShapeDtypeStruct((2, T, ff), BF16)
    gu_w_spec = pl.BlockSpec((None, D, guc), lambda s, i: (s, 0, 0))
    gu_o_spec = pl.BlockSpec((None, tm2, guc), lambda s, i: (s // 2, i, s % 2))

    sid = jnp.bitwise_xor(chip, jnp.array([0, 2, 1, 3], jnp.int32)).astype(jnp.int32)
    (h0, n1), wgu1_h = _embed_norm(x, meta_f, ffn1_norm, tm=tm, name="embed_norm",
                                   comm=_gather_stage([wgu1_s], None, ici=(0, 1)))
    gu1, wgu1_h = _ffn_up(n1, wgu1_s[None], sid, None, tm=tm2, first=0, count=1, name="ffn1_up_own",
                          comm=_gather_stage([wgu1_s], wgu1_h, ici=(2,), d2d=(0, 1)))
    gu1, out = _ffn_up(n1, wgu1_h[0], sid, gu1, tm=tm2, first=1, count=2, name="ffn1_up_near",
                       comm=_join(_gather_stage((), wgu1_h, d2d=(2,)), _gather_ici([wd1_s, wout_s])))
    wgu1, down_w = out[0], out[1:]
    gu1, (wd1, wout_g) = _ffn_up(n1, wgu1, sid, gu1, tm=tm2, first=3, count=1, name="ffn1_up_far",
                                 comm=_gather_d2d(down_w))
    wd1 = wd1.reshape(ff, D)
    (h1, n2), win_h = _ffn_down(gu1, wd1, h0, mix_norm, tm=tm, name="ffn1_down", comm=_gather_ici([win_s]))
    win_g, = _run_comm(_gather_d2d(win_h), name="gather_w_in")
    wout_f = wout_g.reshape(D, D)
    win_f = jnp.moveaxis(win_g, 0, 1).reshape(D, N_SHARD * ins)
    win_main = win_f[:, :n_main]
    win_fg = jnp.pad(win_f[:, n_main:], ((0, 0), (0, LANES - H)))

    proj, fg = _mix_in(n2, win_main, win_fg, tm=tm2, nb=n_main // (3 * CD), name="mix_in")
    proj3 = proj.reshape(B, L, n_main)
    fg3 = fg.reshape(B, L, LANES)
    fc = _fcum(fg3, bf_p, ch=tm, name="forget_cumsum")
    fr = fc[:, :, :H].reshape(B, L // tm, tm, H).transpose(0, 1, 3, 2)
    (o, lse), ffn2_w = _attn_fwd(proj3, fr, tq=tm, n_heads=H, name="attn_fwd",
                                 comm=_gather_ici([wgu2_s, wd2_s]))
    (h2, ymix, n3), (wgu2, wd2) = _mix_out(
        proj3, o, cw8, out_norm_conv, out_norm_attn, wout_f, h1.reshape(B, L, D), pmat, ffn2_norm,
        tm=tm, name="mix_out", comm=_gather_d2d(ffn2_w))
    wd2 = wd2.reshape(ff, D)
    h2 = h2.reshape(T, D)
    n3 = n3.reshape(T, D)

    gu2, _ = _matmul_nn(n3, wgu2, tm=tm2, nb=N_SHARD, w_spec=gu_w_spec, out_shape=gu_shape, out_spec=gu_o_spec,
                        name="ffn2_up")
    (dh3f, dh3b, d_gf, loss_part), _ = _ffn_down_loss(gu2, wd2, h2, final_norm.reshape(1, D), loss_target,
                                                      tm=tm, name="ffn2_down_loss")

    c_arr = jnp.reshape(ci, (1,)).astype(jnp.int32)
    ks = jnp.arange(N_SHARD - 1, dtype=jnp.int32)
    idx = jnp.concatenate([jnp.stack([chip, ci]).astype(jnp.int32), ks + (ks >= chip).astype(jnp.int32)])

    def pair_sums(grads, got, names):
        return [_pair_sum(g, r, c_arr, name="pair_sum_" + nm) for g, r, nm in zip(grads, got, names)]

    def chip_sums(grads, got, landed, names):
        return [_chip_sum(g, r, l, idx, name="chip_sum_" + nm) for g, r, l, nm in zip(grads, got, landed, names)]

    def dw_up(n, dgu, name, comm=None):
        return _matmul_tn(
            n, dgu, tm=L, nb=N_SHARD, kb=D, x_spec=pl.BlockSpec((L, D), lambda s, i: (i, 0)),
            y_spec=pl.BlockSpec((None, L, guc), lambda s, i: (s // 2, i, s % 2)),
            out_shape=jax.ShapeDtypeStruct((N_SHARD, D, guc), F32),
            out_spec=pl.BlockSpec((None, D, guc), lambda s, i: (s, 0, 0)), name=name, comm=comm)

    (dgu2, d_wd2), _ = _ffn_bwd_act(dh3b, gu2, wd2, tm=tm, guc=guc, name="ffn2_bwd_act")
    (dh2, dh2b, d_g3), _ = _ffn_bwd_in(dgu2, wgu2, h2, ffn2_norm, dh3f, tm=tm, scale=1.0, name="ffn2_bwd_in")
    d_wgu2, _ = dw_up(n3, dgu2, "ffn2_dw_up")
    grads_f2 = [d_wgu2, d_wd2.reshape(N_SHARD, ff // N_SHARD, D)]
    names_f2 = ["wgu2", "wd2"]

    dh2b3 = dh2b.reshape(B, L, D)
    (d_bg, d_cv, d_o, d_gc, d_ga, d_cw), got_f2 = _mix_out_bwd(
        dh2b3, proj3, o, cw8, out_norm_conv, out_norm_attn, wout_f, pmat, tm=tm, name="mix_out_bwd",
        comm=_swap_halves(grads_f2))
    sums_f2 = pair_sums(grads_f2, got_f2, names_f2)
    d_wout, _ = _matmul_tn(
        ymix.reshape(T, D), dh2b, tm=tm2, nb=1, kb=D,
        x_spec=pl.BlockSpec((tm2, D), lambda s, i: (i, 0)), y_spec=pl.BlockSpec((tm2, D), lambda s, i: (i, 0)),
        out_shape=jax.ShapeDtypeStruct((D, D), F32), out_spec=pl.BlockSpec((D, D), lambda s, i: (0, 0)),
        name="dw_out")
    d_cc = _conv_bwd(d_cv, proj3, cw8, tm=tm, name="conv_bwd")
    (d_q, d_k, d_v, d_fk, d_fq), landed_f2 = _attn_bwd(proj3, o, d_o, lse, fr, tq=tm, n_heads=H, name="attn_bwd",
                                                       comm=_scatter_chips(sums_f2))
    halves_f2 = chip_sums(grads_f2, got_f2, landed_f2, names_f2)
    d_fg, d_bf = _fcum_bwd(d_fq, d_fk, fg3, bf_p, ch=tm, name="forget_cumsum_bwd")

    parts = [d_bg.reshape(T, CD), d_cc.reshape(T, 2 * CD), d_q.reshape(T, AD), d_k.reshape(T, AD),
             d_v.reshape(T, AD), d_fg.reshape(T, LANES)]
    (dh1, dh1b, d_gm, d_proj), g_f2 = _mix_bwd_in(parts, win_main, win_fg, h1, mix_norm, dh2, tm=tm, scale=0.5,
                                                  name="mix_bwd_in", comm=_share_halves(halves_f2))
    wide = d_proj.shape[1]
    d_win_nat, _ = _matmul_tn(
        n2, d_proj, tm=tm, nb=1, kb=D,
        x_spec=pl.BlockSpec((tm, D), lambda s, i: (i, 0)), y_spec=pl.BlockSpec((tm, wide), lambda s, i: (i, 0)),
        out_shape=jax.ShapeDtypeStruct((D, wide), F32), out_spec=pl.BlockSpec((D, wide), lambda s, i: (0, 0)),
        name="dw_in")
    d_win = jnp.moveaxis(d_win_nat[:, :N_SHARD * ins].reshape(D, N_SHARD, ins), 1, 0)
    grads_mx = [d_win, d_wout.reshape(N_SHARD, D // N_SHARD, D)]
    names_mx = ["win", "wout"]

    (dgu1, d_wd1), got_mx = _ffn_bwd_act(dh1b, gu1, wd1, tm=tm, guc=guc, name="ffn1_bwd_act",
                                         comm=_swap_halves(grads_mx))
    sums_mx = pair_sums(grads_mx, got_mx, names_mx)
    grads_d1 = [d_wd1.reshape(N_SHARD, ff // N_SHARD, D)]
    d_wgu1, out = dw_up(n1, dgu1, "ffn1_dw_up", comm=_join(_scatter_chips(sums_mx), _swap_halves(grads_d1)))
    landed_mx, got_d1 = out[:2], out[2:]
    halves_mx = chip_sums(grads_mx, got_mx, landed_mx, names_mx)
    sums_d1 = pair_sums(grads_d1, got_d1, ["wd1"])
    grads_u1 = [d_wgu1]
    (grad_x, d_meta, d_g1), out = _ffn_bwd_in_first(
        dgu1, wgu1, h0, ffn1_norm, dh1, tm=tm, batch=B, name="ffn1_bwd_in",
        comm=_join(_join(_share_halves(halves_mx), _scatter_chips(sums_d1)), _swap_halves(grads_u1)))
    g_mx, landed_d1, got_u1 = out[:2], out[2:3], out[3:]
    halves_d1 = chip_sums(grads_d1, got_d1, landed_d1, ["wd1"])
    sums_u1 = pair_sums(grads_u1, got_u1, ["wgu1"])
    out = _run_comm(_join(_share_halves(halves_d1), _scatter_chips(sums_u1)), name="scatter_ffn1")
    g_d1, landed_u1 = out[:1], out[1:]
    halves_u1 = chip_sums(grads_u1, got_u1, landed_u1, ["wgu1"])
    g_u1 = _run_comm(_share_halves(halves_u1), name="share_ffn1")
    g_big = [g_u1[0], g_d1[0], g_mx[0], g_mx[1], g_f2[0], g_f2[1]]

    loss_row = jnp.zeros((1, D), F32).at[0, 0].set(loss_part[0, 0])
    slab = _pack_small(D, d_meta, d_g1, d_gm, d_g3, d_gf, d_gc, d_ga, d_bf[:, :H], d_cw[:3])
    slab = slab.at[SMALL_ROWS - 1].set(loss_row[0])
    total = _all_reduce_small(slab, name="reduce_small")
    loss = total[SMALL_ROWS - 1, 0]
    mcols = meta_tokens.shape[-1]
    ccols = conv_w.shape[-1]
    full_like = (jnp.zeros((N_META, D)), ffn1_norm, mix_norm, ffn2_norm, final_norm.reshape(1, D), out_norm_conv,
                 out_norm_attn, b_f, jnp.zeros((1, 3, CD)))
    g_small = _unpack_small(total, full_like)
    g_small[0] = lax.dynamic_slice_in_dim(g_small[0], chip * mcols, mcols, axis=1)
    g_small[8] = lax.dynamic_slice_in_dim(g_small[8], chip * ccols, ccols, axis=2)

    def small_slab(meta, a1, am, a3, af, gc, ga, bf, cw):
        return _pack_small(D, meta, a1, am, a3, af.reshape(1, D), gc, ga, bf, cw[0])

    w_small = small_slab(meta_tokens, ffn1_norm, mix_norm, ffn2_norm, final_norm, out_norm_conv, out_norm_attn, b_f, conv_w)
    m_small = small_slab(m_meta_tokens, m_ffn1_norm, m_mix_norm, m_ffn2_norm, m_final_norm, m_out_norm_conv,
                         m_out_norm_attn, m_b_f, m_conv_w)
    v_small = small_slab(v_meta_tokens, v_ffn1_norm, v_mix_norm, v_ffn2_norm, v_final_norm, v_out_norm_conv,
                         v_out_norm_attn, v_b_f, v_conv_w)
    gs = list(g_small)
    gs[4] = gs[4].reshape(final_norm.shape)
    g_slab = small_slab(gs[0], gs[1], gs[2], gs[3], gs[4], gs[5], gs[6], gs[7], gs[8])
    local_like = (meta_tokens, ffn1_norm, mix_norm, ffn2_norm, final_norm.reshape(1, D), out_norm_conv, out_norm_attn,
                  b_f, conv_w)
    small_out = [_unpack_small(s, local_like)
                 for s in _adamw(w_small, g_slab, m_small, v_small, name="adamw_small")[1:]]
    for lst in small_out:
        lst[4] = lst[4].reshape(final_norm.shape)

    names = ["wgu1", "wd1", "win", "wout", "wgu2", "wd2"]
    w_big = big
    m_big = [m_ffn1_w_gu[0], m_ffn1_w_down[0], m_w_in[0], m_w_out[0], m_ffn2_w_gu[0], m_ffn2_w_down[0]]
    v_big = [v_ffn1_w_gu[0], v_ffn1_w_down[0], v_w_in[0], v_w_out[0], v_ffn2_w_gu[0], v_ffn2_w_down[0]]
    big_out = [_adamw(w, g, m, v, name="adamw_" + nm) for w, g, m, v, nm in zip(w_big, g_big, m_big, v_big, names)]

    def assemble(small, bigs):
        meta, a1, am, a3, af, gc, ga, bf, cw = small
        gu1_, d1_, win_, wout_, gu2_, d2_ = [b[None] for b in bigs]
        return [meta, a1, gu1_, d1_, am, win_, cw, bf, gc, ga, wout_, a3, gu2_, d2_, af]

    gs_out = list(g_small)
    gs_out[4] = gs_out[4].reshape(final_norm.shape)
    grads_out = assemble(gs_out, [b[0] for b in big_out])
    delta_out = assemble(small_out[0], [b[1] for b in big_out])
    m_out = assemble(small_out[1], [b[2] for b in big_out])
    v_out = assemble(small_out[2], [b[3] for b in big_out])
    return (loss, grad_x, *grads_out, *delta_out, *m_out, *v_out)
```
